```python
import math
import jax, jax.numpy as jnp
from jax import lax
import numpy as np

D_MODEL = 1024
BATCH = 8
SEQ = 4096
DEPTH = 2

D_MIX = 1024
MLA_HEADS = 6
MLA_Q_RANK = 256
MLA_KV_RANK = 128
MLA_NOPE = 64
MLA_ROPE = 32
MLA_V = 64
ROPE_THETA = 10000.0
POOL_WINDOWS = (2, 4, 8, 16)
POOL_GROUP = 64
POOL_WIDTH = POOL_GROUP * len(POOL_WINDOWS)
FOX_HEADS = 6
FOX_HEAD_DIM = 64
FOX_GATE_BIAS_INIT = 2.0
BLOCK_Q = 128
D_FF = 2816
EPS = 1e-6
IN_Q_A = MLA_Q_RANK
IN_KV_A = MLA_KV_RANK
IN_K_ROPE = MLA_ROPE
IN_POOL = POOL_WIDTH
IN_FOX_QKV = 3 * FOX_HEADS * FOX_HEAD_DIM
IN_FOX_F = FOX_HEADS
N_IN = IN_Q_A + IN_KV_A + IN_K_ROPE + IN_POOL + IN_FOX_QKV + IN_FOX_F

kernel_name = "hybrid_mla_pool_fox_macaron"


def rmsnorm(x, g):
    xf = x.astype(jnp.float32)
    y = xf * lax.rsqrt(jnp.mean(xf * xf, axis=-1, keepdims=True) + EPS)
    return y.astype(x.dtype) * g


def rope(x, pos):
    r = x.shape[-1]
    inv_freq = ROPE_THETA ** (-jnp.arange(0, r, 2, dtype=jnp.float32) / r)
    ang = pos.astype(jnp.float32)[:, None] * inv_freq[None, :]
    cos = jnp.cos(ang).astype(x.dtype)
    sin = jnp.sin(ang).astype(x.dtype)
    x1, x2 = x[..., : r // 2], x[..., r // 2:]
    return jnp.concatenate([x1 * cos - x2 * sin, x2 * cos + x1 * sin], axis=-1)


def causal_block_attention(q, k, v, scale, log_decay_cum=None):
    b, h, s, dk = q.shape
    dv = v.shape[-1]
    nb = s // BLOCK_Q
    qb = q.reshape(b, h, nb, BLOCK_Q, dk).transpose(2, 0, 1, 3, 4)
    kpos = jnp.arange(s)
    xs = (jnp.arange(nb), qb)
    if log_decay_cum is not None:
        xs = xs + (log_decay_cum.reshape(b, h, nb, BLOCK_Q).transpose(2, 0, 1, 3),)

    def one_block(args):
        i, q_blk = args[0], args[1]
        sc = jnp.einsum('bhqd,bhkd->bhqk', q_blk, k, preferred_element_type=jnp.float32) * scale
        if log_decay_cum is not None:
            c_blk = args[2]
            sc = sc + c_blk[..., :, None] - log_decay_cum[..., None, :].astype(jnp.float32)
        qpos = i * BLOCK_Q + jnp.arange(BLOCK_Q)
        sc = jnp.where(kpos[None, :] <= qpos[:, None], sc, -jnp.inf)
        p = jax.nn.softmax(sc, axis=-1).astype(v.dtype)
        return jnp.einsum('bhqk,bhkd->bhqd', p, v)

    out = lax.map(one_block, xs)
    return out.transpose(1, 2, 0, 3, 4).reshape(b, h, s, dv)


def mla_mixer(q_a, kv_a, k_rope, q_a_norm, w_q_b, kv_a_norm, w_kv_b, pos):
    b, s, _ = q_a.shape
    q = (rmsnorm(q_a, q_a_norm) @ w_q_b).reshape(b, s, MLA_HEADS, MLA_NOPE + MLA_ROPE).transpose(0, 2, 1, 3)
    q_nope, q_pe = q[..., :MLA_NOPE], rope(q[..., MLA_NOPE:], pos)
    kv = (rmsnorm(kv_a, kv_a_norm) @ w_kv_b).reshape(b, s, MLA_HEADS, MLA_NOPE + MLA_V).transpose(0, 2, 1, 3)
    k_nope, v = kv[..., :MLA_NOPE], kv[..., MLA_NOPE:]
    k_pe = jnp.broadcast_to(rope(k_rope, pos)[:, None], (b, MLA_HEADS, s, MLA_ROPE))
    qf = jnp.concatenate([q_nope, q_pe], axis=-1)
    kf = jnp.concatenate([k_nope, k_pe], axis=-1)
    o = causal_block_attention(qf, kf, v, 1.0 / math.sqrt(MLA_NOPE + MLA_ROPE))
    return o.transpose(0, 2, 1, 3).reshape(b, s, MLA_HEADS * MLA_V)


def pool_mixer(u, pool_w, pool_scale):
    b, s, _ = u.shape
    ng = len(POOL_WINDOWS)
    ug = u.reshape(b, s, ng, POOL_GROUP)
    cs = jnp.cumsum(ug.astype(jnp.float32), axis=1)
    count = jnp.arange(1, s + 1, dtype=jnp.float32)
    means = []
    for g, w in enumerate(POOL_WINDOWS):
        c = cs[:, :, g]
        prev = jnp.pad(c[:, : s - w], ((0, 0), (w, 0), (0, 0)))
        means.append((c - prev) / jnp.minimum(count, float(w))[None, :, None])
    pooled = jnp.stack(means, axis=2).astype(u.dtype) - ug
    y = jnp.einsum('bsgc,gcd->bsgd', pooled, pool_w)
    return y.reshape(b, s, POOL_WIDTH) * pool_scale


def fox_mixer(qkv, f_logit, fox_b_f):
    b, s, _ = qkv.shape
    qkv = qkv.reshape(b, s, 3, FOX_HEADS, FOX_HEAD_DIM).transpose(2, 0, 3, 1, 4)
    q, k, v = qkv[0], qkv[1], qkv[2]
    log_f = jax.nn.log_sigmoid((f_logit + fox_b_f).astype(jnp.float32))
    cum = jnp.cumsum(log_f, axis=1).transpose(0, 2, 1)
    o = causal_block_attention(q, k, v, 1.0 / math.sqrt(FOX_HEAD_DIM), cum)
    return o.transpose(0, 2, 1, 3).reshape(b, s, FOX_HEADS * FOX_HEAD_DIM)


def swiglu(h, w_gu, w_down):
    gu = h @ w_gu
    g, u = gu[..., :D_FF], gu[..., D_FF:]
    return (jax.nn.silu(g) * u) @ w_down


def hybrid_mixing(h, w_in, q_a_norm, w_q_b, kv_a_norm, w_kv_b, pool_w, pool_scale, fox_b_f, w_out, pos):
    z = h @ w_in
    o0 = 0
    o1 = o0 + IN_Q_A
    o2 = o1 + IN_KV_A
    o3 = o2 + IN_K_ROPE
    o4 = o3 + IN_POOL
    o5 = o4 + IN_FOX_QKV
    o6 = o5 + IN_FOX_F
    ya = mla_mixer(z[..., o0:o1], z[..., o1:o2], z[..., o2:o3], q_a_norm, w_q_b, kv_a_norm, w_kv_b, pos)
    yb = pool_mixer(z[..., o3:o4], pool_w, pool_scale)
    yc = fox_mixer(z[..., o4:o5], z[..., o5:o6], fox_b_f)
    return jnp.concatenate([ya, yb, yc], axis=-1) @ w_out


def _fwd_setup_inputs(seed: int = 0) -> dict:
    key = jax.random.key(seed)
    ks = jax.random.split(key, 24)
    L, D, F = DEPTH, D_MODEL, D_FF
    f32 = jnp.float32

    def nrm(k, shape, fan_in):
        return jax.random.normal(k, shape, f32) * (fan_in ** -0.5)

    def gain(k, shape):
        return 1.0 + 0.02 * jax.random.normal(k, shape, f32)

    return {
        "x": jax.random.normal(ks[0], (BATCH, SEQ, D), f32),
        "ffn1_norm": gain(ks[1], (L, D)),
        "ffn1_w_gu": nrm(ks[2], (L, D, 2 * F), D),
        "ffn1_w_down": nrm(ks[3], (L, F, D), F),
        "mix_norm": gain(ks[4], (L, D)),
        "w_in": nrm(ks[5], (L, D, N_IN), D),
        "q_a_norm": gain(ks[6], (L, MLA_Q_RANK)),
        "w_q_b": nrm(ks[7], (L, MLA_Q_RANK, MLA_HEADS * (MLA_NOPE + MLA_ROPE)), MLA_Q_RANK),
        "kv_a_norm": gain(ks[8], (L, MLA_KV_RANK)),
        "w_kv_b": nrm(ks[9], (L, MLA_KV_RANK, MLA_HEADS * (MLA_NOPE + MLA_V)), MLA_KV_RANK),
        "pool_w": nrm(ks[10], (L, len(POOL_WINDOWS), POOL_GROUP, POOL_GROUP), POOL_GROUP),
        "pool_scale": gain(ks[11], (L, POOL_WIDTH)),
        "fox_b_f": FOX_GATE_BIAS_INIT + 0.5 * jax.random.normal(ks[12], (L, FOX_HEADS), f32),
        "w_out": nrm(ks[13], (L, D_MIX, D), D_MIX),
        "ffn2_norm": gain(ks[14], (L, D)),
        "ffn2_w_gu": nrm(ks[15], (L, D, 2 * F), D),
        "ffn2_w_down": nrm(ks[16], (L, F, D), F),
        "final_norm": gain(ks[17], (D,)),
    }


def _fwd_reference(x, ffn1_norm, ffn1_w_gu, ffn1_w_down, mix_norm, w_in, q_a_norm, w_q_b, kv_a_norm, w_kv_b,
              pool_w, pool_scale, fox_b_f, w_out, ffn2_norm, ffn2_w_gu, ffn2_w_down, final_norm):
    pos = jnp.arange(x.shape[1], dtype=jnp.int32)
    for l in range(DEPTH):
        x = x + 0.5 * swiglu(rmsnorm(x, ffn1_norm[l]), ffn1_w_gu[l], ffn1_w_down[l])
        x = x + hybrid_mixing(rmsnorm(x, mix_norm[l]), w_in[l], q_a_norm[l], w_q_b[l], kv_a_norm[l], w_kv_b[l],
                              pool_w[l], pool_scale[l], fox_b_f[l], w_out[l], pos)
        x = x + 0.5 * swiglu(rmsnorm(x, ffn2_norm[l]), ffn2_w_gu[l], ffn2_w_down[l])
    return rmsnorm(x, final_norm)


import jax as _jax
import jax.numpy as _jnp

TWIN_FORMAT = 'train_step'
FWD_PARAMS = ['x', 'ffn1_norm', 'ffn1_w_gu', 'ffn1_w_down', 'mix_norm', 'w_in', 'q_a_norm', 'w_q_b', 'kv_a_norm', 'w_kv_b', 'pool_w', 'pool_scale', 'fox_b_f', 'w_out', 'ffn2_norm', 'ffn2_w_gu', 'ffn2_w_down', 'final_norm']
TWIN_WEIGHTS = ['ffn1_norm', 'ffn1_w_gu', 'ffn1_w_down', 'mix_norm', 'w_in', 'q_a_norm', 'w_q_b', 'kv_a_norm', 'w_kv_b', 'pool_w', 'pool_scale', 'fox_b_f', 'w_out', 'ffn2_norm', 'ffn2_w_gu', 'ffn2_w_down', 'final_norm']
TWIN_DIFF_INPUT = 'x'
TWIN_INPUTS = ['x', 'ffn1_norm', 'ffn1_w_gu', 'ffn1_w_down', 'mix_norm', 'w_in', 'q_a_norm', 'w_q_b', 'kv_a_norm', 'w_kv_b', 'pool_w', 'pool_scale', 'fox_b_f', 'w_out', 'ffn2_norm', 'ffn2_w_gu', 'ffn2_w_down', 'final_norm', 'loss_target', 'm_ffn1_norm', 'm_ffn1_w_gu', 'm_ffn1_w_down', 'm_mix_norm', 'm_w_in', 'm_q_a_norm', 'm_w_q_b', 'm_kv_a_norm', 'm_w_kv_b', 'm_pool_w', 'm_pool_scale', 'm_fox_b_f', 'm_w_out', 'm_ffn2_norm', 'm_ffn2_w_gu', 'm_ffn2_w_down', 'm_final_norm', 'v_ffn1_norm', 'v_ffn1_w_gu', 'v_ffn1_w_down', 'v_mix_norm', 'v_w_in', 'v_q_a_norm', 'v_w_q_b', 'v_kv_a_norm', 'v_w_kv_b', 'v_pool_w', 'v_pool_scale', 'v_fox_b_f', 'v_w_out', 'v_ffn2_norm', 'v_ffn2_w_gu', 'v_ffn2_w_down', 'v_final_norm']
TWIN_OUTPUTS = ['loss', 'grad_x', 'grad_ffn1_norm', 'grad_ffn1_w_gu', 'grad_ffn1_w_down', 'grad_mix_norm', 'grad_w_in', 'grad_q_a_norm', 'grad_w_q_b', 'grad_kv_a_norm', 'grad_w_kv_b', 'grad_pool_w', 'grad_pool_scale', 'grad_fox_b_f', 'grad_w_out', 'grad_ffn2_norm', 'grad_ffn2_w_gu', 'grad_ffn2_w_down', 'grad_final_norm', 'delta_ffn1_norm', 'delta_ffn1_w_gu', 'delta_ffn1_w_down', 'delta_mix_norm', 'delta_w_in', 'delta_q_a_norm', 'delta_w_q_b', 'delta_kv_a_norm', 'delta_w_kv_b', 'delta_pool_w', 'delta_pool_scale', 'delta_fox_b_f', 'delta_w_out', 'delta_ffn2_norm', 'delta_ffn2_w_gu', 'delta_ffn2_w_down', 'delta_final_norm', 'new_m_ffn1_norm', 'new_m_ffn1_w_gu', 'new_m_ffn1_w_down', 'new_m_mix_norm', 'new_m_w_in', 'new_m_q_a_norm', 'new_m_w_q_b', 'new_m_kv_a_norm', 'new_m_w_kv_b', 'new_m_pool_w', 'new_m_pool_scale', 'new_m_fox_b_f', 'new_m_w_out', 'new_m_ffn2_norm', 'new_m_ffn2_w_gu', 'new_m_ffn2_w_down', 'new_m_final_norm', 'new_v_ffn1_norm', 'new_v_ffn1_w_gu', 'new_v_ffn1_w_down', 'new_v_mix_norm', 'new_v_w_in', 'new_v_q_a_norm', 'new_v_w_q_b', 'new_v_kv_a_norm', 'new_v_w_kv_b', 'new_v_pool_w', 'new_v_pool_scale', 'new_v_fox_b_f', 'new_v_w_out', 'new_v_ffn2_norm', 'new_v_ffn2_w_gu', 'new_v_ffn2_w_down', 'new_v_final_norm']
TWIN_LEAF_KINDS = {'loss': 'loss', 'grad_x': 'grad_x', 'grad_ffn1_norm': 'grad_w', 'grad_ffn1_w_gu': 'grad_w', 'grad_ffn1_w_down': 'grad_w', 'grad_mix_norm': 'grad_w', 'grad_w_in': 'grad_w', 'grad_q_a_norm': 'grad_w', 'grad_w_q_b': 'grad_w', 'grad_kv_a_norm': 'grad_w', 'grad_w_kv_b': 'grad_w', 'grad_pool_w': 'grad_w', 'grad_pool_scale': 'grad_w', 'grad_fox_b_f': 'grad_w', 'grad_w_out': 'grad_w', 'grad_ffn2_norm': 'grad_w', 'grad_ffn2_w_gu': 'grad_w', 'grad_ffn2_w_down': 'grad_w', 'grad_final_norm': 'grad_w', 'delta_ffn1_norm': 'delta_w', 'delta_ffn1_w_gu': 'delta_w', 'delta_ffn1_w_down': 'delta_w', 'delta_mix_norm': 'delta_w', 'delta_w_in': 'delta_w', 'delta_q_a_norm': 'delta_w', 'delta_w_q_b': 'delta_w', 'delta_kv_a_norm': 'delta_w', 'delta_w_kv_b': 'delta_w', 'delta_pool_w': 'delta_w', 'delta_pool_scale': 'delta_w', 'delta_fox_b_f': 'delta_w', 'delta_w_out': 'delta_w', 'delta_ffn2_norm': 'delta_w', 'delta_ffn2_w_gu': 'delta_w', 'delta_ffn2_w_down': 'delta_w', 'delta_final_norm': 'delta_w', 'new_m_ffn1_norm': 'new_m', 'new_m_ffn1_w_gu': 'new_m', 'new_m_ffn1_w_down': 'new_m', 'new_m_mix_norm': 'new_m', 'new_m_w_in': 'new_m', 'new_m_q_a_norm': 'new_m', 'new_m_w_q_b': 'new_m', 'new_m_kv_a_norm': 'new_m', 'new_m_w_kv_b': 'new_m', 'new_m_pool_w': 'new_m', 'new_m_pool_scale': 'new_m', 'new_m_fox_b_f': 'new_m', 'new_m_w_out': 'new_m', 'new_m_ffn2_norm': 'new_m', 'new_m_ffn2_w_gu': 'new_m', 'new_m_ffn2_w_down': 'new_m', 'new_m_final_norm': 'new_m', 'new_v_ffn1_norm': 'new_v', 'new_v_ffn1_w_gu': 'new_v', 'new_v_ffn1_w_down': 'new_v', 'new_v_mix_norm': 'new_v', 'new_v_w_in': 'new_v', 'new_v_q_a_norm': 'new_v', 'new_v_w_q_b': 'new_v', 'new_v_kv_a_norm': 'new_v', 'new_v_w_kv_b': 'new_v', 'new_v_pool_w': 'new_v', 'new_v_pool_scale': 'new_v', 'new_v_fox_b_f': 'new_v', 'new_v_w_out': 'new_v', 'new_v_ffn2_norm': 'new_v', 'new_v_ffn2_w_gu': 'new_v', 'new_v_ffn2_w_down': 'new_v', 'new_v_final_norm': 'new_v'}


def _forward(args):
    return _fwd_reference(*[args[k] for k in FWD_PARAMS])


def _output_shape():
    def fwd():
        inp = _fwd_setup_inputs(0)
        return _fwd_reference(*[inp[k] for k in FWD_PARAMS])
    out = _jax.eval_shape(fwd)
    return out.shape, out.dtype

N_MICROBATCH = 1
ADAM_LR = 0.001
ADAM_B1 = 0.9
ADAM_B2 = 0.999
ADAM_EPS = 1e-08
ADAM_WD = 0.01
ADAM_STEP = 10
PER_EXAMPLE_BATCH_AXIS = {'x': 0, 'loss_target': 0}
SHARED_INPUTS = []
_WEIGHT_DTYPES = {'ffn1_norm': _jnp.float32, 'ffn1_w_gu': _jnp.float32, 'ffn1_w_down': _jnp.float32, 'mix_norm': _jnp.float32, 'w_in': _jnp.float32, 'q_a_norm': _jnp.float32, 'w_q_b': _jnp.float32, 'kv_a_norm': _jnp.float32, 'w_kv_b': _jnp.float32, 'pool_w': _jnp.float32, 'pool_scale': _jnp.float32, 'fox_b_f': _jnp.float32, 'w_out': _jnp.float32, 'ffn2_norm': _jnp.float32, 'ffn2_w_gu': _jnp.float32, 'ffn2_w_down': _jnp.float32, 'final_norm': _jnp.float32}
MOMENT_SCALE = {'ffn1_norm': 7.860245e-02, 'ffn1_w_gu': 3.202540e-02, 'ffn1_w_down': 5.215353e-02, 'mix_norm': 9.494010e-02, 'w_in': 7.029901e-02, 'q_a_norm': 3.298526e-02, 'w_q_b': 2.301908e-02, 'kv_a_norm': 7.574960e-02, 'w_kv_b': 3.047228e-02, 'pool_w': 1.355593e-01, 'pool_scale': 1.299123e-01, 'fox_b_f': 6.043595e-01, 'w_out': 8.068283e-02, 'ffn2_norm': 6.722552e-02, 'ffn2_w_gu': 2.666178e-02, 'ffn2_w_down': 4.355097e-02, 'final_norm': 3.204565e+01}


def _to_microbatches(a, axis):
    t = _jnp.moveaxis(a, axis, 0)
    t = t.reshape((N_MICROBATCH, t.shape[0] // N_MICROBATCH) + t.shape[1:])
    return _jnp.moveaxis(t, 1, axis + 1)


def setup_inputs(seed: int = 0) -> dict:
    inp = _fwd_setup_inputs(seed)
    key = _jax.random.fold_in(_jax.random.key(seed), 7919)
    shape, _ = _output_shape()
    out = dict(inp)
    out["loss_target"] = _jax.random.normal(_jax.random.fold_in(key, 0), shape, _jnp.float32)
    for i, name in enumerate(TWIN_WEIGHTS):
        w = inp[name].astype(_jnp.float32)
        if MOMENT_SCALE is None:
            s = _jnp.sqrt(_jnp.mean(_jnp.square(w)) + 1e-30)
        else:
            s = MOMENT_SCALE[name]
        km, kv = _jax.random.split(_jax.random.fold_in(key, i + 1))
        out[name] = w
        out["m_" + name] = s * _jax.random.normal(km, w.shape, _jnp.float32)
        out["v_" + name] = (s * s) * _jax.random.uniform(kv, w.shape, _jnp.float32, 0.5, 1.5)
    if N_MICROBATCH > 1:
        for name, axis in PER_EXAMPLE_BATCH_AXIS.items():
            out[name] = _to_microbatches(out[name], axis)
    return {'x': out['x'], 'ffn1_norm': out['ffn1_norm'], 'ffn1_w_gu': out['ffn1_w_gu'], 'ffn1_w_down': out['ffn1_w_down'], 'mix_norm': out['mix_norm'], 'w_in': out['w_in'], 'q_a_norm': out['q_a_norm'], 'w_q_b': out['w_q_b'], 'kv_a_norm': out['kv_a_norm'], 'w_kv_b': out['w_kv_b'], 'pool_w': out['pool_w'], 'pool_scale': out['pool_scale'], 'fox_b_f': out['fox_b_f'], 'w_out': out['w_out'], 'ffn2_norm': out['ffn2_norm'], 'ffn2_w_gu': out['ffn2_w_gu'], 'ffn2_w_down': out['ffn2_w_down'], 'final_norm': out['final_norm'], 'loss_target': out['loss_target'], 'm_ffn1_norm': out['m_ffn1_norm'], 'm_ffn1_w_gu': out['m_ffn1_w_gu'], 'm_ffn1_w_down': out['m_ffn1_w_down'], 'm_mix_norm': out['m_mix_norm'], 'm_w_in': out['m_w_in'], 'm_q_a_norm': out['m_q_a_norm'], 'm_w_q_b': out['m_w_q_b'], 'm_kv_a_norm': out['m_kv_a_norm'], 'm_w_kv_b': out['m_w_kv_b'], 'm_pool_w': out['m_pool_w'], 'm_pool_scale': out['m_pool_scale'], 'm_fox_b_f': out['m_fox_b_f'], 'm_w_out': out['m_w_out'], 'm_ffn2_norm': out['m_ffn2_norm'], 'm_ffn2_w_gu': out['m_ffn2_w_gu'], 'm_ffn2_w_down': out['m_ffn2_w_down'], 'm_final_norm': out['m_final_norm'], 'v_ffn1_norm': out['v_ffn1_norm'], 'v_ffn1_w_gu': out['v_ffn1_w_gu'], 'v_ffn1_w_down': out['v_ffn1_w_down'], 'v_mix_norm': out['v_mix_norm'], 'v_w_in': out['v_w_in'], 'v_q_a_norm': out['v_q_a_norm'], 'v_w_q_b': out['v_w_q_b'], 'v_kv_a_norm': out['v_kv_a_norm'], 'v_w_kv_b': out['v_w_kv_b'], 'v_pool_w': out['v_pool_w'], 'v_pool_scale': out['v_pool_scale'], 'v_fox_b_f': out['v_fox_b_f'], 'v_w_out': out['v_w_out'], 'v_ffn2_norm': out['v_ffn2_norm'], 'v_ffn2_w_gu': out['v_ffn2_w_gu'], 'v_ffn2_w_down': out['v_ffn2_w_down'], 'v_final_norm': out['v_final_norm']}


def _loss(weights, diff, rest, loss_target):
    with _jax.named_scope("forward"):
        args = {**rest, TWIN_DIFF_INPUT: diff, **{k: w.astype(_WEIGHT_DTYPES[k]) for k, w in weights.items()}}
        y = _forward(args)
    with _jax.named_scope("loss_head"):
        err = _jnp.square(y.astype(_jnp.float32) - loss_target)
        return 0.5 * _jnp.sum(_jnp.mean(err, axis=-1)) if err.ndim else 0.5 * err


def _adamw(w, g, m, v):
    m = ADAM_B1 * m + (1.0 - ADAM_B1) * g
    v = ADAM_B2 * v + (1.0 - ADAM_B2) * _jnp.square(g)
    m_hat = m / (1.0 - ADAM_B1 ** ADAM_STEP)
    v_hat = v / (1.0 - ADAM_B2 ** ADAM_STEP)
    delta = -ADAM_LR * (m_hat / (_jnp.sqrt(v_hat) + ADAM_EPS) + ADAM_WD * w)
    return delta, m, v


def reference(x, ffn1_norm, ffn1_w_gu, ffn1_w_down, mix_norm, w_in, q_a_norm, w_q_b, kv_a_norm, w_kv_b, pool_w, pool_scale, fox_b_f, w_out, ffn2_norm, ffn2_w_gu, ffn2_w_down, final_norm, loss_target, m_ffn1_norm, m_ffn1_w_gu, m_ffn1_w_down, m_mix_norm, m_w_in, m_q_a_norm, m_w_q_b, m_kv_a_norm, m_w_kv_b, m_pool_w, m_pool_scale, m_fox_b_f, m_w_out, m_ffn2_norm, m_ffn2_w_gu, m_ffn2_w_down, m_final_norm, v_ffn1_norm, v_ffn1_w_gu, v_ffn1_w_down, v_mix_norm, v_w_in, v_q_a_norm, v_w_q_b, v_kv_a_norm, v_w_kv_b, v_pool_w, v_pool_scale, v_fox_b_f, v_w_out, v_ffn2_norm, v_ffn2_w_gu, v_ffn2_w_down, v_final_norm):
    given = dict(x=x, ffn1_norm=ffn1_norm, ffn1_w_gu=ffn1_w_gu, ffn1_w_down=ffn1_w_down, mix_norm=mix_norm, w_in=w_in, q_a_norm=q_a_norm, w_q_b=w_q_b, kv_a_norm=kv_a_norm, w_kv_b=w_kv_b, pool_w=pool_w, pool_scale=pool_scale, fox_b_f=fox_b_f, w_out=w_out, ffn2_norm=ffn2_norm, ffn2_w_gu=ffn2_w_gu, ffn2_w_down=ffn2_w_down, final_norm=final_norm, loss_target=loss_target, m_ffn1_norm=m_ffn1_norm, m_ffn1_w_gu=m_ffn1_w_gu, m_ffn1_w_down=m_ffn1_w_down, m_mix_norm=m_mix_norm, m_w_in=m_w_in, m_q_a_norm=m_q_a_norm, m_w_q_b=m_w_q_b, m_kv_a_norm=m_kv_a_norm, m_w_kv_b=m_w_kv_b, m_pool_w=m_pool_w, m_pool_scale=m_pool_scale, m_fox_b_f=m_fox_b_f, m_w_out=m_w_out, m_ffn2_norm=m_ffn2_norm, m_ffn2_w_gu=m_ffn2_w_gu, m_ffn2_w_down=m_ffn2_w_down, m_final_norm=m_final_norm, v_ffn1_norm=v_ffn1_norm, v_ffn1_w_gu=v_ffn1_w_gu, v_ffn1_w_down=v_ffn1_w_down, v_mix_norm=v_mix_norm, v_w_in=v_w_in, v_q_a_norm=v_q_a_norm, v_w_q_b=v_w_q_b, v_kv_a_norm=v_kv_a_norm, v_w_kv_b=v_w_kv_b, v_pool_w=v_pool_w, v_pool_scale=v_pool_scale, v_fox_b_f=v_fox_b_f, v_w_out=v_w_out, v_ffn2_norm=v_ffn2_norm, v_ffn2_w_gu=v_ffn2_w_gu, v_ffn2_w_down=v_ffn2_w_down, v_final_norm=v_final_norm)
    weights = {n: given[n] for n in TWIN_WEIGHTS}
    shared = {n: given[n] for n in SHARED_INPUTS}
    per_example = {n: given[n] for n in ['x']}
    grad_fn = _jax.value_and_grad(_loss, argnums=(0, 1))

    def one_microbatch(ex, loss_target):
        ex = dict(ex)
        diff = ex.pop(TWIN_DIFF_INPUT)
        return grad_fn(weights, diff, {**shared, **ex}, loss_target)

    if N_MICROBATCH == 1:
        loss, (grad_w, grad_x) = one_microbatch(per_example, given["loss_target"])
    else:
        def body(carry, xs):
            loss_sum, grad_sum = carry
            l_k, (gw_k, gx_k) = one_microbatch(xs[0], xs[1])
            with _jax.named_scope("update"):
                return (loss_sum + l_k, _jax.tree.map(_jnp.add, grad_sum, gw_k)), gx_k

        init = (_jnp.zeros((), _jnp.float32), _jax.tree.map(_jnp.zeros_like, weights))
        (loss, grad_w), grad_x = _jax.lax.scan(body, init, (per_example, given["loss_target"]))
    with _jax.named_scope("update"):
        delta_w, new_m, new_v = {}, {}, {}
        for n in TWIN_WEIGHTS:
            delta_w[n], new_m[n], new_v[n] = _adamw(weights[n], grad_w[n], given["m_" + n], given["v_" + n])
    return (loss, grad_x, *[grad_w[n] for n in TWIN_WEIGHTS], *[delta_w[n] for n in TWIN_WEIGHTS],
            *[new_m[n] for n in TWIN_WEIGHTS], *[new_v[n] for n in TWIN_WEIGHTS])
```

```python
import functools
import math

import jax
import jax.numpy as jnp
import numpy as np
from jax import lax
from jax.experimental import pallas as pl
from jax.experimental.pallas import tpu as pltpu

F32 = jnp.float32
BF16 = jnp.bfloat16
MESH = pl.DeviceIdType.MESH
HBM_SPEC = pl.BlockSpec(memory_space=pltpu.HBM)

D = 1024
DEPTH = 2
D_FF = 2816
FF_SHARD = 1408
N_CHIPS = 4
H = 6
NOPE, ROPE, VDIM = 64, 32, 64
HALF_ROPE = ROPE // 2
Q_RANK, KV_RANK = 256, 128
POOL_W = 256
FOX_D = 64
N_IN = 1830
NZ = 2048
ROPE_THETA = 10000.0
EPS = 1e-6
POOL_HALO = 16
Z_QA, Z_KVA, Z_KR, Z_POOL, Z_FOX, Z_F = 0, 256, 384, 512, 768, 1920

ADAM_LR, ADAM_B1, ADAM_B2, ADAM_EPS, ADAM_WD, ADAM_STEP = 0.001, 0.9, 0.999, 1e-08, 0.01, 10

VMEM_LIMIT_V7X = 56 * 1024 * 1024


def _cp(sem=None, vmem=VMEM_LIMIT_V7X):
    return pltpu.CompilerParams(dimension_semantics=sem, vmem_limit_bytes=vmem)


def _sigmoid(x):
    return 1.0 / (1.0 + jnp.exp(-x))


def _dot(a, b, dims):
    return lax.dot_general(a, b, (dims, ((), ())), preferred_element_type=F32)


NN = ((1,), (0,))
NT = ((1,), (1,))
TN = ((0,), (0,))


def _mm(a, b, mode, *, name, out_dtype=F32, add=None, alpha=None, tm=512, tn=512, tk=512, n_major_out=False):
    if mode == "nn":
        (m, k), (k2, n) = a.shape, b.shape
    elif mode == "nt":
        (m, k), (n, k2) = a.shape, b.shape
    else:
        (k, m), (k2, n) = a.shape, b.shape
    assert k == k2
    tm, tn, tk = min(tm, m), min(tn, n), min(tk, k)
    assert m % tm == 0 and n % tn == 0 and k % tk == 0, (name, m, n, k, tm, tn, tk)
    nk = k // tk
    dims = {"nn": NN, "nt": NT, "tn": TN}[mode]
    a_spec = pl.BlockSpec((tk, tm), lambda i, j, kk: (kk, i)) if mode == "tn" else pl.BlockSpec((tm, tk), lambda i, j, kk: (i, kk))
    b_spec = pl.BlockSpec((tn, tk), lambda i, j, kk: (j, kk)) if mode == "nt" else pl.BlockSpec((tk, tn), lambda i, j, kk: (kk, j))
    in_specs = [a_spec, b_spec]
    args = [a, b]
    if add is not None:
        in_specs.append(pl.BlockSpec((tm, tn), lambda i, j, kk: (i, j)))
        args.append(add)
    if n_major_out:
        out_shape = jax.ShapeDtypeStruct((n // tn, m, tn), out_dtype)
        out_spec = pl.BlockSpec((None, tm, tn), lambda i, j, kk: (j, i, 0))
    else:
        out_shape = jax.ShapeDtypeStruct((m, n), out_dtype)
        out_spec = pl.BlockSpec((tm, tn), lambda i, j, kk: (i, j))

    def body(*refs):
        a_ref, b_ref = refs[0], refs[1]
        add_ref = refs[2] if add is not None else None
        o_ref, acc = refs[-2], refs[-1]
        kk = pl.program_id(2)

        @pl.when(kk == 0)
        def _():
            acc[...] = jnp.zeros_like(acc)

        acc[...] += _dot(a_ref[...].astype(BF16), b_ref[...].astype(BF16), dims)

        @pl.when(kk == nk - 1)
        def _():
            r = acc[...]
            if alpha is not None:
                r = r * alpha
            if add_ref is not None:
                r = r + add_ref[...].astype(F32)
            o_ref[...] = r.astype(out_dtype)

    return pl.pallas_call(
        body, name=name, grid=(m // tm, n // tn, nk), in_specs=in_specs, out_specs=out_spec, out_shape=out_shape,
        scratch_shapes=[pltpu.VMEM((tm, tn), F32)],
        compiler_params=_cp(("parallel", "parallel", "arbitrary")),
    )(*args)


def _norm_mm(x, col_block, gain, w, *, name, tm=512):
    s = x.shape[0]
    k, n = w.shape
    tm = min(tm, s)

    def body(x_ref, g_ref, w_ref, z_ref, h_ref):
        xv = x_ref[...]
        r = lax.rsqrt(jnp.mean(xv * xv, axis=-1, keepdims=True) + EPS)
        hv = (xv * r * g_ref[...]).astype(BF16)
        h_ref[...] = hv
        z_ref[...] = _dot(hv, w_ref[...], NN)

    return pl.pallas_call(
        body, name=name, grid=(s // tm,),
        in_specs=[pl.BlockSpec((tm, k), lambda i: (i, col_block)), pl.BlockSpec((1, k), lambda i: (0, 0)),
                  pl.BlockSpec((k, n), lambda i: (0, 0))],
        out_specs=[pl.BlockSpec((tm, n), lambda i: (i, 0)), pl.BlockSpec((tm, k), lambda i: (i, 0))],
        out_shape=[jax.ShapeDtypeStruct((s, n), F32), jax.ShapeDtypeStruct((s, k), BF16)],
        compiler_params=_cp(("parallel",)),
    )(x, gain.reshape(1, k), w)


def _rmsnorm_bwd(x, col_block, gain, dh, dres=None, *, name, tm=512):
    s = x.shape[0]
    k = gain.shape[-1]
    tm = min(tm, s)

    def body(*refs):
        x_ref, g_ref, dh_ref = refs[0], refs[1], refs[2]
        dres_ref = refs[3] if dres is not None else None
        dx_ref, dg_ref = refs[-2], refs[-1]
        xv = x_ref[...]
        r = lax.rsqrt(jnp.mean(xv * xv, axis=-1, keepdims=True) + EPS)
        dhv = dh_ref[...].astype(F32)
        a = dhv * g_ref[...]
        dx = r * a - xv * (r * r * r) * jnp.mean(a * xv, axis=-1, keepdims=True)
        if dres_ref is not None:
            dx = dx + dres_ref[...]
        dx_ref[...] = dx

        @pl.when(pl.program_id(0) == 0)
        def _():
            dg_ref[...] = jnp.zeros_like(dg_ref)

        dg_ref[...] += jnp.sum(dhv * xv * r, axis=0, keepdims=True)

    in_specs = [pl.BlockSpec((tm, k), lambda i: (i, col_block)), pl.BlockSpec((1, k), lambda i: (0, 0)),
                pl.BlockSpec((tm, k), lambda i: (i, 0))]
    args = [x, gain.reshape(1, k), dh]
    if dres is not None:
        in_specs.append(pl.BlockSpec((tm, k), lambda i: (i, 0)))
        args.append(dres)
    dx, dg = pl.pallas_call(
        body, name=name, grid=(s // tm,), in_specs=in_specs,
        out_specs=[pl.BlockSpec((tm, k), lambda i: (i, 0)), pl.BlockSpec((1, k), lambda i: (0, 0))],
        out_shape=[jax.ShapeDtypeStruct((s, k), F32), jax.ShapeDtypeStruct((1, k), F32)],
        compiler_params=_cp(("arbitrary",)),
    )(*args)
    return dx, dg.reshape(k)


def _ffn_fwd(x, gain, w_gu4, w_d2, layer, *, name, tm=256):
    s = x.shape[0]
    tm = min(tm, s)

    def body(x_ref, g_ref, wgu_ref, wd_ref, xo_ref, gu_ref):
        xv = x_ref[...]
        r = lax.rsqrt(jnp.mean(xv * xv, axis=-1, keepdims=True) + EPS)
        hv = (xv * r * g_ref[...]).astype(BF16)
        y = jnp.zeros((tm, D), F32)
        for j in range(2):
            g = _dot(hv, wgu_ref[j], NN)
            u = _dot(hv, wgu_ref[2 + j], NN)
            gu_ref[:, j * FF_SHARD:(j + 1) * FF_SHARD] = g.astype(BF16)
            gu_ref[:, D_FF + j * FF_SHARD:D_FF + (j + 1) * FF_SHARD] = u.astype(BF16)
            act = (g * _sigmoid(g) * u).astype(BF16)
            y = y + _dot(act, wd_ref[j], NN)
        xo_ref[...] = xv + 0.5 * y

    return pl.pallas_call(
        body, name=name, grid=(s // tm,),
        in_specs=[pl.BlockSpec((tm, D), lambda i: (i, 0)), pl.BlockSpec((1, D), lambda i: (0, 0)),
                  pl.BlockSpec((N_CHIPS, None, D, FF_SHARD), lambda i: (0, layer, 0, 0), pipeline_mode=pl.Buffered(1)),
                  pl.BlockSpec((2, FF_SHARD, D), lambda i: (0, 0, 0), pipeline_mode=pl.Buffered(1))],
        out_specs=[pl.BlockSpec((tm, D), lambda i: (i, 0)), pl.BlockSpec((tm, 2 * D_FF), lambda i: (i, 0))],
        out_shape=[jax.ShapeDtypeStruct((s, D), F32), jax.ShapeDtypeStruct((s, 2 * D_FF), BF16)],
        compiler_params=_cp(("parallel",)),
    )(x, gain.reshape(1, D), w_gu4, w_d2)


def _ffn_bwd(x, dxo, gu, gain, w_gu4, w_d2, layer, *, name, tm=256):
    s = x.shape[0]
    tm = min(tm, s)

    def body(x_ref, dxo_ref, gu_ref, g_ref, wgu_ref, wd_ref, dx_ref, dgu_ref, act_ref, h_ref, dy_ref, dg_ref):
        xv = x_ref[...]
        r = lax.rsqrt(jnp.mean(xv * xv, axis=-1, keepdims=True) + EPS)
        xh = xv * r
        h_ref[...] = (xh * g_ref[...]).astype(BF16)
        dxov = dxo_ref[...]
        dy = (0.5 * dxov).astype(BF16)
        dy_ref[...] = dy
        dh = jnp.zeros((tm, D), F32)
        for j in range(2):
            g = gu_ref[:, j * FF_SHARD:(j + 1) * FF_SHARD].astype(F32)
            u = gu_ref[:, D_FF + j * FF_SHARD:D_FF + (j + 1) * FF_SHARD].astype(F32)
            sg = _sigmoid(g)
            silu = g * sg
            act_ref[:, j * FF_SHARD:(j + 1) * FF_SHARD] = (silu * u).astype(BF16)
            dact = _dot(dy, wd_ref[j], NT)
            dg = (dact * u * (sg * (1.0 + g * (1.0 - sg)))).astype(BF16)
            du = (dact * silu).astype(BF16)
            dgu_ref[:, j * FF_SHARD:(j + 1) * FF_SHARD] = dg
            dgu_ref[:, D_FF + j * FF_SHARD:D_FF + (j + 1) * FF_SHARD] = du
            dh = dh + _dot(dg, wgu_ref[j], NT) + _dot(du, wgu_ref[2 + j], NT)
        a = dh * g_ref[...]
        dx_ref[...] = dxov + r * a - xh * (r * jnp.mean(a * xh, axis=-1, keepdims=True))

        @pl.when(pl.program_id(0) == 0)
        def _():
            dg_ref[...] = jnp.zeros_like(dg_ref)

        dg_ref[...] += jnp.sum(dh * xh, axis=0, keepdims=True)

    row = lambda i: (i, 0)
    outs = pl.pallas_call(
        body, name=name, grid=(s // tm,),
        in_specs=[pl.BlockSpec((tm, D), row), pl.BlockSpec((tm, D), row), pl.BlockSpec((tm, 2 * D_FF), row),
                  pl.BlockSpec((1, D), lambda i: (0, 0)),
                  pl.BlockSpec((N_CHIPS, None, D, FF_SHARD), lambda i: (0, layer, 0, 0), pipeline_mode=pl.Buffered(1)),
                  pl.BlockSpec((2, FF_SHARD, D), lambda i: (0, 0, 0), pipeline_mode=pl.Buffered(1))],
        out_specs=[pl.BlockSpec((tm, D), row), pl.BlockSpec((tm, 2 * D_FF), row), pl.BlockSpec((tm, D_FF), row),
                   pl.BlockSpec((tm, D), row), pl.BlockSpec((tm, D), row), pl.BlockSpec((1, D), lambda i: (0, 0))],
        out_shape=[jax.ShapeDtypeStruct((s, D), F32), jax.ShapeDtypeStruct((s, 2 * D_FF), BF16),
                   jax.ShapeDtypeStruct((s, D_FF), BF16), jax.ShapeDtypeStruct((s, D), BF16),
                   jax.ShapeDtypeStruct((s, D), BF16), jax.ShapeDtypeStruct((1, D), F32)],
        compiler_params=_cp(("arbitrary",)),
    )(x, dxo, gu, gain.reshape(1, D), w_gu4, w_d2)
    dx, dgu, act, h, dy, dg = outs
    return dx, dgu, act, h, dy, dg.reshape(D)


def _rope(a1, a2, cos, sin, *, name):
    def body(a1_ref, a2_ref, c_ref, s_ref, o1_ref, o2_ref):
        x1, x2, c, sn = a1_ref[...], a2_ref[...], c_ref[...], s_ref[...]
        o1_ref[...] = x1 * c - x2 * sn
        o2_ref[...] = x2 * c + x1 * sn

    return pl.pallas_call(body, name=name, out_shape=[jax.ShapeDtypeStruct(a1.shape, F32)] * 2,
                          compiler_params=_cp())(a1, a2, cos, sin)


def _scores(q_ref, k_ref, cq_ref, ck_ref, scale, i, j, t):
    sc = _dot(q_ref[...], k_ref[...], NT) * scale
    if cq_ref is not None:
        sc = sc + cq_ref[...] - ck_ref[...]
    row = lax.broadcasted_iota(jnp.int32, (t, t), 0)
    col = lax.broadcasted_iota(jnp.int32, (t, t), 1)
    return jnp.where(col <= row + (i - j) * t, sc, -jnp.inf)


def _attn_fwd(q, k, v, cq, ck, scale, *, name, t=512):
    h, s, dk = q.shape
    dv = v.shape[-1]
    t = min(t, s)
    nb = s // t
    decay = cq is not None

    def body(*refs):
        q_ref, k_ref, v_ref = refs[0], refs[1], refs[2]
        cq_ref, ck_ref = (refs[3], refs[4]) if decay else (None, None)
        o_ref, lse_ref, m_sc, l_sc, acc_sc = refs[-5:]
        i, j = pl.program_id(1), pl.program_id(2)

        @pl.when(j == 0)
        def _():
            m_sc[...] = jnp.full_like(m_sc, -jnp.inf)
            l_sc[...] = jnp.zeros_like(l_sc)
            acc_sc[...] = jnp.zeros_like(acc_sc)

        @pl.when(j <= i)
        def _():
            sc = _scores(q_ref, k_ref, cq_ref, ck_ref, scale, i, j, t)
            m_old = m_sc[...]
            m_new = jnp.maximum(m_old, jnp.max(sc, axis=-1, keepdims=True))
            p = jnp.exp(sc - m_new)
            a = jnp.exp(m_old - m_new)
            l_sc[...] = a * l_sc[...] + jnp.sum(p, axis=-1, keepdims=True)
            acc_sc[...] = a * acc_sc[...] + _dot(p.astype(BF16), v_ref[...], NN)
            m_sc[...] = m_new

        @pl.when(j == i)
        def _():
            o_ref[...] = acc_sc[...] / l_sc[...]
            lse_ref[...] = m_sc[...] + jnp.log(l_sc[...])

    qmap = lambda hh, i, j: (hh, i, 0)
    kmap = lambda hh, i, j: (hh, jnp.minimum(i, j), 0)
    in_specs = [pl.BlockSpec((None, t, dk), qmap), pl.BlockSpec((None, t, dk), kmap), pl.BlockSpec((None, t, dv), kmap)]
    args = [q, k, v]
    if decay:
        in_specs += [pl.BlockSpec((None, t, 1), qmap), pl.BlockSpec((None, 1, t), lambda hh, i, j: (hh, 0, jnp.minimum(i, j)))]
        args += [cq, ck]
    return pl.pallas_call(
        body, name=name, grid=(h, nb, nb), in_specs=in_specs,
        out_specs=[pl.BlockSpec((None, t, dv), qmap), pl.BlockSpec((None, t, 1), qmap)],
        out_shape=[jax.ShapeDtypeStruct((h, s, dv), F32), jax.ShapeDtypeStruct((h, s, 1), F32)],
        scratch_shapes=[pltpu.VMEM((t, 1), F32), pltpu.VMEM((t, 1), F32), pltpu.VMEM((t, dv), F32)],
        compiler_params=_cp(("parallel", "parallel", "arbitrary")),
    )(*args)


def _attn_delta(o, do, *, name):
    h, s, dv = o.shape

    def body(o_ref, do_ref, d_ref):
        d_ref[...] = jnp.sum(o_ref[...] * do_ref[...], axis=-1, keepdims=True)

    spec = pl.BlockSpec((None, s, dv), lambda hh: (hh, 0, 0))
    return pl.pallas_call(body, name=name, grid=(h,), in_specs=[spec, spec],
                          out_specs=pl.BlockSpec((None, s, 1), lambda hh: (hh, 0, 0)),
                          out_shape=jax.ShapeDtypeStruct((h, s, 1), F32), compiler_params=_cp(("parallel",)))(o, do)


def _attn_bwd_kv(q, k, v, do, lse, delta, cq, ck, scale, *, name, t=512):
    h, s, dk = q.shape
    dv = v.shape[-1]
    t = min(t, s)
    nb = s // t
    decay = cq is not None

    def body(*refs):
        q_ref, k_ref, v_ref, do_ref, lse_ref, dl_ref = refs[:6]
        cq_ref, ck_ref = (refs[6], refs[7]) if decay else (None, None)
        n_out = 3 if decay else 2
        outs = refs[-(2 * n_out):-n_out]
        accs = refs[-n_out:]
        j, i = pl.program_id(1), pl.program_id(2)

        @pl.when(i == 0)
        def _():
            for acc in accs:
                acc[...] = jnp.zeros_like(acc)

        @pl.when(i >= j)
        def _():
            sc = _scores(q_ref, k_ref, cq_ref, ck_ref, scale, i, j, t)
            p = jnp.exp(sc - lse_ref[...])
            dob = do_ref[...]
            accs[1][...] += _dot(p.astype(BF16), dob, TN)
            dp = _dot(dob, v_ref[...], NT)
            ds = p * (dp - dl_ref[...])
            accs[0][...] += _dot(ds.astype(BF16), q_ref[...], TN)
            if decay:
                accs[2][...] -= jnp.sum(ds, axis=0, keepdims=True)

        @pl.when(i == nb - 1)
        def _():
            outs[0][...] = accs[0][...] * scale
            outs[1][...] = accs[1][...]
            if decay:
                outs[2][...] = accs[2][...]

    kmap = lambda hh, j, i: (hh, j, 0)
    qmap = lambda hh, j, i: (hh, jnp.maximum(i, j), 0)
    in_specs = [pl.BlockSpec((None, t, dk), qmap), pl.BlockSpec((None, t, dk), kmap), pl.BlockSpec((None, t, dv), kmap),
                pl.BlockSpec((None, t, dv), qmap), pl.BlockSpec((None, t, 1), qmap), pl.BlockSpec((None, t, 1), qmap)]
    args = [q, k, v, do, lse, delta]
    out_specs = [pl.BlockSpec((None, t, dk), kmap), pl.BlockSpec((None, t, dv), kmap)]
    out_shape = [jax.ShapeDtypeStruct((h, s, dk), F32), jax.ShapeDtypeStruct((h, s, dv), F32)]
    scratch = [pltpu.VMEM((t, dk), F32), pltpu.VMEM((t, dv), F32)]
    if decay:
        in_specs += [pl.BlockSpec((None, t, 1), qmap), pl.BlockSpec((None, 1, t), lambda hh, j, i: (hh, 0, j))]
        args += [cq, ck]
        out_specs.append(pl.BlockSpec((None, 1, t), lambda hh, j, i: (hh, 0, j)))
        out_shape.append(jax.ShapeDtypeStruct((h, 1, s), F32))
        scratch.append(pltpu.VMEM((1, t), F32))
    return pl.pallas_call(
        body, name=name, grid=(h, nb, nb), in_specs=in_specs, out_specs=out_specs, out_shape=out_shape,
        scratch_shapes=scratch, compiler_params=_cp(("parallel", "parallel", "arbitrary")),
    )(*args)


def _attn_bwd_q(q, k, v, do, lse, delta, cq, ck, scale, *, name, t=512):
    h, s, dk = q.shape
    dv = v.shape[-1]
    t = min(t, s)
    nb = s // t
    decay = cq is not None

    def body(*refs):
        q_ref, k_ref, v_ref, do_ref, lse_ref, dl_ref = refs[:6]
        cq_ref, ck_ref = (refs[6], refs[7]) if decay else (None, None)
        n_out = 2 if decay else 1
        outs = refs[-(2 * n_out):-n_out]
        accs = refs[-n_out:]
        i, j = pl.program_id(1), pl.program_id(2)

        @pl.when(j == 0)
        def _():
            for acc in accs:
                acc[...] = jnp.zeros_like(acc)

        @pl.when(j <= i)
        def _():
            sc = _scores(q_ref, k_ref, cq_ref, ck_ref, scale, i, j, t)
            p = jnp.exp(sc - lse_ref[...])
            dp = _dot(do_ref[...], v_ref[...], NT)
            ds = p * (dp - dl_ref[...])
            accs[0][...] += _dot(ds.astype(BF16), k_ref[...], NN)
            if decay:
                accs[1][...] += jnp.sum(ds, axis=-1, keepdims=True)

        @pl.when(j == i)
        def _():
            outs[0][...] = accs[0][...] * scale
            if decay:
                outs[1][...] = accs[1][...]

    qmap = lambda hh, i, j: (hh, i, 0)
    kmap = lambda hh, i, j: (hh, jnp.minimum(i, j), 0)
    in_specs = [pl.BlockSpec((None, t, dk), qmap), pl.BlockSpec((None, t, dk), kmap), pl.BlockSpec((None, t, dv), kmap),
                pl.BlockSpec((None, t, dv), qmap), pl.BlockSpec((None, t, 1), qmap), pl.BlockSpec((None, t, 1), qmap)]
    args = [q, k, v, do, lse, delta]
    out_specs = [pl.BlockSpec((None, t, dk), qmap)]
    out_shape = [jax.ShapeDtypeStruct((h, s, dk), F32)]
    scratch = [pltpu.VMEM((t, dk), F32)]
    if decay:
        in_specs += [pl.BlockSpec((None, t, 1), qmap), pl.BlockSpec((None, 1, t), lambda hh, i, j: (hh, 0, jnp.minimum(i, j)))]
        args += [cq, ck]
        out_specs.append(pl.BlockSpec((None, t, 1), qmap))
        out_shape.append(jax.ShapeDtypeStruct((h, s, 1), F32))
        scratch.append(pltpu.VMEM((t, 1), F32))
    return pl.pallas_call(
        body, name=name, grid=(h, nb, nb), in_specs=in_specs, out_specs=out_specs, out_shape=out_shape,
        scratch_shapes=scratch, compiler_params=_cp(("parallel", "parallel", "arbitrary")),
    )(*args)


def _lane_scan(x, s, reverse):
    lane = lax.broadcasted_iota(jnp.int32, x.shape, 1)
    sh = 1
    while sh < s:
        if reverse:
            x = x + jnp.where(lane < s - sh, pltpu.roll(x, s - sh, axis=1), 0.0)
        else:
            x = x + jnp.where(lane >= sh, pltpu.roll(x, sh, axis=1), 0.0)
        sh *= 2
    return x


def _gate_fwd(ft, bias, *, name):
    s = ft.shape[1]

    def body(f_ref, b_ref, c_ref):
        xg = f_ref[...] + b_ref[...]
        lf = jnp.minimum(xg, 0.0) - jnp.log(1.0 + jnp.exp(-jnp.abs(xg)))
        c_ref[...] = _lane_scan(lf, s, False)

    return pl.pallas_call(body, name=name, out_shape=jax.ShapeDtypeStruct((8, s), F32), compiler_params=_cp())(ft, bias)


def _gate_bwd(ft, bias, dc, *, name):
    s = ft.shape[1]

    def body(f_ref, b_ref, dc_ref, df_ref, db_ref):
        xg = f_ref[...] + b_ref[...]
        dlf = _lane_scan(dc_ref[...], s, True)
        df = dlf * _sigmoid(-xg)
        df_ref[...] = df
        db_ref[...] = jnp.sum(df, axis=-1, keepdims=True)

    return pl.pallas_call(body, name=name, out_shape=[jax.ShapeDtypeStruct((8, s), F32), jax.ShapeDtypeStruct((8, 1), F32)],
                          compiler_params=_cp())(ft, bias, dc)


def _pool_lane_consts(tm, i):
    lane = lax.broadcasted_iota(jnp.int32, (tm, POOL_W), 1)
    tok = lax.broadcasted_iota(jnp.int32, (tm, POOL_W), 0) + i * tm
    win = jnp.where(lane < 64, 2, jnp.where(lane < 128, 4, jnp.where(lane < 192, 8, 16)))
    cnt = jnp.minimum(tok + 1, win).astype(F32)
    return lane, tok, cnt


def _pick_window(lane, s2, s4, s8, s16):
    return jnp.where(lane < 64, s2, jnp.where(lane < 128, s4, jnp.where(lane < 192, s8, s16)))


def _pool_fwd(z, col_block, bd, scale, *, name, tm=512):
    s = z.shape[0]
    tm = min(tm, s)
    hb = tm // POOL_HALO

    def body(u_ref, halo_ref, bd_ref, sc_ref, y_ref, p_ref, buf):
        i = pl.program_id(0)
        buf[0:POOL_HALO, :] = halo_ref[...] * (i > 0).astype(F32)
        buf[POOL_HALO:, :] = u_ref[...]

        def back(k):
            return buf[POOL_HALO - k:POOL_HALO - k + tm, :]

        u = u_ref[...]
        s2 = u + back(1)
        s4 = s2 + back(2) + back(3)
        s8 = s4 + back(4) + back(5) + back(6) + back(7)
        s16 = s8
        for k in range(8, 16):
            s16 = s16 + back(k)
        lane, _, cnt = _pool_lane_consts(tm, i)
        pooled = (_pick_window(lane, s2, s4, s8, s16) / cnt - u).astype(BF16)
        p_ref[...] = pooled
        y_ref[...] = _dot(pooled, bd_ref[...], NN) * sc_ref[...]

    return pl.pallas_call(
        body, name=name, grid=(s // tm,),
        in_specs=[pl.BlockSpec((tm, POOL_W), lambda i: (i, col_block)),
                  pl.BlockSpec((POOL_HALO, POOL_W), lambda i: (jnp.maximum(i * hb - 1, 0), col_block)),
                  pl.BlockSpec((POOL_W, POOL_W), lambda i: (0, 0)), pl.BlockSpec((1, POOL_W), lambda i: (0, 0))],
        out_specs=[pl.BlockSpec((tm, POOL_W), lambda i: (i, 0)), pl.BlockSpec((tm, POOL_W), lambda i: (i, 0))],
        out_shape=[jax.ShapeDtypeStruct((s, POOL_W), F32), jax.ShapeDtypeStruct((s, POOL_W), BF16)],
        scratch_shapes=[pltpu.VMEM((tm + POOL_HALO, POOL_W), F32)],
        compiler_params=_cp(("parallel",)),
    )(z, z, bd, scale.reshape(1, POOL_W))


def _pool_bwd_a(dy, pooled, bd, scale, *, name, tm=512):
    s = dy.shape[0]
    tm = min(tm, s)

    def body(dy_ref, p_ref, bd_ref, sc_ref, dq_ref, dys_ref, dsc_ref):
        i = pl.program_id(0)
        dyv = dy_ref[...]
        y0 = _dot(p_ref[...], bd_ref[...], NN)
        dys = (dyv * sc_ref[...]).astype(BF16)
        dys_ref[...] = dys
        dp = _dot(dys, bd_ref[...], NT)
        _, _, cnt = _pool_lane_consts(tm, i)
        dq_ref[:, 0:POOL_W] = dp / cnt
        dq_ref[:, POOL_W:] = dp

        @pl.when(i == 0)
        def _():
            dsc_ref[...] = jnp.zeros_like(dsc_ref)

        dsc_ref[...] += jnp.sum(dyv * y0, axis=0, keepdims=True)

    row = lambda i: (i, 0)
    dq, dys, dsc = pl.pallas_call(
        body, name=name, grid=(s // tm,),
        in_specs=[pl.BlockSpec((tm, POOL_W), row), pl.BlockSpec((tm, POOL_W), row),
                  pl.BlockSpec((POOL_W, POOL_W), lambda i: (0, 0)), pl.BlockSpec((1, POOL_W), lambda i: (0, 0))],
        out_specs=[pl.BlockSpec((tm, 2 * POOL_W), row), pl.BlockSpec((tm, POOL_W), row), pl.BlockSpec((1, POOL_W), lambda i: (0, 0))],
        out_shape=[jax.ShapeDtypeStruct((s, 2 * POOL_W), F32), jax.ShapeDtypeStruct((s, POOL_W), BF16),
                   jax.ShapeDtypeStruct((1, POOL_W), F32)],
        compiler_params=_cp(("arbitrary",)),
    )(dy, pooled, bd, scale.reshape(1, POOL_W))
    return dq, dys, dsc.reshape(POOL_W)


def _pool_bwd_b(dq, *, name, tm=512):
    s = dq.shape[0]
    tm = min(tm, s)
    hb = tm // POOL_HALO
    nblk = s // tm

    def body(q_ref, dp_ref, halo_ref, du_ref, buf):
        i = pl.program_id(0)
        buf[0:tm, :] = q_ref[...]
        buf[tm:, :] = halo_ref[...] * (i < nblk - 1).astype(F32)

        def ahead(k):
            return buf[k:k + tm, :]

        q = q_ref[...]
        s2 = q + ahead(1)
        s4 = s2 + ahead(2) + ahead(3)
        s8 = s4 + ahead(4) + ahead(5) + ahead(6) + ahead(7)
        s16 = s8
        for k in range(8, 16):
            s16 = s16 + ahead(k)
        lane = lax.broadcasted_iota(jnp.int32, (tm, POOL_W), 1)
        du_ref[...] = _pick_window(lane, s2, s4, s8, s16) - dp_ref[...]

    return pl.pallas_call(
        body, name=name, grid=(nblk,),
        in_specs=[pl.BlockSpec((tm, POOL_W), lambda i: (i, 0)), pl.BlockSpec((tm, POOL_W), lambda i: (i, 1)),
                  pl.BlockSpec((POOL_HALO, POOL_W), lambda i: (jnp.minimum((i + 1) * hb, nblk * hb - 1), 0))],
        out_specs=pl.BlockSpec((tm, POOL_W), lambda i: (i, 0)),
        out_shape=jax.ShapeDtypeStruct((s, POOL_W), F32),
        scratch_shapes=[pltpu.VMEM((tm + POOL_HALO, POOL_W), F32)],
        compiler_params=_cp(("parallel",)),
    )(dq, dq, dq)


def _loss_head(x, gain, target, *, name, tm=512):
    s = x.shape[0]
    tm = min(tm, s)

    def body(x_ref, g_ref, t_ref, dx_ref, dg_ref, loss_ref):
        xv = x_ref[...]
        r = lax.rsqrt(jnp.mean(xv * xv, axis=-1, keepdims=True) + EPS)
        xh = xv * r
        err = xh * g_ref[...] - t_ref[...]
        dy = err * (1.0 / D)
        a = dy * g_ref[...]
        dx_ref[...] = r * a - xh * (r * jnp.mean(a * xh, axis=-1, keepdims=True))

        @pl.when(pl.program_id(0) == 0)
        def _():
            dg_ref[...] = jnp.zeros_like(dg_ref)
            loss_ref[...] = jnp.zeros_like(loss_ref)

        dg_ref[...] += jnp.sum(dy * xh, axis=0, keepdims=True)
        part = 0.5 * jnp.sum(jnp.mean(err * err, axis=-1, keepdims=True), axis=0, keepdims=True)
        loss_ref[...] += jnp.broadcast_to(part, loss_ref.shape)

    row = lambda i: (i, 0)
    dx, dg, loss = pl.pallas_call(
        body, name=name, grid=(s // tm,),
        in_specs=[pl.BlockSpec((tm, D), row), pl.BlockSpec((1, D), lambda i: (0, 0)), pl.BlockSpec((tm, D), row)],
        out_specs=[pl.BlockSpec((tm, D), row), pl.BlockSpec((1, D), lambda i: (0, 0)), pl.BlockSpec((1, 128), lambda i: (0, 0))],
        out_shape=[jax.ShapeDtypeStruct((s, D), F32), jax.ShapeDtypeStruct((1, D), F32), jax.ShapeDtypeStruct((1, 128), F32)],
        compiler_params=_cp(("arbitrary",)),
    )(x, gain.reshape(1, D), target)
    return dx, dg.reshape(D), loss[0, 0]


def _adamw(w, g, m, v, *, name, tr=512):
    rows, cols = w.shape
    tr = min(tr, rows)
    assert rows % tr == 0, (name, rows, tr)
    c_m = 1.0 - ADAM_B1
    c_v = 1.0 - ADAM_B2
    bc1 = 1.0 - ADAM_B1 ** ADAM_STEP
    bc2 = 1.0 - ADAM_B2 ** ADAM_STEP

    def body(w_ref, g_ref, m_ref, v_ref, d_ref, mo_ref, vo_ref):
        gv = g_ref[...]
        mn = ADAM_B1 * m_ref[...] + c_m * gv
        vn = ADAM_B2 * v_ref[...] + c_v * (gv * gv)
        mo_ref[...] = mn
        vo_ref[...] = vn
        d_ref[...] = -ADAM_LR * ((mn / bc1) / (jnp.sqrt(vn / bc2) + ADAM_EPS) + ADAM_WD * w_ref[...])

    spec = pl.BlockSpec((tr, cols), lambda i: (i, 0))
    return pl.pallas_call(body, name=name, grid=(rows // tr,), in_specs=[spec] * 4, out_specs=[spec] * 3,
                          out_shape=[jax.ShapeDtypeStruct((rows, cols), F32)] * 3,
                          compiler_params=_cp(("parallel",)))(w, g, m, v)


def _row_tile(rows, cap):
    for t in range(min(cap, rows), 0, -1):
        if rows % t == 0 and t % 16 == 0:
            return t
    return rows


def _sum2_bf16(a, b, *, name, tr=256):
    n, rows, cols = a.shape
    tr = _row_tile(rows, tr)

    def body(a_ref, b_ref, o_ref):
        o_ref[...] = (a_ref[...] + b_ref[...]).astype(BF16)

    spec = pl.BlockSpec((None, tr, cols), lambda j, i: (j, i, 0))
    return pl.pallas_call(body, name=name, grid=(n, rows // tr), in_specs=[spec, spec], out_specs=spec,
                          out_shape=jax.ShapeDtypeStruct(a.shape, BF16), compiler_params=_cp(("parallel", "parallel")))(a, b)


def _sum5(own, sib, recv, *, name, tr=256):
    rows, cols = own.shape
    tr = _row_tile(rows, tr)

    def body(a_ref, b_ref, r_ref, o_ref):
        acc = a_ref[...] + b_ref[...]
        for kk in range(3):
            acc = acc + r_ref[kk].astype(F32)
        o_ref[...] = acc

    spec = pl.BlockSpec((tr, cols), lambda i: (i, 0))
    return pl.pallas_call(body, name=name, grid=(rows // tr,),
                          in_specs=[spec, spec, pl.BlockSpec((3, tr, cols), lambda i: (0, i, 0))], out_specs=spec,
                          out_shape=jax.ShapeDtypeStruct((rows, cols), F32), compiler_params=_cp(("parallel",)))(own, sib, recv)


def _place():
    x, y, c = lax.axis_index("x"), lax.axis_index("y"), lax.axis_index("c")
    chips = [(1 - x, y), (x, 1 - y), (1 - x, 1 - y)]
    return x, y, c, 2 * x + y, chips


def _gather_weights(shards):
    n = len(shards)

    def body(*refs):
        ins, outs = refs[:n], refs[n:2 * n]
        send_i, recv_i, send_d, recv_d, loc = refs[2 * n:]
        x, y, c, me, chips = _place()
        local = [pltpu.make_async_copy(ins[t], outs[t].at[me], loc.at[t]) for t in range(n)]
        for cp in local:
            cp.start()

        def ici(t, kk, src_chip, to):
            return pltpu.make_async_remote_copy(
                src_ref=ins[t].at[c], dst_ref=outs[t].at[src_chip, c], send_sem=send_i.at[t * 3 + kk],
                recv_sem=recv_i.at[t * 3 + kk], device_id=to, device_id_type=MESH)

        def d2d(t, kk, src_chip, layer):
            return pltpu.make_async_remote_copy(
                src_ref=outs[t].at[src_chip, layer], dst_ref=outs[t].at[src_chip, layer], send_sem=send_d.at[t * 3 + kk],
                recv_sem=recv_d.at[t * 3 + kk], device_id=(x, y, 1 - c), device_id_type=MESH)

        sends = [ici(t, kk, me, (px, py, c)) for t in range(n) for kk, (px, py) in enumerate(chips)]
        for cp in sends:
            cp.start()
        fwd = []
        for t in range(n):
            for kk, (px, py) in enumerate(chips):
                ici(t, kk, 2 * px + py, (px, py, c)).wait_recv()
                f = d2d(t, kk, 2 * px + py, c)
                f.start()
                fwd.append(f)
        for t in range(n):
            for kk, (px, py) in enumerate(chips):
                d2d(t, kk, 2 * px + py, 1 - c).wait_recv()
        for cp in sends + fwd:
            cp.wait_send()
        for cp in local:
            cp.wait()

    return pl.pallas_call(
        body, name="gather_weights", in_specs=[HBM_SPEC] * n, out_specs=[HBM_SPEC] * n,
        out_shape=[jax.ShapeDtypeStruct((N_CHIPS,) + s.shape, s.dtype) for s in shards],
        scratch_shapes=[pltpu.SemaphoreType.DMA((3 * n,)), pltpu.SemaphoreType.DMA((3 * n,)),
                        pltpu.SemaphoreType.DMA((3 * n,)), pltpu.SemaphoreType.DMA((3 * n,)),
                        pltpu.SemaphoreType.DMA((n,))],
    )(*shards)


def _half(ref, chip, half):
    rows = ref.shape[1] // 2
    return ref.at[chip, pl.ds(half * rows, rows), :]


def _reduce_stage1(grads):
    n = len(grads)

    def body(*refs):
        ins, sib, own = refs[:n], refs[n:2 * n], refs[2 * n:3 * n]
        send, recv, loc = refs[3 * n:]
        x, y, c, me, chips = _place()
        rd = []
        for t in range(n):
            rows = ins[t].shape[1] // 2
            cp = pltpu.make_async_remote_copy(
                src_ref=ins[t].at[:, pl.ds((1 - c) * rows, rows), :], dst_ref=sib[t], send_sem=send.at[t],
                recv_sem=recv.at[t], device_id=(x, y, 1 - c), device_id_type=MESH)
            cp.start()
            rd.append(cp)
            lc = pltpu.make_async_copy(ins[t].at[:, pl.ds(c * rows, rows), :], own[t], loc.at[t])
            lc.start()
            rd.append(lc)
        for cp in rd:
            cp.wait()

    halves = [jax.ShapeDtypeStruct((N_CHIPS, g.shape[1] // 2, g.shape[2]), F32) for g in grads]
    outs = pl.pallas_call(
        body, name="reduce_stage1", in_specs=[HBM_SPEC] * n, out_specs=[HBM_SPEC] * (2 * n), out_shape=halves + halves,
        scratch_shapes=[pltpu.SemaphoreType.DMA((n,)), pltpu.SemaphoreType.DMA((n,)), pltpu.SemaphoreType.DMA((n,))],
    )(*grads)
    return outs[:n], outs[n:]


def _reduce_stage2(psum_bf16, own, sib):
    n = len(psum_bf16)

    def body(*refs):
        ps, ow, sb = refs[:n], refs[n:2 * n], refs[2 * n:3 * n]
        rcv, ow_me, sb_me = refs[3 * n:4 * n], refs[4 * n:5 * n], refs[5 * n:6 * n]
        send, recv, loc = refs[6 * n:]
        x, y, c, me, chips = _place()
        cps = []
        for t in range(n):
            for kk, (px, py) in enumerate(chips):
                cp = pltpu.make_async_remote_copy(
                    src_ref=ps[t].at[2 * px + py], dst_ref=rcv[t].at[kk], send_sem=send.at[t * 3 + kk],
                    recv_sem=recv.at[t * 3 + kk], device_id=(px, py, c), device_id_type=MESH)
                cp.start()
                cps.append(cp)
            for q, (src, dst) in enumerate(((ow[t], ow_me[t]), (sb[t], sb_me[t]))):
                lc = pltpu.make_async_copy(src.at[me], dst, loc.at[2 * t + q])
                lc.start()
                cps.append(lc)
        for cp in cps:
            cp.wait()

    shp = lambda a, lead: jax.ShapeDtypeStruct(lead + a.shape[1:], a.dtype)
    out_shape = ([shp(p, (3,)) for p in psum_bf16] + [shp(o, ()) for o in own] + [shp(o, ()) for o in sib])
    outs = pl.pallas_call(
        body, name="reduce_stage2", in_specs=[HBM_SPEC] * (3 * n), out_specs=[HBM_SPEC] * (3 * n), out_shape=out_shape,
        scratch_shapes=[pltpu.SemaphoreType.DMA((3 * n,)), pltpu.SemaphoreType.DMA((3 * n,)), pltpu.SemaphoreType.DMA((2 * n,))],
    )(*psum_bf16, *own, *sib)
    return outs[:n], outs[n:2 * n], outs[2 * n:]


def _reduce_stage3(halves, layer_of, n_out):
    n = len(halves)

    def body(*refs):
        ins, outs = refs[:n], refs[n:n + n_out]
        send, recv, loc = refs[n + n_out:]
        x, y, c, me, chips = _place()
        cps = []
        for t in range(n):
            o, layer = layer_of[t]
            rows = ins[t].shape[0]
            dst = outs[o].at[layer, pl.ds(c * rows, rows), :]
            cp = pltpu.make_async_remote_copy(src_ref=ins[t], dst_ref=dst, send_sem=send.at[t], recv_sem=recv.at[t],
                                              device_id=(x, y, 1 - c), device_id_type=MESH)
            cp.start()
            cps.append(cp)
            lc = pltpu.make_async_copy(ins[t], dst, loc.at[t])
            lc.start()
            cps.append(lc)
        for cp in cps:
            cp.wait()

    out_shape = [None] * n_out
    for t in range(n):
        o, _ = layer_of[t]
        out_shape[o] = jax.ShapeDtypeStruct((DEPTH, 2 * halves[t].shape[0], halves[t].shape[1]), F32)
    return pl.pallas_call(
        body, name="reduce_stage3", in_specs=[HBM_SPEC] * n, out_specs=[HBM_SPEC] * n_out, out_shape=out_shape,
        scratch_shapes=[pltpu.SemaphoreType.DMA((n,)), pltpu.SemaphoreType.DMA((n,)), pltpu.SemaphoreType.DMA((n,))],
    )(*halves)


def _allreduce_small(v):
    rows, cols = v.shape

    def body(v_ref, o_ref, buf, send, recv, loc):
        x, y, c, me, chips = _place()
        mine = 4 * x + 2 * y + c
        lc = pltpu.make_async_copy(v_ref, buf.at[mine], loc)
        lc.start()
        peers = []
        for fx in range(2):
            for fy in range(2):
                for fc in range(2):
                    if fx or fy or fc:
                        peers.append((fx, fy, fc))
        cps = []
        for kk, (fx, fy, fc) in enumerate(peers):
            to = (x ^ fx, y ^ fy, c ^ fc)
            cp = pltpu.make_async_remote_copy(src_ref=v_ref, dst_ref=buf.at[mine], send_sem=send.at[kk], recv_sem=recv.at[kk],
                                              device_id=to, device_id_type=MESH)
            cp.start()
            cps.append((cp, to))
        for kk, (cp, to) in enumerate(cps):
            src = 4 * to[0] + 2 * to[1] + to[2]
            pltpu.make_async_remote_copy(src_ref=v_ref, dst_ref=buf.at[src], send_sem=send.at[kk], recv_sem=recv.at[kk],
                                         device_id=to, device_id_type=MESH).wait_recv()
        for cp, _ in cps:
            cp.wait_send()
        lc.wait()
        acc = buf[0]
        for d in range(1, 8):
            acc = acc + buf[d]
        o_ref[...] = acc

    return pl.pallas_call(
        body, name="allreduce_small", in_specs=[pl.BlockSpec(memory_space=pltpu.VMEM)],
        out_specs=pl.BlockSpec(memory_space=pltpu.VMEM), out_shape=jax.ShapeDtypeStruct((rows, cols), F32),
        scratch_shapes=[pltpu.VMEM((8, rows, cols), F32), pltpu.SemaphoreType.DMA((7,)), pltpu.SemaphoreType.DMA((7,)),
                        pltpu.SemaphoreType.DMA],
        compiler_params=pltpu.CompilerParams(vmem_limit_bytes=VMEM_LIMIT_V7X),
    )(v)


def _q_perm():
    idx = [hh * (NOPE + ROPE) + d for hh in range(H) for d in range(NOPE)]
    idx += [hh * (NOPE + ROPE) + NOPE + e for hh in range(H) for e in range(HALF_ROPE)]
    idx += [hh * (NOPE + ROPE) + NOPE + HALF_ROPE + e for hh in range(H) for e in range(HALF_ROPE)]
    return np.array(idx, np.int32)


def _kv_perm():
    idx = [hh * (NOPE + VDIM) + d for hh in range(H) for d in range(NOPE)]
    idx += [hh * (NOPE + VDIM) + NOPE + d for hh in range(H) for d in range(VDIM)]
    return np.array(idx, np.int32)


def _inverse(perm):
    inv = np.empty_like(perm)
    inv[perm] = np.arange(perm.size, dtype=perm.dtype)
    return inv


def _pad_w_in(w):
    z = lambda n: jnp.zeros(w.shape[:-1] + (n,), w.dtype)
    return jnp.concatenate([w[..., 0:416], z(96), w[..., 416:1824], w[..., 1824:1830], z(122)], axis=-1)


def _unpad_w_in(w):
    return jnp.concatenate([w[..., 0:416], w[..., 512:1920], w[..., 1920:1926]], axis=-1)


def _block_diag(pw):
    out = jnp.zeros((POOL_W, POOL_W), pw.dtype)
    for g in range(4):
        out = out.at[g * 64:(g + 1) * 64, g * 64:(g + 1) * 64].set(pw[g])
    return out


def _heads(a, width):
    return a.reshape(a.shape[0], H, width).transpose(1, 0, 2)


def _unheads(a):
    return a.transpose(1, 0, 2).reshape(a.shape[1], -1)


def _rope_tables(s):
    inv_freq = ROPE_THETA ** (-jnp.arange(0, ROPE, 2, dtype=F32) / ROPE)
    ang = jnp.arange(s, dtype=jnp.int32).astype(F32)[:, None] * inv_freq[None, :]
    cos, sin = jnp.cos(ang), jnp.sin(ang)
    return jnp.tile(cos, (1, H + 1)), jnp.tile(sin, (1, H + 1))


def _mix_fwd(l, x1, wts, sm, cos, sin):
    s = x1.shape[0]
    z, h2 = _norm_mm(x1, 0, sm["mix_norm"][l], wts["w_in"][l], name=f"mix_in_{l}")
    q, qn = _norm_mm(z, Z_QA // Q_RANK, sm["q_a_norm"][l], wts["w_q_b"][l], name=f"mla_q_{l}")
    kv, kvn = _norm_mm(z, Z_KVA // KV_RANK, sm["kv_a_norm"][l], wts["w_kv_b"][l], name=f"mla_kv_{l}")
    nq = H * NOPE
    pe1 = jnp.concatenate([q[:, nq:nq + H * HALF_ROPE], z[:, Z_KR:Z_KR + HALF_ROPE]], axis=1)
    pe2 = jnp.concatenate([q[:, nq + H * HALF_ROPE:], z[:, Z_KR + HALF_ROPE:Z_KR + ROPE]], axis=1)
    r1, r2 = _rope(pe1, pe2, cos, sin, name=f"rope_{l}")
    hq = H * HALF_ROPE
    qh = jnp.concatenate([q[:, :nq].reshape(s, H, NOPE), r1[:, :hq].reshape(s, H, HALF_ROPE),
                          r2[:, :hq].reshape(s, H, HALF_ROPE)], axis=-1).transpose(1, 0, 2).astype(BF16)
    kpe = jnp.concatenate([r1[:, hq:], r2[:, hq:]], axis=1)
    kh = jnp.concatenate([kv[:, :nq].reshape(s, H, NOPE), jnp.broadcast_to(kpe[:, None, :], (s, H, ROPE))],
                         axis=-1).transpose(1, 0, 2).astype(BF16)
    vh = _heads(kv[:, nq:], VDIM).astype(BF16)
    oa, lse_a = _attn_fwd(qh, kh, vh, None, None, 1.0 / math.sqrt(NOPE + ROPE), name=f"mla_attn_{l}")

    bd = _block_diag(wts["pool_w"][l]).astype(BF16)
    yb, pooled = _pool_fwd(z, Z_POOL // POOL_W, bd, sm["pool_scale"][l], name=f"pool_{l}")

    fq = _heads(z[:, Z_FOX:Z_FOX + H * FOX_D], FOX_D).astype(BF16)
    fk = _heads(z[:, Z_FOX + H * FOX_D:Z_FOX + 2 * H * FOX_D], FOX_D).astype(BF16)
    fv = _heads(z[:, Z_FOX + 2 * H * FOX_D:Z_FOX + 3 * H * FOX_D], FOX_D).astype(BF16)
    ft = jnp.pad(z[:, Z_F:Z_F + H].T, ((0, 8 - H), (0, 0)))
    fb = jnp.pad(sm["fox_b_f"][l], (0, 8 - H)).reshape(8, 1)
    cum = _gate_fwd(ft, fb, name=f"fox_gate_{l}")
    cq = cum[:H].reshape(H, s, 1)
    ck = cum[:H].reshape(H, 1, s)
    oc, lse_c = _attn_fwd(fq, fk, fv, cq, ck, 1.0 / math.sqrt(FOX_D), name=f"fox_attn_{l}")

    cat = jnp.concatenate([_unheads(oa), yb, _unheads(oc)], axis=1).astype(BF16)
    x2 = _mm(cat, wts["w_out"][l], "nn", name=f"mix_out_{l}", add=x1, tn=1024, tk=1024)
    saved = dict(z=z, h2=h2, qn=qn, kvn=kvn, qh=qh, kh=kh, vh=vh, oa=oa, lse_a=lse_a, bd=bd, pooled=pooled,
                 fq=fq, fk=fk, fv=fv, ft=ft, fb=fb, cq=cq, ck=ck, oc=oc, lse_c=lse_c, cat=cat)
    return x2, saved


def _mix_bwd(l, x1, dx2, sv, wts, sm, cos, sin):
    s = x1.shape[0]
    g = {}
    dx2b = dx2.astype(BF16)
    g["w_out"] = _mm(sv["cat"], dx2b, "tn", name=f"d_w_out_{l}", tm=1024, tn=1024)
    dcat = _mm(dx2b, wts["w_out"][l], "nt", name=f"d_cat_{l}", tn=1024, tk=1024)
    nv = H * VDIM
    doa = _heads(dcat[:, :nv], VDIM)
    dyb = dcat[:, nv:nv + POOL_W]
    doc = _heads(dcat[:, nv + POOL_W:], FOX_D)

    dl_c = _attn_delta(sv["oc"], doc, name=f"fox_delta_{l}")
    docb = doc.astype(BF16)
    sc_c = 1.0 / math.sqrt(FOX_D)
    dfk, dfv, dck = _attn_bwd_kv(sv["fq"], sv["fk"], sv["fv"], docb, sv["lse_c"], dl_c, sv["cq"], sv["ck"], sc_c, name=f"fox_bwd_kv_{l}")
    dfq, dcq = _attn_bwd_q(sv["fq"], sv["fk"], sv["fv"], docb, sv["lse_c"], dl_c, sv["cq"], sv["ck"], sc_c, name=f"fox_bwd_q_{l}")
    dc = jnp.pad(dcq.reshape(H, s) + dck.reshape(H, s), ((0, 8 - H), (0, 0)))
    dft, dfb = _gate_bwd(sv["ft"], sv["fb"], dc, name=f"fox_gate_bwd_{l}")
    g["fox_b_f"] = dfb[:H, 0]

    dq, dys, g["pool_scale"] = _pool_bwd_a(dyb, sv["pooled"], sv["bd"], sm["pool_scale"][l], name=f"pool_bwd_a_{l}")
    du = _pool_bwd_b(dq, name=f"pool_bwd_b_{l}")
    dbd = _mm(sv["pooled"], dys, "tn", name=f"d_pool_w_{l}")
    g["pool_w"] = jnp.stack([dbd[i * 64:(i + 1) * 64, i * 64:(i + 1) * 64] for i in range(4)])

    dl_a = _attn_delta(sv["oa"], doa, name=f"mla_delta_{l}")
    doab = doa.astype(BF16)
    sc_a = 1.0 / math.sqrt(NOPE + ROPE)
    dkh, dvh = _attn_bwd_kv(sv["qh"], sv["kh"], sv["vh"], doab, sv["lse_a"], dl_a, None, None, sc_a, name=f"mla_bwd_kv_{l}")
    (dqh,) = _attn_bwd_q(sv["qh"], sv["kh"], sv["vh"], doab, sv["lse_a"], dl_a, None, None, sc_a, name=f"mla_bwd_q_{l}")
    dqs = dqh.transpose(1, 0, 2)
    dks = dkh.transpose(1, 0, 2)
    dr1 = jnp.concatenate([dqs[:, :, NOPE:NOPE + HALF_ROPE].reshape(s, -1), jnp.sum(dks[:, :, NOPE:NOPE + HALF_ROPE], axis=1)], axis=1)
    dr2 = jnp.concatenate([dqs[:, :, NOPE + HALF_ROPE:].reshape(s, -1), jnp.sum(dks[:, :, NOPE + HALF_ROPE:], axis=1)], axis=1)
    dpe1, dpe2 = _rope(dr1, dr2, cos, -sin, name=f"rope_bwd_{l}")
    hq = H * HALF_ROPE
    dq_full = jnp.concatenate([dqs[:, :, :NOPE].reshape(s, -1), dpe1[:, :hq], dpe2[:, :hq]], axis=1).astype(BF16)
    dkv_full = jnp.concatenate([dks[:, :, :NOPE].reshape(s, -1), _unheads(dvh)], axis=1).astype(BF16)
    g["w_q_b"] = _mm(sv["qn"], dq_full, "tn", name=f"d_w_q_b_{l}", tn=576)
    g["w_kv_b"] = _mm(sv["kvn"], dkv_full, "tn", name=f"d_w_kv_b_{l}", tn=768)
    dqn = _mm(dq_full, wts["w_q_b"][l], "nt", name=f"d_qn_{l}", tk=576)
    dkvn = _mm(dkv_full, wts["w_kv_b"][l], "nt", name=f"d_kvn_{l}", tk=768)
    dqa, g["q_a_norm"] = _rmsnorm_bwd(sv["z"], Z_QA // Q_RANK, sm["q_a_norm"][l], dqn, name=f"q_a_norm_bwd_{l}")
    dkva, g["kv_a_norm"] = _rmsnorm_bwd(sv["z"], Z_KVA // KV_RANK, sm["kv_a_norm"][l], dkvn, name=f"kv_a_norm_bwd_{l}")

    zeros = lambda n: jnp.zeros((s, n), BF16)
    dz = jnp.concatenate([
        dqa.astype(BF16), dkva.astype(BF16), dpe1[:, hq:].astype(BF16), dpe2[:, hq:].astype(BF16), zeros(128 - ROPE),
        du.astype(BF16), _unheads(dfq).astype(BF16), _unheads(dfk).astype(BF16), _unheads(dfv).astype(BF16),
        dft[:H].T.astype(BF16), zeros(128 - H)], axis=1)
    g["w_in"] = _mm(sv["h2"], dz, "tn", name=f"d_w_in_{l}", tm=1024, tn=1024)
    dh2 = _mm(dz, wts["w_in"][l], "nt", name=f"d_h2_{l}", tn=1024, tk=1024)
    dx1, g["mix_norm"] = _rmsnorm_bwd(x1, 0, sm["mix_norm"][l], dh2, dx2, name=f"mix_norm_bwd_{l}")
    return dx1, g


def _local_step(x, target, wts, sm):
    s = x.shape[0]
    cos, sin = _rope_tables(s)
    acts = []
    xs = x
    for l in range(DEPTH):
        x1, gu1 = _ffn_fwd(xs, sm["ffn1_norm"][l], wts["ffn1_w_gu"], wts["ffn1_w_d2"][l], l, name=f"ffn1_fwd_{l}")
        x2, sv = _mix_fwd(l, x1, wts, sm, cos, sin)
        x3, gu2 = _ffn_fwd(x2, sm["ffn2_norm"][l], wts["ffn2_w_gu"], wts["ffn2_w_d2"][l], l, name=f"ffn2_fwd_{l}")
        acts.append((xs, gu1, x1, sv, x2, gu2))
        xs = x3
    dx, g_final, loss = _loss_head(xs, sm["final_norm"], target, name="loss_head")
    grads = [dict() for _ in range(DEPTH)]
    for l in reversed(range(DEPTH)):
        x0, gu1, x1, sv, x2, gu2 = acts[l]
        g = grads[l]
        dx, dgu, act, hh, dy, g["ffn2_norm"] = _ffn_bwd(x2, dx, gu2, sm["ffn2_norm"][l], wts["ffn2_w_gu"], wts["ffn2_w_d2"][l], l,
                                                        name=f"ffn2_bwd_{l}")
        g["ffn2_w_down"] = _mm(act, dy, "tn", name=f"d_ffn2_w_down_{l}", tm=FF_SHARD, tn=1024)
        g["ffn2_w_gu"] = _mm(hh, dgu, "tn", name=f"d_ffn2_w_gu_{l}", tm=1024, tn=FF_SHARD, n_major_out=True)
        dx, gm = _mix_bwd(l, x1, dx, sv, wts, sm, cos, sin)
        g.update(gm)
        dx, dgu, act, hh, dy, g["ffn1_norm"] = _ffn_bwd(x0, dx, gu1, sm["ffn1_norm"][l], wts["ffn1_w_gu"], wts["ffn1_w_d2"][l], l,
                                                        name=f"ffn1_bwd_{l}")
        g["ffn1_w_down"] = _mm(act, dy, "tn", name=f"d_ffn1_w_down_{l}", tm=FF_SHARD, tn=1024)
        g["ffn1_w_gu"] = _mm(hh, dgu, "tn", name=f"d_ffn1_w_gu_{l}", tm=1024, tn=FF_SHARD, n_major_out=True)
    return loss, dx, grads, g_final


BIG = ["ffn1_w_gu", "ffn1_w_down", "w_in", "w_q_b", "w_kv_b", "w_out", "ffn2_w_gu", "ffn2_w_down"]
SMALL = ["ffn1_norm", "mix_norm", "q_a_norm", "kv_a_norm", "pool_w", "pool_scale", "fox_b_f", "ffn2_norm"]
SMALL_ROWS = 48


def _prepare_weights(gathered, params):
    qp, kp = _q_perm(), _kv_perm()
    wts = {"ffn1_w_gu": gathered["ffn1_w_gu"], "ffn2_w_gu": gathered["ffn2_w_gu"]}
    for nm in ("ffn1", "ffn2"):
        wd = gathered[nm + "_w_down"]
        wts[nm + "_w_d2"] = [wd[:, l].reshape(2, FF_SHARD, D) for l in range(DEPTH)]
    wts["w_in"] = [gathered["w_in"][:, l].reshape(D, NZ) for l in range(DEPTH)]
    wts["w_out"] = [gathered["w_out"][:, l].reshape(D, D) for l in range(DEPTH)]
    wts["w_q_b"] = [jnp.take(jnp.moveaxis(gathered["w_q_b"][:, l], 0, 1).reshape(Q_RANK, -1), qp, axis=1) for l in range(DEPTH)]
    wts["w_kv_b"] = [jnp.take(jnp.moveaxis(gathered["w_kv_b"][:, l], 0, 1).reshape(KV_RANK, -1), kp, axis=1) for l in range(DEPTH)]
    wts["pool_w"] = params["pool_w"]
    return wts


def _chip_major(name, g):
    if name in ("ffn1_w_gu", "ffn2_w_gu"):
        return g
    if name in ("ffn1_w_down", "ffn2_w_down", "w_in", "w_out"):
        return g.reshape(N_CHIPS, g.shape[0] // N_CHIPS, g.shape[1])
    perm = _q_perm() if name == "w_q_b" else _kv_perm()
    g = jnp.take(g, _inverse(perm), axis=1)
    return jnp.moveaxis(g.reshape(g.shape[0], N_CHIPS, g.shape[1] // N_CHIPS), 1, 0)


def _pack_small(grads, g_final):
    parts = []
    for l in range(DEPTH):
        for nm in SMALL:
            parts.append(grads[l][nm].reshape(-1))
    parts.append(g_final.reshape(-1))
    flat = jnp.concatenate(parts)
    return jnp.pad(flat, (0, SMALL_ROWS * D - flat.shape[0])).reshape(SMALL_ROWS, D)


def _unpack_small(packed, params):
    flat = packed.reshape(-1)
    out = {nm: [] for nm in SMALL}
    off = 0
    for l in range(DEPTH):
        for nm in SMALL:
            shp = params[nm].shape[1:]
            n = int(np.prod(shp))
            out[nm].append(flat[off:off + n].reshape(shp))
            off += n
    res = {nm: jnp.stack(v) for nm, v in out.items()}
    res["final_norm"] = flat[off:off + D]
    return res


def _update(name, w, g, m, v):
    shp = w.shape
    if w.ndim == 1:
        view = (1, shp[0])
    elif w.size <= 65536:
        view = (shp[0], w.size // shp[0])
    else:
        view = (w.size // shp[-1], shp[-1])
    tr = view[0]
    for cand in (512, 352, 256, 128):
        if view[0] % cand == 0 and view[0] > cand:
            tr = cand
            break
    d, mn, vn = _adamw(w.reshape(view), g.reshape(view), m.reshape(view), v.reshape(view), name="adamw_" + name, tr=tr)
    return d.reshape(shp), mn.reshape(shp), vn.reshape(shp)


WEIGHTS = ['ffn1_norm', 'ffn1_w_gu', 'ffn1_w_down', 'mix_norm', 'w_in', 'q_a_norm', 'w_q_b', 'kv_a_norm', 'w_kv_b', 'pool_w',
           'pool_scale', 'fox_b_f', 'w_out', 'ffn2_norm', 'ffn2_w_gu', 'ffn2_w_down', 'final_norm']


def kernel(x, ffn1_norm, ffn1_w_gu, ffn1_w_down, mix_norm, w_in, q_a_norm, w_q_b, kv_a_norm, w_kv_b, pool_w, pool_scale, fox_b_f, w_out, ffn2_norm, ffn2_w_gu, ffn2_w_down, final_norm, loss_target, m_ffn1_norm, m_ffn1_w_gu, m_ffn1_w_down, m_mix_norm, m_w_in, m_q_a_norm, m_w_q_b, m_kv_a_norm, m_w_kv_b, m_pool_w, m_pool_scale, m_fox_b_f, m_w_out, m_ffn2_norm, m_ffn2_w_gu, m_ffn2_w_down, m_final_norm, v_ffn1_norm, v_ffn1_w_gu, v_ffn1_w_down, v_mix_norm, v_w_in, v_q_a_norm, v_w_q_b, v_kv_a_norm, v_w_kv_b, v_pool_w, v_pool_scale, v_fox_b_f, v_w_out, v_ffn2_norm, v_ffn2_w_gu, v_ffn2_w_down, v_final_norm):
    params = dict(ffn1_norm=ffn1_norm, ffn1_w_gu=ffn1_w_gu, ffn1_w_down=ffn1_w_down, mix_norm=mix_norm, w_in=w_in, q_a_norm=q_a_norm,
                  w_q_b=w_q_b, kv_a_norm=kv_a_norm, w_kv_b=w_kv_b, pool_w=pool_w, pool_scale=pool_scale, fox_b_f=fox_b_f, w_out=w_out,
                  ffn2_norm=ffn2_norm, ffn2_w_gu=ffn2_w_gu, ffn2_w_down=ffn2_w_down, final_norm=final_norm)
    mom = dict(ffn1_norm=m_ffn1_norm, ffn1_w_gu=m_ffn1_w_gu, ffn1_w_down=m_ffn1_w_down, mix_norm=m_mix_norm, w_in=m_w_in,
               q_a_norm=m_q_a_norm, w_q_b=m_w_q_b, kv_a_norm=m_kv_a_norm, w_kv_b=m_w_kv_b, pool_w=m_pool_w, pool_scale=m_pool_scale,
               fox_b_f=m_fox_b_f, w_out=m_w_out, ffn2_norm=m_ffn2_norm, ffn2_w_gu=m_ffn2_w_gu, ffn2_w_down=m_ffn2_w_down,
               final_norm=m_final_norm)
    var = dict(ffn1_norm=v_ffn1_norm, ffn1_w_gu=v_ffn1_w_gu, ffn1_w_down=v_ffn1_w_down, mix_norm=v_mix_norm, w_in=v_w_in,
               q_a_norm=v_q_a_norm, w_q_b=v_w_q_b, kv_a_norm=v_kv_a_norm, w_kv_b=v_w_kv_b, pool_w=v_pool_w, pool_scale=v_pool_scale,
               fox_b_f=v_fox_b_f, w_out=v_w_out, ffn2_norm=v_ffn2_norm, ffn2_w_gu=v_ffn2_w_gu, ffn2_w_down=v_ffn2_w_down,
               final_norm=v_final_norm)

    shards = []
    for nm in BIG:
        w = params[nm]
        if nm == "w_in":
            w = _pad_w_in(w)
        shards.append(w.astype(BF16))
    gathered = dict(zip(BIG, _gather_weights(shards)))
    wts = _prepare_weights(gathered, params)

    loss, dx, grads, g_final = _local_step(x[0], loss_target[0], wts, params)
    loss = lax.psum(loss, ("x", "y", "c"))

    full = [_chip_major(nm, grads[l][nm]) for nm in BIG for l in range(DEPTH)]
    sib, own = _reduce_stage1(full)
    psum = [_sum2_bf16(o, sb, name=f"chip_sum_{t}") for t, (o, sb) in enumerate(zip(own, sib))]
    recv, own_me, sib_me = _reduce_stage2(psum, own, sib)
    halves = [_sum5(o, sb, r, name=f"grad_sum_{t}") for t, (o, sb, r) in enumerate(zip(own_me, sib_me, recv))]
    layer_of = [(t // DEPTH, t % DEPTH) for t in range(len(halves))]
    big_g = dict(zip(BIG, _reduce_stage3(halves, layer_of, len(BIG))))
    big_g["w_in"] = _unpad_w_in(big_g["w_in"])
    small_g = _unpack_small(_allreduce_small(_pack_small(grads, g_final)), params)
    gw = {**big_g, **small_g}

    delta, new_m, new_v = {}, {}, {}
    for nm in WEIGHTS:
        delta[nm], new_m[nm], new_v[nm] = _update(nm, params[nm], gw[nm], mom[nm], var[nm])
    return (loss, dx[None], *[gw[n] for n in WEIGHTS], *[delta[n] for n in WEIGHTS], *[new_m[n] for n in WEIGHTS],
            *[new_v[n] for n in WEIGHTS])
```

```python
import functools
import math

import jax
import jax.numpy as jnp
import numpy as np
from jax import lax
from jax.experimental import pallas as pl
from jax.experimental.pallas import tpu as pltpu

F32 = jnp.float32
BF16 = jnp.bfloat16
MESH = pl.DeviceIdType.MESH
HBM_SPEC = pl.BlockSpec(memory_space=pltpu.HBM)

D = 1024
DEPTH = 2
D_FF = 2816
FF_SHARD = 1408
N_CHIPS = 4
H = 6
NOPE, ROPE, VDIM = 64, 32, 64
HALF_ROPE = ROPE // 2
Q_RANK, KV_RANK = 256, 128
POOL_W = 256
FOX_D = 64
N_IN = 1830
NZ = 2048
ROPE_THETA = 10000.0
EPS = 1e-6
POOL_HALO = 16
Z_QA, Z_KVA, Z_KR, Z_POOL, Z_FOX, Z_F = 0, 256, 384, 512, 768, 1920

ADAM_LR, ADAM_B1, ADAM_B2, ADAM_EPS, ADAM_WD, ADAM_STEP = 0.001, 0.9, 0.999, 1e-08, 0.01, 10

VMEM_LIMIT_V7X = 56 * 1024 * 1024


def _cp(sem=None, vmem=VMEM_LIMIT_V7X):
    return pltpu.CompilerParams(dimension_semantics=sem, vmem_limit_bytes=vmem)


def _sigmoid(x):
    return 1.0 / (1.0 + jnp.exp(-x))


def _dot(a, b, dims):
    return lax.dot_general(a, b, (dims, ((), ())), preferred_element_type=F32)


NN = ((1,), (0,))
NT = ((1,), (1,))
TN = ((0,), (0,))


def _mm(a, b, mode, *, name, out_dtype=F32, add=None, alpha=None, tm=512, tn=512, tk=512, n_major_out=False):
    if mode == "nn":
        (m, k), (k2, n) = a.shape, b.shape
    elif mode == "nt":
        (m, k), (n, k2) = a.shape, b.shape
    else:
        (k, m), (k2, n) = a.shape, b.shape
    assert k == k2
    tm, tn, tk = min(tm, m), min(tn, n), min(tk, k)
    assert m % tm == 0 and n % tn == 0 and k % tk == 0, (name, m, n, k, tm, tn, tk)
    nk = k // tk
    dims = {"nn": NN, "nt": NT, "tn": TN}[mode]
    a_spec = pl.BlockSpec((tk, tm), lambda i, j, kk: (kk, i)) if mode == "tn" else pl.BlockSpec((tm, tk), lambda i, j, kk: (i, kk))
    b_spec = pl.BlockSpec((tn, tk), lambda i, j, kk: (j, kk)) if mode == "nt" else pl.BlockSpec((tk, tn), lambda i, j, kk: (kk, j))
    in_specs = [a_spec, b_spec]
    args = [a, b]
    if add is not None:
        in_specs.append(pl.BlockSpec((tm, tn), lambda i, j, kk: (i, j)))
        args.append(add)
    if n_major_out:
        out_shape = jax.ShapeDtypeStruct((n // tn, m, tn), out_dtype)
        out_spec = pl.BlockSpec((None, tm, tn), lambda i, j, kk: (j, i, 0))
    else:
        out_shape = jax.ShapeDtypeStruct((m, n), out_dtype)
        out_spec = pl.BlockSpec((tm, tn), lambda i, j, kk: (i, j))

    def body(*refs):
        a_ref, b_ref = refs[0], refs[1]
        add_ref = refs[2] if add is not None else None
        o_ref, acc = refs[-2], refs[-1]
        kk = pl.program_id(2)

        @pl.when(kk == 0)
        def _():
            acc[...] = jnp.zeros_like(acc)

        acc[...] += _dot(a_ref[...].astype(BF16), b_ref[...].astype(BF16), dims)

        @pl.when(kk == nk - 1)
        def _():
            r = acc[...]
            if alpha is not None:
                r = r * alpha
            if add_ref is not None:
                r = r + add_ref[...].astype(F32)
            o_ref[...] = r.astype(out_dtype)

    return pl.pallas_call(
        body, name=name, grid=(m // tm, n // tn, nk), in_specs=in_specs, out_specs=out_spec, out_shape=out_shape,
        scratch_shapes=[pltpu.VMEM((tm, tn), F32)],
        compiler_params=_cp(("parallel", "parallel", "arbitrary")),
    )(*args)


def _norm_mm(x, col_block, gain, w, *, name, tm=512):
    s = x.shape[0]
    k, n = w.shape
    tm = min(tm, s)

    def body(x_ref, g_ref, w_ref, z_ref, h_ref):
        xv = x_ref[...]
        r = lax.rsqrt(jnp.mean(xv * xv, axis=-1, keepdims=True) + EPS)
        hv = (xv * r * g_ref[...]).astype(BF16)
        h_ref[...] = hv
        z_ref[...] = _dot(hv, w_ref[...], NN)

    return pl.pallas_call(
        body, name=name, grid=(s // tm,),
        in_specs=[pl.BlockSpec((tm, k), lambda i: (i, col_block)), pl.BlockSpec((1, k), lambda i: (0, 0)),
                  pl.BlockSpec((k, n), lambda i: (0, 0))],
        out_specs=[pl.BlockSpec((tm, n), lambda i: (i, 0)), pl.BlockSpec((tm, k), lambda i: (i, 0))],
        out_shape=[jax.ShapeDtypeStruct((s, n), F32), jax.ShapeDtypeStruct((s, k), BF16)],
        compiler_params=_cp(("parallel",)),
    )(x, gain.reshape(1, k), w)


def _rmsnorm_bwd(x, col_block, gain, dh, dres=None, *, name, tm=512):
    s = x.shape[0]
    k = gain.shape[-1]
    tm = min(tm, s)

    def body(*refs):
        x_ref, g_ref, dh_ref = refs[0], refs[1], refs[2]
        dres_ref = refs[3] if dres is not None else None
        dx_ref, dg_ref = refs[-2], refs[-1]
        xv = x_ref[...]
        r = lax.rsqrt(jnp.mean(xv * xv, axis=-1, keepdims=True) + EPS)
        dhv = dh_ref[...].astype(F32)
        a = dhv * g_ref[...]
        dx = r * a - xv * (r * r * r) * jnp.mean(a * xv, axis=-1, keepdims=True)
        if dres_ref is not None:
            dx = dx + dres_ref[...]
        dx_ref[...] = dx

        @pl.when(pl.program_id(0) == 0)
        def _():
            dg_ref[...] = jnp.zeros_like(dg_ref)

        dg_ref[...] += jnp.sum(dhv * xv * r, axis=0, keepdims=True)

    in_specs = [pl.BlockSpec((tm, k), lambda i: (i, col_block)), pl.BlockSpec((1, k), lambda i: (0, 0)),
                pl.BlockSpec((tm, k), lambda i: (i, 0))]
    args = [x, gain.reshape(1, k), dh]
    if dres is not None:
        in_specs.append(pl.BlockSpec((tm, k), lambda i: (i, 0)))
        args.append(dres)
    dx, dg = pl.pallas_call(
        body, name=name, grid=(s // tm,), in_specs=in_specs,
        out_specs=[pl.BlockSpec((tm, k), lambda i: (i, 0)), pl.BlockSpec((1, k), lambda i: (0, 0))],
        out_shape=[jax.ShapeDtypeStruct((s, k), F32), jax.ShapeDtypeStruct((1, k), F32)],
        compiler_params=_cp(("arbitrary",)),
    )(*args)
    return dx, dg.reshape(k)


def _ffn_fwd(x, gain, w_gu4, w_d2, layer, *, name, tm=256):
    s = x.shape[0]
    tm = min(tm, s)

    def body(x_ref, g_ref, wgu_ref, wd_ref, xo_ref, gu_ref):
        xv = x_ref[...]
        r = lax.rsqrt(jnp.mean(xv * xv, axis=-1, keepdims=True) + EPS)
        hv = (xv * r * g_ref[...]).astype(BF16)
        y = jnp.zeros((tm, D), F32)
        for j in range(2):
            g = _dot(hv, wgu_ref[j], NN)
            u = _dot(hv, wgu_ref[2 + j], NN)
            gu_ref[:, j * FF_SHARD:(j + 1) * FF_SHARD] = g.astype(BF16)
            gu_ref[:, D_FF + j * FF_SHARD:D_FF + (j + 1) * FF_SHARD] = u.astype(BF16)
            act = (g * _sigmoid(g) * u).astype(BF16)
            y = y + _dot(act, wd_ref[j], NN)
        xo_ref[...] = xv + 0.5 * y

    return pl.pallas_call(
        body, name=name, grid=(s // tm,),
        in_specs=[pl.BlockSpec((tm, D), lambda i: (i, 0)), pl.BlockSpec((1, D), lambda i: (0, 0)),
                  pl.BlockSpec((N_CHIPS, None, D, FF_SHARD), lambda i: (0, layer, 0, 0), pipeline_mode=pl.Buffered(1)),
                  pl.BlockSpec((2, FF_SHARD, D), lambda i: (0, 0, 0), pipeline_mode=pl.Buffered(1))],
        out_specs=[pl.BlockSpec((tm, D), lambda i: (i, 0)), pl.BlockSpec((tm, 2 * D_FF), lambda i: (i, 0))],
        out_shape=[jax.ShapeDtypeStruct((s, D), F32), jax.ShapeDtypeStruct((s, 2 * D_FF), BF16)],
        compiler_params=_cp(("parallel",)),
    )(x, gain.reshape(1, D), w_gu4, w_d2)


def _ffn_bwd(x, dxo, gu, gain, w_gu4, w_d2, layer, *, name, tm=256):
    s = x.shape[0]
    tm = min(tm, s)

    def body(x_ref, dxo_ref, gu_ref, g_ref, wgu_ref, wd_ref, dx_ref, dgu_ref, act_ref, h_ref, dy_ref, dg_ref):
        xv = x_ref[...]
        r = lax.rsqrt(jnp.mean(xv * xv, axis=-1, keepdims=True) + EPS)
        xh = xv * r
        h_ref[...] = (xh * g_ref[...]).astype(BF16)
        dxov = dxo_ref[...]
        dy = (0.5 * dxov).astype(BF16)
        dy_ref[...] = dy
        dh = jnp.zeros((tm, D), F32)
        for j in range(2):
            g = gu_ref[:, j * FF_SHARD:(j + 1) * FF_SHARD].astype(F32)
            u = gu_ref[:, D_FF + j * FF_SHARD:D_FF + (j + 1) * FF_SHARD].astype(F32)
            sg = _sigmoid(g)
            silu = g * sg
            act_ref[:, j * FF_SHARD:(j + 1) * FF_SHARD] = (silu * u).astype(BF16)
            dact = _dot(dy, wd_ref[j], NT)
            dg = (dact * u * (sg * (1.0 + g * (1.0 - sg)))).astype(BF16)
            du = (dact * silu).astype(BF16)
            dgu_ref[:, j * FF_SHARD:(j + 1) * FF_SHARD] = dg
            dgu_ref[:, D_FF + j * FF_SHARD:D_FF + (j + 1) * FF_SHARD] = du
            dh = dh + _dot(dg, wgu_ref[j], NT) + _dot(du, wgu_ref[2 + j], NT)
        a = dh * g_ref[...]
        dx_ref[...] = dxov + r * a - xh * (r * jnp.mean(a * xh, axis=-1, keepdims=True))

        @pl.when(pl.program_id(0) == 0)
        def _():
            dg_ref[...] = jnp.zeros_like(dg_ref)

        dg_ref[...] += jnp.sum(dh * xh, axis=0, keepdims=True)

    row = lambda i: (i, 0)
    outs = pl.pallas_call(
        body, name=name, grid=(s // tm,),
        in_specs=[pl.BlockSpec((tm, D), row), pl.BlockSpec((tm, D), row), pl.BlockSpec((tm, 2 * D_FF), row),
                  pl.BlockSpec((1, D), lambda i: (0, 0)),
                  pl.BlockSpec((N_CHIPS, None, D, FF_SHARD), lambda i: (0, layer, 0, 0), pipeline_mode=pl.Buffered(1)),
                  pl.BlockSpec((2, FF_SHARD, D), lambda i: (0, 0, 0), pipeline_mode=pl.Buffered(1))],
        out_specs=[pl.BlockSpec((tm, D), row), pl.BlockSpec((tm, 2 * D_FF), row), pl.BlockSpec((tm, D_FF), row),
                   pl.BlockSpec((tm, D), row), pl.BlockSpec((tm, D), row), pl.BlockSpec((1, D), lambda i: (0, 0))],
        out_shape=[jax.ShapeDtypeStruct((s, D), F32), jax.ShapeDtypeStruct((s, 2 * D_FF), BF16),
                   jax.ShapeDtypeStruct((s, D_FF), BF16), jax.ShapeDtypeStruct((s, D), BF16),
                   jax.ShapeDtypeStruct((s, D), BF16), jax.ShapeDtypeStruct((1, D), F32)],
        compiler_params=_cp(("arbitrary",)),
    )(x, dxo, gu, gain.reshape(1, D), w_gu4, w_d2)
    dx, dgu, act, h, dy, dg = outs
    return dx, dgu, act, h, dy, dg.reshape(D)


def _rope(a1, a2, cos, sin, *, name):
    def body(a1_ref, a2_ref, c_ref, s_ref, o1_ref, o2_ref):
        x1, x2, c, sn = a1_ref[...], a2_ref[...], c_ref[...], s_ref[...]
        o1_ref[...] = x1 * c - x2 * sn
        o2_ref[...] = x2 * c + x1 * sn

    return pl.pallas_call(body, name=name, out_shape=[jax.ShapeDtypeStruct(a1.shape, F32)] * 2,
                          compiler_params=_cp())(a1, a2, cos, sin)


DA = 128
SCALE_MLA = 1.0 / math.sqrt(NOPE + ROPE)
SCALE_FOX = 1.0 / math.sqrt(FOX_D)


def _attn_fwd(qa, ka, va, dv, *, name, t=512):
    h, s, _ = qa.shape
    t = min(t, s)
    nb = s // t

    def body(q_ref, k_ref, v_ref, o_ref, lse_ref, m_sc, acc_sc):
        i, j = pl.program_id(1), pl.program_id(2)

        @pl.when(j == 0)
        def _():
            m_sc[...] = jnp.full_like(m_sc, -jnp.inf)
            acc_sc[...] = jnp.zeros_like(acc_sc)

        def step(masked):
            sc = _dot(q_ref[...], k_ref[...], NT)
            if masked:
                row = lax.broadcasted_iota(jnp.int32, (t, t), 0)
                col = lax.broadcasted_iota(jnp.int32, (t, t), 1)
                sc = jnp.where(col <= row, sc, -jnp.inf)
            m_old = m_sc[...]
            m_new = jnp.maximum(m_old, jnp.max(sc, axis=-1, keepdims=True))
            p = jnp.exp(sc - m_new)
            acc_sc[...] = jnp.exp(m_old - m_new) * acc_sc[...] + _dot(p.astype(BF16), v_ref[...], NN)
            m_sc[...] = m_new

        @pl.when(j < i)
        def _():
            step(False)

        @pl.when(j == i)
        def _():
            step(True)
            acc = acc_sc[...]
            l = acc[:, dv:dv + 1]
            o_ref[...] = acc[:, :dv] / l
            lse_ref[...] = m_sc[...] + jnp.log(l)

    qmap = lambda hh, i, j: (hh, i, 0)
    kmap = lambda hh, i, j: (hh, jnp.minimum(i, j), 0)
    return pl.pallas_call(
        body, name=name, grid=(h, nb, nb),
        in_specs=[pl.BlockSpec((None, t, DA), qmap), pl.BlockSpec((None, t, DA), kmap), pl.BlockSpec((None, t, DA), kmap)],
        out_specs=[pl.BlockSpec((None, t, dv), qmap), pl.BlockSpec((None, t, 1), qmap)],
        out_shape=[jax.ShapeDtypeStruct((h, s, dv), F32), jax.ShapeDtypeStruct((h, s, 1), F32)],
        scratch_shapes=[pltpu.VMEM((t, 1), F32), pltpu.VMEM((t, DA), F32)],
        compiler_params=_cp(("parallel", "parallel", "arbitrary")),
    )(qa, ka, va)


def _attn_delta(o, do, *, name):
    h, s, dv = o.shape

    def body(o_ref, do_ref, d_ref):
        d_ref[...] = jnp.sum(o_ref[...] * do_ref[...], axis=-1, keepdims=True)

    spec = pl.BlockSpec((None, s, dv), lambda hh: (hh, 0, 0))
    return pl.pallas_call(body, name=name, grid=(h,), in_specs=[spec, spec],
                          out_specs=pl.BlockSpec((None, s, 1), lambda hh: (hh, 0, 0)),
                          out_shape=jax.ShapeDtypeStruct((h, s, 1), F32), compiler_params=_cp(("parallel",)))(o, do)


def _attn_bwd(qa, ka, va, doa, lse_row, delta_row, decay, *, name, t=512):
    h, s, _ = qa.shape
    t = min(t, s)
    nb = s // t

    def body(*refs):
        q_ref, k_ref, v_ref, do_ref, lse_ref, dl_ref = refs[:6]
        if decay:
            dq_ref, dk_ref, dv_ref, dcq_ref, dck_ref, dq_acc, dk_acc, dv_acc, dcq_acc, dck_acc = refs[6:]
        else:
            dq_ref, dk_ref, dv_ref, dq_acc, dk_acc, dv_acc = refs[6:]
        j, i = pl.program_id(1), pl.program_id(2)

        @pl.when((j == 0) & (i == 0))
        def _():
            dq_acc[...] = jnp.zeros_like(dq_acc)
            if decay:
                dcq_acc[...] = jnp.zeros_like(dcq_acc)

        @pl.when(i == 0)
        def _():
            dk_acc[...] = jnp.zeros_like(dk_acc)
            dv_acc[...] = jnp.zeros_like(dv_acc)
            if decay:
                dck_acc[...] = jnp.zeros_like(dck_acc)

        def step(masked):
            st = _dot(k_ref[...], q_ref[...], NT)
            if masked:
                row = lax.broadcasted_iota(jnp.int32, (t, t), 0)
                col = lax.broadcasted_iota(jnp.int32, (t, t), 1)
                st = jnp.where(row <= col, st, -jnp.inf)
            pt = jnp.exp(st - lse_ref[...])
            dob = do_ref[...]
            dpt = _dot(v_ref[...], dob, NT)
            dst = pt * (dpt - dl_ref[...])
            dsb = dst.astype(BF16)
            dv_acc[...] += _dot(pt.astype(BF16), dob, NN)
            dk_acc[...] += _dot(dsb, q_ref[...], NN)
            dq_acc[i] += _dot(dsb, k_ref[...], TN)
            if decay:
                dcq_acc[i] += jnp.sum(dst, axis=0, keepdims=True)
                dck_acc[...] -= jnp.sum(dst, axis=1, keepdims=True)

        @pl.when(i > j)
        def _():
            step(False)

        @pl.when(i == j)
        def _():
            step(True)

        @pl.when(i == nb - 1)
        def _():
            dk_ref[...] = dk_acc[...]
            dv_ref[...] = dv_acc[...]
            if decay:
                dck_ref[...] = dck_acc[...]

        @pl.when((j == nb - 1) & (i == nb - 1))
        def _():
            dq_ref[...] = dq_acc[...]
            if decay:
                dcq_ref[...] = dcq_acc[...]

    kmap = lambda hh, j, i: (hh, j, 0)
    qmap = lambda hh, j, i: (hh, jnp.maximum(i, j), 0)
    qrow = lambda hh, j, i: (hh, 0, jnp.maximum(i, j))
    whole = lambda hh, j, i: (hh, 0, 0, 0)
    in_specs = [pl.BlockSpec((None, t, DA), qmap), pl.BlockSpec((None, t, DA), kmap), pl.BlockSpec((None, t, DA), kmap),
                pl.BlockSpec((None, t, DA), qmap), pl.BlockSpec((None, 1, t), qrow), pl.BlockSpec((None, 1, t), qrow)]
    out_specs = [pl.BlockSpec((None, nb, t, DA), whole), pl.BlockSpec((None, t, DA), kmap), pl.BlockSpec((None, t, DA), kmap)]
    out_shape = [jax.ShapeDtypeStruct((h, nb, t, DA), F32), jax.ShapeDtypeStruct((h, s, DA), F32), jax.ShapeDtypeStruct((h, s, DA), F32)]
    scratch = [pltpu.VMEM((nb, t, DA), F32), pltpu.VMEM((t, DA), F32), pltpu.VMEM((t, DA), F32)]
    if decay:
        out_specs += [pl.BlockSpec((None, nb, 1, t), whole), pl.BlockSpec((None, t, 1), kmap)]
        out_shape += [jax.ShapeDtypeStruct((h, nb, 1, t), F32), jax.ShapeDtypeStruct((h, s, 1), F32)]
        scratch += [pltpu.VMEM((nb, 1, t), F32), pltpu.VMEM((t, 1), F32)]
    outs = pl.pallas_call(
        body, name=name, grid=(h, nb, nb), in_specs=in_specs, out_specs=out_specs, out_shape=out_shape,
        scratch_shapes=scratch, compiler_params=_cp(("parallel", "arbitrary", "arbitrary")),
    )(qa, ka, va, doa, lse_row, delta_row)
    outs = list(outs)
    outs[0] = outs[0].reshape(h, s, DA)
    if decay:
        outs[3] = outs[3].reshape(h, 1, s)
    return outs


def _lane_scan(x, s, reverse):
    lane = lax.broadcasted_iota(jnp.int32, x.shape, 1)
    sh = 1
    while sh < s:
        if reverse:
            x = x + jnp.where(lane < s - sh, pltpu.roll(x, s - sh, axis=1), 0.0)
        else:
            x = x + jnp.where(lane >= sh, pltpu.roll(x, sh, axis=1), 0.0)
        sh *= 2
    return x


def _gate_fwd(ft, bias, *, name):
    s = ft.shape[1]

    def body(f_ref, b_ref, c_ref):
        xg = f_ref[...] + b_ref[...]
        lf = jnp.minimum(xg, 0.0) - jnp.log(1.0 + jnp.exp(-jnp.abs(xg)))
        c_ref[...] = _lane_scan(lf, s, False)

    return pl.pallas_call(body, name=name, out_shape=jax.ShapeDtypeStruct((8, s), F32), compiler_params=_cp())(ft, bias)


def _gate_bwd(ft, bias, dc, *, name):
    s = ft.shape[1]

    def body(f_ref, b_ref, dc_ref, df_ref, db_ref):
        xg = f_ref[...] + b_ref[...]
        dlf = _lane_scan(dc_ref[...], s, True)
        df = dlf * _sigmoid(-xg)
        df_ref[...] = df
        db_ref[...] = jnp.sum(df, axis=-1, keepdims=True)

    return pl.pallas_call(body, name=name, out_shape=[jax.ShapeDtypeStruct((8, s), F32), jax.ShapeDtypeStruct((8, 1), F32)],
                          compiler_params=_cp())(ft, bias, dc)


def _pool_lane_consts(tm, i):
    lane = lax.broadcasted_iota(jnp.int32, (tm, POOL_W), 1)
    tok = lax.broadcasted_iota(jnp.int32, (tm, POOL_W), 0) + i * tm
    win = jnp.where(lane < 64, 2, jnp.where(lane < 128, 4, jnp.where(lane < 192, 8, 16)))
    cnt = jnp.minimum(tok + 1, win).astype(F32)
    return lane, tok, cnt


def _pick_window(lane, s2, s4, s8, s16):
    return jnp.where(lane < 64, s2, jnp.where(lane < 128, s4, jnp.where(lane < 192, s8, s16)))


def _pool_fwd(z, col_block, bd, scale, *, name, tm=512):
    s = z.shape[0]
    tm = min(tm, s)
    hb = tm // POOL_HALO

    def body(u_ref, halo_ref, bd_ref, sc_ref, y_ref, p_ref, buf):
        i = pl.program_id(0)
        buf[0:POOL_HALO, :] = halo_ref[...] * (i > 0).astype(F32)
        buf[POOL_HALO:, :] = u_ref[...]

        def back(k):
            return buf[POOL_HALO - k:POOL_HALO - k + tm, :]

        u = u_ref[...]
        s2 = u + back(1)
        s4 = s2 + back(2) + back(3)
        s8 = s4 + back(4) + back(5) + back(6) + back(7)
        s16 = s8
        for k in range(8, 16):
            s16 = s16 + back(k)
        lane, _, cnt = _pool_lane_consts(tm, i)
        pooled = (_pick_window(lane, s2, s4, s8, s16) / cnt - u).astype(BF16)
        p_ref[...] = pooled
        y_ref[...] = _dot(pooled, bd_ref[...], NN) * sc_ref[...]

    return pl.pallas_call(
        body, name=name, grid=(s // tm,),
        in_specs=[pl.BlockSpec((tm, POOL_W), lambda i: (i, col_block)),
                  pl.BlockSpec((POOL_HALO, POOL_W), lambda i: (jnp.maximum(i * hb - 1, 0), col_block)),
                  pl.BlockSpec((POOL_W, POOL_W), lambda i: (0, 0)), pl.BlockSpec((1, POOL_W), lambda i: (0, 0))],
        out_specs=[pl.BlockSpec((tm, POOL_W), lambda i: (i, 0)), pl.BlockSpec((tm, POOL_W), lambda i: (i, 0))],
        out_shape=[jax.ShapeDtypeStruct((s, POOL_W), F32), jax.ShapeDtypeStruct((s, POOL_W), BF16)],
        scratch_shapes=[pltpu.VMEM((tm + POOL_HALO, POOL_W), F32)],
        compiler_params=_cp(("parallel",)),
    )(z, z, bd, scale.reshape(1, POOL_W))


def _pool_bwd_a(dy, pooled, bd, scale, *, name, tm=512):
    s = dy.shape[0]
    tm = min(tm, s)

    def body(dy_ref, p_ref, bd_ref, sc_ref, dq_ref, dys_ref, dsc_ref):
        i = pl.program_id(0)
        dyv = dy_ref[...]
        y0 = _dot(p_ref[...], bd_ref[...], NN)
        dys = (dyv * sc_ref[...]).astype(BF16)
        dys_ref[...] = dys
        dp = _dot(dys, bd_ref[...], NT)
        _, _, cnt = _pool_lane_consts(tm, i)
        dq_ref[:, 0:POOL_W] = dp / cnt
        dq_ref[:, POOL_W:] = dp

        @pl.when(i == 0)
        def _():
            dsc_ref[...] = jnp.zeros_like(dsc_ref)

        dsc_ref[...] += jnp.sum(dyv * y0, axis=0, keepdims=True)

    row = lambda i: (i, 0)
    dq, dys, dsc = pl.pallas_call(
        body, name=name, grid=(s // tm,),
        in_specs=[pl.BlockSpec((tm, POOL_W), row), pl.BlockSpec((tm, POOL_W), row),
                  pl.BlockSpec((POOL_W, POOL_W), lambda i: (0, 0)), pl.BlockSpec((1, POOL_W), lambda i: (0, 0))],
        out_specs=[pl.BlockSpec((tm, 2 * POOL_W), row), pl.BlockSpec((tm, POOL_W), row), pl.BlockSpec((1, POOL_W), lambda i: (0, 0))],
        out_shape=[jax.ShapeDtypeStruct((s, 2 * POOL_W), F32), jax.ShapeDtypeStruct((s, POOL_W), BF16),
                   jax.ShapeDtypeStruct((1, POOL_W), F32)],
        compiler_params=_cp(("arbitrary",)),
    )(dy, pooled, bd, scale.reshape(1, POOL_W))
    return dq, dys, dsc.reshape(POOL_W)


def _pool_bwd_b(dq, *, name, tm=512):
    s = dq.shape[0]
    tm = min(tm, s)
    hb = tm // POOL_HALO
    nblk = s // tm

    def body(q_ref, dp_ref, halo_ref, du_ref, buf):
        i = pl.program_id(0)
        buf[0:tm, :] = q_ref[...]
        buf[tm:, :] = halo_ref[...] * (i < nblk - 1).astype(F32)

        def ahead(k):
            return buf[k:k + tm, :]

        q = q_ref[...]
        s2 = q + ahead(1)
        s4 = s2 + ahead(2) + ahead(3)
        s8 = s4 + ahead(4) + ahead(5) + ahead(6) + ahead(7)
        s16 = s8
        for k in range(8, 16):
            s16 = s16 + ahead(k)
        lane = lax.broadcasted_iota(jnp.int32, (tm, POOL_W), 1)
        du_ref[...] = _pick_window(lane, s2, s4, s8, s16) - dp_ref[...]

    return pl.pallas_call(
        body, name=name, grid=(nblk,),
        in_specs=[pl.BlockSpec((tm, POOL_W), lambda i: (i, 0)), pl.BlockSpec((tm, POOL_W), lambda i: (i, 1)),
                  pl.BlockSpec((POOL_HALO, POOL_W), lambda i: (jnp.minimum((i + 1) * hb, nblk * hb - 1), 0))],
        out_specs=pl.BlockSpec((tm, POOL_W), lambda i: (i, 0)),
        out_shape=jax.ShapeDtypeStruct((s, POOL_W), F32),
        scratch_shapes=[pltpu.VMEM((tm + POOL_HALO, POOL_W), F32)],
        compiler_params=_cp(("parallel",)),
    )(dq, dq, dq)


def _loss_head(x, gain, target, *, name, tm=512):
    s = x.shape[0]
    tm = min(tm, s)

    def body(x_ref, g_ref, t_ref, dx_ref, dg_ref, loss_ref):
        xv = x_ref[...]
        r = lax.rsqrt(jnp.mean(xv * xv, axis=-1, keepdims=True) + EPS)
        xh = xv * r
        err = xh * g_ref[...] - t_ref[...]
        dy = err * (1.0 / D)
        a = dy * g_ref[...]
        dx_ref[...] = r * a - xh * (r * jnp.mean(a * xh, axis=-1, keepdims=True))

        @pl.when(pl.program_id(0) == 0)
        def _():
            dg_ref[...] = jnp.zeros_like(dg_ref)
            loss_ref[...] = jnp.zeros_like(loss_ref)

        dg_ref[...] += jnp.sum(dy * xh, axis=0, keepdims=True)
        part = 0.5 * jnp.sum(jnp.mean(err * err, axis=-1, keepdims=True), axis=0, keepdims=True)
        loss_ref[...] += jnp.broadcast_to(part, loss_ref.shape)

    row = lambda i: (i, 0)
    dx, dg, loss = pl.pallas_call(
        body, name=name, grid=(s // tm,),
        in_specs=[pl.BlockSpec((tm, D), row), pl.BlockSpec((1, D), lambda i: (0, 0)), pl.BlockSpec((tm, D), row)],
        out_specs=[pl.BlockSpec((tm, D), row), pl.BlockSpec((1, D), lambda i: (0, 0)), pl.BlockSpec((1, 128), lambda i: (0, 0))],
        out_shape=[jax.ShapeDtypeStruct((s, D), F32), jax.ShapeDtypeStruct((1, D), F32), jax.ShapeDtypeStruct((1, 128), F32)],
        compiler_params=_cp(("arbitrary",)),
    )(x, gain.reshape(1, D), target)
    return dx, dg.reshape(D), loss[0, 0]


def _adamw(w, g, m, v, *, name, tr=512):
    rows, cols = w.shape
    tr = min(tr, rows)
    assert rows % tr == 0, (name, rows, tr)
    c_m = 1.0 - ADAM_B1
    c_v = 1.0 - ADAM_B2
    bc1 = 1.0 - ADAM_B1 ** ADAM_STEP
    bc2 = 1.0 - ADAM_B2 ** ADAM_STEP

    def body(w_ref, g_ref, m_ref, v_ref, d_ref, mo_ref, vo_ref):
        gv = g_ref[...]
        mn = ADAM_B1 * m_ref[...] + c_m * gv
        vn = ADAM_B2 * v_ref[...] + c_v * (gv * gv)
        mo_ref[...] = mn
        vo_ref[...] = vn
        d_ref[...] = -ADAM_LR * ((mn / bc1) / (jnp.sqrt(vn / bc2) + ADAM_EPS) + ADAM_WD * w_ref[...])

    spec = pl.BlockSpec((tr, cols), lambda i: (i, 0))
    return pl.pallas_call(body, name=name, grid=(rows // tr,), in_specs=[spec] * 4, out_specs=[spec] * 3,
                          out_shape=[jax.ShapeDtypeStruct((rows, cols), F32)] * 3,
                          compiler_params=_cp(("parallel",)))(w, g, m, v)


def _row_tile(rows, cap):
    for t in range(min(cap, rows), 0, -1):
        if rows % t == 0 and t % 16 == 0:
            return t
    return rows


def _position():
    return jnp.stack([lax.axis_index("c"), 2 * lax.axis_index("x") + lax.axis_index("y")]).astype(jnp.int32)


def _sum2_bf16(pos, full, sib, *, name, tr=256):
    n, half, cols = sib.shape
    tr = _row_tile(half, tr)
    nb = half // tr

    def body(pos_ref, a_ref, b_ref, o_ref):
        o_ref[...] = (a_ref[...] + b_ref[...]).astype(BF16)

    spec = pl.BlockSpec((None, tr, cols), lambda j, i, p: (j, i, 0))
    return pl.pallas_call(
        body, name=name,
        grid_spec=pltpu.PrefetchScalarGridSpec(
            num_scalar_prefetch=1, grid=(n, nb),
            in_specs=[pl.BlockSpec((None, tr, cols), lambda j, i, p: (j, p[0] * nb + i, 0)), spec], out_specs=spec),
        out_shape=jax.ShapeDtypeStruct(sib.shape, BF16), compiler_params=_cp(("parallel", "parallel")))(pos, full, sib)


def _sum5(pos, full, sib, recv, *, name, tr=256):
    _, rows, cols = full.shape
    half = rows // 2
    tr = _row_tile(half, tr)
    nb = half // tr

    def body(pos_ref, a_ref, b_ref, r_ref, o_ref):
        acc = a_ref[...] + b_ref[...]
        for kk in range(3):
            acc = acc + r_ref[kk].astype(F32)
        o_ref[...] = acc

    return pl.pallas_call(
        body, name=name,
        grid_spec=pltpu.PrefetchScalarGridSpec(
            num_scalar_prefetch=1, grid=(nb,),
            in_specs=[pl.BlockSpec((None, tr, cols), lambda i, p: (p[1], p[0] * nb + i, 0)),
                      pl.BlockSpec((None, tr, cols), lambda i, p: (p[1], i, 0)),
                      pl.BlockSpec((3, tr, cols), lambda i, p: (0, i, 0))],
            out_specs=pl.BlockSpec((tr, cols), lambda i, p: (p[0] * nb + i, 0))),
        out_shape=jax.ShapeDtypeStruct((rows, cols), F32), compiler_params=_cp(("parallel",)))(pos, full, sib, recv)


def _place():
    x, y, c = lax.axis_index("x"), lax.axis_index("y"), lax.axis_index("c")
    chips = [(1 - x, y), (x, 1 - y), (1 - x, 1 - y)]
    return x, y, c, 2 * x + y, chips


def _gather_weights(shards):
    n = len(shards)

    def body(*refs):
        ins, outs = refs[:n], refs[n:2 * n]
        send_i, recv_i, send_d, recv_d, send_o, recv_o = refs[2 * n:]
        x, y, c, me, chips = _place()
        local = [pltpu.make_async_remote_copy(src_ref=ins[t], dst_ref=outs[t].at[me], send_sem=send_o.at[t], recv_sem=recv_o.at[t],
                                              device_id=(x, y, 1 - c), device_id_type=MESH) for t in range(n)]
        for cp in local:
            cp.start()

        def ici(t, kk, src_chip, to):
            return pltpu.make_async_remote_copy(
                src_ref=ins[t].at[c], dst_ref=outs[t].at[src_chip, c], send_sem=send_i.at[t * 3 + kk],
                recv_sem=recv_i.at[t * 3 + kk], device_id=to, device_id_type=MESH)

        def d2d(t, kk, src_chip, layer):
            return pltpu.make_async_remote_copy(
                src_ref=outs[t].at[src_chip, layer], dst_ref=outs[t].at[src_chip, layer], send_sem=send_d.at[t * 3 + kk],
                recv_sem=recv_d.at[t * 3 + kk], device_id=(x, y, 1 - c), device_id_type=MESH)

        sends = [ici(t, kk, me, (px, py, c)) for t in range(n) for kk, (px, py) in enumerate(chips)]
        for cp in sends:
            cp.start()
        fwd = []
        for t in range(n):
            for kk, (px, py) in enumerate(chips):
                ici(t, kk, 2 * px + py, (px, py, c)).wait_recv()
                f = d2d(t, kk, 2 * px + py, c)
                f.start()
                fwd.append(f)
        for t in range(n):
            for kk, (px, py) in enumerate(chips):
                d2d(t, kk, 2 * px + py, 1 - c).wait_recv()
        for cp in sends + fwd:
            cp.wait_send()
        for cp in local:
            cp.wait()

    return pl.pallas_call(
        body, name="gather_weights", in_specs=[HBM_SPEC] * n, out_specs=[HBM_SPEC] * n,
        out_shape=[jax.ShapeDtypeStruct((N_CHIPS,) + s.shape, s.dtype) for s in shards],
        scratch_shapes=[pltpu.SemaphoreType.DMA((3 * n,)), pltpu.SemaphoreType.DMA((3 * n,)),
                        pltpu.SemaphoreType.DMA((3 * n,)), pltpu.SemaphoreType.DMA((3 * n,)),
                        pltpu.SemaphoreType.DMA((n,)), pltpu.SemaphoreType.DMA((n,))],
    )(*shards)


def _reduce_stage1(grads):
    n = len(grads)

    def body(*refs):
        ins, sib = refs[:n], refs[n:2 * n]
        send, recv = refs[2 * n:]
        x, y, c, me, chips = _place()
        cps = []
        for t in range(n):
            rows = ins[t].shape[1] // 2
            cp = pltpu.make_async_remote_copy(
                src_ref=ins[t].at[:, pl.ds((1 - c) * rows, rows), :], dst_ref=sib[t], send_sem=send.at[t],
                recv_sem=recv.at[t], device_id=(x, y, 1 - c), device_id_type=MESH)
            cp.start()
            cps.append(cp)
        for cp in cps:
            cp.wait()

    return pl.pallas_call(
        body, name="reduce_stage1", in_specs=[HBM_SPEC] * n, out_specs=[HBM_SPEC] * n,
        out_shape=[jax.ShapeDtypeStruct((N_CHIPS, g.shape[1] // 2, g.shape[2]), F32) for g in grads],
        scratch_shapes=[pltpu.SemaphoreType.DMA((n,)), pltpu.SemaphoreType.DMA((n,))],
    )(*grads)


def _reduce_stage2(psum_bf16):
    n = len(psum_bf16)

    def body(*refs):
        ps, rcv = refs[:n], refs[n:2 * n]
        send, recv = refs[2 * n:]
        x, y, c, me, chips = _place()
        cps = []
        for t in range(n):
            for kk, (px, py) in enumerate(chips):
                cp = pltpu.make_async_remote_copy(
                    src_ref=ps[t].at[2 * px + py], dst_ref=rcv[t].at[kk], send_sem=send.at[t * 3 + kk],
                    recv_sem=recv.at[t * 3 + kk], device_id=(px, py, c), device_id_type=MESH)
                cp.start()
                cps.append(cp)
        for cp in cps:
            cp.wait()

    return pl.pallas_call(
        body, name="reduce_stage2", in_specs=[HBM_SPEC] * n, out_specs=[HBM_SPEC] * n,
        out_shape=[jax.ShapeDtypeStruct((3,) + p.shape[1:], p.dtype) for p in psum_bf16],
        scratch_shapes=[pltpu.SemaphoreType.DMA((3 * n,)), pltpu.SemaphoreType.DMA((3 * n,))],
    )(*psum_bf16)


def _reduce_stage3(reduced):
    n = len(reduced)

    def body(*refs):
        outs = refs[n:2 * n]
        send, recv = refs[2 * n:]
        x, y, c, me, chips = _place()
        cps = []
        for t in range(n):
            rows = outs[t].shape[0] // 2
            mine = outs[t].at[pl.ds(c * rows, rows), :]
            cp = pltpu.make_async_remote_copy(src_ref=mine, dst_ref=mine, send_sem=send.at[t], recv_sem=recv.at[t],
                                              device_id=(x, y, 1 - c), device_id_type=MESH)
            cp.start()
            cps.append(cp)
        for cp in cps:
            cp.wait()

    return pl.pallas_call(
        body, name="reduce_stage3", in_specs=[HBM_SPEC] * n, out_specs=[HBM_SPEC] * n,
        out_shape=[jax.ShapeDtypeStruct(r.shape, r.dtype) for r in reduced],
        input_output_aliases={t: t for t in range(n)},
        scratch_shapes=[pltpu.SemaphoreType.DMA((n,)), pltpu.SemaphoreType.DMA((n,))],
    )(*reduced)


def _allreduce_small(v):
    rows, cols = v.shape

    def body(v_ref, o_ref, buf, send, recv, loc):
        x, y, c, me, chips = _place()
        mine = 4 * x + 2 * y + c
        lc = pltpu.make_async_copy(v_ref, buf.at[mine], loc)
        lc.start()
        peers = []
        for fx in range(2):
            for fy in range(2):
                for fc in range(2):
                    if fx or fy or fc:
                        peers.append((fx, fy, fc))
        cps = []
        for kk, (fx, fy, fc) in enumerate(peers):
            to = (x ^ fx, y ^ fy, c ^ fc)
            cp = pltpu.make_async_remote_copy(src_ref=v_ref, dst_ref=buf.at[mine], send_sem=send.at[kk], recv_sem=recv.at[kk],
                                              device_id=to, device_id_type=MESH)
            cp.start()
            cps.append((cp, to))
        for kk, (cp, to) in enumerate(cps):
            src = 4 * to[0] + 2 * to[1] + to[2]
            pltpu.make_async_remote_copy(src_ref=v_ref, dst_ref=buf.at[src], send_sem=send.at[kk], recv_sem=recv.at[kk],
                                         device_id=to, device_id_type=MESH).wait_recv()
        for cp, _ in cps:
            cp.wait_send()
        lc.wait()
        acc = buf[0]
        for d in range(1, 8):
            acc = acc + buf[d]
        o_ref[...] = acc

    return pl.pallas_call(
        body, name="allreduce_small", in_specs=[pl.BlockSpec(memory_space=pltpu.VMEM)],
        out_specs=pl.BlockSpec(memory_space=pltpu.VMEM), out_shape=jax.ShapeDtypeStruct((rows, cols), F32),
        scratch_shapes=[pltpu.VMEM((8, rows, cols), F32), pltpu.SemaphoreType.DMA((7,)), pltpu.SemaphoreType.DMA((7,)),
                        pltpu.SemaphoreType.DMA],
        compiler_params=pltpu.CompilerParams(vmem_limit_bytes=VMEM_LIMIT_V7X),
    )(v)


def _q_perm():
    idx = [hh * (NOPE + ROPE) + d for hh in range(H) for d in range(NOPE)]
    idx += [hh * (NOPE + ROPE) + NOPE + e for hh in range(H) for e in range(HALF_ROPE)]
    idx += [hh * (NOPE + ROPE) + NOPE + HALF_ROPE + e for hh in range(H) for e in range(HALF_ROPE)]
    return np.array(idx, np.int32)


def _kv_perm():
    idx = [hh * (NOPE + VDIM) + d for hh in range(H) for d in range(NOPE)]
    idx += [hh * (NOPE + VDIM) + NOPE + d for hh in range(H) for d in range(VDIM)]
    return np.array(idx, np.int32)


def _inverse(perm):
    inv = np.empty_like(perm)
    inv[perm] = np.arange(perm.size, dtype=perm.dtype)
    return inv


def _pad_w_in(w):
    z = lambda n: jnp.zeros(w.shape[:-1] + (n,), w.dtype)
    return jnp.concatenate([w[..., 0:416], z(96), w[..., 416:1824], w[..., 1824:1830], z(122)], axis=-1)


def _unpad_w_in(w):
    return jnp.concatenate([w[..., 0:416], w[..., 512:1920], w[..., 1920:1926]], axis=-1)


def _block_diag(pw):
    out = jnp.zeros((POOL_W, POOL_W), pw.dtype)
    for g in range(4):
        out = out.at[g * 64:(g + 1) * 64, g * 64:(g + 1) * 64].set(pw[g])
    return out


def _heads(a, width):
    return a.reshape(a.shape[0], H, width).transpose(1, 0, 2)


def _unheads(a):
    return a.transpose(1, 0, 2).reshape(a.shape[1], -1)


def _rope_tables(s):
    inv_freq = ROPE_THETA ** (-jnp.arange(0, ROPE, 2, dtype=F32) / ROPE)
    ang = jnp.arange(s, dtype=jnp.int32).astype(F32)[:, None] * inv_freq[None, :]
    cos, sin = jnp.cos(ang), jnp.sin(ang)
    return jnp.tile(cos, (1, H + 1)), jnp.tile(sin, (1, H + 1))


def _mix_fwd(l, x1, wts, sm, cos, sin):
    s = x1.shape[0]
    z, h2 = _norm_mm(x1, 0, sm["mix_norm"][l], wts["w_in"][l], name=f"mix_in_{l}")
    q, qn = _norm_mm(z, Z_QA // Q_RANK, sm["q_a_norm"][l], wts["w_q_b"][l], name=f"mla_q_{l}")
    kv, kvn = _norm_mm(z, Z_KVA // KV_RANK, sm["kv_a_norm"][l], wts["w_kv_b"][l], name=f"mla_kv_{l}")
    nq = H * NOPE
    pe1 = jnp.concatenate([q[:, nq:nq + H * HALF_ROPE], z[:, Z_KR:Z_KR + HALF_ROPE]], axis=1)
    pe2 = jnp.concatenate([q[:, nq + H * HALF_ROPE:], z[:, Z_KR + HALF_ROPE:Z_KR + ROPE]], axis=1)
    r1, r2 = _rope(pe1, pe2, cos, sin, name=f"rope_{l}")
    hq = H * HALF_ROPE
    zpad = lambda n: jnp.zeros((s, H, n), F32)
    ones = lambda n: jnp.ones((s, H, n), F32)
    qh = jnp.concatenate([q[:, :nq].reshape(s, H, NOPE), r1[:, :hq].reshape(s, H, HALF_ROPE),
                          r2[:, :hq].reshape(s, H, HALF_ROPE)], axis=-1) * SCALE_MLA
    qa = jnp.concatenate([qh, zpad(DA - NOPE - ROPE)], axis=-1).transpose(1, 0, 2).astype(BF16)
    kpe = jnp.concatenate([r1[:, hq:], r2[:, hq:]], axis=1)
    ka = jnp.concatenate([kv[:, :nq].reshape(s, H, NOPE), jnp.broadcast_to(kpe[:, None, :], (s, H, ROPE)),
                          zpad(DA - NOPE - ROPE)], axis=-1).transpose(1, 0, 2).astype(BF16)
    va = jnp.concatenate([kv[:, nq:].reshape(s, H, VDIM), ones(1), zpad(DA - VDIM - 1)], axis=-1).transpose(1, 0, 2).astype(BF16)
    oa, lse_a = _attn_fwd(qa, ka, va, VDIM, name=f"mla_attn_{l}")

    bd = _block_diag(wts["pool_w"][l]).astype(BF16)
    yb, pooled = _pool_fwd(z, Z_POOL // POOL_W, bd, sm["pool_scale"][l], name=f"pool_{l}")

    ft = jnp.pad(z[:, Z_F:Z_F + H].T, ((0, 8 - H), (0, 0)))
    fb = jnp.pad(sm["fox_b_f"][l], (0, 8 - H)).reshape(8, 1)
    cum = _gate_fwd(ft, fb, name=f"fox_gate_{l}")
    c = cum[:H].T
    c_hi = lax.reduce_precision(c, 8, 7)
    r = c - c_hi
    c_mid = lax.reduce_precision(r, 8, 7)
    c_lo = lax.reduce_precision(r - c_mid, 8, 7)
    c3 = jnp.stack([c_hi, c_mid, c_lo], axis=-1)
    fqkv = z[:, Z_FOX:Z_FOX + 3 * H * FOX_D].reshape(s, 3, H, FOX_D)
    fqa = jnp.concatenate([fqkv[:, 0] * SCALE_FOX, c3, ones(3), zpad(DA - FOX_D - 6)], axis=-1).transpose(1, 0, 2).astype(BF16)
    fka = jnp.concatenate([fqkv[:, 1], ones(3), -c3, zpad(DA - FOX_D - 6)], axis=-1).transpose(1, 0, 2).astype(BF16)
    fva = jnp.concatenate([fqkv[:, 2], ones(1), zpad(DA - FOX_D - 1)], axis=-1).transpose(1, 0, 2).astype(BF16)
    oc, lse_c = _attn_fwd(fqa, fka, fva, FOX_D, name=f"fox_attn_{l}")

    cat = jnp.concatenate([_unheads(oa), yb, _unheads(oc)], axis=1).astype(BF16)
    x2 = _mm(cat, wts["w_out"][l], "nn", name=f"mix_out_{l}", add=x1, tn=1024, tk=1024)
    saved = dict(z=z, h2=h2, qn=qn, kvn=kvn, qa=qa, ka=ka, va=va, oa=oa, lse_a=lse_a, bd=bd, pooled=pooled,
                 fqa=fqa, fka=fka, fva=fva, ft=ft, fb=fb, oc=oc, lse_c=lse_c, cat=cat)
    return x2, saved


def _mix_bwd(l, x1, dx2, sv, wts, sm, cos, sin):
    s = x1.shape[0]
    g = {}
    dx2b = dx2.astype(BF16)
    g["w_out"] = _mm(sv["cat"], dx2b, "tn", name=f"d_w_out_{l}", tm=1024, tn=1024)
    dcat = _mm(dx2b, wts["w_out"][l], "nt", name=f"d_cat_{l}", tn=1024, tk=1024)
    nv = H * VDIM
    doa = _heads(dcat[:, :nv], VDIM)
    dyb = dcat[:, nv:nv + POOL_W]
    doc = _heads(dcat[:, nv + POOL_W:], FOX_D)
    pad_do = lambda d: jnp.pad(d, ((0, 0), (0, 0), (0, DA - d.shape[-1]))).astype(BF16)

    dl_c = _attn_delta(sv["oc"], doc, name=f"fox_delta_{l}")
    dfqa, dfka, dfva, dcq, dck = _attn_bwd(sv["fqa"], sv["fka"], sv["fva"], pad_do(doc), sv["lse_c"].reshape(H, 1, s),
                                           dl_c.reshape(H, 1, s), True, name=f"fox_attn_bwd_{l}")
    dfq = dfqa[:, :, :FOX_D] * SCALE_FOX
    dfk = dfka[:, :, :FOX_D]
    dfv = dfva[:, :, :FOX_D]
    dc = jnp.pad(dcq.reshape(H, s) + dck.reshape(H, s), ((0, 8 - H), (0, 0)))
    dft, dfb = _gate_bwd(sv["ft"], sv["fb"], dc, name=f"fox_gate_bwd_{l}")
    g["fox_b_f"] = dfb[:H, 0]

    dq, dys, g["pool_scale"] = _pool_bwd_a(dyb, sv["pooled"], sv["bd"], sm["pool_scale"][l], name=f"pool_bwd_a_{l}")
    du = _pool_bwd_b(dq, name=f"pool_bwd_b_{l}")
    dbd = _mm(sv["pooled"], dys, "tn", name=f"d_pool_w_{l}")
    g["pool_w"] = jnp.stack([dbd[i * 64:(i + 1) * 64, i * 64:(i + 1) * 64] for i in range(4)])

    dl_a = _attn_delta(sv["oa"], doa, name=f"mla_delta_{l}")
    dqa_, dka_, dva_ = _attn_bwd(sv["qa"], sv["ka"], sv["va"], pad_do(doa), sv["lse_a"].reshape(H, 1, s),
                                 dl_a.reshape(H, 1, s), False, name=f"mla_attn_bwd_{l}")
    dqs = dqa_[:, :, :NOPE + ROPE].transpose(1, 0, 2) * SCALE_MLA
    dks = dka_[:, :, :NOPE + ROPE].transpose(1, 0, 2)
    dvh = dva_[:, :, :VDIM]
    dr1 = jnp.concatenate([dqs[:, :, NOPE:NOPE + HALF_ROPE].reshape(s, -1), jnp.sum(dks[:, :, NOPE:NOPE + HALF_ROPE], axis=1)], axis=1)
    dr2 = jnp.concatenate([dqs[:, :, NOPE + HALF_ROPE:].reshape(s, -1), jnp.sum(dks[:, :, NOPE + HALF_ROPE:], axis=1)], axis=1)
    dpe1, dpe2 = _rope(dr1, dr2, cos, -sin, name=f"rope_bwd_{l}")
    hq = H * HALF_ROPE
    dq_full = jnp.concatenate([dqs[:, :, :NOPE].reshape(s, -1), dpe1[:, :hq], dpe2[:, :hq]], axis=1).astype(BF16)
    dkv_full = jnp.concatenate([dks[:, :, :NOPE].reshape(s, -1), _unheads(dvh)], axis=1).astype(BF16)
    g["w_q_b"] = _mm(sv["qn"], dq_full, "tn", name=f"d_w_q_b_{l}", tn=576)
    g["w_kv_b"] = _mm(sv["kvn"], dkv_full, "tn", name=f"d_w_kv_b_{l}", tn=768)
    dqn = _mm(dq_full, wts["w_q_b"][l], "nt", name=f"d_qn_{l}", tk=576)
    dkvn = _mm(dkv_full, wts["w_kv_b"][l], "nt", name=f"d_kvn_{l}", tk=768)
    dqa, g["q_a_norm"] = _rmsnorm_bwd(sv["z"], Z_QA // Q_RANK, sm["q_a_norm"][l], dqn, name=f"q_a_norm_bwd_{l}")
    dkva, g["kv_a_norm"] = _rmsnorm_bwd(sv["z"], Z_KVA // KV_RANK, sm["kv_a_norm"][l], dkvn, name=f"kv_a_norm_bwd_{l}")

    zeros = lambda n: jnp.zeros((s, n), BF16)
    dz = jnp.concatenate([
        dqa.astype(BF16), dkva.astype(BF16), dpe1[:, hq:].astype(BF16), dpe2[:, hq:].astype(BF16), zeros(128 - ROPE),
        du.astype(BF16), _unheads(dfq).astype(BF16), _unheads(dfk).astype(BF16), _unheads(dfv).astype(BF16),
        dft[:H].T.astype(BF16), zeros(128 - H)], axis=1)
    g["w_in"] = _mm(sv["h2"], dz, "tn", name=f"d_w_in_{l}", tm=1024, tn=1024)
    dh2 = _mm(dz, wts["w_in"][l], "nt", name=f"d_h2_{l}", tn=1024, tk=1024)
    dx1, g["mix_norm"] = _rmsnorm_bwd(x1, 0, sm["mix_norm"][l], dh2, dx2, name=f"mix_norm_bwd_{l}")
    return dx1, g


def _local_step(x, target, wts, sm):
    s = x.shape[0]
    cos, sin = _rope_tables(s)
    acts = []
    xs = x
    for l in range(DEPTH):
        x1, gu1 = _ffn_fwd(xs, sm["ffn1_norm"][l], wts["ffn1_w_gu"], wts["ffn1_w_d2"][l], l, name=f"ffn1_fwd_{l}")
        x2, sv = _mix_fwd(l, x1, wts, sm, cos, sin)
        x3, gu2 = _ffn_fwd(x2, sm["ffn2_norm"][l], wts["ffn2_w_gu"], wts["ffn2_w_d2"][l], l, name=f"ffn2_fwd_{l}")
        acts.append((xs, gu1, x1, sv, x2, gu2))
        xs = x3
    dx, g_final, loss = _loss_head(xs, sm["final_norm"], target, name="loss_head")
    grads = [dict() for _ in range(DEPTH)]
    for l in reversed(range(DEPTH)):
        x0, gu1, x1, sv, x2, gu2 = acts[l]
        g = grads[l]
        dx, dgu, act, hh, dy, g["ffn2_norm"] = _ffn_bwd(x2, dx, gu2, sm["ffn2_norm"][l], wts["ffn2_w_gu"], wts["ffn2_w_d2"][l], l,
                                                        name=f"ffn2_bwd_{l}")
        g["ffn2_w_down"] = _mm(act, dy, "tn", name=f"d_ffn2_w_down_{l}", tm=FF_SHARD, tn=1024)
        g["ffn2_w_gu"] = _mm(hh, dgu, "tn", name=f"d_ffn2_w_gu_{l}", tm=1024, tn=FF_SHARD, n_major_out=True)
        dx, gm = _mix_bwd(l, x1, dx, sv, wts, sm, cos, sin)
        g.update(gm)
        dx, dgu, act, hh, dy, g["ffn1_norm"] = _ffn_bwd(x0, dx, gu1, sm["ffn1_norm"][l], wts["ffn1_w_gu"], wts["ffn1_w_d2"][l], l,
                                                        name=f"ffn1_bwd_{l}")
        g["ffn1_w_down"] = _mm(act, dy, "tn", name=f"d_ffn1_w_down_{l}", tm=FF_SHARD, tn=1024)
        g["ffn1_w_gu"] = _mm(hh, dgu, "tn", name=f"d_ffn1_w_gu_{l}", tm=1024, tn=FF_SHARD, n_major_out=True)
    return loss, dx, grads, g_final


BIG = ["ffn1_w_gu", "ffn1_w_down", "w_in", "w_q_b", "w_kv_b", "w_out", "ffn2_w_gu", "ffn2_w_down"]
SMALL = ["ffn1_norm", "mix_norm", "q_a_norm", "kv_a_norm", "pool_w", "pool_scale", "fox_b_f", "ffn2_norm"]
SMALL_ROWS = 48


def _prepare_weights(gathered, params):
    qp, kp = _q_perm(), _kv_perm()
    wts = {"ffn1_w_gu": gathered["ffn1_w_gu"], "ffn2_w_gu": gathered["ffn2_w_gu"]}
    for nm in ("ffn1", "ffn2"):
        wd = gathered[nm + "_w_down"]
        wts[nm + "_w_d2"] = [wd[:, l].reshape(2, FF_SHARD, D) for l in range(DEPTH)]
    wts["w_in"] = [gathered["w_in"][:, l].reshape(D, NZ) for l in range(DEPTH)]
    wts["w_out"] = [gathered["w_out"][:, l].reshape(D, D) for l in range(DEPTH)]
    wts["w_q_b"] = [jnp.take(jnp.moveaxis(gathered["w_q_b"][:, l], 0, 1).reshape(Q_RANK, -1), qp, axis=1) for l in range(DEPTH)]
    wts["w_kv_b"] = [jnp.take(jnp.moveaxis(gathered["w_kv_b"][:, l], 0, 1).reshape(KV_RANK, -1), kp, axis=1) for l in range(DEPTH)]
    wts["pool_w"] = params["pool_w"]
    return wts


def _chip_major(name, g):
    if name in ("ffn1_w_gu", "ffn2_w_gu"):
        return g
    if name in ("ffn1_w_down", "ffn2_w_down", "w_in", "w_out"):
        return g.reshape(N_CHIPS, g.shape[0] // N_CHIPS, g.shape[1])
    perm = _q_perm() if name == "w_q_b" else _kv_perm()
    g = jnp.take(g, _inverse(perm), axis=1)
    return jnp.moveaxis(g.reshape(g.shape[0], N_CHIPS, g.shape[1] // N_CHIPS), 1, 0)


def _pack_small(grads, g_final):
    parts = []
    for l in range(DEPTH):
        for nm in SMALL:
            parts.append(grads[l][nm].reshape(-1))
    parts.append(g_final.reshape(-1))
    flat = jnp.concatenate(parts)
    return jnp.pad(flat, (0, SMALL_ROWS * D - flat.shape[0])).reshape(SMALL_ROWS, D)


def _unpack_small(packed, params):
    flat = packed.reshape(-1)
    out = {nm: [] for nm in SMALL}
    off = 0
    for l in range(DEPTH):
        for nm in SMALL:
            shp = params[nm].shape[1:]
            n = int(np.prod(shp))
            out[nm].append(flat[off:off + n].reshape(shp))
            off += n
    res = {nm: jnp.stack(v) for nm, v in out.items()}
    res["final_norm"] = flat[off:off + D]
    return res


def _update(name, w, g, m, v):
    shp = w.shape
    if w.ndim == 1:
        view = (1, shp[0])
    elif w.size <= 65536:
        view = (shp[0], w.size // shp[0])
    else:
        view = (w.size // shp[-1], shp[-1])
    tr = view[0]
    for cand in (512, 352, 256, 128):
        if view[0] % cand == 0 and view[0] > cand:
            tr = cand
            break
    d, mn, vn = _adamw(w.reshape(view), g.reshape(view), m.reshape(view), v.reshape(view), name="adamw_" + name, tr=tr)
    return d.reshape(shp), mn.reshape(shp), vn.reshape(shp)


WEIGHTS = ['ffn1_norm', 'ffn1_w_gu', 'ffn1_w_down', 'mix_norm', 'w_in', 'q_a_norm', 'w_q_b', 'kv_a_norm', 'w_kv_b', 'pool_w',
           'pool_scale', 'fox_b_f', 'w_out', 'ffn2_norm', 'ffn2_w_gu', 'ffn2_w_down', 'final_norm']


def kernel(x, ffn1_norm, ffn1_w_gu, ffn1_w_down, mix_norm, w_in, q_a_norm, w_q_b, kv_a_norm, w_kv_b, pool_w, pool_scale, fox_b_f, w_out, ffn2_norm, ffn2_w_gu, ffn2_w_down, final_norm, loss_target, m_ffn1_norm, m_ffn1_w_gu, m_ffn1_w_down, m_mix_norm, m_w_in, m_q_a_norm, m_w_q_b, m_kv_a_norm, m_w_kv_b, m_pool_w, m_pool_scale, m_fox_b_f, m_w_out, m_ffn2_norm, m_ffn2_w_gu, m_ffn2_w_down, m_final_norm, v_ffn1_norm, v_ffn1_w_gu, v_ffn1_w_down, v_mix_norm, v_w_in, v_q_a_norm, v_w_q_b, v_kv_a_norm, v_w_kv_b, v_pool_w, v_pool_scale, v_fox_b_f, v_w_out, v_ffn2_norm, v_ffn2_w_gu, v_ffn2_w_down, v_final_norm):
    params = dict(ffn1_norm=ffn1_norm, ffn1_w_gu=ffn1_w_gu, ffn1_w_down=ffn1_w_down, mix_norm=mix_norm, w_in=w_in, q_a_norm=q_a_norm,
                  w_q_b=w_q_b, kv_a_norm=kv_a_norm, w_kv_b=w_kv_b, pool_w=pool_w, pool_scale=pool_scale, fox_b_f=fox_b_f, w_out=w_out,
                  ffn2_norm=ffn2_norm, ffn2_w_gu=ffn2_w_gu, ffn2_w_down=ffn2_w_down, final_norm=final_norm)
    mom = dict(ffn1_norm=m_ffn1_norm, ffn1_w_gu=m_ffn1_w_gu, ffn1_w_down=m_ffn1_w_down, mix_norm=m_mix_norm, w_in=m_w_in,
               q_a_norm=m_q_a_norm, w_q_b=m_w_q_b, kv_a_norm=m_kv_a_norm, w_kv_b=m_w_kv_b, pool_w=m_pool_w, pool_scale=m_pool_scale,
               fox_b_f=m_fox_b_f, w_out=m_w_out, ffn2_norm=m_ffn2_norm, ffn2_w_gu=m_ffn2_w_gu, ffn2_w_down=m_ffn2_w_down,
               final_norm=m_final_norm)
    var = dict(ffn1_norm=v_ffn1_norm, ffn1_w_gu=v_ffn1_w_gu, ffn1_w_down=v_ffn1_w_down, mix_norm=v_mix_norm, w_in=v_w_in,
               q_a_norm=v_q_a_norm, w_q_b=v_w_q_b, kv_a_norm=v_kv_a_norm, w_kv_b=v_w_kv_b, pool_w=v_pool_w, pool_scale=v_pool_scale,
               fox_b_f=v_fox_b_f, w_out=v_w_out, ffn2_norm=v_ffn2_norm, ffn2_w_gu=v_ffn2_w_gu, ffn2_w_down=v_ffn2_w_down,
               final_norm=v_final_norm)

    shards = []
    for nm in BIG:
        w = params[nm]
        if nm == "w_in":
            w = _pad_w_in(w)
        shards.append(w.astype(BF16))
    gathered = dict(zip(BIG, _gather_weights(shards)))
    wts = _prepare_weights(gathered, params)

    loss, dx, grads, g_final = _local_step(x[0], loss_target[0], wts, params)
    loss = lax.psum(loss, ("x", "y", "c"))

    full = [_chip_major(nm, grads[l][nm]) for nm in BIG for l in range(DEPTH)]
    pos = _position()
    sib = _reduce_stage1(full)
    psum = [_sum2_bf16(pos, f, sb, name=f"chip_sum_{t}") for t, (f, sb) in enumerate(zip(full, sib))]
    recv = _reduce_stage2(psum)
    reduced = [_sum5(pos, f, sb, r, name=f"grad_sum_{t}") for t, (f, sb, r) in enumerate(zip(full, sib, recv))]
    whole = _reduce_stage3(reduced)
    big_g = {nm: jnp.stack(whole[DEPTH * k:DEPTH * (k + 1)]) for k, nm in enumerate(BIG)}
    big_g["w_in"] = _unpad_w_in(big_g["w_in"])
    small_g = _unpack_small(_allreduce_small(_pack_small(grads, g_final)), params)
    gw = {**big_g, **small_g}

    delta, new_m, new_v = {}, {}, {}
    for nm in WEIGHTS:
        delta[nm], new_m[nm], new_v[nm] = _update(nm, params[nm], gw[nm], mom[nm], var[nm])
    return (loss, dx[None], *[gw[n] for n in WEIGHTS], *[delta[n] for n in WEIGHTS], *[new_m[n] for n in WEIGHTS],
            *[new_v[n] for n in WEIGHTS])
```

```python
import functools
import math

import jax
import jax.numpy as jnp
import numpy as np
from jax import lax
from jax.experimental import pallas as pl
from jax.experimental.pallas import tpu as pltpu

F32 = jnp.float32
BF16 = jnp.bfloat16
MESH = pl.DeviceIdType.MESH
HBM_SPEC = pl.BlockSpec(memory_space=pltpu.HBM)

D = 1024
DEPTH = 2
D_FF = 2816
FF_SHARD = 1408
N_CHIPS = 4
H = 6
NOPE, ROPE, VDIM = 64, 32, 64
HALF_ROPE = ROPE // 2
Q_RANK, KV_RANK = 256, 128
POOL_W = 256
FOX_D = 64
N_IN = 1830
NZ = 2048
ROPE_THETA = 10000.0
EPS = 1e-6
POOL_HALO = 16
Z_QA, Z_KVA, Z_KR, Z_POOL, Z_FOX, Z_F = 0, 256, 384, 512, 768, 1920

ADAM_LR, ADAM_B1, ADAM_B2, ADAM_EPS, ADAM_WD, ADAM_STEP = 0.001, 0.9, 0.999, 1e-08, 0.01, 10

VMEM_LIMIT_V7X = 56 * 1024 * 1024


def _cp(sem=None, vmem=VMEM_LIMIT_V7X):
    return pltpu.CompilerParams(dimension_semantics=sem, vmem_limit_bytes=vmem)


def _sigmoid(x):
    return 1.0 / (1.0 + jnp.exp(-x))


def _dot(a, b, dims):
    return lax.dot_general(a, b, (dims, ((), ())), preferred_element_type=F32)


NN = ((1,), (0,))
NT = ((1,), (1,))
TN = ((0,), (0,))


def _mm(a, b, mode, *, name, out_dtype=F32, add=None, alpha=None, tm=512, tn=512, tk=512, n_major_out=False):
    if mode == "nn":
        (m, k), (k2, n) = a.shape, b.shape
    elif mode == "nt":
        (m, k), (n, k2) = a.shape, b.shape
    else:
        (k, m), (k2, n) = a.shape, b.shape
    assert k == k2
    tm, tn, tk = min(tm, m), min(tn, n), min(tk, k)
    assert m % tm == 0 and n % tn == 0 and k % tk == 0, (name, m, n, k, tm, tn, tk)
    nk = k // tk
    dims = {"nn": NN, "nt": NT, "tn": TN}[mode]
    a_spec = pl.BlockSpec((tk, tm), lambda i, j, kk: (kk, i)) if mode == "tn" else pl.BlockSpec((tm, tk), lambda i, j, kk: (i, kk))
    b_spec = pl.BlockSpec((tn, tk), lambda i, j, kk: (j, kk)) if mode == "nt" else pl.BlockSpec((tk, tn), lambda i, j, kk: (kk, j))
    in_specs = [a_spec, b_spec]
    args = [a, b]
    if add is not None:
        in_specs.append(pl.BlockSpec((tm, tn), lambda i, j, kk: (i, j)))
        args.append(add)
    if n_major_out:
        out_shape = jax.ShapeDtypeStruct((n // tn, m, tn), out_dtype)
        out_spec = pl.BlockSpec((None, tm, tn), lambda i, j, kk: (j, i, 0))
    else:
        out_shape = jax.ShapeDtypeStruct((m, n), out_dtype)
        out_spec = pl.BlockSpec((tm, tn), lambda i, j, kk: (i, j))

    def body(*refs):
        a_ref, b_ref = refs[0], refs[1]
        add_ref = refs[2] if add is not None else None
        o_ref, acc = refs[-2], refs[-1]
        kk = pl.program_id(2)

        @pl.when(kk == 0)
        def _():
            acc[...] = jnp.zeros_like(acc)

        acc[...] += _dot(a_ref[...].astype(BF16), b_ref[...].astype(BF16), dims)

        @pl.when(kk == nk - 1)
        def _():
            r = acc[...]
            if alpha is not None:
                r = r * alpha
            if add_ref is not None:
                r = r + add_ref[...].astype(F32)
            o_ref[...] = r.astype(out_dtype)

    return pl.pallas_call(
        body, name=name, grid=(m // tm, n // tn, nk), in_specs=in_specs, out_specs=out_spec, out_shape=out_shape,
        scratch_shapes=[pltpu.VMEM((tm, tn), F32)],
        compiler_params=_cp(("parallel", "parallel", "arbitrary")),
    )(*args)


def _norm_mm(x, col_block, gain, w, *, name, tm=512):
    s = x.shape[0]
    k, n = w.shape
    tm = min(tm, s)

    def body(x_ref, g_ref, w_ref, z_ref, h_ref):
        xv = x_ref[...]
        r = lax.rsqrt(jnp.mean(xv * xv, axis=-1, keepdims=True) + EPS)
        hv = (xv * r * g_ref[...]).astype(BF16)
        h_ref[...] = hv
        z_ref[...] = _dot(hv, w_ref[...], NN)

    return pl.pallas_call(
        body, name=name, grid=(s // tm,),
        in_specs=[pl.BlockSpec((tm, k), lambda i: (i, col_block)), pl.BlockSpec((1, k), lambda i: (0, 0)),
                  pl.BlockSpec((k, n), lambda i: (0, 0))],
        out_specs=[pl.BlockSpec((tm, n), lambda i: (i, 0)), pl.BlockSpec((tm, k), lambda i: (i, 0))],
        out_shape=[jax.ShapeDtypeStruct((s, n), F32), jax.ShapeDtypeStruct((s, k), BF16)],
        compiler_params=_cp(("parallel",)),
    )(x, gain.reshape(1, k), w)


def _rmsnorm_bwd(x, col_block, gain, dh, dres=None, *, name, tm=512):
    s = x.shape[0]
    k = gain.shape[-1]
    tm = min(tm, s)

    def body(*refs):
        x_ref, g_ref, dh_ref = refs[0], refs[1], refs[2]
        dres_ref = refs[3] if dres is not None else None
        dx_ref, dg_ref = refs[-2], refs[-1]
        xv = x_ref[...]
        r = lax.rsqrt(jnp.mean(xv * xv, axis=-1, keepdims=True) + EPS)
        dhv = dh_ref[...].astype(F32)
        a = dhv * g_ref[...]
        dx = r * a - xv * (r * r * r) * jnp.mean(a * xv, axis=-1, keepdims=True)
        if dres_ref is not None:
            dx = dx + dres_ref[...]
        dx_ref[...] = dx

        @pl.when(pl.program_id(0) == 0)
        def _():
            dg_ref[...] = jnp.zeros_like(dg_ref)

        dg_ref[...] += jnp.sum(dhv * xv * r, axis=0, keepdims=True)

    in_specs = [pl.BlockSpec((tm, k), lambda i: (i, col_block)), pl.BlockSpec((1, k), lambda i: (0, 0)),
                pl.BlockSpec((tm, k), lambda i: (i, 0))]
    args = [x, gain.reshape(1, k), dh]
    if dres is not None:
        in_specs.append(pl.BlockSpec((tm, k), lambda i: (i, 0)))
        args.append(dres)
    dx, dg = pl.pallas_call(
        body, name=name, grid=(s // tm,), in_specs=in_specs,
        out_specs=[pl.BlockSpec((tm, k), lambda i: (i, 0)), pl.BlockSpec((1, k), lambda i: (0, 0))],
        out_shape=[jax.ShapeDtypeStruct((s, k), F32), jax.ShapeDtypeStruct((1, k), F32)],
        compiler_params=_cp(("arbitrary",)),
    )(*args)
    return dx, dg.reshape(k)


def _ffn_fwd(x, gain, w_gu4, w_d2, layer, *, name, tm=256):
    s = x.shape[0]
    tm = min(tm, s)

    def body(x_ref, g_ref, wgu_ref, wd_ref, xo_ref, gu_ref):
        xv = x_ref[...]
        r = lax.rsqrt(jnp.mean(xv * xv, axis=-1, keepdims=True) + EPS)
        hv = (xv * r * g_ref[...]).astype(BF16)
        y = jnp.zeros((tm, D), F32)
        for j in range(2):
            g = _dot(hv, wgu_ref[j], NN)
            u = _dot(hv, wgu_ref[2 + j], NN)
            gu_ref[:, j * FF_SHARD:(j + 1) * FF_SHARD] = g.astype(BF16)
            gu_ref[:, D_FF + j * FF_SHARD:D_FF + (j + 1) * FF_SHARD] = u.astype(BF16)
            act = (g * _sigmoid(g) * u).astype(BF16)
            y = y + _dot(act, wd_ref[j], NN)
        xo_ref[...] = xv + 0.5 * y

    return pl.pallas_call(
        body, name=name, grid=(s // tm,),
        in_specs=[pl.BlockSpec((tm, D), lambda i: (i, 0)), pl.BlockSpec((1, D), lambda i: (0, 0)),
                  pl.BlockSpec((N_CHIPS, None, D, FF_SHARD), lambda i: (0, layer, 0, 0), pipeline_mode=pl.Buffered(1)),
                  pl.BlockSpec((2, FF_SHARD, D), lambda i: (0, 0, 0), pipeline_mode=pl.Buffered(1))],
        out_specs=[pl.BlockSpec((tm, D), lambda i: (i, 0)), pl.BlockSpec((tm, 2 * D_FF), lambda i: (i, 0))],
        out_shape=[jax.ShapeDtypeStruct((s, D), F32), jax.ShapeDtypeStruct((s, 2 * D_FF), BF16)],
        compiler_params=_cp(("parallel",)),
    )(x, gain.reshape(1, D), w_gu4, w_d2)


def _ffn_bwd(x, dxo, gu, gain, w_gu4, w_d2, layer, *, name, tm=256):
    s = x.shape[0]
    tm = min(tm, s)

    def body(x_ref, dxo_ref, gu_ref, g_ref, wgu_ref, wd_ref, dx_ref, dgu_ref, act_ref, h_ref, dy_ref, dg_ref):
        xv = x_ref[...]
        r = lax.rsqrt(jnp.mean(xv * xv, axis=-1, keepdims=True) + EPS)
        xh = xv * r
        h_ref[...] = (xh * g_ref[...]).astype(BF16)
        dxov = dxo_ref[...]
        dy = (0.5 * dxov).astype(BF16)
        dy_ref[...] = dy
        dh = jnp.zeros((tm, D), F32)
        for j in range(2):
            g = gu_ref[:, j * FF_SHARD:(j + 1) * FF_SHARD].astype(F32)
            u = gu_ref[:, D_FF + j * FF_SHARD:D_FF + (j + 1) * FF_SHARD].astype(F32)
            sg = _sigmoid(g)
            silu = g * sg
            act_ref[:, j * FF_SHARD:(j + 1) * FF_SHARD] = (silu * u).astype(BF16)
            dact = _dot(dy, wd_ref[j], NT)
            dg = (dact * u * (sg * (1.0 + g * (1.0 - sg)))).astype(BF16)
            du = (dact * silu).astype(BF16)
            dgu_ref[:, j * FF_SHARD:(j + 1) * FF_SHARD] = dg
            dgu_ref[:, D_FF + j * FF_SHARD:D_FF + (j + 1) * FF_SHARD] = du
            dh = dh + _dot(dg, wgu_ref[j], NT) + _dot(du, wgu_ref[2 + j], NT)
        a = dh * g_ref[...]
        dx_ref[...] = dxov + r * a - xh * (r * jnp.mean(a * xh, axis=-1, keepdims=True))

        @pl.when(pl.program_id(0) == 0)
        def _():
            dg_ref[...] = jnp.zeros_like(dg_ref)

        dg_ref[...] += jnp.sum(dh * xh, axis=0, keepdims=True)

    row = lambda i: (i, 0)
    outs = pl.pallas_call(
        body, name=name, grid=(s // tm,),
        in_specs=[pl.BlockSpec((tm, D), row), pl.BlockSpec((tm, D), row), pl.BlockSpec((tm, 2 * D_FF), row),
                  pl.BlockSpec((1, D), lambda i: (0, 0)),
                  pl.BlockSpec((N_CHIPS, None, D, FF_SHARD), lambda i: (0, layer, 0, 0), pipeline_mode=pl.Buffered(1)),
                  pl.BlockSpec((2, FF_SHARD, D), lambda i: (0, 0, 0), pipeline_mode=pl.Buffered(1))],
        out_specs=[pl.BlockSpec((tm, D), row), pl.BlockSpec((tm, 2 * D_FF), row), pl.BlockSpec((tm, D_FF), row),
                   pl.BlockSpec((tm, D), row), pl.BlockSpec((tm, D), row), pl.BlockSpec((1, D), lambda i: (0, 0))],
        out_shape=[jax.ShapeDtypeStruct((s, D), F32), jax.ShapeDtypeStruct((s, 2 * D_FF), BF16),
                   jax.ShapeDtypeStruct((s, D_FF), BF16), jax.ShapeDtypeStruct((s, D), BF16),
                   jax.ShapeDtypeStruct((s, D), BF16), jax.ShapeDtypeStruct((1, D), F32)],
        compiler_params=_cp(("arbitrary",)),
    )(x, dxo, gu, gain.reshape(1, D), w_gu4, w_d2)
    dx, dgu, act, h, dy, dg = outs
    return dx, dgu, act, h, dy, dg.reshape(D)


def _rope(a1, a2, cos, sin, *, name):
    def body(a1_ref, a2_ref, c_ref, s_ref, o1_ref, o2_ref):
        x1, x2, c, sn = a1_ref[...], a2_ref[...], c_ref[...], s_ref[...]
        o1_ref[...] = x1 * c - x2 * sn
        o2_ref[...] = x2 * c + x1 * sn

    return pl.pallas_call(body, name=name, out_shape=[jax.ShapeDtypeStruct(a1.shape, F32)] * 2,
                          compiler_params=_cp())(a1, a2, cos, sin)


DA = 128
SCALE_MLA = 1.0 / math.sqrt(NOPE + ROPE)
SCALE_FOX = 1.0 / math.sqrt(FOX_D)


def _causal_blocks(nb, key_major):
    if key_major:
        pairs = [(i, j) for j in range(nb) for i in range(j, nb)]
    else:
        pairs = [(i, j) for i in range(nb) for j in range(i + 1)]
    return (jnp.asarray(np.array([p[0] for p in pairs], np.int32)), jnp.asarray(np.array([p[1] for p in pairs], np.int32)))


HEADS_PER_STEP = 2


def _attn_fwd(qa, ka, va, dv, *, name, t=512):
    h, s, _ = qa.shape
    t = min(t, s)
    nb = s // t
    g = HEADS_PER_STEP
    qi, kj = _causal_blocks(nb, key_major=False)

    def body(qi_ref, kj_ref, q_ref, k_ref, v_ref, o_ref, lse_ref, m_sc, acc_sc):
        n = pl.program_id(1)
        i, j = qi_ref[n], kj_ref[n]

        @pl.when(j == 0)
        def _():
            m_sc[...] = jnp.full_like(m_sc, -jnp.inf)
            acc_sc[...] = jnp.zeros_like(acc_sc)

        def step(masked):
            for hh in range(g):
                sc = _dot(q_ref[hh], k_ref[hh], NT)
                if masked:
                    row = lax.broadcasted_iota(jnp.int32, (t, t), 0)
                    col = lax.broadcasted_iota(jnp.int32, (t, t), 1)
                    sc = jnp.where(col <= row, sc, -jnp.inf)
                m_old = m_sc[hh]
                m_new = jnp.maximum(m_old, jnp.max(sc, axis=-1, keepdims=True))
                p = jnp.exp(sc - m_new)
                acc_sc[hh] = jnp.exp(m_old - m_new) * acc_sc[hh] + _dot(p.astype(BF16), v_ref[hh], NN)
                m_sc[hh] = m_new

        @pl.when(j < i)
        def _():
            step(False)

        @pl.when(j == i)
        def _():
            step(True)
            for hh in range(g):
                acc = acc_sc[hh]
                l = acc[:, dv:dv + 1]
                o_ref[hh] = acc[:, :dv] / l
                lse_ref[hh] = m_sc[hh] + jnp.log(l)

    qmap = lambda hg, n, qi_r, kj_r: (hg, qi_r[n], 0)
    kmap = lambda hg, n, qi_r, kj_r: (hg, kj_r[n], 0)
    return pl.pallas_call(
        body, name=name,
        grid_spec=pltpu.PrefetchScalarGridSpec(
            num_scalar_prefetch=2, grid=(h // g, qi.shape[0]),
            in_specs=[pl.BlockSpec((g, t, DA), qmap), pl.BlockSpec((g, t, DA), kmap), pl.BlockSpec((g, t, DA), kmap)],
            out_specs=[pl.BlockSpec((g, t, dv), qmap), pl.BlockSpec((g, t, 1), qmap)],
            scratch_shapes=[pltpu.VMEM((g, t, 1), F32), pltpu.VMEM((g, t, DA), F32)]),
        out_shape=[jax.ShapeDtypeStruct((h, s, dv), F32), jax.ShapeDtypeStruct((h, s, 1), F32)],
        compiler_params=_cp(("parallel", "arbitrary")),
    )(qi, kj, qa, ka, va)


def _attn_delta(o, do, *, name):
    h, s, dv = o.shape

    def body(o_ref, do_ref, d_ref):
        d_ref[...] = jnp.sum(o_ref[...] * do_ref[...], axis=-1, keepdims=True)

    spec = pl.BlockSpec((None, s, dv), lambda hh: (hh, 0, 0))
    return pl.pallas_call(body, name=name, grid=(h,), in_specs=[spec, spec],
                          out_specs=pl.BlockSpec((None, s, 1), lambda hh: (hh, 0, 0)),
                          out_shape=jax.ShapeDtypeStruct((h, s, 1), F32), compiler_params=_cp(("parallel",)))(o, do)


def _attn_bwd(qa, ka, va, doa, lse_row, delta_row, decay, *, name, t=512):
    h, s, _ = qa.shape
    t = min(t, s)
    nb = s // t
    qi, kj = _causal_blocks(nb, key_major=True)
    nsteps = qi.shape[0]

    def body(*refs):
        qi_ref, kj_ref, q_ref, k_ref, v_ref, do_ref, lse_ref, dl_ref = refs[:8]
        if decay:
            dq_ref, dk_ref, dv_ref, dcq_ref, dck_ref, dq_acc, dk_acc, dv_acc, dcq_acc, dck_acc = refs[8:]
        else:
            dq_ref, dk_ref, dv_ref, dq_acc, dk_acc, dv_acc = refs[8:]
        n = pl.program_id(1)
        i, j = qi_ref[n], kj_ref[n]

        @pl.when(n == 0)
        def _():
            dq_acc[...] = jnp.zeros_like(dq_acc)
            if decay:
                dcq_acc[...] = jnp.zeros_like(dcq_acc)

        @pl.when(i == j)
        def _():
            dk_acc[...] = jnp.zeros_like(dk_acc)
            dv_acc[...] = jnp.zeros_like(dv_acc)
            if decay:
                dck_acc[...] = jnp.zeros_like(dck_acc)

        def step(masked):
            st = _dot(k_ref[...], q_ref[...], NT)
            if masked:
                row = lax.broadcasted_iota(jnp.int32, (t, t), 0)
                col = lax.broadcasted_iota(jnp.int32, (t, t), 1)
                st = jnp.where(row <= col, st, -jnp.inf)
            pt = jnp.exp(st - lse_ref[...])
            dob = do_ref[...]
            dpt = _dot(v_ref[...], dob, NT)
            dst = pt * (dpt - dl_ref[...])
            dsb = dst.astype(BF16)
            dv_acc[...] += _dot(pt.astype(BF16), dob, NN)
            dk_acc[...] += _dot(dsb, q_ref[...], NN)
            dq_acc[i] += _dot(dsb, k_ref[...], TN)
            if decay:
                dcq_acc[i] += jnp.sum(dst, axis=0, keepdims=True)
                dck_acc[...] -= jnp.sum(dst, axis=1, keepdims=True)

        @pl.when(i > j)
        def _():
            step(False)

        @pl.when(i == j)
        def _():
            step(True)

        @pl.when(i == nb - 1)
        def _():
            dk_ref[...] = dk_acc[...]
            dv_ref[...] = dv_acc[...]
            if decay:
                dck_ref[...] = dck_acc[...]

        @pl.when(n == nsteps - 1)
        def _():
            dq_ref[...] = dq_acc[...]
            if decay:
                dcq_ref[...] = dcq_acc[...]

    kmap = lambda hh, n, qi_r, kj_r: (hh, kj_r[n], 0)
    qmap = lambda hh, n, qi_r, kj_r: (hh, qi_r[n], 0)
    qrow = lambda hh, n, qi_r, kj_r: (hh, 0, qi_r[n])
    whole = lambda hh, n, qi_r, kj_r: (hh, 0, 0, 0)
    in_specs = [pl.BlockSpec((None, t, DA), qmap), pl.BlockSpec((None, t, DA), kmap), pl.BlockSpec((None, t, DA), kmap),
                pl.BlockSpec((None, t, DA), qmap), pl.BlockSpec((None, 1, t), qrow), pl.BlockSpec((None, 1, t), qrow)]
    out_specs = [pl.BlockSpec((None, nb, t, DA), whole), pl.BlockSpec((None, t, DA), kmap), pl.BlockSpec((None, t, DA), kmap)]
    out_shape = [jax.ShapeDtypeStruct((h, nb, t, DA), F32), jax.ShapeDtypeStruct((h, s, DA), F32), jax.ShapeDtypeStruct((h, s, DA), F32)]
    scratch = [pltpu.VMEM((nb, t, DA), F32), pltpu.VMEM((t, DA), F32), pltpu.VMEM((t, DA), F32)]
    if decay:
        out_specs += [pl.BlockSpec((None, nb, 1, t), whole), pl.BlockSpec((None, t, 1), kmap)]
        out_shape += [jax.ShapeDtypeStruct((h, nb, 1, t), F32), jax.ShapeDtypeStruct((h, s, 1), F32)]
        scratch += [pltpu.VMEM((nb, 1, t), F32), pltpu.VMEM((t, 1), F32)]
    outs = pl.pallas_call(
        body, name=name,
        grid_spec=pltpu.PrefetchScalarGridSpec(num_scalar_prefetch=2, grid=(h, nsteps), in_specs=in_specs, out_specs=out_specs,
                                               scratch_shapes=scratch),
        out_shape=out_shape, compiler_params=_cp(("parallel", "arbitrary")),
    )(qi, kj, qa, ka, va, doa, lse_row, delta_row)
    outs = list(outs)
    outs[0] = outs[0].reshape(h, s, DA)
    if decay:
        outs[3] = outs[3].reshape(h, 1, s)
    return outs


def _lane_scan(x, s, reverse):
    lane = lax.broadcasted_iota(jnp.int32, x.shape, 1)
    sh = 1
    while sh < s:
        if reverse:
            x = x + jnp.where(lane < s - sh, pltpu.roll(x, s - sh, axis=1), 0.0)
        else:
            x = x + jnp.where(lane >= sh, pltpu.roll(x, sh, axis=1), 0.0)
        sh *= 2
    return x


def _gate_fwd(z, col_block, bias, *, name):
    s = z.shape[0]

    def body(z_ref, b_ref, f_ref, c_ref):
        ft = z_ref[...].T[0:8, :]
        f_ref[...] = ft
        xg = ft + b_ref[...]
        lf = jnp.minimum(xg, 0.0) - jnp.log(1.0 + jnp.exp(-jnp.abs(xg)))
        c_ref[...] = _lane_scan(lf, s, False)

    return pl.pallas_call(
        body, name=name, grid=(1,),
        in_specs=[pl.BlockSpec((s, 128), lambda i: (0, col_block)), pl.BlockSpec((8, 1), lambda i: (0, 0))],
        out_specs=[pl.BlockSpec((8, s), lambda i: (0, 0)), pl.BlockSpec((8, s), lambda i: (0, 0))],
        out_shape=[jax.ShapeDtypeStruct((8, s), F32), jax.ShapeDtypeStruct((8, s), F32)],
        compiler_params=_cp(("arbitrary",)))(z, bias)


def _gate_bwd(ft, bias, dc, *, name):
    s = ft.shape[1]

    def body(f_ref, b_ref, dc_ref, df_ref, db_ref):
        xg = f_ref[...] + b_ref[...]
        dlf = _lane_scan(dc_ref[...], s, True)
        df = dlf * _sigmoid(-xg)
        df_ref[...] = df
        db_ref[...] = jnp.sum(df, axis=-1, keepdims=True)

    return pl.pallas_call(body, name=name, out_shape=[jax.ShapeDtypeStruct((8, s), F32), jax.ShapeDtypeStruct((8, 1), F32)],
                          compiler_params=_cp())(ft, bias, dc)


def _pool_lane_consts(tm, i):
    lane = lax.broadcasted_iota(jnp.int32, (tm, POOL_W), 1)
    tok = lax.broadcasted_iota(jnp.int32, (tm, POOL_W), 0) + i * tm
    win = jnp.where(lane < 64, 2, jnp.where(lane < 128, 4, jnp.where(lane < 192, 8, 16)))
    cnt = jnp.minimum(tok + 1, win).astype(F32)
    return lane, tok, cnt


def _pick_window(lane, s2, s4, s8, s16):
    return jnp.where(lane < 64, s2, jnp.where(lane < 128, s4, jnp.where(lane < 192, s8, s16)))


def _pool_fwd(z, col_block, bd, scale, *, name, tm=512):
    s = z.shape[0]
    tm = min(tm, s)
    hb = tm // POOL_HALO

    def body(u_ref, halo_ref, bd_ref, sc_ref, y_ref, p_ref, buf):
        i = pl.program_id(0)
        buf[0:POOL_HALO, :] = halo_ref[...] * (i > 0).astype(F32)
        buf[POOL_HALO:, :] = u_ref[...]

        def back(k):
            return buf[POOL_HALO - k:POOL_HALO - k + tm, :]

        u = u_ref[...]
        s2 = u + back(1)
        s4 = s2 + back(2) + back(3)
        s8 = s4 + back(4) + back(5) + back(6) + back(7)
        s16 = s8
        for k in range(8, 16):
            s16 = s16 + back(k)
        lane, _, cnt = _pool_lane_consts(tm, i)
        pooled = (_pick_window(lane, s2, s4, s8, s16) / cnt - u).astype(BF16)
        p_ref[...] = pooled
        y_ref[...] = _dot(pooled, bd_ref[...], NN) * sc_ref[...]

    return pl.pallas_call(
        body, name=name, grid=(s // tm,),
        in_specs=[pl.BlockSpec((tm, POOL_W), lambda i: (i, col_block)),
                  pl.BlockSpec((POOL_HALO, POOL_W), lambda i: (jnp.maximum(i * hb - 1, 0), col_block)),
                  pl.BlockSpec((POOL_W, POOL_W), lambda i: (0, 0)), pl.BlockSpec((1, POOL_W), lambda i: (0, 0))],
        out_specs=[pl.BlockSpec((tm, POOL_W), lambda i: (i, 0)), pl.BlockSpec((tm, POOL_W), lambda i: (i, 0))],
        out_shape=[jax.ShapeDtypeStruct((s, POOL_W), F32), jax.ShapeDtypeStruct((s, POOL_W), BF16)],
        scratch_shapes=[pltpu.VMEM((tm + POOL_HALO, POOL_W), F32)],
        compiler_params=_cp(("parallel",)),
    )(z, z, bd, scale.reshape(1, POOL_W))


def _pool_bwd_a(dy, pooled, bd, scale, *, name, tm=512):
    s = dy.shape[0]
    tm = min(tm, s)

    def body(dy_ref, p_ref, bd_ref, sc_ref, dq_ref, dys_ref, dsc_ref):
        i = pl.program_id(0)
        dyv = dy_ref[...]
        y0 = _dot(p_ref[...], bd_ref[...], NN)
        dys = (dyv * sc_ref[...]).astype(BF16)
        dys_ref[...] = dys
        dp = _dot(dys, bd_ref[...], NT)
        _, _, cnt = _pool_lane_consts(tm, i)
        dq_ref[:, 0:POOL_W] = dp / cnt
        dq_ref[:, POOL_W:] = dp

        @pl.when(i == 0)
        def _():
            dsc_ref[...] = jnp.zeros_like(dsc_ref)

        dsc_ref[...] += jnp.sum(dyv * y0, axis=0, keepdims=True)

    row = lambda i: (i, 0)
    dq, dys, dsc = pl.pallas_call(
        body, name=name, grid=(s // tm,),
        in_specs=[pl.BlockSpec((tm, POOL_W), row), pl.BlockSpec((tm, POOL_W), row),
                  pl.BlockSpec((POOL_W, POOL_W), lambda i: (0, 0)), pl.BlockSpec((1, POOL_W), lambda i: (0, 0))],
        out_specs=[pl.BlockSpec((tm, 2 * POOL_W), row), pl.BlockSpec((tm, POOL_W), row), pl.BlockSpec((1, POOL_W), lambda i: (0, 0))],
        out_shape=[jax.ShapeDtypeStruct((s, 2 * POOL_W), F32), jax.ShapeDtypeStruct((s, POOL_W), BF16),
                   jax.ShapeDtypeStruct((1, POOL_W), F32)],
        compiler_params=_cp(("arbitrary",)),
    )(dy, pooled, bd, scale.reshape(1, POOL_W))
    return dq, dys, dsc.reshape(POOL_W)


def _pool_bwd_b(dq, *, name, tm=512):
    s = dq.shape[0]
    tm = min(tm, s)
    hb = tm // POOL_HALO
    nblk = s // tm

    def body(q_ref, dp_ref, halo_ref, du_ref, buf):
        i = pl.program_id(0)
        buf[0:tm, :] = q_ref[...]
        buf[tm:, :] = halo_ref[...] * (i < nblk - 1).astype(F32)

        def ahead(k):
            return buf[k:k + tm, :]

        q = q_ref[...]
        s2 = q + ahead(1)
        s4 = s2 + ahead(2) + ahead(3)
        s8 = s4 + ahead(4) + ahead(5) + ahead(6) + ahead(7)
        s16 = s8
        for k in range(8, 16):
            s16 = s16 + ahead(k)
        lane = lax.broadcasted_iota(jnp.int32, (tm, POOL_W), 1)
        du_ref[...] = _pick_window(lane, s2, s4, s8, s16) - dp_ref[...]

    return pl.pallas_call(
        body, name=name, grid=(nblk,),
        in_specs=[pl.BlockSpec((tm, POOL_W), lambda i: (i, 0)), pl.BlockSpec((tm, POOL_W), lambda i: (i, 1)),
                  pl.BlockSpec((POOL_HALO, POOL_W), lambda i: (jnp.minimum((i + 1) * hb, nblk * hb - 1), 0))],
        out_specs=pl.BlockSpec((tm, POOL_W), lambda i: (i, 0)),
        out_shape=jax.ShapeDtypeStruct((s, POOL_W), F32),
        scratch_shapes=[pltpu.VMEM((tm + POOL_HALO, POOL_W), F32)],
        compiler_params=_cp(("parallel",)),
    )(dq, dq, dq)


def _loss_head(x, gain, target, *, name, tm=512):
    s = x.shape[0]
    tm = min(tm, s)

    def body(x_ref, g_ref, t_ref, dx_ref, dg_ref, loss_ref):
        xv = x_ref[...]
        r = lax.rsqrt(jnp.mean(xv * xv, axis=-1, keepdims=True) + EPS)
        xh = xv * r
        err = xh * g_ref[...] - t_ref[...]
        dy = err * (1.0 / D)
        a = dy * g_ref[...]
        dx_ref[...] = r * a - xh * (r * jnp.mean(a * xh, axis=-1, keepdims=True))

        @pl.when(pl.program_id(0) == 0)
        def _():
            dg_ref[...] = jnp.zeros_like(dg_ref)
            loss_ref[...] = jnp.zeros_like(loss_ref)

        dg_ref[...] += jnp.sum(dy * xh, axis=0, keepdims=True)
        part = 0.5 * jnp.sum(jnp.mean(err * err, axis=-1, keepdims=True), axis=0, keepdims=True)
        loss_ref[...] += jnp.broadcast_to(part, loss_ref.shape)

    row = lambda i: (i, 0)
    dx, dg, loss = pl.pallas_call(
        body, name=name, grid=(s // tm,),
        in_specs=[pl.BlockSpec((tm, D), row), pl.BlockSpec((1, D), lambda i: (0, 0)), pl.BlockSpec((tm, D), row)],
        out_specs=[pl.BlockSpec((tm, D), row), pl.BlockSpec((1, D), lambda i: (0, 0)), pl.BlockSpec((1, 128), lambda i: (0, 0))],
        out_shape=[jax.ShapeDtypeStruct((s, D), F32), jax.ShapeDtypeStruct((1, D), F32), jax.ShapeDtypeStruct((1, 128), F32)],
        compiler_params=_cp(("arbitrary",)),
    )(x, gain.reshape(1, D), target)
    return dx, dg.reshape(D), loss[0, 0]


def _adamw(w, g, m, v, *, name, tr=512):
    rows, cols = w.shape
    tr = min(tr, rows)
    assert rows % tr == 0, (name, rows, tr)
    c_m = 1.0 - ADAM_B1
    c_v = 1.0 - ADAM_B2
    bc1 = 1.0 - ADAM_B1 ** ADAM_STEP
    bc2 = 1.0 - ADAM_B2 ** ADAM_STEP

    def body(w_ref, g_ref, m_ref, v_ref, d_ref, mo_ref, vo_ref):
        gv = g_ref[...]
        mn = ADAM_B1 * m_ref[...] + c_m * gv
        vn = ADAM_B2 * v_ref[...] + c_v * (gv * gv)
        mo_ref[...] = mn
        vo_ref[...] = vn
        d_ref[...] = -ADAM_LR * ((mn / bc1) / (jnp.sqrt(vn / bc2) + ADAM_EPS) + ADAM_WD * w_ref[...])

    spec = pl.BlockSpec((tr, cols), lambda i: (i, 0))
    return pl.pallas_call(body, name=name, grid=(rows // tr,), in_specs=[spec] * 4, out_specs=[spec] * 3,
                          out_shape=[jax.ShapeDtypeStruct((rows, cols), F32)] * 3,
                          compiler_params=_cp(("parallel",)))(w, g, m, v)


def _row_tile(rows, cap):
    for t in range(min(cap, rows), 0, -1):
        if rows % t == 0 and t % 16 == 0:
            return t
    return rows


def _position():
    return jnp.stack([lax.axis_index("c"), 2 * lax.axis_index("x") + lax.axis_index("y")]).astype(jnp.int32)


def _sum2_bf16(pos, full, sib, *, name, tr=256):
    n, half, cols = sib.shape
    tr = _row_tile(half, tr)
    nb = half // tr

    def body(pos_ref, a_ref, b_ref, o_ref):
        o_ref[...] = (a_ref[...] + b_ref[...]).astype(BF16)

    spec = pl.BlockSpec((None, tr, cols), lambda j, i, p: (j, i, 0))
    return pl.pallas_call(
        body, name=name,
        grid_spec=pltpu.PrefetchScalarGridSpec(
            num_scalar_prefetch=1, grid=(n, nb),
            in_specs=[pl.BlockSpec((None, tr, cols), lambda j, i, p: (j, p[0] * nb + i, 0)), spec], out_specs=spec),
        out_shape=jax.ShapeDtypeStruct(sib.shape, BF16), compiler_params=_cp(("parallel", "parallel")))(pos, full, sib)


def _sum5(pos, full, sib, recv, *, name, tr=256):
    _, rows, cols = full.shape
    half = rows // 2
    tr = _row_tile(half, tr)
    nb = half // tr

    def body(pos_ref, a_ref, b_ref, r_ref, o_ref):
        acc = a_ref[...] + b_ref[...]
        for kk in range(3):
            acc = acc + r_ref[kk].astype(F32)
        o_ref[...] = acc

    return pl.pallas_call(
        body, name=name,
        grid_spec=pltpu.PrefetchScalarGridSpec(
            num_scalar_prefetch=1, grid=(nb,),
            in_specs=[pl.BlockSpec((None, tr, cols), lambda i, p: (p[1], p[0] * nb + i, 0)),
                      pl.BlockSpec((None, tr, cols), lambda i, p: (p[1], i, 0)),
                      pl.BlockSpec((3, tr, cols), lambda i, p: (0, i, 0))],
            out_specs=pl.BlockSpec((tr, cols), lambda i, p: (p[0] * nb + i, 0))),
        out_shape=jax.ShapeDtypeStruct((rows, cols), F32), compiler_params=_cp(("parallel",)))(pos, full, sib, recv)


def _place():
    x, y, c = lax.axis_index("x"), lax.axis_index("y"), lax.axis_index("c")
    chips = [(1 - x, y), (x, 1 - y), (1 - x, 1 - y)]
    return x, y, c, 2 * x + y, chips


def _gather_weights(shards):
    n = len(shards)

    def body(*refs):
        ins, outs = refs[:n], refs[n:2 * n]
        send_i, recv_i, send_d, recv_d, send_o, recv_o = refs[2 * n:]
        x, y, c, me, chips = _place()
        local = [pltpu.make_async_remote_copy(src_ref=ins[t], dst_ref=outs[t].at[me], send_sem=send_o.at[t], recv_sem=recv_o.at[t],
                                              device_id=(x, y, 1 - c), device_id_type=MESH) for t in range(n)]
        for cp in local:
            cp.start()

        def ici(t, kk, src_chip, to):
            return pltpu.make_async_remote_copy(
                src_ref=ins[t].at[c], dst_ref=outs[t].at[src_chip, c], send_sem=send_i.at[t * 3 + kk],
                recv_sem=recv_i.at[t * 3 + kk], device_id=to, device_id_type=MESH)

        def d2d(t, kk, src_chip, layer):
            return pltpu.make_async_remote_copy(
                src_ref=outs[t].at[src_chip, layer], dst_ref=outs[t].at[src_chip, layer], send_sem=send_d.at[t * 3 + kk],
                recv_sem=recv_d.at[t * 3 + kk], device_id=(x, y, 1 - c), device_id_type=MESH)

        sends = [ici(t, kk, me, (px, py, c)) for t in range(n) for kk, (px, py) in enumerate(chips)]
        for cp in sends:
            cp.start()
        fwd = []
        for t in range(n):
            for kk, (px, py) in enumerate(chips):
                ici(t, kk, 2 * px + py, (px, py, c)).wait_recv()
                f = d2d(t, kk, 2 * px + py, c)
                f.start()
                fwd.append(f)
        for t in range(n):
            for kk, (px, py) in enumerate(chips):
                d2d(t, kk, 2 * px + py, 1 - c).wait_recv()
        for cp in sends + fwd:
            cp.wait_send()
        for cp in local:
            cp.wait()

    return pl.pallas_call(
        body, name="gather_weights", in_specs=[HBM_SPEC] * n, out_specs=[HBM_SPEC] * n,
        out_shape=[jax.ShapeDtypeStruct((N_CHIPS,) + s.shape, s.dtype) for s in shards],
        scratch_shapes=[pltpu.SemaphoreType.DMA((3 * n,)), pltpu.SemaphoreType.DMA((3 * n,)),
                        pltpu.SemaphoreType.DMA((3 * n,)), pltpu.SemaphoreType.DMA((3 * n,)),
                        pltpu.SemaphoreType.DMA((n,)), pltpu.SemaphoreType.DMA((n,))],
    )(*shards)


def _reduce_stage1(grads):
    n = len(grads)

    def body(*refs):
        ins, sib = refs[:n], refs[n:2 * n]
        send, recv = refs[2 * n:]
        x, y, c, me, chips = _place()
        cps = []
        for t in range(n):
            rows = ins[t].shape[1] // 2
            cp = pltpu.make_async_remote_copy(
                src_ref=ins[t].at[:, pl.ds((1 - c) * rows, rows), :], dst_ref=sib[t], send_sem=send.at[t],
                recv_sem=recv.at[t], device_id=(x, y, 1 - c), device_id_type=MESH)
            cp.start()
            cps.append(cp)
        for cp in cps:
            cp.wait()

    return pl.pallas_call(
        body, name="reduce_stage1", in_specs=[HBM_SPEC] * n, out_specs=[HBM_SPEC] * n,
        out_shape=[jax.ShapeDtypeStruct((N_CHIPS, g.shape[1] // 2, g.shape[2]), F32) for g in grads],
        scratch_shapes=[pltpu.SemaphoreType.DMA((n,)), pltpu.SemaphoreType.DMA((n,))],
    )(*grads)


def _reduce_stage2(psum_bf16):
    n = len(psum_bf16)

    def body(*refs):
        ps, rcv = refs[:n], refs[n:2 * n]
        send, recv = refs[2 * n:]
        x, y, c, me, chips = _place()
        cps = []
        for t in range(n):
            for kk, (px, py) in enumerate(chips):
                cp = pltpu.make_async_remote_copy(
                    src_ref=ps[t].at[2 * px + py], dst_ref=rcv[t].at[kk], send_sem=send.at[t * 3 + kk],
                    recv_sem=recv.at[t * 3 + kk], device_id=(px, py, c), device_id_type=MESH)
                cp.start()
                cps.append(cp)
        for cp in cps:
            cp.wait()

    return pl.pallas_call(
        body, name="reduce_stage2", in_specs=[HBM_SPEC] * n, out_specs=[HBM_SPEC] * n,
        out_shape=[jax.ShapeDtypeStruct((3,) + p.shape[1:], p.dtype) for p in psum_bf16],
        scratch_shapes=[pltpu.SemaphoreType.DMA((3 * n,)), pltpu.SemaphoreType.DMA((3 * n,))],
    )(*psum_bf16)


def _reduce_stage3(reduced):
    n = len(reduced)

    def body(*refs):
        outs = refs[n:2 * n]
        send, recv = refs[2 * n:]
        x, y, c, me, chips = _place()
        cps = []
        for t in range(n):
            rows = outs[t].shape[0] // 2
            mine = outs[t].at[pl.ds(c * rows, rows), :]
            cp = pltpu.make_async_remote_copy(src_ref=mine, dst_ref=mine, send_sem=send.at[t], recv_sem=recv.at[t],
                                              device_id=(x, y, 1 - c), device_id_type=MESH)
            cp.start()
            cps.append(cp)
        for cp in cps:
            cp.wait()

    return pl.pallas_call(
        body, name="reduce_stage3", in_specs=[HBM_SPEC] * n, out_specs=[HBM_SPEC] * n,
        out_shape=[jax.ShapeDtypeStruct(r.shape, r.dtype) for r in reduced],
        input_output_aliases={t: t for t in range(n)},
        scratch_shapes=[pltpu.SemaphoreType.DMA((n,)), pltpu.SemaphoreType.DMA((n,))],
    )(*reduced)


def _allreduce_small(v):
    rows, cols = v.shape

    def body(v_ref, o_ref, buf, send, recv, loc):
        x, y, c, me, chips = _place()
        mine = 4 * x + 2 * y + c
        lc = pltpu.make_async_copy(v_ref, buf.at[mine], loc)
        lc.start()
        peers = []
        for fx in range(2):
            for fy in range(2):
                for fc in range(2):
                    if fx or fy or fc:
                        peers.append((fx, fy, fc))
        cps = []
        for kk, (fx, fy, fc) in enumerate(peers):
            to = (x ^ fx, y ^ fy, c ^ fc)
            cp = pltpu.make_async_remote_copy(src_ref=v_ref, dst_ref=buf.at[mine], send_sem=send.at[kk], recv_sem=recv.at[kk],
                                              device_id=to, device_id_type=MESH)
            cp.start()
            cps.append((cp, to))
        for kk, (cp, to) in enumerate(cps):
            src = 4 * to[0] + 2 * to[1] + to[2]
            pltpu.make_async_remote_copy(src_ref=v_ref, dst_ref=buf.at[src], send_sem=send.at[kk], recv_sem=recv.at[kk],
                                         device_id=to, device_id_type=MESH).wait_recv()
        for cp, _ in cps:
            cp.wait_send()
        lc.wait()
        acc = buf[0]
        for d in range(1, 8):
            acc = acc + buf[d]
        o_ref[...] = acc

    return pl.pallas_call(
        body, name="allreduce_small", in_specs=[pl.BlockSpec(memory_space=pltpu.VMEM)],
        out_specs=pl.BlockSpec(memory_space=pltpu.VMEM), out_shape=jax.ShapeDtypeStruct((rows, cols), F32),
        scratch_shapes=[pltpu.VMEM((8, rows, cols), F32), pltpu.SemaphoreType.DMA((7,)), pltpu.SemaphoreType.DMA((7,)),
                        pltpu.SemaphoreType.DMA],
        compiler_params=pltpu.CompilerParams(vmem_limit_bytes=VMEM_LIMIT_V7X),
    )(v)


def _q_perm():
    idx = [hh * (NOPE + ROPE) + d for hh in range(H) for d in range(NOPE)]
    idx += [hh * (NOPE + ROPE) + NOPE + e for hh in range(H) for e in range(HALF_ROPE)]
    idx += [hh * (NOPE + ROPE) + NOPE + HALF_ROPE + e for hh in range(H) for e in range(HALF_ROPE)]
    return np.array(idx, np.int32)


def _kv_perm():
    idx = [hh * (NOPE + VDIM) + d for hh in range(H) for d in range(NOPE)]
    idx += [hh * (NOPE + VDIM) + NOPE + d for hh in range(H) for d in range(VDIM)]
    return np.array(idx, np.int32)


def _inverse(perm):
    inv = np.empty_like(perm)
    inv[perm] = np.arange(perm.size, dtype=perm.dtype)
    return inv


def _pad_w_in(w):
    z = lambda n: jnp.zeros(w.shape[:-1] + (n,), w.dtype)
    return jnp.concatenate([w[..., 0:416], z(96), w[..., 416:1824], w[..., 1824:1830], z(122)], axis=-1)


def _unpad_w_in(w):
    return jnp.concatenate([w[..., 0:416], w[..., 512:1920], w[..., 1920:1926]], axis=-1)


def _block_diag(pw):
    out = jnp.zeros((POOL_W, POOL_W), pw.dtype)
    for g in range(4):
        out = out.at[g * 64:(g + 1) * 64, g * 64:(g + 1) * 64].set(pw[g])
    return out


def _heads(a, width):
    return a.reshape(a.shape[0], H, width).transpose(1, 0, 2)


def _unheads(a):
    return a.transpose(1, 0, 2).reshape(a.shape[1], -1)


def _rope_tables(s):
    inv_freq = ROPE_THETA ** (-jnp.arange(0, ROPE, 2, dtype=F32) / ROPE)
    ang = jnp.arange(s, dtype=jnp.int32).astype(F32)[:, None] * inv_freq[None, :]
    cos, sin = jnp.cos(ang), jnp.sin(ang)
    return jnp.tile(cos, (1, H + 1)), jnp.tile(sin, (1, H + 1))


def _mix_fwd(l, x1, wts, sm, cos, sin):
    s = x1.shape[0]
    z, h2 = _norm_mm(x1, 0, sm["mix_norm"][l], wts["w_in"][l], name=f"mix_in_{l}")
    q, qn = _norm_mm(z, Z_QA // Q_RANK, sm["q_a_norm"][l], wts["w_q_b"][l], name=f"mla_q_{l}")
    kv, kvn = _norm_mm(z, Z_KVA // KV_RANK, sm["kv_a_norm"][l], wts["w_kv_b"][l], name=f"mla_kv_{l}")
    nq = H * NOPE
    pe1 = jnp.concatenate([q[:, nq:nq + H * HALF_ROPE], z[:, Z_KR:Z_KR + HALF_ROPE]], axis=1)
    pe2 = jnp.concatenate([q[:, nq + H * HALF_ROPE:], z[:, Z_KR + HALF_ROPE:Z_KR + ROPE]], axis=1)
    r1, r2 = _rope(pe1, pe2, cos, sin, name=f"rope_{l}")
    hq = H * HALF_ROPE
    zpad = lambda n: jnp.zeros((s, H, n), F32)
    ones = lambda n: jnp.ones((s, H, n), F32)
    qh = jnp.concatenate([q[:, :nq].reshape(s, H, NOPE), r1[:, :hq].reshape(s, H, HALF_ROPE),
                          r2[:, :hq].reshape(s, H, HALF_ROPE)], axis=-1) * SCALE_MLA
    qa = jnp.concatenate([qh, zpad(DA - NOPE - ROPE)], axis=-1).transpose(1, 0, 2).astype(BF16)
    kpe = jnp.concatenate([r1[:, hq:], r2[:, hq:]], axis=1)
    ka = jnp.concatenate([kv[:, :nq].reshape(s, H, NOPE), jnp.broadcast_to(kpe[:, None, :], (s, H, ROPE)),
                          zpad(DA - NOPE - ROPE)], axis=-1).transpose(1, 0, 2).astype(BF16)
    va = jnp.concatenate([kv[:, nq:].reshape(s, H, VDIM), ones(1), zpad(DA - VDIM - 1)], axis=-1).transpose(1, 0, 2).astype(BF16)
    oa, lse_a = _attn_fwd(qa, ka, va, VDIM, name=f"mla_attn_{l}")

    bd = _block_diag(wts["pool_w"][l]).astype(BF16)
    yb, pooled = _pool_fwd(z, Z_POOL // POOL_W, bd, sm["pool_scale"][l], name=f"pool_{l}")

    fb = jnp.pad(sm["fox_b_f"][l], (0, 8 - H)).reshape(8, 1)
    ft, cum = _gate_fwd(z, Z_F // 128, fb, name=f"fox_gate_{l}")
    c = cum[:H].T
    c_hi = lax.reduce_precision(c, 8, 7)
    r = c - c_hi
    c_mid = lax.reduce_precision(r, 8, 7)
    c_lo = lax.reduce_precision(r - c_mid, 8, 7)
    c3 = jnp.stack([c_hi, c_mid, c_lo], axis=-1)
    fqkv = z[:, Z_FOX:Z_FOX + 3 * H * FOX_D].reshape(s, 3, H, FOX_D)
    fqa = jnp.concatenate([fqkv[:, 0] * SCALE_FOX, c3, ones(3), zpad(DA - FOX_D - 6)], axis=-1).transpose(1, 0, 2).astype(BF16)
    fka = jnp.concatenate([fqkv[:, 1], ones(3), -c3, zpad(DA - FOX_D - 6)], axis=-1).transpose(1, 0, 2).astype(BF16)
    fva = jnp.concatenate([fqkv[:, 2], ones(1), zpad(DA - FOX_D - 1)], axis=-1).transpose(1, 0, 2).astype(BF16)
    oc, lse_c = _attn_fwd(fqa, fka, fva, FOX_D, name=f"fox_attn_{l}")

    cat = jnp.concatenate([_unheads(oa), yb, _unheads(oc)], axis=1).astype(BF16)
    x2 = _mm(cat, wts["w_out"][l], "nn", name=f"mix_out_{l}", add=x1, tn=1024, tk=1024)
    saved = dict(z=z, h2=h2, qn=qn, kvn=kvn, qa=qa, ka=ka, va=va, oa=oa, lse_a=lse_a, bd=bd, pooled=pooled,
                 fqa=fqa, fka=fka, fva=fva, ft=ft, fb=fb, oc=oc, lse_c=lse_c, cat=cat)
    return x2, saved


def _mix_bwd(l, x1, dx2, sv, wts, sm, cos, sin):
    s = x1.shape[0]
    g = {}
    dx2b = dx2.astype(BF16)
    g["w_out"] = _mm(sv["cat"], dx2b, "tn", name=f"d_w_out_{l}", tm=1024, tn=1024)
    dcat = _mm(dx2b, wts["w_out"][l], "nt", name=f"d_cat_{l}", tn=1024, tk=1024)
    nv = H * VDIM
    doa = _heads(dcat[:, :nv], VDIM)
    dyb = dcat[:, nv:nv + POOL_W]
    doc = _heads(dcat[:, nv + POOL_W:], FOX_D)
    pad_do = lambda d: jnp.pad(d, ((0, 0), (0, 0), (0, DA - d.shape[-1]))).astype(BF16)

    dl_c = _attn_delta(sv["oc"], doc, name=f"fox_delta_{l}")
    dfqa, dfka, dfva, dcq, dck = _attn_bwd(sv["fqa"], sv["fka"], sv["fva"], pad_do(doc), sv["lse_c"].reshape(H, 1, s),
                                           dl_c.reshape(H, 1, s), True, name=f"fox_attn_bwd_{l}")
    dfq = dfqa[:, :, :FOX_D] * SCALE_FOX
    dfk = dfka[:, :, :FOX_D]
    dfv = dfva[:, :, :FOX_D]
    dc = jnp.pad(dcq.reshape(H, s) + dck.reshape(H, s), ((0, 8 - H), (0, 0)))
    dft, dfb = _gate_bwd(sv["ft"], sv["fb"], dc, name=f"fox_gate_bwd_{l}")
    g["fox_b_f"] = dfb[:H, 0]

    dq, dys, g["pool_scale"] = _pool_bwd_a(dyb, sv["pooled"], sv["bd"], sm["pool_scale"][l], name=f"pool_bwd_a_{l}")
    du = _pool_bwd_b(dq, name=f"pool_bwd_b_{l}")
    dbd = _mm(sv["pooled"], dys, "tn", name=f"d_pool_w_{l}")
    g["pool_w"] = jnp.stack([dbd[i * 64:(i + 1) * 64, i * 64:(i + 1) * 64] for i in range(4)])

    dl_a = _attn_delta(sv["oa"], doa, name=f"mla_delta_{l}")
    dqa_, dka_, dva_ = _attn_bwd(sv["qa"], sv["ka"], sv["va"], pad_do(doa), sv["lse_a"].reshape(H, 1, s),
                                 dl_a.reshape(H, 1, s), False, name=f"mla_attn_bwd_{l}")
    dqs = dqa_[:, :, :NOPE + ROPE].transpose(1, 0, 2) * SCALE_MLA
    dks = dka_[:, :, :NOPE + ROPE].transpose(1, 0, 2)
    dvh = dva_[:, :, :VDIM]
    dr1 = jnp.concatenate([dqs[:, :, NOPE:NOPE + HALF_ROPE].reshape(s, -1), jnp.sum(dks[:, :, NOPE:NOPE + HALF_ROPE], axis=1)], axis=1)
    dr2 = jnp.concatenate([dqs[:, :, NOPE + HALF_ROPE:].reshape(s, -1), jnp.sum(dks[:, :, NOPE + HALF_ROPE:], axis=1)], axis=1)
    dpe1, dpe2 = _rope(dr1, dr2, cos, -sin, name=f"rope_bwd_{l}")
    hq = H * HALF_ROPE
    dq_full = jnp.concatenate([dqs[:, :, :NOPE].reshape(s, -1), dpe1[:, :hq], dpe2[:, :hq]], axis=1).astype(BF16)
    dkv_full = jnp.concatenate([dks[:, :, :NOPE].reshape(s, -1), _unheads(dvh)], axis=1).astype(BF16)
    g["w_q_b"] = _mm(sv["qn"], dq_full, "tn", name=f"d_w_q_b_{l}", tn=576)
    g["w_kv_b"] = _mm(sv["kvn"], dkv_full, "tn", name=f"d_w_kv_b_{l}", tn=768)
    dqn = _mm(dq_full, wts["w_q_b"][l], "nt", name=f"d_qn_{l}", tk=576)
    dkvn = _mm(dkv_full, wts["w_kv_b"][l], "nt", name=f"d_kvn_{l}", tk=768)
    dqa, g["q_a_norm"] = _rmsnorm_bwd(sv["z"], Z_QA // Q_RANK, sm["q_a_norm"][l], dqn, name=f"q_a_norm_bwd_{l}")
    dkva, g["kv_a_norm"] = _rmsnorm_bwd(sv["z"], Z_KVA // KV_RANK, sm["kv_a_norm"][l], dkvn, name=f"kv_a_norm_bwd_{l}")

    zeros = lambda n: jnp.zeros((s, n), BF16)
    dz = jnp.concatenate([
        dqa.astype(BF16), dkva.astype(BF16), dpe1[:, hq:].astype(BF16), dpe2[:, hq:].astype(BF16), zeros(128 - ROPE),
        du.astype(BF16), _unheads(dfq).astype(BF16), _unheads(dfk).astype(BF16), _unheads(dfv).astype(BF16),
        dft[:H].T.astype(BF16), zeros(128 - H)], axis=1)
    g["w_in"] = _mm(sv["h2"], dz, "tn", name=f"d_w_in_{l}", tm=1024, tn=1024)
    dh2 = _mm(dz, wts["w_in"][l], "nt", name=f"d_h2_{l}", tn=1024, tk=1024)
    dx1, g["mix_norm"] = _rmsnorm_bwd(x1, 0, sm["mix_norm"][l], dh2, dx2, name=f"mix_norm_bwd_{l}")
    return dx1, g


def _local_step(x, target, wts, sm):
    s = x.shape[0]
    cos, sin = _rope_tables(s)
    acts = []
    xs = x
    for l in range(DEPTH):
        x1, gu1 = _ffn_fwd(xs, sm["ffn1_norm"][l], wts["ffn1_w_gu"], wts["ffn1_w_d2"][l], l, name=f"ffn1_fwd_{l}")
        x2, sv = _mix_fwd(l, x1, wts, sm, cos, sin)
        x3, gu2 = _ffn_fwd(x2, sm["ffn2_norm"][l], wts["ffn2_w_gu"], wts["ffn2_w_d2"][l], l, name=f"ffn2_fwd_{l}")
        acts.append((xs, gu1, x1, sv, x2, gu2))
        xs = x3
    dx, g_final, loss = _loss_head(xs, sm["final_norm"], target, name="loss_head")
    grads = [dict() for _ in range(DEPTH)]
    for l in reversed(range(DEPTH)):
        x0, gu1, x1, sv, x2, gu2 = acts[l]
        g = grads[l]
        dx, dgu, act, hh, dy, g["ffn2_norm"] = _ffn_bwd(x2, dx, gu2, sm["ffn2_norm"][l], wts["ffn2_w_gu"], wts["ffn2_w_d2"][l], l,
                                                        name=f"ffn2_bwd_{l}")
        g["ffn2_w_down"] = _mm(act, dy, "tn", name=f"d_ffn2_w_down_{l}", tm=FF_SHARD, tn=1024)
        g["ffn2_w_gu"] = _mm(hh, dgu, "tn", name=f"d_ffn2_w_gu_{l}", tm=1024, tn=FF_SHARD, n_major_out=True)
        dx, gm = _mix_bwd(l, x1, dx, sv, wts, sm, cos, sin)
        g.update(gm)
        dx, dgu, act, hh, dy, g["ffn1_norm"] = _ffn_bwd(x0, dx, gu1, sm["ffn1_norm"][l], wts["ffn1_w_gu"], wts["ffn1_w_d2"][l], l,
                                                        name=f"ffn1_bwd_{l}")
        g["ffn1_w_down"] = _mm(act, dy, "tn", name=f"d_ffn1_w_down_{l}", tm=FF_SHARD, tn=1024)
        g["ffn1_w_gu"] = _mm(hh, dgu, "tn", name=f"d_ffn1_w_gu_{l}", tm=1024, tn=FF_SHARD, n_major_out=True)
    return loss, dx, grads, g_final


BIG = ["ffn1_w_gu", "ffn1_w_down", "w_in", "w_q_b", "w_kv_b", "w_out", "ffn2_w_gu", "ffn2_w_down"]
SMALL = ["ffn1_norm", "mix_norm", "q_a_norm", "kv_a_norm", "pool_w", "pool_scale", "fox_b_f", "ffn2_norm"]
SMALL_ROWS = 48


def _prepare_weights(gathered, params):
    qp, kp = _q_perm(), _kv_perm()
    wts = {"ffn1_w_gu": gathered["ffn1_w_gu"], "ffn2_w_gu": gathered["ffn2_w_gu"]}
    for nm in ("ffn1", "ffn2"):
        wd = gathered[nm + "_w_down"]
        wts[nm + "_w_d2"] = [wd[:, l].reshape(2, FF_SHARD, D) for l in range(DEPTH)]
    wts["w_in"] = [gathered["w_in"][:, l].reshape(D, NZ) for l in range(DEPTH)]
    wts["w_out"] = [gathered["w_out"][:, l].reshape(D, D) for l in range(DEPTH)]
    wts["w_q_b"] = [jnp.take(jnp.moveaxis(gathered["w_q_b"][:, l], 0, 1).reshape(Q_RANK, -1), qp, axis=1) for l in range(DEPTH)]
    wts["w_kv_b"] = [jnp.take(jnp.moveaxis(gathered["w_kv_b"][:, l], 0, 1).reshape(KV_RANK, -1), kp, axis=1) for l in range(DEPTH)]
    wts["pool_w"] = params["pool_w"]
    return wts


def _chip_major(name, g):
    if name in ("ffn1_w_gu", "ffn2_w_gu"):
        return g
    if name in ("ffn1_w_down", "ffn2_w_down", "w_in", "w_out"):
        return g.reshape(N_CHIPS, g.shape[0] // N_CHIPS, g.shape[1])
    perm = _q_perm() if name == "w_q_b" else _kv_perm()
    g = jnp.take(g, _inverse(perm), axis=1)
    return jnp.moveaxis(g.reshape(g.shape[0], N_CHIPS, g.shape[1] // N_CHIPS), 1, 0)


def _pack_small(grads, g_final, loss):
    parts = []
    for l in range(DEPTH):
        for nm in SMALL:
            parts.append(grads[l][nm].reshape(-1))
    parts.append(g_final.reshape(-1))
    parts.append(loss.reshape(1))
    flat = jnp.concatenate(parts)
    return jnp.pad(flat, (0, SMALL_ROWS * D - flat.shape[0])).reshape(SMALL_ROWS, D)


def _unpack_small(packed, params):
    flat = packed.reshape(-1)
    out = {nm: [] for nm in SMALL}
    off = 0
    for l in range(DEPTH):
        for nm in SMALL:
            shp = params[nm].shape[1:]
            n = int(np.prod(shp))
            out[nm].append(flat[off:off + n].reshape(shp))
            off += n
    res = {nm: jnp.stack(v) for nm, v in out.items()}
    res["final_norm"] = flat[off:off + D]
    return res, flat[off + D]


def _update(name, w, g, m, v):
    shp = w.shape
    if w.ndim == 1:
        view = (1, shp[0])
    elif w.size <= 65536:
        view = (shp[0], w.size // shp[0])
    else:
        view = (w.size // shp[-1], shp[-1])
    tr = view[0]
    for cand in (512, 352, 256, 128):
        if view[0] % cand == 0 and view[0] > cand:
            tr = cand
            break
    d, mn, vn = _adamw(w.reshape(view), g.reshape(view), m.reshape(view), v.reshape(view), name="adamw_" + name, tr=tr)
    return d.reshape(shp), mn.reshape(shp), vn.reshape(shp)


WEIGHTS = ['ffn1_norm', 'ffn1_w_gu', 'ffn1_w_down', 'mix_norm', 'w_in', 'q_a_norm', 'w_q_b', 'kv_a_norm', 'w_kv_b', 'pool_w',
           'pool_scale', 'fox_b_f', 'w_out', 'ffn2_norm', 'ffn2_w_gu', 'ffn2_w_down', 'final_norm']


def kernel(x, ffn1_norm, ffn1_w_gu, ffn1_w_down, mix_norm, w_in, q_a_norm, w_q_b, kv_a_norm, w_kv_b, pool_w, pool_scale, fox_b_f, w_out, ffn2_norm, ffn2_w_gu, ffn2_w_down, final_norm, loss_target, m_ffn1_norm, m_ffn1_w_gu, m_ffn1_w_down, m_mix_norm, m_w_in, m_q_a_norm, m_w_q_b, m_kv_a_norm, m_w_kv_b, m_pool_w, m_pool_scale, m_fox_b_f, m_w_out, m_ffn2_norm, m_ffn2_w_gu, m_ffn2_w_down, m_final_norm, v_ffn1_norm, v_ffn1_w_gu, v_ffn1_w_down, v_mix_norm, v_w_in, v_q_a_norm, v_w_q_b, v_kv_a_norm, v_w_kv_b, v_pool_w, v_pool_scale, v_fox_b_f, v_w_out, v_ffn2_norm, v_ffn2_w_gu, v_ffn2_w_down, v_final_norm):
    params = dict(ffn1_norm=ffn1_norm, ffn1_w_gu=ffn1_w_gu, ffn1_w_down=ffn1_w_down, mix_norm=mix_norm, w_in=w_in, q_a_norm=q_a_norm,
                  w_q_b=w_q_b, kv_a_norm=kv_a_norm, w_kv_b=w_kv_b, pool_w=pool_w, pool_scale=pool_scale, fox_b_f=fox_b_f, w_out=w_out,
                  ffn2_norm=ffn2_norm, ffn2_w_gu=ffn2_w_gu, ffn2_w_down=ffn2_w_down, final_norm=final_norm)
    mom = dict(ffn1_norm=m_ffn1_norm, ffn1_w_gu=m_ffn1_w_gu, ffn1_w_down=m_ffn1_w_down, mix_norm=m_mix_norm, w_in=m_w_in,
               q_a_norm=m_q_a_norm, w_q_b=m_w_q_b, kv_a_norm=m_kv_a_norm, w_kv_b=m_w_kv_b, pool_w=m_pool_w, pool_scale=m_pool_scale,
               fox_b_f=m_fox_b_f, w_out=m_w_out, ffn2_norm=m_ffn2_norm, ffn2_w_gu=m_ffn2_w_gu, ffn2_w_down=m_ffn2_w_down,
               final_norm=m_final_norm)
    var = dict(ffn1_norm=v_ffn1_norm, ffn1_w_gu=v_ffn1_w_gu, ffn1_w_down=v_ffn1_w_down, mix_norm=v_mix_norm, w_in=v_w_in,
               q_a_norm=v_q_a_norm, w_q_b=v_w_q_b, kv_a_norm=v_kv_a_norm, w_kv_b=v_w_kv_b, pool_w=v_pool_w, pool_scale=v_pool_scale,
               fox_b_f=v_fox_b_f, w_out=v_w_out, ffn2_norm=v_ffn2_norm, ffn2_w_gu=v_ffn2_w_gu, ffn2_w_down=v_ffn2_w_down,
               final_norm=v_final_norm)

    shards = []
    for nm in BIG:
        w = params[nm]
        if nm == "w_in":
            w = _pad_w_in(w)
        shards.append(w.astype(BF16))
    gathered = dict(zip(BIG, _gather_weights(shards)))
    wts = _prepare_weights(gathered, params)

    loss, dx, grads, g_final = _local_step(x[0], loss_target[0], wts, params)

    full = [_chip_major(nm, grads[l][nm]) for nm in BIG for l in range(DEPTH)]
    pos = _position()
    sib = _reduce_stage1(full)
    psum = [_sum2_bf16(pos, f, sb, name=f"chip_sum_{t}") for t, (f, sb) in enumerate(zip(full, sib))]
    recv = _reduce_stage2(psum)
    reduced = [_sum5(pos, f, sb, r, name=f"grad_sum_{t}") for t, (f, sb, r) in enumerate(zip(full, sib, recv))]
    whole = _reduce_stage3(reduced)
    big_g = {nm: jnp.stack(whole[DEPTH * k:DEPTH * (k + 1)]) for k, nm in enumerate(BIG)}
    big_g["w_in"] = _unpad_w_in(big_g["w_in"])
    small_g, loss = _unpack_small(_allreduce_small(_pack_small(grads, g_final, loss)), params)
    gw = {**big_g, **small_g}

    delta, new_m, new_v = {}, {}, {}
    for nm in WEIGHTS:
        delta[nm], new_m[nm], new_v[nm] = _update(nm, params[nm], gw[nm], mom[nm], var[nm])
    return (loss, dx[None], *[gw[n] for n in WEIGHTS], *[delta[n] for n in WEIGHTS], *[new_m[n] for n in WEIGHTS],
            *[new_v[n] for n in WEIGHTS])
```

```python
import functools
import math

import jax
import jax.numpy as jnp
import numpy as np
from jax import lax
from jax.experimental import pallas as pl
from jax.experimental.pallas import tpu as pltpu

F32 = jnp.float32
BF16 = jnp.bfloat16
MESH = pl.DeviceIdType.MESH
HBM_SPEC = pl.BlockSpec(memory_space=pltpu.HBM)

D = 1024
DEPTH = 2
D_FF = 2816
FF_SHARD = 1408
N_CHIPS = 4
H = 6
NOPE, ROPE, VDIM = 64, 32, 64
HALF_ROPE = ROPE // 2
Q_RANK, KV_RANK = 256, 128
POOL_W = 256
FOX_D = 64
N_IN = 1830
NZ = 2048
ROPE_THETA = 10000.0
EPS = 1e-6
POOL_HALO = 16
Z_QA, Z_KVA, Z_KR, Z_POOL, Z_FOX, Z_F = 0, 256, 384, 512, 768, 1920

ADAM_LR, ADAM_B1, ADAM_B2, ADAM_EPS, ADAM_WD, ADAM_STEP = 0.001, 0.9, 0.999, 1e-08, 0.01, 10

VMEM_LIMIT_V7X = 56 * 1024 * 1024


def _cp(sem=None, vmem=VMEM_LIMIT_V7X):
    return pltpu.CompilerParams(dimension_semantics=sem, vmem_limit_bytes=vmem)


def _sigmoid(x):
    return 1.0 / (1.0 + jnp.exp(-x))


def _dot(a, b, dims):
    return lax.dot_general(a, b, (dims, ((), ())), preferred_element_type=F32)


NN = ((1,), (0,))
NT = ((1,), (1,))
TN = ((0,), (0,))


def _mm(a, b, mode, *, name, out_dtype=F32, add=None, alpha=None, tm=512, tn=512, tk=512, n_major_out=False):
    if mode == "nn":
        (m, k), (k2, n) = a.shape, b.shape
    elif mode == "nt":
        (m, k), (n, k2) = a.shape, b.shape
    else:
        (k, m), (k2, n) = a.shape, b.shape
    assert k == k2
    tm, tn, tk = min(tm, m), min(tn, n), min(tk, k)
    assert m % tm == 0 and n % tn == 0 and k % tk == 0, (name, m, n, k, tm, tn, tk)
    nk = k // tk
    dims = {"nn": NN, "nt": NT, "tn": TN}[mode]
    a_spec = pl.BlockSpec((tk, tm), lambda i, j, kk: (kk, i)) if mode == "tn" else pl.BlockSpec((tm, tk), lambda i, j, kk: (i, kk))
    b_spec = pl.BlockSpec((tn, tk), lambda i, j, kk: (j, kk)) if mode == "nt" else pl.BlockSpec((tk, tn), lambda i, j, kk: (kk, j))
    in_specs = [a_spec, b_spec]
    args = [a, b]
    if add is not None:
        in_specs.append(pl.BlockSpec((tm, tn), lambda i, j, kk: (i, j)))
        args.append(add)
    if n_major_out:
        out_shape = jax.ShapeDtypeStruct((n // tn, m, tn), out_dtype)
        out_spec = pl.BlockSpec((None, tm, tn), lambda i, j, kk: (j, i, 0))
    else:
        out_shape = jax.ShapeDtypeStruct((m, n), out_dtype)
        out_spec = pl.BlockSpec((tm, tn), lambda i, j, kk: (i, j))

    def body(*refs):
        a_ref, b_ref = refs[0], refs[1]
        add_ref = refs[2] if add is not None else None
        o_ref, acc = refs[-2], refs[-1]
        kk = pl.program_id(2)

        @pl.when(kk == 0)
        def _():
            acc[...] = jnp.zeros_like(acc)

        acc[...] += _dot(a_ref[...].astype(BF16), b_ref[...].astype(BF16), dims)

        @pl.when(kk == nk - 1)
        def _():
            r = acc[...]
            if alpha is not None:
                r = r * alpha
            if add_ref is not None:
                r = r + add_ref[...].astype(F32)
            o_ref[...] = r.astype(out_dtype)

    return pl.pallas_call(
        body, name=name, grid=(m // tm, n // tn, nk), in_specs=in_specs, out_specs=out_spec, out_shape=out_shape,
        scratch_shapes=[pltpu.VMEM((tm, tn), F32)],
        compiler_params=_cp(("parallel", "parallel", "arbitrary")),
    )(*args)


def _norm_mm(x, col_block, gain, w, *, name, tm=512):
    s = x.shape[0]
    k, n = w.shape
    tm = min(tm, s)

    def body(x_ref, g_ref, w_ref, z_ref, h_ref):
        xv = x_ref[...]
        r = lax.rsqrt(jnp.mean(xv * xv, axis=-1, keepdims=True) + EPS)
        hv = (xv * r * g_ref[...]).astype(BF16)
        h_ref[...] = hv
        z_ref[...] = _dot(hv, w_ref[...], NN)

    return pl.pallas_call(
        body, name=name, grid=(s // tm,),
        in_specs=[pl.BlockSpec((tm, k), lambda i: (i, col_block)), pl.BlockSpec((1, k), lambda i: (0, 0)),
                  pl.BlockSpec((k, n), lambda i: (0, 0))],
        out_specs=[pl.BlockSpec((tm, n), lambda i: (i, 0)), pl.BlockSpec((tm, k), lambda i: (i, 0))],
        out_shape=[jax.ShapeDtypeStruct((s, n), F32), jax.ShapeDtypeStruct((s, k), BF16)],
        compiler_params=_cp(("parallel",)),
    )(x, gain.reshape(1, k), w)


def _rmsnorm_bwd(x, col_block, gain, dh, dres=None, *, name, tm=512):
    s = x.shape[0]
    k = gain.shape[-1]
    tm = min(tm, s)

    def body(*refs):
        x_ref, g_ref, dh_ref = refs[0], refs[1], refs[2]
        dres_ref = refs[3] if dres is not None else None
        dx_ref, dg_ref = refs[-2], refs[-1]
        xv = x_ref[...]
        r = lax.rsqrt(jnp.mean(xv * xv, axis=-1, keepdims=True) + EPS)
        dhv = dh_ref[...].astype(F32)
        a = dhv * g_ref[...]
        dx = r * a - xv * (r * r * r) * jnp.mean(a * xv, axis=-1, keepdims=True)
        if dres_ref is not None:
            dx = dx + dres_ref[...]
        dx_ref[...] = dx

        @pl.when(pl.program_id(0) == 0)
        def _():
            dg_ref[...] = jnp.zeros_like(dg_ref)

        dg_ref[...] += jnp.sum(dhv * xv * r, axis=0, keepdims=True)

    in_specs = [pl.BlockSpec((tm, k), lambda i: (i, col_block)), pl.BlockSpec((1, k), lambda i: (0, 0)),
                pl.BlockSpec((tm, k), lambda i: (i, 0))]
    args = [x, gain.reshape(1, k), dh]
    if dres is not None:
        in_specs.append(pl.BlockSpec((tm, k), lambda i: (i, 0)))
        args.append(dres)
    dx, dg = pl.pallas_call(
        body, name=name, grid=(s // tm,), in_specs=in_specs,
        out_specs=[pl.BlockSpec((tm, k), lambda i: (i, 0)), pl.BlockSpec((1, k), lambda i: (0, 0))],
        out_shape=[jax.ShapeDtypeStruct((s, k), F32), jax.ShapeDtypeStruct((1, k), F32)],
        compiler_params=_cp(("arbitrary",)),
    )(*args)
    return dx, dg.reshape(k)


def _ffn_fwd(x, gain, w_gu4, w_d2, layer, *, name, tm=256):
    s = x.shape[0]
    tm = min(tm, s)

    def body(x_ref, g_ref, wgu_ref, wd_ref, xo_ref, gu_ref):
        xv = x_ref[...]
        r = lax.rsqrt(jnp.mean(xv * xv, axis=-1, keepdims=True) + EPS)
        hv = (xv * r * g_ref[...]).astype(BF16)
        y = jnp.zeros((tm, D), F32)
        for j in range(2):
            g = _dot(hv, wgu_ref[j], NN)
            u = _dot(hv, wgu_ref[2 + j], NN)
            gu_ref[:, j * FF_SHARD:(j + 1) * FF_SHARD] = g.astype(BF16)
            gu_ref[:, D_FF + j * FF_SHARD:D_FF + (j + 1) * FF_SHARD] = u.astype(BF16)
            act = (g * _sigmoid(g) * u).astype(BF16)
            y = y + _dot(act, wd_ref[j], NN)
        xo_ref[...] = xv + 0.5 * y

    return pl.pallas_call(
        body, name=name, grid=(s // tm,),
        in_specs=[pl.BlockSpec((tm, D), lambda i: (i, 0)), pl.BlockSpec((1, D), lambda i: (0, 0)),
                  pl.BlockSpec((N_CHIPS, None, D, FF_SHARD), lambda i: (0, layer, 0, 0), pipeline_mode=pl.Buffered(1)),
                  pl.BlockSpec((2, FF_SHARD, D), lambda i: (0, 0, 0), pipeline_mode=pl.Buffered(1))],
        out_specs=[pl.BlockSpec((tm, D), lambda i: (i, 0)), pl.BlockSpec((tm, 2 * D_FF), lambda i: (i, 0))],
        out_shape=[jax.ShapeDtypeStruct((s, D), F32), jax.ShapeDtypeStruct((s, 2 * D_FF), BF16)],
        compiler_params=_cp(("parallel",)),
    )(x, gain.reshape(1, D), w_gu4, w_d2)


def _ffn_bwd(x, dxo, gu, gain, w_gu4, w_d2, layer, *, name, tm=256):
    s = x.shape[0]
    tm = min(tm, s)

    def body(x_ref, dxo_ref, gu_ref, g_ref, wgu_ref, wd_ref, dx_ref, dgu_ref, act_ref, h_ref, dy_ref, dg_ref):
        xv = x_ref[...]
        r = lax.rsqrt(jnp.mean(xv * xv, axis=-1, keepdims=True) + EPS)
        xh = xv * r
        h_ref[...] = (xh * g_ref[...]).astype(BF16)
        dxov = dxo_ref[...]
        dy = (0.5 * dxov).astype(BF16)
        dy_ref[...] = dy
        dh = jnp.zeros((tm, D), F32)
        for j in range(2):
            g = gu_ref[:, j * FF_SHARD:(j + 1) * FF_SHARD].astype(F32)
            u = gu_ref[:, D_FF + j * FF_SHARD:D_FF + (j + 1) * FF_SHARD].astype(F32)
            sg = _sigmoid(g)
            silu = g * sg
            act_ref[:, j * FF_SHARD:(j + 1) * FF_SHARD] = (silu * u).astype(BF16)
            dact = _dot(dy, wd_ref[j], NT)
            dg = (dact * u * (sg * (1.0 + g * (1.0 - sg)))).astype(BF16)
            du = (dact * silu).astype(BF16)
            dgu_ref[:, j * FF_SHARD:(j + 1) * FF_SHARD] = dg
            dgu_ref[:, D_FF + j * FF_SHARD:D_FF + (j + 1) * FF_SHARD] = du
            dh = dh + _dot(dg, wgu_ref[j], NT) + _dot(du, wgu_ref[2 + j], NT)
        a = dh * g_ref[...]
        dx_ref[...] = dxov + r * a - xh * (r * jnp.mean(a * xh, axis=-1, keepdims=True))

        @pl.when(pl.program_id(0) == 0)
        def _():
            dg_ref[...] = jnp.zeros_like(dg_ref)

        dg_ref[...] += jnp.sum(dh * xh, axis=0, keepdims=True)

    row = lambda i: (i, 0)
    outs = pl.pallas_call(
        body, name=name, grid=(s // tm,),
        in_specs=[pl.BlockSpec((tm, D), row), pl.BlockSpec((tm, D), row), pl.BlockSpec((tm, 2 * D_FF), row),
                  pl.BlockSpec((1, D), lambda i: (0, 0)),
                  pl.BlockSpec((N_CHIPS, None, D, FF_SHARD), lambda i: (0, layer, 0, 0), pipeline_mode=pl.Buffered(1)),
                  pl.BlockSpec((2, FF_SHARD, D), lambda i: (0, 0, 0), pipeline_mode=pl.Buffered(1))],
        out_specs=[pl.BlockSpec((tm, D), row), pl.BlockSpec((tm, 2 * D_FF), row), pl.BlockSpec((tm, D_FF), row),
                   pl.BlockSpec((tm, D), row), pl.BlockSpec((tm, D), row), pl.BlockSpec((1, D), lambda i: (0, 0))],
        out_shape=[jax.ShapeDtypeStruct((s, D), F32), jax.ShapeDtypeStruct((s, 2 * D_FF), BF16),
                   jax.ShapeDtypeStruct((s, D_FF), BF16), jax.ShapeDtypeStruct((s, D), BF16),
                   jax.ShapeDtypeStruct((s, D), BF16), jax.ShapeDtypeStruct((1, D), F32)],
        compiler_params=_cp(("arbitrary",)),
    )(x, dxo, gu, gain.reshape(1, D), w_gu4, w_d2)
    dx, dgu, act, h, dy, dg = outs
    return dx, dgu, act, h, dy, dg.reshape(D)


DA = 128
SCALE_MLA = 1.0 / math.sqrt(NOPE + ROPE)
SCALE_FOX = 1.0 / math.sqrt(FOX_D)


def _causal_blocks(nb, key_major):
    if key_major:
        pairs = [(i, j) for j in range(nb) for i in range(j, nb)]
    else:
        pairs = [(i, j) for i in range(nb) for j in range(i + 1)]
    return (jnp.asarray(np.array([p[0] for p in pairs], np.int32)), jnp.asarray(np.array([p[1] for p in pairs], np.int32)))


HEADS_PER_STEP = 2


def _attn_fwd(qa, ka, va, dv, *, name, t=512):
    h, s, _ = qa.shape
    t = min(t, s)
    nb = s // t
    g = HEADS_PER_STEP
    qi, kj = _causal_blocks(nb, key_major=False)

    def body(qi_ref, kj_ref, q_ref, k_ref, v_ref, o_ref, lse_ref, m_sc, acc_sc):
        n = pl.program_id(1)
        i, j = qi_ref[n], kj_ref[n]

        @pl.when(j == 0)
        def _():
            m_sc[...] = jnp.full_like(m_sc, -jnp.inf)
            acc_sc[...] = jnp.zeros_like(acc_sc)

        def step(masked):
            for hh in range(g):
                sc = _dot(q_ref[hh], k_ref[hh], NT)
                if masked:
                    row = lax.broadcasted_iota(jnp.int32, (t, t), 0)
                    col = lax.broadcasted_iota(jnp.int32, (t, t), 1)
                    sc = jnp.where(col <= row, sc, -jnp.inf)
                m_old = m_sc[hh]
                m_new = jnp.maximum(m_old, jnp.max(sc, axis=-1, keepdims=True))
                p = jnp.exp(sc - m_new)
                acc_sc[hh] = jnp.exp(m_old - m_new) * acc_sc[hh] + _dot(p.astype(BF16), v_ref[hh], NN)
                m_sc[hh] = m_new

        @pl.when(j < i)
        def _():
            step(False)

        @pl.when(j == i)
        def _():
            step(True)
            for hh in range(g):
                acc = acc_sc[hh]
                l = acc[:, dv:dv + 1]
                o_ref[hh] = acc[:, :dv] / l
                lse_ref[hh] = m_sc[hh] + jnp.log(l)

    qmap = lambda hg, n, qi_r, kj_r: (hg, qi_r[n], 0)
    kmap = lambda hg, n, qi_r, kj_r: (hg, kj_r[n], 0)
    return pl.pallas_call(
        body, name=name,
        grid_spec=pltpu.PrefetchScalarGridSpec(
            num_scalar_prefetch=2, grid=(h // g, qi.shape[0]),
            in_specs=[pl.BlockSpec((g, t, DA), qmap), pl.BlockSpec((g, t, DA), kmap), pl.BlockSpec((g, t, DA), kmap)],
            out_specs=[pl.BlockSpec((g, t, dv), qmap), pl.BlockSpec((g, t, 1), qmap)],
            scratch_shapes=[pltpu.VMEM((g, t, 1), F32), pltpu.VMEM((g, t, DA), F32)]),
        out_shape=[jax.ShapeDtypeStruct((h, s, dv), F32), jax.ShapeDtypeStruct((h, s, 1), F32)],
        compiler_params=_cp(("parallel", "arbitrary")),
    )(qi, kj, qa, ka, va)


def _attn_bwd(qa, ka, va, doa, lse_row, delta_row, decay, *, name, t=512):
    h, s, _ = qa.shape
    t = min(t, s)
    nb = s // t
    qi, kj = _causal_blocks(nb, key_major=True)
    nsteps = qi.shape[0]

    def body(*refs):
        qi_ref, kj_ref, q_ref, k_ref, v_ref, do_ref, lse_ref, dl_ref = refs[:8]
        if decay:
            dq_ref, dk_ref, dv_ref, dcq_ref, dck_ref, dq_acc, dk_acc, dv_acc, dcq_acc, dck_acc = refs[8:]
        else:
            dq_ref, dk_ref, dv_ref, dq_acc, dk_acc, dv_acc = refs[8:]
        n = pl.program_id(1)
        i, j = qi_ref[n], kj_ref[n]

        @pl.when(n == 0)
        def _():
            dq_acc[...] = jnp.zeros_like(dq_acc)
            if decay:
                dcq_acc[...] = jnp.zeros_like(dcq_acc)

        @pl.when(i == j)
        def _():
            dk_acc[...] = jnp.zeros_like(dk_acc)
            dv_acc[...] = jnp.zeros_like(dv_acc)
            if decay:
                dck_acc[...] = jnp.zeros_like(dck_acc)

        def step(masked):
            st = _dot(k_ref[...], q_ref[...], NT)
            if masked:
                row = lax.broadcasted_iota(jnp.int32, (t, t), 0)
                col = lax.broadcasted_iota(jnp.int32, (t, t), 1)
                st = jnp.where(row <= col, st, -jnp.inf)
            pt = jnp.exp(st - lse_ref[...])
            dob = do_ref[...]
            dpt = _dot(v_ref[...], dob, NT)
            dst = pt * (dpt - dl_ref[...])
            dsb = dst.astype(BF16)
            dv_acc[...] += _dot(pt.astype(BF16), dob, NN)
            dk_acc[...] += _dot(dsb, q_ref[...], NN)
            dq_acc[i] += _dot(dsb, k_ref[...], TN)
            if decay:
                dcq_acc[i] += jnp.sum(dst, axis=0, keepdims=True)
                dck_acc[...] -= jnp.sum(dst, axis=1, keepdims=True)

        @pl.when(i > j)
        def _():
            step(False)

        @pl.when(i == j)
        def _():
            step(True)

        @pl.when(i == nb - 1)
        def _():
            dk_ref[...] = dk_acc[...]
            dv_ref[...] = dv_acc[...]
            if decay:
                dck_ref[...] = dck_acc[...]

        @pl.when(n == nsteps - 1)
        def _():
            dq_ref[...] = dq_acc[...]
            if decay:
                dcq_ref[...] = dcq_acc[...]

    kmap = lambda hh, n, qi_r, kj_r: (hh, kj_r[n], 0)
    qmap = lambda hh, n, qi_r, kj_r: (hh, qi_r[n], 0)
    qrow = lambda hh, n, qi_r, kj_r: (hh, 0, qi_r[n])
    whole = lambda hh, n, qi_r, kj_r: (hh, 0, 0, 0)
    in_specs = [pl.BlockSpec((None, t, DA), qmap), pl.BlockSpec((None, t, DA), kmap), pl.BlockSpec((None, t, DA), kmap),
                pl.BlockSpec((None, t, DA), qmap), pl.BlockSpec((None, 1, t), qrow), pl.BlockSpec((None, 1, t), qrow)]
    out_specs = [pl.BlockSpec((None, nb, t, DA), whole), pl.BlockSpec((None, t, DA), kmap), pl.BlockSpec((None, t, DA), kmap)]
    out_shape = [jax.ShapeDtypeStruct((h, nb, t, DA), F32), jax.ShapeDtypeStruct((h, s, DA), F32), jax.ShapeDtypeStruct((h, s, DA), F32)]
    scratch = [pltpu.VMEM((nb, t, DA), F32), pltpu.VMEM((t, DA), F32), pltpu.VMEM((t, DA), F32)]
    if decay:
        out_specs += [pl.BlockSpec((None, nb, 1, t), whole), pl.BlockSpec((None, t, 1), kmap)]
        out_shape += [jax.ShapeDtypeStruct((h, nb, 1, t), F32), jax.ShapeDtypeStruct((h, s, 1), F32)]
        scratch += [pltpu.VMEM((nb, 1, t), F32), pltpu.VMEM((t, 1), F32)]
    outs = pl.pallas_call(
        body, name=name,
        grid_spec=pltpu.PrefetchScalarGridSpec(num_scalar_prefetch=2, grid=(h, nsteps), in_specs=in_specs, out_specs=out_specs,
                                               scratch_shapes=scratch),
        out_shape=out_shape, compiler_params=_cp(("parallel", "arbitrary")),
    )(qi, kj, qa, ka, va, doa, lse_row, delta_row)
    outs = list(outs)
    outs[0] = outs[0].reshape(h, s, DA)
    if decay:
        outs[3] = outs[3].reshape(h, 1, s)
    return outs


def _sel(rows, cols, pairs, value=1.0):
    m = np.zeros((rows, cols), np.float32)
    for r, c in pairs:
        m[r, c] = value
    return jnp.asarray(m, BF16)


def _lane_row(lanes):
    m = np.zeros((1, DA), np.float32)
    m[0, list(lanes)] = 1.0
    return jnp.asarray(m)


def _rms(xv, gain):
    r = lax.rsqrt(jnp.mean(xv * xv, axis=-1, keepdims=True) + EPS)
    return xv * r * gain


def _mla_q_prep(z, gain, wq_a, wq_b, cq, sq, *, name, tm=512):
    s = z.shape[0]
    tm = min(tm, s)

    def body(z_ref, g_ref, wa_ref, wb_ref, c_ref, s_ref, qa_ref, qn_ref):
        qn = _rms(z_ref[...], g_ref[...]).astype(BF16)
        qn_ref[...] = qn
        c, sn = c_ref[...], s_ref[...]
        for hh in range(H):
            cols = slice(hh * DA, (hh + 1) * DA)
            qa_ref[hh] = (_dot(qn, wa_ref[:, cols], NN) * c + _dot(qn, wb_ref[:, cols], NN) * sn).astype(BF16)

    row = lambda i: (i, 0)
    fixed = lambda i: (0, 0)
    return pl.pallas_call(
        body, name=name, grid=(s // tm,),
        in_specs=[pl.BlockSpec((tm, Q_RANK), lambda i: (i, Z_QA // Q_RANK)), pl.BlockSpec((1, Q_RANK), fixed),
                  pl.BlockSpec((Q_RANK, H * DA), fixed), pl.BlockSpec((Q_RANK, H * DA), fixed),
                  pl.BlockSpec((tm, DA), row), pl.BlockSpec((tm, DA), row)],
        out_specs=[pl.BlockSpec((H, tm, DA), lambda i: (0, i, 0)), pl.BlockSpec((tm, Q_RANK), row)],
        out_shape=[jax.ShapeDtypeStruct((H, s, DA), BF16), jax.ShapeDtypeStruct((s, Q_RANK), BF16)],
        compiler_params=_cp(("parallel",)),
    )(z, gain.reshape(1, Q_RANK), wq_a, wq_b, cq, sq)


def _mla_kv_prep(z, gain, wk, wv, ck, sk, *, name, tm=512):
    s = z.shape[0]
    tm = min(tm, s)
    one = _lane_row([VDIM])

    def body(zkv_ref, z3_ref, z15_ref, g_ref, wk_ref, wv_ref, c_ref, s_ref, one_ref, ka_ref, va_ref, kvn_ref):
        kvn = _rms(zkv_ref[...], g_ref[...]).astype(BF16)
        kvn_ref[...] = kvn
        kpe = z3_ref[...] * c_ref[...] + z15_ref[...] * s_ref[...]
        for hh in range(H):
            cols = slice(hh * DA, (hh + 1) * DA)
            ka_ref[hh] = (_dot(kvn, wk_ref[:, cols], NN) + kpe).astype(BF16)
            va_ref[hh] = (_dot(kvn, wv_ref[:, cols], NN) + one_ref[...]).astype(BF16)

    row = lambda i: (i, 0)
    fixed = lambda i: (0, 0)
    blk = lambda c: pl.BlockSpec((tm, DA), lambda i: (i, c))
    heads = pl.BlockSpec((H, tm, DA), lambda i: (0, i, 0))
    return pl.pallas_call(
        body, name=name, grid=(s // tm,),
        in_specs=[blk(Z_KVA // DA), blk(Z_KR // DA), blk(Z_F // DA), pl.BlockSpec((1, KV_RANK), fixed),
                  pl.BlockSpec((KV_RANK, H * DA), fixed), pl.BlockSpec((KV_RANK, H * DA), fixed),
                  pl.BlockSpec((tm, DA), row), pl.BlockSpec((tm, DA), row), pl.BlockSpec((1, DA), fixed)],
        out_specs=[heads, heads, pl.BlockSpec((tm, KV_RANK), row)],
        out_shape=[jax.ShapeDtypeStruct((H, s, DA), BF16), jax.ShapeDtypeStruct((H, s, DA), BF16),
                   jax.ShapeDtypeStruct((s, KV_RANK), BF16)],
        compiler_params=_cp(("parallel",)),
    )(z, z, z, gain.reshape(1, KV_RANK), wk, wv, ck, sk, one)


DEC_C = (FOX_D, FOX_D + 1, FOX_D + 2)
DEC_1 = (FOX_D + 3, FOX_D + 4, FOX_D + 5)


def _fox_prep(z, c3t, *, name, tm=512):
    s = z.shape[0]
    tm = min(tm, s)
    w = H * FOX_D
    left = [(r, r) for r in range(FOX_D)]
    right = [(FOX_D + r, r) for r in range(FOX_D)]
    pq = jnp.stack([_sel(DA, DA, left, SCALE_FOX), _sel(DA, DA, right, SCALE_FOX)])
    pk = jnp.stack([_sel(DA, DA, left), _sel(DA, DA, right)])
    pcq = jnp.stack([_sel(32, DA, [(hh + 8 * k, DEC_C[k]) for k in range(3)]) for hh in range(H)])
    pck = jnp.stack([_sel(32, DA, [(hh + 8 * k, DEC_1[k]) for k in range(3)], -1.0) for hh in range(H)])
    rows3 = jnp.concatenate([_lane_row(DEC_1), _lane_row(DEC_C), _lane_row([FOX_D])], axis=0)

    def body(zq_ref, zk_ref, zv_ref, c_ref, pq_ref, pk_ref, pcq_ref, pck_ref, r_ref, qa_ref, ka_ref, va_ref):
        c3 = c_ref[...]
        for pair in range(H // 2):
            lanes = slice(pair * DA, (pair + 1) * DA)
            zq, zk, zv = zq_ref[:, lanes].astype(BF16), zk_ref[:, lanes].astype(BF16), zv_ref[:, lanes].astype(BF16)
            for side in range(2):
                hh = 2 * pair + side
                qa_ref[hh] = (_dot(zq, pq_ref[side], NN) + _dot(c3, pcq_ref[hh], TN) + r_ref[0:1, :]).astype(BF16)
                ka_ref[hh] = (_dot(zk, pk_ref[side], NN) + _dot(c3, pck_ref[hh], TN) + r_ref[1:2, :]).astype(BF16)
                va_ref[hh] = (_dot(zv, pk_ref[side], NN) + r_ref[2:3, :]).astype(BF16)

    fixed2 = lambda i: (0, 0)
    fixed3 = lambda i: (0, 0, 0)
    heads = pl.BlockSpec((H, tm, DA), lambda i: (0, i, 0))
    zblk = lambda c: pl.BlockSpec((tm, w), lambda i: (i, c))
    return pl.pallas_call(
        body, name=name, grid=(s // tm,),
        in_specs=[zblk(Z_FOX // w), zblk(Z_FOX // w + 1), zblk(Z_FOX // w + 2), pl.BlockSpec((32, tm), lambda i: (0, i)),
                  pl.BlockSpec((2, DA, DA), fixed3), pl.BlockSpec((2, DA, DA), fixed3),
                  pl.BlockSpec((H, 32, DA), fixed3), pl.BlockSpec((H, 32, DA), fixed3), pl.BlockSpec((3, DA), fixed2)],
        out_specs=[heads, heads, heads], out_shape=[jax.ShapeDtypeStruct((H, s, DA), BF16)] * 3,
        compiler_params=_cp(("parallel",)),
    )(z, z, z, c3t, pq, pk, pcq, pck, rows3)


def _mix_out(oa, yb, oc, w_out, x1, *, name, tm=512):
    s = yb.shape[0]
    tm = min(tm, s)
    e2 = jnp.stack([_sel(VDIM, DA, [(r, r) for r in range(VDIM)]), _sel(VDIM, DA, [(r, VDIM + r) for r in range(VDIM)])])

    def body(oa_ref, yb_ref, oc_ref, e_ref, w_ref, x_ref, x2_ref, cat_ref):
        def pairs(o_ref):
            return [(_dot(o_ref[2 * p].astype(BF16), e_ref[0], NN) + _dot(o_ref[2 * p + 1].astype(BF16), e_ref[1], NN)).astype(BF16)
                    for p in range(H // 2)]

        cat = jnp.concatenate(pairs(oa_ref) + [yb_ref[...].astype(BF16)] + pairs(oc_ref), axis=1)
        cat_ref[...] = cat
        x2_ref[...] = x_ref[...] + _dot(cat, w_ref[...], NN)

    row = lambda i: (i, 0)
    heads = pl.BlockSpec((H, tm, VDIM), lambda i: (0, i, 0))
    return pl.pallas_call(
        body, name=name, grid=(s // tm,),
        in_specs=[heads, pl.BlockSpec((tm, POOL_W), row), heads, pl.BlockSpec((2, VDIM, DA), lambda i: (0, 0, 0)),
                  pl.BlockSpec((D, D), lambda i: (0, 0)), pl.BlockSpec((tm, D), row)],
        out_specs=[pl.BlockSpec((tm, D), row), pl.BlockSpec((tm, D), row)],
        out_shape=[jax.ShapeDtypeStruct((s, D), F32), jax.ShapeDtypeStruct((s, D), BF16)],
        compiler_params=_cp(("parallel",)),
    )(oa, yb, oc, e2, w_out, x1)


def _mix_out_bwd(dx2b, w_out, oa, oc, *, name, tm=512):
    s = dx2b.shape[0]
    tm = min(tm, s)
    f2 = jnp.stack([_sel(DA, DA, [(r, r) for r in range(VDIM)]), _sel(DA, DA, [(VDIM + r, r) for r in range(VDIM)])])
    nv = H * VDIM

    def body(dx_ref, w_ref, oa_ref, oc_ref, f_ref, doa_ref, doc_ref, dyb_ref, dla_ref, dlc_ref):
        dcat = _dot(dx_ref[...], w_ref[...], NT)
        dyb_ref[...] = dcat[:, nv:nv + POOL_W]
        for base, o_ref, do_ref, dl_ref in ((0, oa_ref, doa_ref, dla_ref), (nv + POOL_W, oc_ref, doc_ref, dlc_ref)):
            for p in range(H // 2):
                blk = dcat[:, base + p * DA:base + (p + 1) * DA].astype(BF16)
                for side in range(2):
                    hh = 2 * p + side
                    do = _dot(blk, f_ref[side], NN)
                    do_ref[hh] = do.astype(BF16)
                    dl_ref[hh] = jnp.sum(do[:, :VDIM] * o_ref[hh], axis=-1, keepdims=True)

    row = lambda i: (i, 0)
    heads = lambda w: pl.BlockSpec((H, tm, w), lambda i: (0, i, 0))
    return pl.pallas_call(
        body, name=name, grid=(s // tm,),
        in_specs=[pl.BlockSpec((tm, D), row), pl.BlockSpec((D, D), lambda i: (0, 0)), heads(VDIM), heads(VDIM),
                  pl.BlockSpec((2, DA, DA), lambda i: (0, 0, 0))],
        out_specs=[heads(DA), heads(DA), pl.BlockSpec((tm, POOL_W), row), heads(1), heads(1)],
        out_shape=[jax.ShapeDtypeStruct((H, s, DA), BF16), jax.ShapeDtypeStruct((H, s, DA), BF16),
                   jax.ShapeDtypeStruct((s, POOL_W), F32), jax.ShapeDtypeStruct((H, s, 1), F32), jax.ShapeDtypeStruct((H, s, 1), F32)],
        compiler_params=_cp(("parallel",)),
    )(dx2b, w_out, oa, oc, f2)


def _mla_bwd_prep(dqa, dka, dva, dft, cq, sq, ck, sk, *, name, tm=512):
    s = dqa.shape[1]
    tm = min(tm, s)
    keep = _lane_row(range(NOPE))

    def body(dq_ref, dk_ref, dv_ref, dft_ref, cq_ref, sq_ref, ck_ref, sk_ref, keep_ref, dqab_ref, dkv_ref, dz3_ref, dz15_ref):
        cqv, sqv = cq_ref[...], sq_ref[...]
        dkpe = jnp.zeros((tm, DA), F32)
        for hh in range(H):
            lanes = slice(hh * DA, (hh + 1) * DA)
            dq = dq_ref[hh]
            dqab_ref[:, lanes] = (dq * cqv).astype(BF16)
            dqab_ref[:, H * DA + hh * DA:H * DA + (hh + 1) * DA] = (dq * sqv).astype(BF16)
            dk = dk_ref[hh]
            dkpe = dkpe + dk
            dkv_ref[:, lanes] = (dk * keep_ref[...]).astype(BF16)
            dkv_ref[:, H * DA + hh * DA:H * DA + (hh + 1) * DA] = (dv_ref[hh] * keep_ref[...]).astype(BF16)
        dz3_ref[...] = (dkpe * ck_ref[...]).astype(BF16)
        dz15_ref[...] = (dkpe * sk_ref[...] + dft_ref[...]).astype(BF16)

    row = lambda i: (i, 0)
    heads = pl.BlockSpec((H, tm, DA), lambda i: (0, i, 0))
    tab = pl.BlockSpec((tm, DA), row)
    return pl.pallas_call(
        body, name=name, grid=(s // tm,),
        in_specs=[heads, heads, heads, tab, tab, tab, tab, tab, pl.BlockSpec((1, DA), lambda i: (0, 0))],
        out_specs=[pl.BlockSpec((tm, 2 * H * DA), row), pl.BlockSpec((tm, 2 * H * DA), row), tab, tab],
        out_shape=[jax.ShapeDtypeStruct((s, 2 * H * DA), BF16), jax.ShapeDtypeStruct((s, 2 * H * DA), BF16),
                   jax.ShapeDtypeStruct((s, DA), BF16), jax.ShapeDtypeStruct((s, DA), BF16)],
        compiler_params=_cp(("parallel",)),
    )(dqa, dka, dva, dft, cq, sq, ck, sk, keep)


def _fox_bwd_prep(dfqa, dfka, dfva, *, name, tm=512):
    s = dfqa.shape[1]
    tm = min(tm, s)
    place = lambda v: jnp.stack([_sel(DA, DA, [(r, r) for r in range(FOX_D)], v), _sel(DA, DA, [(r, FOX_D + r) for r in range(FOX_D)], v)])
    gq, gk = place(SCALE_FOX), place(1.0)

    def body(dq_ref, dk_ref, dv_ref, gq_ref, gk_ref, dz_ref):
        for part, (d_ref, g_ref) in enumerate(((dq_ref, gq_ref), (dk_ref, gk_ref), (dv_ref, gk_ref))):
            for p in range(H // 2):
                blk = _dot(d_ref[2 * p].astype(BF16), g_ref[0], NN) + _dot(d_ref[2 * p + 1].astype(BF16), g_ref[1], NN)
                lo = part * H * FOX_D + p * DA
                dz_ref[:, lo:lo + DA] = blk.astype(BF16)

    heads = pl.BlockSpec((H, tm, DA), lambda i: (0, i, 0))
    sel = pl.BlockSpec((2, DA, DA), lambda i: (0, 0, 0))
    return pl.pallas_call(
        body, name=name, grid=(s // tm,), in_specs=[heads, heads, heads, sel, sel],
        out_specs=pl.BlockSpec((tm, 3 * H * FOX_D), lambda i: (i, 0)),
        out_shape=jax.ShapeDtypeStruct((s, 3 * H * FOX_D), BF16), compiler_params=_cp(("parallel",)),
    )(dfqa, dfka, dfva, gq, gk)


def _lane_scan(x, s, reverse):
    lane = lax.broadcasted_iota(jnp.int32, x.shape, 1)
    sh = 1
    while sh < s:
        if reverse:
            x = x + jnp.where(lane < s - sh, pltpu.roll(x, s - sh, axis=1), 0.0)
        else:
            x = x + jnp.where(lane >= sh, pltpu.roll(x, sh, axis=1), 0.0)
        sh *= 2
    return x


def _gate_fwd(z, col_block, bias, *, name):
    s = z.shape[0]

    def body(z_ref, b_ref, f_ref, c_ref):
        ft = z_ref[...].T[0:8, :]
        f_ref[...] = ft
        xg = ft + b_ref[...]
        lf = jnp.minimum(xg, 0.0) - jnp.log(1.0 + jnp.exp(-jnp.abs(xg)))
        c = _lane_scan(lf, s, False)
        hi = c.astype(BF16).astype(F32)
        r = c - hi
        mid = r.astype(BF16).astype(F32)
        lo = r - mid
        c_ref[...] = jnp.concatenate([hi, mid, lo, jnp.zeros_like(hi)], axis=0).astype(BF16)

    return pl.pallas_call(
        body, name=name, grid=(1,),
        in_specs=[pl.BlockSpec((s, 128), lambda i: (0, col_block)), pl.BlockSpec((8, 1), lambda i: (0, 0))],
        out_specs=[pl.BlockSpec((8, s), lambda i: (0, 0)), pl.BlockSpec((32, s), lambda i: (0, 0))],
        out_shape=[jax.ShapeDtypeStruct((8, s), F32), jax.ShapeDtypeStruct((32, s), BF16)],
        compiler_params=_cp(("arbitrary",)))(z, bias)


def _gate_bwd(ft, bias, dc, *, name):
    s = ft.shape[1]

    def body(f_ref, b_ref, dc_ref, df_ref, db_ref):
        xg = f_ref[...] + b_ref[...]
        dlf = _lane_scan(dc_ref[...], s, True)
        df = dlf * _sigmoid(-xg)
        db_ref[...] = jnp.sum(df, axis=-1, keepdims=True)
        df_ref[...] = jnp.concatenate([df, jnp.zeros((DA - 8, s), F32)], axis=0).T

    return pl.pallas_call(body, name=name, out_shape=[jax.ShapeDtypeStruct((s, DA), F32), jax.ShapeDtypeStruct((8, 1), F32)],
                          compiler_params=_cp())(ft, bias, dc)


def _pool_lane_consts(tm, i):
    lane = lax.broadcasted_iota(jnp.int32, (tm, POOL_W), 1)
    tok = lax.broadcasted_iota(jnp.int32, (tm, POOL_W), 0) + i * tm
    win = jnp.where(lane < 64, 2, jnp.where(lane < 128, 4, jnp.where(lane < 192, 8, 16)))
    cnt = jnp.minimum(tok + 1, win).astype(F32)
    return lane, tok, cnt


def _pick_window(lane, s2, s4, s8, s16):
    return jnp.where(lane < 64, s2, jnp.where(lane < 128, s4, jnp.where(lane < 192, s8, s16)))


def _pool_fwd(z, col_block, bd, scale, *, name, tm=512):
    s = z.shape[0]
    tm = min(tm, s)
    hb = tm // POOL_HALO

    def body(u_ref, halo_ref, bd_ref, sc_ref, y_ref, p_ref, buf):
        i = pl.program_id(0)
        buf[0:POOL_HALO, :] = halo_ref[...] * (i > 0).astype(F32)
        buf[POOL_HALO:, :] = u_ref[...]

        def back(k):
            return buf[POOL_HALO - k:POOL_HALO - k + tm, :]

        u = u_ref[...]
        s2 = u + back(1)
        s4 = s2 + back(2) + back(3)
        s8 = s4 + back(4) + back(5) + back(6) + back(7)
        s16 = s8
        for k in range(8, 16):
            s16 = s16 + back(k)
        lane, _, cnt = _pool_lane_consts(tm, i)
        pooled = (_pick_window(lane, s2, s4, s8, s16) / cnt - u).astype(BF16)
        p_ref[...] = pooled
        y_ref[...] = _dot(pooled, bd_ref[...], NN) * sc_ref[...]

    return pl.pallas_call(
        body, name=name, grid=(s // tm,),
        in_specs=[pl.BlockSpec((tm, POOL_W), lambda i: (i, col_block)),
                  pl.BlockSpec((POOL_HALO, POOL_W), lambda i: (jnp.maximum(i * hb - 1, 0), col_block)),
                  pl.BlockSpec((POOL_W, POOL_W), lambda i: (0, 0)), pl.BlockSpec((1, POOL_W), lambda i: (0, 0))],
        out_specs=[pl.BlockSpec((tm, POOL_W), lambda i: (i, 0)), pl.BlockSpec((tm, POOL_W), lambda i: (i, 0))],
        out_shape=[jax.ShapeDtypeStruct((s, POOL_W), F32), jax.ShapeDtypeStruct((s, POOL_W), BF16)],
        scratch_shapes=[pltpu.VMEM((tm + POOL_HALO, POOL_W), F32)],
        compiler_params=_cp(("parallel",)),
    )(z, z, bd, scale.reshape(1, POOL_W))


def _pool_bwd_a(dy, pooled, bd, scale, *, name, tm=512):
    s = dy.shape[0]
    tm = min(tm, s)

    def body(dy_ref, p_ref, bd_ref, sc_ref, dq_ref, dys_ref, dsc_ref):
        i = pl.program_id(0)
        dyv = dy_ref[...]
        y0 = _dot(p_ref[...], bd_ref[...], NN)
        dys = (dyv * sc_ref[...]).astype(BF16)
        dys_ref[...] = dys
        dp = _dot(dys, bd_ref[...], NT)
        _, _, cnt = _pool_lane_consts(tm, i)
        dq_ref[:, 0:POOL_W] = dp / cnt
        dq_ref[:, POOL_W:] = dp

        @pl.when(i == 0)
        def _():
            dsc_ref[...] = jnp.zeros_like(dsc_ref)

        dsc_ref[...] += jnp.sum(dyv * y0, axis=0, keepdims=True)

    row = lambda i: (i, 0)
    dq, dys, dsc = pl.pallas_call(
        body, name=name, grid=(s // tm,),
        in_specs=[pl.BlockSpec((tm, POOL_W), row), pl.BlockSpec((tm, POOL_W), row),
                  pl.BlockSpec((POOL_W, POOL_W), lambda i: (0, 0)), pl.BlockSpec((1, POOL_W), lambda i: (0, 0))],
        out_specs=[pl.BlockSpec((tm, 2 * POOL_W), row), pl.BlockSpec((tm, POOL_W), row), pl.BlockSpec((1, POOL_W), lambda i: (0, 0))],
        out_shape=[jax.ShapeDtypeStruct((s, 2 * POOL_W), F32), jax.ShapeDtypeStruct((s, POOL_W), BF16),
                   jax.ShapeDtypeStruct((1, POOL_W), F32)],
        compiler_params=_cp(("arbitrary",)),
    )(dy, pooled, bd, scale.reshape(1, POOL_W))
    return dq, dys, dsc.reshape(POOL_W)


def _pool_bwd_b(dq, *, name, tm=512):
    s = dq.shape[0]
    tm = min(tm, s)
    hb = tm // POOL_HALO
    nblk = s // tm

    def body(q_ref, dp_ref, halo_ref, du_ref, buf):
        i = pl.program_id(0)
        buf[0:tm, :] = q_ref[...]
        buf[tm:, :] = halo_ref[...] * (i < nblk - 1).astype(F32)

        def ahead(k):
            return buf[k:k + tm, :]

        q = q_ref[...]
        s2 = q + ahead(1)
        s4 = s2 + ahead(2) + ahead(3)
        s8 = s4 + ahead(4) + ahead(5) + ahead(6) + ahead(7)
        s16 = s8
        for k in range(8, 16):
            s16 = s16 + ahead(k)
        lane = lax.broadcasted_iota(jnp.int32, (tm, POOL_W), 1)
        du_ref[...] = _pick_window(lane, s2, s4, s8, s16) - dp_ref[...]

    return pl.pallas_call(
        body, name=name, grid=(nblk,),
        in_specs=[pl.BlockSpec((tm, POOL_W), lambda i: (i, 0)), pl.BlockSpec((tm, POOL_W), lambda i: (i, 1)),
                  pl.BlockSpec((POOL_HALO, POOL_W), lambda i: (jnp.minimum((i + 1) * hb, nblk * hb - 1), 0))],
        out_specs=pl.BlockSpec((tm, POOL_W), lambda i: (i, 0)),
        out_shape=jax.ShapeDtypeStruct((s, POOL_W), F32),
        scratch_shapes=[pltpu.VMEM((tm + POOL_HALO, POOL_W), F32)],
        compiler_params=_cp(("parallel",)),
    )(dq, dq, dq)


def _loss_head(x, gain, target, *, name, tm=512):
    s = x.shape[0]
    tm = min(tm, s)

    def body(x_ref, g_ref, t_ref, dx_ref, dg_ref, loss_ref):
        xv = x_ref[...]
        r = lax.rsqrt(jnp.mean(xv * xv, axis=-1, keepdims=True) + EPS)
        xh = xv * r
        err = xh * g_ref[...] - t_ref[...]
        dy = err * (1.0 / D)
        a = dy * g_ref[...]
        dx_ref[...] = r * a - xh * (r * jnp.mean(a * xh, axis=-1, keepdims=True))

        @pl.when(pl.program_id(0) == 0)
        def _():
            dg_ref[...] = jnp.zeros_like(dg_ref)
            loss_ref[...] = jnp.zeros_like(loss_ref)

        dg_ref[...] += jnp.sum(dy * xh, axis=0, keepdims=True)
        part = 0.5 * jnp.sum(jnp.mean(err * err, axis=-1, keepdims=True), axis=0, keepdims=True)
        loss_ref[...] += jnp.broadcast_to(part, loss_ref.shape)

    row = lambda i: (i, 0)
    dx, dg, loss = pl.pallas_call(
        body, name=name, grid=(s // tm,),
        in_specs=[pl.BlockSpec((tm, D), row), pl.BlockSpec((1, D), lambda i: (0, 0)), pl.BlockSpec((tm, D), row)],
        out_specs=[pl.BlockSpec((tm, D), row), pl.BlockSpec((1, D), lambda i: (0, 0)), pl.BlockSpec((1, 128), lambda i: (0, 0))],
        out_shape=[jax.ShapeDtypeStruct((s, D), F32), jax.ShapeDtypeStruct((1, D), F32), jax.ShapeDtypeStruct((1, 128), F32)],
        compiler_params=_cp(("arbitrary",)),
    )(x, gain.reshape(1, D), target)
    return dx, dg.reshape(D), loss[0, 0]


def _adamw(w, g, m, v, *, name, tr=512):
    rows, cols = w.shape
    tr = min(tr, rows)
    assert rows % tr == 0, (name, rows, tr)
    c_m = 1.0 - ADAM_B1
    c_v = 1.0 - ADAM_B2
    bc1 = 1.0 - ADAM_B1 ** ADAM_STEP
    bc2 = 1.0 - ADAM_B2 ** ADAM_STEP

    def body(w_ref, g_ref, m_ref, v_ref, d_ref, mo_ref, vo_ref):
        gv = g_ref[...]
        mn = ADAM_B1 * m_ref[...] + c_m * gv
        vn = ADAM_B2 * v_ref[...] + c_v * (gv * gv)
        mo_ref[...] = mn
        vo_ref[...] = vn
        d_ref[...] = -ADAM_LR * ((mn / bc1) / (jnp.sqrt(vn / bc2) + ADAM_EPS) + ADAM_WD * w_ref[...])

    spec = pl.BlockSpec((tr, cols), lambda i: (i, 0))
    return pl.pallas_call(body, name=name, grid=(rows // tr,), in_specs=[spec] * 4, out_specs=[spec] * 3,
                          out_shape=[jax.ShapeDtypeStruct((rows, cols), F32)] * 3,
                          compiler_params=_cp(("parallel",)))(w, g, m, v)


def _row_tile(rows, cap):
    for t in range(min(cap, rows), 0, -1):
        if rows % t == 0 and t % 16 == 0:
            return t
    return rows


def _position():
    return jnp.stack([lax.axis_index("c"), 2 * lax.axis_index("x") + lax.axis_index("y")]).astype(jnp.int32)


def _sum2_bf16(pos, full, sib, *, name, tr=256):
    n, half, cols = sib.shape
    tr = _row_tile(half, tr)
    nb = half // tr

    def body(pos_ref, a_ref, b_ref, o_ref):
        o_ref[...] = (a_ref[...] + b_ref[...]).astype(BF16)

    spec = pl.BlockSpec((None, tr, cols), lambda j, i, p: (j, i, 0))
    return pl.pallas_call(
        body, name=name,
        grid_spec=pltpu.PrefetchScalarGridSpec(
            num_scalar_prefetch=1, grid=(n, nb),
            in_specs=[pl.BlockSpec((None, tr, cols), lambda j, i, p: (j, p[0] * nb + i, 0)), spec], out_specs=spec),
        out_shape=jax.ShapeDtypeStruct(sib.shape, BF16), compiler_params=_cp(("parallel", "parallel")))(pos, full, sib)


def _sum5(pos, full, sib, recv, *, name, tr=256):
    _, rows, cols = full.shape
    half = rows // 2
    tr = _row_tile(half, tr)
    nb = half // tr

    def body(pos_ref, a_ref, b_ref, r_ref, o_ref):
        acc = a_ref[...] + b_ref[...]
        for kk in range(3):
            acc = acc + r_ref[kk].astype(F32)
        o_ref[...] = acc

    return pl.pallas_call(
        body, name=name,
        grid_spec=pltpu.PrefetchScalarGridSpec(
            num_scalar_prefetch=1, grid=(nb,),
            in_specs=[pl.BlockSpec((None, tr, cols), lambda i, p: (p[1], p[0] * nb + i, 0)),
                      pl.BlockSpec((None, tr, cols), lambda i, p: (p[1], i, 0)),
                      pl.BlockSpec((3, tr, cols), lambda i, p: (0, i, 0))],
            out_specs=pl.BlockSpec((tr, cols), lambda i, p: (p[0] * nb + i, 0))),
        out_shape=jax.ShapeDtypeStruct((rows, cols), F32), compiler_params=_cp(("parallel",)))(pos, full, sib, recv)


def _place():
    x, y, c = lax.axis_index("x"), lax.axis_index("y"), lax.axis_index("c")
    chips = [(1 - x, y), (x, 1 - y), (1 - x, 1 - y)]
    return x, y, c, 2 * x + y, chips


def _gather_weights(shards):
    n = len(shards)

    def body(*refs):
        ins, outs = refs[:n], refs[n:2 * n]
        send_i, recv_i, send_d, recv_d, send_o, recv_o = refs[2 * n:]
        x, y, c, me, chips = _place()
        local = [pltpu.make_async_remote_copy(src_ref=ins[t], dst_ref=outs[t].at[me], send_sem=send_o.at[t], recv_sem=recv_o.at[t],
                                              device_id=(x, y, 1 - c), device_id_type=MESH) for t in range(n)]
        for cp in local:
            cp.start()

        def ici(t, kk, src_chip, to):
            return pltpu.make_async_remote_copy(
                src_ref=ins[t].at[c], dst_ref=outs[t].at[src_chip, c], send_sem=send_i.at[t * 3 + kk],
                recv_sem=recv_i.at[t * 3 + kk], device_id=to, device_id_type=MESH)

        def d2d(t, kk, src_chip, layer):
            return pltpu.make_async_remote_copy(
                src_ref=outs[t].at[src_chip, layer], dst_ref=outs[t].at[src_chip, layer], send_sem=send_d.at[t * 3 + kk],
                recv_sem=recv_d.at[t * 3 + kk], device_id=(x, y, 1 - c), device_id_type=MESH)

        sends = [ici(t, kk, me, (px, py, c)) for t in range(n) for kk, (px, py) in enumerate(chips)]
        for cp in sends:
            cp.start()
        fwd = []
        for t in range(n):
            for kk, (px, py) in enumerate(chips):
                ici(t, kk, 2 * px + py, (px, py, c)).wait_recv()
                f = d2d(t, kk, 2 * px + py, c)
                f.start()
                fwd.append(f)
        for t in range(n):
            for kk, (px, py) in enumerate(chips):
                d2d(t, kk, 2 * px + py, 1 - c).wait_recv()
        for cp in sends + fwd:
            cp.wait_send()
        for cp in local:
            cp.wait()

    return pl.pallas_call(
        body, name="gather_weights", in_specs=[HBM_SPEC] * n, out_specs=[HBM_SPEC] * n,
        out_shape=[jax.ShapeDtypeStruct((N_CHIPS,) + s.shape, s.dtype) for s in shards],
        scratch_shapes=[pltpu.SemaphoreType.DMA((3 * n,)), pltpu.SemaphoreType.DMA((3 * n,)),
                        pltpu.SemaphoreType.DMA((3 * n,)), pltpu.SemaphoreType.DMA((3 * n,)),
                        pltpu.SemaphoreType.DMA((n,)), pltpu.SemaphoreType.DMA((n,))],
    )(*shards)


def _reduce_stage1(grads):
    n = len(grads)

    def body(*refs):
        ins, sib = refs[:n], refs[n:2 * n]
        send, recv = refs[2 * n:]
        x, y, c, me, chips = _place()
        cps = []
        for t in range(n):
            rows = ins[t].shape[1] // 2
            cp = pltpu.make_async_remote_copy(
                src_ref=ins[t].at[:, pl.ds((1 - c) * rows, rows), :], dst_ref=sib[t], send_sem=send.at[t],
                recv_sem=recv.at[t], device_id=(x, y, 1 - c), device_id_type=MESH)
            cp.start()
            cps.append(cp)
        for cp in cps:
            cp.wait()

    return pl.pallas_call(
        body, name="reduce_stage1", in_specs=[HBM_SPEC] * n, out_specs=[HBM_SPEC] * n,
        out_shape=[jax.ShapeDtypeStruct((N_CHIPS, g.shape[1] // 2, g.shape[2]), F32) for g in grads],
        scratch_shapes=[pltpu.SemaphoreType.DMA((n,)), pltpu.SemaphoreType.DMA((n,))],
    )(*grads)


def _reduce_stage2(psum_bf16):
    n = len(psum_bf16)

    def body(*refs):
        ps, rcv = refs[:n], refs[n:2 * n]
        send, recv = refs[2 * n:]
        x, y, c, me, chips = _place()
        cps = []
        for t in range(n):
            for kk, (px, py) in enumerate(chips):
                cp = pltpu.make_async_remote_copy(
                    src_ref=ps[t].at[2 * px + py], dst_ref=rcv[t].at[kk], send_sem=send.at[t * 3 + kk],
                    recv_sem=recv.at[t * 3 + kk], device_id=(px, py, c), device_id_type=MESH)
                cp.start()
                cps.append(cp)
        for cp in cps:
            cp.wait()

    return pl.pallas_call(
        body, name="reduce_stage2", in_specs=[HBM_SPEC] * n, out_specs=[HBM_SPEC] * n,
        out_shape=[jax.ShapeDtypeStruct((3,) + p.shape[1:], p.dtype) for p in psum_bf16],
        scratch_shapes=[pltpu.SemaphoreType.DMA((3 * n,)), pltpu.SemaphoreType.DMA((3 * n,))],
    )(*psum_bf16)


def _reduce_stage3(reduced):
    n = len(reduced)

    def body(*refs):
        outs = refs[n:2 * n]
        send, recv = refs[2 * n:]
        x, y, c, me, chips = _place()
        cps = []
        for t in range(n):
            rows = outs[t].shape[0] // 2
            mine = outs[t].at[pl.ds(c * rows, rows), :]
            cp = pltpu.make_async_remote_copy(src_ref=mine, dst_ref=mine, send_sem=send.at[t], recv_sem=recv.at[t],
                                              device_id=(x, y, 1 - c), device_id_type=MESH)
            cp.start()
            cps.append(cp)
        for cp in cps:
            cp.wait()

    return pl.pallas_call(
        body, name="reduce_stage3", in_specs=[HBM_SPEC] * n, out_specs=[HBM_SPEC] * n,
        out_shape=[jax.ShapeDtypeStruct(r.shape, r.dtype) for r in reduced],
        input_output_aliases={t: t for t in range(n)},
        scratch_shapes=[pltpu.SemaphoreType.DMA((n,)), pltpu.SemaphoreType.DMA((n,))],
    )(*reduced)


def _allreduce_small(v):
    rows, cols = v.shape

    def body(v_ref, o_ref, buf, send, recv, loc):
        x, y, c, me, chips = _place()
        mine = 4 * x + 2 * y + c
        lc = pltpu.make_async_copy(v_ref, buf.at[mine], loc)
        lc.start()
        peers = []
        for fx in range(2):
            for fy in range(2):
                for fc in range(2):
                    if fx or fy or fc:
                        peers.append((fx, fy, fc))
        cps = []
        for kk, (fx, fy, fc) in enumerate(peers):
            to = (x ^ fx, y ^ fy, c ^ fc)
            cp = pltpu.make_async_remote_copy(src_ref=v_ref, dst_ref=buf.at[mine], send_sem=send.at[kk], recv_sem=recv.at[kk],
                                              device_id=to, device_id_type=MESH)
            cp.start()
            cps.append((cp, to))
        for kk, (cp, to) in enumerate(cps):
            src = 4 * to[0] + 2 * to[1] + to[2]
            pltpu.make_async_remote_copy(src_ref=v_ref, dst_ref=buf.at[src], send_sem=send.at[kk], recv_sem=recv.at[kk],
                                         device_id=to, device_id_type=MESH).wait_recv()
        for cp, _ in cps:
            cp.wait_send()
        lc.wait()
        acc = buf[0]
        for d in range(1, 8):
            acc = acc + buf[d]
        o_ref[...] = acc

    return pl.pallas_call(
        body, name="allreduce_small", in_specs=[pl.BlockSpec(memory_space=pltpu.VMEM)],
        out_specs=pl.BlockSpec(memory_space=pltpu.VMEM), out_shape=jax.ShapeDtypeStruct((rows, cols), F32),
        scratch_shapes=[pltpu.VMEM((8, rows, cols), F32), pltpu.SemaphoreType.DMA((7,)), pltpu.SemaphoreType.DMA((7,)),
                        pltpu.SemaphoreType.DMA],
        compiler_params=pltpu.CompilerParams(vmem_limit_bytes=VMEM_LIMIT_V7X),
    )(v)


def _pad_w_in(w):
    z = lambda n: jnp.zeros(w.shape[:-1] + (n,), w.dtype)
    return jnp.concatenate([w[..., 0:384], z(64), w[..., 384:416], z(32), w[..., 416:1824],
                            w[..., 1824:1830], z(58), w[..., 400:416], w[..., 384:400], z(32)], axis=-1)


def _unpad_w_in(g):
    x1 = g[..., 448:464] + g[..., Z_F + 80:Z_F + 96]
    x2 = g[..., 464:480] + g[..., Z_F + 64:Z_F + 80]
    return jnp.concatenate([g[..., 0:384], x1, x2, g[..., 512:1920], g[..., 1920:1926]], axis=-1)


def _block_diag(pw):
    out = jnp.zeros((POOL_W, POOL_W), pw.dtype)
    for g in range(4):
        out = out.at[g * 64:(g + 1) * 64, g * 64:(g + 1) * 64].set(pw[g])
    return out


def _rope_tables(s):
    inv_freq = ROPE_THETA ** (-jnp.arange(0, ROPE, 2, dtype=F32) / ROPE)
    ang = jnp.arange(s, dtype=jnp.int32).astype(F32)[:, None] * inv_freq[None, :]
    cos, sin = jnp.cos(ang), jnp.sin(ang)
    zero = lambda n: jnp.zeros((s, n), F32)
    ck = jnp.concatenate([zero(NOPE), cos, cos, zero(DA - NOPE - ROPE)], axis=1)
    sk = jnp.concatenate([zero(NOPE), -sin, sin, zero(DA - NOPE - ROPE)], axis=1)
    cq = jnp.concatenate([jnp.ones((s, NOPE), F32), cos, cos, zero(DA - NOPE - ROPE)], axis=1) * SCALE_MLA
    return dict(cq=cq, sq=sk * SCALE_MLA, ck=ck, sk=sk)


def _mix_fwd(l, x1, wts, sm, tabs):
    z, h2 = _norm_mm(x1, 0, sm["mix_norm"][l], wts["w_in"][l], name=f"mix_in_{l}")
    qa, qn = _mla_q_prep(z, sm["q_a_norm"][l], wts["wq_a"][l], wts["wq_b"][l], tabs["cq"], tabs["sq"], name=f"mla_q_{l}")
    ka, va, kvn = _mla_kv_prep(z, sm["kv_a_norm"][l], wts["wk"][l], wts["wv"][l], tabs["ck"], tabs["sk"], name=f"mla_kv_{l}")
    oa, lse_a = _attn_fwd(qa, ka, va, VDIM, name=f"mla_attn_{l}")

    bd = _block_diag(wts["pool_w"][l]).astype(BF16)
    yb, pooled = _pool_fwd(z, Z_POOL // POOL_W, bd, sm["pool_scale"][l], name=f"pool_{l}")

    fb = jnp.pad(sm["fox_b_f"][l], (0, 8 - H)).reshape(8, 1)
    ft, c3t = _gate_fwd(z, Z_F // DA, fb, name=f"fox_gate_{l}")
    fqa, fka, fva = _fox_prep(z, c3t, name=f"fox_prep_{l}")
    oc, lse_c = _attn_fwd(fqa, fka, fva, FOX_D, name=f"fox_attn_{l}")

    x2, cat = _mix_out(oa, yb, oc, wts["w_out"][l], x1, name=f"mix_out_{l}")
    saved = dict(z=z, h2=h2, qn=qn, kvn=kvn, qa=qa, ka=ka, va=va, oa=oa, lse_a=lse_a, bd=bd, pooled=pooled,
                 fqa=fqa, fka=fka, fva=fva, ft=ft, fb=fb, oc=oc, lse_c=lse_c, cat=cat)
    return x2, saved


def _mix_bwd(l, x1, dx2, sv, wts, sm, tabs):
    s = x1.shape[0]
    g = {}
    dx2b = dx2.astype(BF16)
    g["w_out"] = _mm(sv["cat"], dx2b, "tn", name=f"d_w_out_{l}", tm=1024, tn=1024)
    doa, doc, dyb, dl_a, dl_c = _mix_out_bwd(dx2b, wts["w_out"][l], sv["oa"], sv["oc"], name=f"mix_out_bwd_{l}")

    dfqa, dfka, dfva, dcq, dck = _attn_bwd(sv["fqa"], sv["fka"], sv["fva"], doc, sv["lse_c"].reshape(H, 1, s),
                                           dl_c.reshape(H, 1, s), True, name=f"fox_attn_bwd_{l}")
    dfox = _fox_bwd_prep(dfqa, dfka, dfva, name=f"fox_bwd_prep_{l}")
    dc = jnp.pad(dcq.reshape(H, s) + dck.reshape(H, s), ((0, 8 - H), (0, 0)))
    dft, dfb = _gate_bwd(sv["ft"], sv["fb"], dc, name=f"fox_gate_bwd_{l}")
    g["fox_b_f"] = dfb[:H, 0]

    dq, dys, g["pool_scale"] = _pool_bwd_a(dyb, sv["pooled"], sv["bd"], sm["pool_scale"][l], name=f"pool_bwd_a_{l}")
    du = _pool_bwd_b(dq, name=f"pool_bwd_b_{l}")
    dbd = _mm(sv["pooled"], dys, "tn", name=f"d_pool_w_{l}")
    g["pool_w"] = jnp.stack([dbd[i * 64:(i + 1) * 64, i * 64:(i + 1) * 64] for i in range(4)])

    dqa_, dka_, dva_ = _attn_bwd(sv["qa"], sv["ka"], sv["va"], doa, sv["lse_a"].reshape(H, 1, s),
                                 dl_a.reshape(H, 1, s), False, name=f"mla_attn_bwd_{l}")
    dqab, dkv, dz3, dz15 = _mla_bwd_prep(dqa_, dka_, dva_, dft, tabs["cq"], tabs["sq"], tabs["ck"], tabs["sk"],
                                         name=f"mla_bwd_prep_{l}")
    wq_ab = jnp.concatenate([wts["wq_a"][l], wts["wq_b"][l]], axis=1)
    wkv = jnp.concatenate([wts["wk"][l], wts["wv"][l]], axis=1)
    dwq = _mm(sv["qn"], dqab, "tn", name=f"d_w_q_b_{l}", tn=768).reshape(Q_RANK, 2, H, DA)
    dwkv = _mm(sv["kvn"], dkv, "tn", name=f"d_w_kv_b_{l}", tn=768).reshape(KV_RANK, 2, H, DA)
    da, db = dwq[:, 0], dwq[:, 1]
    swapped = jnp.concatenate([jnp.zeros((Q_RANK, H, NOPE), F32), db[..., NOPE + HALF_ROPE:NOPE + ROPE],
                               db[..., NOPE:NOPE + HALF_ROPE]], axis=-1)
    g["w_q_b"] = (da[..., :NOPE + ROPE] + swapped).reshape(Q_RANK, H * (NOPE + ROPE))
    g["w_kv_b"] = jnp.concatenate([dwkv[:, 0, :, :NOPE], dwkv[:, 1, :, :VDIM]], axis=-1).reshape(KV_RANK, H * (NOPE + VDIM))
    dqn = _mm(dqab, wq_ab, "nt", name=f"d_qn_{l}", tk=768)
    dkvn = _mm(dkv, wkv, "nt", name=f"d_kvn_{l}", tk=768)
    dqa, g["q_a_norm"] = _rmsnorm_bwd(sv["z"], Z_QA // Q_RANK, sm["q_a_norm"][l], dqn, name=f"q_a_norm_bwd_{l}")
    dkva, g["kv_a_norm"] = _rmsnorm_bwd(sv["z"], Z_KVA // KV_RANK, sm["kv_a_norm"][l], dkvn, name=f"kv_a_norm_bwd_{l}")

    dz = jnp.concatenate([dqa.astype(BF16), dkva.astype(BF16), dz3, du.astype(BF16), dfox, dz15], axis=1)
    g["w_in"] = _mm(sv["h2"], dz, "tn", name=f"d_w_in_{l}", tm=1024, tn=1024)
    dh2 = _mm(dz, wts["w_in"][l], "nt", name=f"d_h2_{l}", tn=1024, tk=1024)
    dx1, g["mix_norm"] = _rmsnorm_bwd(x1, 0, sm["mix_norm"][l], dh2, dx2, name=f"mix_norm_bwd_{l}")
    return dx1, g


def _local_step(x, target, wts, sm):
    s = x.shape[0]
    tabs = _rope_tables(s)
    acts = []
    xs = x
    for l in range(DEPTH):
        x1, gu1 = _ffn_fwd(xs, sm["ffn1_norm"][l], wts["ffn1_w_gu"], wts["ffn1_w_d2"][l], l, name=f"ffn1_fwd_{l}")
        x2, sv = _mix_fwd(l, x1, wts, sm, tabs)
        x3, gu2 = _ffn_fwd(x2, sm["ffn2_norm"][l], wts["ffn2_w_gu"], wts["ffn2_w_d2"][l], l, name=f"ffn2_fwd_{l}")
        acts.append((xs, gu1, x1, sv, x2, gu2))
        xs = x3
    dx, g_final, loss = _loss_head(xs, sm["final_norm"], target, name="loss_head")
    grads = [dict() for _ in range(DEPTH)]
    for l in reversed(range(DEPTH)):
        x0, gu1, x1, sv, x2, gu2 = acts[l]
        g = grads[l]
        dx, dgu, act, hh, dy, g["ffn2_norm"] = _ffn_bwd(x2, dx, gu2, sm["ffn2_norm"][l], wts["ffn2_w_gu"], wts["ffn2_w_d2"][l], l,
                                                        name=f"ffn2_bwd_{l}")
        g["ffn2_w_down"] = _mm(act, dy, "tn", name=f"d_ffn2_w_down_{l}", tm=FF_SHARD, tn=1024)
        g["ffn2_w_gu"] = _mm(hh, dgu, "tn", name=f"d_ffn2_w_gu_{l}", tm=1024, tn=FF_SHARD, n_major_out=True)
        dx, gm = _mix_bwd(l, x1, dx, sv, wts, sm, tabs)
        g.update(gm)
        dx, dgu, act, hh, dy, g["ffn1_norm"] = _ffn_bwd(x0, dx, gu1, sm["ffn1_norm"][l], wts["ffn1_w_gu"], wts["ffn1_w_d2"][l], l,
                                                        name=f"ffn1_bwd_{l}")
        g["ffn1_w_down"] = _mm(act, dy, "tn", name=f"d_ffn1_w_down_{l}", tm=FF_SHARD, tn=1024)
        g["ffn1_w_gu"] = _mm(hh, dgu, "tn", name=f"d_ffn1_w_gu_{l}", tm=1024, tn=FF_SHARD, n_major_out=True)
    return loss, dx, grads, g_final


BIG = ["ffn1_w_gu", "ffn1_w_down", "w_in", "w_q_b", "w_kv_b", "w_out", "ffn2_w_gu", "ffn2_w_down"]
SMALL = ["ffn1_norm", "mix_norm", "q_a_norm", "kv_a_norm", "pool_w", "pool_scale", "fox_b_f", "ffn2_norm"]
SMALL_ROWS = 48


def _prepare_weights(gathered, params):
    wts = {"ffn1_w_gu": gathered["ffn1_w_gu"], "ffn2_w_gu": gathered["ffn2_w_gu"]}
    for nm in ("ffn1", "ffn2"):
        wd = gathered[nm + "_w_down"]
        wts[nm + "_w_d2"] = [wd[:, l].reshape(2, FF_SHARD, D) for l in range(DEPTH)]
    wts["w_in"] = [gathered["w_in"][:, l].reshape(D, NZ) for l in range(DEPTH)]
    wts["w_out"] = [gathered["w_out"][:, l].reshape(D, D) for l in range(DEPTH)]
    wts["wq_a"], wts["wq_b"], wts["wk"], wts["wv"] = [], [], [], []
    for l in range(DEPTH):
        wq = jnp.moveaxis(gathered["w_q_b"][:, l], 0, 1).reshape(Q_RANK, H, NOPE + ROPE)
        zq = lambda n: jnp.zeros((Q_RANK, H, n), BF16)
        wts["wq_a"].append(jnp.concatenate([wq, zq(DA - NOPE - ROPE)], axis=-1).reshape(Q_RANK, H * DA))
        wts["wq_b"].append(jnp.concatenate([zq(NOPE), wq[..., NOPE + HALF_ROPE:], wq[..., NOPE:NOPE + HALF_ROPE],
                                            zq(DA - NOPE - ROPE)], axis=-1).reshape(Q_RANK, H * DA))
        wkv = jnp.moveaxis(gathered["w_kv_b"][:, l], 0, 1).reshape(KV_RANK, H, NOPE + VDIM)
        zk = jnp.zeros((KV_RANK, H, DA - NOPE), BF16)
        wts["wk"].append(jnp.concatenate([wkv[..., :NOPE], zk], axis=-1).reshape(KV_RANK, H * DA))
        wts["wv"].append(jnp.concatenate([wkv[..., NOPE:], zk], axis=-1).reshape(KV_RANK, H * DA))
    wts["pool_w"] = params["pool_w"]
    return wts


def _chip_major(name, g):
    if name in ("ffn1_w_gu", "ffn2_w_gu"):
        return g
    if name in ("ffn1_w_down", "ffn2_w_down", "w_in", "w_out"):
        return g.reshape(N_CHIPS, g.shape[0] // N_CHIPS, g.shape[1])
    return jnp.moveaxis(g.reshape(g.shape[0], N_CHIPS, g.shape[1] // N_CHIPS), 1, 0)


def _pack_small(grads, g_final, loss):
    parts = []
    for l in range(DEPTH):
        for nm in SMALL:
            parts.append(grads[l][nm].reshape(-1))
    parts.append(g_final.reshape(-1))
    parts.append(loss.reshape(1))
    flat = jnp.concatenate(parts)
    return jnp.pad(flat, (0, SMALL_ROWS * D - flat.shape[0])).reshape(SMALL_ROWS, D)


def _unpack_small(packed, params):
    flat = packed.reshape(-1)
    out = {nm: [] for nm in SMALL}
    off = 0
    for l in range(DEPTH):
        for nm in SMALL:
            shp = params[nm].shape[1:]
            n = int(np.prod(shp))
            out[nm].append(flat[off:off + n].reshape(shp))
            off += n
    res = {nm: jnp.stack(v) for nm, v in out.items()}
    res["final_norm"] = flat[off:off + D]
    return res, flat[off + D]


def _update(name, w, g, m, v):
    shp = w.shape
    if w.ndim == 1:
        view = (1, shp[0])
    elif w.size <= 65536:
        view = (shp[0], w.size // shp[0])
    else:
        view = (w.size // shp[-1], shp[-1])
    tr = view[0]
    for cand in (512, 352, 256, 128):
        if view[0] % cand == 0 and view[0] > cand:
            tr = cand
            break
    d, mn, vn = _adamw(w.reshape(view), g.reshape(view), m.reshape(view), v.reshape(view), name="adamw_" + name, tr=tr)
    return d.reshape(shp), mn.reshape(shp), vn.reshape(shp)


WEIGHTS = ['ffn1_norm', 'ffn1_w_gu', 'ffn1_w_down', 'mix_norm', 'w_in', 'q_a_norm', 'w_q_b', 'kv_a_norm', 'w_kv_b', 'pool_w',
           'pool_scale', 'fox_b_f', 'w_out', 'ffn2_norm', 'ffn2_w_gu', 'ffn2_w_down', 'final_norm']


def kernel(x, ffn1_norm, ffn1_w_gu, ffn1_w_down, mix_norm, w_in, q_a_norm, w_q_b, kv_a_norm, w_kv_b, pool_w, pool_scale, fox_b_f, w_out, ffn2_norm, ffn2_w_gu, ffn2_w_down, final_norm, loss_target, m_ffn1_norm, m_ffn1_w_gu, m_ffn1_w_down, m_mix_norm, m_w_in, m_q_a_norm, m_w_q_b, m_kv_a_norm, m_w_kv_b, m_pool_w, m_pool_scale, m_fox_b_f, m_w_out, m_ffn2_norm, m_ffn2_w_gu, m_ffn2_w_down, m_final_norm, v_ffn1_norm, v_ffn1_w_gu, v_ffn1_w_down, v_mix_norm, v_w_in, v_q_a_norm, v_w_q_b, v_kv_a_norm, v_w_kv_b, v_pool_w, v_pool_scale, v_fox_b_f, v_w_out, v_ffn2_norm, v_ffn2_w_gu, v_ffn2_w_down, v_final_norm):
    params = dict(ffn1_norm=ffn1_norm, ffn1_w_gu=ffn1_w_gu, ffn1_w_down=ffn1_w_down, mix_norm=mix_norm, w_in=w_in, q_a_norm=q_a_norm,
                  w_q_b=w_q_b, kv_a_norm=kv_a_norm, w_kv_b=w_kv_b, pool_w=pool_w, pool_scale=pool_scale, fox_b_f=fox_b_f, w_out=w_out,
                  ffn2_norm=ffn2_norm, ffn2_w_gu=ffn2_w_gu, ffn2_w_down=ffn2_w_down, final_norm=final_norm)
    mom = dict(ffn1_norm=m_ffn1_norm, ffn1_w_gu=m_ffn1_w_gu, ffn1_w_down=m_ffn1_w_down, mix_norm=m_mix_norm, w_in=m_w_in,
               q_a_norm=m_q_a_norm, w_q_b=m_w_q_b, kv_a_norm=m_kv_a_norm, w_kv_b=m_w_kv_b, pool_w=m_pool_w, pool_scale=m_pool_scale,
               fox_b_f=m_fox_b_f, w_out=m_w_out, ffn2_norm=m_ffn2_norm, ffn2_w_gu=m_ffn2_w_gu, ffn2_w_down=m_ffn2_w_down,
               final_norm=m_final_norm)
    var = dict(ffn1_norm=v_ffn1_norm, ffn1_w_gu=v_ffn1_w_gu, ffn1_w_down=v_ffn1_w_down, mix_norm=v_mix_norm, w_in=v_w_in,
               q_a_norm=v_q_a_norm, w_q_b=v_w_q_b, kv_a_norm=v_kv_a_norm, w_kv_b=v_w_kv_b, pool_w=v_pool_w, pool_scale=v_pool_scale,
               fox_b_f=v_fox_b_f, w_out=v_w_out, ffn2_norm=v_ffn2_norm, ffn2_w_gu=v_ffn2_w_gu, ffn2_w_down=v_ffn2_w_down,
               final_norm=v_final_norm)

    shards = []
    for nm in BIG:
        w = params[nm]
        if nm == "w_in":
            w = _pad_w_in(w)
        shards.append(w.astype(BF16))
    gathered = dict(zip(BIG, _gather_weights(shards)))
    wts = _prepare_weights(gathered, params)

    loss, dx, grads, g_final = _local_step(x[0], loss_target[0], wts, params)

    full = [_chip_major(nm, grads[l][nm]) for nm in BIG for l in range(DEPTH)]
    pos = _position()
    sib = _reduce_stage1(full)
    psum = [_sum2_bf16(pos, f, sb, name=f"chip_sum_{t}") for t, (f, sb) in enumerate(zip(full, sib))]
    recv = _reduce_stage2(psum)
    reduced = [_sum5(pos, f, sb, r, name=f"grad_sum_{t}") for t, (f, sb, r) in enumerate(zip(full, sib, recv))]
    whole = _reduce_stage3(reduced)
    big_g = {nm: jnp.stack(whole[DEPTH * k:DEPTH * (k + 1)]) for k, nm in enumerate(BIG)}
    big_g["w_in"] = _unpad_w_in(big_g["w_in"])
    small_g, loss = _unpack_small(_allreduce_small(_pack_small(grads, g_final, loss)), params)
    gw = {**big_g, **small_g}

    delta, new_m, new_v = {}, {}, {}
    for nm in WEIGHTS:
        delta[nm], new_m[nm], new_v[nm] = _update(nm, params[nm], gw[nm], mom[nm], var[nm])
    return (loss, dx[None], *[gw[n] for n in WEIGHTS], *[delta[n] for n in WEIGHTS], *[new_m[n] for n in WEIGHTS],
            *[new_v[n] for n in WEIGHTS])
```

```python
import functools
import math

import jax
import jax.numpy as jnp
import numpy as np
from jax import lax
from jax.experimental import pallas as pl
from jax.experimental.pallas import tpu as pltpu

F32 = jnp.float32
BF16 = jnp.bfloat16
MESH = pl.DeviceIdType.MESH
HBM_SPEC = pl.BlockSpec(memory_space=pltpu.HBM)

D = 1024
DEPTH = 2
D_FF = 2816
FF_SHARD = 1408
N_CHIPS = 4
H = 6
NOPE, ROPE, VDIM = 64, 32, 64
HALF_ROPE = ROPE // 2
Q_RANK, KV_RANK = 256, 128
POOL_W = 256
FOX_D = 64
N_IN = 1830
NZ = 2048
ROPE_THETA = 10000.0
EPS = 1e-6
POOL_HALO = 16
Z_QA, Z_KVA, Z_KR, Z_POOL, Z_FOX, Z_F = 0, 256, 384, 512, 768, 1920

ADAM_LR, ADAM_B1, ADAM_B2, ADAM_EPS, ADAM_WD, ADAM_STEP = 0.001, 0.9, 0.999, 1e-08, 0.01, 10

VMEM_LIMIT_V7X = 56 * 1024 * 1024


def _cp(sem=None, vmem=VMEM_LIMIT_V7X):
    return pltpu.CompilerParams(dimension_semantics=sem, vmem_limit_bytes=vmem)


def _sigmoid(x):
    return 1.0 / (1.0 + jnp.exp(-x))


def _dot(a, b, dims):
    return lax.dot_general(a, b, (dims, ((), ())), preferred_element_type=F32)


NN = ((1,), (0,))
NT = ((1,), (1,))
TN = ((0,), (0,))


def _mm(a, b, mode, *, name, out_dtype=F32, add=None, alpha=None, tm=512, tn=512, tk=512, n_major_out=False):
    if mode == "nn":
        (m, k), (k2, n) = a.shape, b.shape
    elif mode == "nt":
        (m, k), (n, k2) = a.shape, b.shape
    else:
        (k, m), (k2, n) = a.shape, b.shape
    assert k == k2
    tm, tn, tk = min(tm, m), min(tn, n), min(tk, k)
    assert m % tm == 0 and n % tn == 0 and k % tk == 0, (name, m, n, k, tm, tn, tk)
    nk = k // tk
    dims = {"nn": NN, "nt": NT, "tn": TN}[mode]
    a_spec = pl.BlockSpec((tk, tm), lambda i, j, kk: (kk, i)) if mode == "tn" else pl.BlockSpec((tm, tk), lambda i, j, kk: (i, kk))
    b_spec = pl.BlockSpec((tn, tk), lambda i, j, kk: (j, kk)) if mode == "nt" else pl.BlockSpec((tk, tn), lambda i, j, kk: (kk, j))
    in_specs = [a_spec, b_spec]
    args = [a, b]
    if add is not None:
        in_specs.append(pl.BlockSpec((tm, tn), lambda i, j, kk: (i, j)))
        args.append(add)
    if n_major_out:
        out_shape = jax.ShapeDtypeStruct((n // tn, m, tn), out_dtype)
        out_spec = pl.BlockSpec((None, tm, tn), lambda i, j, kk: (j, i, 0))
    else:
        out_shape = jax.ShapeDtypeStruct((m, n), out_dtype)
        out_spec = pl.BlockSpec((tm, tn), lambda i, j, kk: (i, j))

    def body(*refs):
        a_ref, b_ref = refs[0], refs[1]
        add_ref = refs[2] if add is not None else None
        o_ref, acc = refs[-2], refs[-1]
        kk = pl.program_id(2)

        @pl.when(kk == 0)
        def _():
            acc[...] = jnp.zeros_like(acc)

        acc[...] += _dot(a_ref[...].astype(BF16), b_ref[...].astype(BF16), dims)

        @pl.when(kk == nk - 1)
        def _():
            r = acc[...]
            if alpha is not None:
                r = r * alpha
            if add_ref is not None:
                r = r + add_ref[...].astype(F32)
            o_ref[...] = r.astype(out_dtype)

    return pl.pallas_call(
        body, name=name, grid=(m // tm, n // tn, nk), in_specs=in_specs, out_specs=out_spec, out_shape=out_shape,
        scratch_shapes=[pltpu.VMEM((tm, tn), F32)],
        compiler_params=_cp(("parallel", "parallel", "arbitrary")),
    )(*args)


def _norm_mm(x, col_block, gain, w, *, name, tm=512):
    s = x.shape[0]
    k, n = w.shape
    tm = min(tm, s)

    def body(x_ref, g_ref, w_ref, z_ref, h_ref):
        xv = x_ref[...]
        r = lax.rsqrt(jnp.mean(xv * xv, axis=-1, keepdims=True) + EPS)
        hv = (xv * r * g_ref[...]).astype(BF16)
        h_ref[...] = hv
        z_ref[...] = _dot(hv, w_ref[...], NN)

    return pl.pallas_call(
        body, name=name, grid=(s // tm,),
        in_specs=[pl.BlockSpec((tm, k), lambda i: (i, col_block)), pl.BlockSpec((1, k), lambda i: (0, 0)),
                  pl.BlockSpec((k, n), lambda i: (0, 0))],
        out_specs=[pl.BlockSpec((tm, n), lambda i: (i, 0)), pl.BlockSpec((tm, k), lambda i: (i, 0))],
        out_shape=[jax.ShapeDtypeStruct((s, n), F32), jax.ShapeDtypeStruct((s, k), BF16)],
        compiler_params=_cp(("parallel",)),
    )(x, gain.reshape(1, k), w)


def _rmsnorm_bwd(x, col_block, gain, dh, dres=None, *, name, tm=512):
    s = x.shape[0]
    k = gain.shape[-1]
    tm = min(tm, s)

    def body(*refs):
        x_ref, g_ref, dh_ref = refs[0], refs[1], refs[2]
        dres_ref = refs[3] if dres is not None else None
        dx_ref, dg_ref = refs[-2], refs[-1]
        xv = x_ref[...]
        r = lax.rsqrt(jnp.mean(xv * xv, axis=-1, keepdims=True) + EPS)
        dhv = dh_ref[...].astype(F32)
        a = dhv * g_ref[...]
        dx = r * a - xv * (r * r * r) * jnp.mean(a * xv, axis=-1, keepdims=True)
        if dres_ref is not None:
            dx = dx + dres_ref[...]
        dx_ref[...] = dx

        @pl.when(pl.program_id(0) == 0)
        def _():
            dg_ref[...] = jnp.zeros_like(dg_ref)

        dg_ref[...] += jnp.sum(dhv * xv * r, axis=0, keepdims=True)

    in_specs = [pl.BlockSpec((tm, k), lambda i: (i, col_block)), pl.BlockSpec((1, k), lambda i: (0, 0)),
                pl.BlockSpec((tm, k), lambda i: (i, 0))]
    args = [x, gain.reshape(1, k), dh]
    if dres is not None:
        in_specs.append(pl.BlockSpec((tm, k), lambda i: (i, 0)))
        args.append(dres)
    dx, dg = pl.pallas_call(
        body, name=name, grid=(s // tm,), in_specs=in_specs,
        out_specs=[pl.BlockSpec((tm, k), lambda i: (i, 0)), pl.BlockSpec((1, k), lambda i: (0, 0))],
        out_shape=[jax.ShapeDtypeStruct((s, k), F32), jax.ShapeDtypeStruct((1, k), F32)],
        compiler_params=_cp(("arbitrary",)),
    )(*args)
    return dx, dg.reshape(k)


def _ffn_fwd(x, gain, w_gu4, w_d2, *, name, tm=256):
    s = x.shape[0]
    tm = min(tm, s)

    def body(x_ref, g_ref, wgu_ref, wd_ref, xo_ref, gu_ref):
        xv = x_ref[...]
        r = lax.rsqrt(jnp.mean(xv * xv, axis=-1, keepdims=True) + EPS)
        hv = (xv * r * g_ref[...]).astype(BF16)
        y = jnp.zeros((tm, D), F32)
        for j in range(2):
            g = _dot(hv, wgu_ref[j], NN)
            u = _dot(hv, wgu_ref[2 + j], NN)
            gu_ref[:, j * FF_SHARD:(j + 1) * FF_SHARD] = g.astype(BF16)
            gu_ref[:, D_FF + j * FF_SHARD:D_FF + (j + 1) * FF_SHARD] = u.astype(BF16)
            act = (g * _sigmoid(g) * u).astype(BF16)
            y = y + _dot(act, wd_ref[j], NN)
        xo_ref[...] = xv + 0.5 * y

    return pl.pallas_call(
        body, name=name, grid=(s // tm,),
        in_specs=[pl.BlockSpec((tm, D), lambda i: (i, 0)), pl.BlockSpec((1, D), lambda i: (0, 0)),
                  pl.BlockSpec((N_CHIPS, D, FF_SHARD), lambda i: (0, 0, 0), pipeline_mode=pl.Buffered(1)),
                  pl.BlockSpec((2, FF_SHARD, D), lambda i: (0, 0, 0), pipeline_mode=pl.Buffered(1))],
        out_specs=[pl.BlockSpec((tm, D), lambda i: (i, 0)), pl.BlockSpec((tm, 2 * D_FF), lambda i: (i, 0))],
        out_shape=[jax.ShapeDtypeStruct((s, D), F32), jax.ShapeDtypeStruct((s, 2 * D_FF), BF16)],
        compiler_params=_cp(("parallel",)),
    )(x, gain.reshape(1, D), w_gu4, w_d2)


def _ffn_bwd(x, dxo, gu, gain, w_gu4, w_d2, *, name, tm=256):
    s = x.shape[0]
    tm = min(tm, s)

    def body(x_ref, dxo_ref, gu_ref, g_ref, wgu_ref, wd_ref, dx_ref, dgu_ref, act_ref, h_ref, dy_ref, dg_ref):
        xv = x_ref[...]
        r = lax.rsqrt(jnp.mean(xv * xv, axis=-1, keepdims=True) + EPS)
        xh = xv * r
        h_ref[...] = (xh * g_ref[...]).astype(BF16)
        dxov = dxo_ref[...]
        dy = (0.5 * dxov).astype(BF16)
        dy_ref[...] = dy
        dh = jnp.zeros((tm, D), F32)
        for j in range(2):
            g = gu_ref[:, j * FF_SHARD:(j + 1) * FF_SHARD].astype(F32)
            u = gu_ref[:, D_FF + j * FF_SHARD:D_FF + (j + 1) * FF_SHARD].astype(F32)
            sg = _sigmoid(g)
            silu = g * sg
            act_ref[:, j * FF_SHARD:(j + 1) * FF_SHARD] = (silu * u).astype(BF16)
            dact = _dot(dy, wd_ref[j], NT)
            dg = (dact * u * (sg * (1.0 + g * (1.0 - sg)))).astype(BF16)
            du = (dact * silu).astype(BF16)
            dgu_ref[:, j * FF_SHARD:(j + 1) * FF_SHARD] = dg
            dgu_ref[:, D_FF + j * FF_SHARD:D_FF + (j + 1) * FF_SHARD] = du
            dh = dh + _dot(dg, wgu_ref[j], NT) + _dot(du, wgu_ref[2 + j], NT)
        a = dh * g_ref[...]
        dx_ref[...] = dxov + r * a - xh * (r * jnp.mean(a * xh, axis=-1, keepdims=True))

        @pl.when(pl.program_id(0) == 0)
        def _():
            dg_ref[...] = jnp.zeros_like(dg_ref)

        dg_ref[...] += jnp.sum(dh * xh, axis=0, keepdims=True)

    row = lambda i: (i, 0)
    outs = pl.pallas_call(
        body, name=name, grid=(s // tm,),
        in_specs=[pl.BlockSpec((tm, D), row), pl.BlockSpec((tm, D), row), pl.BlockSpec((tm, 2 * D_FF), row),
                  pl.BlockSpec((1, D), lambda i: (0, 0)),
                  pl.BlockSpec((N_CHIPS, D, FF_SHARD), lambda i: (0, 0, 0), pipeline_mode=pl.Buffered(1)),
                  pl.BlockSpec((2, FF_SHARD, D), lambda i: (0, 0, 0), pipeline_mode=pl.Buffered(1))],
        out_specs=[pl.BlockSpec((tm, D), row), pl.BlockSpec((tm, 2 * D_FF), row), pl.BlockSpec((tm, D_FF), row),
                   pl.BlockSpec((tm, D), row), pl.BlockSpec((tm, D), row), pl.BlockSpec((1, D), lambda i: (0, 0))],
        out_shape=[jax.ShapeDtypeStruct((s, D), F32), jax.ShapeDtypeStruct((s, 2 * D_FF), BF16),
                   jax.ShapeDtypeStruct((s, D_FF), BF16), jax.ShapeDtypeStruct((s, D), BF16),
                   jax.ShapeDtypeStruct((s, D), BF16), jax.ShapeDtypeStruct((1, D), F32)],
        compiler_params=_cp(("arbitrary",)),
    )(x, dxo, gu, gain.reshape(1, D), w_gu4, w_d2)
    dx, dgu, act, h, dy, dg = outs
    return dx, dgu, act, h, dy, dg.reshape(D)


DA = 128
SCALE_MLA = 1.0 / math.sqrt(NOPE + ROPE)
SCALE_FOX = 1.0 / math.sqrt(FOX_D)


def _causal_blocks(nb, key_major):
    if key_major:
        pairs = [(i, j) for j in range(nb) for i in range(j, nb)]
    else:
        pairs = [(i, j) for i in range(nb) for j in range(i + 1)]
    return (jnp.asarray(np.array([p[0] for p in pairs], np.int32)), jnp.asarray(np.array([p[1] for p in pairs], np.int32)))


HEADS_PER_STEP = 2


def _col_to_row(col):
    return jnp.broadcast_to(col, (col.shape[0], DA)).T[0:1, :]


def _attn_fwd(qa, ka, va, dv, *, name, t=512):
    h, s, _ = qa.shape
    t = min(t, s)
    nb = s // t
    g = HEADS_PER_STEP
    qi, kj = _causal_blocks(nb, key_major=False)

    def body(qi_ref, kj_ref, q_ref, k_ref, v_ref, o_ref, lse_ref, m_sc, acc_sc):
        n = pl.program_id(1)
        i, j = qi_ref[n], kj_ref[n]

        @pl.when(j == 0)
        def _():
            m_sc[...] = jnp.full_like(m_sc, -jnp.inf)
            acc_sc[...] = jnp.zeros_like(acc_sc)

        def step(masked):
            for hh in range(g):
                sc = _dot(q_ref[hh], k_ref[hh], NT)
                if masked:
                    row = lax.broadcasted_iota(jnp.int32, (t, t), 0)
                    col = lax.broadcasted_iota(jnp.int32, (t, t), 1)
                    sc = jnp.where(col <= row, sc, -jnp.inf)
                m_old = m_sc[hh]
                m_new = jnp.maximum(m_old, jnp.max(sc, axis=-1, keepdims=True))
                p = jnp.exp(sc - m_new)
                acc_sc[hh] = jnp.exp(m_old - m_new) * acc_sc[hh] + _dot(p.astype(BF16), v_ref[hh], NN)
                m_sc[hh] = m_new

        @pl.when(j < i)
        def _():
            step(False)

        @pl.when(j == i)
        def _():
            step(True)
            for hh in range(g):
                acc = acc_sc[hh]
                l = acc[:, dv:dv + 1]
                o_ref[hh] = acc[:, :dv] / l
                lse_ref[hh] = _col_to_row(m_sc[hh] + jnp.log(l))

    qmap = lambda hg, n, qi_r, kj_r: (hg, qi_r[n], 0)
    kmap = lambda hg, n, qi_r, kj_r: (hg, kj_r[n], 0)
    return pl.pallas_call(
        body, name=name,
        grid_spec=pltpu.PrefetchScalarGridSpec(
            num_scalar_prefetch=2, grid=(h // g, qi.shape[0]),
            in_specs=[pl.BlockSpec((g, t, DA), qmap), pl.BlockSpec((g, t, DA), kmap), pl.BlockSpec((g, t, DA), kmap)],
            out_specs=[pl.BlockSpec((g, t, dv), qmap), pl.BlockSpec((g, 1, t), lambda hg, n, qi_r, kj_r: (hg, 0, qi_r[n]))],
            scratch_shapes=[pltpu.VMEM((g, t, 1), F32), pltpu.VMEM((g, t, DA), F32)]),
        out_shape=[jax.ShapeDtypeStruct((h, s, dv), F32), jax.ShapeDtypeStruct((h, 1, s), F32)],
        compiler_params=_cp(("parallel", "arbitrary")),
    )(qi, kj, qa, ka, va)


def _attn_bwd(qa, ka, va, doa, lse_row, delta_row, decay, *, name, t=512):
    h, s, _ = qa.shape
    t = min(t, s)
    nb = s // t
    g = HEADS_PER_STEP
    qi, kj = _causal_blocks(nb, key_major=True)
    nsteps = qi.shape[0]

    def body(*refs):
        qi_ref, kj_ref, q_ref, k_ref, v_ref, do_ref, lse_ref, dl_ref = refs[:8]
        if decay:
            dq_ref, dk_ref, dv_ref, dcq_ref, dck_ref, dq_acc, dk_acc, dv_acc, dcq_acc, dck_acc = refs[8:]
        else:
            dq_ref, dk_ref, dv_ref, dq_acc, dk_acc, dv_acc = refs[8:]
        n = pl.program_id(1)
        i, j = qi_ref[n], kj_ref[n]

        @pl.when(n == 0)
        def _():
            dq_acc[...] = jnp.zeros_like(dq_acc)
            if decay:
                dcq_acc[...] = jnp.zeros_like(dcq_acc)

        @pl.when(i == j)
        def _():
            dk_acc[...] = jnp.zeros_like(dk_acc)
            dv_acc[...] = jnp.zeros_like(dv_acc)
            if decay:
                dck_acc[...] = jnp.zeros_like(dck_acc)

        def step(masked):
            for hh in range(g):
                st = _dot(k_ref[hh], q_ref[hh], NT)
                if masked:
                    row = lax.broadcasted_iota(jnp.int32, (t, t), 0)
                    col = lax.broadcasted_iota(jnp.int32, (t, t), 1)
                    st = jnp.where(row <= col, st, -jnp.inf)
                pt = jnp.exp(st - lse_ref[hh])
                dob = do_ref[hh]
                dpt = _dot(v_ref[hh], dob, NT)
                dst = pt * (dpt - dl_ref[hh])
                dsb = dst.astype(BF16)
                dv_acc[hh] += _dot(pt.astype(BF16), dob, NN)
                dk_acc[hh] += _dot(dsb, q_ref[hh], NN)
                dq_acc[hh, i] += _dot(dsb, k_ref[hh], TN)
                if decay:
                    dcq_acc[hh, i] += jnp.sum(dst, axis=0, keepdims=True)
                    dck_acc[hh] -= jnp.sum(dst, axis=1, keepdims=True)

        @pl.when(i > j)
        def _():
            step(False)

        @pl.when(i == j)
        def _():
            step(True)

        @pl.when(i == nb - 1)
        def _():
            dk_ref[...] = dk_acc[...]
            dv_ref[...] = dv_acc[...]
            if decay:
                for hh in range(g):
                    dck_ref[hh] = _col_to_row(dck_acc[hh])

        @pl.when(n == nsteps - 1)
        def _():
            dq_ref[...] = dq_acc[...]
            if decay:
                dcq_ref[...] = dcq_acc[...]

    kmap = lambda hg, n, qi_r, kj_r: (hg, kj_r[n], 0)
    qmap = lambda hg, n, qi_r, kj_r: (hg, qi_r[n], 0)
    qrow = lambda hg, n, qi_r, kj_r: (hg, 0, qi_r[n])
    krow = lambda hg, n, qi_r, kj_r: (hg, 0, kj_r[n])
    whole = lambda hg, n, qi_r, kj_r: (hg, 0, 0, 0)
    in_specs = [pl.BlockSpec((g, t, DA), qmap), pl.BlockSpec((g, t, DA), kmap), pl.BlockSpec((g, t, DA), kmap),
                pl.BlockSpec((g, t, DA), qmap), pl.BlockSpec((g, 1, t), qrow), pl.BlockSpec((g, 1, t), qrow)]
    out_specs = [pl.BlockSpec((g, nb, t, DA), whole), pl.BlockSpec((g, t, DA), kmap), pl.BlockSpec((g, t, DA), kmap)]
    out_shape = [jax.ShapeDtypeStruct((h, nb, t, DA), F32), jax.ShapeDtypeStruct((h, s, DA), F32), jax.ShapeDtypeStruct((h, s, DA), F32)]
    scratch = [pltpu.VMEM((g, nb, t, DA), F32), pltpu.VMEM((g, t, DA), F32), pltpu.VMEM((g, t, DA), F32)]
    if decay:
        out_specs += [pl.BlockSpec((g, nb, 1, t), whole), pl.BlockSpec((g, 1, t), krow)]
        out_shape += [jax.ShapeDtypeStruct((h, nb, 1, t), F32), jax.ShapeDtypeStruct((h, 1, s), F32)]
        scratch += [pltpu.VMEM((g, nb, 1, t), F32), pltpu.VMEM((g, t, 1), F32)]
    outs = pl.pallas_call(
        body, name=name,
        grid_spec=pltpu.PrefetchScalarGridSpec(num_scalar_prefetch=2, grid=(h // g, nsteps), in_specs=in_specs, out_specs=out_specs,
                                               scratch_shapes=scratch),
        out_shape=out_shape, compiler_params=_cp(("parallel", "arbitrary")),
    )(qi, kj, qa, ka, va, doa, lse_row, delta_row)
    outs = list(outs)
    outs[0] = outs[0].reshape(h, s, DA)
    if decay:
        outs[3] = outs[3].reshape(h, 1, s)
    return outs


def _sel(rows, cols, pairs, value=1.0):
    m = np.zeros((rows, cols), np.float32)
    for r, c in pairs:
        m[r, c] = value
    return jnp.asarray(m, BF16)


def _lane_row(lanes):
    m = np.zeros((1, DA), np.float32)
    m[0, list(lanes)] = 1.0
    return jnp.asarray(m)


def _rms(xv, gain):
    r = lax.rsqrt(jnp.mean(xv * xv, axis=-1, keepdims=True) + EPS)
    return xv * r * gain


def _mla_q_prep(z, gain, wq_a, wq_b, cq, sq, *, name, tm=512):
    s = z.shape[0]
    tm = min(tm, s)

    def body(z_ref, g_ref, wa_ref, wb_ref, c_ref, s_ref, qa_ref, qn_ref):
        qn = _rms(z_ref[...], g_ref[...]).astype(BF16)
        qn_ref[...] = qn
        c, sn = c_ref[...], s_ref[...]
        for hh in range(H):
            cols = slice(hh * DA, (hh + 1) * DA)
            qa_ref[hh] = (_dot(qn, wa_ref[:, cols], NN) * c + _dot(qn, wb_ref[:, cols], NN) * sn).astype(BF16)

    row = lambda i: (i, 0)
    fixed = lambda i: (0, 0)
    return pl.pallas_call(
        body, name=name, grid=(s // tm,),
        in_specs=[pl.BlockSpec((tm, Q_RANK), lambda i: (i, Z_QA // Q_RANK)), pl.BlockSpec((1, Q_RANK), fixed),
                  pl.BlockSpec((Q_RANK, H * DA), fixed), pl.BlockSpec((Q_RANK, H * DA), fixed),
                  pl.BlockSpec((tm, DA), row), pl.BlockSpec((tm, DA), row)],
        out_specs=[pl.BlockSpec((H, tm, DA), lambda i: (0, i, 0)), pl.BlockSpec((tm, Q_RANK), row)],
        out_shape=[jax.ShapeDtypeStruct((H, s, DA), BF16), jax.ShapeDtypeStruct((s, Q_RANK), BF16)],
        compiler_params=_cp(("parallel",)),
    )(z, gain.reshape(1, Q_RANK), wq_a, wq_b, cq, sq)


def _mla_kv_prep(z, gain, wk, wv, ck, sk, *, name, tm=512):
    s = z.shape[0]
    tm = min(tm, s)
    one = _lane_row([VDIM])

    def body(zkv_ref, z3_ref, z15_ref, g_ref, wk_ref, wv_ref, c_ref, s_ref, one_ref, ka_ref, va_ref, kvn_ref):
        kvn = _rms(zkv_ref[...], g_ref[...]).astype(BF16)
        kvn_ref[...] = kvn
        kpe = z3_ref[...] * c_ref[...] + z15_ref[...] * s_ref[...]
        for hh in range(H):
            cols = slice(hh * DA, (hh + 1) * DA)
            ka_ref[hh] = (_dot(kvn, wk_ref[:, cols], NN) + kpe).astype(BF16)
            va_ref[hh] = (_dot(kvn, wv_ref[:, cols], NN) + one_ref[...]).astype(BF16)

    row = lambda i: (i, 0)
    fixed = lambda i: (0, 0)
    blk = lambda c: pl.BlockSpec((tm, DA), lambda i: (i, c))
    heads = pl.BlockSpec((H, tm, DA), lambda i: (0, i, 0))
    return pl.pallas_call(
        body, name=name, grid=(s // tm,),
        in_specs=[blk(Z_KVA // DA), blk(Z_KR // DA), blk(Z_F // DA), pl.BlockSpec((1, KV_RANK), fixed),
                  pl.BlockSpec((KV_RANK, H * DA), fixed), pl.BlockSpec((KV_RANK, H * DA), fixed),
                  pl.BlockSpec((tm, DA), row), pl.BlockSpec((tm, DA), row), pl.BlockSpec((1, DA), fixed)],
        out_specs=[heads, heads, pl.BlockSpec((tm, KV_RANK), row)],
        out_shape=[jax.ShapeDtypeStruct((H, s, DA), BF16), jax.ShapeDtypeStruct((H, s, DA), BF16),
                   jax.ShapeDtypeStruct((s, KV_RANK), BF16)],
        compiler_params=_cp(("parallel",)),
    )(z, z, z, gain.reshape(1, KV_RANK), wk, wv, ck, sk, one)


DEC_C = (FOX_D, FOX_D + 1, FOX_D + 2)
DEC_1 = (FOX_D + 3, FOX_D + 4, FOX_D + 5)


def _fox_prep(z, c3t, *, name, tm=512):
    s = z.shape[0]
    tm = min(tm, s)
    w = H * FOX_D
    left = [(r, r) for r in range(FOX_D)]
    right = [(FOX_D + r, r) for r in range(FOX_D)]
    pq = jnp.stack([_sel(DA, DA, left, SCALE_FOX), _sel(DA, DA, right, SCALE_FOX)])
    pk = jnp.stack([_sel(DA, DA, left), _sel(DA, DA, right)])
    pcq = jnp.stack([_sel(32, DA, [(hh + 8 * k, DEC_C[k]) for k in range(3)]) for hh in range(H)])
    pck = jnp.stack([_sel(32, DA, [(hh + 8 * k, DEC_1[k]) for k in range(3)], -1.0) for hh in range(H)])
    rows3 = jnp.concatenate([_lane_row(DEC_1), _lane_row(DEC_C), _lane_row([FOX_D])], axis=0)

    def body(zq_ref, zk_ref, zv_ref, c_ref, pq_ref, pk_ref, pcq_ref, pck_ref, r_ref, qa_ref, ka_ref, va_ref):
        c3 = c_ref[...]
        for pair in range(H // 2):
            lanes = slice(pair * DA, (pair + 1) * DA)
            zq, zk, zv = zq_ref[:, lanes].astype(BF16), zk_ref[:, lanes].astype(BF16), zv_ref[:, lanes].astype(BF16)
            for side in range(2):
                hh = 2 * pair + side
                qa_ref[hh] = (_dot(zq, pq_ref[side], NN) + _dot(c3, pcq_ref[hh], TN) + r_ref[0:1, :]).astype(BF16)
                ka_ref[hh] = (_dot(zk, pk_ref[side], NN) + _dot(c3, pck_ref[hh], TN) + r_ref[1:2, :]).astype(BF16)
                va_ref[hh] = (_dot(zv, pk_ref[side], NN) + r_ref[2:3, :]).astype(BF16)

    fixed2 = lambda i: (0, 0)
    fixed3 = lambda i: (0, 0, 0)
    heads = pl.BlockSpec((H, tm, DA), lambda i: (0, i, 0))
    zblk = lambda c: pl.BlockSpec((tm, w), lambda i: (i, c))
    return pl.pallas_call(
        body, name=name, grid=(s // tm,),
        in_specs=[zblk(Z_FOX // w), zblk(Z_FOX // w + 1), zblk(Z_FOX // w + 2), pl.BlockSpec((32, tm), lambda i: (0, i)),
                  pl.BlockSpec((2, DA, DA), fixed3), pl.BlockSpec((2, DA, DA), fixed3),
                  pl.BlockSpec((H, 32, DA), fixed3), pl.BlockSpec((H, 32, DA), fixed3), pl.BlockSpec((3, DA), fixed2)],
        out_specs=[heads, heads, heads], out_shape=[jax.ShapeDtypeStruct((H, s, DA), BF16)] * 3,
        compiler_params=_cp(("parallel",)),
    )(z, z, z, c3t, pq, pk, pcq, pck, rows3)


def _mix_out(oa, yb, oc, w_out, x1, *, name, tm=512):
    s = yb.shape[0]
    tm = min(tm, s)
    e2 = jnp.stack([_sel(VDIM, DA, [(r, r) for r in range(VDIM)]), _sel(VDIM, DA, [(r, VDIM + r) for r in range(VDIM)])])

    def body(oa_ref, yb_ref, oc_ref, e_ref, w_ref, x_ref, x2_ref, cat_ref):
        def pairs(o_ref):
            return [(_dot(o_ref[2 * p].astype(BF16), e_ref[0], NN) + _dot(o_ref[2 * p + 1].astype(BF16), e_ref[1], NN)).astype(BF16)
                    for p in range(H // 2)]

        cat = jnp.concatenate(pairs(oa_ref) + [yb_ref[...].astype(BF16)] + pairs(oc_ref), axis=1)
        cat_ref[...] = cat
        x2_ref[...] = x_ref[...] + _dot(cat, w_ref[...], NN)

    row = lambda i: (i, 0)
    heads = pl.BlockSpec((H, tm, VDIM), lambda i: (0, i, 0))
    return pl.pallas_call(
        body, name=name, grid=(s // tm,),
        in_specs=[heads, pl.BlockSpec((tm, POOL_W), row), heads, pl.BlockSpec((2, VDIM, DA), lambda i: (0, 0, 0)),
                  pl.BlockSpec((D, D), lambda i: (0, 0)), pl.BlockSpec((tm, D), row)],
        out_specs=[pl.BlockSpec((tm, D), row), pl.BlockSpec((tm, D), row)],
        out_shape=[jax.ShapeDtypeStruct((s, D), F32), jax.ShapeDtypeStruct((s, D), BF16)],
        compiler_params=_cp(("parallel",)),
    )(oa, yb, oc, e2, w_out, x1)


def _mix_out_bwd(dx2b, w_out, oa, oc, *, name, tm=512):
    s = dx2b.shape[0]
    tm = min(tm, s)
    f2 = jnp.stack([_sel(DA, DA, [(r, r) for r in range(VDIM)]), _sel(DA, DA, [(VDIM + r, r) for r in range(VDIM)])])
    nv = H * VDIM

    def body(dx_ref, w_ref, oa_ref, oc_ref, f_ref, doa_ref, doc_ref, dyb_ref, dla_ref, dlc_ref):
        dcat = _dot(dx_ref[...], w_ref[...], NT)
        dyb_ref[...] = dcat[:, nv:nv + POOL_W]
        for base, o_ref, do_ref, dl_ref in ((0, oa_ref, doa_ref, dla_ref), (nv + POOL_W, oc_ref, doc_ref, dlc_ref)):
            for p in range(H // 2):
                blk = dcat[:, base + p * DA:base + (p + 1) * DA].astype(BF16)
                for side in range(2):
                    hh = 2 * p + side
                    do = _dot(blk, f_ref[side], NN)
                    do_ref[hh] = do.astype(BF16)
                    dl_ref[hh] = _col_to_row(jnp.sum(do[:, :VDIM] * o_ref[hh], axis=-1, keepdims=True))

    row = lambda i: (i, 0)
    heads = lambda w: pl.BlockSpec((H, tm, w), lambda i: (0, i, 0))
    return pl.pallas_call(
        body, name=name, grid=(s // tm,),
        in_specs=[pl.BlockSpec((tm, D), row), pl.BlockSpec((D, D), lambda i: (0, 0)), heads(VDIM), heads(VDIM),
                  pl.BlockSpec((2, DA, DA), lambda i: (0, 0, 0))],
        out_specs=[heads(DA), heads(DA), pl.BlockSpec((tm, POOL_W), row),
                   pl.BlockSpec((H, 1, tm), lambda i: (0, 0, i)), pl.BlockSpec((H, 1, tm), lambda i: (0, 0, i))],
        out_shape=[jax.ShapeDtypeStruct((H, s, DA), BF16), jax.ShapeDtypeStruct((H, s, DA), BF16),
                   jax.ShapeDtypeStruct((s, POOL_W), F32), jax.ShapeDtypeStruct((H, 1, s), F32), jax.ShapeDtypeStruct((H, 1, s), F32)],
        compiler_params=_cp(("parallel",)),
    )(dx2b, w_out, oa, oc, f2)


def _mla_bwd_prep(dqa, dka, dva, dft, cq, sq, ck, sk, *, name, tm=512):
    s = dqa.shape[1]
    tm = min(tm, s)
    keep = _lane_row(range(NOPE))

    def body(dq_ref, dk_ref, dv_ref, dft_ref, cq_ref, sq_ref, ck_ref, sk_ref, keep_ref, dqab_ref, dkv_ref, dz3_ref, dz15_ref):
        cqv, sqv = cq_ref[...], sq_ref[...]
        dkpe = jnp.zeros((tm, DA), F32)
        for hh in range(H):
            lanes = slice(hh * DA, (hh + 1) * DA)
            dq = dq_ref[hh]
            dqab_ref[:, lanes] = (dq * cqv).astype(BF16)
            dqab_ref[:, H * DA + hh * DA:H * DA + (hh + 1) * DA] = (dq * sqv).astype(BF16)
            dk = dk_ref[hh]
            dkpe = dkpe + dk
            dkv_ref[:, lanes] = (dk * keep_ref[...]).astype(BF16)
            dkv_ref[:, H * DA + hh * DA:H * DA + (hh + 1) * DA] = (dv_ref[hh] * keep_ref[...]).astype(BF16)
        dz3_ref[...] = (dkpe * ck_ref[...]).astype(BF16)
        dz15_ref[...] = (dkpe * sk_ref[...] + dft_ref[...]).astype(BF16)

    row = lambda i: (i, 0)
    heads = pl.BlockSpec((H, tm, DA), lambda i: (0, i, 0))
    tab = pl.BlockSpec((tm, DA), row)
    return pl.pallas_call(
        body, name=name, grid=(s // tm,),
        in_specs=[heads, heads, heads, tab, tab, tab, tab, tab, pl.BlockSpec((1, DA), lambda i: (0, 0))],
        out_specs=[pl.BlockSpec((tm, 2 * H * DA), row), pl.BlockSpec((tm, 2 * H * DA), row), tab, tab],
        out_shape=[jax.ShapeDtypeStruct((s, 2 * H * DA), BF16), jax.ShapeDtypeStruct((s, 2 * H * DA), BF16),
                   jax.ShapeDtypeStruct((s, DA), BF16), jax.ShapeDtypeStruct((s, DA), BF16)],
        compiler_params=_cp(("parallel",)),
    )(dqa, dka, dva, dft, cq, sq, ck, sk, keep)


def _fox_bwd_prep(dfqa, dfka, dfva, *, name, tm=512):
    s = dfqa.shape[1]
    tm = min(tm, s)
    place = lambda v: jnp.stack([_sel(DA, DA, [(r, r) for r in range(FOX_D)], v), _sel(DA, DA, [(r, FOX_D + r) for r in range(FOX_D)], v)])
    gq, gk = place(SCALE_FOX), place(1.0)

    def body(dq_ref, dk_ref, dv_ref, gq_ref, gk_ref, dz_ref):
        for part, (d_ref, g_ref) in enumerate(((dq_ref, gq_ref), (dk_ref, gk_ref), (dv_ref, gk_ref))):
            for p in range(H // 2):
                blk = _dot(d_ref[2 * p].astype(BF16), g_ref[0], NN) + _dot(d_ref[2 * p + 1].astype(BF16), g_ref[1], NN)
                lo = part * H * FOX_D + p * DA
                dz_ref[:, lo:lo + DA] = blk.astype(BF16)

    heads = pl.BlockSpec((H, tm, DA), lambda i: (0, i, 0))
    sel = pl.BlockSpec((2, DA, DA), lambda i: (0, 0, 0))
    return pl.pallas_call(
        body, name=name, grid=(s // tm,), in_specs=[heads, heads, heads, sel, sel],
        out_specs=pl.BlockSpec((tm, 3 * H * FOX_D), lambda i: (i, 0)),
        out_shape=jax.ShapeDtypeStruct((s, 3 * H * FOX_D), BF16), compiler_params=_cp(("parallel",)),
    )(dfqa, dfka, dfva, gq, gk)


def _lane_scan(x, s, reverse):
    lane = lax.broadcasted_iota(jnp.int32, x.shape, 1)
    sh = 1
    while sh < s:
        if reverse:
            x = x + jnp.where(lane < s - sh, pltpu.roll(x, s - sh, axis=1), 0.0)
        else:
            x = x + jnp.where(lane >= sh, pltpu.roll(x, sh, axis=1), 0.0)
        sh *= 2
    return x


def _gate_fwd(z, col_block, bias, *, name):
    s = z.shape[0]

    def body(z_ref, b_ref, f_ref, c_ref):
        ft = z_ref[...].T[0:8, :]
        f_ref[...] = ft
        xg = ft + b_ref[...]
        lf = jnp.minimum(xg, 0.0) - jnp.log(1.0 + jnp.exp(-jnp.abs(xg)))
        c = _lane_scan(lf, s, False)
        hi = c.astype(BF16).astype(F32)
        r = c - hi
        mid = r.astype(BF16).astype(F32)
        lo = r - mid
        c_ref[...] = jnp.concatenate([hi, mid, lo, jnp.zeros_like(hi)], axis=0).astype(BF16)

    return pl.pallas_call(
        body, name=name, grid=(1,),
        in_specs=[pl.BlockSpec((s, 128), lambda i: (0, col_block)), pl.BlockSpec((8, 1), lambda i: (0, 0))],
        out_specs=[pl.BlockSpec((8, s), lambda i: (0, 0)), pl.BlockSpec((32, s), lambda i: (0, 0))],
        out_shape=[jax.ShapeDtypeStruct((8, s), F32), jax.ShapeDtypeStruct((32, s), BF16)],
        compiler_params=_cp(("arbitrary",)))(z, bias)


def _gate_bwd(ft, bias, dc, *, name):
    s = ft.shape[1]

    def body(f_ref, b_ref, dc_ref, df_ref, db_ref):
        xg = f_ref[...] + b_ref[...]
        dlf = _lane_scan(dc_ref[...], s, True)
        df = dlf * _sigmoid(-xg)
        db_ref[...] = jnp.sum(df, axis=-1, keepdims=True)
        df_ref[...] = jnp.concatenate([df, jnp.zeros((DA - 8, s), F32)], axis=0).T

    return pl.pallas_call(body, name=name, out_shape=[jax.ShapeDtypeStruct((s, DA), F32), jax.ShapeDtypeStruct((8, 1), F32)],
                          compiler_params=_cp())(ft, bias, dc)


def _pool_lane_consts(tm, i):
    lane = lax.broadcasted_iota(jnp.int32, (tm, POOL_W), 1)
    tok = lax.broadcasted_iota(jnp.int32, (tm, POOL_W), 0) + i * tm
    win = jnp.where(lane < 64, 2, jnp.where(lane < 128, 4, jnp.where(lane < 192, 8, 16)))
    cnt = jnp.minimum(tok + 1, win).astype(F32)
    return lane, tok, cnt


def _pick_window(lane, s2, s4, s8, s16):
    return jnp.where(lane < 64, s2, jnp.where(lane < 128, s4, jnp.where(lane < 192, s8, s16)))


def _pool_fwd(z, col_block, bd, scale, *, name, tm=512):
    s = z.shape[0]
    tm = min(tm, s)
    hb = tm // POOL_HALO

    def body(u_ref, halo_ref, bd_ref, sc_ref, y_ref, p_ref, buf):
        i = pl.program_id(0)
        buf[0:POOL_HALO, :] = halo_ref[...] * (i > 0).astype(F32)
        buf[POOL_HALO:, :] = u_ref[...]

        def back(k):
            return buf[POOL_HALO - k:POOL_HALO - k + tm, :]

        u = u_ref[...]
        s2 = u + back(1)
        s4 = s2 + back(2) + back(3)
        s8 = s4 + back(4) + back(5) + back(6) + back(7)
        s16 = s8
        for k in range(8, 16):
            s16 = s16 + back(k)
        lane, _, cnt = _pool_lane_consts(tm, i)
        pooled = (_pick_window(lane, s2, s4, s8, s16) / cnt - u).astype(BF16)
        p_ref[...] = pooled
        y_ref[...] = _dot(pooled, bd_ref[...], NN) * sc_ref[...]

    return pl.pallas_call(
        body, name=name, grid=(s // tm,),
        in_specs=[pl.BlockSpec((tm, POOL_W), lambda i: (i, col_block)),
                  pl.BlockSpec((POOL_HALO, POOL_W), lambda i: (jnp.maximum(i * hb - 1, 0), col_block)),
                  pl.BlockSpec((POOL_W, POOL_W), lambda i: (0, 0)), pl.BlockSpec((1, POOL_W), lambda i: (0, 0))],
        out_specs=[pl.BlockSpec((tm, POOL_W), lambda i: (i, 0)), pl.BlockSpec((tm, POOL_W), lambda i: (i, 0))],
        out_shape=[jax.ShapeDtypeStruct((s, POOL_W), F32), jax.ShapeDtypeStruct((s, POOL_W), BF16)],
        scratch_shapes=[pltpu.VMEM((tm + POOL_HALO, POOL_W), F32)],
        compiler_params=_cp(("parallel",)),
    )(z, z, bd, scale.reshape(1, POOL_W))


def _pool_bwd_a(dy, pooled, bd, scale, *, name, tm=512):
    s = dy.shape[0]
    tm = min(tm, s)

    def body(dy_ref, p_ref, bd_ref, sc_ref, dq_ref, dys_ref, dsc_ref):
        i = pl.program_id(0)
        dyv = dy_ref[...]
        y0 = _dot(p_ref[...], bd_ref[...], NN)
        dys = (dyv * sc_ref[...]).astype(BF16)
        dys_ref[...] = dys
        dp = _dot(dys, bd_ref[...], NT)
        _, _, cnt = _pool_lane_consts(tm, i)
        dq_ref[:, 0:POOL_W] = dp / cnt
        dq_ref[:, POOL_W:] = dp

        @pl.when(i == 0)
        def _():
            dsc_ref[...] = jnp.zeros_like(dsc_ref)

        dsc_ref[...] += jnp.sum(dyv * y0, axis=0, keepdims=True)

    row = lambda i: (i, 0)
    dq, dys, dsc = pl.pallas_call(
        body, name=name, grid=(s // tm,),
        in_specs=[pl.BlockSpec((tm, POOL_W), row), pl.BlockSpec((tm, POOL_W), row),
                  pl.BlockSpec((POOL_W, POOL_W), lambda i: (0, 0)), pl.BlockSpec((1, POOL_W), lambda i: (0, 0))],
        out_specs=[pl.BlockSpec((tm, 2 * POOL_W), row), pl.BlockSpec((tm, POOL_W), row), pl.BlockSpec((1, POOL_W), lambda i: (0, 0))],
        out_shape=[jax.ShapeDtypeStruct((s, 2 * POOL_W), F32), jax.ShapeDtypeStruct((s, POOL_W), BF16),
                   jax.ShapeDtypeStruct((1, POOL_W), F32)],
        compiler_params=_cp(("arbitrary",)),
    )(dy, pooled, bd, scale.reshape(1, POOL_W))
    return dq, dys, dsc.reshape(POOL_W)


def _pool_bwd_b(dq, *, name, tm=512):
    s = dq.shape[0]
    tm = min(tm, s)
    hb = tm // POOL_HALO
    nblk = s // tm

    def body(q_ref, dp_ref, halo_ref, du_ref, buf):
        i = pl.program_id(0)
        buf[0:tm, :] = q_ref[...]
        buf[tm:, :] = halo_ref[...] * (i < nblk - 1).astype(F32)

        def ahead(k):
            return buf[k:k + tm, :]

        q = q_ref[...]
        s2 = q + ahead(1)
        s4 = s2 + ahead(2) + ahead(3)
        s8 = s4 + ahead(4) + ahead(5) + ahead(6) + ahead(7)
        s16 = s8
        for k in range(8, 16):
            s16 = s16 + ahead(k)
        lane = lax.broadcasted_iota(jnp.int32, (tm, POOL_W), 1)
        du_ref[...] = _pick_window(lane, s2, s4, s8, s16) - dp_ref[...]

    return pl.pallas_call(
        body, name=name, grid=(nblk,),
        in_specs=[pl.BlockSpec((tm, POOL_W), lambda i: (i, 0)), pl.BlockSpec((tm, POOL_W), lambda i: (i, 1)),
                  pl.BlockSpec((POOL_HALO, POOL_W), lambda i: (jnp.minimum((i + 1) * hb, nblk * hb - 1), 0))],
        out_specs=pl.BlockSpec((tm, POOL_W), lambda i: (i, 0)),
        out_shape=jax.ShapeDtypeStruct((s, POOL_W), F32),
        scratch_shapes=[pltpu.VMEM((tm + POOL_HALO, POOL_W), F32)],
        compiler_params=_cp(("parallel",)),
    )(dq, dq, dq)


def _loss_head(x, gain, target, *, name, tm=512):
    s = x.shape[0]
    tm = min(tm, s)

    def body(x_ref, g_ref, t_ref, dx_ref, dg_ref, loss_ref):
        xv = x_ref[...]
        r = lax.rsqrt(jnp.mean(xv * xv, axis=-1, keepdims=True) + EPS)
        xh = xv * r
        err = xh * g_ref[...] - t_ref[...]
        dy = err * (1.0 / D)
        a = dy * g_ref[...]
        dx_ref[...] = r * a - xh * (r * jnp.mean(a * xh, axis=-1, keepdims=True))

        @pl.when(pl.program_id(0) == 0)
        def _():
            dg_ref[...] = jnp.zeros_like(dg_ref)
            loss_ref[...] = jnp.zeros_like(loss_ref)

        dg_ref[...] += jnp.sum(dy * xh, axis=0, keepdims=True)
        part = 0.5 * jnp.sum(jnp.mean(err * err, axis=-1, keepdims=True), axis=0, keepdims=True)
        loss_ref[...] += jnp.broadcast_to(part, loss_ref.shape)

    row = lambda i: (i, 0)
    dx, dg, loss = pl.pallas_call(
        body, name=name, grid=(s // tm,),
        in_specs=[pl.BlockSpec((tm, D), row), pl.BlockSpec((1, D), lambda i: (0, 0)), pl.BlockSpec((tm, D), row)],
        out_specs=[pl.BlockSpec((tm, D), row), pl.BlockSpec((1, D), lambda i: (0, 0)), pl.BlockSpec((1, 128), lambda i: (0, 0))],
        out_shape=[jax.ShapeDtypeStruct((s, D), F32), jax.ShapeDtypeStruct((1, D), F32), jax.ShapeDtypeStruct((1, 128), F32)],
        compiler_params=_cp(("arbitrary",)),
    )(x, gain.reshape(1, D), target)
    return dx, dg.reshape(D), loss[0, 0]


def _adamw(w, g, m, v, *, name, tr=512):
    rows, cols = w.shape
    tr = min(tr, rows)
    assert rows % tr == 0, (name, rows, tr)
    c_m = 1.0 - ADAM_B1
    c_v = 1.0 - ADAM_B2
    bc1 = 1.0 - ADAM_B1 ** ADAM_STEP
    bc2 = 1.0 - ADAM_B2 ** ADAM_STEP

    def body(w_ref, g_ref, m_ref, v_ref, d_ref, mo_ref, vo_ref):
        gv = g_ref[...]
        mn = ADAM_B1 * m_ref[...] + c_m * gv
        vn = ADAM_B2 * v_ref[...] + c_v * (gv * gv)
        mo_ref[...] = mn
        vo_ref[...] = vn
        d_ref[...] = -ADAM_LR * ((mn / bc1) / (jnp.sqrt(vn / bc2) + ADAM_EPS) + ADAM_WD * w_ref[...])

    spec = pl.BlockSpec((tr, cols), lambda i: (i, 0))
    return pl.pallas_call(body, name=name, grid=(rows // tr,), in_specs=[spec] * 4, out_specs=[spec] * 3,
                          out_shape=[jax.ShapeDtypeStruct((rows, cols), F32)] * 3,
                          compiler_params=_cp(("parallel",)))(w, g, m, v)


def _row_tile(rows, cap):
    for t in range(min(cap, rows), 0, -1):
        if rows % t == 0 and t % 16 == 0:
            return t
    return rows


def _position():
    return jnp.stack([lax.axis_index("c"), 2 * lax.axis_index("x") + lax.axis_index("y")]).astype(jnp.int32)


def _sum2_bf16(pos, full, sib, *, name, tr=256):
    n, half, cols = sib.shape
    tr = _row_tile(half, tr)
    nb = half // tr

    def body(pos_ref, a_ref, b_ref, o_ref):
        o_ref[...] = (a_ref[...] + b_ref[...]).astype(BF16)

    spec = pl.BlockSpec((None, tr, cols), lambda j, i, p: (j, i, 0))
    return pl.pallas_call(
        body, name=name,
        grid_spec=pltpu.PrefetchScalarGridSpec(
            num_scalar_prefetch=1, grid=(n, nb),
            in_specs=[pl.BlockSpec((None, tr, cols), lambda j, i, p: (j, p[0] * nb + i, 0)), spec], out_specs=spec),
        out_shape=jax.ShapeDtypeStruct(sib.shape, BF16), compiler_params=_cp(("parallel", "parallel")))(pos, full, sib)


def _sum5(pos, full, sib, recv, *, name, tr=256):
    _, rows, cols = full.shape
    half = rows // 2
    tr = _row_tile(half, tr)
    nb = half // tr

    def body(pos_ref, a_ref, b_ref, r_ref, o_ref):
        acc = a_ref[...] + b_ref[...]
        for kk in range(3):
            acc = acc + r_ref[kk].astype(F32)
        o_ref[...] = acc

    return pl.pallas_call(
        body, name=name,
        grid_spec=pltpu.PrefetchScalarGridSpec(
            num_scalar_prefetch=1, grid=(nb,),
            in_specs=[pl.BlockSpec((None, tr, cols), lambda i, p: (p[1], p[0] * nb + i, 0)),
                      pl.BlockSpec((None, tr, cols), lambda i, p: (p[1], i, 0)),
                      pl.BlockSpec((3, tr, cols), lambda i, p: (0, i, 0))],
            out_specs=pl.BlockSpec((tr, cols), lambda i, p: (p[0] * nb + i, 0))),
        out_shape=jax.ShapeDtypeStruct((rows, cols), F32), compiler_params=_cp(("parallel",)))(pos, full, sib, recv)


def _place():
    x, y, c = lax.axis_index("x"), lax.axis_index("y"), lax.axis_index("c")
    chips = [(1 - x, y), (x, 1 - y), (1 - x, 1 - y)]
    return x, y, c, 2 * x + y, chips


SEM_SPEC = pl.BlockSpec(memory_space=pltpu.SEMAPHORE)
ANY_SPEC = pl.BlockSpec(memory_space=pl.ANY)


def _gather_copies(ins, outs, send_i, recv_i, send_o, recv_o):
    x, y, c, me, chips = _place()
    n = len(ins)
    started, awaited = [], []
    for t in range(n):
        half = ins[t].shape[0] // 2
        mine = pl.ds(c * half, half)
        started.append(pltpu.make_async_remote_copy(
            src_ref=ins[t], dst_ref=outs[t].at[me], send_sem=send_o.at[t], recv_sem=recv_o.at[t],
            device_id=(x, y, 1 - c), device_id_type=MESH))
        awaited.append(started[-1])
        for kk, (px, py) in enumerate(chips):
            started.append(pltpu.make_async_remote_copy(
                src_ref=ins[t].at[mine], dst_ref=outs[t].at[me, mine], send_sem=send_i.at[t * 3 + kk],
                recv_sem=recv_i.at[t * 3 + kk], device_id=(px, py, c), device_id_type=MESH))
            awaited.append(pltpu.make_async_remote_copy(
                src_ref=ins[t].at[mine], dst_ref=outs[t].at[2 * px + py, mine], send_sem=send_i.at[t * 3 + kk],
                recv_sem=recv_i.at[t * 3 + kk], device_id=(px, py, c), device_id_type=MESH))
    return started, awaited


def _forward_copies(outs, send_d, recv_d):
    x, y, c, me, chips = _place()
    started, awaited = [], []
    for t in range(len(outs)):
        half = outs[t].shape[1] // 2
        for kk, (px, py) in enumerate(chips):
            for lst, hc in ((started, c), (awaited, 1 - c)):
                blk = outs[t].at[2 * px + py, pl.ds(hc * half, half)]
                lst.append(pltpu.make_async_remote_copy(src_ref=blk, dst_ref=blk, send_sem=send_d.at[t * 3 + kk],
                                                        recv_sem=recv_d.at[t * 3 + kk], device_id=(x, y, 1 - c), device_id_type=MESH))
    return started, awaited


def _gather_blocking(shards):
    n = len(shards)

    def body(*refs):
        ins, outs = refs[:n], refs[n:2 * n]
        send_i, recv_i, send_d, recv_d, send_o, recv_o = refs[2 * n:]
        started, awaited = _gather_copies(ins, outs, send_i, recv_i, send_o, recv_o)
        for cp in started:
            cp.start()
        for cp in awaited:
            cp.wait_recv()
        fwd, fwd_in = _forward_copies(outs, send_d, recv_d)
        for cp in fwd:
            cp.start()
        for cp in fwd_in:
            cp.wait_recv()
        for cp in started + fwd:
            cp.wait_send()

    return pl.pallas_call(
        body, name="gather_first", in_specs=[HBM_SPEC] * n, out_specs=[HBM_SPEC] * n,
        out_shape=[jax.ShapeDtypeStruct((N_CHIPS,) + s.shape, s.dtype) for s in shards],
        scratch_shapes=[pltpu.SemaphoreType.DMA((3 * n,)), pltpu.SemaphoreType.DMA((3 * n,)),
                        pltpu.SemaphoreType.DMA((3 * n,)), pltpu.SemaphoreType.DMA((3 * n,)),
                        pltpu.SemaphoreType.DMA((n,)), pltpu.SemaphoreType.DMA((n,))],
    )(*shards)


def _gather_start(shards, after):
    n = len(shards)

    def body(*refs):
        ins = refs[:n]
        send_i, recv_i, send_o, recv_o = refs[2 * n + 1:2 * n + 5]
        outs = refs[3 * n + 5:4 * n + 5]
        token = refs[4 * n + 5]
        started, _ = _gather_copies(ins, outs, send_i, recv_i, send_o, recv_o)
        for cp in started:
            cp.start()
        token[...] = jnp.zeros_like(token)

    lands = [lax.empty((N_CHIPS,) + s.shape, s.dtype) for s in shards]
    sems = [pltpu.SemaphoreType.DMA((3 * n,)), pltpu.SemaphoreType.DMA((3 * n,)), pltpu.SemaphoreType.DMA((n,)), pltpu.SemaphoreType.DMA((n,))]
    res = pl.pallas_call(
        body, name="gather_rest_start",
        in_specs=[HBM_SPEC] * (2 * n) + [ANY_SPEC],
        out_specs=[SEM_SPEC] * 4 + [HBM_SPEC] * (2 * n) + [pl.BlockSpec(memory_space=pltpu.VMEM)],
        out_shape=sems + [jax.ShapeDtypeStruct(s.shape, s.dtype) for s in shards]
        + [jax.ShapeDtypeStruct(a.shape, a.dtype) for a in lands] + [jax.ShapeDtypeStruct((8, 128), F32)],
        input_output_aliases={t: 4 + t for t in range(2 * n)},
        compiler_params=pltpu.CompilerParams(has_side_effects=pltpu.SideEffectType.DATAFLOW_SIDE_EFFECTING),
    )(*[pltpu.with_memory_space_constraint(s, pltpu.HBM) for s in shards],
      *[pltpu.with_memory_space_constraint(a, pltpu.HBM) for a in lands], after)
    return res[:4], res[4:4 + n], res[4 + n:4 + 2 * n], res[-1]


def _gather_wait(sems, shards_thru, lands_thru, after):
    n = len(shards_thru)

    def body(*refs):
        ins, outs_in = refs[:n], refs[n:2 * n]
        send_i, recv_i, send_o, recv_o = refs[2 * n:2 * n + 4]
        started, awaited = _gather_copies(ins, outs_in, send_i, recv_i, send_o, recv_o)
        for cp in started:
            cp.wait_send()
        for cp in awaited:
            cp.wait_recv()

    res = pl.pallas_call(
        body, name="gather_rest_wait",
        in_specs=[HBM_SPEC] * (2 * n) + [SEM_SPEC] * 4 + [ANY_SPEC],
        out_specs=[HBM_SPEC] * (2 * n),
        out_shape=[jax.ShapeDtypeStruct(a.shape, a.dtype) for a in list(shards_thru) + list(lands_thru)],
        input_output_aliases={t: t for t in range(2 * n)},
        compiler_params=pltpu.CompilerParams(has_side_effects=pltpu.SideEffectType.DATAFLOW_SIDE_EFFECTING),
    )(*shards_thru, *lands_thru, *sems, after)
    return res[n:]


def _gather_forward(lands):
    n = len(lands)

    def body(*refs):
        outs = refs[n:2 * n]
        send_d, recv_d = refs[2 * n:]
        fwd, fwd_in = _forward_copies(outs, send_d, recv_d)
        for cp in fwd:
            cp.start()
        for cp in fwd_in:
            cp.wait_recv()
        for cp in fwd:
            cp.wait_send()

    return pl.pallas_call(
        body, name="gather_rest_forward", in_specs=[HBM_SPEC] * n, out_specs=[HBM_SPEC] * n,
        out_shape=[jax.ShapeDtypeStruct(a.shape, a.dtype) for a in lands],
        input_output_aliases={t: t for t in range(n)},
        scratch_shapes=[pltpu.SemaphoreType.DMA((3 * n,)), pltpu.SemaphoreType.DMA((3 * n,))],
    )(*lands)


def _reduce_stage1(grads):
    n = len(grads)

    def body(*refs):
        ins, sib = refs[:n], refs[n:2 * n]
        send, recv = refs[2 * n:]
        x, y, c, me, chips = _place()
        cps = []
        for t in range(n):
            rows = ins[t].shape[1] // 2
            cp = pltpu.make_async_remote_copy(
                src_ref=ins[t].at[:, pl.ds((1 - c) * rows, rows), :], dst_ref=sib[t], send_sem=send.at[t],
                recv_sem=recv.at[t], device_id=(x, y, 1 - c), device_id_type=MESH)
            cp.start()
            cps.append(cp)
        for cp in cps:
            cp.wait()

    return pl.pallas_call(
        body, name="reduce_stage1", in_specs=[HBM_SPEC] * n, out_specs=[HBM_SPEC] * n,
        out_shape=[jax.ShapeDtypeStruct((N_CHIPS, g.shape[1] // 2, g.shape[2]), F32) for g in grads],
        scratch_shapes=[pltpu.SemaphoreType.DMA((n,)), pltpu.SemaphoreType.DMA((n,))],
    )(*grads)


def _stage2_copies(ps, rcv, send, recv):
    x, y, c, me, chips = _place()
    return [pltpu.make_async_remote_copy(
        src_ref=ps[t].at[2 * px + py], dst_ref=rcv[t].at[kk], send_sem=send.at[t * 3 + kk],
        recv_sem=recv.at[t * 3 + kk], device_id=(px, py, c), device_id_type=MESH)
        for t in range(len(ps)) for kk, (px, py) in enumerate(chips)]


def _reduce_stage2(psum_bf16):
    n = len(psum_bf16)

    def body(*refs):
        cps = _stage2_copies(refs[:n], refs[n:2 * n], *refs[2 * n:])
        for cp in cps:
            cp.start()
        for cp in cps:
            cp.wait()

    return pl.pallas_call(
        body, name="reduce_stage2", in_specs=[HBM_SPEC] * n, out_specs=[HBM_SPEC] * n,
        out_shape=[jax.ShapeDtypeStruct((3,) + p.shape[1:], p.dtype) for p in psum_bf16],
        scratch_shapes=[pltpu.SemaphoreType.DMA((3 * n,)), pltpu.SemaphoreType.DMA((3 * n,))],
    )(*psum_bf16)


def _reduce_stage2_start(psum_bf16):
    n = len(psum_bf16)

    def body(*refs):
        ps = refs[:n]
        send, recv = refs[2 * n:2 * n + 2]
        rcv = refs[3 * n + 2:4 * n + 2]
        token = refs[4 * n + 2]
        for cp in _stage2_copies(ps, rcv, send, recv):
            cp.start()
        token[...] = jnp.zeros_like(token)

    lands = [lax.empty((3,) + p.shape[1:], p.dtype) for p in psum_bf16]
    res = pl.pallas_call(
        body, name="reduce_stage2_start",
        in_specs=[HBM_SPEC] * (2 * n),
        out_specs=[SEM_SPEC] * 2 + [HBM_SPEC] * (2 * n) + [pl.BlockSpec(memory_space=pltpu.VMEM)],
        out_shape=[pltpu.SemaphoreType.DMA((3 * n,)), pltpu.SemaphoreType.DMA((3 * n,))]
        + [jax.ShapeDtypeStruct(p.shape, p.dtype) for p in psum_bf16]
        + [jax.ShapeDtypeStruct(a.shape, a.dtype) for a in lands] + [jax.ShapeDtypeStruct((8, 128), F32)],
        input_output_aliases={t: 2 + t for t in range(2 * n)},
        compiler_params=pltpu.CompilerParams(has_side_effects=pltpu.SideEffectType.DATAFLOW_SIDE_EFFECTING),
    )(*[pltpu.with_memory_space_constraint(p, pltpu.HBM) for p in psum_bf16],
      *[pltpu.with_memory_space_constraint(a, pltpu.HBM) for a in lands])
    return res[:2], res[2:2 + n], res[2 + n:2 + 2 * n], res[-1]


def _reduce_stage2_wait(sems, ps_thru, lands_thru, after):
    n = len(ps_thru)

    def body(*refs):
        for cp in _stage2_copies(refs[:n], refs[n:2 * n], refs[2 * n], refs[2 * n + 1]):
            cp.wait()

    res = pl.pallas_call(
        body, name="reduce_stage2_wait",
        in_specs=[HBM_SPEC] * (2 * n) + [SEM_SPEC] * 2 + [ANY_SPEC],
        out_specs=[HBM_SPEC] * (2 * n),
        out_shape=[jax.ShapeDtypeStruct(a.shape, a.dtype) for a in list(ps_thru) + list(lands_thru)],
        input_output_aliases={t: t for t in range(2 * n)},
        compiler_params=pltpu.CompilerParams(has_side_effects=pltpu.SideEffectType.DATAFLOW_SIDE_EFFECTING),
    )(*ps_thru, *lands_thru, *sems, after)
    return res[n:]


def _reduce_stage3(reduced):
    n = len(reduced)

    def body(*refs):
        outs = refs[n:2 * n]
        send, recv = refs[2 * n:]
        x, y, c, me, chips = _place()
        cps = []
        for t in range(n):
            rows = outs[t].shape[0] // 2
            mine = outs[t].at[pl.ds(c * rows, rows), :]
            cp = pltpu.make_async_remote_copy(src_ref=mine, dst_ref=mine, send_sem=send.at[t], recv_sem=recv.at[t],
                                              device_id=(x, y, 1 - c), device_id_type=MESH)
            cp.start()
            cps.append(cp)
        for cp in cps:
            cp.wait()

    return pl.pallas_call(
        body, name="reduce_stage3", in_specs=[HBM_SPEC] * n, out_specs=[HBM_SPEC] * n,
        out_shape=[jax.ShapeDtypeStruct(r.shape, r.dtype) for r in reduced],
        input_output_aliases={t: t for t in range(n)},
        scratch_shapes=[pltpu.SemaphoreType.DMA((n,)), pltpu.SemaphoreType.DMA((n,))],
    )(*reduced)


def _allreduce_small(v):
    rows, cols = v.shape

    def body(v_ref, o_ref, buf, send, recv, loc):
        x, y, c, me, chips = _place()
        mine = 4 * x + 2 * y + c
        lc = pltpu.make_async_copy(v_ref, buf.at[mine], loc)
        lc.start()
        peers = []
        for fx in range(2):
            for fy in range(2):
                for fc in range(2):
                    if fx or fy or fc:
                        peers.append((fx, fy, fc))
        cps = []
        for kk, (fx, fy, fc) in enumerate(peers):
            to = (x ^ fx, y ^ fy, c ^ fc)
            cp = pltpu.make_async_remote_copy(src_ref=v_ref, dst_ref=buf.at[mine], send_sem=send.at[kk], recv_sem=recv.at[kk],
                                              device_id=to, device_id_type=MESH)
            cp.start()
            cps.append((cp, to))
        for kk, (cp, to) in enumerate(cps):
            src = 4 * to[0] + 2 * to[1] + to[2]
            pltpu.make_async_remote_copy(src_ref=v_ref, dst_ref=buf.at[src], send_sem=send.at[kk], recv_sem=recv.at[kk],
                                         device_id=to, device_id_type=MESH).wait_recv()
        for cp, _ in cps:
            cp.wait_send()
        lc.wait()
        acc = buf[0]
        for d in range(1, 8):
            acc = acc + buf[d]
        o_ref[...] = acc

    return pl.pallas_call(
        body, name="allreduce_small", in_specs=[pl.BlockSpec(memory_space=pltpu.VMEM)],
        out_specs=pl.BlockSpec(memory_space=pltpu.VMEM), out_shape=jax.ShapeDtypeStruct((rows, cols), F32),
        scratch_shapes=[pltpu.VMEM((8, rows, cols), F32), pltpu.SemaphoreType.DMA((7,)), pltpu.SemaphoreType.DMA((7,)),
                        pltpu.SemaphoreType.DMA],
        compiler_params=pltpu.CompilerParams(vmem_limit_bytes=VMEM_LIMIT_V7X),
    )(v)


def _pad_w_in(w):
    z = lambda n: jnp.zeros(w.shape[:-1] + (n,), w.dtype)
    return jnp.concatenate([w[..., 0:384], z(64), w[..., 384:416], z(32), w[..., 416:1824],
                            w[..., 1824:1830], z(58), w[..., 400:416], w[..., 384:400], z(32)], axis=-1)


def _unpad_w_in(g):
    x1 = g[..., 448:464] + g[..., Z_F + 80:Z_F + 96]
    x2 = g[..., 464:480] + g[..., Z_F + 64:Z_F + 80]
    return jnp.concatenate([g[..., 0:384], x1, x2, g[..., 512:1920], g[..., 1920:1926]], axis=-1)


def _block_diag(pw):
    out = jnp.zeros((POOL_W, POOL_W), pw.dtype)
    for g in range(4):
        out = out.at[g * 64:(g + 1) * 64, g * 64:(g + 1) * 64].set(pw[g])
    return out


def _rope_tables(s):
    inv_freq = ROPE_THETA ** (-jnp.arange(0, ROPE, 2, dtype=F32) / ROPE)
    ang = jnp.arange(s, dtype=jnp.int32).astype(F32)[:, None] * inv_freq[None, :]
    cos, sin = jnp.cos(ang), jnp.sin(ang)
    zero = lambda n: jnp.zeros((s, n), F32)
    ck = jnp.concatenate([zero(NOPE), cos, cos, zero(DA - NOPE - ROPE)], axis=1)
    sk = jnp.concatenate([zero(NOPE), -sin, sin, zero(DA - NOPE - ROPE)], axis=1)
    cq = jnp.concatenate([jnp.ones((s, NOPE), F32), cos, cos, zero(DA - NOPE - ROPE)], axis=1) * SCALE_MLA
    return dict(cq=cq, sq=sk * SCALE_MLA, ck=ck, sk=sk)


def _mix_fwd(l, x1, wts, sm, tabs):
    z, h2 = _norm_mm(x1, 0, sm["mix_norm"][l], wts["w_in"][l], name=f"mix_in_{l}")
    qa, qn = _mla_q_prep(z, sm["q_a_norm"][l], wts["wq_a"][l], wts["wq_b"][l], tabs["cq"], tabs["sq"], name=f"mla_q_{l}")
    ka, va, kvn = _mla_kv_prep(z, sm["kv_a_norm"][l], wts["wk"][l], wts["wv"][l], tabs["ck"], tabs["sk"], name=f"mla_kv_{l}")
    oa, lse_a = _attn_fwd(qa, ka, va, VDIM, name=f"mla_attn_{l}")

    bd = _block_diag(wts["pool_w"][l]).astype(BF16)
    yb, pooled = _pool_fwd(z, Z_POOL // POOL_W, bd, sm["pool_scale"][l], name=f"pool_{l}")

    fb = jnp.pad(sm["fox_b_f"][l], (0, 8 - H)).reshape(8, 1)
    ft, c3t = _gate_fwd(z, Z_F // DA, fb, name=f"fox_gate_{l}")
    fqa, fka, fva = _fox_prep(z, c3t, name=f"fox_prep_{l}")
    oc, lse_c = _attn_fwd(fqa, fka, fva, FOX_D, name=f"fox_attn_{l}")

    x2, cat = _mix_out(oa, yb, oc, wts["w_out"][l], x1, name=f"mix_out_{l}")
    saved = dict(z=z, h2=h2, qn=qn, kvn=kvn, qa=qa, ka=ka, va=va, oa=oa, lse_a=lse_a, bd=bd, pooled=pooled,
                 fqa=fqa, fka=fka, fva=fva, ft=ft, fb=fb, oc=oc, lse_c=lse_c, cat=cat)
    return x2, saved


def _mix_bwd(l, x1, dx2, sv, wts, sm, tabs):
    s = x1.shape[0]
    g = {}
    dx2b = dx2.astype(BF16)
    g["w_out"] = _mm(sv["cat"], dx2b, "tn", name=f"d_w_out_{l}", tm=1024, tn=1024)
    doa, doc, dyb, dl_a, dl_c = _mix_out_bwd(dx2b, wts["w_out"][l], sv["oa"], sv["oc"], name=f"mix_out_bwd_{l}")

    dfqa, dfka, dfva, dcq, dck = _attn_bwd(sv["fqa"], sv["fka"], sv["fva"], doc, sv["lse_c"], dl_c, True, name=f"fox_attn_bwd_{l}")
    dfox = _fox_bwd_prep(dfqa, dfka, dfva, name=f"fox_bwd_prep_{l}")
    dc = jnp.pad(dcq.reshape(H, s) + dck.reshape(H, s), ((0, 8 - H), (0, 0)))
    dft, dfb = _gate_bwd(sv["ft"], sv["fb"], dc, name=f"fox_gate_bwd_{l}")
    g["fox_b_f"] = dfb[:H, 0]

    dq, dys, g["pool_scale"] = _pool_bwd_a(dyb, sv["pooled"], sv["bd"], sm["pool_scale"][l], name=f"pool_bwd_a_{l}")
    du = _pool_bwd_b(dq, name=f"pool_bwd_b_{l}")
    dbd = _mm(sv["pooled"], dys, "tn", name=f"d_pool_w_{l}")
    g["pool_w"] = jnp.stack([dbd[i * 64:(i + 1) * 64, i * 64:(i + 1) * 64] for i in range(4)])

    dqa_, dka_, dva_ = _attn_bwd(sv["qa"], sv["ka"], sv["va"], doa, sv["lse_a"], dl_a, False, name=f"mla_attn_bwd_{l}")
    dqab, dkv, dz3, dz15 = _mla_bwd_prep(dqa_, dka_, dva_, dft, tabs["cq"], tabs["sq"], tabs["ck"], tabs["sk"],
                                         name=f"mla_bwd_prep_{l}")
    wq_ab = jnp.concatenate([wts["wq_a"][l], wts["wq_b"][l]], axis=1)
    wkv = jnp.concatenate([wts["wk"][l], wts["wv"][l]], axis=1)
    dwq = _mm(sv["qn"], dqab, "tn", name=f"d_w_q_b_{l}", tn=768).reshape(Q_RANK, 2, H, DA)
    dwkv = _mm(sv["kvn"], dkv, "tn", name=f"d_w_kv_b_{l}", tn=768).reshape(KV_RANK, 2, H, DA)
    da, db = dwq[:, 0], dwq[:, 1]
    swapped = jnp.concatenate([jnp.zeros((Q_RANK, H, NOPE), F32), db[..., NOPE + HALF_ROPE:NOPE + ROPE],
                               db[..., NOPE:NOPE + HALF_ROPE]], axis=-1)
    g["w_q_b"] = (da[..., :NOPE + ROPE] + swapped).reshape(Q_RANK, H * (NOPE + ROPE))
    g["w_kv_b"] = jnp.concatenate([dwkv[:, 0, :, :NOPE], dwkv[:, 1, :, :VDIM]], axis=-1).reshape(KV_RANK, H * (NOPE + VDIM))
    dqn = _mm(dqab, wq_ab, "nt", name=f"d_qn_{l}", tk=768)
    dkvn = _mm(dkv, wkv, "nt", name=f"d_kvn_{l}", tk=768)
    dqa, g["q_a_norm"] = _rmsnorm_bwd(sv["z"], Z_QA // Q_RANK, sm["q_a_norm"][l], dqn, name=f"q_a_norm_bwd_{l}")
    dkva, g["kv_a_norm"] = _rmsnorm_bwd(sv["z"], Z_KVA // KV_RANK, sm["kv_a_norm"][l], dkvn, name=f"kv_a_norm_bwd_{l}")

    dz = jnp.concatenate([dqa.astype(BF16), dkva.astype(BF16), dz3, du.astype(BF16), dfox, dz15], axis=1)
    g["w_in"] = _mm(sv["h2"], dz, "tn", name=f"d_w_in_{l}", tm=1024, tn=1024)
    dh2 = _mm(dz, wts["w_in"][l], "nt", name=f"d_h2_{l}", tn=1024, tk=1024)
    dx1, g["mix_norm"] = _rmsnorm_bwd(x1, 0, sm["mix_norm"][l], dh2, dx2, name=f"mix_norm_bwd_{l}")
    return dx1, g


def _local_step(x, target, wts, sm, late_weights=None, layer_done=None):
    s = x.shape[0]
    tabs = _rope_tables(s)
    acts = []
    xs = x
    for l in range(DEPTH):
        x1, gu1 = _ffn_fwd(xs, sm["ffn1_norm"][l], wts["ffn1_w_gu"][l], wts["ffn1_w_d2"][l], name=f"ffn1_fwd_{l}")
        x2, sv = _mix_fwd(l, x1, wts, sm, tabs)
        if l == 0 and late_weights is not None:
            late_weights(x2)
        x3, gu2 = _ffn_fwd(x2, sm["ffn2_norm"][l], wts["ffn2_w_gu"][l], wts["ffn2_w_d2"][l], name=f"ffn2_fwd_{l}")
        acts.append((xs, gu1, x1, sv, x2, gu2))
        xs = x3
    dx, g_final, loss = _loss_head(xs, sm["final_norm"], target, name="loss_head")
    grads = [dict() for _ in range(DEPTH)]
    for l in reversed(range(DEPTH)):
        x0, gu1, x1, sv, x2, gu2 = acts[l]
        g = grads[l]
        dx, dgu, act, hh, dy, g["ffn2_norm"] = _ffn_bwd(x2, dx, gu2, sm["ffn2_norm"][l], wts["ffn2_w_gu"][l], wts["ffn2_w_d2"][l],
                                                        name=f"ffn2_bwd_{l}")
        g["ffn2_w_down"] = _mm(act, dy, "tn", name=f"d_ffn2_w_down_{l}", tm=FF_SHARD, tn=1024)
        g["ffn2_w_gu"] = _mm(hh, dgu, "tn", name=f"d_ffn2_w_gu_{l}", tm=1024, tn=FF_SHARD, n_major_out=True)
        dx, gm = _mix_bwd(l, x1, dx, sv, wts, sm, tabs)
        g.update(gm)
        dx, dgu, act, hh, dy, g["ffn1_norm"] = _ffn_bwd(x0, dx, gu1, sm["ffn1_norm"][l], wts["ffn1_w_gu"][l], wts["ffn1_w_d2"][l],
                                                        name=f"ffn1_bwd_{l}")
        g["ffn1_w_down"] = _mm(act, dy, "tn", name=f"d_ffn1_w_down_{l}", tm=FF_SHARD, tn=1024)
        g["ffn1_w_gu"] = _mm(hh, dgu, "tn", name=f"d_ffn1_w_gu_{l}", tm=1024, tn=FF_SHARD, n_major_out=True)
        if layer_done is not None:
            sm = layer_done(l, g, sm)
    return loss, dx, grads, g_final


BIG = ["ffn1_w_gu", "ffn1_w_down", "w_in", "w_q_b", "w_kv_b", "w_out", "ffn2_w_gu", "ffn2_w_down"]
SMALL = ["ffn1_norm", "mix_norm", "q_a_norm", "kv_a_norm", "pool_w", "pool_scale", "fox_b_f", "ffn2_norm"]
SMALL_ROWS = 48


WEIGHT_VIEWS = ["ffn1_w_gu", "ffn1_w_d2", "w_in", "wq_a", "wq_b", "wk", "wv", "w_out", "ffn2_w_gu", "ffn2_w_d2"]


def _prepare_weights(gathered, wts):
    for (nm, l), w in gathered.items():
        if nm in ("ffn1_w_gu", "ffn2_w_gu"):
            wts[nm][l] = w
        elif nm in ("ffn1_w_down", "ffn2_w_down"):
            wts[nm[:5] + "w_d2"][l] = w.reshape(2, FF_SHARD, D)
        elif nm in ("w_in", "w_out"):
            wts[nm][l] = w.reshape(D, -1)
        elif nm == "w_q_b":
            wq = jnp.moveaxis(w, 0, 1).reshape(Q_RANK, H, NOPE + ROPE)
            zq = lambda n: jnp.zeros((Q_RANK, H, n), BF16)
            wts["wq_a"][l] = jnp.concatenate([wq, zq(DA - NOPE - ROPE)], axis=-1).reshape(Q_RANK, H * DA)
            wts["wq_b"][l] = jnp.concatenate([zq(NOPE), wq[..., NOPE + HALF_ROPE:], wq[..., NOPE:NOPE + HALF_ROPE],
                                              zq(DA - NOPE - ROPE)], axis=-1).reshape(Q_RANK, H * DA)
        else:
            wkv = jnp.moveaxis(w, 0, 1).reshape(KV_RANK, H, NOPE + VDIM)
            zk = jnp.zeros((KV_RANK, H, DA - NOPE), BF16)
            wts["wk"][l] = jnp.concatenate([wkv[..., :NOPE], zk], axis=-1).reshape(KV_RANK, H * DA)
            wts["wv"][l] = jnp.concatenate([wkv[..., NOPE:], zk], axis=-1).reshape(KV_RANK, H * DA)


def _chip_major(name, g):
    if name in ("ffn1_w_gu", "ffn2_w_gu"):
        return g
    if name in ("ffn1_w_down", "ffn2_w_down", "w_in", "w_out"):
        return g.reshape(N_CHIPS, g.shape[0] // N_CHIPS, g.shape[1])
    return jnp.moveaxis(g.reshape(g.shape[0], N_CHIPS, g.shape[1] // N_CHIPS), 1, 0)


def _pack_small(grads, g_final, loss):
    parts = []
    for l in range(DEPTH):
        for nm in SMALL:
            parts.append(grads[l][nm].reshape(-1))
    parts.append(g_final.reshape(-1))
    parts.append(loss.reshape(1))
    flat = jnp.concatenate(parts)
    return jnp.pad(flat, (0, SMALL_ROWS * D - flat.shape[0])).reshape(SMALL_ROWS, D)


def _unpack_small(packed, params):
    flat = packed.reshape(-1)
    out = {nm: [] for nm in SMALL}
    off = 0
    for l in range(DEPTH):
        for nm in SMALL:
            shp = params[nm].shape[1:]
            n = int(np.prod(shp))
            out[nm].append(flat[off:off + n].reshape(shp))
            off += n
    res = {nm: jnp.stack(v) for nm, v in out.items()}
    res["final_norm"] = flat[off:off + D]
    return res, flat[off + D]


def _update(name, w, g, m, v):
    shp = w.shape
    if w.ndim == 1:
        view = (1, shp[0])
    elif w.size <= 65536:
        view = (shp[0], w.size // shp[0])
    else:
        view = (w.size // shp[-1], shp[-1])
    tr = view[0]
    for cand in (512, 352, 256, 128):
        if view[0] % cand == 0 and view[0] > cand:
            tr = cand
            break
    d, mn, vn = _adamw(w.reshape(view), g.reshape(view), m.reshape(view), v.reshape(view), name="adamw_" + name, tr=tr)
    return d.reshape(shp), mn.reshape(shp), vn.reshape(shp)


WEIGHTS = ['ffn1_norm', 'ffn1_w_gu', 'ffn1_w_down', 'mix_norm', 'w_in', 'q_a_norm', 'w_q_b', 'kv_a_norm', 'w_kv_b', 'pool_w',
           'pool_scale', 'fox_b_f', 'w_out', 'ffn2_norm', 'ffn2_w_gu', 'ffn2_w_down', 'final_norm']


def kernel(x, ffn1_norm, ffn1_w_gu, ffn1_w_down, mix_norm, w_in, q_a_norm, w_q_b, kv_a_norm, w_kv_b, pool_w, pool_scale, fox_b_f, w_out, ffn2_norm, ffn2_w_gu, ffn2_w_down, final_norm, loss_target, m_ffn1_norm, m_ffn1_w_gu, m_ffn1_w_down, m_mix_norm, m_w_in, m_q_a_norm, m_w_q_b, m_kv_a_norm, m_w_kv_b, m_pool_w, m_pool_scale, m_fox_b_f, m_w_out, m_ffn2_norm, m_ffn2_w_gu, m_ffn2_w_down, m_final_norm, v_ffn1_norm, v_ffn1_w_gu, v_ffn1_w_down, v_mix_norm, v_w_in, v_q_a_norm, v_w_q_b, v_kv_a_norm, v_w_kv_b, v_pool_w, v_pool_scale, v_fox_b_f, v_w_out, v_ffn2_norm, v_ffn2_w_gu, v_ffn2_w_down, v_final_norm):
    params = dict(ffn1_norm=ffn1_norm, ffn1_w_gu=ffn1_w_gu, ffn1_w_down=ffn1_w_down, mix_norm=mix_norm, w_in=w_in, q_a_norm=q_a_norm,
                  w_q_b=w_q_b, kv_a_norm=kv_a_norm, w_kv_b=w_kv_b, pool_w=pool_w, pool_scale=pool_scale, fox_b_f=fox_b_f, w_out=w_out,
                  ffn2_norm=ffn2_norm, ffn2_w_gu=ffn2_w_gu, ffn2_w_down=ffn2_w_down, final_norm=final_norm)
    mom = dict(ffn1_norm=m_ffn1_norm, ffn1_w_gu=m_ffn1_w_gu, ffn1_w_down=m_ffn1_w_down, mix_norm=m_mix_norm, w_in=m_w_in,
               q_a_norm=m_q_a_norm, w_q_b=m_w_q_b, kv_a_norm=m_kv_a_norm, w_kv_b=m_w_kv_b, pool_w=m_pool_w, pool_scale=m_pool_scale,
               fox_b_f=m_fox_b_f, w_out=m_w_out, ffn2_norm=m_ffn2_norm, ffn2_w_gu=m_ffn2_w_gu, ffn2_w_down=m_ffn2_w_down,
               final_norm=m_final_norm)
    var = dict(ffn1_norm=v_ffn1_norm, ffn1_w_gu=v_ffn1_w_gu, ffn1_w_down=v_ffn1_w_down, mix_norm=v_mix_norm, w_in=v_w_in,
               q_a_norm=v_q_a_norm, w_q_b=v_w_q_b, kv_a_norm=v_kv_a_norm, w_kv_b=v_w_kv_b, pool_w=v_pool_w, pool_scale=v_pool_scale,
               fox_b_f=v_fox_b_f, w_out=v_w_out, ffn2_norm=v_ffn2_norm, ffn2_w_gu=v_ffn2_w_gu, ffn2_w_down=v_ffn2_w_down,
               final_norm=v_final_norm)

    shard = {}
    for nm in BIG:
        w = _pad_w_in(params[nm]) if nm == "w_in" else params[nm]
        for l in range(DEPTH):
            shard[(nm, l)] = w[l].astype(BF16)
    first = [(nm, 0) for nm in BIG if not nm.startswith("ffn2")]
    rest = [k for k in shard if k not in first]
    wts = {nm: [None] * DEPTH for nm in WEIGHT_VIEWS}
    wts["pool_w"] = params["pool_w"]
    got = _gather_blocking([shard[k] for k in first])
    _prepare_weights(dict(zip(first, got)), wts)
    sems, src_thru, land_thru, token = _gather_start([shard[k] for k in rest], got[0])
    sm = dict(params)
    sm["ffn1_norm"] = params["ffn1_norm"] + token[0, 0]

    def late_weights(x2):
        lands = _gather_forward(_gather_wait(sems, src_thru, land_thru, x2))
        _prepare_weights(dict(zip(rest, lands)), wts)

    pos = _position()
    flight = {}

    def reduce_to_chips(l, g, split):
        full = [_chip_major(nm, g[nm]) for nm in BIG]
        sib = _reduce_stage1(full)
        psum = [_sum2_bf16(pos, f, sb, name=f"chip_sum_{l}_{t}") for t, (f, sb) in enumerate(zip(full, sib))]
        return full, sib, (_reduce_stage2_start(psum) if split else _reduce_stage2(psum))

    def layer_done(l, g, sm_now):
        if l == 0:
            return sm_now
        flight[l] = reduce_to_chips(l, g, True)
        sm_next = dict(sm_now)
        sm_next["ffn2_norm"] = sm_now["ffn2_norm"] + flight[l][2][3][0, 0]
        return sm_next

    loss, dx, grads, g_final = _local_step(x[0], loss_target[0], wts, sm, late_weights, layer_done)

    reduced = {}
    for l in range(DEPTH):
        if l in flight:
            full, sib, (sems2, ps_thru, lands2, _) = flight[l]
            recv = _reduce_stage2_wait(sems2, ps_thru, lands2, dx)
        else:
            full, sib, recv = reduce_to_chips(l, grads[l], False)
        for t, nm in enumerate(BIG):
            reduced[(nm, l)] = _sum5(pos, full[t], sib[t], recv[t], name=f"grad_sum_{l}_{t}")
    order = [(nm, l) for nm in BIG for l in range(DEPTH)]
    whole = dict(zip(order, _reduce_stage3([reduced[k] for k in order])))
    big_g = {nm: jnp.stack([whole[(nm, l)] for l in range(DEPTH)]) for nm in BIG}
    big_g["w_in"] = _unpad_w_in(big_g["w_in"])
    small_g, loss = _unpack_small(_allreduce_small(_pack_small(grads, g_final, loss)), params)
    gw = {**big_g, **small_g}

    delta, new_m, new_v = {}, {}, {}
    for nm in WEIGHTS:
        delta[nm], new_m[nm], new_v[nm] = _update(nm, params[nm], gw[nm], mom[nm], var[nm])
    return (loss, dx[None], *[gw[n] for n in WEIGHTS], *[delta[n] for n in WEIGHTS], *[new_m[n] for n in WEIGHTS],
            *[new_v[n] for n in WEIGHTS])
```

```python
import functools
import math

import jax
import jax.numpy as jnp
import numpy as np
from jax import lax
from jax.experimental import pallas as pl
from jax.experimental.pallas import tpu as pltpu

F32 = jnp.float32
BF16 = jnp.bfloat16
MESH = pl.DeviceIdType.MESH
HBM_SPEC = pl.BlockSpec(memory_space=pltpu.HBM)

D = 1024
DEPTH = 2
D_FF = 2816
FF_SHARD = 1408
N_CHIPS = 4
H = 6
NOPE, ROPE, VDIM = 64, 32, 64
HALF_ROPE = ROPE // 2
Q_RANK, KV_RANK = 256, 128
POOL_W = 256
FOX_D = 64
N_IN = 1830
NZ = 2048
ROPE_THETA = 10000.0
EPS = 1e-6
POOL_HALO = 16
Z_QA, Z_KVA, Z_KR, Z_POOL, Z_FOX, Z_F = 0, 256, 384, 512, 768, 1920

ADAM_LR, ADAM_B1, ADAM_B2, ADAM_EPS, ADAM_WD, ADAM_STEP = 0.001, 0.9, 0.999, 1e-08, 0.01, 10

VMEM_LIMIT_V7X = 56 * 1024 * 1024


def _cp(sem=None, vmem=VMEM_LIMIT_V7X):
    return pltpu.CompilerParams(dimension_semantics=sem, vmem_limit_bytes=vmem)


def _sigmoid(x):
    return 1.0 / (1.0 + jnp.exp(-x))


def _dot(a, b, dims):
    return lax.dot_general(a, b, (dims, ((), ())), preferred_element_type=F32)


NN = ((1,), (0,))
NT = ((1,), (1,))
TN = ((0,), (0,))


def _mm(a, b, mode, *, name, out_dtype=F32, add=None, alpha=None, tm=512, tn=512, tk=512, n_major_out=False):
    if mode == "nn":
        (m, k), (k2, n) = a.shape, b.shape
    elif mode == "nt":
        (m, k), (n, k2) = a.shape, b.shape
    else:
        (k, m), (k2, n) = a.shape, b.shape
    assert k == k2
    tm, tn, tk = min(tm, m), min(tn, n), min(tk, k)
    assert m % tm == 0 and n % tn == 0 and k % tk == 0, (name, m, n, k, tm, tn, tk)
    nk = k // tk
    dims = {"nn": NN, "nt": NT, "tn": TN}[mode]
    a_spec = pl.BlockSpec((tk, tm), lambda i, j, kk: (kk, i)) if mode == "tn" else pl.BlockSpec((tm, tk), lambda i, j, kk: (i, kk))
    b_spec = pl.BlockSpec((tn, tk), lambda i, j, kk: (j, kk)) if mode == "nt" else pl.BlockSpec((tk, tn), lambda i, j, kk: (kk, j))
    in_specs = [a_spec, b_spec]
    args = [a, b]
    if add is not None:
        in_specs.append(pl.BlockSpec((tm, tn), lambda i, j, kk: (i, j)))
        args.append(add)
    if n_major_out:
        out_shape = jax.ShapeDtypeStruct((n // tn, m, tn), out_dtype)
        out_spec = pl.BlockSpec((None, tm, tn), lambda i, j, kk: (j, i, 0))
    else:
        out_shape = jax.ShapeDtypeStruct((m, n), out_dtype)
        out_spec = pl.BlockSpec((tm, tn), lambda i, j, kk: (i, j))

    def body(*refs):
        a_ref, b_ref = refs[0], refs[1]
        add_ref = refs[2] if add is not None else None
        o_ref, acc = refs[-2], refs[-1]
        kk = pl.program_id(2)

        @pl.when(kk == 0)
        def _():
            acc[...] = jnp.zeros_like(acc)

        acc[...] += _dot(a_ref[...].astype(BF16), b_ref[...].astype(BF16), dims)

        @pl.when(kk == nk - 1)
        def _():
            r = acc[...]
            if alpha is not None:
                r = r * alpha
            if add_ref is not None:
                r = r + add_ref[...].astype(F32)
            o_ref[...] = r.astype(out_dtype)

    return pl.pallas_call(
        body, name=name, grid=(m // tm, n // tn, nk), in_specs=in_specs, out_specs=out_spec, out_shape=out_shape,
        scratch_shapes=[pltpu.VMEM((tm, tn), F32)],
        compiler_params=_cp(("parallel", "parallel", "arbitrary")),
    )(*args)


def _norm_mm(x, col_block, gain, w, *, name, tm=512):
    s = x.shape[0]
    k, n = w.shape
    tm = min(tm, s)

    def body(x_ref, g_ref, w_ref, z_ref, h_ref):
        xv = x_ref[...]
        r = lax.rsqrt(jnp.mean(xv * xv, axis=-1, keepdims=True) + EPS)
        hv = (xv * r * g_ref[...]).astype(BF16)
        h_ref[...] = hv
        z_ref[...] = _dot(hv, w_ref[...], NN)

    return pl.pallas_call(
        body, name=name, grid=(s // tm,),
        in_specs=[pl.BlockSpec((tm, k), lambda i: (i, col_block)), pl.BlockSpec((1, k), lambda i: (0, 0)),
                  pl.BlockSpec((k, n), lambda i: (0, 0))],
        out_specs=[pl.BlockSpec((tm, n), lambda i: (i, 0)), pl.BlockSpec((tm, k), lambda i: (i, 0))],
        out_shape=[jax.ShapeDtypeStruct((s, n), F32), jax.ShapeDtypeStruct((s, k), BF16)],
        compiler_params=_cp(("parallel",)),
    )(x, gain.reshape(1, k), w)


def _rmsnorm_bwd(x, col_block, gain, dh, dres=None, *, name, tm=512):
    s = x.shape[0]
    k = gain.shape[-1]
    tm = min(tm, s)

    def body(*refs):
        x_ref, g_ref, dh_ref = refs[0], refs[1], refs[2]
        dres_ref = refs[3] if dres is not None else None
        dx_ref, dg_ref = refs[-2], refs[-1]
        xv = x_ref[...]
        r = lax.rsqrt(jnp.mean(xv * xv, axis=-1, keepdims=True) + EPS)
        dhv = dh_ref[...].astype(F32)
        a = dhv * g_ref[...]
        dx = r * a - xv * (r * r * r) * jnp.mean(a * xv, axis=-1, keepdims=True)
        if dres_ref is not None:
            dx = dx + dres_ref[...]
        dx_ref[...] = dx

        @pl.when(pl.program_id(0) == 0)
        def _():
            dg_ref[...] = jnp.zeros_like(dg_ref)

        dg_ref[...] += jnp.sum(dhv * xv * r, axis=0, keepdims=True)

    in_specs = [pl.BlockSpec((tm, k), lambda i: (i, col_block)), pl.BlockSpec((1, k), lambda i: (0, 0)),
                pl.BlockSpec((tm, k), lambda i: (i, 0))]
    args = [x, gain.reshape(1, k), dh]
    if dres is not None:
        in_specs.append(pl.BlockSpec((tm, k), lambda i: (i, 0)))
        args.append(dres)
    dx, dg = pl.pallas_call(
        body, name=name, grid=(s // tm,), in_specs=in_specs,
        out_specs=[pl.BlockSpec((tm, k), lambda i: (i, 0)), pl.BlockSpec((1, k), lambda i: (0, 0))],
        out_shape=[jax.ShapeDtypeStruct((s, k), F32), jax.ShapeDtypeStruct((1, k), F32)],
        compiler_params=_cp(("arbitrary",)),
    )(*args)
    return dx, dg.reshape(k)


def _ffn_fwd(x, gain, w_gu4, w_d2, *, name, tm=256):
    s = x.shape[0]
    tm = min(tm, s)

    def body(x_ref, g_ref, wgu_ref, wd_ref, xo_ref, gu_ref):
        xv = x_ref[...]
        r = lax.rsqrt(jnp.mean(xv * xv, axis=-1, keepdims=True) + EPS)
        hv = (xv * r * g_ref[...]).astype(BF16)
        y = jnp.zeros((tm, D), F32)
        for j in range(2):
            g = _dot(hv, wgu_ref[j], NN)
            u = _dot(hv, wgu_ref[2 + j], NN)
            gu_ref[:, j * FF_SHARD:(j + 1) * FF_SHARD] = g.astype(BF16)
            gu_ref[:, D_FF + j * FF_SHARD:D_FF + (j + 1) * FF_SHARD] = u.astype(BF16)
            act = (g * _sigmoid(g) * u).astype(BF16)
            y = y + _dot(act, wd_ref[j], NN)
        xo_ref[...] = xv + 0.5 * y

    return pl.pallas_call(
        body, name=name, grid=(s // tm,),
        in_specs=[pl.BlockSpec((tm, D), lambda i: (i, 0)), pl.BlockSpec((1, D), lambda i: (0, 0)),
                  pl.BlockSpec((N_CHIPS, D, FF_SHARD), lambda i: (0, 0, 0), pipeline_mode=pl.Buffered(1)),
                  pl.BlockSpec((2, FF_SHARD, D), lambda i: (0, 0, 0), pipeline_mode=pl.Buffered(1))],
        out_specs=[pl.BlockSpec((tm, D), lambda i: (i, 0)), pl.BlockSpec((tm, 2 * D_FF), lambda i: (i, 0))],
        out_shape=[jax.ShapeDtypeStruct((s, D), F32), jax.ShapeDtypeStruct((s, 2 * D_FF), BF16)],
        compiler_params=_cp(("parallel",)),
    )(x, gain.reshape(1, D), w_gu4, w_d2)


def _ffn_bwd(x, dxo, gu, gain, w_gu4, w_d2, *, name, tm=256):
    s = x.shape[0]
    tm = min(tm, s)

    def body(x_ref, dxo_ref, gu_ref, g_ref, wgu_ref, wd_ref, dx_ref, dgu_ref, act_ref, h_ref, dy_ref, dg_ref):
        xv = x_ref[...]
        r = lax.rsqrt(jnp.mean(xv * xv, axis=-1, keepdims=True) + EPS)
        xh = xv * r
        h_ref[...] = (xh * g_ref[...]).astype(BF16)
        dxov = dxo_ref[...]
        dy = (0.5 * dxov).astype(BF16)
        dy_ref[...] = dy
        dh = jnp.zeros((tm, D), F32)
        for j in range(2):
            g = gu_ref[:, j * FF_SHARD:(j + 1) * FF_SHARD].astype(F32)
            u = gu_ref[:, D_FF + j * FF_SHARD:D_FF + (j + 1) * FF_SHARD].astype(F32)
            sg = _sigmoid(g)
            silu = g * sg
            act_ref[:, j * FF_SHARD:(j + 1) * FF_SHARD] = (silu * u).astype(BF16)
            dact = _dot(dy, wd_ref[j], NT)
            dg = (dact * u * (sg * (1.0 + g * (1.0 - sg)))).astype(BF16)
            du = (dact * silu).astype(BF16)
            dgu_ref[:, j * FF_SHARD:(j + 1) * FF_SHARD] = dg
            dgu_ref[:, D_FF + j * FF_SHARD:D_FF + (j + 1) * FF_SHARD] = du
            dh = dh + _dot(dg, wgu_ref[j], NT) + _dot(du, wgu_ref[2 + j], NT)
        a = dh * g_ref[...]
        dx_ref[...] = dxov + r * a - xh * (r * jnp.mean(a * xh, axis=-1, keepdims=True))

        @pl.when(pl.program_id(0) == 0)
        def _():
            dg_ref[...] = jnp.zeros_like(dg_ref)

        dg_ref[...] += jnp.sum(dh * xh, axis=0, keepdims=True)

    row = lambda i: (i, 0)
    outs = pl.pallas_call(
        body, name=name, grid=(s // tm,),
        in_specs=[pl.BlockSpec((tm, D), row), pl.BlockSpec((tm, D), row), pl.BlockSpec((tm, 2 * D_FF), row),
                  pl.BlockSpec((1, D), lambda i: (0, 0)),
                  pl.BlockSpec((N_CHIPS, D, FF_SHARD), lambda i: (0, 0, 0), pipeline_mode=pl.Buffered(1)),
                  pl.BlockSpec((2, FF_SHARD, D), lambda i: (0, 0, 0), pipeline_mode=pl.Buffered(1))],
        out_specs=[pl.BlockSpec((tm, D), row), pl.BlockSpec((tm, 2 * D_FF), row), pl.BlockSpec((tm, D_FF), row),
                   pl.BlockSpec((tm, D), row), pl.BlockSpec((tm, D), row), pl.BlockSpec((1, D), lambda i: (0, 0))],
        out_shape=[jax.ShapeDtypeStruct((s, D), F32), jax.ShapeDtypeStruct((s, 2 * D_FF), BF16),
                   jax.ShapeDtypeStruct((s, D_FF), BF16), jax.ShapeDtypeStruct((s, D), BF16),
                   jax.ShapeDtypeStruct((s, D), BF16), jax.ShapeDtypeStruct((1, D), F32)],
        compiler_params=_cp(("arbitrary",)),
    )(x, dxo, gu, gain.reshape(1, D), w_gu4, w_d2)
    dx, dgu, act, h, dy, dg = outs
    return dx, dgu, act, h, dy, dg.reshape(D)


DA = 128
SCALE_MLA = 1.0 / math.sqrt(NOPE + ROPE)
SCALE_FOX = 1.0 / math.sqrt(FOX_D)


def _causal_blocks(nb, key_major):
    if key_major:
        pairs = [(i, j) for j in range(nb) for i in range(j, nb)]
    else:
        pairs = [(i, j) for i in range(nb) for j in range(i + 1)]
    return (jnp.asarray(np.array([p[0] for p in pairs], np.int32)), jnp.asarray(np.array([p[1] for p in pairs], np.int32)))


HEADS_PER_STEP = 2
ROW_CHUNK = 64

def _col_to_row(col):
    return jnp.broadcast_to(col, (col.shape[0], DA)).T[0:1, :]


def _attn_fwd(qa, ka, va, dv, *, name, t=512):
    h, s, _ = qa.shape
    t = min(t, s)
    nb = s // t
    g = 3
    qi, kj = _causal_blocks(nb, key_major=False)

    rc = min(ROW_CHUNK, t)

    def body(qi_ref, kj_ref, q_ref, k_ref, v_ref, o_ref, lse_ref, m_sc, acc_sc, p_sc, a_sc):
        n = pl.program_id(1)
        i, j = qi_ref[n], kj_ref[n]

        @pl.when(j == 0)
        def _():
            m_sc[...] = jnp.full_like(m_sc, -jnp.inf)
            acc_sc[...] = jnp.zeros_like(acc_sc)

        def step(masked):
            scs = [_dot(q_ref[hh], k_ref[hh], NT) for hh in range(g)]
            for r0 in range(0, t, rc):
                rows = slice(r0, r0 + rc)
                for hh in range(g):
                    sr = scs[hh][rows]
                    if masked:
                        row = lax.broadcasted_iota(jnp.int32, (rc, t), 0) + r0
                        col = lax.broadcasted_iota(jnp.int32, (rc, t), 1)
                        sr = jnp.where(col <= row, sr, -jnp.inf)
                    m_old = m_sc[hh, rows]
                    m_new = jnp.maximum(m_old, jnp.max(sr, axis=-1, keepdims=True))
                    p_sc[hh, rows] = jnp.exp(sr - m_new).astype(BF16)
                    a_sc[hh, rows] = jnp.exp(m_old - m_new)
                    m_sc[hh, rows] = m_new
            for hh in range(g):
                acc_sc[hh] = a_sc[hh] * acc_sc[hh] + _dot(p_sc[hh], v_ref[hh], NN)

        @pl.when(j < i)
        def _():
            step(False)

        @pl.when(j == i)
        def _():
            step(True)
            for hh in range(g):
                acc = acc_sc[hh]
                l = acc[:, dv:dv + 1]
                o_ref[hh] = acc[:, :dv] / l
                lse_ref[hh] = _col_to_row(m_sc[hh] + jnp.log(l))

    qmap = lambda hg, n, qi_r, kj_r: (hg, qi_r[n], 0)
    kmap = lambda hg, n, qi_r, kj_r: (hg, kj_r[n], 0)
    return pl.pallas_call(
        body, name=name,
        grid_spec=pltpu.PrefetchScalarGridSpec(
            num_scalar_prefetch=2, grid=(h // g, qi.shape[0]),
            in_specs=[pl.BlockSpec((g, t, DA), qmap), pl.BlockSpec((g, t, DA), kmap), pl.BlockSpec((g, t, DA), kmap)],
            out_specs=[pl.BlockSpec((g, t, dv), qmap), pl.BlockSpec((g, 1, t), lambda hg, n, qi_r, kj_r: (hg, 0, qi_r[n]))],
            scratch_shapes=[pltpu.VMEM((g, t, 1), F32), pltpu.VMEM((g, t, DA), F32), pltpu.VMEM((g, t, t), BF16),
                            pltpu.VMEM((g, t, 1), F32)]),
        out_shape=[jax.ShapeDtypeStruct((h, s, dv), F32), jax.ShapeDtypeStruct((h, 1, s), F32)],
        compiler_params=_cp(("parallel", "arbitrary")),
    )(qi, kj, qa, ka, va)


def _attn_bwd(qa, ka, va, doa, lse_row, delta_row, decay, *, name, t=512):
    h, s, _ = qa.shape
    t = min(t, s)
    nb = s // t
    g = HEADS_PER_STEP
    rc = min(ROW_CHUNK, t)
    qi, kj = _causal_blocks(nb, key_major=True)
    nsteps = qi.shape[0]

    def body(*refs):
        qi_ref, kj_ref, q_ref, k_ref, v_ref, do_ref, lse_ref, dl_ref = refs[:8]
        p_sc, ds_sc = refs[-2:]
        if decay:
            dq_ref, dk_ref, dv_ref, dcq_ref, dck_ref, dq_acc, dk_acc, dv_acc, dcq_acc, dck_acc = refs[8:-2]
        else:
            dq_ref, dk_ref, dv_ref, dq_acc, dk_acc, dv_acc = refs[8:-2]
        n = pl.program_id(1)
        i, j = qi_ref[n], kj_ref[n]

        @pl.when(n == 0)
        def _():
            dq_acc[...] = jnp.zeros_like(dq_acc)
            if decay:
                dcq_acc[...] = jnp.zeros_like(dcq_acc)

        @pl.when(i == j)
        def _():
            dk_acc[...] = jnp.zeros_like(dk_acc)
            dv_acc[...] = jnp.zeros_like(dv_acc)
            if decay:
                dck_acc[...] = jnp.zeros_like(dck_acc)

        def step(masked):
            sts = [_dot(k_ref[hh], q_ref[hh], NT) for hh in range(g)]
            dpts = [_dot(v_ref[hh], do_ref[hh], NT) for hh in range(g)]
            dcq = [jnp.zeros((1, t), F32) for _ in range(g)]
            for r0 in range(0, t, rc):
                rows = slice(r0, r0 + rc)
                for hh in range(g):
                    st = sts[hh][rows]
                    if masked:
                        row = lax.broadcasted_iota(jnp.int32, (rc, t), 0) + r0
                        col = lax.broadcasted_iota(jnp.int32, (rc, t), 1)
                        st = jnp.where(row <= col, st, -jnp.inf)
                    pt = jnp.exp(st - lse_ref[hh])
                    dst = pt * (dpts[hh][rows] - dl_ref[hh])
                    p_sc[hh, rows] = pt.astype(BF16)
                    ds_sc[hh, rows] = dst.astype(BF16)
                    if decay:
                        dcq[hh] = dcq[hh] + jnp.sum(dst, axis=0, keepdims=True)
                        dck_acc[hh, rows] -= jnp.sum(dst, axis=1, keepdims=True)
            for hh in range(g):
                dv_acc[hh] += _dot(p_sc[hh], do_ref[hh], NN)
                dk_acc[hh] += _dot(ds_sc[hh], q_ref[hh], NN)
                dq_acc[hh, i] += _dot(ds_sc[hh], k_ref[hh], TN)
                if decay:
                    dcq_acc[hh, i] += dcq[hh]

        @pl.when(i > j)
        def _():
            step(False)

        @pl.when(i == j)
        def _():
            step(True)

        @pl.when(i == nb - 1)
        def _():
            dk_ref[...] = dk_acc[...]
            dv_ref[...] = dv_acc[...]
            if decay:
                for hh in range(g):
                    dck_ref[hh] = _col_to_row(dck_acc[hh])

        @pl.when(n == nsteps - 1)
        def _():
            dq_ref[...] = dq_acc[...]
            if decay:
                dcq_ref[...] = dcq_acc[...]

    kmap = lambda hg, n, qi_r, kj_r: (hg, kj_r[n], 0)
    qmap = lambda hg, n, qi_r, kj_r: (hg, qi_r[n], 0)
    qrow = lambda hg, n, qi_r, kj_r: (hg, 0, qi_r[n])
    krow = lambda hg, n, qi_r, kj_r: (hg, 0, kj_r[n])
    whole = lambda hg, n, qi_r, kj_r: (hg, 0, 0, 0)
    in_specs = [pl.BlockSpec((g, t, DA), qmap), pl.BlockSpec((g, t, DA), kmap), pl.BlockSpec((g, t, DA), kmap),
                pl.BlockSpec((g, t, DA), qmap), pl.BlockSpec((g, 1, t), qrow), pl.BlockSpec((g, 1, t), qrow)]
    out_specs = [pl.BlockSpec((g, nb, t, DA), whole), pl.BlockSpec((g, t, DA), kmap), pl.BlockSpec((g, t, DA), kmap)]
    out_shape = [jax.ShapeDtypeStruct((h, nb, t, DA), F32), jax.ShapeDtypeStruct((h, s, DA), F32), jax.ShapeDtypeStruct((h, s, DA), F32)]
    scratch = [pltpu.VMEM((g, nb, t, DA), F32), pltpu.VMEM((g, t, DA), F32), pltpu.VMEM((g, t, DA), F32)]
    if decay:
        out_specs += [pl.BlockSpec((g, nb, 1, t), whole), pl.BlockSpec((g, 1, t), krow)]
        out_shape += [jax.ShapeDtypeStruct((h, nb, 1, t), F32), jax.ShapeDtypeStruct((h, 1, s), F32)]
        scratch += [pltpu.VMEM((g, nb, 1, t), F32), pltpu.VMEM((g, t, 1), F32)]
    scratch += [pltpu.VMEM((g, t, t), BF16), pltpu.VMEM((g, t, t), BF16)]
    outs = pl.pallas_call(
        body, name=name,
        grid_spec=pltpu.PrefetchScalarGridSpec(num_scalar_prefetch=2, grid=(h // g, nsteps), in_specs=in_specs, out_specs=out_specs,
                                               scratch_shapes=scratch),
        out_shape=out_shape, compiler_params=_cp(("parallel", "arbitrary")),
    )(qi, kj, qa, ka, va, doa, lse_row, delta_row)
    outs = list(outs)
    outs[0] = outs[0].reshape(h, s, DA)
    if decay:
        outs[3] = outs[3].reshape(h, 1, s)
    return outs


def _sel(rows, cols, pairs, value=1.0):
    m = np.zeros((rows, cols), np.float32)
    for r, c in pairs:
        m[r, c] = value
    return jnp.asarray(m, BF16)


def _lane_row(lanes):
    m = np.zeros((1, DA), np.float32)
    m[0, list(lanes)] = 1.0
    return jnp.asarray(m)


def _rms(xv, gain):
    r = lax.rsqrt(jnp.mean(xv * xv, axis=-1, keepdims=True) + EPS)
    return xv * r * gain


def _mla_q_prep(z, gain, wq_a, wq_b, cq, sq, *, name, tm=512):
    s = z.shape[0]
    tm = min(tm, s)

    def body(z_ref, g_ref, wa_ref, wb_ref, c_ref, s_ref, qa_ref, qn_ref):
        qn = _rms(z_ref[...], g_ref[...]).astype(BF16)
        qn_ref[...] = qn
        c, sn = c_ref[...], s_ref[...]
        for hh in range(H):
            cols = slice(hh * DA, (hh + 1) * DA)
            qa_ref[hh] = (_dot(qn, wa_ref[:, cols], NN) * c + _dot(qn, wb_ref[:, cols], NN) * sn).astype(BF16)

    row = lambda i: (i, 0)
    fixed = lambda i: (0, 0)
    return pl.pallas_call(
        body, name=name, grid=(s // tm,),
        in_specs=[pl.BlockSpec((tm, Q_RANK), lambda i: (i, Z_QA // Q_RANK)), pl.BlockSpec((1, Q_RANK), fixed),
                  pl.BlockSpec((Q_RANK, H * DA), fixed), pl.BlockSpec((Q_RANK, H * DA), fixed),
                  pl.BlockSpec((tm, DA), row), pl.BlockSpec((tm, DA), row)],
        out_specs=[pl.BlockSpec((H, tm, DA), lambda i: (0, i, 0)), pl.BlockSpec((tm, Q_RANK), row)],
        out_shape=[jax.ShapeDtypeStruct((H, s, DA), BF16), jax.ShapeDtypeStruct((s, Q_RANK), BF16)],
        compiler_params=_cp(("parallel",)),
    )(z, gain.reshape(1, Q_RANK), wq_a, wq_b, cq, sq)


def _mla_kv_prep(z, gain, wk, wv, ck, sk, *, name, tm=512):
    s = z.shape[0]
    tm = min(tm, s)
    one = _lane_row([VDIM])

    def body(zkv_ref, z3_ref, z15_ref, g_ref, wk_ref, wv_ref, c_ref, s_ref, one_ref, ka_ref, va_ref, kvn_ref):
        kvn = _rms(zkv_ref[...], g_ref[...]).astype(BF16)
        kvn_ref[...] = kvn
        kpe = z3_ref[...] * c_ref[...] + z15_ref[...] * s_ref[...]
        for hh in range(H):
            cols = slice(hh * DA, (hh + 1) * DA)
            ka_ref[hh] = (_dot(kvn, wk_ref[:, cols], NN) + kpe).astype(BF16)
            va_ref[hh] = (_dot(kvn, wv_ref[:, cols], NN) + one_ref[...]).astype(BF16)

    row = lambda i: (i, 0)
    fixed = lambda i: (0, 0)
    blk = lambda c: pl.BlockSpec((tm, DA), lambda i: (i, c))
    heads = pl.BlockSpec((H, tm, DA), lambda i: (0, i, 0))
    return pl.pallas_call(
        body, name=name, grid=(s // tm,),
        in_specs=[blk(Z_KVA // DA), blk(Z_KR // DA), blk(Z_F // DA), pl.BlockSpec((1, KV_RANK), fixed),
                  pl.BlockSpec((KV_RANK, H * DA), fixed), pl.BlockSpec((KV_RANK, H * DA), fixed),
                  pl.BlockSpec((tm, DA), row), pl.BlockSpec((tm, DA), row), pl.BlockSpec((1, DA), fixed)],
        out_specs=[heads, heads, pl.BlockSpec((tm, KV_RANK), row)],
        out_shape=[jax.ShapeDtypeStruct((H, s, DA), BF16), jax.ShapeDtypeStruct((H, s, DA), BF16),
                   jax.ShapeDtypeStruct((s, KV_RANK), BF16)],
        compiler_params=_cp(("parallel",)),
    )(z, z, z, gain.reshape(1, KV_RANK), wk, wv, ck, sk, one)


DEC_C = (FOX_D, FOX_D + 1, FOX_D + 2)
DEC_1 = (FOX_D + 3, FOX_D + 4, FOX_D + 5)


def _fox_prep(z, c3t, *, name, tm=512):
    s = z.shape[0]
    tm = min(tm, s)
    w = H * FOX_D
    left = [(r, r) for r in range(FOX_D)]
    right = [(FOX_D + r, r) for r in range(FOX_D)]
    pq = jnp.stack([_sel(DA, DA, left, SCALE_FOX), _sel(DA, DA, right, SCALE_FOX)])
    pk = jnp.stack([_sel(DA, DA, left), _sel(DA, DA, right)])
    pcq = jnp.stack([_sel(32, DA, [(hh + 8 * k, DEC_C[k]) for k in range(3)]) for hh in range(H)])
    pck = jnp.stack([_sel(32, DA, [(hh + 8 * k, DEC_1[k]) for k in range(3)], -1.0) for hh in range(H)])
    rows3 = jnp.concatenate([_lane_row(DEC_1), _lane_row(DEC_C), _lane_row([FOX_D])], axis=0)

    def body(zq_ref, zk_ref, zv_ref, c_ref, pq_ref, pk_ref, pcq_ref, pck_ref, r_ref, qa_ref, ka_ref, va_ref):
        c3 = c_ref[...]
        for pair in range(H // 2):
            lanes = slice(pair * DA, (pair + 1) * DA)
            zq, zk, zv = zq_ref[:, lanes].astype(BF16), zk_ref[:, lanes].astype(BF16), zv_ref[:, lanes].astype(BF16)
            for side in range(2):
                hh = 2 * pair + side
                qa_ref[hh] = (_dot(zq, pq_ref[side], NN) + _dot(c3, pcq_ref[hh], TN) + r_ref[0:1, :]).astype(BF16)
                ka_ref[hh] = (_dot(zk, pk_ref[side], NN) + _dot(c3, pck_ref[hh], TN) + r_ref[1:2, :]).astype(BF16)
                va_ref[hh] = (_dot(zv, pk_ref[side], NN) + r_ref[2:3, :]).astype(BF16)

    fixed2 = lambda i: (0, 0)
    fixed3 = lambda i: (0, 0, 0)
    heads = pl.BlockSpec((H, tm, DA), lambda i: (0, i, 0))
    zblk = lambda c: pl.BlockSpec((tm, w), lambda i: (i, c))
    return pl.pallas_call(
        body, name=name, grid=(s // tm,),
        in_specs=[zblk(Z_FOX // w), zblk(Z_FOX // w + 1), zblk(Z_FOX // w + 2), pl.BlockSpec((32, tm), lambda i: (0, i)),
                  pl.BlockSpec((2, DA, DA), fixed3), pl.BlockSpec((2, DA, DA), fixed3),
                  pl.BlockSpec((H, 32, DA), fixed3), pl.BlockSpec((H, 32, DA), fixed3), pl.BlockSpec((3, DA), fixed2)],
        out_specs=[heads, heads, heads], out_shape=[jax.ShapeDtypeStruct((H, s, DA), BF16)] * 3,
        compiler_params=_cp(("parallel",)),
    )(z, z, z, c3t, pq, pk, pcq, pck, rows3)


def _mix_out(oa, yb, oc, w_out, x1, *, name, tm=512):
    s = yb.shape[0]
    tm = min(tm, s)
    e2 = jnp.stack([_sel(VDIM, DA, [(r, r) for r in range(VDIM)]), _sel(VDIM, DA, [(r, VDIM + r) for r in range(VDIM)])])

    def body(oa_ref, yb_ref, oc_ref, e_ref, w_ref, x_ref, x2_ref, cat_ref):
        def pairs(o_ref):
            return [(_dot(o_ref[2 * p].astype(BF16), e_ref[0], NN) + _dot(o_ref[2 * p + 1].astype(BF16), e_ref[1], NN)).astype(BF16)
                    for p in range(H // 2)]

        cat = jnp.concatenate(pairs(oa_ref) + [yb_ref[...].astype(BF16)] + pairs(oc_ref), axis=1)
        cat_ref[...] = cat
        x2_ref[...] = x_ref[...] + _dot(cat, w_ref[...], NN)

    row = lambda i: (i, 0)
    heads = pl.BlockSpec((H, tm, VDIM), lambda i: (0, i, 0))
    return pl.pallas_call(
        body, name=name, grid=(s // tm,),
        in_specs=[heads, pl.BlockSpec((tm, POOL_W), row), heads, pl.BlockSpec((2, VDIM, DA), lambda i: (0, 0, 0)),
                  pl.BlockSpec((D, D), lambda i: (0, 0)), pl.BlockSpec((tm, D), row)],
        out_specs=[pl.BlockSpec((tm, D), row), pl.BlockSpec((tm, D), row)],
        out_shape=[jax.ShapeDtypeStruct((s, D), F32), jax.ShapeDtypeStruct((s, D), BF16)],
        compiler_params=_cp(("parallel",)),
    )(oa, yb, oc, e2, w_out, x1)


def _mix_out_bwd(dx2b, w_out, oa, oc, *, name, tm=512):
    s = dx2b.shape[0]
    tm = min(tm, s)
    f2 = jnp.stack([_sel(DA, DA, [(r, r) for r in range(VDIM)]), _sel(DA, DA, [(VDIM + r, r) for r in range(VDIM)])])
    nv = H * VDIM

    def body(dx_ref, w_ref, oa_ref, oc_ref, f_ref, doa_ref, doc_ref, dyb_ref, dla_ref, dlc_ref):
        dcat = _dot(dx_ref[...], w_ref[...], NT)
        dyb_ref[...] = dcat[:, nv:nv + POOL_W]
        for base, o_ref, do_ref, dl_ref in ((0, oa_ref, doa_ref, dla_ref), (nv + POOL_W, oc_ref, doc_ref, dlc_ref)):
            for p in range(H // 2):
                blk = dcat[:, base + p * DA:base + (p + 1) * DA].astype(BF16)
                for side in range(2):
                    hh = 2 * p + side
                    do = _dot(blk, f_ref[side], NN)
                    do_ref[hh] = do.astype(BF16)
                    dl_ref[hh] = _col_to_row(jnp.sum(do[:, :VDIM] * o_ref[hh], axis=-1, keepdims=True))

    row = lambda i: (i, 0)
    heads = lambda w: pl.BlockSpec((H, tm, w), lambda i: (0, i, 0))
    return pl.pallas_call(
        body, name=name, grid=(s // tm,),
        in_specs=[pl.BlockSpec((tm, D), row), pl.BlockSpec((D, D), lambda i: (0, 0)), heads(VDIM), heads(VDIM),
                  pl.BlockSpec((2, DA, DA), lambda i: (0, 0, 0))],
        out_specs=[heads(DA), heads(DA), pl.BlockSpec((tm, POOL_W), row),
                   pl.BlockSpec((H, 1, tm), lambda i: (0, 0, i)), pl.BlockSpec((H, 1, tm), lambda i: (0, 0, i))],
        out_shape=[jax.ShapeDtypeStruct((H, s, DA), BF16), jax.ShapeDtypeStruct((H, s, DA), BF16),
                   jax.ShapeDtypeStruct((s, POOL_W), F32), jax.ShapeDtypeStruct((H, 1, s), F32), jax.ShapeDtypeStruct((H, 1, s), F32)],
        compiler_params=_cp(("parallel",)),
    )(dx2b, w_out, oa, oc, f2)


def _mla_bwd_prep(dqa, dka, dva, dft, cq, sq, ck, sk, *, name, tm=512):
    s = dqa.shape[1]
    tm = min(tm, s)
    keep = _lane_row(range(NOPE))

    def body(dq_ref, dk_ref, dv_ref, dft_ref, cq_ref, sq_ref, ck_ref, sk_ref, keep_ref, dqab_ref, dkv_ref, dz3_ref, dz15_ref):
        cqv, sqv = cq_ref[...], sq_ref[...]
        dkpe = jnp.zeros((tm, DA), F32)
        for hh in range(H):
            lanes = slice(hh * DA, (hh + 1) * DA)
            dq = dq_ref[hh]
            dqab_ref[:, lanes] = (dq * cqv).astype(BF16)
            dqab_ref[:, H * DA + hh * DA:H * DA + (hh + 1) * DA] = (dq * sqv).astype(BF16)
            dk = dk_ref[hh]
            dkpe = dkpe + dk
            dkv_ref[:, lanes] = (dk * keep_ref[...]).astype(BF16)
            dkv_ref[:, H * DA + hh * DA:H * DA + (hh + 1) * DA] = (dv_ref[hh] * keep_ref[...]).astype(BF16)
        dz3_ref[...] = (dkpe * ck_ref[...]).astype(BF16)
        dz15_ref[...] = (dkpe * sk_ref[...] + dft_ref[...]).astype(BF16)

    row = lambda i: (i, 0)
    heads = pl.BlockSpec((H, tm, DA), lambda i: (0, i, 0))
    tab = pl.BlockSpec((tm, DA), row)
    return pl.pallas_call(
        body, name=name, grid=(s // tm,),
        in_specs=[heads, heads, heads, tab, tab, tab, tab, tab, pl.BlockSpec((1, DA), lambda i: (0, 0))],
        out_specs=[pl.BlockSpec((tm, 2 * H * DA), row), pl.BlockSpec((tm, 2 * H * DA), row), tab, tab],
        out_shape=[jax.ShapeDtypeStruct((s, 2 * H * DA), BF16), jax.ShapeDtypeStruct((s, 2 * H * DA), BF16),
                   jax.ShapeDtypeStruct((s, DA), BF16), jax.ShapeDtypeStruct((s, DA), BF16)],
        compiler_params=_cp(("parallel",)),
    )(dqa, dka, dva, dft, cq, sq, ck, sk, keep)


def _fox_bwd_prep(dfqa, dfka, dfva, *, name, tm=512):
    s = dfqa.shape[1]
    tm = min(tm, s)
    place = lambda v: jnp.stack([_sel(DA, DA, [(r, r) for r in range(FOX_D)], v), _sel(DA, DA, [(r, FOX_D + r) for r in range(FOX_D)], v)])
    gq, gk = place(SCALE_FOX), place(1.0)

    def body(dq_ref, dk_ref, dv_ref, gq_ref, gk_ref, dz_ref):
        for part, (d_ref, g_ref) in enumerate(((dq_ref, gq_ref), (dk_ref, gk_ref), (dv_ref, gk_ref))):
            for p in range(H // 2):
                blk = _dot(d_ref[2 * p].astype(BF16), g_ref[0], NN) + _dot(d_ref[2 * p + 1].astype(BF16), g_ref[1], NN)
                lo = part * H * FOX_D + p * DA
                dz_ref[:, lo:lo + DA] = blk.astype(BF16)

    heads = pl.BlockSpec((H, tm, DA), lambda i: (0, i, 0))
    sel = pl.BlockSpec((2, DA, DA), lambda i: (0, 0, 0))
    return pl.pallas_call(
        body, name=name, grid=(s // tm,), in_specs=[heads, heads, heads, sel, sel],
        out_specs=pl.BlockSpec((tm, 3 * H * FOX_D), lambda i: (i, 0)),
        out_shape=jax.ShapeDtypeStruct((s, 3 * H * FOX_D), BF16), compiler_params=_cp(("parallel",)),
    )(dfqa, dfka, dfva, gq, gk)


def _lane_scan(x, s, reverse):
    lane = lax.broadcasted_iota(jnp.int32, x.shape, 1)
    sh = 1
    while sh < s:
        if reverse:
            x = x + jnp.where(lane < s - sh, pltpu.roll(x, s - sh, axis=1), 0.0)
        else:
            x = x + jnp.where(lane >= sh, pltpu.roll(x, sh, axis=1), 0.0)
        sh *= 2
    return x


def _gate_fwd(z, col_block, bias, *, name):
    s = z.shape[0]

    def body(z_ref, b_ref, f_ref, c_ref):
        ft = z_ref[...].T[0:8, :]
        f_ref[...] = ft
        xg = ft + b_ref[...]
        lf = jnp.minimum(xg, 0.0) - jnp.log(1.0 + jnp.exp(-jnp.abs(xg)))
        c = _lane_scan(lf, s, False)
        hi = c.astype(BF16).astype(F32)
        r = c - hi
        mid = r.astype(BF16).astype(F32)
        lo = r - mid
        c_ref[...] = jnp.concatenate([hi, mid, lo, jnp.zeros_like(hi)], axis=0).astype(BF16)

    return pl.pallas_call(
        body, name=name, grid=(1,),
        in_specs=[pl.BlockSpec((s, 128), lambda i: (0, col_block)), pl.BlockSpec((8, 1), lambda i: (0, 0))],
        out_specs=[pl.BlockSpec((8, s), lambda i: (0, 0)), pl.BlockSpec((32, s), lambda i: (0, 0))],
        out_shape=[jax.ShapeDtypeStruct((8, s), F32), jax.ShapeDtypeStruct((32, s), BF16)],
        compiler_params=_cp(("arbitrary",)))(z, bias)


def _gate_bwd(ft, bias, dc, *, name):
    s = ft.shape[1]

    def body(f_ref, b_ref, dc_ref, df_ref, db_ref):
        xg = f_ref[...] + b_ref[...]
        dlf = _lane_scan(dc_ref[...], s, True)
        df = dlf * _sigmoid(-xg)
        db_ref[...] = jnp.sum(df, axis=-1, keepdims=True)
        df_ref[...] = jnp.concatenate([df, jnp.zeros((DA - 8, s), F32)], axis=0).T

    return pl.pallas_call(body, name=name, out_shape=[jax.ShapeDtypeStruct((s, DA), F32), jax.ShapeDtypeStruct((8, 1), F32)],
                          compiler_params=_cp())(ft, bias, dc)


def _pool_lane_consts(tm, i):
    lane = lax.broadcasted_iota(jnp.int32, (tm, POOL_W), 1)
    tok = lax.broadcasted_iota(jnp.int32, (tm, POOL_W), 0) + i * tm
    win = jnp.where(lane < 64, 2, jnp.where(lane < 128, 4, jnp.where(lane < 192, 8, 16)))
    cnt = jnp.minimum(tok + 1, win).astype(F32)
    return lane, tok, cnt


def _pick_window(lane, s2, s4, s8, s16):
    return jnp.where(lane < 64, s2, jnp.where(lane < 128, s4, jnp.where(lane < 192, s8, s16)))


def _pool_fwd(z, col_block, bd, scale, *, name, tm=512):
    s = z.shape[0]
    tm = min(tm, s)
    hb = tm // POOL_HALO

    def body(u_ref, halo_ref, bd_ref, sc_ref, y_ref, p_ref, buf):
        i = pl.program_id(0)
        buf[0:POOL_HALO, :] = halo_ref[...] * (i > 0).astype(F32)
        buf[POOL_HALO:, :] = u_ref[...]

        def back(k):
            return buf[POOL_HALO - k:POOL_HALO - k + tm, :]

        u = u_ref[...]
        s2 = u + back(1)
        s4 = s2 + back(2) + back(3)
        s8 = s4 + back(4) + back(5) + back(6) + back(7)
        s16 = s8
        for k in range(8, 16):
            s16 = s16 + back(k)
        lane, _, cnt = _pool_lane_consts(tm, i)
        pooled = (_pick_window(lane, s2, s4, s8, s16) / cnt - u).astype(BF16)
        p_ref[...] = pooled
        y_ref[...] = _dot(pooled, bd_ref[...], NN) * sc_ref[...]

    return pl.pallas_call(
        body, name=name, grid=(s // tm,),
        in_specs=[pl.BlockSpec((tm, POOL_W), lambda i: (i, col_block)),
                  pl.BlockSpec((POOL_HALO, POOL_W), lambda i: (jnp.maximum(i * hb - 1, 0), col_block)),
                  pl.BlockSpec((POOL_W, POOL_W), lambda i: (0, 0)), pl.BlockSpec((1, POOL_W), lambda i: (0, 0))],
        out_specs=[pl.BlockSpec((tm, POOL_W), lambda i: (i, 0)), pl.BlockSpec((tm, POOL_W), lambda i: (i, 0))],
        out_shape=[jax.ShapeDtypeStruct((s, POOL_W), F32), jax.ShapeDtypeStruct((s, POOL_W), BF16)],
        scratch_shapes=[pltpu.VMEM((tm + POOL_HALO, POOL_W), F32)],
        compiler_params=_cp(("parallel",)),
    )(z, z, bd, scale.reshape(1, POOL_W))


def _pool_bwd_a(dy, pooled, bd, scale, *, name, tm=512):
    s = dy.shape[0]
    tm = min(tm, s)

    def body(dy_ref, p_ref, bd_ref, sc_ref, dq_ref, dys_ref, dsc_ref):
        i = pl.program_id(0)
        dyv = dy_ref[...]
        y0 = _dot(p_ref[...], bd_ref[...], NN)
        dys = (dyv * sc_ref[...]).astype(BF16)
        dys_ref[...] = dys
        dp = _dot(dys, bd_ref[...], NT)
        _, _, cnt = _pool_lane_consts(tm, i)
        dq_ref[:, 0:POOL_W] = dp / cnt
        dq_ref[:, POOL_W:] = dp

        @pl.when(i == 0)
        def _():
            dsc_ref[...] = jnp.zeros_like(dsc_ref)

        dsc_ref[...] += jnp.sum(dyv * y0, axis=0, keepdims=True)

    row = lambda i: (i, 0)
    dq, dys, dsc = pl.pallas_call(
        body, name=name, grid=(s // tm,),
        in_specs=[pl.BlockSpec((tm, POOL_W), row), pl.BlockSpec((tm, POOL_W), row),
                  pl.BlockSpec((POOL_W, POOL_W), lambda i: (0, 0)), pl.BlockSpec((1, POOL_W), lambda i: (0, 0))],
        out_specs=[pl.BlockSpec((tm, 2 * POOL_W), row), pl.BlockSpec((tm, POOL_W), row), pl.BlockSpec((1, POOL_W), lambda i: (0, 0))],
        out_shape=[jax.ShapeDtypeStruct((s, 2 * POOL_W), F32), jax.ShapeDtypeStruct((s, POOL_W), BF16),
                   jax.ShapeDtypeStruct((1, POOL_W), F32)],
        compiler_params=_cp(("arbitrary",)),
    )(dy, pooled, bd, scale.reshape(1, POOL_W))
    return dq, dys, dsc.reshape(POOL_W)


def _pool_bwd_b(dq, *, name, tm=512):
    s = dq.shape[0]
    tm = min(tm, s)
    hb = tm // POOL_HALO
    nblk = s // tm

    def body(q_ref, dp_ref, halo_ref, du_ref, buf):
        i = pl.program_id(0)
        buf[0:tm, :] = q_ref[...]
        buf[tm:, :] = halo_ref[...] * (i < nblk - 1).astype(F32)

        def ahead(k):
            return buf[k:k + tm, :]

        q = q_ref[...]
        s2 = q + ahead(1)
        s4 = s2 + ahead(2) + ahead(3)
        s8 = s4 + ahead(4) + ahead(5) + ahead(6) + ahead(7)
        s16 = s8
        for k in range(8, 16):
            s16 = s16 + ahead(k)
        lane = lax.broadcasted_iota(jnp.int32, (tm, POOL_W), 1)
        du_ref[...] = _pick_window(lane, s2, s4, s8, s16) - dp_ref[...]

    return pl.pallas_call(
        body, name=name, grid=(nblk,),
        in_specs=[pl.BlockSpec((tm, POOL_W), lambda i: (i, 0)), pl.BlockSpec((tm, POOL_W), lambda i: (i, 1)),
                  pl.BlockSpec((POOL_HALO, POOL_W), lambda i: (jnp.minimum((i + 1) * hb, nblk * hb - 1), 0))],
        out_specs=pl.BlockSpec((tm, POOL_W), lambda i: (i, 0)),
        out_shape=jax.ShapeDtypeStruct((s, POOL_W), F32),
        scratch_shapes=[pltpu.VMEM((tm + POOL_HALO, POOL_W), F32)],
        compiler_params=_cp(("parallel",)),
    )(dq, dq, dq)


def _loss_head(x, gain, target, *, name, tm=512):
    s = x.shape[0]
    tm = min(tm, s)

    def body(x_ref, g_ref, t_ref, dx_ref, dg_ref, loss_ref):
        xv = x_ref[...]
        r = lax.rsqrt(jnp.mean(xv * xv, axis=-1, keepdims=True) + EPS)
        xh = xv * r
        err = xh * g_ref[...] - t_ref[...]
        dy = err * (1.0 / D)
        a = dy * g_ref[...]
        dx_ref[...] = r * a - xh * (r * jnp.mean(a * xh, axis=-1, keepdims=True))

        @pl.when(pl.program_id(0) == 0)
        def _():
            dg_ref[...] = jnp.zeros_like(dg_ref)
            loss_ref[...] = jnp.zeros_like(loss_ref)

        dg_ref[...] += jnp.sum(dy * xh, axis=0, keepdims=True)
        part = 0.5 * jnp.sum(jnp.mean(err * err, axis=-1, keepdims=True), axis=0, keepdims=True)
        loss_ref[...] += jnp.broadcast_to(part, loss_ref.shape)

    row = lambda i: (i, 0)
    dx, dg, loss = pl.pallas_call(
        body, name=name, grid=(s // tm,),
        in_specs=[pl.BlockSpec((tm, D), row), pl.BlockSpec((1, D), lambda i: (0, 0)), pl.BlockSpec((tm, D), row)],
        out_specs=[pl.BlockSpec((tm, D), row), pl.BlockSpec((1, D), lambda i: (0, 0)), pl.BlockSpec((1, 128), lambda i: (0, 0))],
        out_shape=[jax.ShapeDtypeStruct((s, D), F32), jax.ShapeDtypeStruct((1, D), F32), jax.ShapeDtypeStruct((1, 128), F32)],
        compiler_params=_cp(("arbitrary",)),
    )(x, gain.reshape(1, D), target)
    return dx, dg.reshape(D), loss[0, 0]


def _adamw(w, g, m, v, *, name, tr=512):
    rows, cols = w.shape
    tr = min(tr, rows)
    assert rows % tr == 0, (name, rows, tr)
    c_m = 1.0 - ADAM_B1
    c_v = 1.0 - ADAM_B2
    bc1 = 1.0 - ADAM_B1 ** ADAM_STEP
    bc2 = 1.0 - ADAM_B2 ** ADAM_STEP

    def body(w_ref, g_ref, m_ref, v_ref, d_ref, mo_ref, vo_ref):
        gv = g_ref[...]
        mn = ADAM_B1 * m_ref[...] + c_m * gv
        vn = ADAM_B2 * v_ref[...] + c_v * (gv * gv)
        mo_ref[...] = mn
        vo_ref[...] = vn
        d_ref[...] = -ADAM_LR * ((mn / bc1) / (jnp.sqrt(vn / bc2) + ADAM_EPS) + ADAM_WD * w_ref[...])

    spec = pl.BlockSpec((tr, cols), lambda i: (i, 0))
    return pl.pallas_call(body, name=name, grid=(rows // tr,), in_specs=[spec] * 4, out_specs=[spec] * 3,
                          out_shape=[jax.ShapeDtypeStruct((rows, cols), F32)] * 3,
                          compiler_params=_cp(("parallel",)))(w, g, m, v)


def _row_tile(rows, cap):
    for t in range(min(cap, rows), 0, -1):
        if rows % t == 0 and t % 16 == 0:
            return t
    return rows


def _position():
    return jnp.stack([lax.axis_index("c"), 2 * lax.axis_index("x") + lax.axis_index("y")]).astype(jnp.int32)


def _sum2_bf16(pos, full, sib, *, name, tr=256):
    n, half, cols = sib.shape
    tr = _row_tile(half, tr)
    nb = half // tr

    def body(pos_ref, a_ref, b_ref, o_ref):
        o_ref[...] = (a_ref[...] + b_ref[...]).astype(BF16)

    spec = pl.BlockSpec((None, tr, cols), lambda j, i, p: (j, i, 0))
    return pl.pallas_call(
        body, name=name,
        grid_spec=pltpu.PrefetchScalarGridSpec(
            num_scalar_prefetch=1, grid=(n, nb),
            in_specs=[pl.BlockSpec((None, tr, cols), lambda j, i, p: (j, p[0] * nb + i, 0)), spec], out_specs=spec),
        out_shape=jax.ShapeDtypeStruct(sib.shape, BF16), compiler_params=_cp(("parallel", "parallel")))(pos, full, sib)


def _sum5(pos, full, sib, recv, *, name, tr=256):
    _, rows, cols = full.shape
    half = rows // 2
    tr = _row_tile(half, tr)
    nb = half // tr

    def body(pos_ref, a_ref, b_ref, r_ref, o_ref):
        acc = a_ref[...] + b_ref[...]
        for kk in range(3):
            acc = acc + r_ref[kk].astype(F32)
        o_ref[...] = acc

    return pl.pallas_call(
        body, name=name,
        grid_spec=pltpu.PrefetchScalarGridSpec(
            num_scalar_prefetch=1, grid=(nb,),
            in_specs=[pl.BlockSpec((None, tr, cols), lambda i, p: (p[1], p[0] * nb + i, 0)),
                      pl.BlockSpec((None, tr, cols), lambda i, p: (p[1], i, 0)),
                      pl.BlockSpec((3, tr, cols), lambda i, p: (0, i, 0))],
            out_specs=pl.BlockSpec((tr, cols), lambda i, p: (p[0] * nb + i, 0))),
        out_shape=jax.ShapeDtypeStruct((rows, cols), F32), compiler_params=_cp(("parallel",)))(pos, full, sib, recv)


def _place():
    x, y, c = lax.axis_index("x"), lax.axis_index("y"), lax.axis_index("c")
    chips = [(1 - x, y), (x, 1 - y), (1 - x, 1 - y)]
    return x, y, c, 2 * x + y, chips


SEM_SPEC = pl.BlockSpec(memory_space=pltpu.SEMAPHORE)
ANY_SPEC = pl.BlockSpec(memory_space=pl.ANY)


def _gather_copies(ins, outs, send_i, recv_i, send_o, recv_o):
    x, y, c, me, chips = _place()
    n = len(ins)
    started, awaited = [], []
    for t in range(n):
        half = ins[t].shape[0] // 2
        mine = pl.ds(c * half, half)
        started.append(pltpu.make_async_remote_copy(
            src_ref=ins[t], dst_ref=outs[t].at[me], send_sem=send_o.at[t], recv_sem=recv_o.at[t],
            device_id=(x, y, 1 - c), device_id_type=MESH))
        awaited.append(started[-1])
        for kk, (px, py) in enumerate(chips):
            started.append(pltpu.make_async_remote_copy(
                src_ref=ins[t].at[mine], dst_ref=outs[t].at[me, mine], send_sem=send_i.at[t * 3 + kk],
                recv_sem=recv_i.at[t * 3 + kk], device_id=(px, py, c), device_id_type=MESH))
            awaited.append(pltpu.make_async_remote_copy(
                src_ref=ins[t].at[mine], dst_ref=outs[t].at[2 * px + py, mine], send_sem=send_i.at[t * 3 + kk],
                recv_sem=recv_i.at[t * 3 + kk], device_id=(px, py, c), device_id_type=MESH))
    return started, awaited


def _forward_copies(outs, send_d, recv_d):
    x, y, c, me, chips = _place()
    started, awaited = [], []
    for t in range(len(outs)):
        half = outs[t].shape[1] // 2
        for kk, (px, py) in enumerate(chips):
            for lst, hc in ((started, c), (awaited, 1 - c)):
                blk = outs[t].at[2 * px + py, pl.ds(hc * half, half)]
                lst.append(pltpu.make_async_remote_copy(src_ref=blk, dst_ref=blk, send_sem=send_d.at[t * 3 + kk],
                                                        recv_sem=recv_d.at[t * 3 + kk], device_id=(x, y, 1 - c), device_id_type=MESH))
    return started, awaited


def _gather_blocking(shards):
    n = len(shards)

    def body(*refs):
        ins, outs = refs[:n], refs[n:2 * n]
        send_i, recv_i, send_d, recv_d, send_o, recv_o = refs[2 * n:]
        started, awaited = _gather_copies(ins, outs, send_i, recv_i, send_o, recv_o)
        for cp in started:
            cp.start()
        for cp in awaited:
            cp.wait_recv()
        fwd, fwd_in = _forward_copies(outs, send_d, recv_d)
        for cp in fwd:
            cp.start()
        for cp in fwd_in:
            cp.wait_recv()
        for cp in started + fwd:
            cp.wait_send()

    return pl.pallas_call(
        body, name="gather_first", in_specs=[HBM_SPEC] * n, out_specs=[HBM_SPEC] * n,
        out_shape=[jax.ShapeDtypeStruct((N_CHIPS,) + s.shape, s.dtype) for s in shards],
        scratch_shapes=[pltpu.SemaphoreType.DMA((3 * n,)), pltpu.SemaphoreType.DMA((3 * n,)),
                        pltpu.SemaphoreType.DMA((3 * n,)), pltpu.SemaphoreType.DMA((3 * n,)),
                        pltpu.SemaphoreType.DMA((n,)), pltpu.SemaphoreType.DMA((n,))],
    )(*shards)


def _gather_start(shards, after):
    n = len(shards)

    def body(*refs):
        ins = refs[:n]
        send_i, recv_i, send_o, recv_o = refs[2 * n + 1:2 * n + 5]
        outs = refs[3 * n + 5:4 * n + 5]
        token = refs[4 * n + 5]
        started, _ = _gather_copies(ins, outs, send_i, recv_i, send_o, recv_o)
        for cp in started:
            cp.start()
        token[...] = jnp.zeros_like(token)

    lands = [lax.empty((N_CHIPS,) + s.shape, s.dtype) for s in shards]
    sems = [pltpu.SemaphoreType.DMA((3 * n,)), pltpu.SemaphoreType.DMA((3 * n,)), pltpu.SemaphoreType.DMA((n,)), pltpu.SemaphoreType.DMA((n,))]
    res = pl.pallas_call(
        body, name="gather_rest_start",
        in_specs=[HBM_SPEC] * (2 * n) + [ANY_SPEC],
        out_specs=[SEM_SPEC] * 4 + [HBM_SPEC] * (2 * n) + [pl.BlockSpec(memory_space=pltpu.VMEM)],
        out_shape=sems + [jax.ShapeDtypeStruct(s.shape, s.dtype) for s in shards]
        + [jax.ShapeDtypeStruct(a.shape, a.dtype) for a in lands] + [jax.ShapeDtypeStruct((8, 128), F32)],
        input_output_aliases={t: 4 + t for t in range(2 * n)},
        compiler_params=pltpu.CompilerParams(has_side_effects=pltpu.SideEffectType.DATAFLOW_SIDE_EFFECTING),
    )(*[pltpu.with_memory_space_constraint(s, pltpu.HBM) for s in shards],
      *[pltpu.with_memory_space_constraint(a, pltpu.HBM) for a in lands], after)
    return res[:4], res[4:4 + n], res[4 + n:4 + 2 * n], res[-1]


def _gather_wait(sems, shards_thru, lands_thru, after):
    n = len(shards_thru)

    def body(*refs):
        ins, outs_in = refs[:n], refs[n:2 * n]
        send_i, recv_i, send_o, recv_o = refs[2 * n:2 * n + 4]
        started, awaited = _gather_copies(ins, outs_in, send_i, recv_i, send_o, recv_o)
        for cp in started:
            cp.wait_send()
        for cp in awaited:
            cp.wait_recv()

    res = pl.pallas_call(
        body, name="gather_rest_wait",
        in_specs=[HBM_SPEC] * (2 * n) + [SEM_SPEC] * 4 + [ANY_SPEC],
        out_specs=[HBM_SPEC] * (2 * n),
        out_shape=[jax.ShapeDtypeStruct(a.shape, a.dtype) for a in list(shards_thru) + list(lands_thru)],
        input_output_aliases={t: t for t in range(2 * n)},
        compiler_params=pltpu.CompilerParams(has_side_effects=pltpu.SideEffectType.DATAFLOW_SIDE_EFFECTING),
    )(*shards_thru, *lands_thru, *sems, after)
    return res[n:]


def _gather_forward(lands):
    n = len(lands)

    def body(*refs):
        outs = refs[n:2 * n]
        send_d, recv_d = refs[2 * n:]
        fwd, fwd_in = _forward_copies(outs, send_d, recv_d)
        for cp in fwd:
            cp.start()
        for cp in fwd_in:
            cp.wait_recv()
        for cp in fwd:
            cp.wait_send()

    return pl.pallas_call(
        body, name="gather_rest_forward", in_specs=[HBM_SPEC] * n, out_specs=[HBM_SPEC] * n,
        out_shape=[jax.ShapeDtypeStruct(a.shape, a.dtype) for a in lands],
        input_output_aliases={t: t for t in range(n)},
        scratch_shapes=[pltpu.SemaphoreType.DMA((3 * n,)), pltpu.SemaphoreType.DMA((3 * n,))],
    )(*lands)


def _reduce_stage1(grads):
    n = len(grads)

    def body(*refs):
        ins, sib = refs[:n], refs[n:2 * n]
        send, recv = refs[2 * n:]
        x, y, c, me, chips = _place()
        cps = []
        for t in range(n):
            rows = ins[t].shape[1] // 2
            cp = pltpu.make_async_remote_copy(
                src_ref=ins[t].at[:, pl.ds((1 - c) * rows, rows), :], dst_ref=sib[t], send_sem=send.at[t],
                recv_sem=recv.at[t], device_id=(x, y, 1 - c), device_id_type=MESH)
            cp.start()
            cps.append(cp)
        for cp in cps:
            cp.wait()

    return pl.pallas_call(
        body, name="reduce_stage1", in_specs=[HBM_SPEC] * n, out_specs=[HBM_SPEC] * n,
        out_shape=[jax.ShapeDtypeStruct((N_CHIPS, g.shape[1] // 2, g.shape[2]), F32) for g in grads],
        scratch_shapes=[pltpu.SemaphoreType.DMA((n,)), pltpu.SemaphoreType.DMA((n,))],
    )(*grads)


def _stage2_copies(ps, rcv, send, recv):
    x, y, c, me, chips = _place()
    return [pltpu.make_async_remote_copy(
        src_ref=ps[t].at[2 * px + py], dst_ref=rcv[t].at[kk], send_sem=send.at[t * 3 + kk],
        recv_sem=recv.at[t * 3 + kk], device_id=(px, py, c), device_id_type=MESH)
        for t in range(len(ps)) for kk, (px, py) in enumerate(chips)]


def _reduce_stage2(psum_bf16):
    n = len(psum_bf16)

    def body(*refs):
        cps = _stage2_copies(refs[:n], refs[n:2 * n], *refs[2 * n:])
        for cp in cps:
            cp.start()
        for cp in cps:
            cp.wait()

    return pl.pallas_call(
        body, name="reduce_stage2", in_specs=[HBM_SPEC] * n, out_specs=[HBM_SPEC] * n,
        out_shape=[jax.ShapeDtypeStruct((3,) + p.shape[1:], p.dtype) for p in psum_bf16],
        scratch_shapes=[pltpu.SemaphoreType.DMA((3 * n,)), pltpu.SemaphoreType.DMA((3 * n,))],
    )(*psum_bf16)


def _reduce_stage2_start(psum_bf16):
    n = len(psum_bf16)

    def body(*refs):
        ps = refs[:n]
        send, recv = refs[2 * n:2 * n + 2]
        rcv = refs[3 * n + 2:4 * n + 2]
        token = refs[4 * n + 2]
        for cp in _stage2_copies(ps, rcv, send, recv):
            cp.start()
        token[...] = jnp.zeros_like(token)

    lands = [lax.empty((3,) + p.shape[1:], p.dtype) for p in psum_bf16]
    res = pl.pallas_call(
        body, name="reduce_stage2_start",
        in_specs=[HBM_SPEC] * (2 * n),
        out_specs=[SEM_SPEC] * 2 + [HBM_SPEC] * (2 * n) + [pl.BlockSpec(memory_space=pltpu.VMEM)],
        out_shape=[pltpu.SemaphoreType.DMA((3 * n,)), pltpu.SemaphoreType.DMA((3 * n,))]
        + [jax.ShapeDtypeStruct(p.shape, p.dtype) for p in psum_bf16]
        + [jax.ShapeDtypeStruct(a.shape, a.dtype) for a in lands] + [jax.ShapeDtypeStruct((8, 128), F32)],
        input_output_aliases={t: 2 + t for t in range(2 * n)},
        compiler_params=pltpu.CompilerParams(has_side_effects=pltpu.SideEffectType.DATAFLOW_SIDE_EFFECTING),
    )(*[pltpu.with_memory_space_constraint(p, pltpu.HBM) for p in psum_bf16],
      *[pltpu.with_memory_space_constraint(a, pltpu.HBM) for a in lands])
    return res[:2], res[2:2 + n], res[2 + n:2 + 2 * n], res[-1]


def _reduce_stage2_wait(sems, ps_thru, lands_thru, after):
    n = len(ps_thru)

    def body(*refs):
        for cp in _stage2_copies(refs[:n], refs[n:2 * n], refs[2 * n], refs[2 * n + 1]):
            cp.wait()

    res = pl.pallas_call(
        body, name="reduce_stage2_wait",
        in_specs=[HBM_SPEC] * (2 * n) + [SEM_SPEC] * 2 + [ANY_SPEC],
        out_specs=[HBM_SPEC] * (2 * n),
        out_shape=[jax.ShapeDtypeStruct(a.shape, a.dtype) for a in list(ps_thru) + list(lands_thru)],
        input_output_aliases={t: t for t in range(2 * n)},
        compiler_params=pltpu.CompilerParams(has_side_effects=pltpu.SideEffectType.DATAFLOW_SIDE_EFFECTING),
    )(*ps_thru, *lands_thru, *sems, after)
    return res[n:]


def _reduce_stage3(reduced):
    n = len(reduced)

    def body(*refs):
        outs = refs[n:2 * n]
        send, recv = refs[2 * n:]
        x, y, c, me, chips = _place()
        cps = []
        for t in range(n):
            rows = outs[t].shape[0] // 2
            mine = outs[t].at[pl.ds(c * rows, rows), :]
            cp = pltpu.make_async_remote_copy(src_ref=mine, dst_ref=mine, send_sem=send.at[t], recv_sem=recv.at[t],
                                              device_id=(x, y, 1 - c), device_id_type=MESH)
            cp.start()
            cps.append(cp)
        for cp in cps:
            cp.wait()

    return pl.pallas_call(
        body, name="reduce_stage3", in_specs=[HBM_SPEC] * n, out_specs=[HBM_SPEC] * n,
        out_shape=[jax.ShapeDtypeStruct(r.shape, r.dtype) for r in reduced],
        input_output_aliases={t: t for t in range(n)},
        scratch_shapes=[pltpu.SemaphoreType.DMA((n,)), pltpu.SemaphoreType.DMA((n,))],
    )(*reduced)


def _allreduce_small(v):
    rows, cols = v.shape

    def body(v_ref, o_ref, buf, send, recv, loc):
        x, y, c, me, chips = _place()
        mine = 4 * x + 2 * y + c
        lc = pltpu.make_async_copy(v_ref, buf.at[mine], loc)
        lc.start()
        peers = []
        for fx in range(2):
            for fy in range(2):
                for fc in range(2):
                    if fx or fy or fc:
                        peers.append((fx, fy, fc))
        cps = []
        for kk, (fx, fy, fc) in enumerate(peers):
            to = (x ^ fx, y ^ fy, c ^ fc)
            cp = pltpu.make_async_remote_copy(src_ref=v_ref, dst_ref=buf.at[mine], send_sem=send.at[kk], recv_sem=recv.at[kk],
                                              device_id=to, device_id_type=MESH)
            cp.start()
            cps.append((cp, to))
        for kk, (cp, to) in enumerate(cps):
            src = 4 * to[0] + 2 * to[1] + to[2]
            pltpu.make_async_remote_copy(src_ref=v_ref, dst_ref=buf.at[src], send_sem=send.at[kk], recv_sem=recv.at[kk],
                                         device_id=to, device_id_type=MESH).wait_recv()
        for cp, _ in cps:
            cp.wait_send()
        lc.wait()
        acc = buf[0]
        for d in range(1, 8):
            acc = acc + buf[d]
        o_ref[...] = acc

    return pl.pallas_call(
        body, name="allreduce_small", in_specs=[pl.BlockSpec(memory_space=pltpu.VMEM)],
        out_specs=pl.BlockSpec(memory_space=pltpu.VMEM), out_shape=jax.ShapeDtypeStruct((rows, cols), F32),
        scratch_shapes=[pltpu.VMEM((8, rows, cols), F32), pltpu.SemaphoreType.DMA((7,)), pltpu.SemaphoreType.DMA((7,)),
                        pltpu.SemaphoreType.DMA],
        compiler_params=pltpu.CompilerParams(vmem_limit_bytes=VMEM_LIMIT_V7X),
    )(v)


def _pad_w_in(w):
    z = lambda n: jnp.zeros(w.shape[:-1] + (n,), w.dtype)
    return jnp.concatenate([w[..., 0:384], z(64), w[..., 384:416], z(32), w[..., 416:1824],
                            w[..., 1824:1830], z(58), w[..., 400:416], w[..., 384:400], z(32)], axis=-1)


def _unpad_w_in(g):
    x1 = g[..., 448:464] + g[..., Z_F + 80:Z_F + 96]
    x2 = g[..., 464:480] + g[..., Z_F + 64:Z_F + 80]
    return jnp.concatenate([g[..., 0:384], x1, x2, g[..., 512:1920], g[..., 1920:1926]], axis=-1)


def _block_diag(pw):
    out = jnp.zeros((POOL_W, POOL_W), pw.dtype)
    for g in range(4):
        out = out.at[g * 64:(g + 1) * 64, g * 64:(g + 1) * 64].set(pw[g])
    return out


def _rope_tables(s):
    inv_freq = ROPE_THETA ** (-jnp.arange(0, ROPE, 2, dtype=F32) / ROPE)
    ang = jnp.arange(s, dtype=jnp.int32).astype(F32)[:, None] * inv_freq[None, :]
    cos, sin = jnp.cos(ang), jnp.sin(ang)
    zero = lambda n: jnp.zeros((s, n), F32)
    ck = jnp.concatenate([zero(NOPE), cos, cos, zero(DA - NOPE - ROPE)], axis=1)
    sk = jnp.concatenate([zero(NOPE), -sin, sin, zero(DA - NOPE - ROPE)], axis=1)
    cq = jnp.concatenate([jnp.ones((s, NOPE), F32), cos, cos, zero(DA - NOPE - ROPE)], axis=1) * SCALE_MLA
    return dict(cq=cq, sq=sk * SCALE_MLA, ck=ck, sk=sk)


def _mix_fwd(l, x1, wts, sm, tabs):
    z, h2 = _norm_mm(x1, 0, sm["mix_norm"][l], wts["w_in"][l], name=f"mix_in_{l}")
    qa, qn = _mla_q_prep(z, sm["q_a_norm"][l], wts["wq_a"][l], wts["wq_b"][l], tabs["cq"], tabs["sq"], name=f"mla_q_{l}")
    ka, va, kvn = _mla_kv_prep(z, sm["kv_a_norm"][l], wts["wk"][l], wts["wv"][l], tabs["ck"], tabs["sk"], name=f"mla_kv_{l}")
    oa, lse_a = _attn_fwd(qa, ka, va, VDIM, name=f"mla_attn_{l}")

    bd = _block_diag(wts["pool_w"][l]).astype(BF16)
    yb, pooled = _pool_fwd(z, Z_POOL // POOL_W, bd, sm["pool_scale"][l], name=f"pool_{l}")

    fb = jnp.pad(sm["fox_b_f"][l], (0, 8 - H)).reshape(8, 1)
    ft, c3t = _gate_fwd(z, Z_F // DA, fb, name=f"fox_gate_{l}")
    fqa, fka, fva = _fox_prep(z, c3t, name=f"fox_prep_{l}")
    oc, lse_c = _attn_fwd(fqa, fka, fva, FOX_D, name=f"fox_attn_{l}")

    x2, cat = _mix_out(oa, yb, oc, wts["w_out"][l], x1, name=f"mix_out_{l}")
    saved = dict(z=z, h2=h2, qn=qn, kvn=kvn, qa=qa, ka=ka, va=va, oa=oa, lse_a=lse_a, bd=bd, pooled=pooled,
                 fqa=fqa, fka=fka, fva=fva, ft=ft, fb=fb, oc=oc, lse_c=lse_c, cat=cat)
    return x2, saved


def _mix_bwd(l, x1, dx2, sv, wts, sm, tabs):
    s = x1.shape[0]
    g = {}
    dx2b = dx2.astype(BF16)
    g["w_out"] = _mm(sv["cat"], dx2b, "tn", name=f"d_w_out_{l}", tm=1024, tn=1024)
    doa, doc, dyb, dl_a, dl_c = _mix_out_bwd(dx2b, wts["w_out"][l], sv["oa"], sv["oc"], name=f"mix_out_bwd_{l}")

    dfqa, dfka, dfva, dcq, dck = _attn_bwd(sv["fqa"], sv["fka"], sv["fva"], doc, sv["lse_c"], dl_c, True, name=f"fox_attn_bwd_{l}")
    dfox = _fox_bwd_prep(dfqa, dfka, dfva, name=f"fox_bwd_prep_{l}")
    dc = jnp.pad(dcq.reshape(H, s) + dck.reshape(H, s), ((0, 8 - H), (0, 0)))
    dft, dfb = _gate_bwd(sv["ft"], sv["fb"], dc, name=f"fox_gate_bwd_{l}")
    g["fox_b_f"] = dfb[:H, 0]

    dq, dys, g["pool_scale"] = _pool_bwd_a(dyb, sv["pooled"], sv["bd"], sm["pool_scale"][l], name=f"pool_bwd_a_{l}")
    du = _pool_bwd_b(dq, name=f"pool_bwd_b_{l}")
    dbd = _mm(sv["pooled"], dys, "tn", name=f"d_pool_w_{l}")
    g["pool_w"] = jnp.stack([dbd[i * 64:(i + 1) * 64, i * 64:(i + 1) * 64] for i in range(4)])

    dqa_, dka_, dva_ = _attn_bwd(sv["qa"], sv["ka"], sv["va"], doa, sv["lse_a"], dl_a, False, name=f"mla_attn_bwd_{l}")
    dqab, dkv, dz3, dz15 = _mla_bwd_prep(dqa_, dka_, dva_, dft, tabs["cq"], tabs["sq"], tabs["ck"], tabs["sk"],
                                         name=f"mla_bwd_prep_{l}")
    wq_ab = jnp.concatenate([wts["wq_a"][l], wts["wq_b"][l]], axis=1)
    wkv = jnp.concatenate([wts["wk"][l], wts["wv"][l]], axis=1)
    dwq = _mm(sv["qn"], dqab, "tn", name=f"d_w_q_b_{l}", tn=768).reshape(Q_RANK, 2, H, DA)
    dwkv = _mm(sv["kvn"], dkv, "tn", name=f"d_w_kv_b_{l}", tn=768).reshape(KV_RANK, 2, H, DA)
    da, db = dwq[:, 0], dwq[:, 1]
    swapped = jnp.concatenate([jnp.zeros((Q_RANK, H, NOPE), F32), db[..., NOPE + HALF_ROPE:NOPE + ROPE],
                               db[..., NOPE:NOPE + HALF_ROPE]], axis=-1)
    g["w_q_b"] = (da[..., :NOPE + ROPE] + swapped).reshape(Q_RANK, H * (NOPE + ROPE))
    g["w_kv_b"] = jnp.concatenate([dwkv[:, 0, :, :NOPE], dwkv[:, 1, :, :VDIM]], axis=-1).reshape(KV_RANK, H * (NOPE + VDIM))
    dqn = _mm(dqab, wq_ab, "nt", name=f"d_qn_{l}", tk=768)
    dkvn = _mm(dkv, wkv, "nt", name=f"d_kvn_{l}", tk=768)
    dqa, g["q_a_norm"] = _rmsnorm_bwd(sv["z"], Z_QA // Q_RANK, sm["q_a_norm"][l], dqn, name=f"q_a_norm_bwd_{l}")
    dkva, g["kv_a_norm"] = _rmsnorm_bwd(sv["z"], Z_KVA // KV_RANK, sm["kv_a_norm"][l], dkvn, name=f"kv_a_norm_bwd_{l}")

    dz = jnp.concatenate([dqa.astype(BF16), dkva.astype(BF16), dz3, du.astype(BF16), dfox, dz15], axis=1)
    g["w_in"] = _mm(sv["h2"], dz, "tn", name=f"d_w_in_{l}", tm=1024, tn=1024)
    dh2 = _mm(dz, wts["w_in"][l], "nt", name=f"d_h2_{l}", tn=1024, tk=1024)
    dx1, g["mix_norm"] = _rmsnorm_bwd(x1, 0, sm["mix_norm"][l], dh2, dx2, name=f"mix_norm_bwd_{l}")
    return dx1, g


def _local_step(x, target, wts, sm, late_weights=None, layer_done=None):
    s = x.shape[0]
    tabs = _rope_tables(s)
    acts = []
    xs = x
    for l in range(DEPTH):
        x1, gu1 = _ffn_fwd(xs, sm["ffn1_norm"][l], wts["ffn1_w_gu"][l], wts["ffn1_w_d2"][l], name=f"ffn1_fwd_{l}")
        x2, sv = _mix_fwd(l, x1, wts, sm, tabs)
        if l == 0 and late_weights is not None:
            late_weights(x2)
        x3, gu2 = _ffn_fwd(x2, sm["ffn2_norm"][l], wts["ffn2_w_gu"][l], wts["ffn2_w_d2"][l], name=f"ffn2_fwd_{l}")
        acts.append((xs, gu1, x1, sv, x2, gu2))
        xs = x3
    dx, g_final, loss = _loss_head(xs, sm["final_norm"], target, name="loss_head")
    grads = [dict() for _ in range(DEPTH)]
    for l in reversed(range(DEPTH)):
        x0, gu1, x1, sv, x2, gu2 = acts[l]
        g = grads[l]
        dx, dgu, act, hh, dy, g["ffn2_norm"] = _ffn_bwd(x2, dx, gu2, sm["ffn2_norm"][l], wts["ffn2_w_gu"][l], wts["ffn2_w_d2"][l],
                                                        name=f"ffn2_bwd_{l}")
        g["ffn2_w_down"] = _mm(act, dy, "tn", name=f"d_ffn2_w_down_{l}", tm=FF_SHARD, tn=1024)
        g["ffn2_w_gu"] = _mm(hh, dgu, "tn", name=f"d_ffn2_w_gu_{l}", tm=1024, tn=FF_SHARD, n_major_out=True)
        dx, gm = _mix_bwd(l, x1, dx, sv, wts, sm, tabs)
        g.update(gm)
        dx, dgu, act, hh, dy, g["ffn1_norm"] = _ffn_bwd(x0, dx, gu1, sm["ffn1_norm"][l], wts["ffn1_w_gu"][l], wts["ffn1_w_d2"][l],
                                                        name=f"ffn1_bwd_{l}")
        g["ffn1_w_down"] = _mm(act, dy, "tn", name=f"d_ffn1_w_down_{l}", tm=FF_SHARD, tn=1024)
        g["ffn1_w_gu"] = _mm(hh, dgu, "tn", name=f"d_ffn1_w_gu_{l}", tm=1024, tn=FF_SHARD, n_major_out=True)
        if layer_done is not None:
            sm = layer_done(l, g, sm)
    return loss, dx, grads, g_final


BIG = ["ffn1_w_gu", "ffn1_w_down", "w_in", "w_q_b", "w_kv_b", "w_out", "ffn2_w_gu", "ffn2_w_down"]
SMALL = ["ffn1_norm", "mix_norm", "q_a_norm", "kv_a_norm", "pool_w", "pool_scale", "fox_b_f", "ffn2_norm"]
SMALL_ROWS = 48


WEIGHT_VIEWS = ["ffn1_w_gu", "ffn1_w_d2", "w_in", "wq_a", "wq_b", "wk", "wv", "w_out", "ffn2_w_gu", "ffn2_w_d2"]


def _prepare_weights(gathered, wts):
    for (nm, l), w in gathered.items():
        if nm in ("ffn1_w_gu", "ffn2_w_gu"):
            wts[nm][l] = w
        elif nm in ("ffn1_w_down", "ffn2_w_down"):
            wts[nm[:5] + "w_d2"][l] = w.reshape(2, FF_SHARD, D)
        elif nm in ("w_in", "w_out"):
            wts[nm][l] = w.reshape(D, -1)
        elif nm == "w_q_b":
            wq = jnp.moveaxis(w, 0, 1).reshape(Q_RANK, H, NOPE + ROPE)
            zq = lambda n: jnp.zeros((Q_RANK, H, n), BF16)
            wts["wq_a"][l] = jnp.concatenate([wq, zq(DA - NOPE - ROPE)], axis=-1).reshape(Q_RANK, H * DA)
            wts["wq_b"][l] = jnp.concatenate([zq(NOPE), wq[..., NOPE + HALF_ROPE:], wq[..., NOPE:NOPE + HALF_ROPE],
                                              zq(DA - NOPE - ROPE)], axis=-1).reshape(Q_RANK, H * DA)
        else:
            wkv = jnp.moveaxis(w, 0, 1).reshape(KV_RANK, H, NOPE + VDIM)
            zk = jnp.zeros((KV_RANK, H, DA - NOPE), BF16)
            wts["wk"][l] = jnp.concatenate([wkv[..., :NOPE], zk], axis=-1).reshape(KV_RANK, H * DA)
            wts["wv"][l] = jnp.concatenate([wkv[..., NOPE:], zk], axis=-1).reshape(KV_RANK, H * DA)


def _chip_major(name, g):
    if name in ("ffn1_w_gu", "ffn2_w_gu"):
        return g
    if name in ("ffn1_w_down", "ffn2_w_down", "w_in", "w_out"):
        return g.reshape(N_CHIPS, g.shape[0] // N_CHIPS, g.shape[1])
    return jnp.moveaxis(g.reshape(g.shape[0], N_CHIPS, g.shape[1] // N_CHIPS), 1, 0)


def _pack_small(grads, g_final, loss):
    parts = []
    for l in range(DEPTH):
        for nm in SMALL:
            parts.append(grads[l][nm].reshape(-1))
    parts.append(g_final.reshape(-1))
    parts.append(loss.reshape(1))
    flat = jnp.concatenate(parts)
    return jnp.pad(flat, (0, SMALL_ROWS * D - flat.shape[0])).reshape(SMALL_ROWS, D)


def _unpack_small(packed, params):
    flat = packed.reshape(-1)
    out = {nm: [] for nm in SMALL}
    off = 0
    for l in range(DEPTH):
        for nm in SMALL:
            shp = params[nm].shape[1:]
            n = int(np.prod(shp))
            out[nm].append(flat[off:off + n].reshape(shp))
            off += n
    res = {nm: jnp.stack(v) for nm, v in out.items()}
    res["final_norm"] = flat[off:off + D]
    return res, flat[off + D]


def _update(name, w, g, m, v):
    shp = w.shape
    if w.ndim == 1:
        view = (1, shp[0])
    elif w.size <= 65536:
        view = (shp[0], w.size // shp[0])
    else:
        view = (w.size // shp[-1], shp[-1])
    tr = view[0]
    for cand in (512, 352, 256, 128):
        if view[0] % cand == 0 and view[0] > cand:
            tr = cand
            break
    d, mn, vn = _adamw(w.reshape(view), g.reshape(view), m.reshape(view), v.reshape(view), name="adamw_" + name, tr=tr)
    return d.reshape(shp), mn.reshape(shp), vn.reshape(shp)


WEIGHTS = ['ffn1_norm', 'ffn1_w_gu', 'ffn1_w_down', 'mix_norm', 'w_in', 'q_a_norm', 'w_q_b', 'kv_a_norm', 'w_kv_b', 'pool_w',
           'pool_scale', 'fox_b_f', 'w_out', 'ffn2_norm', 'ffn2_w_gu', 'ffn2_w_down', 'final_norm']


def kernel(x, ffn1_norm, ffn1_w_gu, ffn1_w_down, mix_norm, w_in, q_a_norm, w_q_b, kv_a_norm, w_kv_b, pool_w, pool_scale, fox_b_f, w_out, ffn2_norm, ffn2_w_gu, ffn2_w_down, final_norm, loss_target, m_ffn1_norm, m_ffn1_w_gu, m_ffn1_w_down, m_mix_norm, m_w_in, m_q_a_norm, m_w_q_b, m_kv_a_norm, m_w_kv_b, m_pool_w, m_pool_scale, m_fox_b_f, m_w_out, m_ffn2_norm, m_ffn2_w_gu, m_ffn2_w_down, m_final_norm, v_ffn1_norm, v_ffn1_w_gu, v_ffn1_w_down, v_mix_norm, v_w_in, v_q_a_norm, v_w_q_b, v_kv_a_norm, v_w_kv_b, v_pool_w, v_pool_scale, v_fox_b_f, v_w_out, v_ffn2_norm, v_ffn2_w_gu, v_ffn2_w_down, v_final_norm):
    params = dict(ffn1_norm=ffn1_norm, ffn1_w_gu=ffn1_w_gu, ffn1_w_down=ffn1_w_down, mix_norm=mix_norm, w_in=w_in, q_a_norm=q_a_norm,
                  w_q_b=w_q_b, kv_a_norm=kv_a_norm, w_kv_b=w_kv_b, pool_w=pool_w, pool_scale=pool_scale, fox_b_f=fox_b_f, w_out=w_out,
                  ffn2_norm=ffn2_norm, ffn2_w_gu=ffn2_w_gu, ffn2_w_down=ffn2_w_down, final_norm=final_norm)
    mom = dict(ffn1_norm=m_ffn1_norm, ffn1_w_gu=m_ffn1_w_gu, ffn1_w_down=m_ffn1_w_down, mix_norm=m_mix_norm, w_in=m_w_in,
               q_a_norm=m_q_a_norm, w_q_b=m_w_q_b, kv_a_norm=m_kv_a_norm, w_kv_b=m_w_kv_b, pool_w=m_pool_w, pool_scale=m_pool_scale,
               fox_b_f=m_fox_b_f, w_out=m_w_out, ffn2_norm=m_ffn2_norm, ffn2_w_gu=m_ffn2_w_gu, ffn2_w_down=m_ffn2_w_down,
               final_norm=m_final_norm)
    var = dict(ffn1_norm=v_ffn1_norm, ffn1_w_gu=v_ffn1_w_gu, ffn1_w_down=v_ffn1_w_down, mix_norm=v_mix_norm, w_in=v_w_in,
               q_a_norm=v_q_a_norm, w_q_b=v_w_q_b, kv_a_norm=v_kv_a_norm, w_kv_b=v_w_kv_b, pool_w=v_pool_w, pool_scale=v_pool_scale,
               fox_b_f=v_fox_b_f, w_out=v_w_out, ffn2_norm=v_ffn2_norm, ffn2_w_gu=v_ffn2_w_gu, ffn2_w_down=v_ffn2_w_down,
               final_norm=v_final_norm)

    shard = {}
    for nm in BIG:
        w = _pad_w_in(params[nm]) if nm == "w_in" else params[nm]
        for l in range(DEPTH):
            shard[(nm, l)] = w[l].astype(BF16)
    first = [(nm, 0) for nm in BIG if not nm.startswith("ffn2")]
    rest = [k for k in shard if k not in first]
    wts = {nm: [None] * DEPTH for nm in WEIGHT_VIEWS}
    wts["pool_w"] = params["pool_w"]
    got = _gather_blocking([shard[k] for k in first])
    _prepare_weights(dict(zip(first, got)), wts)
    sems, src_thru, land_thru, token = _gather_start([shard[k] for k in rest], got[0])
    sm = dict(params)
    sm["ffn1_norm"] = params["ffn1_norm"] + token[0, 0]

    def late_weights(x2):
        lands = _gather_forward(_gather_wait(sems, src_thru, land_thru, x2))
        _prepare_weights(dict(zip(rest, lands)), wts)

    pos = _position()
    flight = {}

    def reduce_to_chips(l, g, split):
        full = [_chip_major(nm, g[nm]) for nm in BIG]
        sib = _reduce_stage1(full)
        psum = [_sum2_bf16(pos, f, sb, name=f"chip_sum_{l}_{t}") for t, (f, sb) in enumerate(zip(full, sib))]
        return full, sib, (_reduce_stage2_start(psum) if split else _reduce_stage2(psum))

    def layer_done(l, g, sm_now):
        if l == 0:
            return sm_now
        flight[l] = reduce_to_chips(l, g, True)
        sm_next = dict(sm_now)
        sm_next["ffn2_norm"] = sm_now["ffn2_norm"] + flight[l][2][3][0, 0]
        return sm_next

    loss, dx, grads, g_final = _local_step(x[0], loss_target[0], wts, sm, late_weights, layer_done)

    reduced = {}
    for l in range(DEPTH):
        if l in flight:
            full, sib, (sems2, ps_thru, lands2, _) = flight[l]
            recv = _reduce_stage2_wait(sems2, ps_thru, lands2, dx)
        else:
            full, sib, recv = reduce_to_chips(l, grads[l], False)
        for t, nm in enumerate(BIG):
            reduced[(nm, l)] = _sum5(pos, full[t], sib[t], recv[t], name=f"grad_sum_{l}_{t}")
    order = [(nm, l) for nm in BIG for l in range(DEPTH)]
    whole = dict(zip(order, _reduce_stage3([reduced[k] for k in order])))
    big_g = {nm: jnp.stack([whole[(nm, l)] for l in range(DEPTH)]) for nm in BIG}
    big_g["w_in"] = _unpad_w_in(big_g["w_in"])
    small_g, loss = _unpack_small(_allreduce_small(_pack_small(grads, g_final, loss)), params)
    gw = {**big_g, **small_g}

    delta, new_m, new_v = {}, {}, {}
    for nm in WEIGHTS:
        delta[nm], new_m[nm], new_v[nm] = _update(nm, params[nm], gw[nm], mom[nm], var[nm])
    return (loss, dx[None], *[gw[n] for n in WEIGHTS], *[delta[n] for n in WEIGHTS], *[new_m[n] for n in WEIGHTS],
            *[new_v[n] for n in WEIGHTS])
```

```python
import functools
import math

import jax
import jax.numpy as jnp
import numpy as np
from jax import lax
from jax.experimental import pallas as pl
from jax.experimental.pallas import tpu as pltpu

F32 = jnp.float32
BF16 = jnp.bfloat16
MESH = pl.DeviceIdType.MESH
HBM_SPEC = pl.BlockSpec(memory_space=pltpu.HBM)

D = 1024
DEPTH = 2
D_FF = 2816
FF_SHARD = 1408
N_CHIPS = 4
H = 6
NOPE, ROPE, VDIM = 64, 32, 64
HALF_ROPE = ROPE // 2
Q_RANK, KV_RANK = 256, 128
POOL_W = 256
FOX_D = 64
N_IN = 1830
NZ = 2048
ROPE_THETA = 10000.0
EPS = 1e-6
POOL_HALO = 16
Z_QA, Z_KVA, Z_KR, Z_POOL, Z_FOX, Z_F = 0, 256, 384, 512, 768, 1920

ADAM_LR, ADAM_B1, ADAM_B2, ADAM_EPS, ADAM_WD, ADAM_STEP = 0.001, 0.9, 0.999, 1e-08, 0.01, 10

VMEM_LIMIT_V7X = 56 * 1024 * 1024


def _cp(sem=None, vmem=VMEM_LIMIT_V7X):
    return pltpu.CompilerParams(dimension_semantics=sem, vmem_limit_bytes=vmem)


def _sigmoid(x):
    return 1.0 / (1.0 + jnp.exp(-x))


def _dot(a, b, dims):
    return lax.dot_general(a, b, (dims, ((), ())), preferred_element_type=F32)


NN = ((1,), (0,))
NT = ((1,), (1,))
TN = ((0,), (0,))


def _mm(a, b, mode, *, name, out_dtype=F32, add=None, alpha=None, tm=512, tn=512, tk=512, n_major_out=False):
    if mode == "nn":
        (m, k), (k2, n) = a.shape, b.shape
    elif mode == "nt":
        (m, k), (n, k2) = a.shape, b.shape
    else:
        (k, m), (k2, n) = a.shape, b.shape
    assert k == k2
    tm, tn, tk = min(tm, m), min(tn, n), min(tk, k)
    assert m % tm == 0 and n % tn == 0 and k % tk == 0, (name, m, n, k, tm, tn, tk)
    nk = k // tk
    dims = {"nn": NN, "nt": NT, "tn": TN}[mode]
    a_spec = pl.BlockSpec((tk, tm), lambda i, j, kk: (kk, i)) if mode == "tn" else pl.BlockSpec((tm, tk), lambda i, j, kk: (i, kk))
    b_spec = pl.BlockSpec((tn, tk), lambda i, j, kk: (j, kk)) if mode == "nt" else pl.BlockSpec((tk, tn), lambda i, j, kk: (kk, j))
    in_specs = [a_spec, b_spec]
    args = [a, b]
    if add is not None:
        in_specs.append(pl.BlockSpec((tm, tn), lambda i, j, kk: (i, j)))
        args.append(add)
    if n_major_out:
        out_shape = jax.ShapeDtypeStruct((n // tn, m, tn), out_dtype)
        out_spec = pl.BlockSpec((None, tm, tn), lambda i, j, kk: (j, i, 0))
    else:
        out_shape = jax.ShapeDtypeStruct((m, n), out_dtype)
        out_spec = pl.BlockSpec((tm, tn), lambda i, j, kk: (i, j))

    def body(*refs):
        a_ref, b_ref = refs[0], refs[1]
        add_ref = refs[2] if add is not None else None
        o_ref, acc = refs[-2], refs[-1]
        kk = pl.program_id(2)

        @pl.when(kk == 0)
        def _():
            acc[...] = jnp.zeros_like(acc)

        acc[...] += _dot(a_ref[...].astype(BF16), b_ref[...].astype(BF16), dims)

        @pl.when(kk == nk - 1)
        def _():
            r = acc[...]
            if alpha is not None:
                r = r * alpha
            if add_ref is not None:
                r = r + add_ref[...].astype(F32)
            o_ref[...] = r.astype(out_dtype)

    return pl.pallas_call(
        body, name=name, grid=(m // tm, n // tn, nk), in_specs=in_specs, out_specs=out_spec, out_shape=out_shape,
        scratch_shapes=[pltpu.VMEM((tm, tn), F32)],
        compiler_params=_cp(("parallel", "parallel", "arbitrary")),
    )(*args)


def _norm_mm(x, col_block, gain, w, *, name, tm=512):
    s = x.shape[0]
    k, n = w.shape
    tm = min(tm, s)

    def body(x_ref, g_ref, w_ref, z_ref, h_ref):
        xv = x_ref[...]
        r = lax.rsqrt(jnp.mean(xv * xv, axis=-1, keepdims=True) + EPS)
        hv = (xv * r * g_ref[...]).astype(BF16)
        h_ref[...] = hv
        z_ref[...] = _dot(hv, w_ref[...], NN)

    return pl.pallas_call(
        body, name=name, grid=(s // tm,),
        in_specs=[pl.BlockSpec((tm, k), lambda i: (i, col_block)), pl.BlockSpec((1, k), lambda i: (0, 0)),
                  pl.BlockSpec((k, n), lambda i: (0, 0))],
        out_specs=[pl.BlockSpec((tm, n), lambda i: (i, 0)), pl.BlockSpec((tm, k), lambda i: (i, 0))],
        out_shape=[jax.ShapeDtypeStruct((s, n), F32), jax.ShapeDtypeStruct((s, k), BF16)],
        compiler_params=_cp(("parallel",)),
    )(x, gain.reshape(1, k), w)


def _rmsnorm_bwd(x, col_block, gain, dh, dres=None, *, name, tm=512):
    s = x.shape[0]
    k = gain.shape[-1]
    tm = min(tm, s)

    def body(*refs):
        x_ref, g_ref, dh_ref = refs[0], refs[1], refs[2]
        dres_ref = refs[3] if dres is not None else None
        dx_ref, dg_ref = refs[-2], refs[-1]
        xv = x_ref[...]
        r = lax.rsqrt(jnp.mean(xv * xv, axis=-1, keepdims=True) + EPS)
        dhv = dh_ref[...].astype(F32)
        a = dhv * g_ref[...]
        dx = r * a - xv * (r * r * r) * jnp.mean(a * xv, axis=-1, keepdims=True)
        if dres_ref is not None:
            dx = dx + dres_ref[...]
        dx_ref[...] = dx

        @pl.when(pl.program_id(0) == 0)
        def _():
            dg_ref[...] = jnp.zeros_like(dg_ref)

        dg_ref[...] += jnp.sum(dhv * xv * r, axis=0, keepdims=True)

    in_specs = [pl.BlockSpec((tm, k), lambda i: (i, col_block)), pl.BlockSpec((1, k), lambda i: (0, 0)),
                pl.BlockSpec((tm, k), lambda i: (i, 0))]
    args = [x, gain.reshape(1, k), dh]
    if dres is not None:
        in_specs.append(pl.BlockSpec((tm, k), lambda i: (i, 0)))
        args.append(dres)
    dx, dg = pl.pallas_call(
        body, name=name, grid=(s // tm,), in_specs=in_specs,
        out_specs=[pl.BlockSpec((tm, k), lambda i: (i, 0)), pl.BlockSpec((1, k), lambda i: (0, 0))],
        out_shape=[jax.ShapeDtypeStruct((s, k), F32), jax.ShapeDtypeStruct((1, k), F32)],
        compiler_params=_cp(("arbitrary",)),
    )(*args)
    return dx, dg.reshape(k)


def _ffn_fwd(x, gain, w_gu4, w_d2, *, name, tm=256):
    s = x.shape[0]
    tm = min(tm, s)

    def body(x_ref, g_ref, wgu_ref, wd_ref, xo_ref, gu_ref):
        xv = x_ref[...]
        r = lax.rsqrt(jnp.mean(xv * xv, axis=-1, keepdims=True) + EPS)
        hv = (xv * r * g_ref[...]).astype(BF16)
        y = jnp.zeros((tm, D), F32)
        for j in range(2):
            g = _dot(hv, wgu_ref[j], NN)
            u = _dot(hv, wgu_ref[2 + j], NN)
            gu_ref[:, j * FF_SHARD:(j + 1) * FF_SHARD] = g.astype(BF16)
            gu_ref[:, D_FF + j * FF_SHARD:D_FF + (j + 1) * FF_SHARD] = u.astype(BF16)
            act = (g * _sigmoid(g) * u).astype(BF16)
            y = y + _dot(act, wd_ref[j], NN)
        xo_ref[...] = xv + 0.5 * y

    return pl.pallas_call(
        body, name=name, grid=(s // tm,),
        in_specs=[pl.BlockSpec((tm, D), lambda i: (i, 0)), pl.BlockSpec((1, D), lambda i: (0, 0)),
                  pl.BlockSpec((N_CHIPS, D, FF_SHARD), lambda i: (0, 0, 0), pipeline_mode=pl.Buffered(1)),
                  pl.BlockSpec((2, FF_SHARD, D), lambda i: (0, 0, 0), pipeline_mode=pl.Buffered(1))],
        out_specs=[pl.BlockSpec((tm, D), lambda i: (i, 0)), pl.BlockSpec((tm, 2 * D_FF), lambda i: (i, 0))],
        out_shape=[jax.ShapeDtypeStruct((s, D), F32), jax.ShapeDtypeStruct((s, 2 * D_FF), BF16)],
        compiler_params=_cp(("parallel",)),
    )(x, gain.reshape(1, D), w_gu4, w_d2)


FFN_ROW_CHUNK = 32


def _ffn_bwd(x, dxo, gu, gain, w_gu4, w_d2, *, name, tm=256):
    s = x.shape[0]
    tm = min(tm, s)

    def body(x_ref, dxo_ref, gu_ref, g_ref, wgu_ref, wd_ref, dx_ref, dgu_ref, act_ref, h_ref, dy_ref, dg_ref):
        xv = x_ref[...]
        r = lax.rsqrt(jnp.mean(xv * xv, axis=-1, keepdims=True) + EPS)
        xh = xv * r
        h_ref[...] = (xh * g_ref[...]).astype(BF16)
        dxov = dxo_ref[...]
        dy = (0.5 * dxov).astype(BF16)
        dy_ref[...] = dy
        dh = jnp.zeros((tm, D), F32)
        for j in range(2):
            gcols = slice(j * FF_SHARD, (j + 1) * FF_SHARD)
            ucols = slice(D_FF + j * FF_SHARD, D_FF + (j + 1) * FF_SHARD)
            dact = _dot(dy, wd_ref[j], NT)
            for r0 in range(0, tm, FFN_ROW_CHUNK):
                rows = slice(r0, r0 + FFN_ROW_CHUNK)
                g = gu_ref[rows, gcols].astype(F32)
                u = gu_ref[rows, ucols].astype(F32)
                sg = _sigmoid(g)
                silu = g * sg
                da = dact[rows]
                act_ref[rows, gcols] = (silu * u).astype(BF16)
                dgu_ref[rows, gcols] = (da * u * (sg * (1.0 + g * (1.0 - sg)))).astype(BF16)
                dgu_ref[rows, ucols] = (da * silu).astype(BF16)
            dh = dh + _dot(dgu_ref[:, gcols], wgu_ref[j], NT) + _dot(dgu_ref[:, ucols], wgu_ref[2 + j], NT)
        a = dh * g_ref[...]
        dx_ref[...] = dxov + r * a - xh * (r * jnp.mean(a * xh, axis=-1, keepdims=True))

        @pl.when(pl.program_id(0) == 0)
        def _():
            dg_ref[...] = jnp.zeros_like(dg_ref)

        dg_ref[...] += jnp.sum(dh * xh, axis=0, keepdims=True)

    row = lambda i: (i, 0)
    outs = pl.pallas_call(
        body, name=name, grid=(s // tm,),
        in_specs=[pl.BlockSpec((tm, D), row), pl.BlockSpec((tm, D), row), pl.BlockSpec((tm, 2 * D_FF), row),
                  pl.BlockSpec((1, D), lambda i: (0, 0)),
                  pl.BlockSpec((N_CHIPS, D, FF_SHARD), lambda i: (0, 0, 0), pipeline_mode=pl.Buffered(1)),
                  pl.BlockSpec((2, FF_SHARD, D), lambda i: (0, 0, 0), pipeline_mode=pl.Buffered(1))],
        out_specs=[pl.BlockSpec((tm, D), row), pl.BlockSpec((tm, 2 * D_FF), row), pl.BlockSpec((tm, D_FF), row),
                   pl.BlockSpec((tm, D), row), pl.BlockSpec((tm, D), row), pl.BlockSpec((1, D), lambda i: (0, 0))],
        out_shape=[jax.ShapeDtypeStruct((s, D), F32), jax.ShapeDtypeStruct((s, 2 * D_FF), BF16),
                   jax.ShapeDtypeStruct((s, D_FF), BF16), jax.ShapeDtypeStruct((s, D), BF16),
                   jax.ShapeDtypeStruct((s, D), BF16), jax.ShapeDtypeStruct((1, D), F32)],
        compiler_params=_cp(("arbitrary",)),
    )(x, dxo, gu, gain.reshape(1, D), w_gu4, w_d2)
    dx, dgu, act, h, dy, dg = outs
    return dx, dgu, act, h, dy, dg.reshape(D)


DA = 128
SCALE_MLA = 1.0 / math.sqrt(NOPE + ROPE)
SCALE_FOX = 1.0 / math.sqrt(FOX_D)


def _causal_blocks(nb, key_major):
    if key_major:
        pairs = [(i, j) for j in range(nb) for i in range(j, nb)]
    else:
        pairs = [(i, j) for i in range(nb) for j in range(i + 1)]
    return (jnp.asarray(np.array([p[0] for p in pairs], np.int32)), jnp.asarray(np.array([p[1] for p in pairs], np.int32)))


HEADS_PER_STEP = 2
ROW_CHUNK = 64

def _col_to_row(col):
    return jnp.broadcast_to(col, (col.shape[0], DA)).T[0:1, :]


def _attn_fwd(qa, ka, va, dv, *, name, t=512):
    h, s, _ = qa.shape
    t = min(t, s)
    nb = s // t
    g = 3
    qi, kj = _causal_blocks(nb, key_major=False)

    rc = min(ROW_CHUNK, t)

    def body(qi_ref, kj_ref, q_ref, k_ref, v_ref, o_ref, lse_ref, m_sc, acc_sc, p_sc, a_sc):
        n = pl.program_id(1)
        i, j = qi_ref[n], kj_ref[n]

        @pl.when(j == 0)
        def _():
            m_sc[...] = jnp.full_like(m_sc, -jnp.inf)
            acc_sc[...] = jnp.zeros_like(acc_sc)

        def step(masked):
            scs = [_dot(q_ref[hh], k_ref[hh], NT) for hh in range(g)]
            for r0 in range(0, t, rc):
                rows = slice(r0, r0 + rc)
                for hh in range(g):
                    sr = scs[hh][rows]
                    if masked:
                        row = lax.broadcasted_iota(jnp.int32, (rc, t), 0) + r0
                        col = lax.broadcasted_iota(jnp.int32, (rc, t), 1)
                        sr = jnp.where(col <= row, sr, -jnp.inf)
                    m_old = m_sc[hh, rows]
                    m_new = jnp.maximum(m_old, jnp.max(sr, axis=-1, keepdims=True))
                    p_sc[hh, rows] = jnp.exp(sr - m_new).astype(BF16)
                    a_sc[hh, rows] = jnp.exp(m_old - m_new)
                    m_sc[hh, rows] = m_new
            for hh in range(g):
                acc_sc[hh] = a_sc[hh] * acc_sc[hh] + _dot(p_sc[hh], v_ref[hh], NN)

        @pl.when(j < i)
        def _():
            step(False)

        @pl.when(j == i)
        def _():
            step(True)
            for hh in range(g):
                acc = acc_sc[hh]
                l = acc[:, dv:dv + 1]
                o_ref[hh] = acc[:, :dv] / l
                lse_ref[hh] = _col_to_row(m_sc[hh] + jnp.log(l))

    qmap = lambda hg, n, qi_r, kj_r: (hg, qi_r[n], 0)
    kmap = lambda hg, n, qi_r, kj_r: (hg, kj_r[n], 0)
    return pl.pallas_call(
        body, name=name,
        grid_spec=pltpu.PrefetchScalarGridSpec(
            num_scalar_prefetch=2, grid=(h // g, qi.shape[0]),
            in_specs=[pl.BlockSpec((g, t, DA), qmap), pl.BlockSpec((g, t, DA), kmap), pl.BlockSpec((g, t, DA), kmap)],
            out_specs=[pl.BlockSpec((g, t, dv), qmap), pl.BlockSpec((g, 1, t), lambda hg, n, qi_r, kj_r: (hg, 0, qi_r[n]))],
            scratch_shapes=[pltpu.VMEM((g, t, 1), F32), pltpu.VMEM((g, t, DA), F32), pltpu.VMEM((g, t, t), BF16),
                            pltpu.VMEM((g, t, 1), F32)]),
        out_shape=[jax.ShapeDtypeStruct((h, s, dv), F32), jax.ShapeDtypeStruct((h, 1, s), F32)],
        compiler_params=_cp(("parallel", "arbitrary")),
    )(qi, kj, qa, ka, va)


def _attn_bwd(qa, ka, va, doa, lse_row, delta_row, decay, *, name, t=512):
    h, s, _ = qa.shape
    t = min(t, s)
    nb = s // t
    g = HEADS_PER_STEP
    rc = min(ROW_CHUNK, t)
    qi, kj = _causal_blocks(nb, key_major=True)
    nsteps = qi.shape[0]

    def body(*refs):
        qi_ref, kj_ref, q_ref, k_ref, v_ref, do_ref, lse_ref, dl_ref = refs[:8]
        p_sc, ds_sc = refs[-2:]
        if decay:
            dq_ref, dk_ref, dv_ref, dcq_ref, dck_ref, dq_acc, dk_acc, dv_acc, dcq_acc, dck_acc = refs[8:-2]
        else:
            dq_ref, dk_ref, dv_ref, dq_acc, dk_acc, dv_acc = refs[8:-2]
        n = pl.program_id(1)
        i, j = qi_ref[n], kj_ref[n]

        @pl.when(n == 0)
        def _():
            dq_acc[...] = jnp.zeros_like(dq_acc)
            if decay:
                dcq_acc[...] = jnp.zeros_like(dcq_acc)

        @pl.when(i == j)
        def _():
            dk_acc[...] = jnp.zeros_like(dk_acc)
            dv_acc[...] = jnp.zeros_like(dv_acc)
            if decay:
                dck_acc[...] = jnp.zeros_like(dck_acc)

        def step(masked):
            sts = [_dot(k_ref[hh], q_ref[hh], NT) for hh in range(g)]
            dpts = [_dot(v_ref[hh], do_ref[hh], NT) for hh in range(g)]
            dcq = [jnp.zeros((1, t), F32) for _ in range(g)]
            for r0 in range(0, t, rc):
                rows = slice(r0, r0 + rc)
                for hh in range(g):
                    st = sts[hh][rows]
                    if masked:
                        row = lax.broadcasted_iota(jnp.int32, (rc, t), 0) + r0
                        col = lax.broadcasted_iota(jnp.int32, (rc, t), 1)
                        st = jnp.where(row <= col, st, -jnp.inf)
                    pt = jnp.exp(st - lse_ref[hh])
                    dst = pt * (dpts[hh][rows] - dl_ref[hh])
                    p_sc[hh, rows] = pt.astype(BF16)
                    ds_sc[hh, rows] = dst.astype(BF16)
                    if decay:
                        dcq[hh] = dcq[hh] + jnp.sum(dst, axis=0, keepdims=True)
                        dck_acc[hh, rows] -= jnp.sum(dst, axis=1, keepdims=True)
            for hh in range(g):
                dv_acc[hh] += _dot(p_sc[hh], do_ref[hh], NN)
                dk_acc[hh] += _dot(ds_sc[hh], q_ref[hh], NN)
                dq_acc[hh, i] += _dot(ds_sc[hh], k_ref[hh], TN)
                if decay:
                    dcq_acc[hh, i] += dcq[hh]

        @pl.when(i > j)
        def _():
            step(False)

        @pl.when(i == j)
        def _():
            step(True)

        @pl.when(i == nb - 1)
        def _():
            dk_ref[...] = dk_acc[...]
            dv_ref[...] = dv_acc[...]
            if decay:
                for hh in range(g):
                    dck_ref[hh] = _col_to_row(dck_acc[hh])

        @pl.when(n == nsteps - 1)
        def _():
            dq_ref[...] = dq_acc[...]
            if decay:
                dcq_ref[...] = dcq_acc[...]

    kmap = lambda hg, n, qi_r, kj_r: (hg, kj_r[n], 0)
    qmap = lambda hg, n, qi_r, kj_r: (hg, qi_r[n], 0)
    qrow = lambda hg, n, qi_r, kj_r: (hg, 0, qi_r[n])
    krow = lambda hg, n, qi_r, kj_r: (hg, 0, kj_r[n])
    whole = lambda hg, n, qi_r, kj_r: (hg, 0, 0, 0)
    in_specs = [pl.BlockSpec((g, t, DA), qmap), pl.BlockSpec((g, t, DA), kmap), pl.BlockSpec((g, t, DA), kmap),
                pl.BlockSpec((g, t, DA), qmap), pl.BlockSpec((g, 1, t), qrow), pl.BlockSpec((g, 1, t), qrow)]
    out_specs = [pl.BlockSpec((g, nb, t, DA), whole), pl.BlockSpec((g, t, DA), kmap), pl.BlockSpec((g, t, DA), kmap)]
    out_shape = [jax.ShapeDtypeStruct((h, nb, t, DA), F32), jax.ShapeDtypeStruct((h, s, DA), F32), jax.ShapeDtypeStruct((h, s, DA), F32)]
    scratch = [pltpu.VMEM((g, nb, t, DA), F32), pltpu.VMEM((g, t, DA), F32), pltpu.VMEM((g, t, DA), F32)]
    if decay:
        out_specs += [pl.BlockSpec((g, nb, 1, t), whole), pl.BlockSpec((g, 1, t), krow)]
        out_shape += [jax.ShapeDtypeStruct((h, nb, 1, t), F32), jax.ShapeDtypeStruct((h, 1, s), F32)]
        scratch += [pltpu.VMEM((g, nb, 1, t), F32), pltpu.VMEM((g, t, 1), F32)]
    scratch += [pltpu.VMEM((g, t, t), BF16), pltpu.VMEM((g, t, t), BF16)]
    outs = pl.pallas_call(
        body, name=name,
        grid_spec=pltpu.PrefetchScalarGridSpec(num_scalar_prefetch=2, grid=(h // g, nsteps), in_specs=in_specs, out_specs=out_specs,
                                               scratch_shapes=scratch),
        out_shape=out_shape, compiler_params=_cp(("parallel", "arbitrary")),
    )(qi, kj, qa, ka, va, doa, lse_row, delta_row)
    outs = list(outs)
    outs[0] = outs[0].reshape(h, s, DA)
    if decay:
        outs[3] = outs[3].reshape(h, 1, s)
    return outs


def _sel(rows, cols, pairs, value=1.0):
    m = np.zeros((rows, cols), np.float32)
    for r, c in pairs:
        m[r, c] = value
    return jnp.asarray(m, BF16)


def _lane_row(lanes):
    m = np.zeros((1, DA), np.float32)
    m[0, list(lanes)] = 1.0
    return jnp.asarray(m)


def _rms(xv, gain):
    r = lax.rsqrt(jnp.mean(xv * xv, axis=-1, keepdims=True) + EPS)
    return xv * r * gain


def _mla_q_prep(z, gain, wq_a, wq_b, cq, sq, *, name, tm=512):
    s = z.shape[0]
    tm = min(tm, s)

    def body(z_ref, g_ref, wa_ref, wb_ref, c_ref, s_ref, qa_ref, qn_ref):
        qn = _rms(z_ref[...], g_ref[...]).astype(BF16)
        qn_ref[...] = qn
        c, sn = c_ref[...], s_ref[...]
        for hh in range(H):
            cols = slice(hh * DA, (hh + 1) * DA)
            qa_ref[hh] = (_dot(qn, wa_ref[:, cols], NN) * c + _dot(qn, wb_ref[:, cols], NN) * sn).astype(BF16)

    row = lambda i: (i, 0)
    fixed = lambda i: (0, 0)
    return pl.pallas_call(
        body, name=name, grid=(s // tm,),
        in_specs=[pl.BlockSpec((tm, Q_RANK), lambda i: (i, Z_QA // Q_RANK)), pl.BlockSpec((1, Q_RANK), fixed),
                  pl.BlockSpec((Q_RANK, H * DA), fixed), pl.BlockSpec((Q_RANK, H * DA), fixed),
                  pl.BlockSpec((tm, DA), row), pl.BlockSpec((tm, DA), row)],
        out_specs=[pl.BlockSpec((H, tm, DA), lambda i: (0, i, 0)), pl.BlockSpec((tm, Q_RANK), row)],
        out_shape=[jax.ShapeDtypeStruct((H, s, DA), BF16), jax.ShapeDtypeStruct((s, Q_RANK), BF16)],
        compiler_params=_cp(("parallel",)),
    )(z, gain.reshape(1, Q_RANK), wq_a, wq_b, cq, sq)


def _mla_kv_prep(z, gain, wk, wv, ck, sk, *, name, tm=512):
    s = z.shape[0]
    tm = min(tm, s)
    one = _lane_row([VDIM])

    def body(zkv_ref, z3_ref, z15_ref, g_ref, wk_ref, wv_ref, c_ref, s_ref, one_ref, ka_ref, va_ref, kvn_ref):
        kvn = _rms(zkv_ref[...], g_ref[...]).astype(BF16)
        kvn_ref[...] = kvn
        kpe = z3_ref[...] * c_ref[...] + z15_ref[...] * s_ref[...]
        for hh in range(H):
            cols = slice(hh * DA, (hh + 1) * DA)
            ka_ref[hh] = (_dot(kvn, wk_ref[:, cols], NN) + kpe).astype(BF16)
            va_ref[hh] = (_dot(kvn, wv_ref[:, cols], NN) + one_ref[...]).astype(BF16)

    row = lambda i: (i, 0)
    fixed = lambda i: (0, 0)
    blk = lambda c: pl.BlockSpec((tm, DA), lambda i: (i, c))
    heads = pl.BlockSpec((H, tm, DA), lambda i: (0, i, 0))
    return pl.pallas_call(
        body, name=name, grid=(s // tm,),
        in_specs=[blk(Z_KVA // DA), blk(Z_KR // DA), blk(Z_F // DA), pl.BlockSpec((1, KV_RANK), fixed),
                  pl.BlockSpec((KV_RANK, H * DA), fixed), pl.BlockSpec((KV_RANK, H * DA), fixed),
                  pl.BlockSpec((tm, DA), row), pl.BlockSpec((tm, DA), row), pl.BlockSpec((1, DA), fixed)],
        out_specs=[heads, heads, pl.BlockSpec((tm, KV_RANK), row)],
        out_shape=[jax.ShapeDtypeStruct((H, s, DA), BF16), jax.ShapeDtypeStruct((H, s, DA), BF16),
                   jax.ShapeDtypeStruct((s, KV_RANK), BF16)],
        compiler_params=_cp(("parallel",)),
    )(z, z, z, gain.reshape(1, KV_RANK), wk, wv, ck, sk, one)


DEC_C = (FOX_D, FOX_D + 1, FOX_D + 2)
DEC_1 = (FOX_D + 3, FOX_D + 4, FOX_D + 5)


def _fox_prep(z, c3t, *, name, tm=512):
    s = z.shape[0]
    tm = min(tm, s)
    w = H * FOX_D
    left = [(r, r) for r in range(FOX_D)]
    right = [(FOX_D + r, r) for r in range(FOX_D)]
    pq = jnp.stack([_sel(DA, DA, left, SCALE_FOX), _sel(DA, DA, right, SCALE_FOX)])
    pk = jnp.stack([_sel(DA, DA, left), _sel(DA, DA, right)])
    pcq = jnp.stack([_sel(32, DA, [(hh + 8 * k, DEC_C[k]) for k in range(3)]) for hh in range(H)])
    pck = jnp.stack([_sel(32, DA, [(hh + 8 * k, DEC_1[k]) for k in range(3)], -1.0) for hh in range(H)])
    rows3 = jnp.concatenate([_lane_row(DEC_1), _lane_row(DEC_C), _lane_row([FOX_D])], axis=0)

    def body(zq_ref, zk_ref, zv_ref, c_ref, pq_ref, pk_ref, pcq_ref, pck_ref, r_ref, qa_ref, ka_ref, va_ref):
        c3 = c_ref[...]
        for pair in range(H // 2):
            lanes = slice(pair * DA, (pair + 1) * DA)
            zq, zk, zv = zq_ref[:, lanes].astype(BF16), zk_ref[:, lanes].astype(BF16), zv_ref[:, lanes].astype(BF16)
            for side in range(2):
                hh = 2 * pair + side
                qa_ref[hh] = (_dot(zq, pq_ref[side], NN) + _dot(c3, pcq_ref[hh], TN) + r_ref[0:1, :]).astype(BF16)
                ka_ref[hh] = (_dot(zk, pk_ref[side], NN) + _dot(c3, pck_ref[hh], TN) + r_ref[1:2, :]).astype(BF16)
                va_ref[hh] = (_dot(zv, pk_ref[side], NN) + r_ref[2:3, :]).astype(BF16)

    fixed2 = lambda i: (0, 0)
    fixed3 = lambda i: (0, 0, 0)
    heads = pl.BlockSpec((H, tm, DA), lambda i: (0, i, 0))
    zblk = lambda c: pl.BlockSpec((tm, w), lambda i: (i, c))
    return pl.pallas_call(
        body, name=name, grid=(s // tm,),
        in_specs=[zblk(Z_FOX // w), zblk(Z_FOX // w + 1), zblk(Z_FOX // w + 2), pl.BlockSpec((32, tm), lambda i: (0, i)),
                  pl.BlockSpec((2, DA, DA), fixed3), pl.BlockSpec((2, DA, DA), fixed3),
                  pl.BlockSpec((H, 32, DA), fixed3), pl.BlockSpec((H, 32, DA), fixed3), pl.BlockSpec((3, DA), fixed2)],
        out_specs=[heads, heads, heads], out_shape=[jax.ShapeDtypeStruct((H, s, DA), BF16)] * 3,
        compiler_params=_cp(("parallel",)),
    )(z, z, z, c3t, pq, pk, pcq, pck, rows3)


def _mix_out(oa, yb, oc, w_out, x1, *, name, tm=512):
    s = yb.shape[0]
    tm = min(tm, s)
    e2 = jnp.stack([_sel(VDIM, DA, [(r, r) for r in range(VDIM)]), _sel(VDIM, DA, [(r, VDIM + r) for r in range(VDIM)])])

    def body(oa_ref, yb_ref, oc_ref, e_ref, w_ref, x_ref, x2_ref, cat_ref):
        def pairs(o_ref):
            return [(_dot(o_ref[2 * p].astype(BF16), e_ref[0], NN) + _dot(o_ref[2 * p + 1].astype(BF16), e_ref[1], NN)).astype(BF16)
                    for p in range(H // 2)]

        cat = jnp.concatenate(pairs(oa_ref) + [yb_ref[...].astype(BF16)] + pairs(oc_ref), axis=1)
        cat_ref[...] = cat
        x2_ref[...] = x_ref[...] + _dot(cat, w_ref[...], NN)

    row = lambda i: (i, 0)
    heads = pl.BlockSpec((H, tm, VDIM), lambda i: (0, i, 0))
    return pl.pallas_call(
        body, name=name, grid=(s // tm,),
        in_specs=[heads, pl.BlockSpec((tm, POOL_W), row), heads, pl.BlockSpec((2, VDIM, DA), lambda i: (0, 0, 0)),
                  pl.BlockSpec((D, D), lambda i: (0, 0)), pl.BlockSpec((tm, D), row)],
        out_specs=[pl.BlockSpec((tm, D), row), pl.BlockSpec((tm, D), row)],
        out_shape=[jax.ShapeDtypeStruct((s, D), F32), jax.ShapeDtypeStruct((s, D), BF16)],
        compiler_params=_cp(("parallel",)),
    )(oa, yb, oc, e2, w_out, x1)


def _mix_out_bwd(dx2b, w_out, oa, oc, *, name, tm=512):
    s = dx2b.shape[0]
    tm = min(tm, s)
    f2 = jnp.stack([_sel(DA, DA, [(r, r) for r in range(VDIM)]), _sel(DA, DA, [(VDIM + r, r) for r in range(VDIM)])])
    nv = H * VDIM

    def body(dx_ref, w_ref, oa_ref, oc_ref, f_ref, doa_ref, doc_ref, dyb_ref, dla_ref, dlc_ref):
        dcat = _dot(dx_ref[...], w_ref[...], NT)
        dyb_ref[...] = dcat[:, nv:nv + POOL_W]
        for base, o_ref, do_ref, dl_ref in ((0, oa_ref, doa_ref, dla_ref), (nv + POOL_W, oc_ref, doc_ref, dlc_ref)):
            for p in range(H // 2):
                blk = dcat[:, base + p * DA:base + (p + 1) * DA].astype(BF16)
                for side in range(2):
                    hh = 2 * p + side
                    do = _dot(blk, f_ref[side], NN)
                    do_ref[hh] = do.astype(BF16)
                    dl_ref[hh] = _col_to_row(jnp.sum(do[:, :VDIM] * o_ref[hh], axis=-1, keepdims=True))

    row = lambda i: (i, 0)
    heads = lambda w: pl.BlockSpec((H, tm, w), lambda i: (0, i, 0))
    return pl.pallas_call(
        body, name=name, grid=(s // tm,),
        in_specs=[pl.BlockSpec((tm, D), row), pl.BlockSpec((D, D), lambda i: (0, 0)), heads(VDIM), heads(VDIM),
                  pl.BlockSpec((2, DA, DA), lambda i: (0, 0, 0))],
        out_specs=[heads(DA), heads(DA), pl.BlockSpec((tm, POOL_W), row),
                   pl.BlockSpec((H, 1, tm), lambda i: (0, 0, i)), pl.BlockSpec((H, 1, tm), lambda i: (0, 0, i))],
        out_shape=[jax.ShapeDtypeStruct((H, s, DA), BF16), jax.ShapeDtypeStruct((H, s, DA), BF16),
                   jax.ShapeDtypeStruct((s, POOL_W), F32), jax.ShapeDtypeStruct((H, 1, s), F32), jax.ShapeDtypeStruct((H, 1, s), F32)],
        compiler_params=_cp(("parallel",)),
    )(dx2b, w_out, oa, oc, f2)


def _mla_bwd_prep(dqa, dka, dva, dft, cq, sq, ck, sk, *, name, tm=512):
    s = dqa.shape[1]
    tm = min(tm, s)
    keep = _lane_row(range(NOPE))

    def body(dq_ref, dk_ref, dv_ref, dft_ref, cq_ref, sq_ref, ck_ref, sk_ref, keep_ref, dqab_ref, dkv_ref, dz3_ref, dz15_ref):
        cqv, sqv = cq_ref[...], sq_ref[...]
        dkpe = jnp.zeros((tm, DA), F32)
        for hh in range(H):
            lanes = slice(hh * DA, (hh + 1) * DA)
            dq = dq_ref[hh]
            dqab_ref[:, lanes] = (dq * cqv).astype(BF16)
            dqab_ref[:, H * DA + hh * DA:H * DA + (hh + 1) * DA] = (dq * sqv).astype(BF16)
            dk = dk_ref[hh]
            dkpe = dkpe + dk
            dkv_ref[:, lanes] = (dk * keep_ref[...]).astype(BF16)
            dkv_ref[:, H * DA + hh * DA:H * DA + (hh + 1) * DA] = (dv_ref[hh] * keep_ref[...]).astype(BF16)
        dz3_ref[...] = (dkpe * ck_ref[...]).astype(BF16)
        dz15_ref[...] = (dkpe * sk_ref[...] + dft_ref[...]).astype(BF16)

    row = lambda i: (i, 0)
    heads = pl.BlockSpec((H, tm, DA), lambda i: (0, i, 0))
    tab = pl.BlockSpec((tm, DA), row)
    return pl.pallas_call(
        body, name=name, grid=(s // tm,),
        in_specs=[heads, heads, heads, tab, tab, tab, tab, tab, pl.BlockSpec((1, DA), lambda i: (0, 0))],
        out_specs=[pl.BlockSpec((tm, 2 * H * DA), row), pl.BlockSpec((tm, 2 * H * DA), row), tab, tab],
        out_shape=[jax.ShapeDtypeStruct((s, 2 * H * DA), BF16), jax.ShapeDtypeStruct((s, 2 * H * DA), BF16),
                   jax.ShapeDtypeStruct((s, DA), BF16), jax.ShapeDtypeStruct((s, DA), BF16)],
        compiler_params=_cp(("parallel",)),
    )(dqa, dka, dva, dft, cq, sq, ck, sk, keep)


def _fox_bwd_prep(dfqa, dfka, dfva, *, name, tm=512):
    s = dfqa.shape[1]
    tm = min(tm, s)
    place = lambda v: jnp.stack([_sel(DA, DA, [(r, r) for r in range(FOX_D)], v), _sel(DA, DA, [(r, FOX_D + r) for r in range(FOX_D)], v)])
    gq, gk = place(SCALE_FOX), place(1.0)

    def body(dq_ref, dk_ref, dv_ref, gq_ref, gk_ref, dz_ref):
        for part, (d_ref, g_ref) in enumerate(((dq_ref, gq_ref), (dk_ref, gk_ref), (dv_ref, gk_ref))):
            for p in range(H // 2):
                blk = _dot(d_ref[2 * p].astype(BF16), g_ref[0], NN) + _dot(d_ref[2 * p + 1].astype(BF16), g_ref[1], NN)
                lo = part * H * FOX_D + p * DA
                dz_ref[:, lo:lo + DA] = blk.astype(BF16)

    heads = pl.BlockSpec((H, tm, DA), lambda i: (0, i, 0))
    sel = pl.BlockSpec((2, DA, DA), lambda i: (0, 0, 0))
    return pl.pallas_call(
        body, name=name, grid=(s // tm,), in_specs=[heads, heads, heads, sel, sel],
        out_specs=pl.BlockSpec((tm, 3 * H * FOX_D), lambda i: (i, 0)),
        out_shape=jax.ShapeDtypeStruct((s, 3 * H * FOX_D), BF16), compiler_params=_cp(("parallel",)),
    )(dfqa, dfka, dfva, gq, gk)


def _lane_scan(x, s, reverse):
    lane = lax.broadcasted_iota(jnp.int32, x.shape, 1)
    sh = 1
    while sh < s:
        if reverse:
            x = x + jnp.where(lane < s - sh, pltpu.roll(x, s - sh, axis=1), 0.0)
        else:
            x = x + jnp.where(lane >= sh, pltpu.roll(x, sh, axis=1), 0.0)
        sh *= 2
    return x


def _gate_fwd(z, col_block, bias, *, name):
    s = z.shape[0]

    def body(z_ref, b_ref, f_ref, c_ref):
        ft = z_ref[...].T[0:8, :]
        f_ref[...] = ft
        xg = ft + b_ref[...]
        lf = jnp.minimum(xg, 0.0) - jnp.log(1.0 + jnp.exp(-jnp.abs(xg)))
        c = _lane_scan(lf, s, False)
        hi = c.astype(BF16).astype(F32)
        r = c - hi
        mid = r.astype(BF16).astype(F32)
        lo = r - mid
        c_ref[...] = jnp.concatenate([hi, mid, lo, jnp.zeros_like(hi)], axis=0).astype(BF16)

    return pl.pallas_call(
        body, name=name, grid=(1,),
        in_specs=[pl.BlockSpec((s, 128), lambda i: (0, col_block)), pl.BlockSpec((8, 1), lambda i: (0, 0))],
        out_specs=[pl.BlockSpec((8, s), lambda i: (0, 0)), pl.BlockSpec((32, s), lambda i: (0, 0))],
        out_shape=[jax.ShapeDtypeStruct((8, s), F32), jax.ShapeDtypeStruct((32, s), BF16)],
        compiler_params=_cp(("arbitrary",)))(z, bias)


def _gate_bwd(ft, bias, dc, *, name):
    s = ft.shape[1]

    def body(f_ref, b_ref, dc_ref, df_ref, db_ref):
        xg = f_ref[...] + b_ref[...]
        dlf = _lane_scan(dc_ref[...], s, True)
        df = dlf * _sigmoid(-xg)
        db_ref[...] = jnp.sum(df, axis=-1, keepdims=True)
        df_ref[...] = jnp.concatenate([df, jnp.zeros((DA - 8, s), F32)], axis=0).T

    return pl.pallas_call(body, name=name, out_shape=[jax.ShapeDtypeStruct((s, DA), F32), jax.ShapeDtypeStruct((8, 1), F32)],
                          compiler_params=_cp())(ft, bias, dc)


def _pool_lane_consts(tm, i):
    lane = lax.broadcasted_iota(jnp.int32, (tm, POOL_W), 1)
    tok = lax.broadcasted_iota(jnp.int32, (tm, POOL_W), 0) + i * tm
    win = jnp.where(lane < 64, 2, jnp.where(lane < 128, 4, jnp.where(lane < 192, 8, 16)))
    cnt = jnp.minimum(tok + 1, win).astype(F32)
    return lane, tok, cnt


def _pick_window(lane, s2, s4, s8, s16):
    return jnp.where(lane < 64, s2, jnp.where(lane < 128, s4, jnp.where(lane < 192, s8, s16)))


def _pool_fwd(z, col_block, bd, scale, *, name, tm=512):
    s = z.shape[0]
    tm = min(tm, s)
    hb = tm // POOL_HALO

    def body(u_ref, halo_ref, bd_ref, sc_ref, y_ref, p_ref, buf):
        i = pl.program_id(0)
        buf[0:POOL_HALO, :] = halo_ref[...] * (i > 0).astype(F32)
        buf[POOL_HALO:, :] = u_ref[...]

        def back(k):
            return buf[POOL_HALO - k:POOL_HALO - k + tm, :]

        u = u_ref[...]
        s2 = u + back(1)
        s4 = s2 + back(2) + back(3)
        s8 = s4 + back(4) + back(5) + back(6) + back(7)
        s16 = s8
        for k in range(8, 16):
            s16 = s16 + back(k)
        lane, _, cnt = _pool_lane_consts(tm, i)
        pooled = (_pick_window(lane, s2, s4, s8, s16) / cnt - u).astype(BF16)
        p_ref[...] = pooled
        y_ref[...] = _dot(pooled, bd_ref[...], NN) * sc_ref[...]

    return pl.pallas_call(
        body, name=name, grid=(s // tm,),
        in_specs=[pl.BlockSpec((tm, POOL_W), lambda i: (i, col_block)),
                  pl.BlockSpec((POOL_HALO, POOL_W), lambda i: (jnp.maximum(i * hb - 1, 0), col_block)),
                  pl.BlockSpec((POOL_W, POOL_W), lambda i: (0, 0)), pl.BlockSpec((1, POOL_W), lambda i: (0, 0))],
        out_specs=[pl.BlockSpec((tm, POOL_W), lambda i: (i, 0)), pl.BlockSpec((tm, POOL_W), lambda i: (i, 0))],
        out_shape=[jax.ShapeDtypeStruct((s, POOL_W), F32), jax.ShapeDtypeStruct((s, POOL_W), BF16)],
        scratch_shapes=[pltpu.VMEM((tm + POOL_HALO, POOL_W), F32)],
        compiler_params=_cp(("parallel",)),
    )(z, z, bd, scale.reshape(1, POOL_W))


def _pool_bwd_a(dy, pooled, bd, scale, *, name, tm=512):
    s = dy.shape[0]
    tm = min(tm, s)

    def body(dy_ref, p_ref, bd_ref, sc_ref, dq_ref, dys_ref, dsc_ref):
        i = pl.program_id(0)
        dyv = dy_ref[...]
        y0 = _dot(p_ref[...], bd_ref[...], NN)
        dys = (dyv * sc_ref[...]).astype(BF16)
        dys_ref[...] = dys
        dp = _dot(dys, bd_ref[...], NT)
        _, _, cnt = _pool_lane_consts(tm, i)
        dq_ref[:, 0:POOL_W] = dp / cnt
        dq_ref[:, POOL_W:] = dp

        @pl.when(i == 0)
        def _():
            dsc_ref[...] = jnp.zeros_like(dsc_ref)

        dsc_ref[...] += jnp.sum(dyv * y0, axis=0, keepdims=True)

    row = lambda i: (i, 0)
    dq, dys, dsc = pl.pallas_call(
        body, name=name, grid=(s // tm,),
        in_specs=[pl.BlockSpec((tm, POOL_W), row), pl.BlockSpec((tm, POOL_W), row),
                  pl.BlockSpec((POOL_W, POOL_W), lambda i: (0, 0)), pl.BlockSpec((1, POOL_W), lambda i: (0, 0))],
        out_specs=[pl.BlockSpec((tm, 2 * POOL_W), row), pl.BlockSpec((tm, POOL_W), row), pl.BlockSpec((1, POOL_W), lambda i: (0, 0))],
        out_shape=[jax.ShapeDtypeStruct((s, 2 * POOL_W), F32), jax.ShapeDtypeStruct((s, POOL_W), BF16),
                   jax.ShapeDtypeStruct((1, POOL_W), F32)],
        compiler_params=_cp(("arbitrary",)),
    )(dy, pooled, bd, scale.reshape(1, POOL_W))
    return dq, dys, dsc.reshape(POOL_W)


def _pool_bwd_b(dq, *, name, tm=512):
    s = dq.shape[0]
    tm = min(tm, s)
    hb = tm // POOL_HALO
    nblk = s // tm

    def body(q_ref, dp_ref, halo_ref, du_ref, buf):
        i = pl.program_id(0)
        buf[0:tm, :] = q_ref[...]
        buf[tm:, :] = halo_ref[...] * (i < nblk - 1).astype(F32)

        def ahead(k):
            return buf[k:k + tm, :]

        q = q_ref[...]
        s2 = q + ahead(1)
        s4 = s2 + ahead(2) + ahead(3)
        s8 = s4 + ahead(4) + ahead(5) + ahead(6) + ahead(7)
        s16 = s8
        for k in range(8, 16):
            s16 = s16 + ahead(k)
        lane = lax.broadcasted_iota(jnp.int32, (tm, POOL_W), 1)
        du_ref[...] = _pick_window(lane, s2, s4, s8, s16) - dp_ref[...]

    return pl.pallas_call(
        body, name=name, grid=(nblk,),
        in_specs=[pl.BlockSpec((tm, POOL_W), lambda i: (i, 0)), pl.BlockSpec((tm, POOL_W), lambda i: (i, 1)),
                  pl.BlockSpec((POOL_HALO, POOL_W), lambda i: (jnp.minimum((i + 1) * hb, nblk * hb - 1), 0))],
        out_specs=pl.BlockSpec((tm, POOL_W), lambda i: (i, 0)),
        out_shape=jax.ShapeDtypeStruct((s, POOL_W), F32),
        scratch_shapes=[pltpu.VMEM((tm + POOL_HALO, POOL_W), F32)],
        compiler_params=_cp(("parallel",)),
    )(dq, dq, dq)


def _loss_head(x, gain, target, *, name, tm=512):
    s = x.shape[0]
    tm = min(tm, s)

    def body(x_ref, g_ref, t_ref, dx_ref, dg_ref, loss_ref):
        xv = x_ref[...]
        r = lax.rsqrt(jnp.mean(xv * xv, axis=-1, keepdims=True) + EPS)
        xh = xv * r
        err = xh * g_ref[...] - t_ref[...]
        dy = err * (1.0 / D)
        a = dy * g_ref[...]
        dx_ref[...] = r * a - xh * (r * jnp.mean(a * xh, axis=-1, keepdims=True))

        @pl.when(pl.program_id(0) == 0)
        def _():
            dg_ref[...] = jnp.zeros_like(dg_ref)
            loss_ref[...] = jnp.zeros_like(loss_ref)

        dg_ref[...] += jnp.sum(dy * xh, axis=0, keepdims=True)
        part = 0.5 * jnp.sum(jnp.mean(err * err, axis=-1, keepdims=True), axis=0, keepdims=True)
        loss_ref[...] += jnp.broadcast_to(part, loss_ref.shape)

    row = lambda i: (i, 0)
    dx, dg, loss = pl.pallas_call(
        body, name=name, grid=(s // tm,),
        in_specs=[pl.BlockSpec((tm, D), row), pl.BlockSpec((1, D), lambda i: (0, 0)), pl.BlockSpec((tm, D), row)],
        out_specs=[pl.BlockSpec((tm, D), row), pl.BlockSpec((1, D), lambda i: (0, 0)), pl.BlockSpec((1, 128), lambda i: (0, 0))],
        out_shape=[jax.ShapeDtypeStruct((s, D), F32), jax.ShapeDtypeStruct((1, D), F32), jax.ShapeDtypeStruct((1, 128), F32)],
        compiler_params=_cp(("arbitrary",)),
    )(x, gain.reshape(1, D), target)
    return dx, dg.reshape(D), loss[0, 0]


def _adamw(w, g, m, v, *, name, tr=512):
    rows, cols = w.shape
    tr = min(tr, rows)
    assert rows % tr == 0, (name, rows, tr)
    c_m = 1.0 - ADAM_B1
    c_v = 1.0 - ADAM_B2
    bc1 = 1.0 - ADAM_B1 ** ADAM_STEP
    bc2 = 1.0 - ADAM_B2 ** ADAM_STEP

    def body(w_ref, g_ref, m_ref, v_ref, d_ref, mo_ref, vo_ref):
        gv = g_ref[...]
        mn = ADAM_B1 * m_ref[...] + c_m * gv
        vn = ADAM_B2 * v_ref[...] + c_v * (gv * gv)
        mo_ref[...] = mn
        vo_ref[...] = vn
        d_ref[...] = -ADAM_LR * ((mn / bc1) / (jnp.sqrt(vn / bc2) + ADAM_EPS) + ADAM_WD * w_ref[...])

    spec = pl.BlockSpec((tr, cols), lambda i: (i, 0))
    return pl.pallas_call(body, name=name, grid=(rows // tr,), in_specs=[spec] * 4, out_specs=[spec] * 3,
                          out_shape=[jax.ShapeDtypeStruct((rows, cols), F32)] * 3,
                          compiler_params=_cp(("parallel",)))(w, g, m, v)


def _position():
    return jnp.stack([lax.axis_index("c"), 2 * lax.axis_index("x") + lax.axis_index("y")]).astype(jnp.int32)


SUM_ROW_TILES = 2


def _sum2_bf16(pos, fulls, sibs, *, name):
    n = len(fulls)
    nb = SUM_ROW_TILES

    def body(pos_ref, *refs):
        for t in range(n):
            refs[2 * n + t][...] = (refs[t][...] + refs[n + t][...]).astype(BF16)

    in_specs, sib_specs = [], []
    for sb in sibs:
        _, half, cols = sb.shape
        tr = half // nb
        assert half % nb == 0 and tr % 16 == 0, sb.shape
        in_specs.append(pl.BlockSpec((None, tr, cols), lambda j, i, p: (j, p[0] * nb + i, 0)))
        sib_specs.append(pl.BlockSpec((None, tr, cols), lambda j, i, p: (j, i, 0)))
    return pl.pallas_call(
        body, name=name,
        grid_spec=pltpu.PrefetchScalarGridSpec(num_scalar_prefetch=1, grid=(N_CHIPS, nb), in_specs=in_specs + sib_specs,
                                               out_specs=sib_specs),
        out_shape=[jax.ShapeDtypeStruct(sb.shape, BF16) for sb in sibs],
        compiler_params=_cp(("parallel", "parallel")))(pos, *fulls, *sibs)


def _sum5(pos, fulls, sibs, recvs, *, name):
    n = len(fulls)
    nb = SUM_ROW_TILES

    def body(pos_ref, *refs):
        for t in range(n):
            acc = refs[t][...] + refs[n + t][...]
            for kk in range(3):
                acc = acc + refs[2 * n + t][kk].astype(F32)
            refs[3 * n + t][...] = acc

    f_specs, s_specs, r_specs, o_specs = [], [], [], []
    for f in fulls:
        _, rows, cols = f.shape
        tr = rows // 2 // nb
        f_specs.append(pl.BlockSpec((None, tr, cols), lambda i, p: (p[1], p[0] * nb + i, 0)))
        s_specs.append(pl.BlockSpec((None, tr, cols), lambda i, p: (p[1], i, 0)))
        r_specs.append(pl.BlockSpec((3, tr, cols), lambda i, p: (0, i, 0)))
        o_specs.append(pl.BlockSpec((tr, cols), lambda i, p: (p[0] * nb + i, 0)))
    return pl.pallas_call(
        body, name=name,
        grid_spec=pltpu.PrefetchScalarGridSpec(num_scalar_prefetch=1, grid=(nb,), in_specs=f_specs + s_specs + r_specs,
                                               out_specs=o_specs),
        out_shape=[jax.ShapeDtypeStruct(f.shape[1:], F32) for f in fulls],
        compiler_params=_cp(("parallel",)))(pos, *fulls, *sibs, *recvs)


def _place():
    x, y, c = lax.axis_index("x"), lax.axis_index("y"), lax.axis_index("c")
    chips = [(1 - x, y), (x, 1 - y), (1 - x, 1 - y)]
    return x, y, c, 2 * x + y, chips


SEM_SPEC = pl.BlockSpec(memory_space=pltpu.SEMAPHORE)
ANY_SPEC = pl.BlockSpec(memory_space=pl.ANY)


def _gather_copies(ins, outs, send_i, recv_i, send_o, recv_o):
    x, y, c, me, chips = _place()
    n = len(ins)
    started, awaited = [], []
    for t in range(n):
        half = ins[t].shape[0] // 2
        mine = pl.ds(c * half, half)
        started.append(pltpu.make_async_remote_copy(
            src_ref=ins[t], dst_ref=outs[t].at[me], send_sem=send_o.at[t], recv_sem=recv_o.at[t],
            device_id=(x, y, 1 - c), device_id_type=MESH))
        awaited.append(started[-1])
        for kk, (px, py) in enumerate(chips):
            started.append(pltpu.make_async_remote_copy(
                src_ref=ins[t].at[mine], dst_ref=outs[t].at[me, mine], send_sem=send_i.at[t * 3 + kk],
                recv_sem=recv_i.at[t * 3 + kk], device_id=(px, py, c), device_id_type=MESH))
            awaited.append(pltpu.make_async_remote_copy(
                src_ref=ins[t].at[mine], dst_ref=outs[t].at[2 * px + py, mine], send_sem=send_i.at[t * 3 + kk],
                recv_sem=recv_i.at[t * 3 + kk], device_id=(px, py, c), device_id_type=MESH))
    return started, awaited


def _forward_copies(outs, send_d, recv_d):
    x, y, c, me, chips = _place()
    started, awaited = [], []
    for t in range(len(outs)):
        half = outs[t].shape[1] // 2
        for kk, (px, py) in enumerate(chips):
            for lst, hc in ((started, c), (awaited, 1 - c)):
                blk = outs[t].at[2 * px + py, pl.ds(hc * half, half)]
                lst.append(pltpu.make_async_remote_copy(src_ref=blk, dst_ref=blk, send_sem=send_d.at[t * 3 + kk],
                                                        recv_sem=recv_d.at[t * 3 + kk], device_id=(x, y, 1 - c), device_id_type=MESH))
    return started, awaited


def _gather_blocking(shards):
    n = len(shards)

    def body(*refs):
        ins, outs = refs[:n], refs[n:2 * n]
        send_i, recv_i, send_d, recv_d, send_o, recv_o = refs[2 * n:]
        started, awaited = _gather_copies(ins, outs, send_i, recv_i, send_o, recv_o)
        for cp in started:
            cp.start()
        for cp in awaited:
            cp.wait_recv()
        fwd, fwd_in = _forward_copies(outs, send_d, recv_d)
        for cp in fwd:
            cp.start()
        for cp in fwd_in:
            cp.wait_recv()
        for cp in started + fwd:
            cp.wait_send()

    return pl.pallas_call(
        body, name="gather_first", in_specs=[HBM_SPEC] * n, out_specs=[HBM_SPEC] * n,
        out_shape=[jax.ShapeDtypeStruct((N_CHIPS,) + s.shape, s.dtype) for s in shards],
        scratch_shapes=[pltpu.SemaphoreType.DMA((3 * n,)), pltpu.SemaphoreType.DMA((3 * n,)),
                        pltpu.SemaphoreType.DMA((3 * n,)), pltpu.SemaphoreType.DMA((3 * n,)),
                        pltpu.SemaphoreType.DMA((n,)), pltpu.SemaphoreType.DMA((n,))],
    )(*shards)


def _gather_start(shards, after):
    n = len(shards)

    def body(*refs):
        ins = refs[:n]
        send_i, recv_i, send_o, recv_o = refs[2 * n + 1:2 * n + 5]
        outs = refs[3 * n + 5:4 * n + 5]
        token = refs[4 * n + 5]
        started, _ = _gather_copies(ins, outs, send_i, recv_i, send_o, recv_o)
        for cp in started:
            cp.start()
        token[...] = jnp.zeros_like(token)

    lands = [lax.empty((N_CHIPS,) + s.shape, s.dtype) for s in shards]
    sems = [pltpu.SemaphoreType.DMA((3 * n,)), pltpu.SemaphoreType.DMA((3 * n,)), pltpu.SemaphoreType.DMA((n,)), pltpu.SemaphoreType.DMA((n,))]
    res = pl.pallas_call(
        body, name="gather_rest_start",
        in_specs=[HBM_SPEC] * (2 * n) + [ANY_SPEC],
        out_specs=[SEM_SPEC] * 4 + [HBM_SPEC] * (2 * n) + [pl.BlockSpec(memory_space=pltpu.VMEM)],
        out_shape=sems + [jax.ShapeDtypeStruct(s.shape, s.dtype) for s in shards]
        + [jax.ShapeDtypeStruct(a.shape, a.dtype) for a in lands] + [jax.ShapeDtypeStruct((8, 128), F32)],
        input_output_aliases={t: 4 + t for t in range(2 * n)},
        compiler_params=pltpu.CompilerParams(has_side_effects=pltpu.SideEffectType.DATAFLOW_SIDE_EFFECTING),
    )(*[pltpu.with_memory_space_constraint(s, pltpu.HBM) for s in shards],
      *[pltpu.with_memory_space_constraint(a, pltpu.HBM) for a in lands], after)
    return res[:4], res[4:4 + n], res[4 + n:4 + 2 * n], res[-1]


def _gather_wait(sems, shards_thru, lands_thru, after):
    n = len(shards_thru)

    def body(*refs):
        ins, outs_in = refs[:n], refs[n:2 * n]
        send_i, recv_i, send_o, recv_o = refs[2 * n:2 * n + 4]
        started, awaited = _gather_copies(ins, outs_in, send_i, recv_i, send_o, recv_o)
        for cp in started:
            cp.wait_send()
        for cp in awaited:
            cp.wait_recv()

    res = pl.pallas_call(
        body, name="gather_rest_wait",
        in_specs=[HBM_SPEC] * (2 * n) + [SEM_SPEC] * 4 + [ANY_SPEC],
        out_specs=[HBM_SPEC] * (2 * n),
        out_shape=[jax.ShapeDtypeStruct(a.shape, a.dtype) for a in list(shards_thru) + list(lands_thru)],
        input_output_aliases={t: t for t in range(2 * n)},
        compiler_params=pltpu.CompilerParams(has_side_effects=pltpu.SideEffectType.DATAFLOW_SIDE_EFFECTING),
    )(*shards_thru, *lands_thru, *sems, after)
    return res[n:]


def _gather_forward(lands):
    n = len(lands)

    def body(*refs):
        outs = refs[n:2 * n]
        send_d, recv_d = refs[2 * n:]
        fwd, fwd_in = _forward_copies(outs, send_d, recv_d)
        for cp in fwd:
            cp.start()
        for cp in fwd_in:
            cp.wait_recv()
        for cp in fwd:
            cp.wait_send()

    return pl.pallas_call(
        body, name="gather_rest_forward", in_specs=[HBM_SPEC] * n, out_specs=[HBM_SPEC] * n,
        out_shape=[jax.ShapeDtypeStruct(a.shape, a.dtype) for a in lands],
        input_output_aliases={t: t for t in range(n)},
        scratch_shapes=[pltpu.SemaphoreType.DMA((3 * n,)), pltpu.SemaphoreType.DMA((3 * n,))],
    )(*lands)


def _reduce_stage1(grads):
    n = len(grads)

    def body(*refs):
        ins, sib = refs[:n], refs[n:2 * n]
        send, recv = refs[2 * n:]
        x, y, c, me, chips = _place()
        cps = []
        for t in range(n):
            rows = ins[t].shape[1] // 2
            cp = pltpu.make_async_remote_copy(
                src_ref=ins[t].at[:, pl.ds((1 - c) * rows, rows), :], dst_ref=sib[t], send_sem=send.at[t],
                recv_sem=recv.at[t], device_id=(x, y, 1 - c), device_id_type=MESH)
            cp.start()
            cps.append(cp)
        for cp in cps:
            cp.wait()

    return pl.pallas_call(
        body, name="reduce_stage1", in_specs=[HBM_SPEC] * n, out_specs=[HBM_SPEC] * n,
        out_shape=[jax.ShapeDtypeStruct((N_CHIPS, g.shape[1] // 2, g.shape[2]), F32) for g in grads],
        scratch_shapes=[pltpu.SemaphoreType.DMA((n,)), pltpu.SemaphoreType.DMA((n,))],
    )(*grads)


def _stage2_copies(ps, rcv, send, recv):
    x, y, c, me, chips = _place()
    return [pltpu.make_async_remote_copy(
        src_ref=ps[t].at[2 * px + py], dst_ref=rcv[t].at[kk], send_sem=send.at[t * 3 + kk],
        recv_sem=recv.at[t * 3 + kk], device_id=(px, py, c), device_id_type=MESH)
        for t in range(len(ps)) for kk, (px, py) in enumerate(chips)]


def _reduce_stage2(psum_bf16):
    n = len(psum_bf16)

    def body(*refs):
        cps = _stage2_copies(refs[:n], refs[n:2 * n], *refs[2 * n:])
        for cp in cps:
            cp.start()
        for cp in cps:
            cp.wait()

    return pl.pallas_call(
        body, name="reduce_stage2", in_specs=[HBM_SPEC] * n, out_specs=[HBM_SPEC] * n,
        out_shape=[jax.ShapeDtypeStruct((3,) + p.shape[1:], p.dtype) for p in psum_bf16],
        scratch_shapes=[pltpu.SemaphoreType.DMA((3 * n,)), pltpu.SemaphoreType.DMA((3 * n,))],
    )(*psum_bf16)


def _reduce_stage2_start(psum_bf16):
    n = len(psum_bf16)

    def body(*refs):
        ps = refs[:n]
        send, recv = refs[2 * n:2 * n + 2]
        rcv = refs[3 * n + 2:4 * n + 2]
        token = refs[4 * n + 2]
        for cp in _stage2_copies(ps, rcv, send, recv):
            cp.start()
        token[...] = jnp.zeros_like(token)

    lands = [lax.empty((3,) + p.shape[1:], p.dtype) for p in psum_bf16]
    res = pl.pallas_call(
        body, name="reduce_stage2_start",
        in_specs=[HBM_SPEC] * (2 * n),
        out_specs=[SEM_SPEC] * 2 + [HBM_SPEC] * (2 * n) + [pl.BlockSpec(memory_space=pltpu.VMEM)],
        out_shape=[pltpu.SemaphoreType.DMA((3 * n,)), pltpu.SemaphoreType.DMA((3 * n,))]
        + [jax.ShapeDtypeStruct(p.shape, p.dtype) for p in psum_bf16]
        + [jax.ShapeDtypeStruct(a.shape, a.dtype) for a in lands] + [jax.ShapeDtypeStruct((8, 128), F32)],
        input_output_aliases={t: 2 + t for t in range(2 * n)},
        compiler_params=pltpu.CompilerParams(has_side_effects=pltpu.SideEffectType.DATAFLOW_SIDE_EFFECTING),
    )(*[pltpu.with_memory_space_constraint(p, pltpu.HBM) for p in psum_bf16],
      *[pltpu.with_memory_space_constraint(a, pltpu.HBM) for a in lands])
    return res[:2], res[2:2 + n], res[2 + n:2 + 2 * n], res[-1]


def _reduce_stage2_wait(sems, ps_thru, lands_thru, after):
    n = len(ps_thru)

    def body(*refs):
        for cp in _stage2_copies(refs[:n], refs[n:2 * n], refs[2 * n], refs[2 * n + 1]):
            cp.wait()

    res = pl.pallas_call(
        body, name="reduce_stage2_wait",
        in_specs=[HBM_SPEC] * (2 * n) + [SEM_SPEC] * 2 + [ANY_SPEC],
        out_specs=[HBM_SPEC] * (2 * n),
        out_shape=[jax.ShapeDtypeStruct(a.shape, a.dtype) for a in list(ps_thru) + list(lands_thru)],
        input_output_aliases={t: t for t in range(2 * n)},
        compiler_params=pltpu.CompilerParams(has_side_effects=pltpu.SideEffectType.DATAFLOW_SIDE_EFFECTING),
    )(*ps_thru, *lands_thru, *sems, after)
    return res[n:]


def _reduce_stage3(reduced):
    n = len(reduced)

    def body(*refs):
        outs = refs[n:2 * n]
        send, recv = refs[2 * n:]
        x, y, c, me, chips = _place()
        cps = []
        for t in range(n):
            rows = outs[t].shape[0] // 2
            mine = outs[t].at[pl.ds(c * rows, rows), :]
            cp = pltpu.make_async_remote_copy(src_ref=mine, dst_ref=mine, send_sem=send.at[t], recv_sem=recv.at[t],
                                              device_id=(x, y, 1 - c), device_id_type=MESH)
            cp.start()
            cps.append(cp)
        for cp in cps:
            cp.wait()

    return pl.pallas_call(
        body, name="reduce_stage3", in_specs=[HBM_SPEC] * n, out_specs=[HBM_SPEC] * n,
        out_shape=[jax.ShapeDtypeStruct(r.shape, r.dtype) for r in reduced],
        input_output_aliases={t: t for t in range(n)},
        scratch_shapes=[pltpu.SemaphoreType.DMA((n,)), pltpu.SemaphoreType.DMA((n,))],
    )(*reduced)


def _allreduce_small(v):
    rows, cols = v.shape

    def body(v_ref, o_ref, buf, send, recv, loc):
        x, y, c, me, chips = _place()
        mine = 4 * x + 2 * y + c
        lc = pltpu.make_async_copy(v_ref, buf.at[mine], loc)
        lc.start()
        peers = []
        for fx in range(2):
            for fy in range(2):
                for fc in range(2):
                    if fx or fy or fc:
                        peers.append((fx, fy, fc))
        cps = []
        for kk, (fx, fy, fc) in enumerate(peers):
            to = (x ^ fx, y ^ fy, c ^ fc)
            cp = pltpu.make_async_remote_copy(src_ref=v_ref, dst_ref=buf.at[mine], send_sem=send.at[kk], recv_sem=recv.at[kk],
                                              device_id=to, device_id_type=MESH)
            cp.start()
            cps.append((cp, to))
        for kk, (cp, to) in enumerate(cps):
            src = 4 * to[0] + 2 * to[1] + to[2]
            pltpu.make_async_remote_copy(src_ref=v_ref, dst_ref=buf.at[src], send_sem=send.at[kk], recv_sem=recv.at[kk],
                                         device_id=to, device_id_type=MESH).wait_recv()
        for cp, _ in cps:
            cp.wait_send()
        lc.wait()
        acc = buf[0]
        for d in range(1, 8):
            acc = acc + buf[d]
        o_ref[...] = acc

    return pl.pallas_call(
        body, name="allreduce_small", in_specs=[pl.BlockSpec(memory_space=pltpu.VMEM)],
        out_specs=pl.BlockSpec(memory_space=pltpu.VMEM), out_shape=jax.ShapeDtypeStruct((rows, cols), F32),
        scratch_shapes=[pltpu.VMEM((8, rows, cols), F32), pltpu.SemaphoreType.DMA((7,)), pltpu.SemaphoreType.DMA((7,)),
                        pltpu.SemaphoreType.DMA],
        compiler_params=pltpu.CompilerParams(vmem_limit_bytes=VMEM_LIMIT_V7X),
    )(v)


def _pad_w_in(w):
    z = lambda n: jnp.zeros(w.shape[:-1] + (n,), w.dtype)
    return jnp.concatenate([w[..., 0:384], z(64), w[..., 384:416], z(32), w[..., 416:1824],
                            w[..., 1824:1830], z(58), w[..., 400:416], w[..., 384:400], z(32)], axis=-1)


def _unpad_w_in(g):
    x1 = g[..., 448:464] + g[..., Z_F + 80:Z_F + 96]
    x2 = g[..., 464:480] + g[..., Z_F + 64:Z_F + 80]
    return jnp.concatenate([g[..., 0:384], x1, x2, g[..., 512:1920], g[..., 1920:1926]], axis=-1)


def _block_diag(pw):
    out = jnp.zeros((POOL_W, POOL_W), pw.dtype)
    for g in range(4):
        out = out.at[g * 64:(g + 1) * 64, g * 64:(g + 1) * 64].set(pw[g])
    return out


def _rope_tables(s):
    inv_freq = ROPE_THETA ** (-jnp.arange(0, ROPE, 2, dtype=F32) / ROPE)
    ang = jnp.arange(s, dtype=jnp.int32).astype(F32)[:, None] * inv_freq[None, :]
    cos, sin = jnp.cos(ang), jnp.sin(ang)
    zero = lambda n: jnp.zeros((s, n), F32)
    ck = jnp.concatenate([zero(NOPE), cos, cos, zero(DA - NOPE - ROPE)], axis=1)
    sk = jnp.concatenate([zero(NOPE), -sin, sin, zero(DA - NOPE - ROPE)], axis=1)
    cq = jnp.concatenate([jnp.ones((s, NOPE), F32), cos, cos, zero(DA - NOPE - ROPE)], axis=1) * SCALE_MLA
    return dict(cq=cq, sq=sk * SCALE_MLA, ck=ck, sk=sk)


def _mix_fwd(l, x1, wts, sm, tabs):
    z, h2 = _norm_mm(x1, 0, sm["mix_norm"][l], wts["w_in"][l], name=f"mix_in_{l}")
    qa, qn = _mla_q_prep(z, sm["q_a_norm"][l], wts["wq_a"][l], wts["wq_b"][l], tabs["cq"], tabs["sq"], name=f"mla_q_{l}")
    ka, va, kvn = _mla_kv_prep(z, sm["kv_a_norm"][l], wts["wk"][l], wts["wv"][l], tabs["ck"], tabs["sk"], name=f"mla_kv_{l}")
    oa, lse_a = _attn_fwd(qa, ka, va, VDIM, name=f"mla_attn_{l}")

    bd = _block_diag(wts["pool_w"][l]).astype(BF16)
    yb, pooled = _pool_fwd(z, Z_POOL // POOL_W, bd, sm["pool_scale"][l], name=f"pool_{l}")

    fb = jnp.pad(sm["fox_b_f"][l], (0, 8 - H)).reshape(8, 1)
    ft, c3t = _gate_fwd(z, Z_F // DA, fb, name=f"fox_gate_{l}")
    fqa, fka, fva = _fox_prep(z, c3t, name=f"fox_prep_{l}")
    oc, lse_c = _attn_fwd(fqa, fka, fva, FOX_D, name=f"fox_attn_{l}")

    x2, cat = _mix_out(oa, yb, oc, wts["w_out"][l], x1, name=f"mix_out_{l}")
    saved = dict(z=z, h2=h2, qn=qn, kvn=kvn, qa=qa, ka=ka, va=va, oa=oa, lse_a=lse_a, bd=bd, pooled=pooled,
                 fqa=fqa, fka=fka, fva=fva, ft=ft, fb=fb, oc=oc, lse_c=lse_c, cat=cat)
    return x2, saved


def _mix_bwd(l, x1, dx2, sv, wts, sm, tabs):
    s = x1.shape[0]
    g = {}
    dx2b = dx2.astype(BF16)
    g["w_out"] = _mm(sv["cat"], dx2b, "tn", name=f"d_w_out_{l}", tm=1024, tn=1024, tk=DW_TOKENS)
    doa, doc, dyb, dl_a, dl_c = _mix_out_bwd(dx2b, wts["w_out"][l], sv["oa"], sv["oc"], name=f"mix_out_bwd_{l}")

    dfqa, dfka, dfva, dcq, dck = _attn_bwd(sv["fqa"], sv["fka"], sv["fva"], doc, sv["lse_c"], dl_c, True, name=f"fox_attn_bwd_{l}")
    dfox = _fox_bwd_prep(dfqa, dfka, dfva, name=f"fox_bwd_prep_{l}")
    dc = jnp.pad(dcq.reshape(H, s) + dck.reshape(H, s), ((0, 8 - H), (0, 0)))
    dft, dfb = _gate_bwd(sv["ft"], sv["fb"], dc, name=f"fox_gate_bwd_{l}")
    g["fox_b_f"] = dfb[:H, 0]

    dq, dys, g["pool_scale"] = _pool_bwd_a(dyb, sv["pooled"], sv["bd"], sm["pool_scale"][l], name=f"pool_bwd_a_{l}")
    du = _pool_bwd_b(dq, name=f"pool_bwd_b_{l}")
    dbd = _mm(sv["pooled"], dys, "tn", name=f"d_pool_w_{l}")
    g["pool_w"] = jnp.stack([dbd[i * 64:(i + 1) * 64, i * 64:(i + 1) * 64] for i in range(4)])

    dqa_, dka_, dva_ = _attn_bwd(sv["qa"], sv["ka"], sv["va"], doa, sv["lse_a"], dl_a, False, name=f"mla_attn_bwd_{l}")
    dqab, dkv, dz3, dz15 = _mla_bwd_prep(dqa_, dka_, dva_, dft, tabs["cq"], tabs["sq"], tabs["ck"], tabs["sk"],
                                         name=f"mla_bwd_prep_{l}")
    wq_ab = jnp.concatenate([wts["wq_a"][l], wts["wq_b"][l]], axis=1)
    wkv = jnp.concatenate([wts["wk"][l], wts["wv"][l]], axis=1)
    dwq = _mm(sv["qn"], dqab, "tn", name=f"d_w_q_b_{l}", tn=768, tk=DW_TOKENS).reshape(Q_RANK, 2, H, DA)
    dwkv = _mm(sv["kvn"], dkv, "tn", name=f"d_w_kv_b_{l}", tn=768, tk=DW_TOKENS).reshape(KV_RANK, 2, H, DA)
    da, db = dwq[:, 0], dwq[:, 1]
    swapped = jnp.concatenate([jnp.zeros((Q_RANK, H, NOPE), F32), db[..., NOPE + HALF_ROPE:NOPE + ROPE],
                               db[..., NOPE:NOPE + HALF_ROPE]], axis=-1)
    g["w_q_b"] = (da[..., :NOPE + ROPE] + swapped).reshape(Q_RANK, H * (NOPE + ROPE))
    g["w_kv_b"] = jnp.concatenate([dwkv[:, 0, :, :NOPE], dwkv[:, 1, :, :VDIM]], axis=-1).reshape(KV_RANK, H * (NOPE + VDIM))
    dqn = _mm(dqab, wq_ab, "nt", name=f"d_qn_{l}", tk=2 * H * DA)
    dkvn = _mm(dkv, wkv, "nt", name=f"d_kvn_{l}", tk=2 * H * DA)
    dqa, g["q_a_norm"] = _rmsnorm_bwd(sv["z"], Z_QA // Q_RANK, sm["q_a_norm"][l], dqn, name=f"q_a_norm_bwd_{l}")
    dkva, g["kv_a_norm"] = _rmsnorm_bwd(sv["z"], Z_KVA // KV_RANK, sm["kv_a_norm"][l], dkvn, name=f"kv_a_norm_bwd_{l}")

    dz = jnp.concatenate([dqa.astype(BF16), dkva.astype(BF16), dz3, du.astype(BF16), dfox, dz15], axis=1)
    g["w_in"] = _mm(sv["h2"], dz, "tn", name=f"d_w_in_{l}", tm=1024, tn=1024, tk=DW_TOKENS)
    dh2 = _mm(dz, wts["w_in"][l], "nt", name=f"d_h2_{l}", tn=1024, tk=NZ)
    dx1, g["mix_norm"] = _rmsnorm_bwd(x1, 0, sm["mix_norm"][l], dh2, dx2, name=f"mix_norm_bwd_{l}")
    return dx1, g


DW_TOKENS = 2048


def _local_step(x, target, wts, sm, late_weights=None, layer_done=None):
    s = x.shape[0]
    tabs = _rope_tables(s)
    acts = []
    xs = x
    for l in range(DEPTH):
        x1, gu1 = _ffn_fwd(xs, sm["ffn1_norm"][l], wts["ffn1_w_gu"][l], wts["ffn1_w_d2"][l], name=f"ffn1_fwd_{l}")
        x2, sv = _mix_fwd(l, x1, wts, sm, tabs)
        if l == 0 and late_weights is not None:
            late_weights(x2)
        x3, gu2 = _ffn_fwd(x2, sm["ffn2_norm"][l], wts["ffn2_w_gu"][l], wts["ffn2_w_d2"][l], name=f"ffn2_fwd_{l}")
        acts.append((xs, gu1, x1, sv, x2, gu2))
        xs = x3
    dx, g_final, loss = _loss_head(xs, sm["final_norm"], target, name="loss_head")
    grads = [dict() for _ in range(DEPTH)]
    for l in reversed(range(DEPTH)):
        x0, gu1, x1, sv, x2, gu2 = acts[l]
        g = grads[l]
        dx, dgu, act, hh, dy, g["ffn2_norm"] = _ffn_bwd(x2, dx, gu2, sm["ffn2_norm"][l], wts["ffn2_w_gu"][l], wts["ffn2_w_d2"][l],
                                                        name=f"ffn2_bwd_{l}")
        g["ffn2_w_down"] = _mm(act, dy, "tn", name=f"d_ffn2_w_down_{l}", tm=FF_SHARD, tn=1024, tk=DW_TOKENS)
        g["ffn2_w_gu"] = _mm(hh, dgu, "tn", name=f"d_ffn2_w_gu_{l}", tm=1024, tn=FF_SHARD, tk=DW_TOKENS, n_major_out=True)
        dx, gm = _mix_bwd(l, x1, dx, sv, wts, sm, tabs)
        g.update(gm)
        dx, dgu, act, hh, dy, g["ffn1_norm"] = _ffn_bwd(x0, dx, gu1, sm["ffn1_norm"][l], wts["ffn1_w_gu"][l], wts["ffn1_w_d2"][l],
                                                        name=f"ffn1_bwd_{l}")
        g["ffn1_w_down"] = _mm(act, dy, "tn", name=f"d_ffn1_w_down_{l}", tm=FF_SHARD, tn=1024, tk=DW_TOKENS)
        g["ffn1_w_gu"] = _mm(hh, dgu, "tn", name=f"d_ffn1_w_gu_{l}", tm=1024, tn=FF_SHARD, tk=DW_TOKENS, n_major_out=True)
        if layer_done is not None:
            sm = layer_done(l, g, sm)
    return loss, dx, grads, g_final


BIG = ["ffn1_w_gu", "ffn1_w_down", "w_in", "w_q_b", "w_kv_b", "w_out", "ffn2_w_gu", "ffn2_w_down"]
SMALL = ["ffn1_norm", "mix_norm", "q_a_norm", "kv_a_norm", "pool_w", "pool_scale", "fox_b_f", "ffn2_norm"]
SMALL_ROWS = 48


WEIGHT_VIEWS = ["ffn1_w_gu", "ffn1_w_d2", "w_in", "wq_a", "wq_b", "wk", "wv", "w_out", "ffn2_w_gu", "ffn2_w_d2"]


def _prepare_weights(gathered, wts):
    for (nm, l), w in gathered.items():
        if nm in ("ffn1_w_gu", "ffn2_w_gu"):
            wts[nm][l] = w
        elif nm in ("ffn1_w_down", "ffn2_w_down"):
            wts[nm[:5] + "w_d2"][l] = w.reshape(2, FF_SHARD, D)
        elif nm in ("w_in", "w_out"):
            wts[nm][l] = w.reshape(D, -1)
        elif nm == "w_q_b":
            wq = jnp.moveaxis(w, 0, 1).reshape(Q_RANK, H, NOPE + ROPE)
            zq = lambda n: jnp.zeros((Q_RANK, H, n), BF16)
            wts["wq_a"][l] = jnp.concatenate([wq, zq(DA - NOPE - ROPE)], axis=-1).reshape(Q_RANK, H * DA)
            wts["wq_b"][l] = jnp.concatenate([zq(NOPE), wq[..., NOPE + HALF_ROPE:], wq[..., NOPE:NOPE + HALF_ROPE],
                                              zq(DA - NOPE - ROPE)], axis=-1).reshape(Q_RANK, H * DA)
        else:
            wkv = jnp.moveaxis(w, 0, 1).reshape(KV_RANK, H, NOPE + VDIM)
            zk = jnp.zeros((KV_RANK, H, DA - NOPE), BF16)
            wts["wk"][l] = jnp.concatenate([wkv[..., :NOPE], zk], axis=-1).reshape(KV_RANK, H * DA)
            wts["wv"][l] = jnp.concatenate([wkv[..., NOPE:], zk], axis=-1).reshape(KV_RANK, H * DA)


def _chip_major(name, g):
    if name in ("ffn1_w_gu", "ffn2_w_gu"):
        return g
    if name in ("ffn1_w_down", "ffn2_w_down", "w_in", "w_out"):
        return g.reshape(N_CHIPS, g.shape[0] // N_CHIPS, g.shape[1])
    return jnp.moveaxis(g.reshape(g.shape[0], N_CHIPS, g.shape[1] // N_CHIPS), 1, 0)


def _pack_small(grads, g_final, loss):
    parts = []
    for l in range(DEPTH):
        for nm in SMALL:
            parts.append(grads[l][nm].reshape(-1))
    parts.append(g_final.reshape(-1))
    parts.append(loss.reshape(1))
    flat = jnp.concatenate(parts)
    return jnp.pad(flat, (0, SMALL_ROWS * D - flat.shape[0])).reshape(SMALL_ROWS, D)


def _unpack_small(packed, params):
    flat = packed.reshape(-1)
    out = {nm: [] for nm in SMALL}
    off = 0
    for l in range(DEPTH):
        for nm in SMALL:
            shp = params[nm].shape[1:]
            n = int(np.prod(shp))
            out[nm].append(flat[off:off + n].reshape(shp))
            off += n
    res = {nm: jnp.stack(v) for nm, v in out.items()}
    res["final_norm"] = flat[off:off + D]
    return res, flat[off + D]


def _update(name, w, g, m, v):
    shp = w.shape
    if w.ndim == 1:
        view = (1, shp[0])
    elif w.size <= 65536:
        view = (shp[0], w.size // shp[0])
    else:
        view = (w.size // shp[-1], shp[-1])
    tr = view[0]
    for cand in (512, 352, 256, 128):
        if view[0] % cand == 0 and view[0] > cand:
            tr = cand
            break
    d, mn, vn = _adamw(w.reshape(view), g.reshape(view), m.reshape(view), v.reshape(view), name="adamw_" + name, tr=tr)
    return d.reshape(shp), mn.reshape(shp), vn.reshape(shp)


WEIGHTS = ['ffn1_norm', 'ffn1_w_gu', 'ffn1_w_down', 'mix_norm', 'w_in', 'q_a_norm', 'w_q_b', 'kv_a_norm', 'w_kv_b', 'pool_w',
           'pool_scale', 'fox_b_f', 'w_out', 'ffn2_norm', 'ffn2_w_gu', 'ffn2_w_down', 'final_norm']


def kernel(x, ffn1_norm, ffn1_w_gu, ffn1_w_down, mix_norm, w_in, q_a_norm, w_q_b, kv_a_norm, w_kv_b, pool_w, pool_scale, fox_b_f, w_out, ffn2_norm, ffn2_w_gu, ffn2_w_down, final_norm, loss_target, m_ffn1_norm, m_ffn1_w_gu, m_ffn1_w_down, m_mix_norm, m_w_in, m_q_a_norm, m_w_q_b, m_kv_a_norm, m_w_kv_b, m_pool_w, m_pool_scale, m_fox_b_f, m_w_out, m_ffn2_norm, m_ffn2_w_gu, m_ffn2_w_down, m_final_norm, v_ffn1_norm, v_ffn1_w_gu, v_ffn1_w_down, v_mix_norm, v_w_in, v_q_a_norm, v_w_q_b, v_kv_a_norm, v_w_kv_b, v_pool_w, v_pool_scale, v_fox_b_f, v_w_out, v_ffn2_norm, v_ffn2_w_gu, v_ffn2_w_down, v_final_norm):
    params = dict(ffn1_norm=ffn1_norm, ffn1_w_gu=ffn1_w_gu, ffn1_w_down=ffn1_w_down, mix_norm=mix_norm, w_in=w_in, q_a_norm=q_a_norm,
                  w_q_b=w_q_b, kv_a_norm=kv_a_norm, w_kv_b=w_kv_b, pool_w=pool_w, pool_scale=pool_scale, fox_b_f=fox_b_f, w_out=w_out,
                  ffn2_norm=ffn2_norm, ffn2_w_gu=ffn2_w_gu, ffn2_w_down=ffn2_w_down, final_norm=final_norm)
    mom = dict(ffn1_norm=m_ffn1_norm, ffn1_w_gu=m_ffn1_w_gu, ffn1_w_down=m_ffn1_w_down, mix_norm=m_mix_norm, w_in=m_w_in,
               q_a_norm=m_q_a_norm, w_q_b=m_w_q_b, kv_a_norm=m_kv_a_norm, w_kv_b=m_w_kv_b, pool_w=m_pool_w, pool_scale=m_pool_scale,
               fox_b_f=m_fox_b_f, w_out=m_w_out, ffn2_norm=m_ffn2_norm, ffn2_w_gu=m_ffn2_w_gu, ffn2_w_down=m_ffn2_w_down,
               final_norm=m_final_norm)
    var = dict(ffn1_norm=v_ffn1_norm, ffn1_w_gu=v_ffn1_w_gu, ffn1_w_down=v_ffn1_w_down, mix_norm=v_mix_norm, w_in=v_w_in,
               q_a_norm=v_q_a_norm, w_q_b=v_w_q_b, kv_a_norm=v_kv_a_norm, w_kv_b=v_w_kv_b, pool_w=v_pool_w, pool_scale=v_pool_scale,
               fox_b_f=v_fox_b_f, w_out=v_w_out, ffn2_norm=v_ffn2_norm, ffn2_w_gu=v_ffn2_w_gu, ffn2_w_down=v_ffn2_w_down,
               final_norm=v_final_norm)

    shard = {}
    for nm in BIG:
        w = _pad_w_in(params[nm]) if nm == "w_in" else params[nm]
        for l in range(DEPTH):
            shard[(nm, l)] = w[l].astype(BF16)
    first = [(nm, 0) for nm in BIG if not nm.startswith("ffn2")]
    rest = [k for k in shard if k not in first]
    wts = {nm: [None] * DEPTH for nm in WEIGHT_VIEWS}
    wts["pool_w"] = params["pool_w"]
    got = _gather_blocking([shard[k] for k in first])
    _prepare_weights(dict(zip(first, got)), wts)
    sems, src_thru, land_thru, token = _gather_start([shard[k] for k in rest], got[0])
    sm = dict(params)
    sm["ffn1_norm"] = params["ffn1_norm"] + token[0, 0]

    def late_weights(x2):
        lands = _gather_forward(_gather_wait(sems, src_thru, land_thru, x2))
        _prepare_weights(dict(zip(rest, lands)), wts)

    pos = _position()
    flight = {}

    def reduce_to_chips(l, g, split):
        full = [_chip_major(nm, g[nm]) for nm in BIG]
        sib = _reduce_stage1(full)
        psum = _sum2_bf16(pos, full, sib, name=f"chip_sum_{l}")
        return full, sib, (_reduce_stage2_start(psum) if split else _reduce_stage2(psum))

    def layer_done(l, g, sm_now):
        if l == 0:
            return sm_now
        flight[l] = reduce_to_chips(l, g, True)
        sm_next = dict(sm_now)
        sm_next["ffn2_norm"] = sm_now["ffn2_norm"] + flight[l][2][3][0, 0]
        return sm_next

    loss, dx, grads, g_final = _local_step(x[0], loss_target[0], wts, sm, late_weights, layer_done)

    reduced = {}
    for l in range(DEPTH):
        if l in flight:
            full, sib, (sems2, ps_thru, lands2, _) = flight[l]
            recv = _reduce_stage2_wait(sems2, ps_thru, lands2, dx)
        else:
            full, sib, recv = reduce_to_chips(l, grads[l], False)
        for nm, r in zip(BIG, _sum5(pos, full, sib, recv, name=f"grad_sum_{l}")):
            reduced[(nm, l)] = r
    order = [(nm, l) for nm in BIG for l in range(DEPTH)]
    whole = dict(zip(order, _reduce_stage3([reduced[k] for k in order])))
    big_g = {nm: jnp.stack([whole[(nm, l)] for l in range(DEPTH)]) for nm in BIG}
    big_g["w_in"] = _unpad_w_in(big_g["w_in"])
    small_g, loss = _unpack_small(_allreduce_small(_pack_small(grads, g_final, loss)), params)
    gw = {**big_g, **small_g}

    delta, new_m, new_v = {}, {}, {}
    for nm in WEIGHTS:
        delta[nm], new_m[nm], new_v[nm] = _update(nm, params[nm], gw[nm], mom[nm], var[nm])
    return (loss, dx[None], *[gw[n] for n in WEIGHTS], *[delta[n] for n in WEIGHTS], *[new_m[n] for n in WEIGHTS],
            *[new_v[n] for n in WEIGHTS])
```

```python
import functools
import math

import jax
import jax.numpy as jnp
import numpy as np
from jax import lax
from jax.experimental import pallas as pl
from jax.experimental.pallas import tpu as pltpu

F32 = jnp.float32
BF16 = jnp.bfloat16
MESH = pl.DeviceIdType.MESH
HBM_SPEC = pl.BlockSpec(memory_space=pltpu.HBM)

D = 1024
DEPTH = 2
D_FF = 2816
FF_SHARD = 1408
N_CHIPS = 4
H = 6
NOPE, ROPE, VDIM = 64, 32, 64
HALF_ROPE = ROPE // 2
Q_RANK, KV_RANK = 256, 128
POOL_W = 256
FOX_D = 64
N_IN = 1830
NZ = 2048
ROPE_THETA = 10000.0
EPS = 1e-6
POOL_HALO = 16
Z_QA, Z_KVA, Z_KR, Z_POOL, Z_FOX, Z_F = 0, 256, 384, 512, 768, 1920

ADAM_LR, ADAM_B1, ADAM_B2, ADAM_EPS, ADAM_WD, ADAM_STEP = 0.001, 0.9, 0.999, 1e-08, 0.01, 10

VMEM_LIMIT_V7X = 56 * 1024 * 1024


def _cp(sem=None, vmem=VMEM_LIMIT_V7X):
    return pltpu.CompilerParams(dimension_semantics=sem, vmem_limit_bytes=vmem)


def _sigmoid(x):
    return 1.0 / (1.0 + jnp.exp(-x))


def _dot(a, b, dims):
    return lax.dot_general(a, b, (dims, ((), ())), preferred_element_type=F32)


NN = ((1,), (0,))
NT = ((1,), (1,))
TN = ((0,), (0,))


def _mm(a, b, mode, *, name, out_dtype=F32, add=None, alpha=None, tm=512, tn=512, tk=512, n_major_out=False):
    if mode == "nn":
        (m, k), (k2, n) = a.shape, b.shape
    elif mode == "nt":
        (m, k), (n, k2) = a.shape, b.shape
    else:
        (k, m), (k2, n) = a.shape, b.shape
    assert k == k2
    tm, tn, tk = min(tm, m), min(tn, n), min(tk, k)
    assert m % tm == 0 and n % tn == 0 and k % tk == 0, (name, m, n, k, tm, tn, tk)
    nk = k // tk
    dims = {"nn": NN, "nt": NT, "tn": TN}[mode]
    a_spec = pl.BlockSpec((tk, tm), lambda i, j, kk: (kk, i)) if mode == "tn" else pl.BlockSpec((tm, tk), lambda i, j, kk: (i, kk))
    b_spec = pl.BlockSpec((tn, tk), lambda i, j, kk: (j, kk)) if mode == "nt" else pl.BlockSpec((tk, tn), lambda i, j, kk: (kk, j))
    in_specs = [a_spec, b_spec]
    args = [a, b]
    if add is not None:
        in_specs.append(pl.BlockSpec((tm, tn), lambda i, j, kk: (i, j)))
        args.append(add)
    if n_major_out:
        out_shape = jax.ShapeDtypeStruct((n // tn, m, tn), out_dtype)
        out_spec = pl.BlockSpec((None, tm, tn), lambda i, j, kk: (j, i, 0))
    else:
        out_shape = jax.ShapeDtypeStruct((m, n), out_dtype)
        out_spec = pl.BlockSpec((tm, tn), lambda i, j, kk: (i, j))

    def body(*refs):
        a_ref, b_ref = refs[0], refs[1]
        add_ref = refs[2] if add is not None else None
        o_ref, acc = refs[-2], refs[-1]
        kk = pl.program_id(2)

        @pl.when(kk == 0)
        def _():
            acc[...] = jnp.zeros_like(acc)

        acc[...] += _dot(a_ref[...].astype(BF16), b_ref[...].astype(BF16), dims)

        @pl.when(kk == nk - 1)
        def _():
            r = acc[...]
            if alpha is not None:
                r = r * alpha
            if add_ref is not None:
                r = r + add_ref[...].astype(F32)
            o_ref[...] = r.astype(out_dtype)

    return pl.pallas_call(
        body, name=name, grid=(m // tm, n // tn, nk), in_specs=in_specs, out_specs=out_spec, out_shape=out_shape,
        scratch_shapes=[pltpu.VMEM((tm, tn), F32)],
        compiler_params=_cp(("parallel", "parallel", "arbitrary")),
    )(*args)


def _norm_mm(x, col_block, gain, w, *, name, tm=512):
    s = x.shape[0]
    k, n = w.shape
    tm = min(tm, s)

    def body(x_ref, g_ref, w_ref, z_ref, h_ref):
        xv = x_ref[...]
        r = lax.rsqrt(jnp.mean(xv * xv, axis=-1, keepdims=True) + EPS)
        hv = (xv * r * g_ref[...]).astype(BF16)
        h_ref[...] = hv
        z_ref[...] = _dot(hv, w_ref[...], NN)

    return pl.pallas_call(
        body, name=name, grid=(s // tm,),
        in_specs=[pl.BlockSpec((tm, k), lambda i: (i, col_block)), pl.BlockSpec((1, k), lambda i: (0, 0)),
                  pl.BlockSpec((k, n), lambda i: (0, 0))],
        out_specs=[pl.BlockSpec((tm, n), lambda i: (i, 0)), pl.BlockSpec((tm, k), lambda i: (i, 0))],
        out_shape=[jax.ShapeDtypeStruct((s, n), F32), jax.ShapeDtypeStruct((s, k), BF16)],
        compiler_params=_cp(("parallel",)),
    )(x, gain.reshape(1, k), w)


def _rmsnorm_bwd(x, col_block, gain, dh, dres=None, *, name, tm=512):
    s = x.shape[0]
    k = gain.shape[-1]
    tm = min(tm, s)

    def body(*refs):
        x_ref, g_ref, dh_ref = refs[0], refs[1], refs[2]
        dres_ref = refs[3] if dres is not None else None
        dx_ref, dg_ref = refs[-2], refs[-1]
        xv = x_ref[...]
        r = lax.rsqrt(jnp.mean(xv * xv, axis=-1, keepdims=True) + EPS)
        dhv = dh_ref[...].astype(F32)
        a = dhv * g_ref[...]
        dx = r * a - xv * (r * r * r) * jnp.mean(a * xv, axis=-1, keepdims=True)
        if dres_ref is not None:
            dx = dx + dres_ref[...]
        dx_ref[...] = dx

        @pl.when(pl.program_id(0) == 0)
        def _():
            dg_ref[...] = jnp.zeros_like(dg_ref)

        dg_ref[...] += jnp.sum(dhv * xv * r, axis=0, keepdims=True)

    in_specs = [pl.BlockSpec((tm, k), lambda i: (i, col_block)), pl.BlockSpec((1, k), lambda i: (0, 0)),
                pl.BlockSpec((tm, k), lambda i: (i, 0))]
    args = [x, gain.reshape(1, k), dh]
    if dres is not None:
        in_specs.append(pl.BlockSpec((tm, k), lambda i: (i, 0)))
        args.append(dres)
    dx, dg = pl.pallas_call(
        body, name=name, grid=(s // tm,), in_specs=in_specs,
        out_specs=[pl.BlockSpec((tm, k), lambda i: (i, 0)), pl.BlockSpec((1, k), lambda i: (0, 0))],
        out_shape=[jax.ShapeDtypeStruct((s, k), F32), jax.ShapeDtypeStruct((1, k), F32)],
        compiler_params=_cp(("arbitrary",)),
    )(*args)
    return dx, dg.reshape(k)


def _ffn_fwd(x, gain, w_gu4, w_d2, *, name, tm=256):
    s = x.shape[0]
    tm = min(tm, s)

    def body(x_ref, g_ref, wgu_ref, wd_ref, xo_ref, gu_ref):
        xv = x_ref[...]
        r = lax.rsqrt(jnp.mean(xv * xv, axis=-1, keepdims=True) + EPS)
        hv = (xv * r * g_ref[...]).astype(BF16)
        y = jnp.zeros((tm, D), F32)
        for j in range(2):
            g = _dot(hv, wgu_ref[j], NN)
            u = _dot(hv, wgu_ref[2 + j], NN)
            gu_ref[:, j * FF_SHARD:(j + 1) * FF_SHARD] = g.astype(BF16)
            gu_ref[:, D_FF + j * FF_SHARD:D_FF + (j + 1) * FF_SHARD] = u.astype(BF16)
            act = (g * _sigmoid(g) * u).astype(BF16)
            y = y + _dot(act, wd_ref[j], NN)
        xo_ref[...] = xv + 0.5 * y

    return pl.pallas_call(
        body, name=name, grid=(s // tm,),
        in_specs=[pl.BlockSpec((tm, D), lambda i: (i, 0)), pl.BlockSpec((1, D), lambda i: (0, 0)),
                  pl.BlockSpec((N_CHIPS, D, FF_SHARD), lambda i: (0, 0, 0), pipeline_mode=pl.Buffered(1)),
                  pl.BlockSpec((2, FF_SHARD, D), lambda i: (0, 0, 0), pipeline_mode=pl.Buffered(1))],
        out_specs=[pl.BlockSpec((tm, D), lambda i: (i, 0)), pl.BlockSpec((tm, 2 * D_FF), lambda i: (i, 0))],
        out_shape=[jax.ShapeDtypeStruct((s, D), F32), jax.ShapeDtypeStruct((s, 2 * D_FF), BF16)],
        compiler_params=_cp(("parallel",)),
    )(x, gain.reshape(1, D), w_gu4, w_d2)


FFN_ROW_CHUNK = 32


def _ffn_bwd(x, dxo, gu, gain, w_gu4, w_d2, *, name, tm=256):
    s = x.shape[0]
    tm = min(tm, s)

    def body(x_ref, dxo_ref, gu_ref, g_ref, wgu_ref, wd_ref, dx_ref, dgu_ref, act_ref, h_ref, dy_ref, dg_ref):
        xv = x_ref[...]
        r = lax.rsqrt(jnp.mean(xv * xv, axis=-1, keepdims=True) + EPS)
        xh = xv * r
        h_ref[...] = (xh * g_ref[...]).astype(BF16)
        dxov = dxo_ref[...]
        dy = (0.5 * dxov).astype(BF16)
        dy_ref[...] = dy
        dh = jnp.zeros((tm, D), F32)
        for j in range(2):
            gcols = slice(j * FF_SHARD, (j + 1) * FF_SHARD)
            ucols = slice(D_FF + j * FF_SHARD, D_FF + (j + 1) * FF_SHARD)
            dact = _dot(dy, wd_ref[j], NT)
            for r0 in range(0, tm, FFN_ROW_CHUNK):
                rows = slice(r0, r0 + FFN_ROW_CHUNK)
                g = gu_ref[rows, gcols].astype(F32)
                u = gu_ref[rows, ucols].astype(F32)
                sg = _sigmoid(g)
                silu = g * sg
                da = dact[rows]
                act_ref[rows, gcols] = (silu * u).astype(BF16)
                dgu_ref[rows, gcols] = (da * u * (sg * (1.0 + g * (1.0 - sg)))).astype(BF16)
                dgu_ref[rows, ucols] = (da * silu).astype(BF16)
            dh = dh + _dot(dgu_ref[:, gcols], wgu_ref[j], NT) + _dot(dgu_ref[:, ucols], wgu_ref[2 + j], NT)
        a = dh * g_ref[...]
        dx_ref[...] = dxov + r * a - xh * (r * jnp.mean(a * xh, axis=-1, keepdims=True))

        @pl.when(pl.program_id(0) == 0)
        def _():
            dg_ref[...] = jnp.zeros_like(dg_ref)

        dg_ref[...] += jnp.sum(dh * xh, axis=0, keepdims=True)

    row = lambda i: (i, 0)
    outs = pl.pallas_call(
        body, name=name, grid=(s // tm,),
        in_specs=[pl.BlockSpec((tm, D), row), pl.BlockSpec((tm, D), row), pl.BlockSpec((tm, 2 * D_FF), row),
                  pl.BlockSpec((1, D), lambda i: (0, 0)),
                  pl.BlockSpec((N_CHIPS, D, FF_SHARD), lambda i: (0, 0, 0), pipeline_mode=pl.Buffered(1)),
                  pl.BlockSpec((2, FF_SHARD, D), lambda i: (0, 0, 0), pipeline_mode=pl.Buffered(1))],
        out_specs=[pl.BlockSpec((tm, D), row), pl.BlockSpec((tm, 2 * D_FF), row), pl.BlockSpec((tm, D_FF), row),
                   pl.BlockSpec((tm, D), row), pl.BlockSpec((tm, D), row), pl.BlockSpec((1, D), lambda i: (0, 0))],
        out_shape=[jax.ShapeDtypeStruct((s, D), F32), jax.ShapeDtypeStruct((s, 2 * D_FF), BF16),
                   jax.ShapeDtypeStruct((s, D_FF), BF16), jax.ShapeDtypeStruct((s, D), BF16),
                   jax.ShapeDtypeStruct((s, D), BF16), jax.ShapeDtypeStruct((1, D), F32)],
        compiler_params=_cp(("arbitrary",)),
    )(x, dxo, gu, gain.reshape(1, D), w_gu4, w_d2)
    dx, dgu, act, h, dy, dg = outs
    return dx, dgu, act, h, dy, dg.reshape(D)


DA = 128
SCALE_MLA = 1.0 / math.sqrt(NOPE + ROPE)
SCALE_FOX = 1.0 / math.sqrt(FOX_D)


def _causal_blocks(nb, key_major):
    if key_major:
        pairs = [(i, j) for j in range(nb) for i in range(j, nb)]
    else:
        pairs = [(i, j) for i in range(nb) for j in range(i + 1)]
    return (jnp.asarray(np.array([p[0] for p in pairs], np.int32)), jnp.asarray(np.array([p[1] for p in pairs], np.int32)))


HEADS_PER_STEP = 2
ROW_CHUNK = 64

def _col_to_row(col):
    return jnp.broadcast_to(col, (col.shape[0], DA)).T[0:1, :]


def _attn_fwd(qa, ka, va, dv, *, name, t=512):
    h, s, _ = qa.shape
    t = min(t, s)
    nb = s // t
    g = 3
    qi, kj = _causal_blocks(nb, key_major=False)

    rc = min(ROW_CHUNK, t)

    def body(qi_ref, kj_ref, q_ref, k_ref, v_ref, o_ref, lse_ref, m_sc, acc_sc, p_sc, a_sc):
        n = pl.program_id(1)
        i, j = qi_ref[n], kj_ref[n]

        @pl.when(j == 0)
        def _():
            m_sc[...] = jnp.full_like(m_sc, -jnp.inf)
            acc_sc[...] = jnp.zeros_like(acc_sc)

        def step(masked):
            scs = [_dot(q_ref[hh], k_ref[hh], NT) for hh in range(g)]
            for r0 in range(0, t, rc):
                rows = slice(r0, r0 + rc)
                for hh in range(g):
                    sr = scs[hh][rows]
                    if masked:
                        row = lax.broadcasted_iota(jnp.int32, (rc, t), 0) + r0
                        col = lax.broadcasted_iota(jnp.int32, (rc, t), 1)
                        sr = jnp.where(col <= row, sr, -jnp.inf)
                    m_old = m_sc[hh, rows]
                    m_new = jnp.maximum(m_old, jnp.max(sr, axis=-1, keepdims=True))
                    p_sc[hh, rows] = jnp.exp(sr - m_new).astype(BF16)
                    a_sc[hh, rows] = jnp.exp(m_old - m_new)
                    m_sc[hh, rows] = m_new
            for hh in range(g):
                acc_sc[hh] = a_sc[hh] * acc_sc[hh] + _dot(p_sc[hh], v_ref[hh], NN)

        @pl.when(j < i)
        def _():
            step(False)

        @pl.when(j == i)
        def _():
            step(True)
            for hh in range(g):
                acc = acc_sc[hh]
                l = acc[:, dv:dv + 1]
                o_ref[hh] = acc[:, :dv] / l
                lse_ref[hh] = _col_to_row(m_sc[hh] + jnp.log(l))

    qmap = lambda hg, n, qi_r, kj_r: (hg, qi_r[n], 0)
    kmap = lambda hg, n, qi_r, kj_r: (hg, kj_r[n], 0)
    return pl.pallas_call(
        body, name=name,
        grid_spec=pltpu.PrefetchScalarGridSpec(
            num_scalar_prefetch=2, grid=(h // g, qi.shape[0]),
            in_specs=[pl.BlockSpec((g, t, DA), qmap), pl.BlockSpec((g, t, DA), kmap), pl.BlockSpec((g, t, DA), kmap)],
            out_specs=[pl.BlockSpec((g, t, dv), qmap), pl.BlockSpec((g, 1, t), lambda hg, n, qi_r, kj_r: (hg, 0, qi_r[n]))],
            scratch_shapes=[pltpu.VMEM((g, t, 1), F32), pltpu.VMEM((g, t, DA), F32), pltpu.VMEM((g, t, t), BF16),
                            pltpu.VMEM((g, t, 1), F32)]),
        out_shape=[jax.ShapeDtypeStruct((h, s, dv), F32), jax.ShapeDtypeStruct((h, 1, s), F32)],
        compiler_params=_cp(("parallel", "arbitrary")),
    )(qi, kj, qa, ka, va)


def _attn_bwd(qa, ka, va, doa, lse_row, delta_row, decay, *, name, t=512):
    h, s, _ = qa.shape
    t = min(t, s)
    nb = s // t
    g = 3
    rc = min(ROW_CHUNK, t)
    qi, kj = _causal_blocks(nb, key_major=True)
    nsteps = qi.shape[0]

    def body(*refs):
        qi_ref, kj_ref, q_ref, k_ref, v_ref, do_ref, lse_ref, dl_ref = refs[:8]
        p_sc, ds_sc = refs[-2:]
        if decay:
            dq_ref, dk_ref, dv_ref, dcq_ref, dck_ref, dq_acc, dk_acc, dv_acc, dcq_acc, dck_acc = refs[8:-2]
        else:
            dq_ref, dk_ref, dv_ref, dq_acc, dk_acc, dv_acc = refs[8:-2]
        n = pl.program_id(1)
        i, j = qi_ref[n], kj_ref[n]

        @pl.when(n == 0)
        def _():
            dq_acc[...] = jnp.zeros_like(dq_acc)
            if decay:
                dcq_acc[...] = jnp.zeros_like(dcq_acc)

        @pl.when(i == j)
        def _():
            dk_acc[...] = jnp.zeros_like(dk_acc)
            dv_acc[...] = jnp.zeros_like(dv_acc)
            if decay:
                dck_acc[...] = jnp.zeros_like(dck_acc)

        def step(masked):
            sts = [_dot(k_ref[hh], q_ref[hh], NT) for hh in range(g)]
            dpts = [_dot(v_ref[hh], do_ref[hh], NT) for hh in range(g)]
            dcq = [jnp.zeros((1, t), F32) for _ in range(g)]
            for r0 in range(0, t, rc):
                rows = slice(r0, r0 + rc)
                for hh in range(g):
                    st = sts[hh][rows]
                    if masked:
                        row = lax.broadcasted_iota(jnp.int32, (rc, t), 0) + r0
                        col = lax.broadcasted_iota(jnp.int32, (rc, t), 1)
                        st = jnp.where(row <= col, st, -jnp.inf)
                    pt = jnp.exp(st - lse_ref[hh])
                    dst = pt * (dpts[hh][rows] - dl_ref[hh])
                    p_sc[hh, rows] = pt.astype(BF16)
                    ds_sc[hh, rows] = dst.astype(BF16)
                    if decay:
                        dcq[hh] = dcq[hh] + jnp.sum(dst, axis=0, keepdims=True)
                        dck_acc[hh, rows] -= jnp.sum(dst, axis=1, keepdims=True)
            for hh in range(g):
                dv_acc[hh] += _dot(p_sc[hh], do_ref[hh], NN)
                dk_acc[hh] += _dot(ds_sc[hh], q_ref[hh], NN)
                dq_acc[hh, i] += _dot(ds_sc[hh], k_ref[hh], TN)
                if decay:
                    dcq_acc[hh, i] += dcq[hh]

        @pl.when(i > j)
        def _():
            step(False)

        @pl.when(i == j)
        def _():
            step(True)

        @pl.when(i == nb - 1)
        def _():
            dk_ref[...] = dk_acc[...]
            dv_ref[...] = dv_acc[...]
            if decay:
                for hh in range(g):
                    dck_ref[hh] = _col_to_row(dck_acc[hh])

        @pl.when(n == nsteps - 1)
        def _():
            dq_ref[...] = dq_acc[...]
            if decay:
                dcq_ref[...] = dcq_acc[...]

    kmap = lambda hg, n, qi_r, kj_r: (hg, kj_r[n], 0)
    qmap = lambda hg, n, qi_r, kj_r: (hg, qi_r[n], 0)
    qrow = lambda hg, n, qi_r, kj_r: (hg, 0, qi_r[n])
    krow = lambda hg, n, qi_r, kj_r: (hg, 0, kj_r[n])
    whole = lambda hg, n, qi_r, kj_r: (hg, 0, 0, 0)
    in_specs = [pl.BlockSpec((g, t, DA), qmap), pl.BlockSpec((g, t, DA), kmap), pl.BlockSpec((g, t, DA), kmap),
                pl.BlockSpec((g, t, DA), qmap), pl.BlockSpec((g, 1, t), qrow), pl.BlockSpec((g, 1, t), qrow)]
    out_specs = [pl.BlockSpec((g, nb, t, DA), whole), pl.BlockSpec((g, t, DA), kmap), pl.BlockSpec((g, t, DA), kmap)]
    out_shape = [jax.ShapeDtypeStruct((h, nb, t, DA), F32), jax.ShapeDtypeStruct((h, s, DA), F32), jax.ShapeDtypeStruct((h, s, DA), F32)]
    scratch = [pltpu.VMEM((g, nb, t, DA), F32), pltpu.VMEM((g, t, DA), F32), pltpu.VMEM((g, t, DA), F32)]
    if decay:
        out_specs += [pl.BlockSpec((g, nb, 1, t), whole), pl.BlockSpec((g, 1, t), krow)]
        out_shape += [jax.ShapeDtypeStruct((h, nb, 1, t), F32), jax.ShapeDtypeStruct((h, 1, s), F32)]
        scratch += [pltpu.VMEM((g, nb, 1, t), F32), pltpu.VMEM((g, t, 1), F32)]
    scratch += [pltpu.VMEM((g, t, t), BF16), pltpu.VMEM((g, t, t), BF16)]
    outs = pl.pallas_call(
        body, name=name,
        grid_spec=pltpu.PrefetchScalarGridSpec(num_scalar_prefetch=2, grid=(h // g, nsteps), in_specs=in_specs, out_specs=out_specs,
                                               scratch_shapes=scratch),
        out_shape=out_shape, compiler_params=_cp(("parallel", "arbitrary")),
    )(qi, kj, qa, ka, va, doa, lse_row, delta_row)
    outs = list(outs)
    outs[0] = outs[0].reshape(h, s, DA)
    if decay:
        outs[3] = outs[3].reshape(h, 1, s)
    return outs


def _sel(rows, cols, pairs, value=1.0):
    m = np.zeros((rows, cols), np.float32)
    for r, c in pairs:
        m[r, c] = value
    return jnp.asarray(m, BF16)


def _lane_row(lanes):
    m = np.zeros((1, DA), np.float32)
    m[0, list(lanes)] = 1.0
    return jnp.asarray(m)


def _rms(xv, gain):
    r = lax.rsqrt(jnp.mean(xv * xv, axis=-1, keepdims=True) + EPS)
    return xv * r * gain


def _mla_q_prep(z, gain, wq_a, wq_b, cq, sq, *, name, tm=512):
    s = z.shape[0]
    tm = min(tm, s)

    def body(z_ref, g_ref, wa_ref, wb_ref, c_ref, s_ref, qa_ref, qn_ref):
        qn = _rms(z_ref[...], g_ref[...]).astype(BF16)
        qn_ref[...] = qn
        c, sn = c_ref[...], s_ref[...]
        for hh in range(H):
            cols = slice(hh * DA, (hh + 1) * DA)
            qa_ref[hh] = (_dot(qn, wa_ref[:, cols], NN) * c + _dot(qn, wb_ref[:, cols], NN) * sn).astype(BF16)

    row = lambda i: (i, 0)
    fixed = lambda i: (0, 0)
    return pl.pallas_call(
        body, name=name, grid=(s // tm,),
        in_specs=[pl.BlockSpec((tm, Q_RANK), lambda i: (i, Z_QA // Q_RANK)), pl.BlockSpec((1, Q_RANK), fixed),
                  pl.BlockSpec((Q_RANK, H * DA), fixed), pl.BlockSpec((Q_RANK, H * DA), fixed),
                  pl.BlockSpec((tm, DA), row), pl.BlockSpec((tm, DA), row)],
        out_specs=[pl.BlockSpec((H, tm, DA), lambda i: (0, i, 0)), pl.BlockSpec((tm, Q_RANK), row)],
        out_shape=[jax.ShapeDtypeStruct((H, s, DA), BF16), jax.ShapeDtypeStruct((s, Q_RANK), BF16)],
        compiler_params=_cp(("parallel",)),
    )(z, gain.reshape(1, Q_RANK), wq_a, wq_b, cq, sq)


def _mla_kv_prep(z, gain, wk, wv, ck, sk, *, name, tm=512):
    s = z.shape[0]
    tm = min(tm, s)
    one = _lane_row([VDIM])

    def body(zkv_ref, z3_ref, z15_ref, g_ref, wk_ref, wv_ref, c_ref, s_ref, one_ref, ka_ref, va_ref, kvn_ref):
        kvn = _rms(zkv_ref[...], g_ref[...]).astype(BF16)
        kvn_ref[...] = kvn
        kpe = z3_ref[...] * c_ref[...] + z15_ref[...] * s_ref[...]
        for hh in range(H):
            cols = slice(hh * DA, (hh + 1) * DA)
            ka_ref[hh] = (_dot(kvn, wk_ref[:, cols], NN) + kpe).astype(BF16)
            va_ref[hh] = (_dot(kvn, wv_ref[:, cols], NN) + one_ref[...]).astype(BF16)

    row = lambda i: (i, 0)
    fixed = lambda i: (0, 0)
    blk = lambda c: pl.BlockSpec((tm, DA), lambda i: (i, c))
    heads = pl.BlockSpec((H, tm, DA), lambda i: (0, i, 0))
    return pl.pallas_call(
        body, name=name, grid=(s // tm,),
        in_specs=[blk(Z_KVA // DA), blk(Z_KR // DA), blk(Z_F // DA), pl.BlockSpec((1, KV_RANK), fixed),
                  pl.BlockSpec((KV_RANK, H * DA), fixed), pl.BlockSpec((KV_RANK, H * DA), fixed),
                  pl.BlockSpec((tm, DA), row), pl.BlockSpec((tm, DA), row), pl.BlockSpec((1, DA), fixed)],
        out_specs=[heads, heads, pl.BlockSpec((tm, KV_RANK), row)],
        out_shape=[jax.ShapeDtypeStruct((H, s, DA), BF16), jax.ShapeDtypeStruct((H, s, DA), BF16),
                   jax.ShapeDtypeStruct((s, KV_RANK), BF16)],
        compiler_params=_cp(("parallel",)),
    )(z, z, z, gain.reshape(1, KV_RANK), wk, wv, ck, sk, one)


DEC_C = (FOX_D, FOX_D + 1, FOX_D + 2)
DEC_1 = (FOX_D + 3, FOX_D + 4, FOX_D + 5)


def _fox_prep(z, c3t, *, name, tm=512):
    s = z.shape[0]
    tm = min(tm, s)
    w = H * FOX_D
    left = [(r, r) for r in range(FOX_D)]
    right = [(FOX_D + r, r) for r in range(FOX_D)]
    pq = jnp.stack([_sel(DA, DA, left, SCALE_FOX), _sel(DA, DA, right, SCALE_FOX)])
    pk = jnp.stack([_sel(DA, DA, left), _sel(DA, DA, right)])
    pcq = jnp.stack([_sel(32, DA, [(hh + 8 * k, DEC_C[k]) for k in range(3)]) for hh in range(H)])
    pck = jnp.stack([_sel(32, DA, [(hh + 8 * k, DEC_1[k]) for k in range(3)], -1.0) for hh in range(H)])
    rows3 = jnp.concatenate([_lane_row(DEC_1), _lane_row(DEC_C), _lane_row([FOX_D])], axis=0)

    def body(zq_ref, zk_ref, zv_ref, c_ref, pq_ref, pk_ref, pcq_ref, pck_ref, r_ref, qa_ref, ka_ref, va_ref):
        c3 = c_ref[...]
        for pair in range(H // 2):
            lanes = slice(pair * DA, (pair + 1) * DA)
            zq, zk, zv = zq_ref[:, lanes].astype(BF16), zk_ref[:, lanes].astype(BF16), zv_ref[:, lanes].astype(BF16)
            for side in range(2):
                hh = 2 * pair + side
                qa_ref[hh] = (_dot(zq, pq_ref[side], NN) + _dot(c3, pcq_ref[hh], TN) + r_ref[0:1, :]).astype(BF16)
                ka_ref[hh] = (_dot(zk, pk_ref[side], NN) + _dot(c3, pck_ref[hh], TN) + r_ref[1:2, :]).astype(BF16)
                va_ref[hh] = (_dot(zv, pk_ref[side], NN) + r_ref[2:3, :]).astype(BF16)

    fixed2 = lambda i: (0, 0)
    fixed3 = lambda i: (0, 0, 0)
    heads = pl.BlockSpec((H, tm, DA), lambda i: (0, i, 0))
    zblk = lambda c: pl.BlockSpec((tm, w), lambda i: (i, c))
    return pl.pallas_call(
        body, name=name, grid=(s // tm,),
        in_specs=[zblk(Z_FOX // w), zblk(Z_FOX // w + 1), zblk(Z_FOX // w + 2), pl.BlockSpec((32, tm), lambda i: (0, i)),
                  pl.BlockSpec((2, DA, DA), fixed3), pl.BlockSpec((2, DA, DA), fixed3),
                  pl.BlockSpec((H, 32, DA), fixed3), pl.BlockSpec((H, 32, DA), fixed3), pl.BlockSpec((3, DA), fixed2)],
        out_specs=[heads, heads, heads], out_shape=[jax.ShapeDtypeStruct((H, s, DA), BF16)] * 3,
        compiler_params=_cp(("parallel",)),
    )(z, z, z, c3t, pq, pk, pcq, pck, rows3)


def _mix_out(oa, yb, oc, w_out, x1, *, name, tm=512):
    s = yb.shape[0]
    tm = min(tm, s)
    e2 = jnp.stack([_sel(VDIM, DA, [(r, r) for r in range(VDIM)]), _sel(VDIM, DA, [(r, VDIM + r) for r in range(VDIM)])])

    def body(oa_ref, yb_ref, oc_ref, e_ref, w_ref, x_ref, x2_ref, cat_ref):
        def pairs(o_ref):
            return [(_dot(o_ref[2 * p].astype(BF16), e_ref[0], NN) + _dot(o_ref[2 * p + 1].astype(BF16), e_ref[1], NN)).astype(BF16)
                    for p in range(H // 2)]

        cat = jnp.concatenate(pairs(oa_ref) + [yb_ref[...].astype(BF16)] + pairs(oc_ref), axis=1)
        cat_ref[...] = cat
        x2_ref[...] = x_ref[...] + _dot(cat, w_ref[...], NN)

    row = lambda i: (i, 0)
    heads = pl.BlockSpec((H, tm, VDIM), lambda i: (0, i, 0))
    return pl.pallas_call(
        body, name=name, grid=(s // tm,),
        in_specs=[heads, pl.BlockSpec((tm, POOL_W), row), heads, pl.BlockSpec((2, VDIM, DA), lambda i: (0, 0, 0)),
                  pl.BlockSpec((D, D), lambda i: (0, 0)), pl.BlockSpec((tm, D), row)],
        out_specs=[pl.BlockSpec((tm, D), row), pl.BlockSpec((tm, D), row)],
        out_shape=[jax.ShapeDtypeStruct((s, D), F32), jax.ShapeDtypeStruct((s, D), BF16)],
        compiler_params=_cp(("parallel",)),
    )(oa, yb, oc, e2, w_out, x1)


def _mix_out_bwd(dx2b, w_out, oa, oc, *, name, tm=512):
    s = dx2b.shape[0]
    tm = min(tm, s)
    f2 = jnp.stack([_sel(DA, DA, [(r, r) for r in range(VDIM)]), _sel(DA, DA, [(VDIM + r, r) for r in range(VDIM)])])
    nv = H * VDIM

    def body(dx_ref, w_ref, oa_ref, oc_ref, f_ref, doa_ref, doc_ref, dyb_ref, dla_ref, dlc_ref):
        dcat = _dot(dx_ref[...], w_ref[...], NT)
        dyb_ref[...] = dcat[:, nv:nv + POOL_W]
        for base, o_ref, do_ref, dl_ref in ((0, oa_ref, doa_ref, dla_ref), (nv + POOL_W, oc_ref, doc_ref, dlc_ref)):
            for p in range(H // 2):
                blk = dcat[:, base + p * DA:base + (p + 1) * DA].astype(BF16)
                for side in range(2):
                    hh = 2 * p + side
                    do = _dot(blk, f_ref[side], NN)
                    do_ref[hh] = do.astype(BF16)
                    dl_ref[hh] = _col_to_row(jnp.sum(do[:, :VDIM] * o_ref[hh], axis=-1, keepdims=True))

    row = lambda i: (i, 0)
    heads = lambda w: pl.BlockSpec((H, tm, w), lambda i: (0, i, 0))
    return pl.pallas_call(
        body, name=name, grid=(s // tm,),
        in_specs=[pl.BlockSpec((tm, D), row), pl.BlockSpec((D, D), lambda i: (0, 0)), heads(VDIM), heads(VDIM),
                  pl.BlockSpec((2, DA, DA), lambda i: (0, 0, 0))],
        out_specs=[heads(DA), heads(DA), pl.BlockSpec((tm, POOL_W), row),
                   pl.BlockSpec((H, 1, tm), lambda i: (0, 0, i)), pl.BlockSpec((H, 1, tm), lambda i: (0, 0, i))],
        out_shape=[jax.ShapeDtypeStruct((H, s, DA), BF16), jax.ShapeDtypeStruct((H, s, DA), BF16),
                   jax.ShapeDtypeStruct((s, POOL_W), F32), jax.ShapeDtypeStruct((H, 1, s), F32), jax.ShapeDtypeStruct((H, 1, s), F32)],
        compiler_params=_cp(("parallel",)),
    )(dx2b, w_out, oa, oc, f2)


def _mla_bwd_prep(dqa, dka, dva, dft, cq, sq, ck, sk, *, name, tm=512):
    s = dqa.shape[1]
    tm = min(tm, s)
    keep = _lane_row(range(NOPE))

    def body(dq_ref, dk_ref, dv_ref, dft_ref, cq_ref, sq_ref, ck_ref, sk_ref, keep_ref, dqab_ref, dkv_ref, dz3_ref, dz15_ref):
        cqv, sqv = cq_ref[...], sq_ref[...]
        dkpe = jnp.zeros((tm, DA), F32)
        for hh in range(H):
            lanes = slice(hh * DA, (hh + 1) * DA)
            dq = dq_ref[hh]
            dqab_ref[:, lanes] = (dq * cqv).astype(BF16)
            dqab_ref[:, H * DA + hh * DA:H * DA + (hh + 1) * DA] = (dq * sqv).astype(BF16)
            dk = dk_ref[hh]
            dkpe = dkpe + dk
            dkv_ref[:, lanes] = (dk * keep_ref[...]).astype(BF16)
            dkv_ref[:, H * DA + hh * DA:H * DA + (hh + 1) * DA] = (dv_ref[hh] * keep_ref[...]).astype(BF16)
        dz3_ref[...] = (dkpe * ck_ref[...]).astype(BF16)
        dz15_ref[...] = (dkpe * sk_ref[...] + dft_ref[...]).astype(BF16)

    row = lambda i: (i, 0)
    heads = pl.BlockSpec((H, tm, DA), lambda i: (0, i, 0))
    tab = pl.BlockSpec((tm, DA), row)
    return pl.pallas_call(
        body, name=name, grid=(s // tm,),
        in_specs=[heads, heads, heads, tab, tab, tab, tab, tab, pl.BlockSpec((1, DA), lambda i: (0, 0))],
        out_specs=[pl.BlockSpec((tm, 2 * H * DA), row), pl.BlockSpec((tm, 2 * H * DA), row), tab, tab],
        out_shape=[jax.ShapeDtypeStruct((s, 2 * H * DA), BF16), jax.ShapeDtypeStruct((s, 2 * H * DA), BF16),
                   jax.ShapeDtypeStruct((s, DA), BF16), jax.ShapeDtypeStruct((s, DA), BF16)],
        compiler_params=_cp(("parallel",)),
    )(dqa, dka, dva, dft, cq, sq, ck, sk, keep)


def _fox_bwd_prep(dfqa, dfka, dfva, *, name, tm=512):
    s = dfqa.shape[1]
    tm = min(tm, s)
    place = lambda v: jnp.stack([_sel(DA, DA, [(r, r) for r in range(FOX_D)], v), _sel(DA, DA, [(r, FOX_D + r) for r in range(FOX_D)], v)])
    gq, gk = place(SCALE_FOX), place(1.0)

    def body(dq_ref, dk_ref, dv_ref, gq_ref, gk_ref, dz_ref):
        for part, (d_ref, g_ref) in enumerate(((dq_ref, gq_ref), (dk_ref, gk_ref), (dv_ref, gk_ref))):
            for p in range(H // 2):
                blk = _dot(d_ref[2 * p].astype(BF16), g_ref[0], NN) + _dot(d_ref[2 * p + 1].astype(BF16), g_ref[1], NN)
                lo = part * H * FOX_D + p * DA
                dz_ref[:, lo:lo + DA] = blk.astype(BF16)

    heads = pl.BlockSpec((H, tm, DA), lambda i: (0, i, 0))
    sel = pl.BlockSpec((2, DA, DA), lambda i: (0, 0, 0))
    return pl.pallas_call(
        body, name=name, grid=(s // tm,), in_specs=[heads, heads, heads, sel, sel],
        out_specs=pl.BlockSpec((tm, 3 * H * FOX_D), lambda i: (i, 0)),
        out_shape=jax.ShapeDtypeStruct((s, 3 * H * FOX_D), BF16), compiler_params=_cp(("parallel",)),
    )(dfqa, dfka, dfva, gq, gk)


def _lane_scan(x, s, reverse):
    lane = lax.broadcasted_iota(jnp.int32, x.shape, 1)
    sh = 1
    while sh < s:
        if reverse:
            x = x + jnp.where(lane < s - sh, pltpu.roll(x, s - sh, axis=1), 0.0)
        else:
            x = x + jnp.where(lane >= sh, pltpu.roll(x, sh, axis=1), 0.0)
        sh *= 2
    return x


def _gate_fwd(z, col_block, bias, *, name):
    s = z.shape[0]

    def body(z_ref, b_ref, f_ref, c_ref):
        ft = z_ref[...].T[0:8, :]
        f_ref[...] = ft
        xg = ft + b_ref[...]
        lf = jnp.minimum(xg, 0.0) - jnp.log(1.0 + jnp.exp(-jnp.abs(xg)))
        c = _lane_scan(lf, s, False)
        hi = c.astype(BF16).astype(F32)
        r = c - hi
        mid = r.astype(BF16).astype(F32)
        lo = r - mid
        c_ref[...] = jnp.concatenate([hi, mid, lo, jnp.zeros_like(hi)], axis=0).astype(BF16)

    return pl.pallas_call(
        body, name=name, grid=(1,),
        in_specs=[pl.BlockSpec((s, 128), lambda i: (0, col_block)), pl.BlockSpec((8, 1), lambda i: (0, 0))],
        out_specs=[pl.BlockSpec((8, s), lambda i: (0, 0)), pl.BlockSpec((32, s), lambda i: (0, 0))],
        out_shape=[jax.ShapeDtypeStruct((8, s), F32), jax.ShapeDtypeStruct((32, s), BF16)],
        compiler_params=_cp(("arbitrary",)))(z, bias)


def _gate_bwd(ft, bias, dc, *, name):
    s = ft.shape[1]

    def body(f_ref, b_ref, dc_ref, df_ref, db_ref):
        xg = f_ref[...] + b_ref[...]
        dlf = _lane_scan(dc_ref[...], s, True)
        df = dlf * _sigmoid(-xg)
        db_ref[...] = jnp.sum(df, axis=-1, keepdims=True)
        df_ref[...] = jnp.concatenate([df, jnp.zeros((DA - 8, s), F32)], axis=0).T

    return pl.pallas_call(body, name=name, out_shape=[jax.ShapeDtypeStruct((s, DA), F32), jax.ShapeDtypeStruct((8, 1), F32)],
                          compiler_params=_cp())(ft, bias, dc)


def _pool_lane_consts(tm, i):
    lane = lax.broadcasted_iota(jnp.int32, (tm, POOL_W), 1)
    tok = lax.broadcasted_iota(jnp.int32, (tm, POOL_W), 0) + i * tm
    win = jnp.where(lane < 64, 2, jnp.where(lane < 128, 4, jnp.where(lane < 192, 8, 16)))
    cnt = jnp.minimum(tok + 1, win).astype(F32)
    return lane, tok, cnt


def _pick_window(lane, s2, s4, s8, s16):
    return jnp.where(lane < 64, s2, jnp.where(lane < 128, s4, jnp.where(lane < 192, s8, s16)))


def _pool_fwd(z, col_block, bd, scale, *, name, tm=512):
    s = z.shape[0]
    tm = min(tm, s)
    hb = tm // POOL_HALO

    def body(u_ref, halo_ref, bd_ref, sc_ref, y_ref, p_ref, buf):
        i = pl.program_id(0)
        buf[0:POOL_HALO, :] = halo_ref[...] * (i > 0).astype(F32)
        buf[POOL_HALO:, :] = u_ref[...]

        def back(k):
            return buf[POOL_HALO - k:POOL_HALO - k + tm, :]

        u = u_ref[...]
        s2 = u + back(1)
        s4 = s2 + back(2) + back(3)
        s8 = s4 + back(4) + back(5) + back(6) + back(7)
        s16 = s8
        for k in range(8, 16):
            s16 = s16 + back(k)
        lane, _, cnt = _pool_lane_consts(tm, i)
        pooled = (_pick_window(lane, s2, s4, s8, s16) / cnt - u).astype(BF16)
        p_ref[...] = pooled
        y_ref[...] = _dot(pooled, bd_ref[...], NN) * sc_ref[...]

    return pl.pallas_call(
        body, name=name, grid=(s // tm,),
        in_specs=[pl.BlockSpec((tm, POOL_W), lambda i: (i, col_block)),
                  pl.BlockSpec((POOL_HALO, POOL_W), lambda i: (jnp.maximum(i * hb - 1, 0), col_block)),
                  pl.BlockSpec((POOL_W, POOL_W), lambda i: (0, 0)), pl.BlockSpec((1, POOL_W), lambda i: (0, 0))],
        out_specs=[pl.BlockSpec((tm, POOL_W), lambda i: (i, 0)), pl.BlockSpec((tm, POOL_W), lambda i: (i, 0))],
        out_shape=[jax.ShapeDtypeStruct((s, POOL_W), F32), jax.ShapeDtypeStruct((s, POOL_W), BF16)],
        scratch_shapes=[pltpu.VMEM((tm + POOL_HALO, POOL_W), F32)],
        compiler_params=_cp(("parallel",)),
    )(z, z, bd, scale.reshape(1, POOL_W))


def _pool_bwd_a(dy, pooled, bd, scale, *, name, tm=512):
    s = dy.shape[0]
    tm = min(tm, s)

    def body(dy_ref, p_ref, bd_ref, sc_ref, dq_ref, dys_ref, dsc_ref):
        i = pl.program_id(0)
        dyv = dy_ref[...]
        y0 = _dot(p_ref[...], bd_ref[...], NN)
        dys = (dyv * sc_ref[...]).astype(BF16)
        dys_ref[...] = dys
        dp = _dot(dys, bd_ref[...], NT)
        _, _, cnt = _pool_lane_consts(tm, i)
        dq_ref[:, 0:POOL_W] = dp / cnt
        dq_ref[:, POOL_W:] = dp

        @pl.when(i == 0)
        def _():
            dsc_ref[...] = jnp.zeros_like(dsc_ref)

        dsc_ref[...] += jnp.sum(dyv * y0, axis=0, keepdims=True)

    row = lambda i: (i, 0)
    dq, dys, dsc = pl.pallas_call(
        body, name=name, grid=(s // tm,),
        in_specs=[pl.BlockSpec((tm, POOL_W), row), pl.BlockSpec((tm, POOL_W), row),
                  pl.BlockSpec((POOL_W, POOL_W), lambda i: (0, 0)), pl.BlockSpec((1, POOL_W), lambda i: (0, 0))],
        out_specs=[pl.BlockSpec((tm, 2 * POOL_W), row), pl.BlockSpec((tm, POOL_W), row), pl.BlockSpec((1, POOL_W), lambda i: (0, 0))],
        out_shape=[jax.ShapeDtypeStruct((s, 2 * POOL_W), F32), jax.ShapeDtypeStruct((s, POOL_W), BF16),
                   jax.ShapeDtypeStruct((1, POOL_W), F32)],
        compiler_params=_cp(("arbitrary",)),
    )(dy, pooled, bd, scale.reshape(1, POOL_W))
    return dq, dys, dsc.reshape(POOL_W)


def _pool_bwd_b(dq, *, name, tm=512):
    s = dq.shape[0]
    tm = min(tm, s)
    hb = tm // POOL_HALO
    nblk = s // tm

    def body(q_ref, dp_ref, halo_ref, du_ref, buf):
        i = pl.program_id(0)
        buf[0:tm, :] = q_ref[...]
        buf[tm:, :] = halo_ref[...] * (i < nblk - 1).astype(F32)

        def ahead(k):
            return buf[k:k + tm, :]

        q = q_ref[...]
        s2 = q + ahead(1)
        s4 = s2 + ahead(2) + ahead(3)
        s8 = s4 + ahead(4) + ahead(5) + ahead(6) + ahead(7)
        s16 = s8
        for k in range(8, 16):
            s16 = s16 + ahead(k)
        lane = lax.broadcasted_iota(jnp.int32, (tm, POOL_W), 1)
        du_ref[...] = _pick_window(lane, s2, s4, s8, s16) - dp_ref[...]

    return pl.pallas_call(
        body, name=name, grid=(nblk,),
        in_specs=[pl.BlockSpec((tm, POOL_W), lambda i: (i, 0)), pl.BlockSpec((tm, POOL_W), lambda i: (i, 1)),
                  pl.BlockSpec((POOL_HALO, POOL_W), lambda i: (jnp.minimum((i + 1) * hb, nblk * hb - 1), 0))],
        out_specs=pl.BlockSpec((tm, POOL_W), lambda i: (i, 0)),
        out_shape=jax.ShapeDtypeStruct((s, POOL_W), F32),
        scratch_shapes=[pltpu.VMEM((tm + POOL_HALO, POOL_W), F32)],
        compiler_params=_cp(("parallel",)),
    )(dq, dq, dq)


def _loss_head(x, gain, target, *, name, tm=512):
    s = x.shape[0]
    tm = min(tm, s)

    def body(x_ref, g_ref, t_ref, dx_ref, dg_ref, loss_ref):
        xv = x_ref[...]
        r = lax.rsqrt(jnp.mean(xv * xv, axis=-1, keepdims=True) + EPS)
        xh = xv * r
        err = xh * g_ref[...] - t_ref[...]
        dy = err * (1.0 / D)
        a = dy * g_ref[...]
        dx_ref[...] = r * a - xh * (r * jnp.mean(a * xh, axis=-1, keepdims=True))

        @pl.when(pl.program_id(0) == 0)
        def _():
            dg_ref[...] = jnp.zeros_like(dg_ref)
            loss_ref[...] = jnp.zeros_like(loss_ref)

        dg_ref[...] += jnp.sum(dy * xh, axis=0, keepdims=True)
        part = 0.5 * jnp.sum(jnp.mean(err * err, axis=-1, keepdims=True), axis=0, keepdims=True)
        loss_ref[...] += jnp.broadcast_to(part, loss_ref.shape)

    row = lambda i: (i, 0)
    dx, dg, loss = pl.pallas_call(
        body, name=name, grid=(s // tm,),
        in_specs=[pl.BlockSpec((tm, D), row), pl.BlockSpec((1, D), lambda i: (0, 0)), pl.BlockSpec((tm, D), row)],
        out_specs=[pl.BlockSpec((tm, D), row), pl.BlockSpec((1, D), lambda i: (0, 0)), pl.BlockSpec((1, 128), lambda i: (0, 0))],
        out_shape=[jax.ShapeDtypeStruct((s, D), F32), jax.ShapeDtypeStruct((1, D), F32), jax.ShapeDtypeStruct((1, 128), F32)],
        compiler_params=_cp(("arbitrary",)),
    )(x, gain.reshape(1, D), target)
    return dx, dg.reshape(D), loss[0, 0]


def _adamw(w, g, m, v, *, name, tr=512):
    rows, cols = w.shape
    tr = min(tr, rows)
    assert rows % tr == 0, (name, rows, tr)
    c_m = 1.0 - ADAM_B1
    c_v = 1.0 - ADAM_B2
    bc1 = 1.0 - ADAM_B1 ** ADAM_STEP
    bc2 = 1.0 - ADAM_B2 ** ADAM_STEP

    def body(w_ref, g_ref, m_ref, v_ref, d_ref, mo_ref, vo_ref):
        gv = g_ref[...]
        mn = ADAM_B1 * m_ref[...] + c_m * gv
        vn = ADAM_B2 * v_ref[...] + c_v * (gv * gv)
        mo_ref[...] = mn
        vo_ref[...] = vn
        d_ref[...] = -ADAM_LR * ((mn / bc1) / (jnp.sqrt(vn / bc2) + ADAM_EPS) + ADAM_WD * w_ref[...])

    spec = pl.BlockSpec((tr, cols), lambda i: (i, 0))
    return pl.pallas_call(body, name=name, grid=(rows // tr,), in_specs=[spec] * 4, out_specs=[spec] * 3,
                          out_shape=[jax.ShapeDtypeStruct((rows, cols), F32)] * 3,
                          compiler_params=_cp(("parallel",)))(w, g, m, v)


def _adamw_layer(w, g, m, v, layer, prev, *, name, tr):
    rows, cols = g.shape
    assert rows % tr == 0 and w.shape == (DEPTH * rows, cols), (name, w.shape, g.shape, tr)
    nblk = rows // tr
    c_m = 1.0 - ADAM_B1
    c_v = 1.0 - ADAM_B2
    bc1 = 1.0 - ADAM_B1 ** ADAM_STEP
    bc2 = 1.0 - ADAM_B2 ** ADAM_STEP
    n_prev = 0 if prev is None else 4

    def body(*refs):
        w_ref, g_ref, m_ref, v_ref = refs[:4]
        d_ref, mo_ref, vo_ref, go_ref = refs[4 + n_prev:]
        gv = g_ref[...]
        mn = ADAM_B1 * m_ref[...] + c_m * gv
        vn = ADAM_B2 * v_ref[...] + c_v * (gv * gv)
        mo_ref[...] = mn
        vo_ref[...] = vn
        go_ref[...] = gv
        d_ref[...] = -ADAM_LR * ((mn / bc1) / (jnp.sqrt(vn / bc2) + ADAM_EPS) + ADAM_WD * w_ref[...])

    stacked = pl.BlockSpec((tr, cols), lambda i: (layer * nblk + i, 0))
    args = [w, g, m, v] + ([] if prev is None else list(prev))
    return pl.pallas_call(
        body, name=name, grid=(nblk,),
        in_specs=[stacked, pl.BlockSpec((tr, cols), lambda i: (i, 0)), stacked, stacked] + [ANY_SPEC] * n_prev,
        out_specs=[stacked] * 4, out_shape=[jax.ShapeDtypeStruct(w.shape, F32)] * 4,
        input_output_aliases={4 + k: k for k in range(n_prev)},
        compiler_params=_cp(("parallel",)))(*args)


def _position():
    return jnp.stack([lax.axis_index("c"), 2 * lax.axis_index("x") + lax.axis_index("y")]).astype(jnp.int32)


SUM_ROW_TILES = 2


def _sum2_bf16(pos, fulls, sibs, *, name):
    n = len(fulls)
    nb = SUM_ROW_TILES

    def body(pos_ref, *refs):
        for t in range(n):
            refs[2 * n + t][...] = (refs[t][...] + refs[n + t][...]).astype(BF16)

    in_specs, sib_specs = [], []
    for sb in sibs:
        _, half, cols = sb.shape
        tr = half // nb
        assert half % nb == 0 and tr % 16 == 0, sb.shape
        in_specs.append(pl.BlockSpec((None, tr, cols), lambda j, i, p: (j, p[0] * nb + i, 0)))
        sib_specs.append(pl.BlockSpec((None, tr, cols), lambda j, i, p: (j, i, 0)))
    return pl.pallas_call(
        body, name=name,
        grid_spec=pltpu.PrefetchScalarGridSpec(num_scalar_prefetch=1, grid=(N_CHIPS, nb), in_specs=in_specs + sib_specs,
                                               out_specs=sib_specs),
        out_shape=[jax.ShapeDtypeStruct(sb.shape, BF16) for sb in sibs],
        compiler_params=_cp(("parallel", "parallel")))(pos, *fulls, *sibs)


def _sum5(pos, fulls, sibs, recvs, *, name):
    n = len(fulls)
    nb = SUM_ROW_TILES

    def body(pos_ref, *refs):
        for t in range(n):
            acc = refs[t][...] + refs[n + t][...]
            for kk in range(3):
                acc = acc + refs[2 * n + t][kk].astype(F32)
            refs[3 * n + t][...] = acc

    f_specs, s_specs, r_specs, o_specs = [], [], [], []
    for f in fulls:
        _, rows, cols = f.shape
        tr = rows // 2 // nb
        f_specs.append(pl.BlockSpec((None, tr, cols), lambda i, p: (p[1], p[0] * nb + i, 0)))
        s_specs.append(pl.BlockSpec((None, tr, cols), lambda i, p: (p[1], i, 0)))
        r_specs.append(pl.BlockSpec((3, tr, cols), lambda i, p: (0, i, 0)))
        o_specs.append(pl.BlockSpec((tr, cols), lambda i, p: (p[0] * nb + i, 0)))
    return pl.pallas_call(
        body, name=name,
        grid_spec=pltpu.PrefetchScalarGridSpec(num_scalar_prefetch=1, grid=(nb,), in_specs=f_specs + s_specs + r_specs,
                                               out_specs=o_specs),
        out_shape=[jax.ShapeDtypeStruct(f.shape[1:], F32) for f in fulls],
        compiler_params=_cp(("parallel",)))(pos, *fulls, *sibs, *recvs)


def _place():
    x, y, c = lax.axis_index("x"), lax.axis_index("y"), lax.axis_index("c")
    chips = [(1 - x, y), (x, 1 - y), (1 - x, 1 - y)]
    return x, y, c, 2 * x + y, chips


SEM_SPEC = pl.BlockSpec(memory_space=pltpu.SEMAPHORE)
ANY_SPEC = pl.BlockSpec(memory_space=pl.ANY)


def _gather_copies(ins, outs, send_i, recv_i, send_o, recv_o):
    x, y, c, me, chips = _place()
    n = len(ins)
    started, awaited = [], []
    for t in range(n):
        half = ins[t].shape[0] // 2
        mine = pl.ds(c * half, half)
        started.append(pltpu.make_async_remote_copy(
            src_ref=ins[t], dst_ref=outs[t].at[me], send_sem=send_o.at[t], recv_sem=recv_o.at[t],
            device_id=(x, y, 1 - c), device_id_type=MESH))
        awaited.append(started[-1])
        for kk, (px, py) in enumerate(chips):
            started.append(pltpu.make_async_remote_copy(
                src_ref=ins[t].at[mine], dst_ref=outs[t].at[me, mine], send_sem=send_i.at[t * 3 + kk],
                recv_sem=recv_i.at[t * 3 + kk], device_id=(px, py, c), device_id_type=MESH))
            awaited.append(pltpu.make_async_remote_copy(
                src_ref=ins[t].at[mine], dst_ref=outs[t].at[2 * px + py, mine], send_sem=send_i.at[t * 3 + kk],
                recv_sem=recv_i.at[t * 3 + kk], device_id=(px, py, c), device_id_type=MESH))
    return started, awaited


def _forward_copies(outs, send_d, recv_d):
    x, y, c, me, chips = _place()
    started, awaited = [], []
    for t in range(len(outs)):
        half = outs[t].shape[1] // 2
        for kk, (px, py) in enumerate(chips):
            for lst, hc in ((started, c), (awaited, 1 - c)):
                blk = outs[t].at[2 * px + py, pl.ds(hc * half, half)]
                lst.append(pltpu.make_async_remote_copy(src_ref=blk, dst_ref=blk, send_sem=send_d.at[t * 3 + kk],
                                                        recv_sem=recv_d.at[t * 3 + kk], device_id=(x, y, 1 - c), device_id_type=MESH))
    return started, awaited


def _gather_blocking(shards):
    n = len(shards)

    def body(*refs):
        ins, outs = refs[:n], refs[n:2 * n]
        send_i, recv_i, send_d, recv_d, send_o, recv_o = refs[2 * n:]
        started, awaited = _gather_copies(ins, outs, send_i, recv_i, send_o, recv_o)
        for cp in started:
            cp.start()
        for cp in awaited:
            cp.wait_recv()
        fwd, fwd_in = _forward_copies(outs, send_d, recv_d)
        for cp in fwd:
            cp.start()
        for cp in fwd_in:
            cp.wait_recv()
        for cp in started + fwd:
            cp.wait_send()

    return pl.pallas_call(
        body, name="gather_first", in_specs=[HBM_SPEC] * n, out_specs=[HBM_SPEC] * n,
        out_shape=[jax.ShapeDtypeStruct((N_CHIPS,) + s.shape, s.dtype) for s in shards],
        scratch_shapes=[pltpu.SemaphoreType.DMA((3 * n,)), pltpu.SemaphoreType.DMA((3 * n,)),
                        pltpu.SemaphoreType.DMA((3 * n,)), pltpu.SemaphoreType.DMA((3 * n,)),
                        pltpu.SemaphoreType.DMA((n,)), pltpu.SemaphoreType.DMA((n,))],
    )(*shards)


def _gather_start(shards, after):
    n = len(shards)

    def body(*refs):
        ins = refs[:n]
        send_i, recv_i, send_o, recv_o = refs[2 * n + 1:2 * n + 5]
        outs = refs[3 * n + 5:4 * n + 5]
        token = refs[4 * n + 5]
        started, _ = _gather_copies(ins, outs, send_i, recv_i, send_o, recv_o)
        for cp in started:
            cp.start()
        token[...] = jnp.zeros_like(token)

    lands = [lax.empty((N_CHIPS,) + s.shape, s.dtype) for s in shards]
    sems = [pltpu.SemaphoreType.DMA((3 * n,)), pltpu.SemaphoreType.DMA((3 * n,)), pltpu.SemaphoreType.DMA((n,)), pltpu.SemaphoreType.DMA((n,))]
    res = pl.pallas_call(
        body, name="gather_rest_start",
        in_specs=[HBM_SPEC] * (2 * n) + [ANY_SPEC],
        out_specs=[SEM_SPEC] * 4 + [HBM_SPEC] * (2 * n) + [pl.BlockSpec(memory_space=pltpu.VMEM)],
        out_shape=sems + [jax.ShapeDtypeStruct(s.shape, s.dtype) for s in shards]
        + [jax.ShapeDtypeStruct(a.shape, a.dtype) for a in lands] + [jax.ShapeDtypeStruct((8, 128), F32)],
        input_output_aliases={t: 4 + t for t in range(2 * n)},
        compiler_params=pltpu.CompilerParams(has_side_effects=pltpu.SideEffectType.DATAFLOW_SIDE_EFFECTING),
    )(*[pltpu.with_memory_space_constraint(s, pltpu.HBM) for s in shards],
      *[pltpu.with_memory_space_constraint(a, pltpu.HBM) for a in lands], after)
    return res[:4], res[4:4 + n], res[4 + n:4 + 2 * n], res[-1]


def _gather_wait(sems, shards_thru, lands_thru, after):
    n = len(shards_thru)

    def body(*refs):
        ins, outs_in = refs[:n], refs[n:2 * n]
        send_i, recv_i, send_o, recv_o = refs[2 * n:2 * n + 4]
        started, awaited = _gather_copies(ins, outs_in, send_i, recv_i, send_o, recv_o)
        for cp in started:
            cp.wait_send()
        for cp in awaited:
            cp.wait_recv()

    res = pl.pallas_call(
        body, name="gather_rest_wait",
        in_specs=[HBM_SPEC] * (2 * n) + [SEM_SPEC] * 4 + [ANY_SPEC],
        out_specs=[HBM_SPEC] * (2 * n),
        out_shape=[jax.ShapeDtypeStruct(a.shape, a.dtype) for a in list(shards_thru) + list(lands_thru)],
        input_output_aliases={t: t for t in range(2 * n)},
        compiler_params=pltpu.CompilerParams(has_side_effects=pltpu.SideEffectType.DATAFLOW_SIDE_EFFECTING),
    )(*shards_thru, *lands_thru, *sems, after)
    return res[n:]


def _gather_forward(lands):
    n = len(lands)

    def body(*refs):
        outs = refs[n:2 * n]
        send_d, recv_d = refs[2 * n:]
        fwd, fwd_in = _forward_copies(outs, send_d, recv_d)
        for cp in fwd:
            cp.start()
        for cp in fwd_in:
            cp.wait_recv()
        for cp in fwd:
            cp.wait_send()

    return pl.pallas_call(
        body, name="gather_rest_forward", in_specs=[HBM_SPEC] * n, out_specs=[HBM_SPEC] * n,
        out_shape=[jax.ShapeDtypeStruct(a.shape, a.dtype) for a in lands],
        input_output_aliases={t: t for t in range(n)},
        scratch_shapes=[pltpu.SemaphoreType.DMA((3 * n,)), pltpu.SemaphoreType.DMA((3 * n,))],
    )(*lands)


def _reduce_stage1(grads, tag):
    n = len(grads)

    def body(*refs):
        ins, sib = refs[:n], refs[n:2 * n]
        send, recv = refs[2 * n:]
        x, y, c, me, chips = _place()
        cps = []
        for t in range(n):
            rows = ins[t].shape[1] // 2
            cp = pltpu.make_async_remote_copy(
                src_ref=ins[t].at[:, pl.ds((1 - c) * rows, rows), :], dst_ref=sib[t], send_sem=send.at[t],
                recv_sem=recv.at[t], device_id=(x, y, 1 - c), device_id_type=MESH)
            cp.start()
            cps.append(cp)
        for cp in cps:
            cp.wait()

    return pl.pallas_call(
        body, name="reduce_stage1_" + tag, in_specs=[HBM_SPEC] * n, out_specs=[HBM_SPEC] * n,
        out_shape=[jax.ShapeDtypeStruct((N_CHIPS, g.shape[1] // 2, g.shape[2]), F32) for g in grads],
        scratch_shapes=[pltpu.SemaphoreType.DMA((n,)), pltpu.SemaphoreType.DMA((n,))],
    )(*grads)


def _stage2_copies(ps, rcv, send, recv):
    x, y, c, me, chips = _place()
    return [pltpu.make_async_remote_copy(
        src_ref=ps[t].at[2 * px + py], dst_ref=rcv[t].at[kk], send_sem=send.at[t * 3 + kk],
        recv_sem=recv.at[t * 3 + kk], device_id=(px, py, c), device_id_type=MESH)
        for t in range(len(ps)) for kk, (px, py) in enumerate(chips)]


def _reduce_stage2_start(psum_bf16, tag):
    n = len(psum_bf16)

    def body(*refs):
        ps = refs[:n]
        send, recv = refs[2 * n:2 * n + 2]
        rcv = refs[3 * n + 2:4 * n + 2]
        token = refs[4 * n + 2]
        for cp in _stage2_copies(ps, rcv, send, recv):
            cp.start()
        token[...] = jnp.zeros_like(token)

    lands = [lax.empty((3,) + p.shape[1:], p.dtype) for p in psum_bf16]
    res = pl.pallas_call(
        body, name="reduce_stage2_start_" + tag,
        in_specs=[HBM_SPEC] * (2 * n),
        out_specs=[SEM_SPEC] * 2 + [HBM_SPEC] * (2 * n) + [pl.BlockSpec(memory_space=pltpu.VMEM)],
        out_shape=[pltpu.SemaphoreType.DMA((3 * n,)), pltpu.SemaphoreType.DMA((3 * n,))]
        + [jax.ShapeDtypeStruct(p.shape, p.dtype) for p in psum_bf16]
        + [jax.ShapeDtypeStruct(a.shape, a.dtype) for a in lands] + [jax.ShapeDtypeStruct((8, 128), F32)],
        input_output_aliases={t: 2 + t for t in range(2 * n)},
        compiler_params=pltpu.CompilerParams(has_side_effects=pltpu.SideEffectType.DATAFLOW_SIDE_EFFECTING),
    )(*[pltpu.with_memory_space_constraint(p, pltpu.HBM) for p in psum_bf16],
      *[pltpu.with_memory_space_constraint(a, pltpu.HBM) for a in lands])
    return res[:2], res[2:2 + n], res[2 + n:2 + 2 * n], res[-1]


def _reduce_stage2_wait(sems, ps_thru, lands_thru, after, tag):
    n = len(ps_thru)

    def body(*refs):
        for cp in _stage2_copies(refs[:n], refs[n:2 * n], refs[2 * n], refs[2 * n + 1]):
            cp.wait()

    res = pl.pallas_call(
        body, name="reduce_stage2_wait_" + tag,
        in_specs=[HBM_SPEC] * (2 * n) + [SEM_SPEC] * 2 + [ANY_SPEC],
        out_specs=[HBM_SPEC] * (2 * n),
        out_shape=[jax.ShapeDtypeStruct(a.shape, a.dtype) for a in list(ps_thru) + list(lands_thru)],
        input_output_aliases={t: t for t in range(2 * n)},
        compiler_params=pltpu.CompilerParams(has_side_effects=pltpu.SideEffectType.DATAFLOW_SIDE_EFFECTING),
    )(*ps_thru, *lands_thru, *sems, after)
    return res[n:]


def _reduce_stage3(reduced, tag):
    n = len(reduced)

    def body(*refs):
        outs = refs[n:2 * n]
        send, recv = refs[2 * n:]
        x, y, c, me, chips = _place()
        cps = []
        for t in range(n):
            rows = outs[t].shape[0] // 2
            mine = outs[t].at[pl.ds(c * rows, rows), :]
            cp = pltpu.make_async_remote_copy(src_ref=mine, dst_ref=mine, send_sem=send.at[t], recv_sem=recv.at[t],
                                              device_id=(x, y, 1 - c), device_id_type=MESH)
            cp.start()
            cps.append(cp)
        for cp in cps:
            cp.wait()

    return pl.pallas_call(
        body, name="reduce_stage3_" + tag, in_specs=[HBM_SPEC] * n, out_specs=[HBM_SPEC] * n,
        out_shape=[jax.ShapeDtypeStruct(r.shape, r.dtype) for r in reduced],
        input_output_aliases={t: t for t in range(n)},
        scratch_shapes=[pltpu.SemaphoreType.DMA((n,)), pltpu.SemaphoreType.DMA((n,))],
    )(*reduced)


def _allreduce_small(v):
    rows, cols = v.shape

    def body(v_ref, o_ref, buf, send, recv, loc):
        x, y, c, me, chips = _place()
        mine = 4 * x + 2 * y + c
        lc = pltpu.make_async_copy(v_ref, buf.at[mine], loc)
        lc.start()
        peers = []
        for fx in range(2):
            for fy in range(2):
                for fc in range(2):
                    if fx or fy or fc:
                        peers.append((fx, fy, fc))
        cps = []
        for kk, (fx, fy, fc) in enumerate(peers):
            to = (x ^ fx, y ^ fy, c ^ fc)
            cp = pltpu.make_async_remote_copy(src_ref=v_ref, dst_ref=buf.at[mine], send_sem=send.at[kk], recv_sem=recv.at[kk],
                                              device_id=to, device_id_type=MESH)
            cp.start()
            cps.append((cp, to))
        for kk, (cp, to) in enumerate(cps):
            src = 4 * to[0] + 2 * to[1] + to[2]
            pltpu.make_async_remote_copy(src_ref=v_ref, dst_ref=buf.at[src], send_sem=send.at[kk], recv_sem=recv.at[kk],
                                         device_id=to, device_id_type=MESH).wait_recv()
        for cp, _ in cps:
            cp.wait_send()
        lc.wait()
        acc = buf[0]
        for d in range(1, 8):
            acc = acc + buf[d]
        o_ref[...] = acc

    return pl.pallas_call(
        body, name="allreduce_small", in_specs=[pl.BlockSpec(memory_space=pltpu.VMEM)],
        out_specs=pl.BlockSpec(memory_space=pltpu.VMEM), out_shape=jax.ShapeDtypeStruct((rows, cols), F32),
        scratch_shapes=[pltpu.VMEM((8, rows, cols), F32), pltpu.SemaphoreType.DMA((7,)), pltpu.SemaphoreType.DMA((7,)),
                        pltpu.SemaphoreType.DMA],
        compiler_params=pltpu.CompilerParams(vmem_limit_bytes=VMEM_LIMIT_V7X),
    )(v)


def _pad_w_in(w):
    z = lambda n: jnp.zeros(w.shape[:-1] + (n,), w.dtype)
    return jnp.concatenate([w[..., 0:384], z(64), w[..., 384:416], z(32), w[..., 416:1824],
                            w[..., 1824:1830], z(58), w[..., 400:416], w[..., 384:400], z(32)], axis=-1)


def _unpad_w_in(g):
    x1 = g[..., 448:464] + g[..., Z_F + 80:Z_F + 96]
    x2 = g[..., 464:480] + g[..., Z_F + 64:Z_F + 80]
    return jnp.concatenate([g[..., 0:384], x1, x2, g[..., 512:1920], g[..., 1920:1926]], axis=-1)


def _block_diag(pw):
    out = jnp.zeros((POOL_W, POOL_W), pw.dtype)
    for g in range(4):
        out = out.at[g * 64:(g + 1) * 64, g * 64:(g + 1) * 64].set(pw[g])
    return out


def _rope_tables(s):
    inv_freq = ROPE_THETA ** (-jnp.arange(0, ROPE, 2, dtype=F32) / ROPE)
    ang = jnp.arange(s, dtype=jnp.int32).astype(F32)[:, None] * inv_freq[None, :]
    cos, sin = jnp.cos(ang), jnp.sin(ang)
    zero = lambda n: jnp.zeros((s, n), F32)
    ck = jnp.concatenate([zero(NOPE), cos, cos, zero(DA - NOPE - ROPE)], axis=1)
    sk = jnp.concatenate([zero(NOPE), -sin, sin, zero(DA - NOPE - ROPE)], axis=1)
    cq = jnp.concatenate([jnp.ones((s, NOPE), F32), cos, cos, zero(DA - NOPE - ROPE)], axis=1) * SCALE_MLA
    return dict(cq=cq, sq=sk * SCALE_MLA, ck=ck, sk=sk)


def _mix_fwd(l, x1, wts, sm, tabs):
    z, h2 = _norm_mm(x1, 0, sm["mix_norm"][l], wts["w_in"][l], name=f"mix_in_{l}")
    qa, qn = _mla_q_prep(z, sm["q_a_norm"][l], wts["wq_a"][l], wts["wq_b"][l], tabs["cq"], tabs["sq"], name=f"mla_q_{l}")
    ka, va, kvn = _mla_kv_prep(z, sm["kv_a_norm"][l], wts["wk"][l], wts["wv"][l], tabs["ck"], tabs["sk"], name=f"mla_kv_{l}")
    oa, lse_a = _attn_fwd(qa, ka, va, VDIM, name=f"mla_attn_{l}")

    bd = _block_diag(wts["pool_w"][l]).astype(BF16)
    yb, pooled = _pool_fwd(z, Z_POOL // POOL_W, bd, sm["pool_scale"][l], name=f"pool_{l}")

    fb = jnp.pad(sm["fox_b_f"][l], (0, 8 - H)).reshape(8, 1)
    ft, c3t = _gate_fwd(z, Z_F // DA, fb, name=f"fox_gate_{l}")
    fqa, fka, fva = _fox_prep(z, c3t, name=f"fox_prep_{l}")
    oc, lse_c = _attn_fwd(fqa, fka, fva, FOX_D, name=f"fox_attn_{l}")

    x2, cat = _mix_out(oa, yb, oc, wts["w_out"][l], x1, name=f"mix_out_{l}")
    saved = dict(z=z, h2=h2, qn=qn, kvn=kvn, qa=qa, ka=ka, va=va, oa=oa, lse_a=lse_a, bd=bd, pooled=pooled,
                 fqa=fqa, fka=fka, fva=fva, ft=ft, fb=fb, oc=oc, lse_c=lse_c, cat=cat)
    return x2, saved


def _mix_bwd(l, x1, dx2, sv, wts, sm, tabs):
    s = x1.shape[0]
    g = {}
    dx2b = dx2.astype(BF16)
    g["w_out"] = _mm(sv["cat"], dx2b, "tn", name=f"d_w_out_{l}", tm=1024, tn=1024, tk=DW_TOKENS)
    doa, doc, dyb, dl_a, dl_c = _mix_out_bwd(dx2b, wts["w_out"][l], sv["oa"], sv["oc"], name=f"mix_out_bwd_{l}")

    dfqa, dfka, dfva, dcq, dck = _attn_bwd(sv["fqa"], sv["fka"], sv["fva"], doc, sv["lse_c"], dl_c, True, name=f"fox_attn_bwd_{l}")
    dfox = _fox_bwd_prep(dfqa, dfka, dfva, name=f"fox_bwd_prep_{l}")
    dc = jnp.pad(dcq.reshape(H, s) + dck.reshape(H, s), ((0, 8 - H), (0, 0)))
    dft, dfb = _gate_bwd(sv["ft"], sv["fb"], dc, name=f"fox_gate_bwd_{l}")
    g["fox_b_f"] = dfb[:H, 0]

    dq, dys, g["pool_scale"] = _pool_bwd_a(dyb, sv["pooled"], sv["bd"], sm["pool_scale"][l], name=f"pool_bwd_a_{l}")
    du = _pool_bwd_b(dq, name=f"pool_bwd_b_{l}")
    dbd = _mm(sv["pooled"], dys, "tn", name=f"d_pool_w_{l}")
    g["pool_w"] = jnp.stack([dbd[i * 64:(i + 1) * 64, i * 64:(i + 1) * 64] for i in range(4)])

    dqa_, dka_, dva_ = _attn_bwd(sv["qa"], sv["ka"], sv["va"], doa, sv["lse_a"], dl_a, False, name=f"mla_attn_bwd_{l}")
    dqab, dkv, dz3, dz15 = _mla_bwd_prep(dqa_, dka_, dva_, dft, tabs["cq"], tabs["sq"], tabs["ck"], tabs["sk"],
                                         name=f"mla_bwd_prep_{l}")
    wq_ab = jnp.concatenate([wts["wq_a"][l], wts["wq_b"][l]], axis=1)
    wkv = jnp.concatenate([wts["wk"][l], wts["wv"][l]], axis=1)
    dwq = _mm(sv["qn"], dqab, "tn", name=f"d_w_q_b_{l}", tn=768, tk=DW_TOKENS).reshape(Q_RANK, 2, H, DA)
    dwkv = _mm(sv["kvn"], dkv, "tn", name=f"d_w_kv_b_{l}", tn=768, tk=DW_TOKENS).reshape(KV_RANK, 2, H, DA)
    da, db = dwq[:, 0], dwq[:, 1]
    swapped = jnp.concatenate([jnp.zeros((Q_RANK, H, NOPE), F32), db[..., NOPE + HALF_ROPE:NOPE + ROPE],
                               db[..., NOPE:NOPE + HALF_ROPE]], axis=-1)
    g["w_q_b"] = (da[..., :NOPE + ROPE] + swapped).reshape(Q_RANK, H * (NOPE + ROPE))
    g["w_kv_b"] = jnp.concatenate([dwkv[:, 0, :, :NOPE], dwkv[:, 1, :, :VDIM]], axis=-1).reshape(KV_RANK, H * (NOPE + VDIM))
    dqn = _mm(dqab, wq_ab, "nt", name=f"d_qn_{l}", tk=2 * H * DA)
    dkvn = _mm(dkv, wkv, "nt", name=f"d_kvn_{l}", tk=2 * H * DA)
    dqa, g["q_a_norm"] = _rmsnorm_bwd(sv["z"], Z_QA // Q_RANK, sm["q_a_norm"][l], dqn, name=f"q_a_norm_bwd_{l}")
    dkva, g["kv_a_norm"] = _rmsnorm_bwd(sv["z"], Z_KVA // KV_RANK, sm["kv_a_norm"][l], dkvn, name=f"kv_a_norm_bwd_{l}")

    dz = jnp.concatenate([dqa.astype(BF16), dkva.astype(BF16), dz3, du.astype(BF16), dfox, dz15], axis=1)
    g["w_in"] = _mm(sv["h2"], dz, "tn", name=f"d_w_in_{l}", tm=1024, tn=1024, tk=DW_TOKENS)
    dh2 = _mm(dz, wts["w_in"][l], "nt", name=f"d_h2_{l}", tn=1024, tk=NZ)
    dx1, g["mix_norm"] = _rmsnorm_bwd(x1, 0, sm["mix_norm"][l], dh2, dx2, name=f"mix_norm_bwd_{l}")
    return dx1, g


DW_TOKENS = 2048


def _local_step(x, target, wts, sm, late_weights=None, layer_done=None):
    s = x.shape[0]
    tabs = _rope_tables(s)
    acts = []
    xs = x
    for l in range(DEPTH):
        x1, gu1 = _ffn_fwd(xs, sm["ffn1_norm"][l], wts["ffn1_w_gu"][l], wts["ffn1_w_d2"][l], name=f"ffn1_fwd_{l}")
        x2, sv = _mix_fwd(l, x1, wts, sm, tabs)
        if l == 0 and late_weights is not None:
            late_weights(x2)
        x3, gu2 = _ffn_fwd(x2, sm["ffn2_norm"][l], wts["ffn2_w_gu"][l], wts["ffn2_w_d2"][l], name=f"ffn2_fwd_{l}")
        acts.append((xs, gu1, x1, sv, x2, gu2))
        xs = x3
    dx, g_final, loss = _loss_head(xs, sm["final_norm"], target, name="loss_head")
    grads = [dict() for _ in range(DEPTH)]
    for l in reversed(range(DEPTH)):
        x0, gu1, x1, sv, x2, gu2 = acts[l]
        g = grads[l]
        dx, dgu, act, hh, dy, g["ffn2_norm"] = _ffn_bwd(x2, dx, gu2, sm["ffn2_norm"][l], wts["ffn2_w_gu"][l], wts["ffn2_w_d2"][l],
                                                        name=f"ffn2_bwd_{l}")
        g["ffn2_w_down"] = _mm(act, dy, "tn", name=f"d_ffn2_w_down_{l}", tm=FF_SHARD, tn=1024, tk=DW_TOKENS)
        g["ffn2_w_gu"] = _mm(hh, dgu, "tn", name=f"d_ffn2_w_gu_{l}", tm=1024, tn=FF_SHARD, tk=DW_TOKENS, n_major_out=True)
        dx, gm = _mix_bwd(l, x1, dx, sv, wts, sm, tabs)
        g.update(gm)
        dx, dgu, act, hh, dy, g["ffn1_norm"] = _ffn_bwd(x0, dx, gu1, sm["ffn1_norm"][l], wts["ffn1_w_gu"][l], wts["ffn1_w_d2"][l],
                                                        name=f"ffn1_bwd_{l}")
        g["ffn1_w_down"] = _mm(act, dy, "tn", name=f"d_ffn1_w_down_{l}", tm=FF_SHARD, tn=1024, tk=DW_TOKENS)
        g["ffn1_w_gu"] = _mm(hh, dgu, "tn", name=f"d_ffn1_w_gu_{l}", tm=1024, tn=FF_SHARD, tk=DW_TOKENS, n_major_out=True)
        if layer_done is not None:
            sm = layer_done(l, g, sm)
    return loss, dx, grads, g_final


BIG = ["ffn1_w_gu", "ffn1_w_down", "w_in", "w_q_b", "w_kv_b", "w_out", "ffn2_w_gu", "ffn2_w_down"]
SMALL = ["ffn1_norm", "mix_norm", "q_a_norm", "kv_a_norm", "pool_w", "pool_scale", "fox_b_f", "ffn2_norm"]
SMALL_ROWS = 48


WEIGHT_VIEWS = ["ffn1_w_gu", "ffn1_w_d2", "w_in", "wq_a", "wq_b", "wk", "wv", "w_out", "ffn2_w_gu", "ffn2_w_d2"]


def _prepare_weights(gathered, wts):
    for (nm, l), w in gathered.items():
        if nm in ("ffn1_w_gu", "ffn2_w_gu"):
            wts[nm][l] = w
        elif nm in ("ffn1_w_down", "ffn2_w_down"):
            wts[nm[:5] + "w_d2"][l] = w.reshape(2, FF_SHARD, D)
        elif nm in ("w_in", "w_out"):
            wts[nm][l] = w.reshape(D, -1)
        elif nm == "w_q_b":
            wq = jnp.moveaxis(w, 0, 1).reshape(Q_RANK, H, NOPE + ROPE)
            zq = lambda n: jnp.zeros((Q_RANK, H, n), BF16)
            wts["wq_a"][l] = jnp.concatenate([wq, zq(DA - NOPE - ROPE)], axis=-1).reshape(Q_RANK, H * DA)
            wts["wq_b"][l] = jnp.concatenate([zq(NOPE), wq[..., NOPE + HALF_ROPE:], wq[..., NOPE:NOPE + HALF_ROPE],
                                              zq(DA - NOPE - ROPE)], axis=-1).reshape(Q_RANK, H * DA)
        else:
            wkv = jnp.moveaxis(w, 0, 1).reshape(KV_RANK, H, NOPE + VDIM)
            zk = jnp.zeros((KV_RANK, H, DA - NOPE), BF16)
            wts["wk"][l] = jnp.concatenate([wkv[..., :NOPE], zk], axis=-1).reshape(KV_RANK, H * DA)
            wts["wv"][l] = jnp.concatenate([wkv[..., NOPE:], zk], axis=-1).reshape(KV_RANK, H * DA)


def _chip_major(name, g):
    if name in ("ffn1_w_gu", "ffn2_w_gu"):
        return g
    if name in ("ffn1_w_down", "ffn2_w_down", "w_in", "w_out"):
        return g.reshape(N_CHIPS, g.shape[0] // N_CHIPS, g.shape[1])
    return jnp.moveaxis(g.reshape(g.shape[0], N_CHIPS, g.shape[1] // N_CHIPS), 1, 0)


def _pack_small(grads, g_final, loss):
    parts = []
    for l in range(DEPTH):
        for nm in SMALL:
            parts.append(grads[l][nm].reshape(-1))
    parts.append(g_final.reshape(-1))
    parts.append(loss.reshape(1))
    flat = jnp.concatenate(parts)
    return jnp.pad(flat, (0, SMALL_ROWS * D - flat.shape[0])).reshape(SMALL_ROWS, D)


def _unpack_small(packed, params):
    flat = packed.reshape(-1)
    out = {nm: [] for nm in SMALL}
    off = 0
    for l in range(DEPTH):
        for nm in SMALL:
            shp = params[nm].shape[1:]
            n = int(np.prod(shp))
            out[nm].append(flat[off:off + n].reshape(shp))
            off += n
    res = {nm: jnp.stack(v) for nm, v in out.items()}
    res["final_norm"] = flat[off:off + D]
    return res, flat[off + D]


def _update(name, w, g, m, v):
    shp = w.shape
    if w.ndim == 1:
        view = (1, shp[0])
    elif w.size <= 65536:
        view = (shp[0], w.size // shp[0])
    else:
        view = (w.size // shp[-1], shp[-1])
    tr = view[0]
    for cand in (512, 352, 256, 128):
        if view[0] % cand == 0 and view[0] > cand:
            tr = cand
            break
    d, mn, vn = _adamw(w.reshape(view), g.reshape(view), m.reshape(view), v.reshape(view), name="adamw_" + name, tr=tr)
    return d.reshape(shp), mn.reshape(shp), vn.reshape(shp)


WEIGHTS = ['ffn1_norm', 'ffn1_w_gu', 'ffn1_w_down', 'mix_norm', 'w_in', 'q_a_norm', 'w_q_b', 'kv_a_norm', 'w_kv_b', 'pool_w',
           'pool_scale', 'fox_b_f', 'w_out', 'ffn2_norm', 'ffn2_w_gu', 'ffn2_w_down', 'final_norm']


def kernel(x, ffn1_norm, ffn1_w_gu, ffn1_w_down, mix_norm, w_in, q_a_norm, w_q_b, kv_a_norm, w_kv_b, pool_w, pool_scale, fox_b_f, w_out, ffn2_norm, ffn2_w_gu, ffn2_w_down, final_norm, loss_target, m_ffn1_norm, m_ffn1_w_gu, m_ffn1_w_down, m_mix_norm, m_w_in, m_q_a_norm, m_w_q_b, m_kv_a_norm, m_w_kv_b, m_pool_w, m_pool_scale, m_fox_b_f, m_w_out, m_ffn2_norm, m_ffn2_w_gu, m_ffn2_w_down, m_final_norm, v_ffn1_norm, v_ffn1_w_gu, v_ffn1_w_down, v_mix_norm, v_w_in, v_q_a_norm, v_w_q_b, v_kv_a_norm, v_w_kv_b, v_pool_w, v_pool_scale, v_fox_b_f, v_w_out, v_ffn2_norm, v_ffn2_w_gu, v_ffn2_w_down, v_final_norm):
    params = dict(ffn1_norm=ffn1_norm, ffn1_w_gu=ffn1_w_gu, ffn1_w_down=ffn1_w_down, mix_norm=mix_norm, w_in=w_in, q_a_norm=q_a_norm,
                  w_q_b=w_q_b, kv_a_norm=kv_a_norm, w_kv_b=w_kv_b, pool_w=pool_w, pool_scale=pool_scale, fox_b_f=fox_b_f, w_out=w_out,
                  ffn2_norm=ffn2_norm, ffn2_w_gu=ffn2_w_gu, ffn2_w_down=ffn2_w_down, final_norm=final_norm)
    mom = dict(ffn1_norm=m_ffn1_norm, ffn1_w_gu=m_ffn1_w_gu, ffn1_w_down=m_ffn1_w_down, mix_norm=m_mix_norm, w_in=m_w_in,
               q_a_norm=m_q_a_norm, w_q_b=m_w_q_b, kv_a_norm=m_kv_a_norm, w_kv_b=m_w_kv_b, pool_w=m_pool_w, pool_scale=m_pool_scale,
               fox_b_f=m_fox_b_f, w_out=m_w_out, ffn2_norm=m_ffn2_norm, ffn2_w_gu=m_ffn2_w_gu, ffn2_w_down=m_ffn2_w_down,
               final_norm=m_final_norm)
    var = dict(ffn1_norm=v_ffn1_norm, ffn1_w_gu=v_ffn1_w_gu, ffn1_w_down=v_ffn1_w_down, mix_norm=v_mix_norm, w_in=v_w_in,
               q_a_norm=v_q_a_norm, w_q_b=v_w_q_b, kv_a_norm=v_kv_a_norm, w_kv_b=v_w_kv_b, pool_w=v_pool_w, pool_scale=v_pool_scale,
               fox_b_f=v_fox_b_f, w_out=v_w_out, ffn2_norm=v_ffn2_norm, ffn2_w_gu=v_ffn2_w_gu, ffn2_w_down=v_ffn2_w_down,
               final_norm=v_final_norm)

    shard = {}
    for nm in BIG:
        w = _pad_w_in(params[nm]) if nm == "w_in" else params[nm]
        for l in range(DEPTH):
            shard[(nm, l)] = w[l].astype(BF16)
    first = [(nm, 0) for nm in BIG if not nm.startswith("ffn2")]
    rest = [k for k in shard if k not in first]
    wts = {nm: [None] * DEPTH for nm in WEIGHT_VIEWS}
    wts["pool_w"] = params["pool_w"]
    got = _gather_blocking([shard[k] for k in first])
    _prepare_weights(dict(zip(first, got)), wts)
    sems, src_thru, land_thru, token = _gather_start([shard[k] for k in rest], got[0])
    sm = dict(params)
    sm["ffn1_norm"] = params["ffn1_norm"] + token[0, 0]

    def late_weights(x2):
        lands = _gather_forward(_gather_wait(sems, src_thru, land_thru, x2))
        _prepare_weights(dict(zip(rest, lands)), wts)

    pos = _position()
    flight = {}

    def reduce_to_chips(l, g):
        full = [_chip_major(nm, g[nm]) for nm in BIG]
        sib = _reduce_stage1(full, str(l))
        psum = _sum2_bf16(pos, full, sib, name=f"chip_sum_{l}")
        return full, sib, _reduce_stage2_start(psum, str(l))

    def layer_done(l, g, sm_now):
        flight[l] = reduce_to_chips(l, g)
        if l == 0:
            return sm_now
        sm_next = dict(sm_now)
        sm_next["ffn2_norm"] = sm_now["ffn2_norm"] + flight[l][2][3][0, 0]
        return sm_next

    loss, dx, grads, g_final = _local_step(x[0], loss_target[0], wts, sm, late_weights, layer_done)

    def view2d(a):
        return a.reshape(a.size // a.shape[-1], a.shape[-1])

    after = flight[0][2][3]
    done = {nm: None for nm in BIG}
    for l in reversed(range(DEPTH)):
        full, sib, (sems2, ps_thru, lands2, _) = flight[l]
        recv = _reduce_stage2_wait(sems2, ps_thru, lands2, after, str(l))
        whole = _reduce_stage3(_sum5(pos, full, sib, recv, name=f"grad_sum_{l}"), str(l))
        for nm, g_l in zip(BIG, whole):
            if nm == "w_in":
                g_l = _unpad_w_in(g_l)
            tr = max(t for t in (512, 352, 256, 128) if g_l.shape[0] % t == 0)
            done[nm] = _adamw_layer(view2d(params[nm]), g_l, view2d(mom[nm]), view2d(var[nm]), l, done[nm],
                                    name=f"adamw_{nm}_{l}", tr=tr)
        if l == DEPTH - 1:
            small_g, loss = _unpack_small(_allreduce_small(_pack_small(grads, g_final, loss)), params)
            after = done[BIG[-1]][0][-8:, 0:128] + small_g["final_norm"][0]
    gw, delta, new_m, new_v = dict(small_g), {}, {}, {}
    for nm in BIG:
        delta[nm], new_m[nm], new_v[nm], gw[nm] = [a.reshape(params[nm].shape) for a in done[nm]]
    for nm in small_g:
        delta[nm], new_m[nm], new_v[nm] = _update(nm, params[nm], gw[nm], mom[nm], var[nm])
    return (loss, dx[None], *[gw[n] for n in WEIGHTS], *[delta[n] for n in WEIGHTS], *[new_m[n] for n in WEIGHTS],
            *[new_v[n] for n in WEIGHTS])
```

```python
import functools
import math

import jax
import jax.numpy as jnp
import numpy as np
from jax import lax
from jax.experimental import pallas as pl
from jax.experimental.pallas import tpu as pltpu

F32 = jnp.float32
BF16 = jnp.bfloat16
MESH = pl.DeviceIdType.MESH
HBM_SPEC = pl.BlockSpec(memory_space=pltpu.HBM)

D = 1024
DEPTH = 2
D_FF = 2816
FF_SHARD = 1408
N_CHIPS = 4
H = 6
NOPE, ROPE, VDIM = 64, 32, 64
HALF_ROPE = ROPE // 2
Q_RANK, KV_RANK = 256, 128
POOL_W = 256
FOX_D = 64
N_IN = 1830
NZ = 2048
ROPE_THETA = 10000.0
EPS = 1e-6
POOL_HALO = 16
Z_QA, Z_KVA, Z_KR, Z_POOL, Z_FOX, Z_F = 0, 256, 384, 512, 768, 1920

ADAM_LR, ADAM_B1, ADAM_B2, ADAM_EPS, ADAM_WD, ADAM_STEP = 0.001, 0.9, 0.999, 1e-08, 0.01, 10

VMEM_LIMIT_V7X = 56 * 1024 * 1024


def _cp(sem=None, vmem=VMEM_LIMIT_V7X):
    return pltpu.CompilerParams(dimension_semantics=sem, vmem_limit_bytes=vmem)


def _sigmoid(x):
    return 1.0 / (1.0 + jnp.exp(-x))


def _dot(a, b, dims):
    return lax.dot_general(a, b, (dims, ((), ())), preferred_element_type=F32)


NN = ((1,), (0,))
NT = ((1,), (1,))
TN = ((0,), (0,))


def _mm(a, b, mode, *, name, out_dtype=F32, add=None, alpha=None, tm=512, tn=512, tk=512, n_major_out=False):
    if mode == "nn":
        (m, k), (k2, n) = a.shape, b.shape
    elif mode == "nt":
        (m, k), (n, k2) = a.shape, b.shape
    else:
        (k, m), (k2, n) = a.shape, b.shape
    assert k == k2
    tm, tn, tk = min(tm, m), min(tn, n), min(tk, k)
    assert m % tm == 0 and n % tn == 0 and k % tk == 0, (name, m, n, k, tm, tn, tk)
    nk = k // tk
    dims = {"nn": NN, "nt": NT, "tn": TN}[mode]
    a_spec = pl.BlockSpec((tk, tm), lambda i, j, kk: (kk, i)) if mode == "tn" else pl.BlockSpec((tm, tk), lambda i, j, kk: (i, kk))
    b_spec = pl.BlockSpec((tn, tk), lambda i, j, kk: (j, kk)) if mode == "nt" else pl.BlockSpec((tk, tn), lambda i, j, kk: (kk, j))
    in_specs = [a_spec, b_spec]
    args = [a, b]
    if add is not None:
        in_specs.append(pl.BlockSpec((tm, tn), lambda i, j, kk: (i, j)))
        args.append(add)
    if n_major_out:
        out_shape = jax.ShapeDtypeStruct((n // tn, m, tn), out_dtype)
        out_spec = pl.BlockSpec((None, tm, tn), lambda i, j, kk: (j, i, 0))
    else:
        out_shape = jax.ShapeDtypeStruct((m, n), out_dtype)
        out_spec = pl.BlockSpec((tm, tn), lambda i, j, kk: (i, j))

    def body(*refs):
        a_ref, b_ref = refs[0], refs[1]
        add_ref = refs[2] if add is not None else None
        o_ref, acc = refs[-2], refs[-1]
        kk = pl.program_id(2)

        @pl.when(kk == 0)
        def _():
            acc[...] = jnp.zeros_like(acc)

        acc[...] += _dot(a_ref[...].astype(BF16), b_ref[...].astype(BF16), dims)

        @pl.when(kk == nk - 1)
        def _():
            r = acc[...]
            if alpha is not None:
                r = r * alpha
            if add_ref is not None:
                r = r + add_ref[...].astype(F32)
            o_ref[...] = r.astype(out_dtype)

    return pl.pallas_call(
        body, name=name, grid=(m // tm, n // tn, nk), in_specs=in_specs, out_specs=out_spec, out_shape=out_shape,
        scratch_shapes=[pltpu.VMEM((tm, tn), F32)],
        compiler_params=_cp(("parallel", "parallel", "arbitrary")),
    )(*args)


def _norm_mm(x, col_block, gain, w, *, name, tm=512):
    s = x.shape[0]
    k, n = w.shape
    tm = min(tm, s)

    def body(x_ref, g_ref, w_ref, z_ref, h_ref):
        xv = x_ref[...]
        r = lax.rsqrt(jnp.mean(xv * xv, axis=-1, keepdims=True) + EPS)
        hv = (xv * r * g_ref[...]).astype(BF16)
        h_ref[...] = hv
        z_ref[...] = _dot(hv, w_ref[...], NN)

    return pl.pallas_call(
        body, name=name, grid=(s // tm,),
        in_specs=[pl.BlockSpec((tm, k), lambda i: (i, col_block)), pl.BlockSpec((1, k), lambda i: (0, 0)),
                  pl.BlockSpec((k, n), lambda i: (0, 0))],
        out_specs=[pl.BlockSpec((tm, n), lambda i: (i, 0)), pl.BlockSpec((tm, k), lambda i: (i, 0))],
        out_shape=[jax.ShapeDtypeStruct((s, n), F32), jax.ShapeDtypeStruct((s, k), BF16)],
        compiler_params=_cp(("parallel",)),
    )(x, gain.reshape(1, k), w)


def _rmsnorm_bwd(x, col_block, gain, dh, dres=None, *, name, tm=512):
    s = x.shape[0]
    k = gain.shape[-1]
    tm = min(tm, s)

    def body(*refs):
        x_ref, g_ref, dh_ref = refs[0], refs[1], refs[2]
        dres_ref = refs[3] if dres is not None else None
        dx_ref, dg_ref = refs[-2], refs[-1]
        xv = x_ref[...]
        r = lax.rsqrt(jnp.mean(xv * xv, axis=-1, keepdims=True) + EPS)
        dhv = dh_ref[...].astype(F32)
        a = dhv * g_ref[...]
        dx = r * a - xv * (r * r * r) * jnp.mean(a * xv, axis=-1, keepdims=True)
        if dres_ref is not None:
            dx = dx + dres_ref[...]
        dx_ref[...] = dx

        @pl.when(pl.program_id(0) == 0)
        def _():
            dg_ref[...] = jnp.zeros_like(dg_ref)

        dg_ref[...] += jnp.sum(dhv * xv * r, axis=0, keepdims=True)

    in_specs = [pl.BlockSpec((tm, k), lambda i: (i, col_block)), pl.BlockSpec((1, k), lambda i: (0, 0)),
                pl.BlockSpec((tm, k), lambda i: (i, 0))]
    args = [x, gain.reshape(1, k), dh]
    if dres is not None:
        in_specs.append(pl.BlockSpec((tm, k), lambda i: (i, 0)))
        args.append(dres)
    dx, dg = pl.pallas_call(
        body, name=name, grid=(s // tm,), in_specs=in_specs,
        out_specs=[pl.BlockSpec((tm, k), lambda i: (i, 0)), pl.BlockSpec((1, k), lambda i: (0, 0))],
        out_shape=[jax.ShapeDtypeStruct((s, k), F32), jax.ShapeDtypeStruct((1, k), F32)],
        compiler_params=_cp(("arbitrary",)),
    )(*args)
    return dx, dg.reshape(k)


def _ffn_fwd(x, gain, w_gu4, w_d2, *, name, tm=256):
    s = x.shape[0]
    tm = min(tm, s)

    def body(x_ref, g_ref, wgu_ref, wd_ref, xo_ref, gu_ref):
        xv = x_ref[...]
        r = lax.rsqrt(jnp.mean(xv * xv, axis=-1, keepdims=True) + EPS)
        hv = (xv * r * g_ref[...]).astype(BF16)
        y = jnp.zeros((tm, D), F32)
        for j in range(2):
            g = _dot(hv, wgu_ref[j], NN)
            u = _dot(hv, wgu_ref[2 + j], NN)
            gu_ref[:, j * FF_SHARD:(j + 1) * FF_SHARD] = g.astype(BF16)
            gu_ref[:, D_FF + j * FF_SHARD:D_FF + (j + 1) * FF_SHARD] = u.astype(BF16)
            act = (g * _sigmoid(g) * u).astype(BF16)
            y = y + _dot(act, wd_ref[j], NN)
        xo_ref[...] = xv + 0.5 * y

    return pl.pallas_call(
        body, name=name, grid=(s // tm,),
        in_specs=[pl.BlockSpec((tm, D), lambda i: (i, 0)), pl.BlockSpec((1, D), lambda i: (0, 0)),
                  pl.BlockSpec((N_CHIPS, D, FF_SHARD), lambda i: (0, 0, 0), pipeline_mode=pl.Buffered(1)),
                  pl.BlockSpec((2, FF_SHARD, D), lambda i: (0, 0, 0), pipeline_mode=pl.Buffered(1))],
        out_specs=[pl.BlockSpec((tm, D), lambda i: (i, 0)), pl.BlockSpec((tm, 2 * D_FF), lambda i: (i, 0))],
        out_shape=[jax.ShapeDtypeStruct((s, D), F32), jax.ShapeDtypeStruct((s, 2 * D_FF), BF16)],
        compiler_params=_cp(("parallel",)),
    )(x, gain.reshape(1, D), w_gu4, w_d2)


FFN_ROW_CHUNK = 32


def _ffn_bwd(x, dxo, gu, gain, w_gu4, w_d2, *, name, tm=256):
    s = x.shape[0]
    tm = min(tm, s)

    def body(x_ref, dxo_ref, gu_ref, g_ref, wgu_ref, wd_ref, dx_ref, dgu_ref, act_ref, h_ref, dy_ref, dg_ref):
        xv = x_ref[...]
        r = lax.rsqrt(jnp.mean(xv * xv, axis=-1, keepdims=True) + EPS)
        xh = xv * r
        h_ref[...] = (xh * g_ref[...]).astype(BF16)
        dxov = dxo_ref[...]
        dy = (0.5 * dxov).astype(BF16)
        dy_ref[...] = dy
        dh = jnp.zeros((tm, D), F32)
        for j in range(2):
            gcols = slice(j * FF_SHARD, (j + 1) * FF_SHARD)
            ucols = slice(D_FF + j * FF_SHARD, D_FF + (j + 1) * FF_SHARD)
            dact = _dot(dy, wd_ref[j], NT)
            for r0 in range(0, tm, FFN_ROW_CHUNK):
                rows = slice(r0, r0 + FFN_ROW_CHUNK)
                g = gu_ref[rows, gcols].astype(F32)
                u = gu_ref[rows, ucols].astype(F32)
                sg = _sigmoid(g)
                silu = g * sg
                da = dact[rows]
                act_ref[rows, gcols] = (silu * u).astype(BF16)
                dgu_ref[rows, gcols] = (da * u * (sg * (1.0 + g * (1.0 - sg)))).astype(BF16)
                dgu_ref[rows, ucols] = (da * silu).astype(BF16)
            dh = dh + _dot(dgu_ref[:, gcols], wgu_ref[j], NT) + _dot(dgu_ref[:, ucols], wgu_ref[2 + j], NT)
        a = dh * g_ref[...]
        dx_ref[...] = dxov + r * a - xh * (r * jnp.mean(a * xh, axis=-1, keepdims=True))

        @pl.when(pl.program_id(0) == 0)
        def _():
            dg_ref[...] = jnp.zeros_like(dg_ref)

        dg_ref[...] += jnp.sum(dh * xh, axis=0, keepdims=True)

    row = lambda i: (i, 0)
    outs = pl.pallas_call(
        body, name=name, grid=(s // tm,),
        in_specs=[pl.BlockSpec((tm, D), row), pl.BlockSpec((tm, D), row), pl.BlockSpec((tm, 2 * D_FF), row),
                  pl.BlockSpec((1, D), lambda i: (0, 0)),
                  pl.BlockSpec((N_CHIPS, D, FF_SHARD), lambda i: (0, 0, 0), pipeline_mode=pl.Buffered(1)),
                  pl.BlockSpec((2, FF_SHARD, D), lambda i: (0, 0, 0), pipeline_mode=pl.Buffered(1))],
        out_specs=[pl.BlockSpec((tm, D), row), pl.BlockSpec((tm, 2 * D_FF), row), pl.BlockSpec((tm, D_FF), row),
                   pl.BlockSpec((tm, D), row), pl.BlockSpec((tm, D), row), pl.BlockSpec((1, D), lambda i: (0, 0))],
        out_shape=[jax.ShapeDtypeStruct((s, D), F32), jax.ShapeDtypeStruct((s, 2 * D_FF), BF16),
                   jax.ShapeDtypeStruct((s, D_FF), BF16), jax.ShapeDtypeStruct((s, D), BF16),
                   jax.ShapeDtypeStruct((s, D), BF16), jax.ShapeDtypeStruct((1, D), F32)],
        compiler_params=_cp(("arbitrary",)),
    )(x, dxo, gu, gain.reshape(1, D), w_gu4, w_d2)
    dx, dgu, act, h, dy, dg = outs
    return dx, dgu, act, h, dy, dg.reshape(D)


DA = 128
SCALE_MLA = 1.0 / math.sqrt(NOPE + ROPE)
SCALE_FOX = 1.0 / math.sqrt(FOX_D)


def _causal_blocks(nb, key_major):
    if key_major:
        pairs = [(i, j) for j in range(nb) for i in range(j, nb)]
    else:
        pairs = [(i, j) for i in range(nb) for j in range(i + 1)]
    return (jnp.asarray(np.array([p[0] for p in pairs], np.int32)), jnp.asarray(np.array([p[1] for p in pairs], np.int32)))


HEADS_PER_STEP = 2
ROW_CHUNK = 64

def _col_to_row(col):
    return jnp.broadcast_to(col, (col.shape[0], DA)).T[0:1, :]


def _attn_fwd(qa, ka, va, dv, *, name, t=512):
    h, s, _ = qa.shape
    t = min(t, s)
    nb = s // t
    g = 3
    qi, kj = _causal_blocks(nb, key_major=False)

    rc = min(ROW_CHUNK, t)

    def body(qi_ref, kj_ref, q_ref, k_ref, v_ref, o_ref, lse_ref, m_sc, acc_sc, p_sc, a_sc):
        n = pl.program_id(1)
        i, j = qi_ref[n], kj_ref[n]

        @pl.when(j == 0)
        def _():
            m_sc[...] = jnp.full_like(m_sc, -jnp.inf)
            acc_sc[...] = jnp.zeros_like(acc_sc)

        def step(masked):
            scs = [_dot(q_ref[hh], k_ref[hh], NT) for hh in range(g)]
            for r0 in range(0, t, rc):
                rows = slice(r0, r0 + rc)
                for hh in range(g):
                    sr = scs[hh][rows]
                    if masked:
                        row = lax.broadcasted_iota(jnp.int32, (rc, t), 0) + r0
                        col = lax.broadcasted_iota(jnp.int32, (rc, t), 1)
                        sr = jnp.where(col <= row, sr, -jnp.inf)
                    m_old = m_sc[hh, rows]
                    m_new = jnp.maximum(m_old, jnp.max(sr, axis=-1, keepdims=True))
                    p_sc[hh, rows] = jnp.exp(sr - m_new).astype(BF16)
                    a_sc[hh, rows] = jnp.exp(m_old - m_new)
                    m_sc[hh, rows] = m_new
            for hh in range(g):
                acc_sc[hh] = a_sc[hh] * acc_sc[hh] + _dot(p_sc[hh], v_ref[hh], NN)

        @pl.when(j < i)
        def _():
            step(False)

        @pl.when(j == i)
        def _():
            step(True)
            for hh in range(g):
                acc = acc_sc[hh]
                l = acc[:, dv:dv + 1]
                o_ref[hh] = acc[:, :dv] / l
                lse_ref[hh] = _col_to_row(m_sc[hh] + jnp.log(l))

    qmap = lambda hg, n, qi_r, kj_r: (hg, qi_r[n], 0)
    kmap = lambda hg, n, qi_r, kj_r: (hg, kj_r[n], 0)
    return pl.pallas_call(
        body, name=name,
        grid_spec=pltpu.PrefetchScalarGridSpec(
            num_scalar_prefetch=2, grid=(h // g, qi.shape[0]),
            in_specs=[pl.BlockSpec((g, t, DA), qmap), pl.BlockSpec((g, t, DA), kmap), pl.BlockSpec((g, t, DA), kmap)],
            out_specs=[pl.BlockSpec((g, t, dv), qmap), pl.BlockSpec((g, 1, t), lambda hg, n, qi_r, kj_r: (hg, 0, qi_r[n]))],
            scratch_shapes=[pltpu.VMEM((g, t, 1), F32), pltpu.VMEM((g, t, DA), F32), pltpu.VMEM((g, t, t), BF16),
                            pltpu.VMEM((g, t, 1), F32)]),
        out_shape=[jax.ShapeDtypeStruct((h, s, dv), F32), jax.ShapeDtypeStruct((h, 1, s), F32)],
        compiler_params=_cp(("parallel", "arbitrary")),
    )(qi, kj, qa, ka, va)


def _attn_bwd(qa, ka, va, doa, lse_row, delta_row, decay, *, name, t=512):
    h, s, _ = qa.shape
    t = min(t, s)
    nb = s // t
    g = 3
    rc = min(ROW_CHUNK, t)
    qi, kj = _causal_blocks(nb, key_major=True)
    nsteps = qi.shape[0]

    def body(*refs):
        qi_ref, kj_ref, q_ref, k_ref, v_ref, do_ref, lse_ref, dl_ref = refs[:8]
        p_sc, ds_sc = refs[-2:]
        if decay:
            dq_ref, dk_ref, dv_ref, dcq_ref, dck_ref, dq_acc, dk_acc, dv_acc, dcq_acc, dck_acc = refs[8:-2]
        else:
            dq_ref, dk_ref, dv_ref, dq_acc, dk_acc, dv_acc = refs[8:-2]
        n = pl.program_id(1)
        i, j = qi_ref[n], kj_ref[n]

        @pl.when(n == 0)
        def _():
            dq_acc[...] = jnp.zeros_like(dq_acc)
            if decay:
                dcq_acc[...] = jnp.zeros_like(dcq_acc)

        @pl.when(i == j)
        def _():
            dk_acc[...] = jnp.zeros_like(dk_acc)
            dv_acc[...] = jnp.zeros_like(dv_acc)
            if decay:
                dck_acc[...] = jnp.zeros_like(dck_acc)

        def step(masked):
            sts = [_dot(k_ref[hh], q_ref[hh], NT) for hh in range(g)]
            dpts = [_dot(v_ref[hh], do_ref[hh], NT) for hh in range(g)]
            dcq = [jnp.zeros((1, t), F32) for _ in range(g)]
            for r0 in range(0, t, rc):
                rows = slice(r0, r0 + rc)
                for hh in range(g):
                    st = sts[hh][rows]
                    if masked:
                        row = lax.broadcasted_iota(jnp.int32, (rc, t), 0) + r0
                        col = lax.broadcasted_iota(jnp.int32, (rc, t), 1)
                        st = jnp.where(row <= col, st, -jnp.inf)
                    pt = jnp.exp(st - lse_ref[hh])
                    dst = pt * (dpts[hh][rows] - dl_ref[hh])
                    p_sc[hh, rows] = pt.astype(BF16)
                    ds_sc[hh, rows] = dst.astype(BF16)
                    if decay:
                        dcq[hh] = dcq[hh] + jnp.sum(dst, axis=0, keepdims=True)
                        dck_acc[hh, rows] -= jnp.sum(dst, axis=1, keepdims=True)
            for hh in range(g):
                dv_acc[hh] += _dot(p_sc[hh], do_ref[hh], NN)
                dk_acc[hh] += _dot(ds_sc[hh], q_ref[hh], NN)
                dq_acc[hh, i] += _dot(ds_sc[hh], k_ref[hh], TN)
                if decay:
                    dcq_acc[hh, i] += dcq[hh]

        @pl.when(i > j)
        def _():
            step(False)

        @pl.when(i == j)
        def _():
            step(True)

        @pl.when(i == nb - 1)
        def _():
            dk_ref[...] = dk_acc[...]
            dv_ref[...] = dv_acc[...]
            if decay:
                for hh in range(g):
                    dck_ref[hh] = _col_to_row(dck_acc[hh])

        @pl.when(n == nsteps - 1)
        def _():
            dq_ref[...] = dq_acc[...]
            if decay:
                dcq_ref[...] = dcq_acc[...]

    kmap = lambda hg, n, qi_r, kj_r: (hg, kj_r[n], 0)
    qmap = lambda hg, n, qi_r, kj_r: (hg, qi_r[n], 0)
    qrow = lambda hg, n, qi_r, kj_r: (hg, 0, qi_r[n])
    krow = lambda hg, n, qi_r, kj_r: (hg, 0, kj_r[n])
    whole = lambda hg, n, qi_r, kj_r: (hg, 0, 0, 0)
    in_specs = [pl.BlockSpec((g, t, DA), qmap), pl.BlockSpec((g, t, DA), kmap), pl.BlockSpec((g, t, DA), kmap),
                pl.BlockSpec((g, t, DA), qmap), pl.BlockSpec((g, 1, t), qrow), pl.BlockSpec((g, 1, t), qrow)]
    out_specs = [pl.BlockSpec((g, nb, t, DA), whole), pl.BlockSpec((g, t, DA), kmap), pl.BlockSpec((g, t, DA), kmap)]
    out_shape = [jax.ShapeDtypeStruct((h, nb, t, DA), F32), jax.ShapeDtypeStruct((h, s, DA), F32), jax.ShapeDtypeStruct((h, s, DA), F32)]
    scratch = [pltpu.VMEM((g, nb, t, DA), F32), pltpu.VMEM((g, t, DA), F32), pltpu.VMEM((g, t, DA), F32)]
    if decay:
        out_specs += [pl.BlockSpec((g, nb, 1, t), whole), pl.BlockSpec((g, 1, t), krow)]
        out_shape += [jax.ShapeDtypeStruct((h, nb, 1, t), F32), jax.ShapeDtypeStruct((h, 1, s), F32)]
        scratch += [pltpu.VMEM((g, nb, 1, t), F32), pltpu.VMEM((g, t, 1), F32)]
    scratch += [pltpu.VMEM((g, t, t), BF16), pltpu.VMEM((g, t, t), BF16)]
    outs = pl.pallas_call(
        body, name=name,
        grid_spec=pltpu.PrefetchScalarGridSpec(num_scalar_prefetch=2, grid=(h // g, nsteps), in_specs=in_specs, out_specs=out_specs,
                                               scratch_shapes=scratch),
        out_shape=out_shape, compiler_params=_cp(("parallel", "arbitrary")),
    )(qi, kj, qa, ka, va, doa, lse_row, delta_row)
    outs = list(outs)
    outs[0] = outs[0].reshape(h, s, DA)
    if decay:
        outs[3] = outs[3].reshape(h, 1, s)
    return outs


def _sel(rows, cols, pairs, value=1.0):
    m = np.zeros((rows, cols), np.float32)
    for r, c in pairs:
        m[r, c] = value
    return jnp.asarray(m, BF16)


def _lane_row(lanes):
    m = np.zeros((1, DA), np.float32)
    m[0, list(lanes)] = 1.0
    return jnp.asarray(m)


def _rms(xv, gain):
    r = lax.rsqrt(jnp.mean(xv * xv, axis=-1, keepdims=True) + EPS)
    return xv * r * gain


def _mla_q_prep(z, gain, wq_a, wq_b, cq, sq, *, name, tm=512):
    s = z.shape[0]
    tm = min(tm, s)

    def body(z_ref, g_ref, wa_ref, wb_ref, c_ref, s_ref, qa_ref, qn_ref):
        qn = _rms(z_ref[...], g_ref[...]).astype(BF16)
        qn_ref[...] = qn
        c, sn = c_ref[...], s_ref[...]
        for hh in range(H):
            cols = slice(hh * DA, (hh + 1) * DA)
            qa_ref[hh] = (_dot(qn, wa_ref[:, cols], NN) * c + _dot(qn, wb_ref[:, cols], NN) * sn).astype(BF16)

    row = lambda i: (i, 0)
    fixed = lambda i: (0, 0)
    return pl.pallas_call(
        body, name=name, grid=(s // tm,),
        in_specs=[pl.BlockSpec((tm, Q_RANK), lambda i: (i, Z_QA // Q_RANK)), pl.BlockSpec((1, Q_RANK), fixed),
                  pl.BlockSpec((Q_RANK, H * DA), fixed), pl.BlockSpec((Q_RANK, H * DA), fixed),
                  pl.BlockSpec((tm, DA), row), pl.BlockSpec((tm, DA), row)],
        out_specs=[pl.BlockSpec((H, tm, DA), lambda i: (0, i, 0)), pl.BlockSpec((tm, Q_RANK), row)],
        out_shape=[jax.ShapeDtypeStruct((H, s, DA), BF16), jax.ShapeDtypeStruct((s, Q_RANK), BF16)],
        compiler_params=_cp(("parallel",)),
    )(z, gain.reshape(1, Q_RANK), wq_a, wq_b, cq, sq)


def _mla_kv_prep(z, gain, wk, wv, ck, sk, *, name, tm=512):
    s = z.shape[0]
    tm = min(tm, s)
    one = _lane_row([VDIM])

    def body(zkv_ref, z3_ref, z15_ref, g_ref, wk_ref, wv_ref, c_ref, s_ref, one_ref, ka_ref, va_ref, kvn_ref):
        kvn = _rms(zkv_ref[...], g_ref[...]).astype(BF16)
        kvn_ref[...] = kvn
        kpe = z3_ref[...] * c_ref[...] + z15_ref[...] * s_ref[...]
        for hh in range(H):
            cols = slice(hh * DA, (hh + 1) * DA)
            ka_ref[hh] = (_dot(kvn, wk_ref[:, cols], NN) + kpe).astype(BF16)
            va_ref[hh] = (_dot(kvn, wv_ref[:, cols], NN) + one_ref[...]).astype(BF16)

    row = lambda i: (i, 0)
    fixed = lambda i: (0, 0)
    blk = lambda c: pl.BlockSpec((tm, DA), lambda i: (i, c))
    heads = pl.BlockSpec((H, tm, DA), lambda i: (0, i, 0))
    return pl.pallas_call(
        body, name=name, grid=(s // tm,),
        in_specs=[blk(Z_KVA // DA), blk(Z_KR // DA), blk(Z_F // DA), pl.BlockSpec((1, KV_RANK), fixed),
                  pl.BlockSpec((KV_RANK, H * DA), fixed), pl.BlockSpec((KV_RANK, H * DA), fixed),
                  pl.BlockSpec((tm, DA), row), pl.BlockSpec((tm, DA), row), pl.BlockSpec((1, DA), fixed)],
        out_specs=[heads, heads, pl.BlockSpec((tm, KV_RANK), row)],
        out_shape=[jax.ShapeDtypeStruct((H, s, DA), BF16), jax.ShapeDtypeStruct((H, s, DA), BF16),
                   jax.ShapeDtypeStruct((s, KV_RANK), BF16)],
        compiler_params=_cp(("parallel",)),
    )(z, z, z, gain.reshape(1, KV_RANK), wk, wv, ck, sk, one)


DEC_C = (FOX_D, FOX_D + 1, FOX_D + 2)
DEC_1 = (FOX_D + 3, FOX_D + 4, FOX_D + 5)


def _fox_prep(z, c3t, *, name, tm=512):
    s = z.shape[0]
    tm = min(tm, s)
    w = H * FOX_D
    left = [(r, r) for r in range(FOX_D)]
    right = [(FOX_D + r, r) for r in range(FOX_D)]
    pq = jnp.stack([_sel(DA, DA, left, SCALE_FOX), _sel(DA, DA, right, SCALE_FOX)])
    pk = jnp.stack([_sel(DA, DA, left), _sel(DA, DA, right)])
    pcq = jnp.stack([_sel(32, DA, [(hh + 8 * k, DEC_C[k]) for k in range(3)]) for hh in range(H)])
    pck = jnp.stack([_sel(32, DA, [(hh + 8 * k, DEC_1[k]) for k in range(3)], -1.0) for hh in range(H)])
    rows3 = jnp.concatenate([_lane_row(DEC_1), _lane_row(DEC_C), _lane_row([FOX_D])], axis=0)

    def body(zq_ref, zk_ref, zv_ref, c_ref, pq_ref, pk_ref, pcq_ref, pck_ref, r_ref, qa_ref, ka_ref, va_ref):
        c3 = c_ref[...]
        for pair in range(H // 2):
            lanes = slice(pair * DA, (pair + 1) * DA)
            zq, zk, zv = zq_ref[:, lanes].astype(BF16), zk_ref[:, lanes].astype(BF16), zv_ref[:, lanes].astype(BF16)
            for side in range(2):
                hh = 2 * pair + side
                qa_ref[hh] = (_dot(zq, pq_ref[side], NN) + _dot(c3, pcq_ref[hh], TN) + r_ref[0:1, :]).astype(BF16)
                ka_ref[hh] = (_dot(zk, pk_ref[side], NN) + _dot(c3, pck_ref[hh], TN) + r_ref[1:2, :]).astype(BF16)
                va_ref[hh] = (_dot(zv, pk_ref[side], NN) + r_ref[2:3, :]).astype(BF16)

    fixed2 = lambda i: (0, 0)
    fixed3 = lambda i: (0, 0, 0)
    heads = pl.BlockSpec((H, tm, DA), lambda i: (0, i, 0))
    zblk = lambda c: pl.BlockSpec((tm, w), lambda i: (i, c))
    return pl.pallas_call(
        body, name=name, grid=(s // tm,),
        in_specs=[zblk(Z_FOX // w), zblk(Z_FOX // w + 1), zblk(Z_FOX // w + 2), pl.BlockSpec((32, tm), lambda i: (0, i)),
                  pl.BlockSpec((2, DA, DA), fixed3), pl.BlockSpec((2, DA, DA), fixed3),
                  pl.BlockSpec((H, 32, DA), fixed3), pl.BlockSpec((H, 32, DA), fixed3), pl.BlockSpec((3, DA), fixed2)],
        out_specs=[heads, heads, heads], out_shape=[jax.ShapeDtypeStruct((H, s, DA), BF16)] * 3,
        compiler_params=_cp(("parallel",)),
    )(z, z, z, c3t, pq, pk, pcq, pck, rows3)


def _mix_out(oa, yb, oc, w_out, x1, *, name, tm=512):
    s = yb.shape[0]
    tm = min(tm, s)
    e2 = jnp.stack([_sel(VDIM, DA, [(r, r) for r in range(VDIM)]), _sel(VDIM, DA, [(r, VDIM + r) for r in range(VDIM)])])

    def body(oa_ref, yb_ref, oc_ref, e_ref, w_ref, x_ref, x2_ref, cat_ref):
        def pairs(o_ref):
            return [(_dot(o_ref[2 * p].astype(BF16), e_ref[0], NN) + _dot(o_ref[2 * p + 1].astype(BF16), e_ref[1], NN)).astype(BF16)
                    for p in range(H // 2)]

        cat = jnp.concatenate(pairs(oa_ref) + [yb_ref[...].astype(BF16)] + pairs(oc_ref), axis=1)
        cat_ref[...] = cat
        x2_ref[...] = x_ref[...] + _dot(cat, w_ref[...], NN)

    row = lambda i: (i, 0)
    heads = pl.BlockSpec((H, tm, VDIM), lambda i: (0, i, 0))
    return pl.pallas_call(
        body, name=name, grid=(s // tm,),
        in_specs=[heads, pl.BlockSpec((tm, POOL_W), row), heads, pl.BlockSpec((2, VDIM, DA), lambda i: (0, 0, 0)),
                  pl.BlockSpec((D, D), lambda i: (0, 0)), pl.BlockSpec((tm, D), row)],
        out_specs=[pl.BlockSpec((tm, D), row), pl.BlockSpec((tm, D), row)],
        out_shape=[jax.ShapeDtypeStruct((s, D), F32), jax.ShapeDtypeStruct((s, D), BF16)],
        compiler_params=_cp(("parallel",)),
    )(oa, yb, oc, e2, w_out, x1)


def _mix_out_bwd(dx2b, w_out, oa, oc, *, name, tm=512):
    s = dx2b.shape[0]
    tm = min(tm, s)
    f2 = jnp.stack([_sel(DA, DA, [(r, r) for r in range(VDIM)]), _sel(DA, DA, [(VDIM + r, r) for r in range(VDIM)])])
    nv = H * VDIM

    def body(dx_ref, w_ref, oa_ref, oc_ref, f_ref, doa_ref, doc_ref, dyb_ref, dla_ref, dlc_ref):
        dcat = _dot(dx_ref[...], w_ref[...], NT)
        dyb_ref[...] = dcat[:, nv:nv + POOL_W]
        for base, o_ref, do_ref, dl_ref in ((0, oa_ref, doa_ref, dla_ref), (nv + POOL_W, oc_ref, doc_ref, dlc_ref)):
            for p in range(H // 2):
                blk = dcat[:, base + p * DA:base + (p + 1) * DA].astype(BF16)
                for side in range(2):
                    hh = 2 * p + side
                    do = _dot(blk, f_ref[side], NN)
                    do_ref[hh] = do.astype(BF16)
                    dl_ref[hh] = _col_to_row(jnp.sum(do[:, :VDIM] * o_ref[hh], axis=-1, keepdims=True))

    row = lambda i: (i, 0)
    heads = lambda w: pl.BlockSpec((H, tm, w), lambda i: (0, i, 0))
    return pl.pallas_call(
        body, name=name, grid=(s // tm,),
        in_specs=[pl.BlockSpec((tm, D), row), pl.BlockSpec((D, D), lambda i: (0, 0)), heads(VDIM), heads(VDIM),
                  pl.BlockSpec((2, DA, DA), lambda i: (0, 0, 0))],
        out_specs=[heads(DA), heads(DA), pl.BlockSpec((tm, POOL_W), row),
                   pl.BlockSpec((H, 1, tm), lambda i: (0, 0, i)), pl.BlockSpec((H, 1, tm), lambda i: (0, 0, i))],
        out_shape=[jax.ShapeDtypeStruct((H, s, DA), BF16), jax.ShapeDtypeStruct((H, s, DA), BF16),
                   jax.ShapeDtypeStruct((s, POOL_W), F32), jax.ShapeDtypeStruct((H, 1, s), F32), jax.ShapeDtypeStruct((H, 1, s), F32)],
        compiler_params=_cp(("parallel",)),
    )(dx2b, w_out, oa, oc, f2)


def _mla_bwd_prep(dqa, dka, dva, dft, cq, sq, ck, sk, *, name, tm=512):
    s = dqa.shape[1]
    tm = min(tm, s)
    keep = _lane_row(range(NOPE))

    def body(dq_ref, dk_ref, dv_ref, dft_ref, cq_ref, sq_ref, ck_ref, sk_ref, keep_ref, dqab_ref, dkv_ref, dz3_ref, dz15_ref):
        cqv, sqv = cq_ref[...], sq_ref[...]
        dkpe = jnp.zeros((tm, DA), F32)
        for hh in range(H):
            lanes = slice(hh * DA, (hh + 1) * DA)
            dq = dq_ref[hh]
            dqab_ref[:, lanes] = (dq * cqv).astype(BF16)
            dqab_ref[:, H * DA + hh * DA:H * DA + (hh + 1) * DA] = (dq * sqv).astype(BF16)
            dk = dk_ref[hh]
            dkpe = dkpe + dk
            dkv_ref[:, lanes] = (dk * keep_ref[...]).astype(BF16)
            dkv_ref[:, H * DA + hh * DA:H * DA + (hh + 1) * DA] = (dv_ref[hh] * keep_ref[...]).astype(BF16)
        dz3_ref[...] = (dkpe * ck_ref[...]).astype(BF16)
        dz15_ref[...] = (dkpe * sk_ref[...] + dft_ref[...]).astype(BF16)

    row = lambda i: (i, 0)
    heads = pl.BlockSpec((H, tm, DA), lambda i: (0, i, 0))
    tab = pl.BlockSpec((tm, DA), row)
    return pl.pallas_call(
        body, name=name, grid=(s // tm,),
        in_specs=[heads, heads, heads, tab, tab, tab, tab, tab, pl.BlockSpec((1, DA), lambda i: (0, 0))],
        out_specs=[pl.BlockSpec((tm, 2 * H * DA), row), pl.BlockSpec((tm, 2 * H * DA), row), tab, tab],
        out_shape=[jax.ShapeDtypeStruct((s, 2 * H * DA), BF16), jax.ShapeDtypeStruct((s, 2 * H * DA), BF16),
                   jax.ShapeDtypeStruct((s, DA), BF16), jax.ShapeDtypeStruct((s, DA), BF16)],
        compiler_params=_cp(("parallel",)),
    )(dqa, dka, dva, dft, cq, sq, ck, sk, keep)


def _fox_bwd_prep(dfqa, dfka, dfva, *, name, tm=512):
    s = dfqa.shape[1]
    tm = min(tm, s)
    place = lambda v: jnp.stack([_sel(DA, DA, [(r, r) for r in range(FOX_D)], v), _sel(DA, DA, [(r, FOX_D + r) for r in range(FOX_D)], v)])
    gq, gk = place(SCALE_FOX), place(1.0)

    def body(dq_ref, dk_ref, dv_ref, gq_ref, gk_ref, dz_ref):
        for part, (d_ref, g_ref) in enumerate(((dq_ref, gq_ref), (dk_ref, gk_ref), (dv_ref, gk_ref))):
            for p in range(H // 2):
                blk = _dot(d_ref[2 * p].astype(BF16), g_ref[0], NN) + _dot(d_ref[2 * p + 1].astype(BF16), g_ref[1], NN)
                lo = part * H * FOX_D + p * DA
                dz_ref[:, lo:lo + DA] = blk.astype(BF16)

    heads = pl.BlockSpec((H, tm, DA), lambda i: (0, i, 0))
    sel = pl.BlockSpec((2, DA, DA), lambda i: (0, 0, 0))
    return pl.pallas_call(
        body, name=name, grid=(s // tm,), in_specs=[heads, heads, heads, sel, sel],
        out_specs=pl.BlockSpec((tm, 3 * H * FOX_D), lambda i: (i, 0)),
        out_shape=jax.ShapeDtypeStruct((s, 3 * H * FOX_D), BF16), compiler_params=_cp(("parallel",)),
    )(dfqa, dfka, dfva, gq, gk)


def _lane_scan(x, s, reverse):
    lane = lax.broadcasted_iota(jnp.int32, x.shape, 1)
    sh = 1
    while sh < s:
        if reverse:
            x = x + jnp.where(lane < s - sh, pltpu.roll(x, s - sh, axis=1), 0.0)
        else:
            x = x + jnp.where(lane >= sh, pltpu.roll(x, sh, axis=1), 0.0)
        sh *= 2
    return x


def _gate_fwd(z, col_block, bias, *, name):
    s = z.shape[0]

    def body(z_ref, b_ref, f_ref, c_ref):
        ft = z_ref[...].T[0:8, :]
        f_ref[...] = ft
        xg = ft + b_ref[...]
        lf = jnp.minimum(xg, 0.0) - jnp.log(1.0 + jnp.exp(-jnp.abs(xg)))
        c = _lane_scan(lf, s, False)
        hi = c.astype(BF16).astype(F32)
        r = c - hi
        mid = r.astype(BF16).astype(F32)
        lo = r - mid
        c_ref[...] = jnp.concatenate([hi, mid, lo, jnp.zeros_like(hi)], axis=0).astype(BF16)

    return pl.pallas_call(
        body, name=name, grid=(1,),
        in_specs=[pl.BlockSpec((s, 128), lambda i: (0, col_block)), pl.BlockSpec((8, 1), lambda i: (0, 0))],
        out_specs=[pl.BlockSpec((8, s), lambda i: (0, 0)), pl.BlockSpec((32, s), lambda i: (0, 0))],
        out_shape=[jax.ShapeDtypeStruct((8, s), F32), jax.ShapeDtypeStruct((32, s), BF16)],
        compiler_params=_cp(("arbitrary",)))(z, bias)


def _gate_bwd(ft, bias, dc, *, name):
    s = ft.shape[1]

    def body(f_ref, b_ref, dc_ref, df_ref, db_ref):
        xg = f_ref[...] + b_ref[...]
        dlf = _lane_scan(dc_ref[...], s, True)
        df = dlf * _sigmoid(-xg)
        db_ref[...] = jnp.sum(df, axis=-1, keepdims=True)
        df_ref[...] = jnp.concatenate([df, jnp.zeros((DA - 8, s), F32)], axis=0).T

    return pl.pallas_call(body, name=name, out_shape=[jax.ShapeDtypeStruct((s, DA), F32), jax.ShapeDtypeStruct((8, 1), F32)],
                          compiler_params=_cp())(ft, bias, dc)


def _pool_lane_consts(tm, i):
    lane = lax.broadcasted_iota(jnp.int32, (tm, POOL_W), 1)
    tok = lax.broadcasted_iota(jnp.int32, (tm, POOL_W), 0) + i * tm
    win = jnp.where(lane < 64, 2, jnp.where(lane < 128, 4, jnp.where(lane < 192, 8, 16)))
    cnt = jnp.minimum(tok + 1, win).astype(F32)
    return lane, tok, cnt


def _pick_window(lane, s2, s4, s8, s16):
    return jnp.where(lane < 64, s2, jnp.where(lane < 128, s4, jnp.where(lane < 192, s8, s16)))


def _pool_fwd(z, col_block, bd, scale, *, name, tm=512):
    s = z.shape[0]
    tm = min(tm, s)
    hb = tm // POOL_HALO

    def body(u_ref, halo_ref, bd_ref, sc_ref, y_ref, p_ref, buf):
        i = pl.program_id(0)
        buf[0:POOL_HALO, :] = halo_ref[...] * (i > 0).astype(F32)
        buf[POOL_HALO:, :] = u_ref[...]

        def back(k):
            return buf[POOL_HALO - k:POOL_HALO - k + tm, :]

        u = u_ref[...]
        s2 = u + back(1)
        s4 = s2 + back(2) + back(3)
        s8 = s4 + back(4) + back(5) + back(6) + back(7)
        s16 = s8
        for k in range(8, 16):
            s16 = s16 + back(k)
        lane, _, cnt = _pool_lane_consts(tm, i)
        pooled = (_pick_window(lane, s2, s4, s8, s16) / cnt - u).astype(BF16)
        p_ref[...] = pooled
        y_ref[...] = _dot(pooled, bd_ref[...], NN) * sc_ref[...]

    return pl.pallas_call(
        body, name=name, grid=(s // tm,),
        in_specs=[pl.BlockSpec((tm, POOL_W), lambda i: (i, col_block)),
                  pl.BlockSpec((POOL_HALO, POOL_W), lambda i: (jnp.maximum(i * hb - 1, 0), col_block)),
                  pl.BlockSpec((POOL_W, POOL_W), lambda i: (0, 0)), pl.BlockSpec((1, POOL_W), lambda i: (0, 0))],
        out_specs=[pl.BlockSpec((tm, POOL_W), lambda i: (i, 0)), pl.BlockSpec((tm, POOL_W), lambda i: (i, 0))],
        out_shape=[jax.ShapeDtypeStruct((s, POOL_W), F32), jax.ShapeDtypeStruct((s, POOL_W), BF16)],
        scratch_shapes=[pltpu.VMEM((tm + POOL_HALO, POOL_W), F32)],
        compiler_params=_cp(("parallel",)),
    )(z, z, bd, scale.reshape(1, POOL_W))


def _pool_bwd_a(dy, pooled, bd, scale, *, name, tm=512):
    s = dy.shape[0]
    tm = min(tm, s)

    def body(dy_ref, p_ref, bd_ref, sc_ref, dq_ref, dys_ref, dsc_ref):
        i = pl.program_id(0)
        dyv = dy_ref[...]
        y0 = _dot(p_ref[...], bd_ref[...], NN)
        dys = (dyv * sc_ref[...]).astype(BF16)
        dys_ref[...] = dys
        dp = _dot(dys, bd_ref[...], NT)
        _, _, cnt = _pool_lane_consts(tm, i)
        dq_ref[:, 0:POOL_W] = dp / cnt
        dq_ref[:, POOL_W:] = dp

        @pl.when(i == 0)
        def _():
            dsc_ref[...] = jnp.zeros_like(dsc_ref)

        dsc_ref[...] += jnp.sum(dyv * y0, axis=0, keepdims=True)

    row = lambda i: (i, 0)
    dq, dys, dsc = pl.pallas_call(
        body, name=name, grid=(s // tm,),
        in_specs=[pl.BlockSpec((tm, POOL_W), row), pl.BlockSpec((tm, POOL_W), row),
                  pl.BlockSpec((POOL_W, POOL_W), lambda i: (0, 0)), pl.BlockSpec((1, POOL_W), lambda i: (0, 0))],
        out_specs=[pl.BlockSpec((tm, 2 * POOL_W), row), pl.BlockSpec((tm, POOL_W), row), pl.BlockSpec((1, POOL_W), lambda i: (0, 0))],
        out_shape=[jax.ShapeDtypeStruct((s, 2 * POOL_W), F32), jax.ShapeDtypeStruct((s, POOL_W), BF16),
                   jax.ShapeDtypeStruct((1, POOL_W), F32)],
        compiler_params=_cp(("arbitrary",)),
    )(dy, pooled, bd, scale.reshape(1, POOL_W))
    return dq, dys, dsc.reshape(POOL_W)


def _pool_bwd_b(dq, *, name, tm=512):
    s = dq.shape[0]
    tm = min(tm, s)
    hb = tm // POOL_HALO
    nblk = s // tm

    def body(q_ref, dp_ref, halo_ref, du_ref, buf):
        i = pl.program_id(0)
        buf[0:tm, :] = q_ref[...]
        buf[tm:, :] = halo_ref[...] * (i < nblk - 1).astype(F32)

        def ahead(k):
            return buf[k:k + tm, :]

        q = q_ref[...]
        s2 = q + ahead(1)
        s4 = s2 + ahead(2) + ahead(3)
        s8 = s4 + ahead(4) + ahead(5) + ahead(6) + ahead(7)
        s16 = s8
        for k in range(8, 16):
            s16 = s16 + ahead(k)
        lane = lax.broadcasted_iota(jnp.int32, (tm, POOL_W), 1)
        du_ref[...] = _pick_window(lane, s2, s4, s8, s16) - dp_ref[...]

    return pl.pallas_call(
        body, name=name, grid=(nblk,),
        in_specs=[pl.BlockSpec((tm, POOL_W), lambda i: (i, 0)), pl.BlockSpec((tm, POOL_W), lambda i: (i, 1)),
                  pl.BlockSpec((POOL_HALO, POOL_W), lambda i: (jnp.minimum((i + 1) * hb, nblk * hb - 1), 0))],
        out_specs=pl.BlockSpec((tm, POOL_W), lambda i: (i, 0)),
        out_shape=jax.ShapeDtypeStruct((s, POOL_W), F32),
        scratch_shapes=[pltpu.VMEM((tm + POOL_HALO, POOL_W), F32)],
        compiler_params=_cp(("parallel",)),
    )(dq, dq, dq)


def _loss_head(x, gain, target, *, name, tm=512):
    s = x.shape[0]
    tm = min(tm, s)

    def body(x_ref, g_ref, t_ref, dx_ref, dg_ref, loss_ref):
        xv = x_ref[...]
        r = lax.rsqrt(jnp.mean(xv * xv, axis=-1, keepdims=True) + EPS)
        xh = xv * r
        err = xh * g_ref[...] - t_ref[...]
        dy = err * (1.0 / D)
        a = dy * g_ref[...]
        dx_ref[...] = r * a - xh * (r * jnp.mean(a * xh, axis=-1, keepdims=True))

        @pl.when(pl.program_id(0) == 0)
        def _():
            dg_ref[...] = jnp.zeros_like(dg_ref)
            loss_ref[...] = jnp.zeros_like(loss_ref)

        dg_ref[...] += jnp.sum(dy * xh, axis=0, keepdims=True)
        part = 0.5 * jnp.sum(jnp.mean(err * err, axis=-1, keepdims=True), axis=0, keepdims=True)
        loss_ref[...] += jnp.broadcast_to(part, loss_ref.shape)

    row = lambda i: (i, 0)
    dx, dg, loss = pl.pallas_call(
        body, name=name, grid=(s // tm,),
        in_specs=[pl.BlockSpec((tm, D), row), pl.BlockSpec((1, D), lambda i: (0, 0)), pl.BlockSpec((tm, D), row)],
        out_specs=[pl.BlockSpec((tm, D), row), pl.BlockSpec((1, D), lambda i: (0, 0)), pl.BlockSpec((1, 128), lambda i: (0, 0))],
        out_shape=[jax.ShapeDtypeStruct((s, D), F32), jax.ShapeDtypeStruct((1, D), F32), jax.ShapeDtypeStruct((1, 128), F32)],
        compiler_params=_cp(("arbitrary",)),
    )(x, gain.reshape(1, D), target)
    return dx, dg.reshape(D), loss[0, 0]


def _adamw(w, g, m, v, *, name, tr=512):
    rows, cols = w.shape
    tr = min(tr, rows)
    assert rows % tr == 0, (name, rows, tr)
    c_m = 1.0 - ADAM_B1
    c_v = 1.0 - ADAM_B2
    bc1 = 1.0 - ADAM_B1 ** ADAM_STEP
    bc2 = 1.0 - ADAM_B2 ** ADAM_STEP

    def body(w_ref, g_ref, m_ref, v_ref, d_ref, mo_ref, vo_ref):
        gv = g_ref[...]
        mn = ADAM_B1 * m_ref[...] + c_m * gv
        vn = ADAM_B2 * v_ref[...] + c_v * (gv * gv)
        mo_ref[...] = mn
        vo_ref[...] = vn
        d_ref[...] = -ADAM_LR * ((mn / bc1) / (jnp.sqrt(vn / bc2) + ADAM_EPS) + ADAM_WD * w_ref[...])

    spec = pl.BlockSpec((tr, cols), lambda i: (i, 0))
    return pl.pallas_call(body, name=name, grid=(rows // tr,), in_specs=[spec] * 4, out_specs=[spec] * 3,
                          out_shape=[jax.ShapeDtypeStruct((rows, cols), F32)] * 3,
                          compiler_params=_cp(("parallel",)))(w, g, m, v)


def _adamw_layer(w, g, m, v, layer, prev, *, name, tr):
    rows, cols = g.shape
    assert rows % tr == 0 and w.shape == (DEPTH * rows, cols), (name, w.shape, g.shape, tr)
    nblk = rows // tr
    c_m = 1.0 - ADAM_B1
    c_v = 1.0 - ADAM_B2
    bc1 = 1.0 - ADAM_B1 ** ADAM_STEP
    bc2 = 1.0 - ADAM_B2 ** ADAM_STEP
    n_prev = 0 if prev is None else 4

    def body(*refs):
        w_ref, g_ref, m_ref, v_ref = refs[:4]
        d_ref, mo_ref, vo_ref, go_ref = refs[4 + n_prev:]
        gv = g_ref[...]
        mn = ADAM_B1 * m_ref[...] + c_m * gv
        vn = ADAM_B2 * v_ref[...] + c_v * (gv * gv)
        mo_ref[...] = mn
        vo_ref[...] = vn
        go_ref[...] = gv
        d_ref[...] = -ADAM_LR * ((mn / bc1) / (jnp.sqrt(vn / bc2) + ADAM_EPS) + ADAM_WD * w_ref[...])

    stacked = pl.BlockSpec((tr, cols), lambda i: (layer * nblk + i, 0))
    args = [w, g, m, v] + ([] if prev is None else list(prev))
    return pl.pallas_call(
        body, name=name, grid=(nblk,),
        in_specs=[stacked, pl.BlockSpec((tr, cols), lambda i: (i, 0)), stacked, stacked] + [ANY_SPEC] * n_prev,
        out_specs=[stacked] * 4, out_shape=[jax.ShapeDtypeStruct(w.shape, F32)] * 4,
        input_output_aliases={4 + k: k for k in range(n_prev)},
        compiler_params=_cp(("parallel",)))(*args)


def _position():
    return jnp.stack([lax.axis_index("c"), 2 * lax.axis_index("x") + lax.axis_index("y")]).astype(jnp.int32)


SUM_ROW_TILES = 2


def _sum2_bf16(pos, fulls, sibs, *, name):
    n = len(fulls)
    nb = SUM_ROW_TILES

    def body(pos_ref, *refs):
        for t in range(n):
            refs[2 * n + t][...] = (refs[t][...] + refs[n + t][...]).astype(BF16)

    in_specs, sib_specs = [], []
    for sb in sibs:
        _, half, cols = sb.shape
        tr = half // nb
        assert half % nb == 0 and tr % 16 == 0, sb.shape
        in_specs.append(pl.BlockSpec((None, tr, cols), lambda j, i, p: (j, p[0] * nb + i, 0)))
        sib_specs.append(pl.BlockSpec((None, tr, cols), lambda j, i, p: (j, i, 0)))
    return pl.pallas_call(
        body, name=name,
        grid_spec=pltpu.PrefetchScalarGridSpec(num_scalar_prefetch=1, grid=(N_CHIPS, nb), in_specs=in_specs + sib_specs,
                                               out_specs=sib_specs),
        out_shape=[jax.ShapeDtypeStruct(sb.shape, BF16) for sb in sibs],
        compiler_params=_cp(("parallel", "parallel")))(pos, *fulls, *sibs)


def _sum5(pos, fulls, sibs, recvs, *, name):
    n = len(fulls)
    nb = SUM_ROW_TILES

    def body(pos_ref, *refs):
        for t in range(n):
            acc = refs[t][...] + refs[n + t][...]
            for kk in range(3):
                acc = acc + refs[2 * n + t][kk].astype(F32)
            refs[3 * n + t][...] = acc

    f_specs, s_specs, r_specs, o_specs = [], [], [], []
    for f in fulls:
        _, rows, cols = f.shape
        tr = rows // 2 // nb
        f_specs.append(pl.BlockSpec((None, tr, cols), lambda i, p: (p[1], p[0] * nb + i, 0)))
        s_specs.append(pl.BlockSpec((None, tr, cols), lambda i, p: (p[1], i, 0)))
        r_specs.append(pl.BlockSpec((3, tr, cols), lambda i, p: (0, i, 0)))
        o_specs.append(pl.BlockSpec((tr, cols), lambda i, p: (p[0] * nb + i, 0)))
    return pl.pallas_call(
        body, name=name,
        grid_spec=pltpu.PrefetchScalarGridSpec(num_scalar_prefetch=1, grid=(nb,), in_specs=f_specs + s_specs + r_specs,
                                               out_specs=o_specs),
        out_shape=[jax.ShapeDtypeStruct(f.shape[1:], F32) for f in fulls],
        compiler_params=_cp(("parallel",)))(pos, *fulls, *sibs, *recvs)


def _place():
    x, y, c = lax.axis_index("x"), lax.axis_index("y"), lax.axis_index("c")
    chips = [(1 - x, y), (x, 1 - y), (1 - x, 1 - y)]
    return x, y, c, 2 * x + y, chips


SEM_SPEC = pl.BlockSpec(memory_space=pltpu.SEMAPHORE)
ANY_SPEC = pl.BlockSpec(memory_space=pl.ANY)


def _gather_copies(ins, outs, send_i, recv_i, send_o, recv_o):
    x, y, c, me, chips = _place()
    n = len(ins)
    started, awaited = [], []
    for t in range(n):
        half = ins[t].shape[0] // 2
        mine = pl.ds(c * half, half)
        started.append(pltpu.make_async_remote_copy(
            src_ref=ins[t], dst_ref=outs[t].at[me], send_sem=send_o.at[t], recv_sem=recv_o.at[t],
            device_id=(x, y, 1 - c), device_id_type=MESH))
        awaited.append(started[-1])
        for kk, (px, py) in enumerate(chips):
            started.append(pltpu.make_async_remote_copy(
                src_ref=ins[t].at[mine], dst_ref=outs[t].at[me, mine], send_sem=send_i.at[t * 3 + kk],
                recv_sem=recv_i.at[t * 3 + kk], device_id=(px, py, c), device_id_type=MESH))
            awaited.append(pltpu.make_async_remote_copy(
                src_ref=ins[t].at[mine], dst_ref=outs[t].at[2 * px + py, mine], send_sem=send_i.at[t * 3 + kk],
                recv_sem=recv_i.at[t * 3 + kk], device_id=(px, py, c), device_id_type=MESH))
    return started, awaited


def _forward_copies(outs, send_d, recv_d):
    x, y, c, me, chips = _place()
    started, awaited = [], []
    for t in range(len(outs)):
        half = outs[t].shape[1] // 2
        for kk, (px, py) in enumerate(chips):
            for lst, hc in ((started, c), (awaited, 1 - c)):
                blk = outs[t].at[2 * px + py, pl.ds(hc * half, half)]
                lst.append(pltpu.make_async_remote_copy(src_ref=blk, dst_ref=blk, send_sem=send_d.at[t * 3 + kk],
                                                        recv_sem=recv_d.at[t * 3 + kk], device_id=(x, y, 1 - c), device_id_type=MESH))
    return started, awaited


def _gather_blocking(shards):
    n = len(shards)

    def body(*refs):
        ins, outs = refs[:n], refs[n:2 * n]
        send_i, recv_i, send_d, recv_d, send_o, recv_o = refs[2 * n:]
        started, awaited = _gather_copies(ins, outs, send_i, recv_i, send_o, recv_o)
        for cp in started:
            cp.start()
        for cp in awaited:
            cp.wait_recv()
        fwd, fwd_in = _forward_copies(outs, send_d, recv_d)
        for cp in fwd:
            cp.start()
        for cp in fwd_in:
            cp.wait_recv()
        for cp in started + fwd:
            cp.wait_send()

    return pl.pallas_call(
        body, name="gather_first", in_specs=[HBM_SPEC] * n, out_specs=[HBM_SPEC] * n,
        out_shape=[jax.ShapeDtypeStruct((N_CHIPS,) + s.shape, s.dtype) for s in shards],
        scratch_shapes=[pltpu.SemaphoreType.DMA((3 * n,)), pltpu.SemaphoreType.DMA((3 * n,)),
                        pltpu.SemaphoreType.DMA((3 * n,)), pltpu.SemaphoreType.DMA((3 * n,)),
                        pltpu.SemaphoreType.DMA((n,)), pltpu.SemaphoreType.DMA((n,))],
    )(*shards)


def _gather_start(shards, after):
    n = len(shards)

    def body(*refs):
        ins = refs[:n]
        send_i, recv_i, send_o, recv_o = refs[2 * n + 1:2 * n + 5]
        outs = refs[3 * n + 5:4 * n + 5]
        token = refs[4 * n + 5]
        started, _ = _gather_copies(ins, outs, send_i, recv_i, send_o, recv_o)
        for cp in started:
            cp.start()
        token[...] = jnp.zeros_like(token)

    lands = [lax.empty((N_CHIPS,) + s.shape, s.dtype) for s in shards]
    sems = [pltpu.SemaphoreType.DMA((3 * n,)), pltpu.SemaphoreType.DMA((3 * n,)), pltpu.SemaphoreType.DMA((n,)), pltpu.SemaphoreType.DMA((n,))]
    res = pl.pallas_call(
        body, name="gather_rest_start",
        in_specs=[HBM_SPEC] * (2 * n) + [ANY_SPEC],
        out_specs=[SEM_SPEC] * 4 + [HBM_SPEC] * (2 * n) + [pl.BlockSpec(memory_space=pltpu.VMEM)],
        out_shape=sems + [jax.ShapeDtypeStruct(s.shape, s.dtype) for s in shards]
        + [jax.ShapeDtypeStruct(a.shape, a.dtype) for a in lands] + [jax.ShapeDtypeStruct((8, 128), F32)],
        input_output_aliases={t: 4 + t for t in range(2 * n)},
        compiler_params=pltpu.CompilerParams(has_side_effects=pltpu.SideEffectType.DATAFLOW_SIDE_EFFECTING),
    )(*[pltpu.with_memory_space_constraint(s, pltpu.HBM) for s in shards],
      *[pltpu.with_memory_space_constraint(a, pltpu.HBM) for a in lands], after)
    return res[:4], res[4:4 + n], res[4 + n:4 + 2 * n], res[-1]


def _gather_wait(sems, shards_thru, lands_thru, after):
    n = len(shards_thru)

    def body(*refs):
        ins, outs_in = refs[:n], refs[n:2 * n]
        send_i, recv_i, send_o, recv_o = refs[2 * n:2 * n + 4]
        started, awaited = _gather_copies(ins, outs_in, send_i, recv_i, send_o, recv_o)
        for cp in started:
            cp.wait_send()
        for cp in awaited:
            cp.wait_recv()

    res = pl.pallas_call(
        body, name="gather_rest_wait",
        in_specs=[HBM_SPEC] * (2 * n) + [SEM_SPEC] * 4 + [ANY_SPEC],
        out_specs=[HBM_SPEC] * (2 * n),
        out_shape=[jax.ShapeDtypeStruct(a.shape, a.dtype) for a in list(shards_thru) + list(lands_thru)],
        input_output_aliases={t: t for t in range(2 * n)},
        compiler_params=pltpu.CompilerParams(has_side_effects=pltpu.SideEffectType.DATAFLOW_SIDE_EFFECTING),
    )(*shards_thru, *lands_thru, *sems, after)
    return res[n:]


def _gather_forward(lands):
    n = len(lands)

    def body(*refs):
        outs = refs[n:2 * n]
        send_d, recv_d = refs[2 * n:]
        fwd, fwd_in = _forward_copies(outs, send_d, recv_d)
        for cp in fwd:
            cp.start()
        for cp in fwd_in:
            cp.wait_recv()
        for cp in fwd:
            cp.wait_send()

    return pl.pallas_call(
        body, name="gather_rest_forward", in_specs=[HBM_SPEC] * n, out_specs=[HBM_SPEC] * n,
        out_shape=[jax.ShapeDtypeStruct(a.shape, a.dtype) for a in lands],
        input_output_aliases={t: t for t in range(n)},
        scratch_shapes=[pltpu.SemaphoreType.DMA((3 * n,)), pltpu.SemaphoreType.DMA((3 * n,))],
    )(*lands)


def _stage1_copies(ins, sib, send, recv):
    x, y, c, me, chips = _place()
    cps = []
    for t in range(len(ins)):
        rows = ins[t].shape[1] // 2
        cps.append(pltpu.make_async_remote_copy(
            src_ref=ins[t].at[:, pl.ds((1 - c) * rows, rows), :], dst_ref=sib[t], send_sem=send.at[t],
            recv_sem=recv.at[t], device_id=(x, y, 1 - c), device_id_type=MESH))
    return cps


def _reduce_stage1(grads, tag):
    n = len(grads)

    def body(*refs):
        cps = _stage1_copies(refs[:n], refs[n:2 * n], *refs[2 * n:])
        for cp in cps:
            cp.start()
        for cp in cps:
            cp.wait()

    return pl.pallas_call(
        body, name="reduce_stage1_" + tag, in_specs=[HBM_SPEC] * n, out_specs=[HBM_SPEC] * n,
        out_shape=[jax.ShapeDtypeStruct((N_CHIPS, g.shape[1] // 2, g.shape[2]), F32) for g in grads],
        scratch_shapes=[pltpu.SemaphoreType.DMA((n,)), pltpu.SemaphoreType.DMA((n,))],
    )(*grads)


def _split_start(copies_fn, srcs, land_shapes, n_sems, tag):
    n = len(srcs)

    def body(*refs):
        send, recv = refs[2 * n:2 * n + 2]
        for cp in copies_fn(refs[:n], refs[3 * n + 2:4 * n + 2], send, recv):
            cp.start()
        refs[4 * n + 2][...] = jnp.zeros_like(refs[4 * n + 2])

    lands = [lax.empty(shp, dt) for shp, dt in land_shapes]
    res = pl.pallas_call(
        body, name=tag,
        in_specs=[HBM_SPEC] * (2 * n),
        out_specs=[SEM_SPEC] * 2 + [HBM_SPEC] * (2 * n) + [pl.BlockSpec(memory_space=pltpu.VMEM)],
        out_shape=[pltpu.SemaphoreType.DMA((n_sems,)), pltpu.SemaphoreType.DMA((n_sems,))]
        + [jax.ShapeDtypeStruct(p.shape, p.dtype) for p in srcs]
        + [jax.ShapeDtypeStruct(a.shape, a.dtype) for a in lands] + [jax.ShapeDtypeStruct((8, 128), F32)],
        input_output_aliases={t: 2 + t for t in range(2 * n)},
        compiler_params=pltpu.CompilerParams(has_side_effects=pltpu.SideEffectType.DATAFLOW_SIDE_EFFECTING),
    )(*[pltpu.with_memory_space_constraint(p, pltpu.HBM) for p in srcs],
      *[pltpu.with_memory_space_constraint(a, pltpu.HBM) for a in lands])
    return res[:2], res[2:2 + n], res[2 + n:2 + 2 * n], res[-1]


def _split_wait(copies_fn, sems, srcs_thru, lands_thru, after, tag):
    n = len(srcs_thru)

    def body(*refs):
        for cp in copies_fn(refs[:n], refs[n:2 * n], refs[2 * n], refs[2 * n + 1]):
            cp.wait()

    res = pl.pallas_call(
        body, name=tag,
        in_specs=[HBM_SPEC] * (2 * n) + [SEM_SPEC] * 2 + [ANY_SPEC],
        out_specs=[HBM_SPEC] * (2 * n),
        out_shape=[jax.ShapeDtypeStruct(a.shape, a.dtype) for a in list(srcs_thru) + list(lands_thru)],
        input_output_aliases={t: t for t in range(2 * n)},
        compiler_params=pltpu.CompilerParams(has_side_effects=pltpu.SideEffectType.DATAFLOW_SIDE_EFFECTING),
    )(*srcs_thru, *lands_thru, *sems, after)
    return res[:n], res[n:]


def _stage2_copies(ps, rcv, send, recv):
    x, y, c, me, chips = _place()
    return [pltpu.make_async_remote_copy(
        src_ref=ps[t].at[2 * px + py], dst_ref=rcv[t].at[kk], send_sem=send.at[t * 3 + kk],
        recv_sem=recv.at[t * 3 + kk], device_id=(px, py, c), device_id_type=MESH)
        for t in range(len(ps)) for kk, (px, py) in enumerate(chips)]


def _reduce_stage3(reduced, tag):
    n = len(reduced)

    def body(*refs):
        outs = refs[n:2 * n]
        send, recv = refs[2 * n:]
        x, y, c, me, chips = _place()
        cps = []
        for t in range(n):
            rows = outs[t].shape[0] // 2
            mine = outs[t].at[pl.ds(c * rows, rows), :]
            cp = pltpu.make_async_remote_copy(src_ref=mine, dst_ref=mine, send_sem=send.at[t], recv_sem=recv.at[t],
                                              device_id=(x, y, 1 - c), device_id_type=MESH)
            cp.start()
            cps.append(cp)
        for cp in cps:
            cp.wait()

    return pl.pallas_call(
        body, name="reduce_stage3_" + tag, in_specs=[HBM_SPEC] * n, out_specs=[HBM_SPEC] * n,
        out_shape=[jax.ShapeDtypeStruct(r.shape, r.dtype) for r in reduced],
        input_output_aliases={t: t for t in range(n)},
        scratch_shapes=[pltpu.SemaphoreType.DMA((n,)), pltpu.SemaphoreType.DMA((n,))],
    )(*reduced)


def _allreduce_small(v):
    rows, cols = v.shape

    def body(v_ref, o_ref, buf, send, recv, loc):
        x, y, c, me, chips = _place()
        mine = 4 * x + 2 * y + c
        lc = pltpu.make_async_copy(v_ref, buf.at[mine], loc)
        lc.start()
        peers = []
        for fx in range(2):
            for fy in range(2):
                for fc in range(2):
                    if fx or fy or fc:
                        peers.append((fx, fy, fc))
        cps = []
        for kk, (fx, fy, fc) in enumerate(peers):
            to = (x ^ fx, y ^ fy, c ^ fc)
            cp = pltpu.make_async_remote_copy(src_ref=v_ref, dst_ref=buf.at[mine], send_sem=send.at[kk], recv_sem=recv.at[kk],
                                              device_id=to, device_id_type=MESH)
            cp.start()
            cps.append((cp, to))
        for kk, (cp, to) in enumerate(cps):
            src = 4 * to[0] + 2 * to[1] + to[2]
            pltpu.make_async_remote_copy(src_ref=v_ref, dst_ref=buf.at[src], send_sem=send.at[kk], recv_sem=recv.at[kk],
                                         device_id=to, device_id_type=MESH).wait_recv()
        for cp, _ in cps:
            cp.wait_send()
        lc.wait()
        acc = buf[0]
        for d in range(1, 8):
            acc = acc + buf[d]
        o_ref[...] = acc

    return pl.pallas_call(
        body, name="allreduce_small", in_specs=[pl.BlockSpec(memory_space=pltpu.VMEM)],
        out_specs=pl.BlockSpec(memory_space=pltpu.VMEM), out_shape=jax.ShapeDtypeStruct((rows, cols), F32),
        scratch_shapes=[pltpu.VMEM((8, rows, cols), F32), pltpu.SemaphoreType.DMA((7,)), pltpu.SemaphoreType.DMA((7,)),
                        pltpu.SemaphoreType.DMA],
        compiler_params=pltpu.CompilerParams(vmem_limit_bytes=VMEM_LIMIT_V7X),
    )(v)


def _pad_w_in(w):
    z = lambda n: jnp.zeros(w.shape[:-1] + (n,), w.dtype)
    return jnp.concatenate([w[..., 0:384], z(64), w[..., 384:416], z(32), w[..., 416:1824],
                            w[..., 1824:1830], z(58), w[..., 400:416], w[..., 384:400], z(32)], axis=-1)


def _unpad_w_in(g):
    x1 = g[..., 448:464] + g[..., Z_F + 80:Z_F + 96]
    x2 = g[..., 464:480] + g[..., Z_F + 64:Z_F + 80]
    return jnp.concatenate([g[..., 0:384], x1, x2, g[..., 512:1920], g[..., 1920:1926]], axis=-1)


def _block_diag(pw):
    out = jnp.zeros((POOL_W, POOL_W), pw.dtype)
    for g in range(4):
        out = out.at[g * 64:(g + 1) * 64, g * 64:(g + 1) * 64].set(pw[g])
    return out


def _rope_tables(s):
    inv_freq = ROPE_THETA ** (-jnp.arange(0, ROPE, 2, dtype=F32) / ROPE)
    ang = jnp.arange(s, dtype=jnp.int32).astype(F32)[:, None] * inv_freq[None, :]
    cos, sin = jnp.cos(ang), jnp.sin(ang)
    zero = lambda n: jnp.zeros((s, n), F32)
    ck = jnp.concatenate([zero(NOPE), cos, cos, zero(DA - NOPE - ROPE)], axis=1)
    sk = jnp.concatenate([zero(NOPE), -sin, sin, zero(DA - NOPE - ROPE)], axis=1)
    cq = jnp.concatenate([jnp.ones((s, NOPE), F32), cos, cos, zero(DA - NOPE - ROPE)], axis=1) * SCALE_MLA
    return dict(cq=cq, sq=sk * SCALE_MLA, ck=ck, sk=sk)


def _mix_fwd(l, x1, wts, sm, tabs):
    z, h2 = _norm_mm(x1, 0, sm["mix_norm"][l], wts["w_in"][l], name=f"mix_in_{l}")
    qa, qn = _mla_q_prep(z, sm["q_a_norm"][l], wts["wq_a"][l], wts["wq_b"][l], tabs["cq"], tabs["sq"], name=f"mla_q_{l}")
    ka, va, kvn = _mla_kv_prep(z, sm["kv_a_norm"][l], wts["wk"][l], wts["wv"][l], tabs["ck"], tabs["sk"], name=f"mla_kv_{l}")
    oa, lse_a = _attn_fwd(qa, ka, va, VDIM, name=f"mla_attn_{l}")

    bd = _block_diag(wts["pool_w"][l]).astype(BF16)
    yb, pooled = _pool_fwd(z, Z_POOL // POOL_W, bd, sm["pool_scale"][l], name=f"pool_{l}")

    fb = jnp.pad(sm["fox_b_f"][l], (0, 8 - H)).reshape(8, 1)
    ft, c3t = _gate_fwd(z, Z_F // DA, fb, name=f"fox_gate_{l}")
    fqa, fka, fva = _fox_prep(z, c3t, name=f"fox_prep_{l}")
    oc, lse_c = _attn_fwd(fqa, fka, fva, FOX_D, name=f"fox_attn_{l}")

    x2, cat = _mix_out(oa, yb, oc, wts["w_out"][l], x1, name=f"mix_out_{l}")
    saved = dict(z=z, h2=h2, qn=qn, kvn=kvn, qa=qa, ka=ka, va=va, oa=oa, lse_a=lse_a, bd=bd, pooled=pooled,
                 fqa=fqa, fka=fka, fva=fva, ft=ft, fb=fb, oc=oc, lse_c=lse_c, cat=cat)
    return x2, saved


def _mix_bwd(l, x1, dx2, sv, wts, sm, tabs, tok=None):
    s = x1.shape[0]
    g = {}
    dx2b = (dx2 if tok is None else dx2 + tok).astype(BF16)
    g["w_out"] = _mm(sv["cat"], dx2b, "tn", name=f"d_w_out_{l}", tm=1024, tn=1024, tk=DW_TOKENS)
    doa, doc, dyb, dl_a, dl_c = _mix_out_bwd(dx2b, wts["w_out"][l], sv["oa"], sv["oc"], name=f"mix_out_bwd_{l}")

    dfqa, dfka, dfva, dcq, dck = _attn_bwd(sv["fqa"], sv["fka"], sv["fva"], doc, sv["lse_c"], dl_c, True, name=f"fox_attn_bwd_{l}")
    dfox = _fox_bwd_prep(dfqa, dfka, dfva, name=f"fox_bwd_prep_{l}")
    dc = jnp.pad(dcq.reshape(H, s) + dck.reshape(H, s), ((0, 8 - H), (0, 0)))
    dft, dfb = _gate_bwd(sv["ft"], sv["fb"], dc, name=f"fox_gate_bwd_{l}")
    g["fox_b_f"] = dfb[:H, 0]

    dq, dys, g["pool_scale"] = _pool_bwd_a(dyb, sv["pooled"], sv["bd"], sm["pool_scale"][l], name=f"pool_bwd_a_{l}")
    du = _pool_bwd_b(dq, name=f"pool_bwd_b_{l}")
    dbd = _mm(sv["pooled"], dys, "tn", name=f"d_pool_w_{l}")
    g["pool_w"] = jnp.stack([dbd[i * 64:(i + 1) * 64, i * 64:(i + 1) * 64] for i in range(4)])

    dqa_, dka_, dva_ = _attn_bwd(sv["qa"], sv["ka"], sv["va"], doa, sv["lse_a"], dl_a, False, name=f"mla_attn_bwd_{l}")
    dqab, dkv, dz3, dz15 = _mla_bwd_prep(dqa_, dka_, dva_, dft, tabs["cq"], tabs["sq"], tabs["ck"], tabs["sk"],
                                         name=f"mla_bwd_prep_{l}")
    wq_ab = jnp.concatenate([wts["wq_a"][l], wts["wq_b"][l]], axis=1)
    wkv = jnp.concatenate([wts["wk"][l], wts["wv"][l]], axis=1)
    dwq = _mm(sv["qn"], dqab, "tn", name=f"d_w_q_b_{l}", tn=768, tk=DW_TOKENS).reshape(Q_RANK, 2, H, DA)
    dwkv = _mm(sv["kvn"], dkv, "tn", name=f"d_w_kv_b_{l}", tn=768, tk=DW_TOKENS).reshape(KV_RANK, 2, H, DA)
    da, db = dwq[:, 0], dwq[:, 1]
    swapped = jnp.concatenate([jnp.zeros((Q_RANK, H, NOPE), F32), db[..., NOPE + HALF_ROPE:NOPE + ROPE],
                               db[..., NOPE:NOPE + HALF_ROPE]], axis=-1)
    g["w_q_b"] = (da[..., :NOPE + ROPE] + swapped).reshape(Q_RANK, H * (NOPE + ROPE))
    g["w_kv_b"] = jnp.concatenate([dwkv[:, 0, :, :NOPE], dwkv[:, 1, :, :VDIM]], axis=-1).reshape(KV_RANK, H * (NOPE + VDIM))
    dqn = _mm(dqab, wq_ab, "nt", name=f"d_qn_{l}", tk=2 * H * DA)
    dkvn = _mm(dkv, wkv, "nt", name=f"d_kvn_{l}", tk=2 * H * DA)
    dqa, g["q_a_norm"] = _rmsnorm_bwd(sv["z"], Z_QA // Q_RANK, sm["q_a_norm"][l], dqn, name=f"q_a_norm_bwd_{l}")
    dkva, g["kv_a_norm"] = _rmsnorm_bwd(sv["z"], Z_KVA // KV_RANK, sm["kv_a_norm"][l], dkvn, name=f"kv_a_norm_bwd_{l}")

    dz = jnp.concatenate([dqa.astype(BF16), dkva.astype(BF16), dz3, du.astype(BF16), dfox, dz15], axis=1)
    g["w_in"] = _mm(sv["h2"], dz, "tn", name=f"d_w_in_{l}", tm=1024, tn=1024, tk=DW_TOKENS)
    dh2 = _mm(dz, wts["w_in"][l], "nt", name=f"d_h2_{l}", tn=1024, tk=NZ)
    dx1, g["mix_norm"] = _rmsnorm_bwd(x1, 0, sm["mix_norm"][l], dh2, dx2, name=f"mix_norm_bwd_{l}")
    return dx1, g


DW_TOKENS = 2048


def _local_step(x, target, wts, sm, late_weights=None, grads_ready=None):
    s = x.shape[0]
    tabs = _rope_tables(s)
    acts = []
    xs = x
    for l in range(DEPTH):
        x1, gu1 = _ffn_fwd(xs, sm["ffn1_norm"][l], wts["ffn1_w_gu"][l], wts["ffn1_w_d2"][l], name=f"ffn1_fwd_{l}")
        x2, sv = _mix_fwd(l, x1, wts, sm, tabs)
        if l == 0 and late_weights is not None:
            late_weights(x2)
        x3, gu2 = _ffn_fwd(x2, sm["ffn2_norm"][l], wts["ffn2_w_gu"][l], wts["ffn2_w_d2"][l], name=f"ffn2_fwd_{l}")
        acts.append((xs, gu1, x1, sv, x2, gu2))
        xs = x3
    dx, g_final, loss = _loss_head(xs, sm["final_norm"], target, name="loss_head")
    grads = [dict() for _ in range(DEPTH)]
    for l in reversed(range(DEPTH)):
        x0, gu1, x1, sv, x2, gu2 = acts[l]
        g = grads[l]
        dx, dgu, act, hh, dy, g["ffn2_norm"] = _ffn_bwd(x2, dx, gu2, sm["ffn2_norm"][l], wts["ffn2_w_gu"][l], wts["ffn2_w_d2"][l],
                                                        name=f"ffn2_bwd_{l}")
        g["ffn2_w_down"] = _mm(act, dy, "tn", name=f"d_ffn2_w_down_{l}", tm=FF_SHARD, tn=1024, tk=DW_TOKENS)
        g["ffn2_w_gu"] = _mm(hh, dgu, "tn", name=f"d_ffn2_w_gu_{l}", tm=1024, tn=FF_SHARD, tk=DW_TOKENS, n_major_out=True)
        tok = None
        if grads_ready is not None:
            sm, tok = grads_ready(l, "ffn2", g, sm)
        dx, gm = _mix_bwd(l, x1, dx, sv, wts, sm, tabs, tok)
        g.update(gm)
        if grads_ready is not None:
            sm, _ = grads_ready(l, "mix", g, sm)
        dx, dgu, act, hh, dy, g["ffn1_norm"] = _ffn_bwd(x0, dx, gu1, sm["ffn1_norm"][l], wts["ffn1_w_gu"][l], wts["ffn1_w_d2"][l],
                                                        name=f"ffn1_bwd_{l}")
        g["ffn1_w_down"] = _mm(act, dy, "tn", name=f"d_ffn1_w_down_{l}", tm=FF_SHARD, tn=1024, tk=DW_TOKENS)
        g["ffn1_w_gu"] = _mm(hh, dgu, "tn", name=f"d_ffn1_w_gu_{l}", tm=1024, tn=FF_SHARD, tk=DW_TOKENS, n_major_out=True)
        if grads_ready is not None:
            sm, _ = grads_ready(l, "ffn1", g, sm)
    return loss, dx, grads, g_final


BIG = ["ffn1_w_gu", "ffn1_w_down", "w_in", "w_q_b", "w_kv_b", "w_out", "ffn2_w_gu", "ffn2_w_down"]
SMALL = ["ffn1_norm", "mix_norm", "q_a_norm", "kv_a_norm", "pool_w", "pool_scale", "fox_b_f", "ffn2_norm"]
SMALL_ROWS = 48


WEIGHT_VIEWS = ["ffn1_w_gu", "ffn1_w_d2", "w_in", "wq_a", "wq_b", "wk", "wv", "w_out", "ffn2_w_gu", "ffn2_w_d2"]


def _prepare_weights(gathered, wts):
    for (nm, l), w in gathered.items():
        if nm in ("ffn1_w_gu", "ffn2_w_gu"):
            wts[nm][l] = w
        elif nm in ("ffn1_w_down", "ffn2_w_down"):
            wts[nm[:5] + "w_d2"][l] = w.reshape(2, FF_SHARD, D)
        elif nm in ("w_in", "w_out"):
            wts[nm][l] = w.reshape(D, -1)
        elif nm == "w_q_b":
            wq = jnp.moveaxis(w, 0, 1).reshape(Q_RANK, H, NOPE + ROPE)
            zq = lambda n: jnp.zeros((Q_RANK, H, n), BF16)
            wts["wq_a"][l] = jnp.concatenate([wq, zq(DA - NOPE - ROPE)], axis=-1).reshape(Q_RANK, H * DA)
            wts["wq_b"][l] = jnp.concatenate([zq(NOPE), wq[..., NOPE + HALF_ROPE:], wq[..., NOPE:NOPE + HALF_ROPE],
                                              zq(DA - NOPE - ROPE)], axis=-1).reshape(Q_RANK, H * DA)
        else:
            wkv = jnp.moveaxis(w, 0, 1).reshape(KV_RANK, H, NOPE + VDIM)
            zk = jnp.zeros((KV_RANK, H, DA - NOPE), BF16)
            wts["wk"][l] = jnp.concatenate([wkv[..., :NOPE], zk], axis=-1).reshape(KV_RANK, H * DA)
            wts["wv"][l] = jnp.concatenate([wkv[..., NOPE:], zk], axis=-1).reshape(KV_RANK, H * DA)


def _chip_major(name, g):
    if name in ("ffn1_w_gu", "ffn2_w_gu"):
        return g
    if name in ("ffn1_w_down", "ffn2_w_down", "w_in", "w_out"):
        return g.reshape(N_CHIPS, g.shape[0] // N_CHIPS, g.shape[1])
    return jnp.moveaxis(g.reshape(g.shape[0], N_CHIPS, g.shape[1] // N_CHIPS), 1, 0)


def _pack_small(grads, g_final, loss):
    parts = []
    for l in range(DEPTH):
        for nm in SMALL:
            parts.append(grads[l][nm].reshape(-1))
    parts.append(g_final.reshape(-1))
    parts.append(loss.reshape(1))
    flat = jnp.concatenate(parts)
    return jnp.pad(flat, (0, SMALL_ROWS * D - flat.shape[0])).reshape(SMALL_ROWS, D)


def _unpack_small(packed, params):
    flat = packed.reshape(-1)
    out = {nm: [] for nm in SMALL}
    off = 0
    for l in range(DEPTH):
        for nm in SMALL:
            shp = params[nm].shape[1:]
            n = int(np.prod(shp))
            out[nm].append(flat[off:off + n].reshape(shp))
            off += n
    res = {nm: jnp.stack(v) for nm, v in out.items()}
    res["final_norm"] = flat[off:off + D]
    return res, flat[off + D]


def _update(name, w, g, m, v):
    shp = w.shape
    if w.ndim == 1:
        view = (1, shp[0])
    elif w.size <= 65536:
        view = (shp[0], w.size // shp[0])
    else:
        view = (w.size // shp[-1], shp[-1])
    tr = view[0]
    for cand in (512, 352, 256, 128):
        if view[0] % cand == 0 and view[0] > cand:
            tr = cand
            break
    d, mn, vn = _adamw(w.reshape(view), g.reshape(view), m.reshape(view), v.reshape(view), name="adamw_" + name, tr=tr)
    return d.reshape(shp), mn.reshape(shp), vn.reshape(shp)


WEIGHTS = ['ffn1_norm', 'ffn1_w_gu', 'ffn1_w_down', 'mix_norm', 'w_in', 'q_a_norm', 'w_q_b', 'kv_a_norm', 'w_kv_b', 'pool_w',
           'pool_scale', 'fox_b_f', 'w_out', 'ffn2_norm', 'ffn2_w_gu', 'ffn2_w_down', 'final_norm']


def kernel(x, ffn1_norm, ffn1_w_gu, ffn1_w_down, mix_norm, w_in, q_a_norm, w_q_b, kv_a_norm, w_kv_b, pool_w, pool_scale, fox_b_f, w_out, ffn2_norm, ffn2_w_gu, ffn2_w_down, final_norm, loss_target, m_ffn1_norm, m_ffn1_w_gu, m_ffn1_w_down, m_mix_norm, m_w_in, m_q_a_norm, m_w_q_b, m_kv_a_norm, m_w_kv_b, m_pool_w, m_pool_scale, m_fox_b_f, m_w_out, m_ffn2_norm, m_ffn2_w_gu, m_ffn2_w_down, m_final_norm, v_ffn1_norm, v_ffn1_w_gu, v_ffn1_w_down, v_mix_norm, v_w_in, v_q_a_norm, v_w_q_b, v_kv_a_norm, v_w_kv_b, v_pool_w, v_pool_scale, v_fox_b_f, v_w_out, v_ffn2_norm, v_ffn2_w_gu, v_ffn2_w_down, v_final_norm):
    params = dict(ffn1_norm=ffn1_norm, ffn1_w_gu=ffn1_w_gu, ffn1_w_down=ffn1_w_down, mix_norm=mix_norm, w_in=w_in, q_a_norm=q_a_norm,
                  w_q_b=w_q_b, kv_a_norm=kv_a_norm, w_kv_b=w_kv_b, pool_w=pool_w, pool_scale=pool_scale, fox_b_f=fox_b_f, w_out=w_out,
                  ffn2_norm=ffn2_norm, ffn2_w_gu=ffn2_w_gu, ffn2_w_down=ffn2_w_down, final_norm=final_norm)
    mom = dict(ffn1_norm=m_ffn1_norm, ffn1_w_gu=m_ffn1_w_gu, ffn1_w_down=m_ffn1_w_down, mix_norm=m_mix_norm, w_in=m_w_in,
               q_a_norm=m_q_a_norm, w_q_b=m_w_q_b, kv_a_norm=m_kv_a_norm, w_kv_b=m_w_kv_b, pool_w=m_pool_w, pool_scale=m_pool_scale,
               fox_b_f=m_fox_b_f, w_out=m_w_out, ffn2_norm=m_ffn2_norm, ffn2_w_gu=m_ffn2_w_gu, ffn2_w_down=m_ffn2_w_down,
               final_norm=m_final_norm)
    var = dict(ffn1_norm=v_ffn1_norm, ffn1_w_gu=v_ffn1_w_gu, ffn1_w_down=v_ffn1_w_down, mix_norm=v_mix_norm, w_in=v_w_in,
               q_a_norm=v_q_a_norm, w_q_b=v_w_q_b, kv_a_norm=v_kv_a_norm, w_kv_b=v_w_kv_b, pool_w=v_pool_w, pool_scale=v_pool_scale,
               fox_b_f=v_fox_b_f, w_out=v_w_out, ffn2_norm=v_ffn2_norm, ffn2_w_gu=v_ffn2_w_gu, ffn2_w_down=v_ffn2_w_down,
               final_norm=v_final_norm)

    shard = {}
    for nm in BIG:
        w = _pad_w_in(params[nm]) if nm == "w_in" else params[nm]
        for l in range(DEPTH):
            shard[(nm, l)] = w[l].astype(BF16)
    first = [(nm, 0) for nm in BIG if not nm.startswith("ffn2")]
    rest = [k for k in shard if k not in first]
    wts = {nm: [None] * DEPTH for nm in WEIGHT_VIEWS}
    wts["pool_w"] = params["pool_w"]
    got = _gather_blocking([shard[k] for k in first])
    _prepare_weights(dict(zip(first, got)), wts)
    sems, src_thru, land_thru, token = _gather_start([shard[k] for k in rest], got[0])
    sm = dict(params)
    sm["ffn1_norm"] = params["ffn1_norm"] + token[0, 0]

    def late_weights(x2):
        lands = _gather_forward(_gather_wait(sems, src_thru, land_thru, x2))
        _prepare_weights(dict(zip(rest, lands)), wts)

    pos = _position()
    flight = {}

    groups = {"l1": (1, BIG), "l0a": (0, [nm for nm in BIG if not nm.startswith("ffn1")]),
              "l0b": (0, [nm for nm in BIG if nm.startswith("ffn1")])}
    pending = {}

    def to_chips(key, full, sib):
        psum = _sum2_bf16(pos, full, sib, name=f"chip_sum_{key}")
        s2 = _split_start(_stage2_copies, psum, [((3,) + p.shape[1:], p.dtype) for p in psum], 3 * len(psum),
                          f"reduce_stage2_start_{key}")
        flight[key] = (full, sib, s2)
        return s2[3][0, 0]

    def grads_ready(l, stage, g, sm_now):
        behind, tok = None, None
        if (l, stage) == (1, "ffn1"):
            full = [_chip_major(nm, g[nm]) for nm in BIG]
            pending["l1"] = _split_start(_stage1_copies, full, [((N_CHIPS, f.shape[1] // 2, f.shape[2]), F32) for f in full],
                                         len(full), "reduce_stage1_start_l1")
            behind, tok = "ffn2_norm", pending["l1"][3][0, 0]
        elif (l, stage) == (0, "ffn2"):
            sems1, full_thru, sib_land, _ = pending["l1"]
            full, sib = _split_wait(_stage1_copies, sems1, full_thru, sib_land, g["ffn2_w_down"], "reduce_stage1_wait_l1")
            tok = to_chips("l1", full, sib)
        elif l == 0:
            key = "l0a" if stage == "mix" else "l0b"
            full = [_chip_major(nm, g[nm]) for nm in groups[key][1]]
            behind, tok = "ffn1_norm", to_chips(key, full, _reduce_stage1(full, key))
        if behind is None:
            return sm_now, tok
        sm_next = dict(sm_now)
        sm_next[behind] = sm_now[behind] + tok
        return sm_next, tok

    loss, dx, grads, g_final = _local_step(x[0], loss_target[0], wts, sm, late_weights, grads_ready)

    def view2d(a):
        return a.reshape(a.size // a.shape[-1], a.shape[-1])

    after = flight["l0b"][2][3]
    done = {nm: None for nm in BIG}
    for key in ("l1", "l0a", "l0b"):
        l, names = groups[key]
        full, sib, (sems2, ps_thru, lands2, _) = flight[key]
        _, recv = _split_wait(_stage2_copies, sems2, ps_thru, lands2, after, f"reduce_stage2_wait_{key}")
        whole = _reduce_stage3(_sum5(pos, full, sib, recv, name=f"grad_sum_{key}"), key)
        for nm, g_l in zip(names, whole):
            if nm == "w_in":
                g_l = _unpad_w_in(g_l)
            tr = max(t for t in (512, 352, 256, 128) if g_l.shape[0] % t == 0)
            done[nm] = _adamw_layer(view2d(params[nm]), g_l, view2d(mom[nm]), view2d(var[nm]), l, done[nm],
                                    name=f"adamw_{nm}_{l}", tr=tr)
        after = done[names[-1]][0][-8:, 0:128]
        if key == "l1":
            small_g, loss = _unpack_small(_allreduce_small(_pack_small(grads, g_final, loss)), params)
            after = after + small_g["final_norm"][0]
    gw, delta, new_m, new_v = dict(small_g), {}, {}, {}
    for nm in BIG:
        delta[nm], new_m[nm], new_v[nm], gw[nm] = [a.reshape(params[nm].shape) for a in done[nm]]
    for nm in small_g:
        delta[nm], new_m[nm], new_v[nm] = _update(nm, params[nm], gw[nm], mom[nm], var[nm])
    return (loss, dx[None], *[gw[n] for n in WEIGHTS], *[delta[n] for n in WEIGHTS], *[new_m[n] for n in WEIGHTS],
            *[new_v[n] for n in WEIGHTS])
```

```python
import functools
import math

import jax
import jax.numpy as jnp
import numpy as np
from jax import lax
from jax.experimental import pallas as pl
from jax.experimental.pallas import tpu as pltpu

F32 = jnp.float32
BF16 = jnp.bfloat16
MESH = pl.DeviceIdType.MESH
HBM_SPEC = pl.BlockSpec(memory_space=pltpu.HBM)

D = 1024
DEPTH = 2
D_FF = 2816
FF_SHARD = 1408
N_CHIPS = 4
H = 6
NOPE, ROPE, VDIM = 64, 32, 64
HALF_ROPE = ROPE // 2
Q_RANK, KV_RANK = 256, 128
POOL_W = 256
FOX_D = 64
N_IN = 1830
NZ = 2048
ROPE_THETA = 10000.0
EPS = 1e-6
POOL_HALO = 16
Z_QA, Z_KVA, Z_KR, Z_POOL, Z_FOX, Z_F = 0, 256, 384, 512, 768, 1920

ADAM_LR, ADAM_B1, ADAM_B2, ADAM_EPS, ADAM_WD, ADAM_STEP = 0.001, 0.9, 0.999, 1e-08, 0.01, 10

VMEM_LIMIT_V7X = 56 * 1024 * 1024


def _cp(sem=None, vmem=VMEM_LIMIT_V7X):
    return pltpu.CompilerParams(dimension_semantics=sem, vmem_limit_bytes=vmem)


def _sigmoid(x):
    return 1.0 / (1.0 + jnp.exp(-x))


def _dot(a, b, dims):
    return lax.dot_general(a, b, (dims, ((), ())), preferred_element_type=F32)


NN = ((1,), (0,))
NT = ((1,), (1,))
TN = ((0,), (0,))


def _mm(a, b, mode, *, name, out_dtype=F32, add=None, alpha=None, tm=512, tn=512, tk=512, n_major_out=False):
    if mode == "nn":
        (m, k), (k2, n) = a.shape, b.shape
    elif mode == "nt":
        (m, k), (n, k2) = a.shape, b.shape
    else:
        (k, m), (k2, n) = a.shape, b.shape
    assert k == k2
    tm, tn, tk = min(tm, m), min(tn, n), min(tk, k)
    assert m % tm == 0 and n % tn == 0 and k % tk == 0, (name, m, n, k, tm, tn, tk)
    nk = k // tk
    dims = {"nn": NN, "nt": NT, "tn": TN}[mode]
    a_spec = pl.BlockSpec((tk, tm), lambda i, j, kk: (kk, i)) if mode == "tn" else pl.BlockSpec((tm, tk), lambda i, j, kk: (i, kk))
    b_spec = pl.BlockSpec((tn, tk), lambda i, j, kk: (j, kk)) if mode == "nt" else pl.BlockSpec((tk, tn), lambda i, j, kk: (kk, j))
    in_specs = [a_spec, b_spec]
    args = [a, b]
    if add is not None:
        in_specs.append(pl.BlockSpec((tm, tn), lambda i, j, kk: (i, j)))
        args.append(add)
    if n_major_out:
        out_shape = jax.ShapeDtypeStruct((n // tn, m, tn), out_dtype)
        out_spec = pl.BlockSpec((None, tm, tn), lambda i, j, kk: (j, i, 0))
    else:
        out_shape = jax.ShapeDtypeStruct((m, n), out_dtype)
        out_spec = pl.BlockSpec((tm, tn), lambda i, j, kk: (i, j))

    def body(*refs):
        a_ref, b_ref = refs[0], refs[1]
        add_ref = refs[2] if add is not None else None
        o_ref, acc = refs[-2], refs[-1]
        kk = pl.program_id(2)

        @pl.when(kk == 0)
        def _():
            acc[...] = jnp.zeros_like(acc)

        acc[...] += _dot(a_ref[...].astype(BF16), b_ref[...].astype(BF16), dims)

        @pl.when(kk == nk - 1)
        def _():
            r = acc[...]
            if alpha is not None:
                r = r * alpha
            if add_ref is not None:
                r = r + add_ref[...].astype(F32)
            o_ref[...] = r.astype(out_dtype)

    return pl.pallas_call(
        body, name=name, grid=(m // tm, n // tn, nk), in_specs=in_specs, out_specs=out_spec, out_shape=out_shape,
        scratch_shapes=[pltpu.VMEM((tm, tn), F32)],
        compiler_params=_cp(("parallel", "parallel", "arbitrary")),
    )(*args)


def _norm_mm(x, col_block, gain, w, *, name, tm=512):
    s = x.shape[0]
    k, n = w.shape
    tm = min(tm, s)

    def body(x_ref, g_ref, w_ref, z_ref, h_ref):
        xv = x_ref[...]
        r = lax.rsqrt(jnp.mean(xv * xv, axis=-1, keepdims=True) + EPS)
        hv = (xv * r * g_ref[...]).astype(BF16)
        h_ref[...] = hv
        z_ref[...] = _dot(hv, w_ref[...], NN)

    return pl.pallas_call(
        body, name=name, grid=(s // tm,),
        in_specs=[pl.BlockSpec((tm, k), lambda i: (i, col_block)), pl.BlockSpec((1, k), lambda i: (0, 0)),
                  pl.BlockSpec((k, n), lambda i: (0, 0))],
        out_specs=[pl.BlockSpec((tm, n), lambda i: (i, 0)), pl.BlockSpec((tm, k), lambda i: (i, 0))],
        out_shape=[jax.ShapeDtypeStruct((s, n), F32), jax.ShapeDtypeStruct((s, k), BF16)],
        compiler_params=_cp(("parallel",)),
    )(x, gain.reshape(1, k), w)


def _rmsnorm_bwd(x, col_block, gain, dh, dres=None, *, name, tm=512):
    s = x.shape[0]
    k = gain.shape[-1]
    tm = min(tm, s)

    def body(*refs):
        x_ref, g_ref, dh_ref = refs[0], refs[1], refs[2]
        dres_ref = refs[3] if dres is not None else None
        dx_ref, dg_ref = refs[-2], refs[-1]
        xv = x_ref[...]
        r = lax.rsqrt(jnp.mean(xv * xv, axis=-1, keepdims=True) + EPS)
        dhv = dh_ref[...].astype(F32)
        a = dhv * g_ref[...]
        dx = r * a - xv * (r * r * r) * jnp.mean(a * xv, axis=-1, keepdims=True)
        if dres_ref is not None:
            dx = dx + dres_ref[...]
        dx_ref[...] = dx

        @pl.when(pl.program_id(0) == 0)
        def _():
            dg_ref[...] = jnp.zeros_like(dg_ref)

        dg_ref[...] += jnp.sum(dhv * xv * r, axis=0, keepdims=True)

    in_specs = [pl.BlockSpec((tm, k), lambda i: (i, col_block)), pl.BlockSpec((1, k), lambda i: (0, 0)),
                pl.BlockSpec((tm, k), lambda i: (i, 0))]
    args = [x, gain.reshape(1, k), dh]
    if dres is not None:
        in_specs.append(pl.BlockSpec((tm, k), lambda i: (i, 0)))
        args.append(dres)
    dx, dg = pl.pallas_call(
        body, name=name, grid=(s // tm,), in_specs=in_specs,
        out_specs=[pl.BlockSpec((tm, k), lambda i: (i, 0)), pl.BlockSpec((1, k), lambda i: (0, 0))],
        out_shape=[jax.ShapeDtypeStruct((s, k), F32), jax.ShapeDtypeStruct((1, k), F32)],
        compiler_params=_cp(("arbitrary",)),
    )(*args)
    return dx, dg.reshape(k)


def _ffn_fwd(x, gain, w_gu4, w_d2, *, name, tm=256):
    s = x.shape[0]
    tm = min(tm, s)

    def body(x_ref, g_ref, wgu_ref, wd_ref, xo_ref, gu_ref):
        xv = x_ref[...]
        r = lax.rsqrt(jnp.mean(xv * xv, axis=-1, keepdims=True) + EPS)
        hv = (xv * r * g_ref[...]).astype(BF16)
        y = jnp.zeros((tm, D), F32)
        for j in range(2):
            g = _dot(hv, wgu_ref[j], NN)
            u = _dot(hv, wgu_ref[2 + j], NN)
            gu_ref[:, j * FF_SHARD:(j + 1) * FF_SHARD] = g.astype(BF16)
            gu_ref[:, D_FF + j * FF_SHARD:D_FF + (j + 1) * FF_SHARD] = u.astype(BF16)
            act = (g * _sigmoid(g) * u).astype(BF16)
            y = y + _dot(act, wd_ref[j], NN)
        xo_ref[...] = xv + 0.5 * y

    return pl.pallas_call(
        body, name=name, grid=(s // tm,),
        in_specs=[pl.BlockSpec((tm, D), lambda i: (i, 0)), pl.BlockSpec((1, D), lambda i: (0, 0)),
                  pl.BlockSpec((N_CHIPS, D, FF_SHARD), lambda i: (0, 0, 0), pipeline_mode=pl.Buffered(1)),
                  pl.BlockSpec((2, FF_SHARD, D), lambda i: (0, 0, 0), pipeline_mode=pl.Buffered(1))],
        out_specs=[pl.BlockSpec((tm, D), lambda i: (i, 0)), pl.BlockSpec((tm, 2 * D_FF), lambda i: (i, 0))],
        out_shape=[jax.ShapeDtypeStruct((s, D), F32), jax.ShapeDtypeStruct((s, 2 * D_FF), BF16)],
        compiler_params=_cp(("parallel",)),
    )(x, gain.reshape(1, D), w_gu4, w_d2)


FFN_ROW_CHUNK = 32


def _ffn_bwd(x, dxo, gu, gain, w_gu4, w_d2, *, name, tm=256):
    s = x.shape[0]
    tm = min(tm, s)

    def body(x_ref, dxo_ref, gu_ref, g_ref, wgu_ref, wd_ref, dx_ref, dgu_ref, act_ref, h_ref, dy_ref, dg_ref):
        xv = x_ref[...]
        r = lax.rsqrt(jnp.mean(xv * xv, axis=-1, keepdims=True) + EPS)
        xh = xv * r
        h_ref[...] = (xh * g_ref[...]).astype(BF16)
        dxov = dxo_ref[...]
        dy = (0.5 * dxov).astype(BF16)
        dy_ref[...] = dy
        gcols = [slice(j * FF_SHARD, (j + 1) * FF_SHARD) for j in range(2)]
        ucols = [slice(D_FF + j * FF_SHARD, D_FF + (j + 1) * FF_SHARD) for j in range(2)]
        dacts = [_dot(dy, wd_ref[j], NT) for j in range(2)]
        for r0 in range(0, tm, FFN_ROW_CHUNK):
            rows = slice(r0, r0 + FFN_ROW_CHUNK)
            for j in range(2):
                g = gu_ref[rows, gcols[j]].astype(F32)
                u = gu_ref[rows, ucols[j]].astype(F32)
                sg = _sigmoid(g)
                silu = g * sg
                da = dacts[j][rows]
                act_ref[rows, gcols[j]] = (silu * u).astype(BF16)
                dgu_ref[rows, gcols[j]] = (da * u * (sg * (1.0 + g * (1.0 - sg)))).astype(BF16)
                dgu_ref[rows, ucols[j]] = (da * silu).astype(BF16)
        dh = jnp.zeros((tm, D), F32)
        for j in range(2):
            dh = dh + _dot(dgu_ref[:, gcols[j]], wgu_ref[j], NT) + _dot(dgu_ref[:, ucols[j]], wgu_ref[2 + j], NT)
        a = dh * g_ref[...]
        dx_ref[...] = dxov + r * a - xh * (r * jnp.mean(a * xh, axis=-1, keepdims=True))

        @pl.when(pl.program_id(0) == 0)
        def _():
            dg_ref[...] = jnp.zeros_like(dg_ref)

        dg_ref[...] += jnp.sum(dh * xh, axis=0, keepdims=True)

    row = lambda i: (i, 0)
    outs = pl.pallas_call(
        body, name=name, grid=(s // tm,),
        in_specs=[pl.BlockSpec((tm, D), row), pl.BlockSpec((tm, D), row), pl.BlockSpec((tm, 2 * D_FF), row),
                  pl.BlockSpec((1, D), lambda i: (0, 0)),
                  pl.BlockSpec((N_CHIPS, D, FF_SHARD), lambda i: (0, 0, 0), pipeline_mode=pl.Buffered(1)),
                  pl.BlockSpec((2, FF_SHARD, D), lambda i: (0, 0, 0), pipeline_mode=pl.Buffered(1))],
        out_specs=[pl.BlockSpec((tm, D), row), pl.BlockSpec((tm, 2 * D_FF), row), pl.BlockSpec((tm, D_FF), row),
                   pl.BlockSpec((tm, D), row), pl.BlockSpec((tm, D), row), pl.BlockSpec((1, D), lambda i: (0, 0))],
        out_shape=[jax.ShapeDtypeStruct((s, D), F32), jax.ShapeDtypeStruct((s, 2 * D_FF), BF16),
                   jax.ShapeDtypeStruct((s, D_FF), BF16), jax.ShapeDtypeStruct((s, D), BF16),
                   jax.ShapeDtypeStruct((s, D), BF16), jax.ShapeDtypeStruct((1, D), F32)],
        compiler_params=_cp(("arbitrary",)),
    )(x, dxo, gu, gain.reshape(1, D), w_gu4, w_d2)
    dx, dgu, act, h, dy, dg = outs
    return dx, dgu, act, h, dy, dg.reshape(D)


DA = 128
SCALE_MLA = 1.0 / math.sqrt(NOPE + ROPE)
SCALE_FOX = 1.0 / math.sqrt(FOX_D)


def _causal_blocks(nb, key_major):
    if key_major:
        pairs = [(i, j) for j in range(nb) for i in range(j, nb)]
    else:
        pairs = [(i, j) for i in range(nb) for j in range(i + 1)]
    return (jnp.asarray(np.array([p[0] for p in pairs], np.int32)), jnp.asarray(np.array([p[1] for p in pairs], np.int32)))


HEADS_PER_STEP = 2
ROW_CHUNK = 64

def _col_to_row(col):
    return jnp.broadcast_to(col, (col.shape[0], DA)).T[0:1, :]


def _attn_fwd(qa, ka, va, dv, *, name, t=512):
    h, s, _ = qa.shape
    t = min(t, s)
    nb = s // t
    g = 3
    qi, kj = _causal_blocks(nb, key_major=False)

    rc = min(ROW_CHUNK, t)

    def body(qi_ref, kj_ref, q_ref, k_ref, v_ref, o_ref, lse_ref, m_sc, acc_sc, p_sc, a_sc):
        n = pl.program_id(1)
        i, j = qi_ref[n], kj_ref[n]

        @pl.when(j == 0)
        def _():
            m_sc[...] = jnp.full_like(m_sc, -jnp.inf)
            acc_sc[...] = jnp.zeros_like(acc_sc)

        def step(masked):
            scs = [_dot(q_ref[hh], k_ref[hh], NT) for hh in range(g)]
            for r0 in range(0, t, rc):
                rows = slice(r0, r0 + rc)
                for hh in range(g):
                    sr = scs[hh][rows]
                    if masked:
                        row = lax.broadcasted_iota(jnp.int32, (rc, t), 0) + r0
                        col = lax.broadcasted_iota(jnp.int32, (rc, t), 1)
                        sr = jnp.where(col <= row, sr, -jnp.inf)
                    tiles = [sr[:, c0:c0 + DA] for c0 in range(0, t, DA)]
                    top = tiles[0]
                    for tile in tiles[1:]:
                        top = jnp.maximum(top, tile)
                    m_old = m_sc[hh, rows]
                    m_new = jnp.maximum(m_old, jnp.max(top, axis=-1, keepdims=True))
                    for c0, tile in zip(range(0, t, DA), tiles):
                        p_sc[hh, rows, c0:c0 + DA] = jnp.exp(tile - m_new).astype(BF16)
                    a_sc[hh, rows] = jnp.exp(m_old - m_new)
                    m_sc[hh, rows] = m_new
            for hh in range(g):
                acc_sc[hh] = a_sc[hh] * acc_sc[hh] + _dot(p_sc[hh], v_ref[hh], NN)

        @pl.when(j < i)
        def _():
            step(False)

        @pl.when(j == i)
        def _():
            step(True)
            for hh in range(g):
                acc = acc_sc[hh]
                l = acc[:, dv:dv + 1]
                o_ref[hh] = acc[:, :dv] / l
                lse_ref[hh] = _col_to_row(m_sc[hh][:, 0:1] + jnp.log(l))

    qmap = lambda hg, n, qi_r, kj_r: (hg, qi_r[n], 0)
    kmap = lambda hg, n, qi_r, kj_r: (hg, kj_r[n], 0)
    return pl.pallas_call(
        body, name=name,
        grid_spec=pltpu.PrefetchScalarGridSpec(
            num_scalar_prefetch=2, grid=(h // g, qi.shape[0]),
            in_specs=[pl.BlockSpec((g, t, DA), qmap), pl.BlockSpec((g, t, DA), kmap), pl.BlockSpec((g, t, DA), kmap)],
            out_specs=[pl.BlockSpec((g, t, dv), qmap), pl.BlockSpec((g, 1, t), lambda hg, n, qi_r, kj_r: (hg, 0, qi_r[n]))],
            scratch_shapes=[pltpu.VMEM((g, t, DA), F32), pltpu.VMEM((g, t, DA), F32), pltpu.VMEM((g, t, t), BF16),
                            pltpu.VMEM((g, t, DA), F32)]),
        out_shape=[jax.ShapeDtypeStruct((h, s, dv), F32), jax.ShapeDtypeStruct((h, 1, s), F32)],
        compiler_params=_cp(("parallel", "arbitrary")),
    )(qi, kj, qa, ka, va)


def _attn_bwd(qa, ka, va, doa, lse_row, delta_row, decay, *, name, t=512):
    h, s, _ = qa.shape
    t = min(t, s)
    nb = s // t
    g = 3
    rc = min(ROW_CHUNK, t)
    qi, kj = _causal_blocks(nb, key_major=True)
    nsteps = qi.shape[0]

    def body(*refs):
        qi_ref, kj_ref, q_ref, k_ref, v_ref, do_ref, lse_ref, dl_ref = refs[:8]
        p_sc, ds_sc = refs[-2:]
        if decay:
            dq_ref, dk_ref, dv_ref, dcq_ref, dck_ref, dq_acc, dk_acc, dv_acc, dcq_acc, dck_acc = refs[8:-2]
        else:
            dq_ref, dk_ref, dv_ref, dq_acc, dk_acc, dv_acc = refs[8:-2]
        n = pl.program_id(1)
        i, j = qi_ref[n], kj_ref[n]

        @pl.when(n == 0)
        def _():
            dq_acc[...] = jnp.zeros_like(dq_acc)
            if decay:
                dcq_acc[...] = jnp.zeros_like(dcq_acc)

        @pl.when(i == j)
        def _():
            dk_acc[...] = jnp.zeros_like(dk_acc)
            dv_acc[...] = jnp.zeros_like(dv_acc)
            if decay:
                dck_acc[...] = jnp.zeros_like(dck_acc)

        def step(masked):
            sts = [_dot(k_ref[hh], q_ref[hh], NT) for hh in range(g)]
            dpts = [_dot(v_ref[hh], do_ref[hh], NT) for hh in range(g)]
            dcq = [jnp.zeros((1, t), F32) for _ in range(g)]
            for r0 in range(0, t, rc):
                rows = slice(r0, r0 + rc)
                for hh in range(g):
                    st = sts[hh][rows]
                    if masked:
                        row = lax.broadcasted_iota(jnp.int32, (rc, t), 0) + r0
                        col = lax.broadcasted_iota(jnp.int32, (rc, t), 1)
                        st = jnp.where(row <= col, st, -jnp.inf)
                    pt = jnp.exp(st - lse_ref[hh])
                    dst = pt * (dpts[hh][rows] - dl_ref[hh])
                    p_sc[hh, rows] = pt.astype(BF16)
                    ds_sc[hh, rows] = dst.astype(BF16)
                    if decay:
                        dcq[hh] = dcq[hh] + jnp.sum(dst, axis=0, keepdims=True)
                        dck_acc[hh, rows] -= jnp.sum(dst, axis=1, keepdims=True)
            for hh in range(g):
                dv_acc[hh] += _dot(p_sc[hh], do_ref[hh], NN)
                dk_acc[hh] += _dot(ds_sc[hh], q_ref[hh], NN)
                dq_acc[hh, i] += _dot(ds_sc[hh], k_ref[hh], TN)
                if decay:
                    dcq_acc[hh, i] += dcq[hh]

        @pl.when(i > j)
        def _():
            step(False)

        @pl.when(i == j)
        def _():
            step(True)

        @pl.when(i == nb - 1)
        def _():
            dk_ref[...] = dk_acc[...]
            dv_ref[...] = dv_acc[...]
            if decay:
                for hh in range(g):
                    dck_ref[hh] = _col_to_row(dck_acc[hh])

        @pl.when(n == nsteps - 1)
        def _():
            dq_ref[...] = dq_acc[...]
            if decay:
                dcq_ref[...] = dcq_acc[...]

    kmap = lambda hg, n, qi_r, kj_r: (hg, kj_r[n], 0)
    qmap = lambda hg, n, qi_r, kj_r: (hg, qi_r[n], 0)
    qrow = lambda hg, n, qi_r, kj_r: (hg, 0, qi_r[n])
    krow = lambda hg, n, qi_r, kj_r: (hg, 0, kj_r[n])
    whole = lambda hg, n, qi_r, kj_r: (hg, 0, 0, 0)
    in_specs = [pl.BlockSpec((g, t, DA), qmap), pl.BlockSpec((g, t, DA), kmap), pl.BlockSpec((g, t, DA), kmap),
                pl.BlockSpec((g, t, DA), qmap), pl.BlockSpec((g, 1, t), qrow), pl.BlockSpec((g, 1, t), qrow)]
    out_specs = [pl.BlockSpec((g, nb, t, DA), whole), pl.BlockSpec((g, t, DA), kmap), pl.BlockSpec((g, t, DA), kmap)]
    out_shape = [jax.ShapeDtypeStruct((h, nb, t, DA), F32), jax.ShapeDtypeStruct((h, s, DA), F32), jax.ShapeDtypeStruct((h, s, DA), F32)]
    scratch = [pltpu.VMEM((g, nb, t, DA), F32), pltpu.VMEM((g, t, DA), F32), pltpu.VMEM((g, t, DA), F32)]
    if decay:
        out_specs += [pl.BlockSpec((g, nb, 1, t), whole), pl.BlockSpec((g, 1, t), krow)]
        out_shape += [jax.ShapeDtypeStruct((h, nb, 1, t), F32), jax.ShapeDtypeStruct((h, 1, s), F32)]
        scratch += [pltpu.VMEM((g, nb, 1, t), F32), pltpu.VMEM((g, t, 1), F32)]
    scratch += [pltpu.VMEM((g, t, t), BF16), pltpu.VMEM((g, t, t), BF16)]
    outs = pl.pallas_call(
        body, name=name,
        grid_spec=pltpu.PrefetchScalarGridSpec(num_scalar_prefetch=2, grid=(h // g, nsteps), in_specs=in_specs, out_specs=out_specs,
                                               scratch_shapes=scratch),
        out_shape=out_shape, compiler_params=_cp(("parallel", "arbitrary")),
    )(qi, kj, qa, ka, va, doa, lse_row, delta_row)
    outs = list(outs)
    outs[0] = outs[0].reshape(h, s, DA)
    if decay:
        outs[3] = outs[3].reshape(h, 1, s)
    return outs


def _sel(rows, cols, pairs, value=1.0):
    m = np.zeros((rows, cols), np.float32)
    for r, c in pairs:
        m[r, c] = value
    return jnp.asarray(m, BF16)


def _lane_row(lanes):
    m = np.zeros((1, DA), np.float32)
    m[0, list(lanes)] = 1.0
    return jnp.asarray(m)


def _rms(xv, gain):
    r = lax.rsqrt(jnp.mean(xv * xv, axis=-1, keepdims=True) + EPS)
    return xv * r * gain


def _mla_q_prep(z, gain, wq_a, wq_b, cq, sq, *, name, tm=512):
    s = z.shape[0]
    tm = min(tm, s)

    def body(z_ref, g_ref, wa_ref, wb_ref, c_ref, s_ref, qa_ref, qn_ref):
        qn = _rms(z_ref[...], g_ref[...]).astype(BF16)
        qn_ref[...] = qn
        c, sn = c_ref[...], s_ref[...]
        for hh in range(H):
            cols = slice(hh * DA, (hh + 1) * DA)
            qa_ref[hh] = (_dot(qn, wa_ref[:, cols], NN) * c + _dot(qn, wb_ref[:, cols], NN) * sn).astype(BF16)

    row = lambda i: (i, 0)
    fixed = lambda i: (0, 0)
    return pl.pallas_call(
        body, name=name, grid=(s // tm,),
        in_specs=[pl.BlockSpec((tm, Q_RANK), lambda i: (i, Z_QA // Q_RANK)), pl.BlockSpec((1, Q_RANK), fixed),
                  pl.BlockSpec((Q_RANK, H * DA), fixed), pl.BlockSpec((Q_RANK, H * DA), fixed),
                  pl.BlockSpec((tm, DA), row), pl.BlockSpec((tm, DA), row)],
        out_specs=[pl.BlockSpec((H, tm, DA), lambda i: (0, i, 0)), pl.BlockSpec((tm, Q_RANK), row)],
        out_shape=[jax.ShapeDtypeStruct((H, s, DA), BF16), jax.ShapeDtypeStruct((s, Q_RANK), BF16)],
        compiler_params=_cp(("parallel",)),
    )(z, gain.reshape(1, Q_RANK), wq_a, wq_b, cq, sq)


def _mla_kv_prep(z, gain, wk, wv, ck, sk, *, name, tm=512):
    s = z.shape[0]
    tm = min(tm, s)
    one = _lane_row([VDIM])

    def body(zkv_ref, z3_ref, z15_ref, g_ref, wk_ref, wv_ref, c_ref, s_ref, one_ref, ka_ref, va_ref, kvn_ref):
        kvn = _rms(zkv_ref[...], g_ref[...]).astype(BF16)
        kvn_ref[...] = kvn
        kpe = z3_ref[...] * c_ref[...] + z15_ref[...] * s_ref[...]
        for hh in range(H):
            cols = slice(hh * DA, (hh + 1) * DA)
            ka_ref[hh] = (_dot(kvn, wk_ref[:, cols], NN) + kpe).astype(BF16)
            va_ref[hh] = (_dot(kvn, wv_ref[:, cols], NN) + one_ref[...]).astype(BF16)

    row = lambda i: (i, 0)
    fixed = lambda i: (0, 0)
    blk = lambda c: pl.BlockSpec((tm, DA), lambda i: (i, c))
    heads = pl.BlockSpec((H, tm, DA), lambda i: (0, i, 0))
    return pl.pallas_call(
        body, name=name, grid=(s // tm,),
        in_specs=[blk(Z_KVA // DA), blk(Z_KR // DA), blk(Z_F // DA), pl.BlockSpec((1, KV_RANK), fixed),
                  pl.BlockSpec((KV_RANK, H * DA), fixed), pl.BlockSpec((KV_RANK, H * DA), fixed),
                  pl.BlockSpec((tm, DA), row), pl.BlockSpec((tm, DA), row), pl.BlockSpec((1, DA), fixed)],
        out_specs=[heads, heads, pl.BlockSpec((tm, KV_RANK), row)],
        out_shape=[jax.ShapeDtypeStruct((H, s, DA), BF16), jax.ShapeDtypeStruct((H, s, DA), BF16),
                   jax.ShapeDtypeStruct((s, KV_RANK), BF16)],
        compiler_params=_cp(("parallel",)),
    )(z, z, z, gain.reshape(1, KV_RANK), wk, wv, ck, sk, one)


DEC_C = (FOX_D, FOX_D + 1, FOX_D + 2)
DEC_1 = (FOX_D + 3, FOX_D + 4, FOX_D + 5)


def _fox_prep(z, c3t, *, name, tm=512):
    s = z.shape[0]
    tm = min(tm, s)
    w = H * FOX_D
    left = [(r, r) for r in range(FOX_D)]
    right = [(FOX_D + r, r) for r in range(FOX_D)]
    pq = jnp.stack([_sel(DA, DA, left, SCALE_FOX), _sel(DA, DA, right, SCALE_FOX)])
    pk = jnp.stack([_sel(DA, DA, left), _sel(DA, DA, right)])
    pcq = jnp.stack([_sel(32, DA, [(hh + 8 * k, DEC_C[k]) for k in range(3)]) for hh in range(H)])
    pck = jnp.stack([_sel(32, DA, [(hh + 8 * k, DEC_1[k]) for k in range(3)], -1.0) for hh in range(H)])
    rows3 = jnp.concatenate([_lane_row(DEC_1), _lane_row(DEC_C), _lane_row([FOX_D])], axis=0)

    def body(zq_ref, zk_ref, zv_ref, c_ref, pq_ref, pk_ref, pcq_ref, pck_ref, r_ref, qa_ref, ka_ref, va_ref):
        c3 = c_ref[...]
        for pair in range(H // 2):
            lanes = slice(pair * DA, (pair + 1) * DA)
            zq, zk, zv = zq_ref[:, lanes].astype(BF16), zk_ref[:, lanes].astype(BF16), zv_ref[:, lanes].astype(BF16)
            for side in range(2):
                hh = 2 * pair + side
                qa_ref[hh] = (_dot(zq, pq_ref[side], NN) + _dot(c3, pcq_ref[hh], TN) + r_ref[0:1, :]).astype(BF16)
                ka_ref[hh] = (_dot(zk, pk_ref[side], NN) + _dot(c3, pck_ref[hh], TN) + r_ref[1:2, :]).astype(BF16)
                va_ref[hh] = (_dot(zv, pk_ref[side], NN) + r_ref[2:3, :]).astype(BF16)

    fixed2 = lambda i: (0, 0)
    fixed3 = lambda i: (0, 0, 0)
    heads = pl.BlockSpec((H, tm, DA), lambda i: (0, i, 0))
    zblk = lambda c: pl.BlockSpec((tm, w), lambda i: (i, c))
    return pl.pallas_call(
        body, name=name, grid=(s // tm,),
        in_specs=[zblk(Z_FOX // w), zblk(Z_FOX // w + 1), zblk(Z_FOX // w + 2), pl.BlockSpec((32, tm), lambda i: (0, i)),
                  pl.BlockSpec((2, DA, DA), fixed3), pl.BlockSpec((2, DA, DA), fixed3),
                  pl.BlockSpec((H, 32, DA), fixed3), pl.BlockSpec((H, 32, DA), fixed3), pl.BlockSpec((3, DA), fixed2)],
        out_specs=[heads, heads, heads], out_shape=[jax.ShapeDtypeStruct((H, s, DA), BF16)] * 3,
        compiler_params=_cp(("parallel",)),
    )(z, z, z, c3t, pq, pk, pcq, pck, rows3)


def _mix_out(oa, yb, oc, w_out, x1, *, name, tm=512):
    s = yb.shape[0]
    tm = min(tm, s)
    e2 = jnp.stack([_sel(VDIM, DA, [(r, r) for r in range(VDIM)]), _sel(VDIM, DA, [(r, VDIM + r) for r in range(VDIM)])])

    def body(oa_ref, yb_ref, oc_ref, e_ref, w_ref, x_ref, x2_ref, cat_ref):
        def pairs(o_ref):
            return [(_dot(o_ref[2 * p].astype(BF16), e_ref[0], NN) + _dot(o_ref[2 * p + 1].astype(BF16), e_ref[1], NN)).astype(BF16)
                    for p in range(H // 2)]

        cat = jnp.concatenate(pairs(oa_ref) + [yb_ref[...].astype(BF16)] + pairs(oc_ref), axis=1)
        cat_ref[...] = cat
        x2_ref[...] = x_ref[...] + _dot(cat, w_ref[...], NN)

    row = lambda i: (i, 0)
    heads = pl.BlockSpec((H, tm, VDIM), lambda i: (0, i, 0))
    return pl.pallas_call(
        body, name=name, grid=(s // tm,),
        in_specs=[heads, pl.BlockSpec((tm, POOL_W), row), heads, pl.BlockSpec((2, VDIM, DA), lambda i: (0, 0, 0)),
                  pl.BlockSpec((D, D), lambda i: (0, 0)), pl.BlockSpec((tm, D), row)],
        out_specs=[pl.BlockSpec((tm, D), row), pl.BlockSpec((tm, D), row)],
        out_shape=[jax.ShapeDtypeStruct((s, D), F32), jax.ShapeDtypeStruct((s, D), BF16)],
        compiler_params=_cp(("parallel",)),
    )(oa, yb, oc, e2, w_out, x1)


def _mix_out_bwd(dx2b, w_out, oa, oc, *, name, tm=512):
    s = dx2b.shape[0]
    tm = min(tm, s)
    f2 = jnp.stack([_sel(DA, DA, [(r, r) for r in range(VDIM)]), _sel(DA, DA, [(VDIM + r, r) for r in range(VDIM)])])
    nv = H * VDIM

    def body(dx_ref, w_ref, oa_ref, oc_ref, f_ref, doa_ref, doc_ref, dyb_ref, dla_ref, dlc_ref):
        dcat = _dot(dx_ref[...], w_ref[...], NT)
        dyb_ref[...] = dcat[:, nv:nv + POOL_W]
        for base, o_ref, do_ref, dl_ref in ((0, oa_ref, doa_ref, dla_ref), (nv + POOL_W, oc_ref, doc_ref, dlc_ref)):
            for p in range(H // 2):
                blk = dcat[:, base + p * DA:base + (p + 1) * DA].astype(BF16)
                for side in range(2):
                    hh = 2 * p + side
                    do = _dot(blk, f_ref[side], NN)
                    do_ref[hh] = do.astype(BF16)
                    dl_ref[hh] = _col_to_row(jnp.sum(do[:, :VDIM] * o_ref[hh], axis=-1, keepdims=True))

    row = lambda i: (i, 0)
    heads = lambda w: pl.BlockSpec((H, tm, w), lambda i: (0, i, 0))
    return pl.pallas_call(
        body, name=name, grid=(s // tm,),
        in_specs=[pl.BlockSpec((tm, D), row), pl.BlockSpec((D, D), lambda i: (0, 0)), heads(VDIM), heads(VDIM),
                  pl.BlockSpec((2, DA, DA), lambda i: (0, 0, 0))],
        out_specs=[heads(DA), heads(DA), pl.BlockSpec((tm, POOL_W), row),
                   pl.BlockSpec((H, 1, tm), lambda i: (0, 0, i)), pl.BlockSpec((H, 1, tm), lambda i: (0, 0, i))],
        out_shape=[jax.ShapeDtypeStruct((H, s, DA), BF16), jax.ShapeDtypeStruct((H, s, DA), BF16),
                   jax.ShapeDtypeStruct((s, POOL_W), F32), jax.ShapeDtypeStruct((H, 1, s), F32), jax.ShapeDtypeStruct((H, 1, s), F32)],
        compiler_params=_cp(("parallel",)),
    )(dx2b, w_out, oa, oc, f2)


def _mla_bwd_prep(dqa, dka, dva, dft, cq, sq, ck, sk, *, name, tm=512):
    s = dqa.shape[1]
    tm = min(tm, s)
    keep = _lane_row(range(NOPE))

    def body(dq_ref, dk_ref, dv_ref, dft_ref, cq_ref, sq_ref, ck_ref, sk_ref, keep_ref, dqab_ref, dkv_ref, dz3_ref, dz15_ref):
        cqv, sqv = cq_ref[...], sq_ref[...]
        dkpe = jnp.zeros((tm, DA), F32)
        for hh in range(H):
            lanes = slice(hh * DA, (hh + 1) * DA)
            dq = dq_ref[hh]
            dqab_ref[:, lanes] = (dq * cqv).astype(BF16)
            dqab_ref[:, H * DA + hh * DA:H * DA + (hh + 1) * DA] = (dq * sqv).astype(BF16)
            dk = dk_ref[hh]
            dkpe = dkpe + dk
            dkv_ref[:, lanes] = (dk * keep_ref[...]).astype(BF16)
            dkv_ref[:, H * DA + hh * DA:H * DA + (hh + 1) * DA] = (dv_ref[hh] * keep_ref[...]).astype(BF16)
        dz3_ref[...] = (dkpe * ck_ref[...]).astype(BF16)
        dz15_ref[...] = (dkpe * sk_ref[...] + dft_ref[...]).astype(BF16)

    row = lambda i: (i, 0)
    heads = pl.BlockSpec((H, tm, DA), lambda i: (0, i, 0))
    tab = pl.BlockSpec((tm, DA), row)
    return pl.pallas_call(
        body, name=name, grid=(s // tm,),
        in_specs=[heads, heads, heads, tab, tab, tab, tab, tab, pl.BlockSpec((1, DA), lambda i: (0, 0))],
        out_specs=[pl.BlockSpec((tm, 2 * H * DA), row), pl.BlockSpec((tm, 2 * H * DA), row), tab, tab],
        out_shape=[jax.ShapeDtypeStruct((s, 2 * H * DA), BF16), jax.ShapeDtypeStruct((s, 2 * H * DA), BF16),
                   jax.ShapeDtypeStruct((s, DA), BF16), jax.ShapeDtypeStruct((s, DA), BF16)],
        compiler_params=_cp(("parallel",)),
    )(dqa, dka, dva, dft, cq, sq, ck, sk, keep)


def _fox_bwd_prep(dfqa, dfka, dfva, *, name, tm=512):
    s = dfqa.shape[1]
    tm = min(tm, s)
    place = lambda v: jnp.stack([_sel(DA, DA, [(r, r) for r in range(FOX_D)], v), _sel(DA, DA, [(r, FOX_D + r) for r in range(FOX_D)], v)])
    gq, gk = place(SCALE_FOX), place(1.0)

    def body(dq_ref, dk_ref, dv_ref, gq_ref, gk_ref, dz_ref):
        for part, (d_ref, g_ref) in enumerate(((dq_ref, gq_ref), (dk_ref, gk_ref), (dv_ref, gk_ref))):
            for p in range(H // 2):
                blk = _dot(d_ref[2 * p].astype(BF16), g_ref[0], NN) + _dot(d_ref[2 * p + 1].astype(BF16), g_ref[1], NN)
                lo = part * H * FOX_D + p * DA
                dz_ref[:, lo:lo + DA] = blk.astype(BF16)

    heads = pl.BlockSpec((H, tm, DA), lambda i: (0, i, 0))
    sel = pl.BlockSpec((2, DA, DA), lambda i: (0, 0, 0))
    return pl.pallas_call(
        body, name=name, grid=(s // tm,), in_specs=[heads, heads, heads, sel, sel],
        out_specs=pl.BlockSpec((tm, 3 * H * FOX_D), lambda i: (i, 0)),
        out_shape=jax.ShapeDtypeStruct((s, 3 * H * FOX_D), BF16), compiler_params=_cp(("parallel",)),
    )(dfqa, dfka, dfva, gq, gk)


def _lane_scan(x, s, reverse):
    lane = lax.broadcasted_iota(jnp.int32, x.shape, 1)
    sh = 1
    while sh < s:
        if reverse:
            x = x + jnp.where(lane < s - sh, pltpu.roll(x, s - sh, axis=1), 0.0)
        else:
            x = x + jnp.where(lane >= sh, pltpu.roll(x, sh, axis=1), 0.0)
        sh *= 2
    return x


def _gate_fwd(z, col_block, bias, *, name):
    s = z.shape[0]

    def body(z_ref, b_ref, f_ref, c_ref):
        ft = z_ref[...].T[0:8, :]
        f_ref[...] = ft
        xg = ft + b_ref[...]
        lf = jnp.minimum(xg, 0.0) - jnp.log(1.0 + jnp.exp(-jnp.abs(xg)))
        c = _lane_scan(lf, s, False)
        hi = c.astype(BF16).astype(F32)
        r = c - hi
        mid = r.astype(BF16).astype(F32)
        lo = r - mid
        c_ref[...] = jnp.concatenate([hi, mid, lo, jnp.zeros_like(hi)], axis=0).astype(BF16)

    return pl.pallas_call(
        body, name=name, grid=(1,),
        in_specs=[pl.BlockSpec((s, 128), lambda i: (0, col_block)), pl.BlockSpec((8, 1), lambda i: (0, 0))],
        out_specs=[pl.BlockSpec((8, s), lambda i: (0, 0)), pl.BlockSpec((32, s), lambda i: (0, 0))],
        out_shape=[jax.ShapeDtypeStruct((8, s), F32), jax.ShapeDtypeStruct((32, s), BF16)],
        compiler_params=_cp(("arbitrary",)))(z, bias)


def _gate_bwd(ft, bias, dc, *, name):
    s = ft.shape[1]

    def body(f_ref, b_ref, dc_ref, df_ref, db_ref):
        xg = f_ref[...] + b_ref[...]
        dlf = _lane_scan(dc_ref[...], s, True)
        df = dlf * _sigmoid(-xg)
        db_ref[...] = jnp.sum(df, axis=-1, keepdims=True)
        df_ref[...] = jnp.concatenate([df, jnp.zeros((DA - 8, s), F32)], axis=0).T

    return pl.pallas_call(body, name=name, out_shape=[jax.ShapeDtypeStruct((s, DA), F32), jax.ShapeDtypeStruct((8, 1), F32)],
                          compiler_params=_cp())(ft, bias, dc)


def _pool_lane_consts(tm, i):
    lane = lax.broadcasted_iota(jnp.int32, (tm, POOL_W), 1)
    tok = lax.broadcasted_iota(jnp.int32, (tm, POOL_W), 0) + i * tm
    win = jnp.where(lane < 64, 2, jnp.where(lane < 128, 4, jnp.where(lane < 192, 8, 16)))
    cnt = jnp.minimum(tok + 1, win).astype(F32)
    return lane, tok, cnt


def _pick_window(lane, s2, s4, s8, s16):
    return jnp.where(lane < 64, s2, jnp.where(lane < 128, s4, jnp.where(lane < 192, s8, s16)))


def _pool_fwd(z, col_block, bd, scale, *, name, tm=512):
    s = z.shape[0]
    tm = min(tm, s)
    hb = tm // POOL_HALO

    def body(u_ref, halo_ref, bd_ref, sc_ref, y_ref, p_ref, buf):
        i = pl.program_id(0)
        buf[0:POOL_HALO, :] = halo_ref[...] * (i > 0).astype(F32)
        buf[POOL_HALO:, :] = u_ref[...]

        def back(k):
            return buf[POOL_HALO - k:POOL_HALO - k + tm, :]

        u = u_ref[...]
        s2 = u + back(1)
        s4 = s2 + back(2) + back(3)
        s8 = s4 + back(4) + back(5) + back(6) + back(7)
        s16 = s8
        for k in range(8, 16):
            s16 = s16 + back(k)
        lane, _, cnt = _pool_lane_consts(tm, i)
        pooled = (_pick_window(lane, s2, s4, s8, s16) / cnt - u).astype(BF16)
        p_ref[...] = pooled
        y_ref[...] = _dot(pooled, bd_ref[...], NN) * sc_ref[...]

    return pl.pallas_call(
        body, name=name, grid=(s // tm,),
        in_specs=[pl.BlockSpec((tm, POOL_W), lambda i: (i, col_block)),
                  pl.BlockSpec((POOL_HALO, POOL_W), lambda i: (jnp.maximum(i * hb - 1, 0), col_block)),
                  pl.BlockSpec((POOL_W, POOL_W), lambda i: (0, 0)), pl.BlockSpec((1, POOL_W), lambda i: (0, 0))],
        out_specs=[pl.BlockSpec((tm, POOL_W), lambda i: (i, 0)), pl.BlockSpec((tm, POOL_W), lambda i: (i, 0))],
        out_shape=[jax.ShapeDtypeStruct((s, POOL_W), F32), jax.ShapeDtypeStruct((s, POOL_W), BF16)],
        scratch_shapes=[pltpu.VMEM((tm + POOL_HALO, POOL_W), F32)],
        compiler_params=_cp(("parallel",)),
    )(z, z, bd, scale.reshape(1, POOL_W))


def _pool_bwd_a(dy, pooled, bd, scale, *, name, tm=512):
    s = dy.shape[0]
    tm = min(tm, s)

    def body(dy_ref, p_ref, bd_ref, sc_ref, dq_ref, dys_ref, dsc_ref):
        i = pl.program_id(0)
        dyv = dy_ref[...]
        y0 = _dot(p_ref[...], bd_ref[...], NN)
        dys = (dyv * sc_ref[...]).astype(BF16)
        dys_ref[...] = dys
        dp = _dot(dys, bd_ref[...], NT)
        _, _, cnt = _pool_lane_consts(tm, i)
        dq_ref[:, 0:POOL_W] = dp / cnt
        dq_ref[:, POOL_W:] = dp

        @pl.when(i == 0)
        def _():
            dsc_ref[...] = jnp.zeros_like(dsc_ref)

        dsc_ref[...] += jnp.sum(dyv * y0, axis=0, keepdims=True)

    row = lambda i: (i, 0)
    dq, dys, dsc = pl.pallas_call(
        body, name=name, grid=(s // tm,),
        in_specs=[pl.BlockSpec((tm, POOL_W), row), pl.BlockSpec((tm, POOL_W), row),
                  pl.BlockSpec((POOL_W, POOL_W), lambda i: (0, 0)), pl.BlockSpec((1, POOL_W), lambda i: (0, 0))],
        out_specs=[pl.BlockSpec((tm, 2 * POOL_W), row), pl.BlockSpec((tm, POOL_W), row), pl.BlockSpec((1, POOL_W), lambda i: (0, 0))],
        out_shape=[jax.ShapeDtypeStruct((s, 2 * POOL_W), F32), jax.ShapeDtypeStruct((s, POOL_W), BF16),
                   jax.ShapeDtypeStruct((1, POOL_W), F32)],
        compiler_params=_cp(("arbitrary",)),
    )(dy, pooled, bd, scale.reshape(1, POOL_W))
    return dq, dys, dsc.reshape(POOL_W)


def _pool_bwd_b(dq, *, name, tm=512):
    s = dq.shape[0]
    tm = min(tm, s)
    hb = tm // POOL_HALO
    nblk = s // tm

    def body(q_ref, dp_ref, halo_ref, du_ref, buf):
        i = pl.program_id(0)
        buf[0:tm, :] = q_ref[...]
        buf[tm:, :] = halo_ref[...] * (i < nblk - 1).astype(F32)

        def ahead(k):
            return buf[k:k + tm, :]

        q = q_ref[...]
        s2 = q + ahead(1)
        s4 = s2 + ahead(2) + ahead(3)
        s8 = s4 + ahead(4) + ahead(5) + ahead(6) + ahead(7)
        s16 = s8
        for k in range(8, 16):
            s16 = s16 + ahead(k)
        lane = lax.broadcasted_iota(jnp.int32, (tm, POOL_W), 1)
        du_ref[...] = _pick_window(lane, s2, s4, s8, s16) - dp_ref[...]

    return pl.pallas_call(
        body, name=name, grid=(nblk,),
        in_specs=[pl.BlockSpec((tm, POOL_W), lambda i: (i, 0)), pl.BlockSpec((tm, POOL_W), lambda i: (i, 1)),
                  pl.BlockSpec((POOL_HALO, POOL_W), lambda i: (jnp.minimum((i + 1) * hb, nblk * hb - 1), 0))],
        out_specs=pl.BlockSpec((tm, POOL_W), lambda i: (i, 0)),
        out_shape=jax.ShapeDtypeStruct((s, POOL_W), F32),
        scratch_shapes=[pltpu.VMEM((tm + POOL_HALO, POOL_W), F32)],
        compiler_params=_cp(("parallel",)),
    )(dq, dq, dq)


def _loss_head(x, gain, target, *, name, tm=512):
    s = x.shape[0]
    tm = min(tm, s)

    def body(x_ref, g_ref, t_ref, dx_ref, dg_ref, loss_ref):
        xv = x_ref[...]
        r = lax.rsqrt(jnp.mean(xv * xv, axis=-1, keepdims=True) + EPS)
        xh = xv * r
        err = xh * g_ref[...] - t_ref[...]
        dy = err * (1.0 / D)
        a = dy * g_ref[...]
        dx_ref[...] = r * a - xh * (r * jnp.mean(a * xh, axis=-1, keepdims=True))

        @pl.when(pl.program_id(0) == 0)
        def _():
            dg_ref[...] = jnp.zeros_like(dg_ref)
            loss_ref[...] = jnp.zeros_like(loss_ref)

        dg_ref[...] += jnp.sum(dy * xh, axis=0, keepdims=True)
        part = 0.5 * jnp.sum(jnp.mean(err * err, axis=-1, keepdims=True), axis=0, keepdims=True)
        loss_ref[...] += jnp.broadcast_to(part, loss_ref.shape)

    row = lambda i: (i, 0)
    dx, dg, loss = pl.pallas_call(
        body, name=name, grid=(s // tm,),
        in_specs=[pl.BlockSpec((tm, D), row), pl.BlockSpec((1, D), lambda i: (0, 0)), pl.BlockSpec((tm, D), row)],
        out_specs=[pl.BlockSpec((tm, D), row), pl.BlockSpec((1, D), lambda i: (0, 0)), pl.BlockSpec((1, 128), lambda i: (0, 0))],
        out_shape=[jax.ShapeDtypeStruct((s, D), F32), jax.ShapeDtypeStruct((1, D), F32), jax.ShapeDtypeStruct((1, 128), F32)],
        compiler_params=_cp(("arbitrary",)),
    )(x, gain.reshape(1, D), target)
    return dx, dg.reshape(D), loss[0, 0]


def _adamw(w, g, m, v, *, name, tr=512):
    rows, cols = w.shape
    tr = min(tr, rows)
    assert rows % tr == 0, (name, rows, tr)
    c_m = 1.0 - ADAM_B1
    c_v = 1.0 - ADAM_B2
    bc1 = 1.0 - ADAM_B1 ** ADAM_STEP
    bc2 = 1.0 - ADAM_B2 ** ADAM_STEP

    def body(w_ref, g_ref, m_ref, v_ref, d_ref, mo_ref, vo_ref):
        gv = g_ref[...]
        mn = ADAM_B1 * m_ref[...] + c_m * gv
        vn = ADAM_B2 * v_ref[...] + c_v * (gv * gv)
        mo_ref[...] = mn
        vo_ref[...] = vn
        d_ref[...] = -ADAM_LR * ((mn / bc1) / (jnp.sqrt(vn / bc2) + ADAM_EPS) + ADAM_WD * w_ref[...])

    spec = pl.BlockSpec((tr, cols), lambda i: (i, 0))
    return pl.pallas_call(body, name=name, grid=(rows // tr,), in_specs=[spec] * 4, out_specs=[spec] * 3,
                          out_shape=[jax.ShapeDtypeStruct((rows, cols), F32)] * 3,
                          compiler_params=_cp(("parallel",)))(w, g, m, v)


def _adamw_layer(w, g, m, v, layer, prev, *, name, tr):
    rows, cols = g.shape
    assert rows % tr == 0 and w.shape == (DEPTH * rows, cols), (name, w.shape, g.shape, tr)
    nblk = rows // tr
    c_m = 1.0 - ADAM_B1
    c_v = 1.0 - ADAM_B2
    bc1 = 1.0 - ADAM_B1 ** ADAM_STEP
    bc2 = 1.0 - ADAM_B2 ** ADAM_STEP
    n_prev = 0 if prev is None else 4

    def body(*refs):
        w_ref, g_ref, m_ref, v_ref = refs[:4]
        d_ref, mo_ref, vo_ref, go_ref = refs[4 + n_prev:]
        gv = g_ref[...]
        mn = ADAM_B1 * m_ref[...] + c_m * gv
        vn = ADAM_B2 * v_ref[...] + c_v * (gv * gv)
        mo_ref[...] = mn
        vo_ref[...] = vn
        go_ref[...] = gv
        d_ref[...] = -ADAM_LR * ((mn / bc1) / (jnp.sqrt(vn / bc2) + ADAM_EPS) + ADAM_WD * w_ref[...])

    stacked = pl.BlockSpec((tr, cols), lambda i: (layer * nblk + i, 0))
    args = [w, g, m, v] + ([] if prev is None else list(prev))
    return pl.pallas_call(
        body, name=name, grid=(nblk,),
        in_specs=[stacked, pl.BlockSpec((tr, cols), lambda i: (i, 0)), stacked, stacked] + [ANY_SPEC] * n_prev,
        out_specs=[stacked] * 4, out_shape=[jax.ShapeDtypeStruct(w.shape, F32)] * 4,
        input_output_aliases={4 + k: k for k in range(n_prev)},
        compiler_params=_cp(("parallel",)))(*args)


def _position():
    return jnp.stack([lax.axis_index("c"), 2 * lax.axis_index("x") + lax.axis_index("y")]).astype(jnp.int32)


SUM_ROW_TILES = 2


def _sum2_bf16(pos, fulls, sibs, *, name):
    n = len(fulls)
    nb = SUM_ROW_TILES

    def body(pos_ref, *refs):
        for t in range(n):
            refs[2 * n + t][...] = (refs[t][...] + refs[n + t][...]).astype(BF16)

    in_specs, sib_specs = [], []
    for sb in sibs:
        _, half, cols = sb.shape
        tr = half // nb
        assert half % nb == 0 and tr % 16 == 0, sb.shape
        in_specs.append(pl.BlockSpec((None, tr, cols), lambda j, i, p: (j, p[0] * nb + i, 0)))
        sib_specs.append(pl.BlockSpec((None, tr, cols), lambda j, i, p: (j, i, 0)))
    return pl.pallas_call(
        body, name=name,
        grid_spec=pltpu.PrefetchScalarGridSpec(num_scalar_prefetch=1, grid=(N_CHIPS, nb), in_specs=in_specs + sib_specs,
                                               out_specs=sib_specs),
        out_shape=[jax.ShapeDtypeStruct(sb.shape, BF16) for sb in sibs],
        compiler_params=_cp(("parallel", "parallel")))(pos, *fulls, *sibs)


def _sum5(pos, fulls, sibs, recvs, *, name):
    n = len(fulls)
    nb = SUM_ROW_TILES

    def body(pos_ref, *refs):
        for t in range(n):
            acc = refs[t][...] + refs[n + t][...]
            for kk in range(3):
                acc = acc + refs[2 * n + t][kk].astype(F32)
            refs[3 * n + t][...] = acc

    f_specs, s_specs, r_specs, o_specs = [], [], [], []
    for f in fulls:
        _, rows, cols = f.shape
        tr = rows // 2 // nb
        f_specs.append(pl.BlockSpec((None, tr, cols), lambda i, p: (p[1], p[0] * nb + i, 0)))
        s_specs.append(pl.BlockSpec((None, tr, cols), lambda i, p: (p[1], i, 0)))
        r_specs.append(pl.BlockSpec((3, tr, cols), lambda i, p: (0, i, 0)))
        o_specs.append(pl.BlockSpec((tr, cols), lambda i, p: (p[0] * nb + i, 0)))
    return pl.pallas_call(
        body, name=name,
        grid_spec=pltpu.PrefetchScalarGridSpec(num_scalar_prefetch=1, grid=(nb,), in_specs=f_specs + s_specs + r_specs,
                                               out_specs=o_specs),
        out_shape=[jax.ShapeDtypeStruct(f.shape[1:], F32) for f in fulls],
        compiler_params=_cp(("parallel",)))(pos, *fulls, *sibs, *recvs)


def _place():
    x, y, c = lax.axis_index("x"), lax.axis_index("y"), lax.axis_index("c")
    chips = [(1 - x, y), (x, 1 - y), (1 - x, 1 - y)]
    return x, y, c, 2 * x + y, chips


SEM_SPEC = pl.BlockSpec(memory_space=pltpu.SEMAPHORE)
ANY_SPEC = pl.BlockSpec(memory_space=pl.ANY)


def _gather_copies(ins, outs, send_i, recv_i, send_o, recv_o):
    x, y, c, me, chips = _place()
    n = len(ins)
    started, awaited = [], []
    for t in range(n):
        half = ins[t].shape[0] // 2
        mine = pl.ds(c * half, half)
        started.append(pltpu.make_async_remote_copy(
            src_ref=ins[t], dst_ref=outs[t].at[me], send_sem=send_o.at[t], recv_sem=recv_o.at[t],
            device_id=(x, y, 1 - c), device_id_type=MESH))
        awaited.append(started[-1])
        for kk, (px, py) in enumerate(chips):
            started.append(pltpu.make_async_remote_copy(
                src_ref=ins[t].at[mine], dst_ref=outs[t].at[me, mine], send_sem=send_i.at[t * 3 + kk],
                recv_sem=recv_i.at[t * 3 + kk], device_id=(px, py, c), device_id_type=MESH))
            awaited.append(pltpu.make_async_remote_copy(
                src_ref=ins[t].at[mine], dst_ref=outs[t].at[2 * px + py, mine], send_sem=send_i.at[t * 3 + kk],
                recv_sem=recv_i.at[t * 3 + kk], device_id=(px, py, c), device_id_type=MESH))
    return started, awaited


def _forward_copies(outs, send_d, recv_d):
    x, y, c, me, chips = _place()
    started, awaited = [], []
    for t in range(len(outs)):
        half = outs[t].shape[1] // 2
        for kk, (px, py) in enumerate(chips):
            for lst, hc in ((started, c), (awaited, 1 - c)):
                blk = outs[t].at[2 * px + py, pl.ds(hc * half, half)]
                lst.append(pltpu.make_async_remote_copy(src_ref=blk, dst_ref=blk, send_sem=send_d.at[t * 3 + kk],
                                                        recv_sem=recv_d.at[t * 3 + kk], device_id=(x, y, 1 - c), device_id_type=MESH))
    return started, awaited


def _gather_blocking(shards):
    n = len(shards)

    def body(*refs):
        ins, outs = refs[:n], refs[n:2 * n]
        send_i, recv_i, send_d, recv_d, send_o, recv_o = refs[2 * n:]
        started, awaited = _gather_copies(ins, outs, send_i, recv_i, send_o, recv_o)
        for cp in started:
            cp.start()
        for cp in awaited:
            cp.wait_recv()
        fwd, fwd_in = _forward_copies(outs, send_d, recv_d)
        for cp in fwd:
            cp.start()
        for cp in fwd_in:
            cp.wait_recv()
        for cp in started + fwd:
            cp.wait_send()

    return pl.pallas_call(
        body, name="gather_first", in_specs=[HBM_SPEC] * n, out_specs=[HBM_SPEC] * n,
        out_shape=[jax.ShapeDtypeStruct((N_CHIPS,) + s.shape, s.dtype) for s in shards],
        scratch_shapes=[pltpu.SemaphoreType.DMA((3 * n,)), pltpu.SemaphoreType.DMA((3 * n,)),
                        pltpu.SemaphoreType.DMA((3 * n,)), pltpu.SemaphoreType.DMA((3 * n,)),
                        pltpu.SemaphoreType.DMA((n,)), pltpu.SemaphoreType.DMA((n,))],
    )(*shards)


def _gather_start(shards, after):
    n = len(shards)

    def body(*refs):
        ins = refs[:n]
        send_i, recv_i, send_o, recv_o = refs[2 * n + 1:2 * n + 5]
        outs = refs[3 * n + 5:4 * n + 5]
        token = refs[4 * n + 5]
        started, _ = _gather_copies(ins, outs, send_i, recv_i, send_o, recv_o)
        for cp in started:
            cp.start()
        token[...] = jnp.zeros_like(token)

    lands = [lax.empty((N_CHIPS,) + s.shape, s.dtype) for s in shards]
    sems = [pltpu.SemaphoreType.DMA((3 * n,)), pltpu.SemaphoreType.DMA((3 * n,)), pltpu.SemaphoreType.DMA((n,)), pltpu.SemaphoreType.DMA((n,))]
    res = pl.pallas_call(
        body, name="gather_rest_start",
        in_specs=[HBM_SPEC] * (2 * n) + [ANY_SPEC],
        out_specs=[SEM_SPEC] * 4 + [HBM_SPEC] * (2 * n) + [pl.BlockSpec(memory_space=pltpu.VMEM)],
        out_shape=sems + [jax.ShapeDtypeStruct(s.shape, s.dtype) for s in shards]
        + [jax.ShapeDtypeStruct(a.shape, a.dtype) for a in lands] + [jax.ShapeDtypeStruct((8, 128), F32)],
        input_output_aliases={t: 4 + t for t in range(2 * n)},
        compiler_params=pltpu.CompilerParams(has_side_effects=pltpu.SideEffectType.DATAFLOW_SIDE_EFFECTING),
    )(*[pltpu.with_memory_space_constraint(s, pltpu.HBM) for s in shards],
      *[pltpu.with_memory_space_constraint(a, pltpu.HBM) for a in lands], after)
    return res[:4], res[4:4 + n], res[4 + n:4 + 2 * n], res[-1]


def _gather_wait(sems, shards_thru, lands_thru, after):
    n = len(shards_thru)

    def body(*refs):
        ins, outs_in = refs[:n], refs[n:2 * n]
        send_i, recv_i, send_o, recv_o = refs[2 * n:2 * n + 4]
        started, awaited = _gather_copies(ins, outs_in, send_i, recv_i, send_o, recv_o)
        for cp in started:
            cp.wait_send()
        for cp in awaited:
            cp.wait_recv()

    res = pl.pallas_call(
        body, name="gather_rest_wait",
        in_specs=[HBM_SPEC] * (2 * n) + [SEM_SPEC] * 4 + [ANY_SPEC],
        out_specs=[HBM_SPEC] * (2 * n),
        out_shape=[jax.ShapeDtypeStruct(a.shape, a.dtype) for a in list(shards_thru) + list(lands_thru)],
        input_output_aliases={t: t for t in range(2 * n)},
        compiler_params=pltpu.CompilerParams(has_side_effects=pltpu.SideEffectType.DATAFLOW_SIDE_EFFECTING),
    )(*shards_thru, *lands_thru, *sems, after)
    return res[n:]


def _gather_forward(lands):
    n = len(lands)

    def body(*refs):
        outs = refs[n:2 * n]
        send_d, recv_d = refs[2 * n:]
        fwd, fwd_in = _forward_copies(outs, send_d, recv_d)
        for cp in fwd:
            cp.start()
        for cp in fwd_in:
            cp.wait_recv()
        for cp in fwd:
            cp.wait_send()

    return pl.pallas_call(
        body, name="gather_rest_forward", in_specs=[HBM_SPEC] * n, out_specs=[HBM_SPEC] * n,
        out_shape=[jax.ShapeDtypeStruct(a.shape, a.dtype) for a in lands],
        input_output_aliases={t: t for t in range(n)},
        scratch_shapes=[pltpu.SemaphoreType.DMA((3 * n,)), pltpu.SemaphoreType.DMA((3 * n,))],
    )(*lands)


def _stage1_copies(ins, sib, send, recv):
    x, y, c, me, chips = _place()
    cps = []
    for t in range(len(ins)):
        rows = ins[t].shape[1] // 2
        cps.append(pltpu.make_async_remote_copy(
            src_ref=ins[t].at[:, pl.ds((1 - c) * rows, rows), :], dst_ref=sib[t], send_sem=send.at[t],
            recv_sem=recv.at[t], device_id=(x, y, 1 - c), device_id_type=MESH))
    return cps


def _reduce_stage1(grads, tag):
    n = len(grads)

    def body(*refs):
        cps = _stage1_copies(refs[:n], refs[n:2 * n], *refs[2 * n:])
        for cp in cps:
            cp.start()
        for cp in cps:
            cp.wait()

    return pl.pallas_call(
        body, name="reduce_stage1_" + tag, in_specs=[HBM_SPEC] * n, out_specs=[HBM_SPEC] * n,
        out_shape=[jax.ShapeDtypeStruct((N_CHIPS, g.shape[1] // 2, g.shape[2]), F32) for g in grads],
        scratch_shapes=[pltpu.SemaphoreType.DMA((n,)), pltpu.SemaphoreType.DMA((n,))],
    )(*grads)


def _split_start(copies_fn, srcs, land_shapes, n_sems, tag):
    n = len(srcs)

    def body(*refs):
        send, recv = refs[2 * n:2 * n + 2]
        for cp in copies_fn(refs[:n], refs[3 * n + 2:4 * n + 2], send, recv):
            cp.start()
        refs[4 * n + 2][...] = jnp.zeros_like(refs[4 * n + 2])

    lands = [lax.empty(shp, dt) for shp, dt in land_shapes]
    res = pl.pallas_call(
        body, name=tag,
        in_specs=[HBM_SPEC] * (2 * n),
        out_specs=[SEM_SPEC] * 2 + [HBM_SPEC] * (2 * n) + [pl.BlockSpec(memory_space=pltpu.VMEM)],
        out_shape=[pltpu.SemaphoreType.DMA((n_sems,)), pltpu.SemaphoreType.DMA((n_sems,))]
        + [jax.ShapeDtypeStruct(p.shape, p.dtype) for p in srcs]
        + [jax.ShapeDtypeStruct(a.shape, a.dtype) for a in lands] + [jax.ShapeDtypeStruct((8, 128), F32)],
        input_output_aliases={t: 2 + t for t in range(2 * n)},
        compiler_params=pltpu.CompilerParams(has_side_effects=pltpu.SideEffectType.DATAFLOW_SIDE_EFFECTING),
    )(*[pltpu.with_memory_space_constraint(p, pltpu.HBM) for p in srcs],
      *[pltpu.with_memory_space_constraint(a, pltpu.HBM) for a in lands])
    return res[:2], res[2:2 + n], res[2 + n:2 + 2 * n], res[-1]


def _split_wait(copies_fn, sems, srcs_thru, lands_thru, after, tag):
    n = len(srcs_thru)

    def body(*refs):
        for cp in copies_fn(refs[:n], refs[n:2 * n], refs[2 * n], refs[2 * n + 1]):
            cp.wait()

    res = pl.pallas_call(
        body, name=tag,
        in_specs=[HBM_SPEC] * (2 * n) + [SEM_SPEC] * 2 + [ANY_SPEC],
        out_specs=[HBM_SPEC] * (2 * n),
        out_shape=[jax.ShapeDtypeStruct(a.shape, a.dtype) for a in list(srcs_thru) + list(lands_thru)],
        input_output_aliases={t: t for t in range(2 * n)},
        compiler_params=pltpu.CompilerParams(has_side_effects=pltpu.SideEffectType.DATAFLOW_SIDE_EFFECTING),
    )(*srcs_thru, *lands_thru, *sems, after)
    return res[:n], res[n:]


def _stage2_copies(ps, rcv, send, recv):
    x, y, c, me, chips = _place()
    return [pltpu.make_async_remote_copy(
        src_ref=ps[t].at[2 * px + py], dst_ref=rcv[t].at[kk], send_sem=send.at[t * 3 + kk],
        recv_sem=recv.at[t * 3 + kk], device_id=(px, py, c), device_id_type=MESH)
        for t in range(len(ps)) for kk, (px, py) in enumerate(chips)]


def _reduce_stage3(reduced, tag):
    n = len(reduced)

    def body(*refs):
        outs = refs[n:2 * n]
        send, recv = refs[2 * n:]
        x, y, c, me, chips = _place()
        cps = []
        for t in range(n):
            rows = outs[t].shape[0] // 2
            mine = outs[t].at[pl.ds(c * rows, rows), :]
            cp = pltpu.make_async_remote_copy(src_ref=mine, dst_ref=mine, send_sem=send.at[t], recv_sem=recv.at[t],
                                              device_id=(x, y, 1 - c), device_id_type=MESH)
            cp.start()
            cps.append(cp)
        for cp in cps:
            cp.wait()

    return pl.pallas_call(
        body, name="reduce_stage3_" + tag, in_specs=[HBM_SPEC] * n, out_specs=[HBM_SPEC] * n,
        out_shape=[jax.ShapeDtypeStruct(r.shape, r.dtype) for r in reduced],
        input_output_aliases={t: t for t in range(n)},
        scratch_shapes=[pltpu.SemaphoreType.DMA((n,)), pltpu.SemaphoreType.DMA((n,))],
    )(*reduced)


def _allreduce_small(v):
    rows, cols = v.shape

    def body(v_ref, o_ref, buf, send, recv, loc):
        x, y, c, me, chips = _place()
        mine = 4 * x + 2 * y + c
        lc = pltpu.make_async_copy(v_ref, buf.at[mine], loc)
        lc.start()
        peers = []
        for fx in range(2):
            for fy in range(2):
                for fc in range(2):
                    if fx or fy or fc:
                        peers.append((fx, fy, fc))
        cps = []
        for kk, (fx, fy, fc) in enumerate(peers):
            to = (x ^ fx, y ^ fy, c ^ fc)
            cp = pltpu.make_async_remote_copy(src_ref=v_ref, dst_ref=buf.at[mine], send_sem=send.at[kk], recv_sem=recv.at[kk],
                                              device_id=to, device_id_type=MESH)
            cp.start()
            cps.append((cp, to))
        for kk, (cp, to) in enumerate(cps):
            src = 4 * to[0] + 2 * to[1] + to[2]
            pltpu.make_async_remote_copy(src_ref=v_ref, dst_ref=buf.at[src], send_sem=send.at[kk], recv_sem=recv.at[kk],
                                         device_id=to, device_id_type=MESH).wait_recv()
        for cp, _ in cps:
            cp.wait_send()
        lc.wait()
        acc = buf[0]
        for d in range(1, 8):
            acc = acc + buf[d]
        o_ref[...] = acc

    return pl.pallas_call(
        body, name="allreduce_small", in_specs=[pl.BlockSpec(memory_space=pltpu.VMEM)],
        out_specs=pl.BlockSpec(memory_space=pltpu.VMEM), out_shape=jax.ShapeDtypeStruct((rows, cols), F32),
        scratch_shapes=[pltpu.VMEM((8, rows, cols), F32), pltpu.SemaphoreType.DMA((7,)), pltpu.SemaphoreType.DMA((7,)),
                        pltpu.SemaphoreType.DMA],
        compiler_params=pltpu.CompilerParams(vmem_limit_bytes=VMEM_LIMIT_V7X),
    )(v)


def _pad_w_in(w):
    z = lambda n: jnp.zeros(w.shape[:-1] + (n,), w.dtype)
    return jnp.concatenate([w[..., 0:384], z(64), w[..., 384:416], z(32), w[..., 416:1824],
                            w[..., 1824:1830], z(58), w[..., 400:416], w[..., 384:400], z(32)], axis=-1)


def _unpad_w_in(g):
    x1 = g[..., 448:464] + g[..., Z_F + 80:Z_F + 96]
    x2 = g[..., 464:480] + g[..., Z_F + 64:Z_F + 80]
    return jnp.concatenate([g[..., 0:384], x1, x2, g[..., 512:1920], g[..., 1920:1926]], axis=-1)


def _block_diag(pw):
    out = jnp.zeros((POOL_W, POOL_W), pw.dtype)
    for g in range(4):
        out = out.at[g * 64:(g + 1) * 64, g * 64:(g + 1) * 64].set(pw[g])
    return out


def _rope_tables(s):
    inv_freq = ROPE_THETA ** (-jnp.arange(0, ROPE, 2, dtype=F32) / ROPE)
    ang = jnp.arange(s, dtype=jnp.int32).astype(F32)[:, None] * inv_freq[None, :]
    cos, sin = jnp.cos(ang), jnp.sin(ang)
    zero = lambda n: jnp.zeros((s, n), F32)
    ck = jnp.concatenate([zero(NOPE), cos, cos, zero(DA - NOPE - ROPE)], axis=1)
    sk = jnp.concatenate([zero(NOPE), -sin, sin, zero(DA - NOPE - ROPE)], axis=1)
    cq = jnp.concatenate([jnp.ones((s, NOPE), F32), cos, cos, zero(DA - NOPE - ROPE)], axis=1) * SCALE_MLA
    return dict(cq=cq, sq=sk * SCALE_MLA, ck=ck, sk=sk)


def _mix_fwd(l, x1, wts, sm, tabs):
    z, h2 = _norm_mm(x1, 0, sm["mix_norm"][l], wts["w_in"][l], name=f"mix_in_{l}")
    qa, qn = _mla_q_prep(z, sm["q_a_norm"][l], wts["wq_a"][l], wts["wq_b"][l], tabs["cq"], tabs["sq"], name=f"mla_q_{l}")
    ka, va, kvn = _mla_kv_prep(z, sm["kv_a_norm"][l], wts["wk"][l], wts["wv"][l], tabs["ck"], tabs["sk"], name=f"mla_kv_{l}")
    oa, lse_a = _attn_fwd(qa, ka, va, VDIM, name=f"mla_attn_{l}")

    bd = _block_diag(wts["pool_w"][l]).astype(BF16)
    yb, pooled = _pool_fwd(z, Z_POOL // POOL_W, bd, sm["pool_scale"][l], name=f"pool_{l}")

    fb = jnp.pad(sm["fox_b_f"][l], (0, 8 - H)).reshape(8, 1)
    ft, c3t = _gate_fwd(z, Z_F // DA, fb, name=f"fox_gate_{l}")
    fqa, fka, fva = _fox_prep(z, c3t, name=f"fox_prep_{l}")
    oc, lse_c = _attn_fwd(fqa, fka, fva, FOX_D, name=f"fox_attn_{l}")

    x2, cat = _mix_out(oa, yb, oc, wts["w_out"][l], x1, name=f"mix_out_{l}")
    saved = dict(z=z, h2=h2, qn=qn, kvn=kvn, qa=qa, ka=ka, va=va, oa=oa, lse_a=lse_a, bd=bd, pooled=pooled,
                 fqa=fqa, fka=fka, fva=fva, ft=ft, fb=fb, oc=oc, lse_c=lse_c, cat=cat)
    return x2, saved


def _mix_bwd(l, x1, dx2, sv, wts, sm, tabs, tok=None):
    s = x1.shape[0]
    g = {}
    dx2b = (dx2 if tok is None else dx2 + tok).astype(BF16)
    g["w_out"] = _mm(sv["cat"], dx2b, "tn", name=f"d_w_out_{l}", tm=1024, tn=1024, tk=DW_TOKENS)
    doa, doc, dyb, dl_a, dl_c = _mix_out_bwd(dx2b, wts["w_out"][l], sv["oa"], sv["oc"], name=f"mix_out_bwd_{l}")

    dfqa, dfka, dfva, dcq, dck = _attn_bwd(sv["fqa"], sv["fka"], sv["fva"], doc, sv["lse_c"], dl_c, True, name=f"fox_attn_bwd_{l}")
    dfox = _fox_bwd_prep(dfqa, dfka, dfva, name=f"fox_bwd_prep_{l}")
    dc = jnp.pad(dcq.reshape(H, s) + dck.reshape(H, s), ((0, 8 - H), (0, 0)))
    dft, dfb = _gate_bwd(sv["ft"], sv["fb"], dc, name=f"fox_gate_bwd_{l}")
    g["fox_b_f"] = dfb[:H, 0]

    dq, dys, g["pool_scale"] = _pool_bwd_a(dyb, sv["pooled"], sv["bd"], sm["pool_scale"][l], name=f"pool_bwd_a_{l}")
    du = _pool_bwd_b(dq, name=f"pool_bwd_b_{l}")
    dbd = _mm(sv["pooled"], dys, "tn", name=f"d_pool_w_{l}")
    g["pool_w"] = jnp.stack([dbd[i * 64:(i + 1) * 64, i * 64:(i + 1) * 64] for i in range(4)])

    dqa_, dka_, dva_ = _attn_bwd(sv["qa"], sv["ka"], sv["va"], doa, sv["lse_a"], dl_a, False, name=f"mla_attn_bwd_{l}")
    dqab, dkv, dz3, dz15 = _mla_bwd_prep(dqa_, dka_, dva_, dft, tabs["cq"], tabs["sq"], tabs["ck"], tabs["sk"],
                                         name=f"mla_bwd_prep_{l}")
    wq_ab = jnp.concatenate([wts["wq_a"][l], wts["wq_b"][l]], axis=1)
    wkv = jnp.concatenate([wts["wk"][l], wts["wv"][l]], axis=1)
    dwq = _mm(sv["qn"], dqab, "tn", name=f"d_w_q_b_{l}", tn=768, tk=DW_TOKENS).reshape(Q_RANK, 2, H, DA)
    dwkv = _mm(sv["kvn"], dkv, "tn", name=f"d_w_kv_b_{l}", tn=768, tk=DW_TOKENS).reshape(KV_RANK, 2, H, DA)
    da, db = dwq[:, 0], dwq[:, 1]
    swapped = jnp.concatenate([jnp.zeros((Q_RANK, H, NOPE), F32), db[..., NOPE + HALF_ROPE:NOPE + ROPE],
                               db[..., NOPE:NOPE + HALF_ROPE]], axis=-1)
    g["w_q_b"] = (da[..., :NOPE + ROPE] + swapped).reshape(Q_RANK, H * (NOPE + ROPE))
    g["w_kv_b"] = jnp.concatenate([dwkv[:, 0, :, :NOPE], dwkv[:, 1, :, :VDIM]], axis=-1).reshape(KV_RANK, H * (NOPE + VDIM))
    dqn = _mm(dqab, wq_ab, "nt", name=f"d_qn_{l}", tk=2 * H * DA)
    dkvn = _mm(dkv, wkv, "nt", name=f"d_kvn_{l}", tk=2 * H * DA)
    dqa, g["q_a_norm"] = _rmsnorm_bwd(sv["z"], Z_QA // Q_RANK, sm["q_a_norm"][l], dqn, name=f"q_a_norm_bwd_{l}")
    dkva, g["kv_a_norm"] = _rmsnorm_bwd(sv["z"], Z_KVA // KV_RANK, sm["kv_a_norm"][l], dkvn, name=f"kv_a_norm_bwd_{l}")

    dz = jnp.concatenate([dqa.astype(BF16), dkva.astype(BF16), dz3, du.astype(BF16), dfox, dz15], axis=1)
    g["w_in"] = _mm(sv["h2"], dz, "tn", name=f"d_w_in_{l}", tm=1024, tn=1024, tk=DW_TOKENS)
    dh2 = _mm(dz, wts["w_in"][l], "nt", name=f"d_h2_{l}", tn=1024, tk=NZ)
    dx1, g["mix_norm"] = _rmsnorm_bwd(x1, 0, sm["mix_norm"][l], dh2, dx2, name=f"mix_norm_bwd_{l}")
    return dx1, g


DW_TOKENS = 2048


def _local_step(x, target, wts, sm, late_weights=None, grads_ready=None):
    s = x.shape[0]
    tabs = _rope_tables(s)
    acts = []
    xs = x
    for l in range(DEPTH):
        x1, gu1 = _ffn_fwd(xs, sm["ffn1_norm"][l], wts["ffn1_w_gu"][l], wts["ffn1_w_d2"][l], name=f"ffn1_fwd_{l}")
        x2, sv = _mix_fwd(l, x1, wts, sm, tabs)
        if l == 0 and late_weights is not None:
            late_weights(x2)
        x3, gu2 = _ffn_fwd(x2, sm["ffn2_norm"][l], wts["ffn2_w_gu"][l], wts["ffn2_w_d2"][l], name=f"ffn2_fwd_{l}")
        acts.append((xs, gu1, x1, sv, x2, gu2))
        xs = x3
    dx, g_final, loss = _loss_head(xs, sm["final_norm"], target, name="loss_head")
    grads = [dict() for _ in range(DEPTH)]
    for l in reversed(range(DEPTH)):
        x0, gu1, x1, sv, x2, gu2 = acts[l]
        g = grads[l]
        dx, dgu, act, hh, dy, g["ffn2_norm"] = _ffn_bwd(x2, dx, gu2, sm["ffn2_norm"][l], wts["ffn2_w_gu"][l], wts["ffn2_w_d2"][l],
                                                        name=f"ffn2_bwd_{l}")
        g["ffn2_w_down"] = _mm(act, dy, "tn", name=f"d_ffn2_w_down_{l}", tm=FF_SHARD, tn=1024, tk=DW_TOKENS)
        g["ffn2_w_gu"] = _mm(hh, dgu, "tn", name=f"d_ffn2_w_gu_{l}", tm=1024, tn=FF_SHARD, tk=DW_TOKENS, n_major_out=True)
        tok = None
        if grads_ready is not None:
            sm, tok = grads_ready(l, "ffn2", g, sm)
        dx, gm = _mix_bwd(l, x1, dx, sv, wts, sm, tabs, tok)
        g.update(gm)
        if grads_ready is not None:
            sm, _ = grads_ready(l, "mix", g, sm)
        dx, dgu, act, hh, dy, g["ffn1_norm"] = _ffn_bwd(x0, dx, gu1, sm["ffn1_norm"][l], wts["ffn1_w_gu"][l], wts["ffn1_w_d2"][l],
                                                        name=f"ffn1_bwd_{l}")
        g["ffn1_w_down"] = _mm(act, dy, "tn", name=f"d_ffn1_w_down_{l}", tm=FF_SHARD, tn=1024, tk=DW_TOKENS)
        g["ffn1_w_gu"] = _mm(hh, dgu, "tn", name=f"d_ffn1_w_gu_{l}", tm=1024, tn=FF_SHARD, tk=DW_TOKENS, n_major_out=True)
        if grads_ready is not None:
            sm, _ = grads_ready(l, "ffn1", g, sm)
    return loss, dx, grads, g_final


BIG = ["ffn1_w_gu", "ffn1_w_down", "w_in", "w_q_b", "w_kv_b", "w_out", "ffn2_w_gu", "ffn2_w_down"]
SMALL = ["ffn1_norm", "mix_norm", "q_a_norm", "kv_a_norm", "pool_w", "pool_scale", "fox_b_f", "ffn2_norm"]
SMALL_ROWS = 48


WEIGHT_VIEWS = ["ffn1_w_gu", "ffn1_w_d2", "w_in", "wq_a", "wq_b", "wk", "wv", "w_out", "ffn2_w_gu", "ffn2_w_d2"]


def _prepare_weights(gathered, wts):
    for (nm, l), w in gathered.items():
        if nm in ("ffn1_w_gu", "ffn2_w_gu"):
            wts[nm][l] = w
        elif nm in ("ffn1_w_down", "ffn2_w_down"):
            wts[nm[:5] + "w_d2"][l] = w.reshape(2, FF_SHARD, D)
        elif nm in ("w_in", "w_out"):
            wts[nm][l] = w.reshape(D, -1)
        elif nm == "w_q_b":
            wq = jnp.moveaxis(w, 0, 1).reshape(Q_RANK, H, NOPE + ROPE)
            zq = lambda n: jnp.zeros((Q_RANK, H, n), BF16)
            wts["wq_a"][l] = jnp.concatenate([wq, zq(DA - NOPE - ROPE)], axis=-1).reshape(Q_RANK, H * DA)
            wts["wq_b"][l] = jnp.concatenate([zq(NOPE), wq[..., NOPE + HALF_ROPE:], wq[..., NOPE:NOPE + HALF_ROPE],
                                              zq(DA - NOPE - ROPE)], axis=-1).reshape(Q_RANK, H * DA)
        else:
            wkv = jnp.moveaxis(w, 0, 1).reshape(KV_RANK, H, NOPE + VDIM)
            zk = jnp.zeros((KV_RANK, H, DA - NOPE), BF16)
            wts["wk"][l] = jnp.concatenate([wkv[..., :NOPE], zk], axis=-1).reshape(KV_RANK, H * DA)
            wts["wv"][l] = jnp.concatenate([wkv[..., NOPE:], zk], axis=-1).reshape(KV_RANK, H * DA)


def _chip_major(name, g):
    if name in ("ffn1_w_gu", "ffn2_w_gu"):
        return g
    if name in ("ffn1_w_down", "ffn2_w_down", "w_in", "w_out"):
        return g.reshape(N_CHIPS, g.shape[0] // N_CHIPS, g.shape[1])
    return jnp.moveaxis(g.reshape(g.shape[0], N_CHIPS, g.shape[1] // N_CHIPS), 1, 0)


def _pack_small(grads, g_final, loss):
    parts = []
    for l in range(DEPTH):
        for nm in SMALL:
            parts.append(grads[l][nm].reshape(-1))
    parts.append(g_final.reshape(-1))
    parts.append(loss.reshape(1))
    flat = jnp.concatenate(parts)
    return jnp.pad(flat, (0, SMALL_ROWS * D - flat.shape[0])).reshape(SMALL_ROWS, D)


def _unpack_small(packed, params):
    flat = packed.reshape(-1)
    out = {nm: [] for nm in SMALL}
    off = 0
    for l in range(DEPTH):
        for nm in SMALL:
            shp = params[nm].shape[1:]
            n = int(np.prod(shp))
            out[nm].append(flat[off:off + n].reshape(shp))
            off += n
    res = {nm: jnp.stack(v) for nm, v in out.items()}
    res["final_norm"] = flat[off:off + D]
    return res, flat[off + D]


def _update(name, w, g, m, v):
    shp = w.shape
    if w.ndim == 1:
        view = (1, shp[0])
    elif w.size <= 65536:
        view = (shp[0], w.size // shp[0])
    else:
        view = (w.size // shp[-1], shp[-1])
    tr = view[0]
    for cand in (512, 352, 256, 128):
        if view[0] % cand == 0 and view[0] > cand:
            tr = cand
            break
    d, mn, vn = _adamw(w.reshape(view), g.reshape(view), m.reshape(view), v.reshape(view), name="adamw_" + name, tr=tr)
    return d.reshape(shp), mn.reshape(shp), vn.reshape(shp)


WEIGHTS = ['ffn1_norm', 'ffn1_w_gu', 'ffn1_w_down', 'mix_norm', 'w_in', 'q_a_norm', 'w_q_b', 'kv_a_norm', 'w_kv_b', 'pool_w',
           'pool_scale', 'fox_b_f', 'w_out', 'ffn2_norm', 'ffn2_w_gu', 'ffn2_w_down', 'final_norm']


def kernel(x, ffn1_norm, ffn1_w_gu, ffn1_w_down, mix_norm, w_in, q_a_norm, w_q_b, kv_a_norm, w_kv_b, pool_w, pool_scale, fox_b_f, w_out, ffn2_norm, ffn2_w_gu, ffn2_w_down, final_norm, loss_target, m_ffn1_norm, m_ffn1_w_gu, m_ffn1_w_down, m_mix_norm, m_w_in, m_q_a_norm, m_w_q_b, m_kv_a_norm, m_w_kv_b, m_pool_w, m_pool_scale, m_fox_b_f, m_w_out, m_ffn2_norm, m_ffn2_w_gu, m_ffn2_w_down, m_final_norm, v_ffn1_norm, v_ffn1_w_gu, v_ffn1_w_down, v_mix_norm, v_w_in, v_q_a_norm, v_w_q_b, v_kv_a_norm, v_w_kv_b, v_pool_w, v_pool_scale, v_fox_b_f, v_w_out, v_ffn2_norm, v_ffn2_w_gu, v_ffn2_w_down, v_final_norm):
    params = dict(ffn1_norm=ffn1_norm, ffn1_w_gu=ffn1_w_gu, ffn1_w_down=ffn1_w_down, mix_norm=mix_norm, w_in=w_in, q_a_norm=q_a_norm,
                  w_q_b=w_q_b, kv_a_norm=kv_a_norm, w_kv_b=w_kv_b, pool_w=pool_w, pool_scale=pool_scale, fox_b_f=fox_b_f, w_out=w_out,
                  ffn2_norm=ffn2_norm, ffn2_w_gu=ffn2_w_gu, ffn2_w_down=ffn2_w_down, final_norm=final_norm)
    mom = dict(ffn1_norm=m_ffn1_norm, ffn1_w_gu=m_ffn1_w_gu, ffn1_w_down=m_ffn1_w_down, mix_norm=m_mix_norm, w_in=m_w_in,
               q_a_norm=m_q_a_norm, w_q_b=m_w_q_b, kv_a_norm=m_kv_a_norm, w_kv_b=m_w_kv_b, pool_w=m_pool_w, pool_scale=m_pool_scale,
               fox_b_f=m_fox_b_f, w_out=m_w_out, ffn2_norm=m_ffn2_norm, ffn2_w_gu=m_ffn2_w_gu, ffn2_w_down=m_ffn2_w_down,
               final_norm=m_final_norm)
    var = dict(ffn1_norm=v_ffn1_norm, ffn1_w_gu=v_ffn1_w_gu, ffn1_w_down=v_ffn1_w_down, mix_norm=v_mix_norm, w_in=v_w_in,
               q_a_norm=v_q_a_norm, w_q_b=v_w_q_b, kv_a_norm=v_kv_a_norm, w_kv_b=v_w_kv_b, pool_w=v_pool_w, pool_scale=v_pool_scale,
               fox_b_f=v_fox_b_f, w_out=v_w_out, ffn2_norm=v_ffn2_norm, ffn2_w_gu=v_ffn2_w_gu, ffn2_w_down=v_ffn2_w_down,
               final_norm=v_final_norm)

    shard = {}
    for nm in BIG:
        w = _pad_w_in(params[nm]) if nm == "w_in" else params[nm]
        for l in range(DEPTH):
            shard[(nm, l)] = w[l].astype(BF16)
    first = [(nm, 0) for nm in BIG if not nm.startswith("ffn2")]
    rest = [k for k in shard if k not in first]
    wts = {nm: [None] * DEPTH for nm in WEIGHT_VIEWS}
    wts["pool_w"] = params["pool_w"]
    got = _gather_blocking([shard[k] for k in first])
    _prepare_weights(dict(zip(first, got)), wts)
    sems, src_thru, land_thru, token = _gather_start([shard[k] for k in rest], got[0])
    sm = dict(params)
    sm["ffn1_norm"] = params["ffn1_norm"] + token[0, 0]

    def late_weights(x2):
        lands = _gather_forward(_gather_wait(sems, src_thru, land_thru, x2))
        _prepare_weights(dict(zip(rest, lands)), wts)

    pos = _position()
    flight = {}

    groups = {"l1": (1, BIG), "l0a": (0, [nm for nm in BIG if not nm.startswith("ffn1")]),
              "l0b": (0, [nm for nm in BIG if nm.startswith("ffn1")])}
    pending = {}

    def to_chips(key, full, sib):
        psum = _sum2_bf16(pos, full, sib, name=f"chip_sum_{key}")
        s2 = _split_start(_stage2_copies, psum, [((3,) + p.shape[1:], p.dtype) for p in psum], 3 * len(psum),
                          f"reduce_stage2_start_{key}")
        flight[key] = (full, sib, s2)
        return s2[3][0, 0]

    def grads_ready(l, stage, g, sm_now):
        behind, tok = None, None
        if (l, stage) == (1, "ffn1"):
            full = [_chip_major(nm, g[nm]) for nm in BIG]
            pending["l1"] = _split_start(_stage1_copies, full, [((N_CHIPS, f.shape[1] // 2, f.shape[2]), F32) for f in full],
                                         len(full), "reduce_stage1_start_l1")
            behind, tok = "ffn2_norm", pending["l1"][3][0, 0]
        elif (l, stage) == (0, "ffn2"):
            sems1, full_thru, sib_land, _ = pending["l1"]
            full, sib = _split_wait(_stage1_copies, sems1, full_thru, sib_land, g["ffn2_w_down"], "reduce_stage1_wait_l1")
            tok = to_chips("l1", full, sib)
        elif l == 0:
            key = "l0a" if stage == "mix" else "l0b"
            full = [_chip_major(nm, g[nm]) for nm in groups[key][1]]
            behind, tok = "ffn1_norm", to_chips(key, full, _reduce_stage1(full, key))
        if behind is None:
            return sm_now, tok
        sm_next = dict(sm_now)
        sm_next[behind] = sm_now[behind] + tok
        return sm_next, tok

    loss, dx, grads, g_final = _local_step(x[0], loss_target[0], wts, sm, late_weights, grads_ready)

    def view2d(a):
        return a.reshape(a.size // a.shape[-1], a.shape[-1])

    after = flight["l0b"][2][3]
    done = {nm: None for nm in BIG}
    for key in ("l1", "l0a", "l0b"):
        l, names = groups[key]
        full, sib, (sems2, ps_thru, lands2, _) = flight[key]
        _, recv = _split_wait(_stage2_copies, sems2, ps_thru, lands2, after, f"reduce_stage2_wait_{key}")
        whole = _reduce_stage3(_sum5(pos, full, sib, recv, name=f"grad_sum_{key}"), key)
        for nm, g_l in zip(names, whole):
            if nm == "w_in":
                g_l = _unpad_w_in(g_l)
            tr = max(t for t in (512, 352, 256, 128) if g_l.shape[0] % t == 0)
            done[nm] = _adamw_layer(view2d(params[nm]), g_l, view2d(mom[nm]), view2d(var[nm]), l, done[nm],
                                    name=f"adamw_{nm}_{l}", tr=tr)
        after = done[names[-1]][0][-8:, 0:128]
        if key == "l1":
            small_g, loss = _unpack_small(_allreduce_small(_pack_small(grads, g_final, loss)), params)
            after = after + small_g["final_norm"][0]
    gw, delta, new_m, new_v = dict(small_g), {}, {}, {}
    for nm in BIG:
        delta[nm], new_m[nm], new_v[nm], gw[nm] = [a.reshape(params[nm].shape) for a in done[nm]]
    for nm in small_g:
        delta[nm], new_m[nm], new_v[nm] = _update(nm, params[nm], gw[nm], mom[nm], var[nm])
    return (loss, dx[None], *[gw[n] for n in WEIGHTS], *[delta[n] for n in WEIGHTS], *[new_m[n] for n in WEIGHTS],
            *[new_v[n] for n in WEIGHTS])
```

```python
import functools
import math

import jax
import jax.numpy as jnp
import numpy as np
from jax import lax
from jax.experimental import pallas as pl
from jax.experimental.pallas import tpu as pltpu

F32 = jnp.float32
BF16 = jnp.bfloat16
MESH = pl.DeviceIdType.MESH
HBM_SPEC = pl.BlockSpec(memory_space=pltpu.HBM)

D = 1024
DEPTH = 2
D_FF = 2816
FF_SHARD = 1408
N_CHIPS = 4
H = 6
NOPE, ROPE, VDIM = 64, 32, 64
HALF_ROPE = ROPE // 2
Q_RANK, KV_RANK = 256, 128
POOL_W = 256
FOX_D = 64
N_IN = 1830
NZ = 2048
ROPE_THETA = 10000.0
EPS = 1e-6
POOL_HALO = 16
Z_QA, Z_KVA, Z_KR, Z_POOL, Z_FOX, Z_F = 0, 256, 384, 512, 768, 1920

ADAM_LR, ADAM_B1, ADAM_B2, ADAM_EPS, ADAM_WD, ADAM_STEP = 0.001, 0.9, 0.999, 1e-08, 0.01, 10

VMEM_LIMIT_V7X = 56 * 1024 * 1024


def _cp(sem=None, vmem=VMEM_LIMIT_V7X):
    return pltpu.CompilerParams(dimension_semantics=sem, vmem_limit_bytes=vmem)


def _sigmoid(x):
    return 0.5 * jnp.tanh(0.5 * x) + 0.5


def _dot(a, b, dims):
    return lax.dot_general(a, b, (dims, ((), ())), preferred_element_type=F32)


NN = ((1,), (0,))
NT = ((1,), (1,))
TN = ((0,), (0,))


def _mm(a, b, mode, *, name, out_dtype=F32, add=None, alpha=None, tm=512, tn=512, tk=512, n_major_out=False):
    if mode == "nn":
        (m, k), (k2, n) = a.shape, b.shape
    elif mode == "nt":
        (m, k), (n, k2) = a.shape, b.shape
    else:
        (k, m), (k2, n) = a.shape, b.shape
    assert k == k2
    tm, tn, tk = min(tm, m), min(tn, n), min(tk, k)
    assert m % tm == 0 and n % tn == 0 and k % tk == 0, (name, m, n, k, tm, tn, tk)
    nk = k // tk
    dims = {"nn": NN, "nt": NT, "tn": TN}[mode]
    a_spec = pl.BlockSpec((tk, tm), lambda i, j, kk: (kk, i)) if mode == "tn" else pl.BlockSpec((tm, tk), lambda i, j, kk: (i, kk))
    b_spec = pl.BlockSpec((tn, tk), lambda i, j, kk: (j, kk)) if mode == "nt" else pl.BlockSpec((tk, tn), lambda i, j, kk: (kk, j))
    in_specs = [a_spec, b_spec]
    args = [a, b]
    if add is not None:
        in_specs.append(pl.BlockSpec((tm, tn), lambda i, j, kk: (i, j)))
        args.append(add)
    if n_major_out:
        out_shape = jax.ShapeDtypeStruct((n // tn, m, tn), out_dtype)
        out_spec = pl.BlockSpec((None, tm, tn), lambda i, j, kk: (j, i, 0))
    else:
        out_shape = jax.ShapeDtypeStruct((m, n), out_dtype)
        out_spec = pl.BlockSpec((tm, tn), lambda i, j, kk: (i, j))

    def body(*refs):
        a_ref, b_ref = refs[0], refs[1]
        add_ref = refs[2] if add is not None else None
        o_ref, acc = refs[-2], refs[-1]
        kk = pl.program_id(2)

        @pl.when(kk == 0)
        def _():
            acc[...] = jnp.zeros_like(acc)

        acc[...] += _dot(a_ref[...].astype(BF16), b_ref[...].astype(BF16), dims)

        @pl.when(kk == nk - 1)
        def _():
            r = acc[...]
            if alpha is not None:
                r = r * alpha
            if add_ref is not None:
                r = r + add_ref[...].astype(F32)
            o_ref[...] = r.astype(out_dtype)

    return pl.pallas_call(
        body, name=name, grid=(m // tm, n // tn, nk), in_specs=in_specs, out_specs=out_spec, out_shape=out_shape,
        scratch_shapes=[pltpu.VMEM((tm, tn), F32)],
        compiler_params=_cp(("parallel", "parallel", "arbitrary")),
    )(*args)


def _norm_mm(x, col_block, gain, w, *, name, tm=512):
    s = x.shape[0]
    k, n = w.shape
    tm = min(tm, s)

    def body(x_ref, g_ref, w_ref, z_ref, h_ref):
        xv = x_ref[...]
        r = lax.rsqrt(jnp.mean(xv * xv, axis=-1, keepdims=True) + EPS)
        hv = (xv * r * g_ref[...]).astype(BF16)
        h_ref[...] = hv
        z_ref[...] = _dot(hv, w_ref[...], NN)

    return pl.pallas_call(
        body, name=name, grid=(s // tm,),
        in_specs=[pl.BlockSpec((tm, k), lambda i: (i, col_block)), pl.BlockSpec((1, k), lambda i: (0, 0)),
                  pl.BlockSpec((k, n), lambda i: (0, 0))],
        out_specs=[pl.BlockSpec((tm, n), lambda i: (i, 0)), pl.BlockSpec((tm, k), lambda i: (i, 0))],
        out_shape=[jax.ShapeDtypeStruct((s, n), F32), jax.ShapeDtypeStruct((s, k), BF16)],
        compiler_params=_cp(("parallel",)),
    )(x, gain.reshape(1, k), w)


def _rmsnorm_bwd(x, col_block, gain, dh, dres=None, *, name, tm=512):
    s = x.shape[0]
    k = gain.shape[-1]
    tm = min(tm, s)

    def body(*refs):
        x_ref, g_ref, dh_ref = refs[0], refs[1], refs[2]
        dres_ref = refs[3] if dres is not None else None
        dx_ref, dg_ref = refs[-2], refs[-1]
        xv = x_ref[...]
        r = lax.rsqrt(jnp.mean(xv * xv, axis=-1, keepdims=True) + EPS)
        dhv = dh_ref[...].astype(F32)
        a = dhv * g_ref[...]
        dx = r * a - xv * (r * r * r) * jnp.mean(a * xv, axis=-1, keepdims=True)
        if dres_ref is not None:
            dx = dx + dres_ref[...]
        dx_ref[...] = dx

        @pl.when(pl.program_id(0) == 0)
        def _():
            dg_ref[...] = jnp.zeros_like(dg_ref)

        dg_ref[...] += jnp.sum(dhv * xv * r, axis=0, keepdims=True)

    in_specs = [pl.BlockSpec((tm, k), lambda i: (i, col_block)), pl.BlockSpec((1, k), lambda i: (0, 0)),
                pl.BlockSpec((tm, k), lambda i: (i, 0))]
    args = [x, gain.reshape(1, k), dh]
    if dres is not None:
        in_specs.append(pl.BlockSpec((tm, k), lambda i: (i, 0)))
        args.append(dres)
    dx, dg = pl.pallas_call(
        body, name=name, grid=(s // tm,), in_specs=in_specs,
        out_specs=[pl.BlockSpec((tm, k), lambda i: (i, 0)), pl.BlockSpec((1, k), lambda i: (0, 0))],
        out_shape=[jax.ShapeDtypeStruct((s, k), F32), jax.ShapeDtypeStruct((1, k), F32)],
        compiler_params=_cp(("arbitrary",)),
    )(*args)
    return dx, dg.reshape(k)


def _ffn_fwd(x, gain, w_gu4, w_d2, *, name, tm=256):
    s = x.shape[0]
    tm = min(tm, s)

    def body(x_ref, g_ref, wgu_ref, wd_ref, xo_ref, dgu_ref, act_ref):
        xv = x_ref[...]
        r = lax.rsqrt(jnp.mean(xv * xv, axis=-1, keepdims=True) + EPS)
        hv = (xv * r * g_ref[...]).astype(BF16)
        y = jnp.zeros((tm, D), F32)
        for j in range(2):
            g = _dot(hv, wgu_ref[j], NN)
            u = _dot(hv, wgu_ref[2 + j], NN)
            sg = _sigmoid(g)
            silu = g * sg
            dgu_ref[:, j * FF_SHARD:(j + 1) * FF_SHARD] = (u * (sg * (1.0 + g * (1.0 - sg)))).astype(BF16)
            dgu_ref[:, D_FF + j * FF_SHARD:D_FF + (j + 1) * FF_SHARD] = silu.astype(BF16)
            act = (silu * u).astype(BF16)
            act_ref[:, j * FF_SHARD:(j + 1) * FF_SHARD] = act
            y = y + _dot(act, wd_ref[j], NN)
        xo_ref[...] = xv + 0.5 * y

    row = lambda i: (i, 0)
    return pl.pallas_call(
        body, name=name, grid=(s // tm,),
        in_specs=[pl.BlockSpec((tm, D), row), pl.BlockSpec((1, D), lambda i: (0, 0)),
                  pl.BlockSpec((N_CHIPS, D, FF_SHARD), lambda i: (0, 0, 0), pipeline_mode=pl.Buffered(1)),
                  pl.BlockSpec((2, FF_SHARD, D), lambda i: (0, 0, 0), pipeline_mode=pl.Buffered(1))],
        out_specs=[pl.BlockSpec((tm, D), row), pl.BlockSpec((tm, 2 * D_FF), row), pl.BlockSpec((tm, D_FF), row)],
        out_shape=[jax.ShapeDtypeStruct((s, D), F32), jax.ShapeDtypeStruct((s, 2 * D_FF), BF16),
                   jax.ShapeDtypeStruct((s, D_FF), BF16)],
        compiler_params=_cp(("parallel",)),
    )(x, gain.reshape(1, D), w_gu4, w_d2)


FFN_ROW_CHUNK = 32


def _ffn_bwd(x, dxo, dloc, gain, w_gu4, w_d2, *, name, tm=256):
    s = x.shape[0]
    tm = min(tm, s)

    def body(x_ref, dxo_ref, dloc_ref, g_ref, wgu_ref, wd_ref, dx_ref, dgu_ref, h_ref, dy_ref, dg_ref):
        xv = x_ref[...]
        r = lax.rsqrt(jnp.mean(xv * xv, axis=-1, keepdims=True) + EPS)
        xh = xv * r
        h_ref[...] = (xh * g_ref[...]).astype(BF16)
        dxov = dxo_ref[...]
        dy = (0.5 * dxov).astype(BF16)
        dy_ref[...] = dy
        gcols = [slice(j * FF_SHARD, (j + 1) * FF_SHARD) for j in range(2)]
        ucols = [slice(D_FF + j * FF_SHARD, D_FF + (j + 1) * FF_SHARD) for j in range(2)]
        dacts = [_dot(dy, wd_ref[j], NT) for j in range(2)]
        for r0 in range(0, tm, FFN_ROW_CHUNK):
            rows = slice(r0, r0 + FFN_ROW_CHUNK)
            for j in range(2):
                da = dacts[j][rows]
                dgu_ref[rows, gcols[j]] = (da * dloc_ref[rows, gcols[j]].astype(F32)).astype(BF16)
                dgu_ref[rows, ucols[j]] = (da * dloc_ref[rows, ucols[j]].astype(F32)).astype(BF16)
        dh = jnp.zeros((tm, D), F32)
        for j in range(2):
            dh = dh + _dot(dgu_ref[:, gcols[j]], wgu_ref[j], NT) + _dot(dgu_ref[:, ucols[j]], wgu_ref[2 + j], NT)
        a = dh * g_ref[...]
        dx_ref[...] = dxov + r * a - xh * (r * jnp.mean(a * xh, axis=-1, keepdims=True))

        @pl.when(pl.program_id(0) == 0)
        def _():
            dg_ref[...] = jnp.zeros_like(dg_ref)

        dg_ref[...] += jnp.sum(dh * xh, axis=0, keepdims=True)

    row = lambda i: (i, 0)
    outs = pl.pallas_call(
        body, name=name, grid=(s // tm,),
        in_specs=[pl.BlockSpec((tm, D), row), pl.BlockSpec((tm, D), row), pl.BlockSpec((tm, 2 * D_FF), row),
                  pl.BlockSpec((1, D), lambda i: (0, 0)),
                  pl.BlockSpec((N_CHIPS, D, FF_SHARD), lambda i: (0, 0, 0), pipeline_mode=pl.Buffered(1)),
                  pl.BlockSpec((2, FF_SHARD, D), lambda i: (0, 0, 0), pipeline_mode=pl.Buffered(1))],
        out_specs=[pl.BlockSpec((tm, D), row), pl.BlockSpec((tm, 2 * D_FF), row),
                   pl.BlockSpec((tm, D), row), pl.BlockSpec((tm, D), row), pl.BlockSpec((1, D), lambda i: (0, 0))],
        out_shape=[jax.ShapeDtypeStruct((s, D), F32), jax.ShapeDtypeStruct((s, 2 * D_FF), BF16),
                   jax.ShapeDtypeStruct((s, D), BF16), jax.ShapeDtypeStruct((s, D), BF16), jax.ShapeDtypeStruct((1, D), F32)],
        compiler_params=_cp(("arbitrary",)),
    )(x, dxo, dloc, gain.reshape(1, D), w_gu4, w_d2)
    dx, dgu, h, dy, dg = outs
    return dx, dgu, h, dy, dg.reshape(D)


DA = 128
SCALE_MLA = 1.0 / math.sqrt(NOPE + ROPE)
SCALE_FOX = 1.0 / math.sqrt(FOX_D)


def _causal_blocks(nb, key_major):
    if key_major:
        pairs = [(i, j) for j in range(nb) for i in range(j, nb)]
    else:
        pairs = [(i, j) for i in range(nb) for j in range(i + 1)]
    return (jnp.asarray(np.array([p[0] for p in pairs], np.int32)), jnp.asarray(np.array([p[1] for p in pairs], np.int32)))


HEADS_PER_STEP = 2
ROW_CHUNK = 64

def _col_to_row(col):
    return jnp.broadcast_to(col, (col.shape[0], DA)).T[0:1, :]


def _attn_fwd(qa, ka, va, dv, *, name, t=512):
    h, s, _ = qa.shape
    t = min(t, s)
    nb = s // t
    g = 3
    qi, kj = _causal_blocks(nb, key_major=False)

    rc = min(ROW_CHUNK, t)

    def body(qi_ref, kj_ref, q_ref, k_ref, v_ref, o_ref, lse_ref, m_sc, acc_sc, p_sc, a_sc):
        n = pl.program_id(1)
        i, j = qi_ref[n], kj_ref[n]

        @pl.when(j == 0)
        def _():
            m_sc[...] = jnp.full_like(m_sc, -jnp.inf)
            acc_sc[...] = jnp.zeros_like(acc_sc)

        def step(masked):
            scs = [_dot(q_ref[hh], k_ref[hh], NT) for hh in range(g)]
            for r0 in range(0, t, rc):
                rows = slice(r0, r0 + rc)
                for hh in range(g):
                    sr = scs[hh][rows]
                    if masked:
                        row = lax.broadcasted_iota(jnp.int32, (rc, t), 0) + r0
                        col = lax.broadcasted_iota(jnp.int32, (rc, t), 1)
                        sr = jnp.where(col <= row, sr, -jnp.inf)
                    tiles = [sr[:, c0:c0 + DA] for c0 in range(0, t, DA)]
                    top = tiles[0]
                    for tile in tiles[1:]:
                        top = jnp.maximum(top, tile)
                    m_old = m_sc[hh, rows]
                    m_new = jnp.maximum(m_old, jnp.max(top, axis=-1, keepdims=True))
                    for c0, tile in zip(range(0, t, DA), tiles):
                        p_sc[hh, rows, c0:c0 + DA] = jnp.exp(tile - m_new).astype(BF16)
                    a_sc[hh, rows] = jnp.exp(m_old - m_new)
                    m_sc[hh, rows] = m_new
            for hh in range(g):
                acc_sc[hh] = a_sc[hh] * acc_sc[hh] + _dot(p_sc[hh], v_ref[hh], NN)

        @pl.when(j < i)
        def _():
            step(False)

        @pl.when(j == i)
        def _():
            step(True)
            for hh in range(g):
                acc = acc_sc[hh]
                l = acc[:, dv:dv + 1]
                o_ref[hh] = acc[:, :dv] / l
                lse_ref[hh] = _col_to_row(m_sc[hh][:, 0:1] + jnp.log(l))

    qmap = lambda hg, n, qi_r, kj_r: (hg, qi_r[n], 0)
    kmap = lambda hg, n, qi_r, kj_r: (hg, kj_r[n], 0)
    return pl.pallas_call(
        body, name=name,
        grid_spec=pltpu.PrefetchScalarGridSpec(
            num_scalar_prefetch=2, grid=(h // g, qi.shape[0]),
            in_specs=[pl.BlockSpec((g, t, DA), qmap), pl.BlockSpec((g, t, DA), kmap), pl.BlockSpec((g, t, DA), kmap)],
            out_specs=[pl.BlockSpec((g, t, dv), qmap), pl.BlockSpec((g, 1, t), lambda hg, n, qi_r, kj_r: (hg, 0, qi_r[n]))],
            scratch_shapes=[pltpu.VMEM((g, t, DA), F32), pltpu.VMEM((g, t, DA), F32), pltpu.VMEM((g, t, t), BF16),
                            pltpu.VMEM((g, t, DA), F32)]),
        out_shape=[jax.ShapeDtypeStruct((h, s, dv), F32), jax.ShapeDtypeStruct((h, 1, s), F32)],
        compiler_params=_cp(("parallel", "arbitrary")),
    )(qi, kj, qa, ka, va)


def _attn_bwd(qa, ka, va, doa, lse_row, delta_row, decay, *, name, t=512):
    h, s, _ = qa.shape
    t = min(t, s)
    nb = s // t
    g = 3
    rc = min(ROW_CHUNK, t)
    qi, kj = _causal_blocks(nb, key_major=True)
    nsteps = qi.shape[0]

    def body(*refs):
        qi_ref, kj_ref, q_ref, k_ref, v_ref, do_ref, lse_ref, dl_ref = refs[:8]
        p_sc, ds_sc = refs[-2:]
        if decay:
            dq_ref, dk_ref, dv_ref, dcq_ref, dck_ref, dq_acc, dk_acc, dv_acc, dcq_acc, dck_acc = refs[8:-2]
        else:
            dq_ref, dk_ref, dv_ref, dq_acc, dk_acc, dv_acc = refs[8:-2]
        n = pl.program_id(1)
        i, j = qi_ref[n], kj_ref[n]

        @pl.when(n == 0)
        def _():
            dq_acc[...] = jnp.zeros_like(dq_acc)
            if decay:
                dcq_acc[...] = jnp.zeros_like(dcq_acc)

        @pl.when(i == j)
        def _():
            dk_acc[...] = jnp.zeros_like(dk_acc)
            dv_acc[...] = jnp.zeros_like(dv_acc)
            if decay:
                dck_acc[...] = jnp.zeros_like(dck_acc)

        def step(masked):
            sts = [_dot(k_ref[hh], q_ref[hh], NT) for hh in range(g)]
            dpts = [_dot(v_ref[hh], do_ref[hh], NT) for hh in range(g)]
            dcq = [jnp.zeros((1, t), F32) for _ in range(g)]
            for r0 in range(0, t, rc):
                rows = slice(r0, r0 + rc)
                for hh in range(g):
                    st = sts[hh][rows]
                    if masked:
                        row = lax.broadcasted_iota(jnp.int32, (rc, t), 0) + r0
                        col = lax.broadcasted_iota(jnp.int32, (rc, t), 1)
                        st = jnp.where(row <= col, st, -jnp.inf)
                    pt = jnp.exp(st - lse_ref[hh])
                    dst = pt * (dpts[hh][rows] - dl_ref[hh])
                    p_sc[hh, rows] = pt.astype(BF16)
                    ds_sc[hh, rows] = dst.astype(BF16)
                    if decay:
                        dcq[hh] = dcq[hh] + jnp.sum(dst, axis=0, keepdims=True)
                        dck_acc[hh, rows] -= jnp.sum(dst, axis=1, keepdims=True)
            for hh in range(g):
                dv_acc[hh] += _dot(p_sc[hh], do_ref[hh], NN)
                dk_acc[hh] += _dot(ds_sc[hh], q_ref[hh], NN)
                dq_acc[hh, i] += _dot(ds_sc[hh], k_ref[hh], TN)
                if decay:
                    dcq_acc[hh, i] += dcq[hh]

        @pl.when(i > j)
        def _():
            step(False)

        @pl.when(i == j)
        def _():
            step(True)

        @pl.when(i == nb - 1)
        def _():
            dk_ref[...] = dk_acc[...]
            dv_ref[...] = dv_acc[...]
            if decay:
                for hh in range(g):
                    dck_ref[hh] = _col_to_row(dck_acc[hh])

        @pl.when(n == nsteps - 1)
        def _():
            dq_ref[...] = dq_acc[...]
            if decay:
                dcq_ref[...] = dcq_acc[...]

    kmap = lambda hg, n, qi_r, kj_r: (hg, kj_r[n], 0)
    qmap = lambda hg, n, qi_r, kj_r: (hg, qi_r[n], 0)
    qrow = lambda hg, n, qi_r, kj_r: (hg, 0, qi_r[n])
    krow = lambda hg, n, qi_r, kj_r: (hg, 0, kj_r[n])
    whole = lambda hg, n, qi_r, kj_r: (hg, 0, 0, 0)
    in_specs = [pl.BlockSpec((g, t, DA), qmap), pl.BlockSpec((g, t, DA), kmap), pl.BlockSpec((g, t, DA), kmap),
                pl.BlockSpec((g, t, DA), qmap), pl.BlockSpec((g, 1, t), qrow), pl.BlockSpec((g, 1, t), qrow)]
    out_specs = [pl.BlockSpec((g, nb, t, DA), whole), pl.BlockSpec((g, t, DA), kmap), pl.BlockSpec((g, t, DA), kmap)]
    out_shape = [jax.ShapeDtypeStruct((h, nb, t, DA), F32), jax.ShapeDtypeStruct((h, s, DA), F32), jax.ShapeDtypeStruct((h, s, DA), F32)]
    scratch = [pltpu.VMEM((g, nb, t, DA), F32), pltpu.VMEM((g, t, DA), F32), pltpu.VMEM((g, t, DA), F32)]
    if decay:
        out_specs += [pl.BlockSpec((g, nb, 1, t), whole), pl.BlockSpec((g, 1, t), krow)]
        out_shape += [jax.ShapeDtypeStruct((h, nb, 1, t), F32), jax.ShapeDtypeStruct((h, 1, s), F32)]
        scratch += [pltpu.VMEM((g, nb, 1, t), F32), pltpu.VMEM((g, t, 1), F32)]
    scratch += [pltpu.VMEM((g, t, t), BF16), pltpu.VMEM((g, t, t), BF16)]
    outs = pl.pallas_call(
        body, name=name,
        grid_spec=pltpu.PrefetchScalarGridSpec(num_scalar_prefetch=2, grid=(h // g, nsteps), in_specs=in_specs, out_specs=out_specs,
                                               scratch_shapes=scratch),
        out_shape=out_shape, compiler_params=_cp(("parallel", "arbitrary")),
    )(qi, kj, qa, ka, va, doa, lse_row, delta_row)
    outs = list(outs)
    outs[0] = outs[0].reshape(h, s, DA)
    if decay:
        outs[3] = outs[3].reshape(h, 1, s)
    return outs


def _sel(rows, cols, pairs, value=1.0):
    m = np.zeros((rows, cols), np.float32)
    for r, c in pairs:
        m[r, c] = value
    return jnp.asarray(m, BF16)


def _lane_row(lanes):
    m = np.zeros((1, DA), np.float32)
    m[0, list(lanes)] = 1.0
    return jnp.asarray(m)


def _rms(xv, gain):
    r = lax.rsqrt(jnp.mean(xv * xv, axis=-1, keepdims=True) + EPS)
    return xv * r * gain


def _mla_q_prep(z, gain, wq_a, wq_b, cq, sq, *, name, tm=512):
    s = z.shape[0]
    tm = min(tm, s)

    def body(z_ref, g_ref, wa_ref, wb_ref, c_ref, s_ref, qa_ref, qn_ref):
        qn = _rms(z_ref[...], g_ref[...]).astype(BF16)
        qn_ref[...] = qn
        c, sn = c_ref[...], s_ref[...]
        for hh in range(H):
            cols = slice(hh * DA, (hh + 1) * DA)
            qa_ref[hh] = (_dot(qn, wa_ref[:, cols], NN) * c + _dot(qn, wb_ref[:, cols], NN) * sn).astype(BF16)

    row = lambda i: (i, 0)
    fixed = lambda i: (0, 0)
    return pl.pallas_call(
        body, name=name, grid=(s // tm,),
        in_specs=[pl.BlockSpec((tm, Q_RANK), lambda i: (i, Z_QA // Q_RANK)), pl.BlockSpec((1, Q_RANK), fixed),
                  pl.BlockSpec((Q_RANK, H * DA), fixed), pl.BlockSpec((Q_RANK, H * DA), fixed),
                  pl.BlockSpec((tm, DA), row), pl.BlockSpec((tm, DA), row)],
        out_specs=[pl.BlockSpec((H, tm, DA), lambda i: (0, i, 0)), pl.BlockSpec((tm, Q_RANK), row)],
        out_shape=[jax.ShapeDtypeStruct((H, s, DA), BF16), jax.ShapeDtypeStruct((s, Q_RANK), BF16)],
        compiler_params=_cp(("parallel",)),
    )(z, gain.reshape(1, Q_RANK), wq_a, wq_b, cq, sq)


def _mla_kv_prep(z, gain, wk, wv, ck, sk, *, name, tm=512):
    s = z.shape[0]
    tm = min(tm, s)
    one = _lane_row([VDIM])

    def body(zkv_ref, z3_ref, z15_ref, g_ref, wk_ref, wv_ref, c_ref, s_ref, one_ref, ka_ref, va_ref, kvn_ref):
        kvn = _rms(zkv_ref[...], g_ref[...]).astype(BF16)
        kvn_ref[...] = kvn
        kpe = z3_ref[...] * c_ref[...] + z15_ref[...] * s_ref[...]
        for hh in range(H):
            cols = slice(hh * DA, (hh + 1) * DA)
            ka_ref[hh] = (_dot(kvn, wk_ref[:, cols], NN) + kpe).astype(BF16)
            va_ref[hh] = (_dot(kvn, wv_ref[:, cols], NN) + one_ref[...]).astype(BF16)

    row = lambda i: (i, 0)
    fixed = lambda i: (0, 0)
    blk = lambda c: pl.BlockSpec((tm, DA), lambda i: (i, c))
    heads = pl.BlockSpec((H, tm, DA), lambda i: (0, i, 0))
    return pl.pallas_call(
        body, name=name, grid=(s // tm,),
        in_specs=[blk(Z_KVA // DA), blk(Z_KR // DA), blk(Z_F // DA), pl.BlockSpec((1, KV_RANK), fixed),
                  pl.BlockSpec((KV_RANK, H * DA), fixed), pl.BlockSpec((KV_RANK, H * DA), fixed),
                  pl.BlockSpec((tm, DA), row), pl.BlockSpec((tm, DA), row), pl.BlockSpec((1, DA), fixed)],
        out_specs=[heads, heads, pl.BlockSpec((tm, KV_RANK), row)],
        out_shape=[jax.ShapeDtypeStruct((H, s, DA), BF16), jax.ShapeDtypeStruct((H, s, DA), BF16),
                   jax.ShapeDtypeStruct((s, KV_RANK), BF16)],
        compiler_params=_cp(("parallel",)),
    )(z, z, z, gain.reshape(1, KV_RANK), wk, wv, ck, sk, one)


DEC_C = (FOX_D, FOX_D + 1, FOX_D + 2)
DEC_1 = (FOX_D + 3, FOX_D + 4, FOX_D + 5)


def _fox_prep(z, c3t, *, name, tm=512):
    s = z.shape[0]
    tm = min(tm, s)
    w = H * FOX_D
    left = [(r, r) for r in range(FOX_D)]
    right = [(FOX_D + r, r) for r in range(FOX_D)]
    pq = jnp.stack([_sel(DA, DA, left, SCALE_FOX), _sel(DA, DA, right, SCALE_FOX)])
    pk = jnp.stack([_sel(DA, DA, left), _sel(DA, DA, right)])
    pcq = jnp.stack([_sel(32, DA, [(hh + 8 * k, DEC_C[k]) for k in range(3)]) for hh in range(H)])
    pck = jnp.stack([_sel(32, DA, [(hh + 8 * k, DEC_1[k]) for k in range(3)], -1.0) for hh in range(H)])
    rows3 = jnp.concatenate([_lane_row(DEC_1), _lane_row(DEC_C), _lane_row([FOX_D])], axis=0)

    def body(zq_ref, zk_ref, zv_ref, c_ref, pq_ref, pk_ref, pcq_ref, pck_ref, r_ref, qa_ref, ka_ref, va_ref):
        c3 = c_ref[...]
        for pair in range(H // 2):
            lanes = slice(pair * DA, (pair + 1) * DA)
            zq, zk, zv = zq_ref[:, lanes].astype(BF16), zk_ref[:, lanes].astype(BF16), zv_ref[:, lanes].astype(BF16)
            for side in range(2):
                hh = 2 * pair + side
                qa_ref[hh] = (_dot(zq, pq_ref[side], NN) + _dot(c3, pcq_ref[hh], TN) + r_ref[0:1, :]).astype(BF16)
                ka_ref[hh] = (_dot(zk, pk_ref[side], NN) + _dot(c3, pck_ref[hh], TN) + r_ref[1:2, :]).astype(BF16)
                va_ref[hh] = (_dot(zv, pk_ref[side], NN) + r_ref[2:3, :]).astype(BF16)

    fixed2 = lambda i: (0, 0)
    fixed3 = lambda i: (0, 0, 0)
    heads = pl.BlockSpec((H, tm, DA), lambda i: (0, i, 0))
    zblk = lambda c: pl.BlockSpec((tm, w), lambda i: (i, c))
    return pl.pallas_call(
        body, name=name, grid=(s // tm,),
        in_specs=[zblk(Z_FOX // w), zblk(Z_FOX // w + 1), zblk(Z_FOX // w + 2), pl.BlockSpec((32, tm), lambda i: (0, i)),
                  pl.BlockSpec((2, DA, DA), fixed3), pl.BlockSpec((2, DA, DA), fixed3),
                  pl.BlockSpec((H, 32, DA), fixed3), pl.BlockSpec((H, 32, DA), fixed3), pl.BlockSpec((3, DA), fixed2)],
        out_specs=[heads, heads, heads], out_shape=[jax.ShapeDtypeStruct((H, s, DA), BF16)] * 3,
        compiler_params=_cp(("parallel",)),
    )(z, z, z, c3t, pq, pk, pcq, pck, rows3)


def _mix_out(oa, yb, oc, w_out, x1, *, name, tm=512):
    s = yb.shape[0]
    tm = min(tm, s)
    e2 = jnp.stack([_sel(VDIM, DA, [(r, r) for r in range(VDIM)]), _sel(VDIM, DA, [(r, VDIM + r) for r in range(VDIM)])])

    def body(oa_ref, yb_ref, oc_ref, e_ref, w_ref, x_ref, x2_ref, cat_ref):
        def pairs(o_ref):
            return [(_dot(o_ref[2 * p].astype(BF16), e_ref[0], NN) + _dot(o_ref[2 * p + 1].astype(BF16), e_ref[1], NN)).astype(BF16)
                    for p in range(H // 2)]

        cat = jnp.concatenate(pairs(oa_ref) + [yb_ref[...].astype(BF16)] + pairs(oc_ref), axis=1)
        cat_ref[...] = cat
        x2_ref[...] = x_ref[...] + _dot(cat, w_ref[...], NN)

    row = lambda i: (i, 0)
    heads = pl.BlockSpec((H, tm, VDIM), lambda i: (0, i, 0))
    return pl.pallas_call(
        body, name=name, grid=(s // tm,),
        in_specs=[heads, pl.BlockSpec((tm, POOL_W), row), heads, pl.BlockSpec((2, VDIM, DA), lambda i: (0, 0, 0)),
                  pl.BlockSpec((D, D), lambda i: (0, 0)), pl.BlockSpec((tm, D), row)],
        out_specs=[pl.BlockSpec((tm, D), row), pl.BlockSpec((tm, D), row)],
        out_shape=[jax.ShapeDtypeStruct((s, D), F32), jax.ShapeDtypeStruct((s, D), BF16)],
        compiler_params=_cp(("parallel",)),
    )(oa, yb, oc, e2, w_out, x1)


def _mix_out_bwd(dx2b, w_out, oa, oc, *, name, tm=512):
    s = dx2b.shape[0]
    tm = min(tm, s)
    f2 = jnp.stack([_sel(DA, DA, [(r, r) for r in range(VDIM)]), _sel(DA, DA, [(VDIM + r, r) for r in range(VDIM)])])
    nv = H * VDIM

    def body(dx_ref, w_ref, oa_ref, oc_ref, f_ref, doa_ref, doc_ref, dyb_ref, dla_ref, dlc_ref):
        dcat = _dot(dx_ref[...], w_ref[...], NT)
        dyb_ref[...] = dcat[:, nv:nv + POOL_W]
        for base, o_ref, do_ref, dl_ref in ((0, oa_ref, doa_ref, dla_ref), (nv + POOL_W, oc_ref, doc_ref, dlc_ref)):
            for p in range(H // 2):
                blk = dcat[:, base + p * DA:base + (p + 1) * DA].astype(BF16)
                for side in range(2):
                    hh = 2 * p + side
                    do = _dot(blk, f_ref[side], NN)
                    do_ref[hh] = do.astype(BF16)
                    dl_ref[hh] = _col_to_row(jnp.sum(do[:, :VDIM] * o_ref[hh], axis=-1, keepdims=True))

    row = lambda i: (i, 0)
    heads = lambda w: pl.BlockSpec((H, tm, w), lambda i: (0, i, 0))
    return pl.pallas_call(
        body, name=name, grid=(s // tm,),
        in_specs=[pl.BlockSpec((tm, D), row), pl.BlockSpec((D, D), lambda i: (0, 0)), heads(VDIM), heads(VDIM),
                  pl.BlockSpec((2, DA, DA), lambda i: (0, 0, 0))],
        out_specs=[heads(DA), heads(DA), pl.BlockSpec((tm, POOL_W), row),
                   pl.BlockSpec((H, 1, tm), lambda i: (0, 0, i)), pl.BlockSpec((H, 1, tm), lambda i: (0, 0, i))],
        out_shape=[jax.ShapeDtypeStruct((H, s, DA), BF16), jax.ShapeDtypeStruct((H, s, DA), BF16),
                   jax.ShapeDtypeStruct((s, POOL_W), F32), jax.ShapeDtypeStruct((H, 1, s), F32), jax.ShapeDtypeStruct((H, 1, s), F32)],
        compiler_params=_cp(("parallel",)),
    )(dx2b, w_out, oa, oc, f2)


def _mla_bwd_prep(dqa, dka, dva, dft, cq, sq, ck, sk, *, name, tm=512):
    s = dqa.shape[1]
    tm = min(tm, s)
    keep = _lane_row(range(NOPE))

    def body(dq_ref, dk_ref, dv_ref, dft_ref, cq_ref, sq_ref, ck_ref, sk_ref, keep_ref, dqab_ref, dkv_ref, dz3_ref, dz15_ref):
        cqv, sqv = cq_ref[...], sq_ref[...]
        dkpe = jnp.zeros((tm, DA), F32)
        for hh in range(H):
            lanes = slice(hh * DA, (hh + 1) * DA)
            dq = dq_ref[hh]
            dqab_ref[:, lanes] = (dq * cqv).astype(BF16)
            dqab_ref[:, H * DA + hh * DA:H * DA + (hh + 1) * DA] = (dq * sqv).astype(BF16)
            dk = dk_ref[hh]
            dkpe = dkpe + dk
            dkv_ref[:, lanes] = (dk * keep_ref[...]).astype(BF16)
            dkv_ref[:, H * DA + hh * DA:H * DA + (hh + 1) * DA] = (dv_ref[hh] * keep_ref[...]).astype(BF16)
        dz3_ref[...] = (dkpe * ck_ref[...]).astype(BF16)
        dz15_ref[...] = (dkpe * sk_ref[...] + dft_ref[...]).astype(BF16)

    row = lambda i: (i, 0)
    heads = pl.BlockSpec((H, tm, DA), lambda i: (0, i, 0))
    tab = pl.BlockSpec((tm, DA), row)
    return pl.pallas_call(
        body, name=name, grid=(s // tm,),
        in_specs=[heads, heads, heads, tab, tab, tab, tab, tab, pl.BlockSpec((1, DA), lambda i: (0, 0))],
        out_specs=[pl.BlockSpec((tm, 2 * H * DA), row), pl.BlockSpec((tm, 2 * H * DA), row), tab, tab],
        out_shape=[jax.ShapeDtypeStruct((s, 2 * H * DA), BF16), jax.ShapeDtypeStruct((s, 2 * H * DA), BF16),
                   jax.ShapeDtypeStruct((s, DA), BF16), jax.ShapeDtypeStruct((s, DA), BF16)],
        compiler_params=_cp(("parallel",)),
    )(dqa, dka, dva, dft, cq, sq, ck, sk, keep)


def _fox_bwd_prep(dfqa, dfka, dfva, *, name, tm=512):
    s = dfqa.shape[1]
    tm = min(tm, s)
    place = lambda v: jnp.stack([_sel(DA, DA, [(r, r) for r in range(FOX_D)], v), _sel(DA, DA, [(r, FOX_D + r) for r in range(FOX_D)], v)])
    gq, gk = place(SCALE_FOX), place(1.0)

    def body(dq_ref, dk_ref, dv_ref, gq_ref, gk_ref, dz_ref):
        for part, (d_ref, g_ref) in enumerate(((dq_ref, gq_ref), (dk_ref, gk_ref), (dv_ref, gk_ref))):
            for p in range(H // 2):
                blk = _dot(d_ref[2 * p].astype(BF16), g_ref[0], NN) + _dot(d_ref[2 * p + 1].astype(BF16), g_ref[1], NN)
                lo = part * H * FOX_D + p * DA
                dz_ref[:, lo:lo + DA] = blk.astype(BF16)

    heads = pl.BlockSpec((H, tm, DA), lambda i: (0, i, 0))
    sel = pl.BlockSpec((2, DA, DA), lambda i: (0, 0, 0))
    return pl.pallas_call(
        body, name=name, grid=(s // tm,), in_specs=[heads, heads, heads, sel, sel],
        out_specs=pl.BlockSpec((tm, 3 * H * FOX_D), lambda i: (i, 0)),
        out_shape=jax.ShapeDtypeStruct((s, 3 * H * FOX_D), BF16), compiler_params=_cp(("parallel",)),
    )(dfqa, dfka, dfva, gq, gk)


def _lane_scan(x, s, reverse):
    lane = lax.broadcasted_iota(jnp.int32, x.shape, 1)
    sh = 1
    while sh < s:
        if reverse:
            x = x + jnp.where(lane < s - sh, pltpu.roll(x, s - sh, axis=1), 0.0)
        else:
            x = x + jnp.where(lane >= sh, pltpu.roll(x, sh, axis=1), 0.0)
        sh *= 2
    return x


def _gate_fwd(z, col_block, bias, *, name):
    s = z.shape[0]

    def body(z_ref, b_ref, f_ref, c_ref):
        ft = z_ref[...].T[0:8, :]
        f_ref[...] = ft
        xg = ft + b_ref[...]
        lf = jnp.minimum(xg, 0.0) - jnp.log(1.0 + jnp.exp(-jnp.abs(xg)))
        c = _lane_scan(lf, s, False)
        hi = c.astype(BF16).astype(F32)
        r = c - hi
        mid = r.astype(BF16).astype(F32)
        lo = r - mid
        c_ref[...] = jnp.concatenate([hi, mid, lo, jnp.zeros_like(hi)], axis=0).astype(BF16)

    return pl.pallas_call(
        body, name=name, grid=(1,),
        in_specs=[pl.BlockSpec((s, 128), lambda i: (0, col_block)), pl.BlockSpec((8, 1), lambda i: (0, 0))],
        out_specs=[pl.BlockSpec((8, s), lambda i: (0, 0)), pl.BlockSpec((32, s), lambda i: (0, 0))],
        out_shape=[jax.ShapeDtypeStruct((8, s), F32), jax.ShapeDtypeStruct((32, s), BF16)],
        compiler_params=_cp(("arbitrary",)))(z, bias)


def _gate_bwd(ft, bias, dc, *, name):
    s = ft.shape[1]

    def body(f_ref, b_ref, dc_ref, df_ref, db_ref):
        xg = f_ref[...] + b_ref[...]
        dlf = _lane_scan(dc_ref[...], s, True)
        df = dlf * _sigmoid(-xg)
        db_ref[...] = jnp.sum(df, axis=-1, keepdims=True)
        df_ref[...] = jnp.concatenate([df, jnp.zeros((DA - 8, s), F32)], axis=0).T

    return pl.pallas_call(body, name=name, out_shape=[jax.ShapeDtypeStruct((s, DA), F32), jax.ShapeDtypeStruct((8, 1), F32)],
                          compiler_params=_cp())(ft, bias, dc)


def _pool_lane_consts(tm, i):
    lane = lax.broadcasted_iota(jnp.int32, (tm, POOL_W), 1)
    tok = lax.broadcasted_iota(jnp.int32, (tm, POOL_W), 0) + i * tm
    win = jnp.where(lane < 64, 2, jnp.where(lane < 128, 4, jnp.where(lane < 192, 8, 16)))
    cnt = jnp.minimum(tok + 1, win).astype(F32)
    return lane, tok, cnt


def _pick_window(lane, s2, s4, s8, s16):
    return jnp.where(lane < 64, s2, jnp.where(lane < 128, s4, jnp.where(lane < 192, s8, s16)))


def _pool_fwd(z, col_block, bd, scale, *, name, tm=512):
    s = z.shape[0]
    tm = min(tm, s)
    hb = tm // POOL_HALO

    def body(u_ref, halo_ref, bd_ref, sc_ref, y_ref, p_ref, buf):
        i = pl.program_id(0)
        buf[0:POOL_HALO, :] = halo_ref[...] * (i > 0).astype(F32)
        buf[POOL_HALO:, :] = u_ref[...]

        def back(k):
            return buf[POOL_HALO - k:POOL_HALO - k + tm, :]

        u = u_ref[...]
        s2 = u + back(1)
        s4 = s2 + back(2) + back(3)
        s8 = s4 + back(4) + back(5) + back(6) + back(7)
        s16 = s8
        for k in range(8, 16):
            s16 = s16 + back(k)
        lane, _, cnt = _pool_lane_consts(tm, i)
        pooled = (_pick_window(lane, s2, s4, s8, s16) / cnt - u).astype(BF16)
        p_ref[...] = pooled
        y_ref[...] = _dot(pooled, bd_ref[...], NN) * sc_ref[...]

    return pl.pallas_call(
        body, name=name, grid=(s // tm,),
        in_specs=[pl.BlockSpec((tm, POOL_W), lambda i: (i, col_block)),
                  pl.BlockSpec((POOL_HALO, POOL_W), lambda i: (jnp.maximum(i * hb - 1, 0), col_block)),
                  pl.BlockSpec((POOL_W, POOL_W), lambda i: (0, 0)), pl.BlockSpec((1, POOL_W), lambda i: (0, 0))],
        out_specs=[pl.BlockSpec((tm, POOL_W), lambda i: (i, 0)), pl.BlockSpec((tm, POOL_W), lambda i: (i, 0))],
        out_shape=[jax.ShapeDtypeStruct((s, POOL_W), F32), jax.ShapeDtypeStruct((s, POOL_W), BF16)],
        scratch_shapes=[pltpu.VMEM((tm + POOL_HALO, POOL_W), F32)],
        compiler_params=_cp(("parallel",)),
    )(z, z, bd, scale.reshape(1, POOL_W))


def _pool_bwd_a(dy, pooled, bd, scale, *, name, tm=512):
    s = dy.shape[0]
    tm = min(tm, s)

    def body(dy_ref, p_ref, bd_ref, sc_ref, dq_ref, dys_ref, dsc_ref):
        i = pl.program_id(0)
        dyv = dy_ref[...]
        y0 = _dot(p_ref[...], bd_ref[...], NN)
        dys = (dyv * sc_ref[...]).astype(BF16)
        dys_ref[...] = dys
        dp = _dot(dys, bd_ref[...], NT)
        _, _, cnt = _pool_lane_consts(tm, i)
        dq_ref[:, 0:POOL_W] = dp / cnt
        dq_ref[:, POOL_W:] = dp

        @pl.when(i == 0)
        def _():
            dsc_ref[...] = jnp.zeros_like(dsc_ref)

        dsc_ref[...] += jnp.sum(dyv * y0, axis=0, keepdims=True)

    row = lambda i: (i, 0)
    dq, dys, dsc = pl.pallas_call(
        body, name=name, grid=(s // tm,),
        in_specs=[pl.BlockSpec((tm, POOL_W), row), pl.BlockSpec((tm, POOL_W), row),
                  pl.BlockSpec((POOL_W, POOL_W), lambda i: (0, 0)), pl.BlockSpec((1, POOL_W), lambda i: (0, 0))],
        out_specs=[pl.BlockSpec((tm, 2 * POOL_W), row), pl.BlockSpec((tm, POOL_W), row), pl.BlockSpec((1, POOL_W), lambda i: (0, 0))],
        out_shape=[jax.ShapeDtypeStruct((s, 2 * POOL_W), F32), jax.ShapeDtypeStruct((s, POOL_W), BF16),
                   jax.ShapeDtypeStruct((1, POOL_W), F32)],
        compiler_params=_cp(("arbitrary",)),
    )(dy, pooled, bd, scale.reshape(1, POOL_W))
    return dq, dys, dsc.reshape(POOL_W)


def _pool_bwd_b(dq, *, name, tm=512):
    s = dq.shape[0]
    tm = min(tm, s)
    hb = tm // POOL_HALO
    nblk = s // tm

    def body(q_ref, dp_ref, halo_ref, du_ref, buf):
        i = pl.program_id(0)
        buf[0:tm, :] = q_ref[...]
        buf[tm:, :] = halo_ref[...] * (i < nblk - 1).astype(F32)

        def ahead(k):
            return buf[k:k + tm, :]

        q = q_ref[...]
        s2 = q + ahead(1)
        s4 = s2 + ahead(2) + ahead(3)
        s8 = s4 + ahead(4) + ahead(5) + ahead(6) + ahead(7)
        s16 = s8
        for k in range(8, 16):
            s16 = s16 + ahead(k)
        lane = lax.broadcasted_iota(jnp.int32, (tm, POOL_W), 1)
        du_ref[...] = _pick_window(lane, s2, s4, s8, s16) - dp_ref[...]

    return pl.pallas_call(
        body, name=name, grid=(nblk,),
        in_specs=[pl.BlockSpec((tm, POOL_W), lambda i: (i, 0)), pl.BlockSpec((tm, POOL_W), lambda i: (i, 1)),
                  pl.BlockSpec((POOL_HALO, POOL_W), lambda i: (jnp.minimum((i + 1) * hb, nblk * hb - 1), 0))],
        out_specs=pl.BlockSpec((tm, POOL_W), lambda i: (i, 0)),
        out_shape=jax.ShapeDtypeStruct((s, POOL_W), F32),
        scratch_shapes=[pltpu.VMEM((tm + POOL_HALO, POOL_W), F32)],
        compiler_params=_cp(("parallel",)),
    )(dq, dq, dq)


def _loss_head(x, gain, target, *, name, tm=512):
    s = x.shape[0]
    tm = min(tm, s)

    def body(x_ref, g_ref, t_ref, dx_ref, dg_ref, loss_ref):
        xv = x_ref[...]
        r = lax.rsqrt(jnp.mean(xv * xv, axis=-1, keepdims=True) + EPS)
        xh = xv * r
        err = xh * g_ref[...] - t_ref[...]
        dy = err * (1.0 / D)
        a = dy * g_ref[...]
        dx_ref[...] = r * a - xh * (r * jnp.mean(a * xh, axis=-1, keepdims=True))

        @pl.when(pl.program_id(0) == 0)
        def _():
            dg_ref[...] = jnp.zeros_like(dg_ref)
            loss_ref[...] = jnp.zeros_like(loss_ref)

        dg_ref[...] += jnp.sum(dy * xh, axis=0, keepdims=True)
        part = 0.5 * jnp.sum(jnp.mean(err * err, axis=-1, keepdims=True), axis=0, keepdims=True)
        loss_ref[...] += jnp.broadcast_to(part, loss_ref.shape)

    row = lambda i: (i, 0)
    dx, dg, loss = pl.pallas_call(
        body, name=name, grid=(s // tm,),
        in_specs=[pl.BlockSpec((tm, D), row), pl.BlockSpec((1, D), lambda i: (0, 0)), pl.BlockSpec((tm, D), row)],
        out_specs=[pl.BlockSpec((tm, D), row), pl.BlockSpec((1, D), lambda i: (0, 0)), pl.BlockSpec((1, 128), lambda i: (0, 0))],
        out_shape=[jax.ShapeDtypeStruct((s, D), F32), jax.ShapeDtypeStruct((1, D), F32), jax.ShapeDtypeStruct((1, 128), F32)],
        compiler_params=_cp(("arbitrary",)),
    )(x, gain.reshape(1, D), target)
    return dx, dg.reshape(D), loss[0, 0]


def _adamw(w, g, m, v, *, name, tr=512):
    rows, cols = w.shape
    tr = min(tr, rows)
    assert rows % tr == 0, (name, rows, tr)
    c_m = 1.0 - ADAM_B1
    c_v = 1.0 - ADAM_B2
    bc1 = 1.0 - ADAM_B1 ** ADAM_STEP
    bc2 = 1.0 - ADAM_B2 ** ADAM_STEP

    def body(w_ref, g_ref, m_ref, v_ref, d_ref, mo_ref, vo_ref):
        gv = g_ref[...]
        mn = ADAM_B1 * m_ref[...] + c_m * gv
        vn = ADAM_B2 * v_ref[...] + c_v * (gv * gv)
        mo_ref[...] = mn
        vo_ref[...] = vn
        d_ref[...] = -ADAM_LR * ((mn / bc1) / (jnp.sqrt(vn / bc2) + ADAM_EPS) + ADAM_WD * w_ref[...])

    spec = pl.BlockSpec((tr, cols), lambda i: (i, 0))
    return pl.pallas_call(body, name=name, grid=(rows // tr,), in_specs=[spec] * 4, out_specs=[spec] * 3,
                          out_shape=[jax.ShapeDtypeStruct((rows, cols), F32)] * 3,
                          compiler_params=_cp(("parallel",)))(w, g, m, v)


def _adamw_layer(w, g, m, v, layer, prev, *, name, tr):
    rows, cols = g.shape
    assert rows % tr == 0 and w.shape == (DEPTH * rows, cols), (name, w.shape, g.shape, tr)
    nblk = rows // tr
    c_m = 1.0 - ADAM_B1
    c_v = 1.0 - ADAM_B2
    bc1 = 1.0 - ADAM_B1 ** ADAM_STEP
    bc2 = 1.0 - ADAM_B2 ** ADAM_STEP
    n_prev = 0 if prev is None else 4

    def body(*refs):
        w_ref, g_ref, m_ref, v_ref = refs[:4]
        d_ref, mo_ref, vo_ref, go_ref = refs[4 + n_prev:]
        gv = g_ref[...]
        mn = ADAM_B1 * m_ref[...] + c_m * gv
        vn = ADAM_B2 * v_ref[...] + c_v * (gv * gv)
        mo_ref[...] = mn
        vo_ref[...] = vn
        go_ref[...] = gv
        d_ref[...] = -ADAM_LR * ((mn / bc1) / (jnp.sqrt(vn / bc2) + ADAM_EPS) + ADAM_WD * w_ref[...])

    stacked = pl.BlockSpec((tr, cols), lambda i: (layer * nblk + i, 0))
    args = [w, g, m, v] + ([] if prev is None else list(prev))
    return pl.pallas_call(
        body, name=name, grid=(nblk,),
        in_specs=[stacked, pl.BlockSpec((tr, cols), lambda i: (i, 0)), stacked, stacked] + [ANY_SPEC] * n_prev,
        out_specs=[stacked] * 4, out_shape=[jax.ShapeDtypeStruct(w.shape, F32)] * 4,
        input_output_aliases={4 + k: k for k in range(n_prev)},
        compiler_params=_cp(("parallel",)))(*args)


def _position():
    return jnp.stack([lax.axis_index("c"), 2 * lax.axis_index("x") + lax.axis_index("y")]).astype(jnp.int32)


SUM_ROW_TILES = 2


def _sum2_bf16(pos, fulls, sibs, *, name):
    n = len(fulls)
    nb = SUM_ROW_TILES

    def body(pos_ref, *refs):
        for t in range(n):
            refs[2 * n + t][...] = (refs[t][...] + refs[n + t][...]).astype(BF16)

    in_specs, sib_specs = [], []
    for sb in sibs:
        _, half, cols = sb.shape
        tr = half // nb
        assert half % nb == 0 and tr % 16 == 0, sb.shape
        in_specs.append(pl.BlockSpec((None, tr, cols), lambda j, i, p: (j, p[0] * nb + i, 0)))
        sib_specs.append(pl.BlockSpec((None, tr, cols), lambda j, i, p: (j, i, 0)))
    return pl.pallas_call(
        body, name=name,
        grid_spec=pltpu.PrefetchScalarGridSpec(num_scalar_prefetch=1, grid=(N_CHIPS, nb), in_specs=in_specs + sib_specs,
                                               out_specs=sib_specs),
        out_shape=[jax.ShapeDtypeStruct(sb.shape, BF16) for sb in sibs],
        compiler_params=_cp(("parallel", "parallel")))(pos, *fulls, *sibs)


def _sum5(pos, fulls, sibs, recvs, *, name):
    n = len(fulls)
    nb = SUM_ROW_TILES

    def body(pos_ref, *refs):
        for t in range(n):
            acc = refs[t][...] + refs[n + t][...]
            for kk in range(3):
                acc = acc + refs[2 * n + t][kk].astype(F32)
            refs[3 * n + t][...] = acc

    f_specs, s_specs, r_specs, o_specs = [], [], [], []
    for f in fulls:
        _, rows, cols = f.shape
        tr = rows // 2 // nb
        f_specs.append(pl.BlockSpec((None, tr, cols), lambda i, p: (p[1], p[0] * nb + i, 0)))
        s_specs.append(pl.BlockSpec((None, tr, cols), lambda i, p: (p[1], i, 0)))
        r_specs.append(pl.BlockSpec((3, tr, cols), lambda i, p: (0, i, 0)))
        o_specs.append(pl.BlockSpec((tr, cols), lambda i, p: (p[0] * nb + i, 0)))
    return pl.pallas_call(
        body, name=name,
        grid_spec=pltpu.PrefetchScalarGridSpec(num_scalar_prefetch=1, grid=(nb,), in_specs=f_specs + s_specs + r_specs,
                                               out_specs=o_specs),
        out_shape=[jax.ShapeDtypeStruct(f.shape[1:], F32) for f in fulls],
        compiler_params=_cp(("parallel",)))(pos, *fulls, *sibs, *recvs)


def _place():
    x, y, c = lax.axis_index("x"), lax.axis_index("y"), lax.axis_index("c")
    chips = [(1 - x, y), (x, 1 - y), (1 - x, 1 - y)]
    return x, y, c, 2 * x + y, chips


SEM_SPEC = pl.BlockSpec(memory_space=pltpu.SEMAPHORE)
ANY_SPEC = pl.BlockSpec(memory_space=pl.ANY)


def _gather_copies(ins, outs, send_i, recv_i, send_o, recv_o):
    x, y, c, me, chips = _place()
    n = len(ins)
    started, awaited = [], []
    for t in range(n):
        half = ins[t].shape[0] // 2
        mine = pl.ds(c * half, half)
        started.append(pltpu.make_async_remote_copy(
            src_ref=ins[t], dst_ref=outs[t].at[me], send_sem=send_o.at[t], recv_sem=recv_o.at[t],
            device_id=(x, y, 1 - c), device_id_type=MESH))
        awaited.append(started[-1])
        for kk, (px, py) in enumerate(chips):
            started.append(pltpu.make_async_remote_copy(
                src_ref=ins[t].at[mine], dst_ref=outs[t].at[me, mine], send_sem=send_i.at[t * 3 + kk],
                recv_sem=recv_i.at[t * 3 + kk], device_id=(px, py, c), device_id_type=MESH))
            awaited.append(pltpu.make_async_remote_copy(
                src_ref=ins[t].at[mine], dst_ref=outs[t].at[2 * px + py, mine], send_sem=send_i.at[t * 3 + kk],
                recv_sem=recv_i.at[t * 3 + kk], device_id=(px, py, c), device_id_type=MESH))
    return started, awaited


def _forward_copies(outs, send_d, recv_d):
    x, y, c, me, chips = _place()
    started, awaited = [], []
    for t in range(len(outs)):
        half = outs[t].shape[1] // 2
        for kk, (px, py) in enumerate(chips):
            for lst, hc in ((started, c), (awaited, 1 - c)):
                blk = outs[t].at[2 * px + py, pl.ds(hc * half, half)]
                lst.append(pltpu.make_async_remote_copy(src_ref=blk, dst_ref=blk, send_sem=send_d.at[t * 3 + kk],
                                                        recv_sem=recv_d.at[t * 3 + kk], device_id=(x, y, 1 - c), device_id_type=MESH))
    return started, awaited


def _gather_blocking(shards):
    n = len(shards)

    def body(*refs):
        ins, outs = refs[:n], refs[n:2 * n]
        send_i, recv_i, send_d, recv_d, send_o, recv_o = refs[2 * n:]
        started, awaited = _gather_copies(ins, outs, send_i, recv_i, send_o, recv_o)
        for cp in started:
            cp.start()
        for cp in awaited:
            cp.wait_recv()
        fwd, fwd_in = _forward_copies(outs, send_d, recv_d)
        for cp in fwd:
            cp.start()
        for cp in fwd_in:
            cp.wait_recv()
        for cp in started + fwd:
            cp.wait_send()

    return pl.pallas_call(
        body, name="gather_first", in_specs=[HBM_SPEC] * n, out_specs=[HBM_SPEC] * n,
        out_shape=[jax.ShapeDtypeStruct((N_CHIPS,) + s.shape, s.dtype) for s in shards],
        scratch_shapes=[pltpu.SemaphoreType.DMA((3 * n,)), pltpu.SemaphoreType.DMA((3 * n,)),
                        pltpu.SemaphoreType.DMA((3 * n,)), pltpu.SemaphoreType.DMA((3 * n,)),
                        pltpu.SemaphoreType.DMA((n,)), pltpu.SemaphoreType.DMA((n,))],
    )(*shards)


def _gather_start(shards, after):
    n = len(shards)

    def body(*refs):
        ins = refs[:n]
        send_i, recv_i, send_o, recv_o = refs[2 * n + 1:2 * n + 5]
        outs = refs[3 * n + 5:4 * n + 5]
        token = refs[4 * n + 5]
        started, _ = _gather_copies(ins, outs, send_i, recv_i, send_o, recv_o)
        for cp in started:
            cp.start()
        token[...] = jnp.zeros_like(token)

    lands = [lax.empty((N_CHIPS,) + s.shape, s.dtype) for s in shards]
    sems = [pltpu.SemaphoreType.DMA((3 * n,)), pltpu.SemaphoreType.DMA((3 * n,)), pltpu.SemaphoreType.DMA((n,)), pltpu.SemaphoreType.DMA((n,))]
    res = pl.pallas_call(
        body, name="gather_rest_start",
        in_specs=[HBM_SPEC] * (2 * n) + [ANY_SPEC],
        out_specs=[SEM_SPEC] * 4 + [HBM_SPEC] * (2 * n) + [pl.BlockSpec(memory_space=pltpu.VMEM)],
        out_shape=sems + [jax.ShapeDtypeStruct(s.shape, s.dtype) for s in shards]
        + [jax.ShapeDtypeStruct(a.shape, a.dtype) for a in lands] + [jax.ShapeDtypeStruct((8, 128), F32)],
        input_output_aliases={t: 4 + t for t in range(2 * n)},
        compiler_params=pltpu.CompilerParams(has_side_effects=pltpu.SideEffectType.DATAFLOW_SIDE_EFFECTING),
    )(*[pltpu.with_memory_space_constraint(s, pltpu.HBM) for s in shards],
      *[pltpu.with_memory_space_constraint(a, pltpu.HBM) for a in lands], after)
    return res[:4], res[4:4 + n], res[4 + n:4 + 2 * n], res[-1]


def _gather_wait(sems, shards_thru, lands_thru, after):
    n = len(shards_thru)

    def body(*refs):
        ins, outs_in = refs[:n], refs[n:2 * n]
        send_i, recv_i, send_o, recv_o = refs[2 * n:2 * n + 4]
        started, awaited = _gather_copies(ins, outs_in, send_i, recv_i, send_o, recv_o)
        for cp in started:
            cp.wait_send()
        for cp in awaited:
            cp.wait_recv()

    res = pl.pallas_call(
        body, name="gather_rest_wait",
        in_specs=[HBM_SPEC] * (2 * n) + [SEM_SPEC] * 4 + [ANY_SPEC],
        out_specs=[HBM_SPEC] * (2 * n),
        out_shape=[jax.ShapeDtypeStruct(a.shape, a.dtype) for a in list(shards_thru) + list(lands_thru)],
        input_output_aliases={t: t for t in range(2 * n)},
        compiler_params=pltpu.CompilerParams(has_side_effects=pltpu.SideEffectType.DATAFLOW_SIDE_EFFECTING),
    )(*shards_thru, *lands_thru, *sems, after)
    return res[n:]


def _gather_forward(lands):
    n = len(lands)

    def body(*refs):
        outs = refs[n:2 * n]
        send_d, recv_d = refs[2 * n:]
        fwd, fwd_in = _forward_copies(outs, send_d, recv_d)
        for cp in fwd:
            cp.start()
        for cp in fwd_in:
            cp.wait_recv()
        for cp in fwd:
            cp.wait_send()

    return pl.pallas_call(
        body, name="gather_rest_forward", in_specs=[HBM_SPEC] * n, out_specs=[HBM_SPEC] * n,
        out_shape=[jax.ShapeDtypeStruct(a.shape, a.dtype) for a in lands],
        input_output_aliases={t: t for t in range(n)},
        scratch_shapes=[pltpu.SemaphoreType.DMA((3 * n,)), pltpu.SemaphoreType.DMA((3 * n,))],
    )(*lands)


def _stage1_copies(ins, sib, send, recv):
    x, y, c, me, chips = _place()
    cps = []
    for t in range(len(ins)):
        rows = ins[t].shape[1] // 2
        cps.append(pltpu.make_async_remote_copy(
            src_ref=ins[t].at[:, pl.ds((1 - c) * rows, rows), :], dst_ref=sib[t], send_sem=send.at[t],
            recv_sem=recv.at[t], device_id=(x, y, 1 - c), device_id_type=MESH))
    return cps


def _reduce_stage1(grads, tag):
    n = len(grads)

    def body(*refs):
        cps = _stage1_copies(refs[:n], refs[n:2 * n], *refs[2 * n:])
        for cp in cps:
            cp.start()
        for cp in cps:
            cp.wait()

    return pl.pallas_call(
        body, name="reduce_stage1_" + tag, in_specs=[HBM_SPEC] * n, out_specs=[HBM_SPEC] * n,
        out_shape=[jax.ShapeDtypeStruct((N_CHIPS, g.shape[1] // 2, g.shape[2]), F32) for g in grads],
        scratch_shapes=[pltpu.SemaphoreType.DMA((n,)), pltpu.SemaphoreType.DMA((n,))],
    )(*grads)


def _split_start(copies_fn, srcs, land_shapes, n_sems, tag):
    n = len(srcs)

    def body(*refs):
        send, recv = refs[2 * n:2 * n + 2]
        for cp in copies_fn(refs[:n], refs[3 * n + 2:4 * n + 2], send, recv):
            cp.start()
        refs[4 * n + 2][...] = jnp.zeros_like(refs[4 * n + 2])

    lands = [lax.empty(shp, dt) for shp, dt in land_shapes]
    res = pl.pallas_call(
        body, name=tag,
        in_specs=[HBM_SPEC] * (2 * n),
        out_specs=[SEM_SPEC] * 2 + [HBM_SPEC] * (2 * n) + [pl.BlockSpec(memory_space=pltpu.VMEM)],
        out_shape=[pltpu.SemaphoreType.DMA((n_sems,)), pltpu.SemaphoreType.DMA((n_sems,))]
        + [jax.ShapeDtypeStruct(p.shape, p.dtype) for p in srcs]
        + [jax.ShapeDtypeStruct(a.shape, a.dtype) for a in lands] + [jax.ShapeDtypeStruct((8, 128), F32)],
        input_output_aliases={t: 2 + t for t in range(2 * n)},
        compiler_params=pltpu.CompilerParams(has_side_effects=pltpu.SideEffectType.DATAFLOW_SIDE_EFFECTING),
    )(*[pltpu.with_memory_space_constraint(p, pltpu.HBM) for p in srcs],
      *[pltpu.with_memory_space_constraint(a, pltpu.HBM) for a in lands])
    return res[:2], res[2:2 + n], res[2 + n:2 + 2 * n], res[-1]


def _split_wait(copies_fn, sems, srcs_thru, lands_thru, after, tag):
    n = len(srcs_thru)

    def body(*refs):
        for cp in copies_fn(refs[:n], refs[n:2 * n], refs[2 * n], refs[2 * n + 1]):
            cp.wait()

    res = pl.pallas_call(
        body, name=tag,
        in_specs=[HBM_SPEC] * (2 * n) + [SEM_SPEC] * 2 + [ANY_SPEC],
        out_specs=[HBM_SPEC] * (2 * n),
        out_shape=[jax.ShapeDtypeStruct(a.shape, a.dtype) for a in list(srcs_thru) + list(lands_thru)],
        input_output_aliases={t: t for t in range(2 * n)},
        compiler_params=pltpu.CompilerParams(has_side_effects=pltpu.SideEffectType.DATAFLOW_SIDE_EFFECTING),
    )(*srcs_thru, *lands_thru, *sems, after)
    return res[:n], res[n:]


def _stage2_copies(ps, rcv, send, recv):
    x, y, c, me, chips = _place()
    return [pltpu.make_async_remote_copy(
        src_ref=ps[t].at[2 * px + py], dst_ref=rcv[t].at[kk], send_sem=send.at[t * 3 + kk],
        recv_sem=recv.at[t * 3 + kk], device_id=(px, py, c), device_id_type=MESH)
        for t in range(len(ps)) for kk, (px, py) in enumerate(chips)]


def _reduce_stage3(reduced, tag):
    n = len(reduced)

    def body(*refs):
        outs = refs[n:2 * n]
        send, recv = refs[2 * n:]
        x, y, c, me, chips = _place()
        cps = []
        for t in range(n):
            rows = outs[t].shape[0] // 2
            mine = outs[t].at[pl.ds(c * rows, rows), :]
            cp = pltpu.make_async_remote_copy(src_ref=mine, dst_ref=mine, send_sem=send.at[t], recv_sem=recv.at[t],
                                              device_id=(x, y, 1 - c), device_id_type=MESH)
            cp.start()
            cps.append(cp)
        for cp in cps:
            cp.wait()

    return pl.pallas_call(
        body, name="reduce_stage3_" + tag, in_specs=[HBM_SPEC] * n, out_specs=[HBM_SPEC] * n,
        out_shape=[jax.ShapeDtypeStruct(r.shape, r.dtype) for r in reduced],
        input_output_aliases={t: t for t in range(n)},
        scratch_shapes=[pltpu.SemaphoreType.DMA((n,)), pltpu.SemaphoreType.DMA((n,))],
    )(*reduced)


def _allreduce_small(v):
    rows, cols = v.shape

    def body(v_ref, o_ref, buf, send, recv, loc):
        x, y, c, me, chips = _place()
        mine = 4 * x + 2 * y + c
        lc = pltpu.make_async_copy(v_ref, buf.at[mine], loc)
        lc.start()
        peers = []
        for fx in range(2):
            for fy in range(2):
                for fc in range(2):
                    if fx or fy or fc:
                        peers.append((fx, fy, fc))
        cps = []
        for kk, (fx, fy, fc) in enumerate(peers):
            to = (x ^ fx, y ^ fy, c ^ fc)
            cp = pltpu.make_async_remote_copy(src_ref=v_ref, dst_ref=buf.at[mine], send_sem=send.at[kk], recv_sem=recv.at[kk],
                                              device_id=to, device_id_type=MESH)
            cp.start()
            cps.append((cp, to))
        for kk, (cp, to) in enumerate(cps):
            src = 4 * to[0] + 2 * to[1] + to[2]
            pltpu.make_async_remote_copy(src_ref=v_ref, dst_ref=buf.at[src], send_sem=send.at[kk], recv_sem=recv.at[kk],
                                         device_id=to, device_id_type=MESH).wait_recv()
        for cp, _ in cps:
            cp.wait_send()
        lc.wait()
        acc = buf[0]
        for d in range(1, 8):
            acc = acc + buf[d]
        o_ref[...] = acc

    return pl.pallas_call(
        body, name="allreduce_small", in_specs=[pl.BlockSpec(memory_space=pltpu.VMEM)],
        out_specs=pl.BlockSpec(memory_space=pltpu.VMEM), out_shape=jax.ShapeDtypeStruct((rows, cols), F32),
        scratch_shapes=[pltpu.VMEM((8, rows, cols), F32), pltpu.SemaphoreType.DMA((7,)), pltpu.SemaphoreType.DMA((7,)),
                        pltpu.SemaphoreType.DMA],
        compiler_params=pltpu.CompilerParams(vmem_limit_bytes=VMEM_LIMIT_V7X),
    )(v)


def _pad_w_in(w):
    z = lambda n: jnp.zeros(w.shape[:-1] + (n,), w.dtype)
    return jnp.concatenate([w[..., 0:384], z(64), w[..., 384:416], z(32), w[..., 416:1824],
                            w[..., 1824:1830], z(58), w[..., 400:416], w[..., 384:400], z(32)], axis=-1)


def _unpad_w_in(g):
    x1 = g[..., 448:464] + g[..., Z_F + 80:Z_F + 96]
    x2 = g[..., 464:480] + g[..., Z_F + 64:Z_F + 80]
    return jnp.concatenate([g[..., 0:384], x1, x2, g[..., 512:1920], g[..., 1920:1926]], axis=-1)


def _block_diag(pw):
    out = jnp.zeros((POOL_W, POOL_W), pw.dtype)
    for g in range(4):
        out = out.at[g * 64:(g + 1) * 64, g * 64:(g + 1) * 64].set(pw[g])
    return out


def _rope_tables(s):
    inv_freq = ROPE_THETA ** (-jnp.arange(0, ROPE, 2, dtype=F32) / ROPE)
    ang = jnp.arange(s, dtype=jnp.int32).astype(F32)[:, None] * inv_freq[None, :]
    cos, sin = jnp.cos(ang), jnp.sin(ang)
    zero = lambda n: jnp.zeros((s, n), F32)
    ck = jnp.concatenate([zero(NOPE), cos, cos, zero(DA - NOPE - ROPE)], axis=1)
    sk = jnp.concatenate([zero(NOPE), -sin, sin, zero(DA - NOPE - ROPE)], axis=1)
    cq = jnp.concatenate([jnp.ones((s, NOPE), F32), cos, cos, zero(DA - NOPE - ROPE)], axis=1) * SCALE_MLA
    return dict(cq=cq, sq=sk * SCALE_MLA, ck=ck, sk=sk)


def _mix_fwd(l, x1, wts, sm, tabs):
    z, h2 = _norm_mm(x1, 0, sm["mix_norm"][l], wts["w_in"][l], name=f"mix_in_{l}")
    qa, qn = _mla_q_prep(z, sm["q_a_norm"][l], wts["wq_a"][l], wts["wq_b"][l], tabs["cq"], tabs["sq"], name=f"mla_q_{l}")
    ka, va, kvn = _mla_kv_prep(z, sm["kv_a_norm"][l], wts["wk"][l], wts["wv"][l], tabs["ck"], tabs["sk"], name=f"mla_kv_{l}")
    oa, lse_a = _attn_fwd(qa, ka, va, VDIM, name=f"mla_attn_{l}")

    bd = _block_diag(wts["pool_w"][l]).astype(BF16)
    yb, pooled = _pool_fwd(z, Z_POOL // POOL_W, bd, sm["pool_scale"][l], name=f"pool_{l}")

    fb = jnp.pad(sm["fox_b_f"][l], (0, 8 - H)).reshape(8, 1)
    ft, c3t = _gate_fwd(z, Z_F // DA, fb, name=f"fox_gate_{l}")
    fqa, fka, fva = _fox_prep(z, c3t, name=f"fox_prep_{l}")
    oc, lse_c = _attn_fwd(fqa, fka, fva, FOX_D, name=f"fox_attn_{l}")

    x2, cat = _mix_out(oa, yb, oc, wts["w_out"][l], x1, name=f"mix_out_{l}")
    saved = dict(z=z, h2=h2, qn=qn, kvn=kvn, qa=qa, ka=ka, va=va, oa=oa, lse_a=lse_a, bd=bd, pooled=pooled,
                 fqa=fqa, fka=fka, fva=fva, ft=ft, fb=fb, oc=oc, lse_c=lse_c, cat=cat)
    return x2, saved


def _mix_bwd(l, x1, dx2, sv, wts, sm, tabs, tok=None):
    s = x1.shape[0]
    g = {}
    dx2b = (dx2 if tok is None else dx2 + tok).astype(BF16)
    g["w_out"] = _mm(sv["cat"], dx2b, "tn", name=f"d_w_out_{l}", tm=1024, tn=1024, tk=DW_TOKENS)
    doa, doc, dyb, dl_a, dl_c = _mix_out_bwd(dx2b, wts["w_out"][l], sv["oa"], sv["oc"], name=f"mix_out_bwd_{l}")

    dfqa, dfka, dfva, dcq, dck = _attn_bwd(sv["fqa"], sv["fka"], sv["fva"], doc, sv["lse_c"], dl_c, True, name=f"fox_attn_bwd_{l}")
    dfox = _fox_bwd_prep(dfqa, dfka, dfva, name=f"fox_bwd_prep_{l}")
    dc = jnp.pad(dcq.reshape(H, s) + dck.reshape(H, s), ((0, 8 - H), (0, 0)))
    dft, dfb = _gate_bwd(sv["ft"], sv["fb"], dc, name=f"fox_gate_bwd_{l}")
    g["fox_b_f"] = dfb[:H, 0]

    dq, dys, g["pool_scale"] = _pool_bwd_a(dyb, sv["pooled"], sv["bd"], sm["pool_scale"][l], name=f"pool_bwd_a_{l}")
    du = _pool_bwd_b(dq, name=f"pool_bwd_b_{l}")
    dbd = _mm(sv["pooled"], dys, "tn", name=f"d_pool_w_{l}")
    g["pool_w"] = jnp.stack([dbd[i * 64:(i + 1) * 64, i * 64:(i + 1) * 64] for i in range(4)])

    dqa_, dka_, dva_ = _attn_bwd(sv["qa"], sv["ka"], sv["va"], doa, sv["lse_a"], dl_a, False, name=f"mla_attn_bwd_{l}")
    dqab, dkv, dz3, dz15 = _mla_bwd_prep(dqa_, dka_, dva_, dft, tabs["cq"], tabs["sq"], tabs["ck"], tabs["sk"],
                                         name=f"mla_bwd_prep_{l}")
    wq_ab = jnp.concatenate([wts["wq_a"][l], wts["wq_b"][l]], axis=1)
    wkv = jnp.concatenate([wts["wk"][l], wts["wv"][l]], axis=1)
    dwq = _mm(sv["qn"], dqab, "tn", name=f"d_w_q_b_{l}", tn=768, tk=DW_TOKENS).reshape(Q_RANK, 2, H, DA)
    dwkv = _mm(sv["kvn"], dkv, "tn", name=f"d_w_kv_b_{l}", tn=768, tk=DW_TOKENS).reshape(KV_RANK, 2, H, DA)
    da, db = dwq[:, 0], dwq[:, 1]
    swapped = jnp.concatenate([jnp.zeros((Q_RANK, H, NOPE), F32), db[..., NOPE + HALF_ROPE:NOPE + ROPE],
                               db[..., NOPE:NOPE + HALF_ROPE]], axis=-1)
    g["w_q_b"] = (da[..., :NOPE + ROPE] + swapped).reshape(Q_RANK, H * (NOPE + ROPE))
    g["w_kv_b"] = jnp.concatenate([dwkv[:, 0, :, :NOPE], dwkv[:, 1, :, :VDIM]], axis=-1).reshape(KV_RANK, H * (NOPE + VDIM))
    dqn = _mm(dqab, wq_ab, "nt", name=f"d_qn_{l}", tk=2 * H * DA)
    dkvn = _mm(dkv, wkv, "nt", name=f"d_kvn_{l}", tk=2 * H * DA)
    dqa, g["q_a_norm"] = _rmsnorm_bwd(sv["z"], Z_QA // Q_RANK, sm["q_a_norm"][l], dqn, name=f"q_a_norm_bwd_{l}")
    dkva, g["kv_a_norm"] = _rmsnorm_bwd(sv["z"], Z_KVA // KV_RANK, sm["kv_a_norm"][l], dkvn, name=f"kv_a_norm_bwd_{l}")

    dz = jnp.concatenate([dqa.astype(BF16), dkva.astype(BF16), dz3, du.astype(BF16), dfox, dz15], axis=1)
    g["w_in"] = _mm(sv["h2"], dz, "tn", name=f"d_w_in_{l}", tm=1024, tn=1024, tk=DW_TOKENS)
    dh2 = _mm(dz, wts["w_in"][l], "nt", name=f"d_h2_{l}", tn=1024, tk=NZ)
    dx1, g["mix_norm"] = _rmsnorm_bwd(x1, 0, sm["mix_norm"][l], dh2, dx2, name=f"mix_norm_bwd_{l}")
    return dx1, g


DW_TOKENS = 2048


def _local_step(x, target, wts, sm, late_weights=None, grads_ready=None):
    s = x.shape[0]
    tabs = _rope_tables(s)
    acts = []
    xs = x
    for l in range(DEPTH):
        x1, gu1, act1 = _ffn_fwd(xs, sm["ffn1_norm"][l], wts["ffn1_w_gu"][l], wts["ffn1_w_d2"][l], name=f"ffn1_fwd_{l}")
        x2, sv = _mix_fwd(l, x1, wts, sm, tabs)
        if l == 0 and late_weights is not None:
            late_weights(x2)
        x3, gu2, act2 = _ffn_fwd(x2, sm["ffn2_norm"][l], wts["ffn2_w_gu"][l], wts["ffn2_w_d2"][l], name=f"ffn2_fwd_{l}")
        acts.append((xs, gu1, act1, x1, sv, x2, gu2, act2))
        xs = x3
    dx, g_final, loss = _loss_head(xs, sm["final_norm"], target, name="loss_head")
    grads = [dict() for _ in range(DEPTH)]
    for l in reversed(range(DEPTH)):
        x0, gu1, act1, x1, sv, x2, gu2, act2 = acts[l]
        g = grads[l]
        dx, dgu, hh, dy, g["ffn2_norm"] = _ffn_bwd(x2, dx, gu2, sm["ffn2_norm"][l], wts["ffn2_w_gu"][l], wts["ffn2_w_d2"][l],
                                                   name=f"ffn2_bwd_{l}")
        g["ffn2_w_down"] = _mm(act2, dy, "tn", name=f"d_ffn2_w_down_{l}", tm=FF_SHARD, tn=1024, tk=DW_TOKENS)
        g["ffn2_w_gu"] = _mm(hh, dgu, "tn", name=f"d_ffn2_w_gu_{l}", tm=1024, tn=FF_SHARD, tk=DW_TOKENS, n_major_out=True)
        tok = None
        if grads_ready is not None:
            sm, tok = grads_ready(l, "ffn2", g, sm)
        dx, gm = _mix_bwd(l, x1, dx, sv, wts, sm, tabs, tok)
        g.update(gm)
        if grads_ready is not None:
            sm, _ = grads_ready(l, "mix", g, sm)
        dx, dgu, hh, dy, g["ffn1_norm"] = _ffn_bwd(x0, dx, gu1, sm["ffn1_norm"][l], wts["ffn1_w_gu"][l], wts["ffn1_w_d2"][l],
                                                   name=f"ffn1_bwd_{l}")
        g["ffn1_w_down"] = _mm(act1, dy, "tn", name=f"d_ffn1_w_down_{l}", tm=FF_SHARD, tn=1024, tk=DW_TOKENS)
        g["ffn1_w_gu"] = _mm(hh, dgu, "tn", name=f"d_ffn1_w_gu_{l}", tm=1024, tn=FF_SHARD, tk=DW_TOKENS, n_major_out=True)
        if grads_ready is not None:
            sm, _ = grads_ready(l, "ffn1", g, sm)
    return loss, dx, grads, g_final


BIG = ["ffn1_w_gu", "ffn1_w_down", "w_in", "w_q_b", "w_kv_b", "w_out", "ffn2_w_gu", "ffn2_w_down"]
SMALL = ["ffn1_norm", "mix_norm", "q_a_norm", "kv_a_norm", "pool_w", "pool_scale", "fox_b_f", "ffn2_norm"]
SMALL_ROWS = 48


WEIGHT_VIEWS = ["ffn1_w_gu", "ffn1_w_d2", "w_in", "wq_a", "wq_b", "wk", "wv", "w_out", "ffn2_w_gu", "ffn2_w_d2"]


def _prepare_weights(gathered, wts):
    for (nm, l), w in gathered.items():
        if nm in ("ffn1_w_gu", "ffn2_w_gu"):
            wts[nm][l] = w
        elif nm in ("ffn1_w_down", "ffn2_w_down"):
            wts[nm[:5] + "w_d2"][l] = w.reshape(2, FF_SHARD, D)
        elif nm in ("w_in", "w_out"):
            wts[nm][l] = w.reshape(D, -1)
        elif nm == "w_q_b":
            wq = jnp.moveaxis(w, 0, 1).reshape(Q_RANK, H, NOPE + ROPE)
            zq = lambda n: jnp.zeros((Q_RANK, H, n), BF16)
            wts["wq_a"][l] = jnp.concatenate([wq, zq(DA - NOPE - ROPE)], axis=-1).reshape(Q_RANK, H * DA)
            wts["wq_b"][l] = jnp.concatenate([zq(NOPE), wq[..., NOPE + HALF_ROPE:], wq[..., NOPE:NOPE + HALF_ROPE],
                                              zq(DA - NOPE - ROPE)], axis=-1).reshape(Q_RANK, H * DA)
        else:
            wkv = jnp.moveaxis(w, 0, 1).reshape(KV_RANK, H, NOPE + VDIM)
            zk = jnp.zeros((KV_RANK, H, DA - NOPE), BF16)
            wts["wk"][l] = jnp.concatenate([wkv[..., :NOPE], zk], axis=-1).reshape(KV_RANK, H * DA)
            wts["wv"][l] = jnp.concatenate([wkv[..., NOPE:], zk], axis=-1).reshape(KV_RANK, H * DA)


def _chip_major(name, g):
    if name in ("ffn1_w_gu", "ffn2_w_gu"):
        return g
    if name in ("ffn1_w_down", "ffn2_w_down", "w_in", "w_out"):
        return g.reshape(N_CHIPS, g.shape[0] // N_CHIPS, g.shape[1])
    return jnp.moveaxis(g.reshape(g.shape[0], N_CHIPS, g.shape[1] // N_CHIPS), 1, 0)


def _pack_small(grads, g_final, loss):
    parts = []
    for l in range(DEPTH):
        for nm in SMALL:
            parts.append(grads[l][nm].reshape(-1))
    parts.append(g_final.reshape(-1))
    parts.append(loss.reshape(1))
    flat = jnp.concatenate(parts)
    return jnp.pad(flat, (0, SMALL_ROWS * D - flat.shape[0])).reshape(SMALL_ROWS, D)


def _unpack_small(packed, params):
    flat = packed.reshape(-1)
    out = {nm: [] for nm in SMALL}
    off = 0
    for l in range(DEPTH):
        for nm in SMALL:
            shp = params[nm].shape[1:]
            n = int(np.prod(shp))
            out[nm].append(flat[off:off + n].reshape(shp))
            off += n
    res = {nm: jnp.stack(v) for nm, v in out.items()}
    res["final_norm"] = flat[off:off + D]
    return res, flat[off + D]


def _update(name, w, g, m, v):
    shp = w.shape
    if w.ndim == 1:
        view = (1, shp[0])
    elif w.size <= 65536:
        view = (shp[0], w.size // shp[0])
    else:
        view = (w.size // shp[-1], shp[-1])
    tr = view[0]
    for cand in (512, 352, 256, 128):
        if view[0] % cand == 0 and view[0] > cand:
            tr = cand
            break
    d, mn, vn = _adamw(w.reshape(view), g.reshape(view), m.reshape(view), v.reshape(view), name="adamw_" + name, tr=tr)
    return d.reshape(shp), mn.reshape(shp), vn.reshape(shp)


WEIGHTS = ['ffn1_norm', 'ffn1_w_gu', 'ffn1_w_down', 'mix_norm', 'w_in', 'q_a_norm', 'w_q_b', 'kv_a_norm', 'w_kv_b', 'pool_w',
           'pool_scale', 'fox_b_f', 'w_out', 'ffn2_norm', 'ffn2_w_gu', 'ffn2_w_down', 'final_norm']


def kernel(x, ffn1_norm, ffn1_w_gu, ffn1_w_down, mix_norm, w_in, q_a_norm, w_q_b, kv_a_norm, w_kv_b, pool_w, pool_scale, fox_b_f, w_out, ffn2_norm, ffn2_w_gu, ffn2_w_down, final_norm, loss_target, m_ffn1_norm, m_ffn1_w_gu, m_ffn1_w_down, m_mix_norm, m_w_in, m_q_a_norm, m_w_q_b, m_kv_a_norm, m_w_kv_b, m_pool_w, m_pool_scale, m_fox_b_f, m_w_out, m_ffn2_norm, m_ffn2_w_gu, m_ffn2_w_down, m_final_norm, v_ffn1_norm, v_ffn1_w_gu, v_ffn1_w_down, v_mix_norm, v_w_in, v_q_a_norm, v_w_q_b, v_kv_a_norm, v_w_kv_b, v_pool_w, v_pool_scale, v_fox_b_f, v_w_out, v_ffn2_norm, v_ffn2_w_gu, v_ffn2_w_down, v_final_norm):
    params = dict(ffn1_norm=ffn1_norm, ffn1_w_gu=ffn1_w_gu, ffn1_w_down=ffn1_w_down, mix_norm=mix_norm, w_in=w_in, q_a_norm=q_a_norm,
                  w_q_b=w_q_b, kv_a_norm=kv_a_norm, w_kv_b=w_kv_b, pool_w=pool_w, pool_scale=pool_scale, fox_b_f=fox_b_f, w_out=w_out,
                  ffn2_norm=ffn2_norm, ffn2_w_gu=ffn2_w_gu, ffn2_w_down=ffn2_w_down, final_norm=final_norm)
    mom = dict(ffn1_norm=m_ffn1_norm, ffn1_w_gu=m_ffn1_w_gu, ffn1_w_down=m_ffn1_w_down, mix_norm=m_mix_norm, w_in=m_w_in,
               q_a_norm=m_q_a_norm, w_q_b=m_w_q_b, kv_a_norm=m_kv_a_norm, w_kv_b=m_w_kv_b, pool_w=m_pool_w, pool_scale=m_pool_scale,
               fox_b_f=m_fox_b_f, w_out=m_w_out, ffn2_norm=m_ffn2_norm, ffn2_w_gu=m_ffn2_w_gu, ffn2_w_down=m_ffn2_w_down,
               final_norm=m_final_norm)
    var = dict(ffn1_norm=v_ffn1_norm, ffn1_w_gu=v_ffn1_w_gu, ffn1_w_down=v_ffn1_w_down, mix_norm=v_mix_norm, w_in=v_w_in,
               q_a_norm=v_q_a_norm, w_q_b=v_w_q_b, kv_a_norm=v_kv_a_norm, w_kv_b=v_w_kv_b, pool_w=v_pool_w, pool_scale=v_pool_scale,
               fox_b_f=v_fox_b_f, w_out=v_w_out, ffn2_norm=v_ffn2_norm, ffn2_w_gu=v_ffn2_w_gu, ffn2_w_down=v_ffn2_w_down,
               final_norm=v_final_norm)

    shard = {}
    for nm in BIG:
        w = _pad_w_in(params[nm]) if nm == "w_in" else params[nm]
        for l in range(DEPTH):
            shard[(nm, l)] = w[l].astype(BF16)
    first = [(nm, 0) for nm in BIG if not nm.startswith("ffn2")]
    rest = [k for k in shard if k not in first]
    wts = {nm: [None] * DEPTH for nm in WEIGHT_VIEWS}
    wts["pool_w"] = params["pool_w"]
    got = _gather_blocking([shard[k] for k in first])
    _prepare_weights(dict(zip(first, got)), wts)
    sems, src_thru, land_thru, token = _gather_start([shard[k] for k in rest], got[0])
    sm = dict(params)
    sm["ffn1_norm"] = params["ffn1_norm"] + token[0, 0]

    def late_weights(x2):
        lands = _gather_forward(_gather_wait(sems, src_thru, land_thru, x2))
        _prepare_weights(dict(zip(rest, lands)), wts)

    pos = _position()
    flight = {}

    groups = {"l1": (1, BIG), "l0a": (0, [nm for nm in BIG if not nm.startswith("ffn1")]),
              "l0b": (0, [nm for nm in BIG if nm.startswith("ffn1")])}
    pending = {}

    def to_chips(key, full, sib):
        psum = _sum2_bf16(pos, full, sib, name=f"chip_sum_{key}")
        s2 = _split_start(_stage2_copies, psum, [((3,) + p.shape[1:], p.dtype) for p in psum], 3 * len(psum),
                          f"reduce_stage2_start_{key}")
        flight[key] = (full, sib, s2)
        return s2[3][0, 0]

    def grads_ready(l, stage, g, sm_now):
        behind, tok = None, None
        if (l, stage) == (1, "ffn1"):
            full = [_chip_major(nm, g[nm]) for nm in BIG]
            pending["l1"] = _split_start(_stage1_copies, full, [((N_CHIPS, f.shape[1] // 2, f.shape[2]), F32) for f in full],
                                         len(full), "reduce_stage1_start_l1")
            behind, tok = "ffn2_norm", pending["l1"][3][0, 0]
        elif (l, stage) == (0, "ffn2"):
            sems1, full_thru, sib_land, _ = pending["l1"]
            full, sib = _split_wait(_stage1_copies, sems1, full_thru, sib_land, g["ffn2_w_down"], "reduce_stage1_wait_l1")
            tok = to_chips("l1", full, sib)
        elif l == 0:
            key = "l0a" if stage == "mix" else "l0b"
            full = [_chip_major(nm, g[nm]) for nm in groups[key][1]]
            behind, tok = "ffn1_norm", to_chips(key, full, _reduce_stage1(full, key))
        if behind is None:
            return sm_now, tok
        sm_next = dict(sm_now)
        sm_next[behind] = sm_now[behind] + tok
        return sm_next, tok

    loss, dx, grads, g_final = _local_step(x[0], loss_target[0], wts, sm, late_weights, grads_ready)

    def view2d(a):
        return a.reshape(a.size // a.shape[-1], a.shape[-1])

    after = flight["l0b"][2][3]
    done = {nm: None for nm in BIG}
    for key in ("l1", "l0a", "l0b"):
        l, names = groups[key]
        full, sib, (sems2, ps_thru, lands2, _) = flight[key]
        _, recv = _split_wait(_stage2_copies, sems2, ps_thru, lands2, after, f"reduce_stage2_wait_{key}")
        whole = _reduce_stage3(_sum5(pos, full, sib, recv, name=f"grad_sum_{key}"), key)
        for nm, g_l in zip(names, whole):
            if nm == "w_in":
                g_l = _unpad_w_in(g_l)
            tr = max(t for t in (512, 352, 256, 128) if g_l.shape[0] % t == 0)
            done[nm] = _adamw_layer(view2d(params[nm]), g_l, view2d(mom[nm]), view2d(var[nm]), l, done[nm],
                                    name=f"adamw_{nm}_{l}", tr=tr)
        after = done[names[-1]][0][-8:, 0:128]
        if key == "l1":
            small_g, loss = _unpack_small(_allreduce_small(_pack_small(grads, g_final, loss)), params)
            after = after + small_g["final_norm"][0]
    gw, delta, new_m, new_v = dict(small_g), {}, {}, {}
    for nm in BIG:
        delta[nm], new_m[nm], new_v[nm], gw[nm] = [a.reshape(params[nm].shape) for a in done[nm]]
    for nm in small_g:
        delta[nm], new_m[nm], new_v[nm] = _update(nm, params[nm], gw[nm], mom[nm], var[nm])
    return (loss, dx[None], *[gw[n] for n in WEIGHTS], *[delta[n] for n in WEIGHTS], *[new_m[n] for n in WEIGHTS],
            *[new_v[n] for n in WEIGHTS])
```

```python
import functools
import math

import jax
import jax.numpy as jnp
import numpy as np
from jax import lax
from jax.experimental import pallas as pl
from jax.experimental.pallas import tpu as pltpu

F32 = jnp.float32
BF16 = jnp.bfloat16
MESH = pl.DeviceIdType.MESH
HBM_SPEC = pl.BlockSpec(memory_space=pltpu.HBM)

D = 1024
DEPTH = 2
D_FF = 2816
FF_SHARD = 1408
N_CHIPS = 4
H = 6
NOPE, ROPE, VDIM = 64, 32, 64
HALF_ROPE = ROPE // 2
Q_RANK, KV_RANK = 256, 128
POOL_W = 256
FOX_D = 64
N_IN = 1830
NZ = 2048
ROPE_THETA = 10000.0
EPS = 1e-6
POOL_HALO = 16
Z_QA, Z_KVA, Z_KR, Z_POOL, Z_FOX, Z_F = 0, 256, 384, 512, 768, 1920

ADAM_LR, ADAM_B1, ADAM_B2, ADAM_EPS, ADAM_WD, ADAM_STEP = 0.001, 0.9, 0.999, 1e-08, 0.01, 10

VMEM_LIMIT_V7X = 56 * 1024 * 1024


def _cp(sem=None, vmem=VMEM_LIMIT_V7X):
    return pltpu.CompilerParams(dimension_semantics=sem, vmem_limit_bytes=vmem)


def _sigmoid(x):
    return 0.5 * jnp.tanh(0.5 * x) + 0.5


def _dot(a, b, dims):
    return lax.dot_general(a, b, (dims, ((), ())), preferred_element_type=F32)


NN = ((1,), (0,))
NT = ((1,), (1,))
TN = ((0,), (0,))


def _mm(a, b, mode, *, name, out_dtype=F32, add=None, alpha=None, tm=512, tn=512, tk=512, n_major_out=False):
    if mode == "nn":
        (m, k), (k2, n) = a.shape, b.shape
    elif mode == "nt":
        (m, k), (n, k2) = a.shape, b.shape
    else:
        (k, m), (k2, n) = a.shape, b.shape
    assert k == k2
    tm, tn, tk = min(tm, m), min(tn, n), min(tk, k)
    assert m % tm == 0 and n % tn == 0 and k % tk == 0, (name, m, n, k, tm, tn, tk)
    nk = k // tk
    dims = {"nn": NN, "nt": NT, "tn": TN}[mode]
    a_spec = pl.BlockSpec((tk, tm), lambda i, j, kk: (kk, i)) if mode == "tn" else pl.BlockSpec((tm, tk), lambda i, j, kk: (i, kk))
    b_spec = pl.BlockSpec((tn, tk), lambda i, j, kk: (j, kk)) if mode == "nt" else pl.BlockSpec((tk, tn), lambda i, j, kk: (kk, j))
    in_specs = [a_spec, b_spec]
    args = [a, b]
    if add is not None:
        in_specs.append(pl.BlockSpec((tm, tn), lambda i, j, kk: (i, j)))
        args.append(add)
    if n_major_out:
        out_shape = jax.ShapeDtypeStruct((n // tn, m, tn), out_dtype)
        out_spec = pl.BlockSpec((None, tm, tn), lambda i, j, kk: (j, i, 0))
    else:
        out_shape = jax.ShapeDtypeStruct((m, n), out_dtype)
        out_spec = pl.BlockSpec((tm, tn), lambda i, j, kk: (i, j))

    def body(*refs):
        a_ref, b_ref = refs[0], refs[1]
        add_ref = refs[2] if add is not None else None
        o_ref, acc = refs[-2], refs[-1]
        kk = pl.program_id(2)

        @pl.when(kk == 0)
        def _():
            acc[...] = jnp.zeros_like(acc)

        acc[...] += _dot(a_ref[...].astype(BF16), b_ref[...].astype(BF16), dims)

        @pl.when(kk == nk - 1)
        def _():
            r = acc[...]
            if alpha is not None:
                r = r * alpha
            if add_ref is not None:
                r = r + add_ref[...].astype(F32)
            o_ref[...] = r.astype(out_dtype)

    return pl.pallas_call(
        body, name=name, grid=(m // tm, n // tn, nk), in_specs=in_specs, out_specs=out_spec, out_shape=out_shape,
        scratch_shapes=[pltpu.VMEM((tm, tn), F32)],
        compiler_params=_cp(("parallel", "parallel", "arbitrary")),
    )(*args)


def _norm_mm(x, col_block, gain, w, *, name, tm=512):
    s = x.shape[0]
    k, n = w.shape
    tm = min(tm, s)

    def body(x_ref, g_ref, w_ref, z_ref, h_ref):
        xv = x_ref[...]
        r = lax.rsqrt(jnp.mean(xv * xv, axis=-1, keepdims=True) + EPS)
        hv = (xv * r * g_ref[...]).astype(BF16)
        h_ref[...] = hv
        z_ref[...] = _dot(hv, w_ref[...], NN)

    return pl.pallas_call(
        body, name=name, grid=(s // tm,),
        in_specs=[pl.BlockSpec((tm, k), lambda i: (i, col_block)), pl.BlockSpec((1, k), lambda i: (0, 0)),
                  pl.BlockSpec((k, n), lambda i: (0, 0))],
        out_specs=[pl.BlockSpec((tm, n), lambda i: (i, 0)), pl.BlockSpec((tm, k), lambda i: (i, 0))],
        out_shape=[jax.ShapeDtypeStruct((s, n), F32), jax.ShapeDtypeStruct((s, k), BF16)],
        compiler_params=_cp(("parallel",)),
    )(x, gain.reshape(1, k), w)


def _rmsnorm_bwd(x, col_block, gain, dh, dres=None, *, name, tm=512):
    s = x.shape[0]
    k = gain.shape[-1]
    tm = min(tm, s)

    def body(*refs):
        x_ref, g_ref, dh_ref = refs[0], refs[1], refs[2]
        dres_ref = refs[3] if dres is not None else None
        dx_ref, dg_ref = refs[-2], refs[-1]
        xv = x_ref[...]
        r = lax.rsqrt(jnp.mean(xv * xv, axis=-1, keepdims=True) + EPS)
        dhv = dh_ref[...].astype(F32)
        a = dhv * g_ref[...]
        dx = r * a - xv * (r * r * r) * jnp.mean(a * xv, axis=-1, keepdims=True)
        if dres_ref is not None:
            dx = dx + dres_ref[...]
        dx_ref[...] = dx

        @pl.when(pl.program_id(0) == 0)
        def _():
            dg_ref[...] = jnp.zeros_like(dg_ref)

        dg_ref[...] += jnp.sum(dhv * xv * r, axis=0, keepdims=True)

    in_specs = [pl.BlockSpec((tm, k), lambda i: (i, col_block)), pl.BlockSpec((1, k), lambda i: (0, 0)),
                pl.BlockSpec((tm, k), lambda i: (i, 0))]
    args = [x, gain.reshape(1, k), dh]
    if dres is not None:
        in_specs.append(pl.BlockSpec((tm, k), lambda i: (i, 0)))
        args.append(dres)
    dx, dg = pl.pallas_call(
        body, name=name, grid=(s // tm,), in_specs=in_specs,
        out_specs=[pl.BlockSpec((tm, k), lambda i: (i, 0)), pl.BlockSpec((1, k), lambda i: (0, 0))],
        out_shape=[jax.ShapeDtypeStruct((s, k), F32), jax.ShapeDtypeStruct((1, k), F32)],
        compiler_params=_cp(("arbitrary",)),
    )(*args)
    return dx, dg.reshape(k)


def _ffn_fwd(x, gain, w_gu4, w_d2, *, name, tm=256):
    s = x.shape[0]
    tm = min(tm, s)

    def body(x_ref, g_ref, wgu_ref, wd_ref, xo_ref, dgu_ref, act_ref):
        xv = x_ref[...]
        r = lax.rsqrt(jnp.mean(xv * xv, axis=-1, keepdims=True) + EPS)
        hv = (xv * r * g_ref[...]).astype(BF16)
        y = jnp.zeros((tm, D), F32)
        for j in range(2):
            g = _dot(hv, wgu_ref[j], NN)
            u = _dot(hv, wgu_ref[2 + j], NN)
            sg = _sigmoid(g)
            silu = g * sg
            dgu_ref[:, j * FF_SHARD:(j + 1) * FF_SHARD] = (u * (sg * (1.0 + g * (1.0 - sg)))).astype(BF16)
            dgu_ref[:, D_FF + j * FF_SHARD:D_FF + (j + 1) * FF_SHARD] = silu.astype(BF16)
            act = (silu * u).astype(BF16)
            act_ref[:, j * FF_SHARD:(j + 1) * FF_SHARD] = act
            y = y + _dot(act, wd_ref[j], NN)
        xo_ref[...] = xv + 0.5 * y

    row = lambda i: (i, 0)
    return pl.pallas_call(
        body, name=name, grid=(s // tm,),
        in_specs=[pl.BlockSpec((tm, D), row), pl.BlockSpec((1, D), lambda i: (0, 0)),
                  pl.BlockSpec((N_CHIPS, D, FF_SHARD), lambda i: (0, 0, 0), pipeline_mode=pl.Buffered(1)),
                  pl.BlockSpec((2, FF_SHARD, D), lambda i: (0, 0, 0), pipeline_mode=pl.Buffered(1))],
        out_specs=[pl.BlockSpec((tm, D), row), pl.BlockSpec((tm, 2 * D_FF), row), pl.BlockSpec((tm, D_FF), row)],
        out_shape=[jax.ShapeDtypeStruct((s, D), F32), jax.ShapeDtypeStruct((s, 2 * D_FF), BF16),
                   jax.ShapeDtypeStruct((s, D_FF), BF16)],
        compiler_params=_cp(("parallel",)),
    )(x, gain.reshape(1, D), w_gu4, w_d2)


FFN_ROW_CHUNK = 32


def _ffn_bwd(x, dxo, dloc, gain, w_gu4, w_d2, *, name, tm=256):
    s = x.shape[0]
    tm = min(tm, s)

    def body(x_ref, dxo_ref, dloc_ref, g_ref, wgu_ref, wd_ref, dx_ref, dgu_ref, h_ref, dy_ref, dg_ref):
        xv = x_ref[...]
        r = lax.rsqrt(jnp.mean(xv * xv, axis=-1, keepdims=True) + EPS)
        xh = xv * r
        h_ref[...] = (xh * g_ref[...]).astype(BF16)
        dxov = dxo_ref[...]
        dy = (0.5 * dxov).astype(BF16)
        dy_ref[...] = dy
        gcols = [slice(j * FF_SHARD, (j + 1) * FF_SHARD) for j in range(2)]
        ucols = [slice(D_FF + j * FF_SHARD, D_FF + (j + 1) * FF_SHARD) for j in range(2)]
        dacts = [_dot(dy, wd_ref[j], NT) for j in range(2)]
        for r0 in range(0, tm, FFN_ROW_CHUNK):
            rows = slice(r0, r0 + FFN_ROW_CHUNK)
            for j in range(2):
                da = dacts[j][rows]
                dgu_ref[rows, gcols[j]] = (da * dloc_ref[rows, gcols[j]].astype(F32)).astype(BF16)
                dgu_ref[rows, ucols[j]] = (da * dloc_ref[rows, ucols[j]].astype(F32)).astype(BF16)
        dh = jnp.zeros((tm, D), F32)
        for j in range(2):
            dh = dh + _dot(dgu_ref[:, gcols[j]], wgu_ref[j], NT) + _dot(dgu_ref[:, ucols[j]], wgu_ref[2 + j], NT)
        a = dh * g_ref[...]
        dx_ref[...] = dxov + r * a - xh * (r * jnp.mean(a * xh, axis=-1, keepdims=True))

        @pl.when(pl.program_id(0) == 0)
        def _():
            dg_ref[...] = jnp.zeros_like(dg_ref)

        dg_ref[...] += jnp.sum(dh * xh, axis=0, keepdims=True)

    row = lambda i: (i, 0)
    outs = pl.pallas_call(
        body, name=name, grid=(s // tm,),
        in_specs=[pl.BlockSpec((tm, D), row), pl.BlockSpec((tm, D), row), pl.BlockSpec((tm, 2 * D_FF), row),
                  pl.BlockSpec((1, D), lambda i: (0, 0)),
                  pl.BlockSpec((N_CHIPS, D, FF_SHARD), lambda i: (0, 0, 0), pipeline_mode=pl.Buffered(1)),
                  pl.BlockSpec((2, FF_SHARD, D), lambda i: (0, 0, 0), pipeline_mode=pl.Buffered(1))],
        out_specs=[pl.BlockSpec((tm, D), row), pl.BlockSpec((tm, 2 * D_FF), row),
                   pl.BlockSpec((tm, D), row), pl.BlockSpec((tm, D), row), pl.BlockSpec((1, D), lambda i: (0, 0))],
        out_shape=[jax.ShapeDtypeStruct((s, D), F32), jax.ShapeDtypeStruct((s, 2 * D_FF), BF16),
                   jax.ShapeDtypeStruct((s, D), BF16), jax.ShapeDtypeStruct((s, D), BF16), jax.ShapeDtypeStruct((1, D), F32)],
        compiler_params=_cp(("arbitrary",)),
    )(x, dxo, dloc, gain.reshape(1, D), w_gu4, w_d2)
    dx, dgu, h, dy, dg = outs
    return dx, dgu, h, dy, dg.reshape(D)


DA = 128
SCALE_MLA = 1.0 / math.sqrt(NOPE + ROPE)
SCALE_FOX = 1.0 / math.sqrt(FOX_D)


def _causal_blocks(nb, key_major):
    if key_major:
        pairs = [(i, j) for j in range(nb) for i in range(j, nb)]
    else:
        pairs = [(i, j) for i in range(nb) for j in range(i + 1)]
    return (jnp.asarray(np.array([p[0] for p in pairs], np.int32)), jnp.asarray(np.array([p[1] for p in pairs], np.int32)))


HEADS_PER_STEP = 3
ROW_CHUNK = 64

def _col_to_row(col):
    return jnp.broadcast_to(col, (col.shape[0], DA)).T[0:1, :]


def _attn_fwd(qa, ka, va, dv, *, name, t=512):
    h, s, _ = qa.shape
    t = min(t, s)
    nb = s // t
    g = HEADS_PER_STEP
    qi, kj = _causal_blocks(nb, key_major=False)

    rc = min(ROW_CHUNK, t)

    def body(qi_ref, kj_ref, q_ref, k_ref, v_ref, o_ref, lse_ref, m_sc, acc_sc, p_sc, a_sc):
        n = pl.program_id(1)
        i, j = qi_ref[n], kj_ref[n]

        @pl.when(j == 0)
        def _():
            m_sc[...] = jnp.full_like(m_sc, -jnp.inf)
            acc_sc[...] = jnp.zeros_like(acc_sc)

        def step(masked):
            scs = [_dot(q_ref[hh], k_ref[hh], NT) for hh in range(g)]
            for r0 in range(0, t, rc):
                rows = slice(r0, r0 + rc)
                for hh in range(g):
                    sr = scs[hh][rows]
                    if masked:
                        row = lax.broadcasted_iota(jnp.int32, (rc, t), 0) + r0
                        col = lax.broadcasted_iota(jnp.int32, (rc, t), 1)
                        sr = jnp.where(col <= row, sr, -jnp.inf)
                    tiles = [sr[:, c0:c0 + DA] for c0 in range(0, t, DA)]
                    top = tiles[0]
                    for tile in tiles[1:]:
                        top = jnp.maximum(top, tile)
                    m_old = m_sc[hh, rows]
                    m_new = jnp.maximum(m_old, jnp.max(top, axis=-1, keepdims=True))
                    for c0, tile in zip(range(0, t, DA), tiles):
                        p_sc[hh, rows, c0:c0 + DA] = jnp.exp(tile - m_new).astype(BF16)
                    a_sc[hh, rows] = jnp.exp(m_old - m_new)
                    m_sc[hh, rows] = m_new
            for hh in range(g):
                acc_sc[hh] = a_sc[hh] * acc_sc[hh] + _dot(p_sc[hh], v_ref[hh], NN)

        @pl.when(j < i)
        def _():
            step(False)

        @pl.when(j == i)
        def _():
            step(True)
            for hh in range(g):
                acc = acc_sc[hh]
                l = acc[:, dv:dv + 1]
                o_ref[hh] = acc[:, :dv] / l
                lse_ref[hh] = _col_to_row(m_sc[hh][:, 0:1] + jnp.log(l))

    qmap = lambda hg, n, qi_r, kj_r: (hg, qi_r[n], 0)
    kmap = lambda hg, n, qi_r, kj_r: (hg, kj_r[n], 0)
    return pl.pallas_call(
        body, name=name,
        grid_spec=pltpu.PrefetchScalarGridSpec(
            num_scalar_prefetch=2, grid=(h // g, qi.shape[0]),
            in_specs=[pl.BlockSpec((g, t, DA), qmap), pl.BlockSpec((g, t, DA), kmap), pl.BlockSpec((g, t, DA), kmap)],
            out_specs=[pl.BlockSpec((g, t, dv), qmap), pl.BlockSpec((g, 1, t), lambda hg, n, qi_r, kj_r: (hg, 0, qi_r[n]))],
            scratch_shapes=[pltpu.VMEM((g, t, DA), F32), pltpu.VMEM((g, t, DA), F32), pltpu.VMEM((g, t, t), BF16),
                            pltpu.VMEM((g, t, DA), F32)]),
        out_shape=[jax.ShapeDtypeStruct((h, s, dv), F32), jax.ShapeDtypeStruct((h, 1, s), F32)],
        compiler_params=_cp(("parallel", "arbitrary")),
    )(qi, kj, qa, ka, va)


def _attn_bwd(qa, ka, va, doa, lse_row, delta_row, decay, *, name, t=512):
    h, s, _ = qa.shape
    t = min(t, s)
    nb = s // t
    g = HEADS_PER_STEP
    rc = min(ROW_CHUNK, t)
    qi, kj = _causal_blocks(nb, key_major=True)
    nsteps = qi.shape[0]

    def body(*refs):
        qi_ref, kj_ref, q_ref, k_ref, v_ref, do_ref, lse_ref, dl_ref = refs[:8]
        p_sc, ds_sc = refs[-2:]
        if decay:
            dq_ref, dk_ref, dv_ref, dcq_ref, dck_ref, dq_acc, dk_acc, dv_acc, dcq_acc, dck_acc = refs[8:-2]
        else:
            dq_ref, dk_ref, dv_ref, dq_acc, dk_acc, dv_acc = refs[8:-2]
        n = pl.program_id(1)
        i, j = qi_ref[n], kj_ref[n]

        @pl.when(n == 0)
        def _():
            dq_acc[...] = jnp.zeros_like(dq_acc)
            if decay:
                dcq_acc[...] = jnp.zeros_like(dcq_acc)

        @pl.when(i == j)
        def _():
            dk_acc[...] = jnp.zeros_like(dk_acc)
            dv_acc[...] = jnp.zeros_like(dv_acc)
            if decay:
                dck_acc[...] = jnp.zeros_like(dck_acc)

        def step(masked):
            sts = [_dot(k_ref[hh], q_ref[hh], NT) for hh in range(g)]
            dpts = [_dot(v_ref[hh], do_ref[hh], NT) for hh in range(g)]
            dcq = [jnp.zeros((1, t), F32) for _ in range(g)]
            for r0 in range(0, t, rc):
                rows = slice(r0, r0 + rc)
                for hh in range(g):
                    st = sts[hh][rows]
                    if masked:
                        row = lax.broadcasted_iota(jnp.int32, (rc, t), 0) + r0
                        col = lax.broadcasted_iota(jnp.int32, (rc, t), 1)
                        st = jnp.where(row <= col, st, -jnp.inf)
                    pt = jnp.exp(st - lse_ref[hh])
                    dst = pt * (dpts[hh][rows] - dl_ref[hh])
                    p_sc[hh, rows] = pt.astype(BF16)
                    ds_sc[hh, rows] = dst.astype(BF16)
                    if decay:
                        dcq[hh] = dcq[hh] + jnp.sum(dst, axis=0, keepdims=True)
                        dck_acc[hh, rows] -= jnp.sum(dst, axis=1, keepdims=True)
            for hh in range(g):
                dv_acc[hh] += _dot(p_sc[hh], do_ref[hh], NN)
                dk_acc[hh] += _dot(ds_sc[hh], q_ref[hh], NN)
                dq_acc[hh, i] += _dot(ds_sc[hh], k_ref[hh], TN)
                if decay:
                    dcq_acc[hh, i] += dcq[hh]

        @pl.when(i > j)
        def _():
            step(False)

        @pl.when(i == j)
        def _():
            step(True)

        @pl.when(i == nb - 1)
        def _():
            dk_ref[...] = dk_acc[...]
            dv_ref[...] = dv_acc[...]
            if decay:
                for hh in range(g):
                    dck_ref[hh] = _col_to_row(dck_acc[hh])

        @pl.when(n == nsteps - 1)
        def _():
            dq_ref[...] = dq_acc[...]
            if decay:
                dcq_ref[...] = dcq_acc[...]

    kmap = lambda hg, n, qi_r, kj_r: (hg, kj_r[n], 0)
    qmap = lambda hg, n, qi_r, kj_r: (hg, qi_r[n], 0)
    qrow = lambda hg, n, qi_r, kj_r: (hg, 0, qi_r[n])
    krow = lambda hg, n, qi_r, kj_r: (hg, 0, kj_r[n])
    whole = lambda hg, n, qi_r, kj_r: (hg, 0, 0, 0)
    in_specs = [pl.BlockSpec((g, t, DA), qmap), pl.BlockSpec((g, t, DA), kmap), pl.BlockSpec((g, t, DA), kmap),
                pl.BlockSpec((g, t, DA), qmap), pl.BlockSpec((g, 1, t), qrow), pl.BlockSpec((g, 1, t), qrow)]
    out_specs = [pl.BlockSpec((g, nb, t, DA), whole), pl.BlockSpec((g, t, DA), kmap), pl.BlockSpec((g, t, DA), kmap)]
    out_shape = [jax.ShapeDtypeStruct((h, nb, t, DA), F32), jax.ShapeDtypeStruct((h, s, DA), F32), jax.ShapeDtypeStruct((h, s, DA), F32)]
    scratch = [pltpu.VMEM((g, nb, t, DA), F32), pltpu.VMEM((g, t, DA), F32), pltpu.VMEM((g, t, DA), F32)]
    if decay:
        out_specs += [pl.BlockSpec((g, nb, 1, t), whole), pl.BlockSpec((g, 1, t), krow)]
        out_shape += [jax.ShapeDtypeStruct((h, nb, 1, t), F32), jax.ShapeDtypeStruct((h, 1, s), F32)]
        scratch += [pltpu.VMEM((g, nb, 1, t), F32), pltpu.VMEM((g, t, 1), F32)]
    scratch += [pltpu.VMEM((g, t, t), BF16), pltpu.VMEM((g, t, t), BF16)]
    outs = pl.pallas_call(
        body, name=name,
        grid_spec=pltpu.PrefetchScalarGridSpec(num_scalar_prefetch=2, grid=(h // g, nsteps), in_specs=in_specs, out_specs=out_specs,
                                               scratch_shapes=scratch),
        out_shape=out_shape, compiler_params=_cp(("parallel", "arbitrary")),
    )(qi, kj, qa, ka, va, doa, lse_row, delta_row)
    outs = list(outs)
    outs[0] = outs[0].reshape(h, s, DA)
    if decay:
        outs[3] = outs[3].reshape(h, 1, s)
    return outs


def _sel(rows, cols, pairs, value=1.0):
    m = np.zeros((rows, cols), np.float32)
    for r, c in pairs:
        m[r, c] = value
    return jnp.asarray(m, BF16)


def _lane_row(lanes):
    m = np.zeros((1, DA), np.float32)
    m[0, list(lanes)] = 1.0
    return jnp.asarray(m)


def _rms(xv, gain):
    r = lax.rsqrt(jnp.mean(xv * xv, axis=-1, keepdims=True) + EPS)
    return xv * r * gain


def _mla_q_prep(z, gain, wq_a, wq_b, cq, sq, *, name, tm=512):
    s = z.shape[0]
    tm = min(tm, s)

    def body(z_ref, g_ref, wa_ref, wb_ref, c_ref, s_ref, qa_ref, qn_ref):
        qn = _rms(z_ref[...], g_ref[...]).astype(BF16)
        qn_ref[...] = qn
        c, sn = c_ref[...], s_ref[...]
        for hh in range(H):
            cols = slice(hh * DA, (hh + 1) * DA)
            qa_ref[hh] = (_dot(qn, wa_ref[:, cols], NN) * c + _dot(qn, wb_ref[:, cols], NN) * sn).astype(BF16)

    row = lambda i: (i, 0)
    fixed = lambda i: (0, 0)
    return pl.pallas_call(
        body, name=name, grid=(s // tm,),
        in_specs=[pl.BlockSpec((tm, Q_RANK), lambda i: (i, Z_QA // Q_RANK)), pl.BlockSpec((1, Q_RANK), fixed),
                  pl.BlockSpec((Q_RANK, H * DA), fixed), pl.BlockSpec((Q_RANK, H * DA), fixed),
                  pl.BlockSpec((tm, DA), row), pl.BlockSpec((tm, DA), row)],
        out_specs=[pl.BlockSpec((H, tm, DA), lambda i: (0, i, 0)), pl.BlockSpec((tm, Q_RANK), row)],
        out_shape=[jax.ShapeDtypeStruct((H, s, DA), BF16), jax.ShapeDtypeStruct((s, Q_RANK), BF16)],
        compiler_params=_cp(("parallel",)),
    )(z, gain.reshape(1, Q_RANK), wq_a, wq_b, cq, sq)


def _mla_kv_prep(z, gain, wk, wv, ck, sk, *, name, tm=512):
    s = z.shape[0]
    tm = min(tm, s)
    one = _lane_row([VDIM])

    def body(zkv_ref, z3_ref, z15_ref, g_ref, wk_ref, wv_ref, c_ref, s_ref, one_ref, ka_ref, va_ref, kvn_ref):
        kvn = _rms(zkv_ref[...], g_ref[...]).astype(BF16)
        kvn_ref[...] = kvn
        kpe = z3_ref[...] * c_ref[...] + z15_ref[...] * s_ref[...]
        for hh in range(H):
            cols = slice(hh * DA, (hh + 1) * DA)
            ka_ref[hh] = (_dot(kvn, wk_ref[:, cols], NN) + kpe).astype(BF16)
            va_ref[hh] = (_dot(kvn, wv_ref[:, cols], NN) + one_ref[...]).astype(BF16)

    row = lambda i: (i, 0)
    fixed = lambda i: (0, 0)
    blk = lambda c: pl.BlockSpec((tm, DA), lambda i: (i, c))
    heads = pl.BlockSpec((H, tm, DA), lambda i: (0, i, 0))
    return pl.pallas_call(
        body, name=name, grid=(s // tm,),
        in_specs=[blk(Z_KVA // DA), blk(Z_KR // DA), blk(Z_F // DA), pl.BlockSpec((1, KV_RANK), fixed),
                  pl.BlockSpec((KV_RANK, H * DA), fixed), pl.BlockSpec((KV_RANK, H * DA), fixed),
                  pl.BlockSpec((tm, DA), row), pl.BlockSpec((tm, DA), row), pl.BlockSpec((1, DA), fixed)],
        out_specs=[heads, heads, pl.BlockSpec((tm, KV_RANK), row)],
        out_shape=[jax.ShapeDtypeStruct((H, s, DA), BF16), jax.ShapeDtypeStruct((H, s, DA), BF16),
                   jax.ShapeDtypeStruct((s, KV_RANK), BF16)],
        compiler_params=_cp(("parallel",)),
    )(z, z, z, gain.reshape(1, KV_RANK), wk, wv, ck, sk, one)


DEC_C = (FOX_D, FOX_D + 1, FOX_D + 2)
DEC_1 = (FOX_D + 3, FOX_D + 4, FOX_D + 5)


def _fox_prep(z, c3t, *, name, tm=512):
    s = z.shape[0]
    tm = min(tm, s)
    w = H * FOX_D
    left = [(r, r) for r in range(FOX_D)]
    right = [(FOX_D + r, r) for r in range(FOX_D)]
    pq = jnp.stack([_sel(DA, DA, left, SCALE_FOX), _sel(DA, DA, right, SCALE_FOX)])
    pk = jnp.stack([_sel(DA, DA, left), _sel(DA, DA, right)])
    pcq = jnp.stack([_sel(32, DA, [(hh + 8 * k, DEC_C[k]) for k in range(3)]) for hh in range(H)])
    pck = jnp.stack([_sel(32, DA, [(hh + 8 * k, DEC_1[k]) for k in range(3)], -1.0) for hh in range(H)])
    rows3 = jnp.concatenate([_lane_row(DEC_1), _lane_row(DEC_C), _lane_row([FOX_D])], axis=0)

    def body(zq_ref, zk_ref, zv_ref, c_ref, pq_ref, pk_ref, pcq_ref, pck_ref, r_ref, qa_ref, ka_ref, va_ref):
        c3 = c_ref[...]
        for pair in range(H // 2):
            lanes = slice(pair * DA, (pair + 1) * DA)
            zq, zk, zv = zq_ref[:, lanes].astype(BF16), zk_ref[:, lanes].astype(BF16), zv_ref[:, lanes].astype(BF16)
            for side in range(2):
                hh = 2 * pair + side
                qa_ref[hh] = (_dot(zq, pq_ref[side], NN) + _dot(c3, pcq_ref[hh], TN) + r_ref[0:1, :]).astype(BF16)
                ka_ref[hh] = (_dot(zk, pk_ref[side], NN) + _dot(c3, pck_ref[hh], TN) + r_ref[1:2, :]).astype(BF16)
                va_ref[hh] = (_dot(zv, pk_ref[side], NN) + r_ref[2:3, :]).astype(BF16)

    fixed2 = lambda i: (0, 0)
    fixed3 = lambda i: (0, 0, 0)
    heads = pl.BlockSpec((H, tm, DA), lambda i: (0, i, 0))
    zblk = lambda c: pl.BlockSpec((tm, w), lambda i: (i, c))
    return pl.pallas_call(
        body, name=name, grid=(s // tm,),
        in_specs=[zblk(Z_FOX // w), zblk(Z_FOX // w + 1), zblk(Z_FOX // w + 2), pl.BlockSpec((32, tm), lambda i: (0, i)),
                  pl.BlockSpec((2, DA, DA), fixed3), pl.BlockSpec((2, DA, DA), fixed3),
                  pl.BlockSpec((H, 32, DA), fixed3), pl.BlockSpec((H, 32, DA), fixed3), pl.BlockSpec((3, DA), fixed2)],
        out_specs=[heads, heads, heads], out_shape=[jax.ShapeDtypeStruct((H, s, DA), BF16)] * 3,
        compiler_params=_cp(("parallel",)),
    )(z, z, z, c3t, pq, pk, pcq, pck, rows3)


def _mix_out(oa, yb, oc, w_out, x1, *, name, tm=512):
    s = yb.shape[0]
    tm = min(tm, s)
    e2 = jnp.stack([_sel(VDIM, DA, [(r, r) for r in range(VDIM)]), _sel(VDIM, DA, [(r, VDIM + r) for r in range(VDIM)])])

    def body(oa_ref, yb_ref, oc_ref, e_ref, w_ref, x_ref, x2_ref, cat_ref):
        def pairs(o_ref):
            return [(_dot(o_ref[2 * p].astype(BF16), e_ref[0], NN) + _dot(o_ref[2 * p + 1].astype(BF16), e_ref[1], NN)).astype(BF16)
                    for p in range(H // 2)]

        cat = jnp.concatenate(pairs(oa_ref) + [yb_ref[...].astype(BF16)] + pairs(oc_ref), axis=1)
        cat_ref[...] = cat
        x2_ref[...] = x_ref[...] + _dot(cat, w_ref[...], NN)

    row = lambda i: (i, 0)
    heads = pl.BlockSpec((H, tm, VDIM), lambda i: (0, i, 0))
    return pl.pallas_call(
        body, name=name, grid=(s // tm,),
        in_specs=[heads, pl.BlockSpec((tm, POOL_W), row), heads, pl.BlockSpec((2, VDIM, DA), lambda i: (0, 0, 0)),
                  pl.BlockSpec((D, D), lambda i: (0, 0)), pl.BlockSpec((tm, D), row)],
        out_specs=[pl.BlockSpec((tm, D), row), pl.BlockSpec((tm, D), row)],
        out_shape=[jax.ShapeDtypeStruct((s, D), F32), jax.ShapeDtypeStruct((s, D), BF16)],
        compiler_params=_cp(("parallel",)),
    )(oa, yb, oc, e2, w_out, x1)


def _mix_out_bwd(dx2b, w_out, oa, oc, *, name, tm=512):
    s = dx2b.shape[0]
    tm = min(tm, s)
    f2 = jnp.stack([_sel(DA, DA, [(r, r) for r in range(VDIM)]), _sel(DA, DA, [(VDIM + r, r) for r in range(VDIM)])])
    nv = H * VDIM

    def body(dx_ref, w_ref, oa_ref, oc_ref, f_ref, doa_ref, doc_ref, dyb_ref, dla_ref, dlc_ref):
        dcat = _dot(dx_ref[...], w_ref[...], NT)
        dyb_ref[...] = dcat[:, nv:nv + POOL_W]
        for base, o_ref, do_ref, dl_ref in ((0, oa_ref, doa_ref, dla_ref), (nv + POOL_W, oc_ref, doc_ref, dlc_ref)):
            for p in range(H // 2):
                blk = dcat[:, base + p * DA:base + (p + 1) * DA].astype(BF16)
                for side in range(2):
                    hh = 2 * p + side
                    do = _dot(blk, f_ref[side], NN)
                    do_ref[hh] = do.astype(BF16)
                    dl_ref[hh] = _col_to_row(jnp.sum(do[:, :VDIM] * o_ref[hh], axis=-1, keepdims=True))

    row = lambda i: (i, 0)
    heads = lambda w: pl.BlockSpec((H, tm, w), lambda i: (0, i, 0))
    return pl.pallas_call(
        body, name=name, grid=(s // tm,),
        in_specs=[pl.BlockSpec((tm, D), row), pl.BlockSpec((D, D), lambda i: (0, 0)), heads(VDIM), heads(VDIM),
                  pl.BlockSpec((2, DA, DA), lambda i: (0, 0, 0))],
        out_specs=[heads(DA), heads(DA), pl.BlockSpec((tm, POOL_W), row),
                   pl.BlockSpec((H, 1, tm), lambda i: (0, 0, i)), pl.BlockSpec((H, 1, tm), lambda i: (0, 0, i))],
        out_shape=[jax.ShapeDtypeStruct((H, s, DA), BF16), jax.ShapeDtypeStruct((H, s, DA), BF16),
                   jax.ShapeDtypeStruct((s, POOL_W), F32), jax.ShapeDtypeStruct((H, 1, s), F32), jax.ShapeDtypeStruct((H, 1, s), F32)],
        compiler_params=_cp(("parallel",)),
    )(dx2b, w_out, oa, oc, f2)


def _mla_bwd_prep(dqa, dka, dva, dft, cq, sq, ck, sk, *, name, tm=512):
    s = dqa.shape[1]
    tm = min(tm, s)
    keep = _lane_row(range(NOPE))

    def body(dq_ref, dk_ref, dv_ref, dft_ref, cq_ref, sq_ref, ck_ref, sk_ref, keep_ref, dqab_ref, dkv_ref, dz3_ref, dz15_ref):
        cqv, sqv = cq_ref[...], sq_ref[...]
        dkpe = jnp.zeros((tm, DA), F32)
        for hh in range(H):
            lanes = slice(hh * DA, (hh + 1) * DA)
            dq = dq_ref[hh]
            dqab_ref[:, lanes] = (dq * cqv).astype(BF16)
            dqab_ref[:, H * DA + hh * DA:H * DA + (hh + 1) * DA] = (dq * sqv).astype(BF16)
            dk = dk_ref[hh]
            dkpe = dkpe + dk
            dkv_ref[:, lanes] = (dk * keep_ref[...]).astype(BF16)
            dkv_ref[:, H * DA + hh * DA:H * DA + (hh + 1) * DA] = (dv_ref[hh] * keep_ref[...]).astype(BF16)
        dz3_ref[...] = (dkpe * ck_ref[...]).astype(BF16)
        dz15_ref[...] = (dkpe * sk_ref[...] + dft_ref[...]).astype(BF16)

    row = lambda i: (i, 0)
    heads = pl.BlockSpec((H, tm, DA), lambda i: (0, i, 0))
    tab = pl.BlockSpec((tm, DA), row)
    return pl.pallas_call(
        body, name=name, grid=(s // tm,),
        in_specs=[heads, heads, heads, tab, tab, tab, tab, tab, pl.BlockSpec((1, DA), lambda i: (0, 0))],
        out_specs=[pl.BlockSpec((tm, 2 * H * DA), row), pl.BlockSpec((tm, 2 * H * DA), row), tab, tab],
        out_shape=[jax.ShapeDtypeStruct((s, 2 * H * DA), BF16), jax.ShapeDtypeStruct((s, 2 * H * DA), BF16),
                   jax.ShapeDtypeStruct((s, DA), BF16), jax.ShapeDtypeStruct((s, DA), BF16)],
        compiler_params=_cp(("parallel",)),
    )(dqa, dka, dva, dft, cq, sq, ck, sk, keep)


def _fox_bwd_prep(dfqa, dfka, dfva, *, name, tm=512):
    s = dfqa.shape[1]
    tm = min(tm, s)
    place = lambda v: jnp.stack([_sel(DA, DA, [(r, r) for r in range(FOX_D)], v), _sel(DA, DA, [(r, FOX_D + r) for r in range(FOX_D)], v)])
    gq, gk = place(SCALE_FOX), place(1.0)

    def body(dq_ref, dk_ref, dv_ref, gq_ref, gk_ref, dz_ref):
        for part, (d_ref, g_ref) in enumerate(((dq_ref, gq_ref), (dk_ref, gk_ref), (dv_ref, gk_ref))):
            for p in range(H // 2):
                blk = _dot(d_ref[2 * p].astype(BF16), g_ref[0], NN) + _dot(d_ref[2 * p + 1].astype(BF16), g_ref[1], NN)
                lo = part * H * FOX_D + p * DA
                dz_ref[:, lo:lo + DA] = blk.astype(BF16)

    heads = pl.BlockSpec((H, tm, DA), lambda i: (0, i, 0))
    sel = pl.BlockSpec((2, DA, DA), lambda i: (0, 0, 0))
    return pl.pallas_call(
        body, name=name, grid=(s // tm,), in_specs=[heads, heads, heads, sel, sel],
        out_specs=pl.BlockSpec((tm, 3 * H * FOX_D), lambda i: (i, 0)),
        out_shape=jax.ShapeDtypeStruct((s, 3 * H * FOX_D), BF16), compiler_params=_cp(("parallel",)),
    )(dfqa, dfka, dfva, gq, gk)


def _lane_scan(x, s, reverse):
    lane = lax.broadcasted_iota(jnp.int32, x.shape, 1)
    sh = 1
    while sh < s:
        if reverse:
            x = x + jnp.where(lane < s - sh, pltpu.roll(x, s - sh, axis=1), 0.0)
        else:
            x = x + jnp.where(lane >= sh, pltpu.roll(x, sh, axis=1), 0.0)
        sh *= 2
    return x


def _gate_fwd(z, col_block, bias, *, name):
    s = z.shape[0]

    def body(z_ref, b_ref, f_ref, c_ref):
        ft = z_ref[...].T[0:8, :]
        f_ref[...] = ft
        xg = ft + b_ref[...]
        lf = jnp.minimum(xg, 0.0) - jnp.log(1.0 + jnp.exp(-jnp.abs(xg)))
        c = _lane_scan(lf, s, False)
        hi = c.astype(BF16).astype(F32)
        r = c - hi
        mid = r.astype(BF16).astype(F32)
        lo = r - mid
        c_ref[...] = jnp.concatenate([hi, mid, lo, jnp.zeros_like(hi)], axis=0).astype(BF16)

    return pl.pallas_call(
        body, name=name, grid=(1,),
        in_specs=[pl.BlockSpec((s, 128), lambda i: (0, col_block)), pl.BlockSpec((8, 1), lambda i: (0, 0))],
        out_specs=[pl.BlockSpec((8, s), lambda i: (0, 0)), pl.BlockSpec((32, s), lambda i: (0, 0))],
        out_shape=[jax.ShapeDtypeStruct((8, s), F32), jax.ShapeDtypeStruct((32, s), BF16)],
        compiler_params=_cp(("arbitrary",)))(z, bias)


def _gate_bwd(ft, bias, dc, *, name):
    s = ft.shape[1]

    def body(f_ref, b_ref, dc_ref, df_ref, db_ref):
        xg = f_ref[...] + b_ref[...]
        dlf = _lane_scan(dc_ref[...], s, True)
        df = dlf * _sigmoid(-xg)
        db_ref[...] = jnp.sum(df, axis=-1, keepdims=True)
        df_ref[...] = jnp.concatenate([df, jnp.zeros((DA - 8, s), F32)], axis=0).T

    return pl.pallas_call(body, name=name, out_shape=[jax.ShapeDtypeStruct((s, DA), F32), jax.ShapeDtypeStruct((8, 1), F32)],
                          compiler_params=_cp())(ft, bias, dc)


def _pool_lane_consts(tm, i):
    lane = lax.broadcasted_iota(jnp.int32, (tm, POOL_W), 1)
    tok = lax.broadcasted_iota(jnp.int32, (tm, POOL_W), 0) + i * tm
    win = jnp.where(lane < 64, 2, jnp.where(lane < 128, 4, jnp.where(lane < 192, 8, 16)))
    cnt = jnp.minimum(tok + 1, win).astype(F32)
    return lane, tok, cnt


def _pick_window(lane, s2, s4, s8, s16):
    return jnp.where(lane < 64, s2, jnp.where(lane < 128, s4, jnp.where(lane < 192, s8, s16)))


def _pool_fwd(z, col_block, bd, scale, *, name, tm=512):
    s = z.shape[0]
    tm = min(tm, s)
    hb = tm // POOL_HALO

    def body(u_ref, halo_ref, bd_ref, sc_ref, y_ref, p_ref, buf):
        i = pl.program_id(0)
        buf[0:POOL_HALO, :] = halo_ref[...] * (i > 0).astype(F32)
        buf[POOL_HALO:, :] = u_ref[...]

        def back(k):
            return buf[POOL_HALO - k:POOL_HALO - k + tm, :]

        u = u_ref[...]
        s2 = u + back(1)
        s4 = s2 + back(2) + back(3)
        s8 = s4 + back(4) + back(5) + back(6) + back(7)
        s16 = s8
        for k in range(8, 16):
            s16 = s16 + back(k)
        lane, _, cnt = _pool_lane_consts(tm, i)
        pooled = (_pick_window(lane, s2, s4, s8, s16) / cnt - u).astype(BF16)
        p_ref[...] = pooled
        y_ref[...] = _dot(pooled, bd_ref[...], NN) * sc_ref[...]

    return pl.pallas_call(
        body, name=name, grid=(s // tm,),
        in_specs=[pl.BlockSpec((tm, POOL_W), lambda i: (i, col_block)),
                  pl.BlockSpec((POOL_HALO, POOL_W), lambda i: (jnp.maximum(i * hb - 1, 0), col_block)),
                  pl.BlockSpec((POOL_W, POOL_W), lambda i: (0, 0)), pl.BlockSpec((1, POOL_W), lambda i: (0, 0))],
        out_specs=[pl.BlockSpec((tm, POOL_W), lambda i: (i, 0)), pl.BlockSpec((tm, POOL_W), lambda i: (i, 0))],
        out_shape=[jax.ShapeDtypeStruct((s, POOL_W), F32), jax.ShapeDtypeStruct((s, POOL_W), BF16)],
        scratch_shapes=[pltpu.VMEM((tm + POOL_HALO, POOL_W), F32)],
        compiler_params=_cp(("parallel",)),
    )(z, z, bd, scale.reshape(1, POOL_W))


def _pool_bwd_a(dy, pooled, bd, scale, *, name, tm=512):
    s = dy.shape[0]
    tm = min(tm, s)

    def body(dy_ref, p_ref, bd_ref, sc_ref, dq_ref, dys_ref, dsc_ref):
        i = pl.program_id(0)
        dyv = dy_ref[...]
        y0 = _dot(p_ref[...], bd_ref[...], NN)
        dys = (dyv * sc_ref[...]).astype(BF16)
        dys_ref[...] = dys
        dp = _dot(dys, bd_ref[...], NT)
        _, _, cnt = _pool_lane_consts(tm, i)
        dq_ref[:, 0:POOL_W] = dp / cnt
        dq_ref[:, POOL_W:] = dp

        @pl.when(i == 0)
        def _():
            dsc_ref[...] = jnp.zeros_like(dsc_ref)

        dsc_ref[...] += jnp.sum(dyv * y0, axis=0, keepdims=True)

    row = lambda i: (i, 0)
    dq, dys, dsc = pl.pallas_call(
        body, name=name, grid=(s // tm,),
        in_specs=[pl.BlockSpec((tm, POOL_W), row), pl.BlockSpec((tm, POOL_W), row),
                  pl.BlockSpec((POOL_W, POOL_W), lambda i: (0, 0)), pl.BlockSpec((1, POOL_W), lambda i: (0, 0))],
        out_specs=[pl.BlockSpec((tm, 2 * POOL_W), row), pl.BlockSpec((tm, POOL_W), row), pl.BlockSpec((1, POOL_W), lambda i: (0, 0))],
        out_shape=[jax.ShapeDtypeStruct((s, 2 * POOL_W), F32), jax.ShapeDtypeStruct((s, POOL_W), BF16),
                   jax.ShapeDtypeStruct((1, POOL_W), F32)],
        compiler_params=_cp(("arbitrary",)),
    )(dy, pooled, bd, scale.reshape(1, POOL_W))
    return dq, dys, dsc.reshape(POOL_W)


def _pool_bwd_b(dq, *, name, tm=512):
    s = dq.shape[0]
    tm = min(tm, s)
    hb = tm // POOL_HALO
    nblk = s // tm

    def body(q_ref, dp_ref, halo_ref, du_ref, buf):
        i = pl.program_id(0)
        buf[0:tm, :] = q_ref[...]
        buf[tm:, :] = halo_ref[...] * (i < nblk - 1).astype(F32)

        def ahead(k):
            return buf[k:k + tm, :]

        q = q_ref[...]
        s2 = q + ahead(1)
        s4 = s2 + ahead(2) + ahead(3)
        s8 = s4 + ahead(4) + ahead(5) + ahead(6) + ahead(7)
        s16 = s8
        for k in range(8, 16):
            s16 = s16 + ahead(k)
        lane = lax.broadcasted_iota(jnp.int32, (tm, POOL_W), 1)
        du_ref[...] = _pick_window(lane, s2, s4, s8, s16) - dp_ref[...]

    return pl.pallas_call(
        body, name=name, grid=(nblk,),
        in_specs=[pl.BlockSpec((tm, POOL_W), lambda i: (i, 0)), pl.BlockSpec((tm, POOL_W), lambda i: (i, 1)),
                  pl.BlockSpec((POOL_HALO, POOL_W), lambda i: (jnp.minimum((i + 1) * hb, nblk * hb - 1), 0))],
        out_specs=pl.BlockSpec((tm, POOL_W), lambda i: (i, 0)),
        out_shape=jax.ShapeDtypeStruct((s, POOL_W), F32),
        scratch_shapes=[pltpu.VMEM((tm + POOL_HALO, POOL_W), F32)],
        compiler_params=_cp(("parallel",)),
    )(dq, dq, dq)


def _loss_head(x, gain, target, *, name, tm=512):
    s = x.shape[0]
    tm = min(tm, s)

    def body(x_ref, g_ref, t_ref, dx_ref, dg_ref, loss_ref):
        xv = x_ref[...]
        r = lax.rsqrt(jnp.mean(xv * xv, axis=-1, keepdims=True) + EPS)
        xh = xv * r
        err = xh * g_ref[...] - t_ref[...]
        dy = err * (1.0 / D)
        a = dy * g_ref[...]
        dx_ref[...] = r * a - xh * (r * jnp.mean(a * xh, axis=-1, keepdims=True))

        @pl.when(pl.program_id(0) == 0)
        def _():
            dg_ref[...] = jnp.zeros_like(dg_ref)
            loss_ref[...] = jnp.zeros_like(loss_ref)

        dg_ref[...] += jnp.sum(dy * xh, axis=0, keepdims=True)
        part = 0.5 * jnp.sum(jnp.mean(err * err, axis=-1, keepdims=True), axis=0, keepdims=True)
        loss_ref[...] += jnp.broadcast_to(part, loss_ref.shape)

    row = lambda i: (i, 0)
    dx, dg, loss = pl.pallas_call(
        body, name=name, grid=(s // tm,),
        in_specs=[pl.BlockSpec((tm, D), row), pl.BlockSpec((1, D), lambda i: (0, 0)), pl.BlockSpec((tm, D), row)],
        out_specs=[pl.BlockSpec((tm, D), row), pl.BlockSpec((1, D), lambda i: (0, 0)), pl.BlockSpec((1, 128), lambda i: (0, 0))],
        out_shape=[jax.ShapeDtypeStruct((s, D), F32), jax.ShapeDtypeStruct((1, D), F32), jax.ShapeDtypeStruct((1, 128), F32)],
        compiler_params=_cp(("arbitrary",)),
    )(x, gain.reshape(1, D), target)
    return dx, dg.reshape(D), loss[0, 0]


def _adamw(w, g, m, v, *, name, tr=512):
    rows, cols = w.shape
    tr = min(tr, rows)
    assert rows % tr == 0, (name, rows, tr)
    c_m = 1.0 - ADAM_B1
    c_v = 1.0 - ADAM_B2
    bc1 = 1.0 - ADAM_B1 ** ADAM_STEP
    bc2 = 1.0 - ADAM_B2 ** ADAM_STEP

    def body(w_ref, g_ref, m_ref, v_ref, d_ref, mo_ref, vo_ref):
        gv = g_ref[...]
        mn = ADAM_B1 * m_ref[...] + c_m * gv
        vn = ADAM_B2 * v_ref[...] + c_v * (gv * gv)
        mo_ref[...] = mn
        vo_ref[...] = vn
        d_ref[...] = -ADAM_LR * ((mn / bc1) / (jnp.sqrt(vn / bc2) + ADAM_EPS) + ADAM_WD * w_ref[...])

    spec = pl.BlockSpec((tr, cols), lambda i: (i, 0))
    return pl.pallas_call(body, name=name, grid=(rows // tr,), in_specs=[spec] * 4, out_specs=[spec] * 3,
                          out_shape=[jax.ShapeDtypeStruct((rows, cols), F32)] * 3,
                          compiler_params=_cp(("parallel",)))(w, g, m, v)


def _adamw_layer(w, g, m, v, layer, prev, *, name, tr):
    rows, cols = g.shape
    assert rows % tr == 0 and w.shape == (DEPTH * rows, cols), (name, w.shape, g.shape, tr)
    nblk = rows // tr
    c_m = 1.0 - ADAM_B1
    c_v = 1.0 - ADAM_B2
    bc1 = 1.0 - ADAM_B1 ** ADAM_STEP
    bc2 = 1.0 - ADAM_B2 ** ADAM_STEP
    n_prev = 0 if prev is None else 4

    def body(*refs):
        w_ref, g_ref, m_ref, v_ref = refs[:4]
        d_ref, mo_ref, vo_ref, go_ref = refs[4 + n_prev:]
        gv = g_ref[...]
        mn = ADAM_B1 * m_ref[...] + c_m * gv
        vn = ADAM_B2 * v_ref[...] + c_v * (gv * gv)
        mo_ref[...] = mn
        vo_ref[...] = vn
        go_ref[...] = gv
        d_ref[...] = -ADAM_LR * ((mn / bc1) / (jnp.sqrt(vn / bc2) + ADAM_EPS) + ADAM_WD * w_ref[...])

    stacked = pl.BlockSpec((tr, cols), lambda i: (layer * nblk + i, 0))
    args = [w, g, m, v] + ([] if prev is None else list(prev))
    return pl.pallas_call(
        body, name=name, grid=(nblk,),
        in_specs=[stacked, pl.BlockSpec((tr, cols), lambda i: (i, 0)), stacked, stacked] + [ANY_SPEC] * n_prev,
        out_specs=[stacked] * 4, out_shape=[jax.ShapeDtypeStruct(w.shape, F32)] * 4,
        input_output_aliases={4 + k: k for k in range(n_prev)},
        compiler_params=_cp(("parallel",)))(*args)


def _position():
    return jnp.stack([lax.axis_index("c"), 2 * lax.axis_index("x") + lax.axis_index("y")]).astype(jnp.int32)


SUM_ROW_TILES = 2


def _sum2_bf16(pos, fulls, sibs, *, name):
    n = len(fulls)
    nb = SUM_ROW_TILES

    def body(pos_ref, *refs):
        for t in range(n):
            refs[2 * n + t][...] = (refs[t][...] + refs[n + t][...]).astype(BF16)

    in_specs, sib_specs = [], []
    for sb in sibs:
        _, half, cols = sb.shape
        tr = half // nb
        assert half % nb == 0 and tr % 16 == 0, sb.shape
        in_specs.append(pl.BlockSpec((None, tr, cols), lambda j, i, p: (j, p[0] * nb + i, 0)))
        sib_specs.append(pl.BlockSpec((None, tr, cols), lambda j, i, p: (j, i, 0)))
    return pl.pallas_call(
        body, name=name,
        grid_spec=pltpu.PrefetchScalarGridSpec(num_scalar_prefetch=1, grid=(N_CHIPS, nb), in_specs=in_specs + sib_specs,
                                               out_specs=sib_specs),
        out_shape=[jax.ShapeDtypeStruct(sb.shape, BF16) for sb in sibs],
        compiler_params=_cp(("parallel", "parallel")))(pos, *fulls, *sibs)


def _sum5(pos, fulls, sibs, recvs, *, name):
    n = len(fulls)
    nb = SUM_ROW_TILES

    def body(pos_ref, *refs):
        for t in range(n):
            acc = refs[t][...] + refs[n + t][...]
            for kk in range(3):
                acc = acc + refs[2 * n + t][kk].astype(F32)
            refs[3 * n + t][...] = acc

    f_specs, s_specs, r_specs, o_specs = [], [], [], []
    for f in fulls:
        _, rows, cols = f.shape
        tr = rows // 2 // nb
        f_specs.append(pl.BlockSpec((None, tr, cols), lambda i, p: (p[1], p[0] * nb + i, 0)))
        s_specs.append(pl.BlockSpec((None, tr, cols), lambda i, p: (p[1], i, 0)))
        r_specs.append(pl.BlockSpec((3, tr, cols), lambda i, p: (0, i, 0)))
        o_specs.append(pl.BlockSpec((tr, cols), lambda i, p: (p[0] * nb + i, 0)))
    return pl.pallas_call(
        body, name=name,
        grid_spec=pltpu.PrefetchScalarGridSpec(num_scalar_prefetch=1, grid=(nb,), in_specs=f_specs + s_specs + r_specs,
                                               out_specs=o_specs),
        out_shape=[jax.ShapeDtypeStruct(f.shape[1:], F32) for f in fulls],
        compiler_params=_cp(("parallel",)))(pos, *fulls, *sibs, *recvs)


def _place():
    x, y, c = lax.axis_index("x"), lax.axis_index("y"), lax.axis_index("c")
    chips = [(1 - x, y), (x, 1 - y), (1 - x, 1 - y)]
    return x, y, c, 2 * x + y, chips


SEM_SPEC = pl.BlockSpec(memory_space=pltpu.SEMAPHORE)
ANY_SPEC = pl.BlockSpec(memory_space=pl.ANY)


def _gather_copies(ins, outs, send_i, recv_i, send_o, recv_o):
    x, y, c, me, chips = _place()
    n = len(ins)
    started, awaited = [], []
    for t in range(n):
        half = ins[t].shape[0] // 2
        mine = pl.ds(c * half, half)
        started.append(pltpu.make_async_remote_copy(
            src_ref=ins[t], dst_ref=outs[t].at[me], send_sem=send_o.at[t], recv_sem=recv_o.at[t],
            device_id=(x, y, 1 - c), device_id_type=MESH))
        awaited.append(started[-1])
        for kk, (px, py) in enumerate(chips):
            started.append(pltpu.make_async_remote_copy(
                src_ref=ins[t].at[mine], dst_ref=outs[t].at[me, mine], send_sem=send_i.at[t * 3 + kk],
                recv_sem=recv_i.at[t * 3 + kk], device_id=(px, py, c), device_id_type=MESH))
            awaited.append(pltpu.make_async_remote_copy(
                src_ref=ins[t].at[mine], dst_ref=outs[t].at[2 * px + py, mine], send_sem=send_i.at[t * 3 + kk],
                recv_sem=recv_i.at[t * 3 + kk], device_id=(px, py, c), device_id_type=MESH))
    return started, awaited


def _forward_copies(outs, send_d, recv_d):
    x, y, c, me, chips = _place()
    started, awaited = [], []
    for t in range(len(outs)):
        half = outs[t].shape[1] // 2
        for kk, (px, py) in enumerate(chips):
            for lst, hc in ((started, c), (awaited, 1 - c)):
                blk = outs[t].at[2 * px + py, pl.ds(hc * half, half)]
                lst.append(pltpu.make_async_remote_copy(src_ref=blk, dst_ref=blk, send_sem=send_d.at[t * 3 + kk],
                                                        recv_sem=recv_d.at[t * 3 + kk], device_id=(x, y, 1 - c), device_id_type=MESH))
    return started, awaited


def _gather_blocking(shards):
    n = len(shards)

    def body(*refs):
        ins, outs = refs[:n], refs[n:2 * n]
        send_i, recv_i, send_d, recv_d, send_o, recv_o = refs[2 * n:]
        started, awaited = _gather_copies(ins, outs, send_i, recv_i, send_o, recv_o)
        for cp in started:
            cp.start()
        for cp in awaited:
            cp.wait_recv()
        fwd, fwd_in = _forward_copies(outs, send_d, recv_d)
        for cp in fwd:
            cp.start()
        for cp in fwd_in:
            cp.wait_recv()
        for cp in started + fwd:
            cp.wait_send()

    return pl.pallas_call(
        body, name="gather_first", in_specs=[HBM_SPEC] * n, out_specs=[HBM_SPEC] * n,
        out_shape=[jax.ShapeDtypeStruct((N_CHIPS,) + s.shape, s.dtype) for s in shards],
        scratch_shapes=[pltpu.SemaphoreType.DMA((3 * n,)), pltpu.SemaphoreType.DMA((3 * n,)),
                        pltpu.SemaphoreType.DMA((3 * n,)), pltpu.SemaphoreType.DMA((3 * n,)),
                        pltpu.SemaphoreType.DMA((n,)), pltpu.SemaphoreType.DMA((n,))],
    )(*shards)


def _gather_start(shards, after, tag):
    n = len(shards)

    def body(*refs):
        ins = refs[:n]
        send_i, recv_i, send_o, recv_o = refs[2 * n + 1:2 * n + 5]
        outs = refs[3 * n + 5:4 * n + 5]
        token = refs[4 * n + 5]
        started, _ = _gather_copies(ins, outs, send_i, recv_i, send_o, recv_o)
        for cp in started:
            cp.start()
        token[...] = jnp.zeros_like(token)

    lands = [lax.empty((N_CHIPS,) + s.shape, s.dtype) for s in shards]
    sems = [pltpu.SemaphoreType.DMA((3 * n,)), pltpu.SemaphoreType.DMA((3 * n,)), pltpu.SemaphoreType.DMA((n,)), pltpu.SemaphoreType.DMA((n,))]
    res = pl.pallas_call(
        body, name=f"gather_{tag}_start",
        in_specs=[HBM_SPEC] * (2 * n) + [ANY_SPEC],
        out_specs=[SEM_SPEC] * 4 + [HBM_SPEC] * (2 * n) + [pl.BlockSpec(memory_space=pltpu.VMEM)],
        out_shape=sems + [jax.ShapeDtypeStruct(s.shape, s.dtype) for s in shards]
        + [jax.ShapeDtypeStruct(a.shape, a.dtype) for a in lands] + [jax.ShapeDtypeStruct((8, 128), F32)],
        input_output_aliases={t: 4 + t for t in range(2 * n)},
        compiler_params=pltpu.CompilerParams(has_side_effects=pltpu.SideEffectType.DATAFLOW_SIDE_EFFECTING),
    )(*[pltpu.with_memory_space_constraint(s, pltpu.HBM) for s in shards],
      *[pltpu.with_memory_space_constraint(a, pltpu.HBM) for a in lands], after)
    return res[:4], res[4:4 + n], res[4 + n:4 + 2 * n], res[-1]


def _gather_wait(sems, shards_thru, lands_thru, after, tag):
    n = len(shards_thru)

    def body(*refs):
        ins, outs_in = refs[:n], refs[n:2 * n]
        send_i, recv_i, send_o, recv_o = refs[2 * n:2 * n + 4]
        started, awaited = _gather_copies(ins, outs_in, send_i, recv_i, send_o, recv_o)
        for cp in started:
            cp.wait_send()
        for cp in awaited:
            cp.wait_recv()

    res = pl.pallas_call(
        body, name=f"gather_{tag}_wait",
        in_specs=[HBM_SPEC] * (2 * n) + [SEM_SPEC] * 4 + [ANY_SPEC],
        out_specs=[HBM_SPEC] * (2 * n),
        out_shape=[jax.ShapeDtypeStruct(a.shape, a.dtype) for a in list(shards_thru) + list(lands_thru)],
        input_output_aliases={t: t for t in range(2 * n)},
        compiler_params=pltpu.CompilerParams(has_side_effects=pltpu.SideEffectType.DATAFLOW_SIDE_EFFECTING),
    )(*shards_thru, *lands_thru, *sems, after)
    return res[n:]


def _gather_forward(lands, tag):
    n = len(lands)

    def body(*refs):
        outs = refs[n:2 * n]
        send_d, recv_d = refs[2 * n:]
        fwd, fwd_in = _forward_copies(outs, send_d, recv_d)
        for cp in fwd:
            cp.start()
        for cp in fwd_in:
            cp.wait_recv()
        for cp in fwd:
            cp.wait_send()

    return pl.pallas_call(
        body, name=f"gather_{tag}_forward", in_specs=[HBM_SPEC] * n, out_specs=[HBM_SPEC] * n,
        out_shape=[jax.ShapeDtypeStruct(a.shape, a.dtype) for a in lands],
        input_output_aliases={t: t for t in range(n)},
        scratch_shapes=[pltpu.SemaphoreType.DMA((3 * n,)), pltpu.SemaphoreType.DMA((3 * n,))],
    )(*lands)


def _stage1_copies(ins, sib, send, recv):
    x, y, c, me, chips = _place()
    cps = []
    for t in range(len(ins)):
        rows = ins[t].shape[1] // 2
        cps.append(pltpu.make_async_remote_copy(
            src_ref=ins[t].at[:, pl.ds((1 - c) * rows, rows), :], dst_ref=sib[t], send_sem=send.at[t],
            recv_sem=recv.at[t], device_id=(x, y, 1 - c), device_id_type=MESH))
    return cps


def _reduce_stage1(grads, tag):
    n = len(grads)

    def body(*refs):
        cps = _stage1_copies(refs[:n], refs[n:2 * n], *refs[2 * n:])
        for cp in cps:
            cp.start()
        for cp in cps:
            cp.wait()

    return pl.pallas_call(
        body, name="reduce_stage1_" + tag, in_specs=[HBM_SPEC] * n, out_specs=[HBM_SPEC] * n,
        out_shape=[jax.ShapeDtypeStruct((N_CHIPS, g.shape[1] // 2, g.shape[2]), F32) for g in grads],
        scratch_shapes=[pltpu.SemaphoreType.DMA((n,)), pltpu.SemaphoreType.DMA((n,))],
    )(*grads)


def _split_start(copies_fn, srcs, land_shapes, n_sems, tag):
    n = len(srcs)

    def body(*refs):
        send, recv = refs[2 * n:2 * n + 2]
        for cp in copies_fn(refs[:n], refs[3 * n + 2:4 * n + 2], send, recv):
            cp.start()
        refs[4 * n + 2][...] = jnp.zeros_like(refs[4 * n + 2])

    lands = [lax.empty(shp, dt) for shp, dt in land_shapes]
    res = pl.pallas_call(
        body, name=tag,
        in_specs=[HBM_SPEC] * (2 * n),
        out_specs=[SEM_SPEC] * 2 + [HBM_SPEC] * (2 * n) + [pl.BlockSpec(memory_space=pltpu.VMEM)],
        out_shape=[pltpu.SemaphoreType.DMA((n_sems,)), pltpu.SemaphoreType.DMA((n_sems,))]
        + [jax.ShapeDtypeStruct(p.shape, p.dtype) for p in srcs]
        + [jax.ShapeDtypeStruct(a.shape, a.dtype) for a in lands] + [jax.ShapeDtypeStruct((8, 128), F32)],
        input_output_aliases={t: 2 + t for t in range(2 * n)},
        compiler_params=pltpu.CompilerParams(has_side_effects=pltpu.SideEffectType.DATAFLOW_SIDE_EFFECTING),
    )(*[pltpu.with_memory_space_constraint(p, pltpu.HBM) for p in srcs],
      *[pltpu.with_memory_space_constraint(a, pltpu.HBM) for a in lands])
    return res[:2], res[2:2 + n], res[2 + n:2 + 2 * n], res[-1]


def _split_wait(copies_fn, sems, srcs_thru, lands_thru, after, tag):
    n = len(srcs_thru)

    def body(*refs):
        for cp in copies_fn(refs[:n], refs[n:2 * n], refs[2 * n], refs[2 * n + 1]):
            cp.wait()

    res = pl.pallas_call(
        body, name=tag,
        in_specs=[HBM_SPEC] * (2 * n) + [SEM_SPEC] * 2 + [ANY_SPEC],
        out_specs=[HBM_SPEC] * (2 * n),
        out_shape=[jax.ShapeDtypeStruct(a.shape, a.dtype) for a in list(srcs_thru) + list(lands_thru)],
        input_output_aliases={t: t for t in range(2 * n)},
        compiler_params=pltpu.CompilerParams(has_side_effects=pltpu.SideEffectType.DATAFLOW_SIDE_EFFECTING),
    )(*srcs_thru, *lands_thru, *sems, after)
    return res[:n], res[n:]


def _stage2_copies(ps, rcv, send, recv):
    x, y, c, me, chips = _place()
    return [pltpu.make_async_remote_copy(
        src_ref=ps[t].at[2 * px + py], dst_ref=rcv[t].at[kk], send_sem=send.at[t * 3 + kk],
        recv_sem=recv.at[t * 3 + kk], device_id=(px, py, c), device_id_type=MESH)
        for t in range(len(ps)) for kk, (px, py) in enumerate(chips)]


def _reduce_stage3(reduced, tag):
    n = len(reduced)

    def body(*refs):
        outs = refs[n:2 * n]
        send, recv = refs[2 * n:]
        x, y, c, me, chips = _place()
        cps = []
        for t in range(n):
            rows = outs[t].shape[0] // 2
            mine = outs[t].at[pl.ds(c * rows, rows), :]
            cp = pltpu.make_async_remote_copy(src_ref=mine, dst_ref=mine, send_sem=send.at[t], recv_sem=recv.at[t],
                                              device_id=(x, y, 1 - c), device_id_type=MESH)
            cp.start()
            cps.append(cp)
        for cp in cps:
            cp.wait()

    return pl.pallas_call(
        body, name="reduce_stage3_" + tag, in_specs=[HBM_SPEC] * n, out_specs=[HBM_SPEC] * n,
        out_shape=[jax.ShapeDtypeStruct(r.shape, r.dtype) for r in reduced],
        input_output_aliases={t: t for t in range(n)},
        scratch_shapes=[pltpu.SemaphoreType.DMA((n,)), pltpu.SemaphoreType.DMA((n,))],
    )(*reduced)


def _allreduce_small(v):
    rows, cols = v.shape

    def body(v_ref, o_ref, buf, send, recv, loc):
        x, y, c, me, chips = _place()
        mine = 4 * x + 2 * y + c
        lc = pltpu.make_async_copy(v_ref, buf.at[mine], loc)
        lc.start()
        peers = []
        for fx in range(2):
            for fy in range(2):
                for fc in range(2):
                    if fx or fy or fc:
                        peers.append((fx, fy, fc))
        cps = []
        for kk, (fx, fy, fc) in enumerate(peers):
            to = (x ^ fx, y ^ fy, c ^ fc)
            cp = pltpu.make_async_remote_copy(src_ref=v_ref, dst_ref=buf.at[mine], send_sem=send.at[kk], recv_sem=recv.at[kk],
                                              device_id=to, device_id_type=MESH)
            cp.start()
            cps.append((cp, to))
        for kk, (cp, to) in enumerate(cps):
            src = 4 * to[0] + 2 * to[1] + to[2]
            pltpu.make_async_remote_copy(src_ref=v_ref, dst_ref=buf.at[src], send_sem=send.at[kk], recv_sem=recv.at[kk],
                                         device_id=to, device_id_type=MESH).wait_recv()
        for cp, _ in cps:
            cp.wait_send()
        lc.wait()
        acc = buf[0]
        for d in range(1, 8):
            acc = acc + buf[d]
        o_ref[...] = acc

    return pl.pallas_call(
        body, name="allreduce_small", in_specs=[pl.BlockSpec(memory_space=pltpu.VMEM)],
        out_specs=pl.BlockSpec(memory_space=pltpu.VMEM), out_shape=jax.ShapeDtypeStruct((rows, cols), F32),
        scratch_shapes=[pltpu.VMEM((8, rows, cols), F32), pltpu.SemaphoreType.DMA((7,)), pltpu.SemaphoreType.DMA((7,)),
                        pltpu.SemaphoreType.DMA],
        compiler_params=pltpu.CompilerParams(vmem_limit_bytes=VMEM_LIMIT_V7X),
    )(v)


def _pad_w_in(w):
    z = lambda n: jnp.zeros(w.shape[:-1] + (n,), w.dtype)
    return jnp.concatenate([w[..., 0:384], z(64), w[..., 384:416], z(32), w[..., 416:1824],
                            w[..., 1824:1830], z(58), w[..., 400:416], w[..., 384:400], z(32)], axis=-1)


def _unpad_w_in(g):
    x1 = g[..., 448:464] + g[..., Z_F + 80:Z_F + 96]
    x2 = g[..., 464:480] + g[..., Z_F + 64:Z_F + 80]
    return jnp.concatenate([g[..., 0:384], x1, x2, g[..., 512:1920], g[..., 1920:1926]], axis=-1)


def _block_diag(pw):
    out = jnp.zeros((POOL_W, POOL_W), pw.dtype)
    for g in range(4):
        out = out.at[g * 64:(g + 1) * 64, g * 64:(g + 1) * 64].set(pw[g])
    return out


def _rope_tables(s):
    inv_freq = ROPE_THETA ** (-jnp.arange(0, ROPE, 2, dtype=F32) / ROPE)
    ang = jnp.arange(s, dtype=jnp.int32).astype(F32)[:, None] * inv_freq[None, :]
    cos, sin = jnp.cos(ang), jnp.sin(ang)
    zero = lambda n: jnp.zeros((s, n), F32)
    ck = jnp.concatenate([zero(NOPE), cos, cos, zero(DA - NOPE - ROPE)], axis=1)
    sk = jnp.concatenate([zero(NOPE), -sin, sin, zero(DA - NOPE - ROPE)], axis=1)
    cq = jnp.concatenate([jnp.ones((s, NOPE), F32), cos, cos, zero(DA - NOPE - ROPE)], axis=1) * SCALE_MLA
    return dict(cq=cq, sq=sk * SCALE_MLA, ck=ck, sk=sk)


def _mix_fwd(l, x1, wts, sm, tabs):
    z, h2 = _norm_mm(x1, 0, sm["mix_norm"][l], wts["w_in"][l], name=f"mix_in_{l}")
    qa, qn = _mla_q_prep(z, sm["q_a_norm"][l], wts["wq_a"][l], wts["wq_b"][l], tabs["cq"], tabs["sq"], name=f"mla_q_{l}")
    ka, va, kvn = _mla_kv_prep(z, sm["kv_a_norm"][l], wts["wk"][l], wts["wv"][l], tabs["ck"], tabs["sk"], name=f"mla_kv_{l}")
    oa, lse_a = _attn_fwd(qa, ka, va, VDIM, name=f"mla_attn_{l}")

    bd = _block_diag(wts["pool_w"][l]).astype(BF16)
    yb, pooled = _pool_fwd(z, Z_POOL // POOL_W, bd, sm["pool_scale"][l], name=f"pool_{l}")

    fb = jnp.pad(sm["fox_b_f"][l], (0, 8 - H)).reshape(8, 1)
    ft, c3t = _gate_fwd(z, Z_F // DA, fb, name=f"fox_gate_{l}")
    fqa, fka, fva = _fox_prep(z, c3t, name=f"fox_prep_{l}")
    oc, lse_c = _attn_fwd(fqa, fka, fva, FOX_D, name=f"fox_attn_{l}")

    x2, cat = _mix_out(oa, yb, oc, wts["w_out"][l], x1, name=f"mix_out_{l}")
    saved = dict(z=z, h2=h2, qn=qn, kvn=kvn, qa=qa, ka=ka, va=va, oa=oa, lse_a=lse_a, bd=bd, pooled=pooled,
                 fqa=fqa, fka=fka, fva=fva, ft=ft, fb=fb, oc=oc, lse_c=lse_c, cat=cat)
    return x2, saved


def _mix_bwd(l, x1, dx2, sv, wts, sm, tabs, tok=None):
    s = x1.shape[0]
    g = {}
    dx2b = (dx2 if tok is None else dx2 + tok).astype(BF16)
    g["w_out"] = _mm(sv["cat"], dx2b, "tn", name=f"d_w_out_{l}", tm=1024, tn=1024, tk=DW_TOKENS)
    doa, doc, dyb, dl_a, dl_c = _mix_out_bwd(dx2b, wts["w_out"][l], sv["oa"], sv["oc"], name=f"mix_out_bwd_{l}")

    dfqa, dfka, dfva, dcq, dck = _attn_bwd(sv["fqa"], sv["fka"], sv["fva"], doc, sv["lse_c"], dl_c, True, name=f"fox_attn_bwd_{l}")
    dfox = _fox_bwd_prep(dfqa, dfka, dfva, name=f"fox_bwd_prep_{l}")
    dc = jnp.pad(dcq.reshape(H, s) + dck.reshape(H, s), ((0, 8 - H), (0, 0)))
    dft, dfb = _gate_bwd(sv["ft"], sv["fb"], dc, name=f"fox_gate_bwd_{l}")
    g["fox_b_f"] = dfb[:H, 0]

    dq, dys, g["pool_scale"] = _pool_bwd_a(dyb, sv["pooled"], sv["bd"], sm["pool_scale"][l], name=f"pool_bwd_a_{l}")
    du = _pool_bwd_b(dq, name=f"pool_bwd_b_{l}")
    dbd = _mm(sv["pooled"], dys, "tn", name=f"d_pool_w_{l}")
    g["pool_w"] = jnp.stack([dbd[i * 64:(i + 1) * 64, i * 64:(i + 1) * 64] for i in range(4)])

    dqa_, dka_, dva_ = _attn_bwd(sv["qa"], sv["ka"], sv["va"], doa, sv["lse_a"], dl_a, False, name=f"mla_attn_bwd_{l}")
    dqab, dkv, dz3, dz15 = _mla_bwd_prep(dqa_, dka_, dva_, dft, tabs["cq"], tabs["sq"], tabs["ck"], tabs["sk"],
                                         name=f"mla_bwd_prep_{l}")
    wq_ab = jnp.concatenate([wts["wq_a"][l], wts["wq_b"][l]], axis=1)
    wkv = jnp.concatenate([wts["wk"][l], wts["wv"][l]], axis=1)
    dwq = _mm(sv["qn"], dqab, "tn", name=f"d_w_q_b_{l}", tn=768, tk=DW_TOKENS).reshape(Q_RANK, 2, H, DA)
    dwkv = _mm(sv["kvn"], dkv, "tn", name=f"d_w_kv_b_{l}", tn=768, tk=DW_TOKENS).reshape(KV_RANK, 2, H, DA)
    da, db = dwq[:, 0], dwq[:, 1]
    swapped = jnp.concatenate([jnp.zeros((Q_RANK, H, NOPE), F32), db[..., NOPE + HALF_ROPE:NOPE + ROPE],
                               db[..., NOPE:NOPE + HALF_ROPE]], axis=-1)
    g["w_q_b"] = (da[..., :NOPE + ROPE] + swapped).reshape(Q_RANK, H * (NOPE + ROPE))
    g["w_kv_b"] = jnp.concatenate([dwkv[:, 0, :, :NOPE], dwkv[:, 1, :, :VDIM]], axis=-1).reshape(KV_RANK, H * (NOPE + VDIM))
    dqn = _mm(dqab, wq_ab, "nt", name=f"d_qn_{l}", tk=2 * H * DA)
    dkvn = _mm(dkv, wkv, "nt", name=f"d_kvn_{l}", tk=2 * H * DA)
    dqa, g["q_a_norm"] = _rmsnorm_bwd(sv["z"], Z_QA // Q_RANK, sm["q_a_norm"][l], dqn, name=f"q_a_norm_bwd_{l}")
    dkva, g["kv_a_norm"] = _rmsnorm_bwd(sv["z"], Z_KVA // KV_RANK, sm["kv_a_norm"][l], dkvn, name=f"kv_a_norm_bwd_{l}")

    dz = jnp.concatenate([dqa.astype(BF16), dkva.astype(BF16), dz3, du.astype(BF16), dfox, dz15], axis=1)
    g["w_in"] = _mm(sv["h2"], dz, "tn", name=f"d_w_in_{l}", tm=1024, tn=1024, tk=DW_TOKENS)
    dh2 = _mm(dz, wts["w_in"][l], "nt", name=f"d_h2_{l}", tn=1024, tk=NZ)
    dx1, g["mix_norm"] = _rmsnorm_bwd(x1, 0, sm["mix_norm"][l], dh2, dx2, name=f"mix_norm_bwd_{l}")
    return dx1, g


DW_TOKENS = 2048


def _local_step(x, target, wts, sm, late_weights=None, grads_ready=None):
    s = x.shape[0]
    tabs = _rope_tables(s)
    acts = []
    xs = x
    for l in range(DEPTH):
        x1, gu1, act1 = _ffn_fwd(xs, sm["ffn1_norm"][l], wts["ffn1_w_gu"][l], wts["ffn1_w_d2"][l], name=f"ffn1_fwd_{l}")
        if l == 0 and late_weights is not None:
            sm = late_weights("ffn1", x1, sm)
        x2, sv = _mix_fwd(l, x1, wts, sm, tabs)
        if l == 0 and late_weights is not None:
            sm = late_weights("mix", x2, sm)
        x3, gu2, act2 = _ffn_fwd(x2, sm["ffn2_norm"][l], wts["ffn2_w_gu"][l], wts["ffn2_w_d2"][l], name=f"ffn2_fwd_{l}")
        acts.append((xs, gu1, act1, x1, sv, x2, gu2, act2))
        xs = x3
    dx, g_final, loss = _loss_head(xs, sm["final_norm"], target, name="loss_head")
    grads = [dict() for _ in range(DEPTH)]
    for l in reversed(range(DEPTH)):
        x0, gu1, act1, x1, sv, x2, gu2, act2 = acts[l]
        g = grads[l]
        dx, dgu, hh, dy, g["ffn2_norm"] = _ffn_bwd(x2, dx, gu2, sm["ffn2_norm"][l], wts["ffn2_w_gu"][l], wts["ffn2_w_d2"][l],
                                                   name=f"ffn2_bwd_{l}")
        g["ffn2_w_down"] = _mm(act2, dy, "tn", name=f"d_ffn2_w_down_{l}", tm=FF_SHARD, tn=1024, tk=DW_TOKENS)
        g["ffn2_w_gu"] = _mm(hh, dgu, "tn", name=f"d_ffn2_w_gu_{l}", tm=1024, tn=FF_SHARD, tk=DW_TOKENS, n_major_out=True)
        tok = None
        if grads_ready is not None:
            sm, tok = grads_ready(l, "ffn2", g, sm)
        dx, gm = _mix_bwd(l, x1, dx, sv, wts, sm, tabs, tok)
        g.update(gm)
        if grads_ready is not None:
            sm, _ = grads_ready(l, "mix", g, sm)
        dx, dgu, hh, dy, g["ffn1_norm"] = _ffn_bwd(x0, dx, gu1, sm["ffn1_norm"][l], wts["ffn1_w_gu"][l], wts["ffn1_w_d2"][l],
                                                   name=f"ffn1_bwd_{l}")
        g["ffn1_w_down"] = _mm(act1, dy, "tn", name=f"d_ffn1_w_down_{l}", tm=FF_SHARD, tn=1024, tk=DW_TOKENS)
        g["ffn1_w_gu"] = _mm(hh, dgu, "tn", name=f"d_ffn1_w_gu_{l}", tm=1024, tn=FF_SHARD, tk=DW_TOKENS, n_major_out=True)
        if grads_ready is not None:
            sm, _ = grads_ready(l, "ffn1", g, sm)
    return loss, dx, grads, g_final


BIG = ["ffn1_w_gu", "ffn1_w_down", "w_in", "w_q_b", "w_kv_b", "w_out", "ffn2_w_gu", "ffn2_w_down"]
SMALL = ["ffn1_norm", "mix_norm", "q_a_norm", "kv_a_norm", "pool_w", "pool_scale", "fox_b_f", "ffn2_norm"]
SMALL_ROWS = 48


WEIGHT_VIEWS = ["ffn1_w_gu", "ffn1_w_d2", "w_in", "wq_a", "wq_b", "wk", "wv", "w_out", "ffn2_w_gu", "ffn2_w_d2"]


def _prepare_weights(gathered, wts):
    for (nm, l), w in gathered.items():
        if nm in ("ffn1_w_gu", "ffn2_w_gu"):
            wts[nm][l] = w
        elif nm in ("ffn1_w_down", "ffn2_w_down"):
            wts[nm[:5] + "w_d2"][l] = w.reshape(2, FF_SHARD, D)
        elif nm in ("w_in", "w_out"):
            wts[nm][l] = w.reshape(D, -1)
        elif nm == "w_q_b":
            wq = jnp.moveaxis(w, 0, 1).reshape(Q_RANK, H, NOPE + ROPE)
            zq = lambda n: jnp.zeros((Q_RANK, H, n), BF16)
            wts["wq_a"][l] = jnp.concatenate([wq, zq(DA - NOPE - ROPE)], axis=-1).reshape(Q_RANK, H * DA)
            wts["wq_b"][l] = jnp.concatenate([zq(NOPE), wq[..., NOPE + HALF_ROPE:], wq[..., NOPE:NOPE + HALF_ROPE],
                                              zq(DA - NOPE - ROPE)], axis=-1).reshape(Q_RANK, H * DA)
        else:
            wkv = jnp.moveaxis(w, 0, 1).reshape(KV_RANK, H, NOPE + VDIM)
            zk = jnp.zeros((KV_RANK, H, DA - NOPE), BF16)
            wts["wk"][l] = jnp.concatenate([wkv[..., :NOPE], zk], axis=-1).reshape(KV_RANK, H * DA)
            wts["wv"][l] = jnp.concatenate([wkv[..., NOPE:], zk], axis=-1).reshape(KV_RANK, H * DA)


def _chip_major(name, g):
    if name in ("ffn1_w_gu", "ffn2_w_gu"):
        return g
    if name in ("ffn1_w_down", "ffn2_w_down", "w_in", "w_out"):
        return g.reshape(N_CHIPS, g.shape[0] // N_CHIPS, g.shape[1])
    return jnp.moveaxis(g.reshape(g.shape[0], N_CHIPS, g.shape[1] // N_CHIPS), 1, 0)


def _pack_small(grads, g_final, loss):
    parts = []
    for l in range(DEPTH):
        for nm in SMALL:
            parts.append(grads[l][nm].reshape(-1))
    parts.append(g_final.reshape(-1))
    parts.append(loss.reshape(1))
    flat = jnp.concatenate(parts)
    return jnp.pad(flat, (0, SMALL_ROWS * D - flat.shape[0])).reshape(SMALL_ROWS, D)


def _unpack_small(packed, params):
    flat = packed.reshape(-1)
    out = {nm: [] for nm in SMALL}
    off = 0
    for l in range(DEPTH):
        for nm in SMALL:
            shp = params[nm].shape[1:]
            n = int(np.prod(shp))
            out[nm].append(flat[off:off + n].reshape(shp))
            off += n
    res = {nm: jnp.stack(v) for nm, v in out.items()}
    res["final_norm"] = flat[off:off + D]
    return res, flat[off + D]


def _update(name, w, g, m, v):
    shp = w.shape
    if w.ndim == 1:
        view = (1, shp[0])
    elif w.size <= 65536:
        view = (shp[0], w.size // shp[0])
    else:
        view = (w.size // shp[-1], shp[-1])
    tr = view[0]
    for cand in (512, 352, 256, 128):
        if view[0] % cand == 0 and view[0] > cand:
            tr = cand
            break
    d, mn, vn = _adamw(w.reshape(view), g.reshape(view), m.reshape(view), v.reshape(view), name="adamw_" + name, tr=tr)
    return d.reshape(shp), mn.reshape(shp), vn.reshape(shp)


WEIGHTS = ['ffn1_norm', 'ffn1_w_gu', 'ffn1_w_down', 'mix_norm', 'w_in', 'q_a_norm', 'w_q_b', 'kv_a_norm', 'w_kv_b', 'pool_w',
           'pool_scale', 'fox_b_f', 'w_out', 'ffn2_norm', 'ffn2_w_gu', 'ffn2_w_down', 'final_norm']


def kernel(x, ffn1_norm, ffn1_w_gu, ffn1_w_down, mix_norm, w_in, q_a_norm, w_q_b, kv_a_norm, w_kv_b, pool_w, pool_scale, fox_b_f, w_out, ffn2_norm, ffn2_w_gu, ffn2_w_down, final_norm, loss_target, m_ffn1_norm, m_ffn1_w_gu, m_ffn1_w_down, m_mix_norm, m_w_in, m_q_a_norm, m_w_q_b, m_kv_a_norm, m_w_kv_b, m_pool_w, m_pool_scale, m_fox_b_f, m_w_out, m_ffn2_norm, m_ffn2_w_gu, m_ffn2_w_down, m_final_norm, v_ffn1_norm, v_ffn1_w_gu, v_ffn1_w_down, v_mix_norm, v_w_in, v_q_a_norm, v_w_q_b, v_kv_a_norm, v_w_kv_b, v_pool_w, v_pool_scale, v_fox_b_f, v_w_out, v_ffn2_norm, v_ffn2_w_gu, v_ffn2_w_down, v_final_norm):
    params = dict(ffn1_norm=ffn1_norm, ffn1_w_gu=ffn1_w_gu, ffn1_w_down=ffn1_w_down, mix_norm=mix_norm, w_in=w_in, q_a_norm=q_a_norm,
                  w_q_b=w_q_b, kv_a_norm=kv_a_norm, w_kv_b=w_kv_b, pool_w=pool_w, pool_scale=pool_scale, fox_b_f=fox_b_f, w_out=w_out,
                  ffn2_norm=ffn2_norm, ffn2_w_gu=ffn2_w_gu, ffn2_w_down=ffn2_w_down, final_norm=final_norm)
    mom = dict(ffn1_norm=m_ffn1_norm, ffn1_w_gu=m_ffn1_w_gu, ffn1_w_down=m_ffn1_w_down, mix_norm=m_mix_norm, w_in=m_w_in,
               q_a_norm=m_q_a_norm, w_q_b=m_w_q_b, kv_a_norm=m_kv_a_norm, w_kv_b=m_w_kv_b, pool_w=m_pool_w, pool_scale=m_pool_scale,
               fox_b_f=m_fox_b_f, w_out=m_w_out, ffn2_norm=m_ffn2_norm, ffn2_w_gu=m_ffn2_w_gu, ffn2_w_down=m_ffn2_w_down,
               final_norm=m_final_norm)
    var = dict(ffn1_norm=v_ffn1_norm, ffn1_w_gu=v_ffn1_w_gu, ffn1_w_down=v_ffn1_w_down, mix_norm=v_mix_norm, w_in=v_w_in,
               q_a_norm=v_q_a_norm, w_q_b=v_w_q_b, kv_a_norm=v_kv_a_norm, w_kv_b=v_w_kv_b, pool_w=v_pool_w, pool_scale=v_pool_scale,
               fox_b_f=v_fox_b_f, w_out=v_w_out, ffn2_norm=v_ffn2_norm, ffn2_w_gu=v_ffn2_w_gu, ffn2_w_down=v_ffn2_w_down,
               final_norm=v_final_norm)

    first = [("ffn1_w_gu", 0), ("ffn1_w_down", 0)]
    mix0 = [(nm, 0) for nm in ("w_in", "w_q_b", "w_kv_b", "w_out")]
    rest = [(nm, l) for nm in BIG for l in range(DEPTH) if (nm, l) not in first + mix0]

    def shards(keys, zero=0.0):
        return [((_pad_w_in(params[nm]) if nm == "w_in" else params[nm])[l] + zero).astype(BF16) for nm, l in keys]

    wts = {nm: [None] * DEPTH for nm in WEIGHT_VIEWS}
    wts["pool_w"] = params["pool_w"]
    got = _gather_blocking(shards(first))
    _prepare_weights(dict(zip(first, got)), wts)
    sems_m, src_m, land_m, token_m = _gather_start(shards(mix0), got[0], "mix0")
    sm = dict(params)
    sm["ffn1_norm"] = params["ffn1_norm"] + token_m[0, 0]
    rest_shards = shards(rest, token_m[0, 0])
    flying = {}

    def late_weights(stage, act, sm_now):
        if stage == "ffn1":
            lands = _gather_forward(_gather_wait(sems_m, src_m, land_m, act, "mix0"), "mix0")
            _prepare_weights(dict(zip(mix0, lands)), wts)
            flying["rest"] = _gather_start(rest_shards, lands[0], "rest")
            sm_next = dict(sm_now)
            sm_next["mix_norm"] = sm_now["mix_norm"] + flying["rest"][3][0, 0]
            return sm_next
        sems_r, src_r, land_r, _ = flying["rest"]
        lands = _gather_forward(_gather_wait(sems_r, src_r, land_r, act, "rest"), "rest")
        _prepare_weights(dict(zip(rest, lands)), wts)
        return sm_now

    pos = _position()
    flight = {}

    groups = {"l1": (1, BIG), "l0a": (0, [nm for nm in BIG if not nm.startswith("ffn1")]),
              "l0b": (0, [nm for nm in BIG if nm.startswith("ffn1")])}
    pending = {}

    def to_chips(key, full, sib):
        psum = _sum2_bf16(pos, full, sib, name=f"chip_sum_{key}")
        s2 = _split_start(_stage2_copies, psum, [((3,) + p.shape[1:], p.dtype) for p in psum], 3 * len(psum),
                          f"reduce_stage2_start_{key}")
        flight[key] = (full, sib, s2)
        return s2[3][0, 0]

    def grads_ready(l, stage, g, sm_now):
        behind, tok = None, None
        if (l, stage) == (1, "ffn1"):
            full = [_chip_major(nm, g[nm]) for nm in BIG]
            pending["l1"] = _split_start(_stage1_copies, full, [((N_CHIPS, f.shape[1] // 2, f.shape[2]), F32) for f in full],
                                         len(full), "reduce_stage1_start_l1")
            behind, tok = "ffn2_norm", pending["l1"][3][0, 0]
        elif (l, stage) == (0, "ffn2"):
            sems1, full_thru, sib_land, _ = pending["l1"]
            full, sib = _split_wait(_stage1_copies, sems1, full_thru, sib_land, g["ffn2_w_down"], "reduce_stage1_wait_l1")
            tok = to_chips("l1", full, sib)
        elif (l, stage) == (0, "mix"):
            full = [_chip_major(nm, g[nm]) for nm in groups["l0a"][1]]
            behind, tok = "ffn1_norm", to_chips("l0a", full, _reduce_stage1(full, "l0a"))
        elif (l, stage) == (0, "ffn1"):
            full = [_chip_major(nm, g[nm]) for nm in groups["l0b"][1]]
            pending["l0b"] = _split_start(_stage1_copies, full, [((N_CHIPS, f.shape[1] // 2, f.shape[2]), F32) for f in full],
                                          len(full), "reduce_stage1_start_l0b")
        if behind is None:
            return sm_now, tok
        sm_next = dict(sm_now)
        sm_next[behind] = sm_now[behind] + tok
        return sm_next, tok

    loss, dx, grads, g_final = _local_step(x[0], loss_target[0], wts, sm, late_weights, grads_ready)

    def view2d(a):
        return a.reshape(a.size // a.shape[-1], a.shape[-1])

    after = pending["l0b"][3]
    done = {nm: None for nm in BIG}
    for key in ("l1", "l0a", "l0b"):
        l, names = groups[key]
        full, sib, (sems2, ps_thru, lands2, _) = flight[key]
        _, recv = _split_wait(_stage2_copies, sems2, ps_thru, lands2, after, f"reduce_stage2_wait_{key}")
        whole = _reduce_stage3(_sum5(pos, full, sib, recv, name=f"grad_sum_{key}"), key)
        for nm, g_l in zip(names, whole):
            if nm == "w_in":
                g_l = _unpad_w_in(g_l)
            tr = max(t for t in (512, 352, 256, 128) if g_l.shape[0] % t == 0)
            done[nm] = _adamw_layer(view2d(params[nm]), g_l, view2d(mom[nm]), view2d(var[nm]), l, done[nm],
                                    name=f"adamw_{nm}_{l}", tr=tr)
        after = done[names[-1]][0][-8:, 0:128]
        if key == "l1":
            small_g, loss = _unpack_small(_allreduce_small(_pack_small(grads, g_final, loss)), params)
            sems1, full_thru, sib_land, _ = pending["l0b"]
            full_b, sib_b = _split_wait(_stage1_copies, sems1, full_thru, sib_land, after + small_g["final_norm"][0],
                                        "reduce_stage1_wait_l0b")
            after = after + to_chips("l0b", full_b, sib_b)
    gw, delta, new_m, new_v = dict(small_g), {}, {}, {}
    for nm in BIG:
        delta[nm], new_m[nm], new_v[nm], gw[nm] = [a.reshape(params[nm].shape) for a in done[nm]]
    for nm in small_g:
        delta[nm], new_m[nm], new_v[nm] = _update(nm, params[nm], gw[nm], mom[nm], var[nm])
    return (loss, dx[None], *[gw[n] for n in WEIGHTS], *[delta[n] for n in WEIGHTS], *[new_m[n] for n in WEIGHTS],
            *[new_v[n] for n in WEIGHTS])
```

```python
import functools
import math

import jax
import jax.numpy as jnp
import numpy as np
from jax import lax
from jax.experimental import pallas as pl
from jax.experimental.pallas import tpu as pltpu

F32 = jnp.float32
BF16 = jnp.bfloat16
MESH = pl.DeviceIdType.MESH
HBM_SPEC = pl.BlockSpec(memory_space=pltpu.HBM)

D = 1024
DEPTH = 2
D_FF = 2816
FF_SHARD = 1408
N_CHIPS = 4
H = 6
NOPE, ROPE, VDIM = 64, 32, 64
HALF_ROPE = ROPE // 2
Q_RANK, KV_RANK = 256, 128
POOL_W = 256
FOX_D = 64
N_IN = 1830
NZ = 2048
ROPE_THETA = 10000.0
EPS = 1e-6
POOL_HALO = 16
Z_QA, Z_KVA, Z_KR, Z_POOL, Z_FOX, Z_F = 0, 256, 384, 512, 768, 1920

ADAM_LR, ADAM_B1, ADAM_B2, ADAM_EPS, ADAM_WD, ADAM_STEP = 0.001, 0.9, 0.999, 1e-08, 0.01, 10

VMEM_LIMIT_V7X = 56 * 1024 * 1024


def _cp(sem=None, vmem=VMEM_LIMIT_V7X):
    return pltpu.CompilerParams(dimension_semantics=sem, vmem_limit_bytes=vmem)


def _sigmoid(x):
    return 0.5 * jnp.tanh(0.5 * x) + 0.5


def _dot(a, b, dims):
    return lax.dot_general(a, b, (dims, ((), ())), preferred_element_type=F32)


NN = ((1,), (0,))
NT = ((1,), (1,))
TN = ((0,), (0,))


def _mm(a, b, mode, *, name, out_dtype=F32, add=None, alpha=None, tm=512, tn=512, tk=512, n_major_out=False):
    if mode == "nn":
        (m, k), (k2, n) = a.shape, b.shape
    elif mode == "nt":
        (m, k), (n, k2) = a.shape, b.shape
    else:
        (k, m), (k2, n) = a.shape, b.shape
    assert k == k2
    tm, tn, tk = min(tm, m), min(tn, n), min(tk, k)
    assert m % tm == 0 and n % tn == 0 and k % tk == 0, (name, m, n, k, tm, tn, tk)
    nk = k // tk
    dims = {"nn": NN, "nt": NT, "tn": TN}[mode]
    a_spec = pl.BlockSpec((tk, tm), lambda i, j, kk: (kk, i)) if mode == "tn" else pl.BlockSpec((tm, tk), lambda i, j, kk: (i, kk))
    b_spec = pl.BlockSpec((tn, tk), lambda i, j, kk: (j, kk)) if mode == "nt" else pl.BlockSpec((tk, tn), lambda i, j, kk: (kk, j))
    in_specs = [a_spec, b_spec]
    args = [a, b]
    if add is not None:
        in_specs.append(pl.BlockSpec((tm, tn), lambda i, j, kk: (i, j)))
        args.append(add)
    if n_major_out:
        out_shape = jax.ShapeDtypeStruct((n // tn, m, tn), out_dtype)
        out_spec = pl.BlockSpec((None, tm, tn), lambda i, j, kk: (j, i, 0))
    else:
        out_shape = jax.ShapeDtypeStruct((m, n), out_dtype)
        out_spec = pl.BlockSpec((tm, tn), lambda i, j, kk: (i, j))

    def body(*refs):
        a_ref, b_ref = refs[0], refs[1]
        add_ref = refs[2] if add is not None else None
        o_ref, acc = refs[-2], refs[-1]
        kk = pl.program_id(2)

        @pl.when(kk == 0)
        def _():
            acc[...] = jnp.zeros_like(acc)

        acc[...] += _dot(a_ref[...].astype(BF16), b_ref[...].astype(BF16), dims)

        @pl.when(kk == nk - 1)
        def _():
            r = acc[...]
            if alpha is not None:
                r = r * alpha
            if add_ref is not None:
                r = r + add_ref[...].astype(F32)
            o_ref[...] = r.astype(out_dtype)

    return pl.pallas_call(
        body, name=name, grid=(m // tm, n // tn, nk), in_specs=in_specs, out_specs=out_spec, out_shape=out_shape,
        scratch_shapes=[pltpu.VMEM((tm, tn), F32)],
        compiler_params=_cp(("parallel", "parallel", "arbitrary")),
    )(*args)


def _norm_mm(x, col_block, gain, w, *, name, tm=512):
    s = x.shape[0]
    k, n = w.shape
    tm = min(tm, s)

    def body(x_ref, g_ref, w_ref, z_ref, h_ref):
        xv = x_ref[...]
        r = lax.rsqrt(jnp.mean(xv * xv, axis=-1, keepdims=True) + EPS)
        hv = (xv * r * g_ref[...]).astype(BF16)
        h_ref[...] = hv
        z_ref[...] = _dot(hv, w_ref[...], NN)

    return pl.pallas_call(
        body, name=name, grid=(s // tm,),
        in_specs=[pl.BlockSpec((tm, k), lambda i: (i, col_block)), pl.BlockSpec((1, k), lambda i: (0, 0)),
                  pl.BlockSpec((k, n), lambda i: (0, 0))],
        out_specs=[pl.BlockSpec((tm, n), lambda i: (i, 0)), pl.BlockSpec((tm, k), lambda i: (i, 0))],
        out_shape=[jax.ShapeDtypeStruct((s, n), F32), jax.ShapeDtypeStruct((s, k), BF16)],
        compiler_params=_cp(("parallel",)),
    )(x, gain.reshape(1, k), w)


def _rmsnorm_bwd(x, col_block, gain, dh, dres=None, *, name, tm=512):
    s = x.shape[0]
    k = gain.shape[-1]
    tm = min(tm, s)

    def body(*refs):
        x_ref, g_ref, dh_ref = refs[0], refs[1], refs[2]
        dres_ref = refs[3] if dres is not None else None
        dx_ref, dg_ref = refs[-2], refs[-1]
        xv = x_ref[...]
        r = lax.rsqrt(jnp.mean(xv * xv, axis=-1, keepdims=True) + EPS)
        dhv = dh_ref[...].astype(F32)
        a = dhv * g_ref[...]
        dx = r * a - xv * (r * r * r) * jnp.mean(a * xv, axis=-1, keepdims=True)
        if dres_ref is not None:
            dx = dx + dres_ref[...]
        dx_ref[...] = dx

        @pl.when(pl.program_id(0) == 0)
        def _():
            dg_ref[...] = jnp.zeros_like(dg_ref)

        dg_ref[...] += jnp.sum(dhv * xv * r, axis=0, keepdims=True)

    in_specs = [pl.BlockSpec((tm, k), lambda i: (i, col_block)), pl.BlockSpec((1, k), lambda i: (0, 0)),
                pl.BlockSpec((tm, k), lambda i: (i, 0))]
    args = [x, gain.reshape(1, k), dh]
    if dres is not None:
        in_specs.append(pl.BlockSpec((tm, k), lambda i: (i, 0)))
        args.append(dres)
    dx, dg = pl.pallas_call(
        body, name=name, grid=(s // tm,), in_specs=in_specs,
        out_specs=[pl.BlockSpec((tm, k), lambda i: (i, 0)), pl.BlockSpec((1, k), lambda i: (0, 0))],
        out_shape=[jax.ShapeDtypeStruct((s, k), F32), jax.ShapeDtypeStruct((1, k), F32)],
        compiler_params=_cp(("arbitrary",)),
    )(*args)
    return dx, dg.reshape(k)


def _ffn_fwd(x, gain, w_gu4, w_d2, *, name, tm=256):
    s = x.shape[0]
    tm = min(tm, s)

    def body(x_ref, g_ref, wgu_ref, wd_ref, xo_ref, dgu_ref, act_ref):
        xv = x_ref[...]
        r = lax.rsqrt(jnp.mean(xv * xv, axis=-1, keepdims=True) + EPS)
        hv = (xv * r * g_ref[...]).astype(BF16)
        y = jnp.zeros((tm, D), F32)
        for j in range(2):
            g = _dot(hv, wgu_ref[j], NN)
            u = _dot(hv, wgu_ref[2 + j], NN)
            sg = _sigmoid(g)
            silu = g * sg
            dgu_ref[:, j * FF_SHARD:(j + 1) * FF_SHARD] = (u * (sg * (1.0 + g * (1.0 - sg)))).astype(BF16)
            dgu_ref[:, D_FF + j * FF_SHARD:D_FF + (j + 1) * FF_SHARD] = silu.astype(BF16)
            act = (silu * u).astype(BF16)
            act_ref[:, j * FF_SHARD:(j + 1) * FF_SHARD] = act
            y = y + _dot(act, wd_ref[j], NN)
        xo_ref[...] = xv + 0.5 * y

    row = lambda i: (i, 0)
    return pl.pallas_call(
        body, name=name, grid=(s // tm,),
        in_specs=[pl.BlockSpec((tm, D), row), pl.BlockSpec((1, D), lambda i: (0, 0)),
                  pl.BlockSpec((N_CHIPS, D, FF_SHARD), lambda i: (0, 0, 0), pipeline_mode=pl.Buffered(1)),
                  pl.BlockSpec((2, FF_SHARD, D), lambda i: (0, 0, 0), pipeline_mode=pl.Buffered(1))],
        out_specs=[pl.BlockSpec((tm, D), row), pl.BlockSpec((tm, 2 * D_FF), row), pl.BlockSpec((tm, D_FF), row)],
        out_shape=[jax.ShapeDtypeStruct((s, D), F32), jax.ShapeDtypeStruct((s, 2 * D_FF), BF16),
                   jax.ShapeDtypeStruct((s, D_FF), BF16)],
        compiler_params=_cp(("parallel",)),
    )(x, gain.reshape(1, D), w_gu4, w_d2)


FFN_ROW_CHUNK = 32


def _ffn_bwd(x, dxo, dloc, gain, w_gu4, w_d2, *, name, tm=256):
    s = x.shape[0]
    tm = min(tm, s)

    def body(x_ref, dxo_ref, dloc_ref, g_ref, wgu_ref, wd_ref, dx_ref, dgu_ref, h_ref, dy_ref, dg_ref):
        xv = x_ref[...]
        r = lax.rsqrt(jnp.mean(xv * xv, axis=-1, keepdims=True) + EPS)
        xh = xv * r
        h_ref[...] = (xh * g_ref[...]).astype(BF16)
        dxov = dxo_ref[...]
        dy = (0.5 * dxov).astype(BF16)
        dy_ref[...] = dy
        gcols = [slice(j * FF_SHARD, (j + 1) * FF_SHARD) for j in range(2)]
        ucols = [slice(D_FF + j * FF_SHARD, D_FF + (j + 1) * FF_SHARD) for j in range(2)]
        dacts = [_dot(dy, wd_ref[j], NT) for j in range(2)]
        for r0 in range(0, tm, FFN_ROW_CHUNK):
            rows = slice(r0, r0 + FFN_ROW_CHUNK)
            for j in range(2):
                da = dacts[j][rows]
                dgu_ref[rows, gcols[j]] = (da * dloc_ref[rows, gcols[j]].astype(F32)).astype(BF16)
                dgu_ref[rows, ucols[j]] = (da * dloc_ref[rows, ucols[j]].astype(F32)).astype(BF16)
        dh = jnp.zeros((tm, D), F32)
        for j in range(2):
            dh = dh + _dot(dgu_ref[:, gcols[j]], wgu_ref[j], NT) + _dot(dgu_ref[:, ucols[j]], wgu_ref[2 + j], NT)
        a = dh * g_ref[...]
        dx_ref[...] = dxov + r * a - xh * (r * jnp.mean(a * xh, axis=-1, keepdims=True))

        @pl.when(pl.program_id(0) == 0)
        def _():
            dg_ref[...] = jnp.zeros_like(dg_ref)

        dg_ref[...] += jnp.sum(dh * xh, axis=0, keepdims=True)

    row = lambda i: (i, 0)
    outs = pl.pallas_call(
        body, name=name, grid=(s // tm,),
        in_specs=[pl.BlockSpec((tm, D), row), pl.BlockSpec((tm, D), row), pl.BlockSpec((tm, 2 * D_FF), row),
                  pl.BlockSpec((1, D), lambda i: (0, 0)),
                  pl.BlockSpec((N_CHIPS, D, FF_SHARD), lambda i: (0, 0, 0), pipeline_mode=pl.Buffered(1)),
                  pl.BlockSpec((2, FF_SHARD, D), lambda i: (0, 0, 0), pipeline_mode=pl.Buffered(1))],
        out_specs=[pl.BlockSpec((tm, D), row), pl.BlockSpec((tm, 2 * D_FF), row),
                   pl.BlockSpec((tm, D), row), pl.BlockSpec((tm, D), row), pl.BlockSpec((1, D), lambda i: (0, 0))],
        out_shape=[jax.ShapeDtypeStruct((s, D), F32), jax.ShapeDtypeStruct((s, 2 * D_FF), BF16),
                   jax.ShapeDtypeStruct((s, D), BF16), jax.ShapeDtypeStruct((s, D), BF16), jax.ShapeDtypeStruct((1, D), F32)],
        compiler_params=_cp(("arbitrary",)),
    )(x, dxo, dloc, gain.reshape(1, D), w_gu4, w_d2)
    dx, dgu, h, dy, dg = outs
    return dx, dgu, h, dy, dg.reshape(D)


DA = 128
SCALE_MLA = 1.0 / math.sqrt(NOPE + ROPE)
SCALE_FOX = 1.0 / math.sqrt(FOX_D)


def _causal_blocks(nb, key_major):
    if key_major:
        pairs = [(i, j) for j in range(nb) for i in range(j, nb)]
    else:
        pairs = [(i, j) for i in range(nb) for j in range(i + 1)]
    return (jnp.asarray(np.array([p[0] for p in pairs], np.int32)), jnp.asarray(np.array([p[1] for p in pairs], np.int32)))


HEADS_PER_STEP = 3
ROW_CHUNK = 64

def _col_to_row(col):
    return jnp.broadcast_to(col, (col.shape[0], DA)).T[0:1, :]


def _attn_fwd(qa, ka, va, dv, *, name, t=512):
    h, s, _ = qa.shape
    t = min(t, s)
    nb = s // t
    g = H
    qi, kj = _causal_blocks(nb, key_major=False)

    rc = min(ROW_CHUNK, t)

    def body(qi_ref, kj_ref, q_ref, k_ref, v_ref, o_ref, lse_ref, m_sc, acc_sc, p_sc, a_sc):
        n = pl.program_id(1)
        i, j = qi_ref[n], kj_ref[n]

        @pl.when(j == 0)
        def _():
            m_sc[...] = jnp.full_like(m_sc, -jnp.inf)
            acc_sc[...] = jnp.zeros_like(acc_sc)

        def step(masked):
            scs = [_dot(q_ref[hh], k_ref[hh], NT) for hh in range(g)]
            for r0 in range(0, t, rc):
                rows = slice(r0, r0 + rc)
                for hh in range(g):
                    sr = scs[hh][rows]
                    if masked:
                        row = lax.broadcasted_iota(jnp.int32, (rc, t), 0) + r0
                        col = lax.broadcasted_iota(jnp.int32, (rc, t), 1)
                        sr = jnp.where(col <= row, sr, -jnp.inf)
                    tiles = [sr[:, c0:c0 + DA] for c0 in range(0, t, DA)]
                    top = tiles[0]
                    for tile in tiles[1:]:
                        top = jnp.maximum(top, tile)
                    m_old = m_sc[hh, rows]
                    m_new = jnp.maximum(m_old, jnp.max(top, axis=-1, keepdims=True))
                    for c0, tile in zip(range(0, t, DA), tiles):
                        p_sc[hh, rows, c0:c0 + DA] = jnp.exp(tile - m_new).astype(BF16)
                    a_sc[hh, rows] = jnp.exp(m_old - m_new)
                    m_sc[hh, rows] = m_new
            for hh in range(g):
                acc_sc[hh] = a_sc[hh] * acc_sc[hh] + _dot(p_sc[hh], v_ref[hh], NN)

        @pl.when(j < i)
        def _():
            step(False)

        @pl.when(j == i)
        def _():
            step(True)
            for hh in range(g):
                acc = acc_sc[hh]
                l = acc[:, dv:dv + 1]
                o_ref[hh] = acc[:, :dv] / l
                lse_ref[hh] = _col_to_row(m_sc[hh][:, 0:1] + jnp.log(l))

    qmap = lambda hg, n, qi_r, kj_r: (hg, qi_r[n], 0)
    kmap = lambda hg, n, qi_r, kj_r: (hg, kj_r[n], 0)
    return pl.pallas_call(
        body, name=name,
        grid_spec=pltpu.PrefetchScalarGridSpec(
            num_scalar_prefetch=2, grid=(h // g, qi.shape[0]),
            in_specs=[pl.BlockSpec((g, t, DA), qmap), pl.BlockSpec((g, t, DA), kmap), pl.BlockSpec((g, t, DA), kmap)],
            out_specs=[pl.BlockSpec((g, t, dv), qmap), pl.BlockSpec((g, 1, t), lambda hg, n, qi_r, kj_r: (hg, 0, qi_r[n]))],
            scratch_shapes=[pltpu.VMEM((g, t, DA), F32), pltpu.VMEM((g, t, DA), F32), pltpu.VMEM((g, t, t), BF16),
                            pltpu.VMEM((g, t, DA), F32)]),
        out_shape=[jax.ShapeDtypeStruct((h, s, dv), F32), jax.ShapeDtypeStruct((h, 1, s), F32)],
        compiler_params=_cp(("parallel", "arbitrary")),
    )(qi, kj, qa, ka, va)


def _attn_bwd(qa, ka, va, doa, lse_row, delta_row, decay, *, name, t=512):
    h, s, _ = qa.shape
    t = min(t, s)
    nb = s // t
    g = HEADS_PER_STEP
    rc = min(ROW_CHUNK, t)
    qi, kj = _causal_blocks(nb, key_major=True)
    nsteps = qi.shape[0]

    def body(*refs):
        qi_ref, kj_ref, q_ref, k_ref, v_ref, do_ref, lse_ref, dl_ref = refs[:8]
        p_sc, ds_sc = refs[-2:]
        if decay:
            dq_ref, dk_ref, dv_ref, dcq_ref, dck_ref, dq_acc, dk_acc, dv_acc, dcq_acc, dck_acc = refs[8:-2]
        else:
            dq_ref, dk_ref, dv_ref, dq_acc, dk_acc, dv_acc = refs[8:-2]
        n = pl.program_id(1)
        i, j = qi_ref[n], kj_ref[n]

        @pl.when(n == 0)
        def _():
            dq_acc[...] = jnp.zeros_like(dq_acc)
            if decay:
                dcq_acc[...] = jnp.zeros_like(dcq_acc)

        @pl.when(i == j)
        def _():
            dk_acc[...] = jnp.zeros_like(dk_acc)
            dv_acc[...] = jnp.zeros_like(dv_acc)
            if decay:
                dck_acc[...] = jnp.zeros_like(dck_acc)

        def step(masked):
            sts = [_dot(k_ref[hh], q_ref[hh], NT) for hh in range(g)]
            dpts = [_dot(v_ref[hh], do_ref[hh], NT) for hh in range(g)]
            dcq = [jnp.zeros((1, t), F32) for _ in range(g)]
            for r0 in range(0, t, rc):
                rows = slice(r0, r0 + rc)
                for hh in range(g):
                    st = sts[hh][rows]
                    if masked:
                        row = lax.broadcasted_iota(jnp.int32, (rc, t), 0) + r0
                        col = lax.broadcasted_iota(jnp.int32, (rc, t), 1)
                        st = jnp.where(row <= col, st, -jnp.inf)
                    pt = jnp.exp(st - lse_ref[hh])
                    dst = pt * (dpts[hh][rows] - dl_ref[hh])
                    p_sc[hh, rows] = pt.astype(BF16)
                    ds_sc[hh, rows] = dst.astype(BF16)
                    if decay:
                        dcq[hh] = dcq[hh] + jnp.sum(dst, axis=0, keepdims=True)
                        dck_acc[hh, rows] -= jnp.sum(dst, axis=1, keepdims=True)
            for hh in range(g):
                dv_acc[hh] += _dot(p_sc[hh], do_ref[hh], NN)
                dk_acc[hh] += _dot(ds_sc[hh], q_ref[hh], NN)
                dq_acc[hh, i] += _dot(ds_sc[hh], k_ref[hh], TN)
                if decay:
                    dcq_acc[hh, i] += dcq[hh]

        @pl.when(i > j)
        def _():
            step(False)

        @pl.when(i == j)
        def _():
            step(True)

        @pl.when(i == nb - 1)
        def _():
            dk_ref[...] = dk_acc[...]
            dv_ref[...] = dv_acc[...]
            if decay:
                for hh in range(g):
                    dck_ref[hh] = _col_to_row(dck_acc[hh])

        @pl.when(n == nsteps - 1)
        def _():
            dq_ref[...] = dq_acc[...]
            if decay:
                dcq_ref[...] = dcq_acc[...]

    kmap = lambda hg, n, qi_r, kj_r: (hg, kj_r[n], 0)
    qmap = lambda hg, n, qi_r, kj_r: (hg, qi_r[n], 0)
    qrow = lambda hg, n, qi_r, kj_r: (hg, 0, qi_r[n])
    krow = lambda hg, n, qi_r, kj_r: (hg, 0, kj_r[n])
    whole = lambda hg, n, qi_r, kj_r: (hg, 0, 0, 0)
    in_specs = [pl.BlockSpec((g, t, DA), qmap), pl.BlockSpec((g, t, DA), kmap), pl.BlockSpec((g, t, DA), kmap),
                pl.BlockSpec((g, t, DA), qmap), pl.BlockSpec((g, 1, t), qrow), pl.BlockSpec((g, 1, t), qrow)]
    out_specs = [pl.BlockSpec((g, nb, t, DA), whole), pl.BlockSpec((g, t, DA), kmap), pl.BlockSpec((g, t, DA), kmap)]
    out_shape = [jax.ShapeDtypeStruct((h, nb, t, DA), F32), jax.ShapeDtypeStruct((h, s, DA), F32), jax.ShapeDtypeStruct((h, s, DA), F32)]
    scratch = [pltpu.VMEM((g, nb, t, DA), F32), pltpu.VMEM((g, t, DA), F32), pltpu.VMEM((g, t, DA), F32)]
    if decay:
        out_specs += [pl.BlockSpec((g, nb, 1, t), whole), pl.BlockSpec((g, 1, t), krow)]
        out_shape += [jax.ShapeDtypeStruct((h, nb, 1, t), F32), jax.ShapeDtypeStruct((h, 1, s), F32)]
        scratch += [pltpu.VMEM((g, nb, 1, t), F32), pltpu.VMEM((g, t, 1), F32)]
    scratch += [pltpu.VMEM((g, t, t), BF16), pltpu.VMEM((g, t, t), BF16)]
    outs = pl.pallas_call(
        body, name=name,
        grid_spec=pltpu.PrefetchScalarGridSpec(num_scalar_prefetch=2, grid=(h // g, nsteps), in_specs=in_specs, out_specs=out_specs,
                                               scratch_shapes=scratch),
        out_shape=out_shape, compiler_params=_cp(("parallel", "arbitrary")),
    )(qi, kj, qa, ka, va, doa, lse_row, delta_row)
    outs = list(outs)
    outs[0] = outs[0].reshape(h, s, DA)
    if decay:
        outs[3] = outs[3].reshape(h, 1, s)
    return outs


def _sel(rows, cols, pairs, value=1.0):
    m = np.zeros((rows, cols), np.float32)
    for r, c in pairs:
        m[r, c] = value
    return jnp.asarray(m, BF16)


def _lane_row(lanes):
    m = np.zeros((1, DA), np.float32)
    m[0, list(lanes)] = 1.0
    return jnp.asarray(m)


def _rms(xv, gain):
    r = lax.rsqrt(jnp.mean(xv * xv, axis=-1, keepdims=True) + EPS)
    return xv * r * gain


def _mla_q_prep(z, gain, wq_a, wq_b, cq, sq, *, name, tm=512):
    s = z.shape[0]
    tm = min(tm, s)

    def body(z_ref, g_ref, wa_ref, wb_ref, c_ref, s_ref, qa_ref, qn_ref):
        qn = _rms(z_ref[...], g_ref[...]).astype(BF16)
        qn_ref[...] = qn
        c, sn = c_ref[...], s_ref[...]
        for hh in range(H):
            cols = slice(hh * DA, (hh + 1) * DA)
            qa_ref[hh] = (_dot(qn, wa_ref[:, cols], NN) * c + _dot(qn, wb_ref[:, cols], NN) * sn).astype(BF16)

    row = lambda i: (i, 0)
    fixed = lambda i: (0, 0)
    return pl.pallas_call(
        body, name=name, grid=(s // tm,),
        in_specs=[pl.BlockSpec((tm, Q_RANK), lambda i: (i, Z_QA // Q_RANK)), pl.BlockSpec((1, Q_RANK), fixed),
                  pl.BlockSpec((Q_RANK, H * DA), fixed), pl.BlockSpec((Q_RANK, H * DA), fixed),
                  pl.BlockSpec((tm, DA), row), pl.BlockSpec((tm, DA), row)],
        out_specs=[pl.BlockSpec((H, tm, DA), lambda i: (0, i, 0)), pl.BlockSpec((tm, Q_RANK), row)],
        out_shape=[jax.ShapeDtypeStruct((H, s, DA), BF16), jax.ShapeDtypeStruct((s, Q_RANK), BF16)],
        compiler_params=_cp(("parallel",)),
    )(z, gain.reshape(1, Q_RANK), wq_a, wq_b, cq, sq)


def _mla_kv_prep(z, gain, wk, wv, ck, sk, *, name, tm=512):
    s = z.shape[0]
    tm = min(tm, s)
    one = _lane_row([VDIM])

    def body(zkv_ref, z3_ref, z15_ref, g_ref, wk_ref, wv_ref, c_ref, s_ref, one_ref, ka_ref, va_ref, kvn_ref):
        kvn = _rms(zkv_ref[...], g_ref[...]).astype(BF16)
        kvn_ref[...] = kvn
        kpe = z3_ref[...] * c_ref[...] + z15_ref[...] * s_ref[...]
        for hh in range(H):
            cols = slice(hh * DA, (hh + 1) * DA)
            ka_ref[hh] = (_dot(kvn, wk_ref[:, cols], NN) + kpe).astype(BF16)
            va_ref[hh] = (_dot(kvn, wv_ref[:, cols], NN) + one_ref[...]).astype(BF16)

    row = lambda i: (i, 0)
    fixed = lambda i: (0, 0)
    blk = lambda c: pl.BlockSpec((tm, DA), lambda i: (i, c))
    heads = pl.BlockSpec((H, tm, DA), lambda i: (0, i, 0))
    return pl.pallas_call(
        body, name=name, grid=(s // tm,),
        in_specs=[blk(Z_KVA // DA), blk(Z_KR // DA), blk(Z_F // DA), pl.BlockSpec((1, KV_RANK), fixed),
                  pl.BlockSpec((KV_RANK, H * DA), fixed), pl.BlockSpec((KV_RANK, H * DA), fixed),
                  pl.BlockSpec((tm, DA), row), pl.BlockSpec((tm, DA), row), pl.BlockSpec((1, DA), fixed)],
        out_specs=[heads, heads, pl.BlockSpec((tm, KV_RANK), row)],
        out_shape=[jax.ShapeDtypeStruct((H, s, DA), BF16), jax.ShapeDtypeStruct((H, s, DA), BF16),
                   jax.ShapeDtypeStruct((s, KV_RANK), BF16)],
        compiler_params=_cp(("parallel",)),
    )(z, z, z, gain.reshape(1, KV_RANK), wk, wv, ck, sk, one)


DEC_C = (FOX_D, FOX_D + 1, FOX_D + 2)
DEC_1 = (FOX_D + 3, FOX_D + 4, FOX_D + 5)


def _fox_prep(z, c3t, *, name, tm=512):
    s = z.shape[0]
    tm = min(tm, s)
    w = H * FOX_D
    left = [(r, r) for r in range(FOX_D)]
    right = [(FOX_D + r, r) for r in range(FOX_D)]
    pq = jnp.stack([_sel(DA, DA, left, SCALE_FOX), _sel(DA, DA, right, SCALE_FOX)])
    pk = jnp.stack([_sel(DA, DA, left), _sel(DA, DA, right)])
    pcq = jnp.stack([_sel(32, DA, [(hh + 8 * k, DEC_C[k]) for k in range(3)]) for hh in range(H)])
    pck = jnp.stack([_sel(32, DA, [(hh + 8 * k, DEC_1[k]) for k in range(3)], -1.0) for hh in range(H)])
    rows3 = jnp.concatenate([_lane_row(DEC_1), _lane_row(DEC_C), _lane_row([FOX_D])], axis=0)

    def body(zq_ref, zk_ref, zv_ref, c_ref, pq_ref, pk_ref, pcq_ref, pck_ref, r_ref, qa_ref, ka_ref, va_ref):
        c3 = c_ref[...]
        for pair in range(H // 2):
            lanes = slice(pair * DA, (pair + 1) * DA)
            zq, zk, zv = zq_ref[:, lanes].astype(BF16), zk_ref[:, lanes].astype(BF16), zv_ref[:, lanes].astype(BF16)
            for side in range(2):
                hh = 2 * pair + side
                qa_ref[hh] = (_dot(zq, pq_ref[side], NN) + _dot(c3, pcq_ref[hh], TN) + r_ref[0:1, :]).astype(BF16)
                ka_ref[hh] = (_dot(zk, pk_ref[side], NN) + _dot(c3, pck_ref[hh], TN) + r_ref[1:2, :]).astype(BF16)
                va_ref[hh] = (_dot(zv, pk_ref[side], NN) + r_ref[2:3, :]).astype(BF16)

    fixed2 = lambda i: (0, 0)
    fixed3 = lambda i: (0, 0, 0)
    heads = pl.BlockSpec((H, tm, DA), lambda i: (0, i, 0))
    zblk = lambda c: pl.BlockSpec((tm, w), lambda i: (i, c))
    return pl.pallas_call(
        body, name=name, grid=(s // tm,),
        in_specs=[zblk(Z_FOX // w), zblk(Z_FOX // w + 1), zblk(Z_FOX // w + 2), pl.BlockSpec((32, tm), lambda i: (0, i)),
                  pl.BlockSpec((2, DA, DA), fixed3), pl.BlockSpec((2, DA, DA), fixed3),
                  pl.BlockSpec((H, 32, DA), fixed3), pl.BlockSpec((H, 32, DA), fixed3), pl.BlockSpec((3, DA), fixed2)],
        out_specs=[heads, heads, heads], out_shape=[jax.ShapeDtypeStruct((H, s, DA), BF16)] * 3,
        compiler_params=_cp(("parallel",)),
    )(z, z, z, c3t, pq, pk, pcq, pck, rows3)


def _mix_out(oa, yb, oc, w_out, x1, *, name, tm=512):
    s = yb.shape[0]
    tm = min(tm, s)
    e2 = jnp.stack([_sel(VDIM, DA, [(r, r) for r in range(VDIM)]), _sel(VDIM, DA, [(r, VDIM + r) for r in range(VDIM)])])

    def body(oa_ref, yb_ref, oc_ref, e_ref, w_ref, x_ref, x2_ref, cat_ref):
        def pairs(o_ref):
            return [(_dot(o_ref[2 * p].astype(BF16), e_ref[0], NN) + _dot(o_ref[2 * p + 1].astype(BF16), e_ref[1], NN)).astype(BF16)
                    for p in range(H // 2)]

        cat = jnp.concatenate(pairs(oa_ref) + [yb_ref[...].astype(BF16)] + pairs(oc_ref), axis=1)
        cat_ref[...] = cat
        x2_ref[...] = x_ref[...] + _dot(cat, w_ref[...], NN)

    row = lambda i: (i, 0)
    heads = pl.BlockSpec((H, tm, VDIM), lambda i: (0, i, 0))
    return pl.pallas_call(
        body, name=name, grid=(s // tm,),
        in_specs=[heads, pl.BlockSpec((tm, POOL_W), row), heads, pl.BlockSpec((2, VDIM, DA), lambda i: (0, 0, 0)),
                  pl.BlockSpec((D, D), lambda i: (0, 0)), pl.BlockSpec((tm, D), row)],
        out_specs=[pl.BlockSpec((tm, D), row), pl.BlockSpec((tm, D), row)],
        out_shape=[jax.ShapeDtypeStruct((s, D), F32), jax.ShapeDtypeStruct((s, D), BF16)],
        compiler_params=_cp(("parallel",)),
    )(oa, yb, oc, e2, w_out, x1)


def _mix_out_bwd(dx2b, w_out, oa, oc, *, name, tm=512):
    s = dx2b.shape[0]
    tm = min(tm, s)
    f2 = jnp.stack([_sel(DA, DA, [(r, r) for r in range(VDIM)]), _sel(DA, DA, [(VDIM + r, r) for r in range(VDIM)])])
    nv = H * VDIM

    def body(dx_ref, w_ref, oa_ref, oc_ref, f_ref, doa_ref, doc_ref, dyb_ref, dla_ref, dlc_ref):
        dcat = _dot(dx_ref[...], w_ref[...], NT)
        dyb_ref[...] = dcat[:, nv:nv + POOL_W]
        for base, o_ref, do_ref, dl_ref in ((0, oa_ref, doa_ref, dla_ref), (nv + POOL_W, oc_ref, doc_ref, dlc_ref)):
            for p in range(H // 2):
                blk = dcat[:, base + p * DA:base + (p + 1) * DA].astype(BF16)
                for side in range(2):
                    hh = 2 * p + side
                    do = _dot(blk, f_ref[side], NN)
                    do_ref[hh] = do.astype(BF16)
                    dl_ref[hh] = _col_to_row(jnp.sum(do[:, :VDIM] * o_ref[hh], axis=-1, keepdims=True))

    row = lambda i: (i, 0)
    heads = lambda w: pl.BlockSpec((H, tm, w), lambda i: (0, i, 0))
    return pl.pallas_call(
        body, name=name, grid=(s // tm,),
        in_specs=[pl.BlockSpec((tm, D), row), pl.BlockSpec((D, D), lambda i: (0, 0)), heads(VDIM), heads(VDIM),
                  pl.BlockSpec((2, DA, DA), lambda i: (0, 0, 0))],
        out_specs=[heads(DA), heads(DA), pl.BlockSpec((tm, POOL_W), row),
                   pl.BlockSpec((H, 1, tm), lambda i: (0, 0, i)), pl.BlockSpec((H, 1, tm), lambda i: (0, 0, i))],
        out_shape=[jax.ShapeDtypeStruct((H, s, DA), BF16), jax.ShapeDtypeStruct((H, s, DA), BF16),
                   jax.ShapeDtypeStruct((s, POOL_W), F32), jax.ShapeDtypeStruct((H, 1, s), F32), jax.ShapeDtypeStruct((H, 1, s), F32)],
        compiler_params=_cp(("parallel",)),
    )(dx2b, w_out, oa, oc, f2)


def _mla_bwd_prep(dqa, dka, dva, dft, cq, sq, ck, sk, *, name, tm=512):
    s = dqa.shape[1]
    tm = min(tm, s)
    keep = _lane_row(range(NOPE))

    def body(dq_ref, dk_ref, dv_ref, dft_ref, cq_ref, sq_ref, ck_ref, sk_ref, keep_ref, dqab_ref, dkv_ref, dz3_ref, dz15_ref):
        cqv, sqv = cq_ref[...], sq_ref[...]
        dkpe = jnp.zeros((tm, DA), F32)
        for hh in range(H):
            lanes = slice(hh * DA, (hh + 1) * DA)
            dq = dq_ref[hh]
            dqab_ref[:, lanes] = (dq * cqv).astype(BF16)
            dqab_ref[:, H * DA + hh * DA:H * DA + (hh + 1) * DA] = (dq * sqv).astype(BF16)
            dk = dk_ref[hh]
            dkpe = dkpe + dk
            dkv_ref[:, lanes] = (dk * keep_ref[...]).astype(BF16)
            dkv_ref[:, H * DA + hh * DA:H * DA + (hh + 1) * DA] = (dv_ref[hh] * keep_ref[...]).astype(BF16)
        dz3_ref[...] = (dkpe * ck_ref[...]).astype(BF16)
        dz15_ref[...] = (dkpe * sk_ref[...] + dft_ref[...]).astype(BF16)

    row = lambda i: (i, 0)
    heads = pl.BlockSpec((H, tm, DA), lambda i: (0, i, 0))
    tab = pl.BlockSpec((tm, DA), row)
    return pl.pallas_call(
        body, name=name, grid=(s // tm,),
        in_specs=[heads, heads, heads, tab, tab, tab, tab, tab, pl.BlockSpec((1, DA), lambda i: (0, 0))],
        out_specs=[pl.BlockSpec((tm, 2 * H * DA), row), pl.BlockSpec((tm, 2 * H * DA), row), tab, tab],
        out_shape=[jax.ShapeDtypeStruct((s, 2 * H * DA), BF16), jax.ShapeDtypeStruct((s, 2 * H * DA), BF16),
                   jax.ShapeDtypeStruct((s, DA), BF16), jax.ShapeDtypeStruct((s, DA), BF16)],
        compiler_params=_cp(("parallel",)),
    )(dqa, dka, dva, dft, cq, sq, ck, sk, keep)


def _fox_bwd_prep(dfqa, dfka, dfva, *, name, tm=512):
    s = dfqa.shape[1]
    tm = min(tm, s)
    place = lambda v: jnp.stack([_sel(DA, DA, [(r, r) for r in range(FOX_D)], v), _sel(DA, DA, [(r, FOX_D + r) for r in range(FOX_D)], v)])
    gq, gk = place(SCALE_FOX), place(1.0)

    def body(dq_ref, dk_ref, dv_ref, gq_ref, gk_ref, dz_ref):
        for part, (d_ref, g_ref) in enumerate(((dq_ref, gq_ref), (dk_ref, gk_ref), (dv_ref, gk_ref))):
            for p in range(H // 2):
                blk = _dot(d_ref[2 * p].astype(BF16), g_ref[0], NN) + _dot(d_ref[2 * p + 1].astype(BF16), g_ref[1], NN)
                lo = part * H * FOX_D + p * DA
                dz_ref[:, lo:lo + DA] = blk.astype(BF16)

    heads = pl.BlockSpec((H, tm, DA), lambda i: (0, i, 0))
    sel = pl.BlockSpec((2, DA, DA), lambda i: (0, 0, 0))
    return pl.pallas_call(
        body, name=name, grid=(s // tm,), in_specs=[heads, heads, heads, sel, sel],
        out_specs=pl.BlockSpec((tm, 3 * H * FOX_D), lambda i: (i, 0)),
        out_shape=jax.ShapeDtypeStruct((s, 3 * H * FOX_D), BF16), compiler_params=_cp(("parallel",)),
    )(dfqa, dfka, dfva, gq, gk)


def _lane_scan(x, s, reverse):
    lane = lax.broadcasted_iota(jnp.int32, x.shape, 1)
    sh = 1
    while sh < s:
        if reverse:
            x = x + jnp.where(lane < s - sh, pltpu.roll(x, s - sh, axis=1), 0.0)
        else:
            x = x + jnp.where(lane >= sh, pltpu.roll(x, sh, axis=1), 0.0)
        sh *= 2
    return x


def _gate_fwd(z, col_block, bias, *, name):
    s = z.shape[0]

    def body(z_ref, b_ref, f_ref, c_ref):
        ft = z_ref[...].T[0:8, :]
        f_ref[...] = ft
        xg = ft + b_ref[...]
        lf = jnp.minimum(xg, 0.0) - jnp.log(1.0 + jnp.exp(-jnp.abs(xg)))
        c = _lane_scan(lf, s, False)
        hi = c.astype(BF16).astype(F32)
        r = c - hi
        mid = r.astype(BF16).astype(F32)
        lo = r - mid
        c_ref[...] = jnp.concatenate([hi, mid, lo, jnp.zeros_like(hi)], axis=0).astype(BF16)

    return pl.pallas_call(
        body, name=name, grid=(1,),
        in_specs=[pl.BlockSpec((s, 128), lambda i: (0, col_block)), pl.BlockSpec((8, 1), lambda i: (0, 0))],
        out_specs=[pl.BlockSpec((8, s), lambda i: (0, 0)), pl.BlockSpec((32, s), lambda i: (0, 0))],
        out_shape=[jax.ShapeDtypeStruct((8, s), F32), jax.ShapeDtypeStruct((32, s), BF16)],
        compiler_params=_cp(("arbitrary",)))(z, bias)


def _gate_bwd(ft, bias, dc, *, name):
    s = ft.shape[1]

    def body(f_ref, b_ref, dc_ref, df_ref, db_ref):
        xg = f_ref[...] + b_ref[...]
        dlf = _lane_scan(dc_ref[...], s, True)
        df = dlf * _sigmoid(-xg)
        db_ref[...] = jnp.sum(df, axis=-1, keepdims=True)
        df_ref[...] = jnp.concatenate([df, jnp.zeros((DA - 8, s), F32)], axis=0).T

    return pl.pallas_call(body, name=name, out_shape=[jax.ShapeDtypeStruct((s, DA), F32), jax.ShapeDtypeStruct((8, 1), F32)],
                          compiler_params=_cp())(ft, bias, dc)


def _pool_lane_consts(tm, i):
    lane = lax.broadcasted_iota(jnp.int32, (tm, POOL_W), 1)
    tok = lax.broadcasted_iota(jnp.int32, (tm, POOL_W), 0) + i * tm
    win = jnp.where(lane < 64, 2, jnp.where(lane < 128, 4, jnp.where(lane < 192, 8, 16)))
    cnt = jnp.minimum(tok + 1, win).astype(F32)
    return lane, tok, cnt


def _pick_window(lane, s2, s4, s8, s16):
    return jnp.where(lane < 64, s2, jnp.where(lane < 128, s4, jnp.where(lane < 192, s8, s16)))


def _pool_fwd(z, col_block, bd, scale, *, name, tm=512):
    s = z.shape[0]
    tm = min(tm, s)
    hb = tm // POOL_HALO

    def body(u_ref, halo_ref, bd_ref, sc_ref, y_ref, p_ref, buf):
        i = pl.program_id(0)
        buf[0:POOL_HALO, :] = halo_ref[...] * (i > 0).astype(F32)
        buf[POOL_HALO:, :] = u_ref[...]

        def back(k):
            return buf[POOL_HALO - k:POOL_HALO - k + tm, :]

        u = u_ref[...]
        s2 = u + back(1)
        s4 = s2 + back(2) + back(3)
        s8 = s4 + back(4) + back(5) + back(6) + back(7)
        s16 = s8
        for k in range(8, 16):
            s16 = s16 + back(k)
        lane, _, cnt = _pool_lane_consts(tm, i)
        pooled = (_pick_window(lane, s2, s4, s8, s16) / cnt - u).astype(BF16)
        p_ref[...] = pooled
        y_ref[...] = _dot(pooled, bd_ref[...], NN) * sc_ref[...]

    return pl.pallas_call(
        body, name=name, grid=(s // tm,),
        in_specs=[pl.BlockSpec((tm, POOL_W), lambda i: (i, col_block)),
                  pl.BlockSpec((POOL_HALO, POOL_W), lambda i: (jnp.maximum(i * hb - 1, 0), col_block)),
                  pl.BlockSpec((POOL_W, POOL_W), lambda i: (0, 0)), pl.BlockSpec((1, POOL_W), lambda i: (0, 0))],
        out_specs=[pl.BlockSpec((tm, POOL_W), lambda i: (i, 0)), pl.BlockSpec((tm, POOL_W), lambda i: (i, 0))],
        out_shape=[jax.ShapeDtypeStruct((s, POOL_W), F32), jax.ShapeDtypeStruct((s, POOL_W), BF16)],
        scratch_shapes=[pltpu.VMEM((tm + POOL_HALO, POOL_W), F32)],
        compiler_params=_cp(("parallel",)),
    )(z, z, bd, scale.reshape(1, POOL_W))


def _pool_bwd_a(dy, pooled, bd, scale, *, name, tm=512):
    s = dy.shape[0]
    tm = min(tm, s)

    def body(dy_ref, p_ref, bd_ref, sc_ref, dq_ref, dys_ref, dsc_ref):
        i = pl.program_id(0)
        dyv = dy_ref[...]
        y0 = _dot(p_ref[...], bd_ref[...], NN)
        dys = (dyv * sc_ref[...]).astype(BF16)
        dys_ref[...] = dys
        dp = _dot(dys, bd_ref[...], NT)
        _, _, cnt = _pool_lane_consts(tm, i)
        dq_ref[:, 0:POOL_W] = dp / cnt
        dq_ref[:, POOL_W:] = dp

        @pl.when(i == 0)
        def _():
            dsc_ref[...] = jnp.zeros_like(dsc_ref)

        dsc_ref[...] += jnp.sum(dyv * y0, axis=0, keepdims=True)

    row = lambda i: (i, 0)
    dq, dys, dsc = pl.pallas_call(
        body, name=name, grid=(s // tm,),
        in_specs=[pl.BlockSpec((tm, POOL_W), row), pl.BlockSpec((tm, POOL_W), row),
                  pl.BlockSpec((POOL_W, POOL_W), lambda i: (0, 0)), pl.BlockSpec((1, POOL_W), lambda i: (0, 0))],
        out_specs=[pl.BlockSpec((tm, 2 * POOL_W), row), pl.BlockSpec((tm, POOL_W), row), pl.BlockSpec((1, POOL_W), lambda i: (0, 0))],
        out_shape=[jax.ShapeDtypeStruct((s, 2 * POOL_W), F32), jax.ShapeDtypeStruct((s, POOL_W), BF16),
                   jax.ShapeDtypeStruct((1, POOL_W), F32)],
        compiler_params=_cp(("arbitrary",)),
    )(dy, pooled, bd, scale.reshape(1, POOL_W))
    return dq, dys, dsc.reshape(POOL_W)


def _pool_bwd_b(dq, *, name, tm=512):
    s = dq.shape[0]
    tm = min(tm, s)
    hb = tm // POOL_HALO
    nblk = s // tm

    def body(q_ref, dp_ref, halo_ref, du_ref, buf):
        i = pl.program_id(0)
        buf[0:tm, :] = q_ref[...]
        buf[tm:, :] = halo_ref[...] * (i < nblk - 1).astype(F32)

        def ahead(k):
            return buf[k:k + tm, :]

        q = q_ref[...]
        s2 = q + ahead(1)
        s4 = s2 + ahead(2) + ahead(3)
        s8 = s4 + ahead(4) + ahead(5) + ahead(6) + ahead(7)
        s16 = s8
        for k in range(8, 16):
            s16 = s16 + ahead(k)
        lane = lax.broadcasted_iota(jnp.int32, (tm, POOL_W), 1)
        du_ref[...] = _pick_window(lane, s2, s4, s8, s16) - dp_ref[...]

    return pl.pallas_call(
        body, name=name, grid=(nblk,),
        in_specs=[pl.BlockSpec((tm, POOL_W), lambda i: (i, 0)), pl.BlockSpec((tm, POOL_W), lambda i: (i, 1)),
                  pl.BlockSpec((POOL_HALO, POOL_W), lambda i: (jnp.minimum((i + 1) * hb, nblk * hb - 1), 0))],
        out_specs=pl.BlockSpec((tm, POOL_W), lambda i: (i, 0)),
        out_shape=jax.ShapeDtypeStruct((s, POOL_W), F32),
        scratch_shapes=[pltpu.VMEM((tm + POOL_HALO, POOL_W), F32)],
        compiler_params=_cp(("parallel",)),
    )(dq, dq, dq)


def _loss_head(x, gain, target, *, name, tm=512):
    s = x.shape[0]
    tm = min(tm, s)

    def body(x_ref, g_ref, t_ref, dx_ref, dg_ref, loss_ref):
        xv = x_ref[...]
        r = lax.rsqrt(jnp.mean(xv * xv, axis=-1, keepdims=True) + EPS)
        xh = xv * r
        err = xh * g_ref[...] - t_ref[...]
        dy = err * (1.0 / D)
        a = dy * g_ref[...]
        dx_ref[...] = r * a - xh * (r * jnp.mean(a * xh, axis=-1, keepdims=True))

        @pl.when(pl.program_id(0) == 0)
        def _():
            dg_ref[...] = jnp.zeros_like(dg_ref)
            loss_ref[...] = jnp.zeros_like(loss_ref)

        dg_ref[...] += jnp.sum(dy * xh, axis=0, keepdims=True)
        part = 0.5 * jnp.sum(jnp.mean(err * err, axis=-1, keepdims=True), axis=0, keepdims=True)
        loss_ref[...] += jnp.broadcast_to(part, loss_ref.shape)

    row = lambda i: (i, 0)
    dx, dg, loss = pl.pallas_call(
        body, name=name, grid=(s // tm,),
        in_specs=[pl.BlockSpec((tm, D), row), pl.BlockSpec((1, D), lambda i: (0, 0)), pl.BlockSpec((tm, D), row)],
        out_specs=[pl.BlockSpec((tm, D), row), pl.BlockSpec((1, D), lambda i: (0, 0)), pl.BlockSpec((1, 128), lambda i: (0, 0))],
        out_shape=[jax.ShapeDtypeStruct((s, D), F32), jax.ShapeDtypeStruct((1, D), F32), jax.ShapeDtypeStruct((1, 128), F32)],
        compiler_params=_cp(("arbitrary",)),
    )(x, gain.reshape(1, D), target)
    return dx, dg.reshape(D), loss[0, 0]


def _adamw(w, g, m, v, *, name, tr=512):
    rows, cols = w.shape
    tr = min(tr, rows)
    assert rows % tr == 0, (name, rows, tr)
    c_m = 1.0 - ADAM_B1
    c_v = 1.0 - ADAM_B2
    bc1 = 1.0 - ADAM_B1 ** ADAM_STEP
    bc2 = 1.0 - ADAM_B2 ** ADAM_STEP

    def body(w_ref, g_ref, m_ref, v_ref, d_ref, mo_ref, vo_ref):
        gv = g_ref[...]
        mn = ADAM_B1 * m_ref[...] + c_m * gv
        vn = ADAM_B2 * v_ref[...] + c_v * (gv * gv)
        mo_ref[...] = mn
        vo_ref[...] = vn
        d_ref[...] = -ADAM_LR * ((mn / bc1) / (jnp.sqrt(vn / bc2) + ADAM_EPS) + ADAM_WD * w_ref[...])

    spec = pl.BlockSpec((tr, cols), lambda i: (i, 0))
    return pl.pallas_call(body, name=name, grid=(rows // tr,), in_specs=[spec] * 4, out_specs=[spec] * 3,
                          out_shape=[jax.ShapeDtypeStruct((rows, cols), F32)] * 3,
                          compiler_params=_cp(("parallel",)))(w, g, m, v)


def _adamw_layer(w, g, m, v, layer, prev, *, name, tr):
    rows, cols = g.shape
    assert rows % tr == 0 and w.shape == (DEPTH * rows, cols), (name, w.shape, g.shape, tr)
    nblk = rows // tr
    c_m = 1.0 - ADAM_B1
    c_v = 1.0 - ADAM_B2
    bc1 = 1.0 - ADAM_B1 ** ADAM_STEP
    bc2 = 1.0 - ADAM_B2 ** ADAM_STEP
    n_prev = 0 if prev is None else 4

    def body(*refs):
        w_ref, g_ref, m_ref, v_ref = refs[:4]
        d_ref, mo_ref, vo_ref, go_ref = refs[4 + n_prev:]
        gv = g_ref[...]
        mn = ADAM_B1 * m_ref[...] + c_m * gv
        vn = ADAM_B2 * v_ref[...] + c_v * (gv * gv)
        mo_ref[...] = mn
        vo_ref[...] = vn
        go_ref[...] = gv
        d_ref[...] = -ADAM_LR * ((mn / bc1) / (jnp.sqrt(vn / bc2) + ADAM_EPS) + ADAM_WD * w_ref[...])

    stacked = pl.BlockSpec((tr, cols), lambda i: (layer * nblk + i, 0))
    args = [w, g, m, v] + ([] if prev is None else list(prev))
    return pl.pallas_call(
        body, name=name, grid=(nblk,),
        in_specs=[stacked, pl.BlockSpec((tr, cols), lambda i: (i, 0)), stacked, stacked] + [ANY_SPEC] * n_prev,
        out_specs=[stacked] * 4, out_shape=[jax.ShapeDtypeStruct(w.shape, F32)] * 4,
        input_output_aliases={4 + k: k for k in range(n_prev)},
        compiler_params=_cp(("parallel",)))(*args)


def _position():
    return jnp.stack([lax.axis_index("c"), 2 * lax.axis_index("x") + lax.axis_index("y")]).astype(jnp.int32)


SUM_ROW_TILES = 2


def _sum2_bf16(pos, fulls, sibs, *, name):
    n = len(fulls)
    nb = SUM_ROW_TILES

    def body(pos_ref, *refs):
        for t in range(n):
            refs[2 * n + t][...] = (refs[t][...] + refs[n + t][...]).astype(BF16)

    in_specs, sib_specs = [], []
    for sb in sibs:
        _, half, cols = sb.shape
        tr = half // nb
        assert half % nb == 0 and tr % 16 == 0, sb.shape
        in_specs.append(pl.BlockSpec((None, tr, cols), lambda j, i, p: (j, p[0] * nb + i, 0)))
        sib_specs.append(pl.BlockSpec((None, tr, cols), lambda j, i, p: (j, i, 0)))
    return pl.pallas_call(
        body, name=name,
        grid_spec=pltpu.PrefetchScalarGridSpec(num_scalar_prefetch=1, grid=(N_CHIPS, nb), in_specs=in_specs + sib_specs,
                                               out_specs=sib_specs),
        out_shape=[jax.ShapeDtypeStruct(sb.shape, BF16) for sb in sibs],
        compiler_params=_cp(("parallel", "parallel")))(pos, *fulls, *sibs)


def _sum5(pos, fulls, sibs, recvs, *, name):
    n = len(fulls)
    nb = SUM_ROW_TILES

    def body(pos_ref, *refs):
        for t in range(n):
            acc = refs[t][...] + refs[n + t][...]
            for kk in range(3):
                acc = acc + refs[2 * n + t][kk].astype(F32)
            refs[3 * n + t][...] = acc

    f_specs, s_specs, r_specs, o_specs = [], [], [], []
    for f in fulls:
        _, rows, cols = f.shape
        tr = rows // 2 // nb
        f_specs.append(pl.BlockSpec((None, tr, cols), lambda i, p: (p[1], p[0] * nb + i, 0)))
        s_specs.append(pl.BlockSpec((None, tr, cols), lambda i, p: (p[1], i, 0)))
        r_specs.append(pl.BlockSpec((3, tr, cols), lambda i, p: (0, i, 0)))
        o_specs.append(pl.BlockSpec((tr, cols), lambda i, p: (p[0] * nb + i, 0)))
    return pl.pallas_call(
        body, name=name,
        grid_spec=pltpu.PrefetchScalarGridSpec(num_scalar_prefetch=1, grid=(nb,), in_specs=f_specs + s_specs + r_specs,
                                               out_specs=o_specs),
        out_shape=[jax.ShapeDtypeStruct(f.shape[1:], F32) for f in fulls],
        compiler_params=_cp(("parallel",)))(pos, *fulls, *sibs, *recvs)


def _place():
    x, y, c = lax.axis_index("x"), lax.axis_index("y"), lax.axis_index("c")
    chips = [(1 - x, y), (x, 1 - y), (1 - x, 1 - y)]
    return x, y, c, 2 * x + y, chips


SEM_SPEC = pl.BlockSpec(memory_space=pltpu.SEMAPHORE)
ANY_SPEC = pl.BlockSpec(memory_space=pl.ANY)


def _gather_copies(ins, outs, send_i, recv_i, send_o, recv_o):
    x, y, c, me, chips = _place()
    n = len(ins)
    started, awaited = [], []
    for t in range(n):
        half = ins[t].shape[0] // 2
        mine = pl.ds(c * half, half)
        started.append(pltpu.make_async_remote_copy(
            src_ref=ins[t], dst_ref=outs[t].at[me], send_sem=send_o.at[t], recv_sem=recv_o.at[t],
            device_id=(x, y, 1 - c), device_id_type=MESH))
        awaited.append(started[-1])
        for kk, (px, py) in enumerate(chips):
            started.append(pltpu.make_async_remote_copy(
                src_ref=ins[t].at[mine], dst_ref=outs[t].at[me, mine], send_sem=send_i.at[t * 3 + kk],
                recv_sem=recv_i.at[t * 3 + kk], device_id=(px, py, c), device_id_type=MESH))
            awaited.append(pltpu.make_async_remote_copy(
                src_ref=ins[t].at[mine], dst_ref=outs[t].at[2 * px + py, mine], send_sem=send_i.at[t * 3 + kk],
                recv_sem=recv_i.at[t * 3 + kk], device_id=(px, py, c), device_id_type=MESH))
    return started, awaited


def _forward_copies(outs, send_d, recv_d):
    x, y, c, me, chips = _place()
    started, awaited = [], []
    for t in range(len(outs)):
        half = outs[t].shape[1] // 2
        for kk, (px, py) in enumerate(chips):
            for lst, hc in ((started, c), (awaited, 1 - c)):
                blk = outs[t].at[2 * px + py, pl.ds(hc * half, half)]
                lst.append(pltpu.make_async_remote_copy(src_ref=blk, dst_ref=blk, send_sem=send_d.at[t * 3 + kk],
                                                        recv_sem=recv_d.at[t * 3 + kk], device_id=(x, y, 1 - c), device_id_type=MESH))
    return started, awaited


def _gather_blocking(shards):
    n = len(shards)

    def body(*refs):
        ins, outs = refs[:n], refs[n:2 * n]
        send_i, recv_i, send_d, recv_d, send_o, recv_o = refs[2 * n:]
        started, awaited = _gather_copies(ins, outs, send_i, recv_i, send_o, recv_o)
        for cp in started:
            cp.start()
        for cp in awaited:
            cp.wait_recv()
        fwd, fwd_in = _forward_copies(outs, send_d, recv_d)
        for cp in fwd:
            cp.start()
        for cp in fwd_in:
            cp.wait_recv()
        for cp in started + fwd:
            cp.wait_send()

    return pl.pallas_call(
        body, name="gather_first", in_specs=[HBM_SPEC] * n, out_specs=[HBM_SPEC] * n,
        out_shape=[jax.ShapeDtypeStruct((N_CHIPS,) + s.shape, s.dtype) for s in shards],
        scratch_shapes=[pltpu.SemaphoreType.DMA((3 * n,)), pltpu.SemaphoreType.DMA((3 * n,)),
                        pltpu.SemaphoreType.DMA((3 * n,)), pltpu.SemaphoreType.DMA((3 * n,)),
                        pltpu.SemaphoreType.DMA((n,)), pltpu.SemaphoreType.DMA((n,))],
    )(*shards)


def _gather_start(shards, after, tag):
    n = len(shards)

    def body(*refs):
        ins = refs[:n]
        send_i, recv_i, send_o, recv_o = refs[2 * n + 1:2 * n + 5]
        outs = refs[3 * n + 5:4 * n + 5]
        token = refs[4 * n + 5]
        started, _ = _gather_copies(ins, outs, send_i, recv_i, send_o, recv_o)
        for cp in started:
            cp.start()
        token[...] = jnp.zeros_like(token)

    lands = [lax.empty((N_CHIPS,) + s.shape, s.dtype) for s in shards]
    sems = [pltpu.SemaphoreType.DMA((3 * n,)), pltpu.SemaphoreType.DMA((3 * n,)), pltpu.SemaphoreType.DMA((n,)), pltpu.SemaphoreType.DMA((n,))]
    res = pl.pallas_call(
        body, name=f"gather_{tag}_start",
        in_specs=[HBM_SPEC] * (2 * n) + [ANY_SPEC],
        out_specs=[SEM_SPEC] * 4 + [HBM_SPEC] * (2 * n) + [pl.BlockSpec(memory_space=pltpu.VMEM)],
        out_shape=sems + [jax.ShapeDtypeStruct(s.shape, s.dtype) for s in shards]
        + [jax.ShapeDtypeStruct(a.shape, a.dtype) for a in lands] + [jax.ShapeDtypeStruct((8, 128), F32)],
        input_output_aliases={t: 4 + t for t in range(2 * n)},
        compiler_params=pltpu.CompilerParams(has_side_effects=pltpu.SideEffectType.DATAFLOW_SIDE_EFFECTING),
    )(*[pltpu.with_memory_space_constraint(s, pltpu.HBM) for s in shards],
      *[pltpu.with_memory_space_constraint(a, pltpu.HBM) for a in lands], after)
    return res[:4], res[4:4 + n], res[4 + n:4 + 2 * n], res[-1]


def _gather_wait(sems, shards_thru, lands_thru, after, tag):
    n = len(shards_thru)

    def body(*refs):
        ins, outs_in = refs[:n], refs[n:2 * n]
        send_i, recv_i, send_o, recv_o = refs[2 * n:2 * n + 4]
        started, awaited = _gather_copies(ins, outs_in, send_i, recv_i, send_o, recv_o)
        for cp in started:
            cp.wait_send()
        for cp in awaited:
            cp.wait_recv()

    res = pl.pallas_call(
        body, name=f"gather_{tag}_wait",
        in_specs=[HBM_SPEC] * (2 * n) + [SEM_SPEC] * 4 + [ANY_SPEC],
        out_specs=[HBM_SPEC] * (2 * n),
        out_shape=[jax.ShapeDtypeStruct(a.shape, a.dtype) for a in list(shards_thru) + list(lands_thru)],
        input_output_aliases={t: t for t in range(2 * n)},
        compiler_params=pltpu.CompilerParams(has_side_effects=pltpu.SideEffectType.DATAFLOW_SIDE_EFFECTING),
    )(*shards_thru, *lands_thru, *sems, after)
    return res[n:]


def _gather_forward(lands, tag):
    n = len(lands)

    def body(*refs):
        outs = refs[n:2 * n]
        send_d, recv_d = refs[2 * n:]
        fwd, fwd_in = _forward_copies(outs, send_d, recv_d)
        for cp in fwd:
            cp.start()
        for cp in fwd_in:
            cp.wait_recv()
        for cp in fwd:
            cp.wait_send()

    return pl.pallas_call(
        body, name=f"gather_{tag}_forward", in_specs=[HBM_SPEC] * n, out_specs=[HBM_SPEC] * n,
        out_shape=[jax.ShapeDtypeStruct(a.shape, a.dtype) for a in lands],
        input_output_aliases={t: t for t in range(n)},
        scratch_shapes=[pltpu.SemaphoreType.DMA((3 * n,)), pltpu.SemaphoreType.DMA((3 * n,))],
    )(*lands)


def _stage1_copies(ins, sib, send, recv):
    x, y, c, me, chips = _place()
    cps = []
    for t in range(len(ins)):
        rows = ins[t].shape[1] // 2
        cps.append(pltpu.make_async_remote_copy(
            src_ref=ins[t].at[:, pl.ds((1 - c) * rows, rows), :], dst_ref=sib[t], send_sem=send.at[t],
            recv_sem=recv.at[t], device_id=(x, y, 1 - c), device_id_type=MESH))
    return cps


def _split_start(copies_fn, srcs, land_shapes, n_sems, tag):
    n = len(srcs)

    def body(*refs):
        send, recv = refs[2 * n:2 * n + 2]
        for cp in copies_fn(refs[:n], refs[3 * n + 2:4 * n + 2], send, recv):
            cp.start()
        refs[4 * n + 2][...] = jnp.zeros_like(refs[4 * n + 2])

    lands = [lax.empty(shp, dt) for shp, dt in land_shapes]
    res = pl.pallas_call(
        body, name=tag,
        in_specs=[HBM_SPEC] * (2 * n),
        out_specs=[SEM_SPEC] * 2 + [HBM_SPEC] * (2 * n) + [pl.BlockSpec(memory_space=pltpu.VMEM)],
        out_shape=[pltpu.SemaphoreType.DMA((n_sems,)), pltpu.SemaphoreType.DMA((n_sems,))]
        + [jax.ShapeDtypeStruct(p.shape, p.dtype) for p in srcs]
        + [jax.ShapeDtypeStruct(a.shape, a.dtype) for a in lands] + [jax.ShapeDtypeStruct((8, 128), F32)],
        input_output_aliases={t: 2 + t for t in range(2 * n)},
        compiler_params=pltpu.CompilerParams(has_side_effects=pltpu.SideEffectType.DATAFLOW_SIDE_EFFECTING),
    )(*[pltpu.with_memory_space_constraint(p, pltpu.HBM) for p in srcs],
      *[pltpu.with_memory_space_constraint(a, pltpu.HBM) for a in lands])
    return res[:2], res[2:2 + n], res[2 + n:2 + 2 * n], res[-1]


def _split_wait(copies_fn, sems, srcs_thru, lands_thru, after, tag):
    n = len(srcs_thru)

    def body(*refs):
        for cp in copies_fn(refs[:n], refs[n:2 * n], refs[2 * n], refs[2 * n + 1]):
            cp.wait()

    res = pl.pallas_call(
        body, name=tag,
        in_specs=[HBM_SPEC] * (2 * n) + [SEM_SPEC] * 2 + [ANY_SPEC],
        out_specs=[HBM_SPEC] * (2 * n),
        out_shape=[jax.ShapeDtypeStruct(a.shape, a.dtype) for a in list(srcs_thru) + list(lands_thru)],
        input_output_aliases={t: t for t in range(2 * n)},
        compiler_params=pltpu.CompilerParams(has_side_effects=pltpu.SideEffectType.DATAFLOW_SIDE_EFFECTING),
    )(*srcs_thru, *lands_thru, *sems, after)
    return res[:n], res[n:]


def _stage2_copies(ps, rcv, send, recv):
    x, y, c, me, chips = _place()
    return [pltpu.make_async_remote_copy(
        src_ref=ps[t].at[2 * px + py], dst_ref=rcv[t].at[kk], send_sem=send.at[t * 3 + kk],
        recv_sem=recv.at[t * 3 + kk], device_id=(px, py, c), device_id_type=MESH)
        for t in range(len(ps)) for kk, (px, py) in enumerate(chips)]


def _reduce_stage3(reduced, tag):
    n = len(reduced)

    def body(*refs):
        outs = refs[n:2 * n]
        send, recv = refs[2 * n:]
        x, y, c, me, chips = _place()
        cps = []
        for t in range(n):
            rows = outs[t].shape[0] // 2
            mine = outs[t].at[pl.ds(c * rows, rows), :]
            cp = pltpu.make_async_remote_copy(src_ref=mine, dst_ref=mine, send_sem=send.at[t], recv_sem=recv.at[t],
                                              device_id=(x, y, 1 - c), device_id_type=MESH)
            cp.start()
            cps.append(cp)
        for cp in cps:
            cp.wait()

    return pl.pallas_call(
        body, name="reduce_stage3_" + tag, in_specs=[HBM_SPEC] * n, out_specs=[HBM_SPEC] * n,
        out_shape=[jax.ShapeDtypeStruct(r.shape, r.dtype) for r in reduced],
        input_output_aliases={t: t for t in range(n)},
        scratch_shapes=[pltpu.SemaphoreType.DMA((n,)), pltpu.SemaphoreType.DMA((n,))],
    )(*reduced)


def _allreduce_small(v):
    rows, cols = v.shape

    def body(v_ref, o_ref, buf, send, recv, loc):
        x, y, c, me, chips = _place()
        mine = 4 * x + 2 * y + c
        lc = pltpu.make_async_copy(v_ref, buf.at[mine], loc)
        lc.start()
        peers = []
        for fx in range(2):
            for fy in range(2):
                for fc in range(2):
                    if fx or fy or fc:
                        peers.append((fx, fy, fc))
        cps = []
        for kk, (fx, fy, fc) in enumerate(peers):
            to = (x ^ fx, y ^ fy, c ^ fc)
            cp = pltpu.make_async_remote_copy(src_ref=v_ref, dst_ref=buf.at[mine], send_sem=send.at[kk], recv_sem=recv.at[kk],
                                              device_id=to, device_id_type=MESH)
            cp.start()
            cps.append((cp, to))
        for kk, (cp, to) in enumerate(cps):
            src = 4 * to[0] + 2 * to[1] + to[2]
            pltpu.make_async_remote_copy(src_ref=v_ref, dst_ref=buf.at[src], send_sem=send.at[kk], recv_sem=recv.at[kk],
                                         device_id=to, device_id_type=MESH).wait_recv()
        for cp, _ in cps:
            cp.wait_send()
        lc.wait()
        acc = buf[0]
        for d in range(1, 8):
            acc = acc + buf[d]
        o_ref[...] = acc

    return pl.pallas_call(
        body, name="allreduce_small", in_specs=[pl.BlockSpec(memory_space=pltpu.VMEM)],
        out_specs=pl.BlockSpec(memory_space=pltpu.VMEM), out_shape=jax.ShapeDtypeStruct((rows, cols), F32),
        scratch_shapes=[pltpu.VMEM((8, rows, cols), F32), pltpu.SemaphoreType.DMA((7,)), pltpu.SemaphoreType.DMA((7,)),
                        pltpu.SemaphoreType.DMA],
        compiler_params=pltpu.CompilerParams(vmem_limit_bytes=VMEM_LIMIT_V7X),
    )(v)


def _pad_w_in(w):
    z = lambda n: jnp.zeros(w.shape[:-1] + (n,), w.dtype)
    return jnp.concatenate([w[..., 0:384], z(64), w[..., 384:416], z(32), w[..., 416:1824],
                            w[..., 1824:1830], z(58), w[..., 400:416], w[..., 384:400], z(32)], axis=-1)


def _unpad_w_in(g):
    x1 = g[..., 448:464] + g[..., Z_F + 80:Z_F + 96]
    x2 = g[..., 464:480] + g[..., Z_F + 64:Z_F + 80]
    return jnp.concatenate([g[..., 0:384], x1, x2, g[..., 512:1920], g[..., 1920:1926]], axis=-1)


def _block_diag(pw):
    out = jnp.zeros((POOL_W, POOL_W), pw.dtype)
    for g in range(4):
        out = out.at[g * 64:(g + 1) * 64, g * 64:(g + 1) * 64].set(pw[g])
    return out


def _rope_tables(s):
    inv_freq = ROPE_THETA ** (-jnp.arange(0, ROPE, 2, dtype=F32) / ROPE)
    ang = jnp.arange(s, dtype=jnp.int32).astype(F32)[:, None] * inv_freq[None, :]
    cos, sin = jnp.cos(ang), jnp.sin(ang)
    zero = lambda n: jnp.zeros((s, n), F32)
    ck = jnp.concatenate([zero(NOPE), cos, cos, zero(DA - NOPE - ROPE)], axis=1)
    sk = jnp.concatenate([zero(NOPE), -sin, sin, zero(DA - NOPE - ROPE)], axis=1)
    cq = jnp.concatenate([jnp.ones((s, NOPE), F32), cos, cos, zero(DA - NOPE - ROPE)], axis=1) * SCALE_MLA
    return dict(cq=cq, sq=sk * SCALE_MLA, ck=ck, sk=sk)


def _mix_fwd(l, x1, wts, sm, tabs):
    z, h2 = _norm_mm(x1, 0, sm["mix_norm"][l], wts["w_in"][l], name=f"mix_in_{l}")
    qa, qn = _mla_q_prep(z, sm["q_a_norm"][l], wts["wq_a"][l], wts["wq_b"][l], tabs["cq"], tabs["sq"], name=f"mla_q_{l}")
    ka, va, kvn = _mla_kv_prep(z, sm["kv_a_norm"][l], wts["wk"][l], wts["wv"][l], tabs["ck"], tabs["sk"], name=f"mla_kv_{l}")
    oa, lse_a = _attn_fwd(qa, ka, va, VDIM, name=f"mla_attn_{l}")

    bd = _block_diag(wts["pool_w"][l]).astype(BF16)
    yb, pooled = _pool_fwd(z, Z_POOL // POOL_W, bd, sm["pool_scale"][l], name=f"pool_{l}")

    fb = jnp.pad(sm["fox_b_f"][l], (0, 8 - H)).reshape(8, 1)
    ft, c3t = _gate_fwd(z, Z_F // DA, fb, name=f"fox_gate_{l}")
    fqa, fka, fva = _fox_prep(z, c3t, name=f"fox_prep_{l}")
    oc, lse_c = _attn_fwd(fqa, fka, fva, FOX_D, name=f"fox_attn_{l}")

    x2, cat = _mix_out(oa, yb, oc, wts["w_out"][l], x1, name=f"mix_out_{l}")
    saved = dict(z=z, h2=h2, qn=qn, kvn=kvn, qa=qa, ka=ka, va=va, oa=oa, lse_a=lse_a, bd=bd, pooled=pooled,
                 fqa=fqa, fka=fka, fva=fva, ft=ft, fb=fb, oc=oc, lse_c=lse_c, cat=cat)
    return x2, saved


def _mix_bwd(l, x1, dx2, sv, wts, sm, tabs, tok=None):
    s = x1.shape[0]
    g = {}
    dx2b = (dx2 if tok is None else dx2 + tok).astype(BF16)
    g["w_out"] = _mm(sv["cat"], dx2b, "tn", name=f"d_w_out_{l}", tm=1024, tn=1024, tk=DW_TOKENS)
    doa, doc, dyb, dl_a, dl_c = _mix_out_bwd(dx2b, wts["w_out"][l], sv["oa"], sv["oc"], name=f"mix_out_bwd_{l}")

    dfqa, dfka, dfva, dcq, dck = _attn_bwd(sv["fqa"], sv["fka"], sv["fva"], doc, sv["lse_c"], dl_c, True, name=f"fox_attn_bwd_{l}")
    dfox = _fox_bwd_prep(dfqa, dfka, dfva, name=f"fox_bwd_prep_{l}")
    dc = jnp.pad(dcq.reshape(H, s) + dck.reshape(H, s), ((0, 8 - H), (0, 0)))
    dft, dfb = _gate_bwd(sv["ft"], sv["fb"], dc, name=f"fox_gate_bwd_{l}")
    g["fox_b_f"] = dfb[:H, 0]

    dq, dys, g["pool_scale"] = _pool_bwd_a(dyb, sv["pooled"], sv["bd"], sm["pool_scale"][l], name=f"pool_bwd_a_{l}")
    du = _pool_bwd_b(dq, name=f"pool_bwd_b_{l}")
    dbd = _mm(sv["pooled"], dys, "tn", name=f"d_pool_w_{l}")
    g["pool_w"] = jnp.stack([dbd[i * 64:(i + 1) * 64, i * 64:(i + 1) * 64] for i in range(4)])

    dqa_, dka_, dva_ = _attn_bwd(sv["qa"], sv["ka"], sv["va"], doa, sv["lse_a"], dl_a, False, name=f"mla_attn_bwd_{l}")
    dqab, dkv, dz3, dz15 = _mla_bwd_prep(dqa_, dka_, dva_, dft, tabs["cq"], tabs["sq"], tabs["ck"], tabs["sk"],
                                         name=f"mla_bwd_prep_{l}")
    wq_ab = jnp.concatenate([wts["wq_a"][l], wts["wq_b"][l]], axis=1)
    wkv = jnp.concatenate([wts["wk"][l], wts["wv"][l]], axis=1)
    dwq = _mm(sv["qn"], dqab, "tn", name=f"d_w_q_b_{l}", tn=768, tk=DW_TOKENS).reshape(Q_RANK, 2, H, DA)
    dwkv = _mm(sv["kvn"], dkv, "tn", name=f"d_w_kv_b_{l}", tn=768, tk=DW_TOKENS).reshape(KV_RANK, 2, H, DA)
    da, db = dwq[:, 0], dwq[:, 1]
    swapped = jnp.concatenate([jnp.zeros((Q_RANK, H, NOPE), F32), db[..., NOPE + HALF_ROPE:NOPE + ROPE],
                               db[..., NOPE:NOPE + HALF_ROPE]], axis=-1)
    g["w_q_b"] = (da[..., :NOPE + ROPE] + swapped).reshape(Q_RANK, H * (NOPE + ROPE))
    g["w_kv_b"] = jnp.concatenate([dwkv[:, 0, :, :NOPE], dwkv[:, 1, :, :VDIM]], axis=-1).reshape(KV_RANK, H * (NOPE + VDIM))
    dqn = _mm(dqab, wq_ab, "nt", name=f"d_qn_{l}", tk=2 * H * DA)
    dkvn = _mm(dkv, wkv, "nt", name=f"d_kvn_{l}", tk=2 * H * DA)
    dqa, g["q_a_norm"] = _rmsnorm_bwd(sv["z"], Z_QA // Q_RANK, sm["q_a_norm"][l], dqn, name=f"q_a_norm_bwd_{l}")
    dkva, g["kv_a_norm"] = _rmsnorm_bwd(sv["z"], Z_KVA // KV_RANK, sm["kv_a_norm"][l], dkvn, name=f"kv_a_norm_bwd_{l}")

    dz = jnp.concatenate([dqa.astype(BF16), dkva.astype(BF16), dz3, du.astype(BF16), dfox, dz15], axis=1)
    g["w_in"] = _mm(sv["h2"], dz, "tn", name=f"d_w_in_{l}", tm=1024, tn=1024, tk=DW_TOKENS)
    dh2 = _mm(dz, wts["w_in"][l], "nt", name=f"d_h2_{l}", tn=1024, tk=NZ)
    dx1, g["mix_norm"] = _rmsnorm_bwd(x1, 0, sm["mix_norm"][l], dh2, dx2, name=f"mix_norm_bwd_{l}")
    return dx1, g


DW_TOKENS = 2048


def _local_step(x, target, wts, sm, late_weights=None, grads_ready=None):
    s = x.shape[0]
    tabs = _rope_tables(s)
    acts = []
    xs = x
    for l in range(DEPTH):
        x1, gu1, act1 = _ffn_fwd(xs, sm["ffn1_norm"][l], wts["ffn1_w_gu"][l], wts["ffn1_w_d2"][l], name=f"ffn1_fwd_{l}")
        if l == 0 and late_weights is not None:
            sm = late_weights("ffn1", x1, sm)
        x2, sv = _mix_fwd(l, x1, wts, sm, tabs)
        if l == 0 and late_weights is not None:
            sm = late_weights("mix", x2, sm)
        x3, gu2, act2 = _ffn_fwd(x2, sm["ffn2_norm"][l], wts["ffn2_w_gu"][l], wts["ffn2_w_d2"][l], name=f"ffn2_fwd_{l}")
        acts.append((xs, gu1, act1, x1, sv, x2, gu2, act2))
        xs = x3
    dx, g_final, loss = _loss_head(xs, sm["final_norm"], target, name="loss_head")
    grads = [dict() for _ in range(DEPTH)]
    for l in reversed(range(DEPTH)):
        x0, gu1, act1, x1, sv, x2, gu2, act2 = acts[l]
        g = grads[l]
        dx, dgu, hh, dy, g["ffn2_norm"] = _ffn_bwd(x2, dx, gu2, sm["ffn2_norm"][l], wts["ffn2_w_gu"][l], wts["ffn2_w_d2"][l],
                                                   name=f"ffn2_bwd_{l}")
        g["ffn2_w_down"] = _mm(act2, dy, "tn", name=f"d_ffn2_w_down_{l}", tm=FF_SHARD, tn=1024, tk=DW_TOKENS)
        g["ffn2_w_gu"] = _mm(hh, dgu, "tn", name=f"d_ffn2_w_gu_{l}", tm=1024, tn=FF_SHARD, tk=DW_TOKENS, n_major_out=True)
        tok = None
        if grads_ready is not None:
            sm, tok = grads_ready(l, "ffn2", g, sm)
        dx, gm = _mix_bwd(l, x1, dx, sv, wts, sm, tabs, tok)
        g.update(gm)
        if grads_ready is not None:
            sm, _ = grads_ready(l, "mix", g, sm)
        dx, dgu, hh, dy, g["ffn1_norm"] = _ffn_bwd(x0, dx, gu1, sm["ffn1_norm"][l], wts["ffn1_w_gu"][l], wts["ffn1_w_d2"][l],
                                                   name=f"ffn1_bwd_{l}")
        if grads_ready is not None:
            sm, _ = grads_ready(l, "ffn1_tokens", {"dx": dx}, sm)
        g["ffn1_w_down"] = _mm(act1, dy, "tn", name=f"d_ffn1_w_down_{l}", tm=FF_SHARD, tn=1024, tk=DW_TOKENS)
        g["ffn1_w_gu"] = _mm(hh, dgu, "tn", name=f"d_ffn1_w_gu_{l}", tm=1024, tn=FF_SHARD, tk=DW_TOKENS, n_major_out=True)
        if grads_ready is not None:
            sm, _ = grads_ready(l, "ffn1", g, sm)
    return loss, dx, grads, g_final


BIG = ["ffn1_w_gu", "ffn1_w_down", "w_in", "w_q_b", "w_kv_b", "w_out", "ffn2_w_gu", "ffn2_w_down"]
SMALL = ["ffn1_norm", "mix_norm", "q_a_norm", "kv_a_norm", "pool_w", "pool_scale", "fox_b_f", "ffn2_norm"]
SMALL_ROWS = 48


WEIGHT_VIEWS = ["ffn1_w_gu", "ffn1_w_d2", "w_in", "wq_a", "wq_b", "wk", "wv", "w_out", "ffn2_w_gu", "ffn2_w_d2"]


def _prepare_weights(gathered, wts):
    for (nm, l), w in gathered.items():
        if nm in ("ffn1_w_gu", "ffn2_w_gu"):
            wts[nm][l] = w
        elif nm in ("ffn1_w_down", "ffn2_w_down"):
            wts[nm[:5] + "w_d2"][l] = w.reshape(2, FF_SHARD, D)
        elif nm in ("w_in", "w_out"):
            wts[nm][l] = w.reshape(D, -1)
        elif nm == "w_q_b":
            wq = jnp.moveaxis(w, 0, 1).reshape(Q_RANK, H, NOPE + ROPE)
            zq = lambda n: jnp.zeros((Q_RANK, H, n), BF16)
            wts["wq_a"][l] = jnp.concatenate([wq, zq(DA - NOPE - ROPE)], axis=-1).reshape(Q_RANK, H * DA)
            wts["wq_b"][l] = jnp.concatenate([zq(NOPE), wq[..., NOPE + HALF_ROPE:], wq[..., NOPE:NOPE + HALF_ROPE],
                                              zq(DA - NOPE - ROPE)], axis=-1).reshape(Q_RANK, H * DA)
        else:
            wkv = jnp.moveaxis(w, 0, 1).reshape(KV_RANK, H, NOPE + VDIM)
            zk = jnp.zeros((KV_RANK, H, DA - NOPE), BF16)
            wts["wk"][l] = jnp.concatenate([wkv[..., :NOPE], zk], axis=-1).reshape(KV_RANK, H * DA)
            wts["wv"][l] = jnp.concatenate([wkv[..., NOPE:], zk], axis=-1).reshape(KV_RANK, H * DA)


def _chip_major(name, g):
    if name in ("ffn1_w_gu", "ffn2_w_gu"):
        return g
    if name in ("ffn1_w_down", "ffn2_w_down", "w_in", "w_out"):
        return g.reshape(N_CHIPS, g.shape[0] // N_CHIPS, g.shape[1])
    return jnp.moveaxis(g.reshape(g.shape[0], N_CHIPS, g.shape[1] // N_CHIPS), 1, 0)


def _pack_small(grads, g_final, loss):
    parts = []
    for l in range(DEPTH):
        for nm in SMALL:
            parts.append(grads[l][nm].reshape(-1))
    parts.append(g_final.reshape(-1))
    parts.append(loss.reshape(1))
    flat = jnp.concatenate(parts)
    return jnp.pad(flat, (0, SMALL_ROWS * D - flat.shape[0])).reshape(SMALL_ROWS, D)


def _unpack_small(packed, params):
    flat = packed.reshape(-1)
    out = {nm: [] for nm in SMALL}
    off = 0
    for l in range(DEPTH):
        for nm in SMALL:
            shp = params[nm].shape[1:]
            n = int(np.prod(shp))
            out[nm].append(flat[off:off + n].reshape(shp))
            off += n
    res = {nm: jnp.stack(v) for nm, v in out.items()}
    res["final_norm"] = flat[off:off + D]
    return res, flat[off + D]


def _update(name, w, g, m, v):
    shp = w.shape
    if w.ndim == 1:
        view = (1, shp[0])
    elif w.size <= 65536:
        view = (shp[0], w.size // shp[0])
    else:
        view = (w.size // shp[-1], shp[-1])
    tr = view[0]
    for cand in (512, 352, 256, 128):
        if view[0] % cand == 0 and view[0] > cand:
            tr = cand
            break
    d, mn, vn = _adamw(w.reshape(view), g.reshape(view), m.reshape(view), v.reshape(view), name="adamw_" + name, tr=tr)
    return d.reshape(shp), mn.reshape(shp), vn.reshape(shp)


WEIGHTS = ['ffn1_norm', 'ffn1_w_gu', 'ffn1_w_down', 'mix_norm', 'w_in', 'q_a_norm', 'w_q_b', 'kv_a_norm', 'w_kv_b', 'pool_w',
           'pool_scale', 'fox_b_f', 'w_out', 'ffn2_norm', 'ffn2_w_gu', 'ffn2_w_down', 'final_norm']


def kernel(x, ffn1_norm, ffn1_w_gu, ffn1_w_down, mix_norm, w_in, q_a_norm, w_q_b, kv_a_norm, w_kv_b, pool_w, pool_scale, fox_b_f, w_out, ffn2_norm, ffn2_w_gu, ffn2_w_down, final_norm, loss_target, m_ffn1_norm, m_ffn1_w_gu, m_ffn1_w_down, m_mix_norm, m_w_in, m_q_a_norm, m_w_q_b, m_kv_a_norm, m_w_kv_b, m_pool_w, m_pool_scale, m_fox_b_f, m_w_out, m_ffn2_norm, m_ffn2_w_gu, m_ffn2_w_down, m_final_norm, v_ffn1_norm, v_ffn1_w_gu, v_ffn1_w_down, v_mix_norm, v_w_in, v_q_a_norm, v_w_q_b, v_kv_a_norm, v_w_kv_b, v_pool_w, v_pool_scale, v_fox_b_f, v_w_out, v_ffn2_norm, v_ffn2_w_gu, v_ffn2_w_down, v_final_norm):
    params = dict(ffn1_norm=ffn1_norm, ffn1_w_gu=ffn1_w_gu, ffn1_w_down=ffn1_w_down, mix_norm=mix_norm, w_in=w_in, q_a_norm=q_a_norm,
                  w_q_b=w_q_b, kv_a_norm=kv_a_norm, w_kv_b=w_kv_b, pool_w=pool_w, pool_scale=pool_scale, fox_b_f=fox_b_f, w_out=w_out,
                  ffn2_norm=ffn2_norm, ffn2_w_gu=ffn2_w_gu, ffn2_w_down=ffn2_w_down, final_norm=final_norm)
    mom = dict(ffn1_norm=m_ffn1_norm, ffn1_w_gu=m_ffn1_w_gu, ffn1_w_down=m_ffn1_w_down, mix_norm=m_mix_norm, w_in=m_w_in,
               q_a_norm=m_q_a_norm, w_q_b=m_w_q_b, kv_a_norm=m_kv_a_norm, w_kv_b=m_w_kv_b, pool_w=m_pool_w, pool_scale=m_pool_scale,
               fox_b_f=m_fox_b_f, w_out=m_w_out, ffn2_norm=m_ffn2_norm, ffn2_w_gu=m_ffn2_w_gu, ffn2_w_down=m_ffn2_w_down,
               final_norm=m_final_norm)
    var = dict(ffn1_norm=v_ffn1_norm, ffn1_w_gu=v_ffn1_w_gu, ffn1_w_down=v_ffn1_w_down, mix_norm=v_mix_norm, w_in=v_w_in,
               q_a_norm=v_q_a_norm, w_q_b=v_w_q_b, kv_a_norm=v_kv_a_norm, w_kv_b=v_w_kv_b, pool_w=v_pool_w, pool_scale=v_pool_scale,
               fox_b_f=v_fox_b_f, w_out=v_w_out, ffn2_norm=v_ffn2_norm, ffn2_w_gu=v_ffn2_w_gu, ffn2_w_down=v_ffn2_w_down,
               final_norm=v_final_norm)

    first = [("ffn1_w_gu", 0), ("ffn1_w_down", 0)]
    mix0 = [(nm, 0) for nm in ("w_in", "w_q_b", "w_kv_b", "w_out")]
    rest = [(nm, l) for nm in BIG for l in range(DEPTH) if (nm, l) not in first + mix0]

    def shards(keys, zero=0.0):
        return [((_pad_w_in(params[nm]) if nm == "w_in" else params[nm])[l] + zero).astype(BF16) for nm, l in keys]

    wts = {nm: [None] * DEPTH for nm in WEIGHT_VIEWS}
    wts["pool_w"] = params["pool_w"]
    got = _gather_blocking(shards(first))
    _prepare_weights(dict(zip(first, got)), wts)
    sems_m, src_m, land_m, token_m = _gather_start(shards(mix0), got[0], "mix0")
    sm = dict(params)
    sm["ffn1_norm"] = params["ffn1_norm"] + token_m[0, 0]
    rest_shards = shards(rest, token_m[0, 0])
    flying = {}

    def late_weights(stage, act, sm_now):
        if stage == "ffn1":
            lands = _gather_forward(_gather_wait(sems_m, src_m, land_m, act, "mix0"), "mix0")
            _prepare_weights(dict(zip(mix0, lands)), wts)
            flying["rest"] = _gather_start(rest_shards, lands[0], "rest")
            sm_next = dict(sm_now)
            sm_next["mix_norm"] = sm_now["mix_norm"] + flying["rest"][3][0, 0]
            return sm_next
        sems_r, src_r, land_r, _ = flying["rest"]
        lands = _gather_forward(_gather_wait(sems_r, src_r, land_r, act, "rest"), "rest")
        _prepare_weights(dict(zip(rest, lands)), wts)
        return sm_now

    pos = _position()
    flight = {}

    groups = {"l1": (1, BIG), "l0a": (0, [nm for nm in BIG if not nm.startswith("ffn1")]),
              "l0b": (0, [nm for nm in BIG if nm.startswith("ffn1")])}
    pending = {}

    def to_chips(key, full, sib):
        psum = _sum2_bf16(pos, full, sib, name=f"chip_sum_{key}")
        s2 = _split_start(_stage2_copies, psum, [((3,) + p.shape[1:], p.dtype) for p in psum], 3 * len(psum),
                          f"reduce_stage2_start_{key}")
        flight[key] = (full, sib, s2)
        return s2[3][0, 0]

    def grads_ready(l, stage, g, sm_now):
        behind, tok = None, None
        if (l, stage) == (1, "ffn1"):
            full = [_chip_major(nm, g[nm]) for nm in BIG]
            pending["l1"] = _split_start(_stage1_copies, full, [((N_CHIPS, f.shape[1] // 2, f.shape[2]), F32) for f in full],
                                         len(full), "reduce_stage1_start_l1")
            behind, tok = "ffn2_norm", pending["l1"][3][0, 0]
        elif (l, stage) == (0, "ffn2"):
            sems1, full_thru, sib_land, _ = pending["l1"]
            full, sib = _split_wait(_stage1_copies, sems1, full_thru, sib_land, g["ffn2_w_down"], "reduce_stage1_wait_l1")
            tok = to_chips("l1", full, sib)
        elif (l, stage) == (0, "mix"):
            full = [_chip_major(nm, g[nm]) for nm in groups["l0a"][1]]
            pending["l0a"] = _split_start(_stage1_copies, full, [((N_CHIPS, f.shape[1] // 2, f.shape[2]), F32) for f in full],
                                          len(full), "reduce_stage1_start_l0a")
            behind, tok = "ffn1_norm", pending["l0a"][3][0, 0]
        elif (l, stage) == (0, "ffn1_tokens"):
            sems1, full_thru, sib_land, _ = pending["l0a"]
            full, sib = _split_wait(_stage1_copies, sems1, full_thru, sib_land, g["dx"], "reduce_stage1_wait_l0a")
            to_chips("l0a", full, sib)
        elif (l, stage) == (0, "ffn1"):
            full = [_chip_major(nm, g[nm]) for nm in groups["l0b"][1]]
            pending["l0b"] = _split_start(_stage1_copies, full, [((N_CHIPS, f.shape[1] // 2, f.shape[2]), F32) for f in full],
                                          len(full), "reduce_stage1_start_l0b")
        if behind is None:
            return sm_now, tok
        sm_next = dict(sm_now)
        sm_next[behind] = sm_now[behind] + tok
        return sm_next, tok

    loss, dx, grads, g_final = _local_step(x[0], loss_target[0], wts, sm, late_weights, grads_ready)

    def view2d(a):
        return a.reshape(a.size // a.shape[-1], a.shape[-1])

    after = pending["l0b"][3]
    done = {nm: None for nm in BIG}
    for key in ("l1", "l0a", "l0b"):
        l, names = groups[key]
        full, sib, (sems2, ps_thru, lands2, _) = flight[key]
        _, recv = _split_wait(_stage2_copies, sems2, ps_thru, lands2, after, f"reduce_stage2_wait_{key}")
        whole = _reduce_stage3(_sum5(pos, full, sib, recv, name=f"grad_sum_{key}"), key)
        for nm, g_l in zip(names, whole):
            if nm == "w_in":
                g_l = _unpad_w_in(g_l)
            tr = max(t for t in (512, 352, 256, 128) if g_l.shape[0] % t == 0)
            done[nm] = _adamw_layer(view2d(params[nm]), g_l, view2d(mom[nm]), view2d(var[nm]), l, done[nm],
                                    name=f"adamw_{nm}_{l}", tr=tr)
        after = done[names[-1]][0][-8:, 0:128]
        if key == "l1":
            small_g, loss = _unpack_small(_allreduce_small(_pack_small(grads, g_final, loss)), params)
            sems1, full_thru, sib_land, _ = pending["l0b"]
            full_b, sib_b = _split_wait(_stage1_copies, sems1, full_thru, sib_land, after + small_g["final_norm"][0],
                                        "reduce_stage1_wait_l0b")
            after = after + to_chips("l0b", full_b, sib_b)
    gw, delta, new_m, new_v = dict(small_g), {}, {}, {}
    for nm in BIG:
        delta[nm], new_m[nm], new_v[nm], gw[nm] = [a.reshape(params[nm].shape) for a in done[nm]]
    for nm in small_g:
        delta[nm], new_m[nm], new_v[nm] = _update(nm, params[nm], gw[nm], mom[nm], var[nm])
    return (loss, dx[None], *[gw[n] for n in WEIGHTS], *[delta[n] for n in WEIGHTS], *[new_m[n] for n in WEIGHTS],
            *[new_v[n] for n in WEIGHTS])
```

```python
import functools
import math

import jax
import jax.numpy as jnp
import numpy as np
from jax import lax
from jax.experimental import pallas as pl
from jax.experimental.pallas import tpu as pltpu

F32 = jnp.float32
BF16 = jnp.bfloat16
MESH = pl.DeviceIdType.MESH
HBM_SPEC = pl.BlockSpec(memory_space=pltpu.HBM)

D = 1024
DEPTH = 2
D_FF = 2816
FF_SHARD = 1408
N_CHIPS = 4
H = 6
NOPE, ROPE, VDIM = 64, 32, 64
HALF_ROPE = ROPE // 2
Q_RANK, KV_RANK = 256, 128
POOL_W = 256
FOX_D = 64
N_IN = 1830
NZ = 2048
ROPE_THETA = 10000.0
EPS = 1e-6
POOL_HALO = 16
Z_QA, Z_KVA, Z_KR, Z_POOL, Z_FOX, Z_F = 0, 256, 384, 512, 768, 1920

ADAM_LR, ADAM_B1, ADAM_B2, ADAM_EPS, ADAM_WD, ADAM_STEP = 0.001, 0.9, 0.999, 1e-08, 0.01, 10

VMEM_LIMIT_V7X = 56 * 1024 * 1024


def _cp(sem=None, vmem=VMEM_LIMIT_V7X):
    return pltpu.CompilerParams(dimension_semantics=sem, vmem_limit_bytes=vmem)


def _sigmoid(x):
    return 0.5 * jnp.tanh(0.5 * x) + 0.5


def _dot(a, b, dims):
    return lax.dot_general(a, b, (dims, ((), ())), preferred_element_type=F32)


NN = ((1,), (0,))
NT = ((1,), (1,))
TN = ((0,), (0,))


def _mm(a, b, mode, *, name, out_dtype=F32, add=None, alpha=None, tm=512, tn=512, tk=512, n_major_out=False):
    if mode == "nn":
        (m, k), (k2, n) = a.shape, b.shape
    elif mode == "nt":
        (m, k), (n, k2) = a.shape, b.shape
    else:
        (k, m), (k2, n) = a.shape, b.shape
    assert k == k2
    tm, tn, tk = min(tm, m), min(tn, n), min(tk, k)
    assert m % tm == 0 and n % tn == 0 and k % tk == 0, (name, m, n, k, tm, tn, tk)
    nk = k // tk
    dims = {"nn": NN, "nt": NT, "tn": TN}[mode]
    a_spec = pl.BlockSpec((tk, tm), lambda i, j, kk: (kk, i)) if mode == "tn" else pl.BlockSpec((tm, tk), lambda i, j, kk: (i, kk))
    b_spec = pl.BlockSpec((tn, tk), lambda i, j, kk: (j, kk)) if mode == "nt" else pl.BlockSpec((tk, tn), lambda i, j, kk: (kk, j))
    in_specs = [a_spec, b_spec]
    args = [a, b]
    if add is not None:
        in_specs.append(pl.BlockSpec((tm, tn), lambda i, j, kk: (i, j)))
        args.append(add)
    if n_major_out:
        out_shape = jax.ShapeDtypeStruct((n // tn, m, tn), out_dtype)
        out_spec = pl.BlockSpec((None, tm, tn), lambda i, j, kk: (j, i, 0))
    else:
        out_shape = jax.ShapeDtypeStruct((m, n), out_dtype)
        out_spec = pl.BlockSpec((tm, tn), lambda i, j, kk: (i, j))

    def body(*refs):
        a_ref, b_ref = refs[0], refs[1]
        add_ref = refs[2] if add is not None else None
        o_ref, acc = refs[-2], refs[-1]
        kk = pl.program_id(2)

        @pl.when(kk == 0)
        def _():
            acc[...] = jnp.zeros_like(acc)

        acc[...] += _dot(a_ref[...].astype(BF16), b_ref[...].astype(BF16), dims)

        @pl.when(kk == nk - 1)
        def _():
            r = acc[...]
            if alpha is not None:
                r = r * alpha
            if add_ref is not None:
                r = r + add_ref[...].astype(F32)
            o_ref[...] = r.astype(out_dtype)

    return pl.pallas_call(
        body, name=name, grid=(m // tm, n // tn, nk), in_specs=in_specs, out_specs=out_spec, out_shape=out_shape,
        scratch_shapes=[pltpu.VMEM((tm, tn), F32)],
        compiler_params=_cp(("parallel", "parallel", "arbitrary")),
    )(*args)


def _norm_mm(x, col_block, gain, w, *, name, tm=512):
    s = x.shape[0]
    k, n = w.shape
    tm = min(tm, s)

    def body(x_ref, g_ref, w_ref, z_ref, h_ref):
        xv = x_ref[...]
        r = lax.rsqrt(jnp.mean(xv * xv, axis=-1, keepdims=True) + EPS)
        hv = (xv * r * g_ref[...]).astype(BF16)
        h_ref[...] = hv
        z_ref[...] = _dot(hv, w_ref[...], NN)

    return pl.pallas_call(
        body, name=name, grid=(s // tm,),
        in_specs=[pl.BlockSpec((tm, k), lambda i: (i, col_block)), pl.BlockSpec((1, k), lambda i: (0, 0)),
                  pl.BlockSpec((k, n), lambda i: (0, 0))],
        out_specs=[pl.BlockSpec((tm, n), lambda i: (i, 0)), pl.BlockSpec((tm, k), lambda i: (i, 0))],
        out_shape=[jax.ShapeDtypeStruct((s, n), F32), jax.ShapeDtypeStruct((s, k), BF16)],
        compiler_params=_cp(("parallel",)),
    )(x, gain.reshape(1, k), w)


def _rmsnorm_bwd(x, col_block, gain, dh, dres=None, *, name, tm=512):
    s = x.shape[0]
    k = gain.shape[-1]
    tm = min(tm, s)

    def body(*refs):
        x_ref, g_ref, dh_ref = refs[0], refs[1], refs[2]
        dres_ref = refs[3] if dres is not None else None
        dx_ref, dg_ref = refs[-2], refs[-1]
        xv = x_ref[...]
        r = lax.rsqrt(jnp.mean(xv * xv, axis=-1, keepdims=True) + EPS)
        dhv = dh_ref[...].astype(F32)
        a = dhv * g_ref[...]
        dx = r * a - xv * (r * r * r) * jnp.mean(a * xv, axis=-1, keepdims=True)
        if dres_ref is not None:
            dx = dx + dres_ref[...]
        dx_ref[...] = dx

        @pl.when(pl.program_id(0) == 0)
        def _():
            dg_ref[...] = jnp.zeros_like(dg_ref)

        dg_ref[...] += jnp.sum(dhv * xv * r, axis=0, keepdims=True)

    in_specs = [pl.BlockSpec((tm, k), lambda i: (i, col_block)), pl.BlockSpec((1, k), lambda i: (0, 0)),
                pl.BlockSpec((tm, k), lambda i: (i, 0))]
    args = [x, gain.reshape(1, k), dh]
    if dres is not None:
        in_specs.append(pl.BlockSpec((tm, k), lambda i: (i, 0)))
        args.append(dres)
    dx, dg = pl.pallas_call(
        body, name=name, grid=(s // tm,), in_specs=in_specs,
        out_specs=[pl.BlockSpec((tm, k), lambda i: (i, 0)), pl.BlockSpec((1, k), lambda i: (0, 0))],
        out_shape=[jax.ShapeDtypeStruct((s, k), F32), jax.ShapeDtypeStruct((1, k), F32)],
        compiler_params=_cp(("arbitrary",)),
    )(*args)
    return dx, dg.reshape(k)


def _ffn_fwd(x, gain, w_gu4, w_d2, *, name, tm=256):
    s = x.shape[0]
    tm = min(tm, s)

    def body(x_ref, g_ref, wgu_ref, wd_ref, xo_ref, dgu_ref, act_ref):
        xv = x_ref[...]
        r = lax.rsqrt(jnp.mean(xv * xv, axis=-1, keepdims=True) + EPS)
        hv = (xv * r * g_ref[...]).astype(BF16)
        y = jnp.zeros((tm, D), F32)
        for j in range(2):
            g = _dot(hv, wgu_ref[j], NN)
            u = _dot(hv, wgu_ref[2 + j], NN)
            sg = _sigmoid(g)
            silu = g * sg
            dgu_ref[:, j * FF_SHARD:(j + 1) * FF_SHARD] = (u * (sg * (1.0 + g * (1.0 - sg)))).astype(BF16)
            dgu_ref[:, D_FF + j * FF_SHARD:D_FF + (j + 1) * FF_SHARD] = silu.astype(BF16)
            act = (silu * u).astype(BF16)
            act_ref[:, j * FF_SHARD:(j + 1) * FF_SHARD] = act
            y = y + _dot(act, wd_ref[j], NN)
        xo_ref[...] = xv + 0.5 * y

    row = lambda i: (i, 0)
    return pl.pallas_call(
        body, name=name, grid=(s // tm,),
        in_specs=[pl.BlockSpec((tm, D), row), pl.BlockSpec((1, D), lambda i: (0, 0)),
                  pl.BlockSpec((N_CHIPS, D, FF_SHARD), lambda i: (0, 0, 0), pipeline_mode=pl.Buffered(1)),
                  pl.BlockSpec((2, FF_SHARD, D), lambda i: (0, 0, 0), pipeline_mode=pl.Buffered(1))],
        out_specs=[pl.BlockSpec((tm, D), row), pl.BlockSpec((tm, 2 * D_FF), row), pl.BlockSpec((tm, D_FF), row)],
        out_shape=[jax.ShapeDtypeStruct((s, D), F32), jax.ShapeDtypeStruct((s, 2 * D_FF), BF16),
                   jax.ShapeDtypeStruct((s, D_FF), BF16)],
        compiler_params=_cp(("parallel",)),
    )(x, gain.reshape(1, D), w_gu4, w_d2)


FFN_ROW_CHUNK = 32


def _ffn_bwd(x, dxo, dloc, gain, w_gu4, w_d2, *, name, tm=256):
    s = x.shape[0]
    tm = min(tm, s)

    def body(x_ref, dxo_ref, dloc_ref, g_ref, wgu_ref, wd_ref, dx_ref, dgu_ref, h_ref, dy_ref, dg_ref):
        xv = x_ref[...]
        r = lax.rsqrt(jnp.mean(xv * xv, axis=-1, keepdims=True) + EPS)
        xh = xv * r
        h_ref[...] = (xh * g_ref[...]).astype(BF16)
        dxov = dxo_ref[...]
        dy = (0.5 * dxov).astype(BF16)
        dy_ref[...] = dy
        gcols = [slice(j * FF_SHARD, (j + 1) * FF_SHARD) for j in range(2)]
        ucols = [slice(D_FF + j * FF_SHARD, D_FF + (j + 1) * FF_SHARD) for j in range(2)]
        dacts = [_dot(dy, wd_ref[j], NT) for j in range(2)]
        for r0 in range(0, tm, FFN_ROW_CHUNK):
            rows = slice(r0, r0 + FFN_ROW_CHUNK)
            for j in range(2):
                da = dacts[j][rows]
                dgu_ref[rows, gcols[j]] = (da * dloc_ref[rows, gcols[j]].astype(F32)).astype(BF16)
                dgu_ref[rows, ucols[j]] = (da * dloc_ref[rows, ucols[j]].astype(F32)).astype(BF16)
        dh = jnp.zeros((tm, D), F32)
        for j in range(2):
            dh = dh + _dot(dgu_ref[:, gcols[j]], wgu_ref[j], NT) + _dot(dgu_ref[:, ucols[j]], wgu_ref[2 + j], NT)
        a = dh * g_ref[...]
        dx_ref[...] = dxov + r * a - xh * (r * jnp.mean(a * xh, axis=-1, keepdims=True))

        @pl.when(pl.program_id(0) == 0)
        def _():
            dg_ref[...] = jnp.zeros_like(dg_ref)

        dg_ref[...] += jnp.sum(dh * xh, axis=0, keepdims=True)

    row = lambda i: (i, 0)
    outs = pl.pallas_call(
        body, name=name, grid=(s // tm,),
        in_specs=[pl.BlockSpec((tm, D), row), pl.BlockSpec((tm, D), row), pl.BlockSpec((tm, 2 * D_FF), row),
                  pl.BlockSpec((1, D), lambda i: (0, 0)),
                  pl.BlockSpec((N_CHIPS, D, FF_SHARD), lambda i: (0, 0, 0), pipeline_mode=pl.Buffered(1)),
                  pl.BlockSpec((2, FF_SHARD, D), lambda i: (0, 0, 0), pipeline_mode=pl.Buffered(1))],
        out_specs=[pl.BlockSpec((tm, D), row), pl.BlockSpec((tm, 2 * D_FF), row),
                   pl.BlockSpec((tm, D), row), pl.BlockSpec((tm, D), row), pl.BlockSpec((1, D), lambda i: (0, 0))],
        out_shape=[jax.ShapeDtypeStruct((s, D), F32), jax.ShapeDtypeStruct((s, 2 * D_FF), BF16),
                   jax.ShapeDtypeStruct((s, D), BF16), jax.ShapeDtypeStruct((s, D), BF16), jax.ShapeDtypeStruct((1, D), F32)],
        compiler_params=_cp(("arbitrary",)),
    )(x, dxo, dloc, gain.reshape(1, D), w_gu4, w_d2)
    dx, dgu, h, dy, dg = outs
    return dx, dgu, h, dy, dg.reshape(D)


DA = 128
SCALE_MLA = 1.0 / math.sqrt(NOPE + ROPE)
SCALE_FOX = 1.0 / math.sqrt(FOX_D)


def _causal_blocks(nb, key_major):
    if key_major:
        pairs = [(i, j) for j in range(nb) for i in range(j, nb)]
    else:
        pairs = [(i, j) for i in range(nb) for j in range(i + 1)]
    return (jnp.asarray(np.array([p[0] for p in pairs], np.int32)), jnp.asarray(np.array([p[1] for p in pairs], np.int32)))


HEADS_PER_STEP = 3
ROW_CHUNK = 64

def _col_to_row(col):
    return jnp.broadcast_to(col, (col.shape[0], DA)).T[0:1, :]


def _attn_fwd(qa, ka, va, dv, *, name, t=512):
    h, s, _ = qa.shape
    t = min(t, s)
    nb = s // t
    g = H
    qi, kj = _causal_blocks(nb, key_major=False)

    rc = min(ROW_CHUNK, t)

    def body(qi_ref, kj_ref, q_ref, k_ref, v_ref, o_ref, lse_ref, m_sc, acc_sc, p_sc, a_sc):
        n = pl.program_id(1)
        i, j = qi_ref[n], kj_ref[n]

        @pl.when(j == 0)
        def _():
            m_sc[...] = jnp.full_like(m_sc, -jnp.inf)
            acc_sc[...] = jnp.zeros_like(acc_sc)

        def step(masked):
            scs = [_dot(q_ref[hh], k_ref[hh], NT) for hh in range(g)]
            for r0 in range(0, t, rc):
                rows = slice(r0, r0 + rc)
                for hh in range(g):
                    sr = scs[hh][rows]
                    if masked:
                        row = lax.broadcasted_iota(jnp.int32, (rc, t), 0) + r0
                        col = lax.broadcasted_iota(jnp.int32, (rc, t), 1)
                        sr = jnp.where(col <= row, sr, -jnp.inf)
                    tiles = [sr[:, c0:c0 + DA] for c0 in range(0, t, DA)]
                    top = tiles[0]
                    for tile in tiles[1:]:
                        top = jnp.maximum(top, tile)
                    m_old = m_sc[hh, rows]
                    m_new = jnp.maximum(m_old, jnp.max(top, axis=-1, keepdims=True))
                    for c0, tile in zip(range(0, t, DA), tiles):
                        p_sc[hh, rows, c0:c0 + DA] = jnp.exp(tile - m_new).astype(BF16)
                    a_sc[hh, rows] = jnp.exp(m_old - m_new)
                    m_sc[hh, rows] = m_new
            for hh in range(g):
                acc_sc[hh] = a_sc[hh] * acc_sc[hh] + _dot(p_sc[hh], v_ref[hh], NN)

        @pl.when(j < i)
        def _():
            step(False)

        @pl.when(j == i)
        def _():
            step(True)
            for hh in range(g):
                acc = acc_sc[hh]
                l = acc[:, dv:dv + 1]
                o_ref[hh] = acc[:, :dv] / l
                lse_ref[hh] = _col_to_row(m_sc[hh][:, 0:1] + jnp.log(l))

    qmap = lambda hg, n, qi_r, kj_r: (hg, qi_r[n], 0)
    kmap = lambda hg, n, qi_r, kj_r: (hg, kj_r[n], 0)
    return pl.pallas_call(
        body, name=name,
        grid_spec=pltpu.PrefetchScalarGridSpec(
            num_scalar_prefetch=2, grid=(h // g, qi.shape[0]),
            in_specs=[pl.BlockSpec((g, t, DA), qmap), pl.BlockSpec((g, t, DA), kmap), pl.BlockSpec((g, t, DA), kmap)],
            out_specs=[pl.BlockSpec((g, t, dv), qmap), pl.BlockSpec((g, 1, t), lambda hg, n, qi_r, kj_r: (hg, 0, qi_r[n]))],
            scratch_shapes=[pltpu.VMEM((g, t, DA), F32), pltpu.VMEM((g, t, DA), F32), pltpu.VMEM((g, t, t), BF16),
                            pltpu.VMEM((g, t, DA), F32)]),
        out_shape=[jax.ShapeDtypeStruct((h, s, dv), F32), jax.ShapeDtypeStruct((h, 1, s), F32)],
        compiler_params=_cp(("parallel", "arbitrary")),
    )(qi, kj, qa, ka, va)


def _attn_bwd(qa, ka, va, doa, lse_row, delta_row, decay, *, name, t=512):
    h, s, _ = qa.shape
    t = min(t, s)
    nb = s // t
    g = HEADS_PER_STEP
    rc = min(ROW_CHUNK, t)
    qi, kj = _causal_blocks(nb, key_major=True)
    nsteps = qi.shape[0]

    def body(*refs):
        qi_ref, kj_ref, q_ref, k_ref, v_ref, do_ref, lse_ref, dl_ref = refs[:8]
        p_sc, ds_sc = refs[-2:]
        if decay:
            dq_ref, dk_ref, dv_ref, dcq_ref, dck_ref, dq_acc, dk_acc, dv_acc, dcq_acc, dck_acc = refs[8:-2]
        else:
            dq_ref, dk_ref, dv_ref, dq_acc, dk_acc, dv_acc = refs[8:-2]
        n = pl.program_id(1)
        i, j = qi_ref[n], kj_ref[n]

        @pl.when(n == 0)
        def _():
            dq_acc[...] = jnp.zeros_like(dq_acc)
            if decay:
                dcq_acc[...] = jnp.zeros_like(dcq_acc)

        @pl.when(i == j)
        def _():
            dk_acc[...] = jnp.zeros_like(dk_acc)
            dv_acc[...] = jnp.zeros_like(dv_acc)
            if decay:
                dck_acc[...] = jnp.zeros_like(dck_acc)

        def step(masked):
            sts = [_dot(k_ref[hh], q_ref[hh], NT) for hh in range(g)]
            dpts = [_dot(v_ref[hh], do_ref[hh], NT) for hh in range(g)]
            dcq = [jnp.zeros((1, t), F32) for _ in range(g)]
            for r0 in range(0, t, rc):
                rows = slice(r0, r0 + rc)
                for hh in range(g):
                    st = sts[hh][rows]
                    if masked:
                        row = lax.broadcasted_iota(jnp.int32, (rc, t), 0) + r0
                        col = lax.broadcasted_iota(jnp.int32, (rc, t), 1)
                        st = jnp.where(row <= col, st, -jnp.inf)
                    pt = jnp.exp(st - lse_ref[hh])
                    dst = pt * (dpts[hh][rows] - dl_ref[hh])
                    p_sc[hh, rows] = pt.astype(BF16)
                    ds_sc[hh, rows] = dst.astype(BF16)
                    if decay:
                        dcq[hh] = dcq[hh] + jnp.sum(dst, axis=0, keepdims=True)
                        dck_acc[hh, rows] -= jnp.sum(dst, axis=1, keepdims=True)
            for hh in range(g):
                dv_acc[hh] += _dot(p_sc[hh], do_ref[hh], NN)
                dk_acc[hh] += _dot(ds_sc[hh], q_ref[hh], NN)
                dq_acc[hh, i] += _dot(ds_sc[hh], k_ref[hh], TN)
                if decay:
                    dcq_acc[hh, i] += dcq[hh]

        @pl.when(i > j)
        def _():
            step(False)

        @pl.when(i == j)
        def _():
            step(True)

        @pl.when(i == nb - 1)
        def _():
            dk_ref[...] = dk_acc[...]
            dv_ref[...] = dv_acc[...]
            if decay:
                for hh in range(g):
                    dck_ref[hh] = _col_to_row(dck_acc[hh])

        @pl.when(n == nsteps - 1)
        def _():
            dq_ref[...] = dq_acc[...]
            if decay:
                dcq_ref[...] = dcq_acc[...]

    kmap = lambda hg, n, qi_r, kj_r: (hg, kj_r[n], 0)
    qmap = lambda hg, n, qi_r, kj_r: (hg, qi_r[n], 0)
    qrow = lambda hg, n, qi_r, kj_r: (hg, 0, qi_r[n])
    krow = lambda hg, n, qi_r, kj_r: (hg, 0, kj_r[n])
    whole = lambda hg, n, qi_r, kj_r: (hg, 0, 0, 0)
    in_specs = [pl.BlockSpec((g, t, DA), qmap), pl.BlockSpec((g, t, DA), kmap), pl.BlockSpec((g, t, DA), kmap),
                pl.BlockSpec((g, t, DA), qmap), pl.BlockSpec((g, 1, t), qrow), pl.BlockSpec((g, 1, t), qrow)]
    out_specs = [pl.BlockSpec((g, nb, t, DA), whole), pl.BlockSpec((g, t, DA), kmap), pl.BlockSpec((g, t, DA), kmap)]
    out_shape = [jax.ShapeDtypeStruct((h, nb, t, DA), F32), jax.ShapeDtypeStruct((h, s, DA), F32), jax.ShapeDtypeStruct((h, s, DA), F32)]
    scratch = [pltpu.VMEM((g, nb, t, DA), F32), pltpu.VMEM((g, t, DA), F32), pltpu.VMEM((g, t, DA), F32)]
    if decay:
        out_specs += [pl.BlockSpec((g, nb, 1, t), whole), pl.BlockSpec((g, 1, t), krow)]
        out_shape += [jax.ShapeDtypeStruct((h, nb, 1, t), F32), jax.ShapeDtypeStruct((h, 1, s), F32)]
        scratch += [pltpu.VMEM((g, nb, 1, t), F32), pltpu.VMEM((g, t, 1), F32)]
    scratch += [pltpu.VMEM((g, t, t), BF16), pltpu.VMEM((g, t, t), BF16)]
    outs = pl.pallas_call(
        body, name=name,
        grid_spec=pltpu.PrefetchScalarGridSpec(num_scalar_prefetch=2, grid=(h // g, nsteps), in_specs=in_specs, out_specs=out_specs,
                                               scratch_shapes=scratch),
        out_shape=out_shape, compiler_params=_cp(("parallel", "arbitrary")),
    )(qi, kj, qa, ka, va, doa, lse_row, delta_row)
    outs = list(outs)
    outs[0] = outs[0].reshape(h, s, DA)
    if decay:
        outs[3] = outs[3].reshape(h, 1, s)
    return outs


def _sel(rows, cols, pairs, value=1.0):
    m = np.zeros((rows, cols), np.float32)
    for r, c in pairs:
        m[r, c] = value
    return jnp.asarray(m, BF16)


def _lane_row(lanes):
    m = np.zeros((1, DA), np.float32)
    m[0, list(lanes)] = 1.0
    return jnp.asarray(m)


def _rms(xv, gain):
    r = lax.rsqrt(jnp.mean(xv * xv, axis=-1, keepdims=True) + EPS)
    return xv * r * gain


def _mla_q_prep(z, gain, wq_a, wq_b, cq, sq, *, name, tm=512):
    s = z.shape[0]
    tm = min(tm, s)

    def body(z_ref, g_ref, wa_ref, wb_ref, c_ref, s_ref, qa_ref, qn_ref):
        qn = _rms(z_ref[...], g_ref[...]).astype(BF16)
        qn_ref[...] = qn
        c, sn = c_ref[...], s_ref[...]
        for hh in range(H):
            cols = slice(hh * DA, (hh + 1) * DA)
            qa_ref[hh] = (_dot(qn, wa_ref[:, cols], NN) * c + _dot(qn, wb_ref[:, cols], NN) * sn).astype(BF16)

    row = lambda i: (i, 0)
    fixed = lambda i: (0, 0)
    return pl.pallas_call(
        body, name=name, grid=(s // tm,),
        in_specs=[pl.BlockSpec((tm, Q_RANK), lambda i: (i, Z_QA // Q_RANK)), pl.BlockSpec((1, Q_RANK), fixed),
                  pl.BlockSpec((Q_RANK, H * DA), fixed), pl.BlockSpec((Q_RANK, H * DA), fixed),
                  pl.BlockSpec((tm, DA), row), pl.BlockSpec((tm, DA), row)],
        out_specs=[pl.BlockSpec((H, tm, DA), lambda i: (0, i, 0)), pl.BlockSpec((tm, Q_RANK), row)],
        out_shape=[jax.ShapeDtypeStruct((H, s, DA), BF16), jax.ShapeDtypeStruct((s, Q_RANK), BF16)],
        compiler_params=_cp(("parallel",)),
    )(z, gain.reshape(1, Q_RANK), wq_a, wq_b, cq, sq)


def _mla_kv_prep(z, gain, wk, wv, ck, sk, *, name, tm=512):
    s = z.shape[0]
    tm = min(tm, s)
    one = _lane_row([VDIM])

    def body(zkv_ref, z3_ref, z15_ref, g_ref, wk_ref, wv_ref, c_ref, s_ref, one_ref, ka_ref, va_ref, kvn_ref):
        kvn = _rms(zkv_ref[...], g_ref[...]).astype(BF16)
        kvn_ref[...] = kvn
        kpe = z3_ref[...] * c_ref[...] + z15_ref[...] * s_ref[...]
        for hh in range(H):
            cols = slice(hh * DA, (hh + 1) * DA)
            ka_ref[hh] = (_dot(kvn, wk_ref[:, cols], NN) + kpe).astype(BF16)
            va_ref[hh] = (_dot(kvn, wv_ref[:, cols], NN) + one_ref[...]).astype(BF16)

    row = lambda i: (i, 0)
    fixed = lambda i: (0, 0)
    blk = lambda c: pl.BlockSpec((tm, DA), lambda i: (i, c))
    heads = pl.BlockSpec((H, tm, DA), lambda i: (0, i, 0))
    return pl.pallas_call(
        body, name=name, grid=(s // tm,),
        in_specs=[blk(Z_KVA // DA), blk(Z_KR // DA), blk(Z_F // DA), pl.BlockSpec((1, KV_RANK), fixed),
                  pl.BlockSpec((KV_RANK, H * DA), fixed), pl.BlockSpec((KV_RANK, H * DA), fixed),
                  pl.BlockSpec((tm, DA), row), pl.BlockSpec((tm, DA), row), pl.BlockSpec((1, DA), fixed)],
        out_specs=[heads, heads, pl.BlockSpec((tm, KV_RANK), row)],
        out_shape=[jax.ShapeDtypeStruct((H, s, DA), BF16), jax.ShapeDtypeStruct((H, s, DA), BF16),
                   jax.ShapeDtypeStruct((s, KV_RANK), BF16)],
        compiler_params=_cp(("parallel",)),
    )(z, z, z, gain.reshape(1, KV_RANK), wk, wv, ck, sk, one)


DEC_C = (FOX_D, FOX_D + 1, FOX_D + 2)
DEC_1 = (FOX_D + 3, FOX_D + 4, FOX_D + 5)


def _fox_prep(z, c3t, *, name, tm=512):
    s = z.shape[0]
    tm = min(tm, s)
    w = H * FOX_D
    left = [(r, r) for r in range(FOX_D)]
    right = [(FOX_D + r, r) for r in range(FOX_D)]
    pq = jnp.stack([_sel(DA, DA, left, SCALE_FOX), _sel(DA, DA, right, SCALE_FOX)])
    pk = jnp.stack([_sel(DA, DA, left), _sel(DA, DA, right)])
    pcq = jnp.stack([_sel(32, DA, [(hh + 8 * k, DEC_C[k]) for k in range(3)]) for hh in range(H)])
    pck = jnp.stack([_sel(32, DA, [(hh + 8 * k, DEC_1[k]) for k in range(3)], -1.0) for hh in range(H)])
    rows3 = jnp.concatenate([_lane_row(DEC_1), _lane_row(DEC_C), _lane_row([FOX_D])], axis=0)

    def body(zq_ref, zk_ref, zv_ref, c_ref, pq_ref, pk_ref, pcq_ref, pck_ref, r_ref, qa_ref, ka_ref, va_ref):
        c3 = c_ref[...]
        for pair in range(H // 2):
            lanes = slice(pair * DA, (pair + 1) * DA)
            zq, zk, zv = zq_ref[:, lanes].astype(BF16), zk_ref[:, lanes].astype(BF16), zv_ref[:, lanes].astype(BF16)
            for side in range(2):
                hh = 2 * pair + side
                qa_ref[hh] = (_dot(zq, pq_ref[side], NN) + _dot(c3, pcq_ref[hh], TN) + r_ref[0:1, :]).astype(BF16)
                ka_ref[hh] = (_dot(zk, pk_ref[side], NN) + _dot(c3, pck_ref[hh], TN) + r_ref[1:2, :]).astype(BF16)
                va_ref[hh] = (_dot(zv, pk_ref[side], NN) + r_ref[2:3, :]).astype(BF16)

    fixed2 = lambda i: (0, 0)
    fixed3 = lambda i: (0, 0, 0)
    heads = pl.BlockSpec((H, tm, DA), lambda i: (0, i, 0))
    zblk = lambda c: pl.BlockSpec((tm, w), lambda i: (i, c))
    return pl.pallas_call(
        body, name=name, grid=(s // tm,),
        in_specs=[zblk(Z_FOX // w), zblk(Z_FOX // w + 1), zblk(Z_FOX // w + 2), pl.BlockSpec((32, tm), lambda i: (0, i)),
                  pl.BlockSpec((2, DA, DA), fixed3), pl.BlockSpec((2, DA, DA), fixed3),
                  pl.BlockSpec((H, 32, DA), fixed3), pl.BlockSpec((H, 32, DA), fixed3), pl.BlockSpec((3, DA), fixed2)],
        out_specs=[heads, heads, heads], out_shape=[jax.ShapeDtypeStruct((H, s, DA), BF16)] * 3,
        compiler_params=_cp(("parallel",)),
    )(z, z, z, c3t, pq, pk, pcq, pck, rows3)


def _mix_out(oa, yb, oc, w_out, x1, *, name, tm=512):
    s = yb.shape[0]
    tm = min(tm, s)
    e2 = jnp.stack([_sel(VDIM, DA, [(r, r) for r in range(VDIM)]), _sel(VDIM, DA, [(r, VDIM + r) for r in range(VDIM)])])

    def body(oa_ref, yb_ref, oc_ref, e_ref, w_ref, x_ref, x2_ref, cat_ref):
        def pairs(o_ref):
            return [(_dot(o_ref[2 * p].astype(BF16), e_ref[0], NN) + _dot(o_ref[2 * p + 1].astype(BF16), e_ref[1], NN)).astype(BF16)
                    for p in range(H // 2)]

        cat = jnp.concatenate(pairs(oa_ref) + [yb_ref[...].astype(BF16)] + pairs(oc_ref), axis=1)
        cat_ref[...] = cat
        x2_ref[...] = x_ref[...] + _dot(cat, w_ref[...], NN)

    row = lambda i: (i, 0)
    heads = pl.BlockSpec((H, tm, VDIM), lambda i: (0, i, 0))
    return pl.pallas_call(
        body, name=name, grid=(s // tm,),
        in_specs=[heads, pl.BlockSpec((tm, POOL_W), row), heads, pl.BlockSpec((2, VDIM, DA), lambda i: (0, 0, 0)),
                  pl.BlockSpec((D, D), lambda i: (0, 0)), pl.BlockSpec((tm, D), row)],
        out_specs=[pl.BlockSpec((tm, D), row), pl.BlockSpec((tm, D), row)],
        out_shape=[jax.ShapeDtypeStruct((s, D), F32), jax.ShapeDtypeStruct((s, D), BF16)],
        compiler_params=_cp(("parallel",)),
    )(oa, yb, oc, e2, w_out, x1)


def _mix_out_bwd(dx2b, w_out, oa, oc, *, name, tm=512):
    s = dx2b.shape[0]
    tm = min(tm, s)
    f2 = jnp.stack([_sel(DA, DA, [(r, r) for r in range(VDIM)]), _sel(DA, DA, [(VDIM + r, r) for r in range(VDIM)])])
    nv = H * VDIM

    def body(dx_ref, w_ref, oa_ref, oc_ref, f_ref, doa_ref, doc_ref, dyb_ref, dla_ref, dlc_ref):
        dcat = _dot(dx_ref[...], w_ref[...], NT)
        dyb_ref[...] = dcat[:, nv:nv + POOL_W]
        for base, o_ref, do_ref, dl_ref in ((0, oa_ref, doa_ref, dla_ref), (nv + POOL_W, oc_ref, doc_ref, dlc_ref)):
            for p in range(H // 2):
                blk = dcat[:, base + p * DA:base + (p + 1) * DA].astype(BF16)
                for side in range(2):
                    hh = 2 * p + side
                    do = _dot(blk, f_ref[side], NN)
                    do_ref[hh] = do.astype(BF16)
                    dl_ref[hh] = _col_to_row(jnp.sum(do[:, :VDIM] * o_ref[hh], axis=-1, keepdims=True))

    row = lambda i: (i, 0)
    heads = lambda w: pl.BlockSpec((H, tm, w), lambda i: (0, i, 0))
    return pl.pallas_call(
        body, name=name, grid=(s // tm,),
        in_specs=[pl.BlockSpec((tm, D), row), pl.BlockSpec((D, D), lambda i: (0, 0)), heads(VDIM), heads(VDIM),
                  pl.BlockSpec((2, DA, DA), lambda i: (0, 0, 0))],
        out_specs=[heads(DA), heads(DA), pl.BlockSpec((tm, POOL_W), row),
                   pl.BlockSpec((H, 1, tm), lambda i: (0, 0, i)), pl.BlockSpec((H, 1, tm), lambda i: (0, 0, i))],
        out_shape=[jax.ShapeDtypeStruct((H, s, DA), BF16), jax.ShapeDtypeStruct((H, s, DA), BF16),
                   jax.ShapeDtypeStruct((s, POOL_W), F32), jax.ShapeDtypeStruct((H, 1, s), F32), jax.ShapeDtypeStruct((H, 1, s), F32)],
        compiler_params=_cp(("parallel",)),
    )(dx2b, w_out, oa, oc, f2)


def _mla_bwd_prep(dqa, dka, dva, dft, cq, sq, ck, sk, *, name, tm=512):
    s = dqa.shape[1]
    tm = min(tm, s)
    keep = _lane_row(range(NOPE))

    def body(dq_ref, dk_ref, dv_ref, dft_ref, cq_ref, sq_ref, ck_ref, sk_ref, keep_ref, dqab_ref, dkv_ref, dz3_ref, dz15_ref):
        cqv, sqv = cq_ref[...], sq_ref[...]
        dkpe = jnp.zeros((tm, DA), F32)
        for hh in range(H):
            lanes = slice(hh * DA, (hh + 1) * DA)
            dq = dq_ref[hh]
            dqab_ref[:, lanes] = (dq * cqv).astype(BF16)
            dqab_ref[:, H * DA + hh * DA:H * DA + (hh + 1) * DA] = (dq * sqv).astype(BF16)
            dk = dk_ref[hh]
            dkpe = dkpe + dk
            dkv_ref[:, lanes] = (dk * keep_ref[...]).astype(BF16)
            dkv_ref[:, H * DA + hh * DA:H * DA + (hh + 1) * DA] = (dv_ref[hh] * keep_ref[...]).astype(BF16)
        dz3_ref[...] = (dkpe * ck_ref[...]).astype(BF16)
        dz15_ref[...] = (dkpe * sk_ref[...] + dft_ref[...]).astype(BF16)

    row = lambda i: (i, 0)
    heads = pl.BlockSpec((H, tm, DA), lambda i: (0, i, 0))
    tab = pl.BlockSpec((tm, DA), row)
    return pl.pallas_call(
        body, name=name, grid=(s // tm,),
        in_specs=[heads, heads, heads, tab, tab, tab, tab, tab, pl.BlockSpec((1, DA), lambda i: (0, 0))],
        out_specs=[pl.BlockSpec((tm, 2 * H * DA), row), pl.BlockSpec((tm, 2 * H * DA), row), tab, tab],
        out_shape=[jax.ShapeDtypeStruct((s, 2 * H * DA), BF16), jax.ShapeDtypeStruct((s, 2 * H * DA), BF16),
                   jax.ShapeDtypeStruct((s, DA), BF16), jax.ShapeDtypeStruct((s, DA), BF16)],
        compiler_params=_cp(("parallel",)),
    )(dqa, dka, dva, dft, cq, sq, ck, sk, keep)


def _fox_bwd_prep(dfqa, dfka, dfva, *, name, tm=512):
    s = dfqa.shape[1]
    tm = min(tm, s)
    place = lambda v: jnp.stack([_sel(DA, DA, [(r, r) for r in range(FOX_D)], v), _sel(DA, DA, [(r, FOX_D + r) for r in range(FOX_D)], v)])
    gq, gk = place(SCALE_FOX), place(1.0)

    def body(dq_ref, dk_ref, dv_ref, gq_ref, gk_ref, dz_ref):
        for part, (d_ref, g_ref) in enumerate(((dq_ref, gq_ref), (dk_ref, gk_ref), (dv_ref, gk_ref))):
            for p in range(H // 2):
                blk = _dot(d_ref[2 * p].astype(BF16), g_ref[0], NN) + _dot(d_ref[2 * p + 1].astype(BF16), g_ref[1], NN)
                lo = part * H * FOX_D + p * DA
                dz_ref[:, lo:lo + DA] = blk.astype(BF16)

    heads = pl.BlockSpec((H, tm, DA), lambda i: (0, i, 0))
    sel = pl.BlockSpec((2, DA, DA), lambda i: (0, 0, 0))
    return pl.pallas_call(
        body, name=name, grid=(s // tm,), in_specs=[heads, heads, heads, sel, sel],
        out_specs=pl.BlockSpec((tm, 3 * H * FOX_D), lambda i: (i, 0)),
        out_shape=jax.ShapeDtypeStruct((s, 3 * H * FOX_D), BF16), compiler_params=_cp(("parallel",)),
    )(dfqa, dfka, dfva, gq, gk)


def _lane_scan(x, s, reverse):
    lane = lax.broadcasted_iota(jnp.int32, x.shape, 1)
    sh = 1
    while sh < s:
        if reverse:
            x = x + jnp.where(lane < s - sh, pltpu.roll(x, s - sh, axis=1), 0.0)
        else:
            x = x + jnp.where(lane >= sh, pltpu.roll(x, sh, axis=1), 0.0)
        sh *= 2
    return x


def _gate_fwd(z, col_block, bias, *, name):
    s = z.shape[0]

    def body(z_ref, b_ref, f_ref, c_ref):
        ft = z_ref[...].T[0:8, :]
        f_ref[...] = ft
        xg = ft + b_ref[...]
        lf = jnp.minimum(xg, 0.0) - jnp.log(1.0 + jnp.exp(-jnp.abs(xg)))
        c = _lane_scan(lf, s, False)
        hi = c.astype(BF16).astype(F32)
        r = c - hi
        mid = r.astype(BF16).astype(F32)
        lo = r - mid
        c_ref[...] = jnp.concatenate([hi, mid, lo, jnp.zeros_like(hi)], axis=0).astype(BF16)

    return pl.pallas_call(
        body, name=name, grid=(1,),
        in_specs=[pl.BlockSpec((s, 128), lambda i: (0, col_block)), pl.BlockSpec((8, 1), lambda i: (0, 0))],
        out_specs=[pl.BlockSpec((8, s), lambda i: (0, 0)), pl.BlockSpec((32, s), lambda i: (0, 0))],
        out_shape=[jax.ShapeDtypeStruct((8, s), F32), jax.ShapeDtypeStruct((32, s), BF16)],
        compiler_params=_cp(("arbitrary",)))(z, bias)


def _gate_bwd(ft, bias, dc, *, name):
    s = ft.shape[1]

    def body(f_ref, b_ref, dc_ref, df_ref, db_ref):
        xg = f_ref[...] + b_ref[...]
        dlf = _lane_scan(dc_ref[...], s, True)
        df = dlf * _sigmoid(-xg)
        db_ref[...] = jnp.sum(df, axis=-1, keepdims=True)
        df_ref[...] = jnp.concatenate([df, jnp.zeros((DA - 8, s), F32)], axis=0).T

    return pl.pallas_call(body, name=name, out_shape=[jax.ShapeDtypeStruct((s, DA), F32), jax.ShapeDtypeStruct((8, 1), F32)],
                          compiler_params=_cp())(ft, bias, dc)


def _pool_lane_consts(tm, i):
    lane = lax.broadcasted_iota(jnp.int32, (tm, POOL_W), 1)
    tok = lax.broadcasted_iota(jnp.int32, (tm, POOL_W), 0) + i * tm
    win = jnp.where(lane < 64, 2, jnp.where(lane < 128, 4, jnp.where(lane < 192, 8, 16)))
    cnt = jnp.minimum(tok + 1, win).astype(F32)
    return lane, tok, cnt


def _pick_window(lane, s2, s4, s8, s16):
    return jnp.where(lane < 64, s2, jnp.where(lane < 128, s4, jnp.where(lane < 192, s8, s16)))


def _pool_fwd(z, col_block, bd, scale, *, name, tm=512):
    s = z.shape[0]
    tm = min(tm, s)
    hb = tm // POOL_HALO

    def body(u_ref, halo_ref, bd_ref, sc_ref, y_ref, p_ref, buf):
        i = pl.program_id(0)
        buf[0:POOL_HALO, :] = halo_ref[...] * (i > 0).astype(F32)
        buf[POOL_HALO:, :] = u_ref[...]

        def back(k):
            return buf[POOL_HALO - k:POOL_HALO - k + tm, :]

        u = u_ref[...]
        s2 = u + back(1)
        s4 = s2 + back(2) + back(3)
        s8 = s4 + back(4) + back(5) + back(6) + back(7)
        s16 = s8
        for k in range(8, 16):
            s16 = s16 + back(k)
        lane, _, cnt = _pool_lane_consts(tm, i)
        pooled = (_pick_window(lane, s2, s4, s8, s16) / cnt - u).astype(BF16)
        p_ref[...] = pooled
        y_ref[...] = _dot(pooled, bd_ref[...], NN) * sc_ref[...]

    return pl.pallas_call(
        body, name=name, grid=(s // tm,),
        in_specs=[pl.BlockSpec((tm, POOL_W), lambda i: (i, col_block)),
                  pl.BlockSpec((POOL_HALO, POOL_W), lambda i: (jnp.maximum(i * hb - 1, 0), col_block)),
                  pl.BlockSpec((POOL_W, POOL_W), lambda i: (0, 0)), pl.BlockSpec((1, POOL_W), lambda i: (0, 0))],
        out_specs=[pl.BlockSpec((tm, POOL_W), lambda i: (i, 0)), pl.BlockSpec((tm, POOL_W), lambda i: (i, 0))],
        out_shape=[jax.ShapeDtypeStruct((s, POOL_W), F32), jax.ShapeDtypeStruct((s, POOL_W), BF16)],
        scratch_shapes=[pltpu.VMEM((tm + POOL_HALO, POOL_W), F32)],
        compiler_params=_cp(("parallel",)),
    )(z, z, bd, scale.reshape(1, POOL_W))


def _pool_bwd_a(dy, pooled, bd, scale, *, name, tm=512):
    s = dy.shape[0]
    tm = min(tm, s)

    def body(dy_ref, p_ref, bd_ref, sc_ref, dq_ref, dys_ref, dsc_ref):
        i = pl.program_id(0)
        dyv = dy_ref[...]
        y0 = _dot(p_ref[...], bd_ref[...], NN)
        dys = (dyv * sc_ref[...]).astype(BF16)
        dys_ref[...] = dys
        dp = _dot(dys, bd_ref[...], NT)
        _, _, cnt = _pool_lane_consts(tm, i)
        dq_ref[:, 0:POOL_W] = dp / cnt
        dq_ref[:, POOL_W:] = dp

        @pl.when(i == 0)
        def _():
            dsc_ref[...] = jnp.zeros_like(dsc_ref)

        dsc_ref[...] += jnp.sum(dyv * y0, axis=0, keepdims=True)

    row = lambda i: (i, 0)
    dq, dys, dsc = pl.pallas_call(
        body, name=name, grid=(s // tm,),
        in_specs=[pl.BlockSpec((tm, POOL_W), row), pl.BlockSpec((tm, POOL_W), row),
                  pl.BlockSpec((POOL_W, POOL_W), lambda i: (0, 0)), pl.BlockSpec((1, POOL_W), lambda i: (0, 0))],
        out_specs=[pl.BlockSpec((tm, 2 * POOL_W), row), pl.BlockSpec((tm, POOL_W), row), pl.BlockSpec((1, POOL_W), lambda i: (0, 0))],
        out_shape=[jax.ShapeDtypeStruct((s, 2 * POOL_W), F32), jax.ShapeDtypeStruct((s, POOL_W), BF16),
                   jax.ShapeDtypeStruct((1, POOL_W), F32)],
        compiler_params=_cp(("arbitrary",)),
    )(dy, pooled, bd, scale.reshape(1, POOL_W))
    return dq, dys, dsc.reshape(POOL_W)


def _pool_bwd_b(dq, *, name, tm=512):
    s = dq.shape[0]
    tm = min(tm, s)
    hb = tm // POOL_HALO
    nblk = s // tm

    def body(q_ref, dp_ref, halo_ref, du_ref, buf):
        i = pl.program_id(0)
        buf[0:tm, :] = q_ref[...]
        buf[tm:, :] = halo_ref[...] * (i < nblk - 1).astype(F32)

        def ahead(k):
            return buf[k:k + tm, :]

        q = q_ref[...]
        s2 = q + ahead(1)
        s4 = s2 + ahead(2) + ahead(3)
        s8 = s4 + ahead(4) + ahead(5) + ahead(6) + ahead(7)
        s16 = s8
        for k in range(8, 16):
            s16 = s16 + ahead(k)
        lane = lax.broadcasted_iota(jnp.int32, (tm, POOL_W), 1)
        du_ref[...] = _pick_window(lane, s2, s4, s8, s16) - dp_ref[...]

    return pl.pallas_call(
        body, name=name, grid=(nblk,),
        in_specs=[pl.BlockSpec((tm, POOL_W), lambda i: (i, 0)), pl.BlockSpec((tm, POOL_W), lambda i: (i, 1)),
                  pl.BlockSpec((POOL_HALO, POOL_W), lambda i: (jnp.minimum((i + 1) * hb, nblk * hb - 1), 0))],
        out_specs=pl.BlockSpec((tm, POOL_W), lambda i: (i, 0)),
        out_shape=jax.ShapeDtypeStruct((s, POOL_W), F32),
        scratch_shapes=[pltpu.VMEM((tm + POOL_HALO, POOL_W), F32)],
        compiler_params=_cp(("parallel",)),
    )(dq, dq, dq)


def _loss_head(x, gain, target, *, name, tm=512):
    s = x.shape[0]
    tm = min(tm, s)

    def body(x_ref, g_ref, t_ref, dx_ref, dg_ref, loss_ref):
        xv = x_ref[...]
        r = lax.rsqrt(jnp.mean(xv * xv, axis=-1, keepdims=True) + EPS)
        xh = xv * r
        err = xh * g_ref[...] - t_ref[...]
        dy = err * (1.0 / D)
        a = dy * g_ref[...]
        dx_ref[...] = r * a - xh * (r * jnp.mean(a * xh, axis=-1, keepdims=True))

        @pl.when(pl.program_id(0) == 0)
        def _():
            dg_ref[...] = jnp.zeros_like(dg_ref)
            loss_ref[...] = jnp.zeros_like(loss_ref)

        dg_ref[...] += jnp.sum(dy * xh, axis=0, keepdims=True)
        part = 0.5 * jnp.sum(jnp.mean(err * err, axis=-1, keepdims=True), axis=0, keepdims=True)
        loss_ref[...] += jnp.broadcast_to(part, loss_ref.shape)

    row = lambda i: (i, 0)
    dx, dg, loss = pl.pallas_call(
        body, name=name, grid=(s // tm,),
        in_specs=[pl.BlockSpec((tm, D), row), pl.BlockSpec((1, D), lambda i: (0, 0)), pl.BlockSpec((tm, D), row)],
        out_specs=[pl.BlockSpec((tm, D), row), pl.BlockSpec((1, D), lambda i: (0, 0)), pl.BlockSpec((1, 128), lambda i: (0, 0))],
        out_shape=[jax.ShapeDtypeStruct((s, D), F32), jax.ShapeDtypeStruct((1, D), F32), jax.ShapeDtypeStruct((1, 128), F32)],
        compiler_params=_cp(("arbitrary",)),
    )(x, gain.reshape(1, D), target)
    return dx, dg.reshape(D), loss[0, 0]


def _adamw(w, g, m, v, *, name, tr=512):
    rows, cols = w.shape
    tr = min(tr, rows)
    assert rows % tr == 0, (name, rows, tr)
    c_m = 1.0 - ADAM_B1
    c_v = 1.0 - ADAM_B2
    bc1 = 1.0 - ADAM_B1 ** ADAM_STEP
    bc2 = 1.0 - ADAM_B2 ** ADAM_STEP

    def body(w_ref, g_ref, m_ref, v_ref, d_ref, mo_ref, vo_ref):
        gv = g_ref[...]
        mn = ADAM_B1 * m_ref[...] + c_m * gv
        vn = ADAM_B2 * v_ref[...] + c_v * (gv * gv)
        mo_ref[...] = mn
        vo_ref[...] = vn
        d_ref[...] = -ADAM_LR * ((mn / bc1) / (jnp.sqrt(vn / bc2) + ADAM_EPS) + ADAM_WD * w_ref[...])

    spec = pl.BlockSpec((tr, cols), lambda i: (i, 0))
    return pl.pallas_call(body, name=name, grid=(rows // tr,), in_specs=[spec] * 4, out_specs=[spec] * 3,
                          out_shape=[jax.ShapeDtypeStruct((rows, cols), F32)] * 3,
                          compiler_params=_cp(("parallel",)))(w, g, m, v)


def _adamw_layer(w, g, m, v, layer, prev, *, name, tr):
    rows, cols = g.shape
    assert rows % tr == 0 and w.shape == (DEPTH * rows, cols), (name, w.shape, g.shape, tr)
    nblk = rows // tr
    c_m = 1.0 - ADAM_B1
    c_v = 1.0 - ADAM_B2
    bc1 = 1.0 - ADAM_B1 ** ADAM_STEP
    bc2 = 1.0 - ADAM_B2 ** ADAM_STEP
    n_prev = 0 if prev is None else 4

    def body(*refs):
        w_ref, g_ref, m_ref, v_ref = refs[:4]
        d_ref, mo_ref, vo_ref, go_ref = refs[4 + n_prev:]
        gv = g_ref[...]
        mn = ADAM_B1 * m_ref[...] + c_m * gv
        vn = ADAM_B2 * v_ref[...] + c_v * (gv * gv)
        mo_ref[...] = mn
        vo_ref[...] = vn
        go_ref[...] = gv
        d_ref[...] = -ADAM_LR * ((mn / bc1) / (jnp.sqrt(vn / bc2) + ADAM_EPS) + ADAM_WD * w_ref[...])

    stacked = pl.BlockSpec((tr, cols), lambda i: (layer * nblk + i, 0))
    args = [w, g, m, v] + ([] if prev is None else list(prev))
    return pl.pallas_call(
        body, name=name, grid=(nblk,),
        in_specs=[stacked, pl.BlockSpec((tr, cols), lambda i: (i, 0)), stacked, stacked] + [ANY_SPEC] * n_prev,
        out_specs=[stacked] * 4, out_shape=[jax.ShapeDtypeStruct(w.shape, F32)] * 4,
        input_output_aliases={4 + k: k for k in range(n_prev)},
        compiler_params=_cp(("parallel",)))(*args)


def _position():
    return jnp.stack([lax.axis_index("c"), 2 * lax.axis_index("x") + lax.axis_index("y")]).astype(jnp.int32)


SUM_ROW_TILES = 2


def _sum2_bf16(pos, fulls, sibs, *, name):
    n = len(fulls)
    nb = SUM_ROW_TILES

    def body(pos_ref, *refs):
        for t in range(n):
            refs[2 * n + t][...] = (refs[t][...] + refs[n + t][...]).astype(BF16)

    in_specs, sib_specs = [], []
    for sb in sibs:
        _, half, cols = sb.shape
        tr = half // nb
        assert half % nb == 0 and tr % 16 == 0, sb.shape
        in_specs.append(pl.BlockSpec((None, tr, cols), lambda j, i, p: (j, p[0] * nb + i, 0)))
        sib_specs.append(pl.BlockSpec((None, tr, cols), lambda j, i, p: (j, i, 0)))
    return pl.pallas_call(
        body, name=name,
        grid_spec=pltpu.PrefetchScalarGridSpec(num_scalar_prefetch=1, grid=(N_CHIPS, nb), in_specs=in_specs + sib_specs,
                                               out_specs=sib_specs),
        out_shape=[jax.ShapeDtypeStruct(sb.shape, BF16) for sb in sibs],
        compiler_params=_cp(("parallel", "parallel")))(pos, *fulls, *sibs)


def _sum5(pos, fulls, sibs, recvs, *, name):
    n = len(fulls)
    nb = SUM_ROW_TILES

    def body(pos_ref, *refs):
        for t in range(n):
            acc = refs[t][...] + refs[n + t][...]
            for kk in range(3):
                acc = acc + refs[2 * n + t][kk].astype(F32)
            refs[3 * n + t][...] = acc

    f_specs, s_specs, r_specs, o_specs = [], [], [], []
    for f in fulls:
        _, rows, cols = f.shape
        tr = rows // 2 // nb
        f_specs.append(pl.BlockSpec((None, tr, cols), lambda i, p: (p[1], p[0] * nb + i, 0)))
        s_specs.append(pl.BlockSpec((None, tr, cols), lambda i, p: (p[1], i, 0)))
        r_specs.append(pl.BlockSpec((3, tr, cols), lambda i, p: (0, i, 0)))
        o_specs.append(pl.BlockSpec((tr, cols), lambda i, p: (p[0] * nb + i, 0)))
    return pl.pallas_call(
        body, name=name,
        grid_spec=pltpu.PrefetchScalarGridSpec(num_scalar_prefetch=1, grid=(nb,), in_specs=f_specs + s_specs + r_specs,
                                               out_specs=o_specs),
        out_shape=[jax.ShapeDtypeStruct(f.shape[1:], F32) for f in fulls],
        compiler_params=_cp(("parallel",)))(pos, *fulls, *sibs, *recvs)


def _place():
    x, y, c = lax.axis_index("x"), lax.axis_index("y"), lax.axis_index("c")
    chips = [(1 - x, y), (x, 1 - y), (1 - x, 1 - y)]
    return x, y, c, 2 * x + y, chips


SEM_SPEC = pl.BlockSpec(memory_space=pltpu.SEMAPHORE)
ANY_SPEC = pl.BlockSpec(memory_space=pl.ANY)


def _gather_copies(ins, outs, send_i, recv_i, send_o, recv_o):
    x, y, c, me, chips = _place()
    n = len(ins)
    started, awaited = [], []
    for t in range(n):
        half = ins[t].shape[0] // 2
        mine = pl.ds(c * half, half)
        started.append(pltpu.make_async_remote_copy(
            src_ref=ins[t], dst_ref=outs[t].at[me], send_sem=send_o.at[t], recv_sem=recv_o.at[t],
            device_id=(x, y, 1 - c), device_id_type=MESH))
        awaited.append(started[-1])
        for kk, (px, py) in enumerate(chips):
            started.append(pltpu.make_async_remote_copy(
                src_ref=ins[t].at[mine], dst_ref=outs[t].at[me, mine], send_sem=send_i.at[t * 3 + kk],
                recv_sem=recv_i.at[t * 3 + kk], device_id=(px, py, c), device_id_type=MESH))
            awaited.append(pltpu.make_async_remote_copy(
                src_ref=ins[t].at[mine], dst_ref=outs[t].at[2 * px + py, mine], send_sem=send_i.at[t * 3 + kk],
                recv_sem=recv_i.at[t * 3 + kk], device_id=(px, py, c), device_id_type=MESH))
    return started, awaited


def _forward_copies(outs, send_d, recv_d):
    x, y, c, me, chips = _place()
    started, awaited = [], []
    for t in range(len(outs)):
        half = outs[t].shape[1] // 2
        for kk, (px, py) in enumerate(chips):
            for lst, hc in ((started, c), (awaited, 1 - c)):
                blk = outs[t].at[2 * px + py, pl.ds(hc * half, half)]
                lst.append(pltpu.make_async_remote_copy(src_ref=blk, dst_ref=blk, send_sem=send_d.at[t * 3 + kk],
                                                        recv_sem=recv_d.at[t * 3 + kk], device_id=(x, y, 1 - c), device_id_type=MESH))
    return started, awaited


def _gather_blocking(shards):
    n = len(shards)

    def body(*refs):
        ins, outs = refs[:n], refs[n:2 * n]
        send_i, recv_i, send_d, recv_d, send_o, recv_o = refs[2 * n:]
        started, awaited = _gather_copies(ins, outs, send_i, recv_i, send_o, recv_o)
        for cp in started:
            cp.start()
        for cp in awaited:
            cp.wait_recv()
        fwd, fwd_in = _forward_copies(outs, send_d, recv_d)
        for cp in fwd:
            cp.start()
        for cp in fwd_in:
            cp.wait_recv()
        for cp in started + fwd:
            cp.wait_send()

    return pl.pallas_call(
        body, name="gather_first", in_specs=[HBM_SPEC] * n, out_specs=[HBM_SPEC] * n,
        out_shape=[jax.ShapeDtypeStruct((N_CHIPS,) + s.shape, s.dtype) for s in shards],
        scratch_shapes=[pltpu.SemaphoreType.DMA((3 * n,)), pltpu.SemaphoreType.DMA((3 * n,)),
                        pltpu.SemaphoreType.DMA((3 * n,)), pltpu.SemaphoreType.DMA((3 * n,)),
                        pltpu.SemaphoreType.DMA((n,)), pltpu.SemaphoreType.DMA((n,))],
    )(*shards)


def _gather_start(shards, after, tag):
    n = len(shards)

    def body(*refs):
        ins = refs[:n]
        send_i, recv_i, send_o, recv_o = refs[2 * n + 1:2 * n + 5]
        outs = refs[3 * n + 5:4 * n + 5]
        token = refs[4 * n + 5]
        started, _ = _gather_copies(ins, outs, send_i, recv_i, send_o, recv_o)
        for cp in started:
            cp.start()
        token[...] = jnp.zeros_like(token)

    lands = [lax.empty((N_CHIPS,) + s.shape, s.dtype) for s in shards]
    sems = [pltpu.SemaphoreType.DMA((3 * n,)), pltpu.SemaphoreType.DMA((3 * n,)), pltpu.SemaphoreType.DMA((n,)), pltpu.SemaphoreType.DMA((n,))]
    res = pl.pallas_call(
        body, name=f"gather_{tag}_start",
        in_specs=[HBM_SPEC] * (2 * n) + [ANY_SPEC],
        out_specs=[SEM_SPEC] * 4 + [HBM_SPEC] * (2 * n) + [pl.BlockSpec(memory_space=pltpu.VMEM)],
        out_shape=sems + [jax.ShapeDtypeStruct(s.shape, s.dtype) for s in shards]
        + [jax.ShapeDtypeStruct(a.shape, a.dtype) for a in lands] + [jax.ShapeDtypeStruct((8, 128), F32)],
        input_output_aliases={t: 4 + t for t in range(2 * n)},
        compiler_params=pltpu.CompilerParams(has_side_effects=pltpu.SideEffectType.DATAFLOW_SIDE_EFFECTING),
    )(*[pltpu.with_memory_space_constraint(s, pltpu.HBM) for s in shards],
      *[pltpu.with_memory_space_constraint(a, pltpu.HBM) for a in lands], after)
    return res[:4], res[4:4 + n], res[4 + n:4 + 2 * n], res[-1]


def _gather_wait(sems, shards_thru, lands_thru, after, tag):
    n = len(shards_thru)

    def body(*refs):
        ins, outs_in = refs[:n], refs[n:2 * n]
        send_i, recv_i, send_o, recv_o = refs[2 * n:2 * n + 4]
        started, awaited = _gather_copies(ins, outs_in, send_i, recv_i, send_o, recv_o)
        for cp in started:
            cp.wait_send()
        for cp in awaited:
            cp.wait_recv()

    res = pl.pallas_call(
        body, name=f"gather_{tag}_wait",
        in_specs=[HBM_SPEC] * (2 * n) + [SEM_SPEC] * 4 + [ANY_SPEC],
        out_specs=[HBM_SPEC] * (2 * n),
        out_shape=[jax.ShapeDtypeStruct(a.shape, a.dtype) for a in list(shards_thru) + list(lands_thru)],
        input_output_aliases={t: t for t in range(2 * n)},
        compiler_params=pltpu.CompilerParams(has_side_effects=pltpu.SideEffectType.DATAFLOW_SIDE_EFFECTING),
    )(*shards_thru, *lands_thru, *sems, after)
    return res[n:]


def _gather_forward(lands, tag):
    n = len(lands)

    def body(*refs):
        outs = refs[n:2 * n]
        send_d, recv_d = refs[2 * n:]
        fwd, fwd_in = _forward_copies(outs, send_d, recv_d)
        for cp in fwd:
            cp.start()
        for cp in fwd_in:
            cp.wait_recv()
        for cp in fwd:
            cp.wait_send()

    return pl.pallas_call(
        body, name=f"gather_{tag}_forward", in_specs=[HBM_SPEC] * n, out_specs=[HBM_SPEC] * n,
        out_shape=[jax.ShapeDtypeStruct(a.shape, a.dtype) for a in lands],
        input_output_aliases={t: t for t in range(n)},
        scratch_shapes=[pltpu.SemaphoreType.DMA((3 * n,)), pltpu.SemaphoreType.DMA((3 * n,))],
    )(*lands)


def _stage1_copies(ins, sib, send, recv):
    x, y, c, me, chips = _place()
    cps = []
    for t in range(len(ins)):
        rows = ins[t].shape[1] // 2
        cps.append(pltpu.make_async_remote_copy(
            src_ref=ins[t].at[:, pl.ds((1 - c) * rows, rows), :], dst_ref=sib[t], send_sem=send.at[t],
            recv_sem=recv.at[t], device_id=(x, y, 1 - c), device_id_type=MESH))
    return cps


def _split_start(copies_fn, srcs, land_shapes, n_sems, tag):
    n = len(srcs)

    def body(*refs):
        send, recv = refs[2 * n:2 * n + 2]
        for cp in copies_fn(refs[:n], refs[3 * n + 2:4 * n + 2], send, recv):
            cp.start()
        refs[4 * n + 2][...] = jnp.zeros_like(refs[4 * n + 2])

    lands = [lax.empty(shp, dt) for shp, dt in land_shapes]
    res = pl.pallas_call(
        body, name=tag,
        in_specs=[HBM_SPEC] * (2 * n),
        out_specs=[SEM_SPEC] * 2 + [HBM_SPEC] * (2 * n) + [pl.BlockSpec(memory_space=pltpu.VMEM)],
        out_shape=[pltpu.SemaphoreType.DMA((n_sems,)), pltpu.SemaphoreType.DMA((n_sems,))]
        + [jax.ShapeDtypeStruct(p.shape, p.dtype) for p in srcs]
        + [jax.ShapeDtypeStruct(a.shape, a.dtype) for a in lands] + [jax.ShapeDtypeStruct((8, 128), F32)],
        input_output_aliases={t: 2 + t for t in range(2 * n)},
        compiler_params=pltpu.CompilerParams(has_side_effects=pltpu.SideEffectType.DATAFLOW_SIDE_EFFECTING),
    )(*[pltpu.with_memory_space_constraint(p, pltpu.HBM) for p in srcs],
      *[pltpu.with_memory_space_constraint(a, pltpu.HBM) for a in lands])
    return res[:2], res[2:2 + n], res[2 + n:2 + 2 * n], res[-1]


def _split_wait(copies_fn, sems, srcs_thru, lands_thru, after, tag):
    n = len(srcs_thru)

    def body(*refs):
        for cp in copies_fn(refs[:n], refs[n:2 * n], refs[2 * n], refs[2 * n + 1]):
            cp.wait()

    res = pl.pallas_call(
        body, name=tag,
        in_specs=[HBM_SPEC] * (2 * n) + [SEM_SPEC] * 2 + [ANY_SPEC],
        out_specs=[HBM_SPEC] * (2 * n),
        out_shape=[jax.ShapeDtypeStruct(a.shape, a.dtype) for a in list(srcs_thru) + list(lands_thru)],
        input_output_aliases={t: t for t in range(2 * n)},
        compiler_params=pltpu.CompilerParams(has_side_effects=pltpu.SideEffectType.DATAFLOW_SIDE_EFFECTING),
    )(*srcs_thru, *lands_thru, *sems, after)
    return res[:n], res[n:]


def _stage2_copies(ps, rcv, send, recv):
    x, y, c, me, chips = _place()
    return [pltpu.make_async_remote_copy(
        src_ref=ps[t].at[2 * px + py], dst_ref=rcv[t].at[kk], send_sem=send.at[t * 3 + kk],
        recv_sem=recv.at[t * 3 + kk], device_id=(px, py, c), device_id_type=MESH)
        for t in range(len(ps)) for kk, (px, py) in enumerate(chips)]


def _reduce_stage3(reduced, tag):
    n = len(reduced)

    def body(*refs):
        outs = refs[n:2 * n]
        send, recv = refs[2 * n:]
        x, y, c, me, chips = _place()
        cps = []
        for t in range(n):
            rows = outs[t].shape[0] // 2
            mine = outs[t].at[pl.ds(c * rows, rows), :]
            cp = pltpu.make_async_remote_copy(src_ref=mine, dst_ref=mine, send_sem=send.at[t], recv_sem=recv.at[t],
                                              device_id=(x, y, 1 - c), device_id_type=MESH)
            cp.start()
            cps.append(cp)
        for cp in cps:
            cp.wait()

    return pl.pallas_call(
        body, name="reduce_stage3_" + tag, in_specs=[HBM_SPEC] * n, out_specs=[HBM_SPEC] * n,
        out_shape=[jax.ShapeDtypeStruct(r.shape, r.dtype) for r in reduced],
        input_output_aliases={t: t for t in range(n)},
        scratch_shapes=[pltpu.SemaphoreType.DMA((n,)), pltpu.SemaphoreType.DMA((n,))],
    )(*reduced)


def _allreduce_small(v):
    rows, cols = v.shape

    def body(v_ref, o_ref, buf, send, recv, loc):
        x, y, c, me, chips = _place()
        mine = 4 * x + 2 * y + c
        lc = pltpu.make_async_copy(v_ref, buf.at[mine], loc)
        lc.start()
        peers = []
        for fx in range(2):
            for fy in range(2):
                for fc in range(2):
                    if fx or fy or fc:
                        peers.append((fx, fy, fc))
        cps = []
        for kk, (fx, fy, fc) in enumerate(peers):
            to = (x ^ fx, y ^ fy, c ^ fc)
            cp = pltpu.make_async_remote_copy(src_ref=v_ref, dst_ref=buf.at[mine], send_sem=send.at[kk], recv_sem=recv.at[kk],
                                              device_id=to, device_id_type=MESH)
            cp.start()
            cps.append((cp, to))
        for kk, (cp, to) in enumerate(cps):
            src = 4 * to[0] + 2 * to[1] + to[2]
            pltpu.make_async_remote_copy(src_ref=v_ref, dst_ref=buf.at[src], send_sem=send.at[kk], recv_sem=recv.at[kk],
                                         device_id=to, device_id_type=MESH).wait_recv()
        for cp, _ in cps:
            cp.wait_send()
        lc.wait()
        acc = buf[0]
        for d in range(1, 8):
            acc = acc + buf[d]
        o_ref[...] = acc

    return pl.pallas_call(
        body, name="allreduce_small", in_specs=[pl.BlockSpec(memory_space=pltpu.VMEM)],
        out_specs=pl.BlockSpec(memory_space=pltpu.VMEM), out_shape=jax.ShapeDtypeStruct((rows, cols), F32),
        scratch_shapes=[pltpu.VMEM((8, rows, cols), F32), pltpu.SemaphoreType.DMA((7,)), pltpu.SemaphoreType.DMA((7,)),
                        pltpu.SemaphoreType.DMA],
        compiler_params=pltpu.CompilerParams(vmem_limit_bytes=VMEM_LIMIT_V7X),
    )(v)


def _pad_w_in(w):
    z = lambda n: jnp.zeros(w.shape[:-1] + (n,), w.dtype)
    return jnp.concatenate([w[..., 0:384], z(64), w[..., 384:416], z(32), w[..., 416:1824],
                            w[..., 1824:1830], z(58), w[..., 400:416], w[..., 384:400], z(32)], axis=-1)


def _unpad_w_in(g):
    x1 = g[..., 448:464] + g[..., Z_F + 80:Z_F + 96]
    x2 = g[..., 464:480] + g[..., Z_F + 64:Z_F + 80]
    return jnp.concatenate([g[..., 0:384], x1, x2, g[..., 512:1920], g[..., 1920:1926]], axis=-1)


def _block_diag(pw):
    out = jnp.zeros((POOL_W, POOL_W), pw.dtype)
    for g in range(4):
        out = out.at[g * 64:(g + 1) * 64, g * 64:(g + 1) * 64].set(pw[g])
    return out


def _rope_tables(s):
    inv_freq = ROPE_THETA ** (-jnp.arange(0, ROPE, 2, dtype=F32) / ROPE)
    ang = jnp.arange(s, dtype=jnp.int32).astype(F32)[:, None] * inv_freq[None, :]
    cos, sin = jnp.cos(ang), jnp.sin(ang)
    zero = lambda n: jnp.zeros((s, n), F32)
    ck = jnp.concatenate([zero(NOPE), cos, cos, zero(DA - NOPE - ROPE)], axis=1)
    sk = jnp.concatenate([zero(NOPE), -sin, sin, zero(DA - NOPE - ROPE)], axis=1)
    cq = jnp.concatenate([jnp.ones((s, NOPE), F32), cos, cos, zero(DA - NOPE - ROPE)], axis=1) * SCALE_MLA
    return dict(cq=cq, sq=sk * SCALE_MLA, ck=ck, sk=sk)


def _mix_fwd(l, x1, wts, sm, tabs):
    z, h2 = _norm_mm(x1, 0, sm["mix_norm"][l], wts["w_in"][l], name=f"mix_in_{l}")
    qa, qn = _mla_q_prep(z, sm["q_a_norm"][l], wts["wq_a"][l], wts["wq_b"][l], tabs["cq"], tabs["sq"], name=f"mla_q_{l}")
    ka, va, kvn = _mla_kv_prep(z, sm["kv_a_norm"][l], wts["wk"][l], wts["wv"][l], tabs["ck"], tabs["sk"], name=f"mla_kv_{l}")
    oa, lse_a = _attn_fwd(qa, ka, va, VDIM, name=f"mla_attn_{l}")

    bd = _block_diag(wts["pool_w"][l]).astype(BF16)
    yb, pooled = _pool_fwd(z, Z_POOL // POOL_W, bd, sm["pool_scale"][l], name=f"pool_{l}")

    fb = jnp.pad(sm["fox_b_f"][l], (0, 8 - H)).reshape(8, 1)
    ft, c3t = _gate_fwd(z, Z_F // DA, fb, name=f"fox_gate_{l}")
    fqa, fka, fva = _fox_prep(z, c3t, name=f"fox_prep_{l}")
    oc, lse_c = _attn_fwd(fqa, fka, fva, FOX_D, name=f"fox_attn_{l}")

    x2, cat = _mix_out(oa, yb, oc, wts["w_out"][l], x1, name=f"mix_out_{l}")
    saved = dict(z=z, h2=h2, qn=qn, kvn=kvn, qa=qa, ka=ka, va=va, oa=oa, lse_a=lse_a, bd=bd, pooled=pooled,
                 fqa=fqa, fka=fka, fva=fva, ft=ft, fb=fb, oc=oc, lse_c=lse_c, cat=cat)
    return x2, saved


def _mix_bwd(l, x1, dx2, sv, wts, sm, tabs, tok=None):
    s = x1.shape[0]
    g = {}
    dx2b = (dx2 if tok is None else dx2 + tok).astype(BF16)
    g["w_out"] = _mm(sv["cat"], dx2b, "tn", name=f"d_w_out_{l}", tm=1024, tn=1024, tk=DW_TOKENS)
    doa, doc, dyb, dl_a, dl_c = _mix_out_bwd(dx2b, wts["w_out"][l], sv["oa"], sv["oc"], name=f"mix_out_bwd_{l}")

    dfqa, dfka, dfva, dcq, dck = _attn_bwd(sv["fqa"], sv["fka"], sv["fva"], doc, sv["lse_c"], dl_c, True, name=f"fox_attn_bwd_{l}")
    dfox = _fox_bwd_prep(dfqa, dfka, dfva, name=f"fox_bwd_prep_{l}")
    dc = jnp.pad(dcq.reshape(H, s) + dck.reshape(H, s), ((0, 8 - H), (0, 0)))
    dft, dfb = _gate_bwd(sv["ft"], sv["fb"], dc, name=f"fox_gate_bwd_{l}")
    g["fox_b_f"] = dfb[:H, 0]

    dq, dys, g["pool_scale"] = _pool_bwd_a(dyb, sv["pooled"], sv["bd"], sm["pool_scale"][l], name=f"pool_bwd_a_{l}")
    du = _pool_bwd_b(dq, name=f"pool_bwd_b_{l}")
    dbd = _mm(sv["pooled"], dys, "tn", name=f"d_pool_w_{l}")
    g["pool_w"] = jnp.stack([dbd[i * 64:(i + 1) * 64, i * 64:(i + 1) * 64] for i in range(4)])

    dqa_, dka_, dva_ = _attn_bwd(sv["qa"], sv["ka"], sv["va"], doa, sv["lse_a"], dl_a, False, name=f"mla_attn_bwd_{l}")
    dqab, dkv, dz3, dz15 = _mla_bwd_prep(dqa_, dka_, dva_, dft, tabs["cq"], tabs["sq"], tabs["ck"], tabs["sk"],
                                         name=f"mla_bwd_prep_{l}")
    wq_ab = jnp.concatenate([wts["wq_a"][l], wts["wq_b"][l]], axis=1)
    wkv = jnp.concatenate([wts["wk"][l], wts["wv"][l]], axis=1)
    dwq = _mm(sv["qn"], dqab, "tn", name=f"d_w_q_b_{l}", tn=768, tk=DW_TOKENS).reshape(Q_RANK, 2, H, DA)
    dwkv = _mm(sv["kvn"], dkv, "tn", name=f"d_w_kv_b_{l}", tn=768, tk=DW_TOKENS).reshape(KV_RANK, 2, H, DA)
    da, db = dwq[:, 0], dwq[:, 1]
    swapped = jnp.concatenate([jnp.zeros((Q_RANK, H, NOPE), F32), db[..., NOPE + HALF_ROPE:NOPE + ROPE],
                               db[..., NOPE:NOPE + HALF_ROPE]], axis=-1)
    g["w_q_b"] = (da[..., :NOPE + ROPE] + swapped).reshape(Q_RANK, H * (NOPE + ROPE))
    g["w_kv_b"] = jnp.concatenate([dwkv[:, 0, :, :NOPE], dwkv[:, 1, :, :VDIM]], axis=-1).reshape(KV_RANK, H * (NOPE + VDIM))
    dqn = _mm(dqab, wq_ab, "nt", name=f"d_qn_{l}", tk=2 * H * DA)
    dkvn = _mm(dkv, wkv, "nt", name=f"d_kvn_{l}", tk=2 * H * DA)
    dqa, g["q_a_norm"] = _rmsnorm_bwd(sv["z"], Z_QA // Q_RANK, sm["q_a_norm"][l], dqn, name=f"q_a_norm_bwd_{l}")
    dkva, g["kv_a_norm"] = _rmsnorm_bwd(sv["z"], Z_KVA // KV_RANK, sm["kv_a_norm"][l], dkvn, name=f"kv_a_norm_bwd_{l}")

    dz = jnp.concatenate([dqa.astype(BF16), dkva.astype(BF16), dz3, du.astype(BF16), dfox, dz15], axis=1)
    g["w_in"] = _mm(sv["h2"], dz, "tn", name=f"d_w_in_{l}", tm=1024, tn=1024, tk=DW_TOKENS)
    dh2 = _mm(dz, wts["w_in"][l], "nt", name=f"d_h2_{l}", tn=1024, tk=NZ)
    dx1, g["mix_norm"] = _rmsnorm_bwd(x1, 0, sm["mix_norm"][l], dh2, dx2, name=f"mix_norm_bwd_{l}")
    return dx1, g


DW_TOKENS = 2048


def _local_step(x, target, wts, sm, late_weights=None, grads_ready=None):
    s = x.shape[0]
    tabs = _rope_tables(s)
    acts = []
    xs = x
    for l in range(DEPTH):
        x1, gu1, act1 = _ffn_fwd(xs, sm["ffn1_norm"][l], wts["ffn1_w_gu"][l], wts["ffn1_w_d2"][l], name=f"ffn1_fwd_{l}")
        if l == 0 and late_weights is not None:
            sm = late_weights("ffn1", x1, sm)
        x2, sv = _mix_fwd(l, x1, wts, sm, tabs)
        if l == 0 and late_weights is not None:
            sm = late_weights("mix", x2, sm)
        x3, gu2, act2 = _ffn_fwd(x2, sm["ffn2_norm"][l], wts["ffn2_w_gu"][l], wts["ffn2_w_d2"][l], name=f"ffn2_fwd_{l}")
        acts.append((xs, gu1, act1, x1, sv, x2, gu2, act2))
        xs = x3
    dx, g_final, loss = _loss_head(xs, sm["final_norm"], target, name="loss_head")
    grads = [dict() for _ in range(DEPTH)]
    for l in reversed(range(DEPTH)):
        x0, gu1, act1, x1, sv, x2, gu2, act2 = acts[l]
        g = grads[l]
        dx, dgu, hh, dy, g["ffn2_norm"] = _ffn_bwd(x2, dx, gu2, sm["ffn2_norm"][l], wts["ffn2_w_gu"][l], wts["ffn2_w_d2"][l],
                                                   name=f"ffn2_bwd_{l}")
        g["ffn2_w_down"] = _mm(act2, dy, "tn", name=f"d_ffn2_w_down_{l}", tm=FF_SHARD, tn=1024, tk=DW_TOKENS)
        g["ffn2_w_gu"] = _mm(hh, dgu, "tn", name=f"d_ffn2_w_gu_{l}", tm=1024, tn=FF_SHARD, tk=DW_TOKENS, n_major_out=True)
        tok = None
        if grads_ready is not None:
            sm, tok = grads_ready(l, "ffn2", g, sm)
        dx, gm = _mix_bwd(l, x1, dx, sv, wts, sm, tabs, tok)
        g.update(gm)
        if grads_ready is not None:
            sm, _ = grads_ready(l, "mix", g, sm)
        dx, dgu, hh, dy, g["ffn1_norm"] = _ffn_bwd(x0, dx, gu1, sm["ffn1_norm"][l], wts["ffn1_w_gu"][l], wts["ffn1_w_d2"][l],
                                                   name=f"ffn1_bwd_{l}")
        if grads_ready is not None:
            sm, tok = grads_ready(l, "ffn1_tokens", {"dx": dx}, sm)
            if tok is not None:
                dy = dy + tok.astype(BF16)
        g["ffn1_w_down"] = _mm(act1, dy, "tn", name=f"d_ffn1_w_down_{l}", tm=FF_SHARD, tn=1024, tk=DW_TOKENS)
        g["ffn1_w_gu"] = _mm(hh, dgu, "tn", name=f"d_ffn1_w_gu_{l}", tm=1024, tn=FF_SHARD, tk=DW_TOKENS, n_major_out=True)
        if grads_ready is not None:
            sm, _ = grads_ready(l, "ffn1", g, sm)
    return loss, dx, grads, g_final


BIG = ["ffn1_w_gu", "ffn1_w_down", "w_in", "w_q_b", "w_kv_b", "w_out", "ffn2_w_gu", "ffn2_w_down"]
SMALL = ["ffn1_norm", "mix_norm", "q_a_norm", "kv_a_norm", "pool_w", "pool_scale", "fox_b_f", "ffn2_norm"]
SMALL_ROWS = 48


WEIGHT_VIEWS = ["ffn1_w_gu", "ffn1_w_d2", "w_in", "wq_a", "wq_b", "wk", "wv", "w_out", "ffn2_w_gu", "ffn2_w_d2"]


def _prepare_weights(gathered, wts):
    for (nm, l), w in gathered.items():
        if nm in ("ffn1_w_gu", "ffn2_w_gu"):
            wts[nm][l] = w
        elif nm in ("ffn1_w_down", "ffn2_w_down"):
            wts[nm[:5] + "w_d2"][l] = w.reshape(2, FF_SHARD, D)
        elif nm in ("w_in", "w_out"):
            wts[nm][l] = w.reshape(D, -1)
        elif nm == "w_q_b":
            wq = jnp.moveaxis(w, 0, 1).reshape(Q_RANK, H, NOPE + ROPE)
            zq = lambda n: jnp.zeros((Q_RANK, H, n), BF16)
            wts["wq_a"][l] = jnp.concatenate([wq, zq(DA - NOPE - ROPE)], axis=-1).reshape(Q_RANK, H * DA)
            wts["wq_b"][l] = jnp.concatenate([zq(NOPE), wq[..., NOPE + HALF_ROPE:], wq[..., NOPE:NOPE + HALF_ROPE],
                                              zq(DA - NOPE - ROPE)], axis=-1).reshape(Q_RANK, H * DA)
        else:
            wkv = jnp.moveaxis(w, 0, 1).reshape(KV_RANK, H, NOPE + VDIM)
            zk = jnp.zeros((KV_RANK, H, DA - NOPE), BF16)
            wts["wk"][l] = jnp.concatenate([wkv[..., :NOPE], zk], axis=-1).reshape(KV_RANK, H * DA)
            wts["wv"][l] = jnp.concatenate([wkv[..., NOPE:], zk], axis=-1).reshape(KV_RANK, H * DA)


def _chip_major(name, g):
    if name in ("ffn1_w_gu", "ffn2_w_gu"):
        return g
    if name in ("ffn1_w_down", "ffn2_w_down", "w_in", "w_out"):
        return g.reshape(N_CHIPS, g.shape[0] // N_CHIPS, g.shape[1])
    return jnp.moveaxis(g.reshape(g.shape[0], N_CHIPS, g.shape[1] // N_CHIPS), 1, 0)


def _pack_small(grads, g_final, loss):
    parts = []
    for l in range(DEPTH):
        for nm in SMALL:
            parts.append(grads[l][nm].reshape(-1))
    parts.append(g_final.reshape(-1))
    parts.append(loss.reshape(1))
    flat = jnp.concatenate(parts)
    return jnp.pad(flat, (0, SMALL_ROWS * D - flat.shape[0])).reshape(SMALL_ROWS, D)


def _unpack_small(packed, params):
    flat = packed.reshape(-1)
    out = {nm: [] for nm in SMALL}
    off = 0
    for l in range(DEPTH):
        for nm in SMALL:
            shp = params[nm].shape[1:]
            n = int(np.prod(shp))
            out[nm].append(flat[off:off + n].reshape(shp))
            off += n
    res = {nm: jnp.stack(v) for nm, v in out.items()}
    res["final_norm"] = flat[off:off + D]
    return res, flat[off + D]


def _update(name, w, g, m, v):
    shp = w.shape
    if w.ndim == 1:
        view = (1, shp[0])
    elif w.size <= 65536:
        view = (shp[0], w.size // shp[0])
    else:
        view = (w.size // shp[-1], shp[-1])
    tr = view[0]
    for cand in (512, 352, 256, 128):
        if view[0] % cand == 0 and view[0] > cand:
            tr = cand
            break
    d, mn, vn = _adamw(w.reshape(view), g.reshape(view), m.reshape(view), v.reshape(view), name="adamw_" + name, tr=tr)
    return d.reshape(shp), mn.reshape(shp), vn.reshape(shp)


WEIGHTS = ['ffn1_norm', 'ffn1_w_gu', 'ffn1_w_down', 'mix_norm', 'w_in', 'q_a_norm', 'w_q_b', 'kv_a_norm', 'w_kv_b', 'pool_w',
           'pool_scale', 'fox_b_f', 'w_out', 'ffn2_norm', 'ffn2_w_gu', 'ffn2_w_down', 'final_norm']


def kernel(x, ffn1_norm, ffn1_w_gu, ffn1_w_down, mix_norm, w_in, q_a_norm, w_q_b, kv_a_norm, w_kv_b, pool_w, pool_scale, fox_b_f, w_out, ffn2_norm, ffn2_w_gu, ffn2_w_down, final_norm, loss_target, m_ffn1_norm, m_ffn1_w_gu, m_ffn1_w_down, m_mix_norm, m_w_in, m_q_a_norm, m_w_q_b, m_kv_a_norm, m_w_kv_b, m_pool_w, m_pool_scale, m_fox_b_f, m_w_out, m_ffn2_norm, m_ffn2_w_gu, m_ffn2_w_down, m_final_norm, v_ffn1_norm, v_ffn1_w_gu, v_ffn1_w_down, v_mix_norm, v_w_in, v_q_a_norm, v_w_q_b, v_kv_a_norm, v_w_kv_b, v_pool_w, v_pool_scale, v_fox_b_f, v_w_out, v_ffn2_norm, v_ffn2_w_gu, v_ffn2_w_down, v_final_norm):
    params = dict(ffn1_norm=ffn1_norm, ffn1_w_gu=ffn1_w_gu, ffn1_w_down=ffn1_w_down, mix_norm=mix_norm, w_in=w_in, q_a_norm=q_a_norm,
                  w_q_b=w_q_b, kv_a_norm=kv_a_norm, w_kv_b=w_kv_b, pool_w=pool_w, pool_scale=pool_scale, fox_b_f=fox_b_f, w_out=w_out,
                  ffn2_norm=ffn2_norm, ffn2_w_gu=ffn2_w_gu, ffn2_w_down=ffn2_w_down, final_norm=final_norm)
    mom = dict(ffn1_norm=m_ffn1_norm, ffn1_w_gu=m_ffn1_w_gu, ffn1_w_down=m_ffn1_w_down, mix_norm=m_mix_norm, w_in=m_w_in,
               q_a_norm=m_q_a_norm, w_q_b=m_w_q_b, kv_a_norm=m_kv_a_norm, w_kv_b=m_w_kv_b, pool_w=m_pool_w, pool_scale=m_pool_scale,
               fox_b_f=m_fox_b_f, w_out=m_w_out, ffn2_norm=m_ffn2_norm, ffn2_w_gu=m_ffn2_w_gu, ffn2_w_down=m_ffn2_w_down,
               final_norm=m_final_norm)
    var = dict(ffn1_norm=v_ffn1_norm, ffn1_w_gu=v_ffn1_w_gu, ffn1_w_down=v_ffn1_w_down, mix_norm=v_mix_norm, w_in=v_w_in,
               q_a_norm=v_q_a_norm, w_q_b=v_w_q_b, kv_a_norm=v_kv_a_norm, w_kv_b=v_w_kv_b, pool_w=v_pool_w, pool_scale=v_pool_scale,
               fox_b_f=v_fox_b_f, w_out=v_w_out, ffn2_norm=v_ffn2_norm, ffn2_w_gu=v_ffn2_w_gu, ffn2_w_down=v_ffn2_w_down,
               final_norm=v_final_norm)

    first = [("ffn1_w_gu", 0), ("ffn1_w_down", 0)]
    mix0 = [(nm, 0) for nm in ("w_in", "w_q_b", "w_kv_b", "w_out")]
    rest = [(nm, l) for nm in BIG for l in range(DEPTH) if (nm, l) not in first + mix0]

    def shards(keys, zero=0.0):
        return [((_pad_w_in(params[nm]) if nm == "w_in" else params[nm])[l] + zero).astype(BF16) for nm, l in keys]

    wts = {nm: [None] * DEPTH for nm in WEIGHT_VIEWS}
    wts["pool_w"] = params["pool_w"]
    got = _gather_blocking(shards(first))
    _prepare_weights(dict(zip(first, got)), wts)
    sems_m, src_m, land_m, token_m = _gather_start(shards(mix0), got[0], "mix0")
    sm = dict(params)
    sm["ffn1_norm"] = params["ffn1_norm"] + token_m[0, 0]
    rest_shards = shards(rest, token_m[0, 0])
    flying = {}

    def late_weights(stage, act, sm_now):
        if stage == "ffn1":
            lands = _gather_forward(_gather_wait(sems_m, src_m, land_m, act, "mix0"), "mix0")
            _prepare_weights(dict(zip(mix0, lands)), wts)
            flying["rest"] = _gather_start(rest_shards, lands[0], "rest")
            sm_next = dict(sm_now)
            sm_next["mix_norm"] = sm_now["mix_norm"] + flying["rest"][3][0, 0]
            return sm_next
        sems_r, src_r, land_r, _ = flying["rest"]
        lands = _gather_forward(_gather_wait(sems_r, src_r, land_r, act, "rest"), "rest")
        _prepare_weights(dict(zip(rest, lands)), wts)
        return sm_now

    pos = _position()
    flight = {}

    groups = {"l1": (1, BIG), "l0a": (0, [nm for nm in BIG if not nm.startswith("ffn1")]),
              "l0b": (0, [nm for nm in BIG if nm.startswith("ffn1")])}
    pending = {}

    def to_chips(key, full, sib):
        psum = _sum2_bf16(pos, full, sib, name=f"chip_sum_{key}")
        s2 = _split_start(_stage2_copies, psum, [((3,) + p.shape[1:], p.dtype) for p in psum], 3 * len(psum),
                          f"reduce_stage2_start_{key}")
        flight[key] = (full, sib, s2)
        return s2[3][0, 0]

    def grads_ready(l, stage, g, sm_now):
        behind, tok = None, None
        if (l, stage) == (1, "ffn1"):
            full = [_chip_major(nm, g[nm]) for nm in BIG]
            pending["l1"] = _split_start(_stage1_copies, full, [((N_CHIPS, f.shape[1] // 2, f.shape[2]), F32) for f in full],
                                         len(full), "reduce_stage1_start_l1")
            behind, tok = "ffn2_norm", pending["l1"][3][0, 0]
        elif (l, stage) == (0, "ffn2"):
            sems1, full_thru, sib_land, _ = pending["l1"]
            full, sib = _split_wait(_stage1_copies, sems1, full_thru, sib_land, g["ffn2_w_down"], "reduce_stage1_wait_l1")
            tok = to_chips("l1", full, sib)
        elif (l, stage) == (0, "mix"):
            full = [_chip_major(nm, g[nm]) for nm in groups["l0a"][1]]
            pending["l0a"] = _split_start(_stage1_copies, full, [((N_CHIPS, f.shape[1] // 2, f.shape[2]), F32) for f in full],
                                          len(full), "reduce_stage1_start_l0a")
            behind, tok = "ffn1_norm", pending["l0a"][3][0, 0]
        elif (l, stage) == (0, "ffn1_tokens"):
            sems1, full_thru, sib_land, _ = pending["l0a"]
            full, sib = _split_wait(_stage1_copies, sems1, full_thru, sib_land, g["dx"], "reduce_stage1_wait_l0a")
            tok = to_chips("l0a", full, sib)
        elif (l, stage) == (0, "ffn1"):
            full = [_chip_major(nm, g[nm]) for nm in groups["l0b"][1]]
            pending["l0b"] = _split_start(_stage1_copies, full, [((N_CHIPS, f.shape[1] // 2, f.shape[2]), F32) for f in full],
                                          len(full), "reduce_stage1_start_l0b")
        if behind is None:
            return sm_now, tok
        sm_next = dict(sm_now)
        sm_next[behind] = sm_now[behind] + tok
        return sm_next, tok

    loss, dx, grads, g_final = _local_step(x[0], loss_target[0], wts, sm, late_weights, grads_ready)

    def view2d(a):
        return a.reshape(a.size // a.shape[-1], a.shape[-1])

    after = pending["l0b"][3]
    done = {nm: None for nm in BIG}
    for key in ("l1", "l0a", "l0b"):
        l, names = groups[key]
        full, sib, (sems2, ps_thru, lands2, _) = flight[key]
        _, recv = _split_wait(_stage2_copies, sems2, ps_thru, lands2, after, f"reduce_stage2_wait_{key}")
        whole = _reduce_stage3(_sum5(pos, full, sib, recv, name=f"grad_sum_{key}"), key)
        for nm, g_l in zip(names, whole):
            if nm == "w_in":
                g_l = _unpad_w_in(g_l)
            tr = max(t for t in (512, 352, 256, 128) if g_l.shape[0] % t == 0)
            done[nm] = _adamw_layer(view2d(params[nm]), g_l, view2d(mom[nm]), view2d(var[nm]), l, done[nm],
                                    name=f"adamw_{nm}_{l}", tr=tr)
        after = done[names[-1]][0][-8:, 0:128]
        if key == "l1":
            small_g, loss = _unpack_small(_allreduce_small(_pack_small(grads, g_final, loss)), params)
            sems1, full_thru, sib_land, _ = pending["l0b"]
            full_b, sib_b = _split_wait(_stage1_copies, sems1, full_thru, sib_land, after + small_g["final_norm"][0],
                                        "reduce_stage1_wait_l0b")
            after = after + to_chips("l0b", full_b, sib_b)
    gw, delta, new_m, new_v = dict(small_g), {}, {}, {}
    for nm in BIG:
        delta[nm], new_m[nm], new_v[nm], gw[nm] = [a.reshape(params[nm].shape) for a in done[nm]]
    for nm in small_g:
        delta[nm], new_m[nm], new_v[nm] = _update(nm, params[nm], gw[nm], mom[nm], var[nm])
    return (loss, dx[None], *[gw[n] for n in WEIGHTS], *[delta[n] for n in WEIGHTS], *[new_m[n] for n in WEIGHTS],
            *[new_v[n] for n in WEIGHTS])
```

```python
import functools
import math

import jax
import jax.numpy as jnp
import numpy as np
from jax import lax
from jax.experimental import pallas as pl
from jax.experimental.pallas import tpu as pltpu

F32 = jnp.float32
BF16 = jnp.bfloat16
MESH = pl.DeviceIdType.MESH
HBM_SPEC = pl.BlockSpec(memory_space=pltpu.HBM)

D = 1024
DEPTH = 2
D_FF = 2816
FF_SHARD = 1408
N_CHIPS = 4
H = 6
NOPE, ROPE, VDIM = 64, 32, 64
HALF_ROPE = ROPE // 2
Q_RANK, KV_RANK = 256, 128
POOL_W = 256
FOX_D = 64
N_IN = 1830
NZ = 2048
ROPE_THETA = 10000.0
EPS = 1e-6
POOL_HALO = 16
Z_QA, Z_KVA, Z_KR, Z_POOL, Z_FOX, Z_F = 0, 256, 384, 512, 768, 1920

ADAM_LR, ADAM_B1, ADAM_B2, ADAM_EPS, ADAM_WD, ADAM_STEP = 0.001, 0.9, 0.999, 1e-08, 0.01, 10

VMEM_LIMIT_V7X = 56 * 1024 * 1024


def _cp(sem=None, vmem=VMEM_LIMIT_V7X):
    return pltpu.CompilerParams(dimension_semantics=sem, vmem_limit_bytes=vmem)


def _sigmoid(x):
    return 0.5 * jnp.tanh(0.5 * x) + 0.5


def _dot(a, b, dims):
    return lax.dot_general(a, b, (dims, ((), ())), preferred_element_type=F32)


NN = ((1,), (0,))
NT = ((1,), (1,))
TN = ((0,), (0,))


def _mm(a, b, mode, *, name, out_dtype=F32, add=None, alpha=None, tm=512, tn=512, tk=512, n_major_out=False):
    if mode == "nn":
        (m, k), (k2, n) = a.shape, b.shape
    elif mode == "nt":
        (m, k), (n, k2) = a.shape, b.shape
    else:
        (k, m), (k2, n) = a.shape, b.shape
    assert k == k2
    tm, tn, tk = min(tm, m), min(tn, n), min(tk, k)
    assert m % tm == 0 and n % tn == 0 and k % tk == 0, (name, m, n, k, tm, tn, tk)
    nk = k // tk
    dims = {"nn": NN, "nt": NT, "tn": TN}[mode]
    a_spec = pl.BlockSpec((tk, tm), lambda i, j, kk: (kk, i)) if mode == "tn" else pl.BlockSpec((tm, tk), lambda i, j, kk: (i, kk))
    b_spec = pl.BlockSpec((tn, tk), lambda i, j, kk: (j, kk)) if mode == "nt" else pl.BlockSpec((tk, tn), lambda i, j, kk: (kk, j))
    in_specs = [a_spec, b_spec]
    args = [a, b]
    if add is not None:
        in_specs.append(pl.BlockSpec((tm, tn), lambda i, j, kk: (i, j)))
        args.append(add)
    if n_major_out:
        out_shape = jax.ShapeDtypeStruct((n // tn, m, tn), out_dtype)
        out_spec = pl.BlockSpec((None, tm, tn), lambda i, j, kk: (j, i, 0))
    else:
        out_shape = jax.ShapeDtypeStruct((m, n), out_dtype)
        out_spec = pl.BlockSpec((tm, tn), lambda i, j, kk: (i, j))

    def body(*refs):
        a_ref, b_ref = refs[0], refs[1]
        add_ref = refs[2] if add is not None else None
        o_ref, acc = refs[-2], refs[-1]
        kk = pl.program_id(2)

        @pl.when(kk == 0)
        def _():
            acc[...] = jnp.zeros_like(acc)

        acc[...] += _dot(a_ref[...].astype(BF16), b_ref[...].astype(BF16), dims)

        @pl.when(kk == nk - 1)
        def _():
            r = acc[...]
            if alpha is not None:
                r = r * alpha
            if add_ref is not None:
                r = r + add_ref[...].astype(F32)
            o_ref[...] = r.astype(out_dtype)

    return pl.pallas_call(
        body, name=name, grid=(m // tm, n // tn, nk), in_specs=in_specs, out_specs=out_spec, out_shape=out_shape,
        scratch_shapes=[pltpu.VMEM((tm, tn), F32)],
        compiler_params=_cp(("parallel", "parallel", "arbitrary")),
    )(*args)


def _norm_mm(x, col_block, gain, w, *, name, tm=512):
    s = x.shape[0]
    k, n = w.shape
    tm = min(tm, s)

    def body(x_ref, g_ref, w_ref, z_ref, h_ref):
        xv = x_ref[...]
        r = lax.rsqrt(jnp.mean(xv * xv, axis=-1, keepdims=True) + EPS)
        hv = (xv * r * g_ref[...]).astype(BF16)
        h_ref[...] = hv
        z_ref[...] = _dot(hv, w_ref[...], NN)

    return pl.pallas_call(
        body, name=name, grid=(s // tm,),
        in_specs=[pl.BlockSpec((tm, k), lambda i: (i, col_block)), pl.BlockSpec((1, k), lambda i: (0, 0)),
                  pl.BlockSpec((k, n), lambda i: (0, 0))],
        out_specs=[pl.BlockSpec((tm, n), lambda i: (i, 0)), pl.BlockSpec((tm, k), lambda i: (i, 0))],
        out_shape=[jax.ShapeDtypeStruct((s, n), F32), jax.ShapeDtypeStruct((s, k), BF16)],
        compiler_params=_cp(("parallel",)),
    )(x, gain.reshape(1, k), w)


def _rmsnorm_bwd(x, col_block, gain, da, w, dres=None, *, name, tm=512):
    s = x.shape[0]
    k, n = w.shape
    tm = min(tm, s)

    def body(*refs):
        x_ref, g_ref, da_ref, w_ref = refs[:4]
        dres_ref = refs[4] if dres is not None else None
        dx_ref, dg_ref = refs[-2], refs[-1]
        xv = x_ref[...]
        r = lax.rsqrt(jnp.mean(xv * xv, axis=-1, keepdims=True) + EPS)
        dhv = _dot(da_ref[...], w_ref[...], NT)
        a = dhv * g_ref[...]
        dx = r * a - xv * (r * r * r) * jnp.mean(a * xv, axis=-1, keepdims=True)
        if dres_ref is not None:
            dx = dx + dres_ref[...]
        dx_ref[...] = dx

        @pl.when(pl.program_id(0) == 0)
        def _():
            dg_ref[...] = jnp.zeros_like(dg_ref)

        dg_ref[...] += jnp.sum(dhv * xv * r, axis=0, keepdims=True)

    in_specs = [pl.BlockSpec((tm, k), lambda i: (i, col_block)), pl.BlockSpec((1, k), lambda i: (0, 0)),
                pl.BlockSpec((tm, n), lambda i: (i, 0)), pl.BlockSpec((k, n), lambda i: (0, 0))]
    args = [x, gain.reshape(1, k), da, w]
    if dres is not None:
        in_specs.append(pl.BlockSpec((tm, k), lambda i: (i, 0)))
        args.append(dres)
    dx, dg = pl.pallas_call(
        body, name=name, grid=(s // tm,), in_specs=in_specs,
        out_specs=[pl.BlockSpec((tm, k), lambda i: (i, 0)), pl.BlockSpec((1, k), lambda i: (0, 0))],
        out_shape=[jax.ShapeDtypeStruct((s, k), F32), jax.ShapeDtypeStruct((1, k), F32)],
        compiler_params=_cp(("arbitrary",)),
    )(*args)
    return dx, dg.reshape(k)


def _ffn_fwd(x, gain, w_gu4, w_d2, *, name, tm=256):
    s = x.shape[0]
    tm = min(tm, s)

    def body(x_ref, g_ref, wgu_ref, wd_ref, xo_ref, dgu_ref, act_ref):
        xv = x_ref[...]
        r = lax.rsqrt(jnp.mean(xv * xv, axis=-1, keepdims=True) + EPS)
        hv = (xv * r * g_ref[...]).astype(BF16)
        y = jnp.zeros((tm, D), F32)
        for j in range(2):
            g = _dot(hv, wgu_ref[j], NN)
            u = _dot(hv, wgu_ref[2 + j], NN)
            sg = _sigmoid(g)
            silu = g * sg
            dgu_ref[:, j * FF_SHARD:(j + 1) * FF_SHARD] = (u * (sg * (1.0 + g * (1.0 - sg)))).astype(BF16)
            dgu_ref[:, D_FF + j * FF_SHARD:D_FF + (j + 1) * FF_SHARD] = silu.astype(BF16)
            act = (silu * u).astype(BF16)
            act_ref[:, j * FF_SHARD:(j + 1) * FF_SHARD] = act
            y = y + _dot(act, wd_ref[j], NN)
        xo_ref[...] = xv + 0.5 * y

    row = lambda i: (i, 0)
    return pl.pallas_call(
        body, name=name, grid=(s // tm,),
        in_specs=[pl.BlockSpec((tm, D), row), pl.BlockSpec((1, D), lambda i: (0, 0)),
                  pl.BlockSpec((N_CHIPS, D, FF_SHARD), lambda i: (0, 0, 0), pipeline_mode=pl.Buffered(1)),
                  pl.BlockSpec((2, FF_SHARD, D), lambda i: (0, 0, 0), pipeline_mode=pl.Buffered(1))],
        out_specs=[pl.BlockSpec((tm, D), row), pl.BlockSpec((tm, 2 * D_FF), row), pl.BlockSpec((tm, D_FF), row)],
        out_shape=[jax.ShapeDtypeStruct((s, D), F32), jax.ShapeDtypeStruct((s, 2 * D_FF), BF16),
                   jax.ShapeDtypeStruct((s, D_FF), BF16)],
        compiler_params=_cp(("parallel",)),
    )(x, gain.reshape(1, D), w_gu4, w_d2)


FFN_ROW_CHUNK = 32


def _ffn_bwd(x, dxo, dloc, gain, w_gu4, w_d2, *, name, tm=256):
    s = x.shape[0]
    tm = min(tm, s)

    def body(x_ref, dxo_ref, dloc_ref, g_ref, wgu_ref, wd_ref, dx_ref, dgu_ref, h_ref, dy_ref, dg_ref):
        xv = x_ref[...]
        r = lax.rsqrt(jnp.mean(xv * xv, axis=-1, keepdims=True) + EPS)
        xh = xv * r
        h_ref[...] = (xh * g_ref[...]).astype(BF16)
        dxov = dxo_ref[...]
        dy = (0.5 * dxov).astype(BF16)
        dy_ref[...] = dy
        gcols = [slice(j * FF_SHARD, (j + 1) * FF_SHARD) for j in range(2)]
        ucols = [slice(D_FF + j * FF_SHARD, D_FF + (j + 1) * FF_SHARD) for j in range(2)]
        dacts = [_dot(dy, wd_ref[j], NT) for j in range(2)]
        for r0 in range(0, tm, FFN_ROW_CHUNK):
            rows = slice(r0, r0 + FFN_ROW_CHUNK)
            for j in range(2):
                da = dacts[j][rows]
                dgu_ref[rows, gcols[j]] = (da * dloc_ref[rows, gcols[j]].astype(F32)).astype(BF16)
                dgu_ref[rows, ucols[j]] = (da * dloc_ref[rows, ucols[j]].astype(F32)).astype(BF16)
        dh = jnp.zeros((tm, D), F32)
        for j in range(2):
            dh = dh + _dot(dgu_ref[:, gcols[j]], wgu_ref[j], NT) + _dot(dgu_ref[:, ucols[j]], wgu_ref[2 + j], NT)
        a = dh * g_ref[...]
        dx_ref[...] = dxov + r * a - xh * (r * jnp.mean(a * xh, axis=-1, keepdims=True))

        @pl.when(pl.program_id(0) == 0)
        def _():
            dg_ref[...] = jnp.zeros_like(dg_ref)

        dg_ref[...] += jnp.sum(dh * xh, axis=0, keepdims=True)

    row = lambda i: (i, 0)
    outs = pl.pallas_call(
        body, name=name, grid=(s // tm,),
        in_specs=[pl.BlockSpec((tm, D), row), pl.BlockSpec((tm, D), row), pl.BlockSpec((tm, 2 * D_FF), row),
                  pl.BlockSpec((1, D), lambda i: (0, 0)),
                  pl.BlockSpec((N_CHIPS, D, FF_SHARD), lambda i: (0, 0, 0), pipeline_mode=pl.Buffered(1)),
                  pl.BlockSpec((2, FF_SHARD, D), lambda i: (0, 0, 0), pipeline_mode=pl.Buffered(1))],
        out_specs=[pl.BlockSpec((tm, D), row), pl.BlockSpec((tm, 2 * D_FF), row),
                   pl.BlockSpec((tm, D), row), pl.BlockSpec((tm, D), row), pl.BlockSpec((1, D), lambda i: (0, 0))],
        out_shape=[jax.ShapeDtypeStruct((s, D), F32), jax.ShapeDtypeStruct((s, 2 * D_FF), BF16),
                   jax.ShapeDtypeStruct((s, D), BF16), jax.ShapeDtypeStruct((s, D), BF16), jax.ShapeDtypeStruct((1, D), F32)],
        compiler_params=_cp(("arbitrary",)),
    )(x, dxo, dloc, gain.reshape(1, D), w_gu4, w_d2)
    dx, dgu, h, dy, dg = outs
    return dx, dgu, h, dy, dg.reshape(D)


DA = 128
SCALE_MLA = 1.0 / math.sqrt(NOPE + ROPE)
SCALE_FOX = 1.0 / math.sqrt(FOX_D)


def _causal_blocks(nb, key_major):
    if key_major:
        pairs = [(i, j) for j in range(nb) for i in range(j, nb)]
    else:
        pairs = [(i, j) for i in range(nb) for j in range(i + 1)]
    return (jnp.asarray(np.array([p[0] for p in pairs], np.int32)), jnp.asarray(np.array([p[1] for p in pairs], np.int32)))


HEADS_PER_STEP = 3
ROW_CHUNK = 64

def _col_to_row(col):
    return jnp.broadcast_to(col, (col.shape[0], DA)).T[0:1, :]


def _attn_fwd(qa, ka, va, dv, *, name, t=512):
    h, s, _ = qa.shape
    t = min(t, s)
    nb = s // t
    g = H
    qi, kj = _causal_blocks(nb, key_major=False)

    rc = min(ROW_CHUNK, t)

    def body(qi_ref, kj_ref, q_ref, k_ref, v_ref, o_ref, lse_ref, m_sc, acc_sc, p_sc, a_sc):
        n = pl.program_id(1)
        i, j = qi_ref[n], kj_ref[n]

        @pl.when(j == 0)
        def _():
            m_sc[...] = jnp.full_like(m_sc, -jnp.inf)
            acc_sc[...] = jnp.zeros_like(acc_sc)

        def step(masked):
            scs = [_dot(q_ref[hh], k_ref[hh], NT) for hh in range(g)]
            for r0 in range(0, t, rc):
                rows = slice(r0, r0 + rc)
                for hh in range(g):
                    sr = scs[hh][rows]
                    if masked:
                        row = lax.broadcasted_iota(jnp.int32, (rc, t), 0) + r0
                        col = lax.broadcasted_iota(jnp.int32, (rc, t), 1)
                        sr = jnp.where(col <= row, sr, -jnp.inf)
                    tiles = [sr[:, c0:c0 + DA] for c0 in range(0, t, DA)]
                    top = tiles[0]
                    for tile in tiles[1:]:
                        top = jnp.maximum(top, tile)
                    m_old = m_sc[hh, rows]
                    m_new = jnp.maximum(m_old, jnp.max(top, axis=-1, keepdims=True))
                    for c0, tile in zip(range(0, t, DA), tiles):
                        p_sc[hh, rows, c0:c0 + DA] = jnp.exp(tile - m_new).astype(BF16)
                    a_sc[hh, rows] = jnp.exp(m_old - m_new)
                    m_sc[hh, rows] = m_new
            for hh in range(g):
                acc_sc[hh] = a_sc[hh] * acc_sc[hh] + _dot(p_sc[hh], v_ref[hh], NN)

        @pl.when(j < i)
        def _():
            step(False)

        @pl.when(j == i)
        def _():
            step(True)
            for hh in range(g):
                acc = acc_sc[hh]
                l = acc[:, dv:dv + 1]
                o_ref[hh] = acc[:, :dv] / l
                lse_ref[hh] = _col_to_row(m_sc[hh][:, 0:1] + jnp.log(l))

    qmap = lambda hg, n, qi_r, kj_r: (hg, qi_r[n], 0)
    kmap = lambda hg, n, qi_r, kj_r: (hg, kj_r[n], 0)
    return pl.pallas_call(
        body, name=name,
        grid_spec=pltpu.PrefetchScalarGridSpec(
            num_scalar_prefetch=2, grid=(h // g, qi.shape[0]),
            in_specs=[pl.BlockSpec((g, t, DA), qmap), pl.BlockSpec((g, t, DA), kmap), pl.BlockSpec((g, t, DA), kmap)],
            out_specs=[pl.BlockSpec((g, t, dv), qmap), pl.BlockSpec((g, 1, t), lambda hg, n, qi_r, kj_r: (hg, 0, qi_r[n]))],
            scratch_shapes=[pltpu.VMEM((g, t, DA), F32), pltpu.VMEM((g, t, DA), F32), pltpu.VMEM((g, t, t), BF16),
                            pltpu.VMEM((g, t, DA), F32)]),
        out_shape=[jax.ShapeDtypeStruct((h, s, dv), F32), jax.ShapeDtypeStruct((h, 1, s), F32)],
        compiler_params=_cp(("parallel", "arbitrary")),
    )(qi, kj, qa, ka, va)


def _attn_bwd(qa, ka, va, doa, lse_row, delta_row, decay, *, name, t=512):
    h, s, _ = qa.shape
    t = min(t, s)
    nb = s // t
    g = HEADS_PER_STEP
    rc = min(ROW_CHUNK, t)
    qi, kj = _causal_blocks(nb, key_major=True)
    nsteps = qi.shape[0]

    def body(*refs):
        qi_ref, kj_ref, q_ref, k_ref, v_ref, do_ref, lse_ref, dl_ref = refs[:8]
        p_sc, ds_sc = refs[-2:]
        if decay:
            dq_ref, dk_ref, dv_ref, dcq_ref, dck_ref, dq_acc, dk_acc, dv_acc, dcq_acc, dck_acc = refs[8:-2]
        else:
            dq_ref, dk_ref, dv_ref, dq_acc, dk_acc, dv_acc = refs[8:-2]
        n = pl.program_id(1)
        i, j = qi_ref[n], kj_ref[n]

        @pl.when(n == 0)
        def _():
            dq_acc[...] = jnp.zeros_like(dq_acc)
            if decay:
                dcq_acc[...] = jnp.zeros_like(dcq_acc)

        @pl.when(i == j)
        def _():
            dk_acc[...] = jnp.zeros_like(dk_acc)
            dv_acc[...] = jnp.zeros_like(dv_acc)
            if decay:
                dck_acc[...] = jnp.zeros_like(dck_acc)

        def step(masked):
            sts = [_dot(k_ref[hh], q_ref[hh], NT) for hh in range(g)]
            dpts = [_dot(v_ref[hh], do_ref[hh], NT) for hh in range(g)]
            dcq = [jnp.zeros((1, t), F32) for _ in range(g)]
            for r0 in range(0, t, rc):
                rows = slice(r0, r0 + rc)
                for hh in range(g):
                    st = sts[hh][rows]
                    if masked:
                        row = lax.broadcasted_iota(jnp.int32, (rc, t), 0) + r0
                        col = lax.broadcasted_iota(jnp.int32, (rc, t), 1)
                        st = jnp.where(row <= col, st, -jnp.inf)
                    pt = jnp.exp(st - lse_ref[hh])
                    dst = pt * (dpts[hh][rows] - dl_ref[hh])
                    p_sc[hh, rows] = pt.astype(BF16)
                    ds_sc[hh, rows] = dst.astype(BF16)
                    if decay:
                        dcq[hh] = dcq[hh] + jnp.sum(dst, axis=0, keepdims=True)
                        dck_acc[hh, rows] -= jnp.sum(dst, axis=1, keepdims=True)
            for hh in range(g):
                dv_acc[hh] += _dot(p_sc[hh], do_ref[hh], NN)
                dk_acc[hh] += _dot(ds_sc[hh], q_ref[hh], NN)
                dq_acc[hh, i] += _dot(ds_sc[hh], k_ref[hh], TN)
                if decay:
                    dcq_acc[hh, i] += dcq[hh]

        @pl.when(i > j)
        def _():
            step(False)

        @pl.when(i == j)
        def _():
            step(True)

        @pl.when(i == nb - 1)
        def _():
            dk_ref[...] = dk_acc[...]
            dv_ref[...] = dv_acc[...]
            if decay:
                for hh in range(g):
                    dck_ref[hh] = _col_to_row(dck_acc[hh])

        @pl.when(n == nsteps - 1)
        def _():
            dq_ref[...] = dq_acc[...]
            if decay:
                dcq_ref[...] = dcq_acc[...]

    kmap = lambda hg, n, qi_r, kj_r: (hg, kj_r[n], 0)
    qmap = lambda hg, n, qi_r, kj_r: (hg, qi_r[n], 0)
    qrow = lambda hg, n, qi_r, kj_r: (hg, 0, qi_r[n])
    krow = lambda hg, n, qi_r, kj_r: (hg, 0, kj_r[n])
    whole = lambda hg, n, qi_r, kj_r: (hg, 0, 0, 0)
    in_specs = [pl.BlockSpec((g, t, DA), qmap), pl.BlockSpec((g, t, DA), kmap), pl.BlockSpec((g, t, DA), kmap),
                pl.BlockSpec((g, t, DA), qmap), pl.BlockSpec((g, 1, t), qrow), pl.BlockSpec((g, 1, t), qrow)]
    out_specs = [pl.BlockSpec((g, nb, t, DA), whole), pl.BlockSpec((g, t, DA), kmap), pl.BlockSpec((g, t, DA), kmap)]
    out_shape = [jax.ShapeDtypeStruct((h, nb, t, DA), F32), jax.ShapeDtypeStruct((h, s, DA), F32), jax.ShapeDtypeStruct((h, s, DA), F32)]
    scratch = [pltpu.VMEM((g, nb, t, DA), F32), pltpu.VMEM((g, t, DA), F32), pltpu.VMEM((g, t, DA), F32)]
    if decay:
        out_specs += [pl.BlockSpec((g, nb, 1, t), whole), pl.BlockSpec((g, 1, t), krow)]
        out_shape += [jax.ShapeDtypeStruct((h, nb, 1, t), F32), jax.ShapeDtypeStruct((h, 1, s), F32)]
        scratch += [pltpu.VMEM((g, nb, 1, t), F32), pltpu.VMEM((g, t, 1), F32)]
    scratch += [pltpu.VMEM((g, t, t), BF16), pltpu.VMEM((g, t, t), BF16)]
    outs = pl.pallas_call(
        body, name=name,
        grid_spec=pltpu.PrefetchScalarGridSpec(num_scalar_prefetch=2, grid=(h // g, nsteps), in_specs=in_specs, out_specs=out_specs,
                                               scratch_shapes=scratch),
        out_shape=out_shape, compiler_params=_cp(("parallel", "arbitrary")),
    )(qi, kj, qa, ka, va, doa, lse_row, delta_row)
    outs = list(outs)
    outs[0] = outs[0].reshape(h, s, DA)
    if decay:
        outs[3] = outs[3].reshape(h, 1, s)
    return outs


def _sel(rows, cols, pairs, value=1.0):
    m = np.zeros((rows, cols), np.float32)
    for r, c in pairs:
        m[r, c] = value
    return jnp.asarray(m, BF16)


def _lane_row(lanes):
    m = np.zeros((1, DA), np.float32)
    m[0, list(lanes)] = 1.0
    return jnp.asarray(m)


def _rms(xv, gain):
    r = lax.rsqrt(jnp.mean(xv * xv, axis=-1, keepdims=True) + EPS)
    return xv * r * gain


def _mla_q_prep(z, gain, wq_a, wq_b, cq, sq, *, name, tm=512):
    s = z.shape[0]
    tm = min(tm, s)

    def body(z_ref, g_ref, wa_ref, wb_ref, c_ref, s_ref, qa_ref, qn_ref):
        qn = _rms(z_ref[...], g_ref[...]).astype(BF16)
        qn_ref[...] = qn
        c, sn = c_ref[...], s_ref[...]
        for hh in range(H):
            cols = slice(hh * DA, (hh + 1) * DA)
            qa_ref[hh] = (_dot(qn, wa_ref[:, cols], NN) * c + _dot(qn, wb_ref[:, cols], NN) * sn).astype(BF16)

    row = lambda i: (i, 0)
    fixed = lambda i: (0, 0)
    return pl.pallas_call(
        body, name=name, grid=(s // tm,),
        in_specs=[pl.BlockSpec((tm, Q_RANK), lambda i: (i, Z_QA // Q_RANK)), pl.BlockSpec((1, Q_RANK), fixed),
                  pl.BlockSpec((Q_RANK, H * DA), fixed), pl.BlockSpec((Q_RANK, H * DA), fixed),
                  pl.BlockSpec((tm, DA), row), pl.BlockSpec((tm, DA), row)],
        out_specs=[pl.BlockSpec((H, tm, DA), lambda i: (0, i, 0)), pl.BlockSpec((tm, Q_RANK), row)],
        out_shape=[jax.ShapeDtypeStruct((H, s, DA), BF16), jax.ShapeDtypeStruct((s, Q_RANK), BF16)],
        compiler_params=_cp(("parallel",)),
    )(z, gain.reshape(1, Q_RANK), wq_a, wq_b, cq, sq)


def _mla_kv_prep(z, gain, wk, wv, ck, sk, *, name, tm=512):
    s = z.shape[0]
    tm = min(tm, s)
    one = _lane_row([VDIM])

    def body(zkv_ref, z3_ref, z15_ref, g_ref, wk_ref, wv_ref, c_ref, s_ref, one_ref, ka_ref, va_ref, kvn_ref):
        kvn = _rms(zkv_ref[...], g_ref[...]).astype(BF16)
        kvn_ref[...] = kvn
        kpe = z3_ref[...] * c_ref[...] + z15_ref[...] * s_ref[...]
        for hh in range(H):
            cols = slice(hh * DA, (hh + 1) * DA)
            ka_ref[hh] = (_dot(kvn, wk_ref[:, cols], NN) + kpe).astype(BF16)
            va_ref[hh] = (_dot(kvn, wv_ref[:, cols], NN) + one_ref[...]).astype(BF16)

    row = lambda i: (i, 0)
    fixed = lambda i: (0, 0)
    blk = lambda c: pl.BlockSpec((tm, DA), lambda i: (i, c))
    heads = pl.BlockSpec((H, tm, DA), lambda i: (0, i, 0))
    return pl.pallas_call(
        body, name=name, grid=(s // tm,),
        in_specs=[blk(Z_KVA // DA), blk(Z_KR // DA), blk(Z_F // DA), pl.BlockSpec((1, KV_RANK), fixed),
                  pl.BlockSpec((KV_RANK, H * DA), fixed), pl.BlockSpec((KV_RANK, H * DA), fixed),
                  pl.BlockSpec((tm, DA), row), pl.BlockSpec((tm, DA), row), pl.BlockSpec((1, DA), fixed)],
        out_specs=[heads, heads, pl.BlockSpec((tm, KV_RANK), row)],
        out_shape=[jax.ShapeDtypeStruct((H, s, DA), BF16), jax.ShapeDtypeStruct((H, s, DA), BF16),
                   jax.ShapeDtypeStruct((s, KV_RANK), BF16)],
        compiler_params=_cp(("parallel",)),
    )(z, z, z, gain.reshape(1, KV_RANK), wk, wv, ck, sk, one)


DEC_C = (FOX_D, FOX_D + 1, FOX_D + 2)
DEC_1 = (FOX_D + 3, FOX_D + 4, FOX_D + 5)


def _fox_prep(z, c3t, *, name, tm=512):
    s = z.shape[0]
    tm = min(tm, s)
    w = H * FOX_D
    left = [(r, r) for r in range(FOX_D)]
    right = [(FOX_D + r, r) for r in range(FOX_D)]
    pq = jnp.stack([_sel(DA, DA, left, SCALE_FOX), _sel(DA, DA, right, SCALE_FOX)])
    pk = jnp.stack([_sel(DA, DA, left), _sel(DA, DA, right)])
    pcq = jnp.stack([_sel(32, DA, [(hh + 8 * k, DEC_C[k]) for k in range(3)]) for hh in range(H)])
    pck = jnp.stack([_sel(32, DA, [(hh + 8 * k, DEC_1[k]) for k in range(3)], -1.0) for hh in range(H)])
    rows3 = jnp.concatenate([_lane_row(DEC_1), _lane_row(DEC_C), _lane_row([FOX_D])], axis=0)

    def body(zq_ref, zk_ref, zv_ref, c_ref, pq_ref, pk_ref, pcq_ref, pck_ref, r_ref, qa_ref, ka_ref, va_ref):
        c3 = c_ref[...]
        for pair in range(H // 2):
            lanes = slice(pair * DA, (pair + 1) * DA)
            zq, zk, zv = zq_ref[:, lanes].astype(BF16), zk_ref[:, lanes].astype(BF16), zv_ref[:, lanes].astype(BF16)
            for side in range(2):
                hh = 2 * pair + side
                qa_ref[hh] = (_dot(zq, pq_ref[side], NN) + _dot(c3, pcq_ref[hh], TN) + r_ref[0:1, :]).astype(BF16)
                ka_ref[hh] = (_dot(zk, pk_ref[side], NN) + _dot(c3, pck_ref[hh], TN) + r_ref[1:2, :]).astype(BF16)
                va_ref[hh] = (_dot(zv, pk_ref[side], NN) + r_ref[2:3, :]).astype(BF16)

    fixed2 = lambda i: (0, 0)
    fixed3 = lambda i: (0, 0, 0)
    heads = pl.BlockSpec((H, tm, DA), lambda i: (0, i, 0))
    zblk = lambda c: pl.BlockSpec((tm, w), lambda i: (i, c))
    return pl.pallas_call(
        body, name=name, grid=(s // tm,),
        in_specs=[zblk(Z_FOX // w), zblk(Z_FOX // w + 1), zblk(Z_FOX // w + 2), pl.BlockSpec((32, tm), lambda i: (0, i)),
                  pl.BlockSpec((2, DA, DA), fixed3), pl.BlockSpec((2, DA, DA), fixed3),
                  pl.BlockSpec((H, 32, DA), fixed3), pl.BlockSpec((H, 32, DA), fixed3), pl.BlockSpec((3, DA), fixed2)],
        out_specs=[heads, heads, heads], out_shape=[jax.ShapeDtypeStruct((H, s, DA), BF16)] * 3,
        compiler_params=_cp(("parallel",)),
    )(z, z, z, c3t, pq, pk, pcq, pck, rows3)


def _mix_out(oa, yb, oc, w_out, x1, *, name, tm=512):
    s = yb.shape[0]
    tm = min(tm, s)
    e2 = jnp.stack([_sel(VDIM, DA, [(r, r) for r in range(VDIM)]), _sel(VDIM, DA, [(r, VDIM + r) for r in range(VDIM)])])

    def body(oa_ref, yb_ref, oc_ref, e_ref, w_ref, x_ref, x2_ref, cat_ref):
        def pairs(o_ref):
            return [(_dot(o_ref[2 * p].astype(BF16), e_ref[0], NN) + _dot(o_ref[2 * p + 1].astype(BF16), e_ref[1], NN)).astype(BF16)
                    for p in range(H // 2)]

        cat = jnp.concatenate(pairs(oa_ref) + [yb_ref[...].astype(BF16)] + pairs(oc_ref), axis=1)
        cat_ref[...] = cat
        x2_ref[...] = x_ref[...] + _dot(cat, w_ref[...], NN)

    row = lambda i: (i, 0)
    heads = pl.BlockSpec((H, tm, VDIM), lambda i: (0, i, 0))
    return pl.pallas_call(
        body, name=name, grid=(s // tm,),
        in_specs=[heads, pl.BlockSpec((tm, POOL_W), row), heads, pl.BlockSpec((2, VDIM, DA), lambda i: (0, 0, 0)),
                  pl.BlockSpec((D, D), lambda i: (0, 0)), pl.BlockSpec((tm, D), row)],
        out_specs=[pl.BlockSpec((tm, D), row), pl.BlockSpec((tm, D), row)],
        out_shape=[jax.ShapeDtypeStruct((s, D), F32), jax.ShapeDtypeStruct((s, D), BF16)],
        compiler_params=_cp(("parallel",)),
    )(oa, yb, oc, e2, w_out, x1)


def _mix_out_bwd(dx2b, w_out, oa, oc, *, name, tm=512):
    s = dx2b.shape[0]
    tm = min(tm, s)
    f2 = jnp.stack([_sel(DA, DA, [(r, r) for r in range(VDIM)]), _sel(DA, DA, [(VDIM + r, r) for r in range(VDIM)])])
    nv = H * VDIM

    def body(dx_ref, w_ref, oa_ref, oc_ref, f_ref, doa_ref, doc_ref, dyb_ref, dla_ref, dlc_ref):
        dcat = _dot(dx_ref[...], w_ref[...], NT)
        dyb_ref[...] = dcat[:, nv:nv + POOL_W]
        for base, o_ref, do_ref, dl_ref in ((0, oa_ref, doa_ref, dla_ref), (nv + POOL_W, oc_ref, doc_ref, dlc_ref)):
            for p in range(H // 2):
                blk = dcat[:, base + p * DA:base + (p + 1) * DA].astype(BF16)
                for side in range(2):
                    hh = 2 * p + side
                    do = _dot(blk, f_ref[side], NN)
                    do_ref[hh] = do.astype(BF16)
                    dl_ref[hh] = _col_to_row(jnp.sum(do[:, :VDIM] * o_ref[hh], axis=-1, keepdims=True))

    row = lambda i: (i, 0)
    heads = lambda w: pl.BlockSpec((H, tm, w), lambda i: (0, i, 0))
    return pl.pallas_call(
        body, name=name, grid=(s // tm,),
        in_specs=[pl.BlockSpec((tm, D), row), pl.BlockSpec((D, D), lambda i: (0, 0)), heads(VDIM), heads(VDIM),
                  pl.BlockSpec((2, DA, DA), lambda i: (0, 0, 0))],
        out_specs=[heads(DA), heads(DA), pl.BlockSpec((tm, POOL_W), row),
                   pl.BlockSpec((H, 1, tm), lambda i: (0, 0, i)), pl.BlockSpec((H, 1, tm), lambda i: (0, 0, i))],
        out_shape=[jax.ShapeDtypeStruct((H, s, DA), BF16), jax.ShapeDtypeStruct((H, s, DA), BF16),
                   jax.ShapeDtypeStruct((s, POOL_W), F32), jax.ShapeDtypeStruct((H, 1, s), F32), jax.ShapeDtypeStruct((H, 1, s), F32)],
        compiler_params=_cp(("parallel",)),
    )(dx2b, w_out, oa, oc, f2)


def _mla_bwd_prep(dqa, dka, dva, dft, cq, sq, ck, sk, *, name, tm=512):
    s = dqa.shape[1]
    tm = min(tm, s)
    keep = _lane_row(range(NOPE))

    def body(dq_ref, dk_ref, dv_ref, dft_ref, cq_ref, sq_ref, ck_ref, sk_ref, keep_ref, dqab_ref, dkv_ref, dz3_ref, dz15_ref):
        cqv, sqv = cq_ref[...], sq_ref[...]
        dkpe = jnp.zeros((tm, DA), F32)
        for hh in range(H):
            lanes = slice(hh * DA, (hh + 1) * DA)
            dq = dq_ref[hh]
            dqab_ref[:, lanes] = (dq * cqv).astype(BF16)
            dqab_ref[:, H * DA + hh * DA:H * DA + (hh + 1) * DA] = (dq * sqv).astype(BF16)
            dk = dk_ref[hh]
            dkpe = dkpe + dk
            dkv_ref[:, lanes] = (dk * keep_ref[...]).astype(BF16)
            dkv_ref[:, H * DA + hh * DA:H * DA + (hh + 1) * DA] = (dv_ref[hh] * keep_ref[...]).astype(BF16)
        dz3_ref[...] = (dkpe * ck_ref[...]).astype(BF16)
        dz15_ref[...] = (dkpe * sk_ref[...] + dft_ref[...]).astype(BF16)

    row = lambda i: (i, 0)
    heads = pl.BlockSpec((H, tm, DA), lambda i: (0, i, 0))
    tab = pl.BlockSpec((tm, DA), row)
    return pl.pallas_call(
        body, name=name, grid=(s // tm,),
        in_specs=[heads, heads, heads, tab, tab, tab, tab, tab, pl.BlockSpec((1, DA), lambda i: (0, 0))],
        out_specs=[pl.BlockSpec((tm, 2 * H * DA), row), pl.BlockSpec((tm, 2 * H * DA), row), tab, tab],
        out_shape=[jax.ShapeDtypeStruct((s, 2 * H * DA), BF16), jax.ShapeDtypeStruct((s, 2 * H * DA), BF16),
                   jax.ShapeDtypeStruct((s, DA), BF16), jax.ShapeDtypeStruct((s, DA), BF16)],
        compiler_params=_cp(("parallel",)),
    )(dqa, dka, dva, dft, cq, sq, ck, sk, keep)


def _fox_bwd_prep(dfqa, dfka, dfva, *, name, tm=512):
    s = dfqa.shape[1]
    tm = min(tm, s)
    place = lambda v: jnp.stack([_sel(DA, DA, [(r, r) for r in range(FOX_D)], v), _sel(DA, DA, [(r, FOX_D + r) for r in range(FOX_D)], v)])
    gq, gk = place(SCALE_FOX), place(1.0)

    def body(dq_ref, dk_ref, dv_ref, gq_ref, gk_ref, dz_ref):
        for part, (d_ref, g_ref) in enumerate(((dq_ref, gq_ref), (dk_ref, gk_ref), (dv_ref, gk_ref))):
            for p in range(H // 2):
                blk = _dot(d_ref[2 * p].astype(BF16), g_ref[0], NN) + _dot(d_ref[2 * p + 1].astype(BF16), g_ref[1], NN)
                lo = part * H * FOX_D + p * DA
                dz_ref[:, lo:lo + DA] = blk.astype(BF16)

    heads = pl.BlockSpec((H, tm, DA), lambda i: (0, i, 0))
    sel = pl.BlockSpec((2, DA, DA), lambda i: (0, 0, 0))
    return pl.pallas_call(
        body, name=name, grid=(s // tm,), in_specs=[heads, heads, heads, sel, sel],
        out_specs=pl.BlockSpec((tm, 3 * H * FOX_D), lambda i: (i, 0)),
        out_shape=jax.ShapeDtypeStruct((s, 3 * H * FOX_D), BF16), compiler_params=_cp(("parallel",)),
    )(dfqa, dfka, dfva, gq, gk)


def _lane_scan(x, s, reverse):
    lane = lax.broadcasted_iota(jnp.int32, x.shape, 1)
    sh = 1
    while sh < s:
        if reverse:
            x = x + jnp.where(lane < s - sh, pltpu.roll(x, s - sh, axis=1), 0.0)
        else:
            x = x + jnp.where(lane >= sh, pltpu.roll(x, sh, axis=1), 0.0)
        sh *= 2
    return x


def _gate_fwd(z, col_block, bias, *, name):
    s = z.shape[0]

    def body(z_ref, b_ref, f_ref, c_ref):
        ft = z_ref[...].T[0:8, :]
        f_ref[...] = ft
        xg = ft + b_ref[...]
        lf = jnp.minimum(xg, 0.0) - jnp.log(1.0 + jnp.exp(-jnp.abs(xg)))
        c = _lane_scan(lf, s, False)
        hi = c.astype(BF16).astype(F32)
        r = c - hi
        mid = r.astype(BF16).astype(F32)
        lo = r - mid
        c_ref[...] = jnp.concatenate([hi, mid, lo, jnp.zeros_like(hi)], axis=0).astype(BF16)

    return pl.pallas_call(
        body, name=name, grid=(1,),
        in_specs=[pl.BlockSpec((s, 128), lambda i: (0, col_block)), pl.BlockSpec((8, 1), lambda i: (0, 0))],
        out_specs=[pl.BlockSpec((8, s), lambda i: (0, 0)), pl.BlockSpec((32, s), lambda i: (0, 0))],
        out_shape=[jax.ShapeDtypeStruct((8, s), F32), jax.ShapeDtypeStruct((32, s), BF16)],
        compiler_params=_cp(("arbitrary",)))(z, bias)


def _gate_bwd(ft, bias, dc, *, name):
    s = ft.shape[1]

    def body(f_ref, b_ref, dc_ref, df_ref, db_ref):
        xg = f_ref[...] + b_ref[...]
        dlf = _lane_scan(dc_ref[...], s, True)
        df = dlf * _sigmoid(-xg)
        db_ref[...] = jnp.sum(df, axis=-1, keepdims=True)
        df_ref[...] = jnp.concatenate([df, jnp.zeros((DA - 8, s), F32)], axis=0).T

    return pl.pallas_call(body, name=name, out_shape=[jax.ShapeDtypeStruct((s, DA), F32), jax.ShapeDtypeStruct((8, 1), F32)],
                          compiler_params=_cp())(ft, bias, dc)


def _pool_lane_consts(tm, i):
    lane = lax.broadcasted_iota(jnp.int32, (tm, POOL_W), 1)
    tok = lax.broadcasted_iota(jnp.int32, (tm, POOL_W), 0) + i * tm
    win = jnp.where(lane < 64, 2, jnp.where(lane < 128, 4, jnp.where(lane < 192, 8, 16)))
    cnt = jnp.minimum(tok + 1, win).astype(F32)
    return lane, tok, cnt


def _pick_window(lane, s2, s4, s8, s16):
    return jnp.where(lane < 64, s2, jnp.where(lane < 128, s4, jnp.where(lane < 192, s8, s16)))


def _pool_fwd(z, col_block, bd, scale, *, name, tm=512):
    s = z.shape[0]
    tm = min(tm, s)
    hb = tm // POOL_HALO

    def body(u_ref, halo_ref, bd_ref, sc_ref, y_ref, p_ref, buf):
        i = pl.program_id(0)
        buf[0:POOL_HALO, :] = halo_ref[...] * (i > 0).astype(F32)
        buf[POOL_HALO:, :] = u_ref[...]

        def back(k):
            return buf[POOL_HALO - k:POOL_HALO - k + tm, :]

        u = u_ref[...]
        s2 = u + back(1)
        s4 = s2 + back(2) + back(3)
        s8 = s4 + back(4) + back(5) + back(6) + back(7)
        s16 = s8
        for k in range(8, 16):
            s16 = s16 + back(k)
        lane, _, cnt = _pool_lane_consts(tm, i)
        pooled = (_pick_window(lane, s2, s4, s8, s16) / cnt - u).astype(BF16)
        p_ref[...] = pooled
        y_ref[...] = _dot(pooled, bd_ref[...], NN) * sc_ref[...]

    return pl.pallas_call(
        body, name=name, grid=(s // tm,),
        in_specs=[pl.BlockSpec((tm, POOL_W), lambda i: (i, col_block)),
                  pl.BlockSpec((POOL_HALO, POOL_W), lambda i: (jnp.maximum(i * hb - 1, 0), col_block)),
                  pl.BlockSpec((POOL_W, POOL_W), lambda i: (0, 0)), pl.BlockSpec((1, POOL_W), lambda i: (0, 0))],
        out_specs=[pl.BlockSpec((tm, POOL_W), lambda i: (i, 0)), pl.BlockSpec((tm, POOL_W), lambda i: (i, 0))],
        out_shape=[jax.ShapeDtypeStruct((s, POOL_W), F32), jax.ShapeDtypeStruct((s, POOL_W), BF16)],
        scratch_shapes=[pltpu.VMEM((tm + POOL_HALO, POOL_W), F32)],
        compiler_params=_cp(("parallel",)),
    )(z, z, bd, scale.reshape(1, POOL_W))


def _pool_bwd_a(dy, pooled, bd, scale, *, name, tm=512):
    s = dy.shape[0]
    tm = min(tm, s)

    def body(dy_ref, p_ref, bd_ref, sc_ref, dq_ref, dys_ref, dsc_ref):
        i = pl.program_id(0)
        dyv = dy_ref[...]
        y0 = _dot(p_ref[...], bd_ref[...], NN)
        dys = (dyv * sc_ref[...]).astype(BF16)
        dys_ref[...] = dys
        dp = _dot(dys, bd_ref[...], NT)
        _, _, cnt = _pool_lane_consts(tm, i)
        dq_ref[:, 0:POOL_W] = dp / cnt
        dq_ref[:, POOL_W:] = dp

        @pl.when(i == 0)
        def _():
            dsc_ref[...] = jnp.zeros_like(dsc_ref)

        dsc_ref[...] += jnp.sum(dyv * y0, axis=0, keepdims=True)

    row = lambda i: (i, 0)
    dq, dys, dsc = pl.pallas_call(
        body, name=name, grid=(s // tm,),
        in_specs=[pl.BlockSpec((tm, POOL_W), row), pl.BlockSpec((tm, POOL_W), row),
                  pl.BlockSpec((POOL_W, POOL_W), lambda i: (0, 0)), pl.BlockSpec((1, POOL_W), lambda i: (0, 0))],
        out_specs=[pl.BlockSpec((tm, 2 * POOL_W), row), pl.BlockSpec((tm, POOL_W), row), pl.BlockSpec((1, POOL_W), lambda i: (0, 0))],
        out_shape=[jax.ShapeDtypeStruct((s, 2 * POOL_W), F32), jax.ShapeDtypeStruct((s, POOL_W), BF16),
                   jax.ShapeDtypeStruct((1, POOL_W), F32)],
        compiler_params=_cp(("arbitrary",)),
    )(dy, pooled, bd, scale.reshape(1, POOL_W))
    return dq, dys, dsc.reshape(POOL_W)


def _pool_bwd_b(dq, *, name, tm=512):
    s = dq.shape[0]
    tm = min(tm, s)
    hb = tm // POOL_HALO
    nblk = s // tm

    def body(q_ref, dp_ref, halo_ref, du_ref, buf):
        i = pl.program_id(0)
        buf[0:tm, :] = q_ref[...]
        buf[tm:, :] = halo_ref[...] * (i < nblk - 1).astype(F32)

        def ahead(k):
            return buf[k:k + tm, :]

        q = q_ref[...]
        s2 = q + ahead(1)
        s4 = s2 + ahead(2) + ahead(3)
        s8 = s4 + ahead(4) + ahead(5) + ahead(6) + ahead(7)
        s16 = s8
        for k in range(8, 16):
            s16 = s16 + ahead(k)
        lane = lax.broadcasted_iota(jnp.int32, (tm, POOL_W), 1)
        du_ref[...] = _pick_window(lane, s2, s4, s8, s16) - dp_ref[...]

    return pl.pallas_call(
        body, name=name, grid=(nblk,),
        in_specs=[pl.BlockSpec((tm, POOL_W), lambda i: (i, 0)), pl.BlockSpec((tm, POOL_W), lambda i: (i, 1)),
                  pl.BlockSpec((POOL_HALO, POOL_W), lambda i: (jnp.minimum((i + 1) * hb, nblk * hb - 1), 0))],
        out_specs=pl.BlockSpec((tm, POOL_W), lambda i: (i, 0)),
        out_shape=jax.ShapeDtypeStruct((s, POOL_W), F32),
        scratch_shapes=[pltpu.VMEM((tm + POOL_HALO, POOL_W), F32)],
        compiler_params=_cp(("parallel",)),
    )(dq, dq, dq)


def _loss_head(x, gain, target, *, name, tm=512):
    s = x.shape[0]
    tm = min(tm, s)

    def body(x_ref, g_ref, t_ref, dx_ref, dg_ref, loss_ref):
        xv = x_ref[...]
        r = lax.rsqrt(jnp.mean(xv * xv, axis=-1, keepdims=True) + EPS)
        xh = xv * r
        err = xh * g_ref[...] - t_ref[...]
        dy = err * (1.0 / D)
        a = dy * g_ref[...]
        dx_ref[...] = r * a - xh * (r * jnp.mean(a * xh, axis=-1, keepdims=True))

        @pl.when(pl.program_id(0) == 0)
        def _():
            dg_ref[...] = jnp.zeros_like(dg_ref)
            loss_ref[...] = jnp.zeros_like(loss_ref)

        dg_ref[...] += jnp.sum(dy * xh, axis=0, keepdims=True)
        part = 0.5 * jnp.sum(jnp.mean(err * err, axis=-1, keepdims=True), axis=0, keepdims=True)
        loss_ref[...] += jnp.broadcast_to(part, loss_ref.shape)

    row = lambda i: (i, 0)
    dx, dg, loss = pl.pallas_call(
        body, name=name, grid=(s // tm,),
        in_specs=[pl.BlockSpec((tm, D), row), pl.BlockSpec((1, D), lambda i: (0, 0)), pl.BlockSpec((tm, D), row)],
        out_specs=[pl.BlockSpec((tm, D), row), pl.BlockSpec((1, D), lambda i: (0, 0)), pl.BlockSpec((1, 128), lambda i: (0, 0))],
        out_shape=[jax.ShapeDtypeStruct((s, D), F32), jax.ShapeDtypeStruct((1, D), F32), jax.ShapeDtypeStruct((1, 128), F32)],
        compiler_params=_cp(("arbitrary",)),
    )(x, gain.reshape(1, D), target)
    return dx, dg.reshape(D), loss[0, 0]


def _adamw(w, g, m, v, *, name, tr=512):
    rows, cols = w.shape
    tr = min(tr, rows)
    assert rows % tr == 0, (name, rows, tr)
    c_m = 1.0 - ADAM_B1
    c_v = 1.0 - ADAM_B2
    bc1 = 1.0 - ADAM_B1 ** ADAM_STEP
    bc2 = 1.0 - ADAM_B2 ** ADAM_STEP

    def body(w_ref, g_ref, m_ref, v_ref, d_ref, mo_ref, vo_ref):
        gv = g_ref[...]
        mn = ADAM_B1 * m_ref[...] + c_m * gv
        vn = ADAM_B2 * v_ref[...] + c_v * (gv * gv)
        mo_ref[...] = mn
        vo_ref[...] = vn
        d_ref[...] = -ADAM_LR * ((mn / bc1) / (jnp.sqrt(vn / bc2) + ADAM_EPS) + ADAM_WD * w_ref[...])

    spec = pl.BlockSpec((tr, cols), lambda i: (i, 0))
    return pl.pallas_call(body, name=name, grid=(rows // tr,), in_specs=[spec] * 4, out_specs=[spec] * 3,
                          out_shape=[jax.ShapeDtypeStruct((rows, cols), F32)] * 3,
                          compiler_params=_cp(("parallel",)))(w, g, m, v)


def _adamw_layer(w, g, m, v, layer, prev, *, name, tr):
    rows, cols = g.shape
    assert rows % tr == 0 and w.shape == (DEPTH * rows, cols), (name, w.shape, g.shape, tr)
    nblk = rows // tr
    c_m = 1.0 - ADAM_B1
    c_v = 1.0 - ADAM_B2
    bc1 = 1.0 - ADAM_B1 ** ADAM_STEP
    bc2 = 1.0 - ADAM_B2 ** ADAM_STEP
    n_prev = 0 if prev is None else 4

    def body(*refs):
        w_ref, g_ref, m_ref, v_ref = refs[:4]
        d_ref, mo_ref, vo_ref, go_ref = refs[4 + n_prev:]
        gv = g_ref[...]
        mn = ADAM_B1 * m_ref[...] + c_m * gv
        vn = ADAM_B2 * v_ref[...] + c_v * (gv * gv)
        mo_ref[...] = mn
        vo_ref[...] = vn
        go_ref[...] = gv
        d_ref[...] = -ADAM_LR * ((mn / bc1) / (jnp.sqrt(vn / bc2) + ADAM_EPS) + ADAM_WD * w_ref[...])

    stacked = pl.BlockSpec((tr, cols), lambda i: (layer * nblk + i, 0))
    args = [w, g, m, v] + ([] if prev is None else list(prev))
    return pl.pallas_call(
        body, name=name, grid=(nblk,),
        in_specs=[stacked, pl.BlockSpec((tr, cols), lambda i: (i, 0)), stacked, stacked] + [ANY_SPEC] * n_prev,
        out_specs=[stacked] * 4, out_shape=[jax.ShapeDtypeStruct(w.shape, F32)] * 4,
        input_output_aliases={4 + k: k for k in range(n_prev)},
        compiler_params=_cp(("parallel",)))(*args)


def _position():
    return jnp.stack([lax.axis_index("c"), 2 * lax.axis_index("x") + lax.axis_index("y")]).astype(jnp.int32)


SUM_ROW_TILES = 2


def _sum2_bf16(pos, fulls, sibs, *, name):
    n = len(fulls)
    nb = SUM_ROW_TILES

    def body(pos_ref, *refs):
        for t in range(n):
            refs[2 * n + t][...] = (refs[t][...] + refs[n + t][...]).astype(BF16)

    in_specs, sib_specs = [], []
    for sb in sibs:
        _, half, cols = sb.shape
        tr = half // nb
        assert half % nb == 0 and tr % 16 == 0, sb.shape
        in_specs.append(pl.BlockSpec((None, tr, cols), lambda j, i, p: (j, p[0] * nb + i, 0)))
        sib_specs.append(pl.BlockSpec((None, tr, cols), lambda j, i, p: (j, i, 0)))
    return pl.pallas_call(
        body, name=name,
        grid_spec=pltpu.PrefetchScalarGridSpec(num_scalar_prefetch=1, grid=(N_CHIPS, nb), in_specs=in_specs + sib_specs,
                                               out_specs=sib_specs),
        out_shape=[jax.ShapeDtypeStruct(sb.shape, BF16) for sb in sibs],
        compiler_params=_cp(("parallel", "parallel")))(pos, *fulls, *sibs)


def _sum5(pos, fulls, sibs, recvs, *, name):
    n = len(fulls)
    nb = SUM_ROW_TILES

    def body(pos_ref, *refs):
        for t in range(n):
            acc = refs[t][...] + refs[n + t][...]
            for kk in range(3):
                acc = acc + refs[2 * n + t][kk].astype(F32)
            refs[3 * n + t][...] = acc

    f_specs, s_specs, r_specs, o_specs = [], [], [], []
    for f in fulls:
        _, rows, cols = f.shape
        tr = rows // 2 // nb
        f_specs.append(pl.BlockSpec((None, tr, cols), lambda i, p: (p[1], p[0] * nb + i, 0)))
        s_specs.append(pl.BlockSpec((None, tr, cols), lambda i, p: (p[1], i, 0)))
        r_specs.append(pl.BlockSpec((3, tr, cols), lambda i, p: (0, i, 0)))
        o_specs.append(pl.BlockSpec((tr, cols), lambda i, p: (p[0] * nb + i, 0)))
    return pl.pallas_call(
        body, name=name,
        grid_spec=pltpu.PrefetchScalarGridSpec(num_scalar_prefetch=1, grid=(nb,), in_specs=f_specs + s_specs + r_specs,
                                               out_specs=o_specs),
        out_shape=[jax.ShapeDtypeStruct(f.shape[1:], F32) for f in fulls],
        compiler_params=_cp(("parallel",)))(pos, *fulls, *sibs, *recvs)


def _place():
    x, y, c = lax.axis_index("x"), lax.axis_index("y"), lax.axis_index("c")
    chips = [(1 - x, y), (x, 1 - y), (1 - x, 1 - y)]
    return x, y, c, 2 * x + y, chips


SEM_SPEC = pl.BlockSpec(memory_space=pltpu.SEMAPHORE)
ANY_SPEC = pl.BlockSpec(memory_space=pl.ANY)


def _gather_copies(ins, outs, send_i, recv_i, send_o, recv_o):
    x, y, c, me, chips = _place()
    n = len(ins)
    started, awaited = [], []
    for t in range(n):
        half = ins[t].shape[0] // 2
        mine = pl.ds(c * half, half)
        started.append(pltpu.make_async_remote_copy(
            src_ref=ins[t], dst_ref=outs[t].at[me], send_sem=send_o.at[t], recv_sem=recv_o.at[t],
            device_id=(x, y, 1 - c), device_id_type=MESH))
        awaited.append(started[-1])
        for kk, (px, py) in enumerate(chips):
            started.append(pltpu.make_async_remote_copy(
                src_ref=ins[t].at[mine], dst_ref=outs[t].at[me, mine], send_sem=send_i.at[t * 3 + kk],
                recv_sem=recv_i.at[t * 3 + kk], device_id=(px, py, c), device_id_type=MESH))
            awaited.append(pltpu.make_async_remote_copy(
                src_ref=ins[t].at[mine], dst_ref=outs[t].at[2 * px + py, mine], send_sem=send_i.at[t * 3 + kk],
                recv_sem=recv_i.at[t * 3 + kk], device_id=(px, py, c), device_id_type=MESH))
    return started, awaited


def _forward_copies(outs, send_d, recv_d):
    x, y, c, me, chips = _place()
    started, awaited = [], []
    for t in range(len(outs)):
        half = outs[t].shape[1] // 2
        for kk, (px, py) in enumerate(chips):
            for lst, hc in ((started, c), (awaited, 1 - c)):
                blk = outs[t].at[2 * px + py, pl.ds(hc * half, half)]
                lst.append(pltpu.make_async_remote_copy(src_ref=blk, dst_ref=blk, send_sem=send_d.at[t * 3 + kk],
                                                        recv_sem=recv_d.at[t * 3 + kk], device_id=(x, y, 1 - c), device_id_type=MESH))
    return started, awaited


def _gather_blocking(shards):
    n = len(shards)

    def body(*refs):
        ins, outs = refs[:n], refs[n:2 * n]
        send_i, recv_i, send_d, recv_d, send_o, recv_o = refs[2 * n:]
        started, awaited = _gather_copies(ins, outs, send_i, recv_i, send_o, recv_o)
        for cp in started:
            cp.start()
        for cp in awaited:
            cp.wait_recv()
        fwd, fwd_in = _forward_copies(outs, send_d, recv_d)
        for cp in fwd:
            cp.start()
        for cp in fwd_in:
            cp.wait_recv()
        for cp in started + fwd:
            cp.wait_send()

    return pl.pallas_call(
        body, name="gather_first", in_specs=[HBM_SPEC] * n, out_specs=[HBM_SPEC] * n,
        out_shape=[jax.ShapeDtypeStruct((N_CHIPS,) + s.shape, s.dtype) for s in shards],
        scratch_shapes=[pltpu.SemaphoreType.DMA((3 * n,)), pltpu.SemaphoreType.DMA((3 * n,)),
                        pltpu.SemaphoreType.DMA((3 * n,)), pltpu.SemaphoreType.DMA((3 * n,)),
                        pltpu.SemaphoreType.DMA((n,)), pltpu.SemaphoreType.DMA((n,))],
    )(*shards)


def _gather_start(shards, after, tag):
    n = len(shards)

    def body(*refs):
        ins = refs[:n]
        send_i, recv_i, send_o, recv_o = refs[2 * n + 1:2 * n + 5]
        outs = refs[3 * n + 5:4 * n + 5]
        token = refs[4 * n + 5]
        started, _ = _gather_copies(ins, outs, send_i, recv_i, send_o, recv_o)
        for cp in started:
            cp.start()
        token[...] = jnp.zeros_like(token)

    lands = [lax.empty((N_CHIPS,) + s.shape, s.dtype) for s in shards]
    sems = [pltpu.SemaphoreType.DMA((3 * n,)), pltpu.SemaphoreType.DMA((3 * n,)), pltpu.SemaphoreType.DMA((n,)), pltpu.SemaphoreType.DMA((n,))]
    res = pl.pallas_call(
        body, name=f"gather_{tag}_start",
        in_specs=[HBM_SPEC] * (2 * n) + [ANY_SPEC],
        out_specs=[SEM_SPEC] * 4 + [HBM_SPEC] * (2 * n) + [pl.BlockSpec(memory_space=pltpu.VMEM)],
        out_shape=sems + [jax.ShapeDtypeStruct(s.shape, s.dtype) for s in shards]
        + [jax.ShapeDtypeStruct(a.shape, a.dtype) for a in lands] + [jax.ShapeDtypeStruct((8, 128), F32)],
        input_output_aliases={t: 4 + t for t in range(2 * n)},
        compiler_params=pltpu.CompilerParams(has_side_effects=pltpu.SideEffectType.DATAFLOW_SIDE_EFFECTING),
    )(*[pltpu.with_memory_space_constraint(s, pltpu.HBM) for s in shards],
      *[pltpu.with_memory_space_constraint(a, pltpu.HBM) for a in lands], after)
    return res[:4], res[4:4 + n], res[4 + n:4 + 2 * n], res[-1]


def _gather_wait(sems, shards_thru, lands_thru, after, tag):
    n = len(shards_thru)

    def body(*refs):
        ins, outs_in = refs[:n], refs[n:2 * n]
        send_i, recv_i, send_o, recv_o = refs[2 * n:2 * n + 4]
        started, awaited = _gather_copies(ins, outs_in, send_i, recv_i, send_o, recv_o)
        for cp in started:
            cp.wait_send()
        for cp in awaited:
            cp.wait_recv()

    res = pl.pallas_call(
        body, name=f"gather_{tag}_wait",
        in_specs=[HBM_SPEC] * (2 * n) + [SEM_SPEC] * 4 + [ANY_SPEC],
        out_specs=[HBM_SPEC] * (2 * n),
        out_shape=[jax.ShapeDtypeStruct(a.shape, a.dtype) for a in list(shards_thru) + list(lands_thru)],
        input_output_aliases={t: t for t in range(2 * n)},
        compiler_params=pltpu.CompilerParams(has_side_effects=pltpu.SideEffectType.DATAFLOW_SIDE_EFFECTING),
    )(*shards_thru, *lands_thru, *sems, after)
    return res[n:]


def _gather_forward(lands, tag):
    n = len(lands)

    def body(*refs):
        outs = refs[n:2 * n]
        send_d, recv_d = refs[2 * n:]
        fwd, fwd_in = _forward_copies(outs, send_d, recv_d)
        for cp in fwd:
            cp.start()
        for cp in fwd_in:
            cp.wait_recv()
        for cp in fwd:
            cp.wait_send()

    return pl.pallas_call(
        body, name=f"gather_{tag}_forward", in_specs=[HBM_SPEC] * n, out_specs=[HBM_SPEC] * n,
        out_shape=[jax.ShapeDtypeStruct(a.shape, a.dtype) for a in lands],
        input_output_aliases={t: t for t in range(n)},
        scratch_shapes=[pltpu.SemaphoreType.DMA((3 * n,)), pltpu.SemaphoreType.DMA((3 * n,))],
    )(*lands)


def _stage1_copies(ins, sib, send, recv):
    x, y, c, me, chips = _place()
    cps = []
    for t in range(len(ins)):
        rows = ins[t].shape[1] // 2
        cps.append(pltpu.make_async_remote_copy(
            src_ref=ins[t].at[:, pl.ds((1 - c) * rows, rows), :], dst_ref=sib[t], send_sem=send.at[t],
            recv_sem=recv.at[t], device_id=(x, y, 1 - c), device_id_type=MESH))
    return cps


def _split_start(copies_fn, srcs, land_shapes, n_sems, tag):
    n = len(srcs)

    def body(*refs):
        send, recv = refs[2 * n:2 * n + 2]
        for cp in copies_fn(refs[:n], refs[3 * n + 2:4 * n + 2], send, recv):
            cp.start()
        refs[4 * n + 2][...] = jnp.zeros_like(refs[4 * n + 2])

    lands = [lax.empty(shp, dt) for shp, dt in land_shapes]
    res = pl.pallas_call(
        body, name=tag,
        in_specs=[HBM_SPEC] * (2 * n),
        out_specs=[SEM_SPEC] * 2 + [HBM_SPEC] * (2 * n) + [pl.BlockSpec(memory_space=pltpu.VMEM)],
        out_shape=[pltpu.SemaphoreType.DMA((n_sems,)), pltpu.SemaphoreType.DMA((n_sems,))]
        + [jax.ShapeDtypeStruct(p.shape, p.dtype) for p in srcs]
        + [jax.ShapeDtypeStruct(a.shape, a.dtype) for a in lands] + [jax.ShapeDtypeStruct((8, 128), F32)],
        input_output_aliases={t: 2 + t for t in range(2 * n)},
        compiler_params=pltpu.CompilerParams(has_side_effects=pltpu.SideEffectType.DATAFLOW_SIDE_EFFECTING),
    )(*[pltpu.with_memory_space_constraint(p, pltpu.HBM) for p in srcs],
      *[pltpu.with_memory_space_constraint(a, pltpu.HBM) for a in lands])
    return res[:2], res[2:2 + n], res[2 + n:2 + 2 * n], res[-1]


def _split_wait(copies_fn, sems, srcs_thru, lands_thru, after, tag):
    n = len(srcs_thru)

    def body(*refs):
        for cp in copies_fn(refs[:n], refs[n:2 * n], refs[2 * n], refs[2 * n + 1]):
            cp.wait()

    res = pl.pallas_call(
        body, name=tag,
        in_specs=[HBM_SPEC] * (2 * n) + [SEM_SPEC] * 2 + [ANY_SPEC],
        out_specs=[HBM_SPEC] * (2 * n),
        out_shape=[jax.ShapeDtypeStruct(a.shape, a.dtype) for a in list(srcs_thru) + list(lands_thru)],
        input_output_aliases={t: t for t in range(2 * n)},
        compiler_params=pltpu.CompilerParams(has_side_effects=pltpu.SideEffectType.DATAFLOW_SIDE_EFFECTING),
    )(*srcs_thru, *lands_thru, *sems, after)
    return res[:n], res[n:]


def _stage2_copies(ps, rcv, send, recv):
    x, y, c, me, chips = _place()
    return [pltpu.make_async_remote_copy(
        src_ref=ps[t].at[2 * px + py], dst_ref=rcv[t].at[kk], send_sem=send.at[t * 3 + kk],
        recv_sem=recv.at[t * 3 + kk], device_id=(px, py, c), device_id_type=MESH)
        for t in range(len(ps)) for kk, (px, py) in enumerate(chips)]


def _reduce_stage3(reduced, tag):
    n = len(reduced)

    def body(*refs):
        outs = refs[n:2 * n]
        send, recv = refs[2 * n:]
        x, y, c, me, chips = _place()
        cps = []
        for t in range(n):
            rows = outs[t].shape[0] // 2
            mine = outs[t].at[pl.ds(c * rows, rows), :]
            cp = pltpu.make_async_remote_copy(src_ref=mine, dst_ref=mine, send_sem=send.at[t], recv_sem=recv.at[t],
                                              device_id=(x, y, 1 - c), device_id_type=MESH)
            cp.start()
            cps.append(cp)
        for cp in cps:
            cp.wait()

    return pl.pallas_call(
        body, name="reduce_stage3_" + tag, in_specs=[HBM_SPEC] * n, out_specs=[HBM_SPEC] * n,
        out_shape=[jax.ShapeDtypeStruct(r.shape, r.dtype) for r in reduced],
        input_output_aliases={t: t for t in range(n)},
        scratch_shapes=[pltpu.SemaphoreType.DMA((n,)), pltpu.SemaphoreType.DMA((n,))],
    )(*reduced)


def _allreduce_small(v):
    rows, cols = v.shape

    def body(v_ref, o_ref, buf, send, recv, loc):
        x, y, c, me, chips = _place()
        mine = 4 * x + 2 * y + c
        lc = pltpu.make_async_copy(v_ref, buf.at[mine], loc)
        lc.start()
        peers = []
        for fx in range(2):
            for fy in range(2):
                for fc in range(2):
                    if fx or fy or fc:
                        peers.append((fx, fy, fc))
        cps = []
        for kk, (fx, fy, fc) in enumerate(peers):
            to = (x ^ fx, y ^ fy, c ^ fc)
            cp = pltpu.make_async_remote_copy(src_ref=v_ref, dst_ref=buf.at[mine], send_sem=send.at[kk], recv_sem=recv.at[kk],
                                              device_id=to, device_id_type=MESH)
            cp.start()
            cps.append((cp, to))
        for kk, (cp, to) in enumerate(cps):
            src = 4 * to[0] + 2 * to[1] + to[2]
            pltpu.make_async_remote_copy(src_ref=v_ref, dst_ref=buf.at[src], send_sem=send.at[kk], recv_sem=recv.at[kk],
                                         device_id=to, device_id_type=MESH).wait_recv()
        for cp, _ in cps:
            cp.wait_send()
        lc.wait()
        acc = buf[0]
        for d in range(1, 8):
            acc = acc + buf[d]
        o_ref[...] = acc

    return pl.pallas_call(
        body, name="allreduce_small", in_specs=[pl.BlockSpec(memory_space=pltpu.VMEM)],
        out_specs=pl.BlockSpec(memory_space=pltpu.VMEM), out_shape=jax.ShapeDtypeStruct((rows, cols), F32),
        scratch_shapes=[pltpu.VMEM((8, rows, cols), F32), pltpu.SemaphoreType.DMA((7,)), pltpu.SemaphoreType.DMA((7,)),
                        pltpu.SemaphoreType.DMA],
        compiler_params=pltpu.CompilerParams(vmem_limit_bytes=VMEM_LIMIT_V7X),
    )(v)


def _pad_w_in(w):
    z = lambda n: jnp.zeros(w.shape[:-1] + (n,), w.dtype)
    return jnp.concatenate([w[..., 0:384], z(64), w[..., 384:416], z(32), w[..., 416:1824],
                            w[..., 1824:1830], z(58), w[..., 400:416], w[..., 384:400], z(32)], axis=-1)


def _unpad_w_in(g):
    x1 = g[..., 448:464] + g[..., Z_F + 80:Z_F + 96]
    x2 = g[..., 464:480] + g[..., Z_F + 64:Z_F + 80]
    return jnp.concatenate([g[..., 0:384], x1, x2, g[..., 512:1920], g[..., 1920:1926]], axis=-1)


def _block_diag(pw):
    out = jnp.zeros((POOL_W, POOL_W), pw.dtype)
    for g in range(4):
        out = out.at[g * 64:(g + 1) * 64, g * 64:(g + 1) * 64].set(pw[g])
    return out


def _rope_tables(s):
    inv_freq = ROPE_THETA ** (-jnp.arange(0, ROPE, 2, dtype=F32) / ROPE)
    ang = jnp.arange(s, dtype=jnp.int32).astype(F32)[:, None] * inv_freq[None, :]
    cos, sin = jnp.cos(ang), jnp.sin(ang)
    zero = lambda n: jnp.zeros((s, n), F32)
    ck = jnp.concatenate([zero(NOPE), cos, cos, zero(DA - NOPE - ROPE)], axis=1)
    sk = jnp.concatenate([zero(NOPE), -sin, sin, zero(DA - NOPE - ROPE)], axis=1)
    cq = jnp.concatenate([jnp.ones((s, NOPE), F32), cos, cos, zero(DA - NOPE - ROPE)], axis=1) * SCALE_MLA
    return dict(cq=cq, sq=sk * SCALE_MLA, ck=ck, sk=sk)


def _mix_fwd(l, x1, wts, sm, tabs):
    z, h2 = _norm_mm(x1, 0, sm["mix_norm"][l], wts["w_in"][l], name=f"mix_in_{l}")
    qa, qn = _mla_q_prep(z, sm["q_a_norm"][l], wts["wq_a"][l], wts["wq_b"][l], tabs["cq"], tabs["sq"], name=f"mla_q_{l}")
    ka, va, kvn = _mla_kv_prep(z, sm["kv_a_norm"][l], wts["wk"][l], wts["wv"][l], tabs["ck"], tabs["sk"], name=f"mla_kv_{l}")
    oa, lse_a = _attn_fwd(qa, ka, va, VDIM, name=f"mla_attn_{l}")

    bd = _block_diag(wts["pool_w"][l]).astype(BF16)
    yb, pooled = _pool_fwd(z, Z_POOL // POOL_W, bd, sm["pool_scale"][l], name=f"pool_{l}")

    fb = jnp.pad(sm["fox_b_f"][l], (0, 8 - H)).reshape(8, 1)
    ft, c3t = _gate_fwd(z, Z_F // DA, fb, name=f"fox_gate_{l}")
    fqa, fka, fva = _fox_prep(z, c3t, name=f"fox_prep_{l}")
    oc, lse_c = _attn_fwd(fqa, fka, fva, FOX_D, name=f"fox_attn_{l}")

    x2, cat = _mix_out(oa, yb, oc, wts["w_out"][l], x1, name=f"mix_out_{l}")
    saved = dict(z=z, h2=h2, qn=qn, kvn=kvn, qa=qa, ka=ka, va=va, oa=oa, lse_a=lse_a, bd=bd, pooled=pooled,
                 fqa=fqa, fka=fka, fva=fva, ft=ft, fb=fb, oc=oc, lse_c=lse_c, cat=cat)
    return x2, saved


def _mix_bwd(l, x1, dx2, sv, wts, sm, tabs, tok=None):
    s = x1.shape[0]
    g = {}
    dx2b = (dx2 if tok is None else dx2 + tok).astype(BF16)
    g["w_out"] = _mm(sv["cat"], dx2b, "tn", name=f"d_w_out_{l}", tm=1024, tn=1024, tk=DW_TOKENS)
    doa, doc, dyb, dl_a, dl_c = _mix_out_bwd(dx2b, wts["w_out"][l], sv["oa"], sv["oc"], name=f"mix_out_bwd_{l}")

    dfqa, dfka, dfva, dcq, dck = _attn_bwd(sv["fqa"], sv["fka"], sv["fva"], doc, sv["lse_c"], dl_c, True, name=f"fox_attn_bwd_{l}")
    dfox = _fox_bwd_prep(dfqa, dfka, dfva, name=f"fox_bwd_prep_{l}")
    dc = jnp.pad(dcq.reshape(H, s) + dck.reshape(H, s), ((0, 8 - H), (0, 0)))
    dft, dfb = _gate_bwd(sv["ft"], sv["fb"], dc, name=f"fox_gate_bwd_{l}")
    g["fox_b_f"] = dfb[:H, 0]

    dq, dys, g["pool_scale"] = _pool_bwd_a(dyb, sv["pooled"], sv["bd"], sm["pool_scale"][l], name=f"pool_bwd_a_{l}")
    du = _pool_bwd_b(dq, name=f"pool_bwd_b_{l}")
    dbd = _mm(sv["pooled"], dys, "tn", name=f"d_pool_w_{l}")
    g["pool_w"] = jnp.stack([dbd[i * 64:(i + 1) * 64, i * 64:(i + 1) * 64] for i in range(4)])

    dqa_, dka_, dva_ = _attn_bwd(sv["qa"], sv["ka"], sv["va"], doa, sv["lse_a"], dl_a, False, name=f"mla_attn_bwd_{l}")
    dqab, dkv, dz3, dz15 = _mla_bwd_prep(dqa_, dka_, dva_, dft, tabs["cq"], tabs["sq"], tabs["ck"], tabs["sk"],
                                         name=f"mla_bwd_prep_{l}")
    wq_ab = jnp.concatenate([wts["wq_a"][l], wts["wq_b"][l]], axis=1)
    wkv = jnp.concatenate([wts["wk"][l], wts["wv"][l]], axis=1)
    dwq = _mm(sv["qn"], dqab, "tn", name=f"d_w_q_b_{l}", tn=768, tk=DW_TOKENS).reshape(Q_RANK, 2, H, DA)
    dwkv = _mm(sv["kvn"], dkv, "tn", name=f"d_w_kv_b_{l}", tn=768, tk=DW_TOKENS).reshape(KV_RANK, 2, H, DA)
    da, db = dwq[:, 0], dwq[:, 1]
    swapped = jnp.concatenate([jnp.zeros((Q_RANK, H, NOPE), F32), db[..., NOPE + HALF_ROPE:NOPE + ROPE],
                               db[..., NOPE:NOPE + HALF_ROPE]], axis=-1)
    g["w_q_b"] = (da[..., :NOPE + ROPE] + swapped).reshape(Q_RANK, H * (NOPE + ROPE))
    g["w_kv_b"] = jnp.concatenate([dwkv[:, 0, :, :NOPE], dwkv[:, 1, :, :VDIM]], axis=-1).reshape(KV_RANK, H * (NOPE + VDIM))
    dqa, g["q_a_norm"] = _rmsnorm_bwd(sv["z"], Z_QA // Q_RANK, sm["q_a_norm"][l], dqab, wq_ab, name=f"q_a_norm_bwd_{l}")
    dkva, g["kv_a_norm"] = _rmsnorm_bwd(sv["z"], Z_KVA // KV_RANK, sm["kv_a_norm"][l], dkv, wkv, name=f"kv_a_norm_bwd_{l}")

    dz = jnp.concatenate([dqa.astype(BF16), dkva.astype(BF16), dz3, du.astype(BF16), dfox, dz15], axis=1)
    g["w_in"] = _mm(sv["h2"], dz, "tn", name=f"d_w_in_{l}", tm=1024, tn=1024, tk=DW_TOKENS)
    dx1, g["mix_norm"] = _rmsnorm_bwd(x1, 0, sm["mix_norm"][l], dz, wts["w_in"][l], dx2, name=f"mix_norm_bwd_{l}")
    return dx1, g


DW_TOKENS = 2048


def _local_step(x, target, wts, sm, late_weights=None, grads_ready=None):
    s = x.shape[0]
    tabs = _rope_tables(s)
    acts = []
    xs = x
    for l in range(DEPTH):
        x1, gu1, act1 = _ffn_fwd(xs, sm["ffn1_norm"][l], wts["ffn1_w_gu"][l], wts["ffn1_w_d2"][l], name=f"ffn1_fwd_{l}")
        if l == 0 and late_weights is not None:
            sm = late_weights("ffn1", x1, sm)
        x2, sv = _mix_fwd(l, x1, wts, sm, tabs)
        if l == 0 and late_weights is not None:
            sm = late_weights("mix", x2, sm)
        x3, gu2, act2 = _ffn_fwd(x2, sm["ffn2_norm"][l], wts["ffn2_w_gu"][l], wts["ffn2_w_d2"][l], name=f"ffn2_fwd_{l}")
        acts.append((xs, gu1, act1, x1, sv, x2, gu2, act2))
        xs = x3
    dx, g_final, loss = _loss_head(xs, sm["final_norm"], target, name="loss_head")
    grads = [dict() for _ in range(DEPTH)]
    for l in reversed(range(DEPTH)):
        x0, gu1, act1, x1, sv, x2, gu2, act2 = acts[l]
        g = grads[l]
        dx, dgu, hh, dy, g["ffn2_norm"] = _ffn_bwd(x2, dx, gu2, sm["ffn2_norm"][l], wts["ffn2_w_gu"][l], wts["ffn2_w_d2"][l],
                                                   name=f"ffn2_bwd_{l}")
        g["ffn2_w_down"] = _mm(act2, dy, "tn", name=f"d_ffn2_w_down_{l}", tm=FF_SHARD, tn=1024, tk=DW_TOKENS)
        g["ffn2_w_gu"] = _mm(hh, dgu, "tn", name=f"d_ffn2_w_gu_{l}", tm=1024, tn=FF_SHARD, tk=DW_TOKENS, n_major_out=True)
        tok = None
        if grads_ready is not None:
            sm, tok = grads_ready(l, "ffn2", g, sm)
        dx, gm = _mix_bwd(l, x1, dx, sv, wts, sm, tabs, tok)
        g.update(gm)
        if grads_ready is not None:
            sm, _ = grads_ready(l, "mix", g, sm)
        dx, dgu, hh, dy, g["ffn1_norm"] = _ffn_bwd(x0, dx, gu1, sm["ffn1_norm"][l], wts["ffn1_w_gu"][l], wts["ffn1_w_d2"][l],
                                                   name=f"ffn1_bwd_{l}")
        if grads_ready is not None:
            sm, tok = grads_ready(l, "ffn1_tokens", {"dx": dx}, sm)
            if tok is not None:
                dy = dy + tok.astype(BF16)
        g["ffn1_w_down"] = _mm(act1, dy, "tn", name=f"d_ffn1_w_down_{l}", tm=FF_SHARD, tn=1024, tk=DW_TOKENS)
        g["ffn1_w_gu"] = _mm(hh, dgu, "tn", name=f"d_ffn1_w_gu_{l}", tm=1024, tn=FF_SHARD, tk=DW_TOKENS, n_major_out=True)
        if grads_ready is not None:
            sm, _ = grads_ready(l, "ffn1", g, sm)
    return loss, dx, grads, g_final


BIG = ["ffn1_w_gu", "ffn1_w_down", "w_in", "w_q_b", "w_kv_b", "w_out", "ffn2_w_gu", "ffn2_w_down"]
SMALL = ["ffn1_norm", "mix_norm", "q_a_norm", "kv_a_norm", "pool_w", "pool_scale", "fox_b_f", "ffn2_norm"]
SMALL_ROWS = 48


WEIGHT_VIEWS = ["ffn1_w_gu", "ffn1_w_d2", "w_in", "wq_a", "wq_b", "wk", "wv", "w_out", "ffn2_w_gu", "ffn2_w_d2"]


def _prepare_weights(gathered, wts):
    for (nm, l), w in gathered.items():
        if nm in ("ffn1_w_gu", "ffn2_w_gu"):
            wts[nm][l] = w
        elif nm in ("ffn1_w_down", "ffn2_w_down"):
            wts[nm[:5] + "w_d2"][l] = w.reshape(2, FF_SHARD, D)
        elif nm in ("w_in", "w_out"):
            wts[nm][l] = w.reshape(D, -1)
        elif nm == "w_q_b":
            wq = jnp.moveaxis(w, 0, 1).reshape(Q_RANK, H, NOPE + ROPE)
            zq = lambda n: jnp.zeros((Q_RANK, H, n), BF16)
            wts["wq_a"][l] = jnp.concatenate([wq, zq(DA - NOPE - ROPE)], axis=-1).reshape(Q_RANK, H * DA)
            wts["wq_b"][l] = jnp.concatenate([zq(NOPE), wq[..., NOPE + HALF_ROPE:], wq[..., NOPE:NOPE + HALF_ROPE],
                                              zq(DA - NOPE - ROPE)], axis=-1).reshape(Q_RANK, H * DA)
        else:
            wkv = jnp.moveaxis(w, 0, 1).reshape(KV_RANK, H, NOPE + VDIM)
            zk = jnp.zeros((KV_RANK, H, DA - NOPE), BF16)
            wts["wk"][l] = jnp.concatenate([wkv[..., :NOPE], zk], axis=-1).reshape(KV_RANK, H * DA)
            wts["wv"][l] = jnp.concatenate([wkv[..., NOPE:], zk], axis=-1).reshape(KV_RANK, H * DA)


def _chip_major(name, g):
    if name in ("ffn1_w_gu", "ffn2_w_gu"):
        return g
    if name in ("ffn1_w_down", "ffn2_w_down", "w_in", "w_out"):
        return g.reshape(N_CHIPS, g.shape[0] // N_CHIPS, g.shape[1])
    return jnp.moveaxis(g.reshape(g.shape[0], N_CHIPS, g.shape[1] // N_CHIPS), 1, 0)


def _pack_small(grads, g_final, loss):
    parts = []
    for l in range(DEPTH):
        for nm in SMALL:
            parts.append(grads[l][nm].reshape(-1))
    parts.append(g_final.reshape(-1))
    parts.append(loss.reshape(1))
    flat = jnp.concatenate(parts)
    return jnp.pad(flat, (0, SMALL_ROWS * D - flat.shape[0])).reshape(SMALL_ROWS, D)


def _unpack_small(packed, params):
    flat = packed.reshape(-1)
    out = {nm: [] for nm in SMALL}
    off = 0
    for l in range(DEPTH):
        for nm in SMALL:
            shp = params[nm].shape[1:]
            n = int(np.prod(shp))
            out[nm].append(flat[off:off + n].reshape(shp))
            off += n
    res = {nm: jnp.stack(v) for nm, v in out.items()}
    res["final_norm"] = flat[off:off + D]
    return res, flat[off + D]


def _update(name, w, g, m, v):
    shp = w.shape
    if w.ndim == 1:
        view = (1, shp[0])
    elif w.size <= 65536:
        view = (shp[0], w.size // shp[0])
    else:
        view = (w.size // shp[-1], shp[-1])
    tr = view[0]
    for cand in (512, 352, 256, 128):
        if view[0] % cand == 0 and view[0] > cand:
            tr = cand
            break
    d, mn, vn = _adamw(w.reshape(view), g.reshape(view), m.reshape(view), v.reshape(view), name="adamw_" + name, tr=tr)
    return d.reshape(shp), mn.reshape(shp), vn.reshape(shp)


WEIGHTS = ['ffn1_norm', 'ffn1_w_gu', 'ffn1_w_down', 'mix_norm', 'w_in', 'q_a_norm', 'w_q_b', 'kv_a_norm', 'w_kv_b', 'pool_w',
           'pool_scale', 'fox_b_f', 'w_out', 'ffn2_norm', 'ffn2_w_gu', 'ffn2_w_down', 'final_norm']


def kernel(x, ffn1_norm, ffn1_w_gu, ffn1_w_down, mix_norm, w_in, q_a_norm, w_q_b, kv_a_norm, w_kv_b, pool_w, pool_scale, fox_b_f, w_out, ffn2_norm, ffn2_w_gu, ffn2_w_down, final_norm, loss_target, m_ffn1_norm, m_ffn1_w_gu, m_ffn1_w_down, m_mix_norm, m_w_in, m_q_a_norm, m_w_q_b, m_kv_a_norm, m_w_kv_b, m_pool_w, m_pool_scale, m_fox_b_f, m_w_out, m_ffn2_norm, m_ffn2_w_gu, m_ffn2_w_down, m_final_norm, v_ffn1_norm, v_ffn1_w_gu, v_ffn1_w_down, v_mix_norm, v_w_in, v_q_a_norm, v_w_q_b, v_kv_a_norm, v_w_kv_b, v_pool_w, v_pool_scale, v_fox_b_f, v_w_out, v_ffn2_norm, v_ffn2_w_gu, v_ffn2_w_down, v_final_norm):
    params = dict(ffn1_norm=ffn1_norm, ffn1_w_gu=ffn1_w_gu, ffn1_w_down=ffn1_w_down, mix_norm=mix_norm, w_in=w_in, q_a_norm=q_a_norm,
                  w_q_b=w_q_b, kv_a_norm=kv_a_norm, w_kv_b=w_kv_b, pool_w=pool_w, pool_scale=pool_scale, fox_b_f=fox_b_f, w_out=w_out,
                  ffn2_norm=ffn2_norm, ffn2_w_gu=ffn2_w_gu, ffn2_w_down=ffn2_w_down, final_norm=final_norm)
    mom = dict(ffn1_norm=m_ffn1_norm, ffn1_w_gu=m_ffn1_w_gu, ffn1_w_down=m_ffn1_w_down, mix_norm=m_mix_norm, w_in=m_w_in,
               q_a_norm=m_q_a_norm, w_q_b=m_w_q_b, kv_a_norm=m_kv_a_norm, w_kv_b=m_w_kv_b, pool_w=m_pool_w, pool_scale=m_pool_scale,
               fox_b_f=m_fox_b_f, w_out=m_w_out, ffn2_norm=m_ffn2_norm, ffn2_w_gu=m_ffn2_w_gu, ffn2_w_down=m_ffn2_w_down,
               final_norm=m_final_norm)
    var = dict(ffn1_norm=v_ffn1_norm, ffn1_w_gu=v_ffn1_w_gu, ffn1_w_down=v_ffn1_w_down, mix_norm=v_mix_norm, w_in=v_w_in,
               q_a_norm=v_q_a_norm, w_q_b=v_w_q_b, kv_a_norm=v_kv_a_norm, w_kv_b=v_w_kv_b, pool_w=v_pool_w, pool_scale=v_pool_scale,
               fox_b_f=v_fox_b_f, w_out=v_w_out, ffn2_norm=v_ffn2_norm, ffn2_w_gu=v_ffn2_w_gu, ffn2_w_down=v_ffn2_w_down,
               final_norm=v_final_norm)

    first = [("ffn1_w_gu", 0), ("ffn1_w_down", 0)]
    mix0 = [(nm, 0) for nm in ("w_in", "w_q_b", "w_kv_b", "w_out")]
    rest = [(nm, l) for nm in BIG for l in range(DEPTH) if (nm, l) not in first + mix0]

    def shards(keys, zero=0.0):
        return [((_pad_w_in(params[nm]) if nm == "w_in" else params[nm])[l] + zero).astype(BF16) for nm, l in keys]

    wts = {nm: [None] * DEPTH for nm in WEIGHT_VIEWS}
    wts["pool_w"] = params["pool_w"]
    got = _gather_blocking(shards(first))
    _prepare_weights(dict(zip(first, got)), wts)
    sems_m, src_m, land_m, token_m = _gather_start(shards(mix0), got[0], "mix0")
    sm = dict(params)
    sm["ffn1_norm"] = params["ffn1_norm"] + token_m[0, 0]
    rest_shards = shards(rest, token_m[0, 0])
    flying = {}

    def late_weights(stage, act, sm_now):
        if stage == "ffn1":
            lands = _gather_forward(_gather_wait(sems_m, src_m, land_m, act, "mix0"), "mix0")
            _prepare_weights(dict(zip(mix0, lands)), wts)
            flying["rest"] = _gather_start(rest_shards, lands[0], "rest")
            sm_next = dict(sm_now)
            sm_next["mix_norm"] = sm_now["mix_norm"] + flying["rest"][3][0, 0]
            return sm_next
        sems_r, src_r, land_r, _ = flying["rest"]
        lands = _gather_forward(_gather_wait(sems_r, src_r, land_r, act, "rest"), "rest")
        _prepare_weights(dict(zip(rest, lands)), wts)
        return sm_now

    pos = _position()
    flight = {}

    groups = {"l1": (1, BIG), "l0a": (0, [nm for nm in BIG if not nm.startswith("ffn1")]),
              "l0b": (0, [nm for nm in BIG if nm.startswith("ffn1")])}
    pending = {}

    def to_chips(key, full, sib):
        psum = _sum2_bf16(pos, full, sib, name=f"chip_sum_{key}")
        s2 = _split_start(_stage2_copies, psum, [((3,) + p.shape[1:], p.dtype) for p in psum], 3 * len(psum),
                          f"reduce_stage2_start_{key}")
        flight[key] = (full, sib, s2)
        return s2[3][0, 0]

    def grads_ready(l, stage, g, sm_now):
        behind, tok = None, None
        if (l, stage) == (1, "ffn1"):
            full = [_chip_major(nm, g[nm]) for nm in BIG]
            pending["l1"] = _split_start(_stage1_copies, full, [((N_CHIPS, f.shape[1] // 2, f.shape[2]), F32) for f in full],
                                         len(full), "reduce_stage1_start_l1")
            behind, tok = "ffn2_norm", pending["l1"][3][0, 0]
        elif (l, stage) == (0, "ffn2"):
            sems1, full_thru, sib_land, _ = pending["l1"]
            full, sib = _split_wait(_stage1_copies, sems1, full_thru, sib_land, g["ffn2_w_down"], "reduce_stage1_wait_l1")
            tok = to_chips("l1", full, sib)
        elif (l, stage) == (0, "mix"):
            full = [_chip_major(nm, g[nm]) for nm in groups["l0a"][1]]
            pending["l0a"] = _split_start(_stage1_copies, full, [((N_CHIPS, f.shape[1] // 2, f.shape[2]), F32) for f in full],
                                          len(full), "reduce_stage1_start_l0a")
            behind, tok = "ffn1_norm", pending["l0a"][3][0, 0]
        elif (l, stage) == (0, "ffn1_tokens"):
            sems1, full_thru, sib_land, _ = pending["l0a"]
            full, sib = _split_wait(_stage1_copies, sems1, full_thru, sib_land, g["dx"], "reduce_stage1_wait_l0a")
            tok = to_chips("l0a", full, sib)
        elif (l, stage) == (0, "ffn1"):
            full = [_chip_major(nm, g[nm]) for nm in groups["l0b"][1]]
            pending["l0b"] = _split_start(_stage1_copies, full, [((N_CHIPS, f.shape[1] // 2, f.shape[2]), F32) for f in full],
                                          len(full), "reduce_stage1_start_l0b")
        if behind is None:
            return sm_now, tok
        sm_next = dict(sm_now)
        sm_next[behind] = sm_now[behind] + tok
        return sm_next, tok

    loss, dx, grads, g_final = _local_step(x[0], loss_target[0], wts, sm, late_weights, grads_ready)

    def view2d(a):
        return a.reshape(a.size // a.shape[-1], a.shape[-1])

    after = pending["l0b"][3]
    done = {nm: None for nm in BIG}
    for key in ("l1", "l0a", "l0b"):
        l, names = groups[key]
        full, sib, (sems2, ps_thru, lands2, _) = flight[key]
        _, recv = _split_wait(_stage2_copies, sems2, ps_thru, lands2, after, f"reduce_stage2_wait_{key}")
        whole = _reduce_stage3(_sum5(pos, full, sib, recv, name=f"grad_sum_{key}"), key)
        for nm, g_l in zip(names, whole):
            if nm == "w_in":
                g_l = _unpad_w_in(g_l)
            tr = max(t for t in (512, 352, 256, 128) if g_l.shape[0] % t == 0)
            done[nm] = _adamw_layer(view2d(params[nm]), g_l, view2d(mom[nm]), view2d(var[nm]), l, done[nm],
                                    name=f"adamw_{nm}_{l}", tr=tr)
        after = done[names[-1]][0][-8:, 0:128]
        if key == "l1":
            small_g, loss = _unpack_small(_allreduce_small(_pack_small(grads, g_final, loss)), params)
            sems1, full_thru, sib_land, _ = pending["l0b"]
            full_b, sib_b = _split_wait(_stage1_copies, sems1, full_thru, sib_land, after + small_g["final_norm"][0],
                                        "reduce_stage1_wait_l0b")
            after = after + to_chips("l0b", full_b, sib_b)
    gw, delta, new_m, new_v = dict(small_g), {}, {}, {}
    for nm in BIG:
        delta[nm], new_m[nm], new_v[nm], gw[nm] = [a.reshape(params[nm].shape) for a in done[nm]]
    for nm in small_g:
        delta[nm], new_m[nm], new_v[nm] = _update(nm, params[nm], gw[nm], mom[nm], var[nm])
    return (loss, dx[None], *[gw[n] for n in WEIGHTS], *[delta[n] for n in WEIGHTS], *[new_m[n] for n in WEIGHTS],
            *[new_v[n] for n in WEIGHTS])
```

```python
import functools
import math

import jax
import jax.numpy as jnp
import numpy as np
from jax import lax
from jax.experimental import pallas as pl
from jax.experimental.pallas import tpu as pltpu

F32 = jnp.float32
BF16 = jnp.bfloat16
MESH = pl.DeviceIdType.MESH
HBM_SPEC = pl.BlockSpec(memory_space=pltpu.HBM)

D = 1024
DEPTH = 2
D_FF = 2816
FF_SHARD = 1408
N_CHIPS = 4
H = 6
NOPE, ROPE, VDIM = 64, 32, 64
HALF_ROPE = ROPE // 2
Q_RANK, KV_RANK = 256, 128
POOL_W = 256
FOX_D = 64
N_IN = 1830
NZ = 2048
ROPE_THETA = 10000.0
EPS = 1e-6
POOL_HALO = 16
Z_QA, Z_KVA, Z_KR, Z_POOL, Z_FOX, Z_F = 0, 256, 384, 512, 768, 1920

ADAM_LR, ADAM_B1, ADAM_B2, ADAM_EPS, ADAM_WD, ADAM_STEP = 0.001, 0.9, 0.999, 1e-08, 0.01, 10

VMEM_LIMIT_V7X = 56 * 1024 * 1024


def _cp(sem=None, vmem=VMEM_LIMIT_V7X):
    return pltpu.CompilerParams(dimension_semantics=sem, vmem_limit_bytes=vmem)


def _sigmoid(x):
    return 0.5 * jnp.tanh(0.5 * x) + 0.5


def _dot(a, b, dims):
    return lax.dot_general(a, b, (dims, ((), ())), preferred_element_type=F32)


NN = ((1,), (0,))
NT = ((1,), (1,))
TN = ((0,), (0,))


def _mm(a, b, mode, *, name, out_dtype=F32, add=None, alpha=None, tm=512, tn=512, tk=512, n_major_out=False):
    if mode == "nn":
        (m, k), (k2, n) = a.shape, b.shape
    elif mode == "nt":
        (m, k), (n, k2) = a.shape, b.shape
    else:
        (k, m), (k2, n) = a.shape, b.shape
    assert k == k2
    tm, tn, tk = min(tm, m), min(tn, n), min(tk, k)
    assert m % tm == 0 and n % tn == 0 and k % tk == 0, (name, m, n, k, tm, tn, tk)
    nk = k // tk
    dims = {"nn": NN, "nt": NT, "tn": TN}[mode]
    a_spec = pl.BlockSpec((tk, tm), lambda i, j, kk: (kk, i)) if mode == "tn" else pl.BlockSpec((tm, tk), lambda i, j, kk: (i, kk))
    b_spec = pl.BlockSpec((tn, tk), lambda i, j, kk: (j, kk)) if mode == "nt" else pl.BlockSpec((tk, tn), lambda i, j, kk: (kk, j))
    in_specs = [a_spec, b_spec]
    args = [a, b]
    if add is not None:
        in_specs.append(pl.BlockSpec((tm, tn), lambda i, j, kk: (i, j)))
        args.append(add)
    if n_major_out:
        out_shape = jax.ShapeDtypeStruct((n // tn, m, tn), out_dtype)
        out_spec = pl.BlockSpec((None, tm, tn), lambda i, j, kk: (j, i, 0))
    else:
        out_shape = jax.ShapeDtypeStruct((m, n), out_dtype)
        out_spec = pl.BlockSpec((tm, tn), lambda i, j, kk: (i, j))

    def body(*refs):
        a_ref, b_ref = refs[0], refs[1]
        add_ref = refs[2] if add is not None else None
        o_ref, acc = refs[-2], refs[-1]
        kk = pl.program_id(2)

        @pl.when(kk == 0)
        def _():
            acc[...] = jnp.zeros_like(acc)

        acc[...] += _dot(a_ref[...].astype(BF16), b_ref[...].astype(BF16), dims)

        @pl.when(kk == nk - 1)
        def _():
            r = acc[...]
            if alpha is not None:
                r = r * alpha
            if add_ref is not None:
                r = r + add_ref[...].astype(F32)
            o_ref[...] = r.astype(out_dtype)

    return pl.pallas_call(
        body, name=name, grid=(m // tm, n // tn, nk), in_specs=in_specs, out_specs=out_spec, out_shape=out_shape,
        scratch_shapes=[pltpu.VMEM((tm, tn), F32)],
        compiler_params=_cp(("parallel", "parallel", "arbitrary")),
    )(*args)


def _norm_mm(x, col_block, gain, w, *, name, tm=512):
    s = x.shape[0]
    k, n = w.shape
    tm = min(tm, s)

    def body(x_ref, g_ref, w_ref, z_ref, h_ref):
        xv = x_ref[...]
        r = lax.rsqrt(jnp.mean(xv * xv, axis=-1, keepdims=True) + EPS)
        hv = (xv * r * g_ref[...]).astype(BF16)
        h_ref[...] = hv
        z_ref[...] = _dot(hv, w_ref[...], NN)

    return pl.pallas_call(
        body, name=name, grid=(s // tm,),
        in_specs=[pl.BlockSpec((tm, k), lambda i: (i, col_block)), pl.BlockSpec((1, k), lambda i: (0, 0)),
                  pl.BlockSpec((k, n), lambda i: (0, 0))],
        out_specs=[pl.BlockSpec((tm, n), lambda i: (i, 0)), pl.BlockSpec((tm, k), lambda i: (i, 0))],
        out_shape=[jax.ShapeDtypeStruct((s, n), F32), jax.ShapeDtypeStruct((s, k), BF16)],
        compiler_params=_cp(("parallel",)),
    )(x, gain.reshape(1, k), w)


def _rmsnorm_bwd(x, col_block, gain, da, w, dres=None, *, name, tm=512):
    s = x.shape[0]
    k, n = w.shape
    tm = min(tm, s)

    def body(*refs):
        x_ref, g_ref, da_ref, w_ref = refs[:4]
        dres_ref = refs[4] if dres is not None else None
        dx_ref, dg_ref = refs[-2], refs[-1]
        xv = x_ref[...]
        r = lax.rsqrt(jnp.mean(xv * xv, axis=-1, keepdims=True) + EPS)
        dhv = _dot(da_ref[...], w_ref[...], NT)
        a = dhv * g_ref[...]
        dx = r * a - xv * (r * r * r) * jnp.mean(a * xv, axis=-1, keepdims=True)
        if dres_ref is not None:
            dx = dx + dres_ref[...]
        dx_ref[...] = dx

        @pl.when(pl.program_id(0) == 0)
        def _():
            dg_ref[...] = jnp.zeros_like(dg_ref)

        dg_ref[...] += jnp.sum(dhv * xv * r, axis=0, keepdims=True)

    in_specs = [pl.BlockSpec((tm, k), lambda i: (i, col_block)), pl.BlockSpec((1, k), lambda i: (0, 0)),
                pl.BlockSpec((tm, n), lambda i: (i, 0)), pl.BlockSpec((k, n), lambda i: (0, 0))]
    args = [x, gain.reshape(1, k), da, w]
    if dres is not None:
        in_specs.append(pl.BlockSpec((tm, k), lambda i: (i, 0)))
        args.append(dres)
    dx, dg = pl.pallas_call(
        body, name=name, grid=(s // tm,), in_specs=in_specs,
        out_specs=[pl.BlockSpec((tm, k), lambda i: (i, 0)), pl.BlockSpec((1, k), lambda i: (0, 0))],
        out_shape=[jax.ShapeDtypeStruct((s, k), F32), jax.ShapeDtypeStruct((1, k), F32)],
        compiler_params=_cp(("arbitrary",)),
    )(*args)
    return dx, dg.reshape(k)


def _ffn_fwd(x, gain, w_gu4, w_d2, *, name, tm=256):
    s = x.shape[0]
    tm = min(tm, s)

    def body(x_ref, g_ref, wgu_ref, wd_ref, xo_ref, dgu_ref, act_ref):
        xv = x_ref[...]
        r = lax.rsqrt(jnp.mean(xv * xv, axis=-1, keepdims=True) + EPS)
        hv = (xv * r * g_ref[...]).astype(BF16)
        y = jnp.zeros((tm, D), F32)
        for j in range(2):
            g = _dot(hv, wgu_ref[j], NN)
            u = _dot(hv, wgu_ref[2 + j], NN)
            sg = _sigmoid(g)
            silu = g * sg
            dgu_ref[:, j * FF_SHARD:(j + 1) * FF_SHARD] = (u * (sg * (1.0 + g * (1.0 - sg)))).astype(BF16)
            dgu_ref[:, D_FF + j * FF_SHARD:D_FF + (j + 1) * FF_SHARD] = silu.astype(BF16)
            act = (silu * u).astype(BF16)
            act_ref[:, j * FF_SHARD:(j + 1) * FF_SHARD] = act
            y = y + _dot(act, wd_ref[j], NN)
        xo_ref[...] = xv + 0.5 * y

    row = lambda i: (i, 0)
    return pl.pallas_call(
        body, name=name, grid=(s // tm,),
        in_specs=[pl.BlockSpec((tm, D), row), pl.BlockSpec((1, D), lambda i: (0, 0)),
                  pl.BlockSpec((N_CHIPS, D, FF_SHARD), lambda i: (0, 0, 0), pipeline_mode=pl.Buffered(1)),
                  pl.BlockSpec((2, FF_SHARD, D), lambda i: (0, 0, 0), pipeline_mode=pl.Buffered(1))],
        out_specs=[pl.BlockSpec((tm, D), row), pl.BlockSpec((tm, 2 * D_FF), row), pl.BlockSpec((tm, D_FF), row)],
        out_shape=[jax.ShapeDtypeStruct((s, D), F32), jax.ShapeDtypeStruct((s, 2 * D_FF), BF16),
                   jax.ShapeDtypeStruct((s, D_FF), BF16)],
        compiler_params=_cp(("parallel",)),
    )(x, gain.reshape(1, D), w_gu4, w_d2)


FFN_ROW_CHUNK = 32


def _ffn_bwd(x, dxo, dloc, gain, w_gu4, w_d2, *, name, tm=256):
    s = x.shape[0]
    tm = min(tm, s)

    def body(x_ref, dxo_ref, dloc_ref, g_ref, wgu_ref, wd_ref, dx_ref, dgu_ref, h_ref, dy_ref, dg_ref):
        xv = x_ref[...]
        r = lax.rsqrt(jnp.mean(xv * xv, axis=-1, keepdims=True) + EPS)
        xh = xv * r
        h_ref[...] = (xh * g_ref[...]).astype(BF16)
        dxov = dxo_ref[...]
        dy = (0.5 * dxov).astype(BF16)
        dy_ref[...] = dy
        gcols = [slice(j * FF_SHARD, (j + 1) * FF_SHARD) for j in range(2)]
        ucols = [slice(D_FF + j * FF_SHARD, D_FF + (j + 1) * FF_SHARD) for j in range(2)]
        dacts = [_dot(dy, wd_ref[j], NT) for j in range(2)]
        for r0 in range(0, tm, FFN_ROW_CHUNK):
            rows = slice(r0, r0 + FFN_ROW_CHUNK)
            for j in range(2):
                da = dacts[j][rows]
                dgu_ref[rows, gcols[j]] = (da * dloc_ref[rows, gcols[j]].astype(F32)).astype(BF16)
                dgu_ref[rows, ucols[j]] = (da * dloc_ref[rows, ucols[j]].astype(F32)).astype(BF16)
        dh = jnp.zeros((tm, D), F32)
        for j in range(2):
            dh = dh + _dot(dgu_ref[:, gcols[j]], wgu_ref[j], NT) + _dot(dgu_ref[:, ucols[j]], wgu_ref[2 + j], NT)
        a = dh * g_ref[...]
        dx_ref[...] = dxov + r * a - xh * (r * jnp.mean(a * xh, axis=-1, keepdims=True))

        @pl.when(pl.program_id(0) == 0)
        def _():
            dg_ref[...] = jnp.zeros_like(dg_ref)

        dg_ref[...] += jnp.sum(dh * xh, axis=0, keepdims=True)

    row = lambda i: (i, 0)
    outs = pl.pallas_call(
        body, name=name, grid=(s // tm,),
        in_specs=[pl.BlockSpec((tm, D), row), pl.BlockSpec((tm, D), row), pl.BlockSpec((tm, 2 * D_FF), row),
                  pl.BlockSpec((1, D), lambda i: (0, 0)),
                  pl.BlockSpec((N_CHIPS, D, FF_SHARD), lambda i: (0, 0, 0), pipeline_mode=pl.Buffered(1)),
                  pl.BlockSpec((2, FF_SHARD, D), lambda i: (0, 0, 0), pipeline_mode=pl.Buffered(1))],
        out_specs=[pl.BlockSpec((tm, D), row), pl.BlockSpec((tm, 2 * D_FF), row),
                   pl.BlockSpec((tm, D), row), pl.BlockSpec((tm, D), row), pl.BlockSpec((1, D), lambda i: (0, 0))],
        out_shape=[jax.ShapeDtypeStruct((s, D), F32), jax.ShapeDtypeStruct((s, 2 * D_FF), BF16),
                   jax.ShapeDtypeStruct((s, D), BF16), jax.ShapeDtypeStruct((s, D), BF16), jax.ShapeDtypeStruct((1, D), F32)],
        compiler_params=_cp(("arbitrary",)),
    )(x, dxo, dloc, gain.reshape(1, D), w_gu4, w_d2)
    dx, dgu, h, dy, dg = outs
    return dx, dgu, h, dy, dg.reshape(D)


DA = 128
SCALE_MLA = 1.0 / math.sqrt(NOPE + ROPE)
SCALE_FOX = 1.0 / math.sqrt(FOX_D)


def _causal_blocks(nb, key_major):
    if key_major:
        pairs = [(i, j) for j in range(nb) for i in range(j, nb)]
    else:
        pairs = [(i, j) for i in range(nb) for j in range(i + 1)]
    return (jnp.asarray(np.array([p[0] for p in pairs], np.int32)), jnp.asarray(np.array([p[1] for p in pairs], np.int32)))


HEADS_PER_STEP = 3
ROW_CHUNK = 64

def _col_to_row(col):
    return jnp.broadcast_to(col, (col.shape[0], DA)).T[0:1, :]


def _attn_fwd(qa, ka, va, dv, *, name, t=512):
    h, s, _ = qa.shape
    t = min(t, s)
    nb = s // t
    g = H
    qi, kj = _causal_blocks(nb, key_major=False)

    rc = min(ROW_CHUNK, t)

    def body(qi_ref, kj_ref, q_ref, k_ref, v_ref, o_ref, lse_ref, m_sc, acc_sc, p_sc, a_sc):
        n = pl.program_id(1)
        i, j = qi_ref[n], kj_ref[n]

        @pl.when(j == 0)
        def _():
            m_sc[...] = jnp.full_like(m_sc, -jnp.inf)
            acc_sc[...] = jnp.zeros_like(acc_sc)

        def step(masked):
            scs = [_dot(q_ref[hh], k_ref[hh], NT) for hh in range(g)]
            for r0 in range(0, t, rc):
                rows = slice(r0, r0 + rc)
                for hh in range(g):
                    sr = scs[hh][rows]
                    if masked:
                        row = lax.broadcasted_iota(jnp.int32, (rc, t), 0) + r0
                        col = lax.broadcasted_iota(jnp.int32, (rc, t), 1)
                        sr = jnp.where(col <= row, sr, -jnp.inf)
                    tiles = [sr[:, c0:c0 + DA] for c0 in range(0, t, DA)]
                    top = tiles[0]
                    for tile in tiles[1:]:
                        top = jnp.maximum(top, tile)
                    m_old = m_sc[hh, rows]
                    m_new = jnp.maximum(m_old, jnp.max(top, axis=-1, keepdims=True))
                    for c0, tile in zip(range(0, t, DA), tiles):
                        p_sc[hh, rows, c0:c0 + DA] = jnp.exp(tile - m_new).astype(BF16)
                    a_sc[hh, rows] = jnp.exp(m_old - m_new)
                    m_sc[hh, rows] = m_new
            for hh in range(g):
                acc_sc[hh] = a_sc[hh] * acc_sc[hh] + _dot(p_sc[hh], v_ref[hh], NN)

        @pl.when(j < i)
        def _():
            step(False)

        @pl.when(j == i)
        def _():
            step(True)
            for hh in range(g):
                acc = acc_sc[hh]
                l = acc[:, dv:dv + 1]
                o_ref[hh] = acc[:, :dv] / l
                lse_ref[hh] = _col_to_row(m_sc[hh][:, 0:1] + jnp.log(l))

    qmap = lambda hg, n, qi_r, kj_r: (hg, qi_r[n], 0)
    kmap = lambda hg, n, qi_r, kj_r: (hg, kj_r[n], 0)
    return pl.pallas_call(
        body, name=name,
        grid_spec=pltpu.PrefetchScalarGridSpec(
            num_scalar_prefetch=2, grid=(h // g, qi.shape[0]),
            in_specs=[pl.BlockSpec((g, t, DA), qmap), pl.BlockSpec((g, t, DA), kmap), pl.BlockSpec((g, t, DA), kmap)],
            out_specs=[pl.BlockSpec((g, t, dv), qmap), pl.BlockSpec((g, 1, t), lambda hg, n, qi_r, kj_r: (hg, 0, qi_r[n]))],
            scratch_shapes=[pltpu.VMEM((g, t, DA), F32), pltpu.VMEM((g, t, DA), F32), pltpu.VMEM((g, t, t), BF16),
                            pltpu.VMEM((g, t, DA), F32)]),
        out_shape=[jax.ShapeDtypeStruct((h, s, dv), F32), jax.ShapeDtypeStruct((h, 1, s), F32)],
        compiler_params=_cp(("parallel", "arbitrary")),
    )(qi, kj, qa, ka, va)


def _attn_bwd(qa, ka, va, doa, lse_row, delta_row, decay, *, name, t=512):
    h, s, _ = qa.shape
    t = min(t, s)
    nb = s // t
    g = HEADS_PER_STEP
    rc = min(ROW_CHUNK, t)
    qi, kj = _causal_blocks(nb, key_major=True)
    nsteps = qi.shape[0]

    def body(*refs):
        qi_ref, kj_ref, q_ref, k_ref, v_ref, do_ref, lse_ref, dl_ref = refs[:8]
        p_sc, ds_sc = refs[-2:]
        if decay:
            dq_ref, dk_ref, dv_ref, dcq_ref, dck_ref, dq_acc, dk_acc, dv_acc, dcq_acc, dck_acc = refs[8:-2]
        else:
            dq_ref, dk_ref, dv_ref, dq_acc, dk_acc, dv_acc = refs[8:-2]
        n = pl.program_id(1)
        i, j = qi_ref[n], kj_ref[n]

        @pl.when(n == 0)
        def _():
            dq_acc[...] = jnp.zeros_like(dq_acc)
            if decay:
                dcq_acc[...] = jnp.zeros_like(dcq_acc)

        @pl.when(i == j)
        def _():
            dk_acc[...] = jnp.zeros_like(dk_acc)
            dv_acc[...] = jnp.zeros_like(dv_acc)
            if decay:
                dck_acc[...] = jnp.zeros_like(dck_acc)

        def step(masked):
            sts = [_dot(k_ref[hh], q_ref[hh], NT) for hh in range(g)]
            dpts = [_dot(v_ref[hh], do_ref[hh], NT) for hh in range(g)]
            dcq = [jnp.zeros((1, t), F32) for _ in range(g)]
            for r0 in range(0, t, rc):
                rows = slice(r0, r0 + rc)
                for hh in range(g):
                    st = sts[hh][rows]
                    if masked:
                        row = lax.broadcasted_iota(jnp.int32, (rc, t), 0) + r0
                        col = lax.broadcasted_iota(jnp.int32, (rc, t), 1)
                        st = jnp.where(row <= col, st, -jnp.inf)
                    pt = jnp.exp(st - lse_ref[hh])
                    dst = pt * (dpts[hh][rows] - dl_ref[hh])
                    p_sc[hh, rows] = pt.astype(BF16)
                    ds_sc[hh, rows] = dst.astype(BF16)
                    if decay:
                        dcq[hh] = dcq[hh] + jnp.sum(dst, axis=0, keepdims=True)
                        dck_acc[hh, rows] -= jnp.sum(dst, axis=1, keepdims=True)
            for hh in range(g):
                dv_acc[hh] += _dot(p_sc[hh], do_ref[hh], NN)
                dk_acc[hh] += _dot(ds_sc[hh], q_ref[hh], NN)
                dq_acc[hh, i] += _dot(ds_sc[hh], k_ref[hh], TN)
                if decay:
                    dcq_acc[hh, i] += dcq[hh]

        @pl.when(i > j)
        def _():
            step(False)

        @pl.when(i == j)
        def _():
            step(True)

        @pl.when(i == nb - 1)
        def _():
            dk_ref[...] = dk_acc[...]
            dv_ref[...] = dv_acc[...]
            if decay:
                for hh in range(g):
                    dck_ref[hh] = _col_to_row(dck_acc[hh])

        @pl.when(n == nsteps - 1)
        def _():
            dq_ref[...] = dq_acc[...]
            if decay:
                dcq_ref[...] = dcq_acc[...]

    kmap = lambda hg, n, qi_r, kj_r: (hg, kj_r[n], 0)
    qmap = lambda hg, n, qi_r, kj_r: (hg, qi_r[n], 0)
    qrow = lambda hg, n, qi_r, kj_r: (hg, 0, qi_r[n])
    krow = lambda hg, n, qi_r, kj_r: (hg, 0, kj_r[n])
    whole = lambda hg, n, qi_r, kj_r: (hg, 0, 0, 0)
    in_specs = [pl.BlockSpec((g, t, DA), qmap), pl.BlockSpec((g, t, DA), kmap), pl.BlockSpec((g, t, DA), kmap),
                pl.BlockSpec((g, t, DA), qmap), pl.BlockSpec((g, 1, t), qrow), pl.BlockSpec((g, 1, t), qrow)]
    out_specs = [pl.BlockSpec((g, nb, t, DA), whole), pl.BlockSpec((g, t, DA), kmap), pl.BlockSpec((g, t, DA), kmap)]
    out_shape = [jax.ShapeDtypeStruct((h, nb, t, DA), F32), jax.ShapeDtypeStruct((h, s, DA), F32), jax.ShapeDtypeStruct((h, s, DA), F32)]
    scratch = [pltpu.VMEM((g, nb, t, DA), F32), pltpu.VMEM((g, t, DA), F32), pltpu.VMEM((g, t, DA), F32)]
    if decay:
        out_specs += [pl.BlockSpec((g, nb, 1, t), whole), pl.BlockSpec((g, 1, t), krow)]
        out_shape += [jax.ShapeDtypeStruct((h, nb, 1, t), F32), jax.ShapeDtypeStruct((h, 1, s), F32)]
        scratch += [pltpu.VMEM((g, nb, 1, t), F32), pltpu.VMEM((g, t, 1), F32)]
    scratch += [pltpu.VMEM((g, t, t), BF16), pltpu.VMEM((g, t, t), BF16)]
    outs = pl.pallas_call(
        body, name=name,
        grid_spec=pltpu.PrefetchScalarGridSpec(num_scalar_prefetch=2, grid=(h // g, nsteps), in_specs=in_specs, out_specs=out_specs,
                                               scratch_shapes=scratch),
        out_shape=out_shape, compiler_params=_cp(("parallel", "arbitrary")),
    )(qi, kj, qa, ka, va, doa, lse_row, delta_row)
    outs = list(outs)
    outs[0] = outs[0].reshape(h, s, DA)
    if decay:
        outs[3] = outs[3].reshape(h, 1, s)
    return outs


def _sel(rows, cols, pairs, value=1.0):
    m = np.zeros((rows, cols), np.float32)
    for r, c in pairs:
        m[r, c] = value
    return jnp.asarray(m, BF16)


def _lane_row(lanes):
    m = np.zeros((1, DA), np.float32)
    m[0, list(lanes)] = 1.0
    return jnp.asarray(m)


def _rms(xv, gain):
    r = lax.rsqrt(jnp.mean(xv * xv, axis=-1, keepdims=True) + EPS)
    return xv * r * gain


def _mla_q_prep(z, gain, wq_a, wq_b, cq, sq, *, name, tm=512):
    s = z.shape[0]
    tm = min(tm, s)

    def body(z_ref, g_ref, wa_ref, wb_ref, c_ref, s_ref, qa_ref, qn_ref):
        qn = _rms(z_ref[...], g_ref[...]).astype(BF16)
        qn_ref[...] = qn
        c, sn = c_ref[...], s_ref[...]
        for hh in range(H):
            cols = slice(hh * DA, (hh + 1) * DA)
            qa_ref[hh] = (_dot(qn, wa_ref[:, cols], NN) * c + _dot(qn, wb_ref[:, cols], NN) * sn).astype(BF16)

    row = lambda i: (i, 0)
    fixed = lambda i: (0, 0)
    return pl.pallas_call(
        body, name=name, grid=(s // tm,),
        in_specs=[pl.BlockSpec((tm, Q_RANK), lambda i: (i, Z_QA // Q_RANK)), pl.BlockSpec((1, Q_RANK), fixed),
                  pl.BlockSpec((Q_RANK, H * DA), fixed), pl.BlockSpec((Q_RANK, H * DA), fixed),
                  pl.BlockSpec((tm, DA), row), pl.BlockSpec((tm, DA), row)],
        out_specs=[pl.BlockSpec((H, tm, DA), lambda i: (0, i, 0)), pl.BlockSpec((tm, Q_RANK), row)],
        out_shape=[jax.ShapeDtypeStruct((H, s, DA), BF16), jax.ShapeDtypeStruct((s, Q_RANK), BF16)],
        compiler_params=_cp(("parallel",)),
    )(z, gain.reshape(1, Q_RANK), wq_a, wq_b, cq, sq)


def _mla_kv_prep(z, gain, wk, wv, ck, sk, *, name, tm=512):
    s = z.shape[0]
    tm = min(tm, s)
    one = _lane_row([VDIM])

    def body(zkv_ref, z3_ref, z15_ref, g_ref, wk_ref, wv_ref, c_ref, s_ref, one_ref, ka_ref, va_ref, kvn_ref):
        kvn = _rms(zkv_ref[...], g_ref[...]).astype(BF16)
        kvn_ref[...] = kvn
        kpe = z3_ref[...] * c_ref[...] + z15_ref[...] * s_ref[...]
        for hh in range(H):
            cols = slice(hh * DA, (hh + 1) * DA)
            ka_ref[hh] = (_dot(kvn, wk_ref[:, cols], NN) + kpe).astype(BF16)
            va_ref[hh] = (_dot(kvn, wv_ref[:, cols], NN) + one_ref[...]).astype(BF16)

    row = lambda i: (i, 0)
    fixed = lambda i: (0, 0)
    blk = lambda c: pl.BlockSpec((tm, DA), lambda i: (i, c))
    heads = pl.BlockSpec((H, tm, DA), lambda i: (0, i, 0))
    return pl.pallas_call(
        body, name=name, grid=(s // tm,),
        in_specs=[blk(Z_KVA // DA), blk(Z_KR // DA), blk(Z_F // DA), pl.BlockSpec((1, KV_RANK), fixed),
                  pl.BlockSpec((KV_RANK, H * DA), fixed), pl.BlockSpec((KV_RANK, H * DA), fixed),
                  pl.BlockSpec((tm, DA), row), pl.BlockSpec((tm, DA), row), pl.BlockSpec((1, DA), fixed)],
        out_specs=[heads, heads, pl.BlockSpec((tm, KV_RANK), row)],
        out_shape=[jax.ShapeDtypeStruct((H, s, DA), BF16), jax.ShapeDtypeStruct((H, s, DA), BF16),
                   jax.ShapeDtypeStruct((s, KV_RANK), BF16)],
        compiler_params=_cp(("parallel",)),
    )(z, z, z, gain.reshape(1, KV_RANK), wk, wv, ck, sk, one)


DEC_C = (FOX_D, FOX_D + 1, FOX_D + 2)
DEC_1 = (FOX_D + 3, FOX_D + 4, FOX_D + 5)


def _fox_prep(z, c3t, *, name, tm=512):
    s = z.shape[0]
    tm = min(tm, s)
    w = H * FOX_D
    left = [(r, r) for r in range(FOX_D)]
    right = [(FOX_D + r, r) for r in range(FOX_D)]
    pq = jnp.stack([_sel(DA, DA, left, SCALE_FOX), _sel(DA, DA, right, SCALE_FOX)])
    pk = jnp.stack([_sel(DA, DA, left), _sel(DA, DA, right)])
    pcq = jnp.stack([_sel(32, DA, [(hh + 8 * k, DEC_C[k]) for k in range(3)]) for hh in range(H)])
    pck = jnp.stack([_sel(32, DA, [(hh + 8 * k, DEC_1[k]) for k in range(3)], -1.0) for hh in range(H)])
    rows3 = jnp.concatenate([_lane_row(DEC_1), _lane_row(DEC_C), _lane_row([FOX_D])], axis=0)

    def body(zq_ref, zk_ref, zv_ref, c_ref, pq_ref, pk_ref, pcq_ref, pck_ref, r_ref, qa_ref, ka_ref, va_ref):
        c3 = c_ref[...]
        for pair in range(H // 2):
            lanes = slice(pair * DA, (pair + 1) * DA)
            zq, zk, zv = zq_ref[:, lanes].astype(BF16), zk_ref[:, lanes].astype(BF16), zv_ref[:, lanes].astype(BF16)
            for side in range(2):
                hh = 2 * pair + side
                qa_ref[hh] = (_dot(zq, pq_ref[side], NN) + _dot(c3, pcq_ref[hh], TN) + r_ref[0:1, :]).astype(BF16)
                ka_ref[hh] = (_dot(zk, pk_ref[side], NN) + _dot(c3, pck_ref[hh], TN) + r_ref[1:2, :]).astype(BF16)
                va_ref[hh] = (_dot(zv, pk_ref[side], NN) + r_ref[2:3, :]).astype(BF16)

    fixed2 = lambda i: (0, 0)
    fixed3 = lambda i: (0, 0, 0)
    heads = pl.BlockSpec((H, tm, DA), lambda i: (0, i, 0))
    zblk = lambda c: pl.BlockSpec((tm, w), lambda i: (i, c))
    return pl.pallas_call(
        body, name=name, grid=(s // tm,),
        in_specs=[zblk(Z_FOX // w), zblk(Z_FOX // w + 1), zblk(Z_FOX // w + 2), pl.BlockSpec((32, tm), lambda i: (0, i)),
                  pl.BlockSpec((2, DA, DA), fixed3), pl.BlockSpec((2, DA, DA), fixed3),
                  pl.BlockSpec((H, 32, DA), fixed3), pl.BlockSpec((H, 32, DA), fixed3), pl.BlockSpec((3, DA), fixed2)],
        out_specs=[heads, heads, heads], out_shape=[jax.ShapeDtypeStruct((H, s, DA), BF16)] * 3,
        compiler_params=_cp(("parallel",)),
    )(z, z, z, c3t, pq, pk, pcq, pck, rows3)


def _mix_out(oa, yb, oc, w_out, x1, *, name, tm=512):
    s = yb.shape[0]
    tm = min(tm, s)
    e2 = jnp.stack([_sel(VDIM, DA, [(r, r) for r in range(VDIM)]), _sel(VDIM, DA, [(r, VDIM + r) for r in range(VDIM)])])

    def body(oa_ref, yb_ref, oc_ref, e_ref, w_ref, x_ref, x2_ref, cat_ref):
        def pairs(o_ref):
            return [(_dot(o_ref[2 * p].astype(BF16), e_ref[0], NN) + _dot(o_ref[2 * p + 1].astype(BF16), e_ref[1], NN)).astype(BF16)
                    for p in range(H // 2)]

        cat = jnp.concatenate(pairs(oa_ref) + [yb_ref[...].astype(BF16)] + pairs(oc_ref), axis=1)
        cat_ref[...] = cat
        x2_ref[...] = x_ref[...] + _dot(cat, w_ref[...], NN)

    row = lambda i: (i, 0)
    heads = pl.BlockSpec((H, tm, VDIM), lambda i: (0, i, 0))
    return pl.pallas_call(
        body, name=name, grid=(s // tm,),
        in_specs=[heads, pl.BlockSpec((tm, POOL_W), row), heads, pl.BlockSpec((2, VDIM, DA), lambda i: (0, 0, 0)),
                  pl.BlockSpec((D, D), lambda i: (0, 0)), pl.BlockSpec((tm, D), row)],
        out_specs=[pl.BlockSpec((tm, D), row), pl.BlockSpec((tm, D), row)],
        out_shape=[jax.ShapeDtypeStruct((s, D), F32), jax.ShapeDtypeStruct((s, D), BF16)],
        compiler_params=_cp(("parallel",)),
    )(oa, yb, oc, e2, w_out, x1)


def _mix_out_bwd(dx2b, w_out, oa, oc, *, name, tm=512):
    s = dx2b.shape[0]
    tm = min(tm, s)
    f2 = jnp.stack([_sel(DA, DA, [(r, r) for r in range(VDIM)]), _sel(DA, DA, [(VDIM + r, r) for r in range(VDIM)])])
    nv = H * VDIM

    def body(dx_ref, w_ref, oa_ref, oc_ref, f_ref, doa_ref, doc_ref, dyb_ref, dla_ref, dlc_ref):
        dcat = _dot(dx_ref[...], w_ref[...], NT)
        dyb_ref[...] = dcat[:, nv:nv + POOL_W]
        for base, o_ref, do_ref, dl_ref in ((0, oa_ref, doa_ref, dla_ref), (nv + POOL_W, oc_ref, doc_ref, dlc_ref)):
            for p in range(H // 2):
                blk = dcat[:, base + p * DA:base + (p + 1) * DA].astype(BF16)
                for side in range(2):
                    hh = 2 * p + side
                    do = _dot(blk, f_ref[side], NN)
                    do_ref[hh] = do.astype(BF16)
                    dl_ref[hh] = _col_to_row(jnp.sum(do[:, :VDIM] * o_ref[hh], axis=-1, keepdims=True))

    row = lambda i: (i, 0)
    heads = lambda w: pl.BlockSpec((H, tm, w), lambda i: (0, i, 0))
    return pl.pallas_call(
        body, name=name, grid=(s // tm,),
        in_specs=[pl.BlockSpec((tm, D), row), pl.BlockSpec((D, D), lambda i: (0, 0)), heads(VDIM), heads(VDIM),
                  pl.BlockSpec((2, DA, DA), lambda i: (0, 0, 0))],
        out_specs=[heads(DA), heads(DA), pl.BlockSpec((tm, POOL_W), row),
                   pl.BlockSpec((H, 1, tm), lambda i: (0, 0, i)), pl.BlockSpec((H, 1, tm), lambda i: (0, 0, i))],
        out_shape=[jax.ShapeDtypeStruct((H, s, DA), BF16), jax.ShapeDtypeStruct((H, s, DA), BF16),
                   jax.ShapeDtypeStruct((s, POOL_W), F32), jax.ShapeDtypeStruct((H, 1, s), F32), jax.ShapeDtypeStruct((H, 1, s), F32)],
        compiler_params=_cp(("parallel",)),
    )(dx2b, w_out, oa, oc, f2)


def _mla_bwd_prep(dqa, dka, dva, dft, cq, sq, ck, sk, *, name, tm=512):
    s = dqa.shape[1]
    tm = min(tm, s)
    keep = _lane_row(range(NOPE))

    def body(dq_ref, dk_ref, dv_ref, dft_ref, cq_ref, sq_ref, ck_ref, sk_ref, keep_ref, dqab_ref, dkv_ref, dz3_ref, dz15_ref):
        cqv, sqv = cq_ref[...], sq_ref[...]
        dkpe = jnp.zeros((tm, DA), F32)
        for hh in range(H):
            lanes = slice(hh * DA, (hh + 1) * DA)
            dq = dq_ref[hh]
            dqab_ref[:, lanes] = (dq * cqv).astype(BF16)
            dqab_ref[:, H * DA + hh * DA:H * DA + (hh + 1) * DA] = (dq * sqv).astype(BF16)
            dk = dk_ref[hh]
            dkpe = dkpe + dk
            dkv_ref[:, lanes] = (dk * keep_ref[...]).astype(BF16)
            dkv_ref[:, H * DA + hh * DA:H * DA + (hh + 1) * DA] = (dv_ref[hh] * keep_ref[...]).astype(BF16)
        dz3_ref[...] = (dkpe * ck_ref[...]).astype(BF16)
        dz15_ref[...] = (dkpe * sk_ref[...] + dft_ref[...]).astype(BF16)

    row = lambda i: (i, 0)
    heads = pl.BlockSpec((H, tm, DA), lambda i: (0, i, 0))
    tab = pl.BlockSpec((tm, DA), row)
    return pl.pallas_call(
        body, name=name, grid=(s // tm,),
        in_specs=[heads, heads, heads, tab, tab, tab, tab, tab, pl.BlockSpec((1, DA), lambda i: (0, 0))],
        out_specs=[pl.BlockSpec((tm, 2 * H * DA), row), pl.BlockSpec((tm, 2 * H * DA), row), tab, tab],
        out_shape=[jax.ShapeDtypeStruct((s, 2 * H * DA), BF16), jax.ShapeDtypeStruct((s, 2 * H * DA), BF16),
                   jax.ShapeDtypeStruct((s, DA), BF16), jax.ShapeDtypeStruct((s, DA), BF16)],
        compiler_params=_cp(("parallel",)),
    )(dqa, dka, dva, dft, cq, sq, ck, sk, keep)


def _fox_bwd_prep(dfqa, dfka, dfva, *, name, tm=512):
    s = dfqa.shape[1]
    tm = min(tm, s)
    place = lambda v: jnp.stack([_sel(DA, DA, [(r, r) for r in range(FOX_D)], v), _sel(DA, DA, [(r, FOX_D + r) for r in range(FOX_D)], v)])
    gq, gk = place(SCALE_FOX), place(1.0)

    def body(dq_ref, dk_ref, dv_ref, gq_ref, gk_ref, dz_ref):
        for part, (d_ref, g_ref) in enumerate(((dq_ref, gq_ref), (dk_ref, gk_ref), (dv_ref, gk_ref))):
            for p in range(H // 2):
                blk = _dot(d_ref[2 * p].astype(BF16), g_ref[0], NN) + _dot(d_ref[2 * p + 1].astype(BF16), g_ref[1], NN)
                lo = part * H * FOX_D + p * DA
                dz_ref[:, lo:lo + DA] = blk.astype(BF16)

    heads = pl.BlockSpec((H, tm, DA), lambda i: (0, i, 0))
    sel = pl.BlockSpec((2, DA, DA), lambda i: (0, 0, 0))
    return pl.pallas_call(
        body, name=name, grid=(s // tm,), in_specs=[heads, heads, heads, sel, sel],
        out_specs=pl.BlockSpec((tm, 3 * H * FOX_D), lambda i: (i, 0)),
        out_shape=jax.ShapeDtypeStruct((s, 3 * H * FOX_D), BF16), compiler_params=_cp(("parallel",)),
    )(dfqa, dfka, dfva, gq, gk)


def _lane_scan(x, s, reverse):
    lane = lax.broadcasted_iota(jnp.int32, x.shape, 1)
    sh = 1
    while sh < s:
        if reverse:
            x = x + jnp.where(lane < s - sh, pltpu.roll(x, s - sh, axis=1), 0.0)
        else:
            x = x + jnp.where(lane >= sh, pltpu.roll(x, sh, axis=1), 0.0)
        sh *= 2
    return x


def _gate_fwd(z, col_block, bias, *, name):
    s = z.shape[0]

    def body(z_ref, b_ref, f_ref, c_ref):
        ft = z_ref[...].T[0:8, :]
        f_ref[...] = ft
        xg = ft + b_ref[...]
        lf = jnp.minimum(xg, 0.0) - jnp.log(1.0 + jnp.exp(-jnp.abs(xg)))
        c = _lane_scan(lf, s, False)
        hi = c.astype(BF16).astype(F32)
        r = c - hi
        mid = r.astype(BF16).astype(F32)
        lo = r - mid
        c_ref[...] = jnp.concatenate([hi, mid, lo, jnp.zeros_like(hi)], axis=0).astype(BF16)

    return pl.pallas_call(
        body, name=name, grid=(1,),
        in_specs=[pl.BlockSpec((s, 128), lambda i: (0, col_block)), pl.BlockSpec((8, 1), lambda i: (0, 0))],
        out_specs=[pl.BlockSpec((8, s), lambda i: (0, 0)), pl.BlockSpec((32, s), lambda i: (0, 0))],
        out_shape=[jax.ShapeDtypeStruct((8, s), F32), jax.ShapeDtypeStruct((32, s), BF16)],
        compiler_params=_cp(("arbitrary",)))(z, bias)


def _gate_bwd(ft, bias, dc, *, name):
    s = ft.shape[1]

    def body(f_ref, b_ref, dc_ref, df_ref, db_ref):
        xg = f_ref[...] + b_ref[...]
        dlf = _lane_scan(dc_ref[...], s, True)
        df = dlf * _sigmoid(-xg)
        db_ref[...] = jnp.sum(df, axis=-1, keepdims=True)
        df_ref[...] = jnp.concatenate([df, jnp.zeros((DA - 8, s), F32)], axis=0).T

    return pl.pallas_call(body, name=name, out_shape=[jax.ShapeDtypeStruct((s, DA), F32), jax.ShapeDtypeStruct((8, 1), F32)],
                          compiler_params=_cp())(ft, bias, dc)


def _pool_lane_consts(tm, i):
    lane = lax.broadcasted_iota(jnp.int32, (tm, POOL_W), 1)
    tok = lax.broadcasted_iota(jnp.int32, (tm, POOL_W), 0) + i * tm
    win = jnp.where(lane < 64, 2, jnp.where(lane < 128, 4, jnp.where(lane < 192, 8, 16)))
    cnt = jnp.minimum(tok + 1, win).astype(F32)
    return lane, tok, cnt


def _pick_window(lane, s2, s4, s8, s16):
    return jnp.where(lane < 64, s2, jnp.where(lane < 128, s4, jnp.where(lane < 192, s8, s16)))


def _pool_fwd(z, col_block, bd, scale, *, name, tm=512):
    s = z.shape[0]
    tm = min(tm, s)
    hb = tm // POOL_HALO

    def body(u_ref, halo_ref, bd_ref, sc_ref, y_ref, p_ref, buf):
        i = pl.program_id(0)
        buf[0:POOL_HALO, :] = halo_ref[...] * (i > 0).astype(F32)
        buf[POOL_HALO:, :] = u_ref[...]

        def back(k):
            return buf[POOL_HALO - k:POOL_HALO - k + tm, :]

        u = u_ref[...]
        s2 = u + back(1)
        s4 = s2 + back(2) + back(3)
        s8 = s4 + back(4) + back(5) + back(6) + back(7)
        s16 = s8
        for k in range(8, 16):
            s16 = s16 + back(k)
        lane, _, cnt = _pool_lane_consts(tm, i)
        pooled = (_pick_window(lane, s2, s4, s8, s16) / cnt - u).astype(BF16)
        p_ref[...] = pooled
        y_ref[...] = _dot(pooled, bd_ref[...], NN) * sc_ref[...]

    return pl.pallas_call(
        body, name=name, grid=(s // tm,),
        in_specs=[pl.BlockSpec((tm, POOL_W), lambda i: (i, col_block)),
                  pl.BlockSpec((POOL_HALO, POOL_W), lambda i: (jnp.maximum(i * hb - 1, 0), col_block)),
                  pl.BlockSpec((POOL_W, POOL_W), lambda i: (0, 0)), pl.BlockSpec((1, POOL_W), lambda i: (0, 0))],
        out_specs=[pl.BlockSpec((tm, POOL_W), lambda i: (i, 0)), pl.BlockSpec((tm, POOL_W), lambda i: (i, 0))],
        out_shape=[jax.ShapeDtypeStruct((s, POOL_W), F32), jax.ShapeDtypeStruct((s, POOL_W), BF16)],
        scratch_shapes=[pltpu.VMEM((tm + POOL_HALO, POOL_W), F32)],
        compiler_params=_cp(("parallel",)),
    )(z, z, bd, scale.reshape(1, POOL_W))


def _pool_bwd_a(dy, pooled, bd, scale, *, name, tm=512):
    s = dy.shape[0]
    tm = min(tm, s)

    def body(dy_ref, p_ref, bd_ref, sc_ref, dq_ref, dys_ref, dsc_ref):
        i = pl.program_id(0)
        dyv = dy_ref[...]
        y0 = _dot(p_ref[...], bd_ref[...], NN)
        dys = (dyv * sc_ref[...]).astype(BF16)
        dys_ref[...] = dys
        dp = _dot(dys, bd_ref[...], NT)
        _, _, cnt = _pool_lane_consts(tm, i)
        dq_ref[:, 0:POOL_W] = dp / cnt
        dq_ref[:, POOL_W:] = dp

        @pl.when(i == 0)
        def _():
            dsc_ref[...] = jnp.zeros_like(dsc_ref)

        dsc_ref[...] += jnp.sum(dyv * y0, axis=0, keepdims=True)

    row = lambda i: (i, 0)
    dq, dys, dsc = pl.pallas_call(
        body, name=name, grid=(s // tm,),
        in_specs=[pl.BlockSpec((tm, POOL_W), row), pl.BlockSpec((tm, POOL_W), row),
                  pl.BlockSpec((POOL_W, POOL_W), lambda i: (0, 0)), pl.BlockSpec((1, POOL_W), lambda i: (0, 0))],
        out_specs=[pl.BlockSpec((tm, 2 * POOL_W), row), pl.BlockSpec((tm, POOL_W), row), pl.BlockSpec((1, POOL_W), lambda i: (0, 0))],
        out_shape=[jax.ShapeDtypeStruct((s, 2 * POOL_W), F32), jax.ShapeDtypeStruct((s, POOL_W), BF16),
                   jax.ShapeDtypeStruct((1, POOL_W), F32)],
        compiler_params=_cp(("arbitrary",)),
    )(dy, pooled, bd, scale.reshape(1, POOL_W))
    return dq, dys, dsc.reshape(POOL_W)


def _pool_bwd_b(dq, *, name, tm=512):
    s = dq.shape[0]
    tm = min(tm, s)
    hb = tm // POOL_HALO
    nblk = s // tm

    def body(q_ref, dp_ref, halo_ref, du_ref, buf):
        i = pl.program_id(0)
        buf[0:tm, :] = q_ref[...]
        buf[tm:, :] = halo_ref[...] * (i < nblk - 1).astype(F32)

        def ahead(k):
            return buf[k:k + tm, :]

        q = q_ref[...]
        s2 = q + ahead(1)
        s4 = s2 + ahead(2) + ahead(3)
        s8 = s4 + ahead(4) + ahead(5) + ahead(6) + ahead(7)
        s16 = s8
        for k in range(8, 16):
            s16 = s16 + ahead(k)
        lane = lax.broadcasted_iota(jnp.int32, (tm, POOL_W), 1)
        du_ref[...] = _pick_window(lane, s2, s4, s8, s16) - dp_ref[...]

    return pl.pallas_call(
        body, name=name, grid=(nblk,),
        in_specs=[pl.BlockSpec((tm, POOL_W), lambda i: (i, 0)), pl.BlockSpec((tm, POOL_W), lambda i: (i, 1)),
                  pl.BlockSpec((POOL_HALO, POOL_W), lambda i: (jnp.minimum((i + 1) * hb, nblk * hb - 1), 0))],
        out_specs=pl.BlockSpec((tm, POOL_W), lambda i: (i, 0)),
        out_shape=jax.ShapeDtypeStruct((s, POOL_W), F32),
        scratch_shapes=[pltpu.VMEM((tm + POOL_HALO, POOL_W), F32)],
        compiler_params=_cp(("parallel",)),
    )(dq, dq, dq)


def _loss_head(x, gain, target, *, name, tm=512):
    s = x.shape[0]
    tm = min(tm, s)

    def body(x_ref, g_ref, t_ref, dx_ref, dg_ref, loss_ref):
        xv = x_ref[...]
        r = lax.rsqrt(jnp.mean(xv * xv, axis=-1, keepdims=True) + EPS)
        xh = xv * r
        err = xh * g_ref[...] - t_ref[...]
        dy = err * (1.0 / D)
        a = dy * g_ref[...]
        dx_ref[...] = r * a - xh * (r * jnp.mean(a * xh, axis=-1, keepdims=True))

        @pl.when(pl.program_id(0) == 0)
        def _():
            dg_ref[...] = jnp.zeros_like(dg_ref)
            loss_ref[...] = jnp.zeros_like(loss_ref)

        dg_ref[...] += jnp.sum(dy * xh, axis=0, keepdims=True)
        part = 0.5 * jnp.sum(jnp.mean(err * err, axis=-1, keepdims=True), axis=0, keepdims=True)
        loss_ref[...] += jnp.broadcast_to(part, loss_ref.shape)

    row = lambda i: (i, 0)
    dx, dg, loss = pl.pallas_call(
        body, name=name, grid=(s // tm,),
        in_specs=[pl.BlockSpec((tm, D), row), pl.BlockSpec((1, D), lambda i: (0, 0)), pl.BlockSpec((tm, D), row)],
        out_specs=[pl.BlockSpec((tm, D), row), pl.BlockSpec((1, D), lambda i: (0, 0)), pl.BlockSpec((1, 128), lambda i: (0, 0))],
        out_shape=[jax.ShapeDtypeStruct((s, D), F32), jax.ShapeDtypeStruct((1, D), F32), jax.ShapeDtypeStruct((1, 128), F32)],
        compiler_params=_cp(("arbitrary",)),
    )(x, gain.reshape(1, D), target)
    return dx, dg.reshape(D), loss[0, 0]


def _adamw(w, g, m, v, *, name, tr=512):
    rows, cols = w.shape
    tr = min(tr, rows)
    assert rows % tr == 0, (name, rows, tr)
    c_m = 1.0 - ADAM_B1
    c_v = 1.0 - ADAM_B2
    bc1 = 1.0 - ADAM_B1 ** ADAM_STEP
    bc2 = 1.0 - ADAM_B2 ** ADAM_STEP

    def body(w_ref, g_ref, m_ref, v_ref, d_ref, mo_ref, vo_ref):
        gv = g_ref[...]
        mn = ADAM_B1 * m_ref[...] + c_m * gv
        vn = ADAM_B2 * v_ref[...] + c_v * (gv * gv)
        mo_ref[...] = mn
        vo_ref[...] = vn
        d_ref[...] = -ADAM_LR * ((mn / bc1) / (jnp.sqrt(vn / bc2) + ADAM_EPS) + ADAM_WD * w_ref[...])

    spec = pl.BlockSpec((tr, cols), lambda i: (i, 0))
    return pl.pallas_call(body, name=name, grid=(rows // tr,), in_specs=[spec] * 4, out_specs=[spec] * 3,
                          out_shape=[jax.ShapeDtypeStruct((rows, cols), F32)] * 3,
                          compiler_params=_cp(("parallel",)))(w, g, m, v)


def _adamw_layer(w, g, m, v, layer, prev, *, name, tr):
    rows, cols = g.shape
    assert rows % tr == 0 and w.shape == (DEPTH * rows, cols), (name, w.shape, g.shape, tr)
    nblk = rows // tr
    c_m = 1.0 - ADAM_B1
    c_v = 1.0 - ADAM_B2
    bc1 = 1.0 - ADAM_B1 ** ADAM_STEP
    bc2 = 1.0 - ADAM_B2 ** ADAM_STEP
    n_prev = 0 if prev is None else 4

    def body(*refs):
        w_ref, g_ref, m_ref, v_ref = refs[:4]
        d_ref, mo_ref, vo_ref, go_ref = refs[4 + n_prev:]
        gv = g_ref[...]
        mn = ADAM_B1 * m_ref[...] + c_m * gv
        vn = ADAM_B2 * v_ref[...] + c_v * (gv * gv)
        mo_ref[...] = mn
        vo_ref[...] = vn
        go_ref[...] = gv
        d_ref[...] = -ADAM_LR * ((mn / bc1) / (jnp.sqrt(vn / bc2) + ADAM_EPS) + ADAM_WD * w_ref[...])

    stacked = pl.BlockSpec((tr, cols), lambda i: (layer * nblk + i, 0))
    args = [w, g, m, v] + ([] if prev is None else list(prev))
    return pl.pallas_call(
        body, name=name, grid=(nblk,),
        in_specs=[stacked, pl.BlockSpec((tr, cols), lambda i: (i, 0)), stacked, stacked] + [ANY_SPEC] * n_prev,
        out_specs=[stacked] * 4, out_shape=[jax.ShapeDtypeStruct(w.shape, F32)] * 4,
        input_output_aliases={4 + k: k for k in range(n_prev)},
        compiler_params=_cp(("parallel",)))(*args)


def _position():
    return jnp.stack([lax.axis_index("c"), 2 * lax.axis_index("x") + lax.axis_index("y")]).astype(jnp.int32)


SUM_ROW_TILES = 2


def _sum2_bf16(pos, fulls, sibs, *, name):
    n = len(fulls)
    nb = SUM_ROW_TILES

    def body(pos_ref, *refs):
        for t in range(n):
            refs[2 * n + t][...] = (refs[t][...] + refs[n + t][...]).astype(BF16)

    in_specs, sib_specs = [], []
    for sb in sibs:
        _, half, cols = sb.shape
        tr = half // nb
        assert half % nb == 0 and tr % 16 == 0, sb.shape
        in_specs.append(pl.BlockSpec((None, tr, cols), lambda j, i, p: (j, p[0] * nb + i, 0)))
        sib_specs.append(pl.BlockSpec((None, tr, cols), lambda j, i, p: (j, i, 0)))
    return pl.pallas_call(
        body, name=name,
        grid_spec=pltpu.PrefetchScalarGridSpec(num_scalar_prefetch=1, grid=(N_CHIPS, nb), in_specs=in_specs + sib_specs,
                                               out_specs=sib_specs),
        out_shape=[jax.ShapeDtypeStruct(sb.shape, BF16) for sb in sibs],
        compiler_params=_cp(("parallel", "parallel")))(pos, *fulls, *sibs)


def _sum5(pos, fulls, sibs, recvs, *, name):
    n = len(fulls)
    nb = SUM_ROW_TILES

    def body(pos_ref, *refs):
        for t in range(n):
            acc = refs[t][...] + refs[n + t][...]
            for kk in range(3):
                acc = acc + refs[2 * n + t][kk].astype(F32)
            refs[3 * n + t][...] = acc

    f_specs, s_specs, r_specs, o_specs = [], [], [], []
    for f in fulls:
        _, rows, cols = f.shape
        tr = rows // 2 // nb
        f_specs.append(pl.BlockSpec((None, tr, cols), lambda i, p: (p[1], p[0] * nb + i, 0)))
        s_specs.append(pl.BlockSpec((None, tr, cols), lambda i, p: (p[1], i, 0)))
        r_specs.append(pl.BlockSpec((3, tr, cols), lambda i, p: (0, i, 0)))
        o_specs.append(pl.BlockSpec((tr, cols), lambda i, p: (p[0] * nb + i, 0)))
    return pl.pallas_call(
        body, name=name,
        grid_spec=pltpu.PrefetchScalarGridSpec(num_scalar_prefetch=1, grid=(nb,), in_specs=f_specs + s_specs + r_specs,
                                               out_specs=o_specs),
        out_shape=[jax.ShapeDtypeStruct(f.shape[1:], F32) for f in fulls],
        compiler_params=_cp(("parallel",)))(pos, *fulls, *sibs, *recvs)


def _place():
    x, y, c = lax.axis_index("x"), lax.axis_index("y"), lax.axis_index("c")
    chips = [(1 - x, y), (x, 1 - y), (1 - x, 1 - y)]
    return x, y, c, 2 * x + y, chips


SEM_SPEC = pl.BlockSpec(memory_space=pltpu.SEMAPHORE)
ANY_SPEC = pl.BlockSpec(memory_space=pl.ANY)


def _gather_copies(ins, outs, send_i, recv_i, send_o, recv_o):
    x, y, c, me, chips = _place()
    n = len(ins)
    started, awaited = [], []
    for t in range(n):
        half = ins[t].shape[0] // 2
        mine = pl.ds(c * half, half)
        started.append(pltpu.make_async_remote_copy(
            src_ref=ins[t], dst_ref=outs[t].at[me], send_sem=send_o.at[t], recv_sem=recv_o.at[t],
            device_id=(x, y, 1 - c), device_id_type=MESH))
        awaited.append(started[-1])
        for kk, (px, py) in enumerate(chips):
            started.append(pltpu.make_async_remote_copy(
                src_ref=ins[t].at[mine], dst_ref=outs[t].at[me, mine], send_sem=send_i.at[t * 3 + kk],
                recv_sem=recv_i.at[t * 3 + kk], device_id=(px, py, c), device_id_type=MESH))
            awaited.append(pltpu.make_async_remote_copy(
                src_ref=ins[t].at[mine], dst_ref=outs[t].at[2 * px + py, mine], send_sem=send_i.at[t * 3 + kk],
                recv_sem=recv_i.at[t * 3 + kk], device_id=(px, py, c), device_id_type=MESH))
    return started, awaited


def _forward_copies(outs, send_d, recv_d):
    x, y, c, me, chips = _place()
    started, awaited = [], []
    for t in range(len(outs)):
        half = outs[t].shape[1] // 2
        for kk, (px, py) in enumerate(chips):
            for lst, hc in ((started, c), (awaited, 1 - c)):
                blk = outs[t].at[2 * px + py, pl.ds(hc * half, half)]
                lst.append(pltpu.make_async_remote_copy(src_ref=blk, dst_ref=blk, send_sem=send_d.at[t * 3 + kk],
                                                        recv_sem=recv_d.at[t * 3 + kk], device_id=(x, y, 1 - c), device_id_type=MESH))
    return started, awaited


def _gather_blocking(shards):
    n = len(shards)

    def body(*refs):
        ins, outs = refs[:n], refs[n:2 * n]
        send_i, recv_i, send_d, recv_d, send_o, recv_o = refs[2 * n:]
        started, awaited = _gather_copies(ins, outs, send_i, recv_i, send_o, recv_o)
        for cp in started:
            cp.start()
        for cp in awaited:
            cp.wait_recv()
        fwd, fwd_in = _forward_copies(outs, send_d, recv_d)
        for cp in fwd:
            cp.start()
        for cp in fwd_in:
            cp.wait_recv()
        for cp in started + fwd:
            cp.wait_send()

    return pl.pallas_call(
        body, name="gather_first", in_specs=[HBM_SPEC] * n, out_specs=[HBM_SPEC] * n,
        out_shape=[jax.ShapeDtypeStruct((N_CHIPS,) + s.shape, s.dtype) for s in shards],
        scratch_shapes=[pltpu.SemaphoreType.DMA((3 * n,)), pltpu.SemaphoreType.DMA((3 * n,)),
                        pltpu.SemaphoreType.DMA((3 * n,)), pltpu.SemaphoreType.DMA((3 * n,)),
                        pltpu.SemaphoreType.DMA((n,)), pltpu.SemaphoreType.DMA((n,))],
    )(*shards)


def _gather_start(shards, after, tag):
    n = len(shards)

    def body(*refs):
        ins = refs[:n]
        send_i, recv_i, send_o, recv_o = refs[2 * n + 1:2 * n + 5]
        outs = refs[3 * n + 5:4 * n + 5]
        token = refs[4 * n + 5]
        started, _ = _gather_copies(ins, outs, send_i, recv_i, send_o, recv_o)
        for cp in started:
            cp.start()
        token[...] = jnp.zeros_like(token)

    lands = [lax.empty((N_CHIPS,) + s.shape, s.dtype) for s in shards]
    sems = [pltpu.SemaphoreType.DMA((3 * n,)), pltpu.SemaphoreType.DMA((3 * n,)), pltpu.SemaphoreType.DMA((n,)), pltpu.SemaphoreType.DMA((n,))]
    res = pl.pallas_call(
        body, name=f"gather_{tag}_start",
        in_specs=[HBM_SPEC] * (2 * n) + [ANY_SPEC],
        out_specs=[SEM_SPEC] * 4 + [HBM_SPEC] * (2 * n) + [pl.BlockSpec(memory_space=pltpu.VMEM)],
        out_shape=sems + [jax.ShapeDtypeStruct(s.shape, s.dtype) for s in shards]
        + [jax.ShapeDtypeStruct(a.shape, a.dtype) for a in lands] + [jax.ShapeDtypeStruct((8, 128), F32)],
        input_output_aliases={t: 4 + t for t in range(2 * n)},
        compiler_params=pltpu.CompilerParams(has_side_effects=pltpu.SideEffectType.DATAFLOW_SIDE_EFFECTING),
    )(*[pltpu.with_memory_space_constraint(s, pltpu.HBM) for s in shards],
      *[pltpu.with_memory_space_constraint(a, pltpu.HBM) for a in lands], after)
    return res[:4], res[4:4 + n], res[4 + n:4 + 2 * n], res[-1]


def _gather_wait(sems, shards_thru, lands_thru, after, tag):
    n = len(shards_thru)

    def body(*refs):
        ins, outs_in = refs[:n], refs[n:2 * n]
        send_i, recv_i, send_o, recv_o = refs[2 * n:2 * n + 4]
        started, awaited = _gather_copies(ins, outs_in, send_i, recv_i, send_o, recv_o)
        for cp in started:
            cp.wait_send()
        for cp in awaited:
            cp.wait_recv()

    res = pl.pallas_call(
        body, name=f"gather_{tag}_wait",
        in_specs=[HBM_SPEC] * (2 * n) + [SEM_SPEC] * 4 + [ANY_SPEC],
        out_specs=[HBM_SPEC] * (2 * n),
        out_shape=[jax.ShapeDtypeStruct(a.shape, a.dtype) for a in list(shards_thru) + list(lands_thru)],
        input_output_aliases={t: t for t in range(2 * n)},
        compiler_params=pltpu.CompilerParams(has_side_effects=pltpu.SideEffectType.DATAFLOW_SIDE_EFFECTING),
    )(*shards_thru, *lands_thru, *sems, after)
    return res[n:]


def _gather_forward(lands, tag):
    n = len(lands)

    def body(*refs):
        outs = refs[n:2 * n]
        send_d, recv_d = refs[2 * n:]
        fwd, fwd_in = _forward_copies(outs, send_d, recv_d)
        for cp in fwd:
            cp.start()
        for cp in fwd_in:
            cp.wait_recv()
        for cp in fwd:
            cp.wait_send()

    return pl.pallas_call(
        body, name=f"gather_{tag}_forward", in_specs=[HBM_SPEC] * n, out_specs=[HBM_SPEC] * n,
        out_shape=[jax.ShapeDtypeStruct(a.shape, a.dtype) for a in lands],
        input_output_aliases={t: t for t in range(n)},
        scratch_shapes=[pltpu.SemaphoreType.DMA((3 * n,)), pltpu.SemaphoreType.DMA((3 * n,))],
    )(*lands)


def _gather_forward_start(lands, tag):
    n = len(lands)

    def body(*refs):
        send_d, recv_d = refs[n], refs[n + 1]
        fwd, _ = _forward_copies(refs[n + 2:2 * n + 2], send_d, recv_d)
        for cp in fwd:
            cp.start()
        refs[2 * n + 2][...] = jnp.zeros_like(refs[2 * n + 2])

    res = pl.pallas_call(
        body, name=f"gather_{tag}_forward_start", in_specs=[HBM_SPEC] * n,
        out_specs=[SEM_SPEC] * 2 + [HBM_SPEC] * n + [pl.BlockSpec(memory_space=pltpu.VMEM)],
        out_shape=[pltpu.SemaphoreType.DMA((3 * n,)), pltpu.SemaphoreType.DMA((3 * n,))]
        + [jax.ShapeDtypeStruct(a.shape, a.dtype) for a in lands] + [jax.ShapeDtypeStruct((8, 128), F32)],
        input_output_aliases={t: 2 + t for t in range(n)},
        compiler_params=pltpu.CompilerParams(has_side_effects=pltpu.SideEffectType.DATAFLOW_SIDE_EFFECTING),
    )(*[pltpu.with_memory_space_constraint(a, pltpu.HBM) for a in lands])
    return res[:2], res[2:2 + n], res[-1]


def _gather_forward_wait(sems, lands_thru, after, tag):
    n = len(lands_thru)

    def body(*refs):
        fwd, fwd_in = _forward_copies(refs[:n], refs[n], refs[n + 1])
        for cp in fwd:
            cp.wait_send()
        for cp in fwd_in:
            cp.wait_recv()

    return pl.pallas_call(
        body, name=f"gather_{tag}_forward_wait", in_specs=[HBM_SPEC] * n + [SEM_SPEC] * 2 + [ANY_SPEC],
        out_specs=[HBM_SPEC] * n, out_shape=[jax.ShapeDtypeStruct(a.shape, a.dtype) for a in lands_thru],
        input_output_aliases={t: t for t in range(n)},
        compiler_params=pltpu.CompilerParams(has_side_effects=pltpu.SideEffectType.DATAFLOW_SIDE_EFFECTING),
    )(*lands_thru, *sems, after)


def _stage1_copies(ins, sib, send, recv):
    x, y, c, me, chips = _place()
    cps = []
    for t in range(len(ins)):
        rows = ins[t].shape[1] // 2
        cps.append(pltpu.make_async_remote_copy(
            src_ref=ins[t].at[:, pl.ds((1 - c) * rows, rows), :], dst_ref=sib[t], send_sem=send.at[t],
            recv_sem=recv.at[t], device_id=(x, y, 1 - c), device_id_type=MESH))
    return cps


def _split_start(copies_fn, srcs, land_shapes, n_sems, tag):
    n = len(srcs)

    def body(*refs):
        send, recv = refs[2 * n:2 * n + 2]
        for cp in copies_fn(refs[:n], refs[3 * n + 2:4 * n + 2], send, recv):
            cp.start()
        refs[4 * n + 2][...] = jnp.zeros_like(refs[4 * n + 2])

    lands = [lax.empty(shp, dt) for shp, dt in land_shapes]
    res = pl.pallas_call(
        body, name=tag,
        in_specs=[HBM_SPEC] * (2 * n),
        out_specs=[SEM_SPEC] * 2 + [HBM_SPEC] * (2 * n) + [pl.BlockSpec(memory_space=pltpu.VMEM)],
        out_shape=[pltpu.SemaphoreType.DMA((n_sems,)), pltpu.SemaphoreType.DMA((n_sems,))]
        + [jax.ShapeDtypeStruct(p.shape, p.dtype) for p in srcs]
        + [jax.ShapeDtypeStruct(a.shape, a.dtype) for a in lands] + [jax.ShapeDtypeStruct((8, 128), F32)],
        input_output_aliases={t: 2 + t for t in range(2 * n)},
        compiler_params=pltpu.CompilerParams(has_side_effects=pltpu.SideEffectType.DATAFLOW_SIDE_EFFECTING),
    )(*[pltpu.with_memory_space_constraint(p, pltpu.HBM) for p in srcs],
      *[pltpu.with_memory_space_constraint(a, pltpu.HBM) for a in lands])
    return res[:2], res[2:2 + n], res[2 + n:2 + 2 * n], res[-1]


def _split_wait(copies_fn, sems, srcs_thru, lands_thru, after, tag):
    n = len(srcs_thru)

    def body(*refs):
        for cp in copies_fn(refs[:n], refs[n:2 * n], refs[2 * n], refs[2 * n + 1]):
            cp.wait()

    res = pl.pallas_call(
        body, name=tag,
        in_specs=[HBM_SPEC] * (2 * n) + [SEM_SPEC] * 2 + [ANY_SPEC],
        out_specs=[HBM_SPEC] * (2 * n),
        out_shape=[jax.ShapeDtypeStruct(a.shape, a.dtype) for a in list(srcs_thru) + list(lands_thru)],
        input_output_aliases={t: t for t in range(2 * n)},
        compiler_params=pltpu.CompilerParams(has_side_effects=pltpu.SideEffectType.DATAFLOW_SIDE_EFFECTING),
    )(*srcs_thru, *lands_thru, *sems, after)
    return res[:n], res[n:]


def _stage2_copies(ps, rcv, send, recv):
    x, y, c, me, chips = _place()
    return [pltpu.make_async_remote_copy(
        src_ref=ps[t].at[2 * px + py], dst_ref=rcv[t].at[kk], send_sem=send.at[t * 3 + kk],
        recv_sem=recv.at[t * 3 + kk], device_id=(px, py, c), device_id_type=MESH)
        for t in range(len(ps)) for kk, (px, py) in enumerate(chips)]


def _reduce_stage3(reduced, tag):
    n = len(reduced)

    def body(*refs):
        outs = refs[n:2 * n]
        send, recv = refs[2 * n:]
        x, y, c, me, chips = _place()
        cps = []
        for t in range(n):
            rows = outs[t].shape[0] // 2
            mine = outs[t].at[pl.ds(c * rows, rows), :]
            cp = pltpu.make_async_remote_copy(src_ref=mine, dst_ref=mine, send_sem=send.at[t], recv_sem=recv.at[t],
                                              device_id=(x, y, 1 - c), device_id_type=MESH)
            cp.start()
            cps.append(cp)
        for cp in cps:
            cp.wait()

    return pl.pallas_call(
        body, name="reduce_stage3_" + tag, in_specs=[HBM_SPEC] * n, out_specs=[HBM_SPEC] * n,
        out_shape=[jax.ShapeDtypeStruct(r.shape, r.dtype) for r in reduced],
        input_output_aliases={t: t for t in range(n)},
        scratch_shapes=[pltpu.SemaphoreType.DMA((n,)), pltpu.SemaphoreType.DMA((n,))],
    )(*reduced)


def _allreduce_small(v):
    rows, cols = v.shape

    def body(v_ref, o_ref, buf, send, recv, loc):
        x, y, c, me, chips = _place()
        mine = 4 * x + 2 * y + c
        lc = pltpu.make_async_copy(v_ref, buf.at[mine], loc)
        lc.start()
        peers = []
        for fx in range(2):
            for fy in range(2):
                for fc in range(2):
                    if fx or fy or fc:
                        peers.append((fx, fy, fc))
        cps = []
        for kk, (fx, fy, fc) in enumerate(peers):
            to = (x ^ fx, y ^ fy, c ^ fc)
            cp = pltpu.make_async_remote_copy(src_ref=v_ref, dst_ref=buf.at[mine], send_sem=send.at[kk], recv_sem=recv.at[kk],
                                              device_id=to, device_id_type=MESH)
            cp.start()
            cps.append((cp, to))
        for kk, (cp, to) in enumerate(cps):
            src = 4 * to[0] + 2 * to[1] + to[2]
            pltpu.make_async_remote_copy(src_ref=v_ref, dst_ref=buf.at[src], send_sem=send.at[kk], recv_sem=recv.at[kk],
                                         device_id=to, device_id_type=MESH).wait_recv()
        for cp, _ in cps:
            cp.wait_send()
        lc.wait()
        acc = buf[0]
        for d in range(1, 8):
            acc = acc + buf[d]
        o_ref[...] = acc

    return pl.pallas_call(
        body, name="allreduce_small", in_specs=[pl.BlockSpec(memory_space=pltpu.VMEM)],
        out_specs=pl.BlockSpec(memory_space=pltpu.VMEM), out_shape=jax.ShapeDtypeStruct((rows, cols), F32),
        scratch_shapes=[pltpu.VMEM((8, rows, cols), F32), pltpu.SemaphoreType.DMA((7,)), pltpu.SemaphoreType.DMA((7,)),
                        pltpu.SemaphoreType.DMA],
        compiler_params=pltpu.CompilerParams(vmem_limit_bytes=VMEM_LIMIT_V7X),
    )(v)


def _pad_w_in(w):
    z = lambda n: jnp.zeros(w.shape[:-1] + (n,), w.dtype)
    return jnp.concatenate([w[..., 0:384], z(64), w[..., 384:416], z(32), w[..., 416:1824],
                            w[..., 1824:1830], z(58), w[..., 400:416], w[..., 384:400], z(32)], axis=-1)


def _unpad_w_in(g):
    x1 = g[..., 448:464] + g[..., Z_F + 80:Z_F + 96]
    x2 = g[..., 464:480] + g[..., Z_F + 64:Z_F + 80]
    return jnp.concatenate([g[..., 0:384], x1, x2, g[..., 512:1920], g[..., 1920:1926]], axis=-1)


def _block_diag(pw):
    out = jnp.zeros((POOL_W, POOL_W), pw.dtype)
    for g in range(4):
        out = out.at[g * 64:(g + 1) * 64, g * 64:(g + 1) * 64].set(pw[g])
    return out


def _rope_tables(s):
    inv_freq = ROPE_THETA ** (-jnp.arange(0, ROPE, 2, dtype=F32) / ROPE)
    ang = jnp.arange(s, dtype=jnp.int32).astype(F32)[:, None] * inv_freq[None, :]
    cos, sin = jnp.cos(ang), jnp.sin(ang)
    zero = lambda n: jnp.zeros((s, n), F32)
    ck = jnp.concatenate([zero(NOPE), cos, cos, zero(DA - NOPE - ROPE)], axis=1)
    sk = jnp.concatenate([zero(NOPE), -sin, sin, zero(DA - NOPE - ROPE)], axis=1)
    cq = jnp.concatenate([jnp.ones((s, NOPE), F32), cos, cos, zero(DA - NOPE - ROPE)], axis=1) * SCALE_MLA
    return dict(cq=cq, sq=sk * SCALE_MLA, ck=ck, sk=sk)


def _mix_fwd(l, x1, wts, sm, tabs):
    z, h2 = _norm_mm(x1, 0, sm["mix_norm"][l], wts["w_in"][l], name=f"mix_in_{l}")
    qa, qn = _mla_q_prep(z, sm["q_a_norm"][l], wts["wq_a"][l], wts["wq_b"][l], tabs["cq"], tabs["sq"], name=f"mla_q_{l}")
    ka, va, kvn = _mla_kv_prep(z, sm["kv_a_norm"][l], wts["wk"][l], wts["wv"][l], tabs["ck"], tabs["sk"], name=f"mla_kv_{l}")
    oa, lse_a = _attn_fwd(qa, ka, va, VDIM, name=f"mla_attn_{l}")

    bd = _block_diag(wts["pool_w"][l]).astype(BF16)
    yb, pooled = _pool_fwd(z, Z_POOL // POOL_W, bd, sm["pool_scale"][l], name=f"pool_{l}")

    fb = jnp.pad(sm["fox_b_f"][l], (0, 8 - H)).reshape(8, 1)
    ft, c3t = _gate_fwd(z, Z_F // DA, fb, name=f"fox_gate_{l}")
    fqa, fka, fva = _fox_prep(z, c3t, name=f"fox_prep_{l}")
    oc, lse_c = _attn_fwd(fqa, fka, fva, FOX_D, name=f"fox_attn_{l}")

    x2, cat = _mix_out(oa, yb, oc, wts["w_out"][l], x1, name=f"mix_out_{l}")
    saved = dict(z=z, h2=h2, qn=qn, kvn=kvn, qa=qa, ka=ka, va=va, oa=oa, lse_a=lse_a, bd=bd, pooled=pooled,
                 fqa=fqa, fka=fka, fva=fva, ft=ft, fb=fb, oc=oc, lse_c=lse_c, cat=cat)
    return x2, saved


def _mix_bwd(l, x1, dx2, sv, wts, sm, tabs, tok=None):
    s = x1.shape[0]
    g = {}
    dx2b = (dx2 if tok is None else dx2 + tok).astype(BF16)
    g["w_out"] = _mm(sv["cat"], dx2b, "tn", name=f"d_w_out_{l}", tm=1024, tn=1024, tk=DW_TOKENS)
    doa, doc, dyb, dl_a, dl_c = _mix_out_bwd(dx2b, wts["w_out"][l], sv["oa"], sv["oc"], name=f"mix_out_bwd_{l}")

    dfqa, dfka, dfva, dcq, dck = _attn_bwd(sv["fqa"], sv["fka"], sv["fva"], doc, sv["lse_c"], dl_c, True, name=f"fox_attn_bwd_{l}")
    dfox = _fox_bwd_prep(dfqa, dfka, dfva, name=f"fox_bwd_prep_{l}")
    dc = jnp.pad(dcq.reshape(H, s) + dck.reshape(H, s), ((0, 8 - H), (0, 0)))
    dft, dfb = _gate_bwd(sv["ft"], sv["fb"], dc, name=f"fox_gate_bwd_{l}")
    g["fox_b_f"] = dfb[:H, 0]

    dq, dys, g["pool_scale"] = _pool_bwd_a(dyb, sv["pooled"], sv["bd"], sm["pool_scale"][l], name=f"pool_bwd_a_{l}")
    du = _pool_bwd_b(dq, name=f"pool_bwd_b_{l}")
    dbd = _mm(sv["pooled"], dys, "tn", name=f"d_pool_w_{l}")
    g["pool_w"] = jnp.stack([dbd[i * 64:(i + 1) * 64, i * 64:(i + 1) * 64] for i in range(4)])

    dqa_, dka_, dva_ = _attn_bwd(sv["qa"], sv["ka"], sv["va"], doa, sv["lse_a"], dl_a, False, name=f"mla_attn_bwd_{l}")
    dqab, dkv, dz3, dz15 = _mla_bwd_prep(dqa_, dka_, dva_, dft, tabs["cq"], tabs["sq"], tabs["ck"], tabs["sk"],
                                         name=f"mla_bwd_prep_{l}")
    wq_ab = jnp.concatenate([wts["wq_a"][l], wts["wq_b"][l]], axis=1)
    wkv = jnp.concatenate([wts["wk"][l], wts["wv"][l]], axis=1)
    dwq = _mm(sv["qn"], dqab, "tn", name=f"d_w_q_b_{l}", tn=768, tk=DW_TOKENS).reshape(Q_RANK, 2, H, DA)
    dwkv = _mm(sv["kvn"], dkv, "tn", name=f"d_w_kv_b_{l}", tn=768, tk=DW_TOKENS).reshape(KV_RANK, 2, H, DA)
    da, db = dwq[:, 0], dwq[:, 1]
    swapped = jnp.concatenate([jnp.zeros((Q_RANK, H, NOPE), F32), db[..., NOPE + HALF_ROPE:NOPE + ROPE],
                               db[..., NOPE:NOPE + HALF_ROPE]], axis=-1)
    g["w_q_b"] = (da[..., :NOPE + ROPE] + swapped).reshape(Q_RANK, H * (NOPE + ROPE))
    g["w_kv_b"] = jnp.concatenate([dwkv[:, 0, :, :NOPE], dwkv[:, 1, :, :VDIM]], axis=-1).reshape(KV_RANK, H * (NOPE + VDIM))
    dqa, g["q_a_norm"] = _rmsnorm_bwd(sv["z"], Z_QA // Q_RANK, sm["q_a_norm"][l], dqab, wq_ab, name=f"q_a_norm_bwd_{l}")
    dkva, g["kv_a_norm"] = _rmsnorm_bwd(sv["z"], Z_KVA // KV_RANK, sm["kv_a_norm"][l], dkv, wkv, name=f"kv_a_norm_bwd_{l}")

    dz = jnp.concatenate([dqa.astype(BF16), dkva.astype(BF16), dz3, du.astype(BF16), dfox, dz15], axis=1)
    g["w_in"] = _mm(sv["h2"], dz, "tn", name=f"d_w_in_{l}", tm=1024, tn=1024, tk=DW_TOKENS)
    dx1, g["mix_norm"] = _rmsnorm_bwd(x1, 0, sm["mix_norm"][l], dz, wts["w_in"][l], dx2, name=f"mix_norm_bwd_{l}")
    return dx1, g


DW_TOKENS = 2048


def _local_step(x, target, wts, sm, late_weights=None, grads_ready=None):
    s = x.shape[0]
    tabs = _rope_tables(s)
    acts = []
    xs = x
    for l in range(DEPTH):
        if l == 1 and late_weights is not None:
            sm = late_weights("layer1", xs, sm)
        x1, gu1, act1 = _ffn_fwd(xs, sm["ffn1_norm"][l], wts["ffn1_w_gu"][l], wts["ffn1_w_d2"][l], name=f"ffn1_fwd_{l}")
        if l == 0 and late_weights is not None:
            sm = late_weights("ffn1", x1, sm)
        x2, sv = _mix_fwd(l, x1, wts, sm, tabs)
        if l == 0 and late_weights is not None:
            sm = late_weights("mix", x2, sm)
        x3, gu2, act2 = _ffn_fwd(x2, sm["ffn2_norm"][l], wts["ffn2_w_gu"][l], wts["ffn2_w_d2"][l], name=f"ffn2_fwd_{l}")
        acts.append((xs, gu1, act1, x1, sv, x2, gu2, act2))
        xs = x3
    dx, g_final, loss = _loss_head(xs, sm["final_norm"], target, name="loss_head")
    grads = [dict() for _ in range(DEPTH)]
    for l in reversed(range(DEPTH)):
        x0, gu1, act1, x1, sv, x2, gu2, act2 = acts[l]
        g = grads[l]
        dx, dgu, hh, dy, g["ffn2_norm"] = _ffn_bwd(x2, dx, gu2, sm["ffn2_norm"][l], wts["ffn2_w_gu"][l], wts["ffn2_w_d2"][l],
                                                   name=f"ffn2_bwd_{l}")
        g["ffn2_w_down"] = _mm(act2, dy, "tn", name=f"d_ffn2_w_down_{l}", tm=FF_SHARD, tn=1024, tk=DW_TOKENS)
        g["ffn2_w_gu"] = _mm(hh, dgu, "tn", name=f"d_ffn2_w_gu_{l}", tm=1024, tn=FF_SHARD, tk=DW_TOKENS, n_major_out=True)
        tok = None
        if grads_ready is not None:
            sm, tok = grads_ready(l, "ffn2", g, sm)
        dx, gm = _mix_bwd(l, x1, dx, sv, wts, sm, tabs, tok)
        g.update(gm)
        if grads_ready is not None:
            sm, _ = grads_ready(l, "mix", g, sm)
        dx, dgu, hh, dy, g["ffn1_norm"] = _ffn_bwd(x0, dx, gu1, sm["ffn1_norm"][l], wts["ffn1_w_gu"][l], wts["ffn1_w_d2"][l],
                                                   name=f"ffn1_bwd_{l}")
        if grads_ready is not None:
            sm, tok = grads_ready(l, "ffn1_tokens", {"dx": dx}, sm)
            if tok is not None:
                dy = dy + tok.astype(BF16)
        g["ffn1_w_down"] = _mm(act1, dy, "tn", name=f"d_ffn1_w_down_{l}", tm=FF_SHARD, tn=1024, tk=DW_TOKENS)
        g["ffn1_w_gu"] = _mm(hh, dgu, "tn", name=f"d_ffn1_w_gu_{l}", tm=1024, tn=FF_SHARD, tk=DW_TOKENS, n_major_out=True)
        if grads_ready is not None:
            sm, _ = grads_ready(l, "ffn1", g, sm)
    return loss, dx, grads, g_final


BIG = ["ffn1_w_gu", "ffn1_w_down", "w_in", "w_q_b", "w_kv_b", "w_out", "ffn2_w_gu", "ffn2_w_down"]
SMALL = ["ffn1_norm", "mix_norm", "q_a_norm", "kv_a_norm", "pool_w", "pool_scale", "fox_b_f", "ffn2_norm"]
SMALL_ROWS = 48


WEIGHT_VIEWS = ["ffn1_w_gu", "ffn1_w_d2", "w_in", "wq_a", "wq_b", "wk", "wv", "w_out", "ffn2_w_gu", "ffn2_w_d2"]


def _prepare_weights(gathered, wts):
    for (nm, l), w in gathered.items():
        if nm in ("ffn1_w_gu", "ffn2_w_gu"):
            wts[nm][l] = w
        elif nm in ("ffn1_w_down", "ffn2_w_down"):
            wts[nm[:5] + "w_d2"][l] = w.reshape(2, FF_SHARD, D)
        elif nm in ("w_in", "w_out"):
            wts[nm][l] = w.reshape(D, -1)
        elif nm == "w_q_b":
            wq = jnp.moveaxis(w, 0, 1).reshape(Q_RANK, H, NOPE + ROPE)
            zq = lambda n: jnp.zeros((Q_RANK, H, n), BF16)
            wts["wq_a"][l] = jnp.concatenate([wq, zq(DA - NOPE - ROPE)], axis=-1).reshape(Q_RANK, H * DA)
            wts["wq_b"][l] = jnp.concatenate([zq(NOPE), wq[..., NOPE + HALF_ROPE:], wq[..., NOPE:NOPE + HALF_ROPE],
                                              zq(DA - NOPE - ROPE)], axis=-1).reshape(Q_RANK, H * DA)
        else:
            wkv = jnp.moveaxis(w, 0, 1).reshape(KV_RANK, H, NOPE + VDIM)
            zk = jnp.zeros((KV_RANK, H, DA - NOPE), BF16)
            wts["wk"][l] = jnp.concatenate([wkv[..., :NOPE], zk], axis=-1).reshape(KV_RANK, H * DA)
            wts["wv"][l] = jnp.concatenate([wkv[..., NOPE:], zk], axis=-1).reshape(KV_RANK, H * DA)


def _chip_major(name, g):
    if name in ("ffn1_w_gu", "ffn2_w_gu"):
        return g
    if name in ("ffn1_w_down", "ffn2_w_down", "w_in", "w_out"):
        return g.reshape(N_CHIPS, g.shape[0] // N_CHIPS, g.shape[1])
    return jnp.moveaxis(g.reshape(g.shape[0], N_CHIPS, g.shape[1] // N_CHIPS), 1, 0)


def _pack_small(grads, g_final, loss):
    parts = []
    for l in range(DEPTH):
        for nm in SMALL:
            parts.append(grads[l][nm].reshape(-1))
    parts.append(g_final.reshape(-1))
    parts.append(loss.reshape(1))
    flat = jnp.concatenate(parts)
    return jnp.pad(flat, (0, SMALL_ROWS * D - flat.shape[0])).reshape(SMALL_ROWS, D)


def _unpack_small(packed, params):
    flat = packed.reshape(-1)
    out = {nm: [] for nm in SMALL}
    off = 0
    for l in range(DEPTH):
        for nm in SMALL:
            shp = params[nm].shape[1:]
            n = int(np.prod(shp))
            out[nm].append(flat[off:off + n].reshape(shp))
            off += n
    res = {nm: jnp.stack(v) for nm, v in out.items()}
    res["final_norm"] = flat[off:off + D]
    return res, flat[off + D]


def _update(name, w, g, m, v):
    shp = w.shape
    if w.ndim == 1:
        view = (1, shp[0])
    elif w.size <= 65536:
        view = (shp[0], w.size // shp[0])
    else:
        view = (w.size // shp[-1], shp[-1])
    tr = view[0]
    for cand in (512, 352, 256, 128):
        if view[0] % cand == 0 and view[0] > cand:
            tr = cand
            break
    d, mn, vn = _adamw(w.reshape(view), g.reshape(view), m.reshape(view), v.reshape(view), name="adamw_" + name, tr=tr)
    return d.reshape(shp), mn.reshape(shp), vn.reshape(shp)


WEIGHTS = ['ffn1_norm', 'ffn1_w_gu', 'ffn1_w_down', 'mix_norm', 'w_in', 'q_a_norm', 'w_q_b', 'kv_a_norm', 'w_kv_b', 'pool_w',
           'pool_scale', 'fox_b_f', 'w_out', 'ffn2_norm', 'ffn2_w_gu', 'ffn2_w_down', 'final_norm']


def kernel(x, ffn1_norm, ffn1_w_gu, ffn1_w_down, mix_norm, w_in, q_a_norm, w_q_b, kv_a_norm, w_kv_b, pool_w, pool_scale, fox_b_f, w_out, ffn2_norm, ffn2_w_gu, ffn2_w_down, final_norm, loss_target, m_ffn1_norm, m_ffn1_w_gu, m_ffn1_w_down, m_mix_norm, m_w_in, m_q_a_norm, m_w_q_b, m_kv_a_norm, m_w_kv_b, m_pool_w, m_pool_scale, m_fox_b_f, m_w_out, m_ffn2_norm, m_ffn2_w_gu, m_ffn2_w_down, m_final_norm, v_ffn1_norm, v_ffn1_w_gu, v_ffn1_w_down, v_mix_norm, v_w_in, v_q_a_norm, v_w_q_b, v_kv_a_norm, v_w_kv_b, v_pool_w, v_pool_scale, v_fox_b_f, v_w_out, v_ffn2_norm, v_ffn2_w_gu, v_ffn2_w_down, v_final_norm):
    params = dict(ffn1_norm=ffn1_norm, ffn1_w_gu=ffn1_w_gu, ffn1_w_down=ffn1_w_down, mix_norm=mix_norm, w_in=w_in, q_a_norm=q_a_norm,
                  w_q_b=w_q_b, kv_a_norm=kv_a_norm, w_kv_b=w_kv_b, pool_w=pool_w, pool_scale=pool_scale, fox_b_f=fox_b_f, w_out=w_out,
                  ffn2_norm=ffn2_norm, ffn2_w_gu=ffn2_w_gu, ffn2_w_down=ffn2_w_down, final_norm=final_norm)
    mom = dict(ffn1_norm=m_ffn1_norm, ffn1_w_gu=m_ffn1_w_gu, ffn1_w_down=m_ffn1_w_down, mix_norm=m_mix_norm, w_in=m_w_in,
               q_a_norm=m_q_a_norm, w_q_b=m_w_q_b, kv_a_norm=m_kv_a_norm, w_kv_b=m_w_kv_b, pool_w=m_pool_w, pool_scale=m_pool_scale,
               fox_b_f=m_fox_b_f, w_out=m_w_out, ffn2_norm=m_ffn2_norm, ffn2_w_gu=m_ffn2_w_gu, ffn2_w_down=m_ffn2_w_down,
               final_norm=m_final_norm)
    var = dict(ffn1_norm=v_ffn1_norm, ffn1_w_gu=v_ffn1_w_gu, ffn1_w_down=v_ffn1_w_down, mix_norm=v_mix_norm, w_in=v_w_in,
               q_a_norm=v_q_a_norm, w_q_b=v_w_q_b, kv_a_norm=v_kv_a_norm, w_kv_b=v_w_kv_b, pool_w=v_pool_w, pool_scale=v_pool_scale,
               fox_b_f=v_fox_b_f, w_out=v_w_out, ffn2_norm=v_ffn2_norm, ffn2_w_gu=v_ffn2_w_gu, ffn2_w_down=v_ffn2_w_down,
               final_norm=v_final_norm)

    first = [("ffn1_w_gu", 0), ("ffn1_w_down", 0)]
    mix0 = [(nm, 0) for nm in ("w_in", "w_q_b", "w_kv_b", "w_out")]
    rest = [(nm, l) for nm in BIG for l in range(DEPTH) if (nm, l) not in first + mix0]

    def shards(keys, zero=0.0):
        return [((_pad_w_in(params[nm]) if nm == "w_in" else params[nm])[l] + zero).astype(BF16) for nm, l in keys]

    wts = {nm: [None] * DEPTH for nm in WEIGHT_VIEWS}
    wts["pool_w"] = params["pool_w"]
    got = _gather_blocking(shards(first))
    _prepare_weights(dict(zip(first, got)), wts)
    sems_m, src_m, land_m, token_m = _gather_start(shards(mix0), got[0], "mix0")
    sm = dict(params)
    sm["ffn1_norm"] = params["ffn1_norm"] + token_m[0, 0]
    rest_shards = shards(rest, token_m[0, 0])
    flying = {}

    def late_weights(stage, act, sm_now):
        if stage == "ffn1":
            lands = _gather_forward(_gather_wait(sems_m, src_m, land_m, act, "mix0"), "mix0")
            _prepare_weights(dict(zip(mix0, lands)), wts)
            flying["rest"] = _gather_start(rest_shards, lands[0], "rest")
            sm_next = dict(sm_now)
            sm_next["mix_norm"] = sm_now["mix_norm"] + flying["rest"][3][0, 0]
            return sm_next
        if stage == "mix":
            sems_r, src_r, land_r, _ = flying["rest"]
            lands = dict(zip(rest, _gather_wait(sems_r, src_r, land_r, act, "rest")))
            now = [k for k in rest if k[1] == 0]
            _prepare_weights(dict(zip(now, _gather_forward([lands[k] for k in now], "rest0"))), wts)
            flying["rest1"] = _gather_forward_start([lands[k] for k in rest if k[1] == 1], "rest1")
            sm_next = dict(sm_now)
            sm_next["ffn2_norm"] = sm_now["ffn2_norm"] + flying["rest1"][2][0, 0]
            return sm_next
        sems_f, lands_thru, _ = flying["rest1"]
        lands = _gather_forward_wait(sems_f, lands_thru, act, "rest1")
        _prepare_weights(dict(zip([k for k in rest if k[1] == 1], lands)), wts)
        return sm_now

    pos = _position()
    flight = {}

    groups = {"l1": (1, BIG), "l0a": (0, [nm for nm in BIG if not nm.startswith("ffn1")]),
              "l0b": (0, [nm for nm in BIG if nm.startswith("ffn1")])}
    pending = {}

    def to_chips(key, full, sib):
        psum = _sum2_bf16(pos, full, sib, name=f"chip_sum_{key}")
        s2 = _split_start(_stage2_copies, psum, [((3,) + p.shape[1:], p.dtype) for p in psum], 3 * len(psum),
                          f"reduce_stage2_start_{key}")
        flight[key] = (full, sib, s2)
        return s2[3][0, 0]

    def grads_ready(l, stage, g, sm_now):
        behind, tok = None, None
        if (l, stage) == (1, "ffn1"):
            full = [_chip_major(nm, g[nm]) for nm in BIG]
            pending["l1"] = _split_start(_stage1_copies, full, [((N_CHIPS, f.shape[1] // 2, f.shape[2]), F32) for f in full],
                                         len(full), "reduce_stage1_start_l1")
            behind, tok = "ffn2_norm", pending["l1"][3][0, 0]
        elif (l, stage) == (0, "ffn2"):
            sems1, full_thru, sib_land, _ = pending["l1"]
            full, sib = _split_wait(_stage1_copies, sems1, full_thru, sib_land, g["ffn2_w_down"], "reduce_stage1_wait_l1")
            tok = to_chips("l1", full, sib)
        elif (l, stage) == (0, "mix"):
            full = [_chip_major(nm, g[nm]) for nm in groups["l0a"][1]]
            pending["l0a"] = _split_start(_stage1_copies, full, [((N_CHIPS, f.shape[1] // 2, f.shape[2]), F32) for f in full],
                                          len(full), "reduce_stage1_start_l0a")
            behind, tok = "ffn1_norm", pending["l0a"][3][0, 0]
        elif (l, stage) == (0, "ffn1_tokens"):
            sems1, full_thru, sib_land, _ = pending["l0a"]
            full, sib = _split_wait(_stage1_copies, sems1, full_thru, sib_land, g["dx"], "reduce_stage1_wait_l0a")
            tok = to_chips("l0a", full, sib)
        elif (l, stage) == (0, "ffn1"):
            full = [_chip_major(nm, g[nm]) for nm in groups["l0b"][1]]
            pending["l0b"] = _split_start(_stage1_copies, full, [((N_CHIPS, f.shape[1] // 2, f.shape[2]), F32) for f in full],
                                          len(full), "reduce_stage1_start_l0b")
        if behind is None:
            return sm_now, tok
        sm_next = dict(sm_now)
        sm_next[behind] = sm_now[behind] + tok
        return sm_next, tok

    loss, dx, grads, g_final = _local_step(x[0], loss_target[0], wts, sm, late_weights, grads_ready)

    def view2d(a):
        return a.reshape(a.size // a.shape[-1], a.shape[-1])

    after = pending["l0b"][3]
    done = {nm: None for nm in BIG}
    for key in ("l1", "l0a", "l0b"):
        l, names = groups[key]
        full, sib, (sems2, ps_thru, lands2, _) = flight[key]
        _, recv = _split_wait(_stage2_copies, sems2, ps_thru, lands2, after, f"reduce_stage2_wait_{key}")
        whole = _reduce_stage3(_sum5(pos, full, sib, recv, name=f"grad_sum_{key}"), key)
        for nm, g_l in zip(names, whole):
            if nm == "w_in":
                g_l = _unpad_w_in(g_l)
            tr = max(t for t in (512, 352, 256, 128) if g_l.shape[0] % t == 0)
            done[nm] = _adamw_layer(view2d(params[nm]), g_l, view2d(mom[nm]), view2d(var[nm]), l, done[nm],
                                    name=f"adamw_{nm}_{l}", tr=tr)
        after = done[names[-1]][0][-8:, 0:128]
        if key == "l1":
            small_g, loss = _unpack_small(_allreduce_small(_pack_small(grads, g_final, loss)), params)
            sems1, full_thru, sib_land, _ = pending["l0b"]
            full_b, sib_b = _split_wait(_stage1_copies, sems1, full_thru, sib_land, after + small_g["final_norm"][0],
                                        "reduce_stage1_wait_l0b")
            after = after + to_chips("l0b", full_b, sib_b)
    gw, delta, new_m, new_v = dict(small_g), {}, {}, {}
    for nm in BIG:
        delta[nm], new_m[nm], new_v[nm], gw[nm] = [a.reshape(params[nm].shape) for a in done[nm]]
    for nm in small_g:
        delta[nm], new_m[nm], new_v[nm] = _update(nm, params[nm], gw[nm], mom[nm], var[nm])
    return (loss, dx[None], *[gw[n] for n in WEIGHTS], *[delta[n] for n in WEIGHTS], *[new_m[n] for n in WEIGHTS],
            *[new_v[n] for n in WEIGHTS])
```

```python
import functools
import math

import jax
import jax.numpy as jnp
import numpy as np
from jax import lax
from jax.experimental import pallas as pl
from jax.experimental.pallas import tpu as pltpu

F32 = jnp.float32
BF16 = jnp.bfloat16
MESH = pl.DeviceIdType.MESH
HBM_SPEC = pl.BlockSpec(memory_space=pltpu.HBM)

D = 1024
DEPTH = 2
D_FF = 2816
FF_SHARD = 1408
N_CHIPS = 4
H = 6
NOPE, ROPE, VDIM = 64, 32, 64
HALF_ROPE = ROPE // 2
Q_RANK, KV_RANK = 256, 128
POOL_W = 256
FOX_D = 64
N_IN = 1830
NZ = 2048
ROPE_THETA = 10000.0
EPS = 1e-6
POOL_HALO = 16
Z_QA, Z_KVA, Z_KR, Z_POOL, Z_FOX, Z_F = 0, 256, 384, 512, 768, 1920

ADAM_LR, ADAM_B1, ADAM_B2, ADAM_EPS, ADAM_WD, ADAM_STEP = 0.001, 0.9, 0.999, 1e-08, 0.01, 10

VMEM_LIMIT_V7X = 56 * 1024 * 1024


def _cp(sem=None, vmem=VMEM_LIMIT_V7X):
    return pltpu.CompilerParams(dimension_semantics=sem, vmem_limit_bytes=vmem)


def _sigmoid(x):
    return 0.5 * jnp.tanh(0.5 * x) + 0.5


def _dot(a, b, dims):
    return lax.dot_general(a, b, (dims, ((), ())), preferred_element_type=F32)


NN = ((1,), (0,))
NT = ((1,), (1,))
TN = ((0,), (0,))


def _mm(a, b, mode, *, name, out_dtype=F32, add=None, alpha=None, tm=512, tn=512, tk=512, n_major_out=False):
    if mode == "nn":
        (m, k), (k2, n) = a.shape, b.shape
    elif mode == "nt":
        (m, k), (n, k2) = a.shape, b.shape
    else:
        (k, m), (k2, n) = a.shape, b.shape
    assert k == k2
    tm, tn, tk = min(tm, m), min(tn, n), min(tk, k)
    assert m % tm == 0 and n % tn == 0 and k % tk == 0, (name, m, n, k, tm, tn, tk)
    nk = k // tk
    dims = {"nn": NN, "nt": NT, "tn": TN}[mode]
    a_spec = pl.BlockSpec((tk, tm), lambda i, j, kk: (kk, i)) if mode == "tn" else pl.BlockSpec((tm, tk), lambda i, j, kk: (i, kk))
    b_spec = pl.BlockSpec((tn, tk), lambda i, j, kk: (j, kk)) if mode == "nt" else pl.BlockSpec((tk, tn), lambda i, j, kk: (kk, j))
    in_specs = [a_spec, b_spec]
    args = [a, b]
    if add is not None:
        in_specs.append(pl.BlockSpec((tm, tn), lambda i, j, kk: (i, j)))
        args.append(add)
    if n_major_out:
        out_shape = jax.ShapeDtypeStruct((n // tn, m, tn), out_dtype)
        out_spec = pl.BlockSpec((None, tm, tn), lambda i, j, kk: (j, i, 0))
    else:
        out_shape = jax.ShapeDtypeStruct((m, n), out_dtype)
        out_spec = pl.BlockSpec((tm, tn), lambda i, j, kk: (i, j))

    def body(*refs):
        a_ref, b_ref = refs[0], refs[1]
        add_ref = refs[2] if add is not None else None
        o_ref, acc = refs[-2], refs[-1]
        kk = pl.program_id(2)

        @pl.when(kk == 0)
        def _():
            acc[...] = jnp.zeros_like(acc)

        acc[...] += _dot(a_ref[...].astype(BF16), b_ref[...].astype(BF16), dims)

        @pl.when(kk == nk - 1)
        def _():
            r = acc[...]
            if alpha is not None:
                r = r * alpha
            if add_ref is not None:
                r = r + add_ref[...].astype(F32)
            o_ref[...] = r.astype(out_dtype)

    return pl.pallas_call(
        body, name=name, grid=(m // tm, n // tn, nk), in_specs=in_specs, out_specs=out_spec, out_shape=out_shape,
        scratch_shapes=[pltpu.VMEM((tm, tn), F32)],
        compiler_params=_cp(("parallel", "parallel", "arbitrary")),
    )(*args)


def _rmsnorm_bwd(x, col_block, gain, da, w, dres=None, *, name, tm=512):
    s = x.shape[0]
    k, n = w.shape
    tm = min(tm, s)

    def body(*refs):
        x_ref, g_ref, da_ref, w_ref = refs[:4]
        dres_ref = refs[4] if dres is not None else None
        dx_ref, dg_ref = refs[-2], refs[-1]
        xv = x_ref[...]
        r = lax.rsqrt(jnp.mean(xv * xv, axis=-1, keepdims=True) + EPS)
        dhv = _dot(da_ref[...], w_ref[...], NT)
        a = dhv * g_ref[...]
        dx = r * a - xv * (r * r * r) * jnp.mean(a * xv, axis=-1, keepdims=True)
        if dres_ref is not None:
            dx = dx + dres_ref[...]
        dx_ref[...] = dx

        @pl.when(pl.program_id(0) == 0)
        def _():
            dg_ref[...] = jnp.zeros_like(dg_ref)

        dg_ref[...] += jnp.sum(dhv * xv * r, axis=0, keepdims=True)

    in_specs = [pl.BlockSpec((tm, k), lambda i: (i, col_block)), pl.BlockSpec((1, k), lambda i: (0, 0)),
                pl.BlockSpec((tm, n), lambda i: (i, 0)), pl.BlockSpec((k, n), lambda i: (0, 0))]
    args = [x, gain.reshape(1, k), da, w]
    if dres is not None:
        in_specs.append(pl.BlockSpec((tm, k), lambda i: (i, 0)))
        args.append(dres)
    dx, dg = pl.pallas_call(
        body, name=name, grid=(s // tm,), in_specs=in_specs,
        out_specs=[pl.BlockSpec((tm, k), lambda i: (i, 0)), pl.BlockSpec((1, k), lambda i: (0, 0))],
        out_shape=[jax.ShapeDtypeStruct((s, k), F32), jax.ShapeDtypeStruct((1, k), F32)],
        compiler_params=_cp(("arbitrary",)),
    )(*args)
    return dx, dg.reshape(k)


def _ffn_fwd(x, gain, w_gu4, w_d2, *, name, tm=256):
    s = x.shape[0]
    tm = min(tm, s)

    def body(x_ref, g_ref, wgu_ref, wd_ref, xo_ref, dgu_ref, act_ref):
        xv = x_ref[...]
        r = lax.rsqrt(jnp.mean(xv * xv, axis=-1, keepdims=True) + EPS)
        hv = (xv * r * g_ref[...]).astype(BF16)
        y = jnp.zeros((tm, D), F32)
        for j in range(2):
            g = _dot(hv, wgu_ref[j], NN)
            u = _dot(hv, wgu_ref[2 + j], NN)
            sg = _sigmoid(g)
            silu = g * sg
            dgu_ref[:, j * FF_SHARD:(j + 1) * FF_SHARD] = (u * (sg * (1.0 + g * (1.0 - sg)))).astype(BF16)
            dgu_ref[:, D_FF + j * FF_SHARD:D_FF + (j + 1) * FF_SHARD] = silu.astype(BF16)
            act = (silu * u).astype(BF16)
            act_ref[:, j * FF_SHARD:(j + 1) * FF_SHARD] = act
            y = y + _dot(act, wd_ref[j], NN)
        xo_ref[...] = xv + 0.5 * y

    row = lambda i: (i, 0)
    return pl.pallas_call(
        body, name=name, grid=(s // tm,),
        in_specs=[pl.BlockSpec((tm, D), row), pl.BlockSpec((1, D), lambda i: (0, 0)),
                  pl.BlockSpec((N_CHIPS, D, FF_SHARD), lambda i: (0, 0, 0), pipeline_mode=pl.Buffered(1)),
                  pl.BlockSpec((2, FF_SHARD, D), lambda i: (0, 0, 0), pipeline_mode=pl.Buffered(1))],
        out_specs=[pl.BlockSpec((tm, D), row), pl.BlockSpec((tm, 2 * D_FF), row), pl.BlockSpec((tm, D_FF), row)],
        out_shape=[jax.ShapeDtypeStruct((s, D), F32), jax.ShapeDtypeStruct((s, 2 * D_FF), BF16),
                   jax.ShapeDtypeStruct((s, D_FF), BF16)],
        compiler_params=_cp(("parallel",)),
    )(x, gain.reshape(1, D), w_gu4, w_d2)


FFN_ROW_CHUNK = 32


def _ffn_bwd(x, dxo, dloc, gain, w_gu4, w_d2, *, name, tm=256):
    s = x.shape[0]
    tm = min(tm, s)

    def body(x_ref, dxo_ref, dloc_ref, g_ref, wgu_ref, wd_ref, dx_ref, dgu_ref, h_ref, dy_ref, dg_ref):
        xv = x_ref[...]
        r = lax.rsqrt(jnp.mean(xv * xv, axis=-1, keepdims=True) + EPS)
        xh = xv * r
        h_ref[...] = (xh * g_ref[...]).astype(BF16)
        dxov = dxo_ref[...]
        dy = (0.5 * dxov).astype(BF16)
        dy_ref[...] = dy
        gcols = [slice(j * FF_SHARD, (j + 1) * FF_SHARD) for j in range(2)]
        ucols = [slice(D_FF + j * FF_SHARD, D_FF + (j + 1) * FF_SHARD) for j in range(2)]
        dacts = [_dot(dy, wd_ref[j], NT) for j in range(2)]
        for r0 in range(0, tm, FFN_ROW_CHUNK):
            rows = slice(r0, r0 + FFN_ROW_CHUNK)
            for j in range(2):
                da = dacts[j][rows]
                dgu_ref[rows, gcols[j]] = (da * dloc_ref[rows, gcols[j]].astype(F32)).astype(BF16)
                dgu_ref[rows, ucols[j]] = (da * dloc_ref[rows, ucols[j]].astype(F32)).astype(BF16)
        dh = jnp.zeros((tm, D), F32)
        for j in range(2):
            dh = dh + _dot(dgu_ref[:, gcols[j]], wgu_ref[j], NT) + _dot(dgu_ref[:, ucols[j]], wgu_ref[2 + j], NT)
        a = dh * g_ref[...]
        dx_ref[...] = dxov + r * a - xh * (r * jnp.mean(a * xh, axis=-1, keepdims=True))

        @pl.when(pl.program_id(0) == 0)
        def _():
            dg_ref[...] = jnp.zeros_like(dg_ref)

        dg_ref[...] += jnp.sum(dh * xh, axis=0, keepdims=True)

    row = lambda i: (i, 0)
    outs = pl.pallas_call(
        body, name=name, grid=(s // tm,),
        in_specs=[pl.BlockSpec((tm, D), row), pl.BlockSpec((tm, D), row), pl.BlockSpec((tm, 2 * D_FF), row),
                  pl.BlockSpec((1, D), lambda i: (0, 0)),
                  pl.BlockSpec((N_CHIPS, D, FF_SHARD), lambda i: (0, 0, 0), pipeline_mode=pl.Buffered(1)),
                  pl.BlockSpec((2, FF_SHARD, D), lambda i: (0, 0, 0), pipeline_mode=pl.Buffered(1))],
        out_specs=[pl.BlockSpec((tm, D), row), pl.BlockSpec((tm, 2 * D_FF), row),
                   pl.BlockSpec((tm, D), row), pl.BlockSpec((tm, D), row), pl.BlockSpec((1, D), lambda i: (0, 0))],
        out_shape=[jax.ShapeDtypeStruct((s, D), F32), jax.ShapeDtypeStruct((s, 2 * D_FF), BF16),
                   jax.ShapeDtypeStruct((s, D), BF16), jax.ShapeDtypeStruct((s, D), BF16), jax.ShapeDtypeStruct((1, D), F32)],
        compiler_params=_cp(("arbitrary",)),
    )(x, dxo, dloc, gain.reshape(1, D), w_gu4, w_d2)
    dx, dgu, h, dy, dg = outs
    return dx, dgu, h, dy, dg.reshape(D)


DA = 128
SCALE_MLA = 1.0 / math.sqrt(NOPE + ROPE)
SCALE_FOX = 1.0 / math.sqrt(FOX_D)


def _causal_blocks(nb, key_major):
    if key_major:
        pairs = [(i, j) for j in range(nb) for i in range(j, nb)]
    else:
        pairs = [(i, j) for i in range(nb) for j in range(i + 1)]
    return (jnp.asarray(np.array([p[0] for p in pairs], np.int32)), jnp.asarray(np.array([p[1] for p in pairs], np.int32)))


HEADS_PER_STEP = 3
ROW_CHUNK = 64

def _col_to_row(col):
    return jnp.broadcast_to(col, (col.shape[0], DA)).T[0:1, :]


def _attn_fwd(qa, ka, va, dv, *, name, t=512):
    h, s, _ = qa.shape
    t = min(t, s)
    nb = s // t
    g = H
    qi, kj = _causal_blocks(nb, key_major=False)

    rc = min(ROW_CHUNK, t)

    def body(qi_ref, kj_ref, q_ref, k_ref, v_ref, o_ref, lse_ref, m_sc, acc_sc, p_sc, a_sc):
        n = pl.program_id(1)
        i, j = qi_ref[n], kj_ref[n]

        @pl.when(j == 0)
        def _():
            m_sc[...] = jnp.full_like(m_sc, -jnp.inf)
            acc_sc[...] = jnp.zeros_like(acc_sc)

        def step(masked):
            scs = [_dot(q_ref[hh], k_ref[hh], NT) for hh in range(g)]
            for r0 in range(0, t, rc):
                rows = slice(r0, r0 + rc)
                for hh in range(g):
                    sr = scs[hh][rows]
                    if masked:
                        row = lax.broadcasted_iota(jnp.int32, (rc, t), 0) + r0
                        col = lax.broadcasted_iota(jnp.int32, (rc, t), 1)
                        sr = jnp.where(col <= row, sr, -jnp.inf)
                    tiles = [sr[:, c0:c0 + DA] for c0 in range(0, t, DA)]
                    top = tiles[0]
                    for tile in tiles[1:]:
                        top = jnp.maximum(top, tile)
                    m_old = m_sc[hh, rows]
                    m_new = jnp.maximum(m_old, jnp.max(top, axis=-1, keepdims=True))
                    for c0, tile in zip(range(0, t, DA), tiles):
                        p_sc[hh, rows, c0:c0 + DA] = jnp.exp(tile - m_new).astype(BF16)
                    a_sc[hh, rows] = jnp.exp(m_old - m_new)
                    m_sc[hh, rows] = m_new
            for hh in range(g):
                acc_sc[hh] = a_sc[hh] * acc_sc[hh] + _dot(p_sc[hh], v_ref[hh], NN)

        @pl.when(j < i)
        def _():
            step(False)

        @pl.when(j == i)
        def _():
            step(True)
            for hh in range(g):
                acc = acc_sc[hh]
                l = acc[:, dv:dv + 1]
                o_ref[hh] = acc[:, :dv] / l
                lse_ref[hh] = _col_to_row(m_sc[hh][:, 0:1] + jnp.log(l))

    qmap = lambda hg, n, qi_r, kj_r: (hg, qi_r[n], 0)
    kmap = lambda hg, n, qi_r, kj_r: (hg, kj_r[n], 0)
    return pl.pallas_call(
        body, name=name,
        grid_spec=pltpu.PrefetchScalarGridSpec(
            num_scalar_prefetch=2, grid=(h // g, qi.shape[0]),
            in_specs=[pl.BlockSpec((g, t, DA), qmap), pl.BlockSpec((g, t, DA), kmap), pl.BlockSpec((g, t, DA), kmap)],
            out_specs=[pl.BlockSpec((g, t, dv), qmap), pl.BlockSpec((g, 1, t), lambda hg, n, qi_r, kj_r: (hg, 0, qi_r[n]))],
            scratch_shapes=[pltpu.VMEM((g, t, DA), F32), pltpu.VMEM((g, t, DA), F32), pltpu.VMEM((g, t, t), BF16),
                            pltpu.VMEM((g, t, DA), F32)]),
        out_shape=[jax.ShapeDtypeStruct((h, s, dv), F32), jax.ShapeDtypeStruct((h, 1, s), F32)],
        compiler_params=_cp(("parallel", "arbitrary")),
    )(qi, kj, qa, ka, va)


def _attn_bwd(qa, ka, va, doa, lse_row, delta_row, decay, *, name, t=512):
    h, s, _ = qa.shape
    t = min(t, s)
    nb = s // t
    g = HEADS_PER_STEP
    rc = min(ROW_CHUNK, t)
    qi, kj = _causal_blocks(nb, key_major=True)
    nsteps = qi.shape[0]

    def body(*refs):
        qi_ref, kj_ref, q_ref, k_ref, v_ref, do_ref, lse_ref, dl_ref = refs[:8]
        p_sc, ds_sc = refs[-2:]
        if decay:
            dq_ref, dk_ref, dv_ref, dcq_ref, dck_ref, dq_acc, dk_acc, dv_acc, dcq_acc, dck_acc = refs[8:-2]
        else:
            dq_ref, dk_ref, dv_ref, dq_acc, dk_acc, dv_acc = refs[8:-2]
        n = pl.program_id(1)
        i, j = qi_ref[n], kj_ref[n]

        @pl.when(n == 0)
        def _():
            dq_acc[...] = jnp.zeros_like(dq_acc)
            if decay:
                dcq_acc[...] = jnp.zeros_like(dcq_acc)

        @pl.when(i == j)
        def _():
            dk_acc[...] = jnp.zeros_like(dk_acc)
            dv_acc[...] = jnp.zeros_like(dv_acc)
            if decay:
                dck_acc[...] = jnp.zeros_like(dck_acc)

        def step(masked):
            sts = [_dot(k_ref[hh], q_ref[hh], NT) for hh in range(g)]
            dpts = [_dot(v_ref[hh], do_ref[hh], NT) for hh in range(g)]
            dcq = [jnp.zeros((1, t), F32) for _ in range(g)]
            for r0 in range(0, t, rc):
                rows = slice(r0, r0 + rc)
                for hh in range(g):
                    st = sts[hh][rows]
                    if masked:
                        row = lax.broadcasted_iota(jnp.int32, (rc, t), 0) + r0
                        col = lax.broadcasted_iota(jnp.int32, (rc, t), 1)
                        st = jnp.where(row <= col, st, -jnp.inf)
                    pt = jnp.exp(st - lse_ref[hh])
                    dst = pt * (dpts[hh][rows] - dl_ref[hh])
                    p_sc[hh, rows] = pt.astype(BF16)
                    ds_sc[hh, rows] = dst.astype(BF16)
                    if decay:
                        dcq[hh] = dcq[hh] + jnp.sum(dst, axis=0, keepdims=True)
                        dck_acc[hh, rows] -= jnp.sum(dst, axis=1, keepdims=True)
            for hh in range(g):
                dv_acc[hh] += _dot(p_sc[hh], do_ref[hh], NN)
                dk_acc[hh] += _dot(ds_sc[hh], q_ref[hh], NN)
                dq_acc[hh, i] += _dot(ds_sc[hh], k_ref[hh], TN)
                if decay:
                    dcq_acc[hh, i] += dcq[hh]

        @pl.when(i > j)
        def _():
            step(False)

        @pl.when(i == j)
        def _():
            step(True)

        @pl.when(i == nb - 1)
        def _():
            dk_ref[...] = dk_acc[...]
            dv_ref[...] = dv_acc[...]
            if decay:
                for hh in range(g):
                    dck_ref[hh] = _col_to_row(dck_acc[hh])

        @pl.when(n == nsteps - 1)
        def _():
            dq_ref[...] = dq_acc[...]
            if decay:
                dcq_ref[...] = dcq_acc[...]

    kmap = lambda hg, n, qi_r, kj_r: (hg, kj_r[n], 0)
    qmap = lambda hg, n, qi_r, kj_r: (hg, qi_r[n], 0)
    qrow = lambda hg, n, qi_r, kj_r: (hg, 0, qi_r[n])
    krow = lambda hg, n, qi_r, kj_r: (hg, 0, kj_r[n])
    whole = lambda hg, n, qi_r, kj_r: (hg, 0, 0, 0)
    in_specs = [pl.BlockSpec((g, t, DA), qmap), pl.BlockSpec((g, t, DA), kmap), pl.BlockSpec((g, t, DA), kmap),
                pl.BlockSpec((g, t, DA), qmap), pl.BlockSpec((g, 1, t), qrow), pl.BlockSpec((g, 1, t), qrow)]
    out_specs = [pl.BlockSpec((g, nb, t, DA), whole), pl.BlockSpec((g, t, DA), kmap), pl.BlockSpec((g, t, DA), kmap)]
    out_shape = [jax.ShapeDtypeStruct((h, nb, t, DA), F32), jax.ShapeDtypeStruct((h, s, DA), F32), jax.ShapeDtypeStruct((h, s, DA), F32)]
    scratch = [pltpu.VMEM((g, nb, t, DA), F32), pltpu.VMEM((g, t, DA), F32), pltpu.VMEM((g, t, DA), F32)]
    if decay:
        out_specs += [pl.BlockSpec((g, nb, 1, t), whole), pl.BlockSpec((g, 1, t), krow)]
        out_shape += [jax.ShapeDtypeStruct((h, nb, 1, t), F32), jax.ShapeDtypeStruct((h, 1, s), F32)]
        scratch += [pltpu.VMEM((g, nb, 1, t), F32), pltpu.VMEM((g, t, 1), F32)]
    scratch += [pltpu.VMEM((g, t, t), BF16), pltpu.VMEM((g, t, t), BF16)]
    outs = pl.pallas_call(
        body, name=name,
        grid_spec=pltpu.PrefetchScalarGridSpec(num_scalar_prefetch=2, grid=(h // g, nsteps), in_specs=in_specs, out_specs=out_specs,
                                               scratch_shapes=scratch),
        out_shape=out_shape, compiler_params=_cp(("parallel", "arbitrary")),
    )(qi, kj, qa, ka, va, doa, lse_row, delta_row)
    outs = list(outs)
    outs[0] = outs[0].reshape(h, s, DA)
    if decay:
        outs[3] = outs[3].reshape(h, 1, s)
    return outs


def _sel(rows, cols, pairs, value=1.0):
    m = np.zeros((rows, cols), np.float32)
    for r, c in pairs:
        m[r, c] = value
    return jnp.asarray(m, BF16)


def _lane_row(lanes):
    m = np.zeros((1, DA), np.float32)
    m[0, list(lanes)] = 1.0
    return jnp.asarray(m)


def _rms(xv, gain):
    r = lax.rsqrt(jnp.mean(xv * xv, axis=-1, keepdims=True) + EPS)
    return xv * r * gain


def _mix_in(x, gains, w_in, wq_a, wq_b, wk, wv, tabs, *, name, tm=512):
    s = x.shape[0]
    tm = min(tm, s)
    one = _lane_row([VDIM])
    g_mix, g_q, g_kv = gains

    def body(x_ref, gm_ref, gq_ref, gkv_ref, win_ref, wa_ref, wb_ref, wk_ref, wv_ref, cq_ref, sq_ref, ck_ref, sk_ref,
             one_ref, z_ref, h_ref, qn_ref, kvn_ref, qa_ref, ka_ref, va_ref):
        hv = _rms(x_ref[...], gm_ref[...]).astype(BF16)
        h_ref[...] = hv
        z = _dot(hv, win_ref[...], NN)
        z_ref[...] = z
        qn = _rms(z[:, Z_QA:Z_QA + Q_RANK], gq_ref[...]).astype(BF16)
        kvn = _rms(z[:, Z_KVA:Z_KVA + KV_RANK], gkv_ref[...]).astype(BF16)
        qn_ref[...] = qn
        kvn_ref[...] = kvn
        c, sn = cq_ref[...], sq_ref[...]
        kpe = z[:, Z_KR:Z_KR + DA] * ck_ref[...] + z[:, Z_F:Z_F + DA] * sk_ref[...]
        for hh in range(H):
            cols = slice(hh * DA, (hh + 1) * DA)
            qa_ref[hh] = (_dot(qn, wa_ref[:, cols], NN) * c + _dot(qn, wb_ref[:, cols], NN) * sn).astype(BF16)
            ka_ref[hh] = (_dot(kvn, wk_ref[:, cols], NN) + kpe).astype(BF16)
            va_ref[hh] = (_dot(kvn, wv_ref[:, cols], NN) + one_ref[...]).astype(BF16)

    row = lambda i: (i, 0)
    fixed = lambda i: (0, 0)
    full = lambda a: pl.BlockSpec(a.shape, fixed)
    tab = pl.BlockSpec((tm, DA), row)
    heads = pl.BlockSpec((H, tm, DA), lambda i: (0, i, 0))
    return pl.pallas_call(
        body, name=name, grid=(s // tm,),
        in_specs=[pl.BlockSpec((tm, D), row), pl.BlockSpec((1, D), fixed), pl.BlockSpec((1, Q_RANK), fixed),
                  pl.BlockSpec((1, KV_RANK), fixed), full(w_in), full(wq_a), full(wq_b), full(wk), full(wv),
                  tab, tab, tab, tab, pl.BlockSpec((1, DA), fixed)],
        out_specs=[pl.BlockSpec((tm, NZ), row), pl.BlockSpec((tm, D), row), pl.BlockSpec((tm, Q_RANK), row),
                   pl.BlockSpec((tm, KV_RANK), row), heads, heads, heads],
        out_shape=[jax.ShapeDtypeStruct((s, NZ), F32), jax.ShapeDtypeStruct((s, D), BF16),
                   jax.ShapeDtypeStruct((s, Q_RANK), BF16), jax.ShapeDtypeStruct((s, KV_RANK), BF16)]
        + [jax.ShapeDtypeStruct((H, s, DA), BF16)] * 3,
        compiler_params=_cp(("parallel",)),
    )(x, g_mix.reshape(1, D), g_q.reshape(1, Q_RANK), g_kv.reshape(1, KV_RANK), w_in, wq_a, wq_b, wk, wv,
      tabs["cq"], tabs["sq"], tabs["ck"], tabs["sk"], one)


DEC_C = (FOX_D, FOX_D + 1, FOX_D + 2)
DEC_1 = (FOX_D + 3, FOX_D + 4, FOX_D + 5)


def _fox_prep(z, c3t, *, name, tm=512):
    s = z.shape[0]
    tm = min(tm, s)
    w = H * FOX_D
    left = [(r, r) for r in range(FOX_D)]
    right = [(FOX_D + r, r) for r in range(FOX_D)]
    pq = jnp.stack([_sel(DA, DA, left, SCALE_FOX), _sel(DA, DA, right, SCALE_FOX)])
    pk = jnp.stack([_sel(DA, DA, left), _sel(DA, DA, right)])
    pcq = jnp.stack([_sel(32, DA, [(hh + 8 * k, DEC_C[k]) for k in range(3)]) for hh in range(H)])
    pck = jnp.stack([_sel(32, DA, [(hh + 8 * k, DEC_1[k]) for k in range(3)], -1.0) for hh in range(H)])
    rows3 = jnp.concatenate([_lane_row(DEC_1), _lane_row(DEC_C), _lane_row([FOX_D])], axis=0)

    def body(zq_ref, zk_ref, zv_ref, c_ref, pq_ref, pk_ref, pcq_ref, pck_ref, r_ref, qa_ref, ka_ref, va_ref):
        c3 = c_ref[...]
        for pair in range(H // 2):
            lanes = slice(pair * DA, (pair + 1) * DA)
            zq, zk, zv = zq_ref[:, lanes].astype(BF16), zk_ref[:, lanes].astype(BF16), zv_ref[:, lanes].astype(BF16)
            for side in range(2):
                hh = 2 * pair + side
                qa_ref[hh] = (_dot(zq, pq_ref[side], NN) + _dot(c3, pcq_ref[hh], TN) + r_ref[0:1, :]).astype(BF16)
                ka_ref[hh] = (_dot(zk, pk_ref[side], NN) + _dot(c3, pck_ref[hh], TN) + r_ref[1:2, :]).astype(BF16)
                va_ref[hh] = (_dot(zv, pk_ref[side], NN) + r_ref[2:3, :]).astype(BF16)

    fixed2 = lambda i: (0, 0)
    fixed3 = lambda i: (0, 0, 0)
    heads = pl.BlockSpec((H, tm, DA), lambda i: (0, i, 0))
    zblk = lambda c: pl.BlockSpec((tm, w), lambda i: (i, c))
    return pl.pallas_call(
        body, name=name, grid=(s // tm,),
        in_specs=[zblk(Z_FOX // w), zblk(Z_FOX // w + 1), zblk(Z_FOX // w + 2), pl.BlockSpec((32, tm), lambda i: (0, i)),
                  pl.BlockSpec((2, DA, DA), fixed3), pl.BlockSpec((2, DA, DA), fixed3),
                  pl.BlockSpec((H, 32, DA), fixed3), pl.BlockSpec((H, 32, DA), fixed3), pl.BlockSpec((3, DA), fixed2)],
        out_specs=[heads, heads, heads], out_shape=[jax.ShapeDtypeStruct((H, s, DA), BF16)] * 3,
        compiler_params=_cp(("parallel",)),
    )(z, z, z, c3t, pq, pk, pcq, pck, rows3)


def _mix_out(oa, yb, oc, w_out, x1, *, name, tm=512):
    s = yb.shape[0]
    tm = min(tm, s)
    e2 = jnp.stack([_sel(VDIM, DA, [(r, r) for r in range(VDIM)]), _sel(VDIM, DA, [(r, VDIM + r) for r in range(VDIM)])])

    def body(oa_ref, yb_ref, oc_ref, e_ref, w_ref, x_ref, x2_ref, cat_ref):
        def pairs(o_ref):
            return [(_dot(o_ref[2 * p].astype(BF16), e_ref[0], NN) + _dot(o_ref[2 * p + 1].astype(BF16), e_ref[1], NN)).astype(BF16)
                    for p in range(H // 2)]

        cat = jnp.concatenate(pairs(oa_ref) + [yb_ref[...].astype(BF16)] + pairs(oc_ref), axis=1)
        cat_ref[...] = cat
        x2_ref[...] = x_ref[...] + _dot(cat, w_ref[...], NN)

    row = lambda i: (i, 0)
    heads = pl.BlockSpec((H, tm, VDIM), lambda i: (0, i, 0))
    return pl.pallas_call(
        body, name=name, grid=(s // tm,),
        in_specs=[heads, pl.BlockSpec((tm, POOL_W), row), heads, pl.BlockSpec((2, VDIM, DA), lambda i: (0, 0, 0)),
                  pl.BlockSpec((D, D), lambda i: (0, 0)), pl.BlockSpec((tm, D), row)],
        out_specs=[pl.BlockSpec((tm, D), row), pl.BlockSpec((tm, D), row)],
        out_shape=[jax.ShapeDtypeStruct((s, D), F32), jax.ShapeDtypeStruct((s, D), BF16)],
        compiler_params=_cp(("parallel",)),
    )(oa, yb, oc, e2, w_out, x1)


def _mix_out_bwd(dx2b, w_out, oa, oc, *, name, tm=512):
    s = dx2b.shape[0]
    tm = min(tm, s)
    f2 = jnp.stack([_sel(DA, DA, [(r, r) for r in range(VDIM)]), _sel(DA, DA, [(VDIM + r, r) for r in range(VDIM)])])
    nv = H * VDIM

    def body(dx_ref, w_ref, oa_ref, oc_ref, f_ref, doa_ref, doc_ref, dyb_ref, dla_ref, dlc_ref):
        dcat = _dot(dx_ref[...], w_ref[...], NT)
        dyb_ref[...] = dcat[:, nv:nv + POOL_W]
        for base, o_ref, do_ref, dl_ref in ((0, oa_ref, doa_ref, dla_ref), (nv + POOL_W, oc_ref, doc_ref, dlc_ref)):
            for p in range(H // 2):
                blk = dcat[:, base + p * DA:base + (p + 1) * DA].astype(BF16)
                for side in range(2):
                    hh = 2 * p + side
                    do = _dot(blk, f_ref[side], NN)
                    do_ref[hh] = do.astype(BF16)
                    dl_ref[hh] = _col_to_row(jnp.sum(do[:, :VDIM] * o_ref[hh], axis=-1, keepdims=True))

    row = lambda i: (i, 0)
    heads = lambda w: pl.BlockSpec((H, tm, w), lambda i: (0, i, 0))
    return pl.pallas_call(
        body, name=name, grid=(s // tm,),
        in_specs=[pl.BlockSpec((tm, D), row), pl.BlockSpec((D, D), lambda i: (0, 0)), heads(VDIM), heads(VDIM),
                  pl.BlockSpec((2, DA, DA), lambda i: (0, 0, 0))],
        out_specs=[heads(DA), heads(DA), pl.BlockSpec((tm, POOL_W), row),
                   pl.BlockSpec((H, 1, tm), lambda i: (0, 0, i)), pl.BlockSpec((H, 1, tm), lambda i: (0, 0, i))],
        out_shape=[jax.ShapeDtypeStruct((H, s, DA), BF16), jax.ShapeDtypeStruct((H, s, DA), BF16),
                   jax.ShapeDtypeStruct((s, POOL_W), F32), jax.ShapeDtypeStruct((H, 1, s), F32), jax.ShapeDtypeStruct((H, 1, s), F32)],
        compiler_params=_cp(("parallel",)),
    )(dx2b, w_out, oa, oc, f2)


def _mla_bwd_prep(dqa, dka, dva, dft, cq, sq, ck, sk, *, name, tm=512):
    s = dqa.shape[1]
    tm = min(tm, s)
    keep = _lane_row(range(NOPE))

    def body(dq_ref, dk_ref, dv_ref, dft_ref, cq_ref, sq_ref, ck_ref, sk_ref, keep_ref, dqab_ref, dkv_ref, dz3_ref, dz15_ref):
        cqv, sqv = cq_ref[...], sq_ref[...]
        dkpe = jnp.zeros((tm, DA), F32)
        for hh in range(H):
            lanes = slice(hh * DA, (hh + 1) * DA)
            dq = dq_ref[hh]
            dqab_ref[:, lanes] = (dq * cqv).astype(BF16)
            dqab_ref[:, H * DA + hh * DA:H * DA + (hh + 1) * DA] = (dq * sqv).astype(BF16)
            dk = dk_ref[hh]
            dkpe = dkpe + dk
            dkv_ref[:, lanes] = (dk * keep_ref[...]).astype(BF16)
            dkv_ref[:, H * DA + hh * DA:H * DA + (hh + 1) * DA] = (dv_ref[hh] * keep_ref[...]).astype(BF16)
        dz3_ref[...] = (dkpe * ck_ref[...]).astype(BF16)
        dz15_ref[...] = (dkpe * sk_ref[...] + dft_ref[...]).astype(BF16)

    row = lambda i: (i, 0)
    heads = pl.BlockSpec((H, tm, DA), lambda i: (0, i, 0))
    tab = pl.BlockSpec((tm, DA), row)
    return pl.pallas_call(
        body, name=name, grid=(s // tm,),
        in_specs=[heads, heads, heads, tab, tab, tab, tab, tab, pl.BlockSpec((1, DA), lambda i: (0, 0))],
        out_specs=[pl.BlockSpec((tm, 2 * H * DA), row), pl.BlockSpec((tm, 2 * H * DA), row), tab, tab],
        out_shape=[jax.ShapeDtypeStruct((s, 2 * H * DA), BF16), jax.ShapeDtypeStruct((s, 2 * H * DA), BF16),
                   jax.ShapeDtypeStruct((s, DA), BF16), jax.ShapeDtypeStruct((s, DA), BF16)],
        compiler_params=_cp(("parallel",)),
    )(dqa, dka, dva, dft, cq, sq, ck, sk, keep)


def _fox_bwd_prep(dfqa, dfka, dfva, *, name, tm=512):
    s = dfqa.shape[1]
    tm = min(tm, s)
    place = lambda v: jnp.stack([_sel(DA, DA, [(r, r) for r in range(FOX_D)], v), _sel(DA, DA, [(r, FOX_D + r) for r in range(FOX_D)], v)])
    gq, gk = place(SCALE_FOX), place(1.0)

    def body(dq_ref, dk_ref, dv_ref, gq_ref, gk_ref, dz_ref):
        for part, (d_ref, g_ref) in enumerate(((dq_ref, gq_ref), (dk_ref, gk_ref), (dv_ref, gk_ref))):
            for p in range(H // 2):
                blk = _dot(d_ref[2 * p].astype(BF16), g_ref[0], NN) + _dot(d_ref[2 * p + 1].astype(BF16), g_ref[1], NN)
                lo = part * H * FOX_D + p * DA
                dz_ref[:, lo:lo + DA] = blk.astype(BF16)

    heads = pl.BlockSpec((H, tm, DA), lambda i: (0, i, 0))
    sel = pl.BlockSpec((2, DA, DA), lambda i: (0, 0, 0))
    return pl.pallas_call(
        body, name=name, grid=(s // tm,), in_specs=[heads, heads, heads, sel, sel],
        out_specs=pl.BlockSpec((tm, 3 * H * FOX_D), lambda i: (i, 0)),
        out_shape=jax.ShapeDtypeStruct((s, 3 * H * FOX_D), BF16), compiler_params=_cp(("parallel",)),
    )(dfqa, dfka, dfva, gq, gk)


def _lane_scan(x, s, reverse):
    lane = lax.broadcasted_iota(jnp.int32, x.shape, 1)
    sh = 1
    while sh < s:
        if reverse:
            x = x + jnp.where(lane < s - sh, pltpu.roll(x, s - sh, axis=1), 0.0)
        else:
            x = x + jnp.where(lane >= sh, pltpu.roll(x, sh, axis=1), 0.0)
        sh *= 2
    return x


def _gate_fwd(z, col_block, bias, *, name):
    s = z.shape[0]

    def body(z_ref, b_ref, f_ref, c_ref):
        ft = z_ref[...].T[0:8, :]
        f_ref[...] = ft
        xg = ft + b_ref[...]
        lf = jnp.minimum(xg, 0.0) - jnp.log(1.0 + jnp.exp(-jnp.abs(xg)))
        c = _lane_scan(lf, s, False)
        hi = c.astype(BF16).astype(F32)
        r = c - hi
        mid = r.astype(BF16).astype(F32)
        lo = r - mid
        c_ref[...] = jnp.concatenate([hi, mid, lo, jnp.zeros_like(hi)], axis=0).astype(BF16)

    return pl.pallas_call(
        body, name=name, grid=(1,),
        in_specs=[pl.BlockSpec((s, 128), lambda i: (0, col_block)), pl.BlockSpec((8, 1), lambda i: (0, 0))],
        out_specs=[pl.BlockSpec((8, s), lambda i: (0, 0)), pl.BlockSpec((32, s), lambda i: (0, 0))],
        out_shape=[jax.ShapeDtypeStruct((8, s), F32), jax.ShapeDtypeStruct((32, s), BF16)],
        compiler_params=_cp(("arbitrary",)))(z, bias)


def _gate_bwd(ft, bias, dc, *, name):
    s = ft.shape[1]

    def body(f_ref, b_ref, dc_ref, df_ref, db_ref):
        xg = f_ref[...] + b_ref[...]
        dlf = _lane_scan(dc_ref[...], s, True)
        df = dlf * _sigmoid(-xg)
        db_ref[...] = jnp.sum(df, axis=-1, keepdims=True)
        df_ref[...] = jnp.concatenate([df, jnp.zeros((DA - 8, s), F32)], axis=0).T

    return pl.pallas_call(body, name=name, out_shape=[jax.ShapeDtypeStruct((s, DA), F32), jax.ShapeDtypeStruct((8, 1), F32)],
                          compiler_params=_cp())(ft, bias, dc)


def _pool_lane_consts(tm, i):
    lane = lax.broadcasted_iota(jnp.int32, (tm, POOL_W), 1)
    tok = lax.broadcasted_iota(jnp.int32, (tm, POOL_W), 0) + i * tm
    win = jnp.where(lane < 64, 2, jnp.where(lane < 128, 4, jnp.where(lane < 192, 8, 16)))
    cnt = jnp.minimum(tok + 1, win).astype(F32)
    return lane, tok, cnt


def _pick_window(lane, s2, s4, s8, s16):
    return jnp.where(lane < 64, s2, jnp.where(lane < 128, s4, jnp.where(lane < 192, s8, s16)))


def _pool_fwd(z, col_block, bd, scale, *, name, tm=512):
    s = z.shape[0]
    tm = min(tm, s)
    hb = tm // POOL_HALO

    def body(u_ref, halo_ref, bd_ref, sc_ref, y_ref, p_ref, buf):
        i = pl.program_id(0)
        buf[0:POOL_HALO, :] = halo_ref[...] * (i > 0).astype(F32)
        buf[POOL_HALO:, :] = u_ref[...]

        def back(k):
            return buf[POOL_HALO - k:POOL_HALO - k + tm, :]

        u = u_ref[...]
        s2 = u + back(1)
        s4 = s2 + back(2) + back(3)
        s8 = s4 + back(4) + back(5) + back(6) + back(7)
        s16 = s8
        for k in range(8, 16):
            s16 = s16 + back(k)
        lane, _, cnt = _pool_lane_consts(tm, i)
        pooled = (_pick_window(lane, s2, s4, s8, s16) / cnt - u).astype(BF16)
        p_ref[...] = pooled
        y_ref[...] = _dot(pooled, bd_ref[...], NN) * sc_ref[...]

    return pl.pallas_call(
        body, name=name, grid=(s // tm,),
        in_specs=[pl.BlockSpec((tm, POOL_W), lambda i: (i, col_block)),
                  pl.BlockSpec((POOL_HALO, POOL_W), lambda i: (jnp.maximum(i * hb - 1, 0), col_block)),
                  pl.BlockSpec((POOL_W, POOL_W), lambda i: (0, 0)), pl.BlockSpec((1, POOL_W), lambda i: (0, 0))],
        out_specs=[pl.BlockSpec((tm, POOL_W), lambda i: (i, 0)), pl.BlockSpec((tm, POOL_W), lambda i: (i, 0))],
        out_shape=[jax.ShapeDtypeStruct((s, POOL_W), F32), jax.ShapeDtypeStruct((s, POOL_W), BF16)],
        scratch_shapes=[pltpu.VMEM((tm + POOL_HALO, POOL_W), F32)],
        compiler_params=_cp(("parallel",)),
    )(z, z, bd, scale.reshape(1, POOL_W))


def _pool_bwd_a(dy, pooled, bd, scale, *, name, tm=512):
    s = dy.shape[0]
    tm = min(tm, s)

    def body(dy_ref, p_ref, bd_ref, sc_ref, dq_ref, dys_ref, dsc_ref):
        i = pl.program_id(0)
        dyv = dy_ref[...]
        y0 = _dot(p_ref[...], bd_ref[...], NN)
        dys = (dyv * sc_ref[...]).astype(BF16)
        dys_ref[...] = dys
        dp = _dot(dys, bd_ref[...], NT)
        _, _, cnt = _pool_lane_consts(tm, i)
        dq_ref[:, 0:POOL_W] = dp / cnt
        dq_ref[:, POOL_W:] = dp

        @pl.when(i == 0)
        def _():
            dsc_ref[...] = jnp.zeros_like(dsc_ref)

        dsc_ref[...] += jnp.sum(dyv * y0, axis=0, keepdims=True)

    row = lambda i: (i, 0)
    dq, dys, dsc = pl.pallas_call(
        body, name=name, grid=(s // tm,),
        in_specs=[pl.BlockSpec((tm, POOL_W), row), pl.BlockSpec((tm, POOL_W), row),
                  pl.BlockSpec((POOL_W, POOL_W), lambda i: (0, 0)), pl.BlockSpec((1, POOL_W), lambda i: (0, 0))],
        out_specs=[pl.BlockSpec((tm, 2 * POOL_W), row), pl.BlockSpec((tm, POOL_W), row), pl.BlockSpec((1, POOL_W), lambda i: (0, 0))],
        out_shape=[jax.ShapeDtypeStruct((s, 2 * POOL_W), F32), jax.ShapeDtypeStruct((s, POOL_W), BF16),
                   jax.ShapeDtypeStruct((1, POOL_W), F32)],
        compiler_params=_cp(("arbitrary",)),
    )(dy, pooled, bd, scale.reshape(1, POOL_W))
    return dq, dys, dsc.reshape(POOL_W)


def _pool_bwd_b(dq, *, name, tm=512):
    s = dq.shape[0]
    tm = min(tm, s)
    hb = tm // POOL_HALO
    nblk = s // tm

    def body(q_ref, dp_ref, halo_ref, du_ref, buf):
        i = pl.program_id(0)
        buf[0:tm, :] = q_ref[...]
        buf[tm:, :] = halo_ref[...] * (i < nblk - 1).astype(F32)

        def ahead(k):
            return buf[k:k + tm, :]

        q = q_ref[...]
        s2 = q + ahead(1)
        s4 = s2 + ahead(2) + ahead(3)
        s8 = s4 + ahead(4) + ahead(5) + ahead(6) + ahead(7)
        s16 = s8
        for k in range(8, 16):
            s16 = s16 + ahead(k)
        lane = lax.broadcasted_iota(jnp.int32, (tm, POOL_W), 1)
        du_ref[...] = _pick_window(lane, s2, s4, s8, s16) - dp_ref[...]

    return pl.pallas_call(
        body, name=name, grid=(nblk,),
        in_specs=[pl.BlockSpec((tm, POOL_W), lambda i: (i, 0)), pl.BlockSpec((tm, POOL_W), lambda i: (i, 1)),
                  pl.BlockSpec((POOL_HALO, POOL_W), lambda i: (jnp.minimum((i + 1) * hb, nblk * hb - 1), 0))],
        out_specs=pl.BlockSpec((tm, POOL_W), lambda i: (i, 0)),
        out_shape=jax.ShapeDtypeStruct((s, POOL_W), F32),
        scratch_shapes=[pltpu.VMEM((tm + POOL_HALO, POOL_W), F32)],
        compiler_params=_cp(("parallel",)),
    )(dq, dq, dq)


def _loss_head(x, gain, target, *, name, tm=512):
    s = x.shape[0]
    tm = min(tm, s)

    def body(x_ref, g_ref, t_ref, dx_ref, dg_ref, loss_ref):
        xv = x_ref[...]
        r = lax.rsqrt(jnp.mean(xv * xv, axis=-1, keepdims=True) + EPS)
        xh = xv * r
        err = xh * g_ref[...] - t_ref[...]
        dy = err * (1.0 / D)
        a = dy * g_ref[...]
        dx_ref[...] = r * a - xh * (r * jnp.mean(a * xh, axis=-1, keepdims=True))

        @pl.when(pl.program_id(0) == 0)
        def _():
            dg_ref[...] = jnp.zeros_like(dg_ref)
            loss_ref[...] = jnp.zeros_like(loss_ref)

        dg_ref[...] += jnp.sum(dy * xh, axis=0, keepdims=True)
        part = 0.5 * jnp.sum(jnp.mean(err * err, axis=-1, keepdims=True), axis=0, keepdims=True)
        loss_ref[...] += jnp.broadcast_to(part, loss_ref.shape)

    row = lambda i: (i, 0)
    dx, dg, loss = pl.pallas_call(
        body, name=name, grid=(s // tm,),
        in_specs=[pl.BlockSpec((tm, D), row), pl.BlockSpec((1, D), lambda i: (0, 0)), pl.BlockSpec((tm, D), row)],
        out_specs=[pl.BlockSpec((tm, D), row), pl.BlockSpec((1, D), lambda i: (0, 0)), pl.BlockSpec((1, 128), lambda i: (0, 0))],
        out_shape=[jax.ShapeDtypeStruct((s, D), F32), jax.ShapeDtypeStruct((1, D), F32), jax.ShapeDtypeStruct((1, 128), F32)],
        compiler_params=_cp(("arbitrary",)),
    )(x, gain.reshape(1, D), target)
    return dx, dg.reshape(D), loss[0, 0]


def _adamw(w, g, m, v, *, name, tr=512):
    rows, cols = w.shape
    tr = min(tr, rows)
    assert rows % tr == 0, (name, rows, tr)
    c_m = 1.0 - ADAM_B1
    c_v = 1.0 - ADAM_B2
    bc1 = 1.0 - ADAM_B1 ** ADAM_STEP
    bc2 = 1.0 - ADAM_B2 ** ADAM_STEP

    def body(w_ref, g_ref, m_ref, v_ref, d_ref, mo_ref, vo_ref):
        gv = g_ref[...]
        mn = ADAM_B1 * m_ref[...] + c_m * gv
        vn = ADAM_B2 * v_ref[...] + c_v * (gv * gv)
        mo_ref[...] = mn
        vo_ref[...] = vn
        d_ref[...] = -ADAM_LR * ((mn / bc1) / (jnp.sqrt(vn / bc2) + ADAM_EPS) + ADAM_WD * w_ref[...])

    spec = pl.BlockSpec((tr, cols), lambda i: (i, 0))
    return pl.pallas_call(body, name=name, grid=(rows // tr,), in_specs=[spec] * 4, out_specs=[spec] * 3,
                          out_shape=[jax.ShapeDtypeStruct((rows, cols), F32)] * 3,
                          compiler_params=_cp(("parallel",)))(w, g, m, v)


def _adamw_layer(w, g, m, v, layer, prev, *, name, tr):
    rows, cols = g.shape
    assert rows % tr == 0 and w.shape == (DEPTH * rows, cols), (name, w.shape, g.shape, tr)
    nblk = rows // tr
    c_m = 1.0 - ADAM_B1
    c_v = 1.0 - ADAM_B2
    bc1 = 1.0 - ADAM_B1 ** ADAM_STEP
    bc2 = 1.0 - ADAM_B2 ** ADAM_STEP
    n_prev = 0 if prev is None else 4

    def body(*refs):
        w_ref, g_ref, m_ref, v_ref = refs[:4]
        d_ref, mo_ref, vo_ref, go_ref = refs[4 + n_prev:]
        gv = g_ref[...]
        mn = ADAM_B1 * m_ref[...] + c_m * gv
        vn = ADAM_B2 * v_ref[...] + c_v * (gv * gv)
        mo_ref[...] = mn
        vo_ref[...] = vn
        go_ref[...] = gv
        d_ref[...] = -ADAM_LR * ((mn / bc1) / (jnp.sqrt(vn / bc2) + ADAM_EPS) + ADAM_WD * w_ref[...])

    stacked = pl.BlockSpec((tr, cols), lambda i: (layer * nblk + i, 0))
    args = [w, g, m, v] + ([] if prev is None else list(prev))
    return pl.pallas_call(
        body, name=name, grid=(nblk,),
        in_specs=[stacked, pl.BlockSpec((tr, cols), lambda i: (i, 0)), stacked, stacked] + [ANY_SPEC] * n_prev,
        out_specs=[stacked] * 4, out_shape=[jax.ShapeDtypeStruct(w.shape, F32)] * 4,
        input_output_aliases={4 + k: k for k in range(n_prev)},
        compiler_params=_cp(("parallel",)))(*args)


def _position():
    return jnp.stack([lax.axis_index("c"), 2 * lax.axis_index("x") + lax.axis_index("y")]).astype(jnp.int32)


SUM_ROW_TILES = 2


def _sum2_bf16(pos, fulls, sibs, *, name):
    n = len(fulls)
    nb = SUM_ROW_TILES

    def body(pos_ref, *refs):
        for t in range(n):
            refs[2 * n + t][...] = (refs[t][...] + refs[n + t][...]).astype(BF16)

    in_specs, sib_specs = [], []
    for sb in sibs:
        _, half, cols = sb.shape
        tr = half // nb
        assert half % nb == 0 and tr % 16 == 0, sb.shape
        in_specs.append(pl.BlockSpec((None, tr, cols), lambda j, i, p: (j, p[0] * nb + i, 0)))
        sib_specs.append(pl.BlockSpec((None, tr, cols), lambda j, i, p: (j, i, 0)))
    return pl.pallas_call(
        body, name=name,
        grid_spec=pltpu.PrefetchScalarGridSpec(num_scalar_prefetch=1, grid=(N_CHIPS, nb), in_specs=in_specs + sib_specs,
                                               out_specs=sib_specs),
        out_shape=[jax.ShapeDtypeStruct(sb.shape, BF16) for sb in sibs],
        compiler_params=_cp(("parallel", "parallel")))(pos, *fulls, *sibs)


def _sum5(pos, fulls, sibs, recvs, *, name):
    n = len(fulls)
    nb = SUM_ROW_TILES

    def body(pos_ref, *refs):
        for t in range(n):
            acc = refs[t][...] + refs[n + t][...]
            for kk in range(3):
                acc = acc + refs[2 * n + t][kk].astype(F32)
            refs[3 * n + t][...] = acc

    f_specs, s_specs, r_specs, o_specs = [], [], [], []
    for f in fulls:
        _, rows, cols = f.shape
        tr = rows // 2 // nb
        f_specs.append(pl.BlockSpec((None, tr, cols), lambda i, p: (p[1], p[0] * nb + i, 0)))
        s_specs.append(pl.BlockSpec((None, tr, cols), lambda i, p: (p[1], i, 0)))
        r_specs.append(pl.BlockSpec((3, tr, cols), lambda i, p: (0, i, 0)))
        o_specs.append(pl.BlockSpec((tr, cols), lambda i, p: (p[0] * nb + i, 0)))
    return pl.pallas_call(
        body, name=name,
        grid_spec=pltpu.PrefetchScalarGridSpec(num_scalar_prefetch=1, grid=(nb,), in_specs=f_specs + s_specs + r_specs,
                                               out_specs=o_specs),
        out_shape=[jax.ShapeDtypeStruct(f.shape[1:], F32) for f in fulls],
        compiler_params=_cp(("parallel",)))(pos, *fulls, *sibs, *recvs)


def _place():
    x, y, c = lax.axis_index("x"), lax.axis_index("y"), lax.axis_index("c")
    chips = [(1 - x, y), (x, 1 - y), (1 - x, 1 - y)]
    return x, y, c, 2 * x + y, chips


SEM_SPEC = pl.BlockSpec(memory_space=pltpu.SEMAPHORE)
ANY_SPEC = pl.BlockSpec(memory_space=pl.ANY)


def _gather_copies(ins, outs, send_i, recv_i, send_o, recv_o):
    x, y, c, me, chips = _place()
    n = len(ins)
    started, awaited = [], []
    for t in range(n):
        half = ins[t].shape[0] // 2
        mine = pl.ds(c * half, half)
        started.append(pltpu.make_async_remote_copy(
            src_ref=ins[t], dst_ref=outs[t].at[me], send_sem=send_o.at[t], recv_sem=recv_o.at[t],
            device_id=(x, y, 1 - c), device_id_type=MESH))
        awaited.append(started[-1])
        for kk, (px, py) in enumerate(chips):
            started.append(pltpu.make_async_remote_copy(
                src_ref=ins[t].at[mine], dst_ref=outs[t].at[me, mine], send_sem=send_i.at[t * 3 + kk],
                recv_sem=recv_i.at[t * 3 + kk], device_id=(px, py, c), device_id_type=MESH))
            awaited.append(pltpu.make_async_remote_copy(
                src_ref=ins[t].at[mine], dst_ref=outs[t].at[2 * px + py, mine], send_sem=send_i.at[t * 3 + kk],
                recv_sem=recv_i.at[t * 3 + kk], device_id=(px, py, c), device_id_type=MESH))
    return started, awaited


def _forward_copies(outs, send_d, recv_d):
    x, y, c, me, chips = _place()
    started, awaited = [], []
    for t in range(len(outs)):
        half = outs[t].shape[1] // 2
        for kk, (px, py) in enumerate(chips):
            for lst, hc in ((started, c), (awaited, 1 - c)):
                blk = outs[t].at[2 * px + py, pl.ds(hc * half, half)]
                lst.append(pltpu.make_async_remote_copy(src_ref=blk, dst_ref=blk, send_sem=send_d.at[t * 3 + kk],
                                                        recv_sem=recv_d.at[t * 3 + kk], device_id=(x, y, 1 - c), device_id_type=MESH))
    return started, awaited


def _gather_blocking(shards):
    n = len(shards)

    def body(*refs):
        ins, outs = refs[:n], refs[n:2 * n]
        send_i, recv_i, send_d, recv_d, send_o, recv_o = refs[2 * n:]
        started, awaited = _gather_copies(ins, outs, send_i, recv_i, send_o, recv_o)
        for cp in started:
            cp.start()
        for cp in awaited:
            cp.wait_recv()
        fwd, fwd_in = _forward_copies(outs, send_d, recv_d)
        for cp in fwd:
            cp.start()
        for cp in fwd_in:
            cp.wait_recv()
        for cp in started + fwd:
            cp.wait_send()

    return pl.pallas_call(
        body, name="gather_first", in_specs=[HBM_SPEC] * n, out_specs=[HBM_SPEC] * n,
        out_shape=[jax.ShapeDtypeStruct((N_CHIPS,) + s.shape, s.dtype) for s in shards],
        scratch_shapes=[pltpu.SemaphoreType.DMA((3 * n,)), pltpu.SemaphoreType.DMA((3 * n,)),
                        pltpu.SemaphoreType.DMA((3 * n,)), pltpu.SemaphoreType.DMA((3 * n,)),
                        pltpu.SemaphoreType.DMA((n,)), pltpu.SemaphoreType.DMA((n,))],
    )(*shards)


def _gather_start(shards, after, tag):
    n = len(shards)

    def body(*refs):
        ins = refs[:n]
        send_i, recv_i, send_o, recv_o = refs[2 * n + 1:2 * n + 5]
        outs = refs[3 * n + 5:4 * n + 5]
        token = refs[4 * n + 5]
        started, _ = _gather_copies(ins, outs, send_i, recv_i, send_o, recv_o)
        for cp in started:
            cp.start()
        token[...] = jnp.zeros_like(token)

    lands = [lax.empty((N_CHIPS,) + s.shape, s.dtype) for s in shards]
    sems = [pltpu.SemaphoreType.DMA((3 * n,)), pltpu.SemaphoreType.DMA((3 * n,)), pltpu.SemaphoreType.DMA((n,)), pltpu.SemaphoreType.DMA((n,))]
    res = pl.pallas_call(
        body, name=f"gather_{tag}_start",
        in_specs=[HBM_SPEC] * (2 * n) + [ANY_SPEC],
        out_specs=[SEM_SPEC] * 4 + [HBM_SPEC] * (2 * n) + [pl.BlockSpec(memory_space=pltpu.VMEM)],
        out_shape=sems + [jax.ShapeDtypeStruct(s.shape, s.dtype) for s in shards]
        + [jax.ShapeDtypeStruct(a.shape, a.dtype) for a in lands] + [jax.ShapeDtypeStruct((8, 128), F32)],
        input_output_aliases={t: 4 + t for t in range(2 * n)},
        compiler_params=pltpu.CompilerParams(has_side_effects=pltpu.SideEffectType.DATAFLOW_SIDE_EFFECTING),
    )(*[pltpu.with_memory_space_constraint(s, pltpu.HBM) for s in shards],
      *[pltpu.with_memory_space_constraint(a, pltpu.HBM) for a in lands], after)
    return res[:4], res[4:4 + n], res[4 + n:4 + 2 * n], res[-1]


def _gather_wait(sems, shards_thru, lands_thru, after, tag):
    n = len(shards_thru)

    def body(*refs):
        ins, outs_in = refs[:n], refs[n:2 * n]
        send_i, recv_i, send_o, recv_o = refs[2 * n:2 * n + 4]
        started, awaited = _gather_copies(ins, outs_in, send_i, recv_i, send_o, recv_o)
        for cp in started:
            cp.wait_send()
        for cp in awaited:
            cp.wait_recv()

    res = pl.pallas_call(
        body, name=f"gather_{tag}_wait",
        in_specs=[HBM_SPEC] * (2 * n) + [SEM_SPEC] * 4 + [ANY_SPEC],
        out_specs=[HBM_SPEC] * (2 * n),
        out_shape=[jax.ShapeDtypeStruct(a.shape, a.dtype) for a in list(shards_thru) + list(lands_thru)],
        input_output_aliases={t: t for t in range(2 * n)},
        compiler_params=pltpu.CompilerParams(has_side_effects=pltpu.SideEffectType.DATAFLOW_SIDE_EFFECTING),
    )(*shards_thru, *lands_thru, *sems, after)
    return res[n:]


def _gather_forward(lands, tag):
    n = len(lands)

    def body(*refs):
        outs = refs[n:2 * n]
        send_d, recv_d = refs[2 * n:]
        fwd, fwd_in = _forward_copies(outs, send_d, recv_d)
        for cp in fwd:
            cp.start()
        for cp in fwd_in:
            cp.wait_recv()
        for cp in fwd:
            cp.wait_send()

    return pl.pallas_call(
        body, name=f"gather_{tag}_forward", in_specs=[HBM_SPEC] * n, out_specs=[HBM_SPEC] * n,
        out_shape=[jax.ShapeDtypeStruct(a.shape, a.dtype) for a in lands],
        input_output_aliases={t: t for t in range(n)},
        scratch_shapes=[pltpu.SemaphoreType.DMA((3 * n,)), pltpu.SemaphoreType.DMA((3 * n,))],
    )(*lands)


def _stage1_copies(ins, sib, send, recv):
    x, y, c, me, chips = _place()
    cps = []
    for t in range(len(ins)):
        rows = ins[t].shape[1] // 2
        cps.append(pltpu.make_async_remote_copy(
            src_ref=ins[t].at[:, pl.ds((1 - c) * rows, rows), :], dst_ref=sib[t], send_sem=send.at[t],
            recv_sem=recv.at[t], device_id=(x, y, 1 - c), device_id_type=MESH))
    return cps


def _split_start(copies_fn, srcs, land_shapes, n_sems, tag):
    n = len(srcs)

    def body(*refs):
        send, recv = refs[2 * n:2 * n + 2]
        for cp in copies_fn(refs[:n], refs[3 * n + 2:4 * n + 2], send, recv):
            cp.start()
        refs[4 * n + 2][...] = jnp.zeros_like(refs[4 * n + 2])

    lands = [lax.empty(shp, dt) for shp, dt in land_shapes]
    res = pl.pallas_call(
        body, name=tag,
        in_specs=[HBM_SPEC] * (2 * n),
        out_specs=[SEM_SPEC] * 2 + [HBM_SPEC] * (2 * n) + [pl.BlockSpec(memory_space=pltpu.VMEM)],
        out_shape=[pltpu.SemaphoreType.DMA((n_sems,)), pltpu.SemaphoreType.DMA((n_sems,))]
        + [jax.ShapeDtypeStruct(p.shape, p.dtype) for p in srcs]
        + [jax.ShapeDtypeStruct(a.shape, a.dtype) for a in lands] + [jax.ShapeDtypeStruct((8, 128), F32)],
        input_output_aliases={t: 2 + t for t in range(2 * n)},
        compiler_params=pltpu.CompilerParams(has_side_effects=pltpu.SideEffectType.DATAFLOW_SIDE_EFFECTING),
    )(*[pltpu.with_memory_space_constraint(p, pltpu.HBM) for p in srcs],
      *[pltpu.with_memory_space_constraint(a, pltpu.HBM) for a in lands])
    return res[:2], res[2:2 + n], res[2 + n:2 + 2 * n], res[-1]


def _split_wait(copies_fn, sems, srcs_thru, lands_thru, after, tag):
    n = len(srcs_thru)

    def body(*refs):
        for cp in copies_fn(refs[:n], refs[n:2 * n], refs[2 * n], refs[2 * n + 1]):
            cp.wait()

    res = pl.pallas_call(
        body, name=tag,
        in_specs=[HBM_SPEC] * (2 * n) + [SEM_SPEC] * 2 + [ANY_SPEC],
        out_specs=[HBM_SPEC] * (2 * n),
        out_shape=[jax.ShapeDtypeStruct(a.shape, a.dtype) for a in list(srcs_thru) + list(lands_thru)],
        input_output_aliases={t: t for t in range(2 * n)},
        compiler_params=pltpu.CompilerParams(has_side_effects=pltpu.SideEffectType.DATAFLOW_SIDE_EFFECTING),
    )(*srcs_thru, *lands_thru, *sems, after)
    return res[:n], res[n:]


def _stage2_copies(ps, rcv, send, recv):
    x, y, c, me, chips = _place()
    return [pltpu.make_async_remote_copy(
        src_ref=ps[t].at[2 * px + py], dst_ref=rcv[t].at[kk], send_sem=send.at[t * 3 + kk],
        recv_sem=recv.at[t * 3 + kk], device_id=(px, py, c), device_id_type=MESH)
        for t in range(len(ps)) for kk, (px, py) in enumerate(chips)]


def _reduce_stage3(reduced, tag):
    n = len(reduced)

    def body(*refs):
        outs = refs[n:2 * n]
        send, recv = refs[2 * n:]
        x, y, c, me, chips = _place()
        cps = []
        for t in range(n):
            rows = outs[t].shape[0] // 2
            mine = outs[t].at[pl.ds(c * rows, rows), :]
            cp = pltpu.make_async_remote_copy(src_ref=mine, dst_ref=mine, send_sem=send.at[t], recv_sem=recv.at[t],
                                              device_id=(x, y, 1 - c), device_id_type=MESH)
            cp.start()
            cps.append(cp)
        for cp in cps:
            cp.wait()

    return pl.pallas_call(
        body, name="reduce_stage3_" + tag, in_specs=[HBM_SPEC] * n, out_specs=[HBM_SPEC] * n,
        out_shape=[jax.ShapeDtypeStruct(r.shape, r.dtype) for r in reduced],
        input_output_aliases={t: t for t in range(n)},
        scratch_shapes=[pltpu.SemaphoreType.DMA((n,)), pltpu.SemaphoreType.DMA((n,))],
    )(*reduced)


def _allreduce_small(v):
    rows, cols = v.shape

    def body(v_ref, o_ref, buf, send, recv, loc):
        x, y, c, me, chips = _place()
        mine = 4 * x + 2 * y + c
        lc = pltpu.make_async_copy(v_ref, buf.at[mine], loc)
        lc.start()
        peers = []
        for fx in range(2):
            for fy in range(2):
                for fc in range(2):
                    if fx or fy or fc:
                        peers.append((fx, fy, fc))
        cps = []
        for kk, (fx, fy, fc) in enumerate(peers):
            to = (x ^ fx, y ^ fy, c ^ fc)
            cp = pltpu.make_async_remote_copy(src_ref=v_ref, dst_ref=buf.at[mine], send_sem=send.at[kk], recv_sem=recv.at[kk],
                                              device_id=to, device_id_type=MESH)
            cp.start()
            cps.append((cp, to))
        for kk, (cp, to) in enumerate(cps):
            src = 4 * to[0] + 2 * to[1] + to[2]
            pltpu.make_async_remote_copy(src_ref=v_ref, dst_ref=buf.at[src], send_sem=send.at[kk], recv_sem=recv.at[kk],
                                         device_id=to, device_id_type=MESH).wait_recv()
        for cp, _ in cps:
            cp.wait_send()
        lc.wait()
        acc = buf[0]
        for d in range(1, 8):
            acc = acc + buf[d]
        o_ref[...] = acc

    return pl.pallas_call(
        body, name="allreduce_small", in_specs=[pl.BlockSpec(memory_space=pltpu.VMEM)],
        out_specs=pl.BlockSpec(memory_space=pltpu.VMEM), out_shape=jax.ShapeDtypeStruct((rows, cols), F32),
        scratch_shapes=[pltpu.VMEM((8, rows, cols), F32), pltpu.SemaphoreType.DMA((7,)), pltpu.SemaphoreType.DMA((7,)),
                        pltpu.SemaphoreType.DMA],
        compiler_params=pltpu.CompilerParams(vmem_limit_bytes=VMEM_LIMIT_V7X),
    )(v)


def _pad_w_in(w):
    z = lambda n: jnp.zeros(w.shape[:-1] + (n,), w.dtype)
    return jnp.concatenate([w[..., 0:384], z(64), w[..., 384:416], z(32), w[..., 416:1824],
                            w[..., 1824:1830], z(58), w[..., 400:416], w[..., 384:400], z(32)], axis=-1)


def _unpad_w_in(g):
    x1 = g[..., 448:464] + g[..., Z_F + 80:Z_F + 96]
    x2 = g[..., 464:480] + g[..., Z_F + 64:Z_F + 80]
    return jnp.concatenate([g[..., 0:384], x1, x2, g[..., 512:1920], g[..., 1920:1926]], axis=-1)


def _block_diag(pw):
    out = jnp.zeros((POOL_W, POOL_W), pw.dtype)
    for g in range(4):
        out = out.at[g * 64:(g + 1) * 64, g * 64:(g + 1) * 64].set(pw[g])
    return out


def _rope_tables(s):
    inv_freq = ROPE_THETA ** (-jnp.arange(0, ROPE, 2, dtype=F32) / ROPE)
    ang = jnp.arange(s, dtype=jnp.int32).astype(F32)[:, None] * inv_freq[None, :]
    cos, sin = jnp.cos(ang), jnp.sin(ang)
    zero = lambda n: jnp.zeros((s, n), F32)
    ck = jnp.concatenate([zero(NOPE), cos, cos, zero(DA - NOPE - ROPE)], axis=1)
    sk = jnp.concatenate([zero(NOPE), -sin, sin, zero(DA - NOPE - ROPE)], axis=1)
    cq = jnp.concatenate([jnp.ones((s, NOPE), F32), cos, cos, zero(DA - NOPE - ROPE)], axis=1) * SCALE_MLA
    return dict(cq=cq, sq=sk * SCALE_MLA, ck=ck, sk=sk)


def _mix_fwd(l, x1, wts, sm, tabs):
    z, h2, qn, kvn, qa, ka, va = _mix_in(x1, (sm["mix_norm"][l], sm["q_a_norm"][l], sm["kv_a_norm"][l]), wts["w_in"][l],
                                         wts["wq_a"][l], wts["wq_b"][l], wts["wk"][l], wts["wv"][l], tabs, name=f"mix_in_{l}")
    oa, lse_a = _attn_fwd(qa, ka, va, VDIM, name=f"mla_attn_{l}")

    bd = _block_diag(wts["pool_w"][l]).astype(BF16)
    yb, pooled = _pool_fwd(z, Z_POOL // POOL_W, bd, sm["pool_scale"][l], name=f"pool_{l}")

    fb = jnp.pad(sm["fox_b_f"][l], (0, 8 - H)).reshape(8, 1)
    ft, c3t = _gate_fwd(z, Z_F // DA, fb, name=f"fox_gate_{l}")
    fqa, fka, fva = _fox_prep(z, c3t, name=f"fox_prep_{l}")
    oc, lse_c = _attn_fwd(fqa, fka, fva, FOX_D, name=f"fox_attn_{l}")

    x2, cat = _mix_out(oa, yb, oc, wts["w_out"][l], x1, name=f"mix_out_{l}")
    saved = dict(z=z, h2=h2, qn=qn, kvn=kvn, qa=qa, ka=ka, va=va, oa=oa, lse_a=lse_a, bd=bd, pooled=pooled,
                 fqa=fqa, fka=fka, fva=fva, ft=ft, fb=fb, oc=oc, lse_c=lse_c, cat=cat)
    return x2, saved


def _mix_bwd(l, x1, dx2, sv, wts, sm, tabs, tok=None):
    s = x1.shape[0]
    g = {}
    dx2b = (dx2 if tok is None else dx2 + tok).astype(BF16)
    g["w_out"] = _mm(sv["cat"], dx2b, "tn", name=f"d_w_out_{l}", tm=1024, tn=1024, tk=DW_TOKENS)
    doa, doc, dyb, dl_a, dl_c = _mix_out_bwd(dx2b, wts["w_out"][l], sv["oa"], sv["oc"], name=f"mix_out_bwd_{l}")

    dfqa, dfka, dfva, dcq, dck = _attn_bwd(sv["fqa"], sv["fka"], sv["fva"], doc, sv["lse_c"], dl_c, True, name=f"fox_attn_bwd_{l}")
    dfox = _fox_bwd_prep(dfqa, dfka, dfva, name=f"fox_bwd_prep_{l}")
    dc = jnp.pad(dcq.reshape(H, s) + dck.reshape(H, s), ((0, 8 - H), (0, 0)))
    dft, dfb = _gate_bwd(sv["ft"], sv["fb"], dc, name=f"fox_gate_bwd_{l}")
    g["fox_b_f"] = dfb[:H, 0]

    dq, dys, g["pool_scale"] = _pool_bwd_a(dyb, sv["pooled"], sv["bd"], sm["pool_scale"][l], name=f"pool_bwd_a_{l}")
    du = _pool_bwd_b(dq, name=f"pool_bwd_b_{l}")
    dbd = _mm(sv["pooled"], dys, "tn", name=f"d_pool_w_{l}")
    g["pool_w"] = jnp.stack([dbd[i * 64:(i + 1) * 64, i * 64:(i + 1) * 64] for i in range(4)])

    dqa_, dka_, dva_ = _attn_bwd(sv["qa"], sv["ka"], sv["va"], doa, sv["lse_a"], dl_a, False, name=f"mla_attn_bwd_{l}")
    dqab, dkv, dz3, dz15 = _mla_bwd_prep(dqa_, dka_, dva_, dft, tabs["cq"], tabs["sq"], tabs["ck"], tabs["sk"],
                                         name=f"mla_bwd_prep_{l}")
    wq_ab = jnp.concatenate([wts["wq_a"][l], wts["wq_b"][l]], axis=1)
    wkv = jnp.concatenate([wts["wk"][l], wts["wv"][l]], axis=1)
    dwq = _mm(sv["qn"], dqab, "tn", name=f"d_w_q_b_{l}", tn=768, tk=DW_TOKENS).reshape(Q_RANK, 2, H, DA)
    dwkv = _mm(sv["kvn"], dkv, "tn", name=f"d_w_kv_b_{l}", tn=768, tk=DW_TOKENS).reshape(KV_RANK, 2, H, DA)
    da, db = dwq[:, 0], dwq[:, 1]
    swapped = jnp.concatenate([jnp.zeros((Q_RANK, H, NOPE), F32), db[..., NOPE + HALF_ROPE:NOPE + ROPE],
                               db[..., NOPE:NOPE + HALF_ROPE]], axis=-1)
    g["w_q_b"] = (da[..., :NOPE + ROPE] + swapped).reshape(Q_RANK, H * (NOPE + ROPE))
    g["w_kv_b"] = jnp.concatenate([dwkv[:, 0, :, :NOPE], dwkv[:, 1, :, :VDIM]], axis=-1).reshape(KV_RANK, H * (NOPE + VDIM))
    dqa, g["q_a_norm"] = _rmsnorm_bwd(sv["z"], Z_QA // Q_RANK, sm["q_a_norm"][l], dqab, wq_ab, name=f"q_a_norm_bwd_{l}")
    dkva, g["kv_a_norm"] = _rmsnorm_bwd(sv["z"], Z_KVA // KV_RANK, sm["kv_a_norm"][l], dkv, wkv, name=f"kv_a_norm_bwd_{l}")

    dz = jnp.concatenate([dqa.astype(BF16), dkva.astype(BF16), dz3, du.astype(BF16), dfox, dz15], axis=1)
    g["w_in"] = _mm(sv["h2"], dz, "tn", name=f"d_w_in_{l}", tm=1024, tn=1024, tk=DW_TOKENS)
    dx1, g["mix_norm"] = _rmsnorm_bwd(x1, 0, sm["mix_norm"][l], dz, wts["w_in"][l], dx2, name=f"mix_norm_bwd_{l}")
    return dx1, g


DW_TOKENS = 2048


def _local_step(x, target, wts, sm, late_weights=None, grads_ready=None):
    s = x.shape[0]
    tabs = _rope_tables(s)
    acts = []
    xs = x
    for l in range(DEPTH):
        x1, gu1, act1 = _ffn_fwd(xs, sm["ffn1_norm"][l], wts["ffn1_w_gu"][l], wts["ffn1_w_d2"][l], name=f"ffn1_fwd_{l}")
        if l == 0 and late_weights is not None:
            sm = late_weights("ffn1", x1, sm)
        x2, sv = _mix_fwd(l, x1, wts, sm, tabs)
        if l == 0 and late_weights is not None:
            sm = late_weights("mix", x2, sm)
        x3, gu2, act2 = _ffn_fwd(x2, sm["ffn2_norm"][l], wts["ffn2_w_gu"][l], wts["ffn2_w_d2"][l], name=f"ffn2_fwd_{l}")
        acts.append((xs, gu1, act1, x1, sv, x2, gu2, act2))
        xs = x3
    dx, g_final, loss = _loss_head(xs, sm["final_norm"], target, name="loss_head")
    grads = [dict() for _ in range(DEPTH)]
    for l in reversed(range(DEPTH)):
        x0, gu1, act1, x1, sv, x2, gu2, act2 = acts[l]
        g = grads[l]
        dx, dgu, hh, dy, g["ffn2_norm"] = _ffn_bwd(x2, dx, gu2, sm["ffn2_norm"][l], wts["ffn2_w_gu"][l], wts["ffn2_w_d2"][l],
                                                   name=f"ffn2_bwd_{l}")
        g["ffn2_w_down"] = _mm(act2, dy, "tn", name=f"d_ffn2_w_down_{l}", tm=FF_SHARD, tn=1024, tk=DW_TOKENS)
        g["ffn2_w_gu"] = _mm(hh, dgu, "tn", name=f"d_ffn2_w_gu_{l}", tm=1024, tn=FF_SHARD, tk=DW_TOKENS, n_major_out=True)
        tok = None
        if grads_ready is not None:
            sm, tok = grads_ready(l, "ffn2", g, sm)
        dx, gm = _mix_bwd(l, x1, dx, sv, wts, sm, tabs, tok)
        g.update(gm)
        if grads_ready is not None:
            sm, _ = grads_ready(l, "mix", g, sm)
        dx, dgu, hh, dy, g["ffn1_norm"] = _ffn_bwd(x0, dx, gu1, sm["ffn1_norm"][l], wts["ffn1_w_gu"][l], wts["ffn1_w_d2"][l],
                                                   name=f"ffn1_bwd_{l}")
        if grads_ready is not None:
            sm, tok = grads_ready(l, "ffn1_tokens", {"dx": dx}, sm)
            if tok is not None:
                dy = dy + tok.astype(BF16)
        g["ffn1_w_down"] = _mm(act1, dy, "tn", name=f"d_ffn1_w_down_{l}", tm=FF_SHARD, tn=1024, tk=DW_TOKENS)
        g["ffn1_w_gu"] = _mm(hh, dgu, "tn", name=f"d_ffn1_w_gu_{l}", tm=1024, tn=FF_SHARD, tk=DW_TOKENS, n_major_out=True)
        if grads_ready is not None:
            sm, _ = grads_ready(l, "ffn1", g, sm)
    return loss, dx, grads, g_final


BIG = ["ffn1_w_gu", "ffn1_w_down", "w_in", "w_q_b", "w_kv_b", "w_out", "ffn2_w_gu", "ffn2_w_down"]
SMALL = ["ffn1_norm", "mix_norm", "q_a_norm", "kv_a_norm", "pool_w", "pool_scale", "fox_b_f", "ffn2_norm"]
SMALL_ROWS = 48


WEIGHT_VIEWS = ["ffn1_w_gu", "ffn1_w_d2", "w_in", "wq_a", "wq_b", "wk", "wv", "w_out", "ffn2_w_gu", "ffn2_w_d2"]


def _prepare_weights(gathered, wts):
    for (nm, l), w in gathered.items():
        if nm in ("ffn1_w_gu", "ffn2_w_gu"):
            wts[nm][l] = w
        elif nm in ("ffn1_w_down", "ffn2_w_down"):
            wts[nm[:5] + "w_d2"][l] = w.reshape(2, FF_SHARD, D)
        elif nm in ("w_in", "w_out"):
            wts[nm][l] = w.reshape(D, -1)
        elif nm == "w_q_b":
            wq = jnp.moveaxis(w, 0, 1).reshape(Q_RANK, H, NOPE + ROPE)
            zq = lambda n: jnp.zeros((Q_RANK, H, n), BF16)
            wts["wq_a"][l] = jnp.concatenate([wq, zq(DA - NOPE - ROPE)], axis=-1).reshape(Q_RANK, H * DA)
            wts["wq_b"][l] = jnp.concatenate([zq(NOPE), wq[..., NOPE + HALF_ROPE:], wq[..., NOPE:NOPE + HALF_ROPE],
                                              zq(DA - NOPE - ROPE)], axis=-1).reshape(Q_RANK, H * DA)
        else:
            wkv = jnp.moveaxis(w, 0, 1).reshape(KV_RANK, H, NOPE + VDIM)
            zk = jnp.zeros((KV_RANK, H, DA - NOPE), BF16)
            wts["wk"][l] = jnp.concatenate([wkv[..., :NOPE], zk], axis=-1).reshape(KV_RANK, H * DA)
            wts["wv"][l] = jnp.concatenate([wkv[..., NOPE:], zk], axis=-1).reshape(KV_RANK, H * DA)


def _chip_major(name, g):
    if name in ("ffn1_w_gu", "ffn2_w_gu"):
        return g
    if name in ("ffn1_w_down", "ffn2_w_down", "w_in", "w_out"):
        return g.reshape(N_CHIPS, g.shape[0] // N_CHIPS, g.shape[1])
    return jnp.moveaxis(g.reshape(g.shape[0], N_CHIPS, g.shape[1] // N_CHIPS), 1, 0)


def _pack_small(grads, g_final, loss):
    parts = []
    for l in range(DEPTH):
        for nm in SMALL:
            parts.append(grads[l][nm].reshape(-1))
    parts.append(g_final.reshape(-1))
    parts.append(loss.reshape(1))
    flat = jnp.concatenate(parts)
    return jnp.pad(flat, (0, SMALL_ROWS * D - flat.shape[0])).reshape(SMALL_ROWS, D)


def _unpack_small(packed, params):
    flat = packed.reshape(-1)
    out = {nm: [] for nm in SMALL}
    off = 0
    for l in range(DEPTH):
        for nm in SMALL:
            shp = params[nm].shape[1:]
            n = int(np.prod(shp))
            out[nm].append(flat[off:off + n].reshape(shp))
            off += n
    res = {nm: jnp.stack(v) for nm, v in out.items()}
    res["final_norm"] = flat[off:off + D]
    return res, flat[off + D]


def _update(name, w, g, m, v):
    shp = w.shape
    if w.ndim == 1:
        view = (1, shp[0])
    elif w.size <= 65536:
        view = (shp[0], w.size // shp[0])
    else:
        view = (w.size // shp[-1], shp[-1])
    tr = view[0]
    for cand in (512, 352, 256, 128):
        if view[0] % cand == 0 and view[0] > cand:
            tr = cand
            break
    d, mn, vn = _adamw(w.reshape(view), g.reshape(view), m.reshape(view), v.reshape(view), name="adamw_" + name, tr=tr)
    return d.reshape(shp), mn.reshape(shp), vn.reshape(shp)


WEIGHTS = ['ffn1_norm', 'ffn1_w_gu', 'ffn1_w_down', 'mix_norm', 'w_in', 'q_a_norm', 'w_q_b', 'kv_a_norm', 'w_kv_b', 'pool_w',
           'pool_scale', 'fox_b_f', 'w_out', 'ffn2_norm', 'ffn2_w_gu', 'ffn2_w_down', 'final_norm']


def kernel(x, ffn1_norm, ffn1_w_gu, ffn1_w_down, mix_norm, w_in, q_a_norm, w_q_b, kv_a_norm, w_kv_b, pool_w, pool_scale, fox_b_f, w_out, ffn2_norm, ffn2_w_gu, ffn2_w_down, final_norm, loss_target, m_ffn1_norm, m_ffn1_w_gu, m_ffn1_w_down, m_mix_norm, m_w_in, m_q_a_norm, m_w_q_b, m_kv_a_norm, m_w_kv_b, m_pool_w, m_pool_scale, m_fox_b_f, m_w_out, m_ffn2_norm, m_ffn2_w_gu, m_ffn2_w_down, m_final_norm, v_ffn1_norm, v_ffn1_w_gu, v_ffn1_w_down, v_mix_norm, v_w_in, v_q_a_norm, v_w_q_b, v_kv_a_norm, v_w_kv_b, v_pool_w, v_pool_scale, v_fox_b_f, v_w_out, v_ffn2_norm, v_ffn2_w_gu, v_ffn2_w_down, v_final_norm):
    params = dict(ffn1_norm=ffn1_norm, ffn1_w_gu=ffn1_w_gu, ffn1_w_down=ffn1_w_down, mix_norm=mix_norm, w_in=w_in, q_a_norm=q_a_norm,
                  w_q_b=w_q_b, kv_a_norm=kv_a_norm, w_kv_b=w_kv_b, pool_w=pool_w, pool_scale=pool_scale, fox_b_f=fox_b_f, w_out=w_out,
                  ffn2_norm=ffn2_norm, ffn2_w_gu=ffn2_w_gu, ffn2_w_down=ffn2_w_down, final_norm=final_norm)
    mom = dict(ffn1_norm=m_ffn1_norm, ffn1_w_gu=m_ffn1_w_gu, ffn1_w_down=m_ffn1_w_down, mix_norm=m_mix_norm, w_in=m_w_in,
               q_a_norm=m_q_a_norm, w_q_b=m_w_q_b, kv_a_norm=m_kv_a_norm, w_kv_b=m_w_kv_b, pool_w=m_pool_w, pool_scale=m_pool_scale,
               fox_b_f=m_fox_b_f, w_out=m_w_out, ffn2_norm=m_ffn2_norm, ffn2_w_gu=m_ffn2_w_gu, ffn2_w_down=m_ffn2_w_down,
               final_norm=m_final_norm)
    var = dict(ffn1_norm=v_ffn1_norm, ffn1_w_gu=v_ffn1_w_gu, ffn1_w_down=v_ffn1_w_down, mix_norm=v_mix_norm, w_in=v_w_in,
               q_a_norm=v_q_a_norm, w_q_b=v_w_q_b, kv_a_norm=v_kv_a_norm, w_kv_b=v_w_kv_b, pool_w=v_pool_w, pool_scale=v_pool_scale,
               fox_b_f=v_fox_b_f, w_out=v_w_out, ffn2_norm=v_ffn2_norm, ffn2_w_gu=v_ffn2_w_gu, ffn2_w_down=v_ffn2_w_down,
               final_norm=v_final_norm)

    first = [("ffn1_w_gu", 0), ("ffn1_w_down", 0)]
    mix0 = [(nm, 0) for nm in ("w_in", "w_q_b", "w_kv_b", "w_out")]
    rest = [(nm, l) for nm in BIG for l in range(DEPTH) if (nm, l) not in first + mix0]

    def shards(keys, zero=0.0):
        return [((_pad_w_in(params[nm]) if nm == "w_in" else params[nm])[l] + zero).astype(BF16) for nm, l in keys]

    wts = {nm: [None] * DEPTH for nm in WEIGHT_VIEWS}
    wts["pool_w"] = params["pool_w"]
    got = _gather_blocking(shards(first))
    _prepare_weights(dict(zip(first, got)), wts)
    sems_m, src_m, land_m, token_m = _gather_start(shards(mix0), got[0], "mix0")
    sm = dict(params)
    sm["ffn1_norm"] = params["ffn1_norm"] + token_m[0, 0]
    rest_shards = shards(rest, token_m[0, 0])
    flying = {}

    def late_weights(stage, act, sm_now):
        if stage == "ffn1":
            lands = _gather_forward(_gather_wait(sems_m, src_m, land_m, act, "mix0"), "mix0")
            _prepare_weights(dict(zip(mix0, lands)), wts)
            flying["rest"] = _gather_start(rest_shards, lands[0], "rest")
            sm_next = dict(sm_now)
            sm_next["mix_norm"] = sm_now["mix_norm"] + flying["rest"][3][0, 0]
            return sm_next
        sems_r, src_r, land_r, _ = flying["rest"]
        lands = _gather_forward(_gather_wait(sems_r, src_r, land_r, act, "rest"), "rest")
        _prepare_weights(dict(zip(rest, lands)), wts)
        return sm_now

    pos = _position()
    flight = {}

    groups = {"l1": (1, BIG), "l0a": (0, [nm for nm in BIG if not nm.startswith("ffn1")]),
              "l0b": (0, [nm for nm in BIG if nm.startswith("ffn1")])}
    pending = {}

    def to_chips(key, full, sib):
        psum = _sum2_bf16(pos, full, sib, name=f"chip_sum_{key}")
        s2 = _split_start(_stage2_copies, psum, [((3,) + p.shape[1:], p.dtype) for p in psum], 3 * len(psum),
                          f"reduce_stage2_start_{key}")
        flight[key] = (full, sib, s2)
        return s2[3][0, 0]

    def grads_ready(l, stage, g, sm_now):
        behind, tok = None, None
        if (l, stage) == (1, "ffn1"):
            full = [_chip_major(nm, g[nm]) for nm in BIG]
            pending["l1"] = _split_start(_stage1_copies, full, [((N_CHIPS, f.shape[1] // 2, f.shape[2]), F32) for f in full],
                                         len(full), "reduce_stage1_start_l1")
            behind, tok = "ffn2_norm", pending["l1"][3][0, 0]
        elif (l, stage) == (0, "ffn2"):
            sems1, full_thru, sib_land, _ = pending["l1"]
            full, sib = _split_wait(_stage1_copies, sems1, full_thru, sib_land, g["ffn2_w_down"], "reduce_stage1_wait_l1")
            tok = to_chips("l1", full, sib)
        elif (l, stage) == (0, "mix"):
            full = [_chip_major(nm, g[nm]) for nm in groups["l0a"][1]]
            pending["l0a"] = _split_start(_stage1_copies, full, [((N_CHIPS, f.shape[1] // 2, f.shape[2]), F32) for f in full],
                                          len(full), "reduce_stage1_start_l0a")
            behind, tok = "ffn1_norm", pending["l0a"][3][0, 0]
        elif (l, stage) == (0, "ffn1_tokens"):
            sems1, full_thru, sib_land, _ = pending["l0a"]
            full, sib = _split_wait(_stage1_copies, sems1, full_thru, sib_land, g["dx"], "reduce_stage1_wait_l0a")
            tok = to_chips("l0a", full, sib)
        elif (l, stage) == (0, "ffn1"):
            full = [_chip_major(nm, g[nm]) for nm in groups["l0b"][1]]
            pending["l0b"] = _split_start(_stage1_copies, full, [((N_CHIPS, f.shape[1] // 2, f.shape[2]), F32) for f in full],
                                          len(full), "reduce_stage1_start_l0b")
        if behind is None:
            return sm_now, tok
        sm_next = dict(sm_now)
        sm_next[behind] = sm_now[behind] + tok
        return sm_next, tok

    loss, dx, grads, g_final = _local_step(x[0], loss_target[0], wts, sm, late_weights, grads_ready)

    def view2d(a):
        return a.reshape(a.size // a.shape[-1], a.shape[-1])

    after = pending["l0b"][3]
    done = {nm: None for nm in BIG}
    for key in ("l1", "l0a", "l0b"):
        l, names = groups[key]
        full, sib, (sems2, ps_thru, lands2, _) = flight[key]
        _, recv = _split_wait(_stage2_copies, sems2, ps_thru, lands2, after, f"reduce_stage2_wait_{key}")
        whole = _reduce_stage3(_sum5(pos, full, sib, recv, name=f"grad_sum_{key}"), key)
        for nm, g_l in zip(names, whole):
            if nm == "w_in":
                g_l = _unpad_w_in(g_l)
            tr = max(t for t in (512, 352, 256, 128) if g_l.shape[0] % t == 0)
            done[nm] = _adamw_layer(view2d(params[nm]), g_l, view2d(mom[nm]), view2d(var[nm]), l, done[nm],
                                    name=f"adamw_{nm}_{l}", tr=tr)
        after = done[names[-1]][0][-8:, 0:128]
        if key == "l1":
            small_g, loss = _unpack_small(_allreduce_small(_pack_small(grads, g_final, loss)), params)
            sems1, full_thru, sib_land, _ = pending["l0b"]
            full_b, sib_b = _split_wait(_stage1_copies, sems1, full_thru, sib_land, after + small_g["final_norm"][0],
                                        "reduce_stage1_wait_l0b")
            after = after + to_chips("l0b", full_b, sib_b)
    gw, delta, new_m, new_v = dict(small_g), {}, {}, {}
    for nm in BIG:
        delta[nm], new_m[nm], new_v[nm], gw[nm] = [a.reshape(params[nm].shape) for a in done[nm]]
    for nm in small_g:
        delta[nm], new_m[nm], new_v[nm] = _update(nm, params[nm], gw[nm], mom[nm], var[nm])
    return (loss, dx[None], *[gw[n] for n in WEIGHTS], *[delta[n] for n in WEIGHTS], *[new_m[n] for n in WEIGHTS],
            *[new_v[n] for n in WEIGHTS])
```

```python
import functools
import math

import jax
import jax.numpy as jnp
import numpy as np
from jax import lax
from jax.experimental import pallas as pl
from jax.experimental.pallas import tpu as pltpu

F32 = jnp.float32
BF16 = jnp.bfloat16
MESH = pl.DeviceIdType.MESH
HBM_SPEC = pl.BlockSpec(memory_space=pltpu.HBM)

D = 1024
DEPTH = 2
D_FF = 2816
FF_SHARD = 1408
N_CHIPS = 4
H = 6
NOPE, ROPE, VDIM = 64, 32, 64
HALF_ROPE = ROPE // 2
Q_RANK, KV_RANK = 256, 128
POOL_W = 256
FOX_D = 64
N_IN = 1830
NZ = 2048
ROPE_THETA = 10000.0
EPS = 1e-6
POOL_HALO = 16
Z_QA, Z_KVA, Z_KR, Z_POOL, Z_FOX, Z_F = 0, 256, 384, 512, 768, 1920

ADAM_LR, ADAM_B1, ADAM_B2, ADAM_EPS, ADAM_WD, ADAM_STEP = 0.001, 0.9, 0.999, 1e-08, 0.01, 10

VMEM_LIMIT_V7X = 56 * 1024 * 1024


def _cp(sem=None, vmem=VMEM_LIMIT_V7X):
    return pltpu.CompilerParams(dimension_semantics=sem, vmem_limit_bytes=vmem)


def _sigmoid(x):
    return 0.5 * jnp.tanh(0.5 * x) + 0.5


def _dot(a, b, dims):
    return lax.dot_general(a, b, (dims, ((), ())), preferred_element_type=F32)


NN = ((1,), (0,))
NT = ((1,), (1,))
TN = ((0,), (0,))


def _mm(a, b, mode, *, name, out_dtype=F32, add=None, alpha=None, tm=512, tn=512, tk=512, n_major_out=False):
    if mode == "nn":
        (m, k), (k2, n) = a.shape, b.shape
    elif mode == "nt":
        (m, k), (n, k2) = a.shape, b.shape
    else:
        (k, m), (k2, n) = a.shape, b.shape
    assert k == k2
    tm, tn, tk = min(tm, m), min(tn, n), min(tk, k)
    assert m % tm == 0 and n % tn == 0 and k % tk == 0, (name, m, n, k, tm, tn, tk)
    nk = k // tk
    dims = {"nn": NN, "nt": NT, "tn": TN}[mode]
    a_spec = pl.BlockSpec((tk, tm), lambda i, j, kk: (kk, i)) if mode == "tn" else pl.BlockSpec((tm, tk), lambda i, j, kk: (i, kk))
    b_spec = pl.BlockSpec((tn, tk), lambda i, j, kk: (j, kk)) if mode == "nt" else pl.BlockSpec((tk, tn), lambda i, j, kk: (kk, j))
    in_specs = [a_spec, b_spec]
    args = [a, b]
    if add is not None:
        in_specs.append(pl.BlockSpec((tm, tn), lambda i, j, kk: (i, j)))
        args.append(add)
    if n_major_out:
        out_shape = jax.ShapeDtypeStruct((n // tn, m, tn), out_dtype)
        out_spec = pl.BlockSpec((None, tm, tn), lambda i, j, kk: (j, i, 0))
    else:
        out_shape = jax.ShapeDtypeStruct((m, n), out_dtype)
        out_spec = pl.BlockSpec((tm, tn), lambda i, j, kk: (i, j))

    def body(*refs):
        a_ref, b_ref = refs[0], refs[1]
        add_ref = refs[2] if add is not None else None
        o_ref, acc = refs[-2], refs[-1]
        kk = pl.program_id(2)

        @pl.when(kk == 0)
        def _():
            acc[...] = jnp.zeros_like(acc)

        acc[...] += _dot(a_ref[...].astype(BF16), b_ref[...].astype(BF16), dims)

        @pl.when(kk == nk - 1)
        def _():
            r = acc[...]
            if alpha is not None:
                r = r * alpha
            if add_ref is not None:
                r = r + add_ref[...].astype(F32)
            o_ref[...] = r.astype(out_dtype)

    return pl.pallas_call(
        body, name=name, grid=(m // tm, n // tn, nk), in_specs=in_specs, out_specs=out_spec, out_shape=out_shape,
        scratch_shapes=[pltpu.VMEM((tm, tn), F32)],
        compiler_params=_cp(("parallel", "parallel", "arbitrary")),
    )(*args)


def _rmsnorm_bwd(x, col_block, gain, da, w, dres=None, *, name, tm=512):
    s = x.shape[0]
    k, n = w.shape
    tm = min(tm, s)

    def body(*refs):
        x_ref, g_ref, da_ref, w_ref = refs[:4]
        dres_ref = refs[4] if dres is not None else None
        dx_ref, dg_ref = refs[-2], refs[-1]
        xv = x_ref[...]
        r = lax.rsqrt(jnp.mean(xv * xv, axis=-1, keepdims=True) + EPS)
        dhv = _dot(da_ref[...], w_ref[...], NT)
        a = dhv * g_ref[...]
        dx = r * a - xv * (r * r * r) * jnp.mean(a * xv, axis=-1, keepdims=True)
        if dres_ref is not None:
            dx = dx + dres_ref[...]
        dx_ref[...] = dx

        @pl.when(pl.program_id(0) == 0)
        def _():
            dg_ref[...] = jnp.zeros_like(dg_ref)

        dg_ref[...] += jnp.sum(dhv * xv * r, axis=0, keepdims=True)

    in_specs = [pl.BlockSpec((tm, k), lambda i: (i, col_block)), pl.BlockSpec((1, k), lambda i: (0, 0)),
                pl.BlockSpec((tm, n), lambda i: (i, 0)), pl.BlockSpec((k, n), lambda i: (0, 0))]
    args = [x, gain.reshape(1, k), da, w]
    if dres is not None:
        in_specs.append(pl.BlockSpec((tm, k), lambda i: (i, 0)))
        args.append(dres)
    dx, dg = pl.pallas_call(
        body, name=name, grid=(s // tm,), in_specs=in_specs,
        out_specs=[pl.BlockSpec((tm, k), lambda i: (i, 0)), pl.BlockSpec((1, k), lambda i: (0, 0))],
        out_shape=[jax.ShapeDtypeStruct((s, k), F32), jax.ShapeDtypeStruct((1, k), F32)],
        compiler_params=_cp(("arbitrary",)),
    )(*args)
    return dx, dg.reshape(k)


def _ffn_fwd(x, gain, w_gu4, w_d2, *, name, tm=256):
    s = x.shape[0]
    tm = min(tm, s)

    def body(x_ref, g_ref, wgu_ref, wd_ref, xo_ref, dgu_ref, act_ref):
        xv = x_ref[...]
        r = lax.rsqrt(jnp.mean(xv * xv, axis=-1, keepdims=True) + EPS)
        hv = (xv * r * g_ref[...]).astype(BF16)
        y = jnp.zeros((tm, D), F32)
        for j in range(2):
            g = _dot(hv, wgu_ref[j], NN)
            u = _dot(hv, wgu_ref[2 + j], NN)
            sg = _sigmoid(g)
            silu = g * sg
            dgu_ref[:, j * FF_SHARD:(j + 1) * FF_SHARD] = (u * (sg * (1.0 + g * (1.0 - sg)))).astype(BF16)
            dgu_ref[:, D_FF + j * FF_SHARD:D_FF + (j + 1) * FF_SHARD] = silu.astype(BF16)
            act = (silu * u).astype(BF16)
            act_ref[:, j * FF_SHARD:(j + 1) * FF_SHARD] = act
            y = y + _dot(act, wd_ref[j], NN)
        xo_ref[...] = xv + 0.5 * y

    row = lambda i: (i, 0)
    return pl.pallas_call(
        body, name=name, grid=(s // tm,),
        in_specs=[pl.BlockSpec((tm, D), row), pl.BlockSpec((1, D), lambda i: (0, 0)),
                  pl.BlockSpec((N_CHIPS, D, FF_SHARD), lambda i: (0, 0, 0), pipeline_mode=pl.Buffered(1)),
                  pl.BlockSpec((2, FF_SHARD, D), lambda i: (0, 0, 0), pipeline_mode=pl.Buffered(1))],
        out_specs=[pl.BlockSpec((tm, D), row), pl.BlockSpec((tm, 2 * D_FF), row), pl.BlockSpec((tm, D_FF), row)],
        out_shape=[jax.ShapeDtypeStruct((s, D), F32), jax.ShapeDtypeStruct((s, 2 * D_FF), BF16),
                   jax.ShapeDtypeStruct((s, D_FF), BF16)],
        compiler_params=_cp(("parallel",)),
    )(x, gain.reshape(1, D), w_gu4, w_d2)


FFN_ROW_CHUNK = 32


def _ffn_bwd(x, dxo, dloc, gain, w_gu4, w_d2, *, name, tm=256):
    s = x.shape[0]
    tm = min(tm, s)

    def body(x_ref, dxo_ref, dloc_ref, g_ref, wgu_ref, wd_ref, dx_ref, dgu_ref, h_ref, dy_ref, dg_ref):
        xv = x_ref[...]
        r = lax.rsqrt(jnp.mean(xv * xv, axis=-1, keepdims=True) + EPS)
        xh = xv * r
        h_ref[...] = (xh * g_ref[...]).astype(BF16)
        dxov = dxo_ref[...]
        dy = (0.5 * dxov).astype(BF16)
        dy_ref[...] = dy
        gcols = [slice(j * FF_SHARD, (j + 1) * FF_SHARD) for j in range(2)]
        ucols = [slice(D_FF + j * FF_SHARD, D_FF + (j + 1) * FF_SHARD) for j in range(2)]
        dacts = [_dot(dy, wd_ref[j], NT) for j in range(2)]
        for r0 in range(0, tm, FFN_ROW_CHUNK):
            rows = slice(r0, r0 + FFN_ROW_CHUNK)
            for j in range(2):
                da = dacts[j][rows]
                dgu_ref[rows, gcols[j]] = (da * dloc_ref[rows, gcols[j]].astype(F32)).astype(BF16)
                dgu_ref[rows, ucols[j]] = (da * dloc_ref[rows, ucols[j]].astype(F32)).astype(BF16)
        dh = jnp.zeros((tm, D), F32)
        for j in range(2):
            dh = dh + _dot(dgu_ref[:, gcols[j]], wgu_ref[j], NT) + _dot(dgu_ref[:, ucols[j]], wgu_ref[2 + j], NT)
        a = dh * g_ref[...]
        dx_ref[...] = dxov + r * a - xh * (r * jnp.mean(a * xh, axis=-1, keepdims=True))

        @pl.when(pl.program_id(0) == 0)
        def _():
            dg_ref[...] = jnp.zeros_like(dg_ref)

        dg_ref[...] += jnp.sum(dh * xh, axis=0, keepdims=True)

    row = lambda i: (i, 0)
    outs = pl.pallas_call(
        body, name=name, grid=(s // tm,),
        in_specs=[pl.BlockSpec((tm, D), row), pl.BlockSpec((tm, D), row), pl.BlockSpec((tm, 2 * D_FF), row),
                  pl.BlockSpec((1, D), lambda i: (0, 0)),
                  pl.BlockSpec((N_CHIPS, D, FF_SHARD), lambda i: (0, 0, 0), pipeline_mode=pl.Buffered(1)),
                  pl.BlockSpec((2, FF_SHARD, D), lambda i: (0, 0, 0), pipeline_mode=pl.Buffered(1))],
        out_specs=[pl.BlockSpec((tm, D), row), pl.BlockSpec((tm, 2 * D_FF), row),
                   pl.BlockSpec((tm, D), row), pl.BlockSpec((tm, D), row), pl.BlockSpec((1, D), lambda i: (0, 0))],
        out_shape=[jax.ShapeDtypeStruct((s, D), F32), jax.ShapeDtypeStruct((s, 2 * D_FF), BF16),
                   jax.ShapeDtypeStruct((s, D), BF16), jax.ShapeDtypeStruct((s, D), BF16), jax.ShapeDtypeStruct((1, D), F32)],
        compiler_params=_cp(("arbitrary",)),
    )(x, dxo, dloc, gain.reshape(1, D), w_gu4, w_d2)
    dx, dgu, h, dy, dg = outs
    return dx, dgu, h, dy, dg.reshape(D)


DA = 128
SCALE_MLA = 1.0 / math.sqrt(NOPE + ROPE)
SCALE_FOX = 1.0 / math.sqrt(FOX_D)


def _causal_blocks(nb, key_major):
    if key_major:
        pairs = [(i, j) for j in range(nb) for i in range(j, nb)]
    else:
        pairs = [(i, j) for i in range(nb) for j in range(i + 1)]
    return (jnp.asarray(np.array([p[0] for p in pairs], np.int32)), jnp.asarray(np.array([p[1] for p in pairs], np.int32)))


HEADS_PER_STEP = 3
ROW_CHUNK = 64

def _col_to_row(col):
    return jnp.broadcast_to(col, (col.shape[0], DA)).T[0:1, :]


def _attn_fwd(qa, ka, va, dv, *, name, t=512):
    h, s, _ = qa.shape
    t = min(t, s)
    nb = s // t
    g = H
    qi, kj = _causal_blocks(nb, key_major=False)

    rc = min(ROW_CHUNK, t)

    def body(qi_ref, kj_ref, q_ref, k_ref, v_ref, o_ref, lse_ref, m_sc, acc_sc, p_sc, a_sc):
        n = pl.program_id(1)
        i, j = qi_ref[n], kj_ref[n]

        @pl.when(j == 0)
        def _():
            m_sc[...] = jnp.full_like(m_sc, -jnp.inf)
            acc_sc[...] = jnp.zeros_like(acc_sc)

        def step(masked):
            scs = [_dot(q_ref[hh], k_ref[hh], NT) for hh in range(g)]
            for r0 in range(0, t, rc):
                rows = slice(r0, r0 + rc)
                for hh in range(g):
                    sr = scs[hh][rows]
                    if masked:
                        row = lax.broadcasted_iota(jnp.int32, (rc, t), 0) + r0
                        col = lax.broadcasted_iota(jnp.int32, (rc, t), 1)
                        sr = jnp.where(col <= row, sr, -jnp.inf)
                    tiles = [sr[:, c0:c0 + DA] for c0 in range(0, t, DA)]
                    top = tiles[0]
                    for tile in tiles[1:]:
                        top = jnp.maximum(top, tile)
                    m_old = m_sc[hh, rows]
                    m_new = jnp.maximum(m_old, jnp.max(top, axis=-1, keepdims=True))
                    for c0, tile in zip(range(0, t, DA), tiles):
                        p_sc[hh, rows, c0:c0 + DA] = jnp.exp(tile - m_new).astype(BF16)
                    a_sc[hh, rows] = jnp.exp(m_old - m_new)
                    m_sc[hh, rows] = m_new
            for hh in range(g):
                acc_sc[hh] = a_sc[hh] * acc_sc[hh] + _dot(p_sc[hh], v_ref[hh], NN)

        @pl.when(j < i)
        def _():
            step(False)

        @pl.when(j == i)
        def _():
            step(True)
            for hh in range(g):
                acc = acc_sc[hh]
                l = acc[:, dv:dv + 1]
                o_ref[hh] = acc[:, :dv] / l
                lse_ref[hh] = _col_to_row(m_sc[hh][:, 0:1] + jnp.log(l))

    qmap = lambda hg, n, qi_r, kj_r: (hg, qi_r[n], 0)
    kmap = lambda hg, n, qi_r, kj_r: (hg, kj_r[n], 0)
    return pl.pallas_call(
        body, name=name,
        grid_spec=pltpu.PrefetchScalarGridSpec(
            num_scalar_prefetch=2, grid=(h // g, qi.shape[0]),
            in_specs=[pl.BlockSpec((g, t, DA), qmap), pl.BlockSpec((g, t, DA), kmap), pl.BlockSpec((g, t, DA), kmap)],
            out_specs=[pl.BlockSpec((g, t, dv), qmap), pl.BlockSpec((g, 1, t), lambda hg, n, qi_r, kj_r: (hg, 0, qi_r[n]))],
            scratch_shapes=[pltpu.VMEM((g, t, DA), F32), pltpu.VMEM((g, t, DA), F32), pltpu.VMEM((g, t, t), BF16),
                            pltpu.VMEM((g, t, DA), F32)]),
        out_shape=[jax.ShapeDtypeStruct((h, s, dv), F32), jax.ShapeDtypeStruct((h, 1, s), F32)],
        compiler_params=_cp(("parallel", "arbitrary")),
    )(qi, kj, qa, ka, va)


def _attn_bwd(qa, ka, va, doa, lse_row, delta_row, decay, *, name, t=512):
    h, s, _ = qa.shape
    t = min(t, s)
    nb = s // t
    g = HEADS_PER_STEP
    rc = min(ROW_CHUNK, t)
    qi, kj = _causal_blocks(nb, key_major=True)
    nsteps = qi.shape[0]

    def body(*refs):
        qi_ref, kj_ref, q_ref, k_ref, v_ref, do_ref, lse_ref, dl_ref = refs[:8]
        p_sc, ds_sc = refs[-2:]
        if decay:
            dq_ref, dk_ref, dv_ref, dcq_ref, dck_ref, dq_acc, dk_acc, dv_acc, dcq_acc, dck_acc = refs[8:-2]
        else:
            dq_ref, dk_ref, dv_ref, dq_acc, dk_acc, dv_acc = refs[8:-2]
        n = pl.program_id(1)
        i, j = qi_ref[n], kj_ref[n]

        @pl.when(n == 0)
        def _():
            dq_acc[...] = jnp.zeros_like(dq_acc)
            if decay:
                dcq_acc[...] = jnp.zeros_like(dcq_acc)

        @pl.when(i == j)
        def _():
            dk_acc[...] = jnp.zeros_like(dk_acc)
            dv_acc[...] = jnp.zeros_like(dv_acc)
            if decay:
                dck_acc[...] = jnp.zeros_like(dck_acc)

        def step(masked):
            sts = [_dot(k_ref[hh], q_ref[hh], NT) for hh in range(g)]
            dpts = [_dot(v_ref[hh], do_ref[hh], NT) for hh in range(g)]
            dcq = [jnp.zeros((1, t), F32) for _ in range(g)]
            for r0 in range(0, t, rc):
                rows = slice(r0, r0 + rc)
                for hh in range(g):
                    st = sts[hh][rows]
                    if masked:
                        row = lax.broadcasted_iota(jnp.int32, (rc, t), 0) + r0
                        col = lax.broadcasted_iota(jnp.int32, (rc, t), 1)
                        st = jnp.where(row <= col, st, -jnp.inf)
                    pt = jnp.exp(st - lse_ref[hh])
                    dst = pt * (dpts[hh][rows] - dl_ref[hh])
                    p_sc[hh, rows] = pt.astype(BF16)
                    ds_sc[hh, rows] = dst.astype(BF16)
                    if decay:
                        dcq[hh] = dcq[hh] + jnp.sum(dst, axis=0, keepdims=True)
                        dck_acc[hh, rows] -= jnp.sum(dst, axis=1, keepdims=True)
            for hh in range(g):
                dv_acc[hh] += _dot(p_sc[hh], do_ref[hh], NN)
                dk_acc[hh] += _dot(ds_sc[hh], q_ref[hh], NN)
                dq_acc[hh, i] += _dot(ds_sc[hh], k_ref[hh], TN)
                if decay:
                    dcq_acc[hh, i] += dcq[hh]

        @pl.when(i > j)
        def _():
            step(False)

        @pl.when(i == j)
        def _():
            step(True)

        @pl.when(i == nb - 1)
        def _():
            dk_ref[...] = dk_acc[...]
            dv_ref[...] = dv_acc[...]
            if decay:
                for hh in range(g):
                    dck_ref[hh] = _col_to_row(dck_acc[hh])

        @pl.when(n == nsteps - 1)
        def _():
            dq_ref[...] = dq_acc[...]
            if decay:
                dcq_ref[...] = dcq_acc[...]

    kmap = lambda hg, n, qi_r, kj_r: (hg, kj_r[n], 0)
    qmap = lambda hg, n, qi_r, kj_r: (hg, qi_r[n], 0)
    qrow = lambda hg, n, qi_r, kj_r: (hg, 0, qi_r[n])
    krow = lambda hg, n, qi_r, kj_r: (hg, 0, kj_r[n])
    whole = lambda hg, n, qi_r, kj_r: (hg, 0, 0, 0)
    in_specs = [pl.BlockSpec((g, t, DA), qmap), pl.BlockSpec((g, t, DA), kmap), pl.BlockSpec((g, t, DA), kmap),
                pl.BlockSpec((g, t, DA), qmap), pl.BlockSpec((g, 1, t), qrow), pl.BlockSpec((g, 1, t), qrow)]
    out_specs = [pl.BlockSpec((g, nb, t, DA), whole), pl.BlockSpec((g, t, DA), kmap), pl.BlockSpec((g, t, DA), kmap)]
    out_shape = [jax.ShapeDtypeStruct((h, nb, t, DA), F32), jax.ShapeDtypeStruct((h, s, DA), F32), jax.ShapeDtypeStruct((h, s, DA), F32)]
    scratch = [pltpu.VMEM((g, nb, t, DA), F32), pltpu.VMEM((g, t, DA), F32), pltpu.VMEM((g, t, DA), F32)]
    if decay:
        out_specs += [pl.BlockSpec((g, nb, 1, t), whole), pl.BlockSpec((g, 1, t), krow)]
        out_shape += [jax.ShapeDtypeStruct((h, nb, 1, t), F32), jax.ShapeDtypeStruct((h, 1, s), F32)]
        scratch += [pltpu.VMEM((g, nb, 1, t), F32), pltpu.VMEM((g, t, 1), F32)]
    scratch += [pltpu.VMEM((g, t, t), BF16), pltpu.VMEM((g, t, t), BF16)]
    outs = pl.pallas_call(
        body, name=name,
        grid_spec=pltpu.PrefetchScalarGridSpec(num_scalar_prefetch=2, grid=(h // g, nsteps), in_specs=in_specs, out_specs=out_specs,
                                               scratch_shapes=scratch),
        out_shape=out_shape, compiler_params=_cp(("parallel", "arbitrary")),
    )(qi, kj, qa, ka, va, doa, lse_row, delta_row)
    outs = list(outs)
    outs[0] = outs[0].reshape(h, s, DA)
    if decay:
        outs[3] = outs[3].reshape(h, 1, s)
    return outs


def _sel(rows, cols, pairs, value=1.0):
    m = np.zeros((rows, cols), np.float32)
    for r, c in pairs:
        m[r, c] = value
    return jnp.asarray(m, BF16)


def _lane_row(lanes):
    m = np.zeros((1, DA), np.float32)
    m[0, list(lanes)] = 1.0
    return jnp.asarray(m)


def _rms(xv, gain):
    r = lax.rsqrt(jnp.mean(xv * xv, axis=-1, keepdims=True) + EPS)
    return xv * r * gain


def _mix_in(x, gains, w_in, wq_a, wq_b, wk, wv, tabs, *, name, tm=512):
    s = x.shape[0]
    tm = min(tm, s)
    one = _lane_row([VDIM])
    g_mix, g_q, g_kv = gains

    def body(x_ref, gm_ref, gq_ref, gkv_ref, win_ref, wa_ref, wb_ref, wk_ref, wv_ref, cq_ref, sq_ref, ck_ref, sk_ref,
             one_ref, z_ref, h_ref, qn_ref, kvn_ref, qa_ref, ka_ref, va_ref):
        hv = _rms(x_ref[...], gm_ref[...]).astype(BF16)
        h_ref[...] = hv
        z = _dot(hv, win_ref[...], NN)
        z_ref[...] = z
        qn = _rms(z[:, Z_QA:Z_QA + Q_RANK], gq_ref[...]).astype(BF16)
        kvn = _rms(z[:, Z_KVA:Z_KVA + KV_RANK], gkv_ref[...]).astype(BF16)
        qn_ref[...] = qn
        kvn_ref[...] = kvn
        c, sn = cq_ref[...], sq_ref[...]
        kpe = z[:, Z_KR:Z_KR + DA] * ck_ref[...] + z[:, Z_F:Z_F + DA] * sk_ref[...]
        for hh in range(H):
            cols = slice(hh * DA, (hh + 1) * DA)
            qa_ref[hh] = (_dot(qn, wa_ref[:, cols], NN) * c + _dot(qn, wb_ref[:, cols], NN) * sn).astype(BF16)
            ka_ref[hh] = (_dot(kvn, wk_ref[:, cols], NN) + kpe).astype(BF16)
            va_ref[hh] = (_dot(kvn, wv_ref[:, cols], NN) + one_ref[...]).astype(BF16)

    row = lambda i: (i, 0)
    fixed = lambda i: (0, 0)
    full = lambda a: pl.BlockSpec(a.shape, fixed)
    tab = pl.BlockSpec((tm, DA), row)
    heads = pl.BlockSpec((H, tm, DA), lambda i: (0, i, 0))
    return pl.pallas_call(
        body, name=name, grid=(s // tm,),
        in_specs=[pl.BlockSpec((tm, D), row), pl.BlockSpec((1, D), fixed), pl.BlockSpec((1, Q_RANK), fixed),
                  pl.BlockSpec((1, KV_RANK), fixed), full(w_in), full(wq_a), full(wq_b), full(wk), full(wv),
                  tab, tab, tab, tab, pl.BlockSpec((1, DA), fixed)],
        out_specs=[pl.BlockSpec((tm, NZ), row), pl.BlockSpec((tm, D), row), pl.BlockSpec((tm, Q_RANK), row),
                   pl.BlockSpec((tm, KV_RANK), row), heads, heads, heads],
        out_shape=[jax.ShapeDtypeStruct((s, NZ), F32), jax.ShapeDtypeStruct((s, D), BF16),
                   jax.ShapeDtypeStruct((s, Q_RANK), BF16), jax.ShapeDtypeStruct((s, KV_RANK), BF16)]
        + [jax.ShapeDtypeStruct((H, s, DA), BF16)] * 3,
        compiler_params=_cp(("parallel",)),
    )(x, g_mix.reshape(1, D), g_q.reshape(1, Q_RANK), g_kv.reshape(1, KV_RANK), w_in, wq_a, wq_b, wk, wv,
      tabs["cq"], tabs["sq"], tabs["ck"], tabs["sk"], one)


DEC_C = (FOX_D, FOX_D + 1, FOX_D + 2)
DEC_1 = (FOX_D + 3, FOX_D + 4, FOX_D + 5)


def _fox_prep(z, c3t, *, name, tm=512):
    s = z.shape[0]
    tm = min(tm, s)
    w = H * FOX_D
    left = [(r, r) for r in range(FOX_D)]
    right = [(FOX_D + r, r) for r in range(FOX_D)]
    pq = jnp.stack([_sel(DA, DA, left, SCALE_FOX), _sel(DA, DA, right, SCALE_FOX)])
    pk = jnp.stack([_sel(DA, DA, left), _sel(DA, DA, right)])
    pcq = jnp.stack([_sel(32, DA, [(hh + 8 * k, DEC_C[k]) for k in range(3)]) for hh in range(H)])
    pck = jnp.stack([_sel(32, DA, [(hh + 8 * k, DEC_1[k]) for k in range(3)], -1.0) for hh in range(H)])
    rows3 = jnp.concatenate([_lane_row(DEC_1), _lane_row(DEC_C), _lane_row([FOX_D])], axis=0)

    def body(zq_ref, zk_ref, zv_ref, c_ref, pq_ref, pk_ref, pcq_ref, pck_ref, r_ref, qa_ref, ka_ref, va_ref):
        c3 = c_ref[...]
        for pair in range(H // 2):
            lanes = slice(pair * DA, (pair + 1) * DA)
            zq, zk, zv = zq_ref[:, lanes].astype(BF16), zk_ref[:, lanes].astype(BF16), zv_ref[:, lanes].astype(BF16)
            for side in range(2):
                hh = 2 * pair + side
                qa_ref[hh] = (_dot(zq, pq_ref[side], NN) + _dot(c3, pcq_ref[hh], TN) + r_ref[0:1, :]).astype(BF16)
                ka_ref[hh] = (_dot(zk, pk_ref[side], NN) + _dot(c3, pck_ref[hh], TN) + r_ref[1:2, :]).astype(BF16)
                va_ref[hh] = (_dot(zv, pk_ref[side], NN) + r_ref[2:3, :]).astype(BF16)

    fixed2 = lambda i: (0, 0)
    fixed3 = lambda i: (0, 0, 0)
    heads = pl.BlockSpec((H, tm, DA), lambda i: (0, i, 0))
    zblk = lambda c: pl.BlockSpec((tm, w), lambda i: (i, c))
    return pl.pallas_call(
        body, name=name, grid=(s // tm,),
        in_specs=[zblk(Z_FOX // w), zblk(Z_FOX // w + 1), zblk(Z_FOX // w + 2), pl.BlockSpec((32, tm), lambda i: (0, i)),
                  pl.BlockSpec((2, DA, DA), fixed3), pl.BlockSpec((2, DA, DA), fixed3),
                  pl.BlockSpec((H, 32, DA), fixed3), pl.BlockSpec((H, 32, DA), fixed3), pl.BlockSpec((3, DA), fixed2)],
        out_specs=[heads, heads, heads], out_shape=[jax.ShapeDtypeStruct((H, s, DA), BF16)] * 3,
        compiler_params=_cp(("parallel",)),
    )(z, z, z, c3t, pq, pk, pcq, pck, rows3)


def _mix_out(oa, yb, oc, w_out, x1, *, name, tm=512):
    s = yb.shape[0]
    tm = min(tm, s)
    e2 = jnp.stack([_sel(VDIM, DA, [(r, r) for r in range(VDIM)]), _sel(VDIM, DA, [(r, VDIM + r) for r in range(VDIM)])])

    def body(oa_ref, yb_ref, oc_ref, e_ref, w_ref, x_ref, x2_ref, cat_ref):
        def pairs(o_ref):
            return [(_dot(o_ref[2 * p].astype(BF16), e_ref[0], NN) + _dot(o_ref[2 * p + 1].astype(BF16), e_ref[1], NN)).astype(BF16)
                    for p in range(H // 2)]

        cat = jnp.concatenate(pairs(oa_ref) + [yb_ref[...].astype(BF16)] + pairs(oc_ref), axis=1)
        cat_ref[...] = cat
        x2_ref[...] = x_ref[...] + _dot(cat, w_ref[...], NN)

    row = lambda i: (i, 0)
    heads = pl.BlockSpec((H, tm, VDIM), lambda i: (0, i, 0))
    return pl.pallas_call(
        body, name=name, grid=(s // tm,),
        in_specs=[heads, pl.BlockSpec((tm, POOL_W), row), heads, pl.BlockSpec((2, VDIM, DA), lambda i: (0, 0, 0)),
                  pl.BlockSpec((D, D), lambda i: (0, 0)), pl.BlockSpec((tm, D), row)],
        out_specs=[pl.BlockSpec((tm, D), row), pl.BlockSpec((tm, D), row)],
        out_shape=[jax.ShapeDtypeStruct((s, D), F32), jax.ShapeDtypeStruct((s, D), BF16)],
        compiler_params=_cp(("parallel",)),
    )(oa, yb, oc, e2, w_out, x1)


def _mix_out_bwd(dx2b, w_out, oa, oc, *, name, tm=512):
    s = dx2b.shape[0]
    tm = min(tm, s)
    f2 = jnp.stack([_sel(DA, DA, [(r, r) for r in range(VDIM)]), _sel(DA, DA, [(VDIM + r, r) for r in range(VDIM)])])
    nv = H * VDIM

    def body(dx_ref, w_ref, oa_ref, oc_ref, f_ref, doa_ref, doc_ref, dyb_ref, dla_ref, dlc_ref):
        dcat = _dot(dx_ref[...], w_ref[...], NT)
        dyb_ref[...] = dcat[:, nv:nv + POOL_W]
        for base, o_ref, do_ref, dl_ref in ((0, oa_ref, doa_ref, dla_ref), (nv + POOL_W, oc_ref, doc_ref, dlc_ref)):
            for p in range(H // 2):
                blk = dcat[:, base + p * DA:base + (p + 1) * DA].astype(BF16)
                for side in range(2):
                    hh = 2 * p + side
                    do = _dot(blk, f_ref[side], NN)
                    do_ref[hh] = do.astype(BF16)
                    dl_ref[hh] = _col_to_row(jnp.sum(do[:, :VDIM] * o_ref[hh], axis=-1, keepdims=True))

    row = lambda i: (i, 0)
    heads = lambda w: pl.BlockSpec((H, tm, w), lambda i: (0, i, 0))
    return pl.pallas_call(
        body, name=name, grid=(s // tm,),
        in_specs=[pl.BlockSpec((tm, D), row), pl.BlockSpec((D, D), lambda i: (0, 0)), heads(VDIM), heads(VDIM),
                  pl.BlockSpec((2, DA, DA), lambda i: (0, 0, 0))],
        out_specs=[heads(DA), heads(DA), pl.BlockSpec((tm, POOL_W), row),
                   pl.BlockSpec((H, 1, tm), lambda i: (0, 0, i)), pl.BlockSpec((H, 1, tm), lambda i: (0, 0, i))],
        out_shape=[jax.ShapeDtypeStruct((H, s, DA), BF16), jax.ShapeDtypeStruct((H, s, DA), BF16),
                   jax.ShapeDtypeStruct((s, POOL_W), F32), jax.ShapeDtypeStruct((H, 1, s), F32), jax.ShapeDtypeStruct((H, 1, s), F32)],
        compiler_params=_cp(("parallel",)),
    )(dx2b, w_out, oa, oc, f2)


def _mla_bwd_prep(dqa, dka, dva, dft, cq, sq, ck, sk, *, name, tm=512):
    s = dqa.shape[1]
    tm = min(tm, s)
    keep = _lane_row(range(NOPE))

    def body(dq_ref, dk_ref, dv_ref, dft_ref, cq_ref, sq_ref, ck_ref, sk_ref, keep_ref, dqab_ref, dkv_ref, dz3_ref, dz15_ref):
        cqv, sqv = cq_ref[...], sq_ref[...]
        dkpe = jnp.zeros((tm, DA), F32)
        for hh in range(H):
            lanes = slice(hh * DA, (hh + 1) * DA)
            dq = dq_ref[hh]
            dqab_ref[:, lanes] = (dq * cqv).astype(BF16)
            dqab_ref[:, H * DA + hh * DA:H * DA + (hh + 1) * DA] = (dq * sqv).astype(BF16)
            dk = dk_ref[hh]
            dkpe = dkpe + dk
            dkv_ref[:, lanes] = (dk * keep_ref[...]).astype(BF16)
            dkv_ref[:, H * DA + hh * DA:H * DA + (hh + 1) * DA] = (dv_ref[hh] * keep_ref[...]).astype(BF16)
        dz3_ref[...] = (dkpe * ck_ref[...]).astype(BF16)
        dz15_ref[...] = (dkpe * sk_ref[...] + dft_ref[...]).astype(BF16)

    row = lambda i: (i, 0)
    heads = pl.BlockSpec((H, tm, DA), lambda i: (0, i, 0))
    tab = pl.BlockSpec((tm, DA), row)
    return pl.pallas_call(
        body, name=name, grid=(s // tm,),
        in_specs=[heads, heads, heads, tab, tab, tab, tab, tab, pl.BlockSpec((1, DA), lambda i: (0, 0))],
        out_specs=[pl.BlockSpec((tm, 2 * H * DA), row), pl.BlockSpec((tm, 2 * H * DA), row), tab, tab],
        out_shape=[jax.ShapeDtypeStruct((s, 2 * H * DA), BF16), jax.ShapeDtypeStruct((s, 2 * H * DA), BF16),
                   jax.ShapeDtypeStruct((s, DA), BF16), jax.ShapeDtypeStruct((s, DA), BF16)],
        compiler_params=_cp(("parallel",)),
    )(dqa, dka, dva, dft, cq, sq, ck, sk, keep)


def _fox_bwd_prep(dfqa, dfka, dfva, *, name, tm=512):
    s = dfqa.shape[1]
    tm = min(tm, s)
    place = lambda v: jnp.stack([_sel(DA, DA, [(r, r) for r in range(FOX_D)], v), _sel(DA, DA, [(r, FOX_D + r) for r in range(FOX_D)], v)])
    gq, gk = place(SCALE_FOX), place(1.0)

    def body(dq_ref, dk_ref, dv_ref, gq_ref, gk_ref, dz_ref):
        for part, (d_ref, g_ref) in enumerate(((dq_ref, gq_ref), (dk_ref, gk_ref), (dv_ref, gk_ref))):
            for p in range(H // 2):
                blk = _dot(d_ref[2 * p].astype(BF16), g_ref[0], NN) + _dot(d_ref[2 * p + 1].astype(BF16), g_ref[1], NN)
                lo = part * H * FOX_D + p * DA
                dz_ref[:, lo:lo + DA] = blk.astype(BF16)

    heads = pl.BlockSpec((H, tm, DA), lambda i: (0, i, 0))
    sel = pl.BlockSpec((2, DA, DA), lambda i: (0, 0, 0))
    return pl.pallas_call(
        body, name=name, grid=(s // tm,), in_specs=[heads, heads, heads, sel, sel],
        out_specs=pl.BlockSpec((tm, 3 * H * FOX_D), lambda i: (i, 0)),
        out_shape=jax.ShapeDtypeStruct((s, 3 * H * FOX_D), BF16), compiler_params=_cp(("parallel",)),
    )(dfqa, dfka, dfva, gq, gk)


def _lane_scan(x, s, reverse):
    lane = lax.broadcasted_iota(jnp.int32, x.shape, 1)
    sh = 1
    while sh < s:
        if reverse:
            x = x + jnp.where(lane < s - sh, pltpu.roll(x, s - sh, axis=1), 0.0)
        else:
            x = x + jnp.where(lane >= sh, pltpu.roll(x, sh, axis=1), 0.0)
        sh *= 2
    return x


def _gate_fwd(z, col_block, bias, *, name):
    s = z.shape[0]

    def body(z_ref, b_ref, f_ref, c_ref):
        ft = z_ref[...].T[0:8, :]
        f_ref[...] = ft
        xg = ft + b_ref[...]
        lf = jnp.minimum(xg, 0.0) - jnp.log(1.0 + jnp.exp(-jnp.abs(xg)))
        c = _lane_scan(lf, s, False)
        hi = c.astype(BF16).astype(F32)
        r = c - hi
        mid = r.astype(BF16).astype(F32)
        lo = r - mid
        c_ref[...] = jnp.concatenate([hi, mid, lo, jnp.zeros_like(hi)], axis=0).astype(BF16)

    return pl.pallas_call(
        body, name=name, grid=(1,),
        in_specs=[pl.BlockSpec((s, 128), lambda i: (0, col_block)), pl.BlockSpec((8, 1), lambda i: (0, 0))],
        out_specs=[pl.BlockSpec((8, s), lambda i: (0, 0)), pl.BlockSpec((32, s), lambda i: (0, 0))],
        out_shape=[jax.ShapeDtypeStruct((8, s), F32), jax.ShapeDtypeStruct((32, s), BF16)],
        compiler_params=_cp(("arbitrary",)))(z, bias)


def _gate_bwd(ft, bias, dc, *, name):
    s = ft.shape[1]

    def body(f_ref, b_ref, dc_ref, df_ref, db_ref):
        xg = f_ref[...] + b_ref[...]
        dlf = _lane_scan(dc_ref[...], s, True)
        df = dlf * _sigmoid(-xg)
        db_ref[...] = jnp.sum(df, axis=-1, keepdims=True)
        df_ref[...] = jnp.concatenate([df, jnp.zeros((DA - 8, s), F32)], axis=0).T

    return pl.pallas_call(body, name=name, out_shape=[jax.ShapeDtypeStruct((s, DA), F32), jax.ShapeDtypeStruct((8, 1), F32)],
                          compiler_params=_cp())(ft, bias, dc)


def _pool_lane_consts(tm, i):
    lane = lax.broadcasted_iota(jnp.int32, (tm, POOL_W), 1)
    tok = lax.broadcasted_iota(jnp.int32, (tm, POOL_W), 0) + i * tm
    win = jnp.where(lane < 64, 2, jnp.where(lane < 128, 4, jnp.where(lane < 192, 8, 16)))
    cnt = jnp.minimum(tok + 1, win).astype(F32)
    return lane, tok, cnt


def _pick_window(lane, s2, s4, s8, s16):
    return jnp.where(lane < 64, s2, jnp.where(lane < 128, s4, jnp.where(lane < 192, s8, s16)))


def _pool_fwd(z, col_block, bd, scale, *, name, tm=512):
    s = z.shape[0]
    tm = min(tm, s)
    hb = tm // POOL_HALO

    def body(u_ref, halo_ref, bd_ref, sc_ref, y_ref, p_ref, buf):
        i = pl.program_id(0)
        buf[0:POOL_HALO, :] = halo_ref[...] * (i > 0).astype(F32)
        buf[POOL_HALO:, :] = u_ref[...]

        def back(k):
            return buf[POOL_HALO - k:POOL_HALO - k + tm, :]

        u = u_ref[...]
        s2 = u + back(1)
        s4 = s2 + back(2) + back(3)
        s8 = s4 + back(4) + back(5) + back(6) + back(7)
        s16 = s8
        for k in range(8, 16):
            s16 = s16 + back(k)
        lane, _, cnt = _pool_lane_consts(tm, i)
        pooled = (_pick_window(lane, s2, s4, s8, s16) / cnt - u).astype(BF16)
        p_ref[...] = pooled
        y_ref[...] = _dot(pooled, bd_ref[...], NN) * sc_ref[...]

    return pl.pallas_call(
        body, name=name, grid=(s // tm,),
        in_specs=[pl.BlockSpec((tm, POOL_W), lambda i: (i, col_block)),
                  pl.BlockSpec((POOL_HALO, POOL_W), lambda i: (jnp.maximum(i * hb - 1, 0), col_block)),
                  pl.BlockSpec((POOL_W, POOL_W), lambda i: (0, 0)), pl.BlockSpec((1, POOL_W), lambda i: (0, 0))],
        out_specs=[pl.BlockSpec((tm, POOL_W), lambda i: (i, 0)), pl.BlockSpec((tm, POOL_W), lambda i: (i, 0))],
        out_shape=[jax.ShapeDtypeStruct((s, POOL_W), F32), jax.ShapeDtypeStruct((s, POOL_W), BF16)],
        scratch_shapes=[pltpu.VMEM((tm + POOL_HALO, POOL_W), F32)],
        compiler_params=_cp(("parallel",)),
    )(z, z, bd, scale.reshape(1, POOL_W))


def _pool_bwd_a(dy, pooled, bd, scale, *, name, tm=512):
    s = dy.shape[0]
    tm = min(tm, s)

    def body(dy_ref, p_ref, bd_ref, sc_ref, dq_ref, dbd_ref, dsc_ref):
        i = pl.program_id(0)
        dyv = dy_ref[...]
        pv = p_ref[...]
        y0 = _dot(pv, bd_ref[...], NN)
        dys = (dyv * sc_ref[...]).astype(BF16)
        dp = _dot(dys, bd_ref[...], NT)
        _, _, cnt = _pool_lane_consts(tm, i)
        dq_ref[:, 0:POOL_W] = dp / cnt
        dq_ref[:, POOL_W:] = dp

        @pl.when(i == 0)
        def _():
            dsc_ref[...] = jnp.zeros_like(dsc_ref)
            dbd_ref[...] = jnp.zeros_like(dbd_ref)

        dsc_ref[...] += jnp.sum(dyv * y0, axis=0, keepdims=True)
        dbd_ref[...] += _dot(pv, dys, TN)

    row = lambda i: (i, 0)
    fixed = lambda i: (0, 0)
    dq, dbd, dsc = pl.pallas_call(
        body, name=name, grid=(s // tm,),
        in_specs=[pl.BlockSpec((tm, POOL_W), row), pl.BlockSpec((tm, POOL_W), row),
                  pl.BlockSpec((POOL_W, POOL_W), fixed), pl.BlockSpec((1, POOL_W), fixed)],
        out_specs=[pl.BlockSpec((tm, 2 * POOL_W), row), pl.BlockSpec((POOL_W, POOL_W), fixed), pl.BlockSpec((1, POOL_W), fixed)],
        out_shape=[jax.ShapeDtypeStruct((s, 2 * POOL_W), F32), jax.ShapeDtypeStruct((POOL_W, POOL_W), F32),
                   jax.ShapeDtypeStruct((1, POOL_W), F32)],
        compiler_params=_cp(("arbitrary",)),
    )(dy, pooled, bd, scale.reshape(1, POOL_W))
    return dq, dbd, dsc.reshape(POOL_W)


def _pool_bwd_b(dq, *, name, tm=512):
    s = dq.shape[0]
    tm = min(tm, s)
    hb = tm // POOL_HALO
    nblk = s // tm

    def body(q_ref, dp_ref, halo_ref, du_ref, buf):
        i = pl.program_id(0)
        buf[0:tm, :] = q_ref[...]
        buf[tm:, :] = halo_ref[...] * (i < nblk - 1).astype(F32)

        def ahead(k):
            return buf[k:k + tm, :]

        q = q_ref[...]
        s2 = q + ahead(1)
        s4 = s2 + ahead(2) + ahead(3)
        s8 = s4 + ahead(4) + ahead(5) + ahead(6) + ahead(7)
        s16 = s8
        for k in range(8, 16):
            s16 = s16 + ahead(k)
        lane = lax.broadcasted_iota(jnp.int32, (tm, POOL_W), 1)
        du_ref[...] = _pick_window(lane, s2, s4, s8, s16) - dp_ref[...]

    return pl.pallas_call(
        body, name=name, grid=(nblk,),
        in_specs=[pl.BlockSpec((tm, POOL_W), lambda i: (i, 0)), pl.BlockSpec((tm, POOL_W), lambda i: (i, 1)),
                  pl.BlockSpec((POOL_HALO, POOL_W), lambda i: (jnp.minimum((i + 1) * hb, nblk * hb - 1), 0))],
        out_specs=pl.BlockSpec((tm, POOL_W), lambda i: (i, 0)),
        out_shape=jax.ShapeDtypeStruct((s, POOL_W), F32),
        scratch_shapes=[pltpu.VMEM((tm + POOL_HALO, POOL_W), F32)],
        compiler_params=_cp(("parallel",)),
    )(dq, dq, dq)


def _loss_head(x, gain, target, *, name, tm=512):
    s = x.shape[0]
    tm = min(tm, s)

    def body(x_ref, g_ref, t_ref, dx_ref, dg_ref, loss_ref):
        xv = x_ref[...]
        r = lax.rsqrt(jnp.mean(xv * xv, axis=-1, keepdims=True) + EPS)
        xh = xv * r
        err = xh * g_ref[...] - t_ref[...]
        dy = err * (1.0 / D)
        a = dy * g_ref[...]
        dx_ref[...] = r * a - xh * (r * jnp.mean(a * xh, axis=-1, keepdims=True))

        @pl.when(pl.program_id(0) == 0)
        def _():
            dg_ref[...] = jnp.zeros_like(dg_ref)
            loss_ref[...] = jnp.zeros_like(loss_ref)

        dg_ref[...] += jnp.sum(dy * xh, axis=0, keepdims=True)
        part = 0.5 * jnp.sum(jnp.mean(err * err, axis=-1, keepdims=True), axis=0, keepdims=True)
        loss_ref[...] += jnp.broadcast_to(part, loss_ref.shape)

    row = lambda i: (i, 0)
    dx, dg, loss = pl.pallas_call(
        body, name=name, grid=(s // tm,),
        in_specs=[pl.BlockSpec((tm, D), row), pl.BlockSpec((1, D), lambda i: (0, 0)), pl.BlockSpec((tm, D), row)],
        out_specs=[pl.BlockSpec((tm, D), row), pl.BlockSpec((1, D), lambda i: (0, 0)), pl.BlockSpec((1, 128), lambda i: (0, 0))],
        out_shape=[jax.ShapeDtypeStruct((s, D), F32), jax.ShapeDtypeStruct((1, D), F32), jax.ShapeDtypeStruct((1, 128), F32)],
        compiler_params=_cp(("arbitrary",)),
    )(x, gain.reshape(1, D), target)
    return dx, dg.reshape(D), loss[0, 0]


def _adamw(w, g, m, v, *, name, tr=512):
    rows, cols = w.shape
    tr = min(tr, rows)
    assert rows % tr == 0, (name, rows, tr)
    c_m = 1.0 - ADAM_B1
    c_v = 1.0 - ADAM_B2
    bc1 = 1.0 - ADAM_B1 ** ADAM_STEP
    bc2 = 1.0 - ADAM_B2 ** ADAM_STEP

    def body(w_ref, g_ref, m_ref, v_ref, d_ref, mo_ref, vo_ref):
        gv = g_ref[...]
        mn = ADAM_B1 * m_ref[...] + c_m * gv
        vn = ADAM_B2 * v_ref[...] + c_v * (gv * gv)
        mo_ref[...] = mn
        vo_ref[...] = vn
        d_ref[...] = -ADAM_LR * ((mn / bc1) / (jnp.sqrt(vn / bc2) + ADAM_EPS) + ADAM_WD * w_ref[...])

    spec = pl.BlockSpec((tr, cols), lambda i: (i, 0))
    return pl.pallas_call(body, name=name, grid=(rows // tr,), in_specs=[spec] * 4, out_specs=[spec] * 3,
                          out_shape=[jax.ShapeDtypeStruct((rows, cols), F32)] * 3,
                          compiler_params=_cp(("parallel",)))(w, g, m, v)


def _adamw_layer(w, g, m, v, layer, prev, *, name, tr):
    rows, cols = g.shape
    assert rows % tr == 0 and w.shape == (DEPTH * rows, cols), (name, w.shape, g.shape, tr)
    nblk = rows // tr
    c_m = 1.0 - ADAM_B1
    c_v = 1.0 - ADAM_B2
    bc1 = 1.0 - ADAM_B1 ** ADAM_STEP
    bc2 = 1.0 - ADAM_B2 ** ADAM_STEP
    n_prev = 0 if prev is None else 4

    def body(*refs):
        w_ref, g_ref, m_ref, v_ref = refs[:4]
        d_ref, mo_ref, vo_ref, go_ref = refs[4 + n_prev:]
        gv = g_ref[...]
        mn = ADAM_B1 * m_ref[...] + c_m * gv
        vn = ADAM_B2 * v_ref[...] + c_v * (gv * gv)
        mo_ref[...] = mn
        vo_ref[...] = vn
        go_ref[...] = gv
        d_ref[...] = -ADAM_LR * ((mn / bc1) / (jnp.sqrt(vn / bc2) + ADAM_EPS) + ADAM_WD * w_ref[...])

    stacked = pl.BlockSpec((tr, cols), lambda i: (layer * nblk + i, 0))
    args = [w, g, m, v] + ([] if prev is None else list(prev))
    return pl.pallas_call(
        body, name=name, grid=(nblk,),
        in_specs=[stacked, pl.BlockSpec((tr, cols), lambda i: (i, 0)), stacked, stacked] + [ANY_SPEC] * n_prev,
        out_specs=[stacked] * 4, out_shape=[jax.ShapeDtypeStruct(w.shape, F32)] * 4,
        input_output_aliases={4 + k: k for k in range(n_prev)},
        compiler_params=_cp(("parallel",)))(*args)


def _position():
    return jnp.stack([lax.axis_index("c"), 2 * lax.axis_index("x") + lax.axis_index("y")]).astype(jnp.int32)


SUM_ROW_TILES = 2


def _sum2_bf16(pos, fulls, sibs, *, name):
    n = len(fulls)
    nb = SUM_ROW_TILES

    def body(pos_ref, *refs):
        for t in range(n):
            refs[2 * n + t][...] = (refs[t][...] + refs[n + t][...]).astype(BF16)

    in_specs, sib_specs = [], []
    for sb in sibs:
        _, half, cols = sb.shape
        tr = half // nb
        assert half % nb == 0 and tr % 16 == 0, sb.shape
        in_specs.append(pl.BlockSpec((None, tr, cols), lambda j, i, p: (j, p[0] * nb + i, 0)))
        sib_specs.append(pl.BlockSpec((None, tr, cols), lambda j, i, p: (j, i, 0)))
    return pl.pallas_call(
        body, name=name,
        grid_spec=pltpu.PrefetchScalarGridSpec(num_scalar_prefetch=1, grid=(N_CHIPS, nb), in_specs=in_specs + sib_specs,
                                               out_specs=sib_specs),
        out_shape=[jax.ShapeDtypeStruct(sb.shape, BF16) for sb in sibs],
        compiler_params=_cp(("parallel", "parallel")))(pos, *fulls, *sibs)


def _sum5(pos, fulls, sibs, recvs, *, name):
    n = len(fulls)
    nb = SUM_ROW_TILES

    def body(pos_ref, *refs):
        for t in range(n):
            acc = refs[t][...] + refs[n + t][...]
            for kk in range(3):
                acc = acc + refs[2 * n + t][kk].astype(F32)
            refs[3 * n + t][...] = acc

    f_specs, s_specs, r_specs, o_specs = [], [], [], []
    for f in fulls:
        _, rows, cols = f.shape
        tr = rows // 2 // nb
        f_specs.append(pl.BlockSpec((None, tr, cols), lambda i, p: (p[1], p[0] * nb + i, 0)))
        s_specs.append(pl.BlockSpec((None, tr, cols), lambda i, p: (p[1], i, 0)))
        r_specs.append(pl.BlockSpec((3, tr, cols), lambda i, p: (0, i, 0)))
        o_specs.append(pl.BlockSpec((tr, cols), lambda i, p: (p[0] * nb + i, 0)))
    return pl.pallas_call(
        body, name=name,
        grid_spec=pltpu.PrefetchScalarGridSpec(num_scalar_prefetch=1, grid=(nb,), in_specs=f_specs + s_specs + r_specs,
                                               out_specs=o_specs),
        out_shape=[jax.ShapeDtypeStruct(f.shape[1:], F32) for f in fulls],
        compiler_params=_cp(("parallel",)))(pos, *fulls, *sibs, *recvs)


def _place():
    x, y, c = lax.axis_index("x"), lax.axis_index("y"), lax.axis_index("c")
    chips = [(1 - x, y), (x, 1 - y), (1 - x, 1 - y)]
    return x, y, c, 2 * x + y, chips


SEM_SPEC = pl.BlockSpec(memory_space=pltpu.SEMAPHORE)
ANY_SPEC = pl.BlockSpec(memory_space=pl.ANY)


def _gather_copies(ins, outs, send_i, recv_i, send_o, recv_o):
    x, y, c, me, chips = _place()
    n = len(ins)
    started, awaited = [], []
    for t in range(n):
        half = ins[t].shape[0] // 2
        mine = pl.ds(c * half, half)
        started.append(pltpu.make_async_remote_copy(
            src_ref=ins[t], dst_ref=outs[t].at[me], send_sem=send_o.at[t], recv_sem=recv_o.at[t],
            device_id=(x, y, 1 - c), device_id_type=MESH))
        awaited.append(started[-1])
        for kk, (px, py) in enumerate(chips):
            started.append(pltpu.make_async_remote_copy(
                src_ref=ins[t].at[mine], dst_ref=outs[t].at[me, mine], send_sem=send_i.at[t * 3 + kk],
                recv_sem=recv_i.at[t * 3 + kk], device_id=(px, py, c), device_id_type=MESH))
            awaited.append(pltpu.make_async_remote_copy(
                src_ref=ins[t].at[mine], dst_ref=outs[t].at[2 * px + py, mine], send_sem=send_i.at[t * 3 + kk],
                recv_sem=recv_i.at[t * 3 + kk], device_id=(px, py, c), device_id_type=MESH))
    return started, awaited


def _forward_copies(outs, send_d, recv_d):
    x, y, c, me, chips = _place()
    started, awaited = [], []
    for t in range(len(outs)):
        half = outs[t].shape[1] // 2
        for kk, (px, py) in enumerate(chips):
            for lst, hc in ((started, c), (awaited, 1 - c)):
                blk = outs[t].at[2 * px + py, pl.ds(hc * half, half)]
                lst.append(pltpu.make_async_remote_copy(src_ref=blk, dst_ref=blk, send_sem=send_d.at[t * 3 + kk],
                                                        recv_sem=recv_d.at[t * 3 + kk], device_id=(x, y, 1 - c), device_id_type=MESH))
    return started, awaited


def _gather_blocking(shards):
    n = len(shards)

    def body(*refs):
        ins, outs = refs[:n], refs[n:2 * n]
        send_i, recv_i, send_d, recv_d, send_o, recv_o = refs[2 * n:]
        started, awaited = _gather_copies(ins, outs, send_i, recv_i, send_o, recv_o)
        for cp in started:
            cp.start()
        for cp in awaited:
            cp.wait_recv()
        fwd, fwd_in = _forward_copies(outs, send_d, recv_d)
        for cp in fwd:
            cp.start()
        for cp in fwd_in:
            cp.wait_recv()
        for cp in started + fwd:
            cp.wait_send()

    return pl.pallas_call(
        body, name="gather_first", in_specs=[HBM_SPEC] * n, out_specs=[HBM_SPEC] * n,
        out_shape=[jax.ShapeDtypeStruct((N_CHIPS,) + s.shape, s.dtype) for s in shards],
        scratch_shapes=[pltpu.SemaphoreType.DMA((3 * n,)), pltpu.SemaphoreType.DMA((3 * n,)),
                        pltpu.SemaphoreType.DMA((3 * n,)), pltpu.SemaphoreType.DMA((3 * n,)),
                        pltpu.SemaphoreType.DMA((n,)), pltpu.SemaphoreType.DMA((n,))],
    )(*shards)


def _gather_start(shards, after, tag):
    n = len(shards)

    def body(*refs):
        ins = refs[:n]
        send_i, recv_i, send_o, recv_o = refs[2 * n + 1:2 * n + 5]
        outs = refs[3 * n + 5:4 * n + 5]
        token = refs[4 * n + 5]
        started, _ = _gather_copies(ins, outs, send_i, recv_i, send_o, recv_o)
        for cp in started:
            cp.start()
        token[...] = jnp.zeros_like(token)

    lands = [lax.empty((N_CHIPS,) + s.shape, s.dtype) for s in shards]
    sems = [pltpu.SemaphoreType.DMA((3 * n,)), pltpu.SemaphoreType.DMA((3 * n,)), pltpu.SemaphoreType.DMA((n,)), pltpu.SemaphoreType.DMA((n,))]
    res = pl.pallas_call(
        body, name=f"gather_{tag}_start",
        in_specs=[HBM_SPEC] * (2 * n) + [ANY_SPEC],
        out_specs=[SEM_SPEC] * 4 + [HBM_SPEC] * (2 * n) + [pl.BlockSpec(memory_space=pltpu.VMEM)],
        out_shape=sems + [jax.ShapeDtypeStruct(s.shape, s.dtype) for s in shards]
        + [jax.ShapeDtypeStruct(a.shape, a.dtype) for a in lands] + [jax.ShapeDtypeStruct((8, 128), F32)],
        input_output_aliases={t: 4 + t for t in range(2 * n)},
        compiler_params=pltpu.CompilerParams(has_side_effects=pltpu.SideEffectType.DATAFLOW_SIDE_EFFECTING),
    )(*[pltpu.with_memory_space_constraint(s, pltpu.HBM) for s in shards],
      *[pltpu.with_memory_space_constraint(a, pltpu.HBM) for a in lands], after)
    return res[:4], res[4:4 + n], res[4 + n:4 + 2 * n], res[-1]


def _gather_wait(sems, shards_thru, lands_thru, after, tag):
    n = len(shards_thru)

    def body(*refs):
        ins, outs_in = refs[:n], refs[n:2 * n]
        send_i, recv_i, send_o, recv_o = refs[2 * n:2 * n + 4]
        started, awaited = _gather_copies(ins, outs_in, send_i, recv_i, send_o, recv_o)
        for cp in started:
            cp.wait_send()
        for cp in awaited:
            cp.wait_recv()

    res = pl.pallas_call(
        body, name=f"gather_{tag}_wait",
        in_specs=[HBM_SPEC] * (2 * n) + [SEM_SPEC] * 4 + [ANY_SPEC],
        out_specs=[HBM_SPEC] * (2 * n),
        out_shape=[jax.ShapeDtypeStruct(a.shape, a.dtype) for a in list(shards_thru) + list(lands_thru)],
        input_output_aliases={t: t for t in range(2 * n)},
        compiler_params=pltpu.CompilerParams(has_side_effects=pltpu.SideEffectType.DATAFLOW_SIDE_EFFECTING),
    )(*shards_thru, *lands_thru, *sems, after)
    return res[n:]


def _gather_forward(lands, tag):
    n = len(lands)

    def body(*refs):
        outs = refs[n:2 * n]
        send_d, recv_d = refs[2 * n:]
        fwd, fwd_in = _forward_copies(outs, send_d, recv_d)
        for cp in fwd:
            cp.start()
        for cp in fwd_in:
            cp.wait_recv()
        for cp in fwd:
            cp.wait_send()

    return pl.pallas_call(
        body, name=f"gather_{tag}_forward", in_specs=[HBM_SPEC] * n, out_specs=[HBM_SPEC] * n,
        out_shape=[jax.ShapeDtypeStruct(a.shape, a.dtype) for a in lands],
        input_output_aliases={t: t for t in range(n)},
        scratch_shapes=[pltpu.SemaphoreType.DMA((3 * n,)), pltpu.SemaphoreType.DMA((3 * n,))],
    )(*lands)


def _stage1_copies(ins, sib, send, recv):
    x, y, c, me, chips = _place()
    cps = []
    for t in range(len(ins)):
        rows = ins[t].shape[1] // 2
        cps.append(pltpu.make_async_remote_copy(
            src_ref=ins[t].at[:, pl.ds((1 - c) * rows, rows), :], dst_ref=sib[t], send_sem=send.at[t],
            recv_sem=recv.at[t], device_id=(x, y, 1 - c), device_id_type=MESH))
    return cps


def _split_start(copies_fn, srcs, land_shapes, n_sems, tag):
    n = len(srcs)

    def body(*refs):
        send, recv = refs[2 * n:2 * n + 2]
        for cp in copies_fn(refs[:n], refs[3 * n + 2:4 * n + 2], send, recv):
            cp.start()
        refs[4 * n + 2][...] = jnp.zeros_like(refs[4 * n + 2])

    lands = [lax.empty(shp, dt) for shp, dt in land_shapes]
    res = pl.pallas_call(
        body, name=tag,
        in_specs=[HBM_SPEC] * (2 * n),
        out_specs=[SEM_SPEC] * 2 + [HBM_SPEC] * (2 * n) + [pl.BlockSpec(memory_space=pltpu.VMEM)],
        out_shape=[pltpu.SemaphoreType.DMA((n_sems,)), pltpu.SemaphoreType.DMA((n_sems,))]
        + [jax.ShapeDtypeStruct(p.shape, p.dtype) for p in srcs]
        + [jax.ShapeDtypeStruct(a.shape, a.dtype) for a in lands] + [jax.ShapeDtypeStruct((8, 128), F32)],
        input_output_aliases={t: 2 + t for t in range(2 * n)},
        compiler_params=pltpu.CompilerParams(has_side_effects=pltpu.SideEffectType.DATAFLOW_SIDE_EFFECTING),
    )(*[pltpu.with_memory_space_constraint(p, pltpu.HBM) for p in srcs],
      *[pltpu.with_memory_space_constraint(a, pltpu.HBM) for a in lands])
    return res[:2], res[2:2 + n], res[2 + n:2 + 2 * n], res[-1]


def _split_wait(copies_fn, sems, srcs_thru, lands_thru, after, tag):
    n = len(srcs_thru)

    def body(*refs):
        for cp in copies_fn(refs[:n], refs[n:2 * n], refs[2 * n], refs[2 * n + 1]):
            cp.wait()

    res = pl.pallas_call(
        body, name=tag,
        in_specs=[HBM_SPEC] * (2 * n) + [SEM_SPEC] * 2 + [ANY_SPEC],
        out_specs=[HBM_SPEC] * (2 * n),
        out_shape=[jax.ShapeDtypeStruct(a.shape, a.dtype) for a in list(srcs_thru) + list(lands_thru)],
        input_output_aliases={t: t for t in range(2 * n)},
        compiler_params=pltpu.CompilerParams(has_side_effects=pltpu.SideEffectType.DATAFLOW_SIDE_EFFECTING),
    )(*srcs_thru, *lands_thru, *sems, after)
    return res[:n], res[n:]


def _stage2_copies(ps, rcv, send, recv):
    x, y, c, me, chips = _place()
    return [pltpu.make_async_remote_copy(
        src_ref=ps[t].at[2 * px + py], dst_ref=rcv[t].at[kk], send_sem=send.at[t * 3 + kk],
        recv_sem=recv.at[t * 3 + kk], device_id=(px, py, c), device_id_type=MESH)
        for t in range(len(ps)) for kk, (px, py) in enumerate(chips)]


def _reduce_stage3(reduced, tag):
    n = len(reduced)

    def body(*refs):
        outs = refs[n:2 * n]
        send, recv = refs[2 * n:]
        x, y, c, me, chips = _place()
        cps = []
        for t in range(n):
            rows = outs[t].shape[0] // 2
            mine = outs[t].at[pl.ds(c * rows, rows), :]
            cp = pltpu.make_async_remote_copy(src_ref=mine, dst_ref=mine, send_sem=send.at[t], recv_sem=recv.at[t],
                                              device_id=(x, y, 1 - c), device_id_type=MESH)
            cp.start()
            cps.append(cp)
        for cp in cps:
            cp.wait()

    return pl.pallas_call(
        body, name="reduce_stage3_" + tag, in_specs=[HBM_SPEC] * n, out_specs=[HBM_SPEC] * n,
        out_shape=[jax.ShapeDtypeStruct(r.shape, r.dtype) for r in reduced],
        input_output_aliases={t: t for t in range(n)},
        scratch_shapes=[pltpu.SemaphoreType.DMA((n,)), pltpu.SemaphoreType.DMA((n,))],
    )(*reduced)


def _allreduce_small(v):
    rows, cols = v.shape

    def body(v_ref, o_ref, buf, send, recv, loc):
        x, y, c, me, chips = _place()
        mine = 4 * x + 2 * y + c
        lc = pltpu.make_async_copy(v_ref, buf.at[mine], loc)
        lc.start()
        peers = []
        for fx in range(2):
            for fy in range(2):
                for fc in range(2):
                    if fx or fy or fc:
                        peers.append((fx, fy, fc))
        cps = []
        for kk, (fx, fy, fc) in enumerate(peers):
            to = (x ^ fx, y ^ fy, c ^ fc)
            cp = pltpu.make_async_remote_copy(src_ref=v_ref, dst_ref=buf.at[mine], send_sem=send.at[kk], recv_sem=recv.at[kk],
                                              device_id=to, device_id_type=MESH)
            cp.start()
            cps.append((cp, to))
        for kk, (cp, to) in enumerate(cps):
            src = 4 * to[0] + 2 * to[1] + to[2]
            pltpu.make_async_remote_copy(src_ref=v_ref, dst_ref=buf.at[src], send_sem=send.at[kk], recv_sem=recv.at[kk],
                                         device_id=to, device_id_type=MESH).wait_recv()
        for cp, _ in cps:
            cp.wait_send()
        lc.wait()
        acc = buf[0]
        for d in range(1, 8):
            acc = acc + buf[d]
        o_ref[...] = acc

    return pl.pallas_call(
        body, name="allreduce_small", in_specs=[pl.BlockSpec(memory_space=pltpu.VMEM)],
        out_specs=pl.BlockSpec(memory_space=pltpu.VMEM), out_shape=jax.ShapeDtypeStruct((rows, cols), F32),
        scratch_shapes=[pltpu.VMEM((8, rows, cols), F32), pltpu.SemaphoreType.DMA((7,)), pltpu.SemaphoreType.DMA((7,)),
                        pltpu.SemaphoreType.DMA],
        compiler_params=pltpu.CompilerParams(vmem_limit_bytes=VMEM_LIMIT_V7X),
    )(v)


def _pad_w_in(w):
    z = lambda n: jnp.zeros(w.shape[:-1] + (n,), w.dtype)
    return jnp.concatenate([w[..., 0:384], z(64), w[..., 384:416], z(32), w[..., 416:1824],
                            w[..., 1824:1830], z(58), w[..., 400:416], w[..., 384:400], z(32)], axis=-1)


def _unpad_w_in(g):
    x1 = g[..., 448:464] + g[..., Z_F + 80:Z_F + 96]
    x2 = g[..., 464:480] + g[..., Z_F + 64:Z_F + 80]
    return jnp.concatenate([g[..., 0:384], x1, x2, g[..., 512:1920], g[..., 1920:1926]], axis=-1)


def _block_diag(pw):
    out = jnp.zeros((POOL_W, POOL_W), pw.dtype)
    for g in range(4):
        out = out.at[g * 64:(g + 1) * 64, g * 64:(g + 1) * 64].set(pw[g])
    return out


def _rope_tables(s):
    inv_freq = ROPE_THETA ** (-jnp.arange(0, ROPE, 2, dtype=F32) / ROPE)
    ang = jnp.arange(s, dtype=jnp.int32).astype(F32)[:, None] * inv_freq[None, :]
    cos, sin = jnp.cos(ang), jnp.sin(ang)
    zero = lambda n: jnp.zeros((s, n), F32)
    ck = jnp.concatenate([zero(NOPE), cos, cos, zero(DA - NOPE - ROPE)], axis=1)
    sk = jnp.concatenate([zero(NOPE), -sin, sin, zero(DA - NOPE - ROPE)], axis=1)
    cq = jnp.concatenate([jnp.ones((s, NOPE), F32), cos, cos, zero(DA - NOPE - ROPE)], axis=1) * SCALE_MLA
    return dict(cq=cq, sq=sk * SCALE_MLA, ck=ck, sk=sk)


def _mix_fwd(l, x1, wts, sm, tabs):
    z, h2, qn, kvn, qa, ka, va = _mix_in(x1, (sm["mix_norm"][l], sm["q_a_norm"][l], sm["kv_a_norm"][l]), wts["w_in"][l],
                                         wts["wq_a"][l], wts["wq_b"][l], wts["wk"][l], wts["wv"][l], tabs, name=f"mix_in_{l}")
    oa, lse_a = _attn_fwd(qa, ka, va, VDIM, name=f"mla_attn_{l}")

    bd = _block_diag(wts["pool_w"][l]).astype(BF16)
    yb, pooled = _pool_fwd(z, Z_POOL // POOL_W, bd, sm["pool_scale"][l], name=f"pool_{l}")

    fb = jnp.pad(sm["fox_b_f"][l], (0, 8 - H)).reshape(8, 1)
    ft, c3t = _gate_fwd(z, Z_F // DA, fb, name=f"fox_gate_{l}")
    fqa, fka, fva = _fox_prep(z, c3t, name=f"fox_prep_{l}")
    oc, lse_c = _attn_fwd(fqa, fka, fva, FOX_D, name=f"fox_attn_{l}")

    x2, cat = _mix_out(oa, yb, oc, wts["w_out"][l], x1, name=f"mix_out_{l}")
    saved = dict(z=z, h2=h2, qn=qn, kvn=kvn, qa=qa, ka=ka, va=va, oa=oa, lse_a=lse_a, bd=bd, pooled=pooled,
                 fqa=fqa, fka=fka, fva=fva, ft=ft, fb=fb, oc=oc, lse_c=lse_c, cat=cat)
    return x2, saved


def _mix_bwd(l, x1, dx2, sv, wts, sm, tabs, tok=None):
    s = x1.shape[0]
    g = {}
    dx2b = (dx2 if tok is None else dx2 + tok).astype(BF16)
    g["w_out"] = _mm(sv["cat"], dx2b, "tn", name=f"d_w_out_{l}", tm=1024, tn=1024, tk=DW_TOKENS)
    doa, doc, dyb, dl_a, dl_c = _mix_out_bwd(dx2b, wts["w_out"][l], sv["oa"], sv["oc"], name=f"mix_out_bwd_{l}")

    dfqa, dfka, dfva, dcq, dck = _attn_bwd(sv["fqa"], sv["fka"], sv["fva"], doc, sv["lse_c"], dl_c, True, name=f"fox_attn_bwd_{l}")
    dfox = _fox_bwd_prep(dfqa, dfka, dfva, name=f"fox_bwd_prep_{l}")
    dc = jnp.pad(dcq.reshape(H, s) + dck.reshape(H, s), ((0, 8 - H), (0, 0)))
    dft, dfb = _gate_bwd(sv["ft"], sv["fb"], dc, name=f"fox_gate_bwd_{l}")
    g["fox_b_f"] = dfb[:H, 0]

    dq, dbd, g["pool_scale"] = _pool_bwd_a(dyb, sv["pooled"], sv["bd"], sm["pool_scale"][l], name=f"pool_bwd_a_{l}")
    du = _pool_bwd_b(dq, name=f"pool_bwd_b_{l}")
    g["pool_w"] = jnp.stack([dbd[i * 64:(i + 1) * 64, i * 64:(i + 1) * 64] for i in range(4)])

    dqa_, dka_, dva_ = _attn_bwd(sv["qa"], sv["ka"], sv["va"], doa, sv["lse_a"], dl_a, False, name=f"mla_attn_bwd_{l}")
    dqab, dkv, dz3, dz15 = _mla_bwd_prep(dqa_, dka_, dva_, dft, tabs["cq"], tabs["sq"], tabs["ck"], tabs["sk"],
                                         name=f"mla_bwd_prep_{l}")
    wq_ab = jnp.concatenate([wts["wq_a"][l], wts["wq_b"][l]], axis=1)
    wkv = jnp.concatenate([wts["wk"][l], wts["wv"][l]], axis=1)
    dwq = _mm(sv["qn"], dqab, "tn", name=f"d_w_q_b_{l}", tn=768, tk=DW_TOKENS).reshape(Q_RANK, 2, H, DA)
    dwkv = _mm(sv["kvn"], dkv, "tn", name=f"d_w_kv_b_{l}", tn=768, tk=DW_TOKENS).reshape(KV_RANK, 2, H, DA)
    da, db = dwq[:, 0], dwq[:, 1]
    swapped = jnp.concatenate([jnp.zeros((Q_RANK, H, NOPE), F32), db[..., NOPE + HALF_ROPE:NOPE + ROPE],
                               db[..., NOPE:NOPE + HALF_ROPE]], axis=-1)
    g["w_q_b"] = (da[..., :NOPE + ROPE] + swapped).reshape(Q_RANK, H * (NOPE + ROPE))
    g["w_kv_b"] = jnp.concatenate([dwkv[:, 0, :, :NOPE], dwkv[:, 1, :, :VDIM]], axis=-1).reshape(KV_RANK, H * (NOPE + VDIM))
    dqa, g["q_a_norm"] = _rmsnorm_bwd(sv["z"], Z_QA // Q_RANK, sm["q_a_norm"][l], dqab, wq_ab, name=f"q_a_norm_bwd_{l}")
    dkva, g["kv_a_norm"] = _rmsnorm_bwd(sv["z"], Z_KVA // KV_RANK, sm["kv_a_norm"][l], dkv, wkv, name=f"kv_a_norm_bwd_{l}")

    dz = jnp.concatenate([dqa.astype(BF16), dkva.astype(BF16), dz3, du.astype(BF16), dfox, dz15], axis=1)
    g["w_in"] = _mm(sv["h2"], dz, "tn", name=f"d_w_in_{l}", tm=1024, tn=1024, tk=DW_TOKENS)
    dx1, g["mix_norm"] = _rmsnorm_bwd(x1, 0, sm["mix_norm"][l], dz, wts["w_in"][l], dx2, name=f"mix_norm_bwd_{l}")
    return dx1, g


DW_TOKENS = 2048


def _local_step(x, target, wts, sm, late_weights=None, grads_ready=None):
    s = x.shape[0]
    tabs = _rope_tables(s)
    acts = []
    xs = x
    for l in range(DEPTH):
        x1, gu1, act1 = _ffn_fwd(xs, sm["ffn1_norm"][l], wts["ffn1_w_gu"][l], wts["ffn1_w_d2"][l], name=f"ffn1_fwd_{l}")
        if l == 0 and late_weights is not None:
            sm = late_weights("ffn1", x1, sm)
        x2, sv = _mix_fwd(l, x1, wts, sm, tabs)
        if l == 0 and late_weights is not None:
            sm = late_weights("mix", x2, sm)
        x3, gu2, act2 = _ffn_fwd(x2, sm["ffn2_norm"][l], wts["ffn2_w_gu"][l], wts["ffn2_w_d2"][l], name=f"ffn2_fwd_{l}")
        acts.append((xs, gu1, act1, x1, sv, x2, gu2, act2))
        xs = x3
    dx, g_final, loss = _loss_head(xs, sm["final_norm"], target, name="loss_head")
    grads = [dict() for _ in range(DEPTH)]
    for l in reversed(range(DEPTH)):
        x0, gu1, act1, x1, sv, x2, gu2, act2 = acts[l]
        g = grads[l]
        dx, dgu, hh, dy, g["ffn2_norm"] = _ffn_bwd(x2, dx, gu2, sm["ffn2_norm"][l], wts["ffn2_w_gu"][l], wts["ffn2_w_d2"][l],
                                                   name=f"ffn2_bwd_{l}")
        g["ffn2_w_down"] = _mm(act2, dy, "tn", name=f"d_ffn2_w_down_{l}", tm=FF_SHARD, tn=1024, tk=DW_TOKENS)
        g["ffn2_w_gu"] = _mm(hh, dgu, "tn", name=f"d_ffn2_w_gu_{l}", tm=1024, tn=FF_SHARD, tk=DW_TOKENS, n_major_out=True)
        tok = None
        if grads_ready is not None:
            sm, tok = grads_ready(l, "ffn2", g, sm)
        dx, gm = _mix_bwd(l, x1, dx, sv, wts, sm, tabs, tok)
        g.update(gm)
        if grads_ready is not None:
            sm, _ = grads_ready(l, "mix", g, sm)
        dx, dgu, hh, dy, g["ffn1_norm"] = _ffn_bwd(x0, dx, gu1, sm["ffn1_norm"][l], wts["ffn1_w_gu"][l], wts["ffn1_w_d2"][l],
                                                   name=f"ffn1_bwd_{l}")
        if grads_ready is not None:
            sm, tok = grads_ready(l, "ffn1_tokens", {"dx": dx}, sm)
            if tok is not None:
                dy = dy + tok.astype(BF16)
        g["ffn1_w_down"] = _mm(act1, dy, "tn", name=f"d_ffn1_w_down_{l}", tm=FF_SHARD, tn=1024, tk=DW_TOKENS)
        g["ffn1_w_gu"] = _mm(hh, dgu, "tn", name=f"d_ffn1_w_gu_{l}", tm=1024, tn=FF_SHARD, tk=DW_TOKENS, n_major_out=True)
        if grads_ready is not None:
            sm, _ = grads_ready(l, "ffn1", g, sm)
    return loss, dx, grads, g_final


BIG = ["ffn1_w_gu", "ffn1_w_down", "w_in", "w_q_b", "w_kv_b", "w_out", "ffn2_w_gu", "ffn2_w_down"]
SMALL = ["ffn1_norm", "mix_norm", "q_a_norm", "kv_a_norm", "pool_w", "pool_scale", "fox_b_f", "ffn2_norm"]
SMALL_ROWS = 48


WEIGHT_VIEWS = ["ffn1_w_gu", "ffn1_w_d2", "w_in", "wq_a", "wq_b", "wk", "wv", "w_out", "ffn2_w_gu", "ffn2_w_d2"]


def _prepare_weights(gathered, wts):
    for (nm, l), w in gathered.items():
        if nm in ("ffn1_w_gu", "ffn2_w_gu"):
            wts[nm][l] = w
        elif nm in ("ffn1_w_down", "ffn2_w_down"):
            wts[nm[:5] + "w_d2"][l] = w.reshape(2, FF_SHARD, D)
        elif nm in ("w_in", "w_out"):
            wts[nm][l] = w.reshape(D, -1)
        elif nm == "w_q_b":
            wq = jnp.moveaxis(w, 0, 1).reshape(Q_RANK, H, NOPE + ROPE)
            zq = lambda n: jnp.zeros((Q_RANK, H, n), BF16)
            wts["wq_a"][l] = jnp.concatenate([wq, zq(DA - NOPE - ROPE)], axis=-1).reshape(Q_RANK, H * DA)
            wts["wq_b"][l] = jnp.concatenate([zq(NOPE), wq[..., NOPE + HALF_ROPE:], wq[..., NOPE:NOPE + HALF_ROPE],
                                              zq(DA - NOPE - ROPE)], axis=-1).reshape(Q_RANK, H * DA)
        else:
            wkv = jnp.moveaxis(w, 0, 1).reshape(KV_RANK, H, NOPE + VDIM)
            zk = jnp.zeros((KV_RANK, H, DA - NOPE), BF16)
            wts["wk"][l] = jnp.concatenate([wkv[..., :NOPE], zk], axis=-1).reshape(KV_RANK, H * DA)
            wts["wv"][l] = jnp.concatenate([wkv[..., NOPE:], zk], axis=-1).reshape(KV_RANK, H * DA)


def _chip_major(name, g):
    if name in ("ffn1_w_gu", "ffn2_w_gu"):
        return g
    if name in ("ffn1_w_down", "ffn2_w_down", "w_in", "w_out"):
        return g.reshape(N_CHIPS, g.shape[0] // N_CHIPS, g.shape[1])
    return jnp.moveaxis(g.reshape(g.shape[0], N_CHIPS, g.shape[1] // N_CHIPS), 1, 0)


def _pack_small(grads, g_final, loss):
    parts = []
    for l in range(DEPTH):
        for nm in SMALL:
            parts.append(grads[l][nm].reshape(-1))
    parts.append(g_final.reshape(-1))
    parts.append(loss.reshape(1))
    flat = jnp.concatenate(parts)
    return jnp.pad(flat, (0, SMALL_ROWS * D - flat.shape[0])).reshape(SMALL_ROWS, D)


def _unpack_small(packed, params):
    flat = packed.reshape(-1)
    out = {nm: [] for nm in SMALL}
    off = 0
    for l in range(DEPTH):
        for nm in SMALL:
            shp = params[nm].shape[1:]
            n = int(np.prod(shp))
            out[nm].append(flat[off:off + n].reshape(shp))
            off += n
    res = {nm: jnp.stack(v) for nm, v in out.items()}
    res["final_norm"] = flat[off:off + D]
    return res, flat[off + D]


def _update(name, w, g, m, v):
    shp = w.shape
    if w.ndim == 1:
        view = (1, shp[0])
    elif w.size <= 65536:
        view = (shp[0], w.size // shp[0])
    else:
        view = (w.size // shp[-1], shp[-1])
    tr = view[0]
    for cand in (512, 352, 256, 128):
        if view[0] % cand == 0 and view[0] > cand:
            tr = cand
            break
    d, mn, vn = _adamw(w.reshape(view), g.reshape(view), m.reshape(view), v.reshape(view), name="adamw_" + name, tr=tr)
    return d.reshape(shp), mn.reshape(shp), vn.reshape(shp)


WEIGHTS = ['ffn1_norm', 'ffn1_w_gu', 'ffn1_w_down', 'mix_norm', 'w_in', 'q_a_norm', 'w_q_b', 'kv_a_norm', 'w_kv_b', 'pool_w',
           'pool_scale', 'fox_b_f', 'w_out', 'ffn2_norm', 'ffn2_w_gu', 'ffn2_w_down', 'final_norm']


def kernel(x, ffn1_norm, ffn1_w_gu, ffn1_w_down, mix_norm, w_in, q_a_norm, w_q_b, kv_a_norm, w_kv_b, pool_w, pool_scale, fox_b_f, w_out, ffn2_norm, ffn2_w_gu, ffn2_w_down, final_norm, loss_target, m_ffn1_norm, m_ffn1_w_gu, m_ffn1_w_down, m_mix_norm, m_w_in, m_q_a_norm, m_w_q_b, m_kv_a_norm, m_w_kv_b, m_pool_w, m_pool_scale, m_fox_b_f, m_w_out, m_ffn2_norm, m_ffn2_w_gu, m_ffn2_w_down, m_final_norm, v_ffn1_norm, v_ffn1_w_gu, v_ffn1_w_down, v_mix_norm, v_w_in, v_q_a_norm, v_w_q_b, v_kv_a_norm, v_w_kv_b, v_pool_w, v_pool_scale, v_fox_b_f, v_w_out, v_ffn2_norm, v_ffn2_w_gu, v_ffn2_w_down, v_final_norm):
    params = dict(ffn1_norm=ffn1_norm, ffn1_w_gu=ffn1_w_gu, ffn1_w_down=ffn1_w_down, mix_norm=mix_norm, w_in=w_in, q_a_norm=q_a_norm,
                  w_q_b=w_q_b, kv_a_norm=kv_a_norm, w_kv_b=w_kv_b, pool_w=pool_w, pool_scale=pool_scale, fox_b_f=fox_b_f, w_out=w_out,
                  ffn2_norm=ffn2_norm, ffn2_w_gu=ffn2_w_gu, ffn2_w_down=ffn2_w_down, final_norm=final_norm)
    mom = dict(ffn1_norm=m_ffn1_norm, ffn1_w_gu=m_ffn1_w_gu, ffn1_w_down=m_ffn1_w_down, mix_norm=m_mix_norm, w_in=m_w_in,
               q_a_norm=m_q_a_norm, w_q_b=m_w_q_b, kv_a_norm=m_kv_a_norm, w_kv_b=m_w_kv_b, pool_w=m_pool_w, pool_scale=m_pool_scale,
               fox_b_f=m_fox_b_f, w_out=m_w_out, ffn2_norm=m_ffn2_norm, ffn2_w_gu=m_ffn2_w_gu, ffn2_w_down=m_ffn2_w_down,
               final_norm=m_final_norm)
    var = dict(ffn1_norm=v_ffn1_norm, ffn1_w_gu=v_ffn1_w_gu, ffn1_w_down=v_ffn1_w_down, mix_norm=v_mix_norm, w_in=v_w_in,
               q_a_norm=v_q_a_norm, w_q_b=v_w_q_b, kv_a_norm=v_kv_a_norm, w_kv_b=v_w_kv_b, pool_w=v_pool_w, pool_scale=v_pool_scale,
               fox_b_f=v_fox_b_f, w_out=v_w_out, ffn2_norm=v_ffn2_norm, ffn2_w_gu=v_ffn2_w_gu, ffn2_w_down=v_ffn2_w_down,
               final_norm=v_final_norm)

    first = [("ffn1_w_gu", 0), ("ffn1_w_down", 0)]
    mix0 = [(nm, 0) for nm in ("w_in", "w_q_b", "w_kv_b", "w_out")]
    rest = [(nm, l) for nm in BIG for l in range(DEPTH) if (nm, l) not in first + mix0]

    def shards(keys, zero=0.0):
        return [((_pad_w_in(params[nm]) if nm == "w_in" else params[nm])[l] + zero).astype(BF16) for nm, l in keys]

    wts = {nm: [None] * DEPTH for nm in WEIGHT_VIEWS}
    wts["pool_w"] = params["pool_w"]
    got = _gather_blocking(shards(first))
    _prepare_weights(dict(zip(first, got)), wts)
    sems_m, src_m, land_m, token_m = _gather_start(shards(mix0), got[0], "mix0")
    sm = dict(params)
    sm["ffn1_norm"] = params["ffn1_norm"] + token_m[0, 0]
    rest_shards = shards(rest, token_m[0, 0])
    flying = {}

    def late_weights(stage, act, sm_now):
        if stage == "ffn1":
            lands = _gather_forward(_gather_wait(sems_m, src_m, land_m, act, "mix0"), "mix0")
            _prepare_weights(dict(zip(mix0, lands)), wts)
            flying["rest"] = _gather_start(rest_shards, lands[0], "rest")
            sm_next = dict(sm_now)
            sm_next["mix_norm"] = sm_now["mix_norm"] + flying["rest"][3][0, 0]
            return sm_next
        sems_r, src_r, land_r, _ = flying["rest"]
        lands = _gather_forward(_gather_wait(sems_r, src_r, land_r, act, "rest"), "rest")
        _prepare_weights(dict(zip(rest, lands)), wts)
        return sm_now

    pos = _position()
    flight = {}

    groups = {"l1": (1, BIG), "l0a": (0, [nm for nm in BIG if not nm.startswith("ffn1")]),
              "l0b": (0, [nm for nm in BIG if nm.startswith("ffn1")])}
    pending = {}

    def to_chips(key, full, sib):
        psum = _sum2_bf16(pos, full, sib, name=f"chip_sum_{key}")
        s2 = _split_start(_stage2_copies, psum, [((3,) + p.shape[1:], p.dtype) for p in psum], 3 * len(psum),
                          f"reduce_stage2_start_{key}")
        flight[key] = (full, sib, s2)
        return s2[3][0, 0]

    def grads_ready(l, stage, g, sm_now):
        behind, tok = None, None
        if (l, stage) == (1, "ffn1"):
            full = [_chip_major(nm, g[nm]) for nm in BIG]
            pending["l1"] = _split_start(_stage1_copies, full, [((N_CHIPS, f.shape[1] // 2, f.shape[2]), F32) for f in full],
                                         len(full), "reduce_stage1_start_l1")
            behind, tok = "ffn2_norm", pending["l1"][3][0, 0]
        elif (l, stage) == (0, "ffn2"):
            sems1, full_thru, sib_land, _ = pending["l1"]
            full, sib = _split_wait(_stage1_copies, sems1, full_thru, sib_land, g["ffn2_w_down"], "reduce_stage1_wait_l1")
            tok = to_chips("l1", full, sib)
        elif (l, stage) == (0, "mix"):
            full = [_chip_major(nm, g[nm]) for nm in groups["l0a"][1]]
            pending["l0a"] = _split_start(_stage1_copies, full, [((N_CHIPS, f.shape[1] // 2, f.shape[2]), F32) for f in full],
                                          len(full), "reduce_stage1_start_l0a")
            behind, tok = "ffn1_norm", pending["l0a"][3][0, 0]
        elif (l, stage) == (0, "ffn1_tokens"):
            sems1, full_thru, sib_land, _ = pending["l0a"]
            full, sib = _split_wait(_stage1_copies, sems1, full_thru, sib_land, g["dx"], "reduce_stage1_wait_l0a")
            tok = to_chips("l0a", full, sib)
        elif (l, stage) == (0, "ffn1"):
            full = [_chip_major(nm, g[nm]) for nm in groups["l0b"][1]]
            pending["l0b"] = _split_start(_stage1_copies, full, [((N_CHIPS, f.shape[1] // 2, f.shape[2]), F32) for f in full],
                                          len(full), "reduce_stage1_start_l0b")
        if behind is None:
            return sm_now, tok
        sm_next = dict(sm_now)
        sm_next[behind] = sm_now[behind] + tok
        return sm_next, tok

    loss, dx, grads, g_final = _local_step(x[0], loss_target[0], wts, sm, late_weights, grads_ready)

    def view2d(a):
        return a.reshape(a.size // a.shape[-1], a.shape[-1])

    after = pending["l0b"][3]
    done = {nm: None for nm in BIG}
    for key in ("l1", "l0a", "l0b"):
        l, names = groups[key]
        full, sib, (sems2, ps_thru, lands2, _) = flight[key]
        _, recv = _split_wait(_stage2_copies, sems2, ps_thru, lands2, after, f"reduce_stage2_wait_{key}")
        whole = _reduce_stage3(_sum5(pos, full, sib, recv, name=f"grad_sum_{key}"), key)
        for nm, g_l in zip(names, whole):
            if nm == "w_in":
                g_l = _unpad_w_in(g_l)
            tr = max(t for t in (512, 352, 256, 128) if g_l.shape[0] % t == 0)
            done[nm] = _adamw_layer(view2d(params[nm]), g_l, view2d(mom[nm]), view2d(var[nm]), l, done[nm],
                                    name=f"adamw_{nm}_{l}", tr=tr)
        after = done[names[-1]][0][-8:, 0:128]
        if key == "l1":
            small_g, loss = _unpack_small(_allreduce_small(_pack_small(grads, g_final, loss)), params)
            sems1, full_thru, sib_land, _ = pending["l0b"]
            full_b, sib_b = _split_wait(_stage1_copies, sems1, full_thru, sib_land, after + small_g["final_norm"][0],
                                        "reduce_stage1_wait_l0b")
            after = after + to_chips("l0b", full_b, sib_b)
    gw, delta, new_m, new_v = dict(small_g), {}, {}, {}
    for nm in BIG:
        delta[nm], new_m[nm], new_v[nm], gw[nm] = [a.reshape(params[nm].shape) for a in done[nm]]
    for nm in small_g:
        delta[nm], new_m[nm], new_v[nm] = _update(nm, params[nm], gw[nm], mom[nm], var[nm])
    return (loss, dx[None], *[gw[n] for n in WEIGHTS], *[delta[n] for n in WEIGHTS], *[new_m[n] for n in WEIGHTS],
            *[new_v[n] for n in WEIGHTS])
```

```python
import functools
import math

import jax
import jax.numpy as jnp
import numpy as np
from jax import lax
from jax.experimental import pallas as pl
from jax.experimental.pallas import tpu as pltpu

F32 = jnp.float32
BF16 = jnp.bfloat16
MESH = pl.DeviceIdType.MESH
HBM_SPEC = pl.BlockSpec(memory_space=pltpu.HBM)

D = 1024
DEPTH = 2
D_FF = 2816
FF_SHARD = 1408
N_CHIPS = 4
H = 6
NOPE, ROPE, VDIM = 64, 32, 64
HALF_ROPE = ROPE // 2
Q_RANK, KV_RANK = 256, 128
POOL_W = 256
FOX_D = 64
N_IN = 1830
NZ = 2048
ROPE_THETA = 10000.0
EPS = 1e-6
POOL_HALO = 16
Z_QA, Z_KVA, Z_KR, Z_POOL, Z_FOX, Z_F = 0, 256, 384, 512, 768, 1920

ADAM_LR, ADAM_B1, ADAM_B2, ADAM_EPS, ADAM_WD, ADAM_STEP = 0.001, 0.9, 0.999, 1e-08, 0.01, 10

VMEM_LIMIT_V7X = 56 * 1024 * 1024


def _cp(sem=None, vmem=VMEM_LIMIT_V7X):
    return pltpu.CompilerParams(dimension_semantics=sem, vmem_limit_bytes=vmem)


def _sigmoid(x):
    return 0.5 * jnp.tanh(0.5 * x) + 0.5


def _dot(a, b, dims):
    return lax.dot_general(a, b, (dims, ((), ())), preferred_element_type=F32)


NN = ((1,), (0,))
NT = ((1,), (1,))
TN = ((0,), (0,))


def _mm(a, b, mode, *, name, out_dtype=F32, add=None, alpha=None, tm=512, tn=512, tk=512, n_major_out=False):
    if mode == "nn":
        (m, k), (k2, n) = a.shape, b.shape
    elif mode == "nt":
        (m, k), (n, k2) = a.shape, b.shape
    else:
        (k, m), (k2, n) = a.shape, b.shape
    assert k == k2
    tm, tn, tk = min(tm, m), min(tn, n), min(tk, k)
    assert m % tm == 0 and n % tn == 0 and k % tk == 0, (name, m, n, k, tm, tn, tk)
    nk = k // tk
    dims = {"nn": NN, "nt": NT, "tn": TN}[mode]
    a_spec = pl.BlockSpec((tk, tm), lambda i, j, kk: (kk, i)) if mode == "tn" else pl.BlockSpec((tm, tk), lambda i, j, kk: (i, kk))
    b_spec = pl.BlockSpec((tn, tk), lambda i, j, kk: (j, kk)) if mode == "nt" else pl.BlockSpec((tk, tn), lambda i, j, kk: (kk, j))
    in_specs = [a_spec, b_spec]
    args = [a, b]
    if add is not None:
        in_specs.append(pl.BlockSpec((tm, tn), lambda i, j, kk: (i, j)))
        args.append(add)
    if n_major_out:
        out_shape = jax.ShapeDtypeStruct((n // tn, m, tn), out_dtype)
        out_spec = pl.BlockSpec((None, tm, tn), lambda i, j, kk: (j, i, 0))
    else:
        out_shape = jax.ShapeDtypeStruct((m, n), out_dtype)
        out_spec = pl.BlockSpec((tm, tn), lambda i, j, kk: (i, j))

    def body(*refs):
        a_ref, b_ref = refs[0], refs[1]
        add_ref = refs[2] if add is not None else None
        o_ref, acc = refs[-2], refs[-1]
        kk = pl.program_id(2)

        @pl.when(kk == 0)
        def _():
            acc[...] = jnp.zeros_like(acc)

        acc[...] += _dot(a_ref[...].astype(BF16), b_ref[...].astype(BF16), dims)

        @pl.when(kk == nk - 1)
        def _():
            r = acc[...]
            if alpha is not None:
                r = r * alpha
            if add_ref is not None:
                r = r + add_ref[...].astype(F32)
            o_ref[...] = r.astype(out_dtype)

    return pl.pallas_call(
        body, name=name, grid=(m // tm, n // tn, nk), in_specs=in_specs, out_specs=out_spec, out_shape=out_shape,
        scratch_shapes=[pltpu.VMEM((tm, tn), F32)],
        compiler_params=_cp(("parallel", "parallel", "arbitrary")),
    )(*args)


def _rmsnorm_bwd(x, col_block, gain, da, w, dres=None, *, name, tm=512):
    s = x.shape[0]
    k, n = w.shape
    tm = min(tm, s)

    def body(*refs):
        x_ref, g_ref, da_ref, w_ref = refs[:4]
        dres_ref = refs[4] if dres is not None else None
        dx_ref, dg_ref = refs[-2], refs[-1]
        xv = x_ref[...]
        r = lax.rsqrt(jnp.mean(xv * xv, axis=-1, keepdims=True) + EPS)
        dhv = _dot(da_ref[...], w_ref[...], NT)
        a = dhv * g_ref[...]
        dx = r * a - xv * (r * r * r) * jnp.mean(a * xv, axis=-1, keepdims=True)
        if dres_ref is not None:
            dx = dx + dres_ref[...]
        dx_ref[...] = dx

        @pl.when(pl.program_id(0) == 0)
        def _():
            dg_ref[...] = jnp.zeros_like(dg_ref)

        dg_ref[...] += jnp.sum(dhv * xv * r, axis=0, keepdims=True)

    in_specs = [pl.BlockSpec((tm, k), lambda i: (i, col_block)), pl.BlockSpec((1, k), lambda i: (0, 0)),
                pl.BlockSpec((tm, n), lambda i: (i, 0)), pl.BlockSpec((k, n), lambda i: (0, 0))]
    args = [x, gain.reshape(1, k), da, w]
    if dres is not None:
        in_specs.append(pl.BlockSpec((tm, k), lambda i: (i, 0)))
        args.append(dres)
    dx, dg = pl.pallas_call(
        body, name=name, grid=(s // tm,), in_specs=in_specs,
        out_specs=[pl.BlockSpec((tm, k), lambda i: (i, 0)), pl.BlockSpec((1, k), lambda i: (0, 0))],
        out_shape=[jax.ShapeDtypeStruct((s, k), F32), jax.ShapeDtypeStruct((1, k), F32)],
        compiler_params=_cp(("arbitrary",)),
    )(*args)
    return dx, dg.reshape(k)


def _ffn_fwd(x, gain, w_gu4, w_d2, *, name, tm=256):
    s = x.shape[0]
    tm = min(tm, s)

    def body(x_ref, g_ref, wgu_ref, wd_ref, xo_ref, dgu_ref, act_ref):
        xv = x_ref[...]
        r = lax.rsqrt(jnp.mean(xv * xv, axis=-1, keepdims=True) + EPS)
        hv = (xv * r * g_ref[...]).astype(BF16)
        y = jnp.zeros((tm, D), F32)
        for j in range(2):
            g = _dot(hv, wgu_ref[j], NN)
            u = _dot(hv, wgu_ref[2 + j], NN)
            sg = _sigmoid(g)
            silu = g * sg
            dgu_ref[:, j * FF_SHARD:(j + 1) * FF_SHARD] = (u * (sg * (1.0 + g * (1.0 - sg)))).astype(BF16)
            dgu_ref[:, D_FF + j * FF_SHARD:D_FF + (j + 1) * FF_SHARD] = silu.astype(BF16)
            act = (silu * u).astype(BF16)
            act_ref[:, j * FF_SHARD:(j + 1) * FF_SHARD] = act
            y = y + _dot(act, wd_ref[j], NN)
        xo_ref[...] = xv + 0.5 * y

    row = lambda i: (i, 0)
    return pl.pallas_call(
        body, name=name, grid=(s // tm,),
        in_specs=[pl.BlockSpec((tm, D), row), pl.BlockSpec((1, D), lambda i: (0, 0)),
                  pl.BlockSpec((N_CHIPS, D, FF_SHARD), lambda i: (0, 0, 0), pipeline_mode=pl.Buffered(1)),
                  pl.BlockSpec((2, FF_SHARD, D), lambda i: (0, 0, 0), pipeline_mode=pl.Buffered(1))],
        out_specs=[pl.BlockSpec((tm, D), row), pl.BlockSpec((tm, 2 * D_FF), row), pl.BlockSpec((tm, D_FF), row)],
        out_shape=[jax.ShapeDtypeStruct((s, D), F32), jax.ShapeDtypeStruct((s, 2 * D_FF), BF16),
                   jax.ShapeDtypeStruct((s, D_FF), BF16)],
        compiler_params=_cp(("parallel",)),
    )(x, gain.reshape(1, D), w_gu4, w_d2)


FFN_ROW_CHUNK = 32


def _ffn_bwd(x, dxo, dloc, gain, w_gu4, w_d2, *, name, tm=256):
    s = x.shape[0]
    tm = min(tm, s)

    def body(x_ref, dxo_ref, dloc_ref, g_ref, wgu_ref, wd_ref, dx_ref, dgu_ref, h_ref, dy_ref, dg_ref):
        xv = x_ref[...]
        r = lax.rsqrt(jnp.mean(xv * xv, axis=-1, keepdims=True) + EPS)
        xh = xv * r
        h_ref[...] = (xh * g_ref[...]).astype(BF16)
        dxov = dxo_ref[...]
        dy = (0.5 * dxov).astype(BF16)
        dy_ref[...] = dy
        gcols = [slice(j * FF_SHARD, (j + 1) * FF_SHARD) for j in range(2)]
        ucols = [slice(D_FF + j * FF_SHARD, D_FF + (j + 1) * FF_SHARD) for j in range(2)]
        dacts = [_dot(dy, wd_ref[j], NT) for j in range(2)]
        for r0 in range(0, tm, FFN_ROW_CHUNK):
            rows = slice(r0, r0 + FFN_ROW_CHUNK)
            for j in range(2):
                da = dacts[j][rows]
                dgu_ref[rows, gcols[j]] = (da * dloc_ref[rows, gcols[j]].astype(F32)).astype(BF16)
                dgu_ref[rows, ucols[j]] = (da * dloc_ref[rows, ucols[j]].astype(F32)).astype(BF16)
        dh = jnp.zeros((tm, D), F32)
        for j in range(2):
            dh = dh + _dot(dgu_ref[:, gcols[j]], wgu_ref[j], NT) + _dot(dgu_ref[:, ucols[j]], wgu_ref[2 + j], NT)
        a = dh * g_ref[...]
        dx_ref[...] = dxov + r * a - xh * (r * jnp.mean(a * xh, axis=-1, keepdims=True))

        @pl.when(pl.program_id(0) == 0)
        def _():
            dg_ref[...] = jnp.zeros_like(dg_ref)

        dg_ref[...] += jnp.sum(dh * xh, axis=0, keepdims=True)

    row = lambda i: (i, 0)
    outs = pl.pallas_call(
        body, name=name, grid=(s // tm,),
        in_specs=[pl.BlockSpec((tm, D), row), pl.BlockSpec((tm, D), row), pl.BlockSpec((tm, 2 * D_FF), row),
                  pl.BlockSpec((1, D), lambda i: (0, 0)),
                  pl.BlockSpec((N_CHIPS, D, FF_SHARD), lambda i: (0, 0, 0), pipeline_mode=pl.Buffered(1)),
                  pl.BlockSpec((2, FF_SHARD, D), lambda i: (0, 0, 0), pipeline_mode=pl.Buffered(1))],
        out_specs=[pl.BlockSpec((tm, D), row), pl.BlockSpec((tm, 2 * D_FF), row),
                   pl.BlockSpec((tm, D), row), pl.BlockSpec((tm, D), row), pl.BlockSpec((1, D), lambda i: (0, 0))],
        out_shape=[jax.ShapeDtypeStruct((s, D), F32), jax.ShapeDtypeStruct((s, 2 * D_FF), BF16),
                   jax.ShapeDtypeStruct((s, D), BF16), jax.ShapeDtypeStruct((s, D), BF16), jax.ShapeDtypeStruct((1, D), F32)],
        compiler_params=_cp(("arbitrary",)),
    )(x, dxo, dloc, gain.reshape(1, D), w_gu4, w_d2)
    dx, dgu, h, dy, dg = outs
    return dx, dgu, h, dy, dg.reshape(D)


DA = 128
SCALE_MLA = 1.0 / math.sqrt(NOPE + ROPE)
SCALE_FOX = 1.0 / math.sqrt(FOX_D)


def _causal_blocks(nb, key_major):
    if key_major:
        pairs = [(i, j) for j in range(nb) for i in range(j, nb)]
    else:
        pairs = [(i, j) for i in range(nb) for j in range(i + 1)]
    return (jnp.asarray(np.array([p[0] for p in pairs], np.int32)), jnp.asarray(np.array([p[1] for p in pairs], np.int32)))


HEADS_PER_STEP = 3
ROW_CHUNK = 64

def _col_to_row(col):
    return jnp.broadcast_to(col, (col.shape[0], DA)).T[0:1, :]


def _attn_fwd(qa, ka, va, dv, *, name, t=512):
    h, s, _ = qa.shape
    t = min(t, s)
    nb = s // t
    g = H
    qi, kj = _causal_blocks(nb, key_major=False)

    rc = min(ROW_CHUNK, t)

    def body(qi_ref, kj_ref, q_ref, k_ref, v_ref, o_ref, lse_ref, m_sc, acc_sc, p_sc, a_sc):
        n = pl.program_id(1)
        i, j = qi_ref[n], kj_ref[n]

        @pl.when(j == 0)
        def _():
            m_sc[...] = jnp.full_like(m_sc, -jnp.inf)
            acc_sc[...] = jnp.zeros_like(acc_sc)

        def step(masked):
            scs = [_dot(q_ref[hh], k_ref[hh], NT) for hh in range(g)]
            for r0 in range(0, t, rc):
                rows = slice(r0, r0 + rc)
                for hh in range(g):
                    sr = scs[hh][rows]
                    if masked:
                        row = lax.broadcasted_iota(jnp.int32, (rc, t), 0) + r0
                        col = lax.broadcasted_iota(jnp.int32, (rc, t), 1)
                        sr = jnp.where(col <= row, sr, -jnp.inf)
                    tiles = [sr[:, c0:c0 + DA] for c0 in range(0, t, DA)]
                    top = tiles[0]
                    for tile in tiles[1:]:
                        top = jnp.maximum(top, tile)
                    m_old = m_sc[hh, rows]
                    m_new = jnp.maximum(m_old, jnp.max(top, axis=-1, keepdims=True))
                    for c0, tile in zip(range(0, t, DA), tiles):
                        p_sc[hh, rows, c0:c0 + DA] = jnp.exp(tile - m_new).astype(BF16)
                    a_sc[hh, rows] = jnp.exp(m_old - m_new)
                    m_sc[hh, rows] = m_new
            for hh in range(g):
                acc_sc[hh] = a_sc[hh] * acc_sc[hh] + _dot(p_sc[hh], v_ref[hh], NN)

        @pl.when(j < i)
        def _():
            step(False)

        @pl.when(j == i)
        def _():
            step(True)
            for hh in range(g):
                acc = acc_sc[hh]
                l = acc[:, dv:dv + 1]
                o_ref[hh] = acc[:, :dv] / l
                lse_ref[hh] = _col_to_row(m_sc[hh][:, 0:1] + jnp.log(l))

    qmap = lambda hg, n, qi_r, kj_r: (hg, qi_r[n], 0)
    kmap = lambda hg, n, qi_r, kj_r: (hg, kj_r[n], 0)
    return pl.pallas_call(
        body, name=name,
        grid_spec=pltpu.PrefetchScalarGridSpec(
            num_scalar_prefetch=2, grid=(h // g, qi.shape[0]),
            in_specs=[pl.BlockSpec((g, t, DA), qmap), pl.BlockSpec((g, t, DA), kmap), pl.BlockSpec((g, t, DA), kmap)],
            out_specs=[pl.BlockSpec((g, t, dv), qmap), pl.BlockSpec((g, 1, t), lambda hg, n, qi_r, kj_r: (hg, 0, qi_r[n]))],
            scratch_shapes=[pltpu.VMEM((g, t, DA), F32), pltpu.VMEM((g, t, DA), F32), pltpu.VMEM((g, t, t), BF16),
                            pltpu.VMEM((g, t, DA), F32)]),
        out_shape=[jax.ShapeDtypeStruct((h, s, dv), F32), jax.ShapeDtypeStruct((h, 1, s), F32)],
        compiler_params=_cp(("parallel", "arbitrary")),
    )(qi, kj, qa, ka, va)


def _attn_bwd(qa, ka, va, doa, lse_row, delta_row, decay, *, name, t=512):
    h, s, _ = qa.shape
    t = min(t, s)
    nb = s // t
    g = HEADS_PER_STEP
    rc = min(ROW_CHUNK, t)
    qi, kj = _causal_blocks(nb, key_major=True)
    nsteps = qi.shape[0]

    def body(*refs):
        qi_ref, kj_ref, q_ref, k_ref, v_ref, do_ref, lse_ref, dl_ref = refs[:8]
        p_sc, ds_sc = refs[-2:]
        if decay:
            dq_ref, dk_ref, dv_ref, dcq_ref, dck_ref, dq_acc, dk_acc, dv_acc, dcq_acc, dck_acc = refs[8:-2]
        else:
            dq_ref, dk_ref, dv_ref, dq_acc, dk_acc, dv_acc = refs[8:-2]
        n = pl.program_id(1)
        i, j = qi_ref[n], kj_ref[n]

        @pl.when(n == 0)
        def _():
            dq_acc[...] = jnp.zeros_like(dq_acc)
            if decay:
                dcq_acc[...] = jnp.zeros_like(dcq_acc)

        @pl.when(i == j)
        def _():
            dk_acc[...] = jnp.zeros_like(dk_acc)
            dv_acc[...] = jnp.zeros_like(dv_acc)
            if decay:
                dck_acc[...] = jnp.zeros_like(dck_acc)

        def step(masked):
            sts = [_dot(k_ref[hh], q_ref[hh], NT) for hh in range(g)]
            dpts = [_dot(v_ref[hh], do_ref[hh], NT) for hh in range(g)]
            dcq = [jnp.zeros((1, t), F32) for _ in range(g)]
            for r0 in range(0, t, rc):
                rows = slice(r0, r0 + rc)
                for hh in range(g):
                    st = sts[hh][rows]
                    if masked:
                        row = lax.broadcasted_iota(jnp.int32, (rc, t), 0) + r0
                        col = lax.broadcasted_iota(jnp.int32, (rc, t), 1)
                        st = jnp.where(row <= col, st, -jnp.inf)
                    pt = jnp.exp(st - lse_ref[hh])
                    dst = pt * (dpts[hh][rows] - dl_ref[hh])
                    p_sc[hh, rows] = pt.astype(BF16)
                    ds_sc[hh, rows] = dst.astype(BF16)
                    if decay:
                        dcq[hh] = dcq[hh] + jnp.sum(dst, axis=0, keepdims=True)
                        dck_acc[hh, rows] -= jnp.sum(dst, axis=1, keepdims=True)
            for hh in range(g):
                dv_acc[hh] += _dot(p_sc[hh], do_ref[hh], NN)
                dk_acc[hh] += _dot(ds_sc[hh], q_ref[hh], NN)
                dq_acc[hh, i] += _dot(ds_sc[hh], k_ref[hh], TN)
                if decay:
                    dcq_acc[hh, i] += dcq[hh]

        @pl.when(i > j)
        def _():
            step(False)

        @pl.when(i == j)
        def _():
            step(True)

        @pl.when(i == nb - 1)
        def _():
            dk_ref[...] = dk_acc[...]
            dv_ref[...] = dv_acc[...]
            if decay:
                for hh in range(g):
                    dck_ref[hh] = _col_to_row(dck_acc[hh])

        @pl.when(n == nsteps - 1)
        def _():
            dq_ref[...] = dq_acc[...]
            if decay:
                dcq_ref[...] = dcq_acc[...]

    kmap = lambda hg, n, qi_r, kj_r: (hg, kj_r[n], 0)
    qmap = lambda hg, n, qi_r, kj_r: (hg, qi_r[n], 0)
    qrow = lambda hg, n, qi_r, kj_r: (hg, 0, qi_r[n])
    krow = lambda hg, n, qi_r, kj_r: (hg, 0, kj_r[n])
    whole = lambda hg, n, qi_r, kj_r: (hg, 0, 0, 0)
    in_specs = [pl.BlockSpec((g, t, DA), qmap), pl.BlockSpec((g, t, DA), kmap), pl.BlockSpec((g, t, DA), kmap),
                pl.BlockSpec((g, t, DA), qmap), pl.BlockSpec((g, 1, t), qrow), pl.BlockSpec((g, 1, t), qrow)]
    out_specs = [pl.BlockSpec((g, nb, t, DA), whole), pl.BlockSpec((g, t, DA), kmap), pl.BlockSpec((g, t, DA), kmap)]
    out_shape = [jax.ShapeDtypeStruct((h, nb, t, DA), F32), jax.ShapeDtypeStruct((h, s, DA), F32), jax.ShapeDtypeStruct((h, s, DA), F32)]
    scratch = [pltpu.VMEM((g, nb, t, DA), F32), pltpu.VMEM((g, t, DA), F32), pltpu.VMEM((g, t, DA), F32)]
    if decay:
        out_specs += [pl.BlockSpec((g, nb, 1, t), whole), pl.BlockSpec((g, 1, t), krow)]
        out_shape += [jax.ShapeDtypeStruct((h, nb, 1, t), F32), jax.ShapeDtypeStruct((h, 1, s), F32)]
        scratch += [pltpu.VMEM((g, nb, 1, t), F32), pltpu.VMEM((g, t, 1), F32)]
    scratch += [pltpu.VMEM((g, t, t), BF16), pltpu.VMEM((g, t, t), BF16)]
    outs = pl.pallas_call(
        body, name=name,
        grid_spec=pltpu.PrefetchScalarGridSpec(num_scalar_prefetch=2, grid=(h // g, nsteps), in_specs=in_specs, out_specs=out_specs,
                                               scratch_shapes=scratch),
        out_shape=out_shape, compiler_params=_cp(("parallel", "arbitrary")),
    )(qi, kj, qa, ka, va, doa, lse_row, delta_row)
    outs = list(outs)
    outs[0] = outs[0].reshape(h, s, DA)
    if decay:
        outs[3] = outs[3].reshape(h, 1, s)
    return outs


def _sel(rows, cols, pairs, value=1.0):
    m = np.zeros((rows, cols), np.float32)
    for r, c in pairs:
        m[r, c] = value
    return jnp.asarray(m, BF16)


def _lane_row(lanes):
    m = np.zeros((1, DA), np.float32)
    m[0, list(lanes)] = 1.0
    return jnp.asarray(m)


def _rms(xv, gain):
    r = lax.rsqrt(jnp.mean(xv * xv, axis=-1, keepdims=True) + EPS)
    return xv * r * gain


def _mix_in(x, gains, w_in, wq_a, wq_b, wk, wv, tabs, *, name, tm=512):
    s = x.shape[0]
    tm = min(tm, s)
    one = _lane_row([VDIM])
    g_mix, g_q, g_kv = gains

    def body(x_ref, gm_ref, gq_ref, gkv_ref, win_ref, wa_ref, wb_ref, wk_ref, wv_ref, cq_ref, sq_ref, ck_ref, sk_ref,
             one_ref, z_ref, h_ref, qn_ref, kvn_ref, qa_ref, ka_ref, va_ref):
        hv = _rms(x_ref[...], gm_ref[...]).astype(BF16)
        h_ref[...] = hv
        z = _dot(hv, win_ref[...], NN)
        z_ref[...] = z
        qn = _rms(z[:, Z_QA:Z_QA + Q_RANK], gq_ref[...]).astype(BF16)
        kvn = _rms(z[:, Z_KVA:Z_KVA + KV_RANK], gkv_ref[...]).astype(BF16)
        qn_ref[...] = qn
        kvn_ref[...] = kvn
        c, sn = cq_ref[...], sq_ref[...]
        kpe = z[:, Z_KR:Z_KR + DA] * ck_ref[...] + z[:, Z_F:Z_F + DA] * sk_ref[...]
        for hh in range(H):
            cols = slice(hh * DA, (hh + 1) * DA)
            qa_ref[hh] = (_dot(qn, wa_ref[:, cols], NN) * c + _dot(qn, wb_ref[:, cols], NN) * sn).astype(BF16)
            ka_ref[hh] = (_dot(kvn, wk_ref[:, cols], NN) + kpe).astype(BF16)
            va_ref[hh] = (_dot(kvn, wv_ref[:, cols], NN) + one_ref[...]).astype(BF16)

    row = lambda i: (i, 0)
    fixed = lambda i: (0, 0)
    full = lambda a: pl.BlockSpec(a.shape, fixed)
    tab = pl.BlockSpec((tm, DA), row)
    heads = pl.BlockSpec((H, tm, DA), lambda i: (0, i, 0))
    return pl.pallas_call(
        body, name=name, grid=(s // tm,),
        in_specs=[pl.BlockSpec((tm, D), row), pl.BlockSpec((1, D), fixed), pl.BlockSpec((1, Q_RANK), fixed),
                  pl.BlockSpec((1, KV_RANK), fixed), full(w_in), full(wq_a), full(wq_b), full(wk), full(wv),
                  tab, tab, tab, tab, pl.BlockSpec((1, DA), fixed)],
        out_specs=[pl.BlockSpec((tm, NZ), row), pl.BlockSpec((tm, D), row), pl.BlockSpec((tm, Q_RANK), row),
                   pl.BlockSpec((tm, KV_RANK), row), heads, heads, heads],
        out_shape=[jax.ShapeDtypeStruct((s, NZ), F32), jax.ShapeDtypeStruct((s, D), BF16),
                   jax.ShapeDtypeStruct((s, Q_RANK), BF16), jax.ShapeDtypeStruct((s, KV_RANK), BF16)]
        + [jax.ShapeDtypeStruct((H, s, DA), BF16)] * 3,
        compiler_params=_cp(("parallel",)),
    )(x, g_mix.reshape(1, D), g_q.reshape(1, Q_RANK), g_kv.reshape(1, KV_RANK), w_in, wq_a, wq_b, wk, wv,
      tabs["cq"], tabs["sq"], tabs["ck"], tabs["sk"], one)


DEC_C = (FOX_D, FOX_D + 1, FOX_D + 2)
DEC_1 = (FOX_D + 3, FOX_D + 4, FOX_D + 5)


def _fox_prep(z, c3t, *, name, tm=512):
    s = z.shape[0]
    tm = min(tm, s)
    w = H * FOX_D
    left = [(r, r) for r in range(FOX_D)]
    right = [(FOX_D + r, r) for r in range(FOX_D)]
    pq = jnp.stack([_sel(DA, DA, left, SCALE_FOX), _sel(DA, DA, right, SCALE_FOX)])
    pk = jnp.stack([_sel(DA, DA, left), _sel(DA, DA, right)])
    pcq = jnp.stack([_sel(32, DA, [(hh + 8 * k, DEC_C[k]) for k in range(3)]) for hh in range(H)])
    pck = jnp.stack([_sel(32, DA, [(hh + 8 * k, DEC_1[k]) for k in range(3)], -1.0) for hh in range(H)])
    rows3 = jnp.concatenate([_lane_row(DEC_1), _lane_row(DEC_C), _lane_row([FOX_D])], axis=0)

    def body(zq_ref, zk_ref, zv_ref, c_ref, pq_ref, pk_ref, pcq_ref, pck_ref, r_ref, qa_ref, ka_ref, va_ref):
        c3 = c_ref[...]
        for pair in range(H // 2):
            lanes = slice(pair * DA, (pair + 1) * DA)
            zq, zk, zv = zq_ref[:, lanes].astype(BF16), zk_ref[:, lanes].astype(BF16), zv_ref[:, lanes].astype(BF16)
            for side in range(2):
                hh = 2 * pair + side
                qa_ref[hh] = (_dot(zq, pq_ref[side], NN) + _dot(c3, pcq_ref[hh], TN) + r_ref[0:1, :]).astype(BF16)
                ka_ref[hh] = (_dot(zk, pk_ref[side], NN) + _dot(c3, pck_ref[hh], TN) + r_ref[1:2, :]).astype(BF16)
                va_ref[hh] = (_dot(zv, pk_ref[side], NN) + r_ref[2:3, :]).astype(BF16)

    fixed2 = lambda i: (0, 0)
    fixed3 = lambda i: (0, 0, 0)
    heads = pl.BlockSpec((H, tm, DA), lambda i: (0, i, 0))
    zblk = lambda c: pl.BlockSpec((tm, w), lambda i: (i, c))
    return pl.pallas_call(
        body, name=name, grid=(s // tm,),
        in_specs=[zblk(Z_FOX // w), zblk(Z_FOX // w + 1), zblk(Z_FOX // w + 2), pl.BlockSpec((32, tm), lambda i: (0, i)),
                  pl.BlockSpec((2, DA, DA), fixed3), pl.BlockSpec((2, DA, DA), fixed3),
                  pl.BlockSpec((H, 32, DA), fixed3), pl.BlockSpec((H, 32, DA), fixed3), pl.BlockSpec((3, DA), fixed2)],
        out_specs=[heads, heads, heads], out_shape=[jax.ShapeDtypeStruct((H, s, DA), BF16)] * 3,
        compiler_params=_cp(("parallel",)),
    )(z, z, z, c3t, pq, pk, pcq, pck, rows3)


def _mix_out(oa, yb, oc, w_out, x1, *, name, tm=512):
    s = yb.shape[0]
    tm = min(tm, s)
    e2 = jnp.stack([_sel(VDIM, DA, [(r, r) for r in range(VDIM)]), _sel(VDIM, DA, [(r, VDIM + r) for r in range(VDIM)])])

    def body(oa_ref, yb_ref, oc_ref, e_ref, w_ref, x_ref, x2_ref, cat_ref):
        def pairs(o_ref):
            return [(_dot(o_ref[2 * p].astype(BF16), e_ref[0], NN) + _dot(o_ref[2 * p + 1].astype(BF16), e_ref[1], NN)).astype(BF16)
                    for p in range(H // 2)]

        cat = jnp.concatenate(pairs(oa_ref) + [yb_ref[...].astype(BF16)] + pairs(oc_ref), axis=1)
        cat_ref[...] = cat
        x2_ref[...] = x_ref[...] + _dot(cat, w_ref[...], NN)

    row = lambda i: (i, 0)
    heads = pl.BlockSpec((H, tm, VDIM), lambda i: (0, i, 0))
    return pl.pallas_call(
        body, name=name, grid=(s // tm,),
        in_specs=[heads, pl.BlockSpec((tm, POOL_W), row), heads, pl.BlockSpec((2, VDIM, DA), lambda i: (0, 0, 0)),
                  pl.BlockSpec((D, D), lambda i: (0, 0)), pl.BlockSpec((tm, D), row)],
        out_specs=[pl.BlockSpec((tm, D), row), pl.BlockSpec((tm, D), row)],
        out_shape=[jax.ShapeDtypeStruct((s, D), F32), jax.ShapeDtypeStruct((s, D), BF16)],
        compiler_params=_cp(("parallel",)),
    )(oa, yb, oc, e2, w_out, x1)


def _mix_out_bwd(dx2b, w_out, oa, oc, *, name, tm=512):
    s = dx2b.shape[0]
    tm = min(tm, s)
    f2 = jnp.stack([_sel(DA, DA, [(r, r) for r in range(VDIM)]), _sel(DA, DA, [(VDIM + r, r) for r in range(VDIM)])])
    nv = H * VDIM

    def body(dx_ref, w_ref, oa_ref, oc_ref, f_ref, doa_ref, doc_ref, dyb_ref, dla_ref, dlc_ref):
        dcat = _dot(dx_ref[...], w_ref[...], NT)
        dyb_ref[...] = dcat[:, nv:nv + POOL_W]
        for base, o_ref, do_ref, dl_ref in ((0, oa_ref, doa_ref, dla_ref), (nv + POOL_W, oc_ref, doc_ref, dlc_ref)):
            for p in range(H // 2):
                blk = dcat[:, base + p * DA:base + (p + 1) * DA].astype(BF16)
                for side in range(2):
                    hh = 2 * p + side
                    do = _dot(blk, f_ref[side], NN)
                    do_ref[hh] = do.astype(BF16)
                    dl_ref[hh] = _col_to_row(jnp.sum(do[:, :VDIM] * o_ref[hh], axis=-1, keepdims=True))

    row = lambda i: (i, 0)
    heads = lambda w: pl.BlockSpec((H, tm, w), lambda i: (0, i, 0))
    return pl.pallas_call(
        body, name=name, grid=(s // tm,),
        in_specs=[pl.BlockSpec((tm, D), row), pl.BlockSpec((D, D), lambda i: (0, 0)), heads(VDIM), heads(VDIM),
                  pl.BlockSpec((2, DA, DA), lambda i: (0, 0, 0))],
        out_specs=[heads(DA), heads(DA), pl.BlockSpec((tm, POOL_W), row),
                   pl.BlockSpec((H, 1, tm), lambda i: (0, 0, i)), pl.BlockSpec((H, 1, tm), lambda i: (0, 0, i))],
        out_shape=[jax.ShapeDtypeStruct((H, s, DA), BF16), jax.ShapeDtypeStruct((H, s, DA), BF16),
                   jax.ShapeDtypeStruct((s, POOL_W), F32), jax.ShapeDtypeStruct((H, 1, s), F32), jax.ShapeDtypeStruct((H, 1, s), F32)],
        compiler_params=_cp(("parallel",)),
    )(dx2b, w_out, oa, oc, f2)


def _mla_bwd_prep(dqa, dka, dva, dft, qn, kvn, cq, sq, ck, sk, *, name, tm=512):
    s = dqa.shape[1]
    tm = min(tm, s)
    keep = _lane_row(range(NOPE))

    def body(dq_ref, dk_ref, dv_ref, dft_ref, qn_ref, kvn_ref, cq_ref, sq_ref, ck_ref, sk_ref, keep_ref,
             dqab_ref, dkv_ref, dz3_ref, dz15_ref, dwq_ref, dwkv_ref):
        cqv, sqv = cq_ref[...], sq_ref[...]
        dkpe = jnp.zeros((tm, DA), F32)
        for hh in range(H):
            lanes = slice(hh * DA, (hh + 1) * DA)
            dq = dq_ref[hh]
            dqab_ref[:, lanes] = (dq * cqv).astype(BF16)
            dqab_ref[:, H * DA + hh * DA:H * DA + (hh + 1) * DA] = (dq * sqv).astype(BF16)
            dk = dk_ref[hh]
            dkpe = dkpe + dk
            dkv_ref[:, lanes] = (dk * keep_ref[...]).astype(BF16)
            dkv_ref[:, H * DA + hh * DA:H * DA + (hh + 1) * DA] = (dv_ref[hh] * keep_ref[...]).astype(BF16)
        dz3_ref[...] = (dkpe * ck_ref[...]).astype(BF16)
        dz15_ref[...] = (dkpe * sk_ref[...] + dft_ref[...]).astype(BF16)

        @pl.when(pl.program_id(0) == 0)
        def _():
            dwq_ref[...] = jnp.zeros_like(dwq_ref)
            dwkv_ref[...] = jnp.zeros_like(dwkv_ref)

        dwq_ref[...] += _dot(qn_ref[...], dqab_ref[...], TN)
        dwkv_ref[...] += _dot(kvn_ref[...], dkv_ref[...], TN)

    row = lambda i: (i, 0)
    fixed = lambda i: (0, 0)
    heads = pl.BlockSpec((H, tm, DA), lambda i: (0, i, 0))
    tab = pl.BlockSpec((tm, DA), row)
    wide = 2 * H * DA
    return pl.pallas_call(
        body, name=name, grid=(s // tm,),
        in_specs=[heads, heads, heads, tab, pl.BlockSpec((tm, Q_RANK), row), pl.BlockSpec((tm, KV_RANK), row),
                  tab, tab, tab, tab, pl.BlockSpec((1, DA), fixed)],
        out_specs=[pl.BlockSpec((tm, wide), row), pl.BlockSpec((tm, wide), row), tab, tab,
                   pl.BlockSpec((Q_RANK, wide), fixed), pl.BlockSpec((KV_RANK, wide), fixed)],
        out_shape=[jax.ShapeDtypeStruct((s, wide), BF16), jax.ShapeDtypeStruct((s, wide), BF16),
                   jax.ShapeDtypeStruct((s, DA), BF16), jax.ShapeDtypeStruct((s, DA), BF16),
                   jax.ShapeDtypeStruct((Q_RANK, wide), F32), jax.ShapeDtypeStruct((KV_RANK, wide), F32)],
        compiler_params=_cp(("arbitrary",)),
    )(dqa, dka, dva, dft, qn, kvn, cq, sq, ck, sk, keep)


def _fox_bwd_prep(dfqa, dfka, dfva, *, name, tm=512):
    s = dfqa.shape[1]
    tm = min(tm, s)
    place = lambda v: jnp.stack([_sel(DA, DA, [(r, r) for r in range(FOX_D)], v), _sel(DA, DA, [(r, FOX_D + r) for r in range(FOX_D)], v)])
    gq, gk = place(SCALE_FOX), place(1.0)

    def body(dq_ref, dk_ref, dv_ref, gq_ref, gk_ref, dz_ref):
        for part, (d_ref, g_ref) in enumerate(((dq_ref, gq_ref), (dk_ref, gk_ref), (dv_ref, gk_ref))):
            for p in range(H // 2):
                blk = _dot(d_ref[2 * p].astype(BF16), g_ref[0], NN) + _dot(d_ref[2 * p + 1].astype(BF16), g_ref[1], NN)
                lo = part * H * FOX_D + p * DA
                dz_ref[:, lo:lo + DA] = blk.astype(BF16)

    heads = pl.BlockSpec((H, tm, DA), lambda i: (0, i, 0))
    sel = pl.BlockSpec((2, DA, DA), lambda i: (0, 0, 0))
    return pl.pallas_call(
        body, name=name, grid=(s // tm,), in_specs=[heads, heads, heads, sel, sel],
        out_specs=pl.BlockSpec((tm, 3 * H * FOX_D), lambda i: (i, 0)),
        out_shape=jax.ShapeDtypeStruct((s, 3 * H * FOX_D), BF16), compiler_params=_cp(("parallel",)),
    )(dfqa, dfka, dfva, gq, gk)


def _lane_scan(x, s, reverse):
    lane = lax.broadcasted_iota(jnp.int32, x.shape, 1)
    sh = 1
    while sh < s:
        if reverse:
            x = x + jnp.where(lane < s - sh, pltpu.roll(x, s - sh, axis=1), 0.0)
        else:
            x = x + jnp.where(lane >= sh, pltpu.roll(x, sh, axis=1), 0.0)
        sh *= 2
    return x


def _gate_fwd(z, col_block, bias, *, name):
    s = z.shape[0]

    def body(z_ref, b_ref, f_ref, c_ref):
        ft = z_ref[...].T[0:8, :]
        f_ref[...] = ft
        xg = ft + b_ref[...]
        lf = jnp.minimum(xg, 0.0) - jnp.log(1.0 + jnp.exp(-jnp.abs(xg)))
        c = _lane_scan(lf, s, False)
        hi = c.astype(BF16).astype(F32)
        r = c - hi
        mid = r.astype(BF16).astype(F32)
        lo = r - mid
        c_ref[...] = jnp.concatenate([hi, mid, lo, jnp.zeros_like(hi)], axis=0).astype(BF16)

    return pl.pallas_call(
        body, name=name, grid=(1,),
        in_specs=[pl.BlockSpec((s, 128), lambda i: (0, col_block)), pl.BlockSpec((8, 1), lambda i: (0, 0))],
        out_specs=[pl.BlockSpec((8, s), lambda i: (0, 0)), pl.BlockSpec((32, s), lambda i: (0, 0))],
        out_shape=[jax.ShapeDtypeStruct((8, s), F32), jax.ShapeDtypeStruct((32, s), BF16)],
        compiler_params=_cp(("arbitrary",)))(z, bias)


def _gate_bwd(ft, bias, dc, *, name):
    s = ft.shape[1]

    def body(f_ref, b_ref, dc_ref, df_ref, db_ref):
        xg = f_ref[...] + b_ref[...]
        dlf = _lane_scan(dc_ref[...], s, True)
        df = dlf * _sigmoid(-xg)
        db_ref[...] = jnp.sum(df, axis=-1, keepdims=True)
        df_ref[...] = jnp.concatenate([df, jnp.zeros((DA - 8, s), F32)], axis=0).T

    return pl.pallas_call(body, name=name, out_shape=[jax.ShapeDtypeStruct((s, DA), F32), jax.ShapeDtypeStruct((8, 1), F32)],
                          compiler_params=_cp())(ft, bias, dc)


def _pool_lane_consts(tm, i):
    lane = lax.broadcasted_iota(jnp.int32, (tm, POOL_W), 1)
    tok = lax.broadcasted_iota(jnp.int32, (tm, POOL_W), 0) + i * tm
    win = jnp.where(lane < 64, 2, jnp.where(lane < 128, 4, jnp.where(lane < 192, 8, 16)))
    cnt = jnp.minimum(tok + 1, win).astype(F32)
    return lane, tok, cnt


def _pick_window(lane, s2, s4, s8, s16):
    return jnp.where(lane < 64, s2, jnp.where(lane < 128, s4, jnp.where(lane < 192, s8, s16)))


def _pool_fwd(z, col_block, bd, scale, *, name, tm=512):
    s = z.shape[0]
    tm = min(tm, s)
    hb = tm // POOL_HALO

    def body(u_ref, halo_ref, bd_ref, sc_ref, y_ref, p_ref, buf):
        i = pl.program_id(0)
        buf[0:POOL_HALO, :] = halo_ref[...] * (i > 0).astype(F32)
        buf[POOL_HALO:, :] = u_ref[...]

        def back(k):
            return buf[POOL_HALO - k:POOL_HALO - k + tm, :]

        u = u_ref[...]
        s2 = u + back(1)
        s4 = s2 + back(2) + back(3)
        s8 = s4 + back(4) + back(5) + back(6) + back(7)
        s16 = s8
        for k in range(8, 16):
            s16 = s16 + back(k)
        lane, _, cnt = _pool_lane_consts(tm, i)
        pooled = (_pick_window(lane, s2, s4, s8, s16) / cnt - u).astype(BF16)
        p_ref[...] = pooled
        y_ref[...] = _dot(pooled, bd_ref[...], NN) * sc_ref[...]

    return pl.pallas_call(
        body, name=name, grid=(s // tm,),
        in_specs=[pl.BlockSpec((tm, POOL_W), lambda i: (i, col_block)),
                  pl.BlockSpec((POOL_HALO, POOL_W), lambda i: (jnp.maximum(i * hb - 1, 0), col_block)),
                  pl.BlockSpec((POOL_W, POOL_W), lambda i: (0, 0)), pl.BlockSpec((1, POOL_W), lambda i: (0, 0))],
        out_specs=[pl.BlockSpec((tm, POOL_W), lambda i: (i, 0)), pl.BlockSpec((tm, POOL_W), lambda i: (i, 0))],
        out_shape=[jax.ShapeDtypeStruct((s, POOL_W), F32), jax.ShapeDtypeStruct((s, POOL_W), BF16)],
        scratch_shapes=[pltpu.VMEM((tm + POOL_HALO, POOL_W), F32)],
        compiler_params=_cp(("parallel",)),
    )(z, z, bd, scale.reshape(1, POOL_W))


def _pool_bwd_a(dy, pooled, bd, scale, *, name, tm=512):
    s = dy.shape[0]
    tm = min(tm, s)

    def body(dy_ref, p_ref, bd_ref, sc_ref, dq_ref, dbd_ref, dsc_ref):
        i = pl.program_id(0)
        dyv = dy_ref[...]
        pv = p_ref[...]
        y0 = _dot(pv, bd_ref[...], NN)
        dys = (dyv * sc_ref[...]).astype(BF16)
        dp = _dot(dys, bd_ref[...], NT)
        _, _, cnt = _pool_lane_consts(tm, i)
        dq_ref[:, 0:POOL_W] = dp / cnt
        dq_ref[:, POOL_W:] = dp

        @pl.when(i == 0)
        def _():
            dsc_ref[...] = jnp.zeros_like(dsc_ref)
            dbd_ref[...] = jnp.zeros_like(dbd_ref)

        dsc_ref[...] += jnp.sum(dyv * y0, axis=0, keepdims=True)
        dbd_ref[...] += _dot(pv, dys, TN)

    row = lambda i: (i, 0)
    fixed = lambda i: (0, 0)
    dq, dbd, dsc = pl.pallas_call(
        body, name=name, grid=(s // tm,),
        in_specs=[pl.BlockSpec((tm, POOL_W), row), pl.BlockSpec((tm, POOL_W), row),
                  pl.BlockSpec((POOL_W, POOL_W), fixed), pl.BlockSpec((1, POOL_W), fixed)],
        out_specs=[pl.BlockSpec((tm, 2 * POOL_W), row), pl.BlockSpec((POOL_W, POOL_W), fixed), pl.BlockSpec((1, POOL_W), fixed)],
        out_shape=[jax.ShapeDtypeStruct((s, 2 * POOL_W), F32), jax.ShapeDtypeStruct((POOL_W, POOL_W), F32),
                   jax.ShapeDtypeStruct((1, POOL_W), F32)],
        compiler_params=_cp(("arbitrary",)),
    )(dy, pooled, bd, scale.reshape(1, POOL_W))
    return dq, dbd, dsc.reshape(POOL_W)


def _pool_bwd_b(dq, *, name, tm=512):
    s = dq.shape[0]
    tm = min(tm, s)
    hb = tm // POOL_HALO
    nblk = s // tm

    def body(q_ref, dp_ref, halo_ref, du_ref, buf):
        i = pl.program_id(0)
        buf[0:tm, :] = q_ref[...]
        buf[tm:, :] = halo_ref[...] * (i < nblk - 1).astype(F32)

        def ahead(k):
            return buf[k:k + tm, :]

        q = q_ref[...]
        s2 = q + ahead(1)
        s4 = s2 + ahead(2) + ahead(3)
        s8 = s4 + ahead(4) + ahead(5) + ahead(6) + ahead(7)
        s16 = s8
        for k in range(8, 16):
            s16 = s16 + ahead(k)
        lane = lax.broadcasted_iota(jnp.int32, (tm, POOL_W), 1)
        du_ref[...] = _pick_window(lane, s2, s4, s8, s16) - dp_ref[...]

    return pl.pallas_call(
        body, name=name, grid=(nblk,),
        in_specs=[pl.BlockSpec((tm, POOL_W), lambda i: (i, 0)), pl.BlockSpec((tm, POOL_W), lambda i: (i, 1)),
                  pl.BlockSpec((POOL_HALO, POOL_W), lambda i: (jnp.minimum((i + 1) * hb, nblk * hb - 1), 0))],
        out_specs=pl.BlockSpec((tm, POOL_W), lambda i: (i, 0)),
        out_shape=jax.ShapeDtypeStruct((s, POOL_W), F32),
        scratch_shapes=[pltpu.VMEM((tm + POOL_HALO, POOL_W), F32)],
        compiler_params=_cp(("parallel",)),
    )(dq, dq, dq)


def _loss_head(x, gain, target, *, name, tm=512):
    s = x.shape[0]
    tm = min(tm, s)

    def body(x_ref, g_ref, t_ref, dx_ref, dg_ref, loss_ref):
        xv = x_ref[...]
        r = lax.rsqrt(jnp.mean(xv * xv, axis=-1, keepdims=True) + EPS)
        xh = xv * r
        err = xh * g_ref[...] - t_ref[...]
        dy = err * (1.0 / D)
        a = dy * g_ref[...]
        dx_ref[...] = r * a - xh * (r * jnp.mean(a * xh, axis=-1, keepdims=True))

        @pl.when(pl.program_id(0) == 0)
        def _():
            dg_ref[...] = jnp.zeros_like(dg_ref)
            loss_ref[...] = jnp.zeros_like(loss_ref)

        dg_ref[...] += jnp.sum(dy * xh, axis=0, keepdims=True)
        part = 0.5 * jnp.sum(jnp.mean(err * err, axis=-1, keepdims=True), axis=0, keepdims=True)
        loss_ref[...] += jnp.broadcast_to(part, loss_ref.shape)

    row = lambda i: (i, 0)
    dx, dg, loss = pl.pallas_call(
        body, name=name, grid=(s // tm,),
        in_specs=[pl.BlockSpec((tm, D), row), pl.BlockSpec((1, D), lambda i: (0, 0)), pl.BlockSpec((tm, D), row)],
        out_specs=[pl.BlockSpec((tm, D), row), pl.BlockSpec((1, D), lambda i: (0, 0)), pl.BlockSpec((1, 128), lambda i: (0, 0))],
        out_shape=[jax.ShapeDtypeStruct((s, D), F32), jax.ShapeDtypeStruct((1, D), F32), jax.ShapeDtypeStruct((1, 128), F32)],
        compiler_params=_cp(("arbitrary",)),
    )(x, gain.reshape(1, D), target)
    return dx, dg.reshape(D), loss[0, 0]


def _adamw(w, g, m, v, *, name, tr=512):
    rows, cols = w.shape
    tr = min(tr, rows)
    assert rows % tr == 0, (name, rows, tr)
    c_m = 1.0 - ADAM_B1
    c_v = 1.0 - ADAM_B2
    bc1 = 1.0 - ADAM_B1 ** ADAM_STEP
    bc2 = 1.0 - ADAM_B2 ** ADAM_STEP

    def body(w_ref, g_ref, m_ref, v_ref, d_ref, mo_ref, vo_ref):
        gv = g_ref[...]
        mn = ADAM_B1 * m_ref[...] + c_m * gv
        vn = ADAM_B2 * v_ref[...] + c_v * (gv * gv)
        mo_ref[...] = mn
        vo_ref[...] = vn
        d_ref[...] = -ADAM_LR * ((mn / bc1) / (jnp.sqrt(vn / bc2) + ADAM_EPS) + ADAM_WD * w_ref[...])

    spec = pl.BlockSpec((tr, cols), lambda i: (i, 0))
    return pl.pallas_call(body, name=name, grid=(rows // tr,), in_specs=[spec] * 4, out_specs=[spec] * 3,
                          out_shape=[jax.ShapeDtypeStruct((rows, cols), F32)] * 3,
                          compiler_params=_cp(("parallel",)))(w, g, m, v)


def _adamw_layer(w, g, m, v, layer, prev, *, name, tr):
    rows, cols = g.shape
    assert rows % tr == 0 and w.shape == (DEPTH * rows, cols), (name, w.shape, g.shape, tr)
    nblk = rows // tr
    c_m = 1.0 - ADAM_B1
    c_v = 1.0 - ADAM_B2
    bc1 = 1.0 - ADAM_B1 ** ADAM_STEP
    bc2 = 1.0 - ADAM_B2 ** ADAM_STEP
    n_prev = 0 if prev is None else 4

    def body(*refs):
        w_ref, g_ref, m_ref, v_ref = refs[:4]
        d_ref, mo_ref, vo_ref, go_ref = refs[4 + n_prev:]
        gv = g_ref[...]
        mn = ADAM_B1 * m_ref[...] + c_m * gv
        vn = ADAM_B2 * v_ref[...] + c_v * (gv * gv)
        mo_ref[...] = mn
        vo_ref[...] = vn
        go_ref[...] = gv
        d_ref[...] = -ADAM_LR * ((mn / bc1) / (jnp.sqrt(vn / bc2) + ADAM_EPS) + ADAM_WD * w_ref[...])

    stacked = pl.BlockSpec((tr, cols), lambda i: (layer * nblk + i, 0))
    args = [w, g, m, v] + ([] if prev is None else list(prev))
    return pl.pallas_call(
        body, name=name, grid=(nblk,),
        in_specs=[stacked, pl.BlockSpec((tr, cols), lambda i: (i, 0)), stacked, stacked] + [ANY_SPEC] * n_prev,
        out_specs=[stacked] * 4, out_shape=[jax.ShapeDtypeStruct(w.shape, F32)] * 4,
        input_output_aliases={4 + k: k for k in range(n_prev)},
        compiler_params=_cp(("parallel",)))(*args)


def _position():
    return jnp.stack([lax.axis_index("c"), 2 * lax.axis_index("x") + lax.axis_index("y")]).astype(jnp.int32)


SUM_ROW_TILES = 2


def _sum2_bf16(pos, fulls, sibs, *, name):
    n = len(fulls)
    nb = SUM_ROW_TILES

    def body(pos_ref, *refs):
        for t in range(n):
            refs[2 * n + t][...] = (refs[t][...] + refs[n + t][...]).astype(BF16)

    in_specs, sib_specs = [], []
    for sb in sibs:
        _, half, cols = sb.shape
        tr = half // nb
        assert half % nb == 0 and tr % 16 == 0, sb.shape
        in_specs.append(pl.BlockSpec((None, tr, cols), lambda j, i, p: (j, p[0] * nb + i, 0)))
        sib_specs.append(pl.BlockSpec((None, tr, cols), lambda j, i, p: (j, i, 0)))
    return pl.pallas_call(
        body, name=name,
        grid_spec=pltpu.PrefetchScalarGridSpec(num_scalar_prefetch=1, grid=(N_CHIPS, nb), in_specs=in_specs + sib_specs,
                                               out_specs=sib_specs),
        out_shape=[jax.ShapeDtypeStruct(sb.shape, BF16) for sb in sibs],
        compiler_params=_cp(("parallel", "parallel")))(pos, *fulls, *sibs)


def _sum5(pos, fulls, sibs, recvs, *, name):
    n = len(fulls)
    nb = SUM_ROW_TILES

    def body(pos_ref, *refs):
        for t in range(n):
            acc = refs[t][...] + refs[n + t][...]
            for kk in range(3):
                acc = acc + refs[2 * n + t][kk].astype(F32)
            refs[3 * n + t][...] = acc

    f_specs, s_specs, r_specs, o_specs = [], [], [], []
    for f in fulls:
        _, rows, cols = f.shape
        tr = rows // 2 // nb
        f_specs.append(pl.BlockSpec((None, tr, cols), lambda i, p: (p[1], p[0] * nb + i, 0)))
        s_specs.append(pl.BlockSpec((None, tr, cols), lambda i, p: (p[1], i, 0)))
        r_specs.append(pl.BlockSpec((3, tr, cols), lambda i, p: (0, i, 0)))
        o_specs.append(pl.BlockSpec((tr, cols), lambda i, p: (p[0] * nb + i, 0)))
    return pl.pallas_call(
        body, name=name,
        grid_spec=pltpu.PrefetchScalarGridSpec(num_scalar_prefetch=1, grid=(nb,), in_specs=f_specs + s_specs + r_specs,
                                               out_specs=o_specs),
        out_shape=[jax.ShapeDtypeStruct(f.shape[1:], F32) for f in fulls],
        compiler_params=_cp(("parallel",)))(pos, *fulls, *sibs, *recvs)


def _place():
    x, y, c = lax.axis_index("x"), lax.axis_index("y"), lax.axis_index("c")
    chips = [(1 - x, y), (x, 1 - y), (1 - x, 1 - y)]
    return x, y, c, 2 * x + y, chips


SEM_SPEC = pl.BlockSpec(memory_space=pltpu.SEMAPHORE)
ANY_SPEC = pl.BlockSpec(memory_space=pl.ANY)


def _gather_copies(ins, outs, send_i, recv_i, send_o, recv_o):
    x, y, c, me, chips = _place()
    n = len(ins)
    started, awaited = [], []
    for t in range(n):
        half = ins[t].shape[0] // 2
        mine = pl.ds(c * half, half)
        started.append(pltpu.make_async_remote_copy(
            src_ref=ins[t], dst_ref=outs[t].at[me], send_sem=send_o.at[t], recv_sem=recv_o.at[t],
            device_id=(x, y, 1 - c), device_id_type=MESH))
        awaited.append(started[-1])
        for kk, (px, py) in enumerate(chips):
            started.append(pltpu.make_async_remote_copy(
                src_ref=ins[t].at[mine], dst_ref=outs[t].at[me, mine], send_sem=send_i.at[t * 3 + kk],
                recv_sem=recv_i.at[t * 3 + kk], device_id=(px, py, c), device_id_type=MESH))
            awaited.append(pltpu.make_async_remote_copy(
                src_ref=ins[t].at[mine], dst_ref=outs[t].at[2 * px + py, mine], send_sem=send_i.at[t * 3 + kk],
                recv_sem=recv_i.at[t * 3 + kk], device_id=(px, py, c), device_id_type=MESH))
    return started, awaited


def _forward_copies(outs, send_d, recv_d):
    x, y, c, me, chips = _place()
    started, awaited = [], []
    for t in range(len(outs)):
        half = outs[t].shape[1] // 2
        for kk, (px, py) in enumerate(chips):
            for lst, hc in ((started, c), (awaited, 1 - c)):
                blk = outs[t].at[2 * px + py, pl.ds(hc * half, half)]
                lst.append(pltpu.make_async_remote_copy(src_ref=blk, dst_ref=blk, send_sem=send_d.at[t * 3 + kk],
                                                        recv_sem=recv_d.at[t * 3 + kk], device_id=(x, y, 1 - c), device_id_type=MESH))
    return started, awaited


def _gather_blocking(shards):
    n = len(shards)

    def body(*refs):
        ins, outs = refs[:n], refs[n:2 * n]
        send_i, recv_i, send_d, recv_d, send_o, recv_o = refs[2 * n:]
        started, awaited = _gather_copies(ins, outs, send_i, recv_i, send_o, recv_o)
        for cp in started:
            cp.start()
        for cp in awaited:
            cp.wait_recv()
        fwd, fwd_in = _forward_copies(outs, send_d, recv_d)
        for cp in fwd:
            cp.start()
        for cp in fwd_in:
            cp.wait_recv()
        for cp in started + fwd:
            cp.wait_send()

    return pl.pallas_call(
        body, name="gather_first", in_specs=[HBM_SPEC] * n, out_specs=[HBM_SPEC] * n,
        out_shape=[jax.ShapeDtypeStruct((N_CHIPS,) + s.shape, s.dtype) for s in shards],
        scratch_shapes=[pltpu.SemaphoreType.DMA((3 * n,)), pltpu.SemaphoreType.DMA((3 * n,)),
                        pltpu.SemaphoreType.DMA((3 * n,)), pltpu.SemaphoreType.DMA((3 * n,)),
                        pltpu.SemaphoreType.DMA((n,)), pltpu.SemaphoreType.DMA((n,))],
    )(*shards)


def _gather_start(shards, after, tag):
    n = len(shards)

    def body(*refs):
        ins = refs[:n]
        send_i, recv_i, send_o, recv_o = refs[2 * n + 1:2 * n + 5]
        outs = refs[3 * n + 5:4 * n + 5]
        token = refs[4 * n + 5]
        started, _ = _gather_copies(ins, outs, send_i, recv_i, send_o, recv_o)
        for cp in started:
            cp.start()
        token[...] = jnp.zeros_like(token)

    lands = [lax.empty((N_CHIPS,) + s.shape, s.dtype) for s in shards]
    sems = [pltpu.SemaphoreType.DMA((3 * n,)), pltpu.SemaphoreType.DMA((3 * n,)), pltpu.SemaphoreType.DMA((n,)), pltpu.SemaphoreType.DMA((n,))]
    res = pl.pallas_call(
        body, name=f"gather_{tag}_start",
        in_specs=[HBM_SPEC] * (2 * n) + [ANY_SPEC],
        out_specs=[SEM_SPEC] * 4 + [HBM_SPEC] * (2 * n) + [pl.BlockSpec(memory_space=pltpu.VMEM)],
        out_shape=sems + [jax.ShapeDtypeStruct(s.shape, s.dtype) for s in shards]
        + [jax.ShapeDtypeStruct(a.shape, a.dtype) for a in lands] + [jax.ShapeDtypeStruct((8, 128), F32)],
        input_output_aliases={t: 4 + t for t in range(2 * n)},
        compiler_params=pltpu.CompilerParams(has_side_effects=pltpu.SideEffectType.DATAFLOW_SIDE_EFFECTING),
    )(*[pltpu.with_memory_space_constraint(s, pltpu.HBM) for s in shards],
      *[pltpu.with_memory_space_constraint(a, pltpu.HBM) for a in lands], after)
    return res[:4], res[4:4 + n], res[4 + n:4 + 2 * n], res[-1]


def _gather_wait(sems, shards_thru, lands_thru, after, tag):
    n = len(shards_thru)

    def body(*refs):
        ins, outs_in = refs[:n], refs[n:2 * n]
        send_i, recv_i, send_o, recv_o = refs[2 * n:2 * n + 4]
        started, awaited = _gather_copies(ins, outs_in, send_i, recv_i, send_o, recv_o)
        for cp in started:
            cp.wait_send()
        for cp in awaited:
            cp.wait_recv()

    res = pl.pallas_call(
        body, name=f"gather_{tag}_wait",
        in_specs=[HBM_SPEC] * (2 * n) + [SEM_SPEC] * 4 + [ANY_SPEC],
        out_specs=[HBM_SPEC] * (2 * n),
        out_shape=[jax.ShapeDtypeStruct(a.shape, a.dtype) for a in list(shards_thru) + list(lands_thru)],
        input_output_aliases={t: t for t in range(2 * n)},
        compiler_params=pltpu.CompilerParams(has_side_effects=pltpu.SideEffectType.DATAFLOW_SIDE_EFFECTING),
    )(*shards_thru, *lands_thru, *sems, after)
    return res[n:]


def _gather_forward(lands, tag):
    n = len(lands)

    def body(*refs):
        outs = refs[n:2 * n]
        send_d, recv_d = refs[2 * n:]
        fwd, fwd_in = _forward_copies(outs, send_d, recv_d)
        for cp in fwd:
            cp.start()
        for cp in fwd_in:
            cp.wait_recv()
        for cp in fwd:
            cp.wait_send()

    return pl.pallas_call(
        body, name=f"gather_{tag}_forward", in_specs=[HBM_SPEC] * n, out_specs=[HBM_SPEC] * n,
        out_shape=[jax.ShapeDtypeStruct(a.shape, a.dtype) for a in lands],
        input_output_aliases={t: t for t in range(n)},
        scratch_shapes=[pltpu.SemaphoreType.DMA((3 * n,)), pltpu.SemaphoreType.DMA((3 * n,))],
    )(*lands)


def _stage1_copies(ins, sib, send, recv):
    x, y, c, me, chips = _place()
    cps = []
    for t in range(len(ins)):
        rows = ins[t].shape[1] // 2
        cps.append(pltpu.make_async_remote_copy(
            src_ref=ins[t].at[:, pl.ds((1 - c) * rows, rows), :], dst_ref=sib[t], send_sem=send.at[t],
            recv_sem=recv.at[t], device_id=(x, y, 1 - c), device_id_type=MESH))
    return cps


def _split_start(copies_fn, srcs, land_shapes, n_sems, tag):
    n = len(srcs)

    def body(*refs):
        send, recv = refs[2 * n:2 * n + 2]
        for cp in copies_fn(refs[:n], refs[3 * n + 2:4 * n + 2], send, recv):
            cp.start()
        refs[4 * n + 2][...] = jnp.zeros_like(refs[4 * n + 2])

    lands = [lax.empty(shp, dt) for shp, dt in land_shapes]
    res = pl.pallas_call(
        body, name=tag,
        in_specs=[HBM_SPEC] * (2 * n),
        out_specs=[SEM_SPEC] * 2 + [HBM_SPEC] * (2 * n) + [pl.BlockSpec(memory_space=pltpu.VMEM)],
        out_shape=[pltpu.SemaphoreType.DMA((n_sems,)), pltpu.SemaphoreType.DMA((n_sems,))]
        + [jax.ShapeDtypeStruct(p.shape, p.dtype) for p in srcs]
        + [jax.ShapeDtypeStruct(a.shape, a.dtype) for a in lands] + [jax.ShapeDtypeStruct((8, 128), F32)],
        input_output_aliases={t: 2 + t for t in range(2 * n)},
        compiler_params=pltpu.CompilerParams(has_side_effects=pltpu.SideEffectType.DATAFLOW_SIDE_EFFECTING),
    )(*[pltpu.with_memory_space_constraint(p, pltpu.HBM) for p in srcs],
      *[pltpu.with_memory_space_constraint(a, pltpu.HBM) for a in lands])
    return res[:2], res[2:2 + n], res[2 + n:2 + 2 * n], res[-1]


def _split_wait(copies_fn, sems, srcs_thru, lands_thru, after, tag):
    n = len(srcs_thru)

    def body(*refs):
        for cp in copies_fn(refs[:n], refs[n:2 * n], refs[2 * n], refs[2 * n + 1]):
            cp.wait()

    res = pl.pallas_call(
        body, name=tag,
        in_specs=[HBM_SPEC] * (2 * n) + [SEM_SPEC] * 2 + [ANY_SPEC],
        out_specs=[HBM_SPEC] * (2 * n),
        out_shape=[jax.ShapeDtypeStruct(a.shape, a.dtype) for a in list(srcs_thru) + list(lands_thru)],
        input_output_aliases={t: t for t in range(2 * n)},
        compiler_params=pltpu.CompilerParams(has_side_effects=pltpu.SideEffectType.DATAFLOW_SIDE_EFFECTING),
    )(*srcs_thru, *lands_thru, *sems, after)
    return res[:n], res[n:]


def _stage2_copies(ps, rcv, send, recv):
    x, y, c, me, chips = _place()
    return [pltpu.make_async_remote_copy(
        src_ref=ps[t].at[2 * px + py], dst_ref=rcv[t].at[kk], send_sem=send.at[t * 3 + kk],
        recv_sem=recv.at[t * 3 + kk], device_id=(px, py, c), device_id_type=MESH)
        for t in range(len(ps)) for kk, (px, py) in enumerate(chips)]


def _reduce_stage3(reduced, tag):
    n = len(reduced)

    def body(*refs):
        outs = refs[n:2 * n]
        send, recv = refs[2 * n:]
        x, y, c, me, chips = _place()
        cps = []
        for t in range(n):
            rows = outs[t].shape[0] // 2
            mine = outs[t].at[pl.ds(c * rows, rows), :]
            cp = pltpu.make_async_remote_copy(src_ref=mine, dst_ref=mine, send_sem=send.at[t], recv_sem=recv.at[t],
                                              device_id=(x, y, 1 - c), device_id_type=MESH)
            cp.start()
            cps.append(cp)
        for cp in cps:
            cp.wait()

    return pl.pallas_call(
        body, name="reduce_stage3_" + tag, in_specs=[HBM_SPEC] * n, out_specs=[HBM_SPEC] * n,
        out_shape=[jax.ShapeDtypeStruct(r.shape, r.dtype) for r in reduced],
        input_output_aliases={t: t for t in range(n)},
        scratch_shapes=[pltpu.SemaphoreType.DMA((n,)), pltpu.SemaphoreType.DMA((n,))],
    )(*reduced)


def _allreduce_small(v):
    rows, cols = v.shape

    def body(v_ref, o_ref, buf, send, recv, loc):
        x, y, c, me, chips = _place()
        mine = 4 * x + 2 * y + c
        lc = pltpu.make_async_copy(v_ref, buf.at[mine], loc)
        lc.start()
        peers = []
        for fx in range(2):
            for fy in range(2):
                for fc in range(2):
                    if fx or fy or fc:
                        peers.append((fx, fy, fc))
        cps = []
        for kk, (fx, fy, fc) in enumerate(peers):
            to = (x ^ fx, y ^ fy, c ^ fc)
            cp = pltpu.make_async_remote_copy(src_ref=v_ref, dst_ref=buf.at[mine], send_sem=send.at[kk], recv_sem=recv.at[kk],
                                              device_id=to, device_id_type=MESH)
            cp.start()
            cps.append((cp, to))
        for kk, (cp, to) in enumerate(cps):
            src = 4 * to[0] + 2 * to[1] + to[2]
            pltpu.make_async_remote_copy(src_ref=v_ref, dst_ref=buf.at[src], send_sem=send.at[kk], recv_sem=recv.at[kk],
                                         device_id=to, device_id_type=MESH).wait_recv()
        for cp, _ in cps:
            cp.wait_send()
        lc.wait()
        acc = buf[0]
        for d in range(1, 8):
            acc = acc + buf[d]
        o_ref[...] = acc

    return pl.pallas_call(
        body, name="allreduce_small", in_specs=[pl.BlockSpec(memory_space=pltpu.VMEM)],
        out_specs=pl.BlockSpec(memory_space=pltpu.VMEM), out_shape=jax.ShapeDtypeStruct((rows, cols), F32),
        scratch_shapes=[pltpu.VMEM((8, rows, cols), F32), pltpu.SemaphoreType.DMA((7,)), pltpu.SemaphoreType.DMA((7,)),
                        pltpu.SemaphoreType.DMA],
        compiler_params=pltpu.CompilerParams(vmem_limit_bytes=VMEM_LIMIT_V7X),
    )(v)


def _pad_w_in(w):
    z = lambda n: jnp.zeros(w.shape[:-1] + (n,), w.dtype)
    return jnp.concatenate([w[..., 0:384], z(64), w[..., 384:416], z(32), w[..., 416:1824],
                            w[..., 1824:1830], z(58), w[..., 400:416], w[..., 384:400], z(32)], axis=-1)


def _unpad_w_in(g):
    x1 = g[..., 448:464] + g[..., Z_F + 80:Z_F + 96]
    x2 = g[..., 464:480] + g[..., Z_F + 64:Z_F + 80]
    return jnp.concatenate([g[..., 0:384], x1, x2, g[..., 512:1920], g[..., 1920:1926]], axis=-1)


def _block_diag(pw):
    out = jnp.zeros((POOL_W, POOL_W), pw.dtype)
    for g in range(4):
        out = out.at[g * 64:(g + 1) * 64, g * 64:(g + 1) * 64].set(pw[g])
    return out


def _rope_tables(s):
    inv_freq = ROPE_THETA ** (-jnp.arange(0, ROPE, 2, dtype=F32) / ROPE)
    ang = jnp.arange(s, dtype=jnp.int32).astype(F32)[:, None] * inv_freq[None, :]
    cos, sin = jnp.cos(ang), jnp.sin(ang)
    zero = lambda n: jnp.zeros((s, n), F32)
    ck = jnp.concatenate([zero(NOPE), cos, cos, zero(DA - NOPE - ROPE)], axis=1)
    sk = jnp.concatenate([zero(NOPE), -sin, sin, zero(DA - NOPE - ROPE)], axis=1)
    cq = jnp.concatenate([jnp.ones((s, NOPE), F32), cos, cos, zero(DA - NOPE - ROPE)], axis=1) * SCALE_MLA
    return dict(cq=cq, sq=sk * SCALE_MLA, ck=ck, sk=sk)


def _mix_fwd(l, x1, wts, sm, tabs):
    z, h2, qn, kvn, qa, ka, va = _mix_in(x1, (sm["mix_norm"][l], sm["q_a_norm"][l], sm["kv_a_norm"][l]), wts["w_in"][l],
                                         wts["wq_a"][l], wts["wq_b"][l], wts["wk"][l], wts["wv"][l], tabs, name=f"mix_in_{l}")
    oa, lse_a = _attn_fwd(qa, ka, va, VDIM, name=f"mla_attn_{l}")

    bd = _block_diag(wts["pool_w"][l]).astype(BF16)
    yb, pooled = _pool_fwd(z, Z_POOL // POOL_W, bd, sm["pool_scale"][l], name=f"pool_{l}")

    fb = jnp.pad(sm["fox_b_f"][l], (0, 8 - H)).reshape(8, 1)
    ft, c3t = _gate_fwd(z, Z_F // DA, fb, name=f"fox_gate_{l}")
    fqa, fka, fva = _fox_prep(z, c3t, name=f"fox_prep_{l}")
    oc, lse_c = _attn_fwd(fqa, fka, fva, FOX_D, name=f"fox_attn_{l}")

    x2, cat = _mix_out(oa, yb, oc, wts["w_out"][l], x1, name=f"mix_out_{l}")
    saved = dict(z=z, h2=h2, qn=qn, kvn=kvn, qa=qa, ka=ka, va=va, oa=oa, lse_a=lse_a, bd=bd, pooled=pooled,
                 fqa=fqa, fka=fka, fva=fva, ft=ft, fb=fb, oc=oc, lse_c=lse_c, cat=cat)
    return x2, saved


def _mix_bwd(l, x1, dx2, sv, wts, sm, tabs, tok=None):
    s = x1.shape[0]
    g = {}
    dx2b = (dx2 if tok is None else dx2 + tok).astype(BF16)
    g["w_out"] = _mm(sv["cat"], dx2b, "tn", name=f"d_w_out_{l}", tm=1024, tn=1024, tk=DW_TOKENS)
    doa, doc, dyb, dl_a, dl_c = _mix_out_bwd(dx2b, wts["w_out"][l], sv["oa"], sv["oc"], name=f"mix_out_bwd_{l}")

    dfqa, dfka, dfva, dcq, dck = _attn_bwd(sv["fqa"], sv["fka"], sv["fva"], doc, sv["lse_c"], dl_c, True, name=f"fox_attn_bwd_{l}")
    dfox = _fox_bwd_prep(dfqa, dfka, dfva, name=f"fox_bwd_prep_{l}")
    dc = jnp.pad(dcq.reshape(H, s) + dck.reshape(H, s), ((0, 8 - H), (0, 0)))
    dft, dfb = _gate_bwd(sv["ft"], sv["fb"], dc, name=f"fox_gate_bwd_{l}")
    g["fox_b_f"] = dfb[:H, 0]

    dq, dbd, g["pool_scale"] = _pool_bwd_a(dyb, sv["pooled"], sv["bd"], sm["pool_scale"][l], name=f"pool_bwd_a_{l}")
    du = _pool_bwd_b(dq, name=f"pool_bwd_b_{l}")
    g["pool_w"] = jnp.stack([dbd[i * 64:(i + 1) * 64, i * 64:(i + 1) * 64] for i in range(4)])

    dqa_, dka_, dva_ = _attn_bwd(sv["qa"], sv["ka"], sv["va"], doa, sv["lse_a"], dl_a, False, name=f"mla_attn_bwd_{l}")
    dqab, dkv, dz3, dz15, dwq, dwkv = _mla_bwd_prep(dqa_, dka_, dva_, dft, sv["qn"], sv["kvn"], tabs["cq"], tabs["sq"],
                                                    tabs["ck"], tabs["sk"], name=f"mla_bwd_prep_{l}")
    wq_ab = jnp.concatenate([wts["wq_a"][l], wts["wq_b"][l]], axis=1)
    wkv = jnp.concatenate([wts["wk"][l], wts["wv"][l]], axis=1)
    dwq = dwq.reshape(Q_RANK, 2, H, DA)
    dwkv = dwkv.reshape(KV_RANK, 2, H, DA)
    da, db = dwq[:, 0], dwq[:, 1]
    swapped = jnp.concatenate([jnp.zeros((Q_RANK, H, NOPE), F32), db[..., NOPE + HALF_ROPE:NOPE + ROPE],
                               db[..., NOPE:NOPE + HALF_ROPE]], axis=-1)
    g["w_q_b"] = (da[..., :NOPE + ROPE] + swapped).reshape(Q_RANK, H * (NOPE + ROPE))
    g["w_kv_b"] = jnp.concatenate([dwkv[:, 0, :, :NOPE], dwkv[:, 1, :, :VDIM]], axis=-1).reshape(KV_RANK, H * (NOPE + VDIM))
    dqa, g["q_a_norm"] = _rmsnorm_bwd(sv["z"], Z_QA // Q_RANK, sm["q_a_norm"][l], dqab, wq_ab, name=f"q_a_norm_bwd_{l}")
    dkva, g["kv_a_norm"] = _rmsnorm_bwd(sv["z"], Z_KVA // KV_RANK, sm["kv_a_norm"][l], dkv, wkv, name=f"kv_a_norm_bwd_{l}")

    dz = jnp.concatenate([dqa.astype(BF16), dkva.astype(BF16), dz3, du.astype(BF16), dfox, dz15], axis=1)
    g["w_in"] = _mm(sv["h2"], dz, "tn", name=f"d_w_in_{l}", tm=1024, tn=1024, tk=DW_TOKENS)
    dx1, g["mix_norm"] = _rmsnorm_bwd(x1, 0, sm["mix_norm"][l], dz, wts["w_in"][l], dx2, name=f"mix_norm_bwd_{l}")
    return dx1, g


DW_TOKENS = 2048


def _local_step(x, target, wts, sm, late_weights=None, grads_ready=None):
    s = x.shape[0]
    tabs = _rope_tables(s)
    acts = []
    xs = x
    for l in range(DEPTH):
        x1, gu1, act1 = _ffn_fwd(xs, sm["ffn1_norm"][l], wts["ffn1_w_gu"][l], wts["ffn1_w_d2"][l], name=f"ffn1_fwd_{l}")
        if l == 0 and late_weights is not None:
            sm = late_weights("ffn1", x1, sm)
        x2, sv = _mix_fwd(l, x1, wts, sm, tabs)
        if l == 0 and late_weights is not None:
            sm = late_weights("mix", x2, sm)
        x3, gu2, act2 = _ffn_fwd(x2, sm["ffn2_norm"][l], wts["ffn2_w_gu"][l], wts["ffn2_w_d2"][l], name=f"ffn2_fwd_{l}")
        acts.append((xs, gu1, act1, x1, sv, x2, gu2, act2))
        xs = x3
    dx, g_final, loss = _loss_head(xs, sm["final_norm"], target, name="loss_head")
    grads = [dict() for _ in range(DEPTH)]
    for l in reversed(range(DEPTH)):
        x0, gu1, act1, x1, sv, x2, gu2, act2 = acts[l]
        g = grads[l]
        dx, dgu, hh, dy, g["ffn2_norm"] = _ffn_bwd(x2, dx, gu2, sm["ffn2_norm"][l], wts["ffn2_w_gu"][l], wts["ffn2_w_d2"][l],
                                                   name=f"ffn2_bwd_{l}")
        g["ffn2_w_down"] = _mm(act2, dy, "tn", name=f"d_ffn2_w_down_{l}", tm=FF_SHARD, tn=1024, tk=DW_TOKENS)
        g["ffn2_w_gu"] = _mm(hh, dgu, "tn", name=f"d_ffn2_w_gu_{l}", tm=1024, tn=FF_SHARD, tk=DW_TOKENS, n_major_out=True)
        tok = None
        if grads_ready is not None:
            sm, tok = grads_ready(l, "ffn2", g, sm)
        dx, gm = _mix_bwd(l, x1, dx, sv, wts, sm, tabs, tok)
        g.update(gm)
        if grads_ready is not None:
            sm, _ = grads_ready(l, "mix", g, sm)
        dx, dgu, hh, dy, g["ffn1_norm"] = _ffn_bwd(x0, dx, gu1, sm["ffn1_norm"][l], wts["ffn1_w_gu"][l], wts["ffn1_w_d2"][l],
                                                   name=f"ffn1_bwd_{l}")
        if grads_ready is not None:
            sm, tok = grads_ready(l, "ffn1_tokens", {"dx": dx}, sm)
            if tok is not None:
                dy = dy + tok.astype(BF16)
        g["ffn1_w_down"] = _mm(act1, dy, "tn", name=f"d_ffn1_w_down_{l}", tm=FF_SHARD, tn=1024, tk=DW_TOKENS)
        g["ffn1_w_gu"] = _mm(hh, dgu, "tn", name=f"d_ffn1_w_gu_{l}", tm=1024, tn=FF_SHARD, tk=DW_TOKENS, n_major_out=True)
        if grads_ready is not None:
            sm, _ = grads_ready(l, "ffn1", g, sm)
    return loss, dx, grads, g_final


BIG = ["ffn1_w_gu", "ffn1_w_down", "w_in", "w_q_b", "w_kv_b", "w_out", "ffn2_w_gu", "ffn2_w_down"]
SMALL = ["ffn1_norm", "mix_norm", "q_a_norm", "kv_a_norm", "pool_w", "pool_scale", "fox_b_f", "ffn2_norm"]
SMALL_ROWS = 48


WEIGHT_VIEWS = ["ffn1_w_gu", "ffn1_w_d2", "w_in", "wq_a", "wq_b", "wk", "wv", "w_out", "ffn2_w_gu", "ffn2_w_d2"]


def _prepare_weights(gathered, wts):
    for (nm, l), w in gathered.items():
        if nm in ("ffn1_w_gu", "ffn2_w_gu"):
            wts[nm][l] = w
        elif nm in ("ffn1_w_down", "ffn2_w_down"):
            wts[nm[:5] + "w_d2"][l] = w.reshape(2, FF_SHARD, D)
        elif nm in ("w_in", "w_out"):
            wts[nm][l] = w.reshape(D, -1)
        elif nm == "w_q_b":
            wq = jnp.moveaxis(w, 0, 1).reshape(Q_RANK, H, NOPE + ROPE)
            zq = lambda n: jnp.zeros((Q_RANK, H, n), BF16)
            wts["wq_a"][l] = jnp.concatenate([wq, zq(DA - NOPE - ROPE)], axis=-1).reshape(Q_RANK, H * DA)
            wts["wq_b"][l] = jnp.concatenate([zq(NOPE), wq[..., NOPE + HALF_ROPE:], wq[..., NOPE:NOPE + HALF_ROPE],
                                              zq(DA - NOPE - ROPE)], axis=-1).reshape(Q_RANK, H * DA)
        else:
            wkv = jnp.moveaxis(w, 0, 1).reshape(KV_RANK, H, NOPE + VDIM)
            zk = jnp.zeros((KV_RANK, H, DA - NOPE), BF16)
            wts["wk"][l] = jnp.concatenate([wkv[..., :NOPE], zk], axis=-1).reshape(KV_RANK, H * DA)
            wts["wv"][l] = jnp.concatenate([wkv[..., NOPE:], zk], axis=-1).reshape(KV_RANK, H * DA)


def _chip_major(name, g):
    if name in ("ffn1_w_gu", "ffn2_w_gu"):
        return g
    if name in ("ffn1_w_down", "ffn2_w_down", "w_in", "w_out"):
        return g.reshape(N_CHIPS, g.shape[0] // N_CHIPS, g.shape[1])
    return jnp.moveaxis(g.reshape(g.shape[0], N_CHIPS, g.shape[1] // N_CHIPS), 1, 0)


def _pack_small(grads, g_final, loss):
    parts = []
    for l in range(DEPTH):
        for nm in SMALL:
            parts.append(grads[l][nm].reshape(-1))
    parts.append(g_final.reshape(-1))
    parts.append(loss.reshape(1))
    flat = jnp.concatenate(parts)
    return jnp.pad(flat, (0, SMALL_ROWS * D - flat.shape[0])).reshape(SMALL_ROWS, D)


def _unpack_small(packed, params):
    flat = packed.reshape(-1)
    out = {nm: [] for nm in SMALL}
    off = 0
    for l in range(DEPTH):
        for nm in SMALL:
            shp = params[nm].shape[1:]
            n = int(np.prod(shp))
            out[nm].append(flat[off:off + n].reshape(shp))
            off += n
    res = {nm: jnp.stack(v) for nm, v in out.items()}
    res["final_norm"] = flat[off:off + D]
    return res, flat[off + D]


def _update(name, w, g, m, v):
    shp = w.shape
    if w.ndim == 1:
        view = (1, shp[0])
    elif w.size <= 65536:
        view = (shp[0], w.size // shp[0])
    else:
        view = (w.size // shp[-1], shp[-1])
    tr = view[0]
    for cand in (512, 352, 256, 128):
        if view[0] % cand == 0 and view[0] > cand:
            tr = cand
            break
    d, mn, vn = _adamw(w.reshape(view), g.reshape(view), m.reshape(view), v.reshape(view), name="adamw_" + name, tr=tr)
    return d.reshape(shp), mn.reshape(shp), vn.reshape(shp)


WEIGHTS = ['ffn1_norm', 'ffn1_w_gu', 'ffn1_w_down', 'mix_norm', 'w_in', 'q_a_norm', 'w_q_b', 'kv_a_norm', 'w_kv_b', 'pool_w',
           'pool_scale', 'fox_b_f', 'w_out', 'ffn2_norm', 'ffn2_w_gu', 'ffn2_w_down', 'final_norm']


def kernel(x, ffn1_norm, ffn1_w_gu, ffn1_w_down, mix_norm, w_in, q_a_norm, w_q_b, kv_a_norm, w_kv_b, pool_w, pool_scale, fox_b_f, w_out, ffn2_norm, ffn2_w_gu, ffn2_w_down, final_norm, loss_target, m_ffn1_norm, m_ffn1_w_gu, m_ffn1_w_down, m_mix_norm, m_w_in, m_q_a_norm, m_w_q_b, m_kv_a_norm, m_w_kv_b, m_pool_w, m_pool_scale, m_fox_b_f, m_w_out, m_ffn2_norm, m_ffn2_w_gu, m_ffn2_w_down, m_final_norm, v_ffn1_norm, v_ffn1_w_gu, v_ffn1_w_down, v_mix_norm, v_w_in, v_q_a_norm, v_w_q_b, v_kv_a_norm, v_w_kv_b, v_pool_w, v_pool_scale, v_fox_b_f, v_w_out, v_ffn2_norm, v_ffn2_w_gu, v_ffn2_w_down, v_final_norm):
    params = dict(ffn1_norm=ffn1_norm, ffn1_w_gu=ffn1_w_gu, ffn1_w_down=ffn1_w_down, mix_norm=mix_norm, w_in=w_in, q_a_norm=q_a_norm,
                  w_q_b=w_q_b, kv_a_norm=kv_a_norm, w_kv_b=w_kv_b, pool_w=pool_w, pool_scale=pool_scale, fox_b_f=fox_b_f, w_out=w_out,
                  ffn2_norm=ffn2_norm, ffn2_w_gu=ffn2_w_gu, ffn2_w_down=ffn2_w_down, final_norm=final_norm)
    mom = dict(ffn1_norm=m_ffn1_norm, ffn1_w_gu=m_ffn1_w_gu, ffn1_w_down=m_ffn1_w_down, mix_norm=m_mix_norm, w_in=m_w_in,
               q_a_norm=m_q_a_norm, w_q_b=m_w_q_b, kv_a_norm=m_kv_a_norm, w_kv_b=m_w_kv_b, pool_w=m_pool_w, pool_scale=m_pool_scale,
               fox_b_f=m_fox_b_f, w_out=m_w_out, ffn2_norm=m_ffn2_norm, ffn2_w_gu=m_ffn2_w_gu, ffn2_w_down=m_ffn2_w_down,
               final_norm=m_final_norm)
    var = dict(ffn1_norm=v_ffn1_norm, ffn1_w_gu=v_ffn1_w_gu, ffn1_w_down=v_ffn1_w_down, mix_norm=v_mix_norm, w_in=v_w_in,
               q_a_norm=v_q_a_norm, w_q_b=v_w_q_b, kv_a_norm=v_kv_a_norm, w_kv_b=v_w_kv_b, pool_w=v_pool_w, pool_scale=v_pool_scale,
               fox_b_f=v_fox_b_f, w_out=v_w_out, ffn2_norm=v_ffn2_norm, ffn2_w_gu=v_ffn2_w_gu, ffn2_w_down=v_ffn2_w_down,
               final_norm=v_final_norm)

    first = [("ffn1_w_gu", 0), ("ffn1_w_down", 0)]
    mix0 = [(nm, 0) for nm in ("w_in", "w_q_b", "w_kv_b", "w_out")]
    rest = [(nm, l) for nm in BIG for l in range(DEPTH) if (nm, l) not in first + mix0]

    def shards(keys, zero=0.0):
        return [((_pad_w_in(params[nm]) if nm == "w_in" else params[nm])[l] + zero).astype(BF16) for nm, l in keys]

    wts = {nm: [None] * DEPTH for nm in WEIGHT_VIEWS}
    wts["pool_w"] = params["pool_w"]
    got = _gather_blocking(shards(first))
    _prepare_weights(dict(zip(first, got)), wts)
    sems_m, src_m, land_m, token_m = _gather_start(shards(mix0), got[0], "mix0")
    sm = dict(params)
    sm["ffn1_norm"] = params["ffn1_norm"] + token_m[0, 0]
    rest_shards = shards(rest, token_m[0, 0])
    flying = {}

    def late_weights(stage, act, sm_now):
        if stage == "ffn1":
            lands = _gather_forward(_gather_wait(sems_m, src_m, land_m, act, "mix0"), "mix0")
            _prepare_weights(dict(zip(mix0, lands)), wts)
            flying["rest"] = _gather_start(rest_shards, lands[0], "rest")
            sm_next = dict(sm_now)
            sm_next["mix_norm"] = sm_now["mix_norm"] + flying["rest"][3][0, 0]
            return sm_next
        sems_r, src_r, land_r, _ = flying["rest"]
        lands = _gather_forward(_gather_wait(sems_r, src_r, land_r, act, "rest"), "rest")
        _prepare_weights(dict(zip(rest, lands)), wts)
        return sm_now

    pos = _position()
    flight = {}

    groups = {"l1": (1, BIG), "l0a": (0, [nm for nm in BIG if not nm.startswith("ffn1")]),
              "l0b": (0, [nm for nm in BIG if nm.startswith("ffn1")])}
    pending = {}

    def to_chips(key, full, sib):
        psum = _sum2_bf16(pos, full, sib, name=f"chip_sum_{key}")
        s2 = _split_start(_stage2_copies, psum, [((3,) + p.shape[1:], p.dtype) for p in psum], 3 * len(psum),
                          f"reduce_stage2_start_{key}")
        flight[key] = (full, sib, s2)
        return s2[3][0, 0]

    def grads_ready(l, stage, g, sm_now):
        behind, tok = None, None
        if (l, stage) == (1, "ffn1"):
            full = [_chip_major(nm, g[nm]) for nm in BIG]
            pending["l1"] = _split_start(_stage1_copies, full, [((N_CHIPS, f.shape[1] // 2, f.shape[2]), F32) for f in full],
                                         len(full), "reduce_stage1_start_l1")
            behind, tok = "ffn2_norm", pending["l1"][3][0, 0]
        elif (l, stage) == (0, "ffn2"):
            sems1, full_thru, sib_land, _ = pending["l1"]
            full, sib = _split_wait(_stage1_copies, sems1, full_thru, sib_land, g["ffn2_w_down"], "reduce_stage1_wait_l1")
            tok = to_chips("l1", full, sib)
        elif (l, stage) == (0, "mix"):
            full = [_chip_major(nm, g[nm]) for nm in groups["l0a"][1]]
            pending["l0a"] = _split_start(_stage1_copies, full, [((N_CHIPS, f.shape[1] // 2, f.shape[2]), F32) for f in full],
                                          len(full), "reduce_stage1_start_l0a")
            behind, tok = "ffn1_norm", pending["l0a"][3][0, 0]
        elif (l, stage) == (0, "ffn1_tokens"):
            sems1, full_thru, sib_land, _ = pending["l0a"]
            full, sib = _split_wait(_stage1_copies, sems1, full_thru, sib_land, g["dx"], "reduce_stage1_wait_l0a")
            tok = to_chips("l0a", full, sib)
        elif (l, stage) == (0, "ffn1"):
            full = [_chip_major(nm, g[nm]) for nm in groups["l0b"][1]]
            pending["l0b"] = _split_start(_stage1_copies, full, [((N_CHIPS, f.shape[1] // 2, f.shape[2]), F32) for f in full],
                                          len(full), "reduce_stage1_start_l0b")
        if behind is None:
            return sm_now, tok
        sm_next = dict(sm_now)
        sm_next[behind] = sm_now[behind] + tok
        return sm_next, tok

    loss, dx, grads, g_final = _local_step(x[0], loss_target[0], wts, sm, late_weights, grads_ready)

    def view2d(a):
        return a.reshape(a.size // a.shape[-1], a.shape[-1])

    after = pending["l0b"][3]
    done = {nm: None for nm in BIG}
    for key in ("l1", "l0a", "l0b"):
        l, names = groups[key]
        full, sib, (sems2, ps_thru, lands2, _) = flight[key]
        _, recv = _split_wait(_stage2_copies, sems2, ps_thru, lands2, after, f"reduce_stage2_wait_{key}")
        whole = _reduce_stage3(_sum5(pos, full, sib, recv, name=f"grad_sum_{key}"), key)
        for nm, g_l in zip(names, whole):
            if nm == "w_in":
                g_l = _unpad_w_in(g_l)
            tr = max(t for t in (512, 352, 256, 128) if g_l.shape[0] % t == 0)
            done[nm] = _adamw_layer(view2d(params[nm]), g_l, view2d(mom[nm]), view2d(var[nm]), l, done[nm],
                                    name=f"adamw_{nm}_{l}", tr=tr)
        after = done[names[-1]][0][-8:, 0:128]
        if key == "l1":
            small_g, loss = _unpack_small(_allreduce_small(_pack_small(grads, g_final, loss)), params)
            sems1, full_thru, sib_land, _ = pending["l0b"]
            full_b, sib_b = _split_wait(_stage1_copies, sems1, full_thru, sib_land, after + small_g["final_norm"][0],
                                        "reduce_stage1_wait_l0b")
            after = after + to_chips("l0b", full_b, sib_b)
    gw, delta, new_m, new_v = dict(small_g), {}, {}, {}
    for nm in BIG:
        delta[nm], new_m[nm], new_v[nm], gw[nm] = [a.reshape(params[nm].shape) for a in done[nm]]
    for nm in small_g:
        delta[nm], new_m[nm], new_v[nm] = _update(nm, params[nm], gw[nm], mom[nm], var[nm])
    return (loss, dx[None], *[gw[n] for n in WEIGHTS], *[delta[n] for n in WEIGHTS], *[new_m[n] for n in WEIGHTS],
            *[new_v[n] for n in WEIGHTS])
```

```python
import functools
import math

import jax
import jax.numpy as jnp
import numpy as np
from jax import lax
from jax.experimental import pallas as pl
from jax.experimental.pallas import tpu as pltpu

F32 = jnp.float32
BF16 = jnp.bfloat16
MESH = pl.DeviceIdType.MESH
HBM_SPEC = pl.BlockSpec(memory_space=pltpu.HBM)

D = 1024
DEPTH = 2
D_FF = 2816
FF_SHARD = 1408
N_CHIPS = 4
H = 6
NOPE, ROPE, VDIM = 64, 32, 64
HALF_ROPE = ROPE // 2
Q_RANK, KV_RANK = 256, 128
POOL_W = 256
FOX_D = 64
N_IN = 1830
NZ = 2048
ROPE_THETA = 10000.0
EPS = 1e-6
POOL_HALO = 16
Z_QA, Z_KVA, Z_KR, Z_POOL, Z_FOX, Z_F = 0, 256, 384, 512, 768, 1920

ADAM_LR, ADAM_B1, ADAM_B2, ADAM_EPS, ADAM_WD, ADAM_STEP = 0.001, 0.9, 0.999, 1e-08, 0.01, 10

VMEM_LIMIT_V7X = 56 * 1024 * 1024


def _cp(sem=None, vmem=VMEM_LIMIT_V7X):
    return pltpu.CompilerParams(dimension_semantics=sem, vmem_limit_bytes=vmem)


def _sigmoid(x):
    return 0.5 * jnp.tanh(0.5 * x) + 0.5


def _dot(a, b, dims):
    return lax.dot_general(a, b, (dims, ((), ())), preferred_element_type=F32)


NN = ((1,), (0,))
NT = ((1,), (1,))
TN = ((0,), (0,))


def _mm(a, b, mode, *, name, out_dtype=F32, add=None, alpha=None, tm=512, tn=512, tk=512, n_major_out=False):
    if mode == "nn":
        (m, k), (k2, n) = a.shape, b.shape
    elif mode == "nt":
        (m, k), (n, k2) = a.shape, b.shape
    else:
        (k, m), (k2, n) = a.shape, b.shape
    assert k == k2
    tm, tn, tk = min(tm, m), min(tn, n), min(tk, k)
    assert m % tm == 0 and n % tn == 0 and k % tk == 0, (name, m, n, k, tm, tn, tk)
    nk = k // tk
    dims = {"nn": NN, "nt": NT, "tn": TN}[mode]
    a_spec = pl.BlockSpec((tk, tm), lambda i, j, kk: (kk, i)) if mode == "tn" else pl.BlockSpec((tm, tk), lambda i, j, kk: (i, kk))
    b_spec = pl.BlockSpec((tn, tk), lambda i, j, kk: (j, kk)) if mode == "nt" else pl.BlockSpec((tk, tn), lambda i, j, kk: (kk, j))
    in_specs = [a_spec, b_spec]
    args = [a, b]
    if add is not None:
        in_specs.append(pl.BlockSpec((tm, tn), lambda i, j, kk: (i, j)))
        args.append(add)
    if n_major_out:
        out_shape = jax.ShapeDtypeStruct((n // tn, m, tn), out_dtype)
        out_spec = pl.BlockSpec((None, tm, tn), lambda i, j, kk: (j, i, 0))
    else:
        out_shape = jax.ShapeDtypeStruct((m, n), out_dtype)
        out_spec = pl.BlockSpec((tm, tn), lambda i, j, kk: (i, j))

    def body(*refs):
        a_ref, b_ref = refs[0], refs[1]
        add_ref = refs[2] if add is not None else None
        o_ref, acc = refs[-2], refs[-1]
        kk = pl.program_id(2)

        @pl.when(kk == 0)
        def _():
            acc[...] = jnp.zeros_like(acc)

        acc[...] += _dot(a_ref[...].astype(BF16), b_ref[...].astype(BF16), dims)

        @pl.when(kk == nk - 1)
        def _():
            r = acc[...]
            if alpha is not None:
                r = r * alpha
            if add_ref is not None:
                r = r + add_ref[...].astype(F32)
            o_ref[...] = r.astype(out_dtype)

    return pl.pallas_call(
        body, name=name, grid=(m // tm, n // tn, nk), in_specs=in_specs, out_specs=out_spec, out_shape=out_shape,
        scratch_shapes=[pltpu.VMEM((tm, tn), F32)],
        compiler_params=_cp(("parallel", "parallel", "arbitrary")),
    )(*args)


def _rmsnorm_bwd(x, col_block, gain, da, w, dres=None, *, name, tm=512):
    s = x.shape[0]
    k, n = w.shape
    tm = min(tm, s)

    def body(*refs):
        x_ref, g_ref, da_ref, w_ref = refs[:4]
        dres_ref = refs[4] if dres is not None else None
        dx_ref, dg_ref = refs[-2], refs[-1]
        xv = x_ref[...]
        r = lax.rsqrt(jnp.mean(xv * xv, axis=-1, keepdims=True) + EPS)
        dhv = _dot(da_ref[...], w_ref[...], NT)
        a = dhv * g_ref[...]
        dx = r * a - xv * (r * r * r) * jnp.mean(a * xv, axis=-1, keepdims=True)
        if dres_ref is not None:
            dx = dx + dres_ref[...]
        dx_ref[...] = dx

        @pl.when(pl.program_id(0) == 0)
        def _():
            dg_ref[...] = jnp.zeros_like(dg_ref)

        dg_ref[...] += jnp.sum(dhv * xv * r, axis=0, keepdims=True)

    in_specs = [pl.BlockSpec((tm, k), lambda i: (i, col_block)), pl.BlockSpec((1, k), lambda i: (0, 0)),
                pl.BlockSpec((tm, n), lambda i: (i, 0)), pl.BlockSpec((k, n), lambda i: (0, 0))]
    args = [x, gain.reshape(1, k), da, w]
    if dres is not None:
        in_specs.append(pl.BlockSpec((tm, k), lambda i: (i, 0)))
        args.append(dres)
    dx, dg = pl.pallas_call(
        body, name=name, grid=(s // tm,), in_specs=in_specs,
        out_specs=[pl.BlockSpec((tm, k), lambda i: (i, 0)), pl.BlockSpec((1, k), lambda i: (0, 0))],
        out_shape=[jax.ShapeDtypeStruct((s, k), F32), jax.ShapeDtypeStruct((1, k), F32)],
        compiler_params=_cp(("arbitrary",)),
    )(*args)
    return dx, dg.reshape(k)


def _ffn_fwd(x, gain, w_gu4, w_d2, *, name, tm=256):
    s = x.shape[0]
    tm = min(tm, s)

    def body(x_ref, g_ref, wgu_ref, wd_ref, xo_ref, dgu_ref, act_ref):
        xv = x_ref[...]
        r = lax.rsqrt(jnp.mean(xv * xv, axis=-1, keepdims=True) + EPS)
        hv = (xv * r * g_ref[...]).astype(BF16)
        y = jnp.zeros((tm, D), F32)
        for j in range(2):
            g = _dot(hv, wgu_ref[j], NN)
            u = _dot(hv, wgu_ref[2 + j], NN)
            sg = _sigmoid(g)
            silu = g * sg
            dgu_ref[:, j * FF_SHARD:(j + 1) * FF_SHARD] = (u * (sg * (1.0 + g * (1.0 - sg)))).astype(BF16)
            dgu_ref[:, D_FF + j * FF_SHARD:D_FF + (j + 1) * FF_SHARD] = silu.astype(BF16)
            act = (silu * u).astype(BF16)
            act_ref[:, j * FF_SHARD:(j + 1) * FF_SHARD] = act
            y = y + _dot(act, wd_ref[j], NN)
        xo_ref[...] = xv + 0.5 * y

    row = lambda i: (i, 0)
    return pl.pallas_call(
        body, name=name, grid=(s // tm,),
        in_specs=[pl.BlockSpec((tm, D), row), pl.BlockSpec((1, D), lambda i: (0, 0)),
                  pl.BlockSpec((N_CHIPS, D, FF_SHARD), lambda i: (0, 0, 0), pipeline_mode=pl.Buffered(1)),
                  pl.BlockSpec((2, FF_SHARD, D), lambda i: (0, 0, 0), pipeline_mode=pl.Buffered(1))],
        out_specs=[pl.BlockSpec((tm, D), row), pl.BlockSpec((tm, 2 * D_FF), row), pl.BlockSpec((tm, D_FF), row)],
        out_shape=[jax.ShapeDtypeStruct((s, D), F32), jax.ShapeDtypeStruct((s, 2 * D_FF), BF16),
                   jax.ShapeDtypeStruct((s, D_FF), BF16)],
        compiler_params=_cp(("parallel",)),
    )(x, gain.reshape(1, D), w_gu4, w_d2)


FFN_ROW_CHUNK = 32


def _ffn_bwd(x, dxo, dloc, gain, w_gu4, w_d2, *, name, tm=256):
    s = x.shape[0]
    tm = min(tm, s)

    def body(x_ref, dxo_ref, dloc_ref, g_ref, wgu_ref, wd_ref, dx_ref, dgu_ref, h_ref, dy_ref, dg_ref):
        xv = x_ref[...]
        r = lax.rsqrt(jnp.mean(xv * xv, axis=-1, keepdims=True) + EPS)
        xh = xv * r
        h_ref[...] = (xh * g_ref[...]).astype(BF16)
        dxov = dxo_ref[...]
        dy = (0.5 * dxov).astype(BF16)
        dy_ref[...] = dy
        gcols = [slice(j * FF_SHARD, (j + 1) * FF_SHARD) for j in range(2)]
        ucols = [slice(D_FF + j * FF_SHARD, D_FF + (j + 1) * FF_SHARD) for j in range(2)]
        dacts = [_dot(dy, wd_ref[j], NT) for j in range(2)]
        for r0 in range(0, tm, FFN_ROW_CHUNK):
            rows = slice(r0, r0 + FFN_ROW_CHUNK)
            for j in range(2):
                da = dacts[j][rows]
                dgu_ref[rows, gcols[j]] = (da * dloc_ref[rows, gcols[j]].astype(F32)).astype(BF16)
                dgu_ref[rows, ucols[j]] = (da * dloc_ref[rows, ucols[j]].astype(F32)).astype(BF16)
        dh = jnp.zeros((tm, D), F32)
        for j in range(2):
            dh = dh + _dot(dgu_ref[:, gcols[j]], wgu_ref[j], NT) + _dot(dgu_ref[:, ucols[j]], wgu_ref[2 + j], NT)
        a = dh * g_ref[...]
        dx_ref[...] = dxov + r * a - xh * (r * jnp.mean(a * xh, axis=-1, keepdims=True))

        @pl.when(pl.program_id(0) == 0)
        def _():
            dg_ref[...] = jnp.zeros_like(dg_ref)

        dg_ref[...] += jnp.sum(dh * xh, axis=0, keepdims=True)

    row = lambda i: (i, 0)
    outs = pl.pallas_call(
        body, name=name, grid=(s // tm,),
        in_specs=[pl.BlockSpec((tm, D), row), pl.BlockSpec((tm, D), row), pl.BlockSpec((tm, 2 * D_FF), row),
                  pl.BlockSpec((1, D), lambda i: (0, 0)),
                  pl.BlockSpec((N_CHIPS, D, FF_SHARD), lambda i: (0, 0, 0), pipeline_mode=pl.Buffered(1)),
                  pl.BlockSpec((2, FF_SHARD, D), lambda i: (0, 0, 0), pipeline_mode=pl.Buffered(1))],
        out_specs=[pl.BlockSpec((tm, D), row), pl.BlockSpec((tm, 2 * D_FF), row),
                   pl.BlockSpec((tm, D), row), pl.BlockSpec((tm, D), row), pl.BlockSpec((1, D), lambda i: (0, 0))],
        out_shape=[jax.ShapeDtypeStruct((s, D), F32), jax.ShapeDtypeStruct((s, 2 * D_FF), BF16),
                   jax.ShapeDtypeStruct((s, D), BF16), jax.ShapeDtypeStruct((s, D), BF16), jax.ShapeDtypeStruct((1, D), F32)],
        compiler_params=_cp(("arbitrary",)),
    )(x, dxo, dloc, gain.reshape(1, D), w_gu4, w_d2)
    dx, dgu, h, dy, dg = outs
    return dx, dgu, h, dy, dg.reshape(D)


DA = 128
SCALE_MLA = 1.0 / math.sqrt(NOPE + ROPE)
SCALE_FOX = 1.0 / math.sqrt(FOX_D)


def _causal_blocks(nb, key_major):
    if key_major:
        pairs = [(i, j) for j in range(nb) for i in range(j, nb)]
    else:
        pairs = [(i, j) for i in range(nb) for j in range(i + 1)]
    return (jnp.asarray(np.array([p[0] for p in pairs], np.int32)), jnp.asarray(np.array([p[1] for p in pairs], np.int32)))


HEADS_PER_STEP = 3
ROW_CHUNK = 64

def _col_to_row(col):
    return jnp.broadcast_to(col, (col.shape[0], DA)).T[0:1, :]


def _attn_fwd(qa, ka, va, dv, *, name, t=512):
    h, s, _ = qa.shape
    t = min(t, s)
    nb = s // t
    g = H
    qi, kj = _causal_blocks(nb, key_major=False)

    rc = min(ROW_CHUNK, t)

    def body(qi_ref, kj_ref, q_ref, k_ref, v_ref, o_ref, lse_ref, m_sc, acc_sc, p_sc, a_sc):
        n = pl.program_id(1)
        i, j = qi_ref[n], kj_ref[n]

        @pl.when(j == 0)
        def _():
            m_sc[...] = jnp.full_like(m_sc, -jnp.inf)
            acc_sc[...] = jnp.zeros_like(acc_sc)

        def step(masked):
            scs = [_dot(q_ref[hh], k_ref[hh], NT) for hh in range(g)]
            for r0 in range(0, t, rc):
                rows = slice(r0, r0 + rc)
                for hh in range(g):
                    sr = scs[hh][rows]
                    if masked:
                        row = lax.broadcasted_iota(jnp.int32, (rc, t), 0) + r0
                        col = lax.broadcasted_iota(jnp.int32, (rc, t), 1)
                        sr = jnp.where(col <= row, sr, -jnp.inf)
                    tiles = [sr[:, c0:c0 + DA] for c0 in range(0, t, DA)]
                    top = tiles[0]
                    for tile in tiles[1:]:
                        top = jnp.maximum(top, tile)
                    m_old = m_sc[hh, rows]
                    m_new = jnp.maximum(m_old, jnp.max(top, axis=-1, keepdims=True))
                    for c0, tile in zip(range(0, t, DA), tiles):
                        p_sc[hh, rows, c0:c0 + DA] = jnp.exp(tile - m_new).astype(BF16)
                    a_sc[hh, rows] = jnp.exp(m_old - m_new)
                    m_sc[hh, rows] = m_new
            for hh in range(g):
                acc_sc[hh] = a_sc[hh] * acc_sc[hh] + _dot(p_sc[hh], v_ref[hh], NN)

        @pl.when(j < i)
        def _():
            step(False)

        @pl.when(j == i)
        def _():
            step(True)
            for hh in range(g):
                acc = acc_sc[hh]
                l = acc[:, dv:dv + 1]
                o_ref[hh] = acc[:, :dv] / l
                lse_ref[hh] = _col_to_row(m_sc[hh][:, 0:1] + jnp.log(l))

    qmap = lambda hg, n, qi_r, kj_r: (hg, qi_r[n], 0)
    kmap = lambda hg, n, qi_r, kj_r: (hg, kj_r[n], 0)
    return pl.pallas_call(
        body, name=name,
        grid_spec=pltpu.PrefetchScalarGridSpec(
            num_scalar_prefetch=2, grid=(h // g, qi.shape[0]),
            in_specs=[pl.BlockSpec((g, t, DA), qmap), pl.BlockSpec((g, t, DA), kmap), pl.BlockSpec((g, t, DA), kmap)],
            out_specs=[pl.BlockSpec((g, t, dv), qmap), pl.BlockSpec((g, 1, t), lambda hg, n, qi_r, kj_r: (hg, 0, qi_r[n]))],
            scratch_shapes=[pltpu.VMEM((g, t, DA), F32), pltpu.VMEM((g, t, DA), F32), pltpu.VMEM((g, t, t), BF16),
                            pltpu.VMEM((g, t, DA), F32)]),
        out_shape=[jax.ShapeDtypeStruct((h, s, dv), F32), jax.ShapeDtypeStruct((h, 1, s), F32)],
        compiler_params=_cp(("parallel", "arbitrary")),
    )(qi, kj, qa, ka, va)


def _attn_bwd(qa, ka, va, doa, lse_row, delta_row, decay, *, name, t=512):
    h, s, _ = qa.shape
    t = min(t, s)
    nb = s // t
    g = HEADS_PER_STEP
    rc = min(ROW_CHUNK, t)
    qi, kj = _causal_blocks(nb, key_major=True)
    nsteps = qi.shape[0]

    def body(*refs):
        qi_ref, kj_ref, q_ref, k_ref, v_ref, do_ref, lse_ref, dl_ref = refs[:8]
        p_sc, ds_sc = refs[-2:]
        if decay:
            dq_ref, dk_ref, dv_ref, dcq_ref, dck_ref, dq_acc, dk_acc, dv_acc, dcq_acc, dck_acc = refs[8:-2]
        else:
            dq_ref, dk_ref, dv_ref, dq_acc, dk_acc, dv_acc = refs[8:-2]
        n = pl.program_id(1)
        i, j = qi_ref[n], kj_ref[n]

        @pl.when(n == 0)
        def _():
            dq_acc[...] = jnp.zeros_like(dq_acc)
            if decay:
                dcq_acc[...] = jnp.zeros_like(dcq_acc)

        @pl.when(i == j)
        def _():
            dk_acc[...] = jnp.zeros_like(dk_acc)
            dv_acc[...] = jnp.zeros_like(dv_acc)
            if decay:
                dck_acc[...] = jnp.zeros_like(dck_acc)

        def step(masked):
            sts = [_dot(k_ref[hh], q_ref[hh], NT) for hh in range(g)]
            dpts = [_dot(v_ref[hh], do_ref[hh], NT) for hh in range(g)]
            dcq = [jnp.zeros((1, t), F32) for _ in range(g)]
            for r0 in range(0, t, rc):
                rows = slice(r0, r0 + rc)
                for hh in range(g):
                    st = sts[hh][rows]
                    if masked:
                        row = lax.broadcasted_iota(jnp.int32, (rc, t), 0) + r0
                        col = lax.broadcasted_iota(jnp.int32, (rc, t), 1)
                        st = jnp.where(row <= col, st, -jnp.inf)
                    pt = jnp.exp(st - lse_ref[hh])
                    dst = pt * (dpts[hh][rows] - dl_ref[hh])
                    p_sc[hh, rows] = pt.astype(BF16)
                    ds_sc[hh, rows] = dst.astype(BF16)
                    if decay:
                        dcq[hh] = dcq[hh] + jnp.sum(dst, axis=0, keepdims=True)
                        dck_acc[hh, rows] -= jnp.sum(dst, axis=1, keepdims=True)
            for hh in range(g):
                dv_acc[hh] += _dot(p_sc[hh], do_ref[hh], NN)
                dk_acc[hh] += _dot(ds_sc[hh], q_ref[hh], NN)
                dq_acc[hh, i] += _dot(ds_sc[hh], k_ref[hh], TN)
                if decay:
                    dcq_acc[hh, i] += dcq[hh]

        @pl.when(i > j)
        def _():
            step(False)

        @pl.when(i == j)
        def _():
            step(True)

        @pl.when(i == nb - 1)
        def _():
            dk_ref[...] = dk_acc[...]
            dv_ref[...] = dv_acc[...]
            if decay:
                for hh in range(g):
                    dck_ref[hh] = _col_to_row(dck_acc[hh])

        @pl.when(n == nsteps - 1)
        def _():
            dq_ref[...] = dq_acc[...]
            if decay:
                dcq_ref[...] = dcq_acc[...]

    kmap = lambda hg, n, qi_r, kj_r: (hg, kj_r[n], 0)
    qmap = lambda hg, n, qi_r, kj_r: (hg, qi_r[n], 0)
    qrow = lambda hg, n, qi_r, kj_r: (hg, 0, qi_r[n])
    krow = lambda hg, n, qi_r, kj_r: (hg, 0, kj_r[n])
    whole = lambda hg, n, qi_r, kj_r: (hg, 0, 0, 0)
    in_specs = [pl.BlockSpec((g, t, DA), qmap), pl.BlockSpec((g, t, DA), kmap), pl.BlockSpec((g, t, DA), kmap),
                pl.BlockSpec((g, t, DA), qmap), pl.BlockSpec((g, 1, t), qrow), pl.BlockSpec((g, 1, t), qrow)]
    out_specs = [pl.BlockSpec((g, nb, t, DA), whole), pl.BlockSpec((g, t, DA), kmap), pl.BlockSpec((g, t, DA), kmap)]
    out_shape = [jax.ShapeDtypeStruct((h, nb, t, DA), F32), jax.ShapeDtypeStruct((h, s, DA), F32), jax.ShapeDtypeStruct((h, s, DA), F32)]
    scratch = [pltpu.VMEM((g, nb, t, DA), F32), pltpu.VMEM((g, t, DA), F32), pltpu.VMEM((g, t, DA), F32)]
    if decay:
        out_specs += [pl.BlockSpec((g, nb, 1, t), whole), pl.BlockSpec((g, 1, t), krow)]
        out_shape += [jax.ShapeDtypeStruct((h, nb, 1, t), F32), jax.ShapeDtypeStruct((h, 1, s), F32)]
        scratch += [pltpu.VMEM((g, nb, 1, t), F32), pltpu.VMEM((g, t, 1), F32)]
    scratch += [pltpu.VMEM((g, t, t), BF16), pltpu.VMEM((g, t, t), BF16)]
    outs = pl.pallas_call(
        body, name=name,
        grid_spec=pltpu.PrefetchScalarGridSpec(num_scalar_prefetch=2, grid=(h // g, nsteps), in_specs=in_specs, out_specs=out_specs,
                                               scratch_shapes=scratch),
        out_shape=out_shape, compiler_params=_cp(("parallel", "arbitrary")),
    )(qi, kj, qa, ka, va, doa, lse_row, delta_row)
    outs = list(outs)
    outs[0] = outs[0].reshape(h, s, DA)
    if decay:
        outs[3] = outs[3].reshape(h, 1, s)
    return outs


def _sel(rows, cols, pairs, value=1.0):
    m = np.zeros((rows, cols), np.float32)
    for r, c in pairs:
        m[r, c] = value
    return jnp.asarray(m, BF16)


def _lane_row(lanes):
    m = np.zeros((1, DA), np.float32)
    m[0, list(lanes)] = 1.0
    return jnp.asarray(m)


def _rms(xv, gain):
    r = lax.rsqrt(jnp.mean(xv * xv, axis=-1, keepdims=True) + EPS)
    return xv * r * gain


def _mix_in(x, gains, w_in, wq_a, wq_b, wk, wv, tabs, *, name, tm=512):
    s = x.shape[0]
    tm = min(tm, s)
    one = _lane_row([VDIM])
    g_mix, g_q, g_kv = gains

    def body(x_ref, gm_ref, gq_ref, gkv_ref, win_ref, wa_ref, wb_ref, wk_ref, wv_ref, cq_ref, sq_ref, ck_ref, sk_ref,
             one_ref, z_ref, h_ref, qn_ref, kvn_ref, qa_ref, ka_ref, va_ref):
        hv = _rms(x_ref[...], gm_ref[...]).astype(BF16)
        h_ref[...] = hv
        z = _dot(hv, win_ref[...], NN)
        z_ref[...] = z
        qn = _rms(z[:, Z_QA:Z_QA + Q_RANK], gq_ref[...]).astype(BF16)
        kvn = _rms(z[:, Z_KVA:Z_KVA + KV_RANK], gkv_ref[...]).astype(BF16)
        qn_ref[...] = qn
        kvn_ref[...] = kvn
        c, sn = cq_ref[...], sq_ref[...]
        kpe = z[:, Z_KR:Z_KR + DA] * ck_ref[...] + z[:, Z_F:Z_F + DA] * sk_ref[...]
        for hh in range(H):
            cols = slice(hh * DA, (hh + 1) * DA)
            qa_ref[hh] = (_dot(qn, wa_ref[:, cols], NN) * c + _dot(qn, wb_ref[:, cols], NN) * sn).astype(BF16)
            ka_ref[hh] = (_dot(kvn, wk_ref[:, cols], NN) + kpe).astype(BF16)
            va_ref[hh] = (_dot(kvn, wv_ref[:, cols], NN) + one_ref[...]).astype(BF16)

    row = lambda i: (i, 0)
    fixed = lambda i: (0, 0)
    full = lambda a: pl.BlockSpec(a.shape, fixed)
    tab = pl.BlockSpec((tm, DA), row)
    heads = pl.BlockSpec((H, tm, DA), lambda i: (0, i, 0))
    return pl.pallas_call(
        body, name=name, grid=(s // tm,),
        in_specs=[pl.BlockSpec((tm, D), row), pl.BlockSpec((1, D), fixed), pl.BlockSpec((1, Q_RANK), fixed),
                  pl.BlockSpec((1, KV_RANK), fixed), full(w_in), full(wq_a), full(wq_b), full(wk), full(wv),
                  tab, tab, tab, tab, pl.BlockSpec((1, DA), fixed)],
        out_specs=[pl.BlockSpec((tm, NZ), row), pl.BlockSpec((tm, D), row), pl.BlockSpec((tm, Q_RANK), row),
                   pl.BlockSpec((tm, KV_RANK), row), heads, heads, heads],
        out_shape=[jax.ShapeDtypeStruct((s, NZ), F32), jax.ShapeDtypeStruct((s, D), BF16),
                   jax.ShapeDtypeStruct((s, Q_RANK), BF16), jax.ShapeDtypeStruct((s, KV_RANK), BF16)]
        + [jax.ShapeDtypeStruct((H, s, DA), BF16)] * 3,
        compiler_params=_cp(("parallel",)),
    )(x, g_mix.reshape(1, D), g_q.reshape(1, Q_RANK), g_kv.reshape(1, KV_RANK), w_in, wq_a, wq_b, wk, wv,
      tabs["cq"], tabs["sq"], tabs["ck"], tabs["sk"], one)


DEC_C = (FOX_D, FOX_D + 1, FOX_D + 2)
DEC_1 = (FOX_D + 3, FOX_D + 4, FOX_D + 5)


def _fox_prep(z, c3t, *, name, tm=512):
    s = z.shape[0]
    tm = min(tm, s)
    w = H * FOX_D
    left = [(r, r) for r in range(FOX_D)]
    right = [(FOX_D + r, r) for r in range(FOX_D)]
    pq = jnp.stack([_sel(DA, DA, left, SCALE_FOX), _sel(DA, DA, right, SCALE_FOX)])
    pk = jnp.stack([_sel(DA, DA, left), _sel(DA, DA, right)])
    pcq = jnp.stack([_sel(32, DA, [(hh + 8 * k, DEC_C[k]) for k in range(3)]) for hh in range(H)])
    pck = jnp.stack([_sel(32, DA, [(hh + 8 * k, DEC_1[k]) for k in range(3)], -1.0) for hh in range(H)])
    rows3 = jnp.concatenate([_lane_row(DEC_1), _lane_row(DEC_C), _lane_row([FOX_D])], axis=0)

    def body(zq_ref, zk_ref, zv_ref, c_ref, pq_ref, pk_ref, pcq_ref, pck_ref, r_ref, qa_ref, ka_ref, va_ref):
        c3 = c_ref[...]
        for pair in range(H // 2):
            lanes = slice(pair * DA, (pair + 1) * DA)
            zq, zk, zv = zq_ref[:, lanes].astype(BF16), zk_ref[:, lanes].astype(BF16), zv_ref[:, lanes].astype(BF16)
            for side in range(2):
                hh = 2 * pair + side
                qa_ref[hh] = (_dot(zq, pq_ref[side], NN) + _dot(c3, pcq_ref[hh], TN) + r_ref[0:1, :]).astype(BF16)
                ka_ref[hh] = (_dot(zk, pk_ref[side], NN) + _dot(c3, pck_ref[hh], TN) + r_ref[1:2, :]).astype(BF16)
                va_ref[hh] = (_dot(zv, pk_ref[side], NN) + r_ref[2:3, :]).astype(BF16)

    fixed2 = lambda i: (0, 0)
    fixed3 = lambda i: (0, 0, 0)
    heads = pl.BlockSpec((H, tm, DA), lambda i: (0, i, 0))
    zblk = lambda c: pl.BlockSpec((tm, w), lambda i: (i, c))
    return pl.pallas_call(
        body, name=name, grid=(s // tm,),
        in_specs=[zblk(Z_FOX // w), zblk(Z_FOX // w + 1), zblk(Z_FOX // w + 2), pl.BlockSpec((32, tm), lambda i: (0, i)),
                  pl.BlockSpec((2, DA, DA), fixed3), pl.BlockSpec((2, DA, DA), fixed3),
                  pl.BlockSpec((H, 32, DA), fixed3), pl.BlockSpec((H, 32, DA), fixed3), pl.BlockSpec((3, DA), fixed2)],
        out_specs=[heads, heads, heads], out_shape=[jax.ShapeDtypeStruct((H, s, DA), BF16)] * 3,
        compiler_params=_cp(("parallel",)),
    )(z, z, z, c3t, pq, pk, pcq, pck, rows3)


def _mix_out(oa, yb, oc, w_out, x1, *, name, tm=512):
    s = yb.shape[0]
    tm = min(tm, s)
    e2 = jnp.stack([_sel(VDIM, DA, [(r, r) for r in range(VDIM)]), _sel(VDIM, DA, [(r, VDIM + r) for r in range(VDIM)])])

    def body(oa_ref, yb_ref, oc_ref, e_ref, w_ref, x_ref, x2_ref, cat_ref):
        def pairs(o_ref):
            return [(_dot(o_ref[2 * p].astype(BF16), e_ref[0], NN) + _dot(o_ref[2 * p + 1].astype(BF16), e_ref[1], NN)).astype(BF16)
                    for p in range(H // 2)]

        cat = jnp.concatenate(pairs(oa_ref) + [yb_ref[...].astype(BF16)] + pairs(oc_ref), axis=1)
        cat_ref[...] = cat
        x2_ref[...] = x_ref[...] + _dot(cat, w_ref[...], NN)

    row = lambda i: (i, 0)
    heads = pl.BlockSpec((H, tm, VDIM), lambda i: (0, i, 0))
    return pl.pallas_call(
        body, name=name, grid=(s // tm,),
        in_specs=[heads, pl.BlockSpec((tm, POOL_W), row), heads, pl.BlockSpec((2, VDIM, DA), lambda i: (0, 0, 0)),
                  pl.BlockSpec((D, D), lambda i: (0, 0)), pl.BlockSpec((tm, D), row)],
        out_specs=[pl.BlockSpec((tm, D), row), pl.BlockSpec((tm, D), row)],
        out_shape=[jax.ShapeDtypeStruct((s, D), F32), jax.ShapeDtypeStruct((s, D), BF16)],
        compiler_params=_cp(("parallel",)),
    )(oa, yb, oc, e2, w_out, x1)


def _mix_out_bwd(dx2b, w_out, cat, oa, oc, *, name, tm=512):
    s = dx2b.shape[0]
    tm = min(tm, s)
    f2 = jnp.stack([_sel(DA, DA, [(r, r) for r in range(VDIM)]), _sel(DA, DA, [(VDIM + r, r) for r in range(VDIM)])])
    nv = H * VDIM

    def body(dx_ref, w_ref, cat_ref, oa_ref, oc_ref, f_ref, doa_ref, doc_ref, dyb_ref, dla_ref, dlc_ref, dw_ref):
        @pl.when(pl.program_id(0) == 0)
        def _():
            dw_ref[...] = jnp.zeros_like(dw_ref)

        dw_ref[...] += _dot(cat_ref[...], dx_ref[...], TN)
        dcat = _dot(dx_ref[...], w_ref[...], NT)
        dyb_ref[...] = dcat[:, nv:nv + POOL_W]
        for base, o_ref, do_ref, dl_ref in ((0, oa_ref, doa_ref, dla_ref), (nv + POOL_W, oc_ref, doc_ref, dlc_ref)):
            for p in range(H // 2):
                blk = dcat[:, base + p * DA:base + (p + 1) * DA].astype(BF16)
                for side in range(2):
                    hh = 2 * p + side
                    do = _dot(blk, f_ref[side], NN)
                    do_ref[hh] = do.astype(BF16)
                    dl_ref[hh] = _col_to_row(jnp.sum(do[:, :VDIM] * o_ref[hh], axis=-1, keepdims=True))

    row = lambda i: (i, 0)
    heads = lambda w: pl.BlockSpec((H, tm, w), lambda i: (0, i, 0))
    return pl.pallas_call(
        body, name=name, grid=(s // tm,),
        in_specs=[pl.BlockSpec((tm, D), row), pl.BlockSpec((D, D), lambda i: (0, 0)), pl.BlockSpec((tm, D), row),
                  heads(VDIM), heads(VDIM), pl.BlockSpec((2, DA, DA), lambda i: (0, 0, 0))],
        out_specs=[heads(DA), heads(DA), pl.BlockSpec((tm, POOL_W), row),
                   pl.BlockSpec((H, 1, tm), lambda i: (0, 0, i)), pl.BlockSpec((H, 1, tm), lambda i: (0, 0, i)),
                   pl.BlockSpec((D, D), lambda i: (0, 0))],
        out_shape=[jax.ShapeDtypeStruct((H, s, DA), BF16), jax.ShapeDtypeStruct((H, s, DA), BF16),
                   jax.ShapeDtypeStruct((s, POOL_W), F32), jax.ShapeDtypeStruct((H, 1, s), F32), jax.ShapeDtypeStruct((H, 1, s), F32),
                   jax.ShapeDtypeStruct((D, D), F32)],
        compiler_params=_cp(("arbitrary",)),
    )(dx2b, w_out, cat, oa, oc, f2)


def _mla_bwd_prep(dqa, dka, dva, dft, qn, kvn, cq, sq, ck, sk, *, name, tm=512):
    s = dqa.shape[1]
    tm = min(tm, s)
    keep = _lane_row(range(NOPE))

    def body(dq_ref, dk_ref, dv_ref, dft_ref, qn_ref, kvn_ref, cq_ref, sq_ref, ck_ref, sk_ref, keep_ref,
             dqab_ref, dkv_ref, dz3_ref, dz15_ref, dwq_ref, dwkv_ref):
        cqv, sqv = cq_ref[...], sq_ref[...]
        dkpe = jnp.zeros((tm, DA), F32)
        for hh in range(H):
            lanes = slice(hh * DA, (hh + 1) * DA)
            dq = dq_ref[hh]
            dqab_ref[:, lanes] = (dq * cqv).astype(BF16)
            dqab_ref[:, H * DA + hh * DA:H * DA + (hh + 1) * DA] = (dq * sqv).astype(BF16)
            dk = dk_ref[hh]
            dkpe = dkpe + dk
            dkv_ref[:, lanes] = (dk * keep_ref[...]).astype(BF16)
            dkv_ref[:, H * DA + hh * DA:H * DA + (hh + 1) * DA] = (dv_ref[hh] * keep_ref[...]).astype(BF16)
        dz3_ref[...] = (dkpe * ck_ref[...]).astype(BF16)
        dz15_ref[...] = (dkpe * sk_ref[...] + dft_ref[...]).astype(BF16)

        @pl.when(pl.program_id(0) == 0)
        def _():
            dwq_ref[...] = jnp.zeros_like(dwq_ref)
            dwkv_ref[...] = jnp.zeros_like(dwkv_ref)

        dwq_ref[...] += _dot(qn_ref[...], dqab_ref[...], TN)
        dwkv_ref[...] += _dot(kvn_ref[...], dkv_ref[...], TN)

    row = lambda i: (i, 0)
    fixed = lambda i: (0, 0)
    heads = pl.BlockSpec((H, tm, DA), lambda i: (0, i, 0))
    tab = pl.BlockSpec((tm, DA), row)
    wide = 2 * H * DA
    return pl.pallas_call(
        body, name=name, grid=(s // tm,),
        in_specs=[heads, heads, heads, tab, pl.BlockSpec((tm, Q_RANK), row), pl.BlockSpec((tm, KV_RANK), row),
                  tab, tab, tab, tab, pl.BlockSpec((1, DA), fixed)],
        out_specs=[pl.BlockSpec((tm, wide), row), pl.BlockSpec((tm, wide), row), tab, tab,
                   pl.BlockSpec((Q_RANK, wide), fixed), pl.BlockSpec((KV_RANK, wide), fixed)],
        out_shape=[jax.ShapeDtypeStruct((s, wide), BF16), jax.ShapeDtypeStruct((s, wide), BF16),
                   jax.ShapeDtypeStruct((s, DA), BF16), jax.ShapeDtypeStruct((s, DA), BF16),
                   jax.ShapeDtypeStruct((Q_RANK, wide), F32), jax.ShapeDtypeStruct((KV_RANK, wide), F32)],
        compiler_params=_cp(("arbitrary",)),
    )(dqa, dka, dva, dft, qn, kvn, cq, sq, ck, sk, keep)


def _fox_bwd_prep(dfqa, dfka, dfva, *, name, tm=512):
    s = dfqa.shape[1]
    tm = min(tm, s)
    place = lambda v: jnp.stack([_sel(DA, DA, [(r, r) for r in range(FOX_D)], v), _sel(DA, DA, [(r, FOX_D + r) for r in range(FOX_D)], v)])
    gq, gk = place(SCALE_FOX), place(1.0)

    def body(dq_ref, dk_ref, dv_ref, gq_ref, gk_ref, dz_ref):
        for part, (d_ref, g_ref) in enumerate(((dq_ref, gq_ref), (dk_ref, gk_ref), (dv_ref, gk_ref))):
            for p in range(H // 2):
                blk = _dot(d_ref[2 * p].astype(BF16), g_ref[0], NN) + _dot(d_ref[2 * p + 1].astype(BF16), g_ref[1], NN)
                lo = part * H * FOX_D + p * DA
                dz_ref[:, lo:lo + DA] = blk.astype(BF16)

    heads = pl.BlockSpec((H, tm, DA), lambda i: (0, i, 0))
    sel = pl.BlockSpec((2, DA, DA), lambda i: (0, 0, 0))
    return pl.pallas_call(
        body, name=name, grid=(s // tm,), in_specs=[heads, heads, heads, sel, sel],
        out_specs=pl.BlockSpec((tm, 3 * H * FOX_D), lambda i: (i, 0)),
        out_shape=jax.ShapeDtypeStruct((s, 3 * H * FOX_D), BF16), compiler_params=_cp(("parallel",)),
    )(dfqa, dfka, dfva, gq, gk)


def _lane_scan(x, s, reverse):
    lane = lax.broadcasted_iota(jnp.int32, x.shape, 1)
    sh = 1
    while sh < s:
        if reverse:
            x = x + jnp.where(lane < s - sh, pltpu.roll(x, s - sh, axis=1), 0.0)
        else:
            x = x + jnp.where(lane >= sh, pltpu.roll(x, sh, axis=1), 0.0)
        sh *= 2
    return x


def _gate_fwd(z, col_block, bias, *, name):
    s = z.shape[0]

    def body(z_ref, b_ref, f_ref, c_ref):
        ft = z_ref[...].T[0:8, :]
        f_ref[...] = ft
        xg = ft + b_ref[...]
        lf = jnp.minimum(xg, 0.0) - jnp.log(1.0 + jnp.exp(-jnp.abs(xg)))
        c = _lane_scan(lf, s, False)
        hi = c.astype(BF16).astype(F32)
        r = c - hi
        mid = r.astype(BF16).astype(F32)
        lo = r - mid
        c_ref[...] = jnp.concatenate([hi, mid, lo, jnp.zeros_like(hi)], axis=0).astype(BF16)

    return pl.pallas_call(
        body, name=name, grid=(1,),
        in_specs=[pl.BlockSpec((s, 128), lambda i: (0, col_block)), pl.BlockSpec((8, 1), lambda i: (0, 0))],
        out_specs=[pl.BlockSpec((8, s), lambda i: (0, 0)), pl.BlockSpec((32, s), lambda i: (0, 0))],
        out_shape=[jax.ShapeDtypeStruct((8, s), F32), jax.ShapeDtypeStruct((32, s), BF16)],
        compiler_params=_cp(("arbitrary",)))(z, bias)


def _gate_bwd(ft, bias, dc, *, name):
    s = ft.shape[1]

    def body(f_ref, b_ref, dc_ref, df_ref, db_ref):
        xg = f_ref[...] + b_ref[...]
        dlf = _lane_scan(dc_ref[...], s, True)
        df = dlf * _sigmoid(-xg)
        db_ref[...] = jnp.sum(df, axis=-1, keepdims=True)
        df_ref[...] = jnp.concatenate([df, jnp.zeros((DA - 8, s), F32)], axis=0).T

    return pl.pallas_call(body, name=name, out_shape=[jax.ShapeDtypeStruct((s, DA), F32), jax.ShapeDtypeStruct((8, 1), F32)],
                          compiler_params=_cp())(ft, bias, dc)


def _pool_lane_consts(tm, i):
    lane = lax.broadcasted_iota(jnp.int32, (tm, POOL_W), 1)
    tok = lax.broadcasted_iota(jnp.int32, (tm, POOL_W), 0) + i * tm
    win = jnp.where(lane < 64, 2, jnp.where(lane < 128, 4, jnp.where(lane < 192, 8, 16)))
    cnt = jnp.minimum(tok + 1, win).astype(F32)
    return lane, tok, cnt


def _pick_window(lane, s2, s4, s8, s16):
    return jnp.where(lane < 64, s2, jnp.where(lane < 128, s4, jnp.where(lane < 192, s8, s16)))


def _pool_fwd(z, col_block, bd, scale, *, name, tm=512):
    s = z.shape[0]
    tm = min(tm, s)
    hb = tm // POOL_HALO

    def body(u_ref, halo_ref, bd_ref, sc_ref, y_ref, p_ref, buf):
        i = pl.program_id(0)
        buf[0:POOL_HALO, :] = halo_ref[...] * (i > 0).astype(F32)
        buf[POOL_HALO:, :] = u_ref[...]

        def back(k):
            return buf[POOL_HALO - k:POOL_HALO - k + tm, :]

        u = u_ref[...]
        s2 = u + back(1)
        s4 = s2 + back(2) + back(3)
        s8 = s4 + back(4) + back(5) + back(6) + back(7)
        s16 = s8
        for k in range(8, 16):
            s16 = s16 + back(k)
        lane, _, cnt = _pool_lane_consts(tm, i)
        pooled = (_pick_window(lane, s2, s4, s8, s16) / cnt - u).astype(BF16)
        p_ref[...] = pooled
        y_ref[...] = _dot(pooled, bd_ref[...], NN) * sc_ref[...]

    return pl.pallas_call(
        body, name=name, grid=(s // tm,),
        in_specs=[pl.BlockSpec((tm, POOL_W), lambda i: (i, col_block)),
                  pl.BlockSpec((POOL_HALO, POOL_W), lambda i: (jnp.maximum(i * hb - 1, 0), col_block)),
                  pl.BlockSpec((POOL_W, POOL_W), lambda i: (0, 0)), pl.BlockSpec((1, POOL_W), lambda i: (0, 0))],
        out_specs=[pl.BlockSpec((tm, POOL_W), lambda i: (i, 0)), pl.BlockSpec((tm, POOL_W), lambda i: (i, 0))],
        out_shape=[jax.ShapeDtypeStruct((s, POOL_W), F32), jax.ShapeDtypeStruct((s, POOL_W), BF16)],
        scratch_shapes=[pltpu.VMEM((tm + POOL_HALO, POOL_W), F32)],
        compiler_params=_cp(("parallel",)),
    )(z, z, bd, scale.reshape(1, POOL_W))


def _pool_bwd_a(dy, pooled, bd, scale, *, name, tm=512):
    s = dy.shape[0]
    tm = min(tm, s)

    def body(dy_ref, p_ref, bd_ref, sc_ref, dq_ref, dbd_ref, dsc_ref):
        i = pl.program_id(0)
        dyv = dy_ref[...]
        pv = p_ref[...]
        y0 = _dot(pv, bd_ref[...], NN)
        dys = (dyv * sc_ref[...]).astype(BF16)
        dp = _dot(dys, bd_ref[...], NT)
        _, _, cnt = _pool_lane_consts(tm, i)
        dq_ref[:, 0:POOL_W] = dp / cnt
        dq_ref[:, POOL_W:] = dp

        @pl.when(i == 0)
        def _():
            dsc_ref[...] = jnp.zeros_like(dsc_ref)
            dbd_ref[...] = jnp.zeros_like(dbd_ref)

        dsc_ref[...] += jnp.sum(dyv * y0, axis=0, keepdims=True)
        dbd_ref[...] += _dot(pv, dys, TN)

    row = lambda i: (i, 0)
    fixed = lambda i: (0, 0)
    dq, dbd, dsc = pl.pallas_call(
        body, name=name, grid=(s // tm,),
        in_specs=[pl.BlockSpec((tm, POOL_W), row), pl.BlockSpec((tm, POOL_W), row),
                  pl.BlockSpec((POOL_W, POOL_W), fixed), pl.BlockSpec((1, POOL_W), fixed)],
        out_specs=[pl.BlockSpec((tm, 2 * POOL_W), row), pl.BlockSpec((POOL_W, POOL_W), fixed), pl.BlockSpec((1, POOL_W), fixed)],
        out_shape=[jax.ShapeDtypeStruct((s, 2 * POOL_W), F32), jax.ShapeDtypeStruct((POOL_W, POOL_W), F32),
                   jax.ShapeDtypeStruct((1, POOL_W), F32)],
        compiler_params=_cp(("arbitrary",)),
    )(dy, pooled, bd, scale.reshape(1, POOL_W))
    return dq, dbd, dsc.reshape(POOL_W)


def _pool_bwd_b(dq, *, name, tm=512):
    s = dq.shape[0]
    tm = min(tm, s)
    hb = tm // POOL_HALO
    nblk = s // tm

    def body(q_ref, dp_ref, halo_ref, du_ref, buf):
        i = pl.program_id(0)
        buf[0:tm, :] = q_ref[...]
        buf[tm:, :] = halo_ref[...] * (i < nblk - 1).astype(F32)

        def ahead(k):
            return buf[k:k + tm, :]

        q = q_ref[...]
        s2 = q + ahead(1)
        s4 = s2 + ahead(2) + ahead(3)
        s8 = s4 + ahead(4) + ahead(5) + ahead(6) + ahead(7)
        s16 = s8
        for k in range(8, 16):
            s16 = s16 + ahead(k)
        lane = lax.broadcasted_iota(jnp.int32, (tm, POOL_W), 1)
        du_ref[...] = _pick_window(lane, s2, s4, s8, s16) - dp_ref[...]

    return pl.pallas_call(
        body, name=name, grid=(nblk,),
        in_specs=[pl.BlockSpec((tm, POOL_W), lambda i: (i, 0)), pl.BlockSpec((tm, POOL_W), lambda i: (i, 1)),
                  pl.BlockSpec((POOL_HALO, POOL_W), lambda i: (jnp.minimum((i + 1) * hb, nblk * hb - 1), 0))],
        out_specs=pl.BlockSpec((tm, POOL_W), lambda i: (i, 0)),
        out_shape=jax.ShapeDtypeStruct((s, POOL_W), F32),
        scratch_shapes=[pltpu.VMEM((tm + POOL_HALO, POOL_W), F32)],
        compiler_params=_cp(("parallel",)),
    )(dq, dq, dq)


def _loss_head(x, gain, target, *, name, tm=512):
    s = x.shape[0]
    tm = min(tm, s)

    def body(x_ref, g_ref, t_ref, dx_ref, dg_ref, loss_ref):
        xv = x_ref[...]
        r = lax.rsqrt(jnp.mean(xv * xv, axis=-1, keepdims=True) + EPS)
        xh = xv * r
        err = xh * g_ref[...] - t_ref[...]
        dy = err * (1.0 / D)
        a = dy * g_ref[...]
        dx_ref[...] = r * a - xh * (r * jnp.mean(a * xh, axis=-1, keepdims=True))

        @pl.when(pl.program_id(0) == 0)
        def _():
            dg_ref[...] = jnp.zeros_like(dg_ref)
            loss_ref[...] = jnp.zeros_like(loss_ref)

        dg_ref[...] += jnp.sum(dy * xh, axis=0, keepdims=True)
        part = 0.5 * jnp.sum(jnp.mean(err * err, axis=-1, keepdims=True), axis=0, keepdims=True)
        loss_ref[...] += jnp.broadcast_to(part, loss_ref.shape)

    row = lambda i: (i, 0)
    dx, dg, loss = pl.pallas_call(
        body, name=name, grid=(s // tm,),
        in_specs=[pl.BlockSpec((tm, D), row), pl.BlockSpec((1, D), lambda i: (0, 0)), pl.BlockSpec((tm, D), row)],
        out_specs=[pl.BlockSpec((tm, D), row), pl.BlockSpec((1, D), lambda i: (0, 0)), pl.BlockSpec((1, 128), lambda i: (0, 0))],
        out_shape=[jax.ShapeDtypeStruct((s, D), F32), jax.ShapeDtypeStruct((1, D), F32), jax.ShapeDtypeStruct((1, 128), F32)],
        compiler_params=_cp(("arbitrary",)),
    )(x, gain.reshape(1, D), target)
    return dx, dg.reshape(D), loss[0, 0]


def _adamw(w, g, m, v, *, name, tr=512):
    rows, cols = w.shape
    tr = min(tr, rows)
    assert rows % tr == 0, (name, rows, tr)
    c_m = 1.0 - ADAM_B1
    c_v = 1.0 - ADAM_B2
    bc1 = 1.0 - ADAM_B1 ** ADAM_STEP
    bc2 = 1.0 - ADAM_B2 ** ADAM_STEP

    def body(w_ref, g_ref, m_ref, v_ref, d_ref, mo_ref, vo_ref):
        gv = g_ref[...]
        mn = ADAM_B1 * m_ref[...] + c_m * gv
        vn = ADAM_B2 * v_ref[...] + c_v * (gv * gv)
        mo_ref[...] = mn
        vo_ref[...] = vn
        d_ref[...] = -ADAM_LR * ((mn / bc1) / (jnp.sqrt(vn / bc2) + ADAM_EPS) + ADAM_WD * w_ref[...])

    spec = pl.BlockSpec((tr, cols), lambda i: (i, 0))
    return pl.pallas_call(body, name=name, grid=(rows // tr,), in_specs=[spec] * 4, out_specs=[spec] * 3,
                          out_shape=[jax.ShapeDtypeStruct((rows, cols), F32)] * 3,
                          compiler_params=_cp(("parallel",)))(w, g, m, v)


def _adamw_layer(w, g, m, v, layer, prev, *, name, tr):
    rows, cols = g.shape
    assert rows % tr == 0 and w.shape == (DEPTH * rows, cols), (name, w.shape, g.shape, tr)
    nblk = rows // tr
    c_m = 1.0 - ADAM_B1
    c_v = 1.0 - ADAM_B2
    bc1 = 1.0 - ADAM_B1 ** ADAM_STEP
    bc2 = 1.0 - ADAM_B2 ** ADAM_STEP
    n_prev = 0 if prev is None else 4

    def body(*refs):
        w_ref, g_ref, m_ref, v_ref = refs[:4]
        d_ref, mo_ref, vo_ref, go_ref = refs[4 + n_prev:]
        gv = g_ref[...]
        mn = ADAM_B1 * m_ref[...] + c_m * gv
        vn = ADAM_B2 * v_ref[...] + c_v * (gv * gv)
        mo_ref[...] = mn
        vo_ref[...] = vn
        go_ref[...] = gv
        d_ref[...] = -ADAM_LR * ((mn / bc1) / (jnp.sqrt(vn / bc2) + ADAM_EPS) + ADAM_WD * w_ref[...])

    stacked = pl.BlockSpec((tr, cols), lambda i: (layer * nblk + i, 0))
    args = [w, g, m, v] + ([] if prev is None else list(prev))
    return pl.pallas_call(
        body, name=name, grid=(nblk,),
        in_specs=[stacked, pl.BlockSpec((tr, cols), lambda i: (i, 0)), stacked, stacked] + [ANY_SPEC] * n_prev,
        out_specs=[stacked] * 4, out_shape=[jax.ShapeDtypeStruct(w.shape, F32)] * 4,
        input_output_aliases={4 + k: k for k in range(n_prev)},
        compiler_params=_cp(("parallel",)))(*args)


def _position():
    return jnp.stack([lax.axis_index("c"), 2 * lax.axis_index("x") + lax.axis_index("y")]).astype(jnp.int32)


SUM_ROW_TILES = 2


def _sum2_bf16(pos, fulls, sibs, *, name):
    n = len(fulls)
    nb = SUM_ROW_TILES

    def body(pos_ref, *refs):
        for t in range(n):
            refs[2 * n + t][...] = (refs[t][...] + refs[n + t][...]).astype(BF16)

    in_specs, sib_specs = [], []
    for sb in sibs:
        _, half, cols = sb.shape
        tr = half // nb
        assert half % nb == 0 and tr % 16 == 0, sb.shape
        in_specs.append(pl.BlockSpec((None, tr, cols), lambda j, i, p: (j, p[0] * nb + i, 0)))
        sib_specs.append(pl.BlockSpec((None, tr, cols), lambda j, i, p: (j, i, 0)))
    return pl.pallas_call(
        body, name=name,
        grid_spec=pltpu.PrefetchScalarGridSpec(num_scalar_prefetch=1, grid=(N_CHIPS, nb), in_specs=in_specs + sib_specs,
                                               out_specs=sib_specs),
        out_shape=[jax.ShapeDtypeStruct(sb.shape, BF16) for sb in sibs],
        compiler_params=_cp(("parallel", "parallel")))(pos, *fulls, *sibs)


def _sum5(pos, fulls, sibs, recvs, *, name):
    n = len(fulls)
    nb = SUM_ROW_TILES

    def body(pos_ref, *refs):
        for t in range(n):
            acc = refs[t][...] + refs[n + t][...]
            for kk in range(3):
                acc = acc + refs[2 * n + t][kk].astype(F32)
            refs[3 * n + t][...] = acc

    f_specs, s_specs, r_specs, o_specs = [], [], [], []
    for f in fulls:
        _, rows, cols = f.shape
        tr = rows // 2 // nb
        f_specs.append(pl.BlockSpec((None, tr, cols), lambda i, p: (p[1], p[0] * nb + i, 0)))
        s_specs.append(pl.BlockSpec((None, tr, cols), lambda i, p: (p[1], i, 0)))
        r_specs.append(pl.BlockSpec((3, tr, cols), lambda i, p: (0, i, 0)))
        o_specs.append(pl.BlockSpec((tr, cols), lambda i, p: (p[0] * nb + i, 0)))
    return pl.pallas_call(
        body, name=name,
        grid_spec=pltpu.PrefetchScalarGridSpec(num_scalar_prefetch=1, grid=(nb,), in_specs=f_specs + s_specs + r_specs,
                                               out_specs=o_specs),
        out_shape=[jax.ShapeDtypeStruct(f.shape[1:], F32) for f in fulls],
        compiler_params=_cp(("parallel",)))(pos, *fulls, *sibs, *recvs)


def _place():
    x, y, c = lax.axis_index("x"), lax.axis_index("y"), lax.axis_index("c")
    chips = [(1 - x, y), (x, 1 - y), (1 - x, 1 - y)]
    return x, y, c, 2 * x + y, chips


SEM_SPEC = pl.BlockSpec(memory_space=pltpu.SEMAPHORE)
ANY_SPEC = pl.BlockSpec(memory_space=pl.ANY)


def _gather_copies(ins, outs, send_i, recv_i, send_o, recv_o):
    x, y, c, me, chips = _place()
    n = len(ins)
    started, awaited = [], []
    for t in range(n):
        half = ins[t].shape[0] // 2
        mine = pl.ds(c * half, half)
        started.append(pltpu.make_async_remote_copy(
            src_ref=ins[t], dst_ref=outs[t].at[me], send_sem=send_o.at[t], recv_sem=recv_o.at[t],
            device_id=(x, y, 1 - c), device_id_type=MESH))
        awaited.append(started[-1])
        for kk, (px, py) in enumerate(chips):
            started.append(pltpu.make_async_remote_copy(
                src_ref=ins[t].at[mine], dst_ref=outs[t].at[me, mine], send_sem=send_i.at[t * 3 + kk],
                recv_sem=recv_i.at[t * 3 + kk], device_id=(px, py, c), device_id_type=MESH))
            awaited.append(pltpu.make_async_remote_copy(
                src_ref=ins[t].at[mine], dst_ref=outs[t].at[2 * px + py, mine], send_sem=send_i.at[t * 3 + kk],
                recv_sem=recv_i.at[t * 3 + kk], device_id=(px, py, c), device_id_type=MESH))
    return started, awaited


def _forward_copies(outs, send_d, recv_d):
    x, y, c, me, chips = _place()
    started, awaited = [], []
    for t in range(len(outs)):
        half = outs[t].shape[1] // 2
        for kk, (px, py) in enumerate(chips):
            for lst, hc in ((started, c), (awaited, 1 - c)):
                blk = outs[t].at[2 * px + py, pl.ds(hc * half, half)]
                lst.append(pltpu.make_async_remote_copy(src_ref=blk, dst_ref=blk, send_sem=send_d.at[t * 3 + kk],
                                                        recv_sem=recv_d.at[t * 3 + kk], device_id=(x, y, 1 - c), device_id_type=MESH))
    return started, awaited


def _gather_blocking(shards):
    n = len(shards)

    def body(*refs):
        ins, outs = refs[:n], refs[n:2 * n]
        send_i, recv_i, send_d, recv_d, send_o, recv_o = refs[2 * n:]
        started, awaited = _gather_copies(ins, outs, send_i, recv_i, send_o, recv_o)
        for cp in started:
            cp.start()
        for cp in awaited:
            cp.wait_recv()
        fwd, fwd_in = _forward_copies(outs, send_d, recv_d)
        for cp in fwd:
            cp.start()
        for cp in fwd_in:
            cp.wait_recv()
        for cp in started + fwd:
            cp.wait_send()

    return pl.pallas_call(
        body, name="gather_first", in_specs=[HBM_SPEC] * n, out_specs=[HBM_SPEC] * n,
        out_shape=[jax.ShapeDtypeStruct((N_CHIPS,) + s.shape, s.dtype) for s in shards],
        scratch_shapes=[pltpu.SemaphoreType.DMA((3 * n,)), pltpu.SemaphoreType.DMA((3 * n,)),
                        pltpu.SemaphoreType.DMA((3 * n,)), pltpu.SemaphoreType.DMA((3 * n,)),
                        pltpu.SemaphoreType.DMA((n,)), pltpu.SemaphoreType.DMA((n,))],
    )(*shards)


def _gather_start(shards, after, tag):
    n = len(shards)

    def body(*refs):
        ins = refs[:n]
        send_i, recv_i, send_o, recv_o = refs[2 * n + 1:2 * n + 5]
        outs = refs[3 * n + 5:4 * n + 5]
        token = refs[4 * n + 5]
        started, _ = _gather_copies(ins, outs, send_i, recv_i, send_o, recv_o)
        for cp in started:
            cp.start()
        token[...] = jnp.zeros_like(token)

    lands = [lax.empty((N_CHIPS,) + s.shape, s.dtype) for s in shards]
    sems = [pltpu.SemaphoreType.DMA((3 * n,)), pltpu.SemaphoreType.DMA((3 * n,)), pltpu.SemaphoreType.DMA((n,)), pltpu.SemaphoreType.DMA((n,))]
    res = pl.pallas_call(
        body, name=f"gather_{tag}_start",
        in_specs=[HBM_SPEC] * (2 * n) + [ANY_SPEC],
        out_specs=[SEM_SPEC] * 4 + [HBM_SPEC] * (2 * n) + [pl.BlockSpec(memory_space=pltpu.VMEM)],
        out_shape=sems + [jax.ShapeDtypeStruct(s.shape, s.dtype) for s in shards]
        + [jax.ShapeDtypeStruct(a.shape, a.dtype) for a in lands] + [jax.ShapeDtypeStruct((8, 128), F32)],
        input_output_aliases={t: 4 + t for t in range(2 * n)},
        compiler_params=pltpu.CompilerParams(has_side_effects=pltpu.SideEffectType.DATAFLOW_SIDE_EFFECTING),
    )(*[pltpu.with_memory_space_constraint(s, pltpu.HBM) for s in shards],
      *[pltpu.with_memory_space_constraint(a, pltpu.HBM) for a in lands], after)
    return res[:4], res[4:4 + n], res[4 + n:4 + 2 * n], res[-1]


def _gather_wait(sems, shards_thru, lands_thru, after, tag):
    n = len(shards_thru)

    def body(*refs):
        ins, outs_in = refs[:n], refs[n:2 * n]
        send_i, recv_i, send_o, recv_o = refs[2 * n:2 * n + 4]
        started, awaited = _gather_copies(ins, outs_in, send_i, recv_i, send_o, recv_o)
        for cp in started:
            cp.wait_send()
        for cp in awaited:
            cp.wait_recv()

    res = pl.pallas_call(
        body, name=f"gather_{tag}_wait",
        in_specs=[HBM_SPEC] * (2 * n) + [SEM_SPEC] * 4 + [ANY_SPEC],
        out_specs=[HBM_SPEC] * (2 * n),
        out_shape=[jax.ShapeDtypeStruct(a.shape, a.dtype) for a in list(shards_thru) + list(lands_thru)],
        input_output_aliases={t: t for t in range(2 * n)},
        compiler_params=pltpu.CompilerParams(has_side_effects=pltpu.SideEffectType.DATAFLOW_SIDE_EFFECTING),
    )(*shards_thru, *lands_thru, *sems, after)
    return res[n:]


def _gather_forward(lands, tag):
    n = len(lands)

    def body(*refs):
        outs = refs[n:2 * n]
        send_d, recv_d = refs[2 * n:]
        fwd, fwd_in = _forward_copies(outs, send_d, recv_d)
        for cp in fwd:
            cp.start()
        for cp in fwd_in:
            cp.wait_recv()
        for cp in fwd:
            cp.wait_send()

    return pl.pallas_call(
        body, name=f"gather_{tag}_forward", in_specs=[HBM_SPEC] * n, out_specs=[HBM_SPEC] * n,
        out_shape=[jax.ShapeDtypeStruct(a.shape, a.dtype) for a in lands],
        input_output_aliases={t: t for t in range(n)},
        scratch_shapes=[pltpu.SemaphoreType.DMA((3 * n,)), pltpu.SemaphoreType.DMA((3 * n,))],
    )(*lands)


def _stage1_copies(ins, sib, send, recv):
    x, y, c, me, chips = _place()
    cps = []
    for t in range(len(ins)):
        rows = ins[t].shape[1] // 2
        cps.append(pltpu.make_async_remote_copy(
            src_ref=ins[t].at[:, pl.ds((1 - c) * rows, rows), :], dst_ref=sib[t], send_sem=send.at[t],
            recv_sem=recv.at[t], device_id=(x, y, 1 - c), device_id_type=MESH))
    return cps


def _split_start(copies_fn, srcs, land_shapes, n_sems, tag):
    n = len(srcs)

    def body(*refs):
        send, recv = refs[2 * n:2 * n + 2]
        for cp in copies_fn(refs[:n], refs[3 * n + 2:4 * n + 2], send, recv):
            cp.start()
        refs[4 * n + 2][...] = jnp.zeros_like(refs[4 * n + 2])

    lands = [lax.empty(shp, dt) for shp, dt in land_shapes]
    res = pl.pallas_call(
        body, name=tag,
        in_specs=[HBM_SPEC] * (2 * n),
        out_specs=[SEM_SPEC] * 2 + [HBM_SPEC] * (2 * n) + [pl.BlockSpec(memory_space=pltpu.VMEM)],
        out_shape=[pltpu.SemaphoreType.DMA((n_sems,)), pltpu.SemaphoreType.DMA((n_sems,))]
        + [jax.ShapeDtypeStruct(p.shape, p.dtype) for p in srcs]
        + [jax.ShapeDtypeStruct(a.shape, a.dtype) for a in lands] + [jax.ShapeDtypeStruct((8, 128), F32)],
        input_output_aliases={t: 2 + t for t in range(2 * n)},
        compiler_params=pltpu.CompilerParams(has_side_effects=pltpu.SideEffectType.DATAFLOW_SIDE_EFFECTING),
    )(*[pltpu.with_memory_space_constraint(p, pltpu.HBM) for p in srcs],
      *[pltpu.with_memory_space_constraint(a, pltpu.HBM) for a in lands])
    return res[:2], res[2:2 + n], res[2 + n:2 + 2 * n], res[-1]


def _split_wait(copies_fn, sems, srcs_thru, lands_thru, after, tag):
    n = len(srcs_thru)

    def body(*refs):
        for cp in copies_fn(refs[:n], refs[n:2 * n], refs[2 * n], refs[2 * n + 1]):
            cp.wait()

    res = pl.pallas_call(
        body, name=tag,
        in_specs=[HBM_SPEC] * (2 * n) + [SEM_SPEC] * 2 + [ANY_SPEC],
        out_specs=[HBM_SPEC] * (2 * n),
        out_shape=[jax.ShapeDtypeStruct(a.shape, a.dtype) for a in list(srcs_thru) + list(lands_thru)],
        input_output_aliases={t: t for t in range(2 * n)},
        compiler_params=pltpu.CompilerParams(has_side_effects=pltpu.SideEffectType.DATAFLOW_SIDE_EFFECTING),
    )(*srcs_thru, *lands_thru, *sems, after)
    return res[:n], res[n:]


def _stage2_copies(ps, rcv, send, recv):
    x, y, c, me, chips = _place()
    return [pltpu.make_async_remote_copy(
        src_ref=ps[t].at[2 * px + py], dst_ref=rcv[t].at[kk], send_sem=send.at[t * 3 + kk],
        recv_sem=recv.at[t * 3 + kk], device_id=(px, py, c), device_id_type=MESH)
        for t in range(len(ps)) for kk, (px, py) in enumerate(chips)]


def _reduce_stage3(reduced, tag):
    n = len(reduced)

    def body(*refs):
        outs = refs[n:2 * n]
        send, recv = refs[2 * n:]
        x, y, c, me, chips = _place()
        cps = []
        for t in range(n):
            rows = outs[t].shape[0] // 2
            mine = outs[t].at[pl.ds(c * rows, rows), :]
            cp = pltpu.make_async_remote_copy(src_ref=mine, dst_ref=mine, send_sem=send.at[t], recv_sem=recv.at[t],
                                              device_id=(x, y, 1 - c), device_id_type=MESH)
            cp.start()
            cps.append(cp)
        for cp in cps:
            cp.wait()

    return pl.pallas_call(
        body, name="reduce_stage3_" + tag, in_specs=[HBM_SPEC] * n, out_specs=[HBM_SPEC] * n,
        out_shape=[jax.ShapeDtypeStruct(r.shape, r.dtype) for r in reduced],
        input_output_aliases={t: t for t in range(n)},
        scratch_shapes=[pltpu.SemaphoreType.DMA((n,)), pltpu.SemaphoreType.DMA((n,))],
    )(*reduced)


def _allreduce_small(v):
    rows, cols = v.shape

    def body(v_ref, o_ref, buf, send, recv, loc):
        x, y, c, me, chips = _place()
        mine = 4 * x + 2 * y + c
        lc = pltpu.make_async_copy(v_ref, buf.at[mine], loc)
        lc.start()
        peers = []
        for fx in range(2):
            for fy in range(2):
                for fc in range(2):
                    if fx or fy or fc:
                        peers.append((fx, fy, fc))
        cps = []
        for kk, (fx, fy, fc) in enumerate(peers):
            to = (x ^ fx, y ^ fy, c ^ fc)
            cp = pltpu.make_async_remote_copy(src_ref=v_ref, dst_ref=buf.at[mine], send_sem=send.at[kk], recv_sem=recv.at[kk],
                                              device_id=to, device_id_type=MESH)
            cp.start()
            cps.append((cp, to))
        for kk, (cp, to) in enumerate(cps):
            src = 4 * to[0] + 2 * to[1] + to[2]
            pltpu.make_async_remote_copy(src_ref=v_ref, dst_ref=buf.at[src], send_sem=send.at[kk], recv_sem=recv.at[kk],
                                         device_id=to, device_id_type=MESH).wait_recv()
        for cp, _ in cps:
            cp.wait_send()
        lc.wait()
        acc = buf[0]
        for d in range(1, 8):
            acc = acc + buf[d]
        o_ref[...] = acc

    return pl.pallas_call(
        body, name="allreduce_small", in_specs=[pl.BlockSpec(memory_space=pltpu.VMEM)],
        out_specs=pl.BlockSpec(memory_space=pltpu.VMEM), out_shape=jax.ShapeDtypeStruct((rows, cols), F32),
        scratch_shapes=[pltpu.VMEM((8, rows, cols), F32), pltpu.SemaphoreType.DMA((7,)), pltpu.SemaphoreType.DMA((7,)),
                        pltpu.SemaphoreType.DMA],
        compiler_params=pltpu.CompilerParams(vmem_limit_bytes=VMEM_LIMIT_V7X),
    )(v)


def _pad_w_in(w):
    z = lambda n: jnp.zeros(w.shape[:-1] + (n,), w.dtype)
    return jnp.concatenate([w[..., 0:384], z(64), w[..., 384:416], z(32), w[..., 416:1824],
                            w[..., 1824:1830], z(58), w[..., 400:416], w[..., 384:400], z(32)], axis=-1)


def _unpad_w_in(g):
    x1 = g[..., 448:464] + g[..., Z_F + 80:Z_F + 96]
    x2 = g[..., 464:480] + g[..., Z_F + 64:Z_F + 80]
    return jnp.concatenate([g[..., 0:384], x1, x2, g[..., 512:1920], g[..., 1920:1926]], axis=-1)


def _block_diag(pw):
    out = jnp.zeros((POOL_W, POOL_W), pw.dtype)
    for g in range(4):
        out = out.at[g * 64:(g + 1) * 64, g * 64:(g + 1) * 64].set(pw[g])
    return out


def _rope_tables(s):
    inv_freq = ROPE_THETA ** (-jnp.arange(0, ROPE, 2, dtype=F32) / ROPE)
    ang = jnp.arange(s, dtype=jnp.int32).astype(F32)[:, None] * inv_freq[None, :]
    cos, sin = jnp.cos(ang), jnp.sin(ang)
    zero = lambda n: jnp.zeros((s, n), F32)
    ck = jnp.concatenate([zero(NOPE), cos, cos, zero(DA - NOPE - ROPE)], axis=1)
    sk = jnp.concatenate([zero(NOPE), -sin, sin, zero(DA - NOPE - ROPE)], axis=1)
    cq = jnp.concatenate([jnp.ones((s, NOPE), F32), cos, cos, zero(DA - NOPE - ROPE)], axis=1) * SCALE_MLA
    return dict(cq=cq, sq=sk * SCALE_MLA, ck=ck, sk=sk)


def _mix_fwd(l, x1, wts, sm, tabs):
    z, h2, qn, kvn, qa, ka, va = _mix_in(x1, (sm["mix_norm"][l], sm["q_a_norm"][l], sm["kv_a_norm"][l]), wts["w_in"][l],
                                         wts["wq_a"][l], wts["wq_b"][l], wts["wk"][l], wts["wv"][l], tabs, name=f"mix_in_{l}")
    oa, lse_a = _attn_fwd(qa, ka, va, VDIM, name=f"mla_attn_{l}")

    bd = _block_diag(wts["pool_w"][l]).astype(BF16)
    yb, pooled = _pool_fwd(z, Z_POOL // POOL_W, bd, sm["pool_scale"][l], name=f"pool_{l}")

    fb = jnp.pad(sm["fox_b_f"][l], (0, 8 - H)).reshape(8, 1)
    ft, c3t = _gate_fwd(z, Z_F // DA, fb, name=f"fox_gate_{l}")
    fqa, fka, fva = _fox_prep(z, c3t, name=f"fox_prep_{l}")
    oc, lse_c = _attn_fwd(fqa, fka, fva, FOX_D, name=f"fox_attn_{l}")

    x2, cat = _mix_out(oa, yb, oc, wts["w_out"][l], x1, name=f"mix_out_{l}")
    saved = dict(z=z, h2=h2, qn=qn, kvn=kvn, qa=qa, ka=ka, va=va, oa=oa, lse_a=lse_a, bd=bd, pooled=pooled,
                 fqa=fqa, fka=fka, fva=fva, ft=ft, fb=fb, oc=oc, lse_c=lse_c, cat=cat)
    return x2, saved


def _mix_bwd(l, x1, dx2, sv, wts, sm, tabs, tok=None):
    s = x1.shape[0]
    g = {}
    dx2b = (dx2 if tok is None else dx2 + tok).astype(BF16)
    doa, doc, dyb, dl_a, dl_c, g["w_out"] = _mix_out_bwd(dx2b, wts["w_out"][l], sv["cat"], sv["oa"], sv["oc"],
                                                         name=f"mix_out_bwd_{l}")

    dfqa, dfka, dfva, dcq, dck = _attn_bwd(sv["fqa"], sv["fka"], sv["fva"], doc, sv["lse_c"], dl_c, True, name=f"fox_attn_bwd_{l}")
    dfox = _fox_bwd_prep(dfqa, dfka, dfva, name=f"fox_bwd_prep_{l}")
    dc = jnp.pad(dcq.reshape(H, s) + dck.reshape(H, s), ((0, 8 - H), (0, 0)))
    dft, dfb = _gate_bwd(sv["ft"], sv["fb"], dc, name=f"fox_gate_bwd_{l}")
    g["fox_b_f"] = dfb[:H, 0]

    dq, dbd, g["pool_scale"] = _pool_bwd_a(dyb, sv["pooled"], sv["bd"], sm["pool_scale"][l], name=f"pool_bwd_a_{l}")
    du = _pool_bwd_b(dq, name=f"pool_bwd_b_{l}")
    g["pool_w"] = jnp.stack([dbd[i * 64:(i + 1) * 64, i * 64:(i + 1) * 64] for i in range(4)])

    dqa_, dka_, dva_ = _attn_bwd(sv["qa"], sv["ka"], sv["va"], doa, sv["lse_a"], dl_a, False, name=f"mla_attn_bwd_{l}")
    dqab, dkv, dz3, dz15, dwq, dwkv = _mla_bwd_prep(dqa_, dka_, dva_, dft, sv["qn"], sv["kvn"], tabs["cq"], tabs["sq"],
                                                    tabs["ck"], tabs["sk"], name=f"mla_bwd_prep_{l}")
    wq_ab = jnp.concatenate([wts["wq_a"][l], wts["wq_b"][l]], axis=1)
    wkv = jnp.concatenate([wts["wk"][l], wts["wv"][l]], axis=1)
    dwq = dwq.reshape(Q_RANK, 2, H, DA)
    dwkv = dwkv.reshape(KV_RANK, 2, H, DA)
    da, db = dwq[:, 0], dwq[:, 1]
    swapped = jnp.concatenate([jnp.zeros((Q_RANK, H, NOPE), F32), db[..., NOPE + HALF_ROPE:NOPE + ROPE],
                               db[..., NOPE:NOPE + HALF_ROPE]], axis=-1)
    g["w_q_b"] = (da[..., :NOPE + ROPE] + swapped).reshape(Q_RANK, H * (NOPE + ROPE))
    g["w_kv_b"] = jnp.concatenate([dwkv[:, 0, :, :NOPE], dwkv[:, 1, :, :VDIM]], axis=-1).reshape(KV_RANK, H * (NOPE + VDIM))
    dqa, g["q_a_norm"] = _rmsnorm_bwd(sv["z"], Z_QA // Q_RANK, sm["q_a_norm"][l], dqab, wq_ab, name=f"q_a_norm_bwd_{l}")
    dkva, g["kv_a_norm"] = _rmsnorm_bwd(sv["z"], Z_KVA // KV_RANK, sm["kv_a_norm"][l], dkv, wkv, name=f"kv_a_norm_bwd_{l}")

    dz = jnp.concatenate([dqa.astype(BF16), dkva.astype(BF16), dz3, du.astype(BF16), dfox, dz15], axis=1)
    g["w_in"] = _mm(sv["h2"], dz, "tn", name=f"d_w_in_{l}", tm=1024, tn=1024, tk=DW_TOKENS)
    dx1, g["mix_norm"] = _rmsnorm_bwd(x1, 0, sm["mix_norm"][l], dz, wts["w_in"][l], dx2, name=f"mix_norm_bwd_{l}")
    return dx1, g


DW_TOKENS = 2048


def _local_step(x, target, wts, sm, late_weights=None, grads_ready=None):
    s = x.shape[0]
    tabs = _rope_tables(s)
    acts = []
    xs = x
    for l in range(DEPTH):
        x1, gu1, act1 = _ffn_fwd(xs, sm["ffn1_norm"][l], wts["ffn1_w_gu"][l], wts["ffn1_w_d2"][l], name=f"ffn1_fwd_{l}")
        if l == 0 and late_weights is not None:
            sm = late_weights("ffn1", x1, sm)
        x2, sv = _mix_fwd(l, x1, wts, sm, tabs)
        if l == 0 and late_weights is not None:
            sm = late_weights("mix", x2, sm)
        x3, gu2, act2 = _ffn_fwd(x2, sm["ffn2_norm"][l], wts["ffn2_w_gu"][l], wts["ffn2_w_d2"][l], name=f"ffn2_fwd_{l}")
        acts.append((xs, gu1, act1, x1, sv, x2, gu2, act2))
        xs = x3
    dx, g_final, loss = _loss_head(xs, sm["final_norm"], target, name="loss_head")
    grads = [dict() for _ in range(DEPTH)]
    for l in reversed(range(DEPTH)):
        x0, gu1, act1, x1, sv, x2, gu2, act2 = acts[l]
        g = grads[l]
        dx, dgu, hh, dy, g["ffn2_norm"] = _ffn_bwd(x2, dx, gu2, sm["ffn2_norm"][l], wts["ffn2_w_gu"][l], wts["ffn2_w_d2"][l],
                                                   name=f"ffn2_bwd_{l}")
        g["ffn2_w_down"] = _mm(act2, dy, "tn", name=f"d_ffn2_w_down_{l}", tm=FF_SHARD, tn=1024, tk=DW_TOKENS)
        g["ffn2_w_gu"] = _mm(hh, dgu, "tn", name=f"d_ffn2_w_gu_{l}", tm=1024, tn=FF_SHARD, tk=DW_TOKENS, n_major_out=True)
        tok = None
        if grads_ready is not None:
            sm, tok = grads_ready(l, "ffn2", g, sm)
        dx, gm = _mix_bwd(l, x1, dx, sv, wts, sm, tabs, tok)
        g.update(gm)
        if grads_ready is not None:
            sm, _ = grads_ready(l, "mix", g, sm)
        dx, dgu, hh, dy, g["ffn1_norm"] = _ffn_bwd(x0, dx, gu1, sm["ffn1_norm"][l], wts["ffn1_w_gu"][l], wts["ffn1_w_d2"][l],
                                                   name=f"ffn1_bwd_{l}")
        if grads_ready is not None:
            sm, tok = grads_ready(l, "ffn1_tokens", {"dx": dx}, sm)
            if tok is not None:
                dy = dy + tok.astype(BF16)
        g["ffn1_w_down"] = _mm(act1, dy, "tn", name=f"d_ffn1_w_down_{l}", tm=FF_SHARD, tn=1024, tk=DW_TOKENS)
        g["ffn1_w_gu"] = _mm(hh, dgu, "tn", name=f"d_ffn1_w_gu_{l}", tm=1024, tn=FF_SHARD, tk=DW_TOKENS, n_major_out=True)
        if grads_ready is not None:
            sm, _ = grads_ready(l, "ffn1", g, sm)
    return loss, dx, grads, g_final


BIG = ["ffn1_w_gu", "ffn1_w_down", "w_in", "w_q_b", "w_kv_b", "w_out", "ffn2_w_gu", "ffn2_w_down"]
SMALL = ["ffn1_norm", "mix_norm", "q_a_norm", "kv_a_norm", "pool_w", "pool_scale", "fox_b_f", "ffn2_norm"]
SMALL_ROWS = 48


WEIGHT_VIEWS = ["ffn1_w_gu", "ffn1_w_d2", "w_in", "wq_a", "wq_b", "wk", "wv", "w_out", "ffn2_w_gu", "ffn2_w_d2"]


def _prepare_weights(gathered, wts):
    for (nm, l), w in gathered.items():
        if nm in ("ffn1_w_gu", "ffn2_w_gu"):
            wts[nm][l] = w
        elif nm in ("ffn1_w_down", "ffn2_w_down"):
            wts[nm[:5] + "w_d2"][l] = w.reshape(2, FF_SHARD, D)
        elif nm in ("w_in", "w_out"):
            wts[nm][l] = w.reshape(D, -1)
        elif nm == "w_q_b":
            wq = jnp.moveaxis(w, 0, 1).reshape(Q_RANK, H, NOPE + ROPE)
            zq = lambda n: jnp.zeros((Q_RANK, H, n), BF16)
            wts["wq_a"][l] = jnp.concatenate([wq, zq(DA - NOPE - ROPE)], axis=-1).reshape(Q_RANK, H * DA)
            wts["wq_b"][l] = jnp.concatenate([zq(NOPE), wq[..., NOPE + HALF_ROPE:], wq[..., NOPE:NOPE + HALF_ROPE],
                                              zq(DA - NOPE - ROPE)], axis=-1).reshape(Q_RANK, H * DA)
        else:
            wkv = jnp.moveaxis(w, 0, 1).reshape(KV_RANK, H, NOPE + VDIM)
            zk = jnp.zeros((KV_RANK, H, DA - NOPE), BF16)
            wts["wk"][l] = jnp.concatenate([wkv[..., :NOPE], zk], axis=-1).reshape(KV_RANK, H * DA)
            wts["wv"][l] = jnp.concatenate([wkv[..., NOPE:], zk], axis=-1).reshape(KV_RANK, H * DA)


def _chip_major(name, g):
    if name in ("ffn1_w_gu", "ffn2_w_gu"):
        return g
    if name in ("ffn1_w_down", "ffn2_w_down", "w_in", "w_out"):
        return g.reshape(N_CHIPS, g.shape[0] // N_CHIPS, g.shape[1])
    return jnp.moveaxis(g.reshape(g.shape[0], N_CHIPS, g.shape[1] // N_CHIPS), 1, 0)


def _pack_small(grads, g_final, loss):
    parts = []
    for l in range(DEPTH):
        for nm in SMALL:
            parts.append(grads[l][nm].reshape(-1))
    parts.append(g_final.reshape(-1))
    parts.append(loss.reshape(1))
    flat = jnp.concatenate(parts)
    return jnp.pad(flat, (0, SMALL_ROWS * D - flat.shape[0])).reshape(SMALL_ROWS, D)


def _unpack_small(packed, params):
    flat = packed.reshape(-1)
    out = {nm: [] for nm in SMALL}
    off = 0
    for l in range(DEPTH):
        for nm in SMALL:
            shp = params[nm].shape[1:]
            n = int(np.prod(shp))
            out[nm].append(flat[off:off + n].reshape(shp))
            off += n
    res = {nm: jnp.stack(v) for nm, v in out.items()}
    res["final_norm"] = flat[off:off + D]
    return res, flat[off + D]


def _update(name, w, g, m, v):
    shp = w.shape
    if w.ndim == 1:
        view = (1, shp[0])
    elif w.size <= 65536:
        view = (shp[0], w.size // shp[0])
    else:
        view = (w.size // shp[-1], shp[-1])
    tr = view[0]
    for cand in (512, 352, 256, 128):
        if view[0] % cand == 0 and view[0] > cand:
            tr = cand
            break
    d, mn, vn = _adamw(w.reshape(view), g.reshape(view), m.reshape(view), v.reshape(view), name="adamw_" + name, tr=tr)
    return d.reshape(shp), mn.reshape(shp), vn.reshape(shp)


WEIGHTS = ['ffn1_norm', 'ffn1_w_gu', 'ffn1_w_down', 'mix_norm', 'w_in', 'q_a_norm', 'w_q_b', 'kv_a_norm', 'w_kv_b', 'pool_w',
           'pool_scale', 'fox_b_f', 'w_out', 'ffn2_norm', 'ffn2_w_gu', 'ffn2_w_down', 'final_norm']


def kernel(x, ffn1_norm, ffn1_w_gu, ffn1_w_down, mix_norm, w_in, q_a_norm, w_q_b, kv_a_norm, w_kv_b, pool_w, pool_scale, fox_b_f, w_out, ffn2_norm, ffn2_w_gu, ffn2_w_down, final_norm, loss_target, m_ffn1_norm, m_ffn1_w_gu, m_ffn1_w_down, m_mix_norm, m_w_in, m_q_a_norm, m_w_q_b, m_kv_a_norm, m_w_kv_b, m_pool_w, m_pool_scale, m_fox_b_f, m_w_out, m_ffn2_norm, m_ffn2_w_gu, m_ffn2_w_down, m_final_norm, v_ffn1_norm, v_ffn1_w_gu, v_ffn1_w_down, v_mix_norm, v_w_in, v_q_a_norm, v_w_q_b, v_kv_a_norm, v_w_kv_b, v_pool_w, v_pool_scale, v_fox_b_f, v_w_out, v_ffn2_norm, v_ffn2_w_gu, v_ffn2_w_down, v_final_norm):
    params = dict(ffn1_norm=ffn1_norm, ffn1_w_gu=ffn1_w_gu, ffn1_w_down=ffn1_w_down, mix_norm=mix_norm, w_in=w_in, q_a_norm=q_a_norm,
                  w_q_b=w_q_b, kv_a_norm=kv_a_norm, w_kv_b=w_kv_b, pool_w=pool_w, pool_scale=pool_scale, fox_b_f=fox_b_f, w_out=w_out,
                  ffn2_norm=ffn2_norm, ffn2_w_gu=ffn2_w_gu, ffn2_w_down=ffn2_w_down, final_norm=final_norm)
    mom = dict(ffn1_norm=m_ffn1_norm, ffn1_w_gu=m_ffn1_w_gu, ffn1_w_down=m_ffn1_w_down, mix_norm=m_mix_norm, w_in=m_w_in,
               q_a_norm=m_q_a_norm, w_q_b=m_w_q_b, kv_a_norm=m_kv_a_norm, w_kv_b=m_w_kv_b, pool_w=m_pool_w, pool_scale=m_pool_scale,
               fox_b_f=m_fox_b_f, w_out=m_w_out, ffn2_norm=m_ffn2_norm, ffn2_w_gu=m_ffn2_w_gu, ffn2_w_down=m_ffn2_w_down,
               final_norm=m_final_norm)
    var = dict(ffn1_norm=v_ffn1_norm, ffn1_w_gu=v_ffn1_w_gu, ffn1_w_down=v_ffn1_w_down, mix_norm=v_mix_norm, w_in=v_w_in,
               q_a_norm=v_q_a_norm, w_q_b=v_w_q_b, kv_a_norm=v_kv_a_norm, w_kv_b=v_w_kv_b, pool_w=v_pool_w, pool_scale=v_pool_scale,
               fox_b_f=v_fox_b_f, w_out=v_w_out, ffn2_norm=v_ffn2_norm, ffn2_w_gu=v_ffn2_w_gu, ffn2_w_down=v_ffn2_w_down,
               final_norm=v_final_norm)

    first = [("ffn1_w_gu", 0), ("ffn1_w_down", 0)]
    mix0 = [(nm, 0) for nm in ("w_in", "w_q_b", "w_kv_b", "w_out")]
    rest = [(nm, l) for nm in BIG for l in range(DEPTH) if (nm, l) not in first + mix0]

    def shards(keys, zero=0.0):
        return [((_pad_w_in(params[nm]) if nm == "w_in" else params[nm])[l] + zero).astype(BF16) for nm, l in keys]

    wts = {nm: [None] * DEPTH for nm in WEIGHT_VIEWS}
    wts["pool_w"] = params["pool_w"]
    got = _gather_blocking(shards(first))
    _prepare_weights(dict(zip(first, got)), wts)
    sems_m, src_m, land_m, token_m = _gather_start(shards(mix0), got[0], "mix0")
    sm = dict(params)
    sm["ffn1_norm"] = params["ffn1_norm"] + token_m[0, 0]
    rest_shards = shards(rest, token_m[0, 0])
    flying = {}

    def late_weights(stage, act, sm_now):
        if stage == "ffn1":
            lands = _gather_forward(_gather_wait(sems_m, src_m, land_m, act, "mix0"), "mix0")
            _prepare_weights(dict(zip(mix0, lands)), wts)
            flying["rest"] = _gather_start(rest_shards, lands[0], "rest")
            sm_next = dict(sm_now)
            sm_next["mix_norm"] = sm_now["mix_norm"] + flying["rest"][3][0, 0]
            return sm_next
        sems_r, src_r, land_r, _ = flying["rest"]
        lands = _gather_forward(_gather_wait(sems_r, src_r, land_r, act, "rest"), "rest")
        _prepare_weights(dict(zip(rest, lands)), wts)
        return sm_now

    pos = _position()
    flight = {}

    groups = {"l1": (1, BIG), "l0a": (0, [nm for nm in BIG if not nm.startswith("ffn1")]),
              "l0b": (0, [nm for nm in BIG if nm.startswith("ffn1")])}
    pending = {}

    def to_chips(key, full, sib):
        psum = _sum2_bf16(pos, full, sib, name=f"chip_sum_{key}")
        s2 = _split_start(_stage2_copies, psum, [((3,) + p.shape[1:], p.dtype) for p in psum], 3 * len(psum),
                          f"reduce_stage2_start_{key}")
        flight[key] = (full, sib, s2)
        return s2[3][0, 0]

    def grads_ready(l, stage, g, sm_now):
        behind, tok = None, None
        if (l, stage) == (1, "ffn1"):
            full = [_chip_major(nm, g[nm]) for nm in BIG]
            pending["l1"] = _split_start(_stage1_copies, full, [((N_CHIPS, f.shape[1] // 2, f.shape[2]), F32) for f in full],
                                         len(full), "reduce_stage1_start_l1")
            behind, tok = "ffn2_norm", pending["l1"][3][0, 0]
        elif (l, stage) == (0, "ffn2"):
            sems1, full_thru, sib_land, _ = pending["l1"]
            full, sib = _split_wait(_stage1_copies, sems1, full_thru, sib_land, g["ffn2_w_down"], "reduce_stage1_wait_l1")
            tok = to_chips("l1", full, sib)
        elif (l, stage) == (0, "mix"):
            full = [_chip_major(nm, g[nm]) for nm in groups["l0a"][1]]
            pending["l0a"] = _split_start(_stage1_copies, full, [((N_CHIPS, f.shape[1] // 2, f.shape[2]), F32) for f in full],
                                          len(full), "reduce_stage1_start_l0a")
            behind, tok = "ffn1_norm", pending["l0a"][3][0, 0]
        elif (l, stage) == (0, "ffn1_tokens"):
            sems1, full_thru, sib_land, _ = pending["l0a"]
            full, sib = _split_wait(_stage1_copies, sems1, full_thru, sib_land, g["dx"], "reduce_stage1_wait_l0a")
            tok = to_chips("l0a", full, sib)
        elif (l, stage) == (0, "ffn1"):
            full = [_chip_major(nm, g[nm]) for nm in groups["l0b"][1]]
            pending["l0b"] = _split_start(_stage1_copies, full, [((N_CHIPS, f.shape[1] // 2, f.shape[2]), F32) for f in full],
                                          len(full), "reduce_stage1_start_l0b")
        if behind is None:
            return sm_now, tok
        sm_next = dict(sm_now)
        sm_next[behind] = sm_now[behind] + tok
        return sm_next, tok

    loss, dx, grads, g_final = _local_step(x[0], loss_target[0], wts, sm, late_weights, grads_ready)

    def view2d(a):
        return a.reshape(a.size // a.shape[-1], a.shape[-1])

    after = pending["l0b"][3]
    done = {nm: None for nm in BIG}
    for key in ("l1", "l0a", "l0b"):
        l, names = groups[key]
        full, sib, (sems2, ps_thru, lands2, _) = flight[key]
        _, recv = _split_wait(_stage2_copies, sems2, ps_thru, lands2, after, f"reduce_stage2_wait_{key}")
        whole = _reduce_stage3(_sum5(pos, full, sib, recv, name=f"grad_sum_{key}"), key)
        for nm, g_l in zip(names, whole):
            if nm == "w_in":
                g_l = _unpad_w_in(g_l)
            tr = max(t for t in (512, 352, 256, 128) if g_l.shape[0] % t == 0)
            done[nm] = _adamw_layer(view2d(params[nm]), g_l, view2d(mom[nm]), view2d(var[nm]), l, done[nm],
                                    name=f"adamw_{nm}_{l}", tr=tr)
        after = done[names[-1]][0][-8:, 0:128]
        if key == "l1":
            small_g, loss = _unpack_small(_allreduce_small(_pack_small(grads, g_final, loss)), params)
            sems1, full_thru, sib_land, _ = pending["l0b"]
            full_b, sib_b = _split_wait(_stage1_copies, sems1, full_thru, sib_land, after + small_g["final_norm"][0],
                                        "reduce_stage1_wait_l0b")
            after = after + to_chips("l0b", full_b, sib_b)
    gw, delta, new_m, new_v = dict(small_g), {}, {}, {}
    for nm in BIG:
        delta[nm], new_m[nm], new_v[nm], gw[nm] = [a.reshape(params[nm].shape) for a in done[nm]]
    for nm in small_g:
        delta[nm], new_m[nm], new_v[nm] = _update(nm, params[nm], gw[nm], mom[nm], var[nm])
    return (loss, dx[None], *[gw[n] for n in WEIGHTS], *[delta[n] for n in WEIGHTS], *[new_m[n] for n in WEIGHTS],
            *[new_v[n] for n in WEIGHTS])
```

```python
import functools
import math

import jax
import jax.numpy as jnp
import numpy as np
from jax import lax
from jax.experimental import pallas as pl
from jax.experimental.pallas import tpu as pltpu

F32 = jnp.float32
BF16 = jnp.bfloat16
MESH = pl.DeviceIdType.MESH
HBM_SPEC = pl.BlockSpec(memory_space=pltpu.HBM)

D = 1024
DEPTH = 2
D_FF = 2816
FF_SHARD = 1408
N_CHIPS = 4
H = 6
NOPE, ROPE, VDIM = 64, 32, 64
HALF_ROPE = ROPE // 2
Q_RANK, KV_RANK = 256, 128
POOL_W = 256
FOX_D = 64
N_IN = 1830
NZ = 2048
ROPE_THETA = 10000.0
EPS = 1e-6
POOL_HALO = 16
Z_QA, Z_KVA, Z_KR, Z_POOL, Z_FOX, Z_F = 0, 256, 384, 512, 768, 1920

ADAM_LR, ADAM_B1, ADAM_B2, ADAM_EPS, ADAM_WD, ADAM_STEP = 0.001, 0.9, 0.999, 1e-08, 0.01, 10

VMEM_LIMIT_V7X = 56 * 1024 * 1024


def _cp(sem=None, vmem=VMEM_LIMIT_V7X):
    return pltpu.CompilerParams(dimension_semantics=sem, vmem_limit_bytes=vmem)


def _sigmoid(x):
    return 0.5 * jnp.tanh(0.5 * x) + 0.5


def _dot(a, b, dims):
    return lax.dot_general(a, b, (dims, ((), ())), preferred_element_type=F32)


NN = ((1,), (0,))
NT = ((1,), (1,))
TN = ((0,), (0,))


def _mm(a, b, mode, *, name, out_dtype=F32, add=None, alpha=None, tm=512, tn=512, tk=512, n_major_out=False):
    if mode == "nn":
        (m, k), (k2, n) = a.shape, b.shape
    elif mode == "nt":
        (m, k), (n, k2) = a.shape, b.shape
    else:
        (k, m), (k2, n) = a.shape, b.shape
    assert k == k2
    tm, tn, tk = min(tm, m), min(tn, n), min(tk, k)
    assert m % tm == 0 and n % tn == 0 and k % tk == 0, (name, m, n, k, tm, tn, tk)
    nk = k // tk
    dims = {"nn": NN, "nt": NT, "tn": TN}[mode]
    a_spec = pl.BlockSpec((tk, tm), lambda i, j, kk: (kk, i)) if mode == "tn" else pl.BlockSpec((tm, tk), lambda i, j, kk: (i, kk))
    b_spec = pl.BlockSpec((tn, tk), lambda i, j, kk: (j, kk)) if mode == "nt" else pl.BlockSpec((tk, tn), lambda i, j, kk: (kk, j))
    in_specs = [a_spec, b_spec]
    args = [a, b]
    if add is not None:
        in_specs.append(pl.BlockSpec((tm, tn), lambda i, j, kk: (i, j)))
        args.append(add)
    if n_major_out:
        out_shape = jax.ShapeDtypeStruct((n // tn, m, tn), out_dtype)
        out_spec = pl.BlockSpec((None, tm, tn), lambda i, j, kk: (j, i, 0))
    else:
        out_shape = jax.ShapeDtypeStruct((m, n), out_dtype)
        out_spec = pl.BlockSpec((tm, tn), lambda i, j, kk: (i, j))

    def body(*refs):
        a_ref, b_ref = refs[0], refs[1]
        add_ref = refs[2] if add is not None else None
        o_ref, acc = refs[-2], refs[-1]
        kk = pl.program_id(2)

        @pl.when(kk == 0)
        def _():
            acc[...] = jnp.zeros_like(acc)

        acc[...] += _dot(a_ref[...].astype(BF16), b_ref[...].astype(BF16), dims)

        @pl.when(kk == nk - 1)
        def _():
            r = acc[...]
            if alpha is not None:
                r = r * alpha
            if add_ref is not None:
                r = r + add_ref[...].astype(F32)
            o_ref[...] = r.astype(out_dtype)

    return pl.pallas_call(
        body, name=name, grid=(m // tm, n // tn, nk), in_specs=in_specs, out_specs=out_spec, out_shape=out_shape,
        scratch_shapes=[pltpu.VMEM((tm, tn), F32)],
        compiler_params=_cp(("parallel", "parallel", "arbitrary")),
    )(*args)


def _rmsnorm_bwd(x, col_block, gain, da, w, dres=None, h=None, *, name, tm=512):
    s = x.shape[0]
    k, n = w.shape
    tm = min(tm, s)

    def body(*refs):
        x_ref, g_ref, da_ref, w_ref = refs[:4]
        dres_ref = refs[4] if dres is not None else None
        if h is not None:
            h_ref, dw_ref = refs[4 + (dres is not None)], refs[-1]
            dx_ref, dg_ref = refs[-3], refs[-2]

            @pl.when(pl.program_id(0) == 0)
            def _():
                dw_ref[...] = jnp.zeros_like(dw_ref)

            dw_ref[...] += _dot(h_ref[...], da_ref[...], TN)
        else:
            dx_ref, dg_ref = refs[-2], refs[-1]
        xv = x_ref[...]
        r = lax.rsqrt(jnp.mean(xv * xv, axis=-1, keepdims=True) + EPS)
        dhv = _dot(da_ref[...], w_ref[...], NT)
        a = dhv * g_ref[...]
        dx = r * a - xv * (r * r * r) * jnp.mean(a * xv, axis=-1, keepdims=True)
        if dres_ref is not None:
            dx = dx + dres_ref[...]
        dx_ref[...] = dx

        @pl.when(pl.program_id(0) == 0)
        def _():
            dg_ref[...] = jnp.zeros_like(dg_ref)

        dg_ref[...] += jnp.sum(dhv * xv * r, axis=0, keepdims=True)

    in_specs = [pl.BlockSpec((tm, k), lambda i: (i, col_block)), pl.BlockSpec((1, k), lambda i: (0, 0)),
                pl.BlockSpec((tm, n), lambda i: (i, 0)), pl.BlockSpec((k, n), lambda i: (0, 0))]
    args = [x, gain.reshape(1, k), da, w]
    if dres is not None:
        in_specs.append(pl.BlockSpec((tm, k), lambda i: (i, 0)))
        args.append(dres)
    out_specs = [pl.BlockSpec((tm, k), lambda i: (i, 0)), pl.BlockSpec((1, k), lambda i: (0, 0))]
    out_shape = [jax.ShapeDtypeStruct((s, k), F32), jax.ShapeDtypeStruct((1, k), F32)]
    if h is not None:
        in_specs.append(pl.BlockSpec((tm, k), lambda i: (i, 0)))
        args.append(h)
        out_specs.append(pl.BlockSpec((k, n), lambda i: (0, 0)))
        out_shape.append(jax.ShapeDtypeStruct((k, n), F32))
    outs = pl.pallas_call(
        body, name=name, grid=(s // tm,), in_specs=in_specs, out_specs=out_specs, out_shape=out_shape,
        compiler_params=_cp(("arbitrary",)),
    )(*args)
    return (outs[0], outs[1].reshape(k)) + tuple(outs[2:])


def _ffn_fwd(x, gain, w_gu4, w_d2, *, name, tm=256):
    s = x.shape[0]
    tm = min(tm, s)

    def body(x_ref, g_ref, wgu_ref, wd_ref, xo_ref, dgu_ref, act_ref):
        xv = x_ref[...]
        r = lax.rsqrt(jnp.mean(xv * xv, axis=-1, keepdims=True) + EPS)
        hv = (xv * r * g_ref[...]).astype(BF16)
        y = jnp.zeros((tm, D), F32)
        for j in range(2):
            g = _dot(hv, wgu_ref[j], NN)
            u = _dot(hv, wgu_ref[2 + j], NN)
            sg = _sigmoid(g)
            silu = g * sg
            dgu_ref[:, j * FF_SHARD:(j + 1) * FF_SHARD] = (u * (sg * (1.0 + g * (1.0 - sg)))).astype(BF16)
            dgu_ref[:, D_FF + j * FF_SHARD:D_FF + (j + 1) * FF_SHARD] = silu.astype(BF16)
            act = (silu * u).astype(BF16)
            act_ref[:, j * FF_SHARD:(j + 1) * FF_SHARD] = act
            y = y + _dot(act, wd_ref[j], NN)
        xo_ref[...] = xv + 0.5 * y

    row = lambda i: (i, 0)
    return pl.pallas_call(
        body, name=name, grid=(s // tm,),
        in_specs=[pl.BlockSpec((tm, D), row), pl.BlockSpec((1, D), lambda i: (0, 0)),
                  pl.BlockSpec((N_CHIPS, D, FF_SHARD), lambda i: (0, 0, 0), pipeline_mode=pl.Buffered(1)),
                  pl.BlockSpec((2, FF_SHARD, D), lambda i: (0, 0, 0), pipeline_mode=pl.Buffered(1))],
        out_specs=[pl.BlockSpec((tm, D), row), pl.BlockSpec((tm, 2 * D_FF), row), pl.BlockSpec((tm, D_FF), row)],
        out_shape=[jax.ShapeDtypeStruct((s, D), F32), jax.ShapeDtypeStruct((s, 2 * D_FF), BF16),
                   jax.ShapeDtypeStruct((s, D_FF), BF16)],
        compiler_params=_cp(("parallel",)),
    )(x, gain.reshape(1, D), w_gu4, w_d2)


FFN_ROW_CHUNK = 32


def _ffn_bwd(x, dxo, dloc, gain, w_gu4, w_d2, *, name, tm=256):
    s = x.shape[0]
    tm = min(tm, s)

    def body(x_ref, dxo_ref, dloc_ref, g_ref, wgu_ref, wd_ref, dx_ref, dgu_ref, h_ref, dy_ref, dg_ref):
        xv = x_ref[...]
        r = lax.rsqrt(jnp.mean(xv * xv, axis=-1, keepdims=True) + EPS)
        xh = xv * r
        h_ref[...] = (xh * g_ref[...]).astype(BF16)
        dxov = dxo_ref[...]
        dy = (0.5 * dxov).astype(BF16)
        dy_ref[...] = dy
        gcols = [slice(j * FF_SHARD, (j + 1) * FF_SHARD) for j in range(2)]
        ucols = [slice(D_FF + j * FF_SHARD, D_FF + (j + 1) * FF_SHARD) for j in range(2)]
        dacts = [_dot(dy, wd_ref[j], NT) for j in range(2)]
        for r0 in range(0, tm, FFN_ROW_CHUNK):
            rows = slice(r0, r0 + FFN_ROW_CHUNK)
            for j in range(2):
                da = dacts[j][rows]
                dgu_ref[rows, gcols[j]] = (da * dloc_ref[rows, gcols[j]].astype(F32)).astype(BF16)
                dgu_ref[rows, ucols[j]] = (da * dloc_ref[rows, ucols[j]].astype(F32)).astype(BF16)
        dh = jnp.zeros((tm, D), F32)
        for j in range(2):
            dh = dh + _dot(dgu_ref[:, gcols[j]], wgu_ref[j], NT) + _dot(dgu_ref[:, ucols[j]], wgu_ref[2 + j], NT)
        a = dh * g_ref[...]
        dx_ref[...] = dxov + r * a - xh * (r * jnp.mean(a * xh, axis=-1, keepdims=True))

        @pl.when(pl.program_id(0) == 0)
        def _():
            dg_ref[...] = jnp.zeros_like(dg_ref)

        dg_ref[...] += jnp.sum(dh * xh, axis=0, keepdims=True)

    row = lambda i: (i, 0)
    outs = pl.pallas_call(
        body, name=name, grid=(s // tm,),
        in_specs=[pl.BlockSpec((tm, D), row), pl.BlockSpec((tm, D), row), pl.BlockSpec((tm, 2 * D_FF), row),
                  pl.BlockSpec((1, D), lambda i: (0, 0)),
                  pl.BlockSpec((N_CHIPS, D, FF_SHARD), lambda i: (0, 0, 0), pipeline_mode=pl.Buffered(1)),
                  pl.BlockSpec((2, FF_SHARD, D), lambda i: (0, 0, 0), pipeline_mode=pl.Buffered(1))],
        out_specs=[pl.BlockSpec((tm, D), row), pl.BlockSpec((tm, 2 * D_FF), row),
                   pl.BlockSpec((tm, D), row), pl.BlockSpec((tm, D), row), pl.BlockSpec((1, D), lambda i: (0, 0))],
        out_shape=[jax.ShapeDtypeStruct((s, D), F32), jax.ShapeDtypeStruct((s, 2 * D_FF), BF16),
                   jax.ShapeDtypeStruct((s, D), BF16), jax.ShapeDtypeStruct((s, D), BF16), jax.ShapeDtypeStruct((1, D), F32)],
        compiler_params=_cp(("arbitrary",)),
    )(x, dxo, dloc, gain.reshape(1, D), w_gu4, w_d2)
    dx, dgu, h, dy, dg = outs
    return dx, dgu, h, dy, dg.reshape(D)


DA = 128
SCALE_MLA = 1.0 / math.sqrt(NOPE + ROPE)
SCALE_FOX = 1.0 / math.sqrt(FOX_D)


def _causal_blocks(nb, key_major):
    if key_major:
        pairs = [(i, j) for j in range(nb) for i in range(j, nb)]
    else:
        pairs = [(i, j) for i in range(nb) for j in range(i + 1)]
    return (jnp.asarray(np.array([p[0] for p in pairs], np.int32)), jnp.asarray(np.array([p[1] for p in pairs], np.int32)))


HEADS_PER_STEP = 3
ROW_CHUNK = 64

def _col_to_row(col):
    return jnp.broadcast_to(col, (col.shape[0], DA)).T[0:1, :]


def _attn_fwd(qa, ka, va, dv, *, name, t=512):
    h, s, _ = qa.shape
    t = min(t, s)
    nb = s // t
    g = H
    qi, kj = _causal_blocks(nb, key_major=False)

    rc = min(ROW_CHUNK, t)

    def body(qi_ref, kj_ref, q_ref, k_ref, v_ref, o_ref, lse_ref, m_sc, acc_sc, p_sc, a_sc):
        n = pl.program_id(1)
        i, j = qi_ref[n], kj_ref[n]

        @pl.when(j == 0)
        def _():
            m_sc[...] = jnp.full_like(m_sc, -jnp.inf)
            acc_sc[...] = jnp.zeros_like(acc_sc)

        def step(masked):
            scs = [_dot(q_ref[hh], k_ref[hh], NT) for hh in range(g)]
            for r0 in range(0, t, rc):
                rows = slice(r0, r0 + rc)
                for hh in range(g):
                    sr = scs[hh][rows]
                    if masked:
                        row = lax.broadcasted_iota(jnp.int32, (rc, t), 0) + r0
                        col = lax.broadcasted_iota(jnp.int32, (rc, t), 1)
                        sr = jnp.where(col <= row, sr, -jnp.inf)
                    tiles = [sr[:, c0:c0 + DA] for c0 in range(0, t, DA)]
                    top = tiles[0]
                    for tile in tiles[1:]:
                        top = jnp.maximum(top, tile)
                    m_old = m_sc[hh, rows]
                    m_new = jnp.maximum(m_old, jnp.max(top, axis=-1, keepdims=True))
                    for c0, tile in zip(range(0, t, DA), tiles):
                        p_sc[hh, rows, c0:c0 + DA] = jnp.exp(tile - m_new).astype(BF16)
                    a_sc[hh, rows] = jnp.exp(m_old - m_new)
                    m_sc[hh, rows] = m_new
            for hh in range(g):
                acc_sc[hh] = a_sc[hh] * acc_sc[hh] + _dot(p_sc[hh], v_ref[hh], NN)

        @pl.when(j < i)
        def _():
            step(False)

        @pl.when(j == i)
        def _():
            step(True)
            for hh in range(g):
                acc = acc_sc[hh]
                l = acc[:, dv:dv + 1]
                o_ref[hh] = acc[:, :dv] / l
                lse_ref[hh] = _col_to_row(m_sc[hh][:, 0:1] + jnp.log(l))

    qmap = lambda hg, n, qi_r, kj_r: (hg, qi_r[n], 0)
    kmap = lambda hg, n, qi_r, kj_r: (hg, kj_r[n], 0)
    return pl.pallas_call(
        body, name=name,
        grid_spec=pltpu.PrefetchScalarGridSpec(
            num_scalar_prefetch=2, grid=(h // g, qi.shape[0]),
            in_specs=[pl.BlockSpec((g, t, DA), qmap), pl.BlockSpec((g, t, DA), kmap), pl.BlockSpec((g, t, DA), kmap)],
            out_specs=[pl.BlockSpec((g, t, dv), qmap), pl.BlockSpec((g, 1, t), lambda hg, n, qi_r, kj_r: (hg, 0, qi_r[n]))],
            scratch_shapes=[pltpu.VMEM((g, t, DA), F32), pltpu.VMEM((g, t, DA), F32), pltpu.VMEM((g, t, t), BF16),
                            pltpu.VMEM((g, t, DA), F32)]),
        out_shape=[jax.ShapeDtypeStruct((h, s, dv), F32), jax.ShapeDtypeStruct((h, 1, s), F32)],
        compiler_params=_cp(("parallel", "arbitrary")),
    )(qi, kj, qa, ka, va)


def _attn_bwd(qa, ka, va, doa, lse_row, delta_row, decay, *, name, t=512):
    h, s, _ = qa.shape
    t = min(t, s)
    nb = s // t
    g = HEADS_PER_STEP
    rc = min(ROW_CHUNK, t)
    qi, kj = _causal_blocks(nb, key_major=True)
    nsteps = qi.shape[0]

    def body(*refs):
        qi_ref, kj_ref, q_ref, k_ref, v_ref, do_ref, lse_ref, dl_ref = refs[:8]
        p_sc, ds_sc = refs[-2:]
        if decay:
            dq_ref, dk_ref, dv_ref, dcq_ref, dck_ref, dq_acc, dk_acc, dv_acc, dcq_acc, dck_acc = refs[8:-2]
        else:
            dq_ref, dk_ref, dv_ref, dq_acc, dk_acc, dv_acc = refs[8:-2]
        n = pl.program_id(1)
        i, j = qi_ref[n], kj_ref[n]

        @pl.when(n == 0)
        def _():
            dq_acc[...] = jnp.zeros_like(dq_acc)
            if decay:
                dcq_acc[...] = jnp.zeros_like(dcq_acc)

        @pl.when(i == j)
        def _():
            dk_acc[...] = jnp.zeros_like(dk_acc)
            dv_acc[...] = jnp.zeros_like(dv_acc)
            if decay:
                dck_acc[...] = jnp.zeros_like(dck_acc)

        def step(masked):
            sts = [_dot(k_ref[hh], q_ref[hh], NT) for hh in range(g)]
            dpts = [_dot(v_ref[hh], do_ref[hh], NT) for hh in range(g)]
            dcq = [jnp.zeros((1, t), F32) for _ in range(g)]
            for r0 in range(0, t, rc):
                rows = slice(r0, r0 + rc)
                for hh in range(g):
                    st = sts[hh][rows]
                    if masked:
                        row = lax.broadcasted_iota(jnp.int32, (rc, t), 0) + r0
                        col = lax.broadcasted_iota(jnp.int32, (rc, t), 1)
                        st = jnp.where(row <= col, st, -jnp.inf)
                    pt = jnp.exp(st - lse_ref[hh])
                    dst = pt * (dpts[hh][rows] - dl_ref[hh])
                    p_sc[hh, rows] = pt.astype(BF16)
                    ds_sc[hh, rows] = dst.astype(BF16)
                    if decay:
                        dcq[hh] = dcq[hh] + jnp.sum(dst, axis=0, keepdims=True)
                        dck_acc[hh, rows] -= jnp.sum(dst, axis=1, keepdims=True)
            for hh in range(g):
                dv_acc[hh] += _dot(p_sc[hh], do_ref[hh], NN)
                dk_acc[hh] += _dot(ds_sc[hh], q_ref[hh], NN)
                dq_acc[hh, i] += _dot(ds_sc[hh], k_ref[hh], TN)
                if decay:
                    dcq_acc[hh, i] += dcq[hh]

        @pl.when(i > j)
        def _():
            step(False)

        @pl.when(i == j)
        def _():
            step(True)

        @pl.when(i == nb - 1)
        def _():
            dk_ref[...] = dk_acc[...]
            dv_ref[...] = dv_acc[...]
            if decay:
                for hh in range(g):
                    dck_ref[hh] = _col_to_row(dck_acc[hh])

        @pl.when(n == nsteps - 1)
        def _():
            dq_ref[...] = dq_acc[...]
            if decay:
                dcq_ref[...] = dcq_acc[...]

    kmap = lambda hg, n, qi_r, kj_r: (hg, kj_r[n], 0)
    qmap = lambda hg, n, qi_r, kj_r: (hg, qi_r[n], 0)
    qrow = lambda hg, n, qi_r, kj_r: (hg, 0, qi_r[n])
    krow = lambda hg, n, qi_r, kj_r: (hg, 0, kj_r[n])
    whole = lambda hg, n, qi_r, kj_r: (hg, 0, 0, 0)
    in_specs = [pl.BlockSpec((g, t, DA), qmap), pl.BlockSpec((g, t, DA), kmap), pl.BlockSpec((g, t, DA), kmap),
                pl.BlockSpec((g, t, DA), qmap), pl.BlockSpec((g, 1, t), qrow), pl.BlockSpec((g, 1, t), qrow)]
    out_specs = [pl.BlockSpec((g, nb, t, DA), whole), pl.BlockSpec((g, t, DA), kmap), pl.BlockSpec((g, t, DA), kmap)]
    out_shape = [jax.ShapeDtypeStruct((h, nb, t, DA), F32), jax.ShapeDtypeStruct((h, s, DA), F32), jax.ShapeDtypeStruct((h, s, DA), F32)]
    scratch = [pltpu.VMEM((g, nb, t, DA), F32), pltpu.VMEM((g, t, DA), F32), pltpu.VMEM((g, t, DA), F32)]
    if decay:
        out_specs += [pl.BlockSpec((g, nb, 1, t), whole), pl.BlockSpec((g, 1, t), krow)]
        out_shape += [jax.ShapeDtypeStruct((h, nb, 1, t), F32), jax.ShapeDtypeStruct((h, 1, s), F32)]
        scratch += [pltpu.VMEM((g, nb, 1, t), F32), pltpu.VMEM((g, t, 1), F32)]
    scratch += [pltpu.VMEM((g, t, t), BF16), pltpu.VMEM((g, t, t), BF16)]
    outs = pl.pallas_call(
        body, name=name,
        grid_spec=pltpu.PrefetchScalarGridSpec(num_scalar_prefetch=2, grid=(h // g, nsteps), in_specs=in_specs, out_specs=out_specs,
                                               scratch_shapes=scratch),
        out_shape=out_shape, compiler_params=_cp(("parallel", "arbitrary")),
    )(qi, kj, qa, ka, va, doa, lse_row, delta_row)
    outs = list(outs)
    outs[0] = outs[0].reshape(h, s, DA)
    if decay:
        outs[3] = outs[3].reshape(h, 1, s)
    return outs


def _sel(rows, cols, pairs, value=1.0):
    m = np.zeros((rows, cols), np.float32)
    for r, c in pairs:
        m[r, c] = value
    return jnp.asarray(m, BF16)


def _lane_row(lanes):
    m = np.zeros((1, DA), np.float32)
    m[0, list(lanes)] = 1.0
    return jnp.asarray(m)


def _rms(xv, gain):
    r = lax.rsqrt(jnp.mean(xv * xv, axis=-1, keepdims=True) + EPS)
    return xv * r * gain


def _mix_in(x, gains, w_in, wq_a, wq_b, wk, wv, tabs, *, name, tm=512):
    s = x.shape[0]
    tm = min(tm, s)
    one = _lane_row([VDIM])
    g_mix, g_q, g_kv = gains

    def body(x_ref, gm_ref, gq_ref, gkv_ref, win_ref, wa_ref, wb_ref, wk_ref, wv_ref, cq_ref, sq_ref, ck_ref, sk_ref,
             one_ref, z_ref, h_ref, qn_ref, kvn_ref, qa_ref, ka_ref, va_ref):
        hv = _rms(x_ref[...], gm_ref[...]).astype(BF16)
        h_ref[...] = hv
        z = _dot(hv, win_ref[...], NN)
        z_ref[...] = z
        qn = _rms(z[:, Z_QA:Z_QA + Q_RANK], gq_ref[...]).astype(BF16)
        kvn = _rms(z[:, Z_KVA:Z_KVA + KV_RANK], gkv_ref[...]).astype(BF16)
        qn_ref[...] = qn
        kvn_ref[...] = kvn
        c, sn = cq_ref[...], sq_ref[...]
        kpe = z[:, Z_KR:Z_KR + DA] * ck_ref[...] + z[:, Z_F:Z_F + DA] * sk_ref[...]
        for hh in range(H):
            cols = slice(hh * DA, (hh + 1) * DA)
            qa_ref[hh] = (_dot(qn, wa_ref[:, cols], NN) * c + _dot(qn, wb_ref[:, cols], NN) * sn).astype(BF16)
            ka_ref[hh] = (_dot(kvn, wk_ref[:, cols], NN) + kpe).astype(BF16)
            va_ref[hh] = (_dot(kvn, wv_ref[:, cols], NN) + one_ref[...]).astype(BF16)

    row = lambda i: (i, 0)
    fixed = lambda i: (0, 0)
    full = lambda a: pl.BlockSpec(a.shape, fixed)
    tab = pl.BlockSpec((tm, DA), row)
    heads = pl.BlockSpec((H, tm, DA), lambda i: (0, i, 0))
    return pl.pallas_call(
        body, name=name, grid=(s // tm,),
        in_specs=[pl.BlockSpec((tm, D), row), pl.BlockSpec((1, D), fixed), pl.BlockSpec((1, Q_RANK), fixed),
                  pl.BlockSpec((1, KV_RANK), fixed), full(w_in), full(wq_a), full(wq_b), full(wk), full(wv),
                  tab, tab, tab, tab, pl.BlockSpec((1, DA), fixed)],
        out_specs=[pl.BlockSpec((tm, NZ), row), pl.BlockSpec((tm, D), row), pl.BlockSpec((tm, Q_RANK), row),
                   pl.BlockSpec((tm, KV_RANK), row), heads, heads, heads],
        out_shape=[jax.ShapeDtypeStruct((s, NZ), F32), jax.ShapeDtypeStruct((s, D), BF16),
                   jax.ShapeDtypeStruct((s, Q_RANK), BF16), jax.ShapeDtypeStruct((s, KV_RANK), BF16)]
        + [jax.ShapeDtypeStruct((H, s, DA), BF16)] * 3,
        compiler_params=_cp(("parallel",)),
    )(x, g_mix.reshape(1, D), g_q.reshape(1, Q_RANK), g_kv.reshape(1, KV_RANK), w_in, wq_a, wq_b, wk, wv,
      tabs["cq"], tabs["sq"], tabs["ck"], tabs["sk"], one)


DEC_C = (FOX_D, FOX_D + 1, FOX_D + 2)
DEC_1 = (FOX_D + 3, FOX_D + 4, FOX_D + 5)


def _fox_prep(z, c3t, *, name, tm=512):
    s = z.shape[0]
    tm = min(tm, s)
    w = H * FOX_D
    left = [(r, r) for r in range(FOX_D)]
    right = [(FOX_D + r, r) for r in range(FOX_D)]
    pq = jnp.stack([_sel(DA, DA, left, SCALE_FOX), _sel(DA, DA, right, SCALE_FOX)])
    pk = jnp.stack([_sel(DA, DA, left), _sel(DA, DA, right)])
    pcq = jnp.stack([_sel(32, DA, [(hh + 8 * k, DEC_C[k]) for k in range(3)]) for hh in range(H)])
    pck = jnp.stack([_sel(32, DA, [(hh + 8 * k, DEC_1[k]) for k in range(3)], -1.0) for hh in range(H)])
    rows3 = jnp.concatenate([_lane_row(DEC_1), _lane_row(DEC_C), _lane_row([FOX_D])], axis=0)

    def body(zq_ref, zk_ref, zv_ref, c_ref, pq_ref, pk_ref, pcq_ref, pck_ref, r_ref, qa_ref, ka_ref, va_ref):
        c3 = c_ref[...]
        for pair in range(H // 2):
            lanes = slice(pair * DA, (pair + 1) * DA)
            zq, zk, zv = zq_ref[:, lanes].astype(BF16), zk_ref[:, lanes].astype(BF16), zv_ref[:, lanes].astype(BF16)
            for side in range(2):
                hh = 2 * pair + side
                qa_ref[hh] = (_dot(zq, pq_ref[side], NN) + _dot(c3, pcq_ref[hh], TN) + r_ref[0:1, :]).astype(BF16)
                ka_ref[hh] = (_dot(zk, pk_ref[side], NN) + _dot(c3, pck_ref[hh], TN) + r_ref[1:2, :]).astype(BF16)
                va_ref[hh] = (_dot(zv, pk_ref[side], NN) + r_ref[2:3, :]).astype(BF16)

    fixed2 = lambda i: (0, 0)
    fixed3 = lambda i: (0, 0, 0)
    heads = pl.BlockSpec((H, tm, DA), lambda i: (0, i, 0))
    zblk = lambda c: pl.BlockSpec((tm, w), lambda i: (i, c))
    return pl.pallas_call(
        body, name=name, grid=(s // tm,),
        in_specs=[zblk(Z_FOX // w), zblk(Z_FOX // w + 1), zblk(Z_FOX // w + 2), pl.BlockSpec((32, tm), lambda i: (0, i)),
                  pl.BlockSpec((2, DA, DA), fixed3), pl.BlockSpec((2, DA, DA), fixed3),
                  pl.BlockSpec((H, 32, DA), fixed3), pl.BlockSpec((H, 32, DA), fixed3), pl.BlockSpec((3, DA), fixed2)],
        out_specs=[heads, heads, heads], out_shape=[jax.ShapeDtypeStruct((H, s, DA), BF16)] * 3,
        compiler_params=_cp(("parallel",)),
    )(z, z, z, c3t, pq, pk, pcq, pck, rows3)


def _mix_out(oa, yb, oc, w_out, x1, *, name, tm=512):
    s = yb.shape[0]
    tm = min(tm, s)
    e2 = jnp.stack([_sel(VDIM, DA, [(r, r) for r in range(VDIM)]), _sel(VDIM, DA, [(r, VDIM + r) for r in range(VDIM)])])

    def body(oa_ref, yb_ref, oc_ref, e_ref, w_ref, x_ref, x2_ref, cat_ref):
        def pairs(o_ref):
            return [(_dot(o_ref[2 * p].astype(BF16), e_ref[0], NN) + _dot(o_ref[2 * p + 1].astype(BF16), e_ref[1], NN)).astype(BF16)
                    for p in range(H // 2)]

        cat = jnp.concatenate(pairs(oa_ref) + [yb_ref[...].astype(BF16)] + pairs(oc_ref), axis=1)
        cat_ref[...] = cat
        x2_ref[...] = x_ref[...] + _dot(cat, w_ref[...], NN)

    row = lambda i: (i, 0)
    heads = pl.BlockSpec((H, tm, VDIM), lambda i: (0, i, 0))
    return pl.pallas_call(
        body, name=name, grid=(s // tm,),
        in_specs=[heads, pl.BlockSpec((tm, POOL_W), row), heads, pl.BlockSpec((2, VDIM, DA), lambda i: (0, 0, 0)),
                  pl.BlockSpec((D, D), lambda i: (0, 0)), pl.BlockSpec((tm, D), row)],
        out_specs=[pl.BlockSpec((tm, D), row), pl.BlockSpec((tm, D), row)],
        out_shape=[jax.ShapeDtypeStruct((s, D), F32), jax.ShapeDtypeStruct((s, D), BF16)],
        compiler_params=_cp(("parallel",)),
    )(oa, yb, oc, e2, w_out, x1)


def _mix_out_bwd(dx2b, w_out, cat, oa, oc, *, name, tm=512):
    s = dx2b.shape[0]
    tm = min(tm, s)
    f2 = jnp.stack([_sel(DA, DA, [(r, r) for r in range(VDIM)]), _sel(DA, DA, [(VDIM + r, r) for r in range(VDIM)])])
    nv = H * VDIM

    def body(dx_ref, w_ref, cat_ref, oa_ref, oc_ref, f_ref, doa_ref, doc_ref, dyb_ref, dla_ref, dlc_ref, dw_ref):
        @pl.when(pl.program_id(0) == 0)
        def _():
            dw_ref[...] = jnp.zeros_like(dw_ref)

        dw_ref[...] += _dot(cat_ref[...], dx_ref[...], TN)
        dcat = _dot(dx_ref[...], w_ref[...], NT)
        dyb_ref[...] = dcat[:, nv:nv + POOL_W]
        for base, o_ref, do_ref, dl_ref in ((0, oa_ref, doa_ref, dla_ref), (nv + POOL_W, oc_ref, doc_ref, dlc_ref)):
            for p in range(H // 2):
                blk = dcat[:, base + p * DA:base + (p + 1) * DA].astype(BF16)
                for side in range(2):
                    hh = 2 * p + side
                    do = _dot(blk, f_ref[side], NN)
                    do_ref[hh] = do.astype(BF16)
                    dl_ref[hh] = _col_to_row(jnp.sum(do[:, :VDIM] * o_ref[hh], axis=-1, keepdims=True))

    row = lambda i: (i, 0)
    heads = lambda w: pl.BlockSpec((H, tm, w), lambda i: (0, i, 0))
    return pl.pallas_call(
        body, name=name, grid=(s // tm,),
        in_specs=[pl.BlockSpec((tm, D), row), pl.BlockSpec((D, D), lambda i: (0, 0)), pl.BlockSpec((tm, D), row),
                  heads(VDIM), heads(VDIM), pl.BlockSpec((2, DA, DA), lambda i: (0, 0, 0))],
        out_specs=[heads(DA), heads(DA), pl.BlockSpec((tm, POOL_W), row),
                   pl.BlockSpec((H, 1, tm), lambda i: (0, 0, i)), pl.BlockSpec((H, 1, tm), lambda i: (0, 0, i)),
                   pl.BlockSpec((D, D), lambda i: (0, 0))],
        out_shape=[jax.ShapeDtypeStruct((H, s, DA), BF16), jax.ShapeDtypeStruct((H, s, DA), BF16),
                   jax.ShapeDtypeStruct((s, POOL_W), F32), jax.ShapeDtypeStruct((H, 1, s), F32), jax.ShapeDtypeStruct((H, 1, s), F32),
                   jax.ShapeDtypeStruct((D, D), F32)],
        compiler_params=_cp(("arbitrary",)),
    )(dx2b, w_out, cat, oa, oc, f2)


def _mla_bwd_prep(dqa, dka, dva, dft, qn, kvn, cq, sq, ck, sk, *, name, tm=512):
    s = dqa.shape[1]
    tm = min(tm, s)
    keep = _lane_row(range(NOPE))

    def body(dq_ref, dk_ref, dv_ref, dft_ref, qn_ref, kvn_ref, cq_ref, sq_ref, ck_ref, sk_ref, keep_ref,
             dqab_ref, dkv_ref, dz3_ref, dz15_ref, dwq_ref, dwkv_ref):
        cqv, sqv = cq_ref[...], sq_ref[...]
        dkpe = jnp.zeros((tm, DA), F32)
        for hh in range(H):
            lanes = slice(hh * DA, (hh + 1) * DA)
            dq = dq_ref[hh]
            dqab_ref[:, lanes] = (dq * cqv).astype(BF16)
            dqab_ref[:, H * DA + hh * DA:H * DA + (hh + 1) * DA] = (dq * sqv).astype(BF16)
            dk = dk_ref[hh]
            dkpe = dkpe + dk
            dkv_ref[:, lanes] = (dk * keep_ref[...]).astype(BF16)
            dkv_ref[:, H * DA + hh * DA:H * DA + (hh + 1) * DA] = (dv_ref[hh] * keep_ref[...]).astype(BF16)
        dz3_ref[...] = (dkpe * ck_ref[...]).astype(BF16)
        dz15_ref[...] = (dkpe * sk_ref[...] + dft_ref[...]).astype(BF16)

        @pl.when(pl.program_id(0) == 0)
        def _():
            dwq_ref[...] = jnp.zeros_like(dwq_ref)
            dwkv_ref[...] = jnp.zeros_like(dwkv_ref)

        dwq_ref[...] += _dot(qn_ref[...], dqab_ref[...], TN)
        dwkv_ref[...] += _dot(kvn_ref[...], dkv_ref[...], TN)

    row = lambda i: (i, 0)
    fixed = lambda i: (0, 0)
    heads = pl.BlockSpec((H, tm, DA), lambda i: (0, i, 0))
    tab = pl.BlockSpec((tm, DA), row)
    wide = 2 * H * DA
    return pl.pallas_call(
        body, name=name, grid=(s // tm,),
        in_specs=[heads, heads, heads, tab, pl.BlockSpec((tm, Q_RANK), row), pl.BlockSpec((tm, KV_RANK), row),
                  tab, tab, tab, tab, pl.BlockSpec((1, DA), fixed)],
        out_specs=[pl.BlockSpec((tm, wide), row), pl.BlockSpec((tm, wide), row), tab, tab,
                   pl.BlockSpec((Q_RANK, wide), fixed), pl.BlockSpec((KV_RANK, wide), fixed)],
        out_shape=[jax.ShapeDtypeStruct((s, wide), BF16), jax.ShapeDtypeStruct((s, wide), BF16),
                   jax.ShapeDtypeStruct((s, DA), BF16), jax.ShapeDtypeStruct((s, DA), BF16),
                   jax.ShapeDtypeStruct((Q_RANK, wide), F32), jax.ShapeDtypeStruct((KV_RANK, wide), F32)],
        compiler_params=_cp(("arbitrary",)),
    )(dqa, dka, dva, dft, qn, kvn, cq, sq, ck, sk, keep)


def _fox_bwd_prep(dfqa, dfka, dfva, *, name, tm=512):
    s = dfqa.shape[1]
    tm = min(tm, s)
    place = lambda v: jnp.stack([_sel(DA, DA, [(r, r) for r in range(FOX_D)], v), _sel(DA, DA, [(r, FOX_D + r) for r in range(FOX_D)], v)])
    gq, gk = place(SCALE_FOX), place(1.0)

    def body(dq_ref, dk_ref, dv_ref, gq_ref, gk_ref, dz_ref):
        for part, (d_ref, g_ref) in enumerate(((dq_ref, gq_ref), (dk_ref, gk_ref), (dv_ref, gk_ref))):
            for p in range(H // 2):
                blk = _dot(d_ref[2 * p].astype(BF16), g_ref[0], NN) + _dot(d_ref[2 * p + 1].astype(BF16), g_ref[1], NN)
                lo = part * H * FOX_D + p * DA
                dz_ref[:, lo:lo + DA] = blk.astype(BF16)

    heads = pl.BlockSpec((H, tm, DA), lambda i: (0, i, 0))
    sel = pl.BlockSpec((2, DA, DA), lambda i: (0, 0, 0))
    return pl.pallas_call(
        body, name=name, grid=(s // tm,), in_specs=[heads, heads, heads, sel, sel],
        out_specs=pl.BlockSpec((tm, 3 * H * FOX_D), lambda i: (i, 0)),
        out_shape=jax.ShapeDtypeStruct((s, 3 * H * FOX_D), BF16), compiler_params=_cp(("parallel",)),
    )(dfqa, dfka, dfva, gq, gk)


def _lane_scan(x, s, reverse):
    lane = lax.broadcasted_iota(jnp.int32, x.shape, 1)
    sh = 1
    while sh < s:
        if reverse:
            x = x + jnp.where(lane < s - sh, pltpu.roll(x, s - sh, axis=1), 0.0)
        else:
            x = x + jnp.where(lane >= sh, pltpu.roll(x, sh, axis=1), 0.0)
        sh *= 2
    return x


def _gate_fwd(z, col_block, bias, *, name):
    s = z.shape[0]

    def body(z_ref, b_ref, f_ref, c_ref):
        ft = z_ref[...].T[0:8, :]
        f_ref[...] = ft
        xg = ft + b_ref[...]
        lf = jnp.minimum(xg, 0.0) - jnp.log(1.0 + jnp.exp(-jnp.abs(xg)))
        c = _lane_scan(lf, s, False)
        hi = c.astype(BF16).astype(F32)
        r = c - hi
        mid = r.astype(BF16).astype(F32)
        lo = r - mid
        c_ref[...] = jnp.concatenate([hi, mid, lo, jnp.zeros_like(hi)], axis=0).astype(BF16)

    return pl.pallas_call(
        body, name=name, grid=(1,),
        in_specs=[pl.BlockSpec((s, 128), lambda i: (0, col_block)), pl.BlockSpec((8, 1), lambda i: (0, 0))],
        out_specs=[pl.BlockSpec((8, s), lambda i: (0, 0)), pl.BlockSpec((32, s), lambda i: (0, 0))],
        out_shape=[jax.ShapeDtypeStruct((8, s), F32), jax.ShapeDtypeStruct((32, s), BF16)],
        compiler_params=_cp(("arbitrary",)))(z, bias)


def _gate_bwd(ft, bias, dc, *, name):
    s = ft.shape[1]

    def body(f_ref, b_ref, dc_ref, df_ref, db_ref):
        xg = f_ref[...] + b_ref[...]
        dlf = _lane_scan(dc_ref[...], s, True)
        df = dlf * _sigmoid(-xg)
        db_ref[...] = jnp.sum(df, axis=-1, keepdims=True)
        df_ref[...] = jnp.concatenate([df, jnp.zeros((DA - 8, s), F32)], axis=0).T

    return pl.pallas_call(body, name=name, out_shape=[jax.ShapeDtypeStruct((s, DA), F32), jax.ShapeDtypeStruct((8, 1), F32)],
                          compiler_params=_cp())(ft, bias, dc)


def _pool_lane_consts(tm, i):
    lane = lax.broadcasted_iota(jnp.int32, (tm, POOL_W), 1)
    tok = lax.broadcasted_iota(jnp.int32, (tm, POOL_W), 0) + i * tm
    win = jnp.where(lane < 64, 2, jnp.where(lane < 128, 4, jnp.where(lane < 192, 8, 16)))
    cnt = jnp.minimum(tok + 1, win).astype(F32)
    return lane, tok, cnt


def _pick_window(lane, s2, s4, s8, s16):
    return jnp.where(lane < 64, s2, jnp.where(lane < 128, s4, jnp.where(lane < 192, s8, s16)))


def _pool_fwd(z, col_block, bd, scale, *, name, tm=512):
    s = z.shape[0]
    tm = min(tm, s)
    hb = tm // POOL_HALO

    def body(u_ref, halo_ref, bd_ref, sc_ref, y_ref, p_ref, buf):
        i = pl.program_id(0)
        buf[0:POOL_HALO, :] = halo_ref[...] * (i > 0).astype(F32)
        buf[POOL_HALO:, :] = u_ref[...]

        def back(k):
            return buf[POOL_HALO - k:POOL_HALO - k + tm, :]

        u = u_ref[...]
        s2 = u + back(1)
        s4 = s2 + back(2) + back(3)
        s8 = s4 + back(4) + back(5) + back(6) + back(7)
        s16 = s8
        for k in range(8, 16):
            s16 = s16 + back(k)
        lane, _, cnt = _pool_lane_consts(tm, i)
        pooled = (_pick_window(lane, s2, s4, s8, s16) / cnt - u).astype(BF16)
        p_ref[...] = pooled
        y_ref[...] = _dot(pooled, bd_ref[...], NN) * sc_ref[...]

    return pl.pallas_call(
        body, name=name, grid=(s // tm,),
        in_specs=[pl.BlockSpec((tm, POOL_W), lambda i: (i, col_block)),
                  pl.BlockSpec((POOL_HALO, POOL_W), lambda i: (jnp.maximum(i * hb - 1, 0), col_block)),
                  pl.BlockSpec((POOL_W, POOL_W), lambda i: (0, 0)), pl.BlockSpec((1, POOL_W), lambda i: (0, 0))],
        out_specs=[pl.BlockSpec((tm, POOL_W), lambda i: (i, 0)), pl.BlockSpec((tm, POOL_W), lambda i: (i, 0))],
        out_shape=[jax.ShapeDtypeStruct((s, POOL_W), F32), jax.ShapeDtypeStruct((s, POOL_W), BF16)],
        scratch_shapes=[pltpu.VMEM((tm + POOL_HALO, POOL_W), F32)],
        compiler_params=_cp(("parallel",)),
    )(z, z, bd, scale.reshape(1, POOL_W))


def _pool_bwd_a(dy, pooled, bd, scale, *, name, tm=512):
    s = dy.shape[0]
    tm = min(tm, s)

    def body(dy_ref, p_ref, bd_ref, sc_ref, dq_ref, dbd_ref, dsc_ref):
        i = pl.program_id(0)
        dyv = dy_ref[...]
        pv = p_ref[...]
        y0 = _dot(pv, bd_ref[...], NN)
        dys = (dyv * sc_ref[...]).astype(BF16)
        dp = _dot(dys, bd_ref[...], NT)
        _, _, cnt = _pool_lane_consts(tm, i)
        dq_ref[:, 0:POOL_W] = dp / cnt
        dq_ref[:, POOL_W:] = dp

        @pl.when(i == 0)
        def _():
            dsc_ref[...] = jnp.zeros_like(dsc_ref)
            dbd_ref[...] = jnp.zeros_like(dbd_ref)

        dsc_ref[...] += jnp.sum(dyv * y0, axis=0, keepdims=True)
        dbd_ref[...] += _dot(pv, dys, TN)

    row = lambda i: (i, 0)
    fixed = lambda i: (0, 0)
    dq, dbd, dsc = pl.pallas_call(
        body, name=name, grid=(s // tm,),
        in_specs=[pl.BlockSpec((tm, POOL_W), row), pl.BlockSpec((tm, POOL_W), row),
                  pl.BlockSpec((POOL_W, POOL_W), fixed), pl.BlockSpec((1, POOL_W), fixed)],
        out_specs=[pl.BlockSpec((tm, 2 * POOL_W), row), pl.BlockSpec((POOL_W, POOL_W), fixed), pl.BlockSpec((1, POOL_W), fixed)],
        out_shape=[jax.ShapeDtypeStruct((s, 2 * POOL_W), F32), jax.ShapeDtypeStruct((POOL_W, POOL_W), F32),
                   jax.ShapeDtypeStruct((1, POOL_W), F32)],
        compiler_params=_cp(("arbitrary",)),
    )(dy, pooled, bd, scale.reshape(1, POOL_W))
    return dq, dbd, dsc.reshape(POOL_W)


def _pool_bwd_b(dq, *, name, tm=512):
    s = dq.shape[0]
    tm = min(tm, s)
    hb = tm // POOL_HALO
    nblk = s // tm

    def body(q_ref, dp_ref, halo_ref, du_ref, buf):
        i = pl.program_id(0)
        buf[0:tm, :] = q_ref[...]
        buf[tm:, :] = halo_ref[...] * (i < nblk - 1).astype(F32)

        def ahead(k):
            return buf[k:k + tm, :]

        q = q_ref[...]
        s2 = q + ahead(1)
        s4 = s2 + ahead(2) + ahead(3)
        s8 = s4 + ahead(4) + ahead(5) + ahead(6) + ahead(7)
        s16 = s8
        for k in range(8, 16):
            s16 = s16 + ahead(k)
        lane = lax.broadcasted_iota(jnp.int32, (tm, POOL_W), 1)
        du_ref[...] = _pick_window(lane, s2, s4, s8, s16) - dp_ref[...]

    return pl.pallas_call(
        body, name=name, grid=(nblk,),
        in_specs=[pl.BlockSpec((tm, POOL_W), lambda i: (i, 0)), pl.BlockSpec((tm, POOL_W), lambda i: (i, 1)),
                  pl.BlockSpec((POOL_HALO, POOL_W), lambda i: (jnp.minimum((i + 1) * hb, nblk * hb - 1), 0))],
        out_specs=pl.BlockSpec((tm, POOL_W), lambda i: (i, 0)),
        out_shape=jax.ShapeDtypeStruct((s, POOL_W), F32),
        scratch_shapes=[pltpu.VMEM((tm + POOL_HALO, POOL_W), F32)],
        compiler_params=_cp(("parallel",)),
    )(dq, dq, dq)


def _loss_head(x, gain, target, *, name, tm=512):
    s = x.shape[0]
    tm = min(tm, s)

    def body(x_ref, g_ref, t_ref, dx_ref, dg_ref, loss_ref):
        xv = x_ref[...]
        r = lax.rsqrt(jnp.mean(xv * xv, axis=-1, keepdims=True) + EPS)
        xh = xv * r
        err = xh * g_ref[...] - t_ref[...]
        dy = err * (1.0 / D)
        a = dy * g_ref[...]
        dx_ref[...] = r * a - xh * (r * jnp.mean(a * xh, axis=-1, keepdims=True))

        @pl.when(pl.program_id(0) == 0)
        def _():
            dg_ref[...] = jnp.zeros_like(dg_ref)
            loss_ref[...] = jnp.zeros_like(loss_ref)

        dg_ref[...] += jnp.sum(dy * xh, axis=0, keepdims=True)
        part = 0.5 * jnp.sum(jnp.mean(err * err, axis=-1, keepdims=True), axis=0, keepdims=True)
        loss_ref[...] += jnp.broadcast_to(part, loss_ref.shape)

    row = lambda i: (i, 0)
    dx, dg, loss = pl.pallas_call(
        body, name=name, grid=(s // tm,),
        in_specs=[pl.BlockSpec((tm, D), row), pl.BlockSpec((1, D), lambda i: (0, 0)), pl.BlockSpec((tm, D), row)],
        out_specs=[pl.BlockSpec((tm, D), row), pl.BlockSpec((1, D), lambda i: (0, 0)), pl.BlockSpec((1, 128), lambda i: (0, 0))],
        out_shape=[jax.ShapeDtypeStruct((s, D), F32), jax.ShapeDtypeStruct((1, D), F32), jax.ShapeDtypeStruct((1, 128), F32)],
        compiler_params=_cp(("arbitrary",)),
    )(x, gain.reshape(1, D), target)
    return dx, dg.reshape(D), loss[0, 0]


def _adamw(w, g, m, v, *, name, tr=512):
    rows, cols = w.shape
    tr = min(tr, rows)
    assert rows % tr == 0, (name, rows, tr)
    c_m = 1.0 - ADAM_B1
    c_v = 1.0 - ADAM_B2
    bc1 = 1.0 - ADAM_B1 ** ADAM_STEP
    bc2 = 1.0 - ADAM_B2 ** ADAM_STEP

    def body(w_ref, g_ref, m_ref, v_ref, d_ref, mo_ref, vo_ref):
        gv = g_ref[...]
        mn = ADAM_B1 * m_ref[...] + c_m * gv
        vn = ADAM_B2 * v_ref[...] + c_v * (gv * gv)
        mo_ref[...] = mn
        vo_ref[...] = vn
        d_ref[...] = -ADAM_LR * ((mn / bc1) / (jnp.sqrt(vn / bc2) + ADAM_EPS) + ADAM_WD * w_ref[...])

    spec = pl.BlockSpec((tr, cols), lambda i: (i, 0))
    return pl.pallas_call(body, name=name, grid=(rows // tr,), in_specs=[spec] * 4, out_specs=[spec] * 3,
                          out_shape=[jax.ShapeDtypeStruct((rows, cols), F32)] * 3,
                          compiler_params=_cp(("parallel",)))(w, g, m, v)


def _adamw_layer(w, g, m, v, layer, prev, *, name, tr):
    rows, cols = g.shape
    assert rows % tr == 0 and w.shape == (DEPTH * rows, cols), (name, w.shape, g.shape, tr)
    nblk = rows // tr
    c_m = 1.0 - ADAM_B1
    c_v = 1.0 - ADAM_B2
    bc1 = 1.0 - ADAM_B1 ** ADAM_STEP
    bc2 = 1.0 - ADAM_B2 ** ADAM_STEP
    n_prev = 0 if prev is None else 4

    def body(*refs):
        w_ref, g_ref, m_ref, v_ref = refs[:4]
        d_ref, mo_ref, vo_ref, go_ref = refs[4 + n_prev:]
        gv = g_ref[...]
        mn = ADAM_B1 * m_ref[...] + c_m * gv
        vn = ADAM_B2 * v_ref[...] + c_v * (gv * gv)
        mo_ref[...] = mn
        vo_ref[...] = vn
        go_ref[...] = gv
        d_ref[...] = -ADAM_LR * ((mn / bc1) / (jnp.sqrt(vn / bc2) + ADAM_EPS) + ADAM_WD * w_ref[...])

    stacked = pl.BlockSpec((tr, cols), lambda i: (layer * nblk + i, 0))
    args = [w, g, m, v] + ([] if prev is None else list(prev))
    return pl.pallas_call(
        body, name=name, grid=(nblk,),
        in_specs=[stacked, pl.BlockSpec((tr, cols), lambda i: (i, 0)), stacked, stacked] + [ANY_SPEC] * n_prev,
        out_specs=[stacked] * 4, out_shape=[jax.ShapeDtypeStruct(w.shape, F32)] * 4,
        input_output_aliases={4 + k: k for k in range(n_prev)},
        compiler_params=_cp(("parallel",)))(*args)


def _position():
    return jnp.stack([lax.axis_index("c"), 2 * lax.axis_index("x") + lax.axis_index("y")]).astype(jnp.int32)


SUM_ROW_TILES = 2


def _sum2_bf16(pos, fulls, sibs, *, name):
    n = len(fulls)
    nb = SUM_ROW_TILES

    def body(pos_ref, *refs):
        for t in range(n):
            refs[2 * n + t][...] = (refs[t][...] + refs[n + t][...]).astype(BF16)

    in_specs, sib_specs = [], []
    for sb in sibs:
        _, half, cols = sb.shape
        tr = half // nb
        assert half % nb == 0 and tr % 16 == 0, sb.shape
        in_specs.append(pl.BlockSpec((None, tr, cols), lambda j, i, p: (j, p[0] * nb + i, 0)))
        sib_specs.append(pl.BlockSpec((None, tr, cols), lambda j, i, p: (j, i, 0)))
    return pl.pallas_call(
        body, name=name,
        grid_spec=pltpu.PrefetchScalarGridSpec(num_scalar_prefetch=1, grid=(N_CHIPS, nb), in_specs=in_specs + sib_specs,
                                               out_specs=sib_specs),
        out_shape=[jax.ShapeDtypeStruct(sb.shape, BF16) for sb in sibs],
        compiler_params=_cp(("parallel", "parallel")))(pos, *fulls, *sibs)


def _sum5(pos, fulls, sibs, recvs, *, name):
    n = len(fulls)
    nb = SUM_ROW_TILES

    def body(pos_ref, *refs):
        for t in range(n):
            acc = refs[t][...] + refs[n + t][...]
            for kk in range(3):
                acc = acc + refs[2 * n + t][kk].astype(F32)
            refs[3 * n + t][...] = acc

    f_specs, s_specs, r_specs, o_specs = [], [], [], []
    for f in fulls:
        _, rows, cols = f.shape
        tr = rows // 2 // nb
        f_specs.append(pl.BlockSpec((None, tr, cols), lambda i, p: (p[1], p[0] * nb + i, 0)))
        s_specs.append(pl.BlockSpec((None, tr, cols), lambda i, p: (p[1], i, 0)))
        r_specs.append(pl.BlockSpec((3, tr, cols), lambda i, p: (0, i, 0)))
        o_specs.append(pl.BlockSpec((tr, cols), lambda i, p: (p[0] * nb + i, 0)))
    return pl.pallas_call(
        body, name=name,
        grid_spec=pltpu.PrefetchScalarGridSpec(num_scalar_prefetch=1, grid=(nb,), in_specs=f_specs + s_specs + r_specs,
                                               out_specs=o_specs),
        out_shape=[jax.ShapeDtypeStruct(f.shape[1:], F32) for f in fulls],
        compiler_params=_cp(("parallel",)))(pos, *fulls, *sibs, *recvs)


def _place():
    x, y, c = lax.axis_index("x"), lax.axis_index("y"), lax.axis_index("c")
    chips = [(1 - x, y), (x, 1 - y), (1 - x, 1 - y)]
    return x, y, c, 2 * x + y, chips


SEM_SPEC = pl.BlockSpec(memory_space=pltpu.SEMAPHORE)
ANY_SPEC = pl.BlockSpec(memory_space=pl.ANY)


def _gather_copies(ins, outs, send_i, recv_i, send_o, recv_o):
    x, y, c, me, chips = _place()
    n = len(ins)
    started, awaited = [], []
    for t in range(n):
        half = ins[t].shape[0] // 2
        mine = pl.ds(c * half, half)
        started.append(pltpu.make_async_remote_copy(
            src_ref=ins[t], dst_ref=outs[t].at[me], send_sem=send_o.at[t], recv_sem=recv_o.at[t],
            device_id=(x, y, 1 - c), device_id_type=MESH))
        awaited.append(started[-1])
        for kk, (px, py) in enumerate(chips):
            started.append(pltpu.make_async_remote_copy(
                src_ref=ins[t].at[mine], dst_ref=outs[t].at[me, mine], send_sem=send_i.at[t * 3 + kk],
                recv_sem=recv_i.at[t * 3 + kk], device_id=(px, py, c), device_id_type=MESH))
            awaited.append(pltpu.make_async_remote_copy(
                src_ref=ins[t].at[mine], dst_ref=outs[t].at[2 * px + py, mine], send_sem=send_i.at[t * 3 + kk],
                recv_sem=recv_i.at[t * 3 + kk], device_id=(px, py, c), device_id_type=MESH))
    return started, awaited


def _forward_copies(outs, send_d, recv_d):
    x, y, c, me, chips = _place()
    started, awaited = [], []
    for t in range(len(outs)):
        half = outs[t].shape[1] // 2
        for kk, (px, py) in enumerate(chips):
            for lst, hc in ((started, c), (awaited, 1 - c)):
                blk = outs[t].at[2 * px + py, pl.ds(hc * half, half)]
                lst.append(pltpu.make_async_remote_copy(src_ref=blk, dst_ref=blk, send_sem=send_d.at[t * 3 + kk],
                                                        recv_sem=recv_d.at[t * 3 + kk], device_id=(x, y, 1 - c), device_id_type=MESH))
    return started, awaited


def _gather_blocking(shards):
    n = len(shards)

    def body(*refs):
        ins, outs = refs[:n], refs[n:2 * n]
        send_i, recv_i, send_d, recv_d, send_o, recv_o = refs[2 * n:]
        started, awaited = _gather_copies(ins, outs, send_i, recv_i, send_o, recv_o)
        for cp in started:
            cp.start()
        for cp in awaited:
            cp.wait_recv()
        fwd, fwd_in = _forward_copies(outs, send_d, recv_d)
        for cp in fwd:
            cp.start()
        for cp in fwd_in:
            cp.wait_recv()
        for cp in started + fwd:
            cp.wait_send()

    return pl.pallas_call(
        body, name="gather_first", in_specs=[HBM_SPEC] * n, out_specs=[HBM_SPEC] * n,
        out_shape=[jax.ShapeDtypeStruct((N_CHIPS,) + s.shape, s.dtype) for s in shards],
        scratch_shapes=[pltpu.SemaphoreType.DMA((3 * n,)), pltpu.SemaphoreType.DMA((3 * n,)),
                        pltpu.SemaphoreType.DMA((3 * n,)), pltpu.SemaphoreType.DMA((3 * n,)),
                        pltpu.SemaphoreType.DMA((n,)), pltpu.SemaphoreType.DMA((n,))],
    )(*shards)


def _gather_start(shards, after, tag):
    n = len(shards)

    def body(*refs):
        ins = refs[:n]
        send_i, recv_i, send_o, recv_o = refs[2 * n + 1:2 * n + 5]
        outs = refs[3 * n + 5:4 * n + 5]
        token = refs[4 * n + 5]
        started, _ = _gather_copies(ins, outs, send_i, recv_i, send_o, recv_o)
        for cp in started:
            cp.start()
        token[...] = jnp.zeros_like(token)

    lands = [lax.empty((N_CHIPS,) + s.shape, s.dtype) for s in shards]
    sems = [pltpu.SemaphoreType.DMA((3 * n,)), pltpu.SemaphoreType.DMA((3 * n,)), pltpu.SemaphoreType.DMA((n,)), pltpu.SemaphoreType.DMA((n,))]
    res = pl.pallas_call(
        body, name=f"gather_{tag}_start",
        in_specs=[HBM_SPEC] * (2 * n) + [ANY_SPEC],
        out_specs=[SEM_SPEC] * 4 + [HBM_SPEC] * (2 * n) + [pl.BlockSpec(memory_space=pltpu.VMEM)],
        out_shape=sems + [jax.ShapeDtypeStruct(s.shape, s.dtype) for s in shards]
        + [jax.ShapeDtypeStruct(a.shape, a.dtype) for a in lands] + [jax.ShapeDtypeStruct((8, 128), F32)],
        input_output_aliases={t: 4 + t for t in range(2 * n)},
        compiler_params=pltpu.CompilerParams(has_side_effects=pltpu.SideEffectType.DATAFLOW_SIDE_EFFECTING),
    )(*[pltpu.with_memory_space_constraint(s, pltpu.HBM) for s in shards],
      *[pltpu.with_memory_space_constraint(a, pltpu.HBM) for a in lands], after)
    return res[:4], res[4:4 + n], res[4 + n:4 + 2 * n], res[-1]


def _gather_wait(sems, shards_thru, lands_thru, after, tag):
    n = len(shards_thru)

    def body(*refs):
        ins, outs_in = refs[:n], refs[n:2 * n]
        send_i, recv_i, send_o, recv_o = refs[2 * n:2 * n + 4]
        started, awaited = _gather_copies(ins, outs_in, send_i, recv_i, send_o, recv_o)
        for cp in started:
            cp.wait_send()
        for cp in awaited:
            cp.wait_recv()

    res = pl.pallas_call(
        body, name=f"gather_{tag}_wait",
        in_specs=[HBM_SPEC] * (2 * n) + [SEM_SPEC] * 4 + [ANY_SPEC],
        out_specs=[HBM_SPEC] * (2 * n),
        out_shape=[jax.ShapeDtypeStruct(a.shape, a.dtype) for a in list(shards_thru) + list(lands_thru)],
        input_output_aliases={t: t for t in range(2 * n)},
        compiler_params=pltpu.CompilerParams(has_side_effects=pltpu.SideEffectType.DATAFLOW_SIDE_EFFECTING),
    )(*shards_thru, *lands_thru, *sems, after)
    return res[n:]


def _gather_forward(lands, tag):
    n = len(lands)

    def body(*refs):
        outs = refs[n:2 * n]
        send_d, recv_d = refs[2 * n:]
        fwd, fwd_in = _forward_copies(outs, send_d, recv_d)
        for cp in fwd:
            cp.start()
        for cp in fwd_in:
            cp.wait_recv()
        for cp in fwd:
            cp.wait_send()

    return pl.pallas_call(
        body, name=f"gather_{tag}_forward", in_specs=[HBM_SPEC] * n, out_specs=[HBM_SPEC] * n,
        out_shape=[jax.ShapeDtypeStruct(a.shape, a.dtype) for a in lands],
        input_output_aliases={t: t for t in range(n)},
        scratch_shapes=[pltpu.SemaphoreType.DMA((3 * n,)), pltpu.SemaphoreType.DMA((3 * n,))],
    )(*lands)


def _stage1_copies(ins, sib, send, recv):
    x, y, c, me, chips = _place()
    cps = []
    for t in range(len(ins)):
        rows = ins[t].shape[1] // 2
        cps.append(pltpu.make_async_remote_copy(
            src_ref=ins[t].at[:, pl.ds((1 - c) * rows, rows), :], dst_ref=sib[t], send_sem=send.at[t],
            recv_sem=recv.at[t], device_id=(x, y, 1 - c), device_id_type=MESH))
    return cps


def _split_start(copies_fn, srcs, land_shapes, n_sems, tag):
    n = len(srcs)

    def body(*refs):
        send, recv = refs[2 * n:2 * n + 2]
        for cp in copies_fn(refs[:n], refs[3 * n + 2:4 * n + 2], send, recv):
            cp.start()
        refs[4 * n + 2][...] = jnp.zeros_like(refs[4 * n + 2])

    lands = [lax.empty(shp, dt) for shp, dt in land_shapes]
    res = pl.pallas_call(
        body, name=tag,
        in_specs=[HBM_SPEC] * (2 * n),
        out_specs=[SEM_SPEC] * 2 + [HBM_SPEC] * (2 * n) + [pl.BlockSpec(memory_space=pltpu.VMEM)],
        out_shape=[pltpu.SemaphoreType.DMA((n_sems,)), pltpu.SemaphoreType.DMA((n_sems,))]
        + [jax.ShapeDtypeStruct(p.shape, p.dtype) for p in srcs]
        + [jax.ShapeDtypeStruct(a.shape, a.dtype) for a in lands] + [jax.ShapeDtypeStruct((8, 128), F32)],
        input_output_aliases={t: 2 + t for t in range(2 * n)},
        compiler_params=pltpu.CompilerParams(has_side_effects=pltpu.SideEffectType.DATAFLOW_SIDE_EFFECTING),
    )(*[pltpu.with_memory_space_constraint(p, pltpu.HBM) for p in srcs],
      *[pltpu.with_memory_space_constraint(a, pltpu.HBM) for a in lands])
    return res[:2], res[2:2 + n], res[2 + n:2 + 2 * n], res[-1]


def _split_wait(copies_fn, sems, srcs_thru, lands_thru, after, tag):
    n = len(srcs_thru)

    def body(*refs):
        for cp in copies_fn(refs[:n], refs[n:2 * n], refs[2 * n], refs[2 * n + 1]):
            cp.wait()

    res = pl.pallas_call(
        body, name=tag,
        in_specs=[HBM_SPEC] * (2 * n) + [SEM_SPEC] * 2 + [ANY_SPEC],
        out_specs=[HBM_SPEC] * (2 * n),
        out_shape=[jax.ShapeDtypeStruct(a.shape, a.dtype) for a in list(srcs_thru) + list(lands_thru)],
        input_output_aliases={t: t for t in range(2 * n)},
        compiler_params=pltpu.CompilerParams(has_side_effects=pltpu.SideEffectType.DATAFLOW_SIDE_EFFECTING),
    )(*srcs_thru, *lands_thru, *sems, after)
    return res[:n], res[n:]


def _stage2_copies(ps, rcv, send, recv):
    x, y, c, me, chips = _place()
    return [pltpu.make_async_remote_copy(
        src_ref=ps[t].at[2 * px + py], dst_ref=rcv[t].at[kk], send_sem=send.at[t * 3 + kk],
        recv_sem=recv.at[t * 3 + kk], device_id=(px, py, c), device_id_type=MESH)
        for t in range(len(ps)) for kk, (px, py) in enumerate(chips)]


def _reduce_stage3(reduced, tag):
    n = len(reduced)

    def body(*refs):
        outs = refs[n:2 * n]
        send, recv = refs[2 * n:]
        x, y, c, me, chips = _place()
        cps = []
        for t in range(n):
            rows = outs[t].shape[0] // 2
            mine = outs[t].at[pl.ds(c * rows, rows), :]
            cp = pltpu.make_async_remote_copy(src_ref=mine, dst_ref=mine, send_sem=send.at[t], recv_sem=recv.at[t],
                                              device_id=(x, y, 1 - c), device_id_type=MESH)
            cp.start()
            cps.append(cp)
        for cp in cps:
            cp.wait()

    return pl.pallas_call(
        body, name="reduce_stage3_" + tag, in_specs=[HBM_SPEC] * n, out_specs=[HBM_SPEC] * n,
        out_shape=[jax.ShapeDtypeStruct(r.shape, r.dtype) for r in reduced],
        input_output_aliases={t: t for t in range(n)},
        scratch_shapes=[pltpu.SemaphoreType.DMA((n,)), pltpu.SemaphoreType.DMA((n,))],
    )(*reduced)


def _allreduce_small(v):
    rows, cols = v.shape

    def body(v_ref, o_ref, buf, send, recv, loc):
        x, y, c, me, chips = _place()
        mine = 4 * x + 2 * y + c
        lc = pltpu.make_async_copy(v_ref, buf.at[mine], loc)
        lc.start()
        peers = []
        for fx in range(2):
            for fy in range(2):
                for fc in range(2):
                    if fx or fy or fc:
                        peers.append((fx, fy, fc))
        cps = []
        for kk, (fx, fy, fc) in enumerate(peers):
            to = (x ^ fx, y ^ fy, c ^ fc)
            cp = pltpu.make_async_remote_copy(src_ref=v_ref, dst_ref=buf.at[mine], send_sem=send.at[kk], recv_sem=recv.at[kk],
                                              device_id=to, device_id_type=MESH)
            cp.start()
            cps.append((cp, to))
        for kk, (cp, to) in enumerate(cps):
            src = 4 * to[0] + 2 * to[1] + to[2]
            pltpu.make_async_remote_copy(src_ref=v_ref, dst_ref=buf.at[src], send_sem=send.at[kk], recv_sem=recv.at[kk],
                                         device_id=to, device_id_type=MESH).wait_recv()
        for cp, _ in cps:
            cp.wait_send()
        lc.wait()
        acc = buf[0]
        for d in range(1, 8):
            acc = acc + buf[d]
        o_ref[...] = acc

    return pl.pallas_call(
        body, name="allreduce_small", in_specs=[pl.BlockSpec(memory_space=pltpu.VMEM)],
        out_specs=pl.BlockSpec(memory_space=pltpu.VMEM), out_shape=jax.ShapeDtypeStruct((rows, cols), F32),
        scratch_shapes=[pltpu.VMEM((8, rows, cols), F32), pltpu.SemaphoreType.DMA((7,)), pltpu.SemaphoreType.DMA((7,)),
                        pltpu.SemaphoreType.DMA],
        compiler_params=pltpu.CompilerParams(vmem_limit_bytes=VMEM_LIMIT_V7X),
    )(v)


def _pad_w_in(w):
    z = lambda n: jnp.zeros(w.shape[:-1] + (n,), w.dtype)
    return jnp.concatenate([w[..., 0:384], z(64), w[..., 384:416], z(32), w[..., 416:1824],
                            w[..., 1824:1830], z(58), w[..., 400:416], w[..., 384:400], z(32)], axis=-1)


def _unpad_w_in(g):
    x1 = g[..., 448:464] + g[..., Z_F + 80:Z_F + 96]
    x2 = g[..., 464:480] + g[..., Z_F + 64:Z_F + 80]
    return jnp.concatenate([g[..., 0:384], x1, x2, g[..., 512:1920], g[..., 1920:1926]], axis=-1)


def _block_diag(pw):
    out = jnp.zeros((POOL_W, POOL_W), pw.dtype)
    for g in range(4):
        out = out.at[g * 64:(g + 1) * 64, g * 64:(g + 1) * 64].set(pw[g])
    return out


def _rope_tables(s):
    inv_freq = ROPE_THETA ** (-jnp.arange(0, ROPE, 2, dtype=F32) / ROPE)
    ang = jnp.arange(s, dtype=jnp.int32).astype(F32)[:, None] * inv_freq[None, :]
    cos, sin = jnp.cos(ang), jnp.sin(ang)
    zero = lambda n: jnp.zeros((s, n), F32)
    ck = jnp.concatenate([zero(NOPE), cos, cos, zero(DA - NOPE - ROPE)], axis=1)
    sk = jnp.concatenate([zero(NOPE), -sin, sin, zero(DA - NOPE - ROPE)], axis=1)
    cq = jnp.concatenate([jnp.ones((s, NOPE), F32), cos, cos, zero(DA - NOPE - ROPE)], axis=1) * SCALE_MLA
    return dict(cq=cq, sq=sk * SCALE_MLA, ck=ck, sk=sk)


def _mix_fwd(l, x1, wts, sm, tabs):
    z, h2, qn, kvn, qa, ka, va = _mix_in(x1, (sm["mix_norm"][l], sm["q_a_norm"][l], sm["kv_a_norm"][l]), wts["w_in"][l],
                                         wts["wq_a"][l], wts["wq_b"][l], wts["wk"][l], wts["wv"][l], tabs, name=f"mix_in_{l}")
    oa, lse_a = _attn_fwd(qa, ka, va, VDIM, name=f"mla_attn_{l}")

    bd = _block_diag(wts["pool_w"][l]).astype(BF16)
    yb, pooled = _pool_fwd(z, Z_POOL // POOL_W, bd, sm["pool_scale"][l], name=f"pool_{l}")

    fb = jnp.pad(sm["fox_b_f"][l], (0, 8 - H)).reshape(8, 1)
    ft, c3t = _gate_fwd(z, Z_F // DA, fb, name=f"fox_gate_{l}")
    fqa, fka, fva = _fox_prep(z, c3t, name=f"fox_prep_{l}")
    oc, lse_c = _attn_fwd(fqa, fka, fva, FOX_D, name=f"fox_attn_{l}")

    x2, cat = _mix_out(oa, yb, oc, wts["w_out"][l], x1, name=f"mix_out_{l}")
    saved = dict(z=z, h2=h2, qn=qn, kvn=kvn, qa=qa, ka=ka, va=va, oa=oa, lse_a=lse_a, bd=bd, pooled=pooled,
                 fqa=fqa, fka=fka, fva=fva, ft=ft, fb=fb, oc=oc, lse_c=lse_c, cat=cat)
    return x2, saved


def _mix_bwd(l, x1, dx2, sv, wts, sm, tabs, tok=None):
    s = x1.shape[0]
    g = {}
    dx2b = (dx2 if tok is None else dx2 + tok).astype(BF16)
    doa, doc, dyb, dl_a, dl_c, g["w_out"] = _mix_out_bwd(dx2b, wts["w_out"][l], sv["cat"], sv["oa"], sv["oc"],
                                                         name=f"mix_out_bwd_{l}")

    dfqa, dfka, dfva, dcq, dck = _attn_bwd(sv["fqa"], sv["fka"], sv["fva"], doc, sv["lse_c"], dl_c, True, name=f"fox_attn_bwd_{l}")
    dfox = _fox_bwd_prep(dfqa, dfka, dfva, name=f"fox_bwd_prep_{l}")
    dc = jnp.pad(dcq.reshape(H, s) + dck.reshape(H, s), ((0, 8 - H), (0, 0)))
    dft, dfb = _gate_bwd(sv["ft"], sv["fb"], dc, name=f"fox_gate_bwd_{l}")
    g["fox_b_f"] = dfb[:H, 0]

    dq, dbd, g["pool_scale"] = _pool_bwd_a(dyb, sv["pooled"], sv["bd"], sm["pool_scale"][l], name=f"pool_bwd_a_{l}")
    du = _pool_bwd_b(dq, name=f"pool_bwd_b_{l}")
    g["pool_w"] = jnp.stack([dbd[i * 64:(i + 1) * 64, i * 64:(i + 1) * 64] for i in range(4)])

    dqa_, dka_, dva_ = _attn_bwd(sv["qa"], sv["ka"], sv["va"], doa, sv["lse_a"], dl_a, False, name=f"mla_attn_bwd_{l}")
    dqab, dkv, dz3, dz15, dwq, dwkv = _mla_bwd_prep(dqa_, dka_, dva_, dft, sv["qn"], sv["kvn"], tabs["cq"], tabs["sq"],
                                                    tabs["ck"], tabs["sk"], name=f"mla_bwd_prep_{l}")
    wq_ab = jnp.concatenate([wts["wq_a"][l], wts["wq_b"][l]], axis=1)
    wkv = jnp.concatenate([wts["wk"][l], wts["wv"][l]], axis=1)
    dwq = dwq.reshape(Q_RANK, 2, H, DA)
    dwkv = dwkv.reshape(KV_RANK, 2, H, DA)
    da, db = dwq[:, 0], dwq[:, 1]
    swapped = jnp.concatenate([jnp.zeros((Q_RANK, H, NOPE), F32), db[..., NOPE + HALF_ROPE:NOPE + ROPE],
                               db[..., NOPE:NOPE + HALF_ROPE]], axis=-1)
    g["w_q_b"] = (da[..., :NOPE + ROPE] + swapped).reshape(Q_RANK, H * (NOPE + ROPE))
    g["w_kv_b"] = jnp.concatenate([dwkv[:, 0, :, :NOPE], dwkv[:, 1, :, :VDIM]], axis=-1).reshape(KV_RANK, H * (NOPE + VDIM))
    dqa, g["q_a_norm"] = _rmsnorm_bwd(sv["z"], Z_QA // Q_RANK, sm["q_a_norm"][l], dqab, wq_ab, name=f"q_a_norm_bwd_{l}")
    dkva, g["kv_a_norm"] = _rmsnorm_bwd(sv["z"], Z_KVA // KV_RANK, sm["kv_a_norm"][l], dkv, wkv, name=f"kv_a_norm_bwd_{l}")

    dz = jnp.concatenate([dqa.astype(BF16), dkva.astype(BF16), dz3, du.astype(BF16), dfox, dz15], axis=1)
    dx1, g["mix_norm"], g["w_in"] = _rmsnorm_bwd(x1, 0, sm["mix_norm"][l], dz, wts["w_in"][l], dx2, sv["h2"],
                                                 name=f"mix_norm_bwd_{l}", tm=256)
    return dx1, g


DW_TOKENS = 2048


def _local_step(x, target, wts, sm, late_weights=None, grads_ready=None):
    s = x.shape[0]
    tabs = _rope_tables(s)
    acts = []
    xs = x
    for l in range(DEPTH):
        x1, gu1, act1 = _ffn_fwd(xs, sm["ffn1_norm"][l], wts["ffn1_w_gu"][l], wts["ffn1_w_d2"][l], name=f"ffn1_fwd_{l}")
        if l == 0 and late_weights is not None:
            sm = late_weights("ffn1", x1, sm)
        x2, sv = _mix_fwd(l, x1, wts, sm, tabs)
        if l == 0 and late_weights is not None:
            sm = late_weights("mix", x2, sm)
        x3, gu2, act2 = _ffn_fwd(x2, sm["ffn2_norm"][l], wts["ffn2_w_gu"][l], wts["ffn2_w_d2"][l], name=f"ffn2_fwd_{l}")
        acts.append((xs, gu1, act1, x1, sv, x2, gu2, act2))
        xs = x3
    dx, g_final, loss = _loss_head(xs, sm["final_norm"], target, name="loss_head")
    grads = [dict() for _ in range(DEPTH)]
    for l in reversed(range(DEPTH)):
        x0, gu1, act1, x1, sv, x2, gu2, act2 = acts[l]
        g = grads[l]
        dx, dgu, hh, dy, g["ffn2_norm"] = _ffn_bwd(x2, dx, gu2, sm["ffn2_norm"][l], wts["ffn2_w_gu"][l], wts["ffn2_w_d2"][l],
                                                   name=f"ffn2_bwd_{l}")
        g["ffn2_w_down"] = _mm(act2, dy, "tn", name=f"d_ffn2_w_down_{l}", tm=FF_SHARD, tn=1024, tk=DW_TOKENS)
        g["ffn2_w_gu"] = _mm(hh, dgu, "tn", name=f"d_ffn2_w_gu_{l}", tm=1024, tn=FF_SHARD, tk=DW_TOKENS, n_major_out=True)
        tok = None
        if grads_ready is not None:
            sm, tok = grads_ready(l, "ffn2", g, sm)
        dx, gm = _mix_bwd(l, x1, dx, sv, wts, sm, tabs, tok)
        g.update(gm)
        if grads_ready is not None:
            sm, _ = grads_ready(l, "mix", g, sm)
        dx, dgu, hh, dy, g["ffn1_norm"] = _ffn_bwd(x0, dx, gu1, sm["ffn1_norm"][l], wts["ffn1_w_gu"][l], wts["ffn1_w_d2"][l],
                                                   name=f"ffn1_bwd_{l}")
        if grads_ready is not None:
            sm, tok = grads_ready(l, "ffn1_tokens", {"dx": dx}, sm)
            if tok is not None:
                dy = dy + tok.astype(BF16)
        g["ffn1_w_down"] = _mm(act1, dy, "tn", name=f"d_ffn1_w_down_{l}", tm=FF_SHARD, tn=1024, tk=DW_TOKENS)
        g["ffn1_w_gu"] = _mm(hh, dgu, "tn", name=f"d_ffn1_w_gu_{l}", tm=1024, tn=FF_SHARD, tk=DW_TOKENS, n_major_out=True)
        if grads_ready is not None:
            sm, _ = grads_ready(l, "ffn1", g, sm)
    return loss, dx, grads, g_final


BIG = ["ffn1_w_gu", "ffn1_w_down", "w_in", "w_q_b", "w_kv_b", "w_out", "ffn2_w_gu", "ffn2_w_down"]
SMALL = ["ffn1_norm", "mix_norm", "q_a_norm", "kv_a_norm", "pool_w", "pool_scale", "fox_b_f", "ffn2_norm"]
SMALL_ROWS = 48


WEIGHT_VIEWS = ["ffn1_w_gu", "ffn1_w_d2", "w_in", "wq_a", "wq_b", "wk", "wv", "w_out", "ffn2_w_gu", "ffn2_w_d2"]


def _prepare_weights(gathered, wts):
    for (nm, l), w in gathered.items():
        if nm in ("ffn1_w_gu", "ffn2_w_gu"):
            wts[nm][l] = w
        elif nm in ("ffn1_w_down", "ffn2_w_down"):
            wts[nm[:5] + "w_d2"][l] = w.reshape(2, FF_SHARD, D)
        elif nm in ("w_in", "w_out"):
            wts[nm][l] = w.reshape(D, -1)
        elif nm == "w_q_b":
            wq = jnp.moveaxis(w, 0, 1).reshape(Q_RANK, H, NOPE + ROPE)
            zq = lambda n: jnp.zeros((Q_RANK, H, n), BF16)
            wts["wq_a"][l] = jnp.concatenate([wq, zq(DA - NOPE - ROPE)], axis=-1).reshape(Q_RANK, H * DA)
            wts["wq_b"][l] = jnp.concatenate([zq(NOPE), wq[..., NOPE + HALF_ROPE:], wq[..., NOPE:NOPE + HALF_ROPE],
                                              zq(DA - NOPE - ROPE)], axis=-1).reshape(Q_RANK, H * DA)
        else:
            wkv = jnp.moveaxis(w, 0, 1).reshape(KV_RANK, H, NOPE + VDIM)
            zk = jnp.zeros((KV_RANK, H, DA - NOPE), BF16)
            wts["wk"][l] = jnp.concatenate([wkv[..., :NOPE], zk], axis=-1).reshape(KV_RANK, H * DA)
            wts["wv"][l] = jnp.concatenate([wkv[..., NOPE:], zk], axis=-1).reshape(KV_RANK, H * DA)


def _chip_major(name, g):
    if name in ("ffn1_w_gu", "ffn2_w_gu"):
        return g
    if name in ("ffn1_w_down", "ffn2_w_down", "w_in", "w_out"):
        return g.reshape(N_CHIPS, g.shape[0] // N_CHIPS, g.shape[1])
    return jnp.moveaxis(g.reshape(g.shape[0], N_CHIPS, g.shape[1] // N_CHIPS), 1, 0)


def _pack_small(grads, g_final, loss):
    parts = []
    for l in range(DEPTH):
        for nm in SMALL:
            parts.append(grads[l][nm].reshape(-1))
    parts.append(g_final.reshape(-1))
    parts.append(loss.reshape(1))
    flat = jnp.concatenate(parts)
    return jnp.pad(flat, (0, SMALL_ROWS * D - flat.shape[0])).reshape(SMALL_ROWS, D)


def _unpack_small(packed, params):
    flat = packed.reshape(-1)
    out = {nm: [] for nm in SMALL}
    off = 0
    for l in range(DEPTH):
        for nm in SMALL:
            shp = params[nm].shape[1:]
            n = int(np.prod(shp))
            out[nm].append(flat[off:off + n].reshape(shp))
            off += n
    res = {nm: jnp.stack(v) for nm, v in out.items()}
    res["final_norm"] = flat[off:off + D]
    return res, flat[off + D]


def _update(name, w, g, m, v):
    shp = w.shape
    if w.ndim == 1:
        view = (1, shp[0])
    elif w.size <= 65536:
        view = (shp[0], w.size // shp[0])
    else:
        view = (w.size // shp[-1], shp[-1])
    tr = view[0]
    for cand in (512, 352, 256, 128):
        if view[0] % cand == 0 and view[0] > cand:
            tr = cand
            break
    d, mn, vn = _adamw(w.reshape(view), g.reshape(view), m.reshape(view), v.reshape(view), name="adamw_" + name, tr=tr)
    return d.reshape(shp), mn.reshape(shp), vn.reshape(shp)


WEIGHTS = ['ffn1_norm', 'ffn1_w_gu', 'ffn1_w_down', 'mix_norm', 'w_in', 'q_a_norm', 'w_q_b', 'kv_a_norm', 'w_kv_b', 'pool_w',
           'pool_scale', 'fox_b_f', 'w_out', 'ffn2_norm', 'ffn2_w_gu', 'ffn2_w_down', 'final_norm']


def kernel(x, ffn1_norm, ffn1_w_gu, ffn1_w_down, mix_norm, w_in, q_a_norm, w_q_b, kv_a_norm, w_kv_b, pool_w, pool_scale, fox_b_f, w_out, ffn2_norm, ffn2_w_gu, ffn2_w_down, final_norm, loss_target, m_ffn1_norm, m_ffn1_w_gu, m_ffn1_w_down, m_mix_norm, m_w_in, m_q_a_norm, m_w_q_b, m_kv_a_norm, m_w_kv_b, m_pool_w, m_pool_scale, m_fox_b_f, m_w_out, m_ffn2_norm, m_ffn2_w_gu, m_ffn2_w_down, m_final_norm, v_ffn1_norm, v_ffn1_w_gu, v_ffn1_w_down, v_mix_norm, v_w_in, v_q_a_norm, v_w_q_b, v_kv_a_norm, v_w_kv_b, v_pool_w, v_pool_scale, v_fox_b_f, v_w_out, v_ffn2_norm, v_ffn2_w_gu, v_ffn2_w_down, v_final_norm):
    params = dict(ffn1_norm=ffn1_norm, ffn1_w_gu=ffn1_w_gu, ffn1_w_down=ffn1_w_down, mix_norm=mix_norm, w_in=w_in, q_a_norm=q_a_norm,
                  w_q_b=w_q_b, kv_a_norm=kv_a_norm, w_kv_b=w_kv_b, pool_w=pool_w, pool_scale=pool_scale, fox_b_f=fox_b_f, w_out=w_out,
                  ffn2_norm=ffn2_norm, ffn2_w_gu=ffn2_w_gu, ffn2_w_down=ffn2_w_down, final_norm=final_norm)
    mom = dict(ffn1_norm=m_ffn1_norm, ffn1_w_gu=m_ffn1_w_gu, ffn1_w_down=m_ffn1_w_down, mix_norm=m_mix_norm, w_in=m_w_in,
               q_a_norm=m_q_a_norm, w_q_b=m_w_q_b, kv_a_norm=m_kv_a_norm, w_kv_b=m_w_kv_b, pool_w=m_pool_w, pool_scale=m_pool_scale,
               fox_b_f=m_fox_b_f, w_out=m_w_out, ffn2_norm=m_ffn2_norm, ffn2_w_gu=m_ffn2_w_gu, ffn2_w_down=m_ffn2_w_down,
               final_norm=m_final_norm)
    var = dict(ffn1_norm=v_ffn1_norm, ffn1_w_gu=v_ffn1_w_gu, ffn1_w_down=v_ffn1_w_down, mix_norm=v_mix_norm, w_in=v_w_in,
               q_a_norm=v_q_a_norm, w_q_b=v_w_q_b, kv_a_norm=v_kv_a_norm, w_kv_b=v_w_kv_b, pool_w=v_pool_w, pool_scale=v_pool_scale,
               fox_b_f=v_fox_b_f, w_out=v_w_out, ffn2_norm=v_ffn2_norm, ffn2_w_gu=v_ffn2_w_gu, ffn2_w_down=v_ffn2_w_down,
               final_norm=v_final_norm)

    first = [("ffn1_w_gu", 0), ("ffn1_w_down", 0)]
    mix0 = [(nm, 0) for nm in ("w_in", "w_q_b", "w_kv_b", "w_out")]
    rest = [(nm, l) for nm in BIG for l in range(DEPTH) if (nm, l) not in first + mix0]

    def shards(keys, zero=0.0):
        return [((_pad_w_in(params[nm]) if nm == "w_in" else params[nm])[l] + zero).astype(BF16) for nm, l in keys]

    wts = {nm: [None] * DEPTH for nm in WEIGHT_VIEWS}
    wts["pool_w"] = params["pool_w"]
    got = _gather_blocking(shards(first))
    _prepare_weights(dict(zip(first, got)), wts)
    sems_m, src_m, land_m, token_m = _gather_start(shards(mix0), got[0], "mix0")
    sm = dict(params)
    sm["ffn1_norm"] = params["ffn1_norm"] + token_m[0, 0]
    rest_shards = shards(rest, token_m[0, 0])
    flying = {}

    def late_weights(stage, act, sm_now):
        if stage == "ffn1":
            lands = _gather_forward(_gather_wait(sems_m, src_m, land_m, act, "mix0"), "mix0")
            _prepare_weights(dict(zip(mix0, lands)), wts)
            flying["rest"] = _gather_start(rest_shards, lands[0], "rest")
            sm_next = dict(sm_now)
            sm_next["mix_norm"] = sm_now["mix_norm"] + flying["rest"][3][0, 0]
            return sm_next
        sems_r, src_r, land_r, _ = flying["rest"]
        lands = _gather_forward(_gather_wait(sems_r, src_r, land_r, act, "rest"), "rest")
        _prepare_weights(dict(zip(rest, lands)), wts)
        return sm_now

    pos = _position()
    flight = {}

    groups = {"l1": (1, BIG), "l0a": (0, [nm for nm in BIG if not nm.startswith("ffn1")]),
              "l0b": (0, [nm for nm in BIG if nm.startswith("ffn1")])}
    pending = {}

    def to_chips(key, full, sib):
        psum = _sum2_bf16(pos, full, sib, name=f"chip_sum_{key}")
        s2 = _split_start(_stage2_copies, psum, [((3,) + p.shape[1:], p.dtype) for p in psum], 3 * len(psum),
                          f"reduce_stage2_start_{key}")
        flight[key] = (full, sib, s2)
        return s2[3][0, 0]

    def grads_ready(l, stage, g, sm_now):
        behind, tok = None, None
        if (l, stage) == (1, "ffn1"):
            full = [_chip_major(nm, g[nm]) for nm in BIG]
            pending["l1"] = _split_start(_stage1_copies, full, [((N_CHIPS, f.shape[1] // 2, f.shape[2]), F32) for f in full],
                                         len(full), "reduce_stage1_start_l1")
            behind, tok = "ffn2_norm", pending["l1"][3][0, 0]
        elif (l, stage) == (0, "ffn2"):
            sems1, full_thru, sib_land, _ = pending["l1"]
            full, sib = _split_wait(_stage1_copies, sems1, full_thru, sib_land, g["ffn2_w_down"], "reduce_stage1_wait_l1")
            tok = to_chips("l1", full, sib)
        elif (l, stage) == (0, "mix"):
            full = [_chip_major(nm, g[nm]) for nm in groups["l0a"][1]]
            pending["l0a"] = _split_start(_stage1_copies, full, [((N_CHIPS, f.shape[1] // 2, f.shape[2]), F32) for f in full],
                                          len(full), "reduce_stage1_start_l0a")
            behind, tok = "ffn1_norm", pending["l0a"][3][0, 0]
        elif (l, stage) == (0, "ffn1_tokens"):
            sems1, full_thru, sib_land, _ = pending["l0a"]
            full, sib = _split_wait(_stage1_copies, sems1, full_thru, sib_land, g["dx"], "reduce_stage1_wait_l0a")
            tok = to_chips("l0a", full, sib)
        elif (l, stage) == (0, "ffn1"):
            full = [_chip_major(nm, g[nm]) for nm in groups["l0b"][1]]
            pending["l0b"] = _split_start(_stage1_copies, full, [((N_CHIPS, f.shape[1] // 2, f.shape[2]), F32) for f in full],
                                          len(full), "reduce_stage1_start_l0b")
        if behind is None:
            return sm_now, tok
        sm_next = dict(sm_now)
        sm_next[behind] = sm_now[behind] + tok
        return sm_next, tok

    loss, dx, grads, g_final = _local_step(x[0], loss_target[0], wts, sm, late_weights, grads_ready)

    def view2d(a):
        return a.reshape(a.size // a.shape[-1], a.shape[-1])

    after = pending["l0b"][3]
    done = {nm: None for nm in BIG}
    for key in ("l1", "l0a", "l0b"):
        l, names = groups[key]
        full, sib, (sems2, ps_thru, lands2, _) = flight[key]
        _, recv = _split_wait(_stage2_copies, sems2, ps_thru, lands2, after, f"reduce_stage2_wait_{key}")
        whole = _reduce_stage3(_sum5(pos, full, sib, recv, name=f"grad_sum_{key}"), key)
        for nm, g_l in zip(names, whole):
            if nm == "w_in":
                g_l = _unpad_w_in(g_l)
            tr = max(t for t in (512, 352, 256, 128) if g_l.shape[0] % t == 0)
            done[nm] = _adamw_layer(view2d(params[nm]), g_l, view2d(mom[nm]), view2d(var[nm]), l, done[nm],
                                    name=f"adamw_{nm}_{l}", tr=tr)
        after = done[names[-1]][0][-8:, 0:128]
        if key == "l1":
            small_g, loss = _unpack_small(_allreduce_small(_pack_small(grads, g_final, loss)), params)
            sems1, full_thru, sib_land, _ = pending["l0b"]
            full_b, sib_b = _split_wait(_stage1_copies, sems1, full_thru, sib_land, after + small_g["final_norm"][0],
                                        "reduce_stage1_wait_l0b")
            after = after + to_chips("l0b", full_b, sib_b)
    gw, delta, new_m, new_v = dict(small_g), {}, {}, {}
    for nm in BIG:
        delta[nm], new_m[nm], new_v[nm], gw[nm] = [a.reshape(params[nm].shape) for a in done[nm]]
    for nm in small_g:
        delta[nm], new_m[nm], new_v[nm] = _update(nm, params[nm], gw[nm], mom[nm], var[nm])
    return (loss, dx[None], *[gw[n] for n in WEIGHTS], *[delta[n] for n in WEIGHTS], *[new_m[n] for n in WEIGHTS],
            *[new_v[n] for n in WEIGHTS])
```
